```python
import jax, jax.numpy as jnp
from jax import lax
import numpy as np

D_MODEL = 1024
BATCH = 8
SEQ = 4096
DEPTH = 1

EPS = 1e-6
ROPE_THETA = 10000.0
BLOCK = 128

HEAD_DIM = 64
SWA_HEADS = 8
SWA_KV_HEADS = 2
SWA_GROUP = SWA_HEADS // SWA_KV_HEADS
WINDOW = 128

MLA_HEADS = 8
MLA_NOPE_DIM = 64
MLA_ROPE_DIM = 32
MLA_V_DIM = 64
MLA_QK_DIM = MLA_NOPE_DIM + MLA_ROPE_DIM
Q_LORA_RANK = 384
KV_LORA_RANK = 256

D_FF = -(-8 * D_MODEL // (3 * 256)) * 256

IN_SIZES = [
    SWA_HEADS * HEAD_DIM,
    SWA_KV_HEADS * HEAD_DIM,
    SWA_KV_HEADS * HEAD_DIM,
    Q_LORA_RANK,
    KV_LORA_RANK,
    MLA_ROPE_DIM,
    D_MODEL,
    D_MODEL,
]
IN_WIDTH = int(sum(IN_SIZES))
IN_OFFSETS = [int(v) for v in np.cumsum(IN_SIZES)[:-1]]

kernel_name = "hybrid_swa_sink_mla_gated_block"


def rmsnorm(x, g):
    xf = x.astype(jnp.float32)
    xf = xf * lax.rsqrt(jnp.mean(xf * xf, axis=-1, keepdims=True) + EPS)
    return (xf * g.astype(jnp.float32)).astype(x.dtype)


def rope_tables(seq, dim):
    inv = ROPE_THETA ** (-jnp.arange(0, dim, 2, dtype=jnp.float32) / dim)
    ang = jnp.arange(seq, dtype=jnp.float32)[:, None] * inv[None, :]
    return jnp.cos(ang)[:, None, :], jnp.sin(ang)[:, None, :]


def apply_rope(x, cos, sin):
    xf = x.astype(jnp.float32)
    x1, x2 = jnp.split(xf, 2, axis=-1)
    out = jnp.concatenate([x1 * cos - x2 * sin, x2 * cos + x1 * sin], axis=-1)
    return out.astype(x.dtype)


def swa_sink_attention(q, k, v, sinks):
    B, S = q.shape[0], q.shape[1]
    nb = S // BLOCK
    qb = q.reshape(B, nb, BLOCK, SWA_KV_HEADS, SWA_GROUP, HEAD_DIM)

    def band(t):
        tp = jnp.pad(t, ((0, 0), (BLOCK, 0), (0, 0), (0, 0)))
        tb = tp.reshape(B, nb + 1, BLOCK, SWA_KV_HEADS, HEAD_DIM)
        return jnp.concatenate([tb[:, :-1], tb[:, 1:]], axis=2)

    kw, vw = band(k), band(v)
    s = jnp.einsum('bnqhgd,bnkhd->bnhgqk', qb, kw,
                   preferred_element_type=jnp.float32) * (HEAD_DIM ** -0.5)
    qi = jnp.arange(BLOCK)[:, None]
    kj = jnp.arange(2 * BLOCK)[None, :]
    diff = qi - kj + BLOCK
    band_ok = (diff >= 0) & (diff < WINDOW)
    kpos = jnp.arange(nb)[:, None] * BLOCK + kj - BLOCK
    mask = band_ok[None] & (kpos >= 0)[:, None, :]
    s = jnp.where(mask[None, :, None, None], s, -jnp.inf)
    sink = sinks.astype(jnp.float32).reshape(1, 1, SWA_KV_HEADS, SWA_GROUP, 1, 1)
    m = jnp.maximum(jnp.max(s, axis=-1, keepdims=True), sink)
    p = jnp.exp(s - m)
    denom = jnp.sum(p, axis=-1, keepdims=True) + jnp.exp(sink - m)
    p = (p / denom).astype(v.dtype)
    o = jnp.einsum('bnhgqk,bnkhd->bnqhgd', p, vw)
    return o.reshape(B, S, SWA_HEADS * HEAD_DIM)


def mla_attention(q_nope, q_rope, k_nope, k_rope, v):
    B, S = q_nope.shape[0], q_nope.shape[1]
    nb = S // BLOCK
    scale = MLA_QK_DIM ** -0.5
    kpos = jnp.arange(S)

    def to_blocks(t):
        return jnp.moveaxis(t.reshape(B, nb, BLOCK, *t.shape[2:]), 1, 0)

    def one_block(args):
        qn, qr, n = args
        s = (jnp.einsum('bqhd,bkhd->bhqk', qn, k_nope, preferred_element_type=jnp.float32)
             + jnp.einsum('bqhr,bkr->bhqk', qr, k_rope, preferred_element_type=jnp.float32)) * scale
        qpos = n * BLOCK + jnp.arange(BLOCK)
        causal = kpos[None, :] <= qpos[:, None]
        s = jnp.where(causal[None, None], s, -jnp.inf)
        p = jax.nn.softmax(s, axis=-1).astype(v.dtype)
        return jnp.einsum('bhqk,bkhd->bqhd', p, v)

    o = lax.map(one_block, (to_blocks(q_nope), to_blocks(q_rope), jnp.arange(nb)))
    return jnp.moveaxis(o, 0, 1).reshape(B, S, MLA_HEADS * MLA_V_DIM)


def _fwd_setup_inputs(seed: int = 0) -> dict:
    key = jax.random.key(seed)
    ks = jax.random.split(key, 17)
    f32 = jnp.float32

    def w(k, shape, fan_in):
        return jax.random.normal(k, shape, f32) * (fan_in ** -0.5)

    def gain(k, shape):
        return 1.0 + 0.02 * jax.random.normal(k, shape, f32)

    L = DEPTH
    return {
        "x": jax.random.normal(ks[0], (BATCH, SEQ, D_MODEL), f32),
        "mix_norm_g": gain(ks[1], (L, D_MODEL)),
        "w_in": w(ks[2], (L, D_MODEL, IN_WIDTH), D_MODEL),
        "swa_sinks": 0.5 * jax.random.normal(ks[3], (L, SWA_HEADS), f32),
        "q_norm_g": gain(ks[4], (L, Q_LORA_RANK)),
        "w_uq": w(ks[5], (L, Q_LORA_RANK, MLA_HEADS * MLA_QK_DIM), Q_LORA_RANK),
        "kv_norm_g": gain(ks[6], (L, KV_LORA_RANK)),
        "w_ukv": w(ks[7], (L, KV_LORA_RANK, MLA_HEADS * (MLA_NOPE_DIM + MLA_V_DIM)), KV_LORA_RANK),
        "w_o_swa": w(ks[8], (L, SWA_HEADS * HEAD_DIM, D_MODEL), SWA_HEADS * HEAD_DIM),
        "w_o_mla": w(ks[9], (L, MLA_HEADS * MLA_V_DIM, D_MODEL), MLA_HEADS * MLA_V_DIM),
        "w_out": w(ks[10], (L, D_MODEL, D_MODEL), D_MODEL),
        "ffn_norm_g": gain(ks[11], (L, D_MODEL)),
        "w_gate": w(ks[12], (L, D_MODEL, D_FF), D_MODEL),
        "w_up": w(ks[13], (L, D_MODEL, D_FF), D_MODEL),
        "w_down": w(ks[14], (L, D_FF, D_MODEL), D_FF),
        "final_norm_g": gain(ks[15], (D_MODEL,)),
    }


def _fwd_reference(x, mix_norm_g, w_in, swa_sinks, q_norm_g, w_uq, kv_norm_g, w_ukv,
              w_o_swa, w_o_mla, w_out, ffn_norm_g, w_gate, w_up, w_down, final_norm_g):
    B, S = x.shape[0], x.shape[1]
    cos_a, sin_a = rope_tables(S, HEAD_DIM)
    cos_b, sin_b = rope_tables(S, MLA_ROPE_DIM)

    for l in range(DEPTH):
        h = rmsnorm(x, mix_norm_g[l])
        proj = h @ w_in[l]
        qa, ka, va, q_lat, kv_lat, k_r, g_a, g_b = jnp.split(proj, IN_OFFSETS, axis=-1)

        qa = apply_rope(qa.reshape(B, S, SWA_HEADS, HEAD_DIM), cos_a, sin_a)
        ka = apply_rope(ka.reshape(B, S, SWA_KV_HEADS, HEAD_DIM), cos_a, sin_a)
        va = va.reshape(B, S, SWA_KV_HEADS, HEAD_DIM)
        o_a = swa_sink_attention(qa, ka, va, swa_sinks[l])

        cq = rmsnorm(q_lat, q_norm_g[l])
        qb = (cq @ w_uq[l]).reshape(B, S, MLA_HEADS, MLA_QK_DIM)
        q_nope, q_rope = jnp.split(qb, [MLA_NOPE_DIM], axis=-1)
        q_rope = apply_rope(q_rope, cos_b, sin_b)
        ckv = rmsnorm(kv_lat, kv_norm_g[l])
        kvb = (ckv @ w_ukv[l]).reshape(B, S, MLA_HEADS, MLA_NOPE_DIM + MLA_V_DIM)
        k_nope, vb = jnp.split(kvb, [MLA_NOPE_DIM], axis=-1)
        k_rope = apply_rope(k_r[:, :, None, :], cos_b, sin_b)[:, :, 0, :]
        o_b = mla_attention(q_nope, q_rope, k_nope, k_rope, vb)

        y = jax.nn.sigmoid(g_a) * (o_a @ w_o_swa[l]) + jax.nn.sigmoid(g_b) * (o_b @ w_o_mla[l])
        x = x + y @ w_out[l]

        h = rmsnorm(x, ffn_norm_g[l])
        x = x + (jax.nn.silu(h @ w_gate[l]) * (h @ w_up[l])) @ w_down[l]

    return rmsnorm(x, final_norm_g)


import jax as _jax
import jax.numpy as _jnp

TWIN_FORMAT = 'train_step'
FWD_PARAMS = ['x', 'mix_norm_g', 'w_in', 'swa_sinks', 'q_norm_g', 'w_uq', 'kv_norm_g', 'w_ukv', 'w_o_swa', 'w_o_mla', 'w_out', 'ffn_norm_g', 'w_gate', 'w_up', 'w_down', 'final_norm_g']
TWIN_WEIGHTS = ['mix_norm_g', 'w_in', 'swa_sinks', 'q_norm_g', 'w_uq', 'kv_norm_g', 'w_ukv', 'w_o_swa', 'w_o_mla', 'w_out', 'ffn_norm_g', 'w_gate', 'w_up', 'w_down', 'final_norm_g']
TWIN_DIFF_INPUT = 'x'
TWIN_INPUTS = ['x', 'mix_norm_g', 'w_in', 'swa_sinks', 'q_norm_g', 'w_uq', 'kv_norm_g', 'w_ukv', 'w_o_swa', 'w_o_mla', 'w_out', 'ffn_norm_g', 'w_gate', 'w_up', 'w_down', 'final_norm_g', 'loss_target', 'm_mix_norm_g', 'm_w_in', 'm_swa_sinks', 'm_q_norm_g', 'm_w_uq', 'm_kv_norm_g', 'm_w_ukv', 'm_w_o_swa', 'm_w_o_mla', 'm_w_out', 'm_ffn_norm_g', 'm_w_gate', 'm_w_up', 'm_w_down', 'm_final_norm_g', 'v_mix_norm_g', 'v_w_in', 'v_swa_sinks', 'v_q_norm_g', 'v_w_uq', 'v_kv_norm_g', 'v_w_ukv', 'v_w_o_swa', 'v_w_o_mla', 'v_w_out', 'v_ffn_norm_g', 'v_w_gate', 'v_w_up', 'v_w_down', 'v_final_norm_g']
TWIN_OUTPUTS = ['loss', 'grad_x', 'grad_mix_norm_g', 'grad_w_in', 'grad_swa_sinks', 'grad_q_norm_g', 'grad_w_uq', 'grad_kv_norm_g', 'grad_w_ukv', 'grad_w_o_swa', 'grad_w_o_mla', 'grad_w_out', 'grad_ffn_norm_g', 'grad_w_gate', 'grad_w_up', 'grad_w_down', 'grad_final_norm_g', 'delta_mix_norm_g', 'delta_w_in', 'delta_swa_sinks', 'delta_q_norm_g', 'delta_w_uq', 'delta_kv_norm_g', 'delta_w_ukv', 'delta_w_o_swa', 'delta_w_o_mla', 'delta_w_out', 'delta_ffn_norm_g', 'delta_w_gate', 'delta_w_up', 'delta_w_down', 'delta_final_norm_g', 'new_m_mix_norm_g', 'new_m_w_in', 'new_m_swa_sinks', 'new_m_q_norm_g', 'new_m_w_uq', 'new_m_kv_norm_g', 'new_m_w_ukv', 'new_m_w_o_swa', 'new_m_w_o_mla', 'new_m_w_out', 'new_m_ffn_norm_g', 'new_m_w_gate', 'new_m_w_up', 'new_m_w_down', 'new_m_final_norm_g', 'new_v_mix_norm_g', 'new_v_w_in', 'new_v_swa_sinks', 'new_v_q_norm_g', 'new_v_w_uq', 'new_v_kv_norm_g', 'new_v_w_ukv', 'new_v_w_o_swa', 'new_v_w_o_mla', 'new_v_w_out', 'new_v_ffn_norm_g', 'new_v_w_gate', 'new_v_w_up', 'new_v_w_down', 'new_v_final_norm_g']
TWIN_LEAF_KINDS = {'loss': 'loss', 'grad_x': 'grad_x', 'grad_mix_norm_g': 'grad_w', 'grad_w_in': 'grad_w', 'grad_swa_sinks': 'grad_w', 'grad_q_norm_g': 'grad_w', 'grad_w_uq': 'grad_w', 'grad_kv_norm_g': 'grad_w', 'grad_w_ukv': 'grad_w', 'grad_w_o_swa': 'grad_w', 'grad_w_o_mla': 'grad_w', 'grad_w_out': 'grad_w', 'grad_ffn_norm_g': 'grad_w', 'grad_w_gate': 'grad_w', 'grad_w_up': 'grad_w', 'grad_w_down': 'grad_w', 'grad_final_norm_g': 'grad_w', 'delta_mix_norm_g': 'delta_w', 'delta_w_in': 'delta_w', 'delta_swa_sinks': 'delta_w', 'delta_q_norm_g': 'delta_w', 'delta_w_uq': 'delta_w', 'delta_kv_norm_g': 'delta_w', 'delta_w_ukv': 'delta_w', 'delta_w_o_swa': 'delta_w', 'delta_w_o_mla': 'delta_w', 'delta_w_out': 'delta_w', 'delta_ffn_norm_g': 'delta_w', 'delta_w_gate': 'delta_w', 'delta_w_up': 'delta_w', 'delta_w_down': 'delta_w', 'delta_final_norm_g': 'delta_w', 'new_m_mix_norm_g': 'new_m', 'new_m_w_in': 'new_m', 'new_m_swa_sinks': 'new_m', 'new_m_q_norm_g': 'new_m', 'new_m_w_uq': 'new_m', 'new_m_kv_norm_g': 'new_m', 'new_m_w_ukv': 'new_m', 'new_m_w_o_swa': 'new_m', 'new_m_w_o_mla': 'new_m', 'new_m_w_out': 'new_m', 'new_m_ffn_norm_g': 'new_m', 'new_m_w_gate': 'new_m', 'new_m_w_up': 'new_m', 'new_m_w_down': 'new_m', 'new_m_final_norm_g': 'new_m', 'new_v_mix_norm_g': 'new_v', 'new_v_w_in': 'new_v', 'new_v_swa_sinks': 'new_v', 'new_v_q_norm_g': 'new_v', 'new_v_w_uq': 'new_v', 'new_v_kv_norm_g': 'new_v', 'new_v_w_ukv': 'new_v', 'new_v_w_o_swa': 'new_v', 'new_v_w_o_mla': 'new_v', 'new_v_w_out': 'new_v', 'new_v_ffn_norm_g': 'new_v', 'new_v_w_gate': 'new_v', 'new_v_w_up': 'new_v', 'new_v_w_down': 'new_v', 'new_v_final_norm_g': 'new_v'}


def _forward(args):
    return _fwd_reference(*[args[k] for k in FWD_PARAMS])


def _output_shape():
    out = _jax.eval_shape(lambda: _forward(_fwd_setup_inputs(0)))
    return out.shape, out.dtype

N_MICROBATCH = 1
ADAM_LR = 0.001
ADAM_B1 = 0.9
ADAM_B2 = 0.999
ADAM_EPS = 1e-08
ADAM_WD = 0.01
ADAM_STEP = 10
PER_EXAMPLE_BATCH_AXIS = {'x': 0, 'loss_target': 0}
SHARED_INPUTS = []
_WEIGHT_DTYPES = {'mix_norm_g': _jnp.float32, 'w_in': _jnp.float32, 'swa_sinks': _jnp.float32, 'q_norm_g': _jnp.float32, 'w_uq': _jnp.float32, 'kv_norm_g': _jnp.float32, 'w_ukv': _jnp.float32, 'w_o_swa': _jnp.float32, 'w_o_mla': _jnp.float32, 'w_out': _jnp.float32, 'ffn_norm_g': _jnp.float32, 'w_gate': _jnp.float32, 'w_up': _jnp.float32, 'w_down': _jnp.float32, 'final_norm_g': _jnp.float32}
MOMENT_SCALE = {'mix_norm_g': 5.136014e-02, 'w_in': 2.740157e-02, 'swa_sinks': 2.413010e-02, 'q_norm_g': 3.123744e-02, 'w_uq': 2.249472e-02, 'kv_norm_g': 5.950481e-02, 'w_ukv': 2.794545e-02, 'w_o_swa': 2.169138e-02, 'w_o_mla': 2.305334e-02, 'w_out': 3.166351e-02, 'ffn_norm_g': 1.341341e-01, 'w_gate': 5.669363e-02, 'w_up': 5.496824e-02, 'w_down': 9.086135e-02, 'final_norm_g': 3.201072e+01}


def _to_microbatches(a, axis):
    t = _jnp.moveaxis(a, axis, 0)
    t = t.reshape((N_MICROBATCH, t.shape[0] // N_MICROBATCH) + t.shape[1:])
    return _jnp.moveaxis(t, 1, axis + 1)


def setup_inputs(seed: int = 0) -> dict:
    inp = _fwd_setup_inputs(seed)
    key = _jax.random.fold_in(_jax.random.key(seed), 7919)
    shape, _ = _output_shape()
    out = dict(inp)
    out["loss_target"] = _jax.random.normal(_jax.random.fold_in(key, 0), shape, _jnp.float32)
    for i, name in enumerate(TWIN_WEIGHTS):
        w = inp[name].astype(_jnp.float32)
        if MOMENT_SCALE is None:
            s = _jnp.sqrt(_jnp.mean(_jnp.square(w)) + 1e-30)
        else:
            s = MOMENT_SCALE[name]
        km, kv = _jax.random.split(_jax.random.fold_in(key, i + 1))
        out[name] = w
        out["m_" + name] = s * _jax.random.normal(km, w.shape, _jnp.float32)
        out["v_" + name] = (s * s) * _jax.random.uniform(kv, w.shape, _jnp.float32, 0.5, 1.5)
    if N_MICROBATCH > 1:
        for name, axis in PER_EXAMPLE_BATCH_AXIS.items():
            out[name] = _to_microbatches(out[name], axis)
    return {'x': out['x'], 'mix_norm_g': out['mix_norm_g'], 'w_in': out['w_in'], 'swa_sinks': out['swa_sinks'], 'q_norm_g': out['q_norm_g'], 'w_uq': out['w_uq'], 'kv_norm_g': out['kv_norm_g'], 'w_ukv': out['w_ukv'], 'w_o_swa': out['w_o_swa'], 'w_o_mla': out['w_o_mla'], 'w_out': out['w_out'], 'ffn_norm_g': out['ffn_norm_g'], 'w_gate': out['w_gate'], 'w_up': out['w_up'], 'w_down': out['w_down'], 'final_norm_g': out['final_norm_g'], 'loss_target': out['loss_target'], 'm_mix_norm_g': out['m_mix_norm_g'], 'm_w_in': out['m_w_in'], 'm_swa_sinks': out['m_swa_sinks'], 'm_q_norm_g': out['m_q_norm_g'], 'm_w_uq': out['m_w_uq'], 'm_kv_norm_g': out['m_kv_norm_g'], 'm_w_ukv': out['m_w_ukv'], 'm_w_o_swa': out['m_w_o_swa'], 'm_w_o_mla': out['m_w_o_mla'], 'm_w_out': out['m_w_out'], 'm_ffn_norm_g': out['m_ffn_norm_g'], 'm_w_gate': out['m_w_gate'], 'm_w_up': out['m_w_up'], 'm_w_down': out['m_w_down'], 'm_final_norm_g': out['m_final_norm_g'], 'v_mix_norm_g': out['v_mix_norm_g'], 'v_w_in': out['v_w_in'], 'v_swa_sinks': out['v_swa_sinks'], 'v_q_norm_g': out['v_q_norm_g'], 'v_w_uq': out['v_w_uq'], 'v_kv_norm_g': out['v_kv_norm_g'], 'v_w_ukv': out['v_w_ukv'], 'v_w_o_swa': out['v_w_o_swa'], 'v_w_o_mla': out['v_w_o_mla'], 'v_w_out': out['v_w_out'], 'v_ffn_norm_g': out['v_ffn_norm_g'], 'v_w_gate': out['v_w_gate'], 'v_w_up': out['v_w_up'], 'v_w_down': out['v_w_down'], 'v_final_norm_g': out['v_final_norm_g']}


def _loss(weights, diff, rest, loss_target):
    with _jax.named_scope("forward"):
        args = {**rest, TWIN_DIFF_INPUT: diff, **{k: w.astype(_WEIGHT_DTYPES[k]) for k, w in weights.items()}}
        y = _forward(args)
    with _jax.named_scope("loss_head"):
        err = _jnp.square(y.astype(_jnp.float32) - loss_target)
        return 0.5 * _jnp.sum(_jnp.mean(err, axis=-1)) if err.ndim else 0.5 * err


def _adamw(w, g, m, v):
    m = ADAM_B1 * m + (1.0 - ADAM_B1) * g
    v = ADAM_B2 * v + (1.0 - ADAM_B2) * _jnp.square(g)
    m_hat = m / (1.0 - ADAM_B1 ** ADAM_STEP)
    v_hat = v / (1.0 - ADAM_B2 ** ADAM_STEP)
    delta = -ADAM_LR * (m_hat / (_jnp.sqrt(v_hat) + ADAM_EPS) + ADAM_WD * w)
    return delta, m, v


def reference(x, mix_norm_g, w_in, swa_sinks, q_norm_g, w_uq, kv_norm_g, w_ukv, w_o_swa, w_o_mla, w_out, ffn_norm_g, w_gate, w_up, w_down, final_norm_g, loss_target, m_mix_norm_g, m_w_in, m_swa_sinks, m_q_norm_g, m_w_uq, m_kv_norm_g, m_w_ukv, m_w_o_swa, m_w_o_mla, m_w_out, m_ffn_norm_g, m_w_gate, m_w_up, m_w_down, m_final_norm_g, v_mix_norm_g, v_w_in, v_swa_sinks, v_q_norm_g, v_w_uq, v_kv_norm_g, v_w_ukv, v_w_o_swa, v_w_o_mla, v_w_out, v_ffn_norm_g, v_w_gate, v_w_up, v_w_down, v_final_norm_g):
    given = dict(x=x, mix_norm_g=mix_norm_g, w_in=w_in, swa_sinks=swa_sinks, q_norm_g=q_norm_g, w_uq=w_uq, kv_norm_g=kv_norm_g, w_ukv=w_ukv, w_o_swa=w_o_swa, w_o_mla=w_o_mla, w_out=w_out, ffn_norm_g=ffn_norm_g, w_gate=w_gate, w_up=w_up, w_down=w_down, final_norm_g=final_norm_g, loss_target=loss_target, m_mix_norm_g=m_mix_norm_g, m_w_in=m_w_in, m_swa_sinks=m_swa_sinks, m_q_norm_g=m_q_norm_g, m_w_uq=m_w_uq, m_kv_norm_g=m_kv_norm_g, m_w_ukv=m_w_ukv, m_w_o_swa=m_w_o_swa, m_w_o_mla=m_w_o_mla, m_w_out=m_w_out, m_ffn_norm_g=m_ffn_norm_g, m_w_gate=m_w_gate, m_w_up=m_w_up, m_w_down=m_w_down, m_final_norm_g=m_final_norm_g, v_mix_norm_g=v_mix_norm_g, v_w_in=v_w_in, v_swa_sinks=v_swa_sinks, v_q_norm_g=v_q_norm_g, v_w_uq=v_w_uq, v_kv_norm_g=v_kv_norm_g, v_w_ukv=v_w_ukv, v_w_o_swa=v_w_o_swa, v_w_o_mla=v_w_o_mla, v_w_out=v_w_out, v_ffn_norm_g=v_ffn_norm_g, v_w_gate=v_w_gate, v_w_up=v_w_up, v_w_down=v_w_down, v_final_norm_g=v_final_norm_g)
    weights = {n: given[n] for n in TWIN_WEIGHTS}
    shared = {n: given[n] for n in SHARED_INPUTS}
    per_example = {n: given[n] for n in ['x']}
    grad_fn = _jax.value_and_grad(_loss, argnums=(0, 1))

    def one_microbatch(ex, loss_target):
        ex = dict(ex)
        diff = ex.pop(TWIN_DIFF_INPUT)
        return grad_fn(weights, diff, {**shared, **ex}, loss_target)

    if N_MICROBATCH == 1:
        loss, (grad_w, grad_x) = one_microbatch(per_example, given["loss_target"])
    else:
        def body(carry, xs):
            loss_sum, grad_sum = carry
            l_k, (gw_k, gx_k) = one_microbatch(xs[0], xs[1])
            with _jax.named_scope("update"):
                return (loss_sum + l_k, _jax.tree.map(_jnp.add, grad_sum, gw_k)), gx_k

        init = (_jnp.zeros((), _jnp.float32), _jax.tree.map(_jnp.zeros_like, weights))
        (loss, grad_w), grad_x = _jax.lax.scan(body, init, (per_example, given["loss_target"]))
    with _jax.named_scope("update"):
        delta_w, new_m, new_v = {}, {}, {}
        for n in TWIN_WEIGHTS:
            delta_w[n], new_m[n], new_v[n] = _adamw(weights[n], grad_w[n], given["m_" + n], given["v_" + n])
    return (loss, grad_x, *[grad_w[n] for n in TWIN_WEIGHTS], *[delta_w[n] for n in TWIN_WEIGHTS],
            *[new_m[n] for n in TWIN_WEIGHTS], *[new_v[n] for n in TWIN_WEIGHTS])
```

```python
import functools

import numpy as np
import jax
import jax.numpy as jnp
from jax import lax
from jax.experimental import pallas as pl
from jax.experimental.pallas import tpu as pltpu

F32 = jnp.float32
MXU_DTYPE = jnp.bfloat16
WIRE_DTYPE = jnp.bfloat16

D_MODEL = 1024
EPS = 1e-6
ROPE_THETA = 10000.0
BLOCK = 128
HEAD_DIM = 64
SWA_HEADS = 8
SWA_KV_HEADS = 2
SWA_GROUP = SWA_HEADS // SWA_KV_HEADS
MLA_HEADS = 8
MLA_NOPE = 64
MLA_ROPE = 32
MLA_V = 64
MLA_QK = MLA_NOPE + MLA_ROPE
Q_LORA = 384
KV_LORA = 256
D_FF = 2816
IN_SIZES = (512, 128, 128, Q_LORA, KV_LORA, MLA_ROPE, D_MODEL, D_MODEL)
IN_OFF = tuple(int(v) for v in np.cumsum((0,) + IN_SIZES))
ADAM_LR, ADAM_B1, ADAM_B2, ADAM_EPS, ADAM_WD, ADAM_STEP = 0.001, 0.9, 0.999, 1e-08, 0.01, 10

LANES = 128
SUBLANES = 8
VMEM_LIMIT = 48 * 1024 * 1024
N_DEV = 8
AXES = ("x", "y", "c")

P_GA, P_GB, P_Q, P_QLAT, P_KR, P_K, P_V, P_KVLAT, P_W = 0, 1024, 2048, 3072, 3456, 3584, 3840, 4096, 4352
KR_LANE = 64

NT = (((1,), (1,)), ((), ()))
NN = (((1,), (0,)), ((), ()))
TN = (((0,), (0,)), ((), ()))


def _cparams(sem):
    return pltpu.CompilerParams(dimension_semantics=sem, vmem_limit_bytes=VMEM_LIMIT)


def _mm(a, b, mode, *, name, out_dtype=F32, add=None, tm=512, tn=512, tk=None):
    if mode == "nn":
        (M, K), (K2, N) = a.shape, b.shape
    elif mode == "nt":
        (M, K), (N, K2) = a.shape, b.shape
    else:
        (K, M), (K2, N) = a.shape, b.shape
    assert K == K2, (a.shape, b.shape, mode)
    tk = K if tk is None else tk
    tm, tn = min(tm, M), min(tn, N)
    assert M % tm == 0 and N % tn == 0 and K % tk == 0, (M, N, K, tm, tn, tk)
    nk = K // tk
    dn = {"nn": NN, "nt": NT, "tn": TN}[mode]
    if mode == "tn":
        a_spec = pl.BlockSpec((tk, tm), lambda i, j, k: (k, i))
    else:
        a_spec = pl.BlockSpec((tm, tk), lambda i, j, k: (i, k))
    if mode == "nt":
        b_spec = pl.BlockSpec((tn, tk), lambda i, j, k: (j, k))
    else:
        b_spec = pl.BlockSpec((tk, tn), lambda i, j, k: (k, j))
    o_spec = pl.BlockSpec((tm, tn), lambda i, j, k: (i, j))
    has_add = add is not None

    def body(*refs):
        a_ref, b_ref = refs[0], refs[1]
        add_ref = refs[2] if has_add else None
        o_ref = refs[3] if has_add else refs[2]
        p = lax.dot_general(a_ref[...], b_ref[...], dn, preferred_element_type=F32)

        def finish(acc):
            if has_add:
                acc = acc + add_ref[...]
            o_ref[...] = acc.astype(o_ref.dtype)

        if nk == 1:
            finish(p)
        else:
            acc_ref = refs[-1]
            k = pl.program_id(2)

            @pl.when(k == 0)
            def _():
                acc_ref[...] = p

            @pl.when(k > 0)
            def _():
                acc_ref[...] += p

            @pl.when(k == nk - 1)
            def _():
                finish(acc_ref[...])

    ins = [a, b] + ([add] if has_add else [])
    in_specs = [a_spec, b_spec] + ([o_spec] if has_add else [])
    return pl.pallas_call(
        body, name=name, grid=(M // tm, N // tn, nk), in_specs=in_specs, out_specs=o_spec,
        out_shape=jax.ShapeDtypeStruct((M, N), out_dtype),
        scratch_shapes=[pltpu.VMEM((tm, tn), F32)] if nk > 1 else [],
        compiler_params=_cparams(("parallel", "parallel", "arbitrary")),
    )(*ins)


def _rows(ts, w, cb=0):
    return pl.BlockSpec((ts, w), lambda i: (i, cb))


def _const(r, w):
    return pl.BlockSpec((r, w), lambda i: (0, 0))


def _sublane_sum(v):
    ts, c = v.shape
    return jnp.sum(v.reshape(ts // SUBLANES, SUBLANES, c), axis=0)


def _sigmoid(v):
    return 1.0 / (1.0 + jnp.exp(-v))


def _rope(v, cos, s_up, s_dn, up, dn):
    return v * cos + pltpu.roll(v, up, 1) * s_up + pltpu.roll(v, dn, 1) * s_dn


def _rope_t(dv, cos, s_up, s_dn, up, dn):
    return dv * cos + pltpu.roll(dv * s_up, dn, 1) + pltpu.roll(dv * s_dn, up, 1)


def _rope_tables(seq):
    pos = jnp.arange(seq, dtype=F32)[:, None]

    def base(dim):
        inv = ROPE_THETA ** (-jnp.arange(0, dim, 2, dtype=F32) / dim)
        ang = pos * inv[None, :]
        return jnp.cos(ang), jnp.sin(ang)

    z = lambda n: jnp.zeros((seq, n), F32)
    ca, sa = base(HEAD_DIM)
    a_cos = jnp.concatenate([ca, ca, z(64)], 1)
    a_up = jnp.concatenate([-sa, z(96)], 1)
    a_dn = jnp.concatenate([z(32), sa, z(64)], 1)
    cb, sb = base(MLA_ROPE)
    one = jnp.ones((seq, 64), F32)
    q_cos = jnp.concatenate([one, cb, cb, z(32)], 1)
    k_cos = jnp.concatenate([z(64), cb, cb, z(32)], 1)
    b_up = jnp.concatenate([z(64), -sb, z(48)], 1)
    b_dn = jnp.concatenate([z(80), sb, z(32)], 1)
    return a_cos, a_up, a_dn, q_cos, k_cos, b_up, b_dn


def _norm_fwd(x, g, *, name, ts=256):
    s_, c = x.shape

    def body(x_ref, g_ref, h_ref):
        v = x_ref[...]
        r = lax.rsqrt(jnp.mean(v * v, axis=-1, keepdims=True) + EPS)
        h_ref[...] = (v * r * g_ref[...]).astype(h_ref.dtype)

    return pl.pallas_call(
        body, name=name, grid=(s_ // ts,), in_specs=[_rows(ts, c), _const(1, c)], out_specs=_rows(ts, c),
        out_shape=jax.ShapeDtypeStruct((s_, c), MXU_DTYPE), compiler_params=_cparams(("parallel",)),
    )(x, g)


def _norm_bwd(x, g, dy, res, *, name, ts=256, x_cb=0, x_src_w=None):
    s_ = x.shape[0]
    c = dy.shape[1]
    has_res = res is not None

    def body(*refs):
        x_ref, g_ref, dy_ref = refs[0], refs[1], refs[2]
        res_ref = refs[3] if has_res else None
        dx_ref, dxb_ref, dg_ref = refs[-3], refs[-2], refs[-1]
        v = x_ref[...]
        r = lax.rsqrt(jnp.mean(v * v, axis=-1, keepdims=True) + EPS)
        xh = v * r
        d = dy_ref[...]
        dxh = d * g_ref[...]
        dx = r * (dxh - xh * jnp.mean(dxh * xh, axis=-1, keepdims=True))
        if has_res:
            dx = dx + res_ref[...]
        dx_ref[...] = dx
        dxb_ref[...] = dx.astype(dxb_ref.dtype)

        @pl.when(pl.program_id(0) == 0)
        def _():
            dg_ref[...] = jnp.zeros(dg_ref.shape, F32)

        dg_ref[...] += _sublane_sum(d * xh)

    ins = [x, g, dy] + ([res] if has_res else [])
    in_specs = [_rows(ts, c, x_cb), _const(1, c), _rows(ts, c)] + ([_rows(ts, c)] if has_res else [])
    return pl.pallas_call(
        body, name=name, grid=(s_ // ts,), in_specs=in_specs,
        out_specs=[_rows(ts, c), _rows(ts, c), _const(SUBLANES, c)],
        out_shape=[jax.ShapeDtypeStruct((s_, c), F32), jax.ShapeDtypeStruct((s_, c), MXU_DTYPE),
                   jax.ShapeDtypeStruct((SUBLANES, c), F32)],
        compiler_params=_cparams(("arbitrary",)),
    )(*ins)


def _attn_prep(p, gq, gkv, tabs, *, ts=256):
    s_ = p.shape[0]
    a_cos, a_up, a_dn, _, k_cos, b_up, b_dn = tabs

    def body(q_ref, k_ref, v_ref, ql_ref, kvl_ref, kr_ref, gq_ref, gkv_ref, ac, au, ad, kc, bu, bd,
             qa_ref, ka_ref, va_ref, cq_ref, ckv_ref, kro_ref):
        c_, u_, d_ = ac[...], au[...], ad[...]
        for h in range(SWA_HEADS):
            sl = slice(h * LANES, (h + 1) * LANES)
            qa_ref[:, sl] = _rope(q_ref[:, sl], c_, u_, d_, 96, 32).astype(qa_ref.dtype)
        for h in range(SWA_KV_HEADS):
            sl = slice(h * LANES, (h + 1) * LANES)
            ka_ref[:, sl] = _rope(k_ref[:, sl], c_, u_, d_, 96, 32).astype(ka_ref.dtype)
        va_ref[...] = v_ref[...].astype(va_ref.dtype)
        for src, gref, dst in ((ql_ref, gq_ref, cq_ref), (kvl_ref, gkv_ref, ckv_ref)):
            v = src[...]
            r = lax.rsqrt(jnp.mean(v * v, axis=-1, keepdims=True) + EPS)
            dst[...] = (v * r * gref[...]).astype(dst.dtype)
        kro_ref[...] = _rope(kr_ref[...], kc[...], bu[...], bd[...], 112, 16)

    tab = _rows(ts, LANES)
    return pl.pallas_call(
        body, name="attn_prep", grid=(s_ // ts,),
        in_specs=[_rows(ts, 1024, P_Q // 1024), _rows(ts, 256, P_K // 256), _rows(ts, 256, P_V // 256),
                  _rows(ts, Q_LORA, P_QLAT // Q_LORA), _rows(ts, KV_LORA, P_KVLAT // KV_LORA),
                  _rows(ts, LANES, P_KR // LANES), _const(1, Q_LORA), _const(1, KV_LORA), tab, tab, tab, tab, tab, tab],
        out_specs=[_rows(ts, 1024), _rows(ts, 256), _rows(ts, 256), _rows(ts, Q_LORA), _rows(ts, KV_LORA),
                   _rows(ts, LANES)],
        out_shape=[jax.ShapeDtypeStruct((s_, 1024), MXU_DTYPE), jax.ShapeDtypeStruct((s_, 256), MXU_DTYPE),
                   jax.ShapeDtypeStruct((s_, 256), MXU_DTYPE), jax.ShapeDtypeStruct((s_, Q_LORA), MXU_DTYPE),
                   jax.ShapeDtypeStruct((s_, KV_LORA), MXU_DTYPE), jax.ShapeDtypeStruct((s_, LANES), F32)],
        compiler_params=_cparams(("parallel",)),
    )(p, p, p, p, p, p, gq, gkv, a_cos, a_up, a_dn, k_cos, b_up, b_dn)


def _mla_prep(qp, kp, kro, tabs, *, ts=256):
    s_ = qp.shape[0]
    _, _, _, q_cos, _, b_up, b_dn = tabs

    def body(q_ref, k_ref, kr_ref, qc, bu, bd, qo_ref, ko_ref):
        c_, u_, d_ = qc[...], bu[...], bd[...]
        kr = kr_ref[...]
        for h in range(MLA_HEADS):
            sl = slice(h * LANES, (h + 1) * LANES)
            qo_ref[:, sl] = _rope(q_ref[:, sl], c_, u_, d_, 112, 16).astype(qo_ref.dtype)
            ko_ref[:, sl] = (k_ref[:, sl] + kr).astype(ko_ref.dtype)

    tab = _rows(ts, LANES)
    return pl.pallas_call(
        body, name="mla_prep", grid=(s_ // ts,),
        in_specs=[_rows(ts, 1024), _rows(ts, 1024), tab, tab, tab, tab],
        out_specs=[_rows(ts, 1024), _rows(ts, 1024)],
        out_shape=[jax.ShapeDtypeStruct((s_, 1024), MXU_DTYPE)] * 2,
        compiler_params=_cparams(("parallel",)),
    )(qp, kp, kro, q_cos, b_up, b_dn)


def _mla_unprep(dqc, dkc, tabs, *, ts=256):
    s_ = dqc.shape[0]
    _, _, _, q_cos, k_cos, b_up, b_dn = tabs

    def body(dq_ref, dk_ref, qc, kc, bu, bd, dqo_ref, dko_ref, dkr_ref):
        c_, u_, d_ = qc[...], bu[...], bd[...]
        tot = jnp.zeros((ts, LANES), F32)
        for h in range(MLA_HEADS):
            sl = slice(h * LANES, (h + 1) * LANES)
            dqo_ref[:, sl] = _rope_t(dq_ref[:, sl], c_, u_, d_, 112, 16).astype(dqo_ref.dtype)
            dk = dk_ref[:, sl]
            dko_ref[:, sl] = dk.astype(dko_ref.dtype)
            tot = tot + dk
        dkr_ref[...] = _rope_t(tot, kc[...], u_, d_, 112, 16).astype(dkr_ref.dtype)

    tab = _rows(ts, LANES)
    return pl.pallas_call(
        body, name="mla_unprep", grid=(s_ // ts,),
        in_specs=[_rows(ts, 1024), _rows(ts, 1024), tab, tab, tab, tab],
        out_specs=[_rows(ts, 1024), _rows(ts, 1024), _rows(ts, LANES)],
        out_shape=[jax.ShapeDtypeStruct((s_, 1024), MXU_DTYPE)] * 2 + [jax.ShapeDtypeStruct((s_, LANES), MXU_DTYPE)],
        compiler_params=_cparams(("parallel",)),
    )(dqc, dkc, q_cos, k_cos, b_up, b_dn)


def _swa_unrope(dqa, dka, tabs, *, ts=256):
    s_ = dqa.shape[0]
    a_cos, a_up, a_dn = tabs[0], tabs[1], tabs[2]

    def body(dq_ref, dk_ref, ac, au, ad, dqo_ref, dko_ref):
        c_, u_, d_ = ac[...], au[...], ad[...]
        for h in range(SWA_HEADS):
            sl = slice(h * LANES, (h + 1) * LANES)
            dqo_ref[:, sl] = _rope_t(dq_ref[:, sl], c_, u_, d_, 96, 32).astype(dqo_ref.dtype)
        for h in range(SWA_KV_HEADS):
            sl = slice(h * LANES, (h + 1) * LANES)
            dko_ref[:, sl] = _rope_t(dk_ref[:, sl], c_, u_, d_, 96, 32).astype(dko_ref.dtype)

    tab = _rows(ts, LANES)
    return pl.pallas_call(
        body, name="swa_unrope", grid=(s_ // ts,),
        in_specs=[_rows(ts, 1024), _rows(ts, 256), tab, tab, tab],
        out_specs=[_rows(ts, 1024), _rows(ts, 256)],
        out_shape=[jax.ShapeDtypeStruct((s_, 1024), MXU_DTYPE), jax.ShapeDtypeStruct((s_, 256), MXU_DTYPE)],
        compiler_params=_cparams(("parallel",)),
    )(dqa, dka, a_cos, a_up, a_dn)


def _gate_fwd(p, ta, tb, *, ts=256):
    s_ = p.shape[0]

    def body(ga_ref, gb_ref, ta_ref, tb_ref, y_ref):
        y = _sigmoid(ga_ref[...]) * ta_ref[...] + _sigmoid(gb_ref[...]) * tb_ref[...]
        y_ref[...] = y.astype(y_ref.dtype)

    return pl.pallas_call(
        body, name="gate_fwd", grid=(s_ // ts,),
        in_specs=[_rows(ts, 1024, P_GA // 1024), _rows(ts, 1024, P_GB // 1024), _rows(ts, 1024), _rows(ts, 1024)],
        out_specs=_rows(ts, 1024), out_shape=jax.ShapeDtypeStruct((s_, 1024), MXU_DTYPE),
        compiler_params=_cparams(("parallel",)),
    )(p, p, ta, tb)


def _gate_bwd(p, ta, tb, dy, *, ts=256):
    s_ = p.shape[0]

    def body(ga_ref, gb_ref, ta_ref, tb_ref, dy_ref, dta_ref, dtb_ref, dg_ref):
        d = dy_ref[...]
        sa, sb = _sigmoid(ga_ref[...]), _sigmoid(gb_ref[...])
        dta_ref[...] = (d * sa).astype(dta_ref.dtype)
        dtb_ref[...] = (d * sb).astype(dtb_ref.dtype)
        dg_ref[:, 0:1024] = (d * ta_ref[...] * (sa * (1.0 - sa))).astype(dg_ref.dtype)
        dg_ref[:, 1024:2048] = (d * tb_ref[...] * (sb * (1.0 - sb))).astype(dg_ref.dtype)

    return pl.pallas_call(
        body, name="gate_bwd", grid=(s_ // ts,),
        in_specs=[_rows(ts, 1024, P_GA // 1024), _rows(ts, 1024, P_GB // 1024), _rows(ts, 1024), _rows(ts, 1024),
                  _rows(ts, 1024)],
        out_specs=[_rows(ts, 1024), _rows(ts, 1024), _rows(ts, 2048)],
        out_shape=[jax.ShapeDtypeStruct((s_, 1024), MXU_DTYPE)] * 2 + [jax.ShapeDtypeStruct((s_, 2048), MXU_DTYPE)],
        compiler_params=_cparams(("parallel",)),
    )(p, p, ta, tb, dy)


def _swiglu_fwd(gu, *, ts=256):
    s_ = gu.shape[0]

    def body(g_ref, u_ref, a_ref):
        g = g_ref[...]
        a_ref[...] = (g * _sigmoid(g) * u_ref[...]).astype(a_ref.dtype)

    return pl.pallas_call(
        body, name="swiglu_fwd", grid=(s_ // ts,), in_specs=[_rows(ts, D_FF, 0), _rows(ts, D_FF, 1)],
        out_specs=_rows(ts, D_FF), out_shape=jax.ShapeDtypeStruct((s_, D_FF), MXU_DTYPE),
        compiler_params=_cparams(("parallel",)),
    )(gu, gu)


def _swiglu_bwd(gu, da, *, ts=256):
    s_ = gu.shape[0]

    def body(g_ref, u_ref, da_ref, o_ref):
        g, u, d = g_ref[...], u_ref[...], da_ref[...]
        sg = _sigmoid(g)
        o_ref[:, 0:D_FF] = (d * u * (sg * (1.0 + g * (1.0 - sg)))).astype(o_ref.dtype)
        o_ref[:, D_FF:2 * D_FF] = (d * (g * sg)).astype(o_ref.dtype)

    return pl.pallas_call(
        body, name="swiglu_bwd", grid=(s_ // ts,), in_specs=[_rows(ts, D_FF, 0), _rows(ts, D_FF, 1), _rows(ts, D_FF)],
        out_specs=_rows(ts, 2 * D_FF), out_shape=jax.ShapeDtypeStruct((s_, 2 * D_FF), MXU_DTYPE),
        compiler_params=_cparams(("parallel",)),
    )(gu, gu, da)


def _loss_bwd(x2, g, tgt, *, ts=256):
    s_, c = x2.shape

    def body(x_ref, g_ref, t_ref, dx_ref, dxb_ref, dg_ref, lp_ref, tot_ref):
        v = x_ref[...]
        r = lax.rsqrt(jnp.mean(v * v, axis=-1, keepdims=True) + EPS)
        xh = v * r
        gg = g_ref[...]
        e = xh * gg - t_ref[...]
        do = e * (1.0 / c)
        dxh = do * gg
        dx = r * (dxh - xh * jnp.mean(dxh * xh, axis=-1, keepdims=True))
        dx_ref[...] = dx
        dxb_ref[...] = dx.astype(dxb_ref.dtype)
        i = pl.program_id(0)

        @pl.when(i == 0)
        def _():
            dg_ref[...] = jnp.zeros(dg_ref.shape, F32)
            lp_ref[...] = jnp.zeros(lp_ref.shape, F32)

        dg_ref[...] += _sublane_sum(do * xh)
        lp_ref[...] += _sublane_sum(e * e)
        tot_ref[...] = jnp.full(tot_ref.shape, (0.5 / c) * jnp.sum(lp_ref[...]), F32)

    return pl.pallas_call(
        body, name="loss_bwd", grid=(s_ // ts,), in_specs=[_rows(ts, c), _const(1, c), _rows(ts, c)],
        out_specs=[_rows(ts, c), _rows(ts, c), _const(SUBLANES, c), _const(SUBLANES, c), _const(SUBLANES, LANES)],
        out_shape=[jax.ShapeDtypeStruct((s_, c), F32), jax.ShapeDtypeStruct((s_, c), MXU_DTYPE),
                   jax.ShapeDtypeStruct((SUBLANES, c), F32), jax.ShapeDtypeStruct((SUBLANES, c), F32),
                   jax.ShapeDtypeStruct((SUBLANES, LANES), F32)],
        compiler_params=_cparams(("arbitrary",)),
    )(x2, g, tgt)


def _mla_bwd_prep(dob, o32, *, ts=256):
    s_ = dob.shape[0]

    def body(do_ref, o_ref, dob_ref, dl_ref):
        d = do_ref[...]
        dob_ref[...] = d.astype(dob_ref.dtype)
        prod = d * o_ref[...]
        for h in range(MLA_HEADS):
            dl_ref[h] = jnp.sum(prod[:, h * LANES:(h + 1) * LANES], axis=-1, keepdims=True)

    return pl.pallas_call(
        body, name="mla_bwd_prep", grid=(s_ // ts,), in_specs=[_rows(ts, 1024), _rows(ts, 1024)],
        out_specs=[_rows(ts, 1024), pl.BlockSpec((MLA_HEADS, ts, 1), lambda i: (0, i, 0))],
        out_shape=[jax.ShapeDtypeStruct((s_, 1024), MXU_DTYPE), jax.ShapeDtypeStruct((MLA_HEADS, s_, 1), F32)],
        compiler_params=_cparams(("parallel",)),
    )(dob, o32)


SWA_T = 4 * BLOCK


def _swa_masks():
    qi = lax.broadcasted_iota(jnp.int32, (BLOCK, BLOCK), 0)
    kj = lax.broadcasted_iota(jnp.int32, (BLOCK, BLOCK), 1)
    return kj > qi, kj <= qi


def _swa_in_specs(rev, nsb):
    sbi = (lambda j: nsb - 1 - j) if rev else (lambda j: j)
    cur = pl.BlockSpec((SWA_T, LANES), lambda g, j: (sbi(j), g))
    prev = pl.BlockSpec((BLOCK, LANES), lambda g, j: (jnp.maximum(4 * sbi(j) - 1, 0), g))
    q = pl.BlockSpec((SWA_T, SWA_GROUP * LANES), lambda g, j: (sbi(j), g))
    sink = pl.BlockSpec((1, SUBLANES, LANES), lambda g, j: (g, 0, 0))
    lse = pl.BlockSpec((SWA_GROUP, SWA_T, 1), lambda g, j: (g, sbi(j), 0))
    return q, cur, prev, sink, lse


def _swa_fwd(qa, ka, va, sink_b):
    s_ = qa.shape[0]
    nsb = s_ // SWA_T
    scale = HEAD_DIM ** -0.5

    def body(q_ref, kc_ref, kp_ref, vc_ref, vp_ref, sk_ref, o32_ref, o16_ref, lse_ref):
        sb = pl.program_id(1)
        m_prev, m_cur = _swa_masks()
        for hh in range(SWA_GROUP):
            sink = sk_ref[0, hh:hh + 1, 0:1]
            cs = slice(hh * LANES, (hh + 1) * LANES)
            for b in range(4):
                rs = slice(b * BLOCK, (b + 1) * BLOCK)
                q = q_ref[rs, cs]
                kc, vc = kc_ref[rs, :], vc_ref[rs, :]
                if b == 0:
                    kp, vp, mp = kp_ref[...], vp_ref[...], jnp.logical_and(m_prev, sb > 0)
                else:
                    ps = slice((b - 1) * BLOCK, b * BLOCK)
                    kp, vp, mp = kc_ref[ps, :], vc_ref[ps, :], m_prev
                s_c = jnp.where(m_cur, lax.dot_general(q, kc, NT, preferred_element_type=F32) * scale, -jnp.inf)
                s_p = jnp.where(mp, lax.dot_general(q, kp, NT, preferred_element_type=F32) * scale, -jnp.inf)
                m = jnp.maximum(jnp.maximum(jnp.max(s_c, -1, keepdims=True), jnp.max(s_p, -1, keepdims=True)), sink)
                p_c, p_p = jnp.exp(s_c - m), jnp.exp(s_p - m)
                den = jnp.sum(p_c, -1, keepdims=True) + jnp.sum(p_p, -1, keepdims=True) + jnp.exp(sink - m)
                inv = 1.0 / den
                o = (jnp.dot((p_c * inv).astype(MXU_DTYPE), vc, preferred_element_type=F32)
                     + jnp.dot((p_p * inv).astype(MXU_DTYPE), vp, preferred_element_type=F32))
                o32_ref[rs, cs] = o
                o16_ref[rs, cs] = o.astype(o16_ref.dtype)
                lse_ref[hh, rs, :] = m + jnp.log(den)

    q, cur, prev, sink, lse = _swa_in_specs(False, nsb)
    return pl.pallas_call(
        body, name="swa_fwd", grid=(SWA_KV_HEADS, nsb), in_specs=[q, cur, prev, cur, prev, sink],
        out_specs=[q, q, lse],
        out_shape=[jax.ShapeDtypeStruct((s_, SWA_HEADS * LANES), F32), jax.ShapeDtypeStruct((s_, SWA_HEADS * LANES), MXU_DTYPE),
                   jax.ShapeDtypeStruct((SWA_HEADS, s_, 1), F32)],
        compiler_params=_cparams(("parallel", "arbitrary")),
    )(qa, ka, ka, va, va, sink_b)


def _swa_bwd(qa, ka, va, sink_b, o32, do, lse):
    s_ = qa.shape[0]
    nsb = s_ // SWA_T
    scale = HEAD_DIM ** -0.5

    def body(q_ref, kc_ref, kp_ref, vc_ref, vp_ref, sk_ref, o_ref, do_ref, lse_ref,
             dq_ref, dk_ref, dv_ref, dsk_ref, kacc, vacc, kcar, vcar):
        j = pl.program_id(1)
        sb = nsb - 1 - j
        m_prev, m_cur = _swa_masks()
        kacc[...] = jnp.zeros(kacc.shape, F32)
        vacc[...] = jnp.zeros(vacc.shape, F32)

        @pl.when(j == 0)
        def _():
            kcar[...] = jnp.zeros(kcar.shape, F32)
            vcar[...] = jnp.zeros(vcar.shape, F32)
            dsk_ref[...] = jnp.zeros(dsk_ref.shape, F32)

        for hh in range(SWA_GROUP):
            sink = sk_ref[0, hh:hh + 1, 0:1]
            cs = slice(hh * LANES, (hh + 1) * LANES)
            dsink = jnp.zeros((1, 1), F32)
            for b in range(4):
                rs = slice(b * BLOCK, (b + 1) * BLOCK)
                q = q_ref[rs, cs]
                kc, vc = kc_ref[rs, :], vc_ref[rs, :]
                if b == 0:
                    kp, vp, mp = kp_ref[...], vp_ref[...], jnp.logical_and(m_prev, sb > 0)
                else:
                    ps = slice((b - 1) * BLOCK, b * BLOCK)
                    kp, vp, mp = kc_ref[ps, :], vc_ref[ps, :], m_prev
                d = do_ref[rs, cs]
                delta = jnp.sum(d * o_ref[rs, cs], -1, keepdims=True)
                l = lse_ref[hh, rs, :]
                s_c = jnp.where(m_cur, lax.dot_general(q, kc, NT, preferred_element_type=F32) * scale, -jnp.inf)
                s_p = jnp.where(mp, lax.dot_general(q, kp, NT, preferred_element_type=F32) * scale, -jnp.inf)
                p_c, p_p = jnp.exp(s_c - l), jnp.exp(s_p - l)
                db = d.astype(MXU_DTYPE)
                ds_c = (p_c * (lax.dot_general(db, vc, NT, preferred_element_type=F32) - delta) * scale).astype(MXU_DTYPE)
                ds_p = (p_p * (lax.dot_general(db, vp, NT, preferred_element_type=F32) - delta) * scale).astype(MXU_DTYPE)
                dq_ref[rs, cs] = (jnp.dot(ds_c, kc, preferred_element_type=F32)
                                  + jnp.dot(ds_p, kp, preferred_element_type=F32))
                hi = slice((b + 1) * BLOCK, (b + 2) * BLOCK)
                kacc[hi, :] += lax.dot_general(ds_c, q, TN, preferred_element_type=F32)
                kacc[rs, :] += lax.dot_general(ds_p, q, TN, preferred_element_type=F32)
                vacc[hi, :] += lax.dot_general(p_c.astype(MXU_DTYPE), db, TN, preferred_element_type=F32)
                vacc[rs, :] += lax.dot_general(p_p.astype(MXU_DTYPE), db, TN, preferred_element_type=F32)
                dsink = dsink - jnp.sum(jnp.exp(sink - l) * delta, axis=0, keepdims=True)
            dsk_ref[0, hh:hh + 1, :] += jnp.broadcast_to(dsink, (1, LANES))

        dk_ref[0:3 * BLOCK, :] = kacc[BLOCK:4 * BLOCK, :]
        dk_ref[3 * BLOCK:4 * BLOCK, :] = kacc[4 * BLOCK:5 * BLOCK, :] + kcar[...]
        dv_ref[0:3 * BLOCK, :] = vacc[BLOCK:4 * BLOCK, :].astype(dv_ref.dtype)
        dv_ref[3 * BLOCK:4 * BLOCK, :] = (vacc[4 * BLOCK:5 * BLOCK, :] + vcar[...]).astype(dv_ref.dtype)
        kcar[...] = kacc[0:BLOCK, :]
        vcar[...] = vacc[0:BLOCK, :]

    q, cur, prev, sink, lse_spec = _swa_in_specs(True, nsb)
    return pl.pallas_call(
        body, name="swa_bwd", grid=(SWA_KV_HEADS, nsb),
        in_specs=[q, cur, prev, cur, prev, sink, q, q, lse_spec],
        out_specs=[q, cur, cur, sink],
        out_shape=[jax.ShapeDtypeStruct((s_, SWA_HEADS * LANES), F32), jax.ShapeDtypeStruct((s_, SWA_KV_HEADS * LANES), F32),
                   jax.ShapeDtypeStruct((s_, SWA_KV_HEADS * LANES), MXU_DTYPE),
                   jax.ShapeDtypeStruct((SWA_KV_HEADS, SUBLANES, LANES), F32)],
        scratch_shapes=[pltpu.VMEM((5 * BLOCK, LANES), F32), pltpu.VMEM((5 * BLOCK, LANES), F32),
                        pltpu.VMEM((BLOCK, LANES), F32), pltpu.VMEM((BLOCK, LANES), F32)],
        compiler_params=_cparams(("arbitrary", "arbitrary")),
    )(qa, ka, ka, va, va, sink_b, o32, do, lse)


MLA_T = 512


def _tri_pairs(n, key_major):
    if key_major:
        pairs = [(qi, ki) for ki in range(n) for qi in range(ki, n)]
    else:
        pairs = [(qi, ki) for qi in range(n) for ki in range(qi + 1)]
    return (jnp.asarray([p[0] for p in pairs], jnp.int32), jnp.asarray([p[1] for p in pairs], jnp.int32), len(pairs))


def _causal_scores(q, k, qi, ki, t, scale):
    s = lax.dot_general(q, k, NT, preferred_element_type=F32) * scale
    r = lax.broadcasted_iota(jnp.int32, (t, t), 0) + qi * t
    c = lax.broadcasted_iota(jnp.int32, (t, t), 1) + ki * t
    return jnp.where(c <= r, s, -jnp.inf)


def _mla_fwd(qc, kc, vp):
    s_ = qc.shape[0]
    t = min(MLA_T, s_)
    n = s_ // t
    qt, kt, npairs = _tri_pairs(n, key_major=False)
    scale = MLA_QK ** -0.5

    def body(qt_ref, kt_ref, q_ref, k_ref, v_ref, o32_ref, o16_ref, lse_ref, m_s, l_s, acc_s):
        p_ = pl.program_id(1)
        qi, ki = qt_ref[p_], kt_ref[p_]

        @pl.when(ki == 0)
        def _():
            m_s[...] = jnp.full(m_s.shape, -jnp.inf, F32)
            l_s[...] = jnp.zeros(l_s.shape, F32)
            acc_s[...] = jnp.zeros(acc_s.shape, F32)

        s = _causal_scores(q_ref[...], k_ref[...], qi, ki, t, scale)
        m_new = jnp.maximum(m_s[...], jnp.max(s, -1, keepdims=True))
        alpha = jnp.exp(m_s[...] - m_new)
        p = jnp.exp(s - m_new)
        l_s[...] = alpha * l_s[...] + jnp.sum(p, -1, keepdims=True)
        acc_s[...] = alpha * acc_s[...] + jnp.dot(p.astype(MXU_DTYPE), v_ref[...], preferred_element_type=F32)
        m_s[...] = m_new

        @pl.when(ki == qi)
        def _():
            o = acc_s[...] / l_s[...]
            o32_ref[...] = o
            o16_ref[...] = o.astype(o16_ref.dtype)
            lse_ref[0] = m_s[...] + jnp.log(l_s[...])

    qs = pl.BlockSpec((t, LANES), lambda h, p, qt_, kt_: (qt_[p], h))
    ks = pl.BlockSpec((t, LANES), lambda h, p, qt_, kt_: (kt_[p], h))
    ls = pl.BlockSpec((1, t, 1), lambda h, p, qt_, kt_: (h, qt_[p], 0))
    return pl.pallas_call(
        body, name="mla_fwd",
        grid_spec=pltpu.PrefetchScalarGridSpec(
            num_scalar_prefetch=2, grid=(MLA_HEADS, npairs), in_specs=[qs, ks, ks], out_specs=[qs, qs, ls],
            scratch_shapes=[pltpu.VMEM((t, 1), F32), pltpu.VMEM((t, 1), F32), pltpu.VMEM((t, LANES), F32)]),
        out_shape=[jax.ShapeDtypeStruct((s_, MLA_HEADS * LANES), F32), jax.ShapeDtypeStruct((s_, MLA_HEADS * LANES), MXU_DTYPE),
                   jax.ShapeDtypeStruct((MLA_HEADS, s_, 1), F32)],
        compiler_params=_cparams(("parallel", "arbitrary")),
    )(qt, kt, qc, kc, vp)


def _mla_bwd(qc, kc, vp, dob, lse, delta):
    s_ = qc.shape[0]
    t = min(MLA_T, s_)
    n = s_ // t
    qt, kt, npairs = _tri_pairs(n, key_major=True)
    scale = MLA_QK ** -0.5

    def body(qt_ref, kt_ref, q_ref, k_ref, v_ref, do_ref, lse_ref, dl_ref, dq_ref, dk_ref, dv_ref, dk_s, dv_s):
        p_ = pl.program_id(1)
        qi, ki = qt_ref[p_], kt_ref[p_]

        @pl.when(p_ == 0)
        def _():
            dq_ref[...] = jnp.zeros(dq_ref.shape, F32)

        @pl.when(qi == ki)
        def _():
            dk_s[...] = jnp.zeros(dk_s.shape, F32)
            dv_s[...] = jnp.zeros(dv_s.shape, F32)

        q, k, v, d = q_ref[...], k_ref[...], v_ref[...], do_ref[...]
        s = _causal_scores(q, k, qi, ki, t, scale)
        p = jnp.exp(s - lse_ref[0])
        dp = lax.dot_general(d, v, NT, preferred_element_type=F32)
        ds = (p * (dp - dl_ref[0]) * scale).astype(MXU_DTYPE)
        dv_s[...] += lax.dot_general(p.astype(MXU_DTYPE), d, TN, preferred_element_type=F32)
        dk_s[...] += lax.dot_general(ds, q, TN, preferred_element_type=F32)
        off = pl.multiple_of(qi * t, t)
        dq_ref[pl.ds(off, t), :] += jnp.dot(ds, k, preferred_element_type=F32)

        @pl.when(qi == n - 1)
        def _():
            dk_ref[...] = dk_s[...]
            dv_ref[...] = dv_s[...].astype(dv_ref.dtype)

    qs = pl.BlockSpec((t, LANES), lambda h, p, qt_, kt_: (qt_[p], h))
    ks = pl.BlockSpec((t, LANES), lambda h, p, qt_, kt_: (kt_[p], h))
    ls = pl.BlockSpec((1, t, 1), lambda h, p, qt_, kt_: (h, qt_[p], 0))
    dqs = pl.BlockSpec((s_, LANES), lambda h, p, qt_, kt_: (0, h))
    return pl.pallas_call(
        body, name="mla_bwd",
        grid_spec=pltpu.PrefetchScalarGridSpec(
            num_scalar_prefetch=2, grid=(MLA_HEADS, npairs), in_specs=[qs, ks, ks, qs, ls, ls], out_specs=[dqs, ks, ks],
            scratch_shapes=[pltpu.VMEM((t, LANES), F32), pltpu.VMEM((t, LANES), F32)]),
        out_shape=[jax.ShapeDtypeStruct((s_, MLA_HEADS * LANES), F32), jax.ShapeDtypeStruct((s_, MLA_HEADS * LANES), F32),
                   jax.ShapeDtypeStruct((s_, MLA_HEADS * LANES), MXU_DTYPE)],
        compiler_params=_cparams(("parallel", "arbitrary")),
    )(qt, kt, qc, kc, vp, dob, lse, delta)


def _pad_heads(w, nh, hd, axis):
    shp = w.shape
    w = w.reshape(shp[:axis] + (nh, hd) + shp[axis + 1:])
    pad = [(0, 0)] * w.ndim
    pad[axis + 1] = (0, LANES - hd)
    w = jnp.pad(w, pad)
    return w.reshape(shp[:axis] + (nh * LANES,) + shp[axis + 1:])


def _unpad_heads(w, nh, hd, axis):
    shp = w.shape
    w = w.reshape(shp[:axis] + (nh, LANES) + shp[axis + 1:])
    w = lax.slice_in_dim(w, 0, hd, axis=axis + 1)
    return w.reshape(shp[:axis] + (nh * hd,) + shp[axis + 1:])


def _to_operands(w):
    w_in = w["w_in"]
    piece = lambda i: w_in[:, IN_OFF[i]:IN_OFF[i + 1]]
    z = lambda n: jnp.zeros((D_MODEL, n), w_in.dtype)
    kr = jnp.concatenate([z(KR_LANE), piece(5), z(LANES - KR_LANE - MLA_ROPE)], 1)
    win = jnp.concatenate([piece(6), piece(7), _pad_heads(piece(0), SWA_HEADS, HEAD_DIM, 1), piece(3), kr,
                           _pad_heads(piece(1), SWA_KV_HEADS, HEAD_DIM, 1), _pad_heads(piece(2), SWA_KV_HEADS, HEAD_DIM, 1),
                           piece(4)], 1)
    ukv = w["w_ukv"].reshape(KV_LORA, MLA_HEADS, MLA_NOPE + MLA_V)
    return dict(
        win=win,
        wuq=_pad_heads(w["w_uq"], MLA_HEADS, MLA_QK, 1),
        wuk=_pad_heads(ukv[:, :, :MLA_NOPE].reshape(KV_LORA, -1), MLA_HEADS, MLA_NOPE, 1),
        wuv=_pad_heads(ukv[:, :, MLA_NOPE:].reshape(KV_LORA, -1), MLA_HEADS, MLA_V, 1),
        woa=_pad_heads(w["w_o_swa"], SWA_HEADS, HEAD_DIM, 0),
        wob=_pad_heads(w["w_o_mla"], MLA_HEADS, MLA_V, 0),
        wout=w["w_out"],
        wgu=jnp.concatenate([w["w_gate"], w["w_up"]], 1),
        wd=w["w_down"],
    )


def _from_operand_grads(g):
    d = g["win"]
    dkr = d[:, P_KR + KR_LANE:P_KR + KR_LANE + MLA_ROPE]
    w_in = jnp.concatenate([
        _unpad_heads(d[:, P_Q:P_Q + 1024], SWA_HEADS, HEAD_DIM, 1), _unpad_heads(d[:, P_K:P_K + 256], SWA_KV_HEADS, HEAD_DIM, 1),
        _unpad_heads(d[:, P_V:P_V + 256], SWA_KV_HEADS, HEAD_DIM, 1), d[:, P_QLAT:P_QLAT + Q_LORA],
        d[:, P_KVLAT:P_KVLAT + KV_LORA], dkr, d[:, P_GA:P_GA + 1024], d[:, P_GB:P_GB + 1024]], 1)
    uk = _unpad_heads(g["wuk"], MLA_HEADS, MLA_NOPE, 1).reshape(KV_LORA, MLA_HEADS, MLA_NOPE)
    uv = _unpad_heads(g["wuv"], MLA_HEADS, MLA_V, 1).reshape(KV_LORA, MLA_HEADS, MLA_V)
    return dict(
        w_in=w_in,
        w_uq=_unpad_heads(g["wuq"], MLA_HEADS, MLA_QK, 1),
        w_ukv=jnp.concatenate([uk, uv], 2).reshape(KV_LORA, -1),
        w_o_swa=_unpad_heads(g["woa"], SWA_HEADS, HEAD_DIM, 0),
        w_o_mla=_unpad_heads(g["wob"], MLA_HEADS, MLA_V, 0),
        w_out=g["wout"],
        w_gate=g["wgu"][:, :D_FF],
        w_up=g["wgu"][:, D_FF:],
        w_down=g["wd"],
    )


def _local_step(x, tgt, ops, small):
    s_ = x.shape[0]
    tabs = _rope_tables(s_)
    sink_b = jnp.broadcast_to(small["swa_sinks"].reshape(SWA_KV_HEADS, SWA_GROUP, 1), (SWA_KV_HEADS, SWA_GROUP, LANES))
    sink_b = jnp.pad(sink_b, ((0, 0), (0, SUBLANES - SWA_GROUP), (0, 0)))

    h = _norm_fwd(x, small["mix_norm_g"], name="norm1")
    p = _mm(h, ops["win"], "nn", name="proj_in", tn=2176)
    qa, ka, va, cq, ckv, kro = _attn_prep(p, small["q_norm_g"], small["kv_norm_g"], tabs)
    oa32, oa16, lse_a = _swa_fwd(qa, ka, va, sink_b)
    qp = _mm(cq, ops["wuq"], "nn", name="mla_q_up", tn=1024)
    kp = _mm(ckv, ops["wuk"], "nn", name="mla_k_up", tn=1024)
    vp = _mm(ckv, ops["wuv"], "nn", name="mla_v_up", tn=1024, out_dtype=MXU_DTYPE)
    qc, kc = _mla_prep(qp, kp, kro, tabs)
    ob32, ob16, lse_b = _mla_fwd(qc, kc, vp)
    ta = _mm(oa16, ops["woa"], "nn", name="o_swa", tn=1024)
    tb = _mm(ob16, ops["wob"], "nn", name="o_mla", tn=1024)
    y = _gate_fwd(p, ta, tb)
    x1 = _mm(y, ops["wout"], "nn", name="out_proj", add=x, tn=1024)
    h2 = _norm_fwd(x1, small["ffn_norm_g"], name="norm2")
    gu = _mm(h2, ops["wgu"], "nn", name="ffn_in", tn=512)
    act = _swiglu_fwd(gu)
    x2 = _mm(act, ops["wd"], "nn", name="ffn_out", add=x1, tn=1024)

    dx2, dx2b, dg3, _, tot = _loss_bwd(x2, small["final_norm_g"].reshape(1, D_MODEL), tgt)
    g = {}
    dact = _mm(dx2b, ops["wd"], "nt", name="d_act", tn=1408)
    g["wd"] = _mm(act, dx2b, "tn", name="dw_down", tm=1408, tn=1024, tk=512)
    dgu = _swiglu_bwd(gu, dact)
    dh2 = _mm(dgu, ops["wgu"], "nt", name="d_h2", tn=1024, tk=2816)
    g["wgu"] = _mm(h2, dgu, "tn", name="dw_ffn_in", tm=1024, tn=512, tk=512)
    dx1, dx1b, dg2 = _norm_bwd(x1, small["ffn_norm_g"], dh2, dx2, name="norm2_bwd")
    dy = _mm(dx1b, ops["wout"], "nt", name="d_y", tn=1024)
    g["wout"] = _mm(y, dx1b, "tn", name="dw_out", tm=1024, tn=1024, tk=512)
    dta, dtb, dgab = _gate_bwd(p, ta, tb, dy)
    doa = _mm(dta, ops["woa"], "nt", name="d_oa", tn=1024)
    g["woa"] = _mm(oa16, dta, "tn", name="dw_o_swa", tm=1024, tn=1024, tk=512)
    dob = _mm(dtb, ops["wob"], "nt", name="d_ob", tn=1024)
    g["wob"] = _mm(ob16, dtb, "tn", name="dw_o_mla", tm=1024, tn=1024, tk=512)
    dob16, delta_b = _mla_bwd_prep(dob, ob32)
    dqc, dkc, dvp = _mla_bwd(qc, kc, vp, dob16, lse_b, delta_b)
    dqp, dkp, dkr = _mla_unprep(dqc, dkc, tabs)
    dcq = _mm(dqp, ops["wuq"], "nt", name="d_cq", tn=Q_LORA)
    g["wuq"] = _mm(cq, dqp, "tn", name="dw_uq", tm=Q_LORA, tn=1024, tk=512)
    dckv = _mm(jnp.concatenate([dkp, dvp], 1), jnp.concatenate([ops["wuk"], ops["wuv"]], 1), "nt", name="d_ckv", tn=KV_LORA)
    g["wuk"] = _mm(ckv, dkp, "tn", name="dw_uk", tm=KV_LORA, tn=1024, tk=512)
    g["wuv"] = _mm(ckv, dvp, "tn", name="dw_uv", tm=KV_LORA, tn=1024, tk=512)
    _, dqlat, dgq = _norm_bwd(p, small["q_norm_g"], dcq, None, name="qnorm_bwd", x_cb=P_QLAT // Q_LORA)
    _, dkvlat, dgkv = _norm_bwd(p, small["kv_norm_g"], dckv, None, name="kvnorm_bwd", x_cb=P_KVLAT // KV_LORA)
    dqa, dka, dva, dsk = _swa_bwd(qa, ka, va, sink_b, oa32, doa, lse_a)
    dq_raw, dk_raw = _swa_unrope(dqa, dka, tabs)
    dp = jnp.concatenate([dgab, dq_raw, dqlat, dkr, dk_raw, dva, dkvlat], 1)
    dh = _mm(dp, ops["win"], "nt", name="d_h", tn=1024, tk=2176)
    g["win"] = _mm(h, dp, "tn", name="dw_in", tm=512, tn=2176, tk=512)
    gx, _, dg1 = _norm_bwd(x, small["mix_norm_g"], dh, dx1, name="norm1_bwd")

    sm = dict(mix_norm_g=dg1, ffn_norm_g=dg2, final_norm_g=dg3, q_norm_g=dgq, kv_norm_g=dgkv,
              swa_sinks=dsk[:, :SWA_GROUP, 0].reshape(1, SWA_HEADS))
    return tot[0, 0], gx, g, sm


COL_SHARDED = ("w_in", "w_uq", "w_ukv", "w_o_swa", "w_o_mla", "w_gate", "w_up")
ROW_SHARDED = ("w_out", "w_down")
BIG = ("w_in", "w_uq", "w_ukv", "w_o_swa", "w_o_mla", "w_out", "w_gate", "w_up", "w_down")
FULL_SHAPE = dict(w_in=(1024, 3488), w_uq=(384, 768), w_ukv=(256, 1024), w_o_swa=(512, 1024), w_o_mla=(512, 1024),
                  w_out=(1024, 1024), w_gate=(1024, 2816), w_up=(1024, 2816), w_down=(2816, 1024))
PACK_W = 1024
ROW_TILE = 16


def _shard_shape(n):
    r, c = FULL_SHAPE[n]
    return (r, c // N_DEV) if n in COL_SHARDED else (r // N_DEV, c)


def _pack_rows(n):
    r, c = _shard_shape(n)
    rows = r * c // PACK_W
    return rows, -(-rows // ROW_TILE) * ROW_TILE


PACK_OFF = {}
_o = 0
for _n in BIG:
    PACK_OFF[_n] = _o
    _o += _pack_rows(_n)[1]
PACK_ROWS = _o


def _pack(shards, dtype):
    parts = []
    for n in BIG:
        a = shards[n]
        lead = a.shape[:-2]
        rows, prow = _pack_rows(n)
        a = a.astype(dtype).reshape(lead + (rows, PACK_W))
        parts.append(jnp.pad(a, [(0, 0)] * len(lead) + [(0, prow - rows), (0, 0)]))
    return jnp.concatenate(parts, axis=-2)


def _unpack(packed):
    out = {}
    lead = packed.shape[:-2]
    for n in BIG:
        rows, _ = _pack_rows(n)
        a = lax.slice_in_dim(packed, PACK_OFF[n], PACK_OFF[n] + rows, axis=packed.ndim - 2)
        out[n] = a.reshape(lead + _shard_shape(n))
    return out


def _shards_to_full(sh):
    full = {}
    for n in BIG:
        a = sh[n]
        r, c = FULL_SHAPE[n]
        full[n] = jnp.moveaxis(a, 0, 1).reshape(r, c) if n in COL_SHARDED else a.reshape(r, c)
    return full


def _full_to_shards(full):
    sh = {}
    for n in BIG:
        a = full[n]
        r, c = FULL_SHAPE[n]
        if n in COL_SHARDED:
            sh[n] = jnp.moveaxis(a.reshape(r, N_DEV, c // N_DEV), 1, 0)
        else:
            sh[n] = a.reshape(N_DEV, r // N_DEV, c)
    return sh


MESH = pl.DeviceIdType.MESH
ANY = pl.BlockSpec(memory_space=pl.ANY)


def _position():
    return lax.axis_index("x"), lax.axis_index("y"), lax.axis_index("c")


def _all_gather(block, *, name):
    r, c_ = block.shape

    def body(x_ref, out_ref, send_sems, recv_sems, local_sem):
        x, y, c = _position()
        me, sibling = (x, y, c), (x, y, 1 - c)
        chips = [(1 - x, y), (x, 1 - y), (1 - x, 1 - y)]

        def slot(px, py, pc):
            return out_ref.at[4 * px + 2 * py + pc]

        def copy(k, blk, to, src=None):
            return pltpu.make_async_remote_copy(
                src_ref=slot(*blk) if src is None else src, dst_ref=slot(*blk), send_sem=send_sems.at[k],
                recv_sem=recv_sems.at[k], device_id=to, device_id_type=MESH)

        mine = pltpu.make_async_copy(x_ref, slot(*me), local_sem)
        mine.start()
        first = [copy(0, me, sibling, src=x_ref)]
        first += [copy(1 + j, me, (*chip, c), src=x_ref) for j, chip in enumerate(chips)]
        for cp in first:
            cp.start()
        passed = [copy(4 + j, (*chip, c), sibling) for j, chip in enumerate(chips)]
        for j, chip in enumerate(chips):
            copy(1 + j, (*chip, c), me).wait_recv()
            passed[j].start()
        copy(0, sibling, me).wait_recv()
        for j, chip in enumerate(chips):
            copy(4 + j, (*chip, 1 - c), me).wait_recv()
        for cp in first + passed:
            cp.wait_send()
        mine.wait()

    return pl.pallas_call(
        body, name=name, out_shape=jax.ShapeDtypeStruct((N_DEV, r, c_), block.dtype), in_specs=[ANY], out_specs=ANY,
        scratch_shapes=[pltpu.SemaphoreType.DMA((7,)), pltpu.SemaphoreType.DMA((7,)), pltpu.SemaphoreType.DMA],
    )(block)


def _exchange_sibling(g, *, name):
    _, r, c_ = g.shape

    def body(g_ref, land_ref, send_sems, recv_sems):
        x, y, c = _position()
        copies = [pltpu.make_async_remote_copy(
            src_ref=g_ref.at[2 * j + (1 - c)], dst_ref=land_ref.at[j], send_sem=send_sems.at[j], recv_sem=recv_sems.at[j],
            device_id=(x, y, 1 - c), device_id_type=MESH) for j in range(4)]
        for cp in copies:
            cp.start()
        for cp in copies:
            cp.wait_recv()
        for cp in copies:
            cp.wait_send()

    return pl.pallas_call(
        body, name=name, out_shape=jax.ShapeDtypeStruct((4, r, c_), g.dtype), in_specs=[ANY], out_specs=ANY,
        scratch_shapes=[pltpu.SemaphoreType.DMA((4,)), pltpu.SemaphoreType.DMA((4,))],
    )(g)


def _exchange_chips(p, *, name):
    _, r, c_ = p.shape

    def body(p_ref, land_ref, send_sems, recv_sems, local_sem):
        x, y, c = _position()
        mine = 2 * x + y
        own = pltpu.make_async_copy(p_ref.at[mine], land_ref.at[mine], local_sem)
        own.start()
        copies = []
        for k, (px, py) in enumerate([(1 - x, y), (x, 1 - y), (1 - x, 1 - y)]):
            copies.append(pltpu.make_async_remote_copy(
                src_ref=p_ref.at[2 * px + py], dst_ref=land_ref.at[mine], send_sem=send_sems.at[k], recv_sem=recv_sems.at[k],
                device_id=(px, py, c), device_id_type=MESH))
        for cp in copies:
            cp.start()
        for cp in copies:
            cp.wait_recv()
        for cp in copies:
            cp.wait_send()
        own.wait()

    return pl.pallas_call(
        body, name=name, out_shape=jax.ShapeDtypeStruct((4, r, c_), p.dtype), in_specs=[ANY], out_specs=ANY,
        scratch_shapes=[pltpu.SemaphoreType.DMA((3,)), pltpu.SemaphoreType.DMA((3,)), pltpu.SemaphoreType.DMA],
    )(p)


def _pair_add(g, land, c_idx, *, tr=368):
    _, r, c_ = g.shape

    def body(c_ref, g_ref, l_ref, o_ref):
        o_ref[...] = (g_ref[...].astype(F32) + l_ref[...].astype(F32)).astype(o_ref.dtype)

    return pl.pallas_call(
        body, name="rs_pair_add",
        grid_spec=pltpu.PrefetchScalarGridSpec(
            num_scalar_prefetch=1, grid=(4, r // tr),
            in_specs=[pl.BlockSpec((1, tr, c_), lambda j, i, cr: (2 * j + cr[0], i, 0)),
                      pl.BlockSpec((1, tr, c_), lambda j, i, cr: (j, i, 0))],
            out_specs=pl.BlockSpec((1, tr, c_), lambda j, i, cr: (j, i, 0))),
        out_shape=jax.ShapeDtypeStruct((4, r, c_), g.dtype),
        compiler_params=_cparams(("parallel", "parallel")),
    )(c_idx, g, land)


def _adamw(w, g, m, v):
    m = ADAM_B1 * m + (1.0 - ADAM_B1) * g
    v = ADAM_B2 * v + (1.0 - ADAM_B2) * (g * g)
    m_hat = m / (1.0 - ADAM_B1 ** ADAM_STEP)
    v_hat = v / (1.0 - ADAM_B2 ** ADAM_STEP)
    delta = -ADAM_LR * (m_hat / (jnp.sqrt(v_hat) + ADAM_EPS) + ADAM_WD * w)
    return delta, m, v


def _sum_adamw(land, w, m, v, *, tr=368):
    _, r, c_ = land.shape

    def body(l_ref, w_ref, m_ref, v_ref, g_ref, d_ref, mo_ref, vo_ref):
        g = l_ref[0].astype(F32)
        for j in range(1, 4):
            g = g + l_ref[j].astype(F32)
        d, mn, vn = _adamw(w_ref[...], g, m_ref[...], v_ref[...])
        g_ref[...] = g
        d_ref[...] = d
        mo_ref[...] = mn
        vo_ref[...] = vn

    row = pl.BlockSpec((tr, c_), lambda i: (i, 0))
    shp = jax.ShapeDtypeStruct((r, c_), F32)
    return pl.pallas_call(
        body, name="rs_sum_adamw", grid=(r // tr,),
        in_specs=[pl.BlockSpec((4, tr, c_), lambda i: (0, i, 0)), row, row, row], out_specs=[row] * 4,
        out_shape=[shp] * 4, compiler_params=_cparams(("parallel",)),
    )(land, w, m, v)


SMALL = ("mix_norm_g", "ffn_norm_g", "final_norm_g", "q_norm_g", "kv_norm_g", "swa_sinks")
SMALL_W = dict(mix_norm_g=1024, ffn_norm_g=1024, final_norm_g=1024, q_norm_g=Q_LORA, kv_norm_g=KV_LORA, swa_sinks=SWA_HEADS)


def _small_adamw(parts, w, m, v):
    n_par = len(SMALL)

    def body(p_ref, w_ref, m_ref, v_ref, g_ref, d_ref, mo_ref, vo_ref):
        tot = p_ref[0]
        for dev in range(1, N_DEV):
            tot = tot + p_ref[dev]
        row_id = lax.broadcasted_iota(jnp.int32, (SUBLANES, PACK_W), 0)
        g = jnp.zeros((SUBLANES, PACK_W), F32)
        for k in range(n_par):
            g = jnp.where(row_id == k, jnp.sum(tot[k * SUBLANES:(k + 1) * SUBLANES, :], axis=0, keepdims=True), g)
        d, mn, vn = _adamw(w_ref[...], g, m_ref[...], v_ref[...])
        g_ref[...] = g
        d_ref[...] = d
        mo_ref[...] = mn
        vo_ref[...] = vn

    shp = jax.ShapeDtypeStruct((SUBLANES, PACK_W), F32)
    vm = pl.BlockSpec(memory_space=pltpu.VMEM)
    return pl.pallas_call(body, name="small_adamw", in_specs=[vm] * 4, out_specs=[vm] * 4, out_shape=[shp] * 4)(parts, w, m, v)


def _small_pack(d, rows_each):
    parts = [jnp.pad(d[n].astype(F32), ((0, 0), (0, PACK_W - SMALL_W[n]))) for n in SMALL]
    out = jnp.concatenate(parts, 0)
    pad = -out.shape[0] % SUBLANES
    return jnp.pad(out, ((0, pad), (0, 0)))


def kernel(x, mix_norm_g, w_in, swa_sinks, q_norm_g, w_uq, kv_norm_g, w_ukv, w_o_swa, w_o_mla, w_out, ffn_norm_g, w_gate, w_up, w_down, final_norm_g, loss_target, m_mix_norm_g, m_w_in, m_swa_sinks, m_q_norm_g, m_w_uq, m_kv_norm_g, m_w_ukv, m_w_o_swa, m_w_o_mla, m_w_out, m_ffn_norm_g, m_w_gate, m_w_up, m_w_down, m_final_norm_g, v_mix_norm_g, v_w_in, v_swa_sinks, v_q_norm_g, v_w_uq, v_kv_norm_g, v_w_ukv, v_w_o_swa, v_w_o_mla, v_w_out, v_ffn_norm_g, v_w_gate, v_w_up, v_w_down, v_final_norm_g):
    big_w = dict(w_in=w_in[0], w_uq=w_uq[0], w_ukv=w_ukv[0], w_o_swa=w_o_swa[0], w_o_mla=w_o_mla[0], w_out=w_out[0],
                 w_gate=w_gate[0], w_up=w_up[0], w_down=w_down[0])
    big_m = dict(w_in=m_w_in[0], w_uq=m_w_uq[0], w_ukv=m_w_ukv[0], w_o_swa=m_w_o_swa[0], w_o_mla=m_w_o_mla[0],
                 w_out=m_w_out[0], w_gate=m_w_gate[0], w_up=m_w_up[0], w_down=m_w_down[0])
    big_v = dict(w_in=v_w_in[0], w_uq=v_w_uq[0], w_ukv=v_w_ukv[0], w_o_swa=v_w_o_swa[0], w_o_mla=v_w_o_mla[0],
                 w_out=v_w_out[0], w_gate=v_w_gate[0], w_up=v_w_up[0], w_down=v_w_down[0])
    small_w = dict(mix_norm_g=mix_norm_g, ffn_norm_g=ffn_norm_g, final_norm_g=final_norm_g.reshape(1, D_MODEL),
                   q_norm_g=q_norm_g, kv_norm_g=kv_norm_g, swa_sinks=swa_sinks)
    small_m = dict(mix_norm_g=m_mix_norm_g, ffn_norm_g=m_ffn_norm_g, final_norm_g=m_final_norm_g.reshape(1, D_MODEL),
                   q_norm_g=m_q_norm_g, kv_norm_g=m_kv_norm_g, swa_sinks=m_swa_sinks)
    small_v = dict(mix_norm_g=v_mix_norm_g, ffn_norm_g=v_ffn_norm_g, final_norm_g=v_final_norm_g.reshape(1, D_MODEL),
                   q_norm_g=v_q_norm_g, kv_norm_g=v_kv_norm_g, swa_sinks=v_swa_sinks)

    gathered = _all_gather(_pack(big_w, WIRE_DTYPE), name="ag_weights")
    ops = _to_operands(_shards_to_full(_unpack(gathered)))

    loss_tot, gx, g_ops, g_small = _local_step(x[0], loss_target[0], ops, small_w)

    g_full = _from_operand_grads(g_ops)
    g_pack = _pack(_full_to_shards(g_full), WIRE_DTYPE)
    c_idx = lax.axis_index("c").astype(jnp.int32).reshape(1)
    pair = _pair_add(g_pack, _exchange_sibling(g_pack, name="rs_sibling"), c_idx)
    land = _exchange_chips(pair, name="rs_chips")
    gw, dw, mw, vw = _sum_adamw(land, _pack(big_w, F32), _pack(big_m, F32), _pack(big_v, F32))
    gw, dw, mw, vw = _unpack(gw), _unpack(dw), _unpack(mw), _unpack(vw)

    parts = _all_gather(_small_pack(g_small_rows(g_small), SUBLANES), name="ag_small")
    gs, ds, ms, vs = _small_adamw(parts, _small_pack(small_w, 1), _small_pack(small_m, 1), _small_pack(small_v, 1))

    def small_out(packed):
        out = {}
        for k, n in enumerate(SMALL):
            out[n] = packed[k:k + 1, :SMALL_W[n]]
        out["final_norm_g"] = out["final_norm_g"].reshape(D_MODEL)
        return out

    gs, ds, ms, vs = small_out(gs), small_out(ds), small_out(ms), small_out(vs)
    loss = lax.psum(loss_tot, AXES)

    order = ("mix_norm_g", "w_in", "swa_sinks", "q_norm_g", "w_uq", "kv_norm_g", "w_ukv", "w_o_swa", "w_o_mla", "w_out",
             "ffn_norm_g", "w_gate", "w_up", "w_down", "final_norm_g")

    def leaves(big, small):
        return [big[n][None] if n in big else small[n] for n in order]

    return (loss, gx[None], *leaves(gw, gs), *leaves(dw, ds), *leaves(mw, ms), *leaves(vw, vs))


def g_small_rows(g_small):
    out = dict(g_small)
    out["swa_sinks"] = jnp.pad(g_small["swa_sinks"], ((0, SUBLANES - 1), (0, 0)))
    return out
```

```python
import functools

import numpy as np
import jax
import jax.numpy as jnp
from jax import lax
from jax.experimental import pallas as pl
from jax.experimental.pallas import tpu as pltpu

F32 = jnp.float32
MXU_DTYPE = jnp.bfloat16
WIRE_DTYPE = jnp.bfloat16

D_MODEL = 1024
EPS = 1e-6
ROPE_THETA = 10000.0
BLOCK = 128
HEAD_DIM = 64
SWA_HEADS = 8
SWA_KV_HEADS = 2
SWA_GROUP = SWA_HEADS // SWA_KV_HEADS
MLA_HEADS = 8
MLA_NOPE = 64
MLA_ROPE = 32
MLA_V = 64
MLA_QK = MLA_NOPE + MLA_ROPE
Q_LORA = 384
KV_LORA = 256
D_FF = 2816
IN_SIZES = (512, 128, 128, Q_LORA, KV_LORA, MLA_ROPE, D_MODEL, D_MODEL)
IN_OFF = tuple(int(v) for v in np.cumsum((0,) + IN_SIZES))
ADAM_LR, ADAM_B1, ADAM_B2, ADAM_EPS, ADAM_WD, ADAM_STEP = 0.001, 0.9, 0.999, 1e-08, 0.01, 10

LANES = 128
SUBLANES = 8
VMEM_LIMIT = 48 * 1024 * 1024
N_DEV = 8
AXES = ("x", "y", "c")

P_GA, P_GB, P_Q, P_QLAT, P_KR, P_K, P_V, P_KVLAT, P_W = 0, 1024, 2048, 3072, 3456, 3584, 3840, 4096, 4352
KR_LANE = 64

LOG2E = 1.4426950408889634

NT = (((1,), (1,)), ((), ()))
NN = (((1,), (0,)), ((), ()))
TN = (((0,), (0,)), ((), ()))


def _cparams(sem):
    return pltpu.CompilerParams(dimension_semantics=sem, vmem_limit_bytes=VMEM_LIMIT)


def _mm(a, b, mode, *, name, out_dtype=F32, add=None, tm=512, tn=512, tk=None):
    if mode == "nn":
        (M, K), (K2, N) = a.shape, b.shape
    elif mode == "nt":
        (M, K), (N, K2) = a.shape, b.shape
    else:
        (K, M), (K2, N) = a.shape, b.shape
    assert K == K2, (a.shape, b.shape, mode)
    tk = K if tk is None else tk
    tm, tn = min(tm, M), min(tn, N)
    assert M % tm == 0 and N % tn == 0 and K % tk == 0, (M, N, K, tm, tn, tk)
    nk = K // tk
    dn = {"nn": NN, "nt": NT, "tn": TN}[mode]
    if mode == "tn":
        a_spec = pl.BlockSpec((tk, tm), lambda i, j, k: (k, i))
    else:
        a_spec = pl.BlockSpec((tm, tk), lambda i, j, k: (i, k))
    if mode == "nt":
        b_spec = pl.BlockSpec((tn, tk), lambda i, j, k: (j, k))
    else:
        b_spec = pl.BlockSpec((tk, tn), lambda i, j, k: (k, j))
    o_spec = pl.BlockSpec((tm, tn), lambda i, j, k: (i, j))
    has_add = add is not None

    def body(*refs):
        a_ref, b_ref = refs[0], refs[1]
        add_ref = refs[2] if has_add else None
        o_ref = refs[3] if has_add else refs[2]
        p = lax.dot_general(a_ref[...], b_ref[...], dn, preferred_element_type=F32)

        def finish(acc):
            if has_add:
                acc = acc + add_ref[...]
            o_ref[...] = acc.astype(o_ref.dtype)

        if nk == 1:
            finish(p)
        else:
            acc_ref = refs[-1]
            k = pl.program_id(2)

            @pl.when(k == 0)
            def _():
                acc_ref[...] = p

            @pl.when(k > 0)
            def _():
                acc_ref[...] += p

            @pl.when(k == nk - 1)
            def _():
                finish(acc_ref[...])

    ins = [a, b] + ([add] if has_add else [])
    in_specs = [a_spec, b_spec] + ([o_spec] if has_add else [])
    return pl.pallas_call(
        body, name=name, grid=(M // tm, N // tn, nk), in_specs=in_specs, out_specs=o_spec,
        out_shape=jax.ShapeDtypeStruct((M, N), out_dtype),
        scratch_shapes=[pltpu.VMEM((tm, tn), F32)] if nk > 1 else [],
        compiler_params=_cparams(("parallel", "parallel", "arbitrary")),
    )(*ins)


def _rows(ts, w, cb=0):
    return pl.BlockSpec((ts, w), lambda i: (i, cb))


def _const(r, w):
    return pl.BlockSpec((r, w), lambda i: (0, 0))


def _sublane_sum(v):
    ts, c = v.shape
    return jnp.sum(v.reshape(ts // SUBLANES, SUBLANES, c), axis=0)


def _sigmoid(v):
    return 1.0 / (1.0 + jnp.exp(-v))


def _rope(v, cos, s_up, s_dn, up, dn):
    return v * cos + pltpu.roll(v, up, 1) * s_up + pltpu.roll(v, dn, 1) * s_dn


def _rope_t(dv, cos, s_up, s_dn, up, dn):
    return dv * cos + pltpu.roll(dv * s_up, dn, 1) + pltpu.roll(dv * s_dn, up, 1)


def _rope_tables(seq):
    pos = jnp.arange(seq, dtype=F32)[:, None]

    def base(dim):
        inv = ROPE_THETA ** (-jnp.arange(0, dim, 2, dtype=F32) / dim)
        ang = pos * inv[None, :]
        return jnp.cos(ang), jnp.sin(ang)

    z = lambda n: jnp.zeros((seq, n), F32)
    ca, sa = base(HEAD_DIM)
    a_cos = jnp.concatenate([ca, ca, z(64)], 1)
    a_up = jnp.concatenate([-sa, z(96)], 1)
    a_dn = jnp.concatenate([z(32), sa, z(64)], 1)
    cb, sb = base(MLA_ROPE)
    one = jnp.ones((seq, 64), F32)
    q_cos = jnp.concatenate([one, cb, cb, z(32)], 1)
    k_cos = jnp.concatenate([z(64), cb, cb, z(32)], 1)
    b_up = jnp.concatenate([z(64), -sb, z(48)], 1)
    b_dn = jnp.concatenate([z(80), sb, z(32)], 1)
    return a_cos, a_up, a_dn, q_cos, k_cos, b_up, b_dn


def _norm_fwd(x, g, *, name, ts=256):
    s_, c = x.shape

    def body(x_ref, g_ref, h_ref):
        v = x_ref[...]
        r = lax.rsqrt(jnp.mean(v * v, axis=-1, keepdims=True) + EPS)
        h_ref[...] = (v * r * g_ref[...]).astype(h_ref.dtype)

    return pl.pallas_call(
        body, name=name, grid=(s_ // ts,), in_specs=[_rows(ts, c), _const(1, c)], out_specs=_rows(ts, c),
        out_shape=jax.ShapeDtypeStruct((s_, c), MXU_DTYPE), compiler_params=_cparams(("parallel",)),
    )(x, g)


def _norm_bwd(x, g, dy, res, *, name, ts=256, x_cb=0, x_src_w=None):
    s_ = x.shape[0]
    c = dy.shape[1]
    has_res = res is not None

    def body(*refs):
        x_ref, g_ref, dy_ref = refs[0], refs[1], refs[2]
        res_ref = refs[3] if has_res else None
        dx_ref, dxb_ref, dg_ref = refs[-3], refs[-2], refs[-1]
        v = x_ref[...]
        r = lax.rsqrt(jnp.mean(v * v, axis=-1, keepdims=True) + EPS)
        xh = v * r
        d = dy_ref[...]
        dxh = d * g_ref[...]
        dx = r * (dxh - xh * jnp.mean(dxh * xh, axis=-1, keepdims=True))
        if has_res:
            dx = dx + res_ref[...]
        dx_ref[...] = dx
        dxb_ref[...] = dx.astype(dxb_ref.dtype)

        @pl.when(pl.program_id(0) == 0)
        def _():
            dg_ref[...] = jnp.zeros(dg_ref.shape, F32)

        dg_ref[...] += _sublane_sum(d * xh)

    ins = [x, g, dy] + ([res] if has_res else [])
    in_specs = [_rows(ts, c, x_cb), _const(1, c), _rows(ts, c)] + ([_rows(ts, c)] if has_res else [])
    return pl.pallas_call(
        body, name=name, grid=(s_ // ts,), in_specs=in_specs,
        out_specs=[_rows(ts, c), _rows(ts, c), _const(SUBLANES, c)],
        out_shape=[jax.ShapeDtypeStruct((s_, c), F32), jax.ShapeDtypeStruct((s_, c), MXU_DTYPE),
                   jax.ShapeDtypeStruct((SUBLANES, c), F32)],
        compiler_params=_cparams(("arbitrary",)),
    )(*ins)


def _attn_prep(p, gq, gkv, tabs, *, ts=256):
    s_ = p.shape[0]
    a_cos, a_up, a_dn, _, k_cos, b_up, b_dn = tabs

    def body(q_ref, k_ref, v_ref, ql_ref, kvl_ref, kr_ref, gq_ref, gkv_ref, ac, au, ad, kc, bu, bd,
             qa_ref, ka_ref, va_ref, cq_ref, ckv_ref, kro_ref):
        c_, u_, d_ = ac[...], au[...], ad[...]
        for h in range(SWA_HEADS):
            sl = slice(h * LANES, (h + 1) * LANES)
            qa_ref[:, sl] = _rope(q_ref[:, sl], c_, u_, d_, 96, 32).astype(qa_ref.dtype)
        for h in range(SWA_KV_HEADS):
            sl = slice(h * LANES, (h + 1) * LANES)
            ka_ref[:, sl] = _rope(k_ref[:, sl], c_, u_, d_, 96, 32).astype(ka_ref.dtype)
        va_ref[...] = v_ref[...].astype(va_ref.dtype)
        for src, gref, dst in ((ql_ref, gq_ref, cq_ref), (kvl_ref, gkv_ref, ckv_ref)):
            v = src[...]
            r = lax.rsqrt(jnp.mean(v * v, axis=-1, keepdims=True) + EPS)
            dst[...] = (v * r * gref[...]).astype(dst.dtype)
        kro_ref[...] = _rope(kr_ref[...], kc[...], bu[...], bd[...], 112, 16)

    tab = _rows(ts, LANES)
    return pl.pallas_call(
        body, name="attn_prep", grid=(s_ // ts,),
        in_specs=[_rows(ts, 1024, P_Q // 1024), _rows(ts, 256, P_K // 256), _rows(ts, 256, P_V // 256),
                  _rows(ts, Q_LORA, P_QLAT // Q_LORA), _rows(ts, KV_LORA, P_KVLAT // KV_LORA),
                  _rows(ts, LANES, P_KR // LANES), _const(1, Q_LORA), _const(1, KV_LORA), tab, tab, tab, tab, tab, tab],
        out_specs=[_rows(ts, 1024), _rows(ts, 256), _rows(ts, 256), _rows(ts, Q_LORA), _rows(ts, KV_LORA),
                   _rows(ts, LANES)],
        out_shape=[jax.ShapeDtypeStruct((s_, 1024), MXU_DTYPE), jax.ShapeDtypeStruct((s_, 256), MXU_DTYPE),
                   jax.ShapeDtypeStruct((s_, 256), MXU_DTYPE), jax.ShapeDtypeStruct((s_, Q_LORA), MXU_DTYPE),
                   jax.ShapeDtypeStruct((s_, KV_LORA), MXU_DTYPE), jax.ShapeDtypeStruct((s_, LANES), F32)],
        compiler_params=_cparams(("parallel",)),
    )(p, p, p, p, p, p, gq, gkv, a_cos, a_up, a_dn, k_cos, b_up, b_dn)


def _mla_prep(qp, kp, kro, tabs, *, ts=256):
    s_ = qp.shape[0]
    _, _, _, q_cos, _, b_up, b_dn = tabs

    def body(q_ref, k_ref, kr_ref, qc, bu, bd, qo_ref, ko_ref):
        c_, u_, d_ = qc[...], bu[...], bd[...]
        kr = kr_ref[...]
        for h in range(MLA_HEADS):
            sl = slice(h * LANES, (h + 1) * LANES)
            qo_ref[:, sl] = _rope(q_ref[:, sl], c_, u_, d_, 112, 16).astype(qo_ref.dtype)
            ko_ref[:, sl] = (k_ref[:, sl] + kr).astype(ko_ref.dtype)

    tab = _rows(ts, LANES)
    return pl.pallas_call(
        body, name="mla_prep", grid=(s_ // ts,),
        in_specs=[_rows(ts, 1024), _rows(ts, 1024), tab, tab, tab, tab],
        out_specs=[_rows(ts, 1024), _rows(ts, 1024)],
        out_shape=[jax.ShapeDtypeStruct((s_, 1024), MXU_DTYPE)] * 2,
        compiler_params=_cparams(("parallel",)),
    )(qp, kp, kro, q_cos, b_up, b_dn)


def _mla_unprep(dqc, dkc, dvp, tabs, *, ts=256):
    s_ = dqc.shape[0]
    _, _, _, q_cos, k_cos, b_up, b_dn = tabs

    def body(dq_ref, dk_ref, dv_ref, qc, kc, bu, bd, dqo_ref, dkvo_ref, dkr_ref):
        c_, u_, d_ = qc[...], bu[...], bd[...]
        tot = jnp.zeros((ts, LANES), F32)
        for h in range(MLA_HEADS):
            sl = slice(h * LANES, (h + 1) * LANES)
            dqo_ref[:, sl] = _rope_t(dq_ref[:, sl], c_, u_, d_, 112, 16).astype(dqo_ref.dtype)
            dk = dk_ref[:, sl]
            dkvo_ref[:, sl] = dk.astype(dkvo_ref.dtype)
            tot = tot + dk
        dkvo_ref[:, 1024:2048] = dv_ref[...].astype(dkvo_ref.dtype)
        dkr_ref[...] = _rope_t(tot, kc[...], u_, d_, 112, 16).astype(dkr_ref.dtype)

    tab = _rows(ts, LANES)
    return pl.pallas_call(
        body, name="mla_unprep", grid=(s_ // ts,),
        in_specs=[_rows(ts, 1024), _rows(ts, 1024), _rows(ts, 1024), tab, tab, tab, tab],
        out_specs=[_rows(ts, 1024), _rows(ts, 2048), _rows(ts, LANES)],
        out_shape=[jax.ShapeDtypeStruct((s_, 1024), MXU_DTYPE), jax.ShapeDtypeStruct((s_, 2048), MXU_DTYPE),
                   jax.ShapeDtypeStruct((s_, LANES), MXU_DTYPE)],
        compiler_params=_cparams(("parallel",)),
    )(dqc, dkc, dvp, q_cos, k_cos, b_up, b_dn)


def _swa_unrope(dqa, dka, tabs, *, ts=256):
    s_ = dqa.shape[0]
    a_cos, a_up, a_dn = tabs[0], tabs[1], tabs[2]

    def body(dq_ref, dk_ref, ac, au, ad, dqo_ref, dko_ref):
        c_, u_, d_ = ac[...], au[...], ad[...]
        for h in range(SWA_HEADS):
            sl = slice(h * LANES, (h + 1) * LANES)
            dqo_ref[:, sl] = _rope_t(dq_ref[:, sl], c_, u_, d_, 96, 32).astype(dqo_ref.dtype)
        for h in range(SWA_KV_HEADS):
            sl = slice(h * LANES, (h + 1) * LANES)
            dko_ref[:, sl] = _rope_t(dk_ref[:, sl], c_, u_, d_, 96, 32).astype(dko_ref.dtype)

    tab = _rows(ts, LANES)
    return pl.pallas_call(
        body, name="swa_unrope", grid=(s_ // ts,),
        in_specs=[_rows(ts, 1024), _rows(ts, 256), tab, tab, tab],
        out_specs=[_rows(ts, 1024), _rows(ts, 256)],
        out_shape=[jax.ShapeDtypeStruct((s_, 1024), MXU_DTYPE), jax.ShapeDtypeStruct((s_, 256), MXU_DTYPE)],
        compiler_params=_cparams(("parallel",)),
    )(dqa, dka, a_cos, a_up, a_dn)


def _gate_fwd(p, ta, tb, *, ts=256):
    s_ = p.shape[0]

    def body(ga_ref, gb_ref, ta_ref, tb_ref, y_ref):
        y = _sigmoid(ga_ref[...]) * ta_ref[...] + _sigmoid(gb_ref[...]) * tb_ref[...]
        y_ref[...] = y.astype(y_ref.dtype)

    return pl.pallas_call(
        body, name="gate_fwd", grid=(s_ // ts,),
        in_specs=[_rows(ts, 1024, P_GA // 1024), _rows(ts, 1024, P_GB // 1024), _rows(ts, 1024), _rows(ts, 1024)],
        out_specs=_rows(ts, 1024), out_shape=jax.ShapeDtypeStruct((s_, 1024), MXU_DTYPE),
        compiler_params=_cparams(("parallel",)),
    )(p, p, ta, tb)


def _gate_bwd(p, ta, tb, dy, *, ts=256):
    s_ = p.shape[0]

    def body(ga_ref, gb_ref, ta_ref, tb_ref, dy_ref, dta_ref, dtb_ref, dg_ref):
        d = dy_ref[...]
        sa, sb = _sigmoid(ga_ref[...]), _sigmoid(gb_ref[...])
        dta_ref[...] = (d * sa).astype(dta_ref.dtype)
        dtb_ref[...] = (d * sb).astype(dtb_ref.dtype)
        dg_ref[:, 0:1024] = (d * ta_ref[...] * (sa * (1.0 - sa))).astype(dg_ref.dtype)
        dg_ref[:, 1024:2048] = (d * tb_ref[...] * (sb * (1.0 - sb))).astype(dg_ref.dtype)

    return pl.pallas_call(
        body, name="gate_bwd", grid=(s_ // ts,),
        in_specs=[_rows(ts, 1024, P_GA // 1024), _rows(ts, 1024, P_GB // 1024), _rows(ts, 1024), _rows(ts, 1024),
                  _rows(ts, 1024)],
        out_specs=[_rows(ts, 1024), _rows(ts, 1024), _rows(ts, 2048)],
        out_shape=[jax.ShapeDtypeStruct((s_, 1024), MXU_DTYPE)] * 2 + [jax.ShapeDtypeStruct((s_, 2048), MXU_DTYPE)],
        compiler_params=_cparams(("parallel",)),
    )(p, p, ta, tb, dy)


def _swiglu_fwd(gu, *, ts=256):
    s_ = gu.shape[0]

    def body(g_ref, u_ref, a_ref):
        g = g_ref[...]
        a_ref[...] = (g * _sigmoid(g) * u_ref[...]).astype(a_ref.dtype)

    return pl.pallas_call(
        body, name="swiglu_fwd", grid=(s_ // ts,), in_specs=[_rows(ts, D_FF, 0), _rows(ts, D_FF, 1)],
        out_specs=_rows(ts, D_FF), out_shape=jax.ShapeDtypeStruct((s_, D_FF), MXU_DTYPE),
        compiler_params=_cparams(("parallel",)),
    )(gu, gu)


def _swiglu_bwd(gu, da, *, ts=256):
    s_ = gu.shape[0]

    def body(g_ref, u_ref, da_ref, o_ref):
        g, u, d = g_ref[...], u_ref[...], da_ref[...]
        sg = _sigmoid(g)
        o_ref[:, 0:D_FF] = (d * u * (sg * (1.0 + g * (1.0 - sg)))).astype(o_ref.dtype)
        o_ref[:, D_FF:2 * D_FF] = (d * (g * sg)).astype(o_ref.dtype)

    return pl.pallas_call(
        body, name="swiglu_bwd", grid=(s_ // ts,), in_specs=[_rows(ts, D_FF, 0), _rows(ts, D_FF, 1), _rows(ts, D_FF)],
        out_specs=_rows(ts, 2 * D_FF), out_shape=jax.ShapeDtypeStruct((s_, 2 * D_FF), MXU_DTYPE),
        compiler_params=_cparams(("parallel",)),
    )(gu, gu, da)


def _loss_bwd(x2, g, tgt, *, ts=256):
    s_, c = x2.shape

    def body(x_ref, g_ref, t_ref, dx_ref, dxb_ref, dg_ref, lp_ref, tot_ref):
        v = x_ref[...]
        r = lax.rsqrt(jnp.mean(v * v, axis=-1, keepdims=True) + EPS)
        xh = v * r
        gg = g_ref[...]
        e = xh * gg - t_ref[...]
        do = e * (1.0 / c)
        dxh = do * gg
        dx = r * (dxh - xh * jnp.mean(dxh * xh, axis=-1, keepdims=True))
        dx_ref[...] = dx
        dxb_ref[...] = dx.astype(dxb_ref.dtype)
        i = pl.program_id(0)

        @pl.when(i == 0)
        def _():
            dg_ref[...] = jnp.zeros(dg_ref.shape, F32)
            lp_ref[...] = jnp.zeros(lp_ref.shape, F32)

        dg_ref[...] += _sublane_sum(do * xh)
        lp_ref[...] += _sublane_sum(e * e)
        tot_ref[...] = jnp.full(tot_ref.shape, (0.5 / c) * jnp.sum(lp_ref[...]), F32)

    return pl.pallas_call(
        body, name="loss_bwd", grid=(s_ // ts,), in_specs=[_rows(ts, c), _const(1, c), _rows(ts, c)],
        out_specs=[_rows(ts, c), _rows(ts, c), _const(SUBLANES, c), _const(SUBLANES, c), _const(SUBLANES, LANES)],
        out_shape=[jax.ShapeDtypeStruct((s_, c), F32), jax.ShapeDtypeStruct((s_, c), MXU_DTYPE),
                   jax.ShapeDtypeStruct((SUBLANES, c), F32), jax.ShapeDtypeStruct((SUBLANES, c), F32),
                   jax.ShapeDtypeStruct((SUBLANES, LANES), F32)],
        compiler_params=_cparams(("arbitrary",)),
    )(x2, g, tgt)


def _mla_bwd_prep(dob, o32, *, ts=256):
    s_ = dob.shape[0]

    def body(do_ref, o_ref, dob_ref, dl_ref):
        d = do_ref[...]
        dob_ref[...] = d.astype(dob_ref.dtype)
        prod = d * o_ref[...]
        for h in range(MLA_HEADS):
            dl_ref[h] = jnp.sum(prod[:, h * LANES:(h + 1) * LANES].T, axis=0, keepdims=True)

    return pl.pallas_call(
        body, name="mla_bwd_prep", grid=(s_ // ts,), in_specs=[_rows(ts, 1024), _rows(ts, 1024)],
        out_specs=[_rows(ts, 1024), pl.BlockSpec((MLA_HEADS, 1, ts), lambda i: (0, 0, i))],
        out_shape=[jax.ShapeDtypeStruct((s_, 1024), MXU_DTYPE), jax.ShapeDtypeStruct((MLA_HEADS, 1, s_), F32)],
        compiler_params=_cparams(("parallel",)),
    )(dob, o32)


SWA_T = 4 * BLOCK


def _swa_masks(sb):
    kr = lax.broadcasted_iota(jnp.int32, (2 * BLOCK, BLOCK), 0)
    qc = lax.broadcasted_iota(jnp.int32, (2 * BLOCK, BLOCK), 1)
    band = jnp.logical_and(kr > qc, kr <= qc + BLOCK)
    first = jnp.logical_and(band, kr >= BLOCK)
    return band, jnp.logical_or(first, jnp.logical_and(band, sb > 0))


def _swa_in_specs(rev, nsb):
    sbi = (lambda j: nsb - 1 - j) if rev else (lambda j: j)
    cur = pl.BlockSpec((SWA_T, LANES), lambda g, j: (sbi(j), g))
    prev = pl.BlockSpec((BLOCK, LANES), lambda g, j: (jnp.maximum(4 * sbi(j) - 1, 0), g))
    q = pl.BlockSpec((SWA_T, SWA_GROUP * LANES), lambda g, j: (sbi(j), g))
    sink = pl.BlockSpec((1, SUBLANES, LANES), lambda g, j: (g, 0, 0))
    lse = pl.BlockSpec((SWA_GROUP, 1, SWA_T), lambda g, j: (g, 0, sbi(j)))
    return q, cur, prev, sink, lse


def _swa_fwd(qa, ka, va, sink_b):
    s_ = qa.shape[0]
    nsb = s_ // SWA_T
    c2 = HEAD_DIM ** -0.5 * LOG2E

    def body(q_ref, kc_ref, kp_ref, vc_ref, vp_ref, sk_ref, o32_ref, o16_ref, lse_ref, kx, vx):
        kx[0:BLOCK, :] = kp_ref[...]
        kx[BLOCK:5 * BLOCK, :] = kc_ref[...]
        vx[0:BLOCK, :] = vp_ref[...]
        vx[BLOCK:5 * BLOCK, :] = vc_ref[...]
        band, band0 = _swa_masks(pl.program_id(1))
        for hh in range(SWA_GROUP):
            sink2 = sk_ref[0, hh:hh + 1, 0:1] * LOG2E
            cs = slice(hh * LANES, (hh + 1) * LANES)
            for b in range(4):
                rs = slice(b * BLOCK, (b + 1) * BLOCK)
                ks = slice(b * BLOCK, (b + 2) * BLOCK)
                st = lax.dot_general(kx[ks, :], q_ref[rs, cs], NT, preferred_element_type=F32) * c2
                st = jnp.where(band0 if b == 0 else band, st, -jnp.inf)
                m = jnp.maximum(jnp.max(st, axis=0, keepdims=True), sink2)
                pt = jnp.exp2(st - m)
                den = jnp.sum(pt, axis=0, keepdims=True) + jnp.exp2(sink2 - m)
                o = lax.dot_general((pt * (1.0 / den)).astype(MXU_DTYPE), vx[ks, :], TN, preferred_element_type=F32)
                o32_ref[rs, cs] = o
                o16_ref[rs, cs] = o.astype(o16_ref.dtype)
                lse_ref[hh, :, rs] = m + jnp.log2(den)

    q, cur, prev, sink, lse_spec = _swa_in_specs(False, nsb)
    return pl.pallas_call(
        body, name="swa_fwd", grid=(SWA_KV_HEADS, nsb), in_specs=[q, cur, prev, cur, prev, sink],
        out_specs=[q, q, lse_spec],
        out_shape=[jax.ShapeDtypeStruct((s_, SWA_HEADS * LANES), F32), jax.ShapeDtypeStruct((s_, SWA_HEADS * LANES), MXU_DTYPE),
                   jax.ShapeDtypeStruct((SWA_HEADS, 1, s_), F32)],
        scratch_shapes=[pltpu.VMEM((5 * BLOCK, LANES), MXU_DTYPE), pltpu.VMEM((5 * BLOCK, LANES), MXU_DTYPE)],
        compiler_params=_cparams(("parallel", "arbitrary")),
    )(qa, ka, ka, va, va, sink_b)


def _swa_bwd(qa, ka, va, sink_b, o32, do, lse):
    s_ = qa.shape[0]
    nsb = s_ // SWA_T
    scale = HEAD_DIM ** -0.5
    c2 = scale * LOG2E

    def body(q_ref, kc_ref, kp_ref, vc_ref, vp_ref, sk_ref, o_ref, do_ref, lse_ref,
             dq_ref, dk_ref, dv_ref, dsk_ref, kx, vx, kacc, vacc, kcar, vcar):
        j = pl.program_id(1)
        kx[0:BLOCK, :] = kp_ref[...]
        kx[BLOCK:5 * BLOCK, :] = kc_ref[...]
        vx[0:BLOCK, :] = vp_ref[...]
        vx[BLOCK:5 * BLOCK, :] = vc_ref[...]
        band, band0 = _swa_masks(nsb - 1 - j)
        kacc[...] = jnp.zeros(kacc.shape, F32)
        vacc[...] = jnp.zeros(vacc.shape, F32)

        @pl.when(j == 0)
        def _():
            kcar[...] = jnp.zeros(kcar.shape, F32)
            vcar[...] = jnp.zeros(vcar.shape, F32)
            dsk_ref[...] = jnp.zeros(dsk_ref.shape, F32)

        for hh in range(SWA_GROUP):
            sink2 = sk_ref[0, hh:hh + 1, 0:1] * LOG2E
            cs = slice(hh * LANES, (hh + 1) * LANES)
            dsink = jnp.zeros((1, 1), F32)
            for b in range(4):
                rs = slice(b * BLOCK, (b + 1) * BLOCK)
                ks = slice(b * BLOCK, (b + 2) * BLOCK)
                q, k2, v2 = q_ref[rs, cs], kx[ks, :], vx[ks, :]
                d = do_ref[rs, cs]
                delta = jnp.sum((d * o_ref[rs, cs]).T, axis=0, keepdims=True)
                l2 = lse_ref[hh, :, rs]
                st = lax.dot_general(k2, q, NT, preferred_element_type=F32) * c2
                pt = jnp.exp2(jnp.where(band0 if b == 0 else band, st, -jnp.inf) - l2)
                db = d.astype(MXU_DTYPE)
                dst = (pt * (lax.dot_general(v2, db, NT, preferred_element_type=F32) - delta) * scale).astype(MXU_DTYPE)
                dq_ref[rs, cs] = lax.dot_general(dst, k2, TN, preferred_element_type=F32)
                kacc[ks, :] += jnp.dot(dst, q, preferred_element_type=F32)
                vacc[ks, :] += jnp.dot(pt.astype(MXU_DTYPE), db, preferred_element_type=F32)
                dsink = dsink - jnp.sum(jnp.exp2(sink2 - l2) * delta, axis=1, keepdims=True)
            dsk_ref[0, hh:hh + 1, :] += jnp.broadcast_to(dsink, (1, LANES))

        dk_ref[0:3 * BLOCK, :] = kacc[BLOCK:4 * BLOCK, :]
        dk_ref[3 * BLOCK:4 * BLOCK, :] = kacc[4 * BLOCK:5 * BLOCK, :] + kcar[...]
        dv_ref[0:3 * BLOCK, :] = vacc[BLOCK:4 * BLOCK, :].astype(dv_ref.dtype)
        dv_ref[3 * BLOCK:4 * BLOCK, :] = (vacc[4 * BLOCK:5 * BLOCK, :] + vcar[...]).astype(dv_ref.dtype)
        kcar[...] = kacc[0:BLOCK, :]
        vcar[...] = vacc[0:BLOCK, :]

    q, cur, prev, sink, lse_spec = _swa_in_specs(True, nsb)
    return pl.pallas_call(
        body, name="swa_bwd", grid=(SWA_KV_HEADS, nsb),
        in_specs=[q, cur, prev, cur, prev, sink, q, q, lse_spec],
        out_specs=[q, cur, cur, sink],
        out_shape=[jax.ShapeDtypeStruct((s_, SWA_HEADS * LANES), F32), jax.ShapeDtypeStruct((s_, SWA_KV_HEADS * LANES), F32),
                   jax.ShapeDtypeStruct((s_, SWA_KV_HEADS * LANES), MXU_DTYPE),
                   jax.ShapeDtypeStruct((SWA_KV_HEADS, SUBLANES, LANES), F32)],
        scratch_shapes=[pltpu.VMEM((5 * BLOCK, LANES), MXU_DTYPE), pltpu.VMEM((5 * BLOCK, LANES), MXU_DTYPE),
                        pltpu.VMEM((5 * BLOCK, LANES), F32), pltpu.VMEM((5 * BLOCK, LANES), F32),
                        pltpu.VMEM((BLOCK, LANES), F32), pltpu.VMEM((BLOCK, LANES), F32)],
        compiler_params=_cparams(("arbitrary", "arbitrary")),
    )(qa, ka, ka, va, va, sink_b, o32, do, lse)


MLA_T = 512


def _mla_specs(s_, t):
    qs = pl.BlockSpec((t, LANES), lambda h, i: (i, h))
    kv = pl.BlockSpec((s_, LANES), lambda h, i: (0, h))
    row = pl.BlockSpec((1, 1, t), lambda h, i: (h, 0, i))
    return qs, kv, row


def _causal_scores_t(k, q, t, c2, masked):
    st = lax.dot_general(k, q, NT, preferred_element_type=F32) * c2
    if masked:
        kr = lax.broadcasted_iota(jnp.int32, (t, t), 0)
        qc = lax.broadcasted_iota(jnp.int32, (t, t), 1)
        st = jnp.where(kr <= qc, st, -jnp.inf)
    return st


def _mla_fwd(qc, kc, vp):
    s_ = qc.shape[0]
    t = min(MLA_T, s_)
    c2 = MLA_QK ** -0.5 * LOG2E

    def body(q_ref, k_ref, v_ref, o32_ref, o16_ref, lse_ref, m_s, l_s, acc_s):
        qi = pl.program_id(1)
        q = q_ref[...]
        m_s[...] = jnp.full(m_s.shape, -jnp.inf, F32)
        l_s[...] = jnp.zeros(l_s.shape, F32)
        acc_s[...] = jnp.zeros(acc_s.shape, F32)

        def step(ki, masked):
            off = pl.multiple_of(ki * t, t)
            st = _causal_scores_t(k_ref[pl.ds(off, t), :], q, t, c2, masked)
            m_old = m_s[...]
            m_new = jnp.maximum(m_old, jnp.max(st, axis=0, keepdims=True))
            alpha = jnp.exp2(m_old - m_new)
            pt = jnp.exp2(st - m_new)
            l_s[...] = alpha * l_s[...] + jnp.sum(pt, axis=0, keepdims=True)
            acc_s[...] = alpha * acc_s[...] + lax.dot_general(
                v_ref[pl.ds(off, t), :], pt.astype(MXU_DTYPE), TN, preferred_element_type=F32)
            m_s[...] = m_new

        def full_block(ki, carry):
            step(ki, False)
            return carry

        lax.fori_loop(0, qi, full_block, 0)
        step(qi, True)
        o = (acc_s[...] * (1.0 / l_s[...])).T
        o32_ref[...] = o
        o16_ref[...] = o.astype(o16_ref.dtype)
        lse_ref[0] = m_s[...] + jnp.log2(l_s[...])

    qs, kv, row = _mla_specs(s_, t)
    return pl.pallas_call(
        body, name="mla_fwd", grid=(MLA_HEADS, s_ // t), in_specs=[qs, kv, kv], out_specs=[qs, qs, row],
        out_shape=[jax.ShapeDtypeStruct((s_, MLA_HEADS * LANES), F32), jax.ShapeDtypeStruct((s_, MLA_HEADS * LANES), MXU_DTYPE),
                   jax.ShapeDtypeStruct((MLA_HEADS, 1, s_), F32)],
        scratch_shapes=[pltpu.VMEM((1, t), F32), pltpu.VMEM((1, t), F32), pltpu.VMEM((LANES, t), F32)],
        compiler_params=_cparams(("parallel", "arbitrary")),
    )(qc, kc, vp)


def _mla_bwd(qc, kc, vp, dob, lse, delta):
    s_ = qc.shape[0]
    t = min(MLA_T, s_)
    scale = MLA_QK ** -0.5
    c2 = scale * LOG2E

    def body(q_ref, do_ref, lse_ref, dl_ref, k_ref, v_ref, dq_ref, dk_ref, dv_ref, dqt_s):
        qi = pl.program_id(1)

        @pl.when(qi == 0)
        def _():
            dk_ref[...] = jnp.zeros(dk_ref.shape, F32)
            dv_ref[...] = jnp.zeros(dv_ref.shape, F32)

        q, d, l2, dl = q_ref[...], do_ref[...], lse_ref[0], dl_ref[0]
        dqt_s[...] = jnp.zeros(dqt_s.shape, F32)

        def step(ki, masked):
            off = pl.multiple_of(ki * t, t)
            k = k_ref[pl.ds(off, t), :]
            pt = jnp.exp2(_causal_scores_t(k, q, t, c2, masked) - l2)
            dpt = lax.dot_general(v_ref[pl.ds(off, t), :], d, NT, preferred_element_type=F32)
            dst = (pt * (dpt - dl) * scale).astype(MXU_DTYPE)
            dv_ref[pl.ds(off, t), :] += jnp.dot(pt.astype(MXU_DTYPE), d, preferred_element_type=F32)
            dk_ref[pl.ds(off, t), :] += jnp.dot(dst, q, preferred_element_type=F32)
            dqt_s[...] += lax.dot_general(k, dst, TN, preferred_element_type=F32)

        def full_block(ki, carry):
            step(ki, False)
            return carry

        lax.fori_loop(0, qi, full_block, 0)
        step(qi, True)
        dq_ref[...] = dqt_s[...].T

    qs, kv, row = _mla_specs(s_, t)
    shp = jax.ShapeDtypeStruct((s_, MLA_HEADS * LANES), F32)
    return pl.pallas_call(
        body, name="mla_bwd", grid=(MLA_HEADS, s_ // t), in_specs=[qs, qs, row, row, kv, kv], out_specs=[qs, kv, kv],
        out_shape=[shp, shp, shp], scratch_shapes=[pltpu.VMEM((LANES, t), F32)],
        compiler_params=_cparams(("parallel", "arbitrary")),
    )(qc, dob, lse, delta, kc, vp)


def _pad_heads(w, nh, hd, axis):
    shp = w.shape
    w = w.reshape(shp[:axis] + (nh, hd) + shp[axis + 1:])
    pad = [(0, 0)] * w.ndim
    pad[axis + 1] = (0, LANES - hd)
    w = jnp.pad(w, pad)
    return w.reshape(shp[:axis] + (nh * LANES,) + shp[axis + 1:])


def _unpad_heads(w, nh, hd, axis):
    shp = w.shape
    w = w.reshape(shp[:axis] + (nh, LANES) + shp[axis + 1:])
    w = lax.slice_in_dim(w, 0, hd, axis=axis + 1)
    return w.reshape(shp[:axis] + (nh * hd,) + shp[axis + 1:])


def _to_operands(w):
    w_in = w["w_in"]
    piece = lambda i: w_in[:, IN_OFF[i]:IN_OFF[i + 1]]
    z = lambda n: jnp.zeros((D_MODEL, n), w_in.dtype)
    kr = jnp.concatenate([z(KR_LANE), piece(5), z(LANES - KR_LANE - MLA_ROPE)], 1)
    win = jnp.concatenate([piece(6), piece(7), _pad_heads(piece(0), SWA_HEADS, HEAD_DIM, 1), piece(3), kr,
                           _pad_heads(piece(1), SWA_KV_HEADS, HEAD_DIM, 1), _pad_heads(piece(2), SWA_KV_HEADS, HEAD_DIM, 1),
                           piece(4)], 1)
    ukv = w["w_ukv"].reshape(KV_LORA, MLA_HEADS, MLA_NOPE + MLA_V)
    return dict(
        win=win,
        wuq=_pad_heads(w["w_uq"], MLA_HEADS, MLA_QK, 1),
        wuk=_pad_heads(ukv[:, :, :MLA_NOPE].reshape(KV_LORA, -1), MLA_HEADS, MLA_NOPE, 1),
        wuv=_pad_heads(ukv[:, :, MLA_NOPE:].reshape(KV_LORA, -1), MLA_HEADS, MLA_V, 1),
        woa=_pad_heads(w["w_o_swa"], SWA_HEADS, HEAD_DIM, 0),
        wob=_pad_heads(w["w_o_mla"], MLA_HEADS, MLA_V, 0),
        wout=w["w_out"],
        wgu=jnp.concatenate([w["w_gate"], w["w_up"]], 1),
        wd=w["w_down"],
    )


def _from_operand_grads(g):
    d = g["win"]
    dkr = d[:, P_KR + KR_LANE:P_KR + KR_LANE + MLA_ROPE]
    w_in = jnp.concatenate([
        _unpad_heads(d[:, P_Q:P_Q + 1024], SWA_HEADS, HEAD_DIM, 1), _unpad_heads(d[:, P_K:P_K + 256], SWA_KV_HEADS, HEAD_DIM, 1),
        _unpad_heads(d[:, P_V:P_V + 256], SWA_KV_HEADS, HEAD_DIM, 1), d[:, P_QLAT:P_QLAT + Q_LORA],
        d[:, P_KVLAT:P_KVLAT + KV_LORA], dkr, d[:, P_GA:P_GA + 1024], d[:, P_GB:P_GB + 1024]], 1)
    uk = _unpad_heads(g["wuk"], MLA_HEADS, MLA_NOPE, 1).reshape(KV_LORA, MLA_HEADS, MLA_NOPE)
    uv = _unpad_heads(g["wuv"], MLA_HEADS, MLA_V, 1).reshape(KV_LORA, MLA_HEADS, MLA_V)
    return dict(
        w_in=w_in,
        w_uq=_unpad_heads(g["wuq"], MLA_HEADS, MLA_QK, 1),
        w_ukv=jnp.concatenate([uk, uv], 2).reshape(KV_LORA, -1),
        w_o_swa=_unpad_heads(g["woa"], SWA_HEADS, HEAD_DIM, 0),
        w_o_mla=_unpad_heads(g["wob"], MLA_HEADS, MLA_V, 0),
        w_out=g["wout"],
        w_gate=g["wgu"][:, :D_FF],
        w_up=g["wgu"][:, D_FF:],
        w_down=g["wd"],
    )


def _local_step(x, tgt, ops, small):
    s_ = x.shape[0]
    tabs = _rope_tables(s_)
    sink_b = jnp.broadcast_to(small["swa_sinks"].reshape(SWA_KV_HEADS, SWA_GROUP, 1), (SWA_KV_HEADS, SWA_GROUP, LANES))
    sink_b = jnp.pad(sink_b, ((0, 0), (0, SUBLANES - SWA_GROUP), (0, 0)))

    h = _norm_fwd(x, small["mix_norm_g"], name="norm1")
    p = _mm(h, ops["win"], "nn", name="proj_in", tn=2176)
    qa, ka, va, cq, ckv, kro = _attn_prep(p, small["q_norm_g"], small["kv_norm_g"], tabs)
    oa32, oa16, lse_a = _swa_fwd(qa, ka, va, sink_b)
    qp = _mm(cq, ops["wuq"], "nn", name="mla_q_up", tn=1024)
    kp = _mm(ckv, ops["wuk"], "nn", name="mla_k_up", tn=1024)
    vp = _mm(ckv, ops["wuv"], "nn", name="mla_v_up", tn=1024, out_dtype=MXU_DTYPE)
    qc, kc = _mla_prep(qp, kp, kro, tabs)
    ob32, ob16, lse_b = _mla_fwd(qc, kc, vp)
    ta = _mm(oa16, ops["woa"], "nn", name="o_swa", tn=1024)
    tb = _mm(ob16, ops["wob"], "nn", name="o_mla", tn=1024)
    y = _gate_fwd(p, ta, tb)
    x1 = _mm(y, ops["wout"], "nn", name="out_proj", add=x, tn=1024)
    h2 = _norm_fwd(x1, small["ffn_norm_g"], name="norm2")
    gu = _mm(h2, ops["wgu"], "nn", name="ffn_in", tn=512)
    act = _swiglu_fwd(gu)
    x2 = _mm(act, ops["wd"], "nn", name="ffn_out", add=x1, tn=1024)

    dx2, dx2b, dg3, _, tot = _loss_bwd(x2, small["final_norm_g"].reshape(1, D_MODEL), tgt)
    g = {}
    dact = _mm(dx2b, ops["wd"], "nt", name="d_act", tn=1408)
    g["wd"] = _mm(act, dx2b, "tn", name="dw_down", tm=1408, tn=1024, tk=512)
    dgu = _swiglu_bwd(gu, dact)
    dh2 = _mm(dgu, ops["wgu"], "nt", name="d_h2", tn=1024, tk=2816)
    g["wgu"] = _mm(h2, dgu, "tn", name="dw_ffn_in", tm=1024, tn=512, tk=512)
    dx1, dx1b, dg2 = _norm_bwd(x1, small["ffn_norm_g"], dh2, dx2, name="norm2_bwd")
    dy = _mm(dx1b, ops["wout"], "nt", name="d_y", tn=1024)
    g["wout"] = _mm(y, dx1b, "tn", name="dw_out", tm=1024, tn=1024, tk=512)
    dta, dtb, dgab = _gate_bwd(p, ta, tb, dy)
    doa = _mm(dta, ops["woa"], "nt", name="d_oa", tn=1024)
    g["woa"] = _mm(oa16, dta, "tn", name="dw_o_swa", tm=1024, tn=1024, tk=512)
    dob = _mm(dtb, ops["wob"], "nt", name="d_ob", tn=1024)
    g["wob"] = _mm(ob16, dtb, "tn", name="dw_o_mla", tm=1024, tn=1024, tk=512)
    dob16, delta_b = _mla_bwd_prep(dob, ob32)
    dqc, dkc, dvp = _mla_bwd(qc, kc, vp, dob16, lse_b, delta_b)
    dqp, dkv, dkr = _mla_unprep(dqc, dkc, dvp, tabs)
    dcq = _mm(dqp, ops["wuq"], "nt", name="d_cq", tn=Q_LORA)
    g["wuq"] = _mm(cq, dqp, "tn", name="dw_uq", tm=Q_LORA, tn=1024, tk=512)
    dckv = _mm(dkv, jnp.concatenate([ops["wuk"], ops["wuv"]], 1), "nt", name="d_ckv", tn=KV_LORA)
    g_ukv = _mm(ckv, dkv, "tn", name="dw_ukv", tm=KV_LORA, tn=1024, tk=512)
    g["wuk"], g["wuv"] = g_ukv[:, :1024], g_ukv[:, 1024:]
    _, dqlat, dgq = _norm_bwd(p, small["q_norm_g"], dcq, None, name="qnorm_bwd", x_cb=P_QLAT // Q_LORA)
    _, dkvlat, dgkv = _norm_bwd(p, small["kv_norm_g"], dckv, None, name="kvnorm_bwd", x_cb=P_KVLAT // KV_LORA)
    dqa, dka, dva, dsk = _swa_bwd(qa, ka, va, sink_b, oa32, doa, lse_a)
    dq_raw, dk_raw = _swa_unrope(dqa, dka, tabs)
    dp = jnp.concatenate([dgab, dq_raw, dqlat, dkr, dk_raw, dva, dkvlat], 1)
    dh = _mm(dp, ops["win"], "nt", name="d_h", tn=1024, tk=2176)
    g["win"] = _mm(h, dp, "tn", name="dw_in", tm=512, tn=2176, tk=512)
    gx, _, dg1 = _norm_bwd(x, small["mix_norm_g"], dh, dx1, name="norm1_bwd")

    sm = dict(mix_norm_g=dg1, ffn_norm_g=dg2, final_norm_g=dg3, q_norm_g=dgq, kv_norm_g=dgkv,
              swa_sinks=dsk[:, :SWA_GROUP, 0].reshape(1, SWA_HEADS))
    return tot[0, 0], gx, g, sm


COL_SHARDED = ("w_in", "w_uq", "w_ukv", "w_o_swa", "w_o_mla", "w_gate", "w_up")
ROW_SHARDED = ("w_out", "w_down")
BIG = ("w_in", "w_uq", "w_ukv", "w_o_swa", "w_o_mla", "w_out", "w_gate", "w_up", "w_down")
FULL_SHAPE = dict(w_in=(1024, 3488), w_uq=(384, 768), w_ukv=(256, 1024), w_o_swa=(512, 1024), w_o_mla=(512, 1024),
                  w_out=(1024, 1024), w_gate=(1024, 2816), w_up=(1024, 2816), w_down=(2816, 1024))
PACK_W = 1024
ROW_TILE = 16


def _shard_shape(n):
    r, c = FULL_SHAPE[n]
    return (r, c // N_DEV) if n in COL_SHARDED else (r // N_DEV, c)


def _pack_rows(n):
    r, c = _shard_shape(n)
    rows = r * c // PACK_W
    return rows, -(-rows // ROW_TILE) * ROW_TILE


PACK_OFF = {}
_o = 0
for _n in BIG:
    PACK_OFF[_n] = _o
    _o += _pack_rows(_n)[1]
PACK_ROWS = _o


def _pack(shards, dtype):
    parts = []
    for n in BIG:
        a = shards[n]
        lead = a.shape[:-2]
        rows, prow = _pack_rows(n)
        a = a.astype(dtype).reshape(lead + (rows, PACK_W))
        parts.append(jnp.pad(a, [(0, 0)] * len(lead) + [(0, prow - rows), (0, 0)]))
    return jnp.concatenate(parts, axis=-2)


def _unpack(packed):
    out = {}
    lead = packed.shape[:-2]
    for n in BIG:
        rows, _ = _pack_rows(n)
        a = lax.slice_in_dim(packed, PACK_OFF[n], PACK_OFF[n] + rows, axis=packed.ndim - 2)
        out[n] = a.reshape(lead + _shard_shape(n))
    return out


def _shards_to_full(sh):
    full = {}
    for n in BIG:
        a = sh[n]
        r, c = FULL_SHAPE[n]
        full[n] = jnp.moveaxis(a, 0, 1).reshape(r, c) if n in COL_SHARDED else a.reshape(r, c)
    return full


def _full_to_shards(full):
    sh = {}
    for n in BIG:
        a = full[n]
        r, c = FULL_SHAPE[n]
        if n in COL_SHARDED:
            sh[n] = jnp.moveaxis(a.reshape(r, N_DEV, c // N_DEV), 1, 0)
        else:
            sh[n] = a.reshape(N_DEV, r // N_DEV, c)
    return sh


MESH = pl.DeviceIdType.MESH
ANY = pl.BlockSpec(memory_space=pl.ANY)


def _position():
    return lax.axis_index("x"), lax.axis_index("y"), lax.axis_index("c")


def _all_gather(block, *, name):
    r, c_ = block.shape

    def body(x_ref, out_ref, send_sems, recv_sems, local_sem):
        x, y, c = _position()
        me, sibling = (x, y, c), (x, y, 1 - c)
        chips = [(1 - x, y), (x, 1 - y), (1 - x, 1 - y)]

        def slot(px, py, pc):
            return out_ref.at[4 * px + 2 * py + pc]

        def copy(k, blk, to, src=None):
            return pltpu.make_async_remote_copy(
                src_ref=slot(*blk) if src is None else src, dst_ref=slot(*blk), send_sem=send_sems.at[k],
                recv_sem=recv_sems.at[k], device_id=to, device_id_type=MESH)

        mine = pltpu.make_async_copy(x_ref, slot(*me), local_sem)
        mine.start()
        first = [copy(0, me, sibling, src=x_ref)]
        first += [copy(1 + j, me, (*chip, c), src=x_ref) for j, chip in enumerate(chips)]
        for cp in first:
            cp.start()
        passed = [copy(4 + j, (*chip, c), sibling) for j, chip in enumerate(chips)]
        for j, chip in enumerate(chips):
            copy(1 + j, (*chip, c), me).wait_recv()
            passed[j].start()
        copy(0, sibling, me).wait_recv()
        for j, chip in enumerate(chips):
            copy(4 + j, (*chip, 1 - c), me).wait_recv()
        for cp in first + passed:
            cp.wait_send()
        mine.wait()

    return pl.pallas_call(
        body, name=name, out_shape=jax.ShapeDtypeStruct((N_DEV, r, c_), block.dtype), in_specs=[ANY], out_specs=ANY,
        scratch_shapes=[pltpu.SemaphoreType.DMA((7,)), pltpu.SemaphoreType.DMA((7,)), pltpu.SemaphoreType.DMA],
    )(block)


def _exchange_sibling(g, *, name):
    _, r, c_ = g.shape

    def body(g_ref, land_ref, send_sems, recv_sems):
        x, y, c = _position()
        copies = [pltpu.make_async_remote_copy(
            src_ref=g_ref.at[2 * j + (1 - c)], dst_ref=land_ref.at[j], send_sem=send_sems.at[j], recv_sem=recv_sems.at[j],
            device_id=(x, y, 1 - c), device_id_type=MESH) for j in range(4)]
        for cp in copies:
            cp.start()
        for cp in copies:
            cp.wait_recv()
        for cp in copies:
            cp.wait_send()

    return pl.pallas_call(
        body, name=name, out_shape=jax.ShapeDtypeStruct((4, r, c_), g.dtype), in_specs=[ANY], out_specs=ANY,
        scratch_shapes=[pltpu.SemaphoreType.DMA((4,)), pltpu.SemaphoreType.DMA((4,))],
    )(g)


def _exchange_chips(p, *, name):
    _, r, c_ = p.shape

    def body(p_ref, land_ref, send_sems, recv_sems, local_sem):
        x, y, c = _position()
        mine = 2 * x + y
        own = pltpu.make_async_copy(p_ref.at[mine], land_ref.at[mine], local_sem)
        own.start()
        copies = []
        for k, (px, py) in enumerate([(1 - x, y), (x, 1 - y), (1 - x, 1 - y)]):
            copies.append(pltpu.make_async_remote_copy(
                src_ref=p_ref.at[2 * px + py], dst_ref=land_ref.at[mine], send_sem=send_sems.at[k], recv_sem=recv_sems.at[k],
                device_id=(px, py, c), device_id_type=MESH))
        for cp in copies:
            cp.start()
        for cp in copies:
            cp.wait_recv()
        for cp in copies:
            cp.wait_send()
        own.wait()

    return pl.pallas_call(
        body, name=name, out_shape=jax.ShapeDtypeStruct((4, r, c_), p.dtype), in_specs=[ANY], out_specs=ANY,
        scratch_shapes=[pltpu.SemaphoreType.DMA((3,)), pltpu.SemaphoreType.DMA((3,)), pltpu.SemaphoreType.DMA],
    )(p)


def _pair_add(g, land, c_idx, *, tr=368):
    _, r, c_ = g.shape

    def body(c_ref, g_ref, l_ref, o_ref):
        o_ref[...] = (g_ref[...].astype(F32) + l_ref[...].astype(F32)).astype(o_ref.dtype)

    return pl.pallas_call(
        body, name="rs_pair_add",
        grid_spec=pltpu.PrefetchScalarGridSpec(
            num_scalar_prefetch=1, grid=(4, r // tr),
            in_specs=[pl.BlockSpec((1, tr, c_), lambda j, i, cr: (2 * j + cr[0], i, 0)),
                      pl.BlockSpec((1, tr, c_), lambda j, i, cr: (j, i, 0))],
            out_specs=pl.BlockSpec((1, tr, c_), lambda j, i, cr: (j, i, 0))),
        out_shape=jax.ShapeDtypeStruct((4, r, c_), g.dtype),
        compiler_params=_cparams(("parallel", "parallel")),
    )(c_idx, g, land)


def _adamw(w, g, m, v):
    m = ADAM_B1 * m + (1.0 - ADAM_B1) * g
    v = ADAM_B2 * v + (1.0 - ADAM_B2) * (g * g)
    m_hat = m / (1.0 - ADAM_B1 ** ADAM_STEP)
    v_hat = v / (1.0 - ADAM_B2 ** ADAM_STEP)
    delta = -ADAM_LR * (m_hat / (jnp.sqrt(v_hat) + ADAM_EPS) + ADAM_WD * w)
    return delta, m, v


def _sum_adamw(land, w, m, v, *, tr=368):
    _, r, c_ = land.shape

    def body(l_ref, w_ref, m_ref, v_ref, g_ref, d_ref, mo_ref, vo_ref):
        g = l_ref[0].astype(F32)
        for j in range(1, 4):
            g = g + l_ref[j].astype(F32)
        d, mn, vn = _adamw(w_ref[...], g, m_ref[...], v_ref[...])
        g_ref[...] = g
        d_ref[...] = d
        mo_ref[...] = mn
        vo_ref[...] = vn

    row = pl.BlockSpec((tr, c_), lambda i: (i, 0))
    shp = jax.ShapeDtypeStruct((r, c_), F32)
    return pl.pallas_call(
        body, name="rs_sum_adamw", grid=(r // tr,),
        in_specs=[pl.BlockSpec((4, tr, c_), lambda i: (0, i, 0)), row, row, row], out_specs=[row] * 4,
        out_shape=[shp] * 4, compiler_params=_cparams(("parallel",)),
    )(land, w, m, v)


SMALL = ("mix_norm_g", "ffn_norm_g", "final_norm_g", "q_norm_g", "kv_norm_g", "swa_sinks")
SMALL_W = dict(mix_norm_g=1024, ffn_norm_g=1024, final_norm_g=1024, q_norm_g=Q_LORA, kv_norm_g=KV_LORA, swa_sinks=SWA_HEADS)


def _small_adamw(parts, w, m, v):
    n_par = len(SMALL)

    def body(p_ref, w_ref, m_ref, v_ref, g_ref, d_ref, mo_ref, vo_ref):
        tot = p_ref[0]
        for dev in range(1, N_DEV):
            tot = tot + p_ref[dev]
        row_id = lax.broadcasted_iota(jnp.int32, (SUBLANES, PACK_W), 0)
        g = jnp.zeros((SUBLANES, PACK_W), F32)
        for k in range(n_par):
            g = jnp.where(row_id == k, jnp.sum(tot[k * SUBLANES:(k + 1) * SUBLANES, :], axis=0, keepdims=True), g)
        d, mn, vn = _adamw(w_ref[...], g, m_ref[...], v_ref[...])
        g_ref[...] = g
        d_ref[...] = d
        mo_ref[...] = mn
        vo_ref[...] = vn

    shp = jax.ShapeDtypeStruct((SUBLANES, PACK_W), F32)
    vm = pl.BlockSpec(memory_space=pltpu.VMEM)
    return pl.pallas_call(body, name="small_adamw", in_specs=[vm] * 4, out_specs=[vm] * 4, out_shape=[shp] * 4)(parts, w, m, v)


def _small_pack(d, rows_each):
    parts = [jnp.pad(d[n].astype(F32), ((0, 0), (0, PACK_W - SMALL_W[n]))) for n in SMALL]
    out = jnp.concatenate(parts, 0)
    pad = -out.shape[0] % SUBLANES
    return jnp.pad(out, ((0, pad), (0, 0)))


def kernel(x, mix_norm_g, w_in, swa_sinks, q_norm_g, w_uq, kv_norm_g, w_ukv, w_o_swa, w_o_mla, w_out, ffn_norm_g, w_gate, w_up, w_down, final_norm_g, loss_target, m_mix_norm_g, m_w_in, m_swa_sinks, m_q_norm_g, m_w_uq, m_kv_norm_g, m_w_ukv, m_w_o_swa, m_w_o_mla, m_w_out, m_ffn_norm_g, m_w_gate, m_w_up, m_w_down, m_final_norm_g, v_mix_norm_g, v_w_in, v_swa_sinks, v_q_norm_g, v_w_uq, v_kv_norm_g, v_w_ukv, v_w_o_swa, v_w_o_mla, v_w_out, v_ffn_norm_g, v_w_gate, v_w_up, v_w_down, v_final_norm_g):
    big_w = dict(w_in=w_in[0], w_uq=w_uq[0], w_ukv=w_ukv[0], w_o_swa=w_o_swa[0], w_o_mla=w_o_mla[0], w_out=w_out[0],
                 w_gate=w_gate[0], w_up=w_up[0], w_down=w_down[0])
    big_m = dict(w_in=m_w_in[0], w_uq=m_w_uq[0], w_ukv=m_w_ukv[0], w_o_swa=m_w_o_swa[0], w_o_mla=m_w_o_mla[0],
                 w_out=m_w_out[0], w_gate=m_w_gate[0], w_up=m_w_up[0], w_down=m_w_down[0])
    big_v = dict(w_in=v_w_in[0], w_uq=v_w_uq[0], w_ukv=v_w_ukv[0], w_o_swa=v_w_o_swa[0], w_o_mla=v_w_o_mla[0],
                 w_out=v_w_out[0], w_gate=v_w_gate[0], w_up=v_w_up[0], w_down=v_w_down[0])
    small_w = dict(mix_norm_g=mix_norm_g, ffn_norm_g=ffn_norm_g, final_norm_g=final_norm_g.reshape(1, D_MODEL),
                   q_norm_g=q_norm_g, kv_norm_g=kv_norm_g, swa_sinks=swa_sinks)
    small_m = dict(mix_norm_g=m_mix_norm_g, ffn_norm_g=m_ffn_norm_g, final_norm_g=m_final_norm_g.reshape(1, D_MODEL),
                   q_norm_g=m_q_norm_g, kv_norm_g=m_kv_norm_g, swa_sinks=m_swa_sinks)
    small_v = dict(mix_norm_g=v_mix_norm_g, ffn_norm_g=v_ffn_norm_g, final_norm_g=v_final_norm_g.reshape(1, D_MODEL),
                   q_norm_g=v_q_norm_g, kv_norm_g=v_kv_norm_g, swa_sinks=v_swa_sinks)

    gathered = _all_gather(_pack(big_w, WIRE_DTYPE), name="ag_weights")
    ops = _to_operands(_shards_to_full(_unpack(gathered)))

    loss_tot, gx, g_ops, g_small = _local_step(x[0], loss_target[0], ops, small_w)

    g_full = _from_operand_grads(g_ops)
    g_pack = _pack(_full_to_shards(g_full), WIRE_DTYPE)
    c_idx = lax.axis_index("c").astype(jnp.int32).reshape(1)
    pair = _pair_add(g_pack, _exchange_sibling(g_pack, name="rs_sibling"), c_idx)
    land = _exchange_chips(pair, name="rs_chips")
    gw, dw, mw, vw = _sum_adamw(land, _pack(big_w, F32), _pack(big_m, F32), _pack(big_v, F32))
    gw, dw, mw, vw = _unpack(gw), _unpack(dw), _unpack(mw), _unpack(vw)

    parts = _all_gather(_small_pack(g_small_rows(g_small), SUBLANES), name="ag_small")
    gs, ds, ms, vs = _small_adamw(parts, _small_pack(small_w, 1), _small_pack(small_m, 1), _small_pack(small_v, 1))

    def small_out(packed):
        out = {}
        for k, n in enumerate(SMALL):
            out[n] = packed[k:k + 1, :SMALL_W[n]]
        out["final_norm_g"] = out["final_norm_g"].reshape(D_MODEL)
        return out

    gs, ds, ms, vs = small_out(gs), small_out(ds), small_out(ms), small_out(vs)
    loss = lax.psum(loss_tot, AXES)

    order = ("mix_norm_g", "w_in", "swa_sinks", "q_norm_g", "w_uq", "kv_norm_g", "w_ukv", "w_o_swa", "w_o_mla", "w_out",
             "ffn_norm_g", "w_gate", "w_up", "w_down", "final_norm_g")

    def leaves(big, small):
        return [big[n][None] if n in big else small[n] for n in order]

    return (loss, gx[None], *leaves(gw, gs), *leaves(dw, ds), *leaves(mw, ms), *leaves(vw, vs))


def g_small_rows(g_small):
    out = dict(g_small)
    out["swa_sinks"] = jnp.pad(g_small["swa_sinks"], ((0, SUBLANES - 1), (0, 0)))
    return out
```

```python
import functools

import numpy as np
import jax
import jax.numpy as jnp
from jax import lax
from jax.experimental import pallas as pl
from jax.experimental.pallas import tpu as pltpu

F32 = jnp.float32
MXU_DTYPE = jnp.bfloat16
WIRE_DTYPE = jnp.bfloat16

D_MODEL = 1024
EPS = 1e-6
ROPE_THETA = 10000.0
BLOCK = 128
HEAD_DIM = 64
SWA_HEADS = 8
SWA_KV_HEADS = 2
SWA_GROUP = SWA_HEADS // SWA_KV_HEADS
MLA_HEADS = 8
MLA_NOPE = 64
MLA_ROPE = 32
MLA_V = 64
MLA_QK = MLA_NOPE + MLA_ROPE
Q_LORA = 384
KV_LORA = 256
D_FF = 2816
IN_SIZES = (512, 128, 128, Q_LORA, KV_LORA, MLA_ROPE, D_MODEL, D_MODEL)
IN_OFF = tuple(int(v) for v in np.cumsum((0,) + IN_SIZES))
ADAM_LR, ADAM_B1, ADAM_B2, ADAM_EPS, ADAM_WD, ADAM_STEP = 0.001, 0.9, 0.999, 1e-08, 0.01, 10

LANES = 128
SUBLANES = 8
VMEM_LIMIT = 48 * 1024 * 1024
N_DEV = 8
AXES = ("x", "y", "c")

P_GA, P_GB, P_Q, P_QLAT, P_KR, P_K, P_V, P_KVLAT, P_W = 0, 1024, 2048, 3072, 3456, 3584, 3840, 4096, 4352
KR_LANE = 64

LOG2E = 1.4426950408889634

NT = (((1,), (1,)), ((), ()))
NN = (((1,), (0,)), ((), ()))
TN = (((0,), (0,)), ((), ()))


def _cparams(sem):
    return pltpu.CompilerParams(dimension_semantics=sem, vmem_limit_bytes=VMEM_LIMIT)


def _mm(a, b, mode, *, name, out_dtype=F32, add=None, tm=512, tn=512, tk=None):
    if mode == "nn":
        (M, K), (K2, N) = a.shape, b.shape
    elif mode == "nt":
        (M, K), (N, K2) = a.shape, b.shape
    else:
        (K, M), (K2, N) = a.shape, b.shape
    assert K == K2, (a.shape, b.shape, mode)
    tk = K if tk is None else tk
    tm, tn = min(tm, M), min(tn, N)
    assert M % tm == 0 and N % tn == 0 and K % tk == 0, (M, N, K, tm, tn, tk)
    nk = K // tk
    dn = {"nn": NN, "nt": NT, "tn": TN}[mode]
    if mode == "tn":
        a_spec = pl.BlockSpec((tk, tm), lambda i, j, k: (k, i))
    else:
        a_spec = pl.BlockSpec((tm, tk), lambda i, j, k: (i, k))
    if mode == "nt":
        b_spec = pl.BlockSpec((tn, tk), lambda i, j, k: (j, k))
    else:
        b_spec = pl.BlockSpec((tk, tn), lambda i, j, k: (k, j))
    o_spec = pl.BlockSpec((tm, tn), lambda i, j, k: (i, j))
    has_add = add is not None

    def body(*refs):
        a_ref, b_ref = refs[0], refs[1]
        add_ref = refs[2] if has_add else None
        o_ref = refs[3] if has_add else refs[2]
        p = lax.dot_general(a_ref[...], b_ref[...], dn, preferred_element_type=F32)

        def finish(acc):
            if has_add:
                acc = acc + add_ref[...]
            o_ref[...] = acc.astype(o_ref.dtype)

        if nk == 1:
            finish(p)
        else:
            acc_ref = refs[-1]
            k = pl.program_id(2)

            @pl.when(k == 0)
            def _():
                acc_ref[...] = p

            @pl.when(k > 0)
            def _():
                acc_ref[...] += p

            @pl.when(k == nk - 1)
            def _():
                finish(acc_ref[...])

    ins = [a, b] + ([add] if has_add else [])
    in_specs = [a_spec, b_spec] + ([o_spec] if has_add else [])
    return pl.pallas_call(
        body, name=name, grid=(M // tm, N // tn, nk), in_specs=in_specs, out_specs=o_spec,
        out_shape=jax.ShapeDtypeStruct((M, N), out_dtype),
        scratch_shapes=[pltpu.VMEM((tm, tn), F32)] if nk > 1 else [],
        compiler_params=_cparams(("parallel", "parallel", "arbitrary")),
    )(*ins)


def _rows(ts, w, cb=0):
    return pl.BlockSpec((ts, w), lambda i: (i, cb))


def _const(r, w):
    return pl.BlockSpec((r, w), lambda i: (0, 0))


def _sublane_sum(v):
    ts, c = v.shape
    return jnp.sum(v.reshape(ts // SUBLANES, SUBLANES, c), axis=0)


def _sigmoid(v):
    return 1.0 / (1.0 + jnp.exp(-v))


def _rope(v, cos, s_up, s_dn, up, dn):
    return v * cos + pltpu.roll(v, up, 1) * s_up + pltpu.roll(v, dn, 1) * s_dn


def _rope_t(dv, cos, s_up, s_dn, up, dn):
    return dv * cos + pltpu.roll(dv * s_up, dn, 1) + pltpu.roll(dv * s_dn, up, 1)


def _rope_tables(seq):
    pos = np.arange(seq, dtype=np.float32)[:, None]

    def base(dim):
        inv = np.float32(ROPE_THETA) ** (-np.arange(0, dim, 2, dtype=np.float32) / np.float32(dim))
        ang = (pos * inv.astype(np.float32)[None, :]).astype(np.float32)
        return np.cos(ang).astype(np.float32), np.sin(ang).astype(np.float32)

    z = lambda n: np.zeros((seq, n), np.float32)
    ca, sa = base(HEAD_DIM)
    a_cos = np.concatenate([ca, ca, z(64)], 1)
    a_up = np.concatenate([-sa, z(96)], 1)
    a_dn = np.concatenate([z(32), sa, z(64)], 1)
    cb, sb = base(MLA_ROPE)
    one = np.ones((seq, 64), np.float32)
    q_cos = np.concatenate([one, cb, cb, z(32)], 1)
    k_cos = np.concatenate([z(64), cb, cb, z(32)], 1)
    b_up = np.concatenate([z(64), -sb, z(48)], 1)
    b_dn = np.concatenate([z(80), sb, z(32)], 1)
    return tuple(jnp.asarray(t) for t in (a_cos, a_up, a_dn, q_cos, k_cos, b_up, b_dn))


def _norm_fwd(x, g, *, name, ts=256):
    s_, c = x.shape

    def body(x_ref, g_ref, h_ref):
        v = x_ref[...]
        r = lax.rsqrt(jnp.mean(v * v, axis=-1, keepdims=True) + EPS)
        h_ref[...] = (v * r * g_ref[...]).astype(h_ref.dtype)

    return pl.pallas_call(
        body, name=name, grid=(s_ // ts,), in_specs=[_rows(ts, c), _const(1, c)], out_specs=_rows(ts, c),
        out_shape=jax.ShapeDtypeStruct((s_, c), MXU_DTYPE), compiler_params=_cparams(("parallel",)),
    )(x, g)


def _norm_bwd(x, g, dy, res, *, name, ts=256, x_cb=0, x_src_w=None):
    s_ = x.shape[0]
    c = dy.shape[1]
    has_res = res is not None

    def body(*refs):
        x_ref, g_ref, dy_ref = refs[0], refs[1], refs[2]
        res_ref = refs[3] if has_res else None
        dx_ref, dxb_ref, dg_ref = refs[-3], refs[-2], refs[-1]
        v = x_ref[...]
        r = lax.rsqrt(jnp.mean(v * v, axis=-1, keepdims=True) + EPS)
        xh = v * r
        d = dy_ref[...]
        dxh = d * g_ref[...]
        dx = r * (dxh - xh * jnp.mean(dxh * xh, axis=-1, keepdims=True))
        if has_res:
            dx = dx + res_ref[...]
        dx_ref[...] = dx
        dxb_ref[...] = dx.astype(dxb_ref.dtype)

        @pl.when(pl.program_id(0) == 0)
        def _():
            dg_ref[...] = jnp.zeros(dg_ref.shape, F32)

        dg_ref[...] += _sublane_sum(d * xh)

    ins = [x, g, dy] + ([res] if has_res else [])
    in_specs = [_rows(ts, c, x_cb), _const(1, c), _rows(ts, c)] + ([_rows(ts, c)] if has_res else [])
    return pl.pallas_call(
        body, name=name, grid=(s_ // ts,), in_specs=in_specs,
        out_specs=[_rows(ts, c), _rows(ts, c), _const(SUBLANES, c)],
        out_shape=[jax.ShapeDtypeStruct((s_, c), F32), jax.ShapeDtypeStruct((s_, c), MXU_DTYPE),
                   jax.ShapeDtypeStruct((SUBLANES, c), F32)],
        compiler_params=_cparams(("arbitrary",)),
    )(*ins)


def _attn_prep(p, gq, gkv, tabs, *, ts=256):
    s_ = p.shape[0]
    a_cos, a_up, a_dn, _, k_cos, b_up, b_dn = tabs

    def body(q_ref, k_ref, v_ref, ql_ref, kvl_ref, kr_ref, gq_ref, gkv_ref, ac, au, ad, kc, bu, bd,
             qa_ref, ka_ref, va_ref, cq_ref, ckv_ref, kro_ref):
        c_, u_, d_ = ac[...], au[...], ad[...]
        for h in range(SWA_HEADS):
            sl = slice(h * LANES, (h + 1) * LANES)
            qa_ref[:, sl] = _rope(q_ref[:, sl], c_, u_, d_, 96, 32).astype(qa_ref.dtype)
        for h in range(SWA_KV_HEADS):
            sl = slice(h * LANES, (h + 1) * LANES)
            ka_ref[:, sl] = _rope(k_ref[:, sl], c_, u_, d_, 96, 32).astype(ka_ref.dtype)
        va_ref[...] = v_ref[...].astype(va_ref.dtype)
        for src, gref, dst in ((ql_ref, gq_ref, cq_ref), (kvl_ref, gkv_ref, ckv_ref)):
            v = src[...]
            r = lax.rsqrt(jnp.mean(v * v, axis=-1, keepdims=True) + EPS)
            dst[...] = (v * r * gref[...]).astype(dst.dtype)
        kro_ref[...] = _rope(kr_ref[...], kc[...], bu[...], bd[...], 112, 16)

    tab = _rows(ts, LANES)
    return pl.pallas_call(
        body, name="attn_prep", grid=(s_ // ts,),
        in_specs=[_rows(ts, 1024, P_Q // 1024), _rows(ts, 256, P_K // 256), _rows(ts, 256, P_V // 256),
                  _rows(ts, Q_LORA, P_QLAT // Q_LORA), _rows(ts, KV_LORA, P_KVLAT // KV_LORA),
                  _rows(ts, LANES, P_KR // LANES), _const(1, Q_LORA), _const(1, KV_LORA), tab, tab, tab, tab, tab, tab],
        out_specs=[_rows(ts, 1024), _rows(ts, 256), _rows(ts, 256), _rows(ts, Q_LORA), _rows(ts, KV_LORA),
                   _rows(ts, LANES)],
        out_shape=[jax.ShapeDtypeStruct((s_, 1024), MXU_DTYPE), jax.ShapeDtypeStruct((s_, 256), MXU_DTYPE),
                   jax.ShapeDtypeStruct((s_, 256), MXU_DTYPE), jax.ShapeDtypeStruct((s_, Q_LORA), MXU_DTYPE),
                   jax.ShapeDtypeStruct((s_, KV_LORA), MXU_DTYPE), jax.ShapeDtypeStruct((s_, LANES), F32)],
        compiler_params=_cparams(("parallel",)),
    )(p, p, p, p, p, p, gq, gkv, a_cos, a_up, a_dn, k_cos, b_up, b_dn)


def _mla_prep(qp, kp, kro, tabs, *, ts=256):
    s_ = qp.shape[0]
    _, _, _, q_cos, _, b_up, b_dn = tabs

    def body(q_ref, k_ref, kr_ref, qc, bu, bd, qo_ref, ko_ref):
        c_, u_, d_ = qc[...], bu[...], bd[...]
        kr = kr_ref[...]
        for h in range(MLA_HEADS):
            sl = slice(h * LANES, (h + 1) * LANES)
            qo_ref[:, sl] = _rope(q_ref[:, sl], c_, u_, d_, 112, 16).astype(qo_ref.dtype)
            ko_ref[:, sl] = (k_ref[:, sl] + kr).astype(ko_ref.dtype)

    tab = _rows(ts, LANES)
    return pl.pallas_call(
        body, name="mla_prep", grid=(s_ // ts,),
        in_specs=[_rows(ts, 1024), _rows(ts, 1024), tab, tab, tab, tab],
        out_specs=[_rows(ts, 1024), _rows(ts, 1024)],
        out_shape=[jax.ShapeDtypeStruct((s_, 1024), MXU_DTYPE)] * 2,
        compiler_params=_cparams(("parallel",)),
    )(qp, kp, kro, q_cos, b_up, b_dn)


def _mla_unprep(dqc, dkc, dvp, tabs, *, ts=256):
    s_ = dqc.shape[0]
    _, _, _, q_cos, k_cos, b_up, b_dn = tabs

    def body(dq_ref, dk_ref, dv_ref, qc, kc, bu, bd, dqo_ref, dkvo_ref, dkr_ref):
        c_, u_, d_ = qc[...], bu[...], bd[...]
        tot = jnp.zeros((ts, LANES), F32)
        for h in range(MLA_HEADS):
            sl = slice(h * LANES, (h + 1) * LANES)
            dqo_ref[:, sl] = _rope_t(dq_ref[:, sl], c_, u_, d_, 112, 16).astype(dqo_ref.dtype)
            dk = dk_ref[:, sl]
            dkvo_ref[:, sl] = dk.astype(dkvo_ref.dtype)
            tot = tot + dk
        dkvo_ref[:, 1024:2048] = dv_ref[...].astype(dkvo_ref.dtype)
        dkr_ref[...] = _rope_t(tot, kc[...], u_, d_, 112, 16).astype(dkr_ref.dtype)

    tab = _rows(ts, LANES)
    return pl.pallas_call(
        body, name="mla_unprep", grid=(s_ // ts,),
        in_specs=[_rows(ts, 1024), _rows(ts, 1024), _rows(ts, 1024), tab, tab, tab, tab],
        out_specs=[_rows(ts, 1024), _rows(ts, 2048), _rows(ts, LANES)],
        out_shape=[jax.ShapeDtypeStruct((s_, 1024), MXU_DTYPE), jax.ShapeDtypeStruct((s_, 2048), MXU_DTYPE),
                   jax.ShapeDtypeStruct((s_, LANES), MXU_DTYPE)],
        compiler_params=_cparams(("parallel",)),
    )(dqc, dkc, dvp, q_cos, k_cos, b_up, b_dn)


def _swa_unrope(dqa, dka, tabs, *, ts=256):
    s_ = dqa.shape[0]
    a_cos, a_up, a_dn = tabs[0], tabs[1], tabs[2]

    def body(dq_ref, dk_ref, ac, au, ad, dqo_ref, dko_ref):
        c_, u_, d_ = ac[...], au[...], ad[...]
        for h in range(SWA_HEADS):
            sl = slice(h * LANES, (h + 1) * LANES)
            dqo_ref[:, sl] = _rope_t(dq_ref[:, sl], c_, u_, d_, 96, 32).astype(dqo_ref.dtype)
        for h in range(SWA_KV_HEADS):
            sl = slice(h * LANES, (h + 1) * LANES)
            dko_ref[:, sl] = _rope_t(dk_ref[:, sl], c_, u_, d_, 96, 32).astype(dko_ref.dtype)

    tab = _rows(ts, LANES)
    return pl.pallas_call(
        body, name="swa_unrope", grid=(s_ // ts,),
        in_specs=[_rows(ts, 1024), _rows(ts, 256), tab, tab, tab],
        out_specs=[_rows(ts, 1024), _rows(ts, 256)],
        out_shape=[jax.ShapeDtypeStruct((s_, 1024), MXU_DTYPE), jax.ShapeDtypeStruct((s_, 256), MXU_DTYPE)],
        compiler_params=_cparams(("parallel",)),
    )(dqa, dka, a_cos, a_up, a_dn)


def _gate_fwd(p, ta, tb, *, ts=256):
    s_ = p.shape[0]

    def body(ga_ref, gb_ref, ta_ref, tb_ref, y_ref):
        y = _sigmoid(ga_ref[...]) * ta_ref[...] + _sigmoid(gb_ref[...]) * tb_ref[...]
        y_ref[...] = y.astype(y_ref.dtype)

    return pl.pallas_call(
        body, name="gate_fwd", grid=(s_ // ts,),
        in_specs=[_rows(ts, 1024, P_GA // 1024), _rows(ts, 1024, P_GB // 1024), _rows(ts, 1024), _rows(ts, 1024)],
        out_specs=_rows(ts, 1024), out_shape=jax.ShapeDtypeStruct((s_, 1024), MXU_DTYPE),
        compiler_params=_cparams(("parallel",)),
    )(p, p, ta, tb)


def _gate_bwd(p, ta, tb, dy, *, ts=256):
    s_ = p.shape[0]

    def body(ga_ref, gb_ref, ta_ref, tb_ref, dy_ref, dta_ref, dtb_ref, dg_ref):
        d = dy_ref[...]
        sa, sb = _sigmoid(ga_ref[...]), _sigmoid(gb_ref[...])
        dta_ref[...] = (d * sa).astype(dta_ref.dtype)
        dtb_ref[...] = (d * sb).astype(dtb_ref.dtype)
        dg_ref[:, 0:1024] = (d * ta_ref[...] * (sa * (1.0 - sa))).astype(dg_ref.dtype)
        dg_ref[:, 1024:2048] = (d * tb_ref[...] * (sb * (1.0 - sb))).astype(dg_ref.dtype)

    return pl.pallas_call(
        body, name="gate_bwd", grid=(s_ // ts,),
        in_specs=[_rows(ts, 1024, P_GA // 1024), _rows(ts, 1024, P_GB // 1024), _rows(ts, 1024), _rows(ts, 1024),
                  _rows(ts, 1024)],
        out_specs=[_rows(ts, 1024), _rows(ts, 1024), _rows(ts, 2048)],
        out_shape=[jax.ShapeDtypeStruct((s_, 1024), MXU_DTYPE)] * 2 + [jax.ShapeDtypeStruct((s_, 2048), MXU_DTYPE)],
        compiler_params=_cparams(("parallel",)),
    )(p, p, ta, tb, dy)


def _swiglu_fwd(gu, *, ts=256):
    s_ = gu.shape[0]

    def body(g_ref, u_ref, a_ref):
        g = g_ref[...]
        a_ref[...] = (g * _sigmoid(g) * u_ref[...]).astype(a_ref.dtype)

    return pl.pallas_call(
        body, name="swiglu_fwd", grid=(s_ // ts,), in_specs=[_rows(ts, D_FF, 0), _rows(ts, D_FF, 1)],
        out_specs=_rows(ts, D_FF), out_shape=jax.ShapeDtypeStruct((s_, D_FF), MXU_DTYPE),
        compiler_params=_cparams(("parallel",)),
    )(gu, gu)


def _swiglu_bwd(gu, da, *, ts=256):
    s_ = gu.shape[0]

    def body(g_ref, u_ref, da_ref, o_ref):
        g, u, d = g_ref[...], u_ref[...], da_ref[...]
        sg = _sigmoid(g)
        o_ref[:, 0:D_FF] = (d * u * (sg * (1.0 + g * (1.0 - sg)))).astype(o_ref.dtype)
        o_ref[:, D_FF:2 * D_FF] = (d * (g * sg)).astype(o_ref.dtype)

    return pl.pallas_call(
        body, name="swiglu_bwd", grid=(s_ // ts,), in_specs=[_rows(ts, D_FF, 0), _rows(ts, D_FF, 1), _rows(ts, D_FF)],
        out_specs=_rows(ts, 2 * D_FF), out_shape=jax.ShapeDtypeStruct((s_, 2 * D_FF), MXU_DTYPE),
        compiler_params=_cparams(("parallel",)),
    )(gu, gu, da)


def _loss_bwd(x2, g, tgt, *, ts=256):
    s_, c = x2.shape

    def body(x_ref, g_ref, t_ref, dx_ref, dxb_ref, dg_ref, lp_ref, tot_ref):
        v = x_ref[...]
        r = lax.rsqrt(jnp.mean(v * v, axis=-1, keepdims=True) + EPS)
        xh = v * r
        gg = g_ref[...]
        e = xh * gg - t_ref[...]
        do = e * (1.0 / c)
        dxh = do * gg
        dx = r * (dxh - xh * jnp.mean(dxh * xh, axis=-1, keepdims=True))
        dx_ref[...] = dx
        dxb_ref[...] = dx.astype(dxb_ref.dtype)
        i = pl.program_id(0)

        @pl.when(i == 0)
        def _():
            dg_ref[...] = jnp.zeros(dg_ref.shape, F32)
            lp_ref[...] = jnp.zeros(lp_ref.shape, F32)

        dg_ref[...] += _sublane_sum(do * xh)
        lp_ref[...] += _sublane_sum(e * e)
        tot_ref[...] = jnp.full(tot_ref.shape, (0.5 / c) * jnp.sum(lp_ref[...]), F32)

    return pl.pallas_call(
        body, name="loss_bwd", grid=(s_ // ts,), in_specs=[_rows(ts, c), _const(1, c), _rows(ts, c)],
        out_specs=[_rows(ts, c), _rows(ts, c), _const(SUBLANES, c), _const(SUBLANES, c), _const(SUBLANES, LANES)],
        out_shape=[jax.ShapeDtypeStruct((s_, c), F32), jax.ShapeDtypeStruct((s_, c), MXU_DTYPE),
                   jax.ShapeDtypeStruct((SUBLANES, c), F32), jax.ShapeDtypeStruct((SUBLANES, c), F32),
                   jax.ShapeDtypeStruct((SUBLANES, LANES), F32)],
        compiler_params=_cparams(("arbitrary",)),
    )(x2, g, tgt)


def _mla_bwd_prep(dob, o32, *, ts=256):
    s_ = dob.shape[0]

    def body(do_ref, o_ref, dob_ref, dl_ref):
        d = do_ref[...]
        dob_ref[...] = d.astype(dob_ref.dtype)
        prod = d * o_ref[...]
        for h in range(MLA_HEADS):
            dl_ref[h] = jnp.sum(prod[:, h * LANES:(h + 1) * LANES].T, axis=0, keepdims=True)

    return pl.pallas_call(
        body, name="mla_bwd_prep", grid=(s_ // ts,), in_specs=[_rows(ts, 1024), _rows(ts, 1024)],
        out_specs=[_rows(ts, 1024), pl.BlockSpec((MLA_HEADS, 1, ts), lambda i: (0, 0, i))],
        out_shape=[jax.ShapeDtypeStruct((s_, 1024), MXU_DTYPE), jax.ShapeDtypeStruct((MLA_HEADS, 1, s_), F32)],
        compiler_params=_cparams(("parallel",)),
    )(dob, o32)


SWA_T = 4 * BLOCK


def _swa_masks(sb):
    kr = lax.broadcasted_iota(jnp.int32, (2 * BLOCK, BLOCK), 0)
    qc = lax.broadcasted_iota(jnp.int32, (2 * BLOCK, BLOCK), 1)
    band = jnp.logical_and(kr > qc, kr <= qc + BLOCK)
    first = jnp.logical_and(band, kr >= BLOCK)
    return band, jnp.logical_or(first, jnp.logical_and(band, sb > 0))


def _swa_in_specs(rev, nsb):
    sbi = (lambda j: nsb - 1 - j) if rev else (lambda j: j)
    cur = pl.BlockSpec((SWA_T, LANES), lambda g, j: (sbi(j), g))
    prev = pl.BlockSpec((BLOCK, LANES), lambda g, j: (jnp.maximum(4 * sbi(j) - 1, 0), g))
    q = pl.BlockSpec((SWA_T, SWA_GROUP * LANES), lambda g, j: (sbi(j), g))
    sink = pl.BlockSpec((1, SUBLANES, LANES), lambda g, j: (g, 0, 0))
    lse = pl.BlockSpec((SWA_GROUP, 1, SWA_T), lambda g, j: (g, 0, sbi(j)))
    return q, cur, prev, sink, lse


def _swa_fwd(qa, ka, va, sink_b):
    s_ = qa.shape[0]
    nsb = s_ // SWA_T
    c2 = HEAD_DIM ** -0.5 * LOG2E

    def body(q_ref, kc_ref, kp_ref, vc_ref, vp_ref, sk_ref, o32_ref, o16_ref, lse_ref, kx, vx):
        kx[0:BLOCK, :] = kp_ref[...]
        kx[BLOCK:5 * BLOCK, :] = kc_ref[...]
        vx[0:BLOCK, :] = vp_ref[...]
        vx[BLOCK:5 * BLOCK, :] = vc_ref[...]
        band, band0 = _swa_masks(pl.program_id(1))
        for hh in range(SWA_GROUP):
            sink2 = sk_ref[0, hh:hh + 1, 0:1] * LOG2E
            cs = slice(hh * LANES, (hh + 1) * LANES)
            for b in range(4):
                rs = slice(b * BLOCK, (b + 1) * BLOCK)
                ks = slice(b * BLOCK, (b + 2) * BLOCK)
                st = lax.dot_general(kx[ks, :], q_ref[rs, cs], NT, preferred_element_type=F32) * c2
                st = jnp.where(band0 if b == 0 else band, st, -jnp.inf)
                m = jnp.maximum(jnp.max(st, axis=0, keepdims=True), sink2)
                pt = jnp.exp2(st - m)
                den = jnp.sum(pt, axis=0, keepdims=True) + jnp.exp2(sink2 - m)
                o = lax.dot_general((pt * (1.0 / den)).astype(MXU_DTYPE), vx[ks, :], TN, preferred_element_type=F32)
                o32_ref[rs, cs] = o
                o16_ref[rs, cs] = o.astype(o16_ref.dtype)
                lse_ref[hh, :, rs] = m + jnp.log2(den)

    q, cur, prev, sink, lse_spec = _swa_in_specs(False, nsb)
    return pl.pallas_call(
        body, name="swa_fwd", grid=(SWA_KV_HEADS, nsb), in_specs=[q, cur, prev, cur, prev, sink],
        out_specs=[q, q, lse_spec],
        out_shape=[jax.ShapeDtypeStruct((s_, SWA_HEADS * LANES), F32), jax.ShapeDtypeStruct((s_, SWA_HEADS * LANES), MXU_DTYPE),
                   jax.ShapeDtypeStruct((SWA_HEADS, 1, s_), F32)],
        scratch_shapes=[pltpu.VMEM((5 * BLOCK, LANES), MXU_DTYPE), pltpu.VMEM((5 * BLOCK, LANES), MXU_DTYPE)],
        compiler_params=_cparams(("parallel", "arbitrary")),
    )(qa, ka, ka, va, va, sink_b)


def _swa_bwd(qa, ka, va, sink_b, o32, do, lse):
    s_ = qa.shape[0]
    nsb = s_ // SWA_T
    scale = HEAD_DIM ** -0.5
    c2 = scale * LOG2E

    def body(q_ref, kc_ref, kp_ref, vc_ref, vp_ref, sk_ref, o_ref, do_ref, lse_ref,
             dq_ref, dk_ref, dv_ref, dsk_ref, kx, vx, kacc, vacc, kcar, vcar):
        j = pl.program_id(1)
        kx[0:BLOCK, :] = kp_ref[...]
        kx[BLOCK:5 * BLOCK, :] = kc_ref[...]
        vx[0:BLOCK, :] = vp_ref[...]
        vx[BLOCK:5 * BLOCK, :] = vc_ref[...]
        band, band0 = _swa_masks(nsb - 1 - j)
        kacc[...] = jnp.zeros(kacc.shape, F32)
        vacc[...] = jnp.zeros(vacc.shape, F32)

        @pl.when(j == 0)
        def _():
            kcar[...] = jnp.zeros(kcar.shape, F32)
            vcar[...] = jnp.zeros(vcar.shape, F32)
            dsk_ref[...] = jnp.zeros(dsk_ref.shape, F32)

        for hh in range(SWA_GROUP):
            sink2 = sk_ref[0, hh:hh + 1, 0:1] * LOG2E
            cs = slice(hh * LANES, (hh + 1) * LANES)
            dsink = jnp.zeros((1, 1), F32)
            for b in range(4):
                rs = slice(b * BLOCK, (b + 1) * BLOCK)
                ks = slice(b * BLOCK, (b + 2) * BLOCK)
                q, k2, v2 = q_ref[rs, cs], kx[ks, :], vx[ks, :]
                d = do_ref[rs, cs]
                delta = jnp.sum((d * o_ref[rs, cs]).T, axis=0, keepdims=True)
                l2 = lse_ref[hh, :, rs]
                st = lax.dot_general(k2, q, NT, preferred_element_type=F32) * c2
                pt = jnp.exp2(jnp.where(band0 if b == 0 else band, st, -jnp.inf) - l2)
                db = d.astype(MXU_DTYPE)
                dst = (pt * (lax.dot_general(v2, db, NT, preferred_element_type=F32) - delta) * scale).astype(MXU_DTYPE)
                dq_ref[rs, cs] = lax.dot_general(dst, k2, TN, preferred_element_type=F32)
                kacc[ks, :] += jnp.dot(dst, q, preferred_element_type=F32)
                vacc[ks, :] += jnp.dot(pt.astype(MXU_DTYPE), db, preferred_element_type=F32)
                dsink = dsink - jnp.sum(jnp.exp2(sink2 - l2) * delta, axis=1, keepdims=True)
            dsk_ref[0, hh:hh + 1, :] += jnp.broadcast_to(dsink, (1, LANES))

        dk_ref[0:3 * BLOCK, :] = kacc[BLOCK:4 * BLOCK, :]
        dk_ref[3 * BLOCK:4 * BLOCK, :] = kacc[4 * BLOCK:5 * BLOCK, :] + kcar[...]
        dv_ref[0:3 * BLOCK, :] = vacc[BLOCK:4 * BLOCK, :].astype(dv_ref.dtype)
        dv_ref[3 * BLOCK:4 * BLOCK, :] = (vacc[4 * BLOCK:5 * BLOCK, :] + vcar[...]).astype(dv_ref.dtype)
        kcar[...] = kacc[0:BLOCK, :]
        vcar[...] = vacc[0:BLOCK, :]

    q, cur, prev, sink, lse_spec = _swa_in_specs(True, nsb)
    return pl.pallas_call(
        body, name="swa_bwd", grid=(SWA_KV_HEADS, nsb),
        in_specs=[q, cur, prev, cur, prev, sink, q, q, lse_spec],
        out_specs=[q, cur, cur, sink],
        out_shape=[jax.ShapeDtypeStruct((s_, SWA_HEADS * LANES), F32), jax.ShapeDtypeStruct((s_, SWA_KV_HEADS * LANES), F32),
                   jax.ShapeDtypeStruct((s_, SWA_KV_HEADS * LANES), MXU_DTYPE),
                   jax.ShapeDtypeStruct((SWA_KV_HEADS, SUBLANES, LANES), F32)],
        scratch_shapes=[pltpu.VMEM((5 * BLOCK, LANES), MXU_DTYPE), pltpu.VMEM((5 * BLOCK, LANES), MXU_DTYPE),
                        pltpu.VMEM((5 * BLOCK, LANES), F32), pltpu.VMEM((5 * BLOCK, LANES), F32),
                        pltpu.VMEM((BLOCK, LANES), F32), pltpu.VMEM((BLOCK, LANES), F32)],
        compiler_params=_cparams(("arbitrary", "arbitrary")),
    )(qa, ka, ka, va, va, sink_b, o32, do, lse)


MLA_T = 512


def _mla_specs(s_, t):
    qs = pl.BlockSpec((t, LANES), lambda h, i: (i, h))
    kv = pl.BlockSpec((s_, LANES), lambda h, i: (0, h))
    row = pl.BlockSpec((1, 1, t), lambda h, i: (h, 0, i))
    return qs, kv, row


def _causal_scores_t(k, q, t, c2, masked):
    st = lax.dot_general(k, q, NT, preferred_element_type=F32) * c2
    if masked:
        kr = lax.broadcasted_iota(jnp.int32, (t, t), 0)
        qc = lax.broadcasted_iota(jnp.int32, (t, t), 1)
        st = jnp.where(kr <= qc, st, -jnp.inf)
    return st


def _mla_fwd(qc, kc, vp):
    s_ = qc.shape[0]
    t = min(MLA_T, s_)
    c2 = MLA_QK ** -0.5 * LOG2E

    def body(q_ref, k_ref, v_ref, o32_ref, o16_ref, lse_ref, m_s, l_s, acc_s):
        qi = pl.program_id(1)
        q = q_ref[...]
        m_s[...] = jnp.full(m_s.shape, -jnp.inf, F32)
        l_s[...] = jnp.zeros(l_s.shape, F32)
        acc_s[...] = jnp.zeros(acc_s.shape, F32)

        def step(ki, masked):
            off = pl.multiple_of(ki * t, t)
            st = _causal_scores_t(k_ref[pl.ds(off, t), :], q, t, c2, masked)
            m_old = m_s[...]
            m_new = jnp.maximum(m_old, jnp.max(st, axis=0, keepdims=True))
            alpha = jnp.exp2(m_old - m_new)
            pt = jnp.exp2(st - m_new)
            l_s[...] = alpha * l_s[...] + jnp.sum(pt, axis=0, keepdims=True)
            acc_s[...] = alpha * acc_s[...] + lax.dot_general(
                v_ref[pl.ds(off, t), :], pt.astype(MXU_DTYPE), TN, preferred_element_type=F32)
            m_s[...] = m_new

        def full_block(ki, carry):
            step(ki, False)
            return carry

        lax.fori_loop(0, qi, full_block, 0)
        step(qi, True)
        o = (acc_s[...] * (1.0 / l_s[...])).T
        o32_ref[...] = o
        o16_ref[...] = o.astype(o16_ref.dtype)
        lse_ref[0] = m_s[...] + jnp.log2(l_s[...])

    qs, kv, row = _mla_specs(s_, t)
    return pl.pallas_call(
        body, name="mla_fwd", grid=(MLA_HEADS, s_ // t), in_specs=[qs, kv, kv], out_specs=[qs, qs, row],
        out_shape=[jax.ShapeDtypeStruct((s_, MLA_HEADS * LANES), F32), jax.ShapeDtypeStruct((s_, MLA_HEADS * LANES), MXU_DTYPE),
                   jax.ShapeDtypeStruct((MLA_HEADS, 1, s_), F32)],
        scratch_shapes=[pltpu.VMEM((1, t), F32), pltpu.VMEM((1, t), F32), pltpu.VMEM((LANES, t), F32)],
        compiler_params=_cparams(("parallel", "arbitrary")),
    )(qc, kc, vp)


def _mla_bwd(qc, kc, vp, dob, lse, delta):
    s_ = qc.shape[0]
    t = min(MLA_T, s_)
    scale = MLA_QK ** -0.5
    c2 = scale * LOG2E

    def body(q_ref, do_ref, lse_ref, dl_ref, k_ref, v_ref, dq_ref, dk_ref, dv_ref, dqt_s):
        qi = pl.program_id(1)

        @pl.when(qi == 0)
        def _():
            dk_ref[...] = jnp.zeros(dk_ref.shape, F32)
            dv_ref[...] = jnp.zeros(dv_ref.shape, F32)

        q, d, l2, dl = q_ref[...], do_ref[...], lse_ref[0], dl_ref[0]
        dqt_s[...] = jnp.zeros(dqt_s.shape, F32)

        def step(ki, masked):
            off = pl.multiple_of(ki * t, t)
            k = k_ref[pl.ds(off, t), :]
            pt = jnp.exp2(_causal_scores_t(k, q, t, c2, masked) - l2)
            dpt = lax.dot_general(v_ref[pl.ds(off, t), :], d, NT, preferred_element_type=F32)
            dst = (pt * (dpt - dl) * scale).astype(MXU_DTYPE)
            dv_ref[pl.ds(off, t), :] += jnp.dot(pt.astype(MXU_DTYPE), d, preferred_element_type=F32)
            dk_ref[pl.ds(off, t), :] += jnp.dot(dst, q, preferred_element_type=F32)
            dqt_s[...] += lax.dot_general(k, dst, TN, preferred_element_type=F32)

        def full_block(ki, carry):
            step(ki, False)
            return carry

        lax.fori_loop(0, qi, full_block, 0)
        step(qi, True)
        dq_ref[...] = dqt_s[...].T

    qs, kv, row = _mla_specs(s_, t)
    shp = jax.ShapeDtypeStruct((s_, MLA_HEADS * LANES), F32)
    return pl.pallas_call(
        body, name="mla_bwd", grid=(MLA_HEADS, s_ // t), in_specs=[qs, qs, row, row, kv, kv], out_specs=[qs, kv, kv],
        out_shape=[shp, shp, shp], scratch_shapes=[pltpu.VMEM((LANES, t), F32)],
        compiler_params=_cparams(("parallel", "arbitrary")),
    )(qc, dob, lse, delta, kc, vp)


def _pad_heads(w, nh, hd, axis):
    shp = w.shape
    w = w.reshape(shp[:axis] + (nh, hd) + shp[axis + 1:])
    pad = [(0, 0)] * w.ndim
    pad[axis + 1] = (0, LANES - hd)
    w = jnp.pad(w, pad)
    return w.reshape(shp[:axis] + (nh * LANES,) + shp[axis + 1:])


def _unpad_heads(w, nh, hd, axis):
    shp = w.shape
    w = w.reshape(shp[:axis] + (nh, LANES) + shp[axis + 1:])
    w = lax.slice_in_dim(w, 0, hd, axis=axis + 1)
    return w.reshape(shp[:axis] + (nh * hd,) + shp[axis + 1:])


PACK_W = 1024
ROW_TILE = 16
FULL_SHAPE = dict(w_in=(1024, 3488), w_uq=(384, 768), w_ukv=(256, 1024), w_o_swa=(512, 1024), w_o_mla=(512, 1024),
                  w_out=(1024, 1024), w_gate=(1024, 2816), w_up=(1024, 2816), w_down=(2816, 1024))
BIG = tuple(FULL_SHAPE)
ROW_SHARDED = ("w_out", "w_down")
W_IN_COLS = FULL_SHAPE["w_in"][1] // N_DEV
W_IN_ROWS = -(-W_IN_COLS // ROW_TILE) * ROW_TILE
FF_COLS = D_FF // N_DEV
WIRE = (("w_in", 0, W_IN_ROWS), ("w_gate", 448, FF_COLS), ("w_up", 800, FF_COLS), ("w_down", 1152, FF_COLS),
        ("w_out", 1504, 128), ("small", 1632, 208))
SMALL_FLAT = (("w_uq", 0, 36), ("w_ukv", 48, 32), ("w_o_swa", 80, 64), ("w_o_mla", 144, 64))
PACK_ROWS = 1840
SMALL_ROW0 = 1632


def _shard_shape(n):
    r, c = FULL_SHAPE[n]
    return (r // N_DEV, c) if n in ROW_SHARDED else (r, c // N_DEV)


def _wire_pack(sh, dtype):
    c = lambda n: sh[n].astype(dtype)
    rows = [jnp.pad(c("w_in").T, ((0, W_IN_ROWS - W_IN_COLS), (0, 0))), c("w_gate").T, c("w_up").T, c("w_down"), c("w_out")]
    for n, _, r in SMALL_FLAT:
        rows.append(jnp.pad(c(n).reshape(r, PACK_W), ((0, -r % ROW_TILE), (0, 0))))
    return jnp.concatenate(rows, 0)


def _wire_unpack(p):
    out = dict(w_in=p[0:W_IN_COLS].T, w_gate=p[448:800].T, w_up=p[800:1152].T, w_down=p[1152:1504], w_out=p[1504:1632])
    for n, off, r in SMALL_FLAT:
        out[n] = p[SMALL_ROW0 + off:SMALL_ROW0 + off + r].reshape(_shard_shape(n))
    return out


def _w_in_row_maps():
    sp = lambda col: (col // W_IN_COLS) * W_IN_ROWS + col % W_IN_COLS
    fwd = np.full((P_W,), -1, np.int64)

    def put(t0, c0, n):
        fwd[t0:t0 + n] = [sp(c) for c in range(c0, c0 + n)]

    put(P_GA, IN_OFF[6], D_MODEL)
    put(P_GB, IN_OFF[7], D_MODEL)
    for h in range(SWA_HEADS):
        put(P_Q + LANES * h, IN_OFF[0] + HEAD_DIM * h, HEAD_DIM)
    put(P_QLAT, IN_OFF[3], Q_LORA)
    put(P_KR + KR_LANE, IN_OFF[5], MLA_ROPE)
    for h in range(SWA_KV_HEADS):
        put(P_K + LANES * h, IN_OFF[1] + HEAD_DIM * h, HEAD_DIM)
        put(P_V + LANES * h, IN_OFF[2] + HEAD_DIM * h, HEAD_DIM)
    put(P_KVLAT, IN_OFF[4], KV_LORA)
    inv = np.full((N_DEV * W_IN_ROWS,), -1, np.int64)
    inv[fwd[fwd >= 0]] = np.nonzero(fwd >= 0)[0]
    return fwd, inv


def _take_rows(src, idx):
    parts, i, n = [], 0, len(idx)
    while i < n:
        j = i + 1
        if idx[i] < 0:
            while j < n and idx[j] < 0:
                j += 1
            parts.append(jnp.zeros((j - i, src.shape[1]), src.dtype))
        else:
            while j < n and idx[j] == idx[j - 1] + 1:
                j += 1
            parts.append(src[int(idx[i]):int(idx[i]) + (j - i)])
        i = j
    return jnp.concatenate(parts, 0)


def _to_operands(win_g, wgu_g, wd_g, wout_g, small_g):
    def full(n, off, r):
        a = small_g[:, off:off + r].reshape((N_DEV,) + _shard_shape(n))
        return jnp.moveaxis(a, 0, 1).reshape(FULL_SHAPE[n])

    w = {n: full(n, off, r) for n, off, r in SMALL_FLAT}
    ukv = w["w_ukv"].reshape(KV_LORA, MLA_HEADS, MLA_NOPE + MLA_V)
    return dict(
        winT=_take_rows(win_g.reshape(N_DEV * W_IN_ROWS, PACK_W), _w_in_row_maps()[0]),
        wguT=wgu_g.reshape(2 * D_FF, D_MODEL),
        wd=wd_g.reshape(D_FF, D_MODEL),
        wout=wout_g.reshape(D_MODEL, D_MODEL),
        wuq=_pad_heads(w["w_uq"], MLA_HEADS, MLA_QK, 1),
        wuk=_pad_heads(ukv[:, :, :MLA_NOPE].reshape(KV_LORA, -1), MLA_HEADS, MLA_NOPE, 1),
        wuv=_pad_heads(ukv[:, :, MLA_NOPE:].reshape(KV_LORA, -1), MLA_HEADS, MLA_V, 1),
        woa=_pad_heads(w["w_o_swa"], SWA_HEADS, HEAD_DIM, 0),
        wob=_pad_heads(w["w_o_mla"], MLA_HEADS, MLA_V, 0),
    )


def _grad_pieces(g):
    uk = _unpad_heads(g["wukv"][:, :1024], MLA_HEADS, MLA_NOPE, 1).reshape(KV_LORA, MLA_HEADS, MLA_NOPE)
    uv = _unpad_heads(g["wukv"][:, 1024:], MLA_HEADS, MLA_V, 1).reshape(KV_LORA, MLA_HEADS, MLA_V)
    w = dict(w_uq=_unpad_heads(g["wuq"], MLA_HEADS, MLA_QK, 1), w_ukv=jnp.concatenate([uk, uv], 2).reshape(KV_LORA, -1),
             w_o_swa=_unpad_heads(g["woa"], SWA_HEADS, HEAD_DIM, 0), w_o_mla=_unpad_heads(g["wob"], MLA_HEADS, MLA_V, 0))

    def flat(n, r):
        rr, cc = FULL_SHAPE[n]
        a = jnp.moveaxis(w[n].reshape(rr, N_DEV, cc // N_DEV), 1, 0).reshape(N_DEV, r, PACK_W)
        return jnp.pad(a, ((0, 0), (0, -r % ROW_TILE), (0, 0))).astype(WIRE_DTYPE)

    return [_take_rows(g["winT"], _w_in_row_maps()[1]).reshape(N_DEV, W_IN_ROWS, PACK_W),
            g["wguT"].reshape(2, N_DEV, FF_COLS, PACK_W), g["wd"].reshape(N_DEV, FF_COLS, PACK_W),
            g["wout"].reshape(N_DEV, D_MODEL // N_DEV, PACK_W),
            jnp.concatenate([flat(n, r) for n, _, r in SMALL_FLAT], 1)]


def _local_step(x, tgt, ops, small):
    s_ = x.shape[0]
    tabs = _rope_tables(s_)
    sink_b = jnp.broadcast_to(small["swa_sinks"].reshape(SWA_KV_HEADS, SWA_GROUP, 1), (SWA_KV_HEADS, SWA_GROUP, LANES))
    sink_b = jnp.pad(sink_b, ((0, 0), (0, SUBLANES - SWA_GROUP), (0, 0)))

    h = _norm_fwd(x, small["mix_norm_g"], name="norm1")
    p = _mm(h, ops["winT"], "nt", name="proj_in", tn=2176)
    qa, ka, va, cq, ckv, kro = _attn_prep(p, small["q_norm_g"], small["kv_norm_g"], tabs)
    oa32, oa16, lse_a = _swa_fwd(qa, ka, va, sink_b)
    qp = _mm(cq, ops["wuq"], "nn", name="mla_q_up", tn=1024)
    kp = _mm(ckv, ops["wuk"], "nn", name="mla_k_up", tn=1024)
    vp = _mm(ckv, ops["wuv"], "nn", name="mla_v_up", tn=1024, out_dtype=MXU_DTYPE)
    qc, kc = _mla_prep(qp, kp, kro, tabs)
    ob32, ob16, lse_b = _mla_fwd(qc, kc, vp)
    ta = _mm(oa16, ops["woa"], "nn", name="o_swa", tn=1024)
    tb = _mm(ob16, ops["wob"], "nn", name="o_mla", tn=1024)
    y = _gate_fwd(p, ta, tb)
    x1 = _mm(y, ops["wout"], "nn", name="out_proj", add=x, tn=1024)
    h2 = _norm_fwd(x1, small["ffn_norm_g"], name="norm2")
    gu = _mm(h2, ops["wguT"], "nt", name="ffn_in", tn=512)
    act = _swiglu_fwd(gu)
    x2 = _mm(act, ops["wd"], "nn", name="ffn_out", add=x1, tn=1024)

    dx2, dx2b, dg3, _, tot = _loss_bwd(x2, small["final_norm_g"].reshape(1, D_MODEL), tgt)
    g = {}
    dact = _mm(dx2b, ops["wd"], "nt", name="d_act", tn=1408)
    g["wd"] = _mm(act, dx2b, "tn", name="dw_down", tm=1408, tn=1024, tk=512, out_dtype=WIRE_DTYPE)
    dgu = _swiglu_bwd(gu, dact)
    dh2 = _mm(dgu, ops["wguT"], "nn", name="d_h2", tn=1024, tk=2816)
    g["wguT"] = _mm(dgu, h2, "tn", name="dw_ffn_in", tm=512, tn=1024, tk=512, out_dtype=WIRE_DTYPE)
    dx1, dx1b, dg2 = _norm_bwd(x1, small["ffn_norm_g"], dh2, dx2, name="norm2_bwd")
    dy = _mm(dx1b, ops["wout"], "nt", name="d_y", tn=1024)
    g["wout"] = _mm(y, dx1b, "tn", name="dw_out", tm=1024, tn=1024, tk=512, out_dtype=WIRE_DTYPE)
    dta, dtb, dgab = _gate_bwd(p, ta, tb, dy)
    doa = _mm(dta, ops["woa"], "nt", name="d_oa", tn=1024)
    g["woa"] = _mm(oa16, dta, "tn", name="dw_o_swa", tm=1024, tn=1024, tk=512)
    dob = _mm(dtb, ops["wob"], "nt", name="d_ob", tn=1024)
    g["wob"] = _mm(ob16, dtb, "tn", name="dw_o_mla", tm=1024, tn=1024, tk=512)
    dob16, delta_b = _mla_bwd_prep(dob, ob32)
    dqc, dkc, dvp = _mla_bwd(qc, kc, vp, dob16, lse_b, delta_b)
    dqp, dkv, dkr = _mla_unprep(dqc, dkc, dvp, tabs)
    dcq = _mm(dqp, ops["wuq"], "nt", name="d_cq", tn=Q_LORA)
    g["wuq"] = _mm(cq, dqp, "tn", name="dw_uq", tm=Q_LORA, tn=1024, tk=512)
    dckv = _mm(dkv, jnp.concatenate([ops["wuk"], ops["wuv"]], 1), "nt", name="d_ckv", tn=KV_LORA)
    g["wukv"] = _mm(ckv, dkv, "tn", name="dw_ukv", tm=KV_LORA, tn=1024, tk=512)
    _, dqlat, dgq = _norm_bwd(p, small["q_norm_g"], dcq, None, name="qnorm_bwd", x_cb=P_QLAT // Q_LORA)
    _, dkvlat, dgkv = _norm_bwd(p, small["kv_norm_g"], dckv, None, name="kvnorm_bwd", x_cb=P_KVLAT // KV_LORA)
    dqa, dka, dva, dsk = _swa_bwd(qa, ka, va, sink_b, oa32, doa, lse_a)
    dq_raw, dk_raw = _swa_unrope(dqa, dka, tabs)
    dp = jnp.concatenate([dgab, dq_raw, dqlat, dkr, dk_raw, dva, dkvlat], 1)
    dh = _mm(dp, ops["winT"], "nn", name="d_h", tn=1024, tk=2176)
    g["winT"] = _mm(dp, h, "tn", name="dw_in", tm=2176, tn=1024, tk=512, out_dtype=WIRE_DTYPE)
    gx, _, dg1 = _norm_bwd(x, small["mix_norm_g"], dh, dx1, name="norm1_bwd")

    sm = dict(mix_norm_g=dg1, ffn_norm_g=dg2, final_norm_g=dg3, q_norm_g=dgq, kv_norm_g=dgkv,
              swa_sinks=dsk[:, :SWA_GROUP, 0].reshape(1, SWA_HEADS))
    return tot[0, 0], gx, g, sm


MESH = pl.DeviceIdType.MESH
ANY = pl.BlockSpec(memory_space=pl.ANY)


def _position():
    return lax.axis_index("x"), lax.axis_index("y"), lax.axis_index("c")


WEIGHT_PIECES = ((0, lambda d: (d,), 0, W_IN_ROWS), (1, lambda d: (0, d), 448, FF_COLS), (1, lambda d: (1, d), 800, FF_COLS),
                 (2, lambda d: (d,), 1152, FF_COLS), (3, lambda d: (d,), 1504, 128), (4, lambda d: (d,), 1632, 208))
WEIGHT_PIECE_SHAPES = ((N_DEV, W_IN_ROWS, PACK_W), (2, N_DEV, FF_COLS, PACK_W), (N_DEV, FF_COLS, PACK_W),
                       (N_DEV, 128, PACK_W), (N_DEV, 208, PACK_W))


def _all_gather(block, pieces, shapes, *, name):
    n_out = len(shapes)

    def body(x_ref, *refs):
        outs, (send_sems, recv_sems, local_sem) = refs[:n_out], refs[n_out:]
        x, y, c = _position()
        me, sibling = (x, y, c), (x, y, 1 - c)
        chips = [(1 - x, y), (x, 1 - y), (1 - x, 1 - y)]

        def dst(piece, blk):
            arr, lead, _, _ = piece
            return outs[arr].at[lead(4 * blk[0] + 2 * blk[1] + blk[2])]

        def own(piece):
            return x_ref.at[pl.ds(piece[2], piece[3])]

        def copies(k, blk, to, from_input):
            return [pltpu.make_async_remote_copy(
                src_ref=own(p) if from_input else dst(p, blk), dst_ref=dst(p, blk), send_sem=send_sems.at[k],
                recv_sem=recv_sems.at[k], device_id=to, device_id_type=MESH) for p in pieces]

        def whole_block(k):
            return pltpu.make_async_remote_copy(src_ref=x_ref, dst_ref=x_ref, send_sem=send_sems.at[k],
                                                recv_sem=recv_sems.at[k], device_id=me, device_id_type=MESH)

        for p in pieces:
            pltpu.make_async_copy(own(p), dst(p, me), local_sem).start()
        for cp in copies(0, me, sibling, True):
            cp.start()
        for j, chip in enumerate(chips):
            for cp in copies(1 + j, me, (*chip, c), True):
                cp.start()
        for j, chip in enumerate(chips):
            whole_block(1 + j).wait_recv()
            for cp in copies(4 + j, (*chip, c), sibling, False):
                cp.start()
        whole_block(0).wait_recv()
        for j in range(3):
            whole_block(4 + j).wait_recv()
        for k in range(7):
            whole_block(k).wait_send()
        pltpu.make_async_copy(x_ref, x_ref, local_sem).wait()

    return pl.pallas_call(
        body, name=name, out_shape=[jax.ShapeDtypeStruct(s, block.dtype) for s in shapes], in_specs=[ANY],
        out_specs=[ANY] * n_out,
        scratch_shapes=[pltpu.SemaphoreType.DMA((7,)), pltpu.SemaphoreType.DMA((7,)), pltpu.SemaphoreType.DMA],
    )(block)


def _exchange_sibling(parts, pieces, *, name):
    n_in = len(parts)
    shp = jax.ShapeDtypeStruct((4, PACK_ROWS, PACK_W), parts[0].dtype)

    def body(*refs):
        ins, (mine_ref, land_ref, send_sem, recv_sem, local_sem) = refs[:n_in], refs[n_in:]
        x, y, c = _position()
        for arr, lead, off, rows in pieces:
            for j in range(4):
                pltpu.make_async_copy(ins[arr].at[lead(2 * j + c)], mine_ref.at[j, pl.ds(off, rows)], local_sem).start()
                pltpu.make_async_remote_copy(
                    src_ref=ins[arr].at[lead(2 * j + (1 - c))], dst_ref=land_ref.at[j, pl.ds(off, rows)], send_sem=send_sem,
                    recv_sem=recv_sem, device_id=(x, y, 1 - c), device_id_type=MESH).start()
        everything = pltpu.make_async_remote_copy(src_ref=mine_ref, dst_ref=land_ref, send_sem=send_sem, recv_sem=recv_sem,
                                                  device_id=(x, y, 1 - c), device_id_type=MESH)
        everything.wait_recv()
        everything.wait_send()
        pltpu.make_async_copy(mine_ref, mine_ref, local_sem).wait()

    return pl.pallas_call(
        body, name=name, out_shape=[shp, shp], in_specs=[ANY] * n_in, out_specs=[ANY, ANY],
        scratch_shapes=[pltpu.SemaphoreType.DMA, pltpu.SemaphoreType.DMA, pltpu.SemaphoreType.DMA],
    )(*parts)


def _exchange_chips(p, *, name):
    _, r, c_ = p.shape

    def body(p_ref, land_ref, send_sems, recv_sems, local_sem):
        x, y, c = _position()
        mine = 2 * x + y
        own = pltpu.make_async_copy(p_ref.at[mine], land_ref.at[mine], local_sem)
        own.start()
        copies = []
        for k, (px, py) in enumerate([(1 - x, y), (x, 1 - y), (1 - x, 1 - y)]):
            copies.append(pltpu.make_async_remote_copy(
                src_ref=p_ref.at[2 * px + py], dst_ref=land_ref.at[mine], send_sem=send_sems.at[k], recv_sem=recv_sems.at[k],
                device_id=(px, py, c), device_id_type=MESH))
        for cp in copies:
            cp.start()
        for cp in copies:
            cp.wait_recv()
        for cp in copies:
            cp.wait_send()
        own.wait()

    return pl.pallas_call(
        body, name=name, out_shape=jax.ShapeDtypeStruct((4, r, c_), p.dtype), in_specs=[ANY], out_specs=ANY,
        scratch_shapes=[pltpu.SemaphoreType.DMA((3,)), pltpu.SemaphoreType.DMA((3,)), pltpu.SemaphoreType.DMA],
    )(p)


def _pair_add(mine, theirs, *, tr=368):
    _, r, c_ = mine.shape

    def body(a_ref, b_ref, o_ref):
        o_ref[...] = (a_ref[...].astype(F32) + b_ref[...].astype(F32)).astype(o_ref.dtype)

    spec = pl.BlockSpec((1, tr, c_), lambda j, i: (j, i, 0))
    return pl.pallas_call(
        body, name="rs_pair_add", grid=(4, r // tr), in_specs=[spec, spec], out_specs=spec,
        out_shape=jax.ShapeDtypeStruct(mine.shape, mine.dtype), compiler_params=_cparams(("parallel", "parallel")),
    )(mine, theirs)


def _adamw(w, g, m, v):
    m = ADAM_B1 * m + (1.0 - ADAM_B1) * g
    v = ADAM_B2 * v + (1.0 - ADAM_B2) * (g * g)
    m_hat = m / (1.0 - ADAM_B1 ** ADAM_STEP)
    v_hat = v / (1.0 - ADAM_B2 ** ADAM_STEP)
    delta = -ADAM_LR * (m_hat / (jnp.sqrt(v_hat) + ADAM_EPS) + ADAM_WD * w)
    return delta, m, v


def _chip_sum(land, *, tr=368):
    _, r, c_ = land.shape

    def body(l_ref, g_ref):
        g = l_ref[0].astype(F32)
        for j in range(1, 4):
            g = g + l_ref[j].astype(F32)
        g_ref[...] = g

    return pl.pallas_call(
        body, name="rs_chip_sum", grid=(r // tr,), in_specs=[pl.BlockSpec((4, tr, c_), lambda i: (0, i, 0))],
        out_specs=pl.BlockSpec((tr, c_), lambda i: (i, 0)), out_shape=jax.ShapeDtypeStruct((r, c_), F32),
        compiler_params=_cparams(("parallel",)),
    )(land)


def _adamw_call(w, g, m, v, *, name, max_rows=256):
    r, c_ = w.shape
    tr = max_rows if r > max_rows and r % max_rows == 0 else r

    def body(w_ref, g_ref, m_ref, v_ref, d_ref, mo_ref, vo_ref):
        d, mn, vn = _adamw(w_ref[...], g_ref[...], m_ref[...], v_ref[...])
        d_ref[...] = d
        mo_ref[...] = mn
        vo_ref[...] = vn

    row = pl.BlockSpec((tr, c_), lambda i: (i, 0))
    shp = jax.ShapeDtypeStruct((r, c_), F32)
    return pl.pallas_call(
        body, name=name, grid=(r // tr,), in_specs=[row] * 4, out_specs=[row] * 3, out_shape=[shp] * 3,
        compiler_params=_cparams(("parallel",)),
    )(w, g, m, v)


SMALL = ("mix_norm_g", "ffn_norm_g", "final_norm_g", "q_norm_g", "kv_norm_g", "swa_sinks")
SMALL_W = dict(mix_norm_g=1024, ffn_norm_g=1024, final_norm_g=1024, q_norm_g=Q_LORA, kv_norm_g=KV_LORA, swa_sinks=SWA_HEADS)


def _small_adamw(parts, w, m, v):
    n_par = len(SMALL)

    def body(p_ref, w_ref, m_ref, v_ref, g_ref, d_ref, mo_ref, vo_ref):
        tot = p_ref[0]
        for dev in range(1, N_DEV):
            tot = tot + p_ref[dev]
        row_id = lax.broadcasted_iota(jnp.int32, (SUBLANES, PACK_W), 0)
        g = jnp.zeros((SUBLANES, PACK_W), F32)
        for k in range(n_par):
            g = jnp.where(row_id == k, jnp.sum(tot[k * SUBLANES:(k + 1) * SUBLANES, :], axis=0, keepdims=True), g)
        d, mn, vn = _adamw(w_ref[...], g, m_ref[...], v_ref[...])
        g_ref[...] = g
        d_ref[...] = d
        mo_ref[...] = mn
        vo_ref[...] = vn

    shp = jax.ShapeDtypeStruct((SUBLANES, PACK_W), F32)
    vm = pl.BlockSpec(memory_space=pltpu.VMEM)
    return pl.pallas_call(body, name="small_adamw", in_specs=[vm] * 4, out_specs=[vm] * 4, out_shape=[shp] * 4)(parts, w, m, v)


def _small_pack(d, rows_each):
    parts = [jnp.pad(d[n].astype(F32), ((0, 0), (0, PACK_W - SMALL_W[n]))) for n in SMALL]
    out = jnp.concatenate(parts, 0)
    pad = -out.shape[0] % SUBLANES
    return jnp.pad(out, ((0, pad), (0, 0)))


def kernel(x, mix_norm_g, w_in, swa_sinks, q_norm_g, w_uq, kv_norm_g, w_ukv, w_o_swa, w_o_mla, w_out, ffn_norm_g, w_gate, w_up, w_down, final_norm_g, loss_target, m_mix_norm_g, m_w_in, m_swa_sinks, m_q_norm_g, m_w_uq, m_kv_norm_g, m_w_ukv, m_w_o_swa, m_w_o_mla, m_w_out, m_ffn_norm_g, m_w_gate, m_w_up, m_w_down, m_final_norm_g, v_mix_norm_g, v_w_in, v_swa_sinks, v_q_norm_g, v_w_uq, v_kv_norm_g, v_w_ukv, v_w_o_swa, v_w_o_mla, v_w_out, v_ffn_norm_g, v_w_gate, v_w_up, v_w_down, v_final_norm_g):
    big_w = dict(w_in=w_in[0], w_uq=w_uq[0], w_ukv=w_ukv[0], w_o_swa=w_o_swa[0], w_o_mla=w_o_mla[0], w_out=w_out[0],
                 w_gate=w_gate[0], w_up=w_up[0], w_down=w_down[0])
    big_m = dict(w_in=m_w_in[0], w_uq=m_w_uq[0], w_ukv=m_w_ukv[0], w_o_swa=m_w_o_swa[0], w_o_mla=m_w_o_mla[0],
                 w_out=m_w_out[0], w_gate=m_w_gate[0], w_up=m_w_up[0], w_down=m_w_down[0])
    big_v = dict(w_in=v_w_in[0], w_uq=v_w_uq[0], w_ukv=v_w_ukv[0], w_o_swa=v_w_o_swa[0], w_o_mla=v_w_o_mla[0],
                 w_out=v_w_out[0], w_gate=v_w_gate[0], w_up=v_w_up[0], w_down=v_w_down[0])
    small_w = dict(mix_norm_g=mix_norm_g, ffn_norm_g=ffn_norm_g, final_norm_g=final_norm_g.reshape(1, D_MODEL),
                   q_norm_g=q_norm_g, kv_norm_g=kv_norm_g, swa_sinks=swa_sinks)
    small_m = dict(mix_norm_g=m_mix_norm_g, ffn_norm_g=m_ffn_norm_g, final_norm_g=m_final_norm_g.reshape(1, D_MODEL),
                   q_norm_g=m_q_norm_g, kv_norm_g=m_kv_norm_g, swa_sinks=m_swa_sinks)
    small_v = dict(mix_norm_g=v_mix_norm_g, ffn_norm_g=v_ffn_norm_g, final_norm_g=v_final_norm_g.reshape(1, D_MODEL),
                   q_norm_g=v_q_norm_g, kv_norm_g=v_kv_norm_g, swa_sinks=v_swa_sinks)

    gathered = _all_gather(_wire_pack(big_w, WIRE_DTYPE), WEIGHT_PIECES, WEIGHT_PIECE_SHAPES, name="ag_weights")
    ops = _to_operands(*gathered)

    loss_tot, gx, g_ops, g_small = _local_step(x[0], loss_target[0], ops, small_w)

    mine, theirs = _exchange_sibling(_grad_pieces(g_ops), WEIGHT_PIECES, name="rs_sibling")
    land = _exchange_chips(_pair_add(mine, theirs), name="rs_chips")
    gw = _wire_unpack(_chip_sum(land))
    dw, mw, vw = {}, {}, {}
    for n in BIG:
        dw[n], mw[n], vw[n] = _adamw_call(big_w[n], gw[n], big_m[n], big_v[n], name="adamw_" + n)

    small_rows = _small_pack(g_small_rows(g_small), SUBLANES)
    parts, = _all_gather(small_rows, ((0, lambda d: (d,), 0, small_rows.shape[0]),), ((N_DEV,) + small_rows.shape,),
                         name="ag_small")
    gs, ds, ms, vs = _small_adamw(parts, _small_pack(small_w, 1), _small_pack(small_m, 1), _small_pack(small_v, 1))

    def small_out(packed):
        out = {}
        for k, n in enumerate(SMALL):
            out[n] = packed[k:k + 1, :SMALL_W[n]]
        out["final_norm_g"] = out["final_norm_g"].reshape(D_MODEL)
        return out

    gs, ds, ms, vs = small_out(gs), small_out(ds), small_out(ms), small_out(vs)
    loss = lax.psum(loss_tot, AXES)

    order = ("mix_norm_g", "w_in", "swa_sinks", "q_norm_g", "w_uq", "kv_norm_g", "w_ukv", "w_o_swa", "w_o_mla", "w_out",
             "ffn_norm_g", "w_gate", "w_up", "w_down", "final_norm_g")

    def leaves(big, small):
        return [big[n][None] if n in big else small[n] for n in order]

    return (loss, gx[None], *leaves(gw, gs), *leaves(dw, ds), *leaves(mw, ms), *leaves(vw, vs))


def g_small_rows(g_small):
    out = dict(g_small)
    out["swa_sinks"] = jnp.pad(g_small["swa_sinks"], ((0, SUBLANES - 1), (0, 0)))
    return out
```

```python
import functools

import numpy as np
import jax
import jax.numpy as jnp
from jax import lax
from jax.experimental import pallas as pl
from jax.experimental.pallas import tpu as pltpu

F32 = jnp.float32
MXU_DTYPE = jnp.bfloat16
WIRE_DTYPE = jnp.bfloat16

D_MODEL = 1024
EPS = 1e-6
ROPE_THETA = 10000.0
BLOCK = 128
HEAD_DIM = 64
SWA_HEADS = 8
SWA_KV_HEADS = 2
SWA_GROUP = SWA_HEADS // SWA_KV_HEADS
MLA_HEADS = 8
MLA_NOPE = 64
MLA_ROPE = 32
MLA_V = 64
MLA_QK = MLA_NOPE + MLA_ROPE
Q_LORA = 384
KV_LORA = 256
D_FF = 2816
IN_SIZES = (512, 128, 128, Q_LORA, KV_LORA, MLA_ROPE, D_MODEL, D_MODEL)
IN_OFF = tuple(int(v) for v in np.cumsum((0,) + IN_SIZES))
ADAM_LR, ADAM_B1, ADAM_B2, ADAM_EPS, ADAM_WD, ADAM_STEP = 0.001, 0.9, 0.999, 1e-08, 0.01, 10

LANES = 128
SUBLANES = 8
VMEM_LIMIT = 48 * 1024 * 1024
N_DEV = 8
AXES = ("x", "y", "c")

P_GA, P_GB, P_Q, P_QLAT, P_KR, P_K, P_V, P_KVLAT, P_W = 0, 1024, 2048, 3072, 3456, 3584, 3840, 4096, 4352
KR_LANE = 64

LOG2E = 1.4426950408889634

NT = (((1,), (1,)), ((), ()))
NN = (((1,), (0,)), ((), ()))
TN = (((0,), (0,)), ((), ()))


def _cparams(sem):
    return pltpu.CompilerParams(dimension_semantics=sem, vmem_limit_bytes=VMEM_LIMIT)


def _mm(a, b, mode, *, name, out_dtype=F32, add=None, tm=512, tn=512, tk=None):
    if mode == "nn":
        (M, K), (K2, N) = a.shape, b.shape
    elif mode == "nt":
        (M, K), (N, K2) = a.shape, b.shape
    else:
        (K, M), (K2, N) = a.shape, b.shape
    assert K == K2, (a.shape, b.shape, mode)
    tk = K if tk is None else tk
    tm, tn = min(tm, M), min(tn, N)
    assert M % tm == 0 and N % tn == 0 and K % tk == 0, (M, N, K, tm, tn, tk)
    nk = K // tk
    dn = {"nn": NN, "nt": NT, "tn": TN}[mode]
    if mode == "tn":
        a_spec = pl.BlockSpec((tk, tm), lambda i, j, k: (k, i))
    else:
        a_spec = pl.BlockSpec((tm, tk), lambda i, j, k: (i, k))
    if mode == "nt":
        b_spec = pl.BlockSpec((tn, tk), lambda i, j, k: (j, k))
    else:
        b_spec = pl.BlockSpec((tk, tn), lambda i, j, k: (k, j))
    o_spec = pl.BlockSpec((tm, tn), lambda i, j, k: (i, j))
    has_add = add is not None

    def body(*refs):
        a_ref, b_ref = refs[0], refs[1]
        add_ref = refs[2] if has_add else None
        o_ref = refs[3] if has_add else refs[2]
        p = lax.dot_general(a_ref[...], b_ref[...], dn, preferred_element_type=F32)

        def finish(acc):
            if has_add:
                acc = acc + add_ref[...]
            o_ref[...] = acc.astype(o_ref.dtype)

        if nk == 1:
            finish(p)
        else:
            acc_ref = refs[-1]
            k = pl.program_id(2)

            @pl.when(k == 0)
            def _():
                acc_ref[...] = p

            @pl.when(k > 0)
            def _():
                acc_ref[...] += p

            @pl.when(k == nk - 1)
            def _():
                finish(acc_ref[...])

    ins = [a, b] + ([add] if has_add else [])
    in_specs = [a_spec, b_spec] + ([o_spec] if has_add else [])
    return pl.pallas_call(
        body, name=name, grid=(M // tm, N // tn, nk), in_specs=in_specs, out_specs=o_spec,
        out_shape=jax.ShapeDtypeStruct((M, N), out_dtype),
        scratch_shapes=[pltpu.VMEM((tm, tn), F32)] if nk > 1 else [],
        compiler_params=_cparams(("parallel", "parallel", "arbitrary")),
    )(*ins)


def _rows(ts, w, cb=0):
    return pl.BlockSpec((ts, w), lambda i: (i, cb))


def _const(r, w):
    return pl.BlockSpec((r, w), lambda i: (0, 0))


def _sublane_sum(v):
    ts, c = v.shape
    return jnp.sum(v.reshape(ts // SUBLANES, SUBLANES, c), axis=0)


def _sigmoid(v):
    return 1.0 / (1.0 + jnp.exp(-v))


def _rope(v, cos, s_up, s_dn, up, dn):
    return v * cos + pltpu.roll(v, up, 1) * s_up + pltpu.roll(v, dn, 1) * s_dn


def _rope_t(dv, cos, s_up, s_dn, up, dn):
    return dv * cos + pltpu.roll(dv * s_up, dn, 1) + pltpu.roll(dv * s_dn, up, 1)


def _rope_tables(seq):
    pos = np.arange(seq, dtype=np.float32)[:, None]

    def base(dim):
        inv = np.float32(ROPE_THETA) ** (-np.arange(0, dim, 2, dtype=np.float32) / np.float32(dim))
        ang = (pos * inv.astype(np.float32)[None, :]).astype(np.float32)
        return np.cos(ang).astype(np.float32), np.sin(ang).astype(np.float32)

    z = lambda n: np.zeros((seq, n), np.float32)
    ca, sa = base(HEAD_DIM)
    a_cos = np.concatenate([ca, ca, z(64)], 1)
    a_up = np.concatenate([-sa, z(96)], 1)
    a_dn = np.concatenate([z(32), sa, z(64)], 1)
    cb, sb = base(MLA_ROPE)
    one = np.ones((seq, 64), np.float32)
    q_cos = np.concatenate([one, cb, cb, z(32)], 1)
    k_cos = np.concatenate([z(64), cb, cb, z(32)], 1)
    b_up = np.concatenate([z(64), -sb, z(48)], 1)
    b_dn = np.concatenate([z(80), sb, z(32)], 1)
    return tuple(jnp.asarray(t) for t in (a_cos, a_up, a_dn, q_cos, k_cos, b_up, b_dn))


def _norm_fwd(x, g, *, name, ts=256):
    s_, c = x.shape

    def body(x_ref, g_ref, h_ref):
        v = x_ref[...]
        r = lax.rsqrt(jnp.mean(v * v, axis=-1, keepdims=True) + EPS)
        h_ref[...] = (v * r * g_ref[...]).astype(h_ref.dtype)

    return pl.pallas_call(
        body, name=name, grid=(s_ // ts,), in_specs=[_rows(ts, c), _const(1, c)], out_specs=_rows(ts, c),
        out_shape=jax.ShapeDtypeStruct((s_, c), MXU_DTYPE), compiler_params=_cparams(("parallel",)),
    )(x, g)


def _norm_bwd(x, g, dy, res, *, name, ts=256, x_cb=0, x_src_w=None):
    s_ = x.shape[0]
    c = dy.shape[1]
    has_res = res is not None

    def body(*refs):
        x_ref, g_ref, dy_ref = refs[0], refs[1], refs[2]
        res_ref = refs[3] if has_res else None
        dx_ref, dxb_ref, dg_ref = refs[-3], refs[-2], refs[-1]
        v = x_ref[...]
        r = lax.rsqrt(jnp.mean(v * v, axis=-1, keepdims=True) + EPS)
        xh = v * r
        d = dy_ref[...]
        dxh = d * g_ref[...]
        dx = r * (dxh - xh * jnp.mean(dxh * xh, axis=-1, keepdims=True))
        if has_res:
            dx = dx + res_ref[...]
        dx_ref[...] = dx
        dxb_ref[...] = dx.astype(dxb_ref.dtype)

        @pl.when(pl.program_id(0) == 0)
        def _():
            dg_ref[...] = jnp.zeros(dg_ref.shape, F32)

        dg_ref[...] += _sublane_sum(d * xh)

    ins = [x, g, dy] + ([res] if has_res else [])
    in_specs = [_rows(ts, c, x_cb), _const(1, c), _rows(ts, c)] + ([_rows(ts, c)] if has_res else [])
    return pl.pallas_call(
        body, name=name, grid=(s_ // ts,), in_specs=in_specs,
        out_specs=[_rows(ts, c), _rows(ts, c), _const(SUBLANES, c)],
        out_shape=[jax.ShapeDtypeStruct((s_, c), F32), jax.ShapeDtypeStruct((s_, c), MXU_DTYPE),
                   jax.ShapeDtypeStruct((SUBLANES, c), F32)],
        compiler_params=_cparams(("arbitrary",)),
    )(*ins)


def _attn_prep(p, gq, gkv, tabs, *, ts=256):
    s_ = p.shape[0]
    a_cos, a_up, a_dn, _, k_cos, b_up, b_dn = tabs

    def body(q_ref, k_ref, v_ref, ql_ref, kvl_ref, kr_ref, gq_ref, gkv_ref, ac, au, ad, kc, bu, bd,
             qa_ref, ka_ref, va_ref, cq_ref, ckv_ref, kro_ref):
        c_, u_, d_ = ac[...], au[...], ad[...]
        for h in range(SWA_HEADS):
            sl = slice(h * LANES, (h + 1) * LANES)
            qa_ref[:, sl] = _rope(q_ref[:, sl], c_, u_, d_, 96, 32).astype(qa_ref.dtype)
        for h in range(SWA_KV_HEADS):
            sl = slice(h * LANES, (h + 1) * LANES)
            ka_ref[:, sl] = _rope(k_ref[:, sl], c_, u_, d_, 96, 32).astype(ka_ref.dtype)
        va_ref[...] = v_ref[...].astype(va_ref.dtype)
        for src, gref, dst in ((ql_ref, gq_ref, cq_ref), (kvl_ref, gkv_ref, ckv_ref)):
            v = src[...]
            r = lax.rsqrt(jnp.mean(v * v, axis=-1, keepdims=True) + EPS)
            dst[...] = (v * r * gref[...]).astype(dst.dtype)
        kro_ref[...] = _rope(kr_ref[...], kc[...], bu[...], bd[...], 112, 16)

    tab = _rows(ts, LANES)
    return pl.pallas_call(
        body, name="attn_prep", grid=(s_ // ts,),
        in_specs=[_rows(ts, 1024, P_Q // 1024), _rows(ts, 256, P_K // 256), _rows(ts, 256, P_V // 256),
                  _rows(ts, Q_LORA, P_QLAT // Q_LORA), _rows(ts, KV_LORA, P_KVLAT // KV_LORA),
                  _rows(ts, LANES, P_KR // LANES), _const(1, Q_LORA), _const(1, KV_LORA), tab, tab, tab, tab, tab, tab],
        out_specs=[_rows(ts, 1024), _rows(ts, 256), _rows(ts, 256), _rows(ts, Q_LORA), _rows(ts, KV_LORA),
                   _rows(ts, LANES)],
        out_shape=[jax.ShapeDtypeStruct((s_, 1024), MXU_DTYPE), jax.ShapeDtypeStruct((s_, 256), MXU_DTYPE),
                   jax.ShapeDtypeStruct((s_, 256), MXU_DTYPE), jax.ShapeDtypeStruct((s_, Q_LORA), MXU_DTYPE),
                   jax.ShapeDtypeStruct((s_, KV_LORA), MXU_DTYPE), jax.ShapeDtypeStruct((s_, LANES), F32)],
        compiler_params=_cparams(("parallel",)),
    )(p, p, p, p, p, p, gq, gkv, a_cos, a_up, a_dn, k_cos, b_up, b_dn)


def _mla_prep(qp, kp, kro, tabs, *, ts=256):
    s_ = qp.shape[0]
    _, _, _, q_cos, _, b_up, b_dn = tabs

    def body(q_ref, k_ref, kr_ref, qc, bu, bd, qo_ref, ko_ref):
        c_, u_, d_ = qc[...], bu[...], bd[...]
        kr = kr_ref[...]
        for h in range(MLA_HEADS):
            sl = slice(h * LANES, (h + 1) * LANES)
            qo_ref[:, sl] = _rope(q_ref[:, sl], c_, u_, d_, 112, 16).astype(qo_ref.dtype)
            ko_ref[:, sl] = (k_ref[:, sl] + kr).astype(ko_ref.dtype)

    tab = _rows(ts, LANES)
    return pl.pallas_call(
        body, name="mla_prep", grid=(s_ // ts,),
        in_specs=[_rows(ts, 1024), _rows(ts, 1024), tab, tab, tab, tab],
        out_specs=[_rows(ts, 1024), _rows(ts, 1024)],
        out_shape=[jax.ShapeDtypeStruct((s_, 1024), MXU_DTYPE)] * 2,
        compiler_params=_cparams(("parallel",)),
    )(qp, kp, kro, q_cos, b_up, b_dn)


def _mla_unprep(dqc, dkc, dvp, tabs, *, ts=256):
    s_ = dqc.shape[0]
    _, _, _, q_cos, k_cos, b_up, b_dn = tabs

    def body(dq_ref, dk_ref, dv_ref, qc, kc, bu, bd, dqo_ref, dkvo_ref, dkr_ref):
        c_, u_, d_ = qc[...], bu[...], bd[...]
        tot = jnp.zeros((ts, LANES), F32)
        for h in range(MLA_HEADS):
            sl = slice(h * LANES, (h + 1) * LANES)
            dqo_ref[:, sl] = _rope_t(dq_ref[:, sl], c_, u_, d_, 112, 16).astype(dqo_ref.dtype)
            dk = dk_ref[:, sl]
            dkvo_ref[:, sl] = dk.astype(dkvo_ref.dtype)
            tot = tot + dk
        dkvo_ref[:, 1024:2048] = dv_ref[...].astype(dkvo_ref.dtype)
        dkr_ref[...] = _rope_t(tot, kc[...], u_, d_, 112, 16).astype(dkr_ref.dtype)

    tab = _rows(ts, LANES)
    return pl.pallas_call(
        body, name="mla_unprep", grid=(s_ // ts,),
        in_specs=[_rows(ts, 1024), _rows(ts, 1024), _rows(ts, 1024), tab, tab, tab, tab],
        out_specs=[_rows(ts, 1024), _rows(ts, 2048), _rows(ts, LANES)],
        out_shape=[jax.ShapeDtypeStruct((s_, 1024), MXU_DTYPE), jax.ShapeDtypeStruct((s_, 2048), MXU_DTYPE),
                   jax.ShapeDtypeStruct((s_, LANES), MXU_DTYPE)],
        compiler_params=_cparams(("parallel",)),
    )(dqc, dkc, dvp, q_cos, k_cos, b_up, b_dn)


def _swa_unrope(dqa, dka, tabs, *, ts=256):
    s_ = dqa.shape[0]
    a_cos, a_up, a_dn = tabs[0], tabs[1], tabs[2]

    def body(dq_ref, dk_ref, ac, au, ad, dqo_ref, dko_ref):
        c_, u_, d_ = ac[...], au[...], ad[...]
        for h in range(SWA_HEADS):
            sl = slice(h * LANES, (h + 1) * LANES)
            dqo_ref[:, sl] = _rope_t(dq_ref[:, sl], c_, u_, d_, 96, 32).astype(dqo_ref.dtype)
        for h in range(SWA_KV_HEADS):
            sl = slice(h * LANES, (h + 1) * LANES)
            dko_ref[:, sl] = _rope_t(dk_ref[:, sl], c_, u_, d_, 96, 32).astype(dko_ref.dtype)

    tab = _rows(ts, LANES)
    return pl.pallas_call(
        body, name="swa_unrope", grid=(s_ // ts,),
        in_specs=[_rows(ts, 1024), _rows(ts, 256), tab, tab, tab],
        out_specs=[_rows(ts, 1024), _rows(ts, 256)],
        out_shape=[jax.ShapeDtypeStruct((s_, 1024), MXU_DTYPE), jax.ShapeDtypeStruct((s_, 256), MXU_DTYPE)],
        compiler_params=_cparams(("parallel",)),
    )(dqa, dka, a_cos, a_up, a_dn)


def _gate_fwd(p, ta, tb, *, ts=256):
    s_ = p.shape[0]

    def body(ga_ref, gb_ref, ta_ref, tb_ref, y_ref):
        y = _sigmoid(ga_ref[...]) * ta_ref[...] + _sigmoid(gb_ref[...]) * tb_ref[...]
        y_ref[...] = y.astype(y_ref.dtype)

    return pl.pallas_call(
        body, name="gate_fwd", grid=(s_ // ts,),
        in_specs=[_rows(ts, 1024, P_GA // 1024), _rows(ts, 1024, P_GB // 1024), _rows(ts, 1024), _rows(ts, 1024)],
        out_specs=_rows(ts, 1024), out_shape=jax.ShapeDtypeStruct((s_, 1024), MXU_DTYPE),
        compiler_params=_cparams(("parallel",)),
    )(p, p, ta, tb)


def _gate_bwd(p, ta, tb, dy, *, ts=256):
    s_ = p.shape[0]

    def body(ga_ref, gb_ref, ta_ref, tb_ref, dy_ref, dta_ref, dtb_ref, dg_ref):
        d = dy_ref[...]
        sa, sb = _sigmoid(ga_ref[...]), _sigmoid(gb_ref[...])
        dta_ref[...] = (d * sa).astype(dta_ref.dtype)
        dtb_ref[...] = (d * sb).astype(dtb_ref.dtype)
        dg_ref[:, 0:1024] = (d * ta_ref[...] * (sa * (1.0 - sa))).astype(dg_ref.dtype)
        dg_ref[:, 1024:2048] = (d * tb_ref[...] * (sb * (1.0 - sb))).astype(dg_ref.dtype)

    return pl.pallas_call(
        body, name="gate_bwd", grid=(s_ // ts,),
        in_specs=[_rows(ts, 1024, P_GA // 1024), _rows(ts, 1024, P_GB // 1024), _rows(ts, 1024), _rows(ts, 1024),
                  _rows(ts, 1024)],
        out_specs=[_rows(ts, 1024), _rows(ts, 1024), _rows(ts, 2048)],
        out_shape=[jax.ShapeDtypeStruct((s_, 1024), MXU_DTYPE)] * 2 + [jax.ShapeDtypeStruct((s_, 2048), MXU_DTYPE)],
        compiler_params=_cparams(("parallel",)),
    )(p, p, ta, tb, dy)


def _swiglu_fwd(gu, *, ts=256):
    s_ = gu.shape[0]

    def body(g_ref, u_ref, a_ref):
        g = g_ref[...]
        a_ref[...] = (g * _sigmoid(g) * u_ref[...]).astype(a_ref.dtype)

    return pl.pallas_call(
        body, name="swiglu_fwd", grid=(s_ // ts,), in_specs=[_rows(ts, D_FF, 0), _rows(ts, D_FF, 1)],
        out_specs=_rows(ts, D_FF), out_shape=jax.ShapeDtypeStruct((s_, D_FF), MXU_DTYPE),
        compiler_params=_cparams(("parallel",)),
    )(gu, gu)


def _swiglu_bwd(gu, da, *, ts=256):
    s_ = gu.shape[0]

    def body(g_ref, u_ref, da_ref, o_ref):
        g, u, d = g_ref[...], u_ref[...], da_ref[...]
        sg = _sigmoid(g)
        o_ref[:, 0:D_FF] = (d * u * (sg * (1.0 + g * (1.0 - sg)))).astype(o_ref.dtype)
        o_ref[:, D_FF:2 * D_FF] = (d * (g * sg)).astype(o_ref.dtype)

    return pl.pallas_call(
        body, name="swiglu_bwd", grid=(s_ // ts,), in_specs=[_rows(ts, D_FF, 0), _rows(ts, D_FF, 1), _rows(ts, D_FF)],
        out_specs=_rows(ts, 2 * D_FF), out_shape=jax.ShapeDtypeStruct((s_, 2 * D_FF), MXU_DTYPE),
        compiler_params=_cparams(("parallel",)),
    )(gu, gu, da)


def _loss_bwd(x2, g, tgt, *, ts=256):
    s_, c = x2.shape

    def body(x_ref, g_ref, t_ref, dx_ref, dxb_ref, dg_ref, lp_ref, tot_ref):
        v = x_ref[...]
        r = lax.rsqrt(jnp.mean(v * v, axis=-1, keepdims=True) + EPS)
        xh = v * r
        gg = g_ref[...]
        e = xh * gg - t_ref[...]
        do = e * (1.0 / c)
        dxh = do * gg
        dx = r * (dxh - xh * jnp.mean(dxh * xh, axis=-1, keepdims=True))
        dx_ref[...] = dx
        dxb_ref[...] = dx.astype(dxb_ref.dtype)
        i = pl.program_id(0)

        @pl.when(i == 0)
        def _():
            dg_ref[...] = jnp.zeros(dg_ref.shape, F32)
            lp_ref[...] = jnp.zeros(lp_ref.shape, F32)

        dg_ref[...] += _sublane_sum(do * xh)
        lp_ref[...] += _sublane_sum(e * e)
        tot_ref[...] = jnp.full(tot_ref.shape, (0.5 / c) * jnp.sum(lp_ref[...]), F32)

    return pl.pallas_call(
        body, name="loss_bwd", grid=(s_ // ts,), in_specs=[_rows(ts, c), _const(1, c), _rows(ts, c)],
        out_specs=[_rows(ts, c), _rows(ts, c), _const(SUBLANES, c), _const(SUBLANES, c), _const(SUBLANES, LANES)],
        out_shape=[jax.ShapeDtypeStruct((s_, c), F32), jax.ShapeDtypeStruct((s_, c), MXU_DTYPE),
                   jax.ShapeDtypeStruct((SUBLANES, c), F32), jax.ShapeDtypeStruct((SUBLANES, c), F32),
                   jax.ShapeDtypeStruct((SUBLANES, LANES), F32)],
        compiler_params=_cparams(("arbitrary",)),
    )(x2, g, tgt)


def _mla_bwd_prep(dob, o32, *, ts=256):
    s_ = dob.shape[0]

    def body(do_ref, o_ref, dob_ref, dl_ref):
        d = do_ref[...]
        dob_ref[...] = d.astype(dob_ref.dtype)
        prod = d * o_ref[...]
        for h in range(MLA_HEADS):
            dl_ref[h] = jnp.sum(prod[:, h * LANES:(h + 1) * LANES].T, axis=0, keepdims=True)

    return pl.pallas_call(
        body, name="mla_bwd_prep", grid=(s_ // ts,), in_specs=[_rows(ts, 1024), _rows(ts, 1024)],
        out_specs=[_rows(ts, 1024), pl.BlockSpec((MLA_HEADS, 1, ts), lambda i: (0, 0, i))],
        out_shape=[jax.ShapeDtypeStruct((s_, 1024), MXU_DTYPE), jax.ShapeDtypeStruct((MLA_HEADS, 1, s_), F32)],
        compiler_params=_cparams(("parallel",)),
    )(dob, o32)


SWA_T = 4 * BLOCK


def _swa_masks(sb):
    kr = lax.broadcasted_iota(jnp.int32, (2 * BLOCK, BLOCK), 0)
    qc = lax.broadcasted_iota(jnp.int32, (2 * BLOCK, BLOCK), 1)
    band = jnp.logical_and(kr > qc, kr <= qc + BLOCK)
    first = jnp.logical_and(band, kr >= BLOCK)
    return band, jnp.logical_or(first, jnp.logical_and(band, sb > 0))


def _swa_in_specs(rev, nsb):
    sbi = (lambda j: nsb - 1 - j) if rev else (lambda j: j)
    cur = pl.BlockSpec((SWA_T, LANES), lambda g, j: (sbi(j), g))
    prev = pl.BlockSpec((BLOCK, LANES), lambda g, j: (jnp.maximum(4 * sbi(j) - 1, 0), g))
    q = pl.BlockSpec((SWA_T, SWA_GROUP * LANES), lambda g, j: (sbi(j), g))
    sink = pl.BlockSpec((1, SUBLANES, LANES), lambda g, j: (g, 0, 0))
    lse = pl.BlockSpec((SWA_GROUP, 1, SWA_T), lambda g, j: (g, 0, sbi(j)))
    return q, cur, prev, sink, lse


def _swa_fwd(qa, ka, va, sink_b):
    s_ = qa.shape[0]
    nsb = s_ // SWA_T
    c2 = HEAD_DIM ** -0.5 * LOG2E

    def body(q_ref, kc_ref, kp_ref, vc_ref, vp_ref, sk_ref, o32_ref, o16_ref, lse_ref, kx, vx):
        kx[0:BLOCK, :] = kp_ref[...]
        kx[BLOCK:5 * BLOCK, :] = kc_ref[...]
        vx[0:BLOCK, :] = vp_ref[...]
        vx[BLOCK:5 * BLOCK, :] = vc_ref[...]
        band, band0 = _swa_masks(pl.program_id(1))
        for hh in range(SWA_GROUP):
            sink2 = sk_ref[0, hh:hh + 1, 0:1] * LOG2E
            cs = slice(hh * LANES, (hh + 1) * LANES)
            for b in range(4):
                rs = slice(b * BLOCK, (b + 1) * BLOCK)
                ks = slice(b * BLOCK, (b + 2) * BLOCK)
                st = lax.dot_general(kx[ks, :], q_ref[rs, cs], NT, preferred_element_type=F32) * c2
                st = jnp.where(band0 if b == 0 else band, st, -jnp.inf)
                m = jnp.maximum(jnp.max(st, axis=0, keepdims=True), sink2)
                pt = jnp.exp2(st - m)
                den = jnp.sum(pt, axis=0, keepdims=True) + jnp.exp2(sink2 - m)
                o = lax.dot_general((pt * (1.0 / den)).astype(MXU_DTYPE), vx[ks, :], TN, preferred_element_type=F32)
                o32_ref[rs, cs] = o
                o16_ref[rs, cs] = o.astype(o16_ref.dtype)
                lse_ref[hh, :, rs] = m + jnp.log2(den)

    q, cur, prev, sink, lse_spec = _swa_in_specs(False, nsb)
    return pl.pallas_call(
        body, name="swa_fwd", grid=(SWA_KV_HEADS, nsb), in_specs=[q, cur, prev, cur, prev, sink],
        out_specs=[q, q, lse_spec],
        out_shape=[jax.ShapeDtypeStruct((s_, SWA_HEADS * LANES), F32), jax.ShapeDtypeStruct((s_, SWA_HEADS * LANES), MXU_DTYPE),
                   jax.ShapeDtypeStruct((SWA_HEADS, 1, s_), F32)],
        scratch_shapes=[pltpu.VMEM((5 * BLOCK, LANES), MXU_DTYPE), pltpu.VMEM((5 * BLOCK, LANES), MXU_DTYPE)],
        compiler_params=_cparams(("parallel", "arbitrary")),
    )(qa, ka, ka, va, va, sink_b)


def _swa_bwd(qa, ka, va, sink_b, o32, do, lse):
    s_ = qa.shape[0]
    nsb = s_ // SWA_T
    scale = HEAD_DIM ** -0.5
    c2 = scale * LOG2E

    def body(q_ref, kc_ref, kp_ref, vc_ref, vp_ref, sk_ref, o_ref, do_ref, lse_ref,
             dq_ref, dk_ref, dv_ref, dsk_ref, kx, vx, kacc, vacc, kcar, vcar):
        j = pl.program_id(1)
        kx[0:BLOCK, :] = kp_ref[...]
        kx[BLOCK:5 * BLOCK, :] = kc_ref[...]
        vx[0:BLOCK, :] = vp_ref[...]
        vx[BLOCK:5 * BLOCK, :] = vc_ref[...]
        band, band0 = _swa_masks(nsb - 1 - j)
        kacc[...] = jnp.zeros(kacc.shape, F32)
        vacc[...] = jnp.zeros(vacc.shape, F32)

        @pl.when(j == 0)
        def _():
            kcar[...] = jnp.zeros(kcar.shape, F32)
            vcar[...] = jnp.zeros(vcar.shape, F32)
            dsk_ref[...] = jnp.zeros(dsk_ref.shape, F32)

        for hh in range(SWA_GROUP):
            sink2 = sk_ref[0, hh:hh + 1, 0:1] * LOG2E
            cs = slice(hh * LANES, (hh + 1) * LANES)
            dsink = jnp.zeros((1, 1), F32)
            for b in range(4):
                rs = slice(b * BLOCK, (b + 1) * BLOCK)
                ks = slice(b * BLOCK, (b + 2) * BLOCK)
                q, k2, v2 = q_ref[rs, cs], kx[ks, :], vx[ks, :]
                d = do_ref[rs, cs]
                delta = jnp.sum((d * o_ref[rs, cs]).T, axis=0, keepdims=True)
                l2 = lse_ref[hh, :, rs]
                st = lax.dot_general(k2, q, NT, preferred_element_type=F32) * c2
                pt = jnp.exp2(jnp.where(band0 if b == 0 else band, st, -jnp.inf) - l2)
                db = d.astype(MXU_DTYPE)
                dst = (pt * (lax.dot_general(v2, db, NT, preferred_element_type=F32) - delta) * scale).astype(MXU_DTYPE)
                dq_ref[rs, cs] = lax.dot_general(dst, k2, TN, preferred_element_type=F32)
                kacc[ks, :] += jnp.dot(dst, q, preferred_element_type=F32)
                vacc[ks, :] += jnp.dot(pt.astype(MXU_DTYPE), db, preferred_element_type=F32)
                dsink = dsink - jnp.sum(jnp.exp2(sink2 - l2) * delta, axis=1, keepdims=True)
            dsk_ref[0, hh:hh + 1, :] += jnp.broadcast_to(dsink, (1, LANES))

        dk_ref[0:3 * BLOCK, :] = kacc[BLOCK:4 * BLOCK, :]
        dk_ref[3 * BLOCK:4 * BLOCK, :] = kacc[4 * BLOCK:5 * BLOCK, :] + kcar[...]
        dv_ref[0:3 * BLOCK, :] = vacc[BLOCK:4 * BLOCK, :].astype(dv_ref.dtype)
        dv_ref[3 * BLOCK:4 * BLOCK, :] = (vacc[4 * BLOCK:5 * BLOCK, :] + vcar[...]).astype(dv_ref.dtype)
        kcar[...] = kacc[0:BLOCK, :]
        vcar[...] = vacc[0:BLOCK, :]

    q, cur, prev, sink, lse_spec = _swa_in_specs(True, nsb)
    return pl.pallas_call(
        body, name="swa_bwd", grid=(SWA_KV_HEADS, nsb),
        in_specs=[q, cur, prev, cur, prev, sink, q, q, lse_spec],
        out_specs=[q, cur, cur, sink],
        out_shape=[jax.ShapeDtypeStruct((s_, SWA_HEADS * LANES), F32), jax.ShapeDtypeStruct((s_, SWA_KV_HEADS * LANES), F32),
                   jax.ShapeDtypeStruct((s_, SWA_KV_HEADS * LANES), MXU_DTYPE),
                   jax.ShapeDtypeStruct((SWA_KV_HEADS, SUBLANES, LANES), F32)],
        scratch_shapes=[pltpu.VMEM((5 * BLOCK, LANES), MXU_DTYPE), pltpu.VMEM((5 * BLOCK, LANES), MXU_DTYPE),
                        pltpu.VMEM((5 * BLOCK, LANES), F32), pltpu.VMEM((5 * BLOCK, LANES), F32),
                        pltpu.VMEM((BLOCK, LANES), F32), pltpu.VMEM((BLOCK, LANES), F32)],
        compiler_params=_cparams(("arbitrary", "arbitrary")),
    )(qa, ka, ka, va, va, sink_b, o32, do, lse)


MLA_T = 512


def _mla_specs(s_, t):
    qs = pl.BlockSpec((t, LANES), lambda h, i: (i, h))
    kv = pl.BlockSpec((s_, LANES), lambda h, i: (0, h))
    row = pl.BlockSpec((1, 1, t), lambda h, i: (h, 0, i))
    return qs, kv, row


def _causal_scores_t(k, q, t, c2, masked):
    st = lax.dot_general(k, q, NT, preferred_element_type=F32) * c2
    if masked:
        kr = lax.broadcasted_iota(jnp.int32, (t, t), 0)
        qc = lax.broadcasted_iota(jnp.int32, (t, t), 1)
        st = jnp.where(kr <= qc, st, -jnp.inf)
    return st


def _mla_fwd(qc, kc, vp):
    s_ = qc.shape[0]
    t = min(MLA_T, s_)
    c2 = MLA_QK ** -0.5 * LOG2E

    def body(q_ref, k_ref, v_ref, o32_ref, o16_ref, lse_ref, m_s, l_s, acc_s):
        qi = pl.program_id(1)
        q = q_ref[...]
        m_s[...] = jnp.full(m_s.shape, -jnp.inf, F32)
        l_s[...] = jnp.zeros(l_s.shape, F32)
        acc_s[...] = jnp.zeros(acc_s.shape, F32)

        def step(ki, masked):
            off = pl.multiple_of(ki * t, t)
            st = _causal_scores_t(k_ref[pl.ds(off, t), :], q, t, c2, masked)
            m_old = m_s[...]
            m_new = jnp.maximum(m_old, jnp.max(st, axis=0, keepdims=True))
            alpha = jnp.exp2(m_old - m_new)
            pt = jnp.exp2(st - m_new)
            l_s[...] = alpha * l_s[...] + jnp.sum(pt, axis=0, keepdims=True)
            acc_s[...] = alpha * acc_s[...] + lax.dot_general(
                v_ref[pl.ds(off, t), :], pt.astype(MXU_DTYPE), TN, preferred_element_type=F32)
            m_s[...] = m_new

        def full_block(ki, carry):
            step(ki, False)
            return carry

        lax.fori_loop(0, qi, full_block, 0)
        step(qi, True)
        o = (acc_s[...] * (1.0 / l_s[...])).T
        o32_ref[...] = o
        o16_ref[...] = o.astype(o16_ref.dtype)
        lse_ref[0] = m_s[...] + jnp.log2(l_s[...])

    qs, kv, row = _mla_specs(s_, t)
    return pl.pallas_call(
        body, name="mla_fwd", grid=(MLA_HEADS, s_ // t), in_specs=[qs, kv, kv], out_specs=[qs, qs, row],
        out_shape=[jax.ShapeDtypeStruct((s_, MLA_HEADS * LANES), F32), jax.ShapeDtypeStruct((s_, MLA_HEADS * LANES), MXU_DTYPE),
                   jax.ShapeDtypeStruct((MLA_HEADS, 1, s_), F32)],
        scratch_shapes=[pltpu.VMEM((1, t), F32), pltpu.VMEM((1, t), F32), pltpu.VMEM((LANES, t), F32)],
        compiler_params=_cparams(("parallel", "arbitrary")),
    )(qc, kc, vp)


def _mla_bwd(qc, kc, vp, dob, lse, delta):
    s_ = qc.shape[0]
    t = min(MLA_T, s_)
    scale = MLA_QK ** -0.5
    c2 = scale * LOG2E

    def body(q_ref, do_ref, lse_ref, dl_ref, k_ref, v_ref, dq_ref, dk_ref, dv_ref, dqt_s):
        qi = pl.program_id(1)

        @pl.when(qi == 0)
        def _():
            dk_ref[...] = jnp.zeros(dk_ref.shape, F32)
            dv_ref[...] = jnp.zeros(dv_ref.shape, F32)

        q, d, l2, dl = q_ref[...], do_ref[...], lse_ref[0], dl_ref[0]
        dqt_s[...] = jnp.zeros(dqt_s.shape, F32)

        def step(ki, masked):
            off = pl.multiple_of(ki * t, t)
            k = k_ref[pl.ds(off, t), :]
            pt = jnp.exp2(_causal_scores_t(k, q, t, c2, masked) - l2)
            dpt = lax.dot_general(v_ref[pl.ds(off, t), :], d, NT, preferred_element_type=F32)
            dst = (pt * (dpt - dl) * scale).astype(MXU_DTYPE)
            dv_ref[pl.ds(off, t), :] += jnp.dot(pt.astype(MXU_DTYPE), d, preferred_element_type=F32)
            dk_ref[pl.ds(off, t), :] += jnp.dot(dst, q, preferred_element_type=F32)
            dqt_s[...] += lax.dot_general(k, dst, TN, preferred_element_type=F32)

        def full_block(ki, carry):
            step(ki, False)
            return carry

        lax.fori_loop(0, qi, full_block, 0)
        step(qi, True)
        dq_ref[...] = dqt_s[...].T

    qs, kv, row = _mla_specs(s_, t)
    shp = jax.ShapeDtypeStruct((s_, MLA_HEADS * LANES), F32)
    return pl.pallas_call(
        body, name="mla_bwd", grid=(MLA_HEADS, s_ // t), in_specs=[qs, qs, row, row, kv, kv], out_specs=[qs, kv, kv],
        out_shape=[shp, shp, shp], scratch_shapes=[pltpu.VMEM((LANES, t), F32)],
        compiler_params=_cparams(("parallel", "arbitrary")),
    )(qc, dob, lse, delta, kc, vp)


def _pad_heads(w, nh, hd, axis):
    shp = w.shape
    w = w.reshape(shp[:axis] + (nh, hd) + shp[axis + 1:])
    pad = [(0, 0)] * w.ndim
    pad[axis + 1] = (0, LANES - hd)
    w = jnp.pad(w, pad)
    return w.reshape(shp[:axis] + (nh * LANES,) + shp[axis + 1:])


def _unpad_heads(w, nh, hd, axis):
    shp = w.shape
    w = w.reshape(shp[:axis] + (nh, LANES) + shp[axis + 1:])
    w = lax.slice_in_dim(w, 0, hd, axis=axis + 1)
    return w.reshape(shp[:axis] + (nh * hd,) + shp[axis + 1:])


PACK_W = 1024
ROW_TILE = 16
FULL_SHAPE = dict(w_in=(1024, 3488), w_uq=(384, 768), w_ukv=(256, 1024), w_o_swa=(512, 1024), w_o_mla=(512, 1024),
                  w_out=(1024, 1024), w_gate=(1024, 2816), w_up=(1024, 2816), w_down=(2816, 1024))
BIG = tuple(FULL_SHAPE)
ROW_SHARDED = ("w_out", "w_down")
W_IN_COLS = FULL_SHAPE["w_in"][1] // N_DEV
W_IN_ROWS = -(-W_IN_COLS // ROW_TILE) * ROW_TILE
FF_COLS = D_FF // N_DEV
WIRE = (("w_in", 0, W_IN_ROWS), ("w_gate", 448, FF_COLS), ("w_up", 800, FF_COLS), ("w_down", 1152, FF_COLS),
        ("w_out", 1504, 128), ("small", 1632, 208))
SMALL_FLAT = (("w_uq", 0, 36), ("w_ukv", 48, 32), ("w_o_swa", 80, 64), ("w_o_mla", 144, 64))
PACK_ROWS = 1840
SMALL_ROW0 = 1632


def _shard_shape(n):
    r, c = FULL_SHAPE[n]
    return (r // N_DEV, c) if n in ROW_SHARDED else (r, c // N_DEV)


def _wire_pack(sh, dtype):
    c = lambda n: sh[n].astype(dtype)
    rows = [jnp.pad(c("w_in").T, ((0, W_IN_ROWS - W_IN_COLS), (0, 0))), c("w_gate").T, c("w_up").T, c("w_down"), c("w_out")]
    for n, _, r in SMALL_FLAT:
        rows.append(jnp.pad(c(n).reshape(r, PACK_W), ((0, -r % ROW_TILE), (0, 0))))
    return jnp.concatenate(rows, 0)


def _wire_unpack(p):
    out = dict(w_in=p[0:W_IN_COLS].T, w_gate=p[448:800].T, w_up=p[800:1152].T, w_down=p[1152:1504], w_out=p[1504:1632])
    for n, off, r in SMALL_FLAT:
        out[n] = p[SMALL_ROW0 + off:SMALL_ROW0 + off + r].reshape(_shard_shape(n))
    return out


def _w_in_row_maps():
    sp = lambda col: (col // W_IN_COLS) * W_IN_ROWS + col % W_IN_COLS
    fwd = np.full((P_W,), -1, np.int64)

    def put(t0, c0, n):
        fwd[t0:t0 + n] = [sp(c) for c in range(c0, c0 + n)]

    put(P_GA, IN_OFF[6], D_MODEL)
    put(P_GB, IN_OFF[7], D_MODEL)
    for h in range(SWA_HEADS):
        put(P_Q + LANES * h, IN_OFF[0] + HEAD_DIM * h, HEAD_DIM)
    put(P_QLAT, IN_OFF[3], Q_LORA)
    put(P_KR + KR_LANE, IN_OFF[5], MLA_ROPE)
    for h in range(SWA_KV_HEADS):
        put(P_K + LANES * h, IN_OFF[1] + HEAD_DIM * h, HEAD_DIM)
        put(P_V + LANES * h, IN_OFF[2] + HEAD_DIM * h, HEAD_DIM)
    put(P_KVLAT, IN_OFF[4], KV_LORA)
    inv = np.full((N_DEV * W_IN_ROWS,), -1, np.int64)
    inv[fwd[fwd >= 0]] = np.nonzero(fwd >= 0)[0]
    return fwd, inv


def _take_rows(src, idx, *, name):
    n_out, n_src, width = len(idx), src.shape[0], src.shape[1]
    assert n_out % BLOCK == 0 and n_src % BLOCK == 0
    n_tiles = n_out // BLOCK
    blocks = [sorted({int(v) // BLOCK for v in idx[i * BLOCK:(i + 1) * BLOCK] if v >= 0}) for i in range(n_tiles)]
    k_max = max(1, max(len(b) for b in blocks))
    tab = np.zeros((n_tiles, k_max), np.int32)
    sel = np.zeros((n_tiles, k_max, BLOCK, BLOCK), np.float32)
    for i, blks in enumerate(blocks):
        for m, b in enumerate(blks):
            tab[i, m] = b
            for r in range(BLOCK):
                v = int(idx[i * BLOCK + r])
                if v >= 0 and v // BLOCK == b:
                    sel[i, m, r, v % BLOCK] = 1.0

    def body(tab_ref, sel_ref, *refs):
        o_ref = refs[k_max]
        acc = jnp.dot(sel_ref[0, 0], refs[0][...], preferred_element_type=F32)
        for m in range(1, k_max):
            acc = acc + jnp.dot(sel_ref[0, m], refs[m][...], preferred_element_type=F32)
        o_ref[...] = acc.astype(o_ref.dtype)

    def src_spec(m):
        return pl.BlockSpec((BLOCK, width), lambda i, t: (t[i * k_max + m], 0))

    return pl.pallas_call(
        body, name=name,
        grid_spec=pltpu.PrefetchScalarGridSpec(
            num_scalar_prefetch=1, grid=(n_tiles,),
            in_specs=[pl.BlockSpec((1, k_max, BLOCK, BLOCK), lambda i, t: (i, 0, 0, 0))] + [src_spec(m) for m in range(k_max)],
            out_specs=pl.BlockSpec((BLOCK, width), lambda i, t: (i, 0))),
        out_shape=jax.ShapeDtypeStruct((n_out, width), src.dtype),
        compiler_params=_cparams(("parallel",)),
    )(jnp.asarray(tab.reshape(-1)), jnp.asarray(sel, src.dtype), *([src] * k_max))


def _to_operands(win_g, wgu_g, wd_g, wout_g, small_g):
    def full(n, off, r):
        a = small_g[:, off:off + r].reshape((N_DEV,) + _shard_shape(n))
        return jnp.moveaxis(a, 0, 1).reshape(FULL_SHAPE[n])

    w = {n: full(n, off, r) for n, off, r in SMALL_FLAT}
    ukv = w["w_ukv"].reshape(KV_LORA, MLA_HEADS, MLA_NOPE + MLA_V)
    return dict(
        winT=_take_rows(win_g.reshape(N_DEV * W_IN_ROWS, PACK_W), _w_in_row_maps()[0], name="w_in_rows"),
        wguT=wgu_g.reshape(2 * D_FF, D_MODEL),
        wd=wd_g.reshape(D_FF, D_MODEL),
        wout=wout_g.reshape(D_MODEL, D_MODEL),
        wuq=_pad_heads(w["w_uq"], MLA_HEADS, MLA_QK, 1),
        wuk=_pad_heads(ukv[:, :, :MLA_NOPE].reshape(KV_LORA, -1), MLA_HEADS, MLA_NOPE, 1),
        wuv=_pad_heads(ukv[:, :, MLA_NOPE:].reshape(KV_LORA, -1), MLA_HEADS, MLA_V, 1),
        woa=_pad_heads(w["w_o_swa"], SWA_HEADS, HEAD_DIM, 0),
        wob=_pad_heads(w["w_o_mla"], MLA_HEADS, MLA_V, 0),
    )


def _grad_pack(g):
    uk = _unpad_heads(g["wukv"][:, :1024], MLA_HEADS, MLA_NOPE, 1).reshape(KV_LORA, MLA_HEADS, MLA_NOPE)
    uv = _unpad_heads(g["wukv"][:, 1024:], MLA_HEADS, MLA_V, 1).reshape(KV_LORA, MLA_HEADS, MLA_V)
    w = dict(w_uq=_unpad_heads(g["wuq"], MLA_HEADS, MLA_QK, 1), w_ukv=jnp.concatenate([uk, uv], 2).reshape(KV_LORA, -1),
             w_o_swa=_unpad_heads(g["woa"], SWA_HEADS, HEAD_DIM, 0), w_o_mla=_unpad_heads(g["wob"], MLA_HEADS, MLA_V, 0))

    def flat(n, r):
        rr, cc = FULL_SHAPE[n]
        a = jnp.moveaxis(w[n].reshape(rr, N_DEV, cc // N_DEV), 1, 0).reshape(N_DEV, r, PACK_W)
        return jnp.pad(a, ((0, 0), (0, -r % ROW_TILE), (0, 0))).astype(WIRE_DTYPE)

    gu = g["wguT"].reshape(2, N_DEV, FF_COLS, PACK_W)
    return jnp.concatenate(
        [_take_rows(g["winT"], _w_in_row_maps()[1], name="dw_in_rows").reshape(N_DEV, W_IN_ROWS, PACK_W), gu[0], gu[1],
         g["wd"].reshape(N_DEV, FF_COLS, PACK_W), g["wout"].reshape(N_DEV, D_MODEL // N_DEV, PACK_W)]
        + [flat(n, r) for n, _, r in SMALL_FLAT], 1)


def _local_step(x, tgt, ops, small):
    s_ = x.shape[0]
    tabs = _rope_tables(s_)
    sink_b = jnp.broadcast_to(small["swa_sinks"].reshape(SWA_KV_HEADS, SWA_GROUP, 1), (SWA_KV_HEADS, SWA_GROUP, LANES))
    sink_b = jnp.pad(sink_b, ((0, 0), (0, SUBLANES - SWA_GROUP), (0, 0)))

    h = _norm_fwd(x, small["mix_norm_g"], name="norm1")
    p = _mm(h, ops["winT"], "nt", name="proj_in", tn=2176)
    qa, ka, va, cq, ckv, kro = _attn_prep(p, small["q_norm_g"], small["kv_norm_g"], tabs)
    oa32, oa16, lse_a = _swa_fwd(qa, ka, va, sink_b)
    qp = _mm(cq, ops["wuq"], "nn", name="mla_q_up", tn=1024)
    kp = _mm(ckv, ops["wuk"], "nn", name="mla_k_up", tn=1024)
    vp = _mm(ckv, ops["wuv"], "nn", name="mla_v_up", tn=1024, out_dtype=MXU_DTYPE)
    qc, kc = _mla_prep(qp, kp, kro, tabs)
    ob32, ob16, lse_b = _mla_fwd(qc, kc, vp)
    ta = _mm(oa16, ops["woa"], "nn", name="o_swa", tn=1024)
    tb = _mm(ob16, ops["wob"], "nn", name="o_mla", tn=1024)
    y = _gate_fwd(p, ta, tb)
    x1 = _mm(y, ops["wout"], "nn", name="out_proj", add=x, tn=1024)
    h2 = _norm_fwd(x1, small["ffn_norm_g"], name="norm2")
    gu = _mm(h2, ops["wguT"], "nt", name="ffn_in", tn=512)
    act = _swiglu_fwd(gu)
    x2 = _mm(act, ops["wd"], "nn", name="ffn_out", add=x1, tn=1024)

    dx2, dx2b, dg3, _, tot = _loss_bwd(x2, small["final_norm_g"].reshape(1, D_MODEL), tgt)
    g = {}
    dact = _mm(dx2b, ops["wd"], "nt", name="d_act", tn=1408)
    g["wd"] = _mm(act, dx2b, "tn", name="dw_down", tm=1408, tn=1024, tk=512, out_dtype=WIRE_DTYPE)
    dgu = _swiglu_bwd(gu, dact)
    dh2 = _mm(dgu, ops["wguT"], "nn", name="d_h2", tn=1024, tk=2816)
    g["wguT"] = _mm(dgu, h2, "tn", name="dw_ffn_in", tm=512, tn=1024, tk=512, out_dtype=WIRE_DTYPE)
    dx1, dx1b, dg2 = _norm_bwd(x1, small["ffn_norm_g"], dh2, dx2, name="norm2_bwd")
    dy = _mm(dx1b, ops["wout"], "nt", name="d_y", tn=1024)
    g["wout"] = _mm(y, dx1b, "tn", name="dw_out", tm=1024, tn=1024, tk=512, out_dtype=WIRE_DTYPE)
    dta, dtb, dgab = _gate_bwd(p, ta, tb, dy)
    doa = _mm(dta, ops["woa"], "nt", name="d_oa", tn=1024)
    g["woa"] = _mm(oa16, dta, "tn", name="dw_o_swa", tm=1024, tn=1024, tk=512)
    dob = _mm(dtb, ops["wob"], "nt", name="d_ob", tn=1024)
    g["wob"] = _mm(ob16, dtb, "tn", name="dw_o_mla", tm=1024, tn=1024, tk=512)
    dob16, delta_b = _mla_bwd_prep(dob, ob32)
    dqc, dkc, dvp = _mla_bwd(qc, kc, vp, dob16, lse_b, delta_b)
    dqp, dkv, dkr = _mla_unprep(dqc, dkc, dvp, tabs)
    dcq = _mm(dqp, ops["wuq"], "nt", name="d_cq", tn=Q_LORA)
    g["wuq"] = _mm(cq, dqp, "tn", name="dw_uq", tm=Q_LORA, tn=1024, tk=512)
    dckv = _mm(dkv, jnp.concatenate([ops["wuk"], ops["wuv"]], 1), "nt", name="d_ckv", tn=KV_LORA)
    g["wukv"] = _mm(ckv, dkv, "tn", name="dw_ukv", tm=KV_LORA, tn=1024, tk=512)
    _, dqlat, dgq = _norm_bwd(p, small["q_norm_g"], dcq, None, name="qnorm_bwd", x_cb=P_QLAT // Q_LORA)
    _, dkvlat, dgkv = _norm_bwd(p, small["kv_norm_g"], dckv, None, name="kvnorm_bwd", x_cb=P_KVLAT // KV_LORA)
    dqa, dka, dva, dsk = _swa_bwd(qa, ka, va, sink_b, oa32, doa, lse_a)
    dq_raw, dk_raw = _swa_unrope(dqa, dka, tabs)
    dp = jnp.concatenate([dgab, dq_raw, dqlat, dkr, dk_raw, dva, dkvlat], 1)
    dh = _mm(dp, ops["winT"], "nn", name="d_h", tn=1024, tk=2176)
    g["winT"] = _mm(dp, h, "tn", name="dw_in", tm=2176, tn=1024, tk=512, out_dtype=WIRE_DTYPE)
    gx, _, dg1 = _norm_bwd(x, small["mix_norm_g"], dh, dx1, name="norm1_bwd")

    sm = dict(mix_norm_g=dg1, ffn_norm_g=dg2, final_norm_g=dg3, q_norm_g=dgq, kv_norm_g=dgkv,
              swa_sinks=dsk[:, :SWA_GROUP, 0].reshape(1, SWA_HEADS))
    return tot, gx, g, sm


MESH = pl.DeviceIdType.MESH
ANY = pl.BlockSpec(memory_space=pl.ANY)


def _position():
    return lax.axis_index("x"), lax.axis_index("y"), lax.axis_index("c")


WEIGHT_PIECES = ((0, lambda d: (d,), 0, W_IN_ROWS), (1, lambda d: (0, d), 448, FF_COLS), (1, lambda d: (1, d), 800, FF_COLS),
                 (2, lambda d: (d,), 1152, FF_COLS), (3, lambda d: (d,), 1504, 128), (4, lambda d: (d,), 1632, 208))
WEIGHT_PIECE_SHAPES = ((N_DEV, W_IN_ROWS, PACK_W), (2, N_DEV, FF_COLS, PACK_W), (N_DEV, FF_COLS, PACK_W),
                       (N_DEV, 128, PACK_W), (N_DEV, 208, PACK_W))


def _all_gather(block, pieces, shapes, *, name):
    n_out = len(shapes)

    def body(x_ref, *refs):
        outs, (send_sems, recv_sems, local_sem) = refs[:n_out], refs[n_out:]
        x, y, c = _position()
        me, sibling = (x, y, c), (x, y, 1 - c)
        chips = [(1 - x, y), (x, 1 - y), (1 - x, 1 - y)]

        def dst(piece, blk):
            arr, lead, _, _ = piece
            return outs[arr].at[lead(4 * blk[0] + 2 * blk[1] + blk[2])]

        def own(piece):
            return x_ref.at[pl.ds(piece[2], piece[3])]

        def copies(k, blk, to, from_input):
            return [pltpu.make_async_remote_copy(
                src_ref=own(p) if from_input else dst(p, blk), dst_ref=dst(p, blk), send_sem=send_sems.at[k],
                recv_sem=recv_sems.at[k], device_id=to, device_id_type=MESH) for p in pieces]

        def whole_block(k):
            return pltpu.make_async_remote_copy(src_ref=x_ref, dst_ref=x_ref, send_sem=send_sems.at[k],
                                                recv_sem=recv_sems.at[k], device_id=me, device_id_type=MESH)

        for p in pieces:
            pltpu.make_async_copy(own(p), dst(p, me), local_sem).start()
        for cp in copies(0, me, sibling, True):
            cp.start()
        for j, chip in enumerate(chips):
            for cp in copies(1 + j, me, (*chip, c), True):
                cp.start()
        for j, chip in enumerate(chips):
            whole_block(1 + j).wait_recv()
            for cp in copies(4 + j, (*chip, c), sibling, False):
                cp.start()
        whole_block(0).wait_recv()
        for j in range(3):
            whole_block(4 + j).wait_recv()
        for k in range(7):
            whole_block(k).wait_send()
        pltpu.make_async_copy(x_ref, x_ref, local_sem).wait()

    return pl.pallas_call(
        body, name=name, out_shape=[jax.ShapeDtypeStruct(s, block.dtype) for s in shapes], in_specs=[ANY],
        out_specs=[ANY] * n_out,
        scratch_shapes=[pltpu.SemaphoreType.DMA((7,)), pltpu.SemaphoreType.DMA((7,)), pltpu.SemaphoreType.DMA],
    )(block)


def _exchange_sibling(g, *, name):
    _, r, c_ = g.shape

    def body(g_ref, land_ref, send_sems, recv_sems):
        x, y, c = _position()
        copies = [pltpu.make_async_remote_copy(
            src_ref=g_ref.at[2 * j + (1 - c)], dst_ref=land_ref.at[j], send_sem=send_sems.at[j], recv_sem=recv_sems.at[j],
            device_id=(x, y, 1 - c), device_id_type=MESH) for j in range(4)]
        for cp in copies:
            cp.start()
        for cp in copies:
            cp.wait_recv()
        for cp in copies:
            cp.wait_send()

    return pl.pallas_call(
        body, name=name, out_shape=jax.ShapeDtypeStruct((4, r, c_), g.dtype), in_specs=[ANY], out_specs=ANY,
        scratch_shapes=[pltpu.SemaphoreType.DMA((4,)), pltpu.SemaphoreType.DMA((4,))],
    )(g)


def _exchange_chips(p, *, name):
    _, r, c_ = p.shape

    def body(p_ref, land_ref, send_sems, recv_sems, local_sem):
        x, y, c = _position()
        mine = 2 * x + y
        own = pltpu.make_async_copy(p_ref.at[mine], land_ref.at[mine], local_sem)
        own.start()
        copies = []
        for k, (px, py) in enumerate([(1 - x, y), (x, 1 - y), (1 - x, 1 - y)]):
            copies.append(pltpu.make_async_remote_copy(
                src_ref=p_ref.at[2 * px + py], dst_ref=land_ref.at[mine], send_sem=send_sems.at[k], recv_sem=recv_sems.at[k],
                device_id=(px, py, c), device_id_type=MESH))
        for cp in copies:
            cp.start()
        for cp in copies:
            cp.wait_recv()
        for cp in copies:
            cp.wait_send()
        own.wait()

    return pl.pallas_call(
        body, name=name, out_shape=jax.ShapeDtypeStruct((4, r, c_), p.dtype), in_specs=[ANY], out_specs=ANY,
        scratch_shapes=[pltpu.SemaphoreType.DMA((3,)), pltpu.SemaphoreType.DMA((3,)), pltpu.SemaphoreType.DMA],
    )(p)


def _pair_add(g, land, c_idx, *, tr=368):
    _, r, c_ = g.shape

    def body(c_ref, g_ref, l_ref, o_ref):
        o_ref[...] = (g_ref[...].astype(F32) + l_ref[...].astype(F32)).astype(o_ref.dtype)

    return pl.pallas_call(
        body, name="rs_pair_add",
        grid_spec=pltpu.PrefetchScalarGridSpec(
            num_scalar_prefetch=1, grid=(4, r // tr),
            in_specs=[pl.BlockSpec((1, tr, c_), lambda j, i, cr: (2 * j + cr[0], i, 0)),
                      pl.BlockSpec((1, tr, c_), lambda j, i, cr: (j, i, 0))],
            out_specs=pl.BlockSpec((1, tr, c_), lambda j, i, cr: (j, i, 0))),
        out_shape=jax.ShapeDtypeStruct((4, r, c_), g.dtype),
        compiler_params=_cparams(("parallel", "parallel")),
    )(c_idx, g, land)


def _adamw(w, g, m, v):
    m = ADAM_B1 * m + (1.0 - ADAM_B1) * g
    v = ADAM_B2 * v + (1.0 - ADAM_B2) * (g * g)
    m_hat = m / (1.0 - ADAM_B1 ** ADAM_STEP)
    v_hat = v / (1.0 - ADAM_B2 ** ADAM_STEP)
    delta = -ADAM_LR * (m_hat / (jnp.sqrt(v_hat) + ADAM_EPS) + ADAM_WD * w)
    return delta, m, v


def _chip_sum(land, *, tr=368):
    _, r, c_ = land.shape

    def body(l_ref, g_ref):
        g = l_ref[0].astype(F32)
        for j in range(1, 4):
            g = g + l_ref[j].astype(F32)
        g_ref[...] = g

    return pl.pallas_call(
        body, name="rs_chip_sum", grid=(r // tr,), in_specs=[pl.BlockSpec((4, tr, c_), lambda i: (0, i, 0))],
        out_specs=pl.BlockSpec((tr, c_), lambda i: (i, 0)), out_shape=jax.ShapeDtypeStruct((r, c_), F32),
        compiler_params=_cparams(("parallel",)),
    )(land)


def _adamw_call(w, g, m, v, *, name, max_rows=256):
    r, c_ = w.shape
    tr = max_rows if r > max_rows and r % max_rows == 0 else r

    def body(w_ref, g_ref, m_ref, v_ref, d_ref, mo_ref, vo_ref):
        d, mn, vn = _adamw(w_ref[...], g_ref[...], m_ref[...], v_ref[...])
        d_ref[...] = d
        mo_ref[...] = mn
        vo_ref[...] = vn

    row = pl.BlockSpec((tr, c_), lambda i: (i, 0))
    shp = jax.ShapeDtypeStruct((r, c_), F32)
    return pl.pallas_call(
        body, name=name, grid=(r // tr,), in_specs=[row] * 4, out_specs=[row] * 3, out_shape=[shp] * 3,
        compiler_params=_cparams(("parallel",)),
    )(w, g, m, v)


SMALL = ("mix_norm_g", "ffn_norm_g", "final_norm_g", "q_norm_g", "kv_norm_g", "swa_sinks")
SMALL_W = dict(mix_norm_g=1024, ffn_norm_g=1024, final_norm_g=1024, q_norm_g=Q_LORA, kv_norm_g=KV_LORA, swa_sinks=SWA_HEADS)


def _small_adamw(parts, w, m, v):
    n_par = parts.shape[1] // SUBLANES

    def body(p_ref, w_ref, m_ref, v_ref, g_ref, d_ref, mo_ref, vo_ref):
        tot = p_ref[0]
        for dev in range(1, N_DEV):
            tot = tot + p_ref[dev]
        row_id = lax.broadcasted_iota(jnp.int32, (SUBLANES, PACK_W), 0)
        g = jnp.zeros((SUBLANES, PACK_W), F32)
        for k in range(n_par):
            g = jnp.where(row_id == k, jnp.sum(tot[k * SUBLANES:(k + 1) * SUBLANES, :], axis=0, keepdims=True), g)
        d, mn, vn = _adamw(w_ref[...], g, m_ref[...], v_ref[...])
        g_ref[...] = g
        d_ref[...] = d
        mo_ref[...] = mn
        vo_ref[...] = vn

    shp = jax.ShapeDtypeStruct((SUBLANES, PACK_W), F32)
    vm = pl.BlockSpec(memory_space=pltpu.VMEM)
    return pl.pallas_call(body, name="small_adamw", in_specs=[vm] * 4, out_specs=[vm] * 4, out_shape=[shp] * 4)(parts, w, m, v)


def _small_pack(d, rows_each):
    parts = [jnp.pad(d[n].astype(F32), ((0, 0), (0, PACK_W - SMALL_W[n]))) for n in SMALL]
    out = jnp.concatenate(parts, 0)
    pad = -out.shape[0] % SUBLANES
    return jnp.pad(out, ((0, pad), (0, 0)))


def kernel(x, mix_norm_g, w_in, swa_sinks, q_norm_g, w_uq, kv_norm_g, w_ukv, w_o_swa, w_o_mla, w_out, ffn_norm_g, w_gate, w_up, w_down, final_norm_g, loss_target, m_mix_norm_g, m_w_in, m_swa_sinks, m_q_norm_g, m_w_uq, m_kv_norm_g, m_w_ukv, m_w_o_swa, m_w_o_mla, m_w_out, m_ffn_norm_g, m_w_gate, m_w_up, m_w_down, m_final_norm_g, v_mix_norm_g, v_w_in, v_swa_sinks, v_q_norm_g, v_w_uq, v_kv_norm_g, v_w_ukv, v_w_o_swa, v_w_o_mla, v_w_out, v_ffn_norm_g, v_w_gate, v_w_up, v_w_down, v_final_norm_g):
    big_w = dict(w_in=w_in[0], w_uq=w_uq[0], w_ukv=w_ukv[0], w_o_swa=w_o_swa[0], w_o_mla=w_o_mla[0], w_out=w_out[0],
                 w_gate=w_gate[0], w_up=w_up[0], w_down=w_down[0])
    big_m = dict(w_in=m_w_in[0], w_uq=m_w_uq[0], w_ukv=m_w_ukv[0], w_o_swa=m_w_o_swa[0], w_o_mla=m_w_o_mla[0],
                 w_out=m_w_out[0], w_gate=m_w_gate[0], w_up=m_w_up[0], w_down=m_w_down[0])
    big_v = dict(w_in=v_w_in[0], w_uq=v_w_uq[0], w_ukv=v_w_ukv[0], w_o_swa=v_w_o_swa[0], w_o_mla=v_w_o_mla[0],
                 w_out=v_w_out[0], w_gate=v_w_gate[0], w_up=v_w_up[0], w_down=v_w_down[0])
    small_w = dict(mix_norm_g=mix_norm_g, ffn_norm_g=ffn_norm_g, final_norm_g=final_norm_g.reshape(1, D_MODEL),
                   q_norm_g=q_norm_g, kv_norm_g=kv_norm_g, swa_sinks=swa_sinks)
    small_m = dict(mix_norm_g=m_mix_norm_g, ffn_norm_g=m_ffn_norm_g, final_norm_g=m_final_norm_g.reshape(1, D_MODEL),
                   q_norm_g=m_q_norm_g, kv_norm_g=m_kv_norm_g, swa_sinks=m_swa_sinks)
    small_v = dict(mix_norm_g=v_mix_norm_g, ffn_norm_g=v_ffn_norm_g, final_norm_g=v_final_norm_g.reshape(1, D_MODEL),
                   q_norm_g=v_q_norm_g, kv_norm_g=v_kv_norm_g, swa_sinks=v_swa_sinks)

    gathered = _all_gather(_wire_pack(big_w, WIRE_DTYPE), WEIGHT_PIECES, WEIGHT_PIECE_SHAPES, name="ag_weights")
    ops = _to_operands(*gathered)

    loss_tot, gx, g_ops, g_small = _local_step(x[0], loss_target[0], ops, small_w)

    g_pack = _grad_pack(g_ops)
    c_idx = lax.axis_index("c").astype(jnp.int32).reshape(1)
    land = _exchange_chips(_pair_add(g_pack, _exchange_sibling(g_pack, name="rs_sibling"), c_idx), name="rs_chips")
    gw = _wire_unpack(_chip_sum(land))
    dw, mw, vw = {}, {}, {}
    for n in BIG:
        dw[n], mw[n], vw[n] = _adamw_call(big_w[n], gw[n], big_m[n], big_v[n], name="adamw_" + n)

    loss_rows = jnp.pad(loss_tot[0:1, 0:1], ((0, SUBLANES - 1), (0, PACK_W - 1)))
    small_rows = jnp.concatenate([_small_pack(g_small_rows(g_small), SUBLANES), loss_rows], 0)
    parts, = _all_gather(small_rows, ((0, lambda d: (d,), 0, small_rows.shape[0]),), ((N_DEV,) + small_rows.shape,),
                         name="ag_small")
    gs, ds, ms, vs = _small_adamw(parts, _small_pack(small_w, 1), _small_pack(small_m, 1), _small_pack(small_v, 1))
    loss = gs[len(SMALL), 0]

    def small_out(packed):
        out = {}
        for k, n in enumerate(SMALL):
            out[n] = packed[k:k + 1, :SMALL_W[n]]
        out["final_norm_g"] = out["final_norm_g"].reshape(D_MODEL)
        return out

    gs, ds, ms, vs = small_out(gs), small_out(ds), small_out(ms), small_out(vs)

    order = ("mix_norm_g", "w_in", "swa_sinks", "q_norm_g", "w_uq", "kv_norm_g", "w_ukv", "w_o_swa", "w_o_mla", "w_out",
             "ffn_norm_g", "w_gate", "w_up", "w_down", "final_norm_g")

    def leaves(big, small):
        return [big[n][None] if n in big else small[n] for n in order]

    return (loss, gx[None], *leaves(gw, gs), *leaves(dw, ds), *leaves(mw, ms), *leaves(vw, vs))


def g_small_rows(g_small):
    out = dict(g_small)
    out["swa_sinks"] = jnp.pad(g_small["swa_sinks"], ((0, SUBLANES - 1), (0, 0)))
    return out
```

```python
import functools

import numpy as np
import jax
import jax.numpy as jnp
from jax import lax
from jax.experimental import pallas as pl
from jax.experimental.pallas import tpu as pltpu

F32 = jnp.float32
MXU_DTYPE = jnp.bfloat16
WIRE_DTYPE = jnp.bfloat16

D_MODEL = 1024
EPS = 1e-6
ROPE_THETA = 10000.0
BLOCK = 128
HEAD_DIM = 64
SWA_HEADS = 8
SWA_KV_HEADS = 2
SWA_GROUP = SWA_HEADS // SWA_KV_HEADS
MLA_HEADS = 8
MLA_NOPE = 64
MLA_ROPE = 32
MLA_V = 64
MLA_QK = MLA_NOPE + MLA_ROPE
Q_LORA = 384
KV_LORA = 256
D_FF = 2816
IN_SIZES = (512, 128, 128, Q_LORA, KV_LORA, MLA_ROPE, D_MODEL, D_MODEL)
IN_OFF = tuple(int(v) for v in np.cumsum((0,) + IN_SIZES))
ADAM_LR, ADAM_B1, ADAM_B2, ADAM_EPS, ADAM_WD, ADAM_STEP = 0.001, 0.9, 0.999, 1e-08, 0.01, 10

LANES = 128
SUBLANES = 8
VMEM_LIMIT = 48 * 1024 * 1024
N_DEV = 8
AXES = ("x", "y", "c")

P_GA, P_GB, P_Q, P_QLAT, P_KR, P_K, P_V, P_KVLAT, P_W = 0, 1024, 2048, 3072, 3456, 3584, 3840, 4096, 4352
KR_LANE = 64

LOG2E = 1.4426950408889634

NT = (((1,), (1,)), ((), ()))
NN = (((1,), (0,)), ((), ()))
TN = (((0,), (0,)), ((), ()))


def _cparams(sem):
    return pltpu.CompilerParams(dimension_semantics=sem, vmem_limit_bytes=VMEM_LIMIT)


def _mm(a, b, mode, *, name, out_dtype=F32, add=None, tm=512, tn=512, tk=None):
    if mode == "nn":
        (M, K), (K2, N) = a.shape, b.shape
    elif mode == "nt":
        (M, K), (N, K2) = a.shape, b.shape
    else:
        (K, M), (K2, N) = a.shape, b.shape
    assert K == K2, (a.shape, b.shape, mode)
    tk = K if tk is None else tk
    tm, tn = min(tm, M), min(tn, N)
    assert M % tm == 0 and N % tn == 0 and K % tk == 0, (M, N, K, tm, tn, tk)
    nk = K // tk
    dn = {"nn": NN, "nt": NT, "tn": TN}[mode]
    if mode == "tn":
        a_spec = pl.BlockSpec((tk, tm), lambda i, j, k: (k, i))
    else:
        a_spec = pl.BlockSpec((tm, tk), lambda i, j, k: (i, k))
    if mode == "nt":
        b_spec = pl.BlockSpec((tn, tk), lambda i, j, k: (j, k))
    else:
        b_spec = pl.BlockSpec((tk, tn), lambda i, j, k: (k, j))
    o_spec = pl.BlockSpec((tm, tn), lambda i, j, k: (i, j))
    has_add = add is not None

    def body(*refs):
        a_ref, b_ref = refs[0], refs[1]
        add_ref = refs[2] if has_add else None
        o_ref = refs[3] if has_add else refs[2]
        p = lax.dot_general(a_ref[...], b_ref[...], dn, preferred_element_type=F32)

        def finish(acc):
            if has_add:
                acc = acc + add_ref[...]
            o_ref[...] = acc.astype(o_ref.dtype)

        if nk == 1:
            finish(p)
        else:
            acc_ref = refs[-1]
            k = pl.program_id(2)

            @pl.when(k == 0)
            def _():
                acc_ref[...] = p

            @pl.when(k > 0)
            def _():
                acc_ref[...] += p

            @pl.when(k == nk - 1)
            def _():
                finish(acc_ref[...])

    ins = [a, b] + ([add] if has_add else [])
    in_specs = [a_spec, b_spec] + ([o_spec] if has_add else [])
    return pl.pallas_call(
        body, name=name, grid=(M // tm, N // tn, nk), in_specs=in_specs, out_specs=o_spec,
        out_shape=jax.ShapeDtypeStruct((M, N), out_dtype),
        scratch_shapes=[pltpu.VMEM((tm, tn), F32)] if nk > 1 else [],
        compiler_params=_cparams(("parallel", "parallel", "arbitrary")),
    )(*ins)


def _rows(ts, w, cb=0):
    return pl.BlockSpec((ts, w), lambda i: (i, cb))


def _const(r, w):
    return pl.BlockSpec((r, w), lambda i: (0, 0))


def _sublane_sum(v):
    ts, c = v.shape
    return jnp.sum(v.reshape(ts // SUBLANES, SUBLANES, c), axis=0)


def _sigmoid(v):
    return 1.0 / (1.0 + jnp.exp(-v))


def _rope(v, cos, s_up, s_dn, up, dn):
    return v * cos + pltpu.roll(v, up, 1) * s_up + pltpu.roll(v, dn, 1) * s_dn


def _rope_t(dv, cos, s_up, s_dn, up, dn):
    return dv * cos + pltpu.roll(dv * s_up, dn, 1) + pltpu.roll(dv * s_dn, up, 1)


def _rope_tables(seq):
    pos = np.arange(seq, dtype=np.float32)[:, None]

    def base(dim):
        inv = np.float32(ROPE_THETA) ** (-np.arange(0, dim, 2, dtype=np.float32) / np.float32(dim))
        ang = (pos * inv.astype(np.float32)[None, :]).astype(np.float32)
        return np.cos(ang).astype(np.float32), np.sin(ang).astype(np.float32)

    z = lambda n: np.zeros((seq, n), np.float32)
    ca, sa = base(HEAD_DIM)
    a_cos = np.concatenate([ca, ca, z(64)], 1)
    a_up = np.concatenate([-sa, z(96)], 1)
    a_dn = np.concatenate([z(32), sa, z(64)], 1)
    cb, sb = base(MLA_ROPE)
    one = np.ones((seq, 64), np.float32)
    q_cos = np.concatenate([one, cb, cb, z(32)], 1)
    k_cos = np.concatenate([z(64), cb, cb, z(32)], 1)
    b_up = np.concatenate([z(64), -sb, z(48)], 1)
    b_dn = np.concatenate([z(80), sb, z(32)], 1)
    return tuple(jnp.asarray(t) for t in (a_cos, a_up, a_dn, q_cos, k_cos, b_up, b_dn))


def _norm_fwd(x, g, *, name, ts=256):
    s_, c = x.shape

    def body(x_ref, g_ref, h_ref):
        v = x_ref[...]
        r = lax.rsqrt(jnp.mean(v * v, axis=-1, keepdims=True) + EPS)
        h_ref[...] = (v * r * g_ref[...]).astype(h_ref.dtype)

    return pl.pallas_call(
        body, name=name, grid=(s_ // ts,), in_specs=[_rows(ts, c), _const(1, c)], out_specs=_rows(ts, c),
        out_shape=jax.ShapeDtypeStruct((s_, c), MXU_DTYPE), compiler_params=_cparams(("parallel",)),
    )(x, g)


def _norm_bwd(x, g, dy, res, *, name, ts=256, x_cb=0, x_src_w=None):
    s_ = x.shape[0]
    c = dy.shape[1]
    has_res = res is not None

    def body(*refs):
        x_ref, g_ref, dy_ref = refs[0], refs[1], refs[2]
        res_ref = refs[3] if has_res else None
        dx_ref, dxb_ref, dg_ref = refs[-3], refs[-2], refs[-1]
        v = x_ref[...]
        r = lax.rsqrt(jnp.mean(v * v, axis=-1, keepdims=True) + EPS)
        xh = v * r
        d = dy_ref[...]
        dxh = d * g_ref[...]
        dx = r * (dxh - xh * jnp.mean(dxh * xh, axis=-1, keepdims=True))
        if has_res:
            dx = dx + res_ref[...]
        dx_ref[...] = dx
        dxb_ref[...] = dx.astype(dxb_ref.dtype)

        @pl.when(pl.program_id(0) == 0)
        def _():
            dg_ref[...] = jnp.zeros(dg_ref.shape, F32)

        dg_ref[...] += _sublane_sum(d * xh)

    ins = [x, g, dy] + ([res] if has_res else [])
    in_specs = [_rows(ts, c, x_cb), _const(1, c), _rows(ts, c)] + ([_rows(ts, c)] if has_res else [])
    return pl.pallas_call(
        body, name=name, grid=(s_ // ts,), in_specs=in_specs,
        out_specs=[_rows(ts, c), _rows(ts, c), _const(SUBLANES, c)],
        out_shape=[jax.ShapeDtypeStruct((s_, c), F32), jax.ShapeDtypeStruct((s_, c), MXU_DTYPE),
                   jax.ShapeDtypeStruct((SUBLANES, c), F32)],
        compiler_params=_cparams(("arbitrary",)),
    )(*ins)


def _attn_prep(p, gq, gkv, tabs, *, ts=256):
    s_ = p.shape[0]
    a_cos, a_up, a_dn, _, k_cos, b_up, b_dn = tabs

    def body(q_ref, k_ref, v_ref, ql_ref, kvl_ref, kr_ref, gq_ref, gkv_ref, ac, au, ad, kc, bu, bd,
             qa_ref, ka_ref, va_ref, cq_ref, ckv_ref, kro_ref):
        c_, u_, d_ = ac[...], au[...], ad[...]
        for h in range(SWA_HEADS):
            sl = slice(h * LANES, (h + 1) * LANES)
            qa_ref[:, sl] = _rope(q_ref[:, sl], c_, u_, d_, 96, 32).astype(qa_ref.dtype)
        for h in range(SWA_KV_HEADS):
            sl = slice(h * LANES, (h + 1) * LANES)
            ka_ref[:, sl] = _rope(k_ref[:, sl], c_, u_, d_, 96, 32).astype(ka_ref.dtype)
        va_ref[...] = v_ref[...].astype(va_ref.dtype)
        for src, gref, dst in ((ql_ref, gq_ref, cq_ref), (kvl_ref, gkv_ref, ckv_ref)):
            v = src[...]
            r = lax.rsqrt(jnp.mean(v * v, axis=-1, keepdims=True) + EPS)
            dst[...] = (v * r * gref[...]).astype(dst.dtype)
        kro_ref[...] = _rope(kr_ref[...], kc[...], bu[...], bd[...], 112, 16)

    tab = _rows(ts, LANES)
    return pl.pallas_call(
        body, name="attn_prep", grid=(s_ // ts,),
        in_specs=[_rows(ts, 1024, P_Q // 1024), _rows(ts, 256, P_K // 256), _rows(ts, 256, P_V // 256),
                  _rows(ts, Q_LORA, P_QLAT // Q_LORA), _rows(ts, KV_LORA, P_KVLAT // KV_LORA),
                  _rows(ts, LANES, P_KR // LANES), _const(1, Q_LORA), _const(1, KV_LORA), tab, tab, tab, tab, tab, tab],
        out_specs=[_rows(ts, 1024), _rows(ts, 256), _rows(ts, 256), _rows(ts, Q_LORA), _rows(ts, KV_LORA),
                   _rows(ts, LANES)],
        out_shape=[jax.ShapeDtypeStruct((s_, 1024), MXU_DTYPE), jax.ShapeDtypeStruct((s_, 256), MXU_DTYPE),
                   jax.ShapeDtypeStruct((s_, 256), MXU_DTYPE), jax.ShapeDtypeStruct((s_, Q_LORA), MXU_DTYPE),
                   jax.ShapeDtypeStruct((s_, KV_LORA), MXU_DTYPE), jax.ShapeDtypeStruct((s_, LANES), F32)],
        compiler_params=_cparams(("parallel",)),
    )(p, p, p, p, p, p, gq, gkv, a_cos, a_up, a_dn, k_cos, b_up, b_dn)


def _mla_prep(qp, kp, kro, tabs, *, ts=256):
    s_ = qp.shape[0]
    _, _, _, q_cos, _, b_up, b_dn = tabs

    def body(q_ref, k_ref, kr_ref, qc, bu, bd, qo_ref, ko_ref):
        c_, u_, d_ = qc[...], bu[...], bd[...]
        kr = kr_ref[...]
        for h in range(MLA_HEADS):
            sl = slice(h * LANES, (h + 1) * LANES)
            qo_ref[:, sl] = _rope(q_ref[:, sl], c_, u_, d_, 112, 16).astype(qo_ref.dtype)
            ko_ref[:, sl] = (k_ref[:, sl] + kr).astype(ko_ref.dtype)

    tab = _rows(ts, LANES)
    return pl.pallas_call(
        body, name="mla_prep", grid=(s_ // ts,),
        in_specs=[_rows(ts, 1024), _rows(ts, 1024), tab, tab, tab, tab],
        out_specs=[_rows(ts, 1024), _rows(ts, 1024)],
        out_shape=[jax.ShapeDtypeStruct((s_, 1024), MXU_DTYPE)] * 2,
        compiler_params=_cparams(("parallel",)),
    )(qp, kp, kro, q_cos, b_up, b_dn)


def _mla_unprep(dqc, dkc, dvp, tabs, *, ts=256):
    s_ = dqc.shape[0]
    _, _, _, q_cos, k_cos, b_up, b_dn = tabs

    def body(dq_ref, dk_ref, dv_ref, qc, kc, bu, bd, dqo_ref, dkvo_ref, dkr_ref):
        c_, u_, d_ = qc[...], bu[...], bd[...]
        tot = jnp.zeros((ts, LANES), F32)
        for h in range(MLA_HEADS):
            sl = slice(h * LANES, (h + 1) * LANES)
            dqo_ref[:, sl] = _rope_t(dq_ref[:, sl], c_, u_, d_, 112, 16).astype(dqo_ref.dtype)
            dk = dk_ref[:, sl]
            dkvo_ref[:, sl] = dk.astype(dkvo_ref.dtype)
            tot = tot + dk
        dkvo_ref[:, 1024:2048] = dv_ref[...].astype(dkvo_ref.dtype)
        dkr_ref[...] = _rope_t(tot, kc[...], u_, d_, 112, 16).astype(dkr_ref.dtype)

    tab = _rows(ts, LANES)
    return pl.pallas_call(
        body, name="mla_unprep", grid=(s_ // ts,),
        in_specs=[_rows(ts, 1024), _rows(ts, 1024), _rows(ts, 1024), tab, tab, tab, tab],
        out_specs=[_rows(ts, 1024), _rows(ts, 2048), _rows(ts, LANES)],
        out_shape=[jax.ShapeDtypeStruct((s_, 1024), MXU_DTYPE), jax.ShapeDtypeStruct((s_, 2048), MXU_DTYPE),
                   jax.ShapeDtypeStruct((s_, LANES), MXU_DTYPE)],
        compiler_params=_cparams(("parallel",)),
    )(dqc, dkc, dvp, q_cos, k_cos, b_up, b_dn)


def _swa_unrope(dqa, dka, tabs, *, ts=256):
    s_ = dqa.shape[0]
    a_cos, a_up, a_dn = tabs[0], tabs[1], tabs[2]

    def body(dq_ref, dk_ref, ac, au, ad, dqo_ref, dko_ref):
        c_, u_, d_ = ac[...], au[...], ad[...]
        for h in range(SWA_HEADS):
            sl = slice(h * LANES, (h + 1) * LANES)
            dqo_ref[:, sl] = _rope_t(dq_ref[:, sl], c_, u_, d_, 96, 32).astype(dqo_ref.dtype)
        for h in range(SWA_KV_HEADS):
            sl = slice(h * LANES, (h + 1) * LANES)
            dko_ref[:, sl] = _rope_t(dk_ref[:, sl], c_, u_, d_, 96, 32).astype(dko_ref.dtype)

    tab = _rows(ts, LANES)
    return pl.pallas_call(
        body, name="swa_unrope", grid=(s_ // ts,),
        in_specs=[_rows(ts, 1024), _rows(ts, 256), tab, tab, tab],
        out_specs=[_rows(ts, 1024), _rows(ts, 256)],
        out_shape=[jax.ShapeDtypeStruct((s_, 1024), MXU_DTYPE), jax.ShapeDtypeStruct((s_, 256), MXU_DTYPE)],
        compiler_params=_cparams(("parallel",)),
    )(dqa, dka, a_cos, a_up, a_dn)


def _gate_fwd(p, ta, tb, *, ts=256):
    s_ = p.shape[0]

    def body(ga_ref, gb_ref, ta_ref, tb_ref, y_ref):
        y = _sigmoid(ga_ref[...]) * ta_ref[...] + _sigmoid(gb_ref[...]) * tb_ref[...]
        y_ref[...] = y.astype(y_ref.dtype)

    return pl.pallas_call(
        body, name="gate_fwd", grid=(s_ // ts,),
        in_specs=[_rows(ts, 1024, P_GA // 1024), _rows(ts, 1024, P_GB // 1024), _rows(ts, 1024), _rows(ts, 1024)],
        out_specs=_rows(ts, 1024), out_shape=jax.ShapeDtypeStruct((s_, 1024), MXU_DTYPE),
        compiler_params=_cparams(("parallel",)),
    )(p, p, ta, tb)


def _gate_bwd(p, ta, tb, dy, *, ts=256):
    s_ = p.shape[0]

    def body(ga_ref, gb_ref, ta_ref, tb_ref, dy_ref, dta_ref, dtb_ref, dg_ref):
        d = dy_ref[...]
        sa, sb = _sigmoid(ga_ref[...]), _sigmoid(gb_ref[...])
        dta_ref[...] = (d * sa).astype(dta_ref.dtype)
        dtb_ref[...] = (d * sb).astype(dtb_ref.dtype)
        dg_ref[:, 0:1024] = (d * ta_ref[...] * (sa * (1.0 - sa))).astype(dg_ref.dtype)
        dg_ref[:, 1024:2048] = (d * tb_ref[...] * (sb * (1.0 - sb))).astype(dg_ref.dtype)

    return pl.pallas_call(
        body, name="gate_bwd", grid=(s_ // ts,),
        in_specs=[_rows(ts, 1024, P_GA // 1024), _rows(ts, 1024, P_GB // 1024), _rows(ts, 1024), _rows(ts, 1024),
                  _rows(ts, 1024)],
        out_specs=[_rows(ts, 1024), _rows(ts, 1024), _rows(ts, 2048)],
        out_shape=[jax.ShapeDtypeStruct((s_, 1024), MXU_DTYPE)] * 2 + [jax.ShapeDtypeStruct((s_, 2048), MXU_DTYPE)],
        compiler_params=_cparams(("parallel",)),
    )(p, p, ta, tb, dy)


def _swiglu_fwd(gu, *, ts=256):
    s_ = gu.shape[0]

    def body(g_ref, u_ref, a_ref):
        g = g_ref[...]
        a_ref[...] = (g * _sigmoid(g) * u_ref[...]).astype(a_ref.dtype)

    return pl.pallas_call(
        body, name="swiglu_fwd", grid=(s_ // ts,), in_specs=[_rows(ts, D_FF, 0), _rows(ts, D_FF, 1)],
        out_specs=_rows(ts, D_FF), out_shape=jax.ShapeDtypeStruct((s_, D_FF), MXU_DTYPE),
        compiler_params=_cparams(("parallel",)),
    )(gu, gu)


def _swiglu_bwd(gu, da, *, ts=256):
    s_ = gu.shape[0]

    def body(g_ref, u_ref, da_ref, o_ref):
        g, u, d = g_ref[...], u_ref[...], da_ref[...]
        sg = _sigmoid(g)
        o_ref[:, 0:D_FF] = (d * u * (sg * (1.0 + g * (1.0 - sg)))).astype(o_ref.dtype)
        o_ref[:, D_FF:2 * D_FF] = (d * (g * sg)).astype(o_ref.dtype)

    return pl.pallas_call(
        body, name="swiglu_bwd", grid=(s_ // ts,), in_specs=[_rows(ts, D_FF, 0), _rows(ts, D_FF, 1), _rows(ts, D_FF)],
        out_specs=_rows(ts, 2 * D_FF), out_shape=jax.ShapeDtypeStruct((s_, 2 * D_FF), MXU_DTYPE),
        compiler_params=_cparams(("parallel",)),
    )(gu, gu, da)


def _loss_bwd(x2, g, tgt, *, ts=256):
    s_, c = x2.shape

    def body(x_ref, g_ref, t_ref, dx_ref, dxb_ref, dg_ref, lp_ref, tot_ref):
        v = x_ref[...]
        r = lax.rsqrt(jnp.mean(v * v, axis=-1, keepdims=True) + EPS)
        xh = v * r
        gg = g_ref[...]
        e = xh * gg - t_ref[...]
        do = e * (1.0 / c)
        dxh = do * gg
        dx = r * (dxh - xh * jnp.mean(dxh * xh, axis=-1, keepdims=True))
        dx_ref[...] = dx
        dxb_ref[...] = dx.astype(dxb_ref.dtype)
        i = pl.program_id(0)

        @pl.when(i == 0)
        def _():
            dg_ref[...] = jnp.zeros(dg_ref.shape, F32)
            lp_ref[...] = jnp.zeros(lp_ref.shape, F32)

        dg_ref[...] += _sublane_sum(do * xh)
        lp_ref[...] += _sublane_sum(e * e)
        tot_ref[...] = jnp.full(tot_ref.shape, (0.5 / c) * jnp.sum(lp_ref[...]), F32)

    return pl.pallas_call(
        body, name="loss_bwd", grid=(s_ // ts,), in_specs=[_rows(ts, c), _const(1, c), _rows(ts, c)],
        out_specs=[_rows(ts, c), _rows(ts, c), _const(SUBLANES, c), _const(SUBLANES, c), _const(SUBLANES, LANES)],
        out_shape=[jax.ShapeDtypeStruct((s_, c), F32), jax.ShapeDtypeStruct((s_, c), MXU_DTYPE),
                   jax.ShapeDtypeStruct((SUBLANES, c), F32), jax.ShapeDtypeStruct((SUBLANES, c), F32),
                   jax.ShapeDtypeStruct((SUBLANES, LANES), F32)],
        compiler_params=_cparams(("arbitrary",)),
    )(x2, g, tgt)


def _mla_bwd_prep(dob, o32, *, ts=256):
    s_ = dob.shape[0]

    def body(do_ref, o_ref, dob_ref, dl_ref):
        d = do_ref[...]
        dob_ref[...] = d.astype(dob_ref.dtype)
        prod = d * o_ref[...]
        for h in range(MLA_HEADS):
            dl_ref[h] = jnp.sum(prod[:, h * LANES:(h + 1) * LANES].T, axis=0, keepdims=True)

    return pl.pallas_call(
        body, name="mla_bwd_prep", grid=(s_ // ts,), in_specs=[_rows(ts, 1024), _rows(ts, 1024)],
        out_specs=[_rows(ts, 1024), pl.BlockSpec((MLA_HEADS, 1, ts), lambda i: (0, 0, i))],
        out_shape=[jax.ShapeDtypeStruct((s_, 1024), MXU_DTYPE), jax.ShapeDtypeStruct((MLA_HEADS, 1, s_), F32)],
        compiler_params=_cparams(("parallel",)),
    )(dob, o32)


SWA_T = 4 * BLOCK


def _swa_masks(sb):
    kr = lax.broadcasted_iota(jnp.int32, (2 * BLOCK, BLOCK), 0)
    qc = lax.broadcasted_iota(jnp.int32, (2 * BLOCK, BLOCK), 1)
    band = jnp.logical_and(kr > qc, kr <= qc + BLOCK)
    first = jnp.logical_and(band, kr >= BLOCK)
    return band, jnp.logical_or(first, jnp.logical_and(band, sb > 0))


def _swa_in_specs(rev, nsb):
    sbi = (lambda j: nsb - 1 - j) if rev else (lambda j: j)
    cur = pl.BlockSpec((SWA_T, LANES), lambda g, j: (sbi(j), g))
    prev = pl.BlockSpec((BLOCK, LANES), lambda g, j: (jnp.maximum(4 * sbi(j) - 1, 0), g))
    q = pl.BlockSpec((SWA_T, SWA_GROUP * LANES), lambda g, j: (sbi(j), g))
    sink = pl.BlockSpec((1, SUBLANES, LANES), lambda g, j: (g, 0, 0))
    lse = pl.BlockSpec((SWA_GROUP, 1, SWA_T), lambda g, j: (g, 0, sbi(j)))
    return q, cur, prev, sink, lse


def _swa_fwd(qa, ka, va, sink_b):
    s_ = qa.shape[0]
    nsb = s_ // SWA_T
    c2 = HEAD_DIM ** -0.5 * LOG2E

    def body(q_ref, kc_ref, kp_ref, vc_ref, vp_ref, sk_ref, o32_ref, o16_ref, lse_ref, kx, vx):
        kx[0:BLOCK, :] = kp_ref[...]
        kx[BLOCK:5 * BLOCK, :] = kc_ref[...]
        vx[0:BLOCK, :] = vp_ref[...]
        vx[BLOCK:5 * BLOCK, :] = vc_ref[...]
        band, band0 = _swa_masks(pl.program_id(1))
        for hh in range(SWA_GROUP):
            sink2 = sk_ref[0, hh:hh + 1, 0:1] * LOG2E
            cs = slice(hh * LANES, (hh + 1) * LANES)
            for b in range(4):
                rs = slice(b * BLOCK, (b + 1) * BLOCK)
                ks = slice(b * BLOCK, (b + 2) * BLOCK)
                st = lax.dot_general(kx[ks, :], q_ref[rs, cs], NT, preferred_element_type=F32) * c2
                st = jnp.where(band0 if b == 0 else band, st, -jnp.inf)
                m = jnp.maximum(jnp.max(st, axis=0, keepdims=True), sink2)
                pt = jnp.exp2(st - m)
                den = jnp.sum(pt, axis=0, keepdims=True) + jnp.exp2(sink2 - m)
                o = lax.dot_general((pt * (1.0 / den)).astype(MXU_DTYPE), vx[ks, :], TN, preferred_element_type=F32)
                o32_ref[rs, cs] = o
                o16_ref[rs, cs] = o.astype(o16_ref.dtype)
                lse_ref[hh, :, rs] = m + jnp.log2(den)

    q, cur, prev, sink, lse_spec = _swa_in_specs(False, nsb)
    return pl.pallas_call(
        body, name="swa_fwd", grid=(SWA_KV_HEADS, nsb), in_specs=[q, cur, prev, cur, prev, sink],
        out_specs=[q, q, lse_spec],
        out_shape=[jax.ShapeDtypeStruct((s_, SWA_HEADS * LANES), F32), jax.ShapeDtypeStruct((s_, SWA_HEADS * LANES), MXU_DTYPE),
                   jax.ShapeDtypeStruct((SWA_HEADS, 1, s_), F32)],
        scratch_shapes=[pltpu.VMEM((5 * BLOCK, LANES), MXU_DTYPE), pltpu.VMEM((5 * BLOCK, LANES), MXU_DTYPE)],
        compiler_params=_cparams(("parallel", "arbitrary")),
    )(qa, ka, ka, va, va, sink_b)


def _swa_bwd(qa, ka, va, sink_b, o32, do, lse):
    s_ = qa.shape[0]
    nsb = s_ // SWA_T
    scale = HEAD_DIM ** -0.5
    c2 = scale * LOG2E

    def body(q_ref, kc_ref, kp_ref, vc_ref, vp_ref, sk_ref, o_ref, do_ref, lse_ref,
             dq_ref, dk_ref, dv_ref, dsk_ref, kx, vx, kacc, vacc, kcar, vcar):
        j = pl.program_id(1)
        kx[0:BLOCK, :] = kp_ref[...]
        kx[BLOCK:5 * BLOCK, :] = kc_ref[...]
        vx[0:BLOCK, :] = vp_ref[...]
        vx[BLOCK:5 * BLOCK, :] = vc_ref[...]
        band, band0 = _swa_masks(nsb - 1 - j)
        kacc[...] = jnp.zeros(kacc.shape, F32)
        vacc[...] = jnp.zeros(vacc.shape, F32)

        @pl.when(j == 0)
        def _():
            kcar[...] = jnp.zeros(kcar.shape, F32)
            vcar[...] = jnp.zeros(vcar.shape, F32)
            dsk_ref[...] = jnp.zeros(dsk_ref.shape, F32)

        for hh in range(SWA_GROUP):
            sink2 = sk_ref[0, hh:hh + 1, 0:1] * LOG2E
            cs = slice(hh * LANES, (hh + 1) * LANES)
            dsink = jnp.zeros((1, 1), F32)
            for b in range(4):
                rs = slice(b * BLOCK, (b + 1) * BLOCK)
                ks = slice(b * BLOCK, (b + 2) * BLOCK)
                q, k2, v2 = q_ref[rs, cs], kx[ks, :], vx[ks, :]
                d = do_ref[rs, cs]
                delta = jnp.sum((d * o_ref[rs, cs]).T, axis=0, keepdims=True)
                l2 = lse_ref[hh, :, rs]
                st = lax.dot_general(k2, q, NT, preferred_element_type=F32) * c2
                pt = jnp.exp2(jnp.where(band0 if b == 0 else band, st, -jnp.inf) - l2)
                db = d.astype(MXU_DTYPE)
                dst = (pt * (lax.dot_general(v2, db, NT, preferred_element_type=F32) - delta) * scale).astype(MXU_DTYPE)
                dq_ref[rs, cs] = lax.dot_general(dst, k2, TN, preferred_element_type=F32)
                kacc[ks, :] += jnp.dot(dst, q, preferred_element_type=F32)
                vacc[ks, :] += jnp.dot(pt.astype(MXU_DTYPE), db, preferred_element_type=F32)
                dsink = dsink - jnp.sum(jnp.exp2(sink2 - l2) * delta, axis=1, keepdims=True)
            dsk_ref[0, hh:hh + 1, :] += jnp.broadcast_to(dsink, (1, LANES))

        dk_ref[0:3 * BLOCK, :] = kacc[BLOCK:4 * BLOCK, :]
        dk_ref[3 * BLOCK:4 * BLOCK, :] = kacc[4 * BLOCK:5 * BLOCK, :] + kcar[...]
        dv_ref[0:3 * BLOCK, :] = vacc[BLOCK:4 * BLOCK, :].astype(dv_ref.dtype)
        dv_ref[3 * BLOCK:4 * BLOCK, :] = (vacc[4 * BLOCK:5 * BLOCK, :] + vcar[...]).astype(dv_ref.dtype)
        kcar[...] = kacc[0:BLOCK, :]
        vcar[...] = vacc[0:BLOCK, :]

    q, cur, prev, sink, lse_spec = _swa_in_specs(True, nsb)
    return pl.pallas_call(
        body, name="swa_bwd", grid=(SWA_KV_HEADS, nsb),
        in_specs=[q, cur, prev, cur, prev, sink, q, q, lse_spec],
        out_specs=[q, cur, cur, sink],
        out_shape=[jax.ShapeDtypeStruct((s_, SWA_HEADS * LANES), F32), jax.ShapeDtypeStruct((s_, SWA_KV_HEADS * LANES), F32),
                   jax.ShapeDtypeStruct((s_, SWA_KV_HEADS * LANES), MXU_DTYPE),
                   jax.ShapeDtypeStruct((SWA_KV_HEADS, SUBLANES, LANES), F32)],
        scratch_shapes=[pltpu.VMEM((5 * BLOCK, LANES), MXU_DTYPE), pltpu.VMEM((5 * BLOCK, LANES), MXU_DTYPE),
                        pltpu.VMEM((5 * BLOCK, LANES), F32), pltpu.VMEM((5 * BLOCK, LANES), F32),
                        pltpu.VMEM((BLOCK, LANES), F32), pltpu.VMEM((BLOCK, LANES), F32)],
        compiler_params=_cparams(("arbitrary", "arbitrary")),
    )(qa, ka, ka, va, va, sink_b, o32, do, lse)


MLA_T = 512


def _mla_specs(s_, t):
    qs = pl.BlockSpec((t, LANES), lambda h, i: (i, h))
    kv = pl.BlockSpec((s_, LANES), lambda h, i: (0, h))
    row = pl.BlockSpec((1, 1, t), lambda h, i: (h, 0, i))
    return qs, kv, row


def _causal_scores_t(k, q, t, c2, masked):
    st = lax.dot_general(k, q, NT, preferred_element_type=F32) * c2
    if masked:
        kr = lax.broadcasted_iota(jnp.int32, (t, t), 0)
        qc = lax.broadcasted_iota(jnp.int32, (t, t), 1)
        st = jnp.where(kr <= qc, st, -jnp.inf)
    return st


def _mla_fwd(qc, kc, vp):
    s_ = qc.shape[0]
    t = min(MLA_T, s_)
    c2 = MLA_QK ** -0.5 * LOG2E

    def body(q_ref, k_ref, v_ref, o32_ref, o16_ref, lse_ref, m_s, l_s, acc_s):
        qi = pl.program_id(1)
        q = q_ref[...]
        m_s[...] = jnp.full(m_s.shape, -jnp.inf, F32)
        l_s[...] = jnp.zeros(l_s.shape, F32)
        acc_s[...] = jnp.zeros(acc_s.shape, F32)

        def step(ki, masked):
            off = pl.multiple_of(ki * t, t)
            st = _causal_scores_t(k_ref[pl.ds(off, t), :], q, t, c2, masked)
            m_old = m_s[...]
            m_new = jnp.maximum(m_old, jnp.max(st, axis=0, keepdims=True))
            alpha = jnp.exp2(m_old - m_new)
            pt = jnp.exp2(st - m_new)
            l_s[...] = alpha * l_s[...] + jnp.sum(pt, axis=0, keepdims=True)
            acc_s[...] = alpha * acc_s[...] + lax.dot_general(
                v_ref[pl.ds(off, t), :], pt.astype(MXU_DTYPE), TN, preferred_element_type=F32)
            m_s[...] = m_new

        def full_block(ki, carry):
            step(ki, False)
            return carry

        lax.fori_loop(0, qi, full_block, 0)
        step(qi, True)
        o = (acc_s[...] * (1.0 / l_s[...])).T
        o32_ref[...] = o
        o16_ref[...] = o.astype(o16_ref.dtype)
        lse_ref[0] = m_s[...] + jnp.log2(l_s[...])

    qs, kv, row = _mla_specs(s_, t)
    return pl.pallas_call(
        body, name="mla_fwd", grid=(MLA_HEADS, s_ // t), in_specs=[qs, kv, kv], out_specs=[qs, qs, row],
        out_shape=[jax.ShapeDtypeStruct((s_, MLA_HEADS * LANES), F32), jax.ShapeDtypeStruct((s_, MLA_HEADS * LANES), MXU_DTYPE),
                   jax.ShapeDtypeStruct((MLA_HEADS, 1, s_), F32)],
        scratch_shapes=[pltpu.VMEM((1, t), F32), pltpu.VMEM((1, t), F32), pltpu.VMEM((LANES, t), F32)],
        compiler_params=_cparams(("parallel", "arbitrary")),
    )(qc, kc, vp)


def _mla_bwd(qc, kc, vp, dob, lse, delta):
    s_ = qc.shape[0]
    t = min(MLA_T, s_)
    scale = MLA_QK ** -0.5
    c2 = scale * LOG2E

    def body(q_ref, do_ref, lse_ref, dl_ref, k_ref, v_ref, dq_ref, dk_ref, dv_ref, dqt_s):
        qi = pl.program_id(1)

        @pl.when(qi == 0)
        def _():
            dk_ref[...] = jnp.zeros(dk_ref.shape, F32)
            dv_ref[...] = jnp.zeros(dv_ref.shape, F32)

        q, d, l2, dl = q_ref[...], do_ref[...], lse_ref[0], dl_ref[0]
        dqt_s[...] = jnp.zeros(dqt_s.shape, F32)

        def step(ki, masked):
            off = pl.multiple_of(ki * t, t)
            k = k_ref[pl.ds(off, t), :]
            pt = jnp.exp2(_causal_scores_t(k, q, t, c2, masked) - l2)
            dpt = lax.dot_general(v_ref[pl.ds(off, t), :], d, NT, preferred_element_type=F32)
            dst = (pt * (dpt - dl) * scale).astype(MXU_DTYPE)
            dv_ref[pl.ds(off, t), :] += jnp.dot(pt.astype(MXU_DTYPE), d, preferred_element_type=F32)
            dk_ref[pl.ds(off, t), :] += jnp.dot(dst, q, preferred_element_type=F32)
            dqt_s[...] += lax.dot_general(k, dst, TN, preferred_element_type=F32)

        def full_block(ki, carry):
            step(ki, False)
            return carry

        lax.fori_loop(0, qi, full_block, 0)
        step(qi, True)
        dq_ref[...] = dqt_s[...].T

    qs, kv, row = _mla_specs(s_, t)
    shp = jax.ShapeDtypeStruct((s_, MLA_HEADS * LANES), F32)
    return pl.pallas_call(
        body, name="mla_bwd", grid=(MLA_HEADS, s_ // t), in_specs=[qs, qs, row, row, kv, kv], out_specs=[qs, kv, kv],
        out_shape=[shp, shp, shp], scratch_shapes=[pltpu.VMEM((LANES, t), F32)],
        compiler_params=_cparams(("parallel", "arbitrary")),
    )(qc, dob, lse, delta, kc, vp)


def _pad_heads(w, nh, hd, axis):
    shp = w.shape
    w = w.reshape(shp[:axis] + (nh, hd) + shp[axis + 1:])
    pad = [(0, 0)] * w.ndim
    pad[axis + 1] = (0, LANES - hd)
    w = jnp.pad(w, pad)
    return w.reshape(shp[:axis] + (nh * LANES,) + shp[axis + 1:])


def _unpad_heads(w, nh, hd, axis):
    shp = w.shape
    w = w.reshape(shp[:axis] + (nh, LANES) + shp[axis + 1:])
    w = lax.slice_in_dim(w, 0, hd, axis=axis + 1)
    return w.reshape(shp[:axis] + (nh * hd,) + shp[axis + 1:])


PACK_W = 1024
ROW_TILE = 16
FULL_SHAPE = dict(w_in=(1024, 3488), w_uq=(384, 768), w_ukv=(256, 1024), w_o_swa=(512, 1024), w_o_mla=(512, 1024),
                  w_out=(1024, 1024), w_gate=(1024, 2816), w_up=(1024, 2816), w_down=(2816, 1024))
BIG = tuple(FULL_SHAPE)
ROW_SHARDED = ("w_out", "w_down")
W_IN_COLS = FULL_SHAPE["w_in"][1] // N_DEV
W_IN_ROWS = -(-W_IN_COLS // ROW_TILE) * ROW_TILE
FF_COLS = D_FF // N_DEV
OUT_ROWS = D_MODEL // N_DEV
SMALL_ROW0 = W_IN_ROWS + OUT_ROWS
SMALL_FLAT = (("w_uq", 0, 36), ("w_ukv", 48, 32), ("w_o_swa", 80, 64), ("w_o_mla", 144, 64))
SMALL_ROWS = 208
EARLY_ROWS = SMALL_ROW0 + SMALL_ROWS
LATE_ROWS = 3 * FF_COLS
PACK_ROWS = EARLY_ROWS + LATE_ROWS


def _shard_shape(n):
    r, c = FULL_SHAPE[n]
    return (r // N_DEV, c) if n in ROW_SHARDED else (r, c // N_DEV)


def _wire_pack(sh, dtype):
    c = lambda n: sh[n].astype(dtype)
    rows = [jnp.pad(c("w_in").T, ((0, W_IN_ROWS - W_IN_COLS), (0, 0))), c("w_out")]
    for n, _, r in SMALL_FLAT:
        rows.append(jnp.pad(c(n).reshape(r, PACK_W), ((0, -r % ROW_TILE), (0, 0))))
    return jnp.concatenate(rows + [c("w_gate").T, c("w_up").T, c("w_down")], 0)


def _early_unpack(p):
    out = dict(w_in=p[0:W_IN_COLS].T, w_out=p[W_IN_ROWS:SMALL_ROW0])
    for n, off, r in SMALL_FLAT:
        out[n] = p[SMALL_ROW0 + off:SMALL_ROW0 + off + r].reshape(_shard_shape(n))
    return out


def _w_in_row_maps():
    sp = lambda col: (col // W_IN_COLS) * W_IN_ROWS + col % W_IN_COLS
    fwd = np.full((P_W,), -1, np.int64)

    def put(t0, c0, n):
        fwd[t0:t0 + n] = [sp(c) for c in range(c0, c0 + n)]

    put(P_GA, IN_OFF[6], D_MODEL)
    put(P_GB, IN_OFF[7], D_MODEL)
    for h in range(SWA_HEADS):
        put(P_Q + LANES * h, IN_OFF[0] + HEAD_DIM * h, HEAD_DIM)
    put(P_QLAT, IN_OFF[3], Q_LORA)
    put(P_KR + KR_LANE, IN_OFF[5], MLA_ROPE)
    for h in range(SWA_KV_HEADS):
        put(P_K + LANES * h, IN_OFF[1] + HEAD_DIM * h, HEAD_DIM)
        put(P_V + LANES * h, IN_OFF[2] + HEAD_DIM * h, HEAD_DIM)
    put(P_KVLAT, IN_OFF[4], KV_LORA)
    inv = np.full((N_DEV * W_IN_ROWS,), -1, np.int64)
    inv[fwd[fwd >= 0]] = np.nonzero(fwd >= 0)[0]
    return fwd, inv


def _take_rows(src, idx, *, name):
    n_out, n_src, width = len(idx), src.shape[0], src.shape[1]
    assert n_out % BLOCK == 0 and n_src % BLOCK == 0
    n_tiles = n_out // BLOCK
    blocks = [sorted({int(v) // BLOCK for v in idx[i * BLOCK:(i + 1) * BLOCK] if v >= 0}) for i in range(n_tiles)]
    k_max = max(1, max(len(b) for b in blocks))
    tab = np.zeros((n_tiles, k_max), np.int32)
    sel = np.zeros((n_tiles, k_max, BLOCK, BLOCK), np.float32)
    for i, blks in enumerate(blocks):
        for m, b in enumerate(blks):
            tab[i, m] = b
            for r in range(BLOCK):
                v = int(idx[i * BLOCK + r])
                if v >= 0 and v // BLOCK == b:
                    sel[i, m, r, v % BLOCK] = 1.0

    def body(tab_ref, sel_ref, *refs):
        o_ref = refs[k_max]
        acc = jnp.dot(sel_ref[0, 0], refs[0][...], preferred_element_type=F32)
        for m in range(1, k_max):
            acc = acc + jnp.dot(sel_ref[0, m], refs[m][...], preferred_element_type=F32)
        o_ref[...] = acc.astype(o_ref.dtype)

    def src_spec(m):
        return pl.BlockSpec((BLOCK, width), lambda i, t: (t[i * k_max + m], 0))

    return pl.pallas_call(
        body, name=name,
        grid_spec=pltpu.PrefetchScalarGridSpec(
            num_scalar_prefetch=1, grid=(n_tiles,),
            in_specs=[pl.BlockSpec((1, k_max, BLOCK, BLOCK), lambda i, t: (i, 0, 0, 0))] + [src_spec(m) for m in range(k_max)],
            out_specs=pl.BlockSpec((BLOCK, width), lambda i, t: (i, 0))),
        out_shape=jax.ShapeDtypeStruct((n_out, width), src.dtype),
        compiler_params=_cparams(("parallel",)),
    )(jnp.asarray(tab.reshape(-1)), jnp.asarray(sel, src.dtype), *([src] * k_max))


def _to_operands(win_g, wout_g, small_g):
    def full(n, off, r):
        a = small_g[:, off:off + r].reshape((N_DEV,) + _shard_shape(n))
        return jnp.moveaxis(a, 0, 1).reshape(FULL_SHAPE[n])

    w = {n: full(n, off, r) for n, off, r in SMALL_FLAT}
    ukv = w["w_ukv"].reshape(KV_LORA, MLA_HEADS, MLA_NOPE + MLA_V)
    return dict(
        winT=_take_rows(win_g.reshape(N_DEV * W_IN_ROWS, PACK_W), _w_in_row_maps()[0], name="w_in_rows"),
        wout=wout_g.reshape(D_MODEL, D_MODEL),
        wuq=_pad_heads(w["w_uq"], MLA_HEADS, MLA_QK, 1),
        wuk=_pad_heads(ukv[:, :, :MLA_NOPE].reshape(KV_LORA, -1), MLA_HEADS, MLA_NOPE, 1),
        wuv=_pad_heads(ukv[:, :, MLA_NOPE:].reshape(KV_LORA, -1), MLA_HEADS, MLA_V, 1),
        woa=_pad_heads(w["w_o_swa"], SWA_HEADS, HEAD_DIM, 0),
        wob=_pad_heads(w["w_o_mla"], MLA_HEADS, MLA_V, 0),
    )


def _grad_pack(g):
    uk = _unpad_heads(g["wukv"][:, :1024], MLA_HEADS, MLA_NOPE, 1).reshape(KV_LORA, MLA_HEADS, MLA_NOPE)
    uv = _unpad_heads(g["wukv"][:, 1024:], MLA_HEADS, MLA_V, 1).reshape(KV_LORA, MLA_HEADS, MLA_V)
    w = dict(w_uq=_unpad_heads(g["wuq"], MLA_HEADS, MLA_QK, 1), w_ukv=jnp.concatenate([uk, uv], 2).reshape(KV_LORA, -1),
             w_o_swa=_unpad_heads(g["woa"], SWA_HEADS, HEAD_DIM, 0), w_o_mla=_unpad_heads(g["wob"], MLA_HEADS, MLA_V, 0))

    def flat(n, r):
        rr, cc = FULL_SHAPE[n]
        a = jnp.moveaxis(w[n].reshape(rr, N_DEV, cc // N_DEV), 1, 0).reshape(N_DEV, r, PACK_W)
        return jnp.pad(a, ((0, 0), (0, -r % ROW_TILE), (0, 0))).astype(WIRE_DTYPE)

    return jnp.concatenate(
        [_take_rows(g["winT"], _w_in_row_maps()[1], name="dw_in_rows").reshape(N_DEV, W_IN_ROWS, PACK_W),
         g["wout"].reshape(N_DEV, OUT_ROWS, PACK_W)] + [flat(n, r) for n, _, r in SMALL_FLAT], 1)


def _local_step(x, tgt, ops, small, late_weights, late_grads):
    s_ = x.shape[0]
    tabs = _rope_tables(s_)
    sink_b = jnp.broadcast_to(small["swa_sinks"].reshape(SWA_KV_HEADS, SWA_GROUP, 1), (SWA_KV_HEADS, SWA_GROUP, LANES))
    sink_b = jnp.pad(sink_b, ((0, 0), (0, SUBLANES - SWA_GROUP), (0, 0)))

    h = _norm_fwd(x, small["mix_norm_g"], name="norm1")
    p = _mm(h, ops["winT"], "nt", name="proj_in", tm=1024, tn=2176)
    qa, ka, va, cq, ckv, kro = _attn_prep(p, small["q_norm_g"], small["kv_norm_g"], tabs)
    oa32, oa16, lse_a = _swa_fwd(qa, ka, va, sink_b)
    qp = _mm(cq, ops["wuq"], "nn", name="mla_q_up", tm=1024, tn=1024)
    kp = _mm(ckv, ops["wuk"], "nn", name="mla_k_up", tm=1024, tn=1024)
    vp = _mm(ckv, ops["wuv"], "nn", name="mla_v_up", tm=1024, tn=1024, out_dtype=MXU_DTYPE)
    qc, kc = _mla_prep(qp, kp, kro, tabs)
    ob32, ob16, lse_b = _mla_fwd(qc, kc, vp)
    ta = _mm(oa16, ops["woa"], "nn", name="o_swa", tm=1024, tn=1024)
    tb = _mm(ob16, ops["wob"], "nn", name="o_mla", tm=1024, tn=1024)
    y = _gate_fwd(p, ta, tb)
    x1 = _mm(y, ops["wout"], "nn", name="out_proj", add=x, tm=1024, tn=1024)
    wgu_t, wd = late_weights(x1)
    h2 = _norm_fwd(x1, small["ffn_norm_g"], name="norm2")
    gu = _mm(h2, wgu_t, "nt", name="ffn_in", tm=1024, tn=1408)
    act = _swiglu_fwd(gu)
    x2 = _mm(act, wd, "nn", name="ffn_out", add=x1, tn=1024)

    dx2, dx2b, dg3, _, tot = _loss_bwd(x2, small["final_norm_g"].reshape(1, D_MODEL), tgt)
    g = {}
    dact = _mm(dx2b, wd, "nt", name="d_act", tm=1024, tn=1408)
    g_wd = _mm(act, dx2b, "tn", name="dw_down", tm=1408, tn=1024, tk=1024, out_dtype=WIRE_DTYPE)
    dgu = _swiglu_bwd(gu, dact)
    dh2 = _mm(dgu, wgu_t, "nn", name="d_h2", tn=1024, tk=2816)
    g_wgu = _mm(dgu, h2, "tn", name="dw_ffn_in", tm=1408, tn=1024, tk=1024, out_dtype=WIRE_DTYPE)
    token = late_grads(g_wgu, g_wd)
    dx1, dx1b, dg2 = _norm_bwd(x1, small["ffn_norm_g"] + token[0:1, 0:1], dh2, dx2, name="norm2_bwd")
    dy = _mm(dx1b, ops["wout"], "nt", name="d_y", tm=1024, tn=1024)
    g["wout"] = _mm(y, dx1b, "tn", name="dw_out", tm=1024, tn=1024, tk=1024, out_dtype=WIRE_DTYPE)
    dta, dtb, dgab = _gate_bwd(p, ta, tb, dy)
    doa = _mm(dta, ops["woa"], "nt", name="d_oa", tm=1024, tn=1024)
    g["woa"] = _mm(oa16, dta, "tn", name="dw_o_swa", tm=1024, tn=1024, tk=1024)
    dob = _mm(dtb, ops["wob"], "nt", name="d_ob", tm=1024, tn=1024)
    g["wob"] = _mm(ob16, dtb, "tn", name="dw_o_mla", tm=1024, tn=1024, tk=1024)
    dob16, delta_b = _mla_bwd_prep(dob, ob32)
    dqc, dkc, dvp = _mla_bwd(qc, kc, vp, dob16, lse_b, delta_b)
    dqp, dkv, dkr = _mla_unprep(dqc, dkc, dvp, tabs)
    dcq = _mm(dqp, ops["wuq"], "nt", name="d_cq", tn=Q_LORA)
    g["wuq"] = _mm(cq, dqp, "tn", name="dw_uq", tm=Q_LORA, tn=1024, tk=512)
    dckv = _mm(dkv, jnp.concatenate([ops["wuk"], ops["wuv"]], 1), "nt", name="d_ckv", tn=KV_LORA)
    g["wukv"] = _mm(ckv, dkv, "tn", name="dw_ukv", tm=KV_LORA, tn=1024, tk=512)
    _, dqlat, dgq = _norm_bwd(p, small["q_norm_g"], dcq, None, name="qnorm_bwd", x_cb=P_QLAT // Q_LORA)
    _, dkvlat, dgkv = _norm_bwd(p, small["kv_norm_g"], dckv, None, name="kvnorm_bwd", x_cb=P_KVLAT // KV_LORA)
    dqa, dka, dva, dsk = _swa_bwd(qa, ka, va, sink_b, oa32, doa, lse_a)
    dq_raw, dk_raw = _swa_unrope(dqa, dka, tabs)
    dp = jnp.concatenate([dgab, dq_raw, dqlat, dkr, dk_raw, dva, dkvlat], 1)
    dh = _mm(dp, ops["winT"], "nn", name="d_h", tm=1024, tn=1024, tk=2176)
    g["winT"] = _mm(dp, h, "tn", name="dw_in", tm=2176, tn=1024, tk=1024, out_dtype=WIRE_DTYPE)
    gx, _, dg1 = _norm_bwd(x, small["mix_norm_g"], dh, dx1, name="norm1_bwd")

    sm = dict(mix_norm_g=dg1, ffn_norm_g=dg2, final_norm_g=dg3, q_norm_g=dgq, kv_norm_g=dgkv,
              swa_sinks=dsk[:, :SWA_GROUP, 0].reshape(1, SWA_HEADS))
    return tot, gx, g, sm


MESH = pl.DeviceIdType.MESH
ANY = pl.BlockSpec(memory_space=pl.ANY)


def _position():
    return lax.axis_index("x"), lax.axis_index("y"), lax.axis_index("c")


EARLY_PIECES = ((0, lambda d: (d,), 0, W_IN_ROWS), (1, lambda d: (d,), W_IN_ROWS, OUT_ROWS),
                (2, lambda d: (d,), SMALL_ROW0, SMALL_ROWS))
EARLY_PIECE_SHAPES = ((N_DEV, W_IN_ROWS, PACK_W), (N_DEV, OUT_ROWS, PACK_W), (N_DEV, SMALL_ROWS, PACK_W))


def _all_gather(block, pieces, shapes, *, name):
    n_out = len(shapes)
    n_rows = sum(p[3] for p in pieces)

    def body(x_ref, *refs):
        outs, (send_sems, recv_sems, local_sem) = refs[:n_out], refs[n_out:]
        x, y, c = _position()
        me, sibling = (x, y, c), (x, y, 1 - c)
        chips = [(1 - x, y), (x, 1 - y), (1 - x, 1 - y)]

        def dst(piece, blk):
            arr, lead, _, _ = piece
            return outs[arr].at[lead(4 * blk[0] + 2 * blk[1] + blk[2])]

        def own(piece):
            return x_ref.at[pl.ds(piece[2], piece[3])]

        def copies(k, blk, to, from_input):
            return [pltpu.make_async_remote_copy(
                src_ref=own(p) if from_input else dst(p, blk), dst_ref=dst(p, blk), send_sem=send_sems.at[k],
                recv_sem=recv_sems.at[k], device_id=to, device_id_type=MESH) for p in pieces]

        gathered_rows = x_ref.at[pl.ds(0, n_rows)]

        def whole_block(k):
            return pltpu.make_async_remote_copy(src_ref=gathered_rows, dst_ref=gathered_rows, send_sem=send_sems.at[k],
                                                recv_sem=recv_sems.at[k], device_id=me, device_id_type=MESH)

        for p in pieces:
            pltpu.make_async_copy(own(p), dst(p, me), local_sem).start()
        for cp in copies(0, me, sibling, True):
            cp.start()
        for j, chip in enumerate(chips):
            for cp in copies(1 + j, me, (*chip, c), True):
                cp.start()
        for j, chip in enumerate(chips):
            whole_block(1 + j).wait_recv()
            for cp in copies(4 + j, (*chip, c), sibling, False):
                cp.start()
        whole_block(0).wait_recv()
        for j in range(3):
            whole_block(4 + j).wait_recv()
        for k in range(7):
            whole_block(k).wait_send()
        pltpu.make_async_copy(gathered_rows, gathered_rows, local_sem).wait()

    return pl.pallas_call(
        body, name=name, out_shape=[jax.ShapeDtypeStruct(s, block.dtype) for s in shapes], in_specs=[ANY],
        out_specs=[ANY] * n_out,
        scratch_shapes=[pltpu.SemaphoreType.DMA((7,)), pltpu.SemaphoreType.DMA((7,)), pltpu.SemaphoreType.DMA],
    )(block)


HBM = pl.BlockSpec(memory_space=pltpu.HBM)
SEM = pl.BlockSpec(memory_space=pltpu.SEMAPHORE)
GU_SHAPE = (2, N_DEV, FF_COLS, PACK_W)
D_SHAPE = (N_DEV, FF_COLS, PACK_W)
LAND_SHAPE = (N_DEV, LATE_ROWS, PACK_W)


def _split_params():
    return pltpu.CompilerParams(has_side_effects=pltpu.SideEffectType.DATAFLOW_SIDE_EFFECTING)


def _peer(x, y, c, k):
    return ((1 - x) if k & 4 else x, (1 - y) if k & 2 else y, (1 - c) if k & 1 else c)


def _empty_hbm(shape, dtype):
    return pltpu.with_memory_space_constraint(lax.empty(shape, dtype), pltpu.HBM)


def _wait_all(rows, send_sems, recv_sems, me):
    for k in range(N_DEV - 1):
        cp = pltpu.make_async_remote_copy(src_ref=rows, dst_ref=rows, send_sem=send_sems.at[k], recv_sem=recv_sems.at[k],
                                          device_id=me, device_id_type=MESH)
        cp.wait_send()
        cp.wait_recv()


def _gather_late_start(pack):
    def body(p_ref, gu_ref, d_ref, send_sems, recv_sems, p_thru, gu_thru, d_thru, token):
        x, y, c = _position()
        me = 4 * x + 2 * y + c
        for k in range(1, N_DEV):
            for i, dst in enumerate((gu_ref.at[0, me], gu_ref.at[1, me], d_ref.at[me])):
                pltpu.make_async_remote_copy(
                    src_ref=p_ref.at[pl.ds(EARLY_ROWS + i * FF_COLS, FF_COLS)], dst_ref=dst, send_sem=send_sems.at[k - 1],
                    recv_sem=recv_sems.at[k - 1], device_id=_peer(x, y, c, k), device_id_type=MESH).start()
        token[...] = jnp.zeros_like(token)

    sems, dt = pltpu.SemaphoreType.DMA((N_DEV - 1,)), pack.dtype
    return pl.pallas_call(
        body, name="ag_late_start",
        out_shape=(sems, sems, pltpu.HBM(pack.shape, dt), pltpu.HBM(GU_SHAPE, dt), pltpu.HBM(D_SHAPE, dt),
                   jax.ShapeDtypeStruct((SUBLANES, LANES), F32)),
        in_specs=(HBM, HBM, HBM), out_specs=(SEM, SEM, HBM, HBM, HBM, pl.BlockSpec(memory_space=pltpu.VMEM)),
        input_output_aliases={0: 2, 1: 3, 2: 4}, compiler_params=_split_params(),
    )(pltpu.with_memory_space_constraint(pack, pltpu.HBM), _empty_hbm(GU_SHAPE, dt), _empty_hbm(D_SHAPE, dt))


def _gather_late_wait(send_sems, recv_sems, pack, gu, d, after):
    def body(p_ref, gu_ref, d_ref, send_sems, recv_sems, after_ref, p_out, gu_out, d_out):
        _wait_all(p_ref.at[pl.ds(EARLY_ROWS, LATE_ROWS)], send_sems, recv_sems, _position())

    _, gu, d = pl.pallas_call(
        body, name="ag_late_wait",
        out_shape=(pltpu.HBM(pack.shape, pack.dtype), pltpu.HBM(gu.shape, gu.dtype), pltpu.HBM(d.shape, d.dtype)),
        in_specs=(HBM, HBM, HBM, SEM, SEM, ANY), out_specs=(HBM, HBM, HBM), input_output_aliases={0: 0, 1: 1, 2: 2},
        compiler_params=_split_params(),
    )(pack, gu, d, send_sems, recv_sems, after)
    return gu, d


def _scatter_late_start(g_gu, g_d):
    def body(gu_ref, d_ref, land_ref, send_sems, recv_sems, gu_thru, d_thru, land_thru, token):
        x, y, c = _position()
        me = 4 * x + 2 * y + c
        for k in range(1, N_DEV):
            px, py, pc = _peer(x, y, c, k)
            dest = 4 * px + 2 * py + pc
            for i, src in enumerate((gu_ref.at[0, dest], gu_ref.at[1, dest], d_ref.at[dest])):
                pltpu.make_async_remote_copy(
                    src_ref=src, dst_ref=land_ref.at[me, pl.ds(i * FF_COLS, FF_COLS)], send_sem=send_sems.at[k - 1],
                    recv_sem=recv_sems.at[k - 1], device_id=(px, py, pc), device_id_type=MESH).start()
        token[...] = jnp.zeros_like(token)

    sems, dt = pltpu.SemaphoreType.DMA((N_DEV - 1,)), g_gu.dtype
    return pl.pallas_call(
        body, name="rs_late_start",
        out_shape=(sems, sems, pltpu.HBM(GU_SHAPE, dt), pltpu.HBM(D_SHAPE, dt), pltpu.HBM(LAND_SHAPE, dt),
                   jax.ShapeDtypeStruct((SUBLANES, LANES), F32)),
        in_specs=(HBM, HBM, HBM), out_specs=(SEM, SEM, HBM, HBM, HBM, pl.BlockSpec(memory_space=pltpu.VMEM)),
        input_output_aliases={0: 2, 1: 3, 2: 4}, compiler_params=_split_params(),
    )(pltpu.with_memory_space_constraint(g_gu, pltpu.HBM), pltpu.with_memory_space_constraint(g_d, pltpu.HBM),
      _empty_hbm(LAND_SHAPE, dt))


def _scatter_late_wait(send_sems, recv_sems, g_gu, g_d, land, after):
    def body(gu_ref, d_ref, land_ref, send_sems, recv_sems, after_ref, gu_out, d_out, land_out):
        _wait_all(land_ref.at[0], send_sems, recv_sems, _position())

    return pl.pallas_call(
        body, name="rs_late_wait",
        out_shape=(pltpu.HBM(g_gu.shape, g_gu.dtype), pltpu.HBM(g_d.shape, g_d.dtype), pltpu.HBM(land.shape, land.dtype)),
        in_specs=(HBM, HBM, HBM, SEM, SEM, ANY), out_specs=(HBM, HBM, HBM), input_output_aliases={0: 0, 1: 1, 2: 2},
        compiler_params=_split_params(),
    )(g_gu, g_d, land, send_sems, recv_sems, after)


def _late_sum(own, own_lead, land, piece, idx, *, name):
    lead_rank = own.ndim - 2

    def body(idx_ref, own_ref, *refs):
        o_ref = refs[N_DEV - 1]
        acc = own_ref[(0,) * lead_rank].astype(F32)
        for k in range(N_DEV - 1):
            acc = acc + refs[k][0].astype(F32)
        o_ref[...] = acc

    own_spec = pl.BlockSpec((1,) * lead_rank + (FF_COLS, PACK_W), lambda i, t: own_lead(t[0]) + (0, 0))

    def land_spec(k):
        return pl.BlockSpec((1, FF_COLS, PACK_W), lambda i, t: (t[k + 1], piece, 0))

    return pl.pallas_call(
        body, name=name,
        grid_spec=pltpu.PrefetchScalarGridSpec(
            num_scalar_prefetch=1, grid=(1,), in_specs=[own_spec] + [land_spec(k) for k in range(N_DEV - 1)],
            out_specs=pl.BlockSpec((FF_COLS, PACK_W), lambda i, t: (0, 0))),
        out_shape=jax.ShapeDtypeStruct((FF_COLS, PACK_W), F32), compiler_params=_cparams(("arbitrary",)),
    )(idx, own, *([land] * (N_DEV - 1)))


def _exchange_sibling(g, *, name):
    _, r, c_ = g.shape

    def body(g_ref, land_ref, send_sems, recv_sems):
        x, y, c = _position()
        copies = [pltpu.make_async_remote_copy(
            src_ref=g_ref.at[2 * j + (1 - c)], dst_ref=land_ref.at[j], send_sem=send_sems.at[j], recv_sem=recv_sems.at[j],
            device_id=(x, y, 1 - c), device_id_type=MESH) for j in range(4)]
        for cp in copies:
            cp.start()
        for cp in copies:
            cp.wait_recv()
        for cp in copies:
            cp.wait_send()

    return pl.pallas_call(
        body, name=name, out_shape=jax.ShapeDtypeStruct((4, r, c_), g.dtype), in_specs=[ANY], out_specs=ANY,
        scratch_shapes=[pltpu.SemaphoreType.DMA((4,)), pltpu.SemaphoreType.DMA((4,))],
    )(g)


def _exchange_chips(p, *, name):
    _, r, c_ = p.shape

    def body(p_ref, land_ref, send_sems, recv_sems, local_sem):
        x, y, c = _position()
        mine = 2 * x + y
        own = pltpu.make_async_copy(p_ref.at[mine], land_ref.at[mine], local_sem)
        own.start()
        copies = []
        for k, (px, py) in enumerate([(1 - x, y), (x, 1 - y), (1 - x, 1 - y)]):
            copies.append(pltpu.make_async_remote_copy(
                src_ref=p_ref.at[2 * px + py], dst_ref=land_ref.at[mine], send_sem=send_sems.at[k], recv_sem=recv_sems.at[k],
                device_id=(px, py, c), device_id_type=MESH))
        for cp in copies:
            cp.start()
        for cp in copies:
            cp.wait_recv()
        for cp in copies:
            cp.wait_send()
        own.wait()

    return pl.pallas_call(
        body, name=name, out_shape=jax.ShapeDtypeStruct((4, r, c_), p.dtype), in_specs=[ANY], out_specs=ANY,
        scratch_shapes=[pltpu.SemaphoreType.DMA((3,)), pltpu.SemaphoreType.DMA((3,)), pltpu.SemaphoreType.DMA],
    )(p)


def _pair_add(g, land, c_idx):
    _, r, c_ = g.shape
    tr = r

    def body(c_ref, g_ref, l_ref, o_ref):
        o_ref[...] = (g_ref[...].astype(F32) + l_ref[...].astype(F32)).astype(o_ref.dtype)

    return pl.pallas_call(
        body, name="rs_pair_add",
        grid_spec=pltpu.PrefetchScalarGridSpec(
            num_scalar_prefetch=1, grid=(4, r // tr),
            in_specs=[pl.BlockSpec((1, tr, c_), lambda j, i, cr: (2 * j + cr[0], i, 0)),
                      pl.BlockSpec((1, tr, c_), lambda j, i, cr: (j, i, 0))],
            out_specs=pl.BlockSpec((1, tr, c_), lambda j, i, cr: (j, i, 0))),
        out_shape=jax.ShapeDtypeStruct((4, r, c_), g.dtype),
        compiler_params=_cparams(("parallel", "parallel")),
    )(c_idx, g, land)


def _adamw(w, g, m, v):
    m = ADAM_B1 * m + (1.0 - ADAM_B1) * g
    v = ADAM_B2 * v + (1.0 - ADAM_B2) * (g * g)
    m_hat = m / (1.0 - ADAM_B1 ** ADAM_STEP)
    v_hat = v / (1.0 - ADAM_B2 ** ADAM_STEP)
    delta = -ADAM_LR * (m_hat / (jnp.sqrt(v_hat) + ADAM_EPS) + ADAM_WD * w)
    return delta, m, v


def _chip_sum(land):
    _, r, c_ = land.shape
    tr = r

    def body(l_ref, g_ref):
        g = l_ref[0].astype(F32)
        for j in range(1, 4):
            g = g + l_ref[j].astype(F32)
        g_ref[...] = g

    return pl.pallas_call(
        body, name="rs_chip_sum", grid=(r // tr,), in_specs=[pl.BlockSpec((4, tr, c_), lambda i: (0, i, 0))],
        out_specs=pl.BlockSpec((tr, c_), lambda i: (i, 0)), out_shape=jax.ShapeDtypeStruct((r, c_), F32),
        compiler_params=_cparams(("parallel",)),
    )(land)


def _adamw_call(w, g, m, v, *, name, max_rows=256):
    r, c_ = w.shape
    tr = max_rows if r > max_rows and r % max_rows == 0 else r

    def body(w_ref, g_ref, m_ref, v_ref, d_ref, mo_ref, vo_ref):
        d, mn, vn = _adamw(w_ref[...], g_ref[...], m_ref[...], v_ref[...])
        d_ref[...] = d
        mo_ref[...] = mn
        vo_ref[...] = vn

    row = pl.BlockSpec((tr, c_), lambda i: (i, 0))
    shp = jax.ShapeDtypeStruct((r, c_), F32)
    return pl.pallas_call(
        body, name=name, grid=(r // tr,), in_specs=[row] * 4, out_specs=[row] * 3, out_shape=[shp] * 3,
        compiler_params=_cparams(("parallel",)),
    )(w, g, m, v)


SMALL = ("mix_norm_g", "ffn_norm_g", "final_norm_g", "q_norm_g", "kv_norm_g", "swa_sinks")
SMALL_W = dict(mix_norm_g=1024, ffn_norm_g=1024, final_norm_g=1024, q_norm_g=Q_LORA, kv_norm_g=KV_LORA, swa_sinks=SWA_HEADS)


def _small_adamw(parts, w, m, v):
    n_par = parts.shape[1] // SUBLANES

    def body(p_ref, w_ref, m_ref, v_ref, g_ref, d_ref, mo_ref, vo_ref):
        tot = p_ref[0]
        for dev in range(1, N_DEV):
            tot = tot + p_ref[dev]
        row_id = lax.broadcasted_iota(jnp.int32, (SUBLANES, PACK_W), 0)
        g = jnp.zeros((SUBLANES, PACK_W), F32)
        for k in range(n_par):
            g = jnp.where(row_id == k, jnp.sum(tot[k * SUBLANES:(k + 1) * SUBLANES, :], axis=0, keepdims=True), g)
        d, mn, vn = _adamw(w_ref[...], g, m_ref[...], v_ref[...])
        g_ref[...] = g
        d_ref[...] = d
        mo_ref[...] = mn
        vo_ref[...] = vn

    shp = jax.ShapeDtypeStruct((SUBLANES, PACK_W), F32)
    vm = pl.BlockSpec(memory_space=pltpu.VMEM)
    return pl.pallas_call(body, name="small_adamw", in_specs=[vm] * 4, out_specs=[vm] * 4, out_shape=[shp] * 4)(parts, w, m, v)


def _small_pack(d, rows_each):
    parts = [jnp.pad(d[n].astype(F32), ((0, 0), (0, PACK_W - SMALL_W[n]))) for n in SMALL]
    out = jnp.concatenate(parts, 0)
    pad = -out.shape[0] % SUBLANES
    return jnp.pad(out, ((0, pad), (0, 0)))


def kernel(x, mix_norm_g, w_in, swa_sinks, q_norm_g, w_uq, kv_norm_g, w_ukv, w_o_swa, w_o_mla, w_out, ffn_norm_g, w_gate, w_up, w_down, final_norm_g, loss_target, m_mix_norm_g, m_w_in, m_swa_sinks, m_q_norm_g, m_w_uq, m_kv_norm_g, m_w_ukv, m_w_o_swa, m_w_o_mla, m_w_out, m_ffn_norm_g, m_w_gate, m_w_up, m_w_down, m_final_norm_g, v_mix_norm_g, v_w_in, v_swa_sinks, v_q_norm_g, v_w_uq, v_kv_norm_g, v_w_ukv, v_w_o_swa, v_w_o_mla, v_w_out, v_ffn_norm_g, v_w_gate, v_w_up, v_w_down, v_final_norm_g):
    big_w = dict(w_in=w_in[0], w_uq=w_uq[0], w_ukv=w_ukv[0], w_o_swa=w_o_swa[0], w_o_mla=w_o_mla[0], w_out=w_out[0],
                 w_gate=w_gate[0], w_up=w_up[0], w_down=w_down[0])
    big_m = dict(w_in=m_w_in[0], w_uq=m_w_uq[0], w_ukv=m_w_ukv[0], w_o_swa=m_w_o_swa[0], w_o_mla=m_w_o_mla[0],
                 w_out=m_w_out[0], w_gate=m_w_gate[0], w_up=m_w_up[0], w_down=m_w_down[0])
    big_v = dict(w_in=v_w_in[0], w_uq=v_w_uq[0], w_ukv=v_w_ukv[0], w_o_swa=v_w_o_swa[0], w_o_mla=v_w_o_mla[0],
                 w_out=v_w_out[0], w_gate=v_w_gate[0], w_up=v_w_up[0], w_down=v_w_down[0])
    small_w = dict(mix_norm_g=mix_norm_g, ffn_norm_g=ffn_norm_g, final_norm_g=final_norm_g.reshape(1, D_MODEL),
                   q_norm_g=q_norm_g, kv_norm_g=kv_norm_g, swa_sinks=swa_sinks)
    small_m = dict(mix_norm_g=m_mix_norm_g, ffn_norm_g=m_ffn_norm_g, final_norm_g=m_final_norm_g.reshape(1, D_MODEL),
                   q_norm_g=m_q_norm_g, kv_norm_g=m_kv_norm_g, swa_sinks=m_swa_sinks)
    small_v = dict(mix_norm_g=v_mix_norm_g, ffn_norm_g=v_ffn_norm_g, final_norm_g=v_final_norm_g.reshape(1, D_MODEL),
                   q_norm_g=v_q_norm_g, kv_norm_g=v_kv_norm_g, swa_sinks=v_swa_sinks)

    px, py, pc = _position()
    me = 4 * px + 2 * py + pc
    idx = jnp.stack([me] + [4 * qx + 2 * qy + qc for qx, qy, qc in (_peer(px, py, pc, k) for k in range(1, N_DEV))])
    idx = idx.astype(jnp.int32)

    pack = _wire_pack(big_w, WIRE_DTYPE)
    ops = _to_operands(*_all_gather(pack, EARLY_PIECES, EARLY_PIECE_SHAPES, name="ag_early"))
    ag_send, ag_recv, pack_thru, gu_buf, d_buf, ag_token = _gather_late_start(pack)
    own_gu = pack[EARLY_ROWS:EARLY_ROWS + 2 * FF_COLS].reshape(2, 1, FF_COLS, PACK_W)
    own_d = pack[EARLY_ROWS + 2 * FF_COLS:].reshape(1, FF_COLS, PACK_W)

    def late_weights(after):
        gu, d = _gather_late_wait(ag_send, ag_recv, pack_thru, gu_buf, d_buf, after)
        gu = lax.dynamic_update_slice(gu, own_gu, (0, me, 0, 0))
        d = lax.dynamic_update_slice(d, own_d, (me, 0, 0))
        return gu.reshape(2 * D_FF, D_MODEL), d.reshape(D_FF, D_MODEL)

    rs_state = []

    def late_grads(g_gu, g_d):
        *state, token = _scatter_late_start(g_gu.reshape(GU_SHAPE), g_d.reshape(D_SHAPE))
        rs_state.extend(state)
        return token

    first_w = dict(small_w, mix_norm_g=mix_norm_g + ag_token[0:1, 0:1])
    loss_tot, gx, g_ops, g_small = _local_step(x[0], loss_target[0], ops, first_w, late_weights, late_grads)

    g_gu, g_d, land_late = _scatter_late_wait(*rs_state, g_ops["winT"])
    gw = dict(w_gate=_late_sum(g_gu, lambda m: (0, m), land_late, 0, idx, name="rs_late_gate").T,
              w_up=_late_sum(g_gu, lambda m: (1, m), land_late, 1, idx, name="rs_late_up").T,
              w_down=_late_sum(g_d, lambda m: (m,), land_late, 2, idx, name="rs_late_down"))

    g_pack = _grad_pack(g_ops)
    c_idx = pc.astype(jnp.int32).reshape(1)
    land = _exchange_chips(_pair_add(g_pack, _exchange_sibling(g_pack, name="rs_sibling"), c_idx), name="rs_chips")
    gw.update(_early_unpack(_chip_sum(land)))
    dw, mw, vw = {}, {}, {}
    for n in BIG:
        dw[n], mw[n], vw[n] = _adamw_call(big_w[n], gw[n], big_m[n], big_v[n], name="adamw_" + n)

    loss_rows = jnp.pad(loss_tot[0:1, 0:1], ((0, SUBLANES - 1), (0, PACK_W - 1)))
    small_rows = jnp.concatenate([_small_pack(g_small_rows(g_small), SUBLANES), loss_rows], 0)
    parts, = _all_gather(small_rows, ((0, lambda d: (d,), 0, small_rows.shape[0]),), ((N_DEV,) + small_rows.shape,),
                         name="ag_small")
    gs, ds, ms, vs = _small_adamw(parts, _small_pack(small_w, 1), _small_pack(small_m, 1), _small_pack(small_v, 1))
    loss = gs[len(SMALL), 0]

    def small_out(packed):
        out = {}
        for k, n in enumerate(SMALL):
            out[n] = packed[k:k + 1, :SMALL_W[n]]
        out["final_norm_g"] = out["final_norm_g"].reshape(D_MODEL)
        return out

    gs, ds, ms, vs = small_out(gs), small_out(ds), small_out(ms), small_out(vs)

    order = ("mix_norm_g", "w_in", "swa_sinks", "q_norm_g", "w_uq", "kv_norm_g", "w_ukv", "w_o_swa", "w_o_mla", "w_out",
             "ffn_norm_g", "w_gate", "w_up", "w_down", "final_norm_g")

    def leaves(big, small):
        return [big[n][None] if n in big else small[n] for n in order]

    return (loss, gx[None], *leaves(gw, gs), *leaves(dw, ds), *leaves(mw, ms), *leaves(vw, vs))


def g_small_rows(g_small):
    out = dict(g_small)
    out["swa_sinks"] = jnp.pad(g_small["swa_sinks"], ((0, SUBLANES - 1), (0, 0)))
    return out
```

```python
import types

import numpy as np
import jax
import jax.numpy as jnp
from jax import lax
from jax.experimental import pallas as pl
from jax.experimental.pallas import tpu as pltpu

F32 = jnp.float32
MXU_DTYPE = jnp.bfloat16
WIRE_DTYPE = jnp.bfloat16

D_MODEL = 1024
EPS = 1e-6
ROPE_THETA = 10000.0
BLOCK = 128
HEAD_DIM = 64
SWA_HEADS = 8
SWA_KV_HEADS = 2
SWA_GROUP = SWA_HEADS // SWA_KV_HEADS
MLA_HEADS = 8
MLA_NOPE = 64
MLA_ROPE = 32
MLA_V = 64
MLA_QK = MLA_NOPE + MLA_ROPE
Q_LORA = 384
KV_LORA = 256
D_FF = 2816
IN_SIZES = (512, 128, 128, Q_LORA, KV_LORA, MLA_ROPE, D_MODEL, D_MODEL)
IN_OFF = tuple(int(v) for v in np.cumsum((0,) + IN_SIZES))
ADAM_LR, ADAM_B1, ADAM_B2, ADAM_EPS, ADAM_WD, ADAM_STEP = 0.001, 0.9, 0.999, 1e-08, 0.01, 10

LANES = 128
SUBLANES = 8
VMEM_LIMIT = 48 * 1024 * 1024
N_DEV = 8
AXES = ("x", "y", "c")

P_GA, P_GB, P_Q, P_QLAT, P_KR, P_K, P_V, P_KVLAT, P_W = 0, 1024, 2048, 3072, 3456, 3584, 3840, 4096, 4352
KR_LANE = 64

LOG2E = 1.4426950408889634

NT = (((1,), (1,)), ((), ()))
NN = (((1,), (0,)), ((), ()))
TN = (((0,), (0,)), ((), ()))


def _cparams(sem):
    return pltpu.CompilerParams(dimension_semantics=sem, vmem_limit_bytes=VMEM_LIMIT)


def _mm(a, b, mode, *, name, out_dtype=F32, add=None, after=None, tm=512, tn=512, tk=None):
    if mode == "nn":
        (M, K), (K2, N) = a.shape, b.shape
    elif mode == "nt":
        (M, K), (N, K2) = a.shape, b.shape
    else:
        (K, M), (K2, N) = a.shape, b.shape
    assert K == K2, (a.shape, b.shape, mode)
    tk = K if tk is None else tk
    tm, tn = min(tm, M), min(tn, N)
    assert M % tm == 0 and N % tn == 0 and K % tk == 0, (M, N, K, tm, tn, tk)
    nk = K // tk
    dn = {"nn": NN, "nt": NT, "tn": TN}[mode]
    if mode == "tn":
        a_spec = pl.BlockSpec((tk, tm), lambda i, j, k: (k, i))
    else:
        a_spec = pl.BlockSpec((tm, tk), lambda i, j, k: (i, k))
    if mode == "nt":
        b_spec = pl.BlockSpec((tn, tk), lambda i, j, k: (j, k))
    else:
        b_spec = pl.BlockSpec((tk, tn), lambda i, j, k: (k, j))
    o_spec = pl.BlockSpec((tm, tn), lambda i, j, k: (i, j))
    has_add, has_after = add is not None, after is not None

    def body(*refs):
        a_ref, b_ref = refs[0], refs[1]
        add_ref = refs[2] if has_add else None
        o_ref = refs[2 + has_add + has_after]
        p = lax.dot_general(a_ref[...], b_ref[...], dn, preferred_element_type=F32)

        def finish(acc):
            if has_add:
                acc = acc + add_ref[...]
            o_ref[...] = acc.astype(o_ref.dtype)

        if nk == 1:
            finish(p)
        else:
            acc_ref = refs[-1]
            k = pl.program_id(2)

            @pl.when(k == 0)
            def _():
                acc_ref[...] = p

            @pl.when(k > 0)
            def _():
                acc_ref[...] += p

            @pl.when(k == nk - 1)
            def _():
                finish(acc_ref[...])

    ins = [a, b] + ([add] if has_add else []) + ([after] if has_after else [])
    in_specs = [a_spec, b_spec] + ([o_spec] if has_add else []) + ([pl.BlockSpec(memory_space=pl.ANY)] if has_after else [])
    return pl.pallas_call(
        body, name=name, grid=(M // tm, N // tn, nk), in_specs=in_specs, out_specs=o_spec,
        out_shape=jax.ShapeDtypeStruct((M, N), out_dtype),
        scratch_shapes=[pltpu.VMEM((tm, tn), F32)] if nk > 1 else [],
        compiler_params=_cparams(("parallel", "parallel", "arbitrary")),
    )(*ins)


def _rows(ts, w, cb=0):
    return pl.BlockSpec((ts, w), lambda i: (i, cb))


def _const(r, w):
    return pl.BlockSpec((r, w), lambda i: (0, 0))


def _sublane_sum(v):
    ts, c = v.shape
    return jnp.sum(v.reshape(ts // SUBLANES, SUBLANES, c), axis=0)


def _sigmoid(v):
    return 1.0 / (1.0 + jnp.exp(-v))


def _rope(v, cos, s_up, s_dn, up, dn):
    return v * cos + pltpu.roll(v, up, 1) * s_up + pltpu.roll(v, dn, 1) * s_dn


def _rope_t(dv, cos, s_up, s_dn, up, dn):
    return dv * cos + pltpu.roll(dv * s_up, dn, 1) + pltpu.roll(dv * s_dn, up, 1)


def _rope_tables(seq):
    pos = np.arange(seq, dtype=np.float32)[:, None]

    def base(dim):
        inv = np.float32(ROPE_THETA) ** (-np.arange(0, dim, 2, dtype=np.float32) / np.float32(dim))
        ang = (pos * inv.astype(np.float32)[None, :]).astype(np.float32)
        return np.cos(ang).astype(np.float32), np.sin(ang).astype(np.float32)

    z = lambda n: np.zeros((seq, n), np.float32)
    ca, sa = base(HEAD_DIM)
    a_cos = np.concatenate([ca, ca, z(64)], 1)
    a_up = np.concatenate([-sa, z(96)], 1)
    a_dn = np.concatenate([z(32), sa, z(64)], 1)
    cb, sb = base(MLA_ROPE)
    one = np.ones((seq, 64), np.float32)
    q_cos = np.concatenate([one, cb, cb, z(32)], 1)
    k_cos = np.concatenate([z(64), cb, cb, z(32)], 1)
    b_up = np.concatenate([z(64), -sb, z(48)], 1)
    b_dn = np.concatenate([z(80), sb, z(32)], 1)
    return tuple(jnp.asarray(t) for t in (a_cos, a_up, a_dn, q_cos, k_cos, b_up, b_dn))


def _norm_fwd(x, g, *, name, ts=256):
    s_, c = x.shape

    def body(x_ref, g_ref, h_ref):
        v = x_ref[...]
        r = lax.rsqrt(jnp.mean(v * v, axis=-1, keepdims=True) + EPS)
        h_ref[...] = (v * r * g_ref[...]).astype(h_ref.dtype)

    return pl.pallas_call(
        body, name=name, grid=(s_ // ts,), in_specs=[_rows(ts, c), _const(1, c)], out_specs=_rows(ts, c),
        out_shape=jax.ShapeDtypeStruct((s_, c), MXU_DTYPE), compiler_params=_cparams(("parallel",)),
    )(x, g)


def _norm_bwd(x, g, dy, res, *, name, ts=256, x_cb=0, x_src_w=None):
    s_ = x.shape[0]
    c = dy.shape[1]
    has_res = res is not None

    def body(*refs):
        x_ref, g_ref, dy_ref = refs[0], refs[1], refs[2]
        res_ref = refs[3] if has_res else None
        dx_ref, dxb_ref, dg_ref = refs[-3], refs[-2], refs[-1]
        v = x_ref[...]
        r = lax.rsqrt(jnp.mean(v * v, axis=-1, keepdims=True) + EPS)
        xh = v * r
        d = dy_ref[...]
        dxh = d * g_ref[...]
        dx = r * (dxh - xh * jnp.mean(dxh * xh, axis=-1, keepdims=True))
        if has_res:
            dx = dx + res_ref[...]
        dx_ref[...] = dx
        dxb_ref[...] = dx.astype(dxb_ref.dtype)

        @pl.when(pl.program_id(0) == 0)
        def _():
            dg_ref[...] = jnp.zeros(dg_ref.shape, F32)

        dg_ref[...] += _sublane_sum(d * xh)

    ins = [x, g, dy] + ([res] if has_res else [])
    in_specs = [_rows(ts, c, x_cb), _const(1, c), _rows(ts, c)] + ([_rows(ts, c)] if has_res else [])
    return pl.pallas_call(
        body, name=name, grid=(s_ // ts,), in_specs=in_specs,
        out_specs=[_rows(ts, c), _rows(ts, c), _const(SUBLANES, c)],
        out_shape=[jax.ShapeDtypeStruct((s_, c), F32), jax.ShapeDtypeStruct((s_, c), MXU_DTYPE),
                   jax.ShapeDtypeStruct((SUBLANES, c), F32)],
        compiler_params=_cparams(("arbitrary",)),
    )(*ins)


def _attn_prep(p, gq, gkv, tabs, *, ts=256):
    s_ = p.shape[0]
    a_cos, a_up, a_dn, _, k_cos, b_up, b_dn = tabs

    def body(q_ref, k_ref, v_ref, ql_ref, kvl_ref, kr_ref, gq_ref, gkv_ref, ac, au, ad, kc, bu, bd,
             qa_ref, ka_ref, va_ref, cq_ref, ckv_ref, kro_ref):
        c_, u_, d_ = ac[...], au[...], ad[...]
        for h in range(SWA_HEADS):
            sl = slice(h * LANES, (h + 1) * LANES)
            qa_ref[:, sl] = _rope(q_ref[:, sl], c_, u_, d_, 96, 32).astype(qa_ref.dtype)
        for h in range(SWA_KV_HEADS):
            sl = slice(h * LANES, (h + 1) * LANES)
            ka_ref[:, sl] = _rope(k_ref[:, sl], c_, u_, d_, 96, 32).astype(ka_ref.dtype)
        va_ref[...] = v_ref[...].astype(va_ref.dtype)
        for src, gref, dst in ((ql_ref, gq_ref, cq_ref), (kvl_ref, gkv_ref, ckv_ref)):
            v = src[...]
            r = lax.rsqrt(jnp.mean(v * v, axis=-1, keepdims=True) + EPS)
            dst[...] = (v * r * gref[...]).astype(dst.dtype)
        kro_ref[...] = _rope(kr_ref[...], kc[...], bu[...], bd[...], 112, 16)

    tab = _rows(ts, LANES)
    return pl.pallas_call(
        body, name="attn_prep", grid=(s_ // ts,),
        in_specs=[_rows(ts, 1024, P_Q // 1024), _rows(ts, 256, P_K // 256), _rows(ts, 256, P_V // 256),
                  _rows(ts, Q_LORA, P_QLAT // Q_LORA), _rows(ts, KV_LORA, P_KVLAT // KV_LORA),
                  _rows(ts, LANES, P_KR // LANES), _const(1, Q_LORA), _const(1, KV_LORA), tab, tab, tab, tab, tab, tab],
        out_specs=[_rows(ts, 1024), _rows(ts, 256), _rows(ts, 256), _rows(ts, Q_LORA), _rows(ts, KV_LORA),
                   _rows(ts, LANES)],
        out_shape=[jax.ShapeDtypeStruct((s_, 1024), MXU_DTYPE), jax.ShapeDtypeStruct((s_, 256), MXU_DTYPE),
                   jax.ShapeDtypeStruct((s_, 256), MXU_DTYPE), jax.ShapeDtypeStruct((s_, Q_LORA), MXU_DTYPE),
                   jax.ShapeDtypeStruct((s_, KV_LORA), MXU_DTYPE), jax.ShapeDtypeStruct((s_, LANES), F32)],
        compiler_params=_cparams(("parallel",)),
    )(p, p, p, p, p, p, gq, gkv, a_cos, a_up, a_dn, k_cos, b_up, b_dn)


def _mla_prep(qp, kp, kro, tabs, *, ts=256):
    s_ = qp.shape[0]
    _, _, _, q_cos, _, b_up, b_dn = tabs

    def body(q_ref, k_ref, kr_ref, qc, bu, bd, qo_ref, ko_ref):
        c_, u_, d_ = qc[...], bu[...], bd[...]
        kr = kr_ref[...]
        for h in range(MLA_HEADS):
            sl = slice(h * LANES, (h + 1) * LANES)
            qo_ref[:, sl] = _rope(q_ref[:, sl], c_, u_, d_, 112, 16).astype(qo_ref.dtype)
            ko_ref[:, sl] = (k_ref[:, sl] + kr).astype(ko_ref.dtype)

    tab = _rows(ts, LANES)
    return pl.pallas_call(
        body, name="mla_prep", grid=(s_ // ts,),
        in_specs=[_rows(ts, 1024), _rows(ts, 1024), tab, tab, tab, tab],
        out_specs=[_rows(ts, 1024), _rows(ts, 1024)],
        out_shape=[jax.ShapeDtypeStruct((s_, 1024), MXU_DTYPE)] * 2,
        compiler_params=_cparams(("parallel",)),
    )(qp, kp, kro, q_cos, b_up, b_dn)


def _mla_unprep(dqc, dkc, dvp, tabs, *, ts=256):
    s_ = dqc.shape[0]
    _, _, _, q_cos, k_cos, b_up, b_dn = tabs

    def body(dq_ref, dk_ref, dv_ref, qc, kc, bu, bd, dqo_ref, dkvo_ref, dkr_ref):
        c_, u_, d_ = qc[...], bu[...], bd[...]
        tot = jnp.zeros((ts, LANES), F32)
        for h in range(MLA_HEADS):
            sl = slice(h * LANES, (h + 1) * LANES)
            dqo_ref[:, sl] = _rope_t(dq_ref[:, sl], c_, u_, d_, 112, 16).astype(dqo_ref.dtype)
            dk = dk_ref[:, sl]
            dkvo_ref[:, sl] = dk.astype(dkvo_ref.dtype)
            tot = tot + dk
        dkvo_ref[:, 1024:2048] = dv_ref[...].astype(dkvo_ref.dtype)
        dkr_ref[...] = _rope_t(tot, kc[...], u_, d_, 112, 16).astype(dkr_ref.dtype)

    tab = _rows(ts, LANES)
    return pl.pallas_call(
        body, name="mla_unprep", grid=(s_ // ts,),
        in_specs=[_rows(ts, 1024), _rows(ts, 1024), _rows(ts, 1024), tab, tab, tab, tab],
        out_specs=[_rows(ts, 1024), _rows(ts, 2048), _rows(ts, LANES)],
        out_shape=[jax.ShapeDtypeStruct((s_, 1024), MXU_DTYPE), jax.ShapeDtypeStruct((s_, 2048), MXU_DTYPE),
                   jax.ShapeDtypeStruct((s_, LANES), MXU_DTYPE)],
        compiler_params=_cparams(("parallel",)),
    )(dqc, dkc, dvp, q_cos, k_cos, b_up, b_dn)


def _swa_unrope(dqa, dka, tabs, *, ts=256):
    s_ = dqa.shape[0]
    a_cos, a_up, a_dn = tabs[0], tabs[1], tabs[2]

    def body(dq_ref, dk_ref, ac, au, ad, dqo_ref, dko_ref):
        c_, u_, d_ = ac[...], au[...], ad[...]
        for h in range(SWA_HEADS):
            sl = slice(h * LANES, (h + 1) * LANES)
            dqo_ref[:, sl] = _rope_t(dq_ref[:, sl], c_, u_, d_, 96, 32).astype(dqo_ref.dtype)
        for h in range(SWA_KV_HEADS):
            sl = slice(h * LANES, (h + 1) * LANES)
            dko_ref[:, sl] = _rope_t(dk_ref[:, sl], c_, u_, d_, 96, 32).astype(dko_ref.dtype)

    tab = _rows(ts, LANES)
    return pl.pallas_call(
        body, name="swa_unrope", grid=(s_ // ts,),
        in_specs=[_rows(ts, 1024), _rows(ts, 256), tab, tab, tab],
        out_specs=[_rows(ts, 1024), _rows(ts, 256)],
        out_shape=[jax.ShapeDtypeStruct((s_, 1024), MXU_DTYPE), jax.ShapeDtypeStruct((s_, 256), MXU_DTYPE)],
        compiler_params=_cparams(("parallel",)),
    )(dqa, dka, a_cos, a_up, a_dn)


def _gate_fwd(p, ta, tb, *, ts=256):
    s_ = p.shape[0]

    def body(ga_ref, gb_ref, ta_ref, tb_ref, y_ref):
        y = _sigmoid(ga_ref[...]) * ta_ref[...] + _sigmoid(gb_ref[...]) * tb_ref[...]
        y_ref[...] = y.astype(y_ref.dtype)

    return pl.pallas_call(
        body, name="gate_fwd", grid=(s_ // ts,),
        in_specs=[_rows(ts, 1024, P_GA // 1024), _rows(ts, 1024, P_GB // 1024), _rows(ts, 1024), _rows(ts, 1024)],
        out_specs=_rows(ts, 1024), out_shape=jax.ShapeDtypeStruct((s_, 1024), MXU_DTYPE),
        compiler_params=_cparams(("parallel",)),
    )(p, p, ta, tb)


def _gate_bwd(p, ta, tb, dy, *, ts=256):
    s_ = p.shape[0]

    def body(ga_ref, gb_ref, ta_ref, tb_ref, dy_ref, dta_ref, dtb_ref, dg_ref):
        d = dy_ref[...]
        sa, sb = _sigmoid(ga_ref[...]), _sigmoid(gb_ref[...])
        dta_ref[...] = (d * sa).astype(dta_ref.dtype)
        dtb_ref[...] = (d * sb).astype(dtb_ref.dtype)
        dg_ref[:, 0:1024] = (d * ta_ref[...] * (sa * (1.0 - sa))).astype(dg_ref.dtype)
        dg_ref[:, 1024:2048] = (d * tb_ref[...] * (sb * (1.0 - sb))).astype(dg_ref.dtype)

    return pl.pallas_call(
        body, name="gate_bwd", grid=(s_ // ts,),
        in_specs=[_rows(ts, 1024, P_GA // 1024), _rows(ts, 1024, P_GB // 1024), _rows(ts, 1024), _rows(ts, 1024),
                  _rows(ts, 1024)],
        out_specs=[_rows(ts, 1024), _rows(ts, 1024), _rows(ts, 2048)],
        out_shape=[jax.ShapeDtypeStruct((s_, 1024), MXU_DTYPE)] * 2 + [jax.ShapeDtypeStruct((s_, 2048), MXU_DTYPE)],
        compiler_params=_cparams(("parallel",)),
    )(p, p, ta, tb, dy)


def _swiglu_fwd(gu, *, ts=256):
    s_ = gu.shape[0]

    def body(g_ref, u_ref, a_ref):
        g = g_ref[...]
        a_ref[...] = (g * _sigmoid(g) * u_ref[...]).astype(a_ref.dtype)

    return pl.pallas_call(
        body, name="swiglu_fwd", grid=(s_ // ts,), in_specs=[_rows(ts, D_FF, 0), _rows(ts, D_FF, 1)],
        out_specs=_rows(ts, D_FF), out_shape=jax.ShapeDtypeStruct((s_, D_FF), MXU_DTYPE),
        compiler_params=_cparams(("parallel",)),
    )(gu, gu)


def _swiglu_bwd(gu, da, *, ts=256):
    s_ = gu.shape[0]

    def body(g_ref, u_ref, da_ref, o_ref):
        g, u, d = g_ref[...], u_ref[...], da_ref[...]
        sg = _sigmoid(g)
        o_ref[:, 0:D_FF] = (d * u * (sg * (1.0 + g * (1.0 - sg)))).astype(o_ref.dtype)
        o_ref[:, D_FF:2 * D_FF] = (d * (g * sg)).astype(o_ref.dtype)

    return pl.pallas_call(
        body, name="swiglu_bwd", grid=(s_ // ts,), in_specs=[_rows(ts, D_FF, 0), _rows(ts, D_FF, 1), _rows(ts, D_FF)],
        out_specs=_rows(ts, 2 * D_FF), out_shape=jax.ShapeDtypeStruct((s_, 2 * D_FF), MXU_DTYPE),
        compiler_params=_cparams(("parallel",)),
    )(gu, gu, da)


def _loss_bwd(x2, g, tgt, *, ts=256):
    s_, c = x2.shape

    def body(x_ref, g_ref, t_ref, dx_ref, dxb_ref, dg_ref, lp_ref, tot_ref):
        v = x_ref[...]
        r = lax.rsqrt(jnp.mean(v * v, axis=-1, keepdims=True) + EPS)
        xh = v * r
        gg = g_ref[...]
        e = xh * gg - t_ref[...]
        do = e * (1.0 / c)
        dxh = do * gg
        dx = r * (dxh - xh * jnp.mean(dxh * xh, axis=-1, keepdims=True))
        dx_ref[...] = dx
        dxb_ref[...] = dx.astype(dxb_ref.dtype)
        i = pl.program_id(0)

        @pl.when(i == 0)
        def _():
            dg_ref[...] = jnp.zeros(dg_ref.shape, F32)
            lp_ref[...] = jnp.zeros(lp_ref.shape, F32)

        dg_ref[...] += _sublane_sum(do * xh)
        lp_ref[...] += _sublane_sum(e * e)
        tot_ref[...] = jnp.full(tot_ref.shape, (0.5 / c) * jnp.sum(lp_ref[...]), F32)

    return pl.pallas_call(
        body, name="loss_bwd", grid=(s_ // ts,), in_specs=[_rows(ts, c), _const(1, c), _rows(ts, c)],
        out_specs=[_rows(ts, c), _rows(ts, c), _const(SUBLANES, c), _const(SUBLANES, c), _const(SUBLANES, LANES)],
        out_shape=[jax.ShapeDtypeStruct((s_, c), F32), jax.ShapeDtypeStruct((s_, c), MXU_DTYPE),
                   jax.ShapeDtypeStruct((SUBLANES, c), F32), jax.ShapeDtypeStruct((SUBLANES, c), F32),
                   jax.ShapeDtypeStruct((SUBLANES, LANES), F32)],
        compiler_params=_cparams(("arbitrary",)),
    )(x2, g, tgt)


def _mla_bwd_prep(dob, o32, *, ts=256):
    s_ = dob.shape[0]

    def body(do_ref, o_ref, dob_ref, dl_ref):
        d = do_ref[...]
        dob_ref[...] = d.astype(dob_ref.dtype)
        prod = d * o_ref[...]
        for h in range(MLA_HEADS):
            dl_ref[h] = jnp.sum(prod[:, h * LANES:(h + 1) * LANES].T, axis=0, keepdims=True)

    return pl.pallas_call(
        body, name="mla_bwd_prep", grid=(s_ // ts,), in_specs=[_rows(ts, 1024), _rows(ts, 1024)],
        out_specs=[_rows(ts, 1024), pl.BlockSpec((MLA_HEADS, 1, ts), lambda i: (0, 0, i))],
        out_shape=[jax.ShapeDtypeStruct((s_, 1024), MXU_DTYPE), jax.ShapeDtypeStruct((MLA_HEADS, 1, s_), F32)],
        compiler_params=_cparams(("parallel",)),
    )(dob, o32)


SWA_T = 4 * BLOCK


def _swa_masks(sb):
    kr = lax.broadcasted_iota(jnp.int32, (2 * BLOCK, BLOCK), 0)
    qc = lax.broadcasted_iota(jnp.int32, (2 * BLOCK, BLOCK), 1)
    band = jnp.logical_and(kr > qc, kr <= qc + BLOCK)
    first = jnp.logical_and(band, kr >= BLOCK)
    return band, jnp.logical_or(first, jnp.logical_and(band, sb > 0))


def _swa_in_specs(rev, nsb):
    sbi = (lambda j: nsb - 1 - j) if rev else (lambda j: j)
    cur = pl.BlockSpec((SWA_T, LANES), lambda g, j: (sbi(j), g))
    prev = pl.BlockSpec((BLOCK, LANES), lambda g, j: (jnp.maximum(4 * sbi(j) - 1, 0), g))
    q = pl.BlockSpec((SWA_T, SWA_GROUP * LANES), lambda g, j: (sbi(j), g))
    sink = pl.BlockSpec((1, SUBLANES, LANES), lambda g, j: (g, 0, 0))
    lse = pl.BlockSpec((SWA_GROUP, 1, SWA_T), lambda g, j: (g, 0, sbi(j)))
    return q, cur, prev, sink, lse


def _swa_fwd(qa, ka, va, sink_b):
    s_ = qa.shape[0]
    nsb = s_ // SWA_T
    c2 = HEAD_DIM ** -0.5 * LOG2E

    def body(q_ref, kc_ref, kp_ref, vc_ref, vp_ref, sk_ref, o32_ref, o16_ref, lse_ref, kx, vx):
        kx[0:BLOCK, :] = kp_ref[...]
        kx[BLOCK:5 * BLOCK, :] = kc_ref[...]
        vx[0:BLOCK, :] = vp_ref[...]
        vx[BLOCK:5 * BLOCK, :] = vc_ref[...]
        band, band0 = _swa_masks(pl.program_id(1))
        for hh in range(SWA_GROUP):
            sink2 = sk_ref[0, hh:hh + 1, 0:1] * LOG2E
            cs = slice(hh * LANES, (hh + 1) * LANES)
            for b in range(4):
                rs = slice(b * BLOCK, (b + 1) * BLOCK)
                ks = slice(b * BLOCK, (b + 2) * BLOCK)
                st = lax.dot_general(kx[ks, :], q_ref[rs, cs], NT, preferred_element_type=F32) * c2
                st = jnp.where(band0 if b == 0 else band, st, -jnp.inf)
                m = jnp.maximum(jnp.max(st, axis=0, keepdims=True), sink2)
                pt = jnp.exp2(st - m)
                den = jnp.sum(pt, axis=0, keepdims=True) + jnp.exp2(sink2 - m)
                o = lax.dot_general((pt * (1.0 / den)).astype(MXU_DTYPE), vx[ks, :], TN, preferred_element_type=F32)
                o32_ref[rs, cs] = o
                o16_ref[rs, cs] = o.astype(o16_ref.dtype)
                lse_ref[hh, :, rs] = m + jnp.log2(den)

    q, cur, prev, sink, lse_spec = _swa_in_specs(False, nsb)
    return pl.pallas_call(
        body, name="swa_fwd", grid=(SWA_KV_HEADS, nsb), in_specs=[q, cur, prev, cur, prev, sink],
        out_specs=[q, q, lse_spec],
        out_shape=[jax.ShapeDtypeStruct((s_, SWA_HEADS * LANES), F32), jax.ShapeDtypeStruct((s_, SWA_HEADS * LANES), MXU_DTYPE),
                   jax.ShapeDtypeStruct((SWA_HEADS, 1, s_), F32)],
        scratch_shapes=[pltpu.VMEM((5 * BLOCK, LANES), MXU_DTYPE), pltpu.VMEM((5 * BLOCK, LANES), MXU_DTYPE)],
        compiler_params=_cparams(("parallel", "arbitrary")),
    )(qa, ka, ka, va, va, sink_b)


def _swa_bwd(qa, ka, va, sink_b, o32, do, lse):
    s_ = qa.shape[0]
    nsb = s_ // SWA_T
    scale = HEAD_DIM ** -0.5
    c2 = scale * LOG2E

    def body(q_ref, kc_ref, kp_ref, vc_ref, vp_ref, sk_ref, o_ref, do_ref, lse_ref,
             dq_ref, dk_ref, dv_ref, dsk_ref, kx, vx, kacc, vacc, kcar, vcar):
        j = pl.program_id(1)
        kx[0:BLOCK, :] = kp_ref[...]
        kx[BLOCK:5 * BLOCK, :] = kc_ref[...]
        vx[0:BLOCK, :] = vp_ref[...]
        vx[BLOCK:5 * BLOCK, :] = vc_ref[...]
        band, band0 = _swa_masks(nsb - 1 - j)
        kacc[...] = jnp.zeros(kacc.shape, F32)
        vacc[...] = jnp.zeros(vacc.shape, F32)

        @pl.when(j == 0)
        def _():
            kcar[...] = jnp.zeros(kcar.shape, F32)
            vcar[...] = jnp.zeros(vcar.shape, F32)
            dsk_ref[...] = jnp.zeros(dsk_ref.shape, F32)

        for hh in range(SWA_GROUP):
            sink2 = sk_ref[0, hh:hh + 1, 0:1] * LOG2E
            cs = slice(hh * LANES, (hh + 1) * LANES)
            dsink = jnp.zeros((1, 1), F32)
            for b in range(4):
                rs = slice(b * BLOCK, (b + 1) * BLOCK)
                ks = slice(b * BLOCK, (b + 2) * BLOCK)
                q, k2, v2 = q_ref[rs, cs], kx[ks, :], vx[ks, :]
                d = do_ref[rs, cs]
                delta = jnp.sum((d * o_ref[rs, cs]).T, axis=0, keepdims=True)
                l2 = lse_ref[hh, :, rs]
                st = lax.dot_general(k2, q, NT, preferred_element_type=F32) * c2
                pt = jnp.exp2(jnp.where(band0 if b == 0 else band, st, -jnp.inf) - l2)
                db = d.astype(MXU_DTYPE)
                dst = (pt * (lax.dot_general(v2, db, NT, preferred_element_type=F32) - delta) * scale).astype(MXU_DTYPE)
                dq_ref[rs, cs] = lax.dot_general(dst, k2, TN, preferred_element_type=F32)
                kacc[ks, :] += jnp.dot(dst, q, preferred_element_type=F32)
                vacc[ks, :] += jnp.dot(pt.astype(MXU_DTYPE), db, preferred_element_type=F32)
                dsink = dsink - jnp.sum(jnp.exp2(sink2 - l2) * delta, axis=1, keepdims=True)
            dsk_ref[0, hh:hh + 1, :] += jnp.broadcast_to(dsink, (1, LANES))

        dk_ref[0:3 * BLOCK, :] = kacc[BLOCK:4 * BLOCK, :]
        dk_ref[3 * BLOCK:4 * BLOCK, :] = kacc[4 * BLOCK:5 * BLOCK, :] + kcar[...]
        dv_ref[0:3 * BLOCK, :] = vacc[BLOCK:4 * BLOCK, :].astype(dv_ref.dtype)
        dv_ref[3 * BLOCK:4 * BLOCK, :] = (vacc[4 * BLOCK:5 * BLOCK, :] + vcar[...]).astype(dv_ref.dtype)
        kcar[...] = kacc[0:BLOCK, :]
        vcar[...] = vacc[0:BLOCK, :]

    q, cur, prev, sink, lse_spec = _swa_in_specs(True, nsb)
    return pl.pallas_call(
        body, name="swa_bwd", grid=(SWA_KV_HEADS, nsb),
        in_specs=[q, cur, prev, cur, prev, sink, q, q, lse_spec],
        out_specs=[q, cur, cur, sink],
        out_shape=[jax.ShapeDtypeStruct((s_, SWA_HEADS * LANES), F32), jax.ShapeDtypeStruct((s_, SWA_KV_HEADS * LANES), F32),
                   jax.ShapeDtypeStruct((s_, SWA_KV_HEADS * LANES), MXU_DTYPE),
                   jax.ShapeDtypeStruct((SWA_KV_HEADS, SUBLANES, LANES), F32)],
        scratch_shapes=[pltpu.VMEM((5 * BLOCK, LANES), MXU_DTYPE), pltpu.VMEM((5 * BLOCK, LANES), MXU_DTYPE),
                        pltpu.VMEM((5 * BLOCK, LANES), F32), pltpu.VMEM((5 * BLOCK, LANES), F32),
                        pltpu.VMEM((BLOCK, LANES), F32), pltpu.VMEM((BLOCK, LANES), F32)],
        compiler_params=_cparams(("arbitrary", "arbitrary")),
    )(qa, ka, ka, va, va, sink_b, o32, do, lse)


MLA_T = 512
MLA_GROUP = 2


def _mla_specs(s_, t):
    w = MLA_GROUP * LANES
    qs = pl.BlockSpec((t, w), lambda g, i: (i, g))
    kv = pl.BlockSpec((s_, w), lambda g, i: (0, g))
    row = pl.BlockSpec((MLA_GROUP, 1, t), lambda g, i: (g, 0, i))
    return qs, kv, row


def _causal_scores_t(k, q, t, c2, masked):
    st = lax.dot_general(k, q, NT, preferred_element_type=F32) * c2
    if masked:
        kr = lax.broadcasted_iota(jnp.int32, (t, t), 0)
        qc = lax.broadcasted_iota(jnp.int32, (t, t), 1)
        st = jnp.where(kr <= qc, st, -jnp.inf)
    return st


def _mla_fwd(qc, kc, vp):
    s_ = qc.shape[0]
    t = min(MLA_T, s_)
    c2 = MLA_QK ** -0.5 * LOG2E

    def body(q_ref, k_ref, v_ref, o32_ref, o16_ref, lse_ref, m_s, acc_s):
        qi = pl.program_id(1)
        m_s[...] = jnp.full(m_s.shape, -jnp.inf, F32)
        acc_s[...] = jnp.zeros(acc_s.shape, F32)
        ones_lane = lax.broadcasted_iota(jnp.int32, (t, LANES), 1) == MLA_V

        def step(ki, masked):
            off = pl.multiple_of(ki * t, t)
            for g in range(MLA_GROUP):
                cs = slice(g * LANES, (g + 1) * LANES)
                st = _causal_scores_t(k_ref[pl.ds(off, t), cs], q_ref[:, cs], t, c2, masked)
                m_old = m_s[g]
                m_new = jnp.maximum(m_old, jnp.max(st, axis=0, keepdims=True))
                alpha = jnp.exp2(m_old - m_new)
                pt = jnp.exp2(st - m_new).astype(MXU_DTYPE)
                v = v_ref[pl.ds(off, t), cs]
                v = jnp.where(ones_lane, jnp.ones((), v.dtype), v)
                acc_s[g] = alpha * acc_s[g] + lax.dot_general(v, pt, TN, preferred_element_type=F32)
                m_s[g] = m_new

        def full_block(ki, carry):
            step(ki, False)
            return carry

        lax.fori_loop(0, qi, full_block, 0)
        step(qi, True)
        for g in range(MLA_GROUP):
            cs = slice(g * LANES, (g + 1) * LANES)
            acc = acc_s[g]
            l = acc[MLA_V:MLA_V + 1, :]
            o = (acc * (1.0 / l)).T
            o32_ref[:, cs] = o
            o16_ref[:, cs] = o.astype(o16_ref.dtype)
            lse_ref[g] = m_s[g] + jnp.log2(l)

    qs, kv, row = _mla_specs(s_, t)
    return pl.pallas_call(
        body, name="mla_fwd", grid=(MLA_HEADS // MLA_GROUP, s_ // t), in_specs=[qs, kv, kv], out_specs=[qs, qs, row],
        out_shape=[jax.ShapeDtypeStruct((s_, MLA_HEADS * LANES), F32), jax.ShapeDtypeStruct((s_, MLA_HEADS * LANES), MXU_DTYPE),
                   jax.ShapeDtypeStruct((MLA_HEADS, 1, s_), F32)],
        scratch_shapes=[pltpu.VMEM((MLA_GROUP, 1, t), F32), pltpu.VMEM((MLA_GROUP, LANES, t), F32)],
        compiler_params=_cparams(("parallel", "arbitrary")),
    )(qc, kc, vp)


def _mla_bwd(qc, kc, vp, dob, lse, delta):
    s_ = qc.shape[0]
    t = min(MLA_T, s_)
    scale = MLA_QK ** -0.5
    c2 = scale * LOG2E

    def body(q_ref, do_ref, lse_ref, dl_ref, k_ref, v_ref, dq_ref, dk_ref, dv_ref, dqt_s):
        qi = pl.program_id(1)

        @pl.when(qi == 0)
        def _():
            dk_ref[...] = jnp.zeros(dk_ref.shape, F32)
            dv_ref[...] = jnp.zeros(dv_ref.shape, F32)

        dqt_s[...] = jnp.zeros(dqt_s.shape, F32)

        def step(ki, masked):
            off = pl.multiple_of(ki * t, t)
            for g in range(MLA_GROUP):
                cs = slice(g * LANES, (g + 1) * LANES)
                q, d, k = q_ref[:, cs], do_ref[:, cs], k_ref[pl.ds(off, t), cs]
                pt = jnp.exp2(_causal_scores_t(k, q, t, c2, masked) - lse_ref[g])
                dpt = lax.dot_general(v_ref[pl.ds(off, t), cs], d, NT, preferred_element_type=F32)
                dst = (pt * (dpt - dl_ref[g]) * scale).astype(MXU_DTYPE)
                dv_ref[pl.ds(off, t), cs] += jnp.dot(pt.astype(MXU_DTYPE), d, preferred_element_type=F32)
                dk_ref[pl.ds(off, t), cs] += jnp.dot(dst, q, preferred_element_type=F32)
                dqt_s[g] += lax.dot_general(k, dst, TN, preferred_element_type=F32)

        def full_block(ki, carry):
            step(ki, False)
            return carry

        lax.fori_loop(0, qi, full_block, 0)
        step(qi, True)
        for g in range(MLA_GROUP):
            dq_ref[:, g * LANES:(g + 1) * LANES] = dqt_s[g].T

    qs, kv, row = _mla_specs(s_, t)
    shp = jax.ShapeDtypeStruct((s_, MLA_HEADS * LANES), F32)
    return pl.pallas_call(
        body, name="mla_bwd", grid=(MLA_HEADS // MLA_GROUP, s_ // t), in_specs=[qs, qs, row, row, kv, kv],
        out_specs=[qs, kv, kv], out_shape=[shp, shp, shp], scratch_shapes=[pltpu.VMEM((MLA_GROUP, LANES, t), F32)],
        compiler_params=_cparams(("parallel", "arbitrary")),
    )(qc, dob, lse, delta, kc, vp)


def _pad_heads(w, nh, hd, axis):
    shp = w.shape
    w = w.reshape(shp[:axis] + (nh, hd) + shp[axis + 1:])
    pad = [(0, 0)] * w.ndim
    pad[axis + 1] = (0, LANES - hd)
    w = jnp.pad(w, pad)
    return w.reshape(shp[:axis] + (nh * LANES,) + shp[axis + 1:])


def _unpad_heads(w, nh, hd, axis):
    shp = w.shape
    w = w.reshape(shp[:axis] + (nh, LANES) + shp[axis + 1:])
    w = lax.slice_in_dim(w, 0, hd, axis=axis + 1)
    return w.reshape(shp[:axis] + (nh * hd,) + shp[axis + 1:])


PACK_W = 1024
ROW_TILE = 16
FULL_SHAPE = dict(w_in=(1024, 3488), w_uq=(384, 768), w_ukv=(256, 1024), w_o_swa=(512, 1024), w_o_mla=(512, 1024),
                  w_out=(1024, 1024), w_gate=(1024, 2816), w_up=(1024, 2816), w_down=(2816, 1024))
BIG = tuple(FULL_SHAPE)
ROW_SHARDED = ("w_out", "w_down")
W_IN_COLS = FULL_SHAPE["w_in"][1] // N_DEV
W_IN_ROWS = -(-W_IN_COLS // ROW_TILE) * ROW_TILE
FF_COLS = D_FF // N_DEV
OUT_ROWS = D_MODEL // N_DEV
SMALL_ROW0 = W_IN_ROWS + OUT_ROWS
SMALL_FLAT = (("w_uq", 0, 36), ("w_ukv", 48, 32), ("w_o_swa", 80, 64), ("w_o_mla", 144, 64))
SMALL_ROWS = 208
EARLY_ROWS = SMALL_ROW0 + SMALL_ROWS
LATE_ROWS = 3 * FF_COLS
PACK_ROWS = EARLY_ROWS + LATE_ROWS


def _shard_shape(n):
    r, c = FULL_SHAPE[n]
    return (r // N_DEV, c) if n in ROW_SHARDED else (r, c // N_DEV)


def _wire_pack(sh, dtype):
    c = lambda n: sh[n].astype(dtype)
    rows = [jnp.pad(c("w_in").T, ((0, W_IN_ROWS - W_IN_COLS), (0, 0))), c("w_out")]
    for n, _, r in SMALL_FLAT:
        rows.append(jnp.pad(c(n).reshape(r, PACK_W), ((0, -r % ROW_TILE), (0, 0))))
    return jnp.concatenate(rows + [c("w_gate").T, c("w_up").T, c("w_down")], 0)


MID_ROWS = OUT_ROWS + SMALL_ROWS


def _mid_unpack(p):
    out = dict(w_out=p[0:OUT_ROWS])
    for n, off, r in SMALL_FLAT:
        out[n] = p[OUT_ROWS + off:OUT_ROWS + off + r].reshape(_shard_shape(n))
    return out


def _w_in_row_maps():
    sp = lambda col: (col // W_IN_COLS) * W_IN_ROWS + col % W_IN_COLS
    fwd = np.full((P_W,), -1, np.int64)

    def put(t0, c0, n):
        fwd[t0:t0 + n] = [sp(c) for c in range(c0, c0 + n)]

    put(P_GA, IN_OFF[6], D_MODEL)
    put(P_GB, IN_OFF[7], D_MODEL)
    for h in range(SWA_HEADS):
        put(P_Q + LANES * h, IN_OFF[0] + HEAD_DIM * h, HEAD_DIM)
    put(P_QLAT, IN_OFF[3], Q_LORA)
    put(P_KR + KR_LANE, IN_OFF[5], MLA_ROPE)
    for h in range(SWA_KV_HEADS):
        put(P_K + LANES * h, IN_OFF[1] + HEAD_DIM * h, HEAD_DIM)
        put(P_V + LANES * h, IN_OFF[2] + HEAD_DIM * h, HEAD_DIM)
    put(P_KVLAT, IN_OFF[4], KV_LORA)
    inv = np.full((N_DEV * W_IN_ROWS,), -1, np.int64)
    inv[fwd[fwd >= 0]] = np.nonzero(fwd >= 0)[0]
    return fwd, inv


def _take_rows(src, idx, *, name):
    n_out, n_src, width = len(idx), src.shape[0], src.shape[1]
    assert n_out % BLOCK == 0 and n_src % BLOCK == 0
    n_tiles = n_out // BLOCK
    blocks = [sorted({int(v) // BLOCK for v in idx[i * BLOCK:(i + 1) * BLOCK] if v >= 0}) for i in range(n_tiles)]
    k_max = max(1, max(len(b) for b in blocks))
    tab = np.zeros((n_tiles, k_max), np.int32)
    sel = np.zeros((n_tiles, k_max, BLOCK, BLOCK), np.float32)
    for i, blks in enumerate(blocks):
        for m, b in enumerate(blks):
            tab[i, m] = b
            for r in range(BLOCK):
                v = int(idx[i * BLOCK + r])
                if v >= 0 and v // BLOCK == b:
                    sel[i, m, r, v % BLOCK] = 1.0

    def body(tab_ref, sel_ref, *refs):
        o_ref = refs[k_max]
        acc = jnp.dot(sel_ref[0, 0], refs[0][...], preferred_element_type=F32)
        for m in range(1, k_max):
            acc = acc + jnp.dot(sel_ref[0, m], refs[m][...], preferred_element_type=F32)
        o_ref[...] = acc.astype(o_ref.dtype)

    def src_spec(m):
        return pl.BlockSpec((BLOCK, width), lambda i, t: (t[i * k_max + m], 0))

    return pl.pallas_call(
        body, name=name,
        grid_spec=pltpu.PrefetchScalarGridSpec(
            num_scalar_prefetch=1, grid=(n_tiles,),
            in_specs=[pl.BlockSpec((1, k_max, BLOCK, BLOCK), lambda i, t: (i, 0, 0, 0))] + [src_spec(m) for m in range(k_max)],
            out_specs=pl.BlockSpec((BLOCK, width), lambda i, t: (i, 0))),
        out_shape=jax.ShapeDtypeStruct((n_out, width), src.dtype),
        compiler_params=_cparams(("parallel",)),
    )(jnp.asarray(tab.reshape(-1)), jnp.asarray(sel, src.dtype), *([src] * k_max))


def _w_in_operand(win_g):
    return _take_rows(win_g.reshape(N_DEV * W_IN_ROWS, PACK_W), _w_in_row_maps()[0], name="w_in_rows")


def _mid_operands(wout_g, small_g):
    def full(n, off, r):
        a = small_g[:, off:off + r].reshape((N_DEV,) + _shard_shape(n))
        return jnp.moveaxis(a, 0, 1).reshape(FULL_SHAPE[n])

    w = {n: full(n, off, r) for n, off, r in SMALL_FLAT}
    ukv = w["w_ukv"].reshape(KV_LORA, MLA_HEADS, MLA_NOPE + MLA_V)
    return dict(
        wout=wout_g.reshape(D_MODEL, D_MODEL),
        wuq=_pad_heads(w["w_uq"], MLA_HEADS, MLA_QK, 1),
        wuk=_pad_heads(ukv[:, :, :MLA_NOPE].reshape(KV_LORA, -1), MLA_HEADS, MLA_NOPE, 1),
        wuv=_pad_heads(ukv[:, :, MLA_NOPE:].reshape(KV_LORA, -1), MLA_HEADS, MLA_V, 1),
        woa=_pad_heads(w["w_o_swa"], SWA_HEADS, HEAD_DIM, 0),
        wob=_pad_heads(w["w_o_mla"], MLA_HEADS, MLA_V, 0),
    )


def _mid_grad_pack(g):
    uk = _unpad_heads(g["wukv"][:, :1024], MLA_HEADS, MLA_NOPE, 1).reshape(KV_LORA, MLA_HEADS, MLA_NOPE)
    uv = _unpad_heads(g["wukv"][:, 1024:], MLA_HEADS, MLA_V, 1).reshape(KV_LORA, MLA_HEADS, MLA_V)
    w = dict(w_uq=_unpad_heads(g["wuq"], MLA_HEADS, MLA_QK, 1), w_ukv=jnp.concatenate([uk, uv], 2).reshape(KV_LORA, -1),
             w_o_swa=_unpad_heads(g["woa"], SWA_HEADS, HEAD_DIM, 0), w_o_mla=_unpad_heads(g["wob"], MLA_HEADS, MLA_V, 0))

    def flat(n, r):
        rr, cc = FULL_SHAPE[n]
        a = jnp.moveaxis(w[n].reshape(rr, N_DEV, cc // N_DEV), 1, 0).reshape(N_DEV, r, PACK_W)
        return jnp.pad(a, ((0, 0), (0, -r % ROW_TILE), (0, 0))).astype(WIRE_DTYPE)

    return jnp.concatenate([g["wout"].reshape(N_DEV, OUT_ROWS, PACK_W)] + [flat(n, r) for n, _, r in SMALL_FLAT], 1)


def _w_in_grad_chunks(g_win_t):
    return _take_rows(g_win_t, _w_in_row_maps()[1], name="dw_in_rows").reshape(N_DEV, W_IN_ROWS, PACK_W)


def _local_step(x, tgt, win_t, small, weights, grads):
    s_ = x.shape[0]
    tabs = _rope_tables(s_)
    sink_b = jnp.broadcast_to(small["swa_sinks"].reshape(SWA_KV_HEADS, SWA_GROUP, 1), (SWA_KV_HEADS, SWA_GROUP, LANES))
    sink_b = jnp.pad(sink_b, ((0, 0), (0, SUBLANES - SWA_GROUP), (0, 0)))

    h = _norm_fwd(x, small["mix_norm_g"], name="norm1")
    p = _mm(h, win_t, "nt", name="proj_in", tm=1024, tn=2176)
    qa, ka, va, cq, ckv, kro = _attn_prep(p, small["q_norm_g"], small["kv_norm_g"], tabs)
    ops = weights.mid(cq)
    oa32, oa16, lse_a = _swa_fwd(qa, ka, va, sink_b)
    qp = _mm(cq, ops["wuq"], "nn", name="mla_q_up", tm=1024, tn=1024)
    kp = _mm(ckv, ops["wuk"], "nn", name="mla_k_up", tm=1024, tn=1024)
    vp = _mm(ckv, ops["wuv"], "nn", name="mla_v_up", tm=1024, tn=1024, out_dtype=MXU_DTYPE)
    qc, kc = _mla_prep(qp, kp, kro, tabs)
    ob32, ob16, lse_b = _mla_fwd(qc, kc, vp)
    ta = _mm(oa16, ops["woa"], "nn", name="o_swa", tm=1024, tn=1024)
    tb = _mm(ob16, ops["wob"], "nn", name="o_mla", tm=1024, tn=1024)
    y = _gate_fwd(p, ta, tb)
    x1 = _mm(y, ops["wout"], "nn", name="out_proj", add=x, tm=1024, tn=1024)
    wgu_t, wd = weights.late(x1)
    h2 = _norm_fwd(x1, small["ffn_norm_g"], name="norm2")
    gu = _mm(h2, wgu_t, "nt", name="ffn_in", tm=1024, tn=1408)
    act = _swiglu_fwd(gu)
    x2 = _mm(act, wd, "nn", name="ffn_out", add=x1, tn=1024)

    dx2, dx2b, dg3, _, tot = _loss_bwd(x2, small["final_norm_g"].reshape(1, D_MODEL), tgt)
    g = {}
    dact = _mm(dx2b, wd, "nt", name="d_act", tm=1024, tn=1408)
    g_wd = _mm(act, dx2b, "tn", name="dw_down", tm=1408, tn=1024, tk=1024, out_dtype=WIRE_DTYPE)
    dgu = _swiglu_bwd(gu, dact)
    dh2 = _mm(dgu, wgu_t, "nn", name="d_h2", tn=1024, tk=2816)
    g_wgu = _mm(dgu, h2, "tn", name="dw_ffn_in", tm=1408, tn=1024, tk=1024, out_dtype=WIRE_DTYPE)
    token = grads.late(g_wgu, g_wd)
    dx1, dx1b, dg2 = _norm_bwd(x1, small["ffn_norm_g"] + token[0:1, 0:1], dh2, dx2, name="norm2_bwd")
    dy = _mm(dx1b, ops["wout"], "nt", name="d_y", tm=1024, tn=1024)
    g["wout"] = _mm(y, dx1b, "tn", name="dw_out", tm=1024, tn=1024, tk=1024, out_dtype=WIRE_DTYPE)
    dta, dtb, dgab = _gate_bwd(p, ta, tb, dy)
    doa = _mm(dta, ops["woa"], "nt", name="d_oa", tm=1024, tn=1024)
    g["woa"] = _mm(oa16, dta, "tn", name="dw_o_swa", tm=1024, tn=1024, tk=1024)
    dob = _mm(dtb, ops["wob"], "nt", name="d_ob", tm=1024, tn=1024)
    g["wob"] = _mm(ob16, dtb, "tn", name="dw_o_mla", tm=1024, tn=1024, tk=1024)
    dob16, delta_b = _mla_bwd_prep(dob, ob32)
    dqc, dkc, dvp = _mla_bwd(qc, kc, vp, dob16, lse_b, delta_b)
    dqp, dkv, dkr = _mla_unprep(dqc, dkc, dvp, tabs)
    dcq = _mm(dqp, ops["wuq"], "nt", name="d_cq", tn=Q_LORA)
    g["wuq"] = _mm(cq, dqp, "tn", name="dw_uq", tm=Q_LORA, tn=1024, tk=512)
    dckv = _mm(dkv, jnp.concatenate([ops["wuk"], ops["wuv"]], 1), "nt", name="d_ckv", tn=KV_LORA)
    g["wukv"] = _mm(ckv, dkv, "tn", name="dw_ukv", tm=KV_LORA, tn=1024, tk=512)
    token = grads.mid(g)
    _, dqlat, dgq = _norm_bwd(p, small["q_norm_g"] + token[0:1, 0:1], dcq, None, name="qnorm_bwd", x_cb=P_QLAT // Q_LORA)
    _, dkvlat, dgkv = _norm_bwd(p, small["kv_norm_g"], dckv, None, name="kvnorm_bwd", x_cb=P_KVLAT // KV_LORA)
    dqa, dka, dva, dsk = _swa_bwd(qa, ka, va, sink_b, oa32, doa, lse_a)
    dq_raw, dk_raw = _swa_unrope(dqa, dka, tabs)
    dp = jnp.concatenate([dgab, dq_raw, dqlat, dkr, dk_raw, dva, dkvlat], 1)
    token = grads.last(_mm(dp, h, "tn", name="dw_in", tm=2176, tn=1024, tk=1024, out_dtype=WIRE_DTYPE))
    dh = _mm(dp, win_t, "nn", name="d_h", after=token, tm=1024, tn=1024, tk=2176)
    gx, _, dg1 = _norm_bwd(x, small["mix_norm_g"], dh, dx1, name="norm1_bwd")

    sm = dict(mix_norm_g=dg1, ffn_norm_g=dg2, final_norm_g=dg3, q_norm_g=dgq, kv_norm_g=dgkv,
              swa_sinks=dsk[:, :SWA_GROUP, 0].reshape(1, SWA_HEADS))
    return tot, gx, sm


MESH = pl.DeviceIdType.MESH
ANY = pl.BlockSpec(memory_space=pl.ANY)


def _position():
    return lax.axis_index("x"), lax.axis_index("y"), lax.axis_index("c")


def _all_gather(block, pieces, shapes, *, name):
    n_out = len(shapes)
    n_rows = sum(p[3] for p in pieces)

    def body(x_ref, *refs):
        outs, (send_sems, recv_sems, local_sem) = refs[:n_out], refs[n_out:]
        x, y, c = _position()
        me, sibling = (x, y, c), (x, y, 1 - c)
        chips = [(1 - x, y), (x, 1 - y), (1 - x, 1 - y)]

        def dst(piece, blk):
            arr, lead, _, _ = piece
            return outs[arr].at[lead(4 * blk[0] + 2 * blk[1] + blk[2])]

        def own(piece):
            return x_ref.at[pl.ds(piece[2], piece[3])]

        def copies(k, blk, to, from_input):
            return [pltpu.make_async_remote_copy(
                src_ref=own(p) if from_input else dst(p, blk), dst_ref=dst(p, blk), send_sem=send_sems.at[k],
                recv_sem=recv_sems.at[k], device_id=to, device_id_type=MESH) for p in pieces]

        gathered_rows = x_ref.at[pl.ds(0, n_rows)]

        def whole_block(k):
            return pltpu.make_async_remote_copy(src_ref=gathered_rows, dst_ref=gathered_rows, send_sem=send_sems.at[k],
                                                recv_sem=recv_sems.at[k], device_id=me, device_id_type=MESH)

        for p in pieces:
            pltpu.make_async_copy(own(p), dst(p, me), local_sem).start()
        for cp in copies(0, me, sibling, True):
            cp.start()
        for j, chip in enumerate(chips):
            for cp in copies(1 + j, me, (*chip, c), True):
                cp.start()
        for j, chip in enumerate(chips):
            whole_block(1 + j).wait_recv()
            for cp in copies(4 + j, (*chip, c), sibling, False):
                cp.start()
        whole_block(0).wait_recv()
        for j in range(3):
            whole_block(4 + j).wait_recv()
        for k in range(7):
            whole_block(k).wait_send()
        pltpu.make_async_copy(gathered_rows, gathered_rows, local_sem).wait()

    return pl.pallas_call(
        body, name=name, out_shape=[jax.ShapeDtypeStruct(s, block.dtype) for s in shapes], in_specs=[ANY],
        out_specs=[ANY] * n_out,
        scratch_shapes=[pltpu.SemaphoreType.DMA((7,)), pltpu.SemaphoreType.DMA((7,)), pltpu.SemaphoreType.DMA],
    )(block)


HBM = pl.BlockSpec(memory_space=pltpu.HBM)
SEM = pl.BlockSpec(memory_space=pltpu.SEMAPHORE)
GU_SHAPE = (2, N_DEV, FF_COLS, PACK_W)
D_SHAPE = (N_DEV, FF_COLS, PACK_W)
LAND_SHAPE = (N_DEV, LATE_ROWS, PACK_W)


def _split_params():
    return pltpu.CompilerParams(has_side_effects=pltpu.SideEffectType.DATAFLOW_SIDE_EFFECTING)


def _peer(x, y, c, k):
    return ((1 - x) if k & 4 else x, (1 - y) if k & 2 else y, (1 - c) if k & 1 else c)


def _empty_hbm(shape, dtype):
    return pltpu.with_memory_space_constraint(lax.empty(shape, dtype), pltpu.HBM)


def _wait_all(rows, send_sems, recv_sems, me):
    for k in range(N_DEV - 1):
        cp = pltpu.make_async_remote_copy(src_ref=rows, dst_ref=rows, send_sem=send_sems.at[k], recv_sem=recv_sems.at[k],
                                          device_id=me, device_id_type=MESH)
        cp.wait_send()
        cp.wait_recv()


def _token_shape():
    return jax.ShapeDtypeStruct((SUBLANES, LANES), F32)


def _gather_start(pack, row0, pieces, shapes, *, name):
    n = len(shapes)

    def body(*refs):
        p_ref, bufs, send_sems, recv_sems, token = refs[0], refs[1:1 + n], refs[1 + n], refs[2 + n], refs[-1]
        x, y, c = _position()
        me = 4 * x + 2 * y + c
        for k in range(1, N_DEV):
            off = row0
            for buf, lead, rows in pieces:
                pltpu.make_async_remote_copy(
                    src_ref=p_ref.at[pl.ds(off, rows)], dst_ref=bufs[buf].at[lead(me)], send_sem=send_sems.at[k - 1],
                    recv_sem=recv_sems.at[k - 1], device_id=_peer(x, y, c, k), device_id_type=MESH).start()
                off += rows
        token[...] = jnp.zeros_like(token)

    sems, dt = pltpu.SemaphoreType.DMA((N_DEV - 1,)), pack.dtype
    return pl.pallas_call(
        body, name=name,
        out_shape=(sems, sems, pltpu.HBM(pack.shape, dt)) + tuple(pltpu.HBM(s, dt) for s in shapes) + (_token_shape(),),
        in_specs=(HBM,) * (1 + n), out_specs=(SEM, SEM) + (HBM,) * (1 + n) + (pl.BlockSpec(memory_space=pltpu.VMEM),),
        input_output_aliases={i: 2 + i for i in range(1 + n)}, compiler_params=_split_params(),
    )(pltpu.with_memory_space_constraint(pack, pltpu.HBM), *[_empty_hbm(s, dt) for s in shapes])


def _gather_wait(started, row0, n_rows, after, *, name):
    send_sems, recv_sems, pack, *bufs = started[:-1]
    n = len(bufs)

    def body(*refs):
        _wait_all(refs[0].at[pl.ds(row0, n_rows)], refs[1 + n], refs[2 + n], _position())

    outs = pl.pallas_call(
        body, name=name, out_shape=tuple(pltpu.HBM(a.shape, a.dtype) for a in (pack, *bufs)),
        in_specs=(HBM,) * (1 + n) + (SEM, SEM, ANY), out_specs=(HBM,) * (1 + n),
        input_output_aliases={i: i for i in range(1 + n)}, compiler_params=_split_params(),
    )(pack, *bufs, send_sems, recv_sems, after)
    return outs[1:]


def _scatter_start(srcs, pieces, *, name):
    n = len(srcs)
    land_shape = (N_DEV, sum(p[2] for p in pieces), PACK_W)

    def body(*refs):
        src_refs, land_ref, send_sems, recv_sems, token = refs[:n], refs[n], refs[n + 1], refs[n + 2], refs[-1]
        x, y, c = _position()
        me = 4 * x + 2 * y + c
        for k in range(1, N_DEV):
            px, py, pc = _peer(x, y, c, k)
            off = 0
            for si, lead, rows in pieces:
                pltpu.make_async_remote_copy(
                    src_ref=src_refs[si].at[lead(4 * px + 2 * py + pc)], dst_ref=land_ref.at[me, pl.ds(off, rows)],
                    send_sem=send_sems.at[k - 1], recv_sem=recv_sems.at[k - 1], device_id=(px, py, pc),
                    device_id_type=MESH).start()
                off += rows
        token[...] = jnp.zeros_like(token)

    sems, dt = pltpu.SemaphoreType.DMA((N_DEV - 1,)), srcs[0].dtype
    return pl.pallas_call(
        body, name=name,
        out_shape=(sems, sems) + tuple(pltpu.HBM(a.shape, dt) for a in srcs) + (pltpu.HBM(land_shape, dt), _token_shape()),
        in_specs=(HBM,) * (n + 1), out_specs=(SEM, SEM) + (HBM,) * (n + 1) + (pl.BlockSpec(memory_space=pltpu.VMEM),),
        input_output_aliases={i: 2 + i for i in range(n + 1)}, compiler_params=_split_params(),
    )(*[pltpu.with_memory_space_constraint(a, pltpu.HBM) for a in srcs], _empty_hbm(land_shape, dt))


def _scatter_wait(started, after, *, name):
    send_sems, recv_sems, *bufs = started[:-1]
    n = len(bufs)

    def body(*refs):
        _wait_all(refs[n - 1].at[0], refs[n], refs[n + 1], _position())

    return pl.pallas_call(
        body, name=name, out_shape=tuple(pltpu.HBM(a.shape, a.dtype) for a in bufs),
        in_specs=(HBM,) * n + (SEM, SEM, ANY), out_specs=(HBM,) * n, input_output_aliases={i: i for i in range(n)},
        compiler_params=_split_params(),
    )(*bufs, send_sems, recv_sems, after)


def _peer_sum(own, own_lead, land, block, rows, idx, *, name):
    lead_rank = own.ndim - 2

    def body(idx_ref, own_ref, *refs):
        o_ref = refs[N_DEV - 1]
        acc = own_ref[(0,) * lead_rank].astype(F32)
        for k in range(N_DEV - 1):
            acc = acc + refs[k][0].astype(F32)
        o_ref[...] = acc

    own_spec = pl.BlockSpec((1,) * lead_rank + (rows, PACK_W), lambda i, t: own_lead(t[0]) + (0, 0))

    def land_spec(k):
        return pl.BlockSpec((1, rows, PACK_W), lambda i, t: (t[k + 1], block, 0))

    return pl.pallas_call(
        body, name=name,
        grid_spec=pltpu.PrefetchScalarGridSpec(
            num_scalar_prefetch=1, grid=(1,), in_specs=[own_spec] + [land_spec(k) for k in range(N_DEV - 1)],
            out_specs=pl.BlockSpec((rows, PACK_W), lambda i, t: (0, 0))),
        out_shape=jax.ShapeDtypeStruct((rows, PACK_W), F32), compiler_params=_cparams(("arbitrary",)),
    )(idx, own, *([land] * (N_DEV - 1)))


def _adamw(w, g, m, v):
    m = ADAM_B1 * m + (1.0 - ADAM_B1) * g
    v = ADAM_B2 * v + (1.0 - ADAM_B2) * (g * g)
    m_hat = m / (1.0 - ADAM_B1 ** ADAM_STEP)
    v_hat = v / (1.0 - ADAM_B2 ** ADAM_STEP)
    delta = -ADAM_LR * (m_hat / (jnp.sqrt(v_hat) + ADAM_EPS) + ADAM_WD * w)
    return delta, m, v


def _adamw_call(w, g, m, v, *, name, max_rows=256):
    r, c_ = w.shape
    tr = max_rows if r > max_rows and r % max_rows == 0 else r

    def body(w_ref, g_ref, m_ref, v_ref, d_ref, mo_ref, vo_ref):
        d, mn, vn = _adamw(w_ref[...], g_ref[...], m_ref[...], v_ref[...])
        d_ref[...] = d
        mo_ref[...] = mn
        vo_ref[...] = vn

    row = pl.BlockSpec((tr, c_), lambda i: (i, 0))
    shp = jax.ShapeDtypeStruct((r, c_), F32)
    return pl.pallas_call(
        body, name=name, grid=(r // tr,), in_specs=[row] * 4, out_specs=[row] * 3, out_shape=[shp] * 3,
        compiler_params=_cparams(("parallel",)),
    )(w, g, m, v)


SMALL = ("mix_norm_g", "ffn_norm_g", "final_norm_g", "q_norm_g", "kv_norm_g", "swa_sinks")
SMALL_W = dict(mix_norm_g=1024, ffn_norm_g=1024, final_norm_g=1024, q_norm_g=Q_LORA, kv_norm_g=KV_LORA, swa_sinks=SWA_HEADS)


def _small_adamw(parts, w, m, v):
    n_par = parts.shape[1] // SUBLANES

    def body(p_ref, w_ref, m_ref, v_ref, g_ref, d_ref, mo_ref, vo_ref):
        tot = p_ref[0]
        for dev in range(1, N_DEV):
            tot = tot + p_ref[dev]
        row_id = lax.broadcasted_iota(jnp.int32, (SUBLANES, PACK_W), 0)
        g = jnp.zeros((SUBLANES, PACK_W), F32)
        for k in range(n_par):
            g = jnp.where(row_id == k, jnp.sum(tot[k * SUBLANES:(k + 1) * SUBLANES, :], axis=0, keepdims=True), g)
        d, mn, vn = _adamw(w_ref[...], g, m_ref[...], v_ref[...])
        g_ref[...] = g
        d_ref[...] = d
        mo_ref[...] = mn
        vo_ref[...] = vn

    shp = jax.ShapeDtypeStruct((SUBLANES, PACK_W), F32)
    vm = pl.BlockSpec(memory_space=pltpu.VMEM)
    return pl.pallas_call(body, name="small_adamw", in_specs=[vm] * 4, out_specs=[vm] * 4, out_shape=[shp] * 4)(parts, w, m, v)


def _small_pack(d, rows_each):
    parts = [jnp.pad(d[n].astype(F32), ((0, 0), (0, PACK_W - SMALL_W[n]))) for n in SMALL]
    out = jnp.concatenate(parts, 0)
    pad = -out.shape[0] % SUBLANES
    return jnp.pad(out, ((0, pad), (0, 0)))


def kernel(x, mix_norm_g, w_in, swa_sinks, q_norm_g, w_uq, kv_norm_g, w_ukv, w_o_swa, w_o_mla, w_out, ffn_norm_g, w_gate, w_up, w_down, final_norm_g, loss_target, m_mix_norm_g, m_w_in, m_swa_sinks, m_q_norm_g, m_w_uq, m_kv_norm_g, m_w_ukv, m_w_o_swa, m_w_o_mla, m_w_out, m_ffn_norm_g, m_w_gate, m_w_up, m_w_down, m_final_norm_g, v_mix_norm_g, v_w_in, v_swa_sinks, v_q_norm_g, v_w_uq, v_kv_norm_g, v_w_ukv, v_w_o_swa, v_w_o_mla, v_w_out, v_ffn_norm_g, v_w_gate, v_w_up, v_w_down, v_final_norm_g):
    big_w = dict(w_in=w_in[0], w_uq=w_uq[0], w_ukv=w_ukv[0], w_o_swa=w_o_swa[0], w_o_mla=w_o_mla[0], w_out=w_out[0],
                 w_gate=w_gate[0], w_up=w_up[0], w_down=w_down[0])
    big_m = dict(w_in=m_w_in[0], w_uq=m_w_uq[0], w_ukv=m_w_ukv[0], w_o_swa=m_w_o_swa[0], w_o_mla=m_w_o_mla[0],
                 w_out=m_w_out[0], w_gate=m_w_gate[0], w_up=m_w_up[0], w_down=m_w_down[0])
    big_v = dict(w_in=v_w_in[0], w_uq=v_w_uq[0], w_ukv=v_w_ukv[0], w_o_swa=v_w_o_swa[0], w_o_mla=v_w_o_mla[0],
                 w_out=v_w_out[0], w_gate=v_w_gate[0], w_up=v_w_up[0], w_down=v_w_down[0])
    small_w = dict(mix_norm_g=mix_norm_g, ffn_norm_g=ffn_norm_g, final_norm_g=final_norm_g.reshape(1, D_MODEL),
                   q_norm_g=q_norm_g, kv_norm_g=kv_norm_g, swa_sinks=swa_sinks)
    small_m = dict(mix_norm_g=m_mix_norm_g, ffn_norm_g=m_ffn_norm_g, final_norm_g=m_final_norm_g.reshape(1, D_MODEL),
                   q_norm_g=m_q_norm_g, kv_norm_g=m_kv_norm_g, swa_sinks=m_swa_sinks)
    small_v = dict(mix_norm_g=v_mix_norm_g, ffn_norm_g=v_ffn_norm_g, final_norm_g=v_final_norm_g.reshape(1, D_MODEL),
                   q_norm_g=v_q_norm_g, kv_norm_g=v_kv_norm_g, swa_sinks=v_swa_sinks)

    px, py, pc = _position()
    me = 4 * px + 2 * py + pc
    idx = jnp.stack([me] + [4 * qx + 2 * qy + qc for qx, qy, qc in (_peer(px, py, pc, k) for k in range(1, N_DEV))])
    idx = idx.astype(jnp.int32)

    dev = lambda d: (d,)
    pack = _wire_pack(big_w, WIRE_DTYPE)
    win_g, = _all_gather(pack, ((0, dev, 0, W_IN_ROWS),), ((N_DEV, W_IN_ROWS, PACK_W),), name="ag_early")
    ag_mid = _gather_start(pack, W_IN_ROWS, ((0, dev, OUT_ROWS), (1, dev, SMALL_ROWS)),
                           ((N_DEV, OUT_ROWS, PACK_W), (N_DEV, SMALL_ROWS, PACK_W)), name="ag_mid_start")
    ag_late = _gather_start(pack, EARLY_ROWS, ((0, lambda d: (0, d), FF_COLS), (0, lambda d: (1, d), FF_COLS),
                                                (1, dev, FF_COLS)), (GU_SHAPE, D_SHAPE), name="ag_late_start")

    def own_rows(r0, r1, shape):
        return pack[r0:r1].reshape(shape)

    def mid_weights(after):
        wout_g, small_g = _gather_wait(ag_mid, W_IN_ROWS, MID_ROWS, after, name="ag_mid_wait")
        wout_g = lax.dynamic_update_slice(wout_g, own_rows(W_IN_ROWS, SMALL_ROW0, (1, OUT_ROWS, PACK_W)), (me, 0, 0))
        small_g = lax.dynamic_update_slice(small_g, own_rows(SMALL_ROW0, EARLY_ROWS, (1, SMALL_ROWS, PACK_W)), (me, 0, 0))
        return _mid_operands(wout_g, small_g)

    def late_weights(after):
        gu, d = _gather_wait(ag_late, EARLY_ROWS, LATE_ROWS, after, name="ag_late_wait")
        gu = lax.dynamic_update_slice(gu, own_rows(EARLY_ROWS, EARLY_ROWS + 2 * FF_COLS, (2, 1, FF_COLS, PACK_W)), (0, me, 0, 0))
        d = lax.dynamic_update_slice(d, own_rows(EARLY_ROWS + 2 * FF_COLS, PACK_ROWS, (1, FF_COLS, PACK_W)), (me, 0, 0))
        return gu.reshape(2 * D_FF, D_MODEL), d.reshape(D_FF, D_MODEL)

    rs = {}

    def late_grads(g_gu, g_d):
        rs["late"] = _scatter_start([g_gu.reshape(GU_SHAPE), g_d.reshape(D_SHAPE)],
                                    ((0, lambda d: (0, d), FF_COLS), (0, lambda d: (1, d), FF_COLS), (1, dev, FF_COLS)),
                                    name="rs_late_start")
        return rs["late"][-1]

    def mid_grads(g):
        rs["mid"] = _scatter_start([_mid_grad_pack(g)], ((0, dev, MID_ROWS),), name="rs_mid_start")
        return rs["mid"][-1]

    def last_grads(g_win_t):
        rs["last"] = _scatter_start([_w_in_grad_chunks(g_win_t)], ((0, dev, W_IN_ROWS),), name="rs_last_start")
        return rs["last"][-1]

    first_w = dict(small_w, mix_norm_g=mix_norm_g + ag_mid[-1][0:1, 0:1] + ag_late[-1][0:1, 0:1])
    loss_tot, gx, g_small = _local_step(
        x[0], loss_target[0], _w_in_operand(win_g), first_w, types.SimpleNamespace(mid=mid_weights, late=late_weights),
        types.SimpleNamespace(late=late_grads, mid=mid_grads, last=last_grads))

    g_gu, g_d, land_late = _scatter_wait(rs["late"], gx, name="rs_late_wait")
    g_mid, land_mid = _scatter_wait(rs["mid"], gx, name="rs_mid_wait")
    g_win, land_last = _scatter_wait(rs["last"], gx, name="rs_last_wait")
    gw = dict(w_gate=_peer_sum(g_gu, lambda m: (0, m), land_late, 0, FF_COLS, idx, name="rs_sum_gate").T,
              w_up=_peer_sum(g_gu, lambda m: (1, m), land_late, 1, FF_COLS, idx, name="rs_sum_up").T,
              w_down=_peer_sum(g_d, dev, land_late, 2, FF_COLS, idx, name="rs_sum_down"),
              w_in=_peer_sum(g_win, dev, land_last, 0, W_IN_ROWS, idx, name="rs_sum_in")[0:W_IN_COLS].T)
    gw.update(_mid_unpack(_peer_sum(g_mid, dev, land_mid, 0, MID_ROWS, idx, name="rs_sum_mid")))
    dw, mw, vw = {}, {}, {}
    for n in BIG:
        dw[n], mw[n], vw[n] = _adamw_call(big_w[n], gw[n], big_m[n], big_v[n], name="adamw_" + n)

    loss_rows = jnp.pad(loss_tot[0:1, 0:1], ((0, SUBLANES - 1), (0, PACK_W - 1)))
    small_rows = jnp.concatenate([_small_pack(g_small_rows(g_small), SUBLANES), loss_rows], 0)
    parts, = _all_gather(small_rows, ((0, lambda d: (d,), 0, small_rows.shape[0]),), ((N_DEV,) + small_rows.shape,),
                         name="ag_small")
    gs, ds, ms, vs = _small_adamw(parts, _small_pack(small_w, 1), _small_pack(small_m, 1), _small_pack(small_v, 1))
    loss = gs[len(SMALL), 0]

    def small_out(packed):
        out = {}
        for k, n in enumerate(SMALL):
            out[n] = packed[k:k + 1, :SMALL_W[n]]
        out["final_norm_g"] = out["final_norm_g"].reshape(D_MODEL)
        return out

    gs, ds, ms, vs = small_out(gs), small_out(ds), small_out(ms), small_out(vs)

    order = ("mix_norm_g", "w_in", "swa_sinks", "q_norm_g", "w_uq", "kv_norm_g", "w_ukv", "w_o_swa", "w_o_mla", "w_out",
             "ffn_norm_g", "w_gate", "w_up", "w_down", "final_norm_g")

    def leaves(big, small):
        return [big[n][None] if n in big else small[n] for n in order]

    return (loss, gx[None], *leaves(gw, gs), *leaves(dw, ds), *leaves(mw, ms), *leaves(vw, vs))


def g_small_rows(g_small):
    out = dict(g_small)
    out["swa_sinks"] = jnp.pad(g_small["swa_sinks"], ((0, SUBLANES - 1), (0, 0)))
    return out
```

```python
import types

import numpy as np
import jax
import jax.numpy as jnp
from jax import lax
from jax.experimental import pallas as pl
from jax.experimental.pallas import tpu as pltpu

F32 = jnp.float32
MXU_DTYPE = jnp.bfloat16
WIRE_DTYPE = jnp.bfloat16

D_MODEL = 1024
EPS = 1e-6
ROPE_THETA = 10000.0
BLOCK = 128
HEAD_DIM = 64
SWA_HEADS = 8
SWA_KV_HEADS = 2
SWA_GROUP = SWA_HEADS // SWA_KV_HEADS
MLA_HEADS = 8
MLA_NOPE = 64
MLA_ROPE = 32
MLA_V = 64
MLA_QK = MLA_NOPE + MLA_ROPE
Q_LORA = 384
KV_LORA = 256
D_FF = 2816
IN_SIZES = (512, 128, 128, Q_LORA, KV_LORA, MLA_ROPE, D_MODEL, D_MODEL)
IN_OFF = tuple(int(v) for v in np.cumsum((0,) + IN_SIZES))
ADAM_LR, ADAM_B1, ADAM_B2, ADAM_EPS, ADAM_WD, ADAM_STEP = 0.001, 0.9, 0.999, 1e-08, 0.01, 10

LANES = 128
SUBLANES = 8
VMEM_LIMIT = 48 * 1024 * 1024
N_DEV = 8
AXES = ("x", "y", "c")

P_GA, P_GB, P_Q, P_QLAT, P_KR, P_K, P_V, P_KVLAT, P_W = 0, 1024, 2048, 3072, 3456, 3584, 3840, 4096, 4352
KR_LANE = 64

LOG2E = 1.4426950408889634

NT = (((1,), (1,)), ((), ()))
NN = (((1,), (0,)), ((), ()))
TN = (((0,), (0,)), ((), ()))


def _cparams(sem):
    return pltpu.CompilerParams(dimension_semantics=sem, vmem_limit_bytes=VMEM_LIMIT)


def _mm(a, b, mode, *, name, out_dtype=F32, add=None, after=None, tm=512, tn=512, tk=None):
    if mode == "nn":
        (M, K), (K2, N) = a.shape, b.shape
    elif mode == "nt":
        (M, K), (N, K2) = a.shape, b.shape
    else:
        (K, M), (K2, N) = a.shape, b.shape
    assert K == K2, (a.shape, b.shape, mode)
    tk = K if tk is None else tk
    tm, tn = min(tm, M), min(tn, N)
    assert M % tm == 0 and N % tn == 0 and K % tk == 0, (M, N, K, tm, tn, tk)
    nk = K // tk
    dn = {"nn": NN, "nt": NT, "tn": TN}[mode]
    if mode == "tn":
        a_spec = pl.BlockSpec((tk, tm), lambda i, j, k: (k, i))
    else:
        a_spec = pl.BlockSpec((tm, tk), lambda i, j, k: (i, k))
    if mode == "nt":
        b_spec = pl.BlockSpec((tn, tk), lambda i, j, k: (j, k))
    else:
        b_spec = pl.BlockSpec((tk, tn), lambda i, j, k: (k, j))
    o_spec = pl.BlockSpec((tm, tn), lambda i, j, k: (i, j))
    has_add, has_after = add is not None, after is not None

    def body(*refs):
        a_ref, b_ref = refs[0], refs[1]
        add_ref = refs[2] if has_add else None
        o_ref = refs[2 + has_add + has_after]
        p = lax.dot_general(a_ref[...], b_ref[...], dn, preferred_element_type=F32)

        def finish(acc):
            if has_add:
                acc = acc + add_ref[...]
            o_ref[...] = acc.astype(o_ref.dtype)

        if nk == 1:
            finish(p)
        else:
            acc_ref = refs[-1]
            k = pl.program_id(2)

            @pl.when(k == 0)
            def _():
                acc_ref[...] = p

            @pl.when(k > 0)
            def _():
                acc_ref[...] += p

            @pl.when(k == nk - 1)
            def _():
                finish(acc_ref[...])

    ins = [a, b] + ([add] if has_add else []) + ([after] if has_after else [])
    in_specs = [a_spec, b_spec] + ([o_spec] if has_add else []) + ([pl.BlockSpec(memory_space=pl.ANY)] if has_after else [])
    return pl.pallas_call(
        body, name=name, grid=(M // tm, N // tn, nk), in_specs=in_specs, out_specs=o_spec,
        out_shape=jax.ShapeDtypeStruct((M, N), out_dtype),
        scratch_shapes=[pltpu.VMEM((tm, tn), F32)] if nk > 1 else [],
        compiler_params=_cparams(("parallel", "parallel", "arbitrary")),
    )(*ins)


def _rows(ts, w, cb=0):
    return pl.BlockSpec((ts, w), lambda i: (i, cb))


def _const(r, w):
    return pl.BlockSpec((r, w), lambda i: (0, 0))


def _sublane_sum(v):
    ts, c = v.shape
    return jnp.sum(v.reshape(ts // SUBLANES, SUBLANES, c), axis=0)


def _sigmoid(v):
    return 1.0 / (1.0 + jnp.exp(-v))


def _rope(v, cos, s_up, s_dn, up, dn):
    return v * cos + pltpu.roll(v, up, 1) * s_up + pltpu.roll(v, dn, 1) * s_dn


def _rope_t(dv, cos, s_up, s_dn, up, dn):
    return dv * cos + pltpu.roll(dv * s_up, dn, 1) + pltpu.roll(dv * s_dn, up, 1)


def _rope_tables(seq):
    pos = np.arange(seq, dtype=np.float32)[:, None]

    def base(dim):
        inv = np.float32(ROPE_THETA) ** (-np.arange(0, dim, 2, dtype=np.float32) / np.float32(dim))
        ang = (pos * inv.astype(np.float32)[None, :]).astype(np.float32)
        return np.cos(ang).astype(np.float32), np.sin(ang).astype(np.float32)

    z = lambda n: np.zeros((seq, n), np.float32)
    ca, sa = base(HEAD_DIM)
    a_cos = np.concatenate([ca, ca, z(64)], 1)
    a_up = np.concatenate([-sa, z(96)], 1)
    a_dn = np.concatenate([z(32), sa, z(64)], 1)
    cb, sb = base(MLA_ROPE)
    one = np.ones((seq, 64), np.float32)
    q_cos = np.concatenate([one, cb, cb, z(32)], 1)
    k_cos = np.concatenate([z(64), cb, cb, z(32)], 1)
    b_up = np.concatenate([z(64), -sb, z(48)], 1)
    b_dn = np.concatenate([z(80), sb, z(32)], 1)
    return tuple(jnp.asarray(t) for t in (a_cos, a_up, a_dn, q_cos, k_cos, b_up, b_dn))


def _norm_fwd(x, g, *, name, ts=256):
    s_, c = x.shape

    def body(x_ref, g_ref, h_ref):
        v = x_ref[...]
        r = lax.rsqrt(jnp.mean(v * v, axis=-1, keepdims=True) + EPS)
        h_ref[...] = (v * r * g_ref[...]).astype(h_ref.dtype)

    return pl.pallas_call(
        body, name=name, grid=(s_ // ts,), in_specs=[_rows(ts, c), _const(1, c)], out_specs=_rows(ts, c),
        out_shape=jax.ShapeDtypeStruct((s_, c), MXU_DTYPE), compiler_params=_cparams(("parallel",)),
    )(x, g)


def _norm_bwd(x, g, dy, res, *, name, ts=256, x_cb=0, x_src_w=None):
    s_ = x.shape[0]
    c = dy.shape[1]
    has_res = res is not None

    def body(*refs):
        x_ref, g_ref, dy_ref = refs[0], refs[1], refs[2]
        res_ref = refs[3] if has_res else None
        dx_ref, dxb_ref, dg_ref = refs[-3], refs[-2], refs[-1]
        v = x_ref[...]
        r = lax.rsqrt(jnp.mean(v * v, axis=-1, keepdims=True) + EPS)
        xh = v * r
        d = dy_ref[...]
        dxh = d * g_ref[...]
        dx = r * (dxh - xh * jnp.mean(dxh * xh, axis=-1, keepdims=True))
        if has_res:
            dx = dx + res_ref[...]
        dx_ref[...] = dx
        dxb_ref[...] = dx.astype(dxb_ref.dtype)

        @pl.when(pl.program_id(0) == 0)
        def _():
            dg_ref[...] = jnp.zeros(dg_ref.shape, F32)

        dg_ref[...] += _sublane_sum(d * xh)

    ins = [x, g, dy] + ([res] if has_res else [])
    in_specs = [_rows(ts, c, x_cb), _const(1, c), _rows(ts, c)] + ([_rows(ts, c)] if has_res else [])
    return pl.pallas_call(
        body, name=name, grid=(s_ // ts,), in_specs=in_specs,
        out_specs=[_rows(ts, c), _rows(ts, c), _const(SUBLANES, c)],
        out_shape=[jax.ShapeDtypeStruct((s_, c), F32), jax.ShapeDtypeStruct((s_, c), MXU_DTYPE),
                   jax.ShapeDtypeStruct((SUBLANES, c), F32)],
        compiler_params=_cparams(("arbitrary",)),
    )(*ins)


def _attn_prep(p, gq, gkv, tabs, *, ts=256):
    s_ = p.shape[0]
    a_cos, a_up, a_dn, _, k_cos, b_up, b_dn = tabs

    def body(q_ref, k_ref, v_ref, ql_ref, kvl_ref, kr_ref, gq_ref, gkv_ref, ac, au, ad, kc, bu, bd,
             qa_ref, ka_ref, va_ref, cq_ref, ckv_ref, kro_ref):
        c_, u_, d_ = ac[...], au[...], ad[...]
        for h in range(SWA_HEADS):
            sl = slice(h * LANES, (h + 1) * LANES)
            qa_ref[:, sl] = _rope(q_ref[:, sl], c_, u_, d_, 96, 32).astype(qa_ref.dtype)
        for h in range(SWA_KV_HEADS):
            sl = slice(h * LANES, (h + 1) * LANES)
            ka_ref[:, sl] = _rope(k_ref[:, sl], c_, u_, d_, 96, 32).astype(ka_ref.dtype)
        va_ref[...] = v_ref[...].astype(va_ref.dtype)
        for src, gref, dst in ((ql_ref, gq_ref, cq_ref), (kvl_ref, gkv_ref, ckv_ref)):
            v = src[...]
            r = lax.rsqrt(jnp.mean(v * v, axis=-1, keepdims=True) + EPS)
            dst[...] = (v * r * gref[...]).astype(dst.dtype)
        kro_ref[...] = _rope(kr_ref[...], kc[...], bu[...], bd[...], 112, 16)

    tab = _rows(ts, LANES)
    return pl.pallas_call(
        body, name="attn_prep", grid=(s_ // ts,),
        in_specs=[_rows(ts, 1024, P_Q // 1024), _rows(ts, 256, P_K // 256), _rows(ts, 256, P_V // 256),
                  _rows(ts, Q_LORA, P_QLAT // Q_LORA), _rows(ts, KV_LORA, P_KVLAT // KV_LORA),
                  _rows(ts, LANES, P_KR // LANES), _const(1, Q_LORA), _const(1, KV_LORA), tab, tab, tab, tab, tab, tab],
        out_specs=[_rows(ts, 1024), _rows(ts, 256), _rows(ts, 256), _rows(ts, Q_LORA), _rows(ts, KV_LORA),
                   _rows(ts, LANES)],
        out_shape=[jax.ShapeDtypeStruct((s_, 1024), MXU_DTYPE), jax.ShapeDtypeStruct((s_, 256), MXU_DTYPE),
                   jax.ShapeDtypeStruct((s_, 256), MXU_DTYPE), jax.ShapeDtypeStruct((s_, Q_LORA), MXU_DTYPE),
                   jax.ShapeDtypeStruct((s_, KV_LORA), MXU_DTYPE), jax.ShapeDtypeStruct((s_, LANES), F32)],
        compiler_params=_cparams(("parallel",)),
    )(p, p, p, p, p, p, gq, gkv, a_cos, a_up, a_dn, k_cos, b_up, b_dn)


def _mla_prep(qp, kp, kro, tabs, *, ts=256):
    s_ = qp.shape[0]
    _, _, _, q_cos, _, b_up, b_dn = tabs

    def body(q_ref, k_ref, kr_ref, qc, bu, bd, qo_ref, ko_ref):
        c_, u_, d_ = qc[...], bu[...], bd[...]
        kr = kr_ref[...]
        for h in range(MLA_HEADS):
            sl = slice(h * LANES, (h + 1) * LANES)
            qo_ref[:, sl] = _rope(q_ref[:, sl], c_, u_, d_, 112, 16).astype(qo_ref.dtype)
            ko_ref[:, sl] = (k_ref[:, sl] + kr).astype(ko_ref.dtype)

    tab = _rows(ts, LANES)
    return pl.pallas_call(
        body, name="mla_prep", grid=(s_ // ts,),
        in_specs=[_rows(ts, 1024), _rows(ts, 1024), tab, tab, tab, tab],
        out_specs=[_rows(ts, 1024), _rows(ts, 1024)],
        out_shape=[jax.ShapeDtypeStruct((s_, 1024), MXU_DTYPE)] * 2,
        compiler_params=_cparams(("parallel",)),
    )(qp, kp, kro, q_cos, b_up, b_dn)


def _mla_unprep(dqc, dkc, dvp, tabs, *, ts=256):
    s_ = dqc.shape[0]
    _, _, _, q_cos, k_cos, b_up, b_dn = tabs

    def body(dq_ref, dk_ref, dv_ref, qc, kc, bu, bd, dqo_ref, dkvo_ref, dkr_ref):
        c_, u_, d_ = qc[...], bu[...], bd[...]
        tot = jnp.zeros((ts, LANES), F32)
        for h in range(MLA_HEADS):
            sl = slice(h * LANES, (h + 1) * LANES)
            dqo_ref[:, sl] = _rope_t(dq_ref[:, sl], c_, u_, d_, 112, 16).astype(dqo_ref.dtype)
            dk = dk_ref[:, sl]
            dkvo_ref[:, sl] = dk.astype(dkvo_ref.dtype)
            tot = tot + dk
        dkvo_ref[:, 1024:2048] = dv_ref[...].astype(dkvo_ref.dtype)
        dkr_ref[...] = _rope_t(tot, kc[...], u_, d_, 112, 16).astype(dkr_ref.dtype)

    tab = _rows(ts, LANES)
    return pl.pallas_call(
        body, name="mla_unprep", grid=(s_ // ts,),
        in_specs=[_rows(ts, 1024), _rows(ts, 1024), _rows(ts, 1024), tab, tab, tab, tab],
        out_specs=[_rows(ts, 1024), _rows(ts, 2048), _rows(ts, LANES)],
        out_shape=[jax.ShapeDtypeStruct((s_, 1024), MXU_DTYPE), jax.ShapeDtypeStruct((s_, 2048), MXU_DTYPE),
                   jax.ShapeDtypeStruct((s_, LANES), MXU_DTYPE)],
        compiler_params=_cparams(("parallel",)),
    )(dqc, dkc, dvp, q_cos, k_cos, b_up, b_dn)


def _swa_unrope(dqa, dka, tabs, *, ts=256):
    s_ = dqa.shape[0]
    a_cos, a_up, a_dn = tabs[0], tabs[1], tabs[2]

    def body(dq_ref, dk_ref, ac, au, ad, dqo_ref, dko_ref):
        c_, u_, d_ = ac[...], au[...], ad[...]
        for h in range(SWA_HEADS):
            sl = slice(h * LANES, (h + 1) * LANES)
            dqo_ref[:, sl] = _rope_t(dq_ref[:, sl], c_, u_, d_, 96, 32).astype(dqo_ref.dtype)
        for h in range(SWA_KV_HEADS):
            sl = slice(h * LANES, (h + 1) * LANES)
            dko_ref[:, sl] = _rope_t(dk_ref[:, sl], c_, u_, d_, 96, 32).astype(dko_ref.dtype)

    tab = _rows(ts, LANES)
    return pl.pallas_call(
        body, name="swa_unrope", grid=(s_ // ts,),
        in_specs=[_rows(ts, 1024), _rows(ts, 256), tab, tab, tab],
        out_specs=[_rows(ts, 1024), _rows(ts, 256)],
        out_shape=[jax.ShapeDtypeStruct((s_, 1024), MXU_DTYPE), jax.ShapeDtypeStruct((s_, 256), MXU_DTYPE)],
        compiler_params=_cparams(("parallel",)),
    )(dqa, dka, a_cos, a_up, a_dn)


def _gate_fwd(p, ta, tb, *, ts=256):
    s_ = p.shape[0]

    def body(ga_ref, gb_ref, ta_ref, tb_ref, y_ref):
        y = _sigmoid(ga_ref[...]) * ta_ref[...] + _sigmoid(gb_ref[...]) * tb_ref[...]
        y_ref[...] = y.astype(y_ref.dtype)

    return pl.pallas_call(
        body, name="gate_fwd", grid=(s_ // ts,),
        in_specs=[_rows(ts, 1024, P_GA // 1024), _rows(ts, 1024, P_GB // 1024), _rows(ts, 1024), _rows(ts, 1024)],
        out_specs=_rows(ts, 1024), out_shape=jax.ShapeDtypeStruct((s_, 1024), MXU_DTYPE),
        compiler_params=_cparams(("parallel",)),
    )(p, p, ta, tb)


def _gate_bwd(p, ta, tb, dy, *, ts=256):
    s_ = p.shape[0]

    def body(ga_ref, gb_ref, ta_ref, tb_ref, dy_ref, dta_ref, dtb_ref, dg_ref):
        d = dy_ref[...]
        sa, sb = _sigmoid(ga_ref[...]), _sigmoid(gb_ref[...])
        dta_ref[...] = (d * sa).astype(dta_ref.dtype)
        dtb_ref[...] = (d * sb).astype(dtb_ref.dtype)
        dg_ref[:, 0:1024] = (d * ta_ref[...] * (sa * (1.0 - sa))).astype(dg_ref.dtype)
        dg_ref[:, 1024:2048] = (d * tb_ref[...] * (sb * (1.0 - sb))).astype(dg_ref.dtype)

    return pl.pallas_call(
        body, name="gate_bwd", grid=(s_ // ts,),
        in_specs=[_rows(ts, 1024, P_GA // 1024), _rows(ts, 1024, P_GB // 1024), _rows(ts, 1024), _rows(ts, 1024),
                  _rows(ts, 1024)],
        out_specs=[_rows(ts, 1024), _rows(ts, 1024), _rows(ts, 2048)],
        out_shape=[jax.ShapeDtypeStruct((s_, 1024), MXU_DTYPE)] * 2 + [jax.ShapeDtypeStruct((s_, 2048), MXU_DTYPE)],
        compiler_params=_cparams(("parallel",)),
    )(p, p, ta, tb, dy)


def _swiglu_fwd(gu, *, ts=256):
    s_ = gu.shape[0]

    def body(g_ref, u_ref, a_ref):
        g = g_ref[...]
        a_ref[...] = (g * _sigmoid(g) * u_ref[...]).astype(a_ref.dtype)

    return pl.pallas_call(
        body, name="swiglu_fwd", grid=(s_ // ts,), in_specs=[_rows(ts, D_FF, 0), _rows(ts, D_FF, 1)],
        out_specs=_rows(ts, D_FF), out_shape=jax.ShapeDtypeStruct((s_, D_FF), MXU_DTYPE),
        compiler_params=_cparams(("parallel",)),
    )(gu, gu)


def _swiglu_bwd(gu, da, *, ts=256):
    s_ = gu.shape[0]

    def body(g_ref, u_ref, da_ref, o_ref):
        g, u, d = g_ref[...], u_ref[...], da_ref[...]
        sg = _sigmoid(g)
        o_ref[:, 0:D_FF] = (d * u * (sg * (1.0 + g * (1.0 - sg)))).astype(o_ref.dtype)
        o_ref[:, D_FF:2 * D_FF] = (d * (g * sg)).astype(o_ref.dtype)

    return pl.pallas_call(
        body, name="swiglu_bwd", grid=(s_ // ts,), in_specs=[_rows(ts, D_FF, 0), _rows(ts, D_FF, 1), _rows(ts, D_FF)],
        out_specs=_rows(ts, 2 * D_FF), out_shape=jax.ShapeDtypeStruct((s_, 2 * D_FF), MXU_DTYPE),
        compiler_params=_cparams(("parallel",)),
    )(gu, gu, da)


def _loss_bwd(x2, g, tgt, *, ts=256):
    s_, c = x2.shape

    def body(x_ref, g_ref, t_ref, dx_ref, dxb_ref, dg_ref, lp_ref, tot_ref):
        v = x_ref[...]
        r = lax.rsqrt(jnp.mean(v * v, axis=-1, keepdims=True) + EPS)
        xh = v * r
        gg = g_ref[...]
        e = xh * gg - t_ref[...]
        do = e * (1.0 / c)
        dxh = do * gg
        dx = r * (dxh - xh * jnp.mean(dxh * xh, axis=-1, keepdims=True))
        dx_ref[...] = dx
        dxb_ref[...] = dx.astype(dxb_ref.dtype)
        i = pl.program_id(0)

        @pl.when(i == 0)
        def _():
            dg_ref[...] = jnp.zeros(dg_ref.shape, F32)
            lp_ref[...] = jnp.zeros(lp_ref.shape, F32)

        dg_ref[...] += _sublane_sum(do * xh)
        lp_ref[...] += _sublane_sum(e * e)
        tot_ref[...] = jnp.full(tot_ref.shape, (0.5 / c) * jnp.sum(lp_ref[...]), F32)

    return pl.pallas_call(
        body, name="loss_bwd", grid=(s_ // ts,), in_specs=[_rows(ts, c), _const(1, c), _rows(ts, c)],
        out_specs=[_rows(ts, c), _rows(ts, c), _const(SUBLANES, c), _const(SUBLANES, c), _const(SUBLANES, LANES)],
        out_shape=[jax.ShapeDtypeStruct((s_, c), F32), jax.ShapeDtypeStruct((s_, c), MXU_DTYPE),
                   jax.ShapeDtypeStruct((SUBLANES, c), F32), jax.ShapeDtypeStruct((SUBLANES, c), F32),
                   jax.ShapeDtypeStruct((SUBLANES, LANES), F32)],
        compiler_params=_cparams(("arbitrary",)),
    )(x2, g, tgt)


def _mla_bwd_prep(dob, o32, *, ts=256):
    s_ = dob.shape[0]

    def body(do_ref, o_ref, dob_ref, dl_ref):
        d = do_ref[...]
        dob_ref[...] = d.astype(dob_ref.dtype)
        prod = d * o_ref[...]
        for h in range(MLA_HEADS):
            dl_ref[h] = jnp.sum(prod[:, h * LANES:(h + 1) * LANES].T, axis=0, keepdims=True)

    return pl.pallas_call(
        body, name="mla_bwd_prep", grid=(s_ // ts,), in_specs=[_rows(ts, 1024), _rows(ts, 1024)],
        out_specs=[_rows(ts, 1024), pl.BlockSpec((MLA_HEADS, 1, ts), lambda i: (0, 0, i))],
        out_shape=[jax.ShapeDtypeStruct((s_, 1024), MXU_DTYPE), jax.ShapeDtypeStruct((MLA_HEADS, 1, s_), F32)],
        compiler_params=_cparams(("parallel",)),
    )(dob, o32)


SWA_T = 4 * BLOCK


def _swa_masks(sb):
    kr = lax.broadcasted_iota(jnp.int32, (2 * BLOCK, BLOCK), 0)
    qc = lax.broadcasted_iota(jnp.int32, (2 * BLOCK, BLOCK), 1)
    band = jnp.logical_and(kr > qc, kr <= qc + BLOCK)
    first = jnp.logical_and(band, kr >= BLOCK)
    return band, jnp.logical_or(first, jnp.logical_and(band, sb > 0))


def _swa_in_specs(rev, nsb):
    sbi = (lambda j: nsb - 1 - j) if rev else (lambda j: j)
    cur = pl.BlockSpec((SWA_T, LANES), lambda g, j: (sbi(j), g))
    prev = pl.BlockSpec((BLOCK, LANES), lambda g, j: (jnp.maximum(4 * sbi(j) - 1, 0), g))
    q = pl.BlockSpec((SWA_T, SWA_GROUP * LANES), lambda g, j: (sbi(j), g))
    sink = pl.BlockSpec((1, SUBLANES, LANES), lambda g, j: (g, 0, 0))
    lse = pl.BlockSpec((SWA_GROUP, 1, SWA_T), lambda g, j: (g, 0, sbi(j)))
    return q, cur, prev, sink, lse


def _swa_fwd(qa, ka, va, sink_b):
    s_ = qa.shape[0]
    nsb = s_ // SWA_T
    c2 = HEAD_DIM ** -0.5 * LOG2E

    def body(q_ref, kc_ref, kp_ref, vc_ref, vp_ref, sk_ref, o32_ref, o16_ref, lse_ref, kx, vx):
        kx[0:BLOCK, :] = kp_ref[...]
        kx[BLOCK:5 * BLOCK, :] = kc_ref[...]
        vx[0:BLOCK, :] = vp_ref[...]
        vx[BLOCK:5 * BLOCK, :] = vc_ref[...]
        band, band0 = _swa_masks(pl.program_id(1))
        for hh in range(SWA_GROUP):
            sink2 = sk_ref[0, hh:hh + 1, 0:1] * LOG2E
            cs = slice(hh * LANES, (hh + 1) * LANES)
            for b in range(4):
                rs = slice(b * BLOCK, (b + 1) * BLOCK)
                ks = slice(b * BLOCK, (b + 2) * BLOCK)
                st = lax.dot_general(kx[ks, :], q_ref[rs, cs], NT, preferred_element_type=F32) * c2
                st = jnp.where(band0 if b == 0 else band, st, -jnp.inf)
                m = jnp.maximum(jnp.max(st, axis=0, keepdims=True), sink2)
                pt = jnp.exp2(st - m)
                den = jnp.sum(pt, axis=0, keepdims=True) + jnp.exp2(sink2 - m)
                o = lax.dot_general((pt * (1.0 / den)).astype(MXU_DTYPE), vx[ks, :], TN, preferred_element_type=F32)
                o32_ref[rs, cs] = o
                o16_ref[rs, cs] = o.astype(o16_ref.dtype)
                lse_ref[hh, :, rs] = m + jnp.log2(den)

    q, cur, prev, sink, lse_spec = _swa_in_specs(False, nsb)
    return pl.pallas_call(
        body, name="swa_fwd", grid=(SWA_KV_HEADS, nsb), in_specs=[q, cur, prev, cur, prev, sink],
        out_specs=[q, q, lse_spec],
        out_shape=[jax.ShapeDtypeStruct((s_, SWA_HEADS * LANES), F32), jax.ShapeDtypeStruct((s_, SWA_HEADS * LANES), MXU_DTYPE),
                   jax.ShapeDtypeStruct((SWA_HEADS, 1, s_), F32)],
        scratch_shapes=[pltpu.VMEM((5 * BLOCK, LANES), MXU_DTYPE), pltpu.VMEM((5 * BLOCK, LANES), MXU_DTYPE)],
        compiler_params=_cparams(("parallel", "arbitrary")),
    )(qa, ka, ka, va, va, sink_b)


def _swa_bwd(qa, ka, va, sink_b, o32, do, lse):
    s_ = qa.shape[0]
    nsb = s_ // SWA_T
    scale = HEAD_DIM ** -0.5
    c2 = scale * LOG2E

    def body(q_ref, kc_ref, kp_ref, vc_ref, vp_ref, sk_ref, o_ref, do_ref, lse_ref,
             dq_ref, dk_ref, dv_ref, dsk_ref, kx, vx, kacc, vacc, kcar, vcar):
        j = pl.program_id(1)
        kx[0:BLOCK, :] = kp_ref[...]
        kx[BLOCK:5 * BLOCK, :] = kc_ref[...]
        vx[0:BLOCK, :] = vp_ref[...]
        vx[BLOCK:5 * BLOCK, :] = vc_ref[...]
        band, band0 = _swa_masks(nsb - 1 - j)
        kacc[...] = jnp.zeros(kacc.shape, F32)
        vacc[...] = jnp.zeros(vacc.shape, F32)

        @pl.when(j == 0)
        def _():
            kcar[...] = jnp.zeros(kcar.shape, F32)
            vcar[...] = jnp.zeros(vcar.shape, F32)
            dsk_ref[...] = jnp.zeros(dsk_ref.shape, F32)

        for hh in range(SWA_GROUP):
            sink2 = sk_ref[0, hh:hh + 1, 0:1] * LOG2E
            cs = slice(hh * LANES, (hh + 1) * LANES)
            dsink = jnp.zeros((1, 1), F32)
            for b in range(4):
                rs = slice(b * BLOCK, (b + 1) * BLOCK)
                ks = slice(b * BLOCK, (b + 2) * BLOCK)
                q, k2, v2 = q_ref[rs, cs], kx[ks, :], vx[ks, :]
                d = do_ref[rs, cs]
                delta = jnp.sum((d * o_ref[rs, cs]).T, axis=0, keepdims=True)
                l2 = lse_ref[hh, :, rs]
                st = lax.dot_general(k2, q, NT, preferred_element_type=F32) * c2
                pt = jnp.exp2(jnp.where(band0 if b == 0 else band, st, -jnp.inf) - l2)
                db = d.astype(MXU_DTYPE)
                dst = (pt * (lax.dot_general(v2, db, NT, preferred_element_type=F32) - delta) * scale).astype(MXU_DTYPE)
                dq_ref[rs, cs] = lax.dot_general(dst, k2, TN, preferred_element_type=F32)
                kacc[ks, :] += jnp.dot(dst, q, preferred_element_type=F32)
                vacc[ks, :] += jnp.dot(pt.astype(MXU_DTYPE), db, preferred_element_type=F32)
                dsink = dsink - jnp.sum(jnp.exp2(sink2 - l2) * delta, axis=1, keepdims=True)
            dsk_ref[0, hh:hh + 1, :] += jnp.broadcast_to(dsink, (1, LANES))

        dk_ref[0:3 * BLOCK, :] = kacc[BLOCK:4 * BLOCK, :]
        dk_ref[3 * BLOCK:4 * BLOCK, :] = kacc[4 * BLOCK:5 * BLOCK, :] + kcar[...]
        dv_ref[0:3 * BLOCK, :] = vacc[BLOCK:4 * BLOCK, :].astype(dv_ref.dtype)
        dv_ref[3 * BLOCK:4 * BLOCK, :] = (vacc[4 * BLOCK:5 * BLOCK, :] + vcar[...]).astype(dv_ref.dtype)
        kcar[...] = kacc[0:BLOCK, :]
        vcar[...] = vacc[0:BLOCK, :]

    q, cur, prev, sink, lse_spec = _swa_in_specs(True, nsb)
    return pl.pallas_call(
        body, name="swa_bwd", grid=(SWA_KV_HEADS, nsb),
        in_specs=[q, cur, prev, cur, prev, sink, q, q, lse_spec],
        out_specs=[q, cur, cur, sink],
        out_shape=[jax.ShapeDtypeStruct((s_, SWA_HEADS * LANES), F32), jax.ShapeDtypeStruct((s_, SWA_KV_HEADS * LANES), F32),
                   jax.ShapeDtypeStruct((s_, SWA_KV_HEADS * LANES), MXU_DTYPE),
                   jax.ShapeDtypeStruct((SWA_KV_HEADS, SUBLANES, LANES), F32)],
        scratch_shapes=[pltpu.VMEM((5 * BLOCK, LANES), MXU_DTYPE), pltpu.VMEM((5 * BLOCK, LANES), MXU_DTYPE),
                        pltpu.VMEM((5 * BLOCK, LANES), F32), pltpu.VMEM((5 * BLOCK, LANES), F32),
                        pltpu.VMEM((BLOCK, LANES), F32), pltpu.VMEM((BLOCK, LANES), F32)],
        compiler_params=_cparams(("arbitrary", "arbitrary")),
    )(qa, ka, ka, va, va, sink_b, o32, do, lse)


MLA_T = 512
MLA_GROUP = 2


def _mla_specs(s_, t):
    w = MLA_GROUP * LANES
    qs = pl.BlockSpec((t, w), lambda g, i: (i, g))
    kv = pl.BlockSpec((s_, w), lambda g, i: (0, g))
    row = pl.BlockSpec((MLA_GROUP, 1, t), lambda g, i: (g, 0, i))
    return qs, kv, row


def _causal_scores_t(k, q, t, c2, masked):
    st = lax.dot_general(k, q, NT, preferred_element_type=F32) * c2
    if masked:
        kr = lax.broadcasted_iota(jnp.int32, (t, t), 0)
        qc = lax.broadcasted_iota(jnp.int32, (t, t), 1)
        st = jnp.where(kr <= qc, st, -jnp.inf)
    return st


def _mla_fwd(qc, kc, vp):
    s_ = qc.shape[0]
    t = min(MLA_T, s_)
    c2 = MLA_QK ** -0.5 * LOG2E

    def body(q_ref, k_ref, v_ref, o32_ref, o16_ref, lse_ref, m_s, acc_s):
        qi = pl.program_id(1)
        m_s[...] = jnp.full(m_s.shape, -jnp.inf, F32)
        acc_s[...] = jnp.zeros(acc_s.shape, F32)
        ones_lane = lax.broadcasted_iota(jnp.int32, (t, LANES), 1) == MLA_V

        def step(ki, masked):
            off = pl.multiple_of(ki * t, t)
            for g in range(MLA_GROUP):
                cs = slice(g * LANES, (g + 1) * LANES)
                st = _causal_scores_t(k_ref[pl.ds(off, t), cs], q_ref[:, cs], t, c2, masked)
                m_old = m_s[g]
                m_new = jnp.maximum(m_old, jnp.max(st, axis=0, keepdims=True))
                alpha = jnp.exp2(m_old - m_new)
                pt = jnp.exp2(st - m_new).astype(MXU_DTYPE)
                v = v_ref[pl.ds(off, t), cs]
                v = jnp.where(ones_lane, jnp.ones((), v.dtype), v)
                acc_s[g] = alpha * acc_s[g] + lax.dot_general(v, pt, TN, preferred_element_type=F32)
                m_s[g] = m_new

        def full_block(ki, carry):
            step(ki, False)
            return carry

        lax.fori_loop(0, qi, full_block, 0)
        step(qi, True)
        for g in range(MLA_GROUP):
            cs = slice(g * LANES, (g + 1) * LANES)
            acc = acc_s[g]
            l = acc[MLA_V:MLA_V + 1, :]
            o = (acc * (1.0 / l)).T
            o32_ref[:, cs] = o
            o16_ref[:, cs] = o.astype(o16_ref.dtype)
            lse_ref[g] = m_s[g] + jnp.log2(l)

    qs, kv, row = _mla_specs(s_, t)
    return pl.pallas_call(
        body, name="mla_fwd", grid=(MLA_HEADS // MLA_GROUP, s_ // t), in_specs=[qs, kv, kv], out_specs=[qs, qs, row],
        out_shape=[jax.ShapeDtypeStruct((s_, MLA_HEADS * LANES), F32), jax.ShapeDtypeStruct((s_, MLA_HEADS * LANES), MXU_DTYPE),
                   jax.ShapeDtypeStruct((MLA_HEADS, 1, s_), F32)],
        scratch_shapes=[pltpu.VMEM((MLA_GROUP, 1, t), F32), pltpu.VMEM((MLA_GROUP, LANES, t), F32)],
        compiler_params=_cparams(("parallel", "arbitrary")),
    )(qc, kc, vp)


def _mla_bwd(qc, kc, vp, dob, lse, delta):
    s_ = qc.shape[0]
    t = min(MLA_T, s_)
    scale = MLA_QK ** -0.5
    c2 = scale * LOG2E

    def body(q_ref, do_ref, lse_ref, dl_ref, k_ref, v_ref, dq_ref, dk_ref, dv_ref, dqt_s):
        qi = pl.program_id(1)

        @pl.when(qi == 0)
        def _():
            dk_ref[...] = jnp.zeros(dk_ref.shape, F32)
            dv_ref[...] = jnp.zeros(dv_ref.shape, F32)

        dqt_s[...] = jnp.zeros(dqt_s.shape, F32)

        def step(ki, masked):
            off = pl.multiple_of(ki * t, t)
            for g in range(MLA_GROUP):
                cs = slice(g * LANES, (g + 1) * LANES)
                q, d, k = q_ref[:, cs], do_ref[:, cs], k_ref[pl.ds(off, t), cs]
                pt = jnp.exp2(_causal_scores_t(k, q, t, c2, masked) - lse_ref[g])
                dpt = lax.dot_general(v_ref[pl.ds(off, t), cs], d, NT, preferred_element_type=F32)
                dst = (pt * (dpt - dl_ref[g]) * scale).astype(MXU_DTYPE)
                dv_ref[pl.ds(off, t), cs] += jnp.dot(pt.astype(MXU_DTYPE), d, preferred_element_type=F32)
                dk_ref[pl.ds(off, t), cs] += jnp.dot(dst, q, preferred_element_type=F32)
                dqt_s[g] += lax.dot_general(k, dst, TN, preferred_element_type=F32)

        def full_block(ki, carry):
            step(ki, False)
            return carry

        lax.fori_loop(0, qi, full_block, 0)
        step(qi, True)
        for g in range(MLA_GROUP):
            dq_ref[:, g * LANES:(g + 1) * LANES] = dqt_s[g].T

    qs, kv, row = _mla_specs(s_, t)
    shp = jax.ShapeDtypeStruct((s_, MLA_HEADS * LANES), F32)
    return pl.pallas_call(
        body, name="mla_bwd", grid=(MLA_HEADS // MLA_GROUP, s_ // t), in_specs=[qs, qs, row, row, kv, kv],
        out_specs=[qs, kv, kv], out_shape=[shp, shp, shp], scratch_shapes=[pltpu.VMEM((MLA_GROUP, LANES, t), F32)],
        compiler_params=_cparams(("parallel", "arbitrary")),
    )(qc, dob, lse, delta, kc, vp)


def _pad_heads(w, nh, hd, axis):
    shp = w.shape
    w = w.reshape(shp[:axis] + (nh, hd) + shp[axis + 1:])
    pad = [(0, 0)] * w.ndim
    pad[axis + 1] = (0, LANES - hd)
    w = jnp.pad(w, pad)
    return w.reshape(shp[:axis] + (nh * LANES,) + shp[axis + 1:])


def _unpad_heads(w, nh, hd, axis):
    shp = w.shape
    w = w.reshape(shp[:axis] + (nh, LANES) + shp[axis + 1:])
    w = lax.slice_in_dim(w, 0, hd, axis=axis + 1)
    return w.reshape(shp[:axis] + (nh * hd,) + shp[axis + 1:])


PACK_W = 1024
ROW_TILE = 16
FULL_SHAPE = dict(w_in=(1024, 3488), w_uq=(384, 768), w_ukv=(256, 1024), w_o_swa=(512, 1024), w_o_mla=(512, 1024),
                  w_out=(1024, 1024), w_gate=(1024, 2816), w_up=(1024, 2816), w_down=(2816, 1024))
BIG = tuple(FULL_SHAPE)
ROW_SHARDED = ("w_out", "w_down")
W_IN_COLS = FULL_SHAPE["w_in"][1] // N_DEV
W_IN_ROWS = -(-W_IN_COLS // ROW_TILE) * ROW_TILE
FF_COLS = D_FF // N_DEV
OUT_ROWS = D_MODEL // N_DEV
SMALL_ROW0 = W_IN_ROWS + OUT_ROWS
SMALL_FLAT = (("w_uq", 0, 36), ("w_ukv", 48, 32), ("w_o_swa", 80, 64), ("w_o_mla", 144, 64))
SMALL_ROWS = 208
EARLY_ROWS = SMALL_ROW0 + SMALL_ROWS
LATE_ROWS = 3 * FF_COLS
PACK_ROWS = EARLY_ROWS + LATE_ROWS


def _shard_shape(n):
    r, c = FULL_SHAPE[n]
    return (r // N_DEV, c) if n in ROW_SHARDED else (r, c // N_DEV)


def _wire_pack(sh, dtype):
    c = lambda n: sh[n].astype(dtype)
    rows = [jnp.pad(c("w_in").T, ((0, W_IN_ROWS - W_IN_COLS), (0, 0))), c("w_out")]
    for n, _, r in SMALL_FLAT:
        rows.append(jnp.pad(c(n).reshape(r, PACK_W), ((0, -r % ROW_TILE), (0, 0))))
    return jnp.concatenate(rows + [c("w_gate").T, c("w_up").T, c("w_down")], 0)


MID_ROWS = OUT_ROWS + SMALL_ROWS


def _mid_unpack(p):
    out = dict(w_out=p[0:OUT_ROWS])
    for n, off, r in SMALL_FLAT:
        out[n] = p[OUT_ROWS + off:OUT_ROWS + off + r].reshape(_shard_shape(n))
    return out


def _w_in_row_maps():
    sp = lambda col: (col // W_IN_COLS) * W_IN_ROWS + col % W_IN_COLS
    fwd = np.full((P_W,), -1, np.int64)

    def put(t0, c0, n):
        fwd[t0:t0 + n] = [sp(c) for c in range(c0, c0 + n)]

    put(P_GA, IN_OFF[6], D_MODEL)
    put(P_GB, IN_OFF[7], D_MODEL)
    for h in range(SWA_HEADS):
        put(P_Q + LANES * h, IN_OFF[0] + HEAD_DIM * h, HEAD_DIM)
    put(P_QLAT, IN_OFF[3], Q_LORA)
    put(P_KR + KR_LANE, IN_OFF[5], MLA_ROPE)
    for h in range(SWA_KV_HEADS):
        put(P_K + LANES * h, IN_OFF[1] + HEAD_DIM * h, HEAD_DIM)
        put(P_V + LANES * h, IN_OFF[2] + HEAD_DIM * h, HEAD_DIM)
    put(P_KVLAT, IN_OFF[4], KV_LORA)
    inv = np.full((N_DEV * W_IN_ROWS,), -1, np.int64)
    inv[fwd[fwd >= 0]] = np.nonzero(fwd >= 0)[0]
    return fwd, inv


def _take_rows(src, idx, *, name):
    n_out, n_src, width = len(idx), src.shape[0], src.shape[1]
    assert n_out % BLOCK == 0 and n_src % BLOCK == 0
    n_tiles = n_out // BLOCK
    blocks = [sorted({int(v) // BLOCK for v in idx[i * BLOCK:(i + 1) * BLOCK] if v >= 0}) for i in range(n_tiles)]
    k_max = max(1, max(len(b) for b in blocks))
    tab = np.zeros((n_tiles, k_max), np.int32)
    sel = np.zeros((n_tiles, k_max, BLOCK, BLOCK), np.float32)
    for i, blks in enumerate(blocks):
        for m, b in enumerate(blks):
            tab[i, m] = b
            for r in range(BLOCK):
                v = int(idx[i * BLOCK + r])
                if v >= 0 and v // BLOCK == b:
                    sel[i, m, r, v % BLOCK] = 1.0

    def body(tab_ref, sel_ref, *refs):
        o_ref = refs[k_max]
        acc = jnp.dot(sel_ref[0, 0], refs[0][...], preferred_element_type=F32)
        for m in range(1, k_max):
            acc = acc + jnp.dot(sel_ref[0, m], refs[m][...], preferred_element_type=F32)
        o_ref[...] = acc.astype(o_ref.dtype)

    def src_spec(m):
        return pl.BlockSpec((BLOCK, width), lambda i, t: (t[i * k_max + m], 0))

    return pl.pallas_call(
        body, name=name,
        grid_spec=pltpu.PrefetchScalarGridSpec(
            num_scalar_prefetch=1, grid=(n_tiles,),
            in_specs=[pl.BlockSpec((1, k_max, BLOCK, BLOCK), lambda i, t: (i, 0, 0, 0))] + [src_spec(m) for m in range(k_max)],
            out_specs=pl.BlockSpec((BLOCK, width), lambda i, t: (i, 0))),
        out_shape=jax.ShapeDtypeStruct((n_out, width), src.dtype),
        compiler_params=_cparams(("parallel",)),
    )(jnp.asarray(tab.reshape(-1)), jnp.asarray(sel, src.dtype), *([src] * k_max))


def _w_in_operand(win_g):
    return _take_rows(win_g.reshape(N_DEV * W_IN_ROWS, PACK_W), _w_in_row_maps()[0], name="w_in_rows")


def _mid_operands(wout_g, small_g):
    def full(n, off, r):
        a = small_g[:, off:off + r].reshape((N_DEV,) + _shard_shape(n))
        return jnp.moveaxis(a, 0, 1).reshape(FULL_SHAPE[n])

    w = {n: full(n, off, r) for n, off, r in SMALL_FLAT}
    ukv = w["w_ukv"].reshape(KV_LORA, MLA_HEADS, MLA_NOPE + MLA_V)
    return dict(
        wout=wout_g.reshape(D_MODEL, D_MODEL),
        wuq=_pad_heads(w["w_uq"], MLA_HEADS, MLA_QK, 1),
        wuk=_pad_heads(ukv[:, :, :MLA_NOPE].reshape(KV_LORA, -1), MLA_HEADS, MLA_NOPE, 1),
        wuv=_pad_heads(ukv[:, :, MLA_NOPE:].reshape(KV_LORA, -1), MLA_HEADS, MLA_V, 1),
        woa=_pad_heads(w["w_o_swa"], SWA_HEADS, HEAD_DIM, 0),
        wob=_pad_heads(w["w_o_mla"], MLA_HEADS, MLA_V, 0),
    )


def _mid_grad_pack(g):
    uk = _unpad_heads(g["wukv"][:, :1024], MLA_HEADS, MLA_NOPE, 1).reshape(KV_LORA, MLA_HEADS, MLA_NOPE)
    uv = _unpad_heads(g["wukv"][:, 1024:], MLA_HEADS, MLA_V, 1).reshape(KV_LORA, MLA_HEADS, MLA_V)
    w = dict(w_uq=_unpad_heads(g["wuq"], MLA_HEADS, MLA_QK, 1), w_ukv=jnp.concatenate([uk, uv], 2).reshape(KV_LORA, -1),
             w_o_swa=_unpad_heads(g["woa"], SWA_HEADS, HEAD_DIM, 0), w_o_mla=_unpad_heads(g["wob"], MLA_HEADS, MLA_V, 0))

    def flat(n, r):
        rr, cc = FULL_SHAPE[n]
        a = jnp.moveaxis(w[n].reshape(rr, N_DEV, cc // N_DEV), 1, 0).reshape(N_DEV, r, PACK_W)
        return jnp.pad(a, ((0, 0), (0, -r % ROW_TILE), (0, 0))).astype(WIRE_DTYPE)

    return jnp.concatenate([g["wout"].reshape(N_DEV, OUT_ROWS, PACK_W)] + [flat(n, r) for n, _, r in SMALL_FLAT], 1)


def _w_in_grad_chunks(g_win_t):
    return _take_rows(g_win_t, _w_in_row_maps()[1], name="dw_in_rows").reshape(N_DEV, W_IN_ROWS, PACK_W)


def _local_step(x, tgt, win_t, small, weights, grads):
    s_ = x.shape[0]
    tabs = _rope_tables(s_)
    sink_b = jnp.broadcast_to(small["swa_sinks"].reshape(SWA_KV_HEADS, SWA_GROUP, 1), (SWA_KV_HEADS, SWA_GROUP, LANES))
    sink_b = jnp.pad(sink_b, ((0, 0), (0, SUBLANES - SWA_GROUP), (0, 0)))

    h = _norm_fwd(x, small["mix_norm_g"], name="norm1")
    p = _mm(h, win_t, "nt", name="proj_in", tm=1024, tn=2176)
    qa, ka, va, cq, ckv, kro = _attn_prep(p, small["q_norm_g"], small["kv_norm_g"], tabs)
    ops = weights.mid(cq)
    oa32, oa16, lse_a = _swa_fwd(qa, ka, va, sink_b)
    qp = _mm(cq, ops["wuq"], "nn", name="mla_q_up", tm=1024, tn=1024)
    kp = _mm(ckv, ops["wuk"], "nn", name="mla_k_up", tm=1024, tn=1024)
    vp = _mm(ckv, ops["wuv"], "nn", name="mla_v_up", tm=1024, tn=1024, out_dtype=MXU_DTYPE)
    qc, kc = _mla_prep(qp, kp, kro, tabs)
    ob32, ob16, lse_b = _mla_fwd(qc, kc, vp)
    ta = _mm(oa16, ops["woa"], "nn", name="o_swa", tm=1024, tn=1024)
    tb = _mm(ob16, ops["wob"], "nn", name="o_mla", tm=1024, tn=1024)
    y = _gate_fwd(p, ta, tb)
    x1 = _mm(y, ops["wout"], "nn", name="out_proj", add=x, tm=1024, tn=1024)
    wgu_t, wd = weights.late(x1)
    h2 = _norm_fwd(x1, small["ffn_norm_g"], name="norm2")
    gu = _mm(h2, wgu_t, "nt", name="ffn_in", tm=1024, tn=1408)
    act = _swiglu_fwd(gu)
    x2 = _mm(act, wd, "nn", name="ffn_out", add=x1, tn=1024)

    dx2, dx2b, dg3, _, tot = _loss_bwd(x2, small["final_norm_g"].reshape(1, D_MODEL), tgt)
    g = {}
    dact = _mm(dx2b, wd, "nt", name="d_act", tm=1024, tn=1408)
    g_wd = _mm(act, dx2b, "tn", name="dw_down", tm=1408, tn=1024, tk=1024, out_dtype=WIRE_DTYPE)
    dgu = _swiglu_bwd(gu, dact)
    dh2 = _mm(dgu, wgu_t, "nn", name="d_h2", tn=1024, tk=2816)
    g_wgu = _mm(dgu, h2, "tn", name="dw_ffn_in", tm=1408, tn=1024, tk=1024, out_dtype=WIRE_DTYPE)
    token = grads.late(g_wgu, g_wd)
    dx1, dx1b, dg2 = _norm_bwd(x1, small["ffn_norm_g"] + token[0:1, 0:1], dh2, dx2, name="norm2_bwd")
    dy = _mm(dx1b, ops["wout"], "nt", name="d_y", tm=1024, tn=1024)
    g["wout"] = _mm(y, dx1b, "tn", name="dw_out", tm=1024, tn=1024, tk=1024, out_dtype=WIRE_DTYPE)
    dta, dtb, dgab = _gate_bwd(p, ta, tb, dy)
    doa = _mm(dta, ops["woa"], "nt", name="d_oa", tm=1024, tn=1024)
    g["woa"] = _mm(oa16, dta, "tn", name="dw_o_swa", tm=1024, tn=1024, tk=1024)
    dob = _mm(dtb, ops["wob"], "nt", name="d_ob", tm=1024, tn=1024)
    g["wob"] = _mm(ob16, dtb, "tn", name="dw_o_mla", tm=1024, tn=1024, tk=1024)
    dob16, delta_b = _mla_bwd_prep(dob, ob32)
    dqc, dkc, dvp = _mla_bwd(qc, kc, vp, dob16, lse_b, delta_b)
    dqp, dkv, dkr = _mla_unprep(dqc, dkc, dvp, tabs)
    dcq = _mm(dqp, ops["wuq"], "nt", name="d_cq", tn=Q_LORA)
    g["wuq"] = _mm(cq, dqp, "tn", name="dw_uq", tm=Q_LORA, tn=1024, tk=512)
    dckv = _mm(dkv, jnp.concatenate([ops["wuk"], ops["wuv"]], 1), "nt", name="d_ckv", tn=KV_LORA)
    g["wukv"] = _mm(ckv, dkv, "tn", name="dw_ukv", tm=KV_LORA, tn=1024, tk=512)
    token = grads.mid(g)
    _, dqlat, dgq = _norm_bwd(p, small["q_norm_g"] + token[0:1, 0:1], dcq, None, name="qnorm_bwd", x_cb=P_QLAT // Q_LORA)
    _, dkvlat, dgkv = _norm_bwd(p, small["kv_norm_g"], dckv, None, name="kvnorm_bwd", x_cb=P_KVLAT // KV_LORA)
    dqa, dka, dva, dsk = _swa_bwd(qa, ka, va, sink_b, oa32, doa, lse_a)
    dq_raw, dk_raw = _swa_unrope(dqa, dka, tabs)
    dp = jnp.concatenate([dgab, dq_raw, dqlat, dkr, dk_raw, dva, dkvlat], 1)
    token = grads.last(_mm(dp, h, "tn", name="dw_in", tm=2176, tn=1024, tk=1024, out_dtype=WIRE_DTYPE))
    dh = _mm(dp, win_t, "nn", name="d_h", after=token, tm=1024, tn=1024, tk=2176)
    gx, _, dg1 = _norm_bwd(x, small["mix_norm_g"], dh, dx1, name="norm1_bwd")

    sm = dict(mix_norm_g=dg1, ffn_norm_g=dg2, final_norm_g=dg3, q_norm_g=dgq, kv_norm_g=dgkv,
              swa_sinks=dsk[:, :SWA_GROUP, 0].reshape(1, SWA_HEADS))
    return tot, gx, sm


MESH = pl.DeviceIdType.MESH
ANY = pl.BlockSpec(memory_space=pl.ANY)


def _position():
    return lax.axis_index("x"), lax.axis_index("y"), lax.axis_index("c")


def _all_gather(block, pieces, shapes, *, name):
    n_out = len(shapes)
    n_rows = sum(p[3] for p in pieces)

    def body(x_ref, *refs):
        outs, (send_sems, recv_sems, local_sem) = refs[:n_out], refs[n_out:]
        x, y, c = _position()
        me, sibling = (x, y, c), (x, y, 1 - c)
        chips = [(1 - x, y), (x, 1 - y), (1 - x, 1 - y)]

        def dst(piece, blk):
            arr, lead, _, _ = piece
            return outs[arr].at[lead(4 * blk[0] + 2 * blk[1] + blk[2])]

        def own(piece):
            return x_ref.at[pl.ds(piece[2], piece[3])]

        def copies(k, blk, to, from_input):
            return [pltpu.make_async_remote_copy(
                src_ref=own(p) if from_input else dst(p, blk), dst_ref=dst(p, blk), send_sem=send_sems.at[k],
                recv_sem=recv_sems.at[k], device_id=to, device_id_type=MESH) for p in pieces]

        gathered_rows = x_ref.at[pl.ds(0, n_rows)]

        def whole_block(k):
            return pltpu.make_async_remote_copy(src_ref=gathered_rows, dst_ref=gathered_rows, send_sem=send_sems.at[k],
                                                recv_sem=recv_sems.at[k], device_id=me, device_id_type=MESH)

        for p in pieces:
            pltpu.make_async_copy(own(p), dst(p, me), local_sem).start()
        for cp in copies(0, me, sibling, True):
            cp.start()
        for j, chip in enumerate(chips):
            for cp in copies(1 + j, me, (*chip, c), True):
                cp.start()
        for j, chip in enumerate(chips):
            whole_block(1 + j).wait_recv()
            for cp in copies(4 + j, (*chip, c), sibling, False):
                cp.start()
        whole_block(0).wait_recv()
        for j in range(3):
            whole_block(4 + j).wait_recv()
        for k in range(7):
            whole_block(k).wait_send()
        pltpu.make_async_copy(gathered_rows, gathered_rows, local_sem).wait()

    return pl.pallas_call(
        body, name=name, out_shape=[jax.ShapeDtypeStruct(s, block.dtype) for s in shapes], in_specs=[ANY],
        out_specs=[ANY] * n_out,
        scratch_shapes=[pltpu.SemaphoreType.DMA((7,)), pltpu.SemaphoreType.DMA((7,)), pltpu.SemaphoreType.DMA],
    )(block)


HBM = pl.BlockSpec(memory_space=pltpu.HBM)
SEM = pl.BlockSpec(memory_space=pltpu.SEMAPHORE)
GU_SHAPE = (2, N_DEV, FF_COLS, PACK_W)
D_SHAPE = (N_DEV, FF_COLS, PACK_W)
LAND_SHAPE = (N_DEV, LATE_ROWS, PACK_W)


def _split_params():
    return pltpu.CompilerParams(has_side_effects=pltpu.SideEffectType.DATAFLOW_SIDE_EFFECTING)


def _peer(x, y, c, k):
    return ((1 - x) if k & 4 else x, (1 - y) if k & 2 else y, (1 - c) if k & 1 else c)


def _empty_hbm(shape, dtype):
    return pltpu.with_memory_space_constraint(lax.empty(shape, dtype), pltpu.HBM)


def _wait_all(rows, send_sems, recv_sems, me):
    for k in range(N_DEV - 1):
        cp = pltpu.make_async_remote_copy(src_ref=rows, dst_ref=rows, send_sem=send_sems.at[k], recv_sem=recv_sems.at[k],
                                          device_id=me, device_id_type=MESH)
        cp.wait_send()
        cp.wait_recv()


def _token_shape():
    return jax.ShapeDtypeStruct((SUBLANES, LANES), F32)


def _gather_start(pack, row0, pieces, shapes, *, name):
    n = len(shapes)

    def body(*refs):
        p_ref, bufs, send_sems, recv_sems, token = refs[0], refs[1:1 + n], refs[1 + n], refs[2 + n], refs[-1]
        x, y, c = _position()
        me = 4 * x + 2 * y + c
        for k in range(1, N_DEV):
            off = row0
            for buf, lead, rows in pieces:
                pltpu.make_async_remote_copy(
                    src_ref=p_ref.at[pl.ds(off, rows)], dst_ref=bufs[buf].at[lead(me)], send_sem=send_sems.at[k - 1],
                    recv_sem=recv_sems.at[k - 1], device_id=_peer(x, y, c, k), device_id_type=MESH).start()
                off += rows
        token[...] = jnp.zeros_like(token)

    sems, dt = pltpu.SemaphoreType.DMA((N_DEV - 1,)), pack.dtype
    return pl.pallas_call(
        body, name=name,
        out_shape=(sems, sems, pltpu.HBM(pack.shape, dt)) + tuple(pltpu.HBM(s, dt) for s in shapes) + (_token_shape(),),
        in_specs=(HBM,) * (1 + n), out_specs=(SEM, SEM) + (HBM,) * (1 + n) + (pl.BlockSpec(memory_space=pltpu.VMEM),),
        input_output_aliases={i: 2 + i for i in range(1 + n)}, compiler_params=_split_params(),
    )(pltpu.with_memory_space_constraint(pack, pltpu.HBM), *[_empty_hbm(s, dt) for s in shapes])


def _gather_wait(started, row0, n_rows, after, *, name):
    send_sems, recv_sems, pack, *bufs = started[:-1]
    n = len(bufs)

    def body(*refs):
        _wait_all(refs[0].at[pl.ds(row0, n_rows)], refs[1 + n], refs[2 + n], _position())

    outs = pl.pallas_call(
        body, name=name, out_shape=tuple(pltpu.HBM(a.shape, a.dtype) for a in (pack, *bufs)),
        in_specs=(HBM,) * (1 + n) + (SEM, SEM, ANY), out_specs=(HBM,) * (1 + n),
        input_output_aliases={i: i for i in range(1 + n)}, compiler_params=_split_params(),
    )(pack, *bufs, send_sems, recv_sems, after)
    return outs[0], outs[1:]


def _scatter_start(srcs, pieces, *, name):
    n = len(srcs)
    land_shape = (N_DEV, sum(p[2] for p in pieces), PACK_W)

    def body(*refs):
        src_refs, land_ref, send_sems, recv_sems, token = refs[:n], refs[n], refs[n + 1], refs[n + 2], refs[-1]
        x, y, c = _position()
        me = 4 * x + 2 * y + c
        for k in range(1, N_DEV):
            px, py, pc = _peer(x, y, c, k)
            off = 0
            for si, lead, rows in pieces:
                pltpu.make_async_remote_copy(
                    src_ref=src_refs[si].at[lead(4 * px + 2 * py + pc)], dst_ref=land_ref.at[me, pl.ds(off, rows)],
                    send_sem=send_sems.at[k - 1], recv_sem=recv_sems.at[k - 1], device_id=(px, py, pc),
                    device_id_type=MESH).start()
                off += rows
        token[...] = jnp.zeros_like(token)

    sems, dt = pltpu.SemaphoreType.DMA((N_DEV - 1,)), srcs[0].dtype
    return pl.pallas_call(
        body, name=name,
        out_shape=(sems, sems) + tuple(pltpu.HBM(a.shape, dt) for a in srcs) + (pltpu.HBM(land_shape, dt), _token_shape()),
        in_specs=(HBM,) * (n + 1), out_specs=(SEM, SEM) + (HBM,) * (n + 1) + (pl.BlockSpec(memory_space=pltpu.VMEM),),
        input_output_aliases={i: 2 + i for i in range(n + 1)}, compiler_params=_split_params(),
    )(*[pltpu.with_memory_space_constraint(a, pltpu.HBM) for a in srcs], _empty_hbm(land_shape, dt))


def _scatter_wait(started, after, *, name):
    send_sems, recv_sems, *bufs = started[:-1]
    n = len(bufs)

    def body(*refs):
        _wait_all(refs[n - 1].at[0], refs[n], refs[n + 1], _position())

    return pl.pallas_call(
        body, name=name, out_shape=tuple(pltpu.HBM(a.shape, a.dtype) for a in bufs),
        in_specs=(HBM,) * n + (SEM, SEM, ANY), out_specs=(HBM,) * n, input_output_aliases={i: i for i in range(n)},
        compiler_params=_split_params(),
    )(*bufs, send_sems, recv_sems, after)


def _peer_sum(own, own_lead, land, block, rows, idx, *, name):
    lead_rank = own.ndim - 2

    def body(idx_ref, own_ref, *refs):
        o_ref = refs[N_DEV - 1]
        acc = own_ref[(0,) * lead_rank].astype(F32)
        for k in range(N_DEV - 1):
            acc = acc + refs[k][0].astype(F32)
        o_ref[...] = acc

    own_spec = pl.BlockSpec((1,) * lead_rank + (rows, PACK_W), lambda i, t: own_lead(t[0]) + (0, 0))

    def land_spec(k):
        return pl.BlockSpec((1, rows, PACK_W), lambda i, t: (t[k + 1], block, 0))

    return pl.pallas_call(
        body, name=name,
        grid_spec=pltpu.PrefetchScalarGridSpec(
            num_scalar_prefetch=1, grid=(1,), in_specs=[own_spec] + [land_spec(k) for k in range(N_DEV - 1)],
            out_specs=pl.BlockSpec((rows, PACK_W), lambda i, t: (0, 0))),
        out_shape=jax.ShapeDtypeStruct((rows, PACK_W), F32), compiler_params=_cparams(("arbitrary",)),
    )(idx, own, *([land] * (N_DEV - 1)))


def _adamw(w, g, m, v):
    m = ADAM_B1 * m + (1.0 - ADAM_B1) * g
    v = ADAM_B2 * v + (1.0 - ADAM_B2) * (g * g)
    m_hat = m / (1.0 - ADAM_B1 ** ADAM_STEP)
    v_hat = v / (1.0 - ADAM_B2 ** ADAM_STEP)
    delta = -ADAM_LR * (m_hat / (jnp.sqrt(v_hat) + ADAM_EPS) + ADAM_WD * w)
    return delta, m, v


def _adamw_call(w, g, m, v, *, name, max_rows=256):
    r, c_ = w.shape
    tr = max_rows if r > max_rows and r % max_rows == 0 else r

    def body(w_ref, g_ref, m_ref, v_ref, d_ref, mo_ref, vo_ref):
        d, mn, vn = _adamw(w_ref[...], g_ref[...], m_ref[...], v_ref[...])
        d_ref[...] = d
        mo_ref[...] = mn
        vo_ref[...] = vn

    row = pl.BlockSpec((tr, c_), lambda i: (i, 0))
    shp = jax.ShapeDtypeStruct((r, c_), F32)
    return pl.pallas_call(
        body, name=name, grid=(r // tr,), in_specs=[row] * 4, out_specs=[row] * 3, out_shape=[shp] * 3,
        compiler_params=_cparams(("parallel",)),
    )(w, g, m, v)


SMALL = ("mix_norm_g", "ffn_norm_g", "final_norm_g", "q_norm_g", "kv_norm_g", "swa_sinks")
SMALL_W = dict(mix_norm_g=1024, ffn_norm_g=1024, final_norm_g=1024, q_norm_g=Q_LORA, kv_norm_g=KV_LORA, swa_sinks=SWA_HEADS)


def _small_adamw(parts, w, m, v):
    n_par = parts.shape[1] // SUBLANES

    def body(p_ref, w_ref, m_ref, v_ref, g_ref, d_ref, mo_ref, vo_ref):
        tot = p_ref[0]
        for dev in range(1, N_DEV):
            tot = tot + p_ref[dev]
        row_id = lax.broadcasted_iota(jnp.int32, (SUBLANES, PACK_W), 0)
        g = jnp.zeros((SUBLANES, PACK_W), F32)
        for k in range(n_par):
            g = jnp.where(row_id == k, jnp.sum(tot[k * SUBLANES:(k + 1) * SUBLANES, :], axis=0, keepdims=True), g)
        d, mn, vn = _adamw(w_ref[...], g, m_ref[...], v_ref[...])
        g_ref[...] = g
        d_ref[...] = d
        mo_ref[...] = mn
        vo_ref[...] = vn

    shp = jax.ShapeDtypeStruct((SUBLANES, PACK_W), F32)
    vm = pl.BlockSpec(memory_space=pltpu.VMEM)
    return pl.pallas_call(body, name="small_adamw", in_specs=[vm] * 4, out_specs=[vm] * 4, out_shape=[shp] * 4)(parts, w, m, v)


def _small_pack(d, rows_each):
    parts = [jnp.pad(d[n].astype(F32), ((0, 0), (0, PACK_W - SMALL_W[n]))) for n in SMALL]
    out = jnp.concatenate(parts, 0)
    pad = -out.shape[0] % SUBLANES
    return jnp.pad(out, ((0, pad), (0, 0)))


def kernel(x, mix_norm_g, w_in, swa_sinks, q_norm_g, w_uq, kv_norm_g, w_ukv, w_o_swa, w_o_mla, w_out, ffn_norm_g, w_gate, w_up, w_down, final_norm_g, loss_target, m_mix_norm_g, m_w_in, m_swa_sinks, m_q_norm_g, m_w_uq, m_kv_norm_g, m_w_ukv, m_w_o_swa, m_w_o_mla, m_w_out, m_ffn_norm_g, m_w_gate, m_w_up, m_w_down, m_final_norm_g, v_mix_norm_g, v_w_in, v_swa_sinks, v_q_norm_g, v_w_uq, v_kv_norm_g, v_w_ukv, v_w_o_swa, v_w_o_mla, v_w_out, v_ffn_norm_g, v_w_gate, v_w_up, v_w_down, v_final_norm_g):
    big_w = dict(w_in=w_in[0], w_uq=w_uq[0], w_ukv=w_ukv[0], w_o_swa=w_o_swa[0], w_o_mla=w_o_mla[0], w_out=w_out[0],
                 w_gate=w_gate[0], w_up=w_up[0], w_down=w_down[0])
    big_m = dict(w_in=m_w_in[0], w_uq=m_w_uq[0], w_ukv=m_w_ukv[0], w_o_swa=m_w_o_swa[0], w_o_mla=m_w_o_mla[0],
                 w_out=m_w_out[0], w_gate=m_w_gate[0], w_up=m_w_up[0], w_down=m_w_down[0])
    big_v = dict(w_in=v_w_in[0], w_uq=v_w_uq[0], w_ukv=v_w_ukv[0], w_o_swa=v_w_o_swa[0], w_o_mla=v_w_o_mla[0],
                 w_out=v_w_out[0], w_gate=v_w_gate[0], w_up=v_w_up[0], w_down=v_w_down[0])
    small_w = dict(mix_norm_g=mix_norm_g, ffn_norm_g=ffn_norm_g, final_norm_g=final_norm_g.reshape(1, D_MODEL),
                   q_norm_g=q_norm_g, kv_norm_g=kv_norm_g, swa_sinks=swa_sinks)
    small_m = dict(mix_norm_g=m_mix_norm_g, ffn_norm_g=m_ffn_norm_g, final_norm_g=m_final_norm_g.reshape(1, D_MODEL),
                   q_norm_g=m_q_norm_g, kv_norm_g=m_kv_norm_g, swa_sinks=m_swa_sinks)
    small_v = dict(mix_norm_g=v_mix_norm_g, ffn_norm_g=v_ffn_norm_g, final_norm_g=v_final_norm_g.reshape(1, D_MODEL),
                   q_norm_g=v_q_norm_g, kv_norm_g=v_kv_norm_g, swa_sinks=v_swa_sinks)

    px, py, pc = _position()
    me = 4 * px + 2 * py + pc
    idx = jnp.stack([me] + [4 * qx + 2 * qy + qc for qx, qy, qc in (_peer(px, py, pc, k) for k in range(1, N_DEV))])
    idx = idx.astype(jnp.int32)

    dev = lambda d: (d,)
    pack = _wire_pack(big_w, WIRE_DTYPE)
    win_g, = _all_gather(pack, ((0, dev, 0, W_IN_ROWS),), ((N_DEV, W_IN_ROWS, PACK_W),), name="ag_early")
    ag_mid = _gather_start(pack, W_IN_ROWS, ((0, dev, OUT_ROWS), (1, dev, SMALL_ROWS)),
                           ((N_DEV, OUT_ROWS, PACK_W), (N_DEV, SMALL_ROWS, PACK_W)), name="ag_mid_start")
    ag = {}

    def own_rows(r0, r1, shape):
        return pack[r0:r1].reshape(shape)

    def mid_weights(after):
        pack_mid, (wout_g, small_g) = _gather_wait(ag_mid, W_IN_ROWS, MID_ROWS, after, name="ag_mid_wait")
        ag["late"] = _gather_start(pack_mid, EARLY_ROWS, ((0, lambda d: (0, d), FF_COLS), (0, lambda d: (1, d), FF_COLS),
                                                           (1, dev, FF_COLS)), (GU_SHAPE, D_SHAPE), name="ag_late_start")
        wout_g = lax.dynamic_update_slice(wout_g, own_rows(W_IN_ROWS, SMALL_ROW0, (1, OUT_ROWS, PACK_W)), (me, 0, 0))
        small_g = lax.dynamic_update_slice(small_g, own_rows(SMALL_ROW0, EARLY_ROWS, (1, SMALL_ROWS, PACK_W)), (me, 0, 0))
        ops = _mid_operands(wout_g, small_g)
        ops["wuq"] = ops["wuq"] + ag["late"][-1][0:1, 0:1].astype(ops["wuq"].dtype)
        return ops

    def late_weights(after):
        _, (gu, d) = _gather_wait(ag["late"], EARLY_ROWS, LATE_ROWS, after, name="ag_late_wait")
        gu = lax.dynamic_update_slice(gu, own_rows(EARLY_ROWS, EARLY_ROWS + 2 * FF_COLS, (2, 1, FF_COLS, PACK_W)), (0, me, 0, 0))
        d = lax.dynamic_update_slice(d, own_rows(EARLY_ROWS + 2 * FF_COLS, PACK_ROWS, (1, FF_COLS, PACK_W)), (me, 0, 0))
        return gu.reshape(2 * D_FF, D_MODEL), d.reshape(D_FF, D_MODEL)

    rs = {}

    def late_grads(g_gu, g_d):
        rs["late"] = _scatter_start([g_gu.reshape(GU_SHAPE), g_d.reshape(D_SHAPE)],
                                    ((0, lambda d: (0, d), FF_COLS), (0, lambda d: (1, d), FF_COLS), (1, dev, FF_COLS)),
                                    name="rs_late_start")
        return rs["late"][-1]

    def mid_grads(g):
        rs["mid"] = _scatter_start([_mid_grad_pack(g)], ((0, dev, MID_ROWS),), name="rs_mid_start")
        return rs["mid"][-1]

    def last_grads(g_win_t):
        rs["last"] = _scatter_start([_w_in_grad_chunks(g_win_t)], ((0, dev, W_IN_ROWS),), name="rs_last_start")
        return rs["last"][-1]

    first_w = dict(small_w, mix_norm_g=mix_norm_g + ag_mid[-1][0:1, 0:1])
    loss_tot, gx, g_small = _local_step(
        x[0], loss_target[0], _w_in_operand(win_g), first_w, types.SimpleNamespace(mid=mid_weights, late=late_weights),
        types.SimpleNamespace(late=late_grads, mid=mid_grads, last=last_grads))

    g_gu, g_d, land_late = _scatter_wait(rs["late"], gx, name="rs_late_wait")
    g_mid, land_mid = _scatter_wait(rs["mid"], gx, name="rs_mid_wait")
    g_win, land_last = _scatter_wait(rs["last"], gx, name="rs_last_wait")
    gw = dict(w_gate=_peer_sum(g_gu, lambda m: (0, m), land_late, 0, FF_COLS, idx, name="rs_sum_gate").T,
              w_up=_peer_sum(g_gu, lambda m: (1, m), land_late, 1, FF_COLS, idx, name="rs_sum_up").T,
              w_down=_peer_sum(g_d, dev, land_late, 2, FF_COLS, idx, name="rs_sum_down"),
              w_in=_peer_sum(g_win, dev, land_last, 0, W_IN_ROWS, idx, name="rs_sum_in")[0:W_IN_COLS].T)
    gw.update(_mid_unpack(_peer_sum(g_mid, dev, land_mid, 0, MID_ROWS, idx, name="rs_sum_mid")))
    dw, mw, vw = {}, {}, {}
    for n in BIG:
        dw[n], mw[n], vw[n] = _adamw_call(big_w[n], gw[n], big_m[n], big_v[n], name="adamw_" + n)

    loss_rows = jnp.pad(loss_tot[0:1, 0:1], ((0, SUBLANES - 1), (0, PACK_W - 1)))
    small_rows = jnp.concatenate([_small_pack(g_small_rows(g_small), SUBLANES), loss_rows], 0)
    parts, = _all_gather(small_rows, ((0, lambda d: (d,), 0, small_rows.shape[0]),), ((N_DEV,) + small_rows.shape,),
                         name="ag_small")
    gs, ds, ms, vs = _small_adamw(parts, _small_pack(small_w, 1), _small_pack(small_m, 1), _small_pack(small_v, 1))
    loss = gs[len(SMALL), 0]

    def small_out(packed):
        out = {}
        for k, n in enumerate(SMALL):
            out[n] = packed[k:k + 1, :SMALL_W[n]]
        out["final_norm_g"] = out["final_norm_g"].reshape(D_MODEL)
        return out

    gs, ds, ms, vs = small_out(gs), small_out(ds), small_out(ms), small_out(vs)

    order = ("mix_norm_g", "w_in", "swa_sinks", "q_norm_g", "w_uq", "kv_norm_g", "w_ukv", "w_o_swa", "w_o_mla", "w_out",
             "ffn_norm_g", "w_gate", "w_up", "w_down", "final_norm_g")

    def leaves(big, small):
        return [big[n][None] if n in big else small[n] for n in order]

    return (loss, gx[None], *leaves(gw, gs), *leaves(dw, ds), *leaves(mw, ms), *leaves(vw, vs))


def g_small_rows(g_small):
    out = dict(g_small)
    out["swa_sinks"] = jnp.pad(g_small["swa_sinks"], ((0, SUBLANES - 1), (0, 0)))
    return out
```

```python
import types

import numpy as np
import jax
import jax.numpy as jnp
from jax import lax
from jax.experimental import pallas as pl
from jax.experimental.pallas import tpu as pltpu

F32 = jnp.float32
MXU_DTYPE = jnp.bfloat16
WIRE_DTYPE = jnp.bfloat16

D_MODEL = 1024
EPS = 1e-6
ROPE_THETA = 10000.0
BLOCK = 128
HEAD_DIM = 64
SWA_HEADS = 8
SWA_KV_HEADS = 2
SWA_GROUP = SWA_HEADS // SWA_KV_HEADS
MLA_HEADS = 8
MLA_NOPE = 64
MLA_ROPE = 32
MLA_V = 64
MLA_QK = MLA_NOPE + MLA_ROPE
Q_LORA = 384
KV_LORA = 256
D_FF = 2816
IN_SIZES = (512, 128, 128, Q_LORA, KV_LORA, MLA_ROPE, D_MODEL, D_MODEL)
IN_OFF = tuple(int(v) for v in np.cumsum((0,) + IN_SIZES))
ADAM_LR, ADAM_B1, ADAM_B2, ADAM_EPS, ADAM_WD, ADAM_STEP = 0.001, 0.9, 0.999, 1e-08, 0.01, 10

LANES = 128
SUBLANES = 8
VMEM_LIMIT = 48 * 1024 * 1024
N_DEV = 8
AXES = ("x", "y", "c")

P_GA, P_GB, P_Q, P_QLAT, P_KR, P_K, P_V, P_KVLAT, P_W = 0, 1024, 2048, 3072, 3456, 3584, 3840, 4096, 4352
KR_LANE = 64

LOG2E = 1.4426950408889634

NT = (((1,), (1,)), ((), ()))
NN = (((1,), (0,)), ((), ()))
TN = (((0,), (0,)), ((), ()))


def _cparams(sem):
    return pltpu.CompilerParams(dimension_semantics=sem, vmem_limit_bytes=VMEM_LIMIT)


def _mm(a, b, mode, *, name, out_dtype=F32, add=None, after=None, tm=512, tn=512, tk=None):
    if mode == "nn":
        (M, K), (K2, N) = a.shape, b.shape
    elif mode == "nt":
        (M, K), (N, K2) = a.shape, b.shape
    else:
        (K, M), (K2, N) = a.shape, b.shape
    assert K == K2, (a.shape, b.shape, mode)
    tk = K if tk is None else tk
    tm, tn = min(tm, M), min(tn, N)
    assert M % tm == 0 and N % tn == 0 and K % tk == 0, (M, N, K, tm, tn, tk)
    nk = K // tk
    dn = {"nn": NN, "nt": NT, "tn": TN}[mode]
    if mode == "tn":
        a_spec = pl.BlockSpec((tk, tm), lambda i, j, k: (k, i))
    else:
        a_spec = pl.BlockSpec((tm, tk), lambda i, j, k: (i, k))
    if mode == "nt":
        b_spec = pl.BlockSpec((tn, tk), lambda i, j, k: (j, k))
    else:
        b_spec = pl.BlockSpec((tk, tn), lambda i, j, k: (k, j))
    o_spec = pl.BlockSpec((tm, tn), lambda i, j, k: (i, j))
    has_add, has_after = add is not None, after is not None

    def body(*refs):
        a_ref, b_ref = refs[0], refs[1]
        add_ref = refs[2] if has_add else None
        o_ref = refs[2 + has_add + has_after]
        p = lax.dot_general(a_ref[...], b_ref[...], dn, preferred_element_type=F32)

        def finish(acc):
            if has_add:
                acc = acc + add_ref[...]
            o_ref[...] = acc.astype(o_ref.dtype)

        if nk == 1:
            finish(p)
        else:
            acc_ref = refs[-1]
            k = pl.program_id(2)

            @pl.when(k == 0)
            def _():
                acc_ref[...] = p

            @pl.when(k > 0)
            def _():
                acc_ref[...] += p

            @pl.when(k == nk - 1)
            def _():
                finish(acc_ref[...])

    ins = [a, b] + ([add] if has_add else []) + ([after] if has_after else [])
    in_specs = [a_spec, b_spec] + ([o_spec] if has_add else []) + ([pl.BlockSpec(memory_space=pl.ANY)] if has_after else [])
    return pl.pallas_call(
        body, name=name, grid=(M // tm, N // tn, nk), in_specs=in_specs, out_specs=o_spec,
        out_shape=jax.ShapeDtypeStruct((M, N), out_dtype),
        scratch_shapes=[pltpu.VMEM((tm, tn), F32)] if nk > 1 else [],
        compiler_params=_cparams(("parallel", "parallel", "arbitrary")),
    )(*ins)


def _rows(ts, w, cb=0):
    return pl.BlockSpec((ts, w), lambda i: (i, cb))


def _const(r, w):
    return pl.BlockSpec((r, w), lambda i: (0, 0))


def _sublane_sum(v):
    ts, c = v.shape
    return jnp.sum(v.reshape(ts // SUBLANES, SUBLANES, c), axis=0)


def _sigmoid(v):
    return 1.0 / (1.0 + jnp.exp(-v))


def _rope(v, cos, s_up, s_dn, up, dn):
    return v * cos + pltpu.roll(v, up, 1) * s_up + pltpu.roll(v, dn, 1) * s_dn


def _rope_t(dv, cos, s_up, s_dn, up, dn):
    return dv * cos + pltpu.roll(dv * s_up, dn, 1) + pltpu.roll(dv * s_dn, up, 1)


def _rope_tables(seq):
    pos = np.arange(seq, dtype=np.float32)[:, None]

    def base(dim):
        inv = np.float32(ROPE_THETA) ** (-np.arange(0, dim, 2, dtype=np.float32) / np.float32(dim))
        ang = (pos * inv.astype(np.float32)[None, :]).astype(np.float32)
        return np.cos(ang).astype(np.float32), np.sin(ang).astype(np.float32)

    z = lambda n: np.zeros((seq, n), np.float32)
    ca, sa = base(HEAD_DIM)
    a_cos = np.concatenate([ca, ca, z(64)], 1)
    a_up = np.concatenate([-sa, z(96)], 1)
    a_dn = np.concatenate([z(32), sa, z(64)], 1)
    cb, sb = base(MLA_ROPE)
    one = np.ones((seq, 64), np.float32)
    q_cos = np.concatenate([one, cb, cb, z(32)], 1)
    k_cos = np.concatenate([z(64), cb, cb, z(32)], 1)
    b_up = np.concatenate([z(64), -sb, z(48)], 1)
    b_dn = np.concatenate([z(80), sb, z(32)], 1)
    return tuple(jnp.asarray(t) for t in (a_cos, a_up, a_dn, q_cos, k_cos, b_up, b_dn))


def _norm_fwd(x, g, *, name, ts=256):
    s_, c = x.shape

    def body(x_ref, g_ref, h_ref):
        v = x_ref[...]
        r = lax.rsqrt(jnp.mean(v * v, axis=-1, keepdims=True) + EPS)
        h_ref[...] = (v * r * g_ref[...]).astype(h_ref.dtype)

    return pl.pallas_call(
        body, name=name, grid=(s_ // ts,), in_specs=[_rows(ts, c), _const(1, c)], out_specs=_rows(ts, c),
        out_shape=jax.ShapeDtypeStruct((s_, c), MXU_DTYPE), compiler_params=_cparams(("parallel",)),
    )(x, g)


def _norm_bwd(x, g, dy, res, *, name, ts=256, x_cb=0, x_src_w=None):
    s_ = x.shape[0]
    c = dy.shape[1]
    has_res = res is not None

    def body(*refs):
        x_ref, g_ref, dy_ref = refs[0], refs[1], refs[2]
        res_ref = refs[3] if has_res else None
        dx_ref, dxb_ref, dg_ref = refs[-3], refs[-2], refs[-1]
        v = x_ref[...]
        r = lax.rsqrt(jnp.mean(v * v, axis=-1, keepdims=True) + EPS)
        xh = v * r
        d = dy_ref[...]
        dxh = d * g_ref[...]
        dx = r * (dxh - xh * jnp.mean(dxh * xh, axis=-1, keepdims=True))
        if has_res:
            dx = dx + res_ref[...]
        dx_ref[...] = dx
        dxb_ref[...] = dx.astype(dxb_ref.dtype)

        @pl.when(pl.program_id(0) == 0)
        def _():
            dg_ref[...] = jnp.zeros(dg_ref.shape, F32)

        dg_ref[...] += _sublane_sum(d * xh)

    ins = [x, g, dy] + ([res] if has_res else [])
    in_specs = [_rows(ts, c, x_cb), _const(1, c), _rows(ts, c)] + ([_rows(ts, c)] if has_res else [])
    return pl.pallas_call(
        body, name=name, grid=(s_ // ts,), in_specs=in_specs,
        out_specs=[_rows(ts, c), _rows(ts, c), _const(SUBLANES, c)],
        out_shape=[jax.ShapeDtypeStruct((s_, c), F32), jax.ShapeDtypeStruct((s_, c), MXU_DTYPE),
                   jax.ShapeDtypeStruct((SUBLANES, c), F32)],
        compiler_params=_cparams(("arbitrary",)),
    )(*ins)


def _attn_prep(p, gq, gkv, tabs, *, ts=256):
    s_ = p.shape[0]
    a_cos, a_up, a_dn, _, k_cos, b_up, b_dn = tabs

    def body(q_ref, k_ref, v_ref, ql_ref, kvl_ref, kr_ref, gq_ref, gkv_ref, ac, au, ad, kc, bu, bd,
             qa_ref, ka_ref, va_ref, cq_ref, ckv_ref, kro_ref):
        c_, u_, d_ = ac[...], au[...], ad[...]
        for h in range(SWA_HEADS):
            sl = slice(h * LANES, (h + 1) * LANES)
            qa_ref[:, sl] = _rope(q_ref[:, sl], c_, u_, d_, 96, 32).astype(qa_ref.dtype)
        for h in range(SWA_KV_HEADS):
            sl = slice(h * LANES, (h + 1) * LANES)
            ka_ref[:, sl] = _rope(k_ref[:, sl], c_, u_, d_, 96, 32).astype(ka_ref.dtype)
        va_ref[...] = v_ref[...].astype(va_ref.dtype)
        for src, gref, dst in ((ql_ref, gq_ref, cq_ref), (kvl_ref, gkv_ref, ckv_ref)):
            v = src[...]
            r = lax.rsqrt(jnp.mean(v * v, axis=-1, keepdims=True) + EPS)
            dst[...] = (v * r * gref[...]).astype(dst.dtype)
        kro_ref[...] = _rope(kr_ref[...], kc[...], bu[...], bd[...], 112, 16)

    tab = _rows(ts, LANES)
    return pl.pallas_call(
        body, name="attn_prep", grid=(s_ // ts,),
        in_specs=[_rows(ts, 1024, P_Q // 1024), _rows(ts, 256, P_K // 256), _rows(ts, 256, P_V // 256),
                  _rows(ts, Q_LORA, P_QLAT // Q_LORA), _rows(ts, KV_LORA, P_KVLAT // KV_LORA),
                  _rows(ts, LANES, P_KR // LANES), _const(1, Q_LORA), _const(1, KV_LORA), tab, tab, tab, tab, tab, tab],
        out_specs=[_rows(ts, 1024), _rows(ts, 256), _rows(ts, 256), _rows(ts, Q_LORA), _rows(ts, KV_LORA),
                   _rows(ts, LANES)],
        out_shape=[jax.ShapeDtypeStruct((s_, 1024), MXU_DTYPE), jax.ShapeDtypeStruct((s_, 256), MXU_DTYPE),
                   jax.ShapeDtypeStruct((s_, 256), MXU_DTYPE), jax.ShapeDtypeStruct((s_, Q_LORA), MXU_DTYPE),
                   jax.ShapeDtypeStruct((s_, KV_LORA), MXU_DTYPE), jax.ShapeDtypeStruct((s_, LANES), F32)],
        compiler_params=_cparams(("parallel",)),
    )(p, p, p, p, p, p, gq, gkv, a_cos, a_up, a_dn, k_cos, b_up, b_dn)


def _mla_prep(qp, kp, kro, tabs, *, ts=256):
    s_ = qp.shape[0]
    _, _, _, q_cos, _, b_up, b_dn = tabs

    def body(q_ref, k_ref, kr_ref, qc, bu, bd, qo_ref, ko_ref):
        c_, u_, d_ = qc[...], bu[...], bd[...]
        kr = kr_ref[...]
        for h in range(MLA_HEADS):
            sl = slice(h * LANES, (h + 1) * LANES)
            qo_ref[:, sl] = _rope(q_ref[:, sl], c_, u_, d_, 112, 16).astype(qo_ref.dtype)
            ko_ref[:, sl] = (k_ref[:, sl] + kr).astype(ko_ref.dtype)

    tab = _rows(ts, LANES)
    return pl.pallas_call(
        body, name="mla_prep", grid=(s_ // ts,),
        in_specs=[_rows(ts, 1024), _rows(ts, 1024), tab, tab, tab, tab],
        out_specs=[_rows(ts, 1024), _rows(ts, 1024)],
        out_shape=[jax.ShapeDtypeStruct((s_, 1024), MXU_DTYPE)] * 2,
        compiler_params=_cparams(("parallel",)),
    )(qp, kp, kro, q_cos, b_up, b_dn)


def _mla_unprep(dqc, dkc, dvp, tabs, *, ts=256):
    s_ = dqc.shape[0]
    _, _, _, q_cos, k_cos, b_up, b_dn = tabs

    def body(dq_ref, dk_ref, dv_ref, qc, kc, bu, bd, dqo_ref, dkvo_ref, dkr_ref):
        c_, u_, d_ = qc[...], bu[...], bd[...]
        tot = jnp.zeros((ts, LANES), F32)
        for h in range(MLA_HEADS):
            sl = slice(h * LANES, (h + 1) * LANES)
            dqo_ref[:, sl] = _rope_t(dq_ref[:, sl], c_, u_, d_, 112, 16).astype(dqo_ref.dtype)
            dk = dk_ref[:, sl]
            dkvo_ref[:, sl] = dk.astype(dkvo_ref.dtype)
            tot = tot + dk
        dkvo_ref[:, 1024:2048] = dv_ref[...].astype(dkvo_ref.dtype)
        dkr_ref[...] = _rope_t(tot, kc[...], u_, d_, 112, 16).astype(dkr_ref.dtype)

    tab = _rows(ts, LANES)
    return pl.pallas_call(
        body, name="mla_unprep", grid=(s_ // ts,),
        in_specs=[_rows(ts, 1024), _rows(ts, 1024), _rows(ts, 1024), tab, tab, tab, tab],
        out_specs=[_rows(ts, 1024), _rows(ts, 2048), _rows(ts, LANES)],
        out_shape=[jax.ShapeDtypeStruct((s_, 1024), MXU_DTYPE), jax.ShapeDtypeStruct((s_, 2048), MXU_DTYPE),
                   jax.ShapeDtypeStruct((s_, LANES), MXU_DTYPE)],
        compiler_params=_cparams(("parallel",)),
    )(dqc, dkc, dvp, q_cos, k_cos, b_up, b_dn)


def _swa_unrope(dqa, dka, tabs, *, ts=256):
    s_ = dqa.shape[0]
    a_cos, a_up, a_dn = tabs[0], tabs[1], tabs[2]

    def body(dq_ref, dk_ref, ac, au, ad, dqo_ref, dko_ref):
        c_, u_, d_ = ac[...], au[...], ad[...]
        for h in range(SWA_HEADS):
            sl = slice(h * LANES, (h + 1) * LANES)
            dqo_ref[:, sl] = _rope_t(dq_ref[:, sl], c_, u_, d_, 96, 32).astype(dqo_ref.dtype)
        for h in range(SWA_KV_HEADS):
            sl = slice(h * LANES, (h + 1) * LANES)
            dko_ref[:, sl] = _rope_t(dk_ref[:, sl], c_, u_, d_, 96, 32).astype(dko_ref.dtype)

    tab = _rows(ts, LANES)
    return pl.pallas_call(
        body, name="swa_unrope", grid=(s_ // ts,),
        in_specs=[_rows(ts, 1024), _rows(ts, 256), tab, tab, tab],
        out_specs=[_rows(ts, 1024), _rows(ts, 256)],
        out_shape=[jax.ShapeDtypeStruct((s_, 1024), MXU_DTYPE), jax.ShapeDtypeStruct((s_, 256), MXU_DTYPE)],
        compiler_params=_cparams(("parallel",)),
    )(dqa, dka, a_cos, a_up, a_dn)


def _attn_out_gate(oa, ob, woa, wob, p, *, ts=512):
    s_ = p.shape[0]

    def body(oa_ref, ob_ref, wa_ref, wb_ref, ga_ref, gb_ref, ta_ref, tb_ref, y_ref):
        ta = jnp.dot(oa_ref[...], wa_ref[...], preferred_element_type=F32)
        tb = jnp.dot(ob_ref[...], wb_ref[...], preferred_element_type=F32)
        ta_ref[...] = ta
        tb_ref[...] = tb
        y_ref[...] = (_sigmoid(ga_ref[...]) * ta + _sigmoid(gb_ref[...]) * tb).astype(y_ref.dtype)

    w = _const(1024, 1024)
    return pl.pallas_call(
        body, name="attn_out_gate", grid=(s_ // ts,),
        in_specs=[_rows(ts, 1024), _rows(ts, 1024), w, w, _rows(ts, 1024, P_GA // 1024), _rows(ts, 1024, P_GB // 1024)],
        out_specs=[_rows(ts, 1024)] * 3,
        out_shape=[jax.ShapeDtypeStruct((s_, 1024), F32)] * 2 + [jax.ShapeDtypeStruct((s_, 1024), MXU_DTYPE)],
        compiler_params=_cparams(("parallel",)),
    )(oa, ob, woa, wob, p, p)


def _d_y_gate(dx1b, wout, p, ta, tb, *, ts=512):
    s_ = p.shape[0]

    def body(dx_ref, w_ref, ga_ref, gb_ref, ta_ref, tb_ref, dta_ref, dtb_ref, dg_ref):
        d = lax.dot_general(dx_ref[...], w_ref[...], NT, preferred_element_type=F32)
        sa, sb = _sigmoid(ga_ref[...]), _sigmoid(gb_ref[...])
        dta_ref[...] = (d * sa).astype(dta_ref.dtype)
        dtb_ref[...] = (d * sb).astype(dtb_ref.dtype)
        dg_ref[:, 0:1024] = (d * ta_ref[...] * (sa * (1.0 - sa))).astype(dg_ref.dtype)
        dg_ref[:, 1024:2048] = (d * tb_ref[...] * (sb * (1.0 - sb))).astype(dg_ref.dtype)

    return pl.pallas_call(
        body, name="d_y_gate", grid=(s_ // ts,),
        in_specs=[_rows(ts, 1024), _const(1024, 1024), _rows(ts, 1024, P_GA // 1024), _rows(ts, 1024, P_GB // 1024),
                  _rows(ts, 1024), _rows(ts, 1024)],
        out_specs=[_rows(ts, 1024), _rows(ts, 1024), _rows(ts, 2048)],
        out_shape=[jax.ShapeDtypeStruct((s_, 1024), MXU_DTYPE)] * 2 + [jax.ShapeDtypeStruct((s_, 2048), MXU_DTYPE)],
        compiler_params=_cparams(("parallel",)),
    )(dx1b, wout, p, p, ta, tb)


FF_TILE = D_FF // 2


def _ffn_in_act(h2, wgu_t, *, tm=512):
    s_ = h2.shape[0]

    def body(h_ref, w_ref, gu_ref, a_ref):
        p = lax.dot_general(h_ref[...], w_ref[...], NT, preferred_element_type=F32)
        gu_ref[...] = p
        g = p[:, :FF_TILE]
        a_ref[...] = (g * _sigmoid(g) * p[:, FF_TILE:]).astype(a_ref.dtype)

    return pl.pallas_call(
        body, name="ffn_in", grid=(s_ // tm, 2),
        in_specs=[pl.BlockSpec((tm, D_MODEL), lambda i, j: (i, 0)), pl.BlockSpec((2 * FF_TILE, D_MODEL), lambda i, j: (j, 0))],
        out_specs=[pl.BlockSpec((tm, 2 * FF_TILE), lambda i, j: (i, j)), pl.BlockSpec((tm, FF_TILE), lambda i, j: (i, j))],
        out_shape=[jax.ShapeDtypeStruct((s_, 2 * D_FF), F32), jax.ShapeDtypeStruct((s_, D_FF), MXU_DTYPE)],
        compiler_params=_cparams(("parallel", "parallel")),
    )(h2, wgu_t)


def _d_act_swiglu(dx2b, wd, gu, *, tm=512):
    s_ = dx2b.shape[0]

    def body(d_ref, w_ref, gu_ref, o_ref):
        da = lax.dot_general(d_ref[...], w_ref[...], NT, preferred_element_type=F32)
        g, u = gu_ref[:, :FF_TILE], gu_ref[:, FF_TILE:]
        sg = _sigmoid(g)
        o_ref[:, :FF_TILE] = (da * u * (sg * (1.0 + g * (1.0 - sg)))).astype(o_ref.dtype)
        o_ref[:, FF_TILE:] = (da * (g * sg)).astype(o_ref.dtype)

    gu_spec = pl.BlockSpec((tm, 2 * FF_TILE), lambda i, j: (i, j))
    return pl.pallas_call(
        body, name="d_act", grid=(s_ // tm, 2),
        in_specs=[pl.BlockSpec((tm, D_MODEL), lambda i, j: (i, 0)), pl.BlockSpec((FF_TILE, D_MODEL), lambda i, j: (j, 0)), gu_spec],
        out_specs=gu_spec, out_shape=jax.ShapeDtypeStruct((s_, 2 * D_FF), MXU_DTYPE),
        compiler_params=_cparams(("parallel", "parallel")),
    )(dx2b, wd, gu)


def _loss_bwd(x2, g, tgt, *, ts=256):
    s_, c = x2.shape

    def body(x_ref, g_ref, t_ref, dx_ref, dxb_ref, dg_ref, lp_ref, tot_ref):
        v = x_ref[...]
        r = lax.rsqrt(jnp.mean(v * v, axis=-1, keepdims=True) + EPS)
        xh = v * r
        gg = g_ref[...]
        e = xh * gg - t_ref[...]
        do = e * (1.0 / c)
        dxh = do * gg
        dx = r * (dxh - xh * jnp.mean(dxh * xh, axis=-1, keepdims=True))
        dx_ref[...] = dx
        dxb_ref[...] = dx.astype(dxb_ref.dtype)
        i = pl.program_id(0)

        @pl.when(i == 0)
        def _():
            dg_ref[...] = jnp.zeros(dg_ref.shape, F32)
            lp_ref[...] = jnp.zeros(lp_ref.shape, F32)

        dg_ref[...] += _sublane_sum(do * xh)
        lp_ref[...] += _sublane_sum(e * e)
        tot_ref[...] = jnp.full(tot_ref.shape, (0.5 / c) * jnp.sum(lp_ref[...]), F32)

    return pl.pallas_call(
        body, name="loss_bwd", grid=(s_ // ts,), in_specs=[_rows(ts, c), _const(1, c), _rows(ts, c)],
        out_specs=[_rows(ts, c), _rows(ts, c), _const(SUBLANES, c), _const(SUBLANES, c), _const(SUBLANES, LANES)],
        out_shape=[jax.ShapeDtypeStruct((s_, c), F32), jax.ShapeDtypeStruct((s_, c), MXU_DTYPE),
                   jax.ShapeDtypeStruct((SUBLANES, c), F32), jax.ShapeDtypeStruct((SUBLANES, c), F32),
                   jax.ShapeDtypeStruct((SUBLANES, LANES), F32)],
        compiler_params=_cparams(("arbitrary",)),
    )(x2, g, tgt)


def _mla_bwd_prep(dob, o32, *, ts=256):
    s_ = dob.shape[0]

    def body(do_ref, o_ref, dob_ref, dl_ref):
        d = do_ref[...]
        dob_ref[...] = d.astype(dob_ref.dtype)
        prod = d * o_ref[...]
        for h in range(MLA_HEADS):
            dl_ref[h] = jnp.sum(prod[:, h * LANES:(h + 1) * LANES].T, axis=0, keepdims=True)

    return pl.pallas_call(
        body, name="mla_bwd_prep", grid=(s_ // ts,), in_specs=[_rows(ts, 1024), _rows(ts, 1024)],
        out_specs=[_rows(ts, 1024), pl.BlockSpec((MLA_HEADS, 1, ts), lambda i: (0, 0, i))],
        out_shape=[jax.ShapeDtypeStruct((s_, 1024), MXU_DTYPE), jax.ShapeDtypeStruct((MLA_HEADS, 1, s_), F32)],
        compiler_params=_cparams(("parallel",)),
    )(dob, o32)


SWA_T = 4 * BLOCK


def _swa_masks(sb):
    kr = lax.broadcasted_iota(jnp.int32, (2 * BLOCK, BLOCK), 0)
    qc = lax.broadcasted_iota(jnp.int32, (2 * BLOCK, BLOCK), 1)
    band = jnp.logical_and(kr > qc, kr <= qc + BLOCK)
    first = jnp.logical_and(band, kr >= BLOCK)
    return band, jnp.logical_or(first, jnp.logical_and(band, sb > 0))


def _swa_in_specs(rev, nsb):
    sbi = (lambda j: nsb - 1 - j) if rev else (lambda j: j)
    cur = pl.BlockSpec((SWA_T, LANES), lambda g, j: (sbi(j), g))
    prev = pl.BlockSpec((BLOCK, LANES), lambda g, j: (jnp.maximum(4 * sbi(j) - 1, 0), g))
    q = pl.BlockSpec((SWA_T, SWA_GROUP * LANES), lambda g, j: (sbi(j), g))
    sink = pl.BlockSpec((1, SUBLANES, LANES), lambda g, j: (g, 0, 0))
    lse = pl.BlockSpec((SWA_GROUP, 1, SWA_T), lambda g, j: (g, 0, sbi(j)))
    return q, cur, prev, sink, lse


def _swa_fwd(qa, ka, va, sink_b):
    s_ = qa.shape[0]
    nsb = s_ // SWA_T
    c2 = HEAD_DIM ** -0.5 * LOG2E

    def body(q_ref, kc_ref, kp_ref, vc_ref, vp_ref, sk_ref, o32_ref, o16_ref, lse_ref, kx, vx):
        kx[0:BLOCK, :] = kp_ref[...]
        kx[BLOCK:5 * BLOCK, :] = kc_ref[...]
        vx[0:BLOCK, :] = vp_ref[...]
        vx[BLOCK:5 * BLOCK, :] = vc_ref[...]
        band, band0 = _swa_masks(pl.program_id(1))
        for hh in range(SWA_GROUP):
            sink2 = sk_ref[0, hh:hh + 1, 0:1] * LOG2E
            cs = slice(hh * LANES, (hh + 1) * LANES)
            for b in range(4):
                rs = slice(b * BLOCK, (b + 1) * BLOCK)
                ks = slice(b * BLOCK, (b + 2) * BLOCK)
                st = lax.dot_general(kx[ks, :], q_ref[rs, cs], NT, preferred_element_type=F32) * c2
                st = jnp.where(band0 if b == 0 else band, st, -jnp.inf)
                m = jnp.maximum(jnp.max(st, axis=0, keepdims=True), sink2)
                pt = jnp.exp2(st - m)
                den = jnp.sum(pt, axis=0, keepdims=True) + jnp.exp2(sink2 - m)
                o = lax.dot_general((pt * (1.0 / den)).astype(MXU_DTYPE), vx[ks, :], TN, preferred_element_type=F32)
                o32_ref[rs, cs] = o
                o16_ref[rs, cs] = o.astype(o16_ref.dtype)
                lse_ref[hh, :, rs] = m + jnp.log2(den)

    q, cur, prev, sink, lse_spec = _swa_in_specs(False, nsb)
    return pl.pallas_call(
        body, name="swa_fwd", grid=(SWA_KV_HEADS, nsb), in_specs=[q, cur, prev, cur, prev, sink],
        out_specs=[q, q, lse_spec],
        out_shape=[jax.ShapeDtypeStruct((s_, SWA_HEADS * LANES), F32), jax.ShapeDtypeStruct((s_, SWA_HEADS * LANES), MXU_DTYPE),
                   jax.ShapeDtypeStruct((SWA_HEADS, 1, s_), F32)],
        scratch_shapes=[pltpu.VMEM((5 * BLOCK, LANES), MXU_DTYPE), pltpu.VMEM((5 * BLOCK, LANES), MXU_DTYPE)],
        compiler_params=_cparams(("parallel", "arbitrary")),
    )(qa, ka, ka, va, va, sink_b)


def _swa_bwd(qa, ka, va, sink_b, o32, do, lse):
    s_ = qa.shape[0]
    nsb = s_ // SWA_T
    scale = HEAD_DIM ** -0.5
    c2 = scale * LOG2E

    def body(q_ref, kc_ref, kp_ref, vc_ref, vp_ref, sk_ref, o_ref, do_ref, lse_ref,
             dq_ref, dk_ref, dv_ref, dsk_ref, kx, vx, kacc, vacc, kcar, vcar):
        j = pl.program_id(1)
        kx[0:BLOCK, :] = kp_ref[...]
        kx[BLOCK:5 * BLOCK, :] = kc_ref[...]
        vx[0:BLOCK, :] = vp_ref[...]
        vx[BLOCK:5 * BLOCK, :] = vc_ref[...]
        band, band0 = _swa_masks(nsb - 1 - j)
        kacc[...] = jnp.zeros(kacc.shape, F32)
        vacc[...] = jnp.zeros(vacc.shape, F32)

        @pl.when(j == 0)
        def _():
            kcar[...] = jnp.zeros(kcar.shape, F32)
            vcar[...] = jnp.zeros(vcar.shape, F32)
            dsk_ref[...] = jnp.zeros(dsk_ref.shape, F32)

        for hh in range(SWA_GROUP):
            sink2 = sk_ref[0, hh:hh + 1, 0:1] * LOG2E
            cs = slice(hh * LANES, (hh + 1) * LANES)
            dsink = jnp.zeros((1, 1), F32)
            for b in range(4):
                rs = slice(b * BLOCK, (b + 1) * BLOCK)
                ks = slice(b * BLOCK, (b + 2) * BLOCK)
                q, k2, v2 = q_ref[rs, cs], kx[ks, :], vx[ks, :]
                d = do_ref[rs, cs]
                delta = jnp.sum((d * o_ref[rs, cs]).T, axis=0, keepdims=True)
                l2 = lse_ref[hh, :, rs]
                st = lax.dot_general(k2, q, NT, preferred_element_type=F32) * c2
                pt = jnp.exp2(jnp.where(band0 if b == 0 else band, st, -jnp.inf) - l2)
                db = d.astype(MXU_DTYPE)
                dst = (pt * (lax.dot_general(v2, db, NT, preferred_element_type=F32) - delta) * scale).astype(MXU_DTYPE)
                dq_ref[rs, cs] = lax.dot_general(dst, k2, TN, preferred_element_type=F32)
                kacc[ks, :] += jnp.dot(dst, q, preferred_element_type=F32)
                vacc[ks, :] += jnp.dot(pt.astype(MXU_DTYPE), db, preferred_element_type=F32)
                dsink = dsink - jnp.sum(jnp.exp2(sink2 - l2) * delta, axis=1, keepdims=True)
            dsk_ref[0, hh:hh + 1, :] += jnp.broadcast_to(dsink, (1, LANES))

        dk_ref[0:3 * BLOCK, :] = kacc[BLOCK:4 * BLOCK, :]
        dk_ref[3 * BLOCK:4 * BLOCK, :] = kacc[4 * BLOCK:5 * BLOCK, :] + kcar[...]
        dv_ref[0:3 * BLOCK, :] = vacc[BLOCK:4 * BLOCK, :].astype(dv_ref.dtype)
        dv_ref[3 * BLOCK:4 * BLOCK, :] = (vacc[4 * BLOCK:5 * BLOCK, :] + vcar[...]).astype(dv_ref.dtype)
        kcar[...] = kacc[0:BLOCK, :]
        vcar[...] = vacc[0:BLOCK, :]

    q, cur, prev, sink, lse_spec = _swa_in_specs(True, nsb)
    return pl.pallas_call(
        body, name="swa_bwd", grid=(SWA_KV_HEADS, nsb),
        in_specs=[q, cur, prev, cur, prev, sink, q, q, lse_spec],
        out_specs=[q, cur, cur, sink],
        out_shape=[jax.ShapeDtypeStruct((s_, SWA_HEADS * LANES), F32), jax.ShapeDtypeStruct((s_, SWA_KV_HEADS * LANES), F32),
                   jax.ShapeDtypeStruct((s_, SWA_KV_HEADS * LANES), MXU_DTYPE),
                   jax.ShapeDtypeStruct((SWA_KV_HEADS, SUBLANES, LANES), F32)],
        scratch_shapes=[pltpu.VMEM((5 * BLOCK, LANES), MXU_DTYPE), pltpu.VMEM((5 * BLOCK, LANES), MXU_DTYPE),
                        pltpu.VMEM((5 * BLOCK, LANES), F32), pltpu.VMEM((5 * BLOCK, LANES), F32),
                        pltpu.VMEM((BLOCK, LANES), F32), pltpu.VMEM((BLOCK, LANES), F32)],
        compiler_params=_cparams(("arbitrary", "arbitrary")),
    )(qa, ka, ka, va, va, sink_b, o32, do, lse)


MLA_T = 512
MLA_GROUP = 2


def _mla_specs(s_, t):
    w = MLA_GROUP * LANES
    qs = pl.BlockSpec((t, w), lambda g, i: (i, g))
    kv = pl.BlockSpec((s_, w), lambda g, i: (0, g))
    row = pl.BlockSpec((MLA_GROUP, 1, t), lambda g, i: (g, 0, i))
    return qs, kv, row


def _causal_scores_t(k, q, t, c2, masked):
    st = lax.dot_general(k, q, NT, preferred_element_type=F32) * c2
    if masked:
        kr = lax.broadcasted_iota(jnp.int32, (t, t), 0)
        qc = lax.broadcasted_iota(jnp.int32, (t, t), 1)
        st = jnp.where(kr <= qc, st, -jnp.inf)
    return st


def _mla_fwd(qc, kc, vp):
    s_ = qc.shape[0]
    t = min(MLA_T, s_)
    c2 = MLA_QK ** -0.5 * LOG2E

    def body(q_ref, k_ref, v_ref, o32_ref, o16_ref, lse_ref, m_s, acc_s):
        qi = pl.program_id(1)
        m_s[...] = jnp.full(m_s.shape, -jnp.inf, F32)
        acc_s[...] = jnp.zeros(acc_s.shape, F32)
        ones_lane = lax.broadcasted_iota(jnp.int32, (t, LANES), 1) == MLA_V

        def step(ki, masked):
            off = pl.multiple_of(ki * t, t)
            for g in range(MLA_GROUP):
                cs = slice(g * LANES, (g + 1) * LANES)
                st = _causal_scores_t(k_ref[pl.ds(off, t), cs], q_ref[:, cs], t, c2, masked)
                m_old = m_s[g]
                m_new = jnp.maximum(m_old, jnp.max(st, axis=0, keepdims=True))
                alpha = jnp.exp2(m_old - m_new)
                pt = jnp.exp2(st - m_new).astype(MXU_DTYPE)
                v = v_ref[pl.ds(off, t), cs]
                v = jnp.where(ones_lane, jnp.ones((), v.dtype), v)
                acc_s[g] = alpha * acc_s[g] + lax.dot_general(v, pt, TN, preferred_element_type=F32)
                m_s[g] = m_new

        def full_block(ki, carry):
            step(ki, False)
            return carry

        lax.fori_loop(0, qi, full_block, 0)
        step(qi, True)
        for g in range(MLA_GROUP):
            cs = slice(g * LANES, (g + 1) * LANES)
            acc = acc_s[g]
            l = acc[MLA_V:MLA_V + 1, :]
            o = (acc * (1.0 / l)).T
            o32_ref[:, cs] = o
            o16_ref[:, cs] = o.astype(o16_ref.dtype)
            lse_ref[g] = m_s[g] + jnp.log2(l)

    qs, kv, row = _mla_specs(s_, t)
    return pl.pallas_call(
        body, name="mla_fwd", grid=(MLA_HEADS // MLA_GROUP, s_ // t), in_specs=[qs, kv, kv], out_specs=[qs, qs, row],
        out_shape=[jax.ShapeDtypeStruct((s_, MLA_HEADS * LANES), F32), jax.ShapeDtypeStruct((s_, MLA_HEADS * LANES), MXU_DTYPE),
                   jax.ShapeDtypeStruct((MLA_HEADS, 1, s_), F32)],
        scratch_shapes=[pltpu.VMEM((MLA_GROUP, 1, t), F32), pltpu.VMEM((MLA_GROUP, LANES, t), F32)],
        compiler_params=_cparams(("parallel", "arbitrary")),
    )(qc, kc, vp)


def _mla_bwd(qc, kc, vp, dob, lse, delta):
    s_ = qc.shape[0]
    t = min(MLA_T, s_)
    scale = MLA_QK ** -0.5
    c2 = scale * LOG2E

    def body(q_ref, do_ref, lse_ref, dl_ref, k_ref, v_ref, dq_ref, dk_ref, dv_ref, dqt_s):
        qi = pl.program_id(1)

        @pl.when(qi == 0)
        def _():
            dk_ref[...] = jnp.zeros(dk_ref.shape, F32)
            dv_ref[...] = jnp.zeros(dv_ref.shape, F32)

        dqt_s[...] = jnp.zeros(dqt_s.shape, F32)

        def step(ki, masked):
            off = pl.multiple_of(ki * t, t)
            for g in range(MLA_GROUP):
                cs = slice(g * LANES, (g + 1) * LANES)
                q, d, k = q_ref[:, cs], do_ref[:, cs], k_ref[pl.ds(off, t), cs]
                pt = jnp.exp2(_causal_scores_t(k, q, t, c2, masked) - lse_ref[g])
                dpt = lax.dot_general(v_ref[pl.ds(off, t), cs], d, NT, preferred_element_type=F32)
                dst = (pt * (dpt - dl_ref[g]) * scale).astype(MXU_DTYPE)
                dv_ref[pl.ds(off, t), cs] += jnp.dot(pt.astype(MXU_DTYPE), d, preferred_element_type=F32)
                dk_ref[pl.ds(off, t), cs] += jnp.dot(dst, q, preferred_element_type=F32)
                dqt_s[g] += lax.dot_general(k, dst, TN, preferred_element_type=F32)

        def full_block(ki, carry):
            step(ki, False)
            return carry

        lax.fori_loop(0, qi, full_block, 0)
        step(qi, True)
        for g in range(MLA_GROUP):
            dq_ref[:, g * LANES:(g + 1) * LANES] = dqt_s[g].T

    qs, kv, row = _mla_specs(s_, t)
    shp = jax.ShapeDtypeStruct((s_, MLA_HEADS * LANES), F32)
    return pl.pallas_call(
        body, name="mla_bwd", grid=(MLA_HEADS // MLA_GROUP, s_ // t), in_specs=[qs, qs, row, row, kv, kv],
        out_specs=[qs, kv, kv], out_shape=[shp, shp, shp], scratch_shapes=[pltpu.VMEM((MLA_GROUP, LANES, t), F32)],
        compiler_params=_cparams(("parallel", "arbitrary")),
    )(qc, dob, lse, delta, kc, vp)


def _pad_heads(w, nh, hd, axis):
    shp = w.shape
    w = w.reshape(shp[:axis] + (nh, hd) + shp[axis + 1:])
    pad = [(0, 0)] * w.ndim
    pad[axis + 1] = (0, LANES - hd)
    w = jnp.pad(w, pad)
    return w.reshape(shp[:axis] + (nh * LANES,) + shp[axis + 1:])


def _unpad_heads(w, nh, hd, axis):
    shp = w.shape
    w = w.reshape(shp[:axis] + (nh, LANES) + shp[axis + 1:])
    w = lax.slice_in_dim(w, 0, hd, axis=axis + 1)
    return w.reshape(shp[:axis] + (nh * hd,) + shp[axis + 1:])


PACK_W = 1024
ROW_TILE = 16
FULL_SHAPE = dict(w_in=(1024, 3488), w_uq=(384, 768), w_ukv=(256, 1024), w_o_swa=(512, 1024), w_o_mla=(512, 1024),
                  w_out=(1024, 1024), w_gate=(1024, 2816), w_up=(1024, 2816), w_down=(2816, 1024))
BIG = tuple(FULL_SHAPE)
ROW_SHARDED = ("w_out", "w_down")
W_IN_COLS = FULL_SHAPE["w_in"][1] // N_DEV
W_IN_ROWS = -(-W_IN_COLS // ROW_TILE) * ROW_TILE
FF_COLS = D_FF // N_DEV
OUT_ROWS = D_MODEL // N_DEV
SMALL_ROW0 = W_IN_ROWS + OUT_ROWS
SMALL_FLAT = (("w_uq", 0, 36), ("w_ukv", 48, 32), ("w_o_swa", 80, 64), ("w_o_mla", 144, 64))
SMALL_ROWS = 208
EARLY_ROWS = SMALL_ROW0 + SMALL_ROWS
LATE_ROWS = 3 * FF_COLS
PACK_ROWS = EARLY_ROWS + LATE_ROWS


def _shard_shape(n):
    r, c = FULL_SHAPE[n]
    return (r // N_DEV, c) if n in ROW_SHARDED else (r, c // N_DEV)


def _wire_pack(sh, dtype):
    c = lambda n: sh[n].astype(dtype)
    rows = [jnp.pad(c("w_in").T, ((0, W_IN_ROWS - W_IN_COLS), (0, 0))), c("w_out")]
    for n, _, r in SMALL_FLAT:
        rows.append(jnp.pad(c(n).reshape(r, PACK_W), ((0, -r % ROW_TILE), (0, 0))))
    return jnp.concatenate(rows + [c("w_gate").T, c("w_up").T, c("w_down")], 0)


MID_ROWS = OUT_ROWS + SMALL_ROWS


def _mid_unpack(p):
    out = dict(w_out=p[0:OUT_ROWS])
    for n, off, r in SMALL_FLAT:
        out[n] = p[OUT_ROWS + off:OUT_ROWS + off + r].reshape(_shard_shape(n))
    return out


def _w_in_row_maps():
    sp = lambda col: (col // W_IN_COLS) * W_IN_ROWS + col % W_IN_COLS
    fwd = np.full((P_W,), -1, np.int64)

    def put(t0, c0, n):
        fwd[t0:t0 + n] = [sp(c) for c in range(c0, c0 + n)]

    put(P_GA, IN_OFF[6], D_MODEL)
    put(P_GB, IN_OFF[7], D_MODEL)
    for h in range(SWA_HEADS):
        put(P_Q + LANES * h, IN_OFF[0] + HEAD_DIM * h, HEAD_DIM)
    put(P_QLAT, IN_OFF[3], Q_LORA)
    put(P_KR + KR_LANE, IN_OFF[5], MLA_ROPE)
    for h in range(SWA_KV_HEADS):
        put(P_K + LANES * h, IN_OFF[1] + HEAD_DIM * h, HEAD_DIM)
        put(P_V + LANES * h, IN_OFF[2] + HEAD_DIM * h, HEAD_DIM)
    put(P_KVLAT, IN_OFF[4], KV_LORA)
    inv = np.full((N_DEV * W_IN_ROWS,), -1, np.int64)
    inv[fwd[fwd >= 0]] = np.nonzero(fwd >= 0)[0]
    return fwd, inv


def _take_rows(src, idx, *, name):
    n_out, n_src, width = len(idx), src.shape[0], src.shape[1]
    assert n_out % BLOCK == 0 and n_src % BLOCK == 0
    n_tiles = n_out // BLOCK
    blocks = [sorted({int(v) // BLOCK for v in idx[i * BLOCK:(i + 1) * BLOCK] if v >= 0}) for i in range(n_tiles)]
    k_max = max(1, max(len(b) for b in blocks))
    tab = np.zeros((n_tiles, k_max), np.int32)
    sel = np.zeros((n_tiles, k_max, BLOCK, BLOCK), np.float32)
    for i, blks in enumerate(blocks):
        for m, b in enumerate(blks):
            tab[i, m] = b
            for r in range(BLOCK):
                v = int(idx[i * BLOCK + r])
                if v >= 0 and v // BLOCK == b:
                    sel[i, m, r, v % BLOCK] = 1.0

    def body(tab_ref, sel_ref, *refs):
        o_ref = refs[k_max]
        acc = jnp.dot(sel_ref[0, 0], refs[0][...], preferred_element_type=F32)
        for m in range(1, k_max):
            acc = acc + jnp.dot(sel_ref[0, m], refs[m][...], preferred_element_type=F32)
        o_ref[...] = acc.astype(o_ref.dtype)

    def src_spec(m):
        return pl.BlockSpec((BLOCK, width), lambda i, t: (t[i * k_max + m], 0))

    return pl.pallas_call(
        body, name=name,
        grid_spec=pltpu.PrefetchScalarGridSpec(
            num_scalar_prefetch=1, grid=(n_tiles,),
            in_specs=[pl.BlockSpec((1, k_max, BLOCK, BLOCK), lambda i, t: (i, 0, 0, 0))] + [src_spec(m) for m in range(k_max)],
            out_specs=pl.BlockSpec((BLOCK, width), lambda i, t: (i, 0))),
        out_shape=jax.ShapeDtypeStruct((n_out, width), src.dtype),
        compiler_params=_cparams(("parallel",)),
    )(jnp.asarray(tab.reshape(-1)), jnp.asarray(sel, src.dtype), *([src] * k_max))


def _w_in_operand(win_g):
    return _take_rows(win_g.reshape(N_DEV * W_IN_ROWS, PACK_W), _w_in_row_maps()[0], name="w_in_rows")


def _mid_operands(wout_g, small_g):
    def full(n, off, r):
        a = small_g[:, off:off + r].reshape((N_DEV,) + _shard_shape(n))
        return jnp.moveaxis(a, 0, 1).reshape(FULL_SHAPE[n])

    w = {n: full(n, off, r) for n, off, r in SMALL_FLAT}
    ukv = w["w_ukv"].reshape(KV_LORA, MLA_HEADS, MLA_NOPE + MLA_V)
    return dict(
        wout=wout_g.reshape(D_MODEL, D_MODEL),
        wuq=_pad_heads(w["w_uq"], MLA_HEADS, MLA_QK, 1),
        wuk=_pad_heads(ukv[:, :, :MLA_NOPE].reshape(KV_LORA, -1), MLA_HEADS, MLA_NOPE, 1),
        wuv=_pad_heads(ukv[:, :, MLA_NOPE:].reshape(KV_LORA, -1), MLA_HEADS, MLA_V, 1),
        woa=_pad_heads(w["w_o_swa"], SWA_HEADS, HEAD_DIM, 0),
        wob=_pad_heads(w["w_o_mla"], MLA_HEADS, MLA_V, 0),
    )


def _mid_grad_pack(g):
    uk = _unpad_heads(g["wukv"][:, :1024], MLA_HEADS, MLA_NOPE, 1).reshape(KV_LORA, MLA_HEADS, MLA_NOPE)
    uv = _unpad_heads(g["wukv"][:, 1024:], MLA_HEADS, MLA_V, 1).reshape(KV_LORA, MLA_HEADS, MLA_V)
    w = dict(w_uq=_unpad_heads(g["wuq"], MLA_HEADS, MLA_QK, 1), w_ukv=jnp.concatenate([uk, uv], 2).reshape(KV_LORA, -1),
             w_o_swa=_unpad_heads(g["woa"], SWA_HEADS, HEAD_DIM, 0), w_o_mla=_unpad_heads(g["wob"], MLA_HEADS, MLA_V, 0))

    def flat(n, r):
        rr, cc = FULL_SHAPE[n]
        a = jnp.moveaxis(w[n].reshape(rr, N_DEV, cc // N_DEV), 1, 0).reshape(N_DEV, r, PACK_W)
        return jnp.pad(a, ((0, 0), (0, -r % ROW_TILE), (0, 0))).astype(WIRE_DTYPE)

    return jnp.concatenate([g["wout"].reshape(N_DEV, OUT_ROWS, PACK_W)] + [flat(n, r) for n, _, r in SMALL_FLAT], 1)


def _w_in_grad_chunks(g_win_t):
    return _take_rows(g_win_t, _w_in_row_maps()[1], name="dw_in_rows").reshape(N_DEV, W_IN_ROWS, PACK_W)


def _local_step(x, tgt, win_t, small, weights, grads):
    s_ = x.shape[0]
    tabs = _rope_tables(s_)
    sink_b = jnp.broadcast_to(small["swa_sinks"].reshape(SWA_KV_HEADS, SWA_GROUP, 1), (SWA_KV_HEADS, SWA_GROUP, LANES))
    sink_b = jnp.pad(sink_b, ((0, 0), (0, SUBLANES - SWA_GROUP), (0, 0)))

    h = _norm_fwd(x, small["mix_norm_g"], name="norm1")
    p = _mm(h, win_t, "nt", name="proj_in", tm=1024, tn=2176)
    qa, ka, va, cq, ckv, kro = _attn_prep(p, small["q_norm_g"], small["kv_norm_g"], tabs)
    ops = weights.mid(cq)
    oa32, oa16, lse_a = _swa_fwd(qa, ka, va, sink_b)
    qp = _mm(cq, ops["wuq"], "nn", name="mla_q_up", tm=1024, tn=1024)
    kp = _mm(ckv, ops["wuk"], "nn", name="mla_k_up", tm=1024, tn=1024)
    vp = _mm(ckv, ops["wuv"], "nn", name="mla_v_up", tm=1024, tn=1024, out_dtype=MXU_DTYPE)
    qc, kc = _mla_prep(qp, kp, kro, tabs)
    ob32, ob16, lse_b = _mla_fwd(qc, kc, vp)
    ta, tb, y = _attn_out_gate(oa16, ob16, ops["woa"], ops["wob"], p)
    x1 = _mm(y, ops["wout"], "nn", name="out_proj", add=x, tm=1024, tn=1024)
    wgu_t, wd = weights.late(x1)
    h2 = _norm_fwd(x1, small["ffn_norm_g"], name="norm2")
    gu, act = _ffn_in_act(h2, wgu_t)
    x2 = _mm(act, wd, "nn", name="ffn_out", add=x1, tn=1024)

    dx2, dx2b, dg3, _, tot = _loss_bwd(x2, small["final_norm_g"].reshape(1, D_MODEL), tgt)
    g = {}
    g_wd = _mm(act, dx2b, "tn", name="dw_down", tm=1408, tn=1024, tk=1024, out_dtype=WIRE_DTYPE)
    dgu = _d_act_swiglu(dx2b, wd, gu)
    dh2 = _mm(dgu, wgu_t, "nn", name="d_h2", tn=1024, tk=2816)
    g_wgu = _mm(dgu, h2, "tn", name="dw_ffn_in", tm=1408, tn=1024, tk=1024, out_dtype=WIRE_DTYPE)
    token = grads.late(g_wgu, g_wd)
    dx1, dx1b, dg2 = _norm_bwd(x1, small["ffn_norm_g"] + token[0:1, 0:1], dh2, dx2, name="norm2_bwd")
    g["wout"] = _mm(y, dx1b, "tn", name="dw_out", tm=1024, tn=1024, tk=1024, out_dtype=WIRE_DTYPE)
    dta, dtb, dgab = _d_y_gate(dx1b, ops["wout"], p, ta, tb)
    doa = _mm(dta, ops["woa"], "nt", name="d_oa", tm=1024, tn=1024)
    g["woa"] = _mm(oa16, dta, "tn", name="dw_o_swa", tm=1024, tn=1024, tk=1024)
    dob = _mm(dtb, ops["wob"], "nt", name="d_ob", tm=1024, tn=1024)
    g["wob"] = _mm(ob16, dtb, "tn", name="dw_o_mla", tm=1024, tn=1024, tk=1024)
    dob16, delta_b = _mla_bwd_prep(dob, ob32)
    dqc, dkc, dvp = _mla_bwd(qc, kc, vp, dob16, lse_b, delta_b)
    dqp, dkv, dkr = _mla_unprep(dqc, dkc, dvp, tabs)
    dcq = _mm(dqp, ops["wuq"], "nt", name="d_cq", tn=Q_LORA)
    g["wuq"] = _mm(cq, dqp, "tn", name="dw_uq", tm=Q_LORA, tn=1024, tk=512)
    dckv = _mm(dkv, jnp.concatenate([ops["wuk"], ops["wuv"]], 1), "nt", name="d_ckv", tn=KV_LORA)
    g["wukv"] = _mm(ckv, dkv, "tn", name="dw_ukv", tm=KV_LORA, tn=1024, tk=512)
    token = grads.mid(g)
    _, dqlat, dgq = _norm_bwd(p, small["q_norm_g"] + token[0:1, 0:1], dcq, None, name="qnorm_bwd", x_cb=P_QLAT // Q_LORA)
    _, dkvlat, dgkv = _norm_bwd(p, small["kv_norm_g"], dckv, None, name="kvnorm_bwd", x_cb=P_KVLAT // KV_LORA)
    dqa, dka, dva, dsk = _swa_bwd(qa, ka, va, sink_b, oa32, doa, lse_a)
    dq_raw, dk_raw = _swa_unrope(dqa, dka, tabs)
    dp = jnp.concatenate([dgab, dq_raw, dqlat, dkr, dk_raw, dva, dkvlat], 1)
    token = grads.last(_mm(dp, h, "tn", name="dw_in", tm=2176, tn=1024, tk=1024, out_dtype=WIRE_DTYPE))
    dh = _mm(dp, win_t, "nn", name="d_h", after=token, tm=1024, tn=1024, tk=2176)
    gx, _, dg1 = _norm_bwd(x, small["mix_norm_g"], dh, dx1, name="norm1_bwd")

    sm = dict(mix_norm_g=dg1, ffn_norm_g=dg2, final_norm_g=dg3, q_norm_g=dgq, kv_norm_g=dgkv,
              swa_sinks=dsk[:, :SWA_GROUP, 0].reshape(1, SWA_HEADS))
    return tot, gx, sm


MESH = pl.DeviceIdType.MESH
ANY = pl.BlockSpec(memory_space=pl.ANY)


def _position():
    return lax.axis_index("x"), lax.axis_index("y"), lax.axis_index("c")


def _all_gather(block, pieces, shapes, *, name):
    n_out = len(shapes)
    n_rows = sum(p[3] for p in pieces)

    def body(x_ref, *refs):
        outs, (send_sems, recv_sems, local_sem) = refs[:n_out], refs[n_out:]
        x, y, c = _position()
        me, sibling = (x, y, c), (x, y, 1 - c)
        chips = [(1 - x, y), (x, 1 - y), (1 - x, 1 - y)]

        def dst(piece, blk):
            arr, lead, _, _ = piece
            return outs[arr].at[lead(4 * blk[0] + 2 * blk[1] + blk[2])]

        def own(piece):
            return x_ref.at[pl.ds(piece[2], piece[3])]

        def copies(k, blk, to, from_input):
            return [pltpu.make_async_remote_copy(
                src_ref=own(p) if from_input else dst(p, blk), dst_ref=dst(p, blk), send_sem=send_sems.at[k],
                recv_sem=recv_sems.at[k], device_id=to, device_id_type=MESH) for p in pieces]

        gathered_rows = x_ref.at[pl.ds(0, n_rows)]

        def whole_block(k):
            return pltpu.make_async_remote_copy(src_ref=gathered_rows, dst_ref=gathered_rows, send_sem=send_sems.at[k],
                                                recv_sem=recv_sems.at[k], device_id=me, device_id_type=MESH)

        for p in pieces:
            pltpu.make_async_copy(own(p), dst(p, me), local_sem).start()
        for cp in copies(0, me, sibling, True):
            cp.start()
        for j, chip in enumerate(chips):
            for cp in copies(1 + j, me, (*chip, c), True):
                cp.start()
        for j, chip in enumerate(chips):
            whole_block(1 + j).wait_recv()
            for cp in copies(4 + j, (*chip, c), sibling, False):
                cp.start()
        whole_block(0).wait_recv()
        for j in range(3):
            whole_block(4 + j).wait_recv()
        for k in range(7):
            whole_block(k).wait_send()
        pltpu.make_async_copy(gathered_rows, gathered_rows, local_sem).wait()

    return pl.pallas_call(
        body, name=name, out_shape=[jax.ShapeDtypeStruct(s, block.dtype) for s in shapes], in_specs=[ANY],
        out_specs=[ANY] * n_out,
        scratch_shapes=[pltpu.SemaphoreType.DMA((7,)), pltpu.SemaphoreType.DMA((7,)), pltpu.SemaphoreType.DMA],
    )(block)


HBM = pl.BlockSpec(memory_space=pltpu.HBM)
SEM = pl.BlockSpec(memory_space=pltpu.SEMAPHORE)
TILE_DEVS = FF_TILE // FF_COLS
GU_SHAPE = (2, 2, TILE_DEVS, FF_COLS, PACK_W)


def _gate_slab(d):
    return (d // TILE_DEVS, 0, d % TILE_DEVS)


def _up_slab(d):
    return (d // TILE_DEVS, 1, d % TILE_DEVS)
D_SHAPE = (N_DEV, FF_COLS, PACK_W)
LAND_SHAPE = (N_DEV, LATE_ROWS, PACK_W)


def _split_params():
    return pltpu.CompilerParams(has_side_effects=pltpu.SideEffectType.DATAFLOW_SIDE_EFFECTING)


def _peer(x, y, c, k):
    return ((1 - x) if k & 4 else x, (1 - y) if k & 2 else y, (1 - c) if k & 1 else c)


def _empty_hbm(shape, dtype):
    return pltpu.with_memory_space_constraint(lax.empty(shape, dtype), pltpu.HBM)


def _wait_all(rows, send_sems, recv_sems, me):
    for k in range(N_DEV - 1):
        cp = pltpu.make_async_remote_copy(src_ref=rows, dst_ref=rows, send_sem=send_sems.at[k], recv_sem=recv_sems.at[k],
                                          device_id=me, device_id_type=MESH)
        cp.wait_send()
        cp.wait_recv()


def _token_shape():
    return jax.ShapeDtypeStruct((SUBLANES, LANES), F32)


def _gather_start(pack, row0, pieces, shapes, *, name):
    n = len(shapes)

    def body(*refs):
        p_ref, bufs, send_sems, recv_sems, token = refs[0], refs[1:1 + n], refs[1 + n], refs[2 + n], refs[-1]
        x, y, c = _position()
        me = 4 * x + 2 * y + c
        for k in range(1, N_DEV):
            off = row0
            for buf, lead, rows in pieces:
                pltpu.make_async_remote_copy(
                    src_ref=p_ref.at[pl.ds(off, rows)], dst_ref=bufs[buf].at[lead(me)], send_sem=send_sems.at[k - 1],
                    recv_sem=recv_sems.at[k - 1], device_id=_peer(x, y, c, k), device_id_type=MESH).start()
                off += rows
        token[...] = jnp.zeros_like(token)

    sems, dt = pltpu.SemaphoreType.DMA((N_DEV - 1,)), pack.dtype
    return pl.pallas_call(
        body, name=name,
        out_shape=(sems, sems, pltpu.HBM(pack.shape, dt)) + tuple(pltpu.HBM(s, dt) for s in shapes) + (_token_shape(),),
        in_specs=(HBM,) * (1 + n), out_specs=(SEM, SEM) + (HBM,) * (1 + n) + (pl.BlockSpec(memory_space=pltpu.VMEM),),
        input_output_aliases={i: 2 + i for i in range(1 + n)}, compiler_params=_split_params(),
    )(pltpu.with_memory_space_constraint(pack, pltpu.HBM), *[_empty_hbm(s, dt) for s in shapes])


def _gather_wait(started, row0, n_rows, after, *, name):
    send_sems, recv_sems, pack, *bufs = started[:-1]
    n = len(bufs)

    def body(*refs):
        _wait_all(refs[0].at[pl.ds(row0, n_rows)], refs[1 + n], refs[2 + n], _position())

    outs = pl.pallas_call(
        body, name=name, out_shape=tuple(pltpu.HBM(a.shape, a.dtype) for a in (pack, *bufs)),
        in_specs=(HBM,) * (1 + n) + (SEM, SEM, ANY), out_specs=(HBM,) * (1 + n),
        input_output_aliases={i: i for i in range(1 + n)}, compiler_params=_split_params(),
    )(pack, *bufs, send_sems, recv_sems, after)
    return outs[0], outs[1:]


def _scatter_start(srcs, pieces, *, name):
    n = len(srcs)
    land_shape = (N_DEV, sum(p[2] for p in pieces), PACK_W)

    def body(*refs):
        src_refs, land_ref, send_sems, recv_sems, token = refs[:n], refs[n], refs[n + 1], refs[n + 2], refs[-1]
        x, y, c = _position()
        me = 4 * x + 2 * y + c
        for k in range(1, N_DEV):
            px, py, pc = _peer(x, y, c, k)
            off = 0
            for si, lead, rows in pieces:
                pltpu.make_async_remote_copy(
                    src_ref=src_refs[si].at[lead(4 * px + 2 * py + pc)], dst_ref=land_ref.at[me, pl.ds(off, rows)],
                    send_sem=send_sems.at[k - 1], recv_sem=recv_sems.at[k - 1], device_id=(px, py, pc),
                    device_id_type=MESH).start()
                off += rows
        token[...] = jnp.zeros_like(token)

    sems, dt = pltpu.SemaphoreType.DMA((N_DEV - 1,)), srcs[0].dtype
    return pl.pallas_call(
        body, name=name,
        out_shape=(sems, sems) + tuple(pltpu.HBM(a.shape, dt) for a in srcs) + (pltpu.HBM(land_shape, dt), _token_shape()),
        in_specs=(HBM,) * (n + 1), out_specs=(SEM, SEM) + (HBM,) * (n + 1) + (pl.BlockSpec(memory_space=pltpu.VMEM),),
        input_output_aliases={i: 2 + i for i in range(n + 1)}, compiler_params=_split_params(),
    )(*[pltpu.with_memory_space_constraint(a, pltpu.HBM) for a in srcs], _empty_hbm(land_shape, dt))


def _scatter_wait(started, after, *, name):
    send_sems, recv_sems, *bufs = started[:-1]
    n = len(bufs)

    def body(*refs):
        _wait_all(refs[n - 1].at[0], refs[n], refs[n + 1], _position())

    return pl.pallas_call(
        body, name=name, out_shape=tuple(pltpu.HBM(a.shape, a.dtype) for a in bufs),
        in_specs=(HBM,) * n + (SEM, SEM, ANY), out_specs=(HBM,) * n, input_output_aliases={i: i for i in range(n)},
        compiler_params=_split_params(),
    )(*bufs, send_sems, recv_sems, after)


def _peer_sum(own, own_lead, land, block, rows, idx, *, name):
    lead_rank = own.ndim - 2

    def body(idx_ref, own_ref, *refs):
        o_ref = refs[N_DEV - 1]
        acc = own_ref[(0,) * lead_rank].astype(F32)
        for k in range(N_DEV - 1):
            acc = acc + refs[k][0].astype(F32)
        o_ref[...] = acc

    own_spec = pl.BlockSpec((1,) * lead_rank + (rows, PACK_W), lambda i, t: own_lead(t[0]) + (0, 0))

    def land_spec(k):
        return pl.BlockSpec((1, rows, PACK_W), lambda i, t: (t[k + 1], block, 0))

    return pl.pallas_call(
        body, name=name,
        grid_spec=pltpu.PrefetchScalarGridSpec(
            num_scalar_prefetch=1, grid=(1,), in_specs=[own_spec] + [land_spec(k) for k in range(N_DEV - 1)],
            out_specs=pl.BlockSpec((rows, PACK_W), lambda i, t: (0, 0))),
        out_shape=jax.ShapeDtypeStruct((rows, PACK_W), F32), compiler_params=_cparams(("arbitrary",)),
    )(idx, own, *([land] * (N_DEV - 1)))


def _adamw(w, g, m, v):
    m = ADAM_B1 * m + (1.0 - ADAM_B1) * g
    v = ADAM_B2 * v + (1.0 - ADAM_B2) * (g * g)
    m_hat = m / (1.0 - ADAM_B1 ** ADAM_STEP)
    v_hat = v / (1.0 - ADAM_B2 ** ADAM_STEP)
    delta = -ADAM_LR * (m_hat / (jnp.sqrt(v_hat) + ADAM_EPS) + ADAM_WD * w)
    return delta, m, v


def _adamw_call(w, g, m, v, *, name, max_rows=256):
    r, c_ = w.shape
    tr = max_rows if r > max_rows and r % max_rows == 0 else r

    def body(w_ref, g_ref, m_ref, v_ref, d_ref, mo_ref, vo_ref):
        d, mn, vn = _adamw(w_ref[...], g_ref[...], m_ref[...], v_ref[...])
        d_ref[...] = d
        mo_ref[...] = mn
        vo_ref[...] = vn

    row = pl.BlockSpec((tr, c_), lambda i: (i, 0))
    shp = jax.ShapeDtypeStruct((r, c_), F32)
    return pl.pallas_call(
        body, name=name, grid=(r // tr,), in_specs=[row] * 4, out_specs=[row] * 3, out_shape=[shp] * 3,
        compiler_params=_cparams(("parallel",)),
    )(w, g, m, v)


SMALL = ("mix_norm_g", "ffn_norm_g", "final_norm_g", "q_norm_g", "kv_norm_g", "swa_sinks")
SMALL_W = dict(mix_norm_g=1024, ffn_norm_g=1024, final_norm_g=1024, q_norm_g=Q_LORA, kv_norm_g=KV_LORA, swa_sinks=SWA_HEADS)


def _small_adamw(parts, w, m, v):
    n_par = parts.shape[1] // SUBLANES

    def body(p_ref, w_ref, m_ref, v_ref, g_ref, d_ref, mo_ref, vo_ref):
        tot = p_ref[0]
        for dev in range(1, N_DEV):
            tot = tot + p_ref[dev]
        row_id = lax.broadcasted_iota(jnp.int32, (SUBLANES, PACK_W), 0)
        g = jnp.zeros((SUBLANES, PACK_W), F32)
        for k in range(n_par):
            g = jnp.where(row_id == k, jnp.sum(tot[k * SUBLANES:(k + 1) * SUBLANES, :], axis=0, keepdims=True), g)
        d, mn, vn = _adamw(w_ref[...], g, m_ref[...], v_ref[...])
        g_ref[...] = g
        d_ref[...] = d
        mo_ref[...] = mn
        vo_ref[...] = vn

    shp = jax.ShapeDtypeStruct((SUBLANES, PACK_W), F32)
    vm = pl.BlockSpec(memory_space=pltpu.VMEM)
    return pl.pallas_call(body, name="small_adamw", in_specs=[vm] * 4, out_specs=[vm] * 4, out_shape=[shp] * 4)(parts, w, m, v)


def _small_pack(d, rows_each):
    parts = [jnp.pad(d[n].astype(F32), ((0, 0), (0, PACK_W - SMALL_W[n]))) for n in SMALL]
    out = jnp.concatenate(parts, 0)
    pad = -out.shape[0] % SUBLANES
    return jnp.pad(out, ((0, pad), (0, 0)))


def kernel(x, mix_norm_g, w_in, swa_sinks, q_norm_g, w_uq, kv_norm_g, w_ukv, w_o_swa, w_o_mla, w_out, ffn_norm_g, w_gate, w_up, w_down, final_norm_g, loss_target, m_mix_norm_g, m_w_in, m_swa_sinks, m_q_norm_g, m_w_uq, m_kv_norm_g, m_w_ukv, m_w_o_swa, m_w_o_mla, m_w_out, m_ffn_norm_g, m_w_gate, m_w_up, m_w_down, m_final_norm_g, v_mix_norm_g, v_w_in, v_swa_sinks, v_q_norm_g, v_w_uq, v_kv_norm_g, v_w_ukv, v_w_o_swa, v_w_o_mla, v_w_out, v_ffn_norm_g, v_w_gate, v_w_up, v_w_down, v_final_norm_g):
    big_w = dict(w_in=w_in[0], w_uq=w_uq[0], w_ukv=w_ukv[0], w_o_swa=w_o_swa[0], w_o_mla=w_o_mla[0], w_out=w_out[0],
                 w_gate=w_gate[0], w_up=w_up[0], w_down=w_down[0])
    big_m = dict(w_in=m_w_in[0], w_uq=m_w_uq[0], w_ukv=m_w_ukv[0], w_o_swa=m_w_o_swa[0], w_o_mla=m_w_o_mla[0],
                 w_out=m_w_out[0], w_gate=m_w_gate[0], w_up=m_w_up[0], w_down=m_w_down[0])
    big_v = dict(w_in=v_w_in[0], w_uq=v_w_uq[0], w_ukv=v_w_ukv[0], w_o_swa=v_w_o_swa[0], w_o_mla=v_w_o_mla[0],
                 w_out=v_w_out[0], w_gate=v_w_gate[0], w_up=v_w_up[0], w_down=v_w_down[0])
    small_w = dict(mix_norm_g=mix_norm_g, ffn_norm_g=ffn_norm_g, final_norm_g=final_norm_g.reshape(1, D_MODEL),
                   q_norm_g=q_norm_g, kv_norm_g=kv_norm_g, swa_sinks=swa_sinks)
    small_m = dict(mix_norm_g=m_mix_norm_g, ffn_norm_g=m_ffn_norm_g, final_norm_g=m_final_norm_g.reshape(1, D_MODEL),
                   q_norm_g=m_q_norm_g, kv_norm_g=m_kv_norm_g, swa_sinks=m_swa_sinks)
    small_v = dict(mix_norm_g=v_mix_norm_g, ffn_norm_g=v_ffn_norm_g, final_norm_g=v_final_norm_g.reshape(1, D_MODEL),
                   q_norm_g=v_q_norm_g, kv_norm_g=v_kv_norm_g, swa_sinks=v_swa_sinks)

    px, py, pc = _position()
    me = 4 * px + 2 * py + pc
    idx = jnp.stack([me] + [4 * qx + 2 * qy + qc for qx, qy, qc in (_peer(px, py, pc, k) for k in range(1, N_DEV))])
    idx = idx.astype(jnp.int32)

    dev = lambda d: (d,)
    pack = _wire_pack(big_w, WIRE_DTYPE)
    win_g, = _all_gather(pack, ((0, dev, 0, W_IN_ROWS),), ((N_DEV, W_IN_ROWS, PACK_W),), name="ag_early")
    ag_mid = _gather_start(pack, W_IN_ROWS, ((0, dev, OUT_ROWS), (1, dev, SMALL_ROWS)),
                           ((N_DEV, OUT_ROWS, PACK_W), (N_DEV, SMALL_ROWS, PACK_W)), name="ag_mid_start")
    ag = {}

    def own_rows(r0, r1, shape):
        return pack[r0:r1].reshape(shape)

    def mid_weights(after):
        pack_mid, (wout_g, small_g) = _gather_wait(ag_mid, W_IN_ROWS, MID_ROWS, after, name="ag_mid_wait")
        ag["late"] = _gather_start(pack_mid, EARLY_ROWS, ((0, _gate_slab, FF_COLS), (0, _up_slab, FF_COLS), (1, dev, FF_COLS)),
                                   (GU_SHAPE, D_SHAPE), name="ag_late_start")
        wout_g = lax.dynamic_update_slice(wout_g, own_rows(W_IN_ROWS, SMALL_ROW0, (1, OUT_ROWS, PACK_W)), (me, 0, 0))
        small_g = lax.dynamic_update_slice(small_g, own_rows(SMALL_ROW0, EARLY_ROWS, (1, SMALL_ROWS, PACK_W)), (me, 0, 0))
        ops = _mid_operands(wout_g, small_g)
        ops["wuq"] = ops["wuq"] + ag["late"][-1][0:1, 0:1].astype(ops["wuq"].dtype)
        return ops

    def late_weights(after):
        _, (gu, d) = _gather_wait(ag["late"], EARLY_ROWS, LATE_ROWS, after, name="ag_late_wait")
        slab = (1, 1, 1, FF_COLS, PACK_W)
        gu = lax.dynamic_update_slice(gu, own_rows(EARLY_ROWS, EARLY_ROWS + FF_COLS, slab), _gate_slab(me) + (0, 0))
        gu = lax.dynamic_update_slice(gu, own_rows(EARLY_ROWS + FF_COLS, EARLY_ROWS + 2 * FF_COLS, slab), _up_slab(me) + (0, 0))
        d = lax.dynamic_update_slice(d, own_rows(EARLY_ROWS + 2 * FF_COLS, PACK_ROWS, (1, FF_COLS, PACK_W)), (me, 0, 0))
        return gu.reshape(2 * D_FF, D_MODEL), d.reshape(D_FF, D_MODEL)

    rs = {}

    def late_grads(g_gu, g_d):
        rs["late"] = _scatter_start([g_gu.reshape(GU_SHAPE), g_d.reshape(D_SHAPE)],
                                    ((0, _gate_slab, FF_COLS), (0, _up_slab, FF_COLS), (1, dev, FF_COLS)),
                                    name="rs_late_start")
        return rs["late"][-1]

    def mid_grads(g):
        rs["mid"] = _scatter_start([_mid_grad_pack(g)], ((0, dev, MID_ROWS),), name="rs_mid_start")
        return rs["mid"][-1]

    def last_grads(g_win_t):
        rs["last"] = _scatter_start([_w_in_grad_chunks(g_win_t)], ((0, dev, W_IN_ROWS),), name="rs_last_start")
        return rs["last"][-1]

    first_w = dict(small_w, mix_norm_g=mix_norm_g + ag_mid[-1][0:1, 0:1])
    loss_tot, gx, g_small = _local_step(
        x[0], loss_target[0], _w_in_operand(win_g), first_w, types.SimpleNamespace(mid=mid_weights, late=late_weights),
        types.SimpleNamespace(late=late_grads, mid=mid_grads, last=last_grads))

    g_gu, g_d, land_late = _scatter_wait(rs["late"], gx, name="rs_late_wait")
    g_mid, land_mid = _scatter_wait(rs["mid"], gx, name="rs_mid_wait")
    g_win, land_last = _scatter_wait(rs["last"], gx, name="rs_last_wait")
    gw = dict(w_gate=_peer_sum(g_gu, _gate_slab, land_late, 0, FF_COLS, idx, name="rs_sum_gate").T,
              w_up=_peer_sum(g_gu, _up_slab, land_late, 1, FF_COLS, idx, name="rs_sum_up").T,
              w_down=_peer_sum(g_d, dev, land_late, 2, FF_COLS, idx, name="rs_sum_down"),
              w_in=_peer_sum(g_win, dev, land_last, 0, W_IN_ROWS, idx, name="rs_sum_in")[0:W_IN_COLS].T)
    gw.update(_mid_unpack(_peer_sum(g_mid, dev, land_mid, 0, MID_ROWS, idx, name="rs_sum_mid")))
    dw, mw, vw = {}, {}, {}
    for n in BIG:
        dw[n], mw[n], vw[n] = _adamw_call(big_w[n], gw[n], big_m[n], big_v[n], name="adamw_" + n)

    loss_rows = jnp.pad(loss_tot[0:1, 0:1], ((0, SUBLANES - 1), (0, PACK_W - 1)))
    small_rows = jnp.concatenate([_small_pack(g_small_rows(g_small), SUBLANES), loss_rows], 0)
    parts, = _all_gather(small_rows, ((0, lambda d: (d,), 0, small_rows.shape[0]),), ((N_DEV,) + small_rows.shape,),
                         name="ag_small")
    gs, ds, ms, vs = _small_adamw(parts, _small_pack(small_w, 1), _small_pack(small_m, 1), _small_pack(small_v, 1))
    loss = gs[len(SMALL), 0]

    def small_out(packed):
        out = {}
        for k, n in enumerate(SMALL):
            out[n] = packed[k:k + 1, :SMALL_W[n]]
        out["final_norm_g"] = out["final_norm_g"].reshape(D_MODEL)
        return out

    gs, ds, ms, vs = small_out(gs), small_out(ds), small_out(ms), small_out(vs)

    order = ("mix_norm_g", "w_in", "swa_sinks", "q_norm_g", "w_uq", "kv_norm_g", "w_ukv", "w_o_swa", "w_o_mla", "w_out",
             "ffn_norm_g", "w_gate", "w_up", "w_down", "final_norm_g")

    def leaves(big, small):
        return [big[n][None] if n in big else small[n] for n in order]

    return (loss, gx[None], *leaves(gw, gs), *leaves(dw, ds), *leaves(mw, ms), *leaves(vw, vs))


def g_small_rows(g_small):
    out = dict(g_small)
    out["swa_sinks"] = jnp.pad(g_small["swa_sinks"], ((0, SUBLANES - 1), (0, 0)))
    return out
```

```python
import types

import numpy as np
import jax
import jax.numpy as jnp
from jax import lax
from jax.experimental import pallas as pl
from jax.experimental.pallas import tpu as pltpu

F32 = jnp.float32
MXU_DTYPE = jnp.bfloat16
WIRE_DTYPE = jnp.bfloat16

D_MODEL = 1024
EPS = 1e-6
ROPE_THETA = 10000.0
BLOCK = 128
HEAD_DIM = 64
SWA_HEADS = 8
SWA_KV_HEADS = 2
SWA_GROUP = SWA_HEADS // SWA_KV_HEADS
MLA_HEADS = 8
MLA_NOPE = 64
MLA_ROPE = 32
MLA_V = 64
MLA_QK = MLA_NOPE + MLA_ROPE
Q_LORA = 384
KV_LORA = 256
D_FF = 2816
IN_SIZES = (512, 128, 128, Q_LORA, KV_LORA, MLA_ROPE, D_MODEL, D_MODEL)
IN_OFF = tuple(int(v) for v in np.cumsum((0,) + IN_SIZES))
ADAM_LR, ADAM_B1, ADAM_B2, ADAM_EPS, ADAM_WD, ADAM_STEP = 0.001, 0.9, 0.999, 1e-08, 0.01, 10

LANES = 128
SUBLANES = 8
VMEM_LIMIT = 48 * 1024 * 1024
N_DEV = 8
AXES = ("x", "y", "c")

P_GA, P_GB, P_Q, P_QLAT, P_KR, P_K, P_V, P_KVLAT, P_W = 0, 1024, 2048, 3072, 3456, 3584, 3840, 4096, 4352
KR_LANE = 64

LOG2E = 1.4426950408889634

NT = (((1,), (1,)), ((), ()))
NN = (((1,), (0,)), ((), ()))
TN = (((0,), (0,)), ((), ()))


def _cparams(sem):
    return pltpu.CompilerParams(dimension_semantics=sem, vmem_limit_bytes=VMEM_LIMIT)


def _mm(a, b, mode, *, name, out_dtype=F32, add=None, after=None, tm=512, tn=512, tk=None):
    if mode == "nn":
        (M, K), (K2, N) = a.shape, b.shape
    elif mode == "nt":
        (M, K), (N, K2) = a.shape, b.shape
    else:
        (K, M), (K2, N) = a.shape, b.shape
    assert K == K2, (a.shape, b.shape, mode)
    tk = K if tk is None else tk
    tm, tn = min(tm, M), min(tn, N)
    assert M % tm == 0 and N % tn == 0 and K % tk == 0, (M, N, K, tm, tn, tk)
    nk = K // tk
    dn = {"nn": NN, "nt": NT, "tn": TN}[mode]
    if mode == "tn":
        a_spec = pl.BlockSpec((tk, tm), lambda i, j, k: (k, i))
    else:
        a_spec = pl.BlockSpec((tm, tk), lambda i, j, k: (i, k))
    if mode == "nt":
        b_spec = pl.BlockSpec((tn, tk), lambda i, j, k: (j, k))
    else:
        b_spec = pl.BlockSpec((tk, tn), lambda i, j, k: (k, j))
    o_spec = pl.BlockSpec((tm, tn), lambda i, j, k: (i, j))
    has_add, has_after = add is not None, after is not None

    def body(*refs):
        a_ref, b_ref = refs[0], refs[1]
        add_ref = refs[2] if has_add else None
        o_ref = refs[2 + has_add + has_after]
        p = lax.dot_general(a_ref[...], b_ref[...], dn, preferred_element_type=F32)

        def finish(acc):
            if has_add:
                acc = acc + add_ref[...]
            o_ref[...] = acc.astype(o_ref.dtype)

        if nk == 1:
            finish(p)
        else:
            acc_ref = refs[-1]
            k = pl.program_id(2)

            @pl.when(k == 0)
            def _():
                acc_ref[...] = p

            @pl.when(k > 0)
            def _():
                acc_ref[...] += p

            @pl.when(k == nk - 1)
            def _():
                finish(acc_ref[...])

    ins = [a, b] + ([add] if has_add else []) + ([after] if has_after else [])
    in_specs = [a_spec, b_spec] + ([o_spec] if has_add else []) + ([pl.BlockSpec(memory_space=pl.ANY)] if has_after else [])
    return pl.pallas_call(
        body, name=name, grid=(M // tm, N // tn, nk), in_specs=in_specs, out_specs=o_spec,
        out_shape=jax.ShapeDtypeStruct((M, N), out_dtype),
        scratch_shapes=[pltpu.VMEM((tm, tn), F32)] if nk > 1 else [],
        compiler_params=_cparams(("parallel", "parallel", "arbitrary")),
    )(*ins)


def _rows(ts, w, cb=0):
    return pl.BlockSpec((ts, w), lambda i: (i, cb))


def _const(r, w):
    return pl.BlockSpec((r, w), lambda i: (0, 0))


def _sublane_sum(v):
    ts, c = v.shape
    return jnp.sum(v.reshape(ts // SUBLANES, SUBLANES, c), axis=0)


def _sigmoid(v):
    return 1.0 / (1.0 + jnp.exp(-v))


def _rope(v, cos, s_up, s_dn, up, dn):
    return v * cos + pltpu.roll(v, up, 1) * s_up + pltpu.roll(v, dn, 1) * s_dn


def _rope_t(dv, cos, s_up, s_dn, up, dn):
    return dv * cos + pltpu.roll(dv * s_up, dn, 1) + pltpu.roll(dv * s_dn, up, 1)


def _rope_tables(seq):
    pos = np.arange(seq, dtype=np.float32)[:, None]

    def base(dim):
        inv = np.float32(ROPE_THETA) ** (-np.arange(0, dim, 2, dtype=np.float32) / np.float32(dim))
        ang = (pos * inv.astype(np.float32)[None, :]).astype(np.float32)
        return np.cos(ang).astype(np.float32), np.sin(ang).astype(np.float32)

    z = lambda n: np.zeros((seq, n), np.float32)
    ca, sa = base(HEAD_DIM)
    a_cos = np.concatenate([ca, ca, z(64)], 1)
    a_up = np.concatenate([-sa, z(96)], 1)
    a_dn = np.concatenate([z(32), sa, z(64)], 1)
    cb, sb = base(MLA_ROPE)
    one = np.ones((seq, 64), np.float32)
    q_cos = np.concatenate([one, cb, cb, z(32)], 1)
    k_cos = np.concatenate([z(64), cb, cb, z(32)], 1)
    b_up = np.concatenate([z(64), -sb, z(48)], 1)
    b_dn = np.concatenate([z(80), sb, z(32)], 1)
    return tuple(jnp.asarray(t) for t in (a_cos, a_up, a_dn, q_cos, k_cos, b_up, b_dn))


def _norm_fwd(x, g, *, name, ts=256):
    s_, c = x.shape

    def body(x_ref, g_ref, h_ref):
        v = x_ref[...]
        r = lax.rsqrt(jnp.mean(v * v, axis=-1, keepdims=True) + EPS)
        h_ref[...] = (v * r * g_ref[...]).astype(h_ref.dtype)

    return pl.pallas_call(
        body, name=name, grid=(s_ // ts,), in_specs=[_rows(ts, c), _const(1, c)], out_specs=_rows(ts, c),
        out_shape=jax.ShapeDtypeStruct((s_, c), MXU_DTYPE), compiler_params=_cparams(("parallel",)),
    )(x, g)


def _norm_bwd(x, g, dy, res, *, name, ts=256, x_cb=0, x_src_w=None):
    s_ = x.shape[0]
    c = dy.shape[1]
    has_res = res is not None

    def body(*refs):
        x_ref, g_ref, dy_ref = refs[0], refs[1], refs[2]
        res_ref = refs[3] if has_res else None
        dx_ref, dxb_ref, dg_ref = refs[-3], refs[-2], refs[-1]
        v = x_ref[...]
        r = lax.rsqrt(jnp.mean(v * v, axis=-1, keepdims=True) + EPS)
        xh = v * r
        d = dy_ref[...]
        dxh = d * g_ref[...]
        dx = r * (dxh - xh * jnp.mean(dxh * xh, axis=-1, keepdims=True))
        if has_res:
            dx = dx + res_ref[...]
        dx_ref[...] = dx
        dxb_ref[...] = dx.astype(dxb_ref.dtype)

        @pl.when(pl.program_id(0) == 0)
        def _():
            dg_ref[...] = jnp.zeros(dg_ref.shape, F32)

        dg_ref[...] += _sublane_sum(d * xh)

    ins = [x, g, dy] + ([res] if has_res else [])
    in_specs = [_rows(ts, c, x_cb), _const(1, c), _rows(ts, c)] + ([_rows(ts, c)] if has_res else [])
    return pl.pallas_call(
        body, name=name, grid=(s_ // ts,), in_specs=in_specs,
        out_specs=[_rows(ts, c), _rows(ts, c), _const(SUBLANES, c)],
        out_shape=[jax.ShapeDtypeStruct((s_, c), F32), jax.ShapeDtypeStruct((s_, c), MXU_DTYPE),
                   jax.ShapeDtypeStruct((SUBLANES, c), F32)],
        compiler_params=_cparams(("arbitrary",)),
    )(*ins)


def _attn_prep(p, gq, gkv, tabs, *, ts=256):
    s_ = p.shape[0]
    a_cos, a_up, a_dn, _, k_cos, b_up, b_dn = tabs

    def body(q_ref, k_ref, v_ref, ql_ref, kvl_ref, kr_ref, gq_ref, gkv_ref, ac, au, ad, kc, bu, bd,
             qa_ref, ka_ref, va_ref, cq_ref, ckv_ref, kro_ref):
        c_, u_, d_ = ac[...], au[...], ad[...]
        for h in range(SWA_HEADS):
            sl = slice(h * LANES, (h + 1) * LANES)
            qa_ref[:, sl] = _rope(q_ref[:, sl], c_, u_, d_, 96, 32).astype(qa_ref.dtype)
        for h in range(SWA_KV_HEADS):
            sl = slice(h * LANES, (h + 1) * LANES)
            ka_ref[:, sl] = _rope(k_ref[:, sl], c_, u_, d_, 96, 32).astype(ka_ref.dtype)
        va_ref[...] = v_ref[...].astype(va_ref.dtype)
        for src, gref, dst in ((ql_ref, gq_ref, cq_ref), (kvl_ref, gkv_ref, ckv_ref)):
            v = src[...]
            r = lax.rsqrt(jnp.mean(v * v, axis=-1, keepdims=True) + EPS)
            dst[...] = (v * r * gref[...]).astype(dst.dtype)
        kro_ref[...] = _rope(kr_ref[...], kc[...], bu[...], bd[...], 112, 16)

    tab = _rows(ts, LANES)
    return pl.pallas_call(
        body, name="attn_prep", grid=(s_ // ts,),
        in_specs=[_rows(ts, 1024, P_Q // 1024), _rows(ts, 256, P_K // 256), _rows(ts, 256, P_V // 256),
                  _rows(ts, Q_LORA, P_QLAT // Q_LORA), _rows(ts, KV_LORA, P_KVLAT // KV_LORA),
                  _rows(ts, LANES, P_KR // LANES), _const(1, Q_LORA), _const(1, KV_LORA), tab, tab, tab, tab, tab, tab],
        out_specs=[_rows(ts, 1024), _rows(ts, 256), _rows(ts, 256), _rows(ts, Q_LORA), _rows(ts, KV_LORA),
                   _rows(ts, LANES)],
        out_shape=[jax.ShapeDtypeStruct((s_, 1024), MXU_DTYPE), jax.ShapeDtypeStruct((s_, 256), MXU_DTYPE),
                   jax.ShapeDtypeStruct((s_, 256), MXU_DTYPE), jax.ShapeDtypeStruct((s_, Q_LORA), MXU_DTYPE),
                   jax.ShapeDtypeStruct((s_, KV_LORA), MXU_DTYPE), jax.ShapeDtypeStruct((s_, LANES), F32)],
        compiler_params=_cparams(("parallel",)),
    )(p, p, p, p, p, p, gq, gkv, a_cos, a_up, a_dn, k_cos, b_up, b_dn)


def _mla_prep(qp, kp, kro, tabs, *, ts=256):
    s_ = qp.shape[0]
    _, _, _, q_cos, _, b_up, b_dn = tabs

    def body(q_ref, k_ref, kr_ref, qc, bu, bd, qo_ref, ko_ref):
        c_, u_, d_ = qc[...], bu[...], bd[...]
        kr = kr_ref[...]
        for h in range(MLA_HEADS):
            sl = slice(h * LANES, (h + 1) * LANES)
            qo_ref[:, sl] = _rope(q_ref[:, sl], c_, u_, d_, 112, 16).astype(qo_ref.dtype)
            ko_ref[:, sl] = (k_ref[:, sl] + kr).astype(ko_ref.dtype)

    tab = _rows(ts, LANES)
    return pl.pallas_call(
        body, name="mla_prep", grid=(s_ // ts,),
        in_specs=[_rows(ts, 1024), _rows(ts, 1024), tab, tab, tab, tab],
        out_specs=[_rows(ts, 1024), _rows(ts, 1024)],
        out_shape=[jax.ShapeDtypeStruct((s_, 1024), MXU_DTYPE)] * 2,
        compiler_params=_cparams(("parallel",)),
    )(qp, kp, kro, q_cos, b_up, b_dn)


def _mla_unprep(dqc, dkt, dvt, tabs):
    s_ = dqc.shape[0]
    ts = dkt.shape[3]
    _, _, _, q_cos, k_cos, b_up, b_dn = tabs

    def body(dq_ref, dk_ref, dv_ref, qc, kc, bu, bd, dqo_ref, dkvo_ref, dkr_ref):
        c_, u_, d_ = qc[...], bu[...], bd[...]
        tot = jnp.zeros((ts, LANES), F32)
        for h in range(MLA_HEADS):
            sl = slice(h * LANES, (h + 1) * LANES)
            dqo_ref[:, sl] = _rope_t(dq_ref[:, sl], c_, u_, d_, 112, 16).astype(dqo_ref.dtype)
            dk = dk_ref[h, 0].T
            dkvo_ref[:, sl] = dk.astype(dkvo_ref.dtype)
            tot = tot + dk
            dkvo_ref[:, 1024 + h * LANES:1024 + (h + 1) * LANES] = dv_ref[h, 0].T.astype(dkvo_ref.dtype)
        dkr_ref[...] = _rope_t(tot, kc[...], u_, d_, 112, 16).astype(dkr_ref.dtype)

    tab = _rows(ts, LANES)
    acc = pl.BlockSpec((MLA_HEADS, 1, LANES, ts), lambda i: (0, i, 0, 0))
    return pl.pallas_call(
        body, name="mla_unprep", grid=(s_ // ts,),
        in_specs=[_rows(ts, 1024), acc, acc, tab, tab, tab, tab],
        out_specs=[_rows(ts, 1024), _rows(ts, 2048), _rows(ts, LANES)],
        out_shape=[jax.ShapeDtypeStruct((s_, 1024), MXU_DTYPE), jax.ShapeDtypeStruct((s_, 2048), MXU_DTYPE),
                   jax.ShapeDtypeStruct((s_, LANES), MXU_DTYPE)],
        compiler_params=_cparams(("parallel",)),
    )(dqc, dkt, dvt, q_cos, k_cos, b_up, b_dn)


def _swa_unrope(dqa, dka, tabs, *, ts=256):
    s_ = dqa.shape[0]
    a_cos, a_up, a_dn = tabs[0], tabs[1], tabs[2]

    def body(dq_ref, dk_ref, ac, au, ad, dqo_ref, dko_ref):
        c_, u_, d_ = ac[...], au[...], ad[...]
        for h in range(SWA_HEADS):
            sl = slice(h * LANES, (h + 1) * LANES)
            dqo_ref[:, sl] = _rope_t(dq_ref[:, sl], c_, u_, d_, 96, 32).astype(dqo_ref.dtype)
        for h in range(SWA_KV_HEADS):
            sl = slice(h * LANES, (h + 1) * LANES)
            dko_ref[:, sl] = _rope_t(dk_ref[:, sl], c_, u_, d_, 96, 32).astype(dko_ref.dtype)

    tab = _rows(ts, LANES)
    return pl.pallas_call(
        body, name="swa_unrope", grid=(s_ // ts,),
        in_specs=[_rows(ts, 1024), _rows(ts, 256), tab, tab, tab],
        out_specs=[_rows(ts, 1024), _rows(ts, 256)],
        out_shape=[jax.ShapeDtypeStruct((s_, 1024), MXU_DTYPE), jax.ShapeDtypeStruct((s_, 256), MXU_DTYPE)],
        compiler_params=_cparams(("parallel",)),
    )(dqa, dka, a_cos, a_up, a_dn)


def _attn_out_gate(oa, ob, woa, wob, p, *, ts=512):
    s_ = p.shape[0]

    def body(oa_ref, ob_ref, wa_ref, wb_ref, ga_ref, gb_ref, ta_ref, tb_ref, y_ref):
        ta = jnp.dot(oa_ref[...], wa_ref[...], preferred_element_type=F32)
        tb = jnp.dot(ob_ref[...], wb_ref[...], preferred_element_type=F32)
        ta_ref[...] = ta
        tb_ref[...] = tb
        y_ref[...] = (_sigmoid(ga_ref[...]) * ta + _sigmoid(gb_ref[...]) * tb).astype(y_ref.dtype)

    w = _const(1024, 1024)
    return pl.pallas_call(
        body, name="attn_out_gate", grid=(s_ // ts,),
        in_specs=[_rows(ts, 1024), _rows(ts, 1024), w, w, _rows(ts, 1024, P_GA // 1024), _rows(ts, 1024, P_GB // 1024)],
        out_specs=[_rows(ts, 1024)] * 3,
        out_shape=[jax.ShapeDtypeStruct((s_, 1024), F32)] * 2 + [jax.ShapeDtypeStruct((s_, 1024), MXU_DTYPE)],
        compiler_params=_cparams(("parallel",)),
    )(oa, ob, woa, wob, p, p)


def _d_y_gate(dx1b, wout, p, ta, tb, *, ts=512):
    s_ = p.shape[0]

    def body(dx_ref, w_ref, ga_ref, gb_ref, ta_ref, tb_ref, dta_ref, dtb_ref, dg_ref):
        d = lax.dot_general(dx_ref[...], w_ref[...], NT, preferred_element_type=F32)
        sa, sb = _sigmoid(ga_ref[...]), _sigmoid(gb_ref[...])
        dta_ref[...] = (d * sa).astype(dta_ref.dtype)
        dtb_ref[...] = (d * sb).astype(dtb_ref.dtype)
        dg_ref[:, 0:1024] = (d * ta_ref[...] * (sa * (1.0 - sa))).astype(dg_ref.dtype)
        dg_ref[:, 1024:2048] = (d * tb_ref[...] * (sb * (1.0 - sb))).astype(dg_ref.dtype)

    return pl.pallas_call(
        body, name="d_y_gate", grid=(s_ // ts,),
        in_specs=[_rows(ts, 1024), _const(1024, 1024), _rows(ts, 1024, P_GA // 1024), _rows(ts, 1024, P_GB // 1024),
                  _rows(ts, 1024), _rows(ts, 1024)],
        out_specs=[_rows(ts, 1024), _rows(ts, 1024), _rows(ts, 2048)],
        out_shape=[jax.ShapeDtypeStruct((s_, 1024), MXU_DTYPE)] * 2 + [jax.ShapeDtypeStruct((s_, 2048), MXU_DTYPE)],
        compiler_params=_cparams(("parallel",)),
    )(dx1b, wout, p, p, ta, tb)


FF_TILE = D_FF // 2


def _ffn_in_act(h2, wgu_t, *, tm=512):
    s_ = h2.shape[0]

    def body(h_ref, w_ref, gu_ref, a_ref):
        p = lax.dot_general(h_ref[...], w_ref[...], NT, preferred_element_type=F32)
        gu_ref[...] = p
        g = p[:, :FF_TILE]
        a_ref[...] = (g * _sigmoid(g) * p[:, FF_TILE:]).astype(a_ref.dtype)

    return pl.pallas_call(
        body, name="ffn_in", grid=(s_ // tm, 2),
        in_specs=[pl.BlockSpec((tm, D_MODEL), lambda i, j: (i, 0)), pl.BlockSpec((2 * FF_TILE, D_MODEL), lambda i, j: (j, 0))],
        out_specs=[pl.BlockSpec((tm, 2 * FF_TILE), lambda i, j: (i, j)), pl.BlockSpec((tm, FF_TILE), lambda i, j: (i, j))],
        out_shape=[jax.ShapeDtypeStruct((s_, 2 * D_FF), F32), jax.ShapeDtypeStruct((s_, D_FF), MXU_DTYPE)],
        compiler_params=_cparams(("parallel", "parallel")),
    )(h2, wgu_t)


def _d_act_swiglu(dx2b, wd, gu, *, tm=512):
    s_ = dx2b.shape[0]

    def body(d_ref, w_ref, gu_ref, o_ref):
        da = lax.dot_general(d_ref[...], w_ref[...], NT, preferred_element_type=F32)
        g, u = gu_ref[:, :FF_TILE], gu_ref[:, FF_TILE:]
        sg = _sigmoid(g)
        o_ref[:, :FF_TILE] = (da * u * (sg * (1.0 + g * (1.0 - sg)))).astype(o_ref.dtype)
        o_ref[:, FF_TILE:] = (da * (g * sg)).astype(o_ref.dtype)

    gu_spec = pl.BlockSpec((tm, 2 * FF_TILE), lambda i, j: (i, j))
    return pl.pallas_call(
        body, name="d_act", grid=(s_ // tm, 2),
        in_specs=[pl.BlockSpec((tm, D_MODEL), lambda i, j: (i, 0)), pl.BlockSpec((FF_TILE, D_MODEL), lambda i, j: (j, 0)), gu_spec],
        out_specs=gu_spec, out_shape=jax.ShapeDtypeStruct((s_, 2 * D_FF), MXU_DTYPE),
        compiler_params=_cparams(("parallel", "parallel")),
    )(dx2b, wd, gu)


def _loss_bwd(x2, g, tgt, *, ts=256):
    s_, c = x2.shape

    def body(x_ref, g_ref, t_ref, dx_ref, dxb_ref, dg_ref, lp_ref, tot_ref):
        v = x_ref[...]
        r = lax.rsqrt(jnp.mean(v * v, axis=-1, keepdims=True) + EPS)
        xh = v * r
        gg = g_ref[...]
        e = xh * gg - t_ref[...]
        do = e * (1.0 / c)
        dxh = do * gg
        dx = r * (dxh - xh * jnp.mean(dxh * xh, axis=-1, keepdims=True))
        dx_ref[...] = dx
        dxb_ref[...] = dx.astype(dxb_ref.dtype)
        i = pl.program_id(0)

        @pl.when(i == 0)
        def _():
            dg_ref[...] = jnp.zeros(dg_ref.shape, F32)
            lp_ref[...] = jnp.zeros(lp_ref.shape, F32)

        dg_ref[...] += _sublane_sum(do * xh)
        lp_ref[...] += _sublane_sum(e * e)
        tot_ref[...] = jnp.full(tot_ref.shape, (0.5 / c) * jnp.sum(lp_ref[...]), F32)

    return pl.pallas_call(
        body, name="loss_bwd", grid=(s_ // ts,), in_specs=[_rows(ts, c), _const(1, c), _rows(ts, c)],
        out_specs=[_rows(ts, c), _rows(ts, c), _const(SUBLANES, c), _const(SUBLANES, c), _const(SUBLANES, LANES)],
        out_shape=[jax.ShapeDtypeStruct((s_, c), F32), jax.ShapeDtypeStruct((s_, c), MXU_DTYPE),
                   jax.ShapeDtypeStruct((SUBLANES, c), F32), jax.ShapeDtypeStruct((SUBLANES, c), F32),
                   jax.ShapeDtypeStruct((SUBLANES, LANES), F32)],
        compiler_params=_cparams(("arbitrary",)),
    )(x2, g, tgt)


def _mla_bwd_prep(dob, o32, *, ts=256):
    s_ = dob.shape[0]

    def body(do_ref, o_ref, dob_ref, dl_ref):
        d = do_ref[...]
        dob_ref[...] = d.astype(dob_ref.dtype)
        prod = d * o_ref[...]
        for h in range(MLA_HEADS):
            dl_ref[h] = jnp.sum(prod[:, h * LANES:(h + 1) * LANES].T, axis=0, keepdims=True)

    return pl.pallas_call(
        body, name="mla_bwd_prep", grid=(s_ // ts,), in_specs=[_rows(ts, 1024), _rows(ts, 1024)],
        out_specs=[_rows(ts, 1024), pl.BlockSpec((MLA_HEADS, 1, ts), lambda i: (0, 0, i))],
        out_shape=[jax.ShapeDtypeStruct((s_, 1024), MXU_DTYPE), jax.ShapeDtypeStruct((MLA_HEADS, 1, s_), F32)],
        compiler_params=_cparams(("parallel",)),
    )(dob, o32)


SWA_T = 4 * BLOCK


def _swa_masks(sb):
    kr = lax.broadcasted_iota(jnp.int32, (2 * BLOCK, BLOCK), 0)
    qc = lax.broadcasted_iota(jnp.int32, (2 * BLOCK, BLOCK), 1)
    band = jnp.logical_and(kr > qc, kr <= qc + BLOCK)
    first = jnp.logical_and(band, kr >= BLOCK)
    return band, jnp.logical_or(first, jnp.logical_and(band, sb > 0))


def _swa_in_specs(rev, nsb):
    sbi = (lambda j: nsb - 1 - j) if rev else (lambda j: j)
    cur = pl.BlockSpec((SWA_T, LANES), lambda g, j: (sbi(j), g))
    prev = pl.BlockSpec((BLOCK, LANES), lambda g, j: (jnp.maximum(4 * sbi(j) - 1, 0), g))
    q = pl.BlockSpec((SWA_T, SWA_GROUP * LANES), lambda g, j: (sbi(j), g))
    sink = pl.BlockSpec((1, SUBLANES, LANES), lambda g, j: (g, 0, 0))
    lse = pl.BlockSpec((SWA_GROUP, 1, SWA_T), lambda g, j: (g, 0, sbi(j)))
    return q, cur, prev, sink, lse


def _swa_fwd(qa, ka, va, sink_b):
    s_ = qa.shape[0]
    nsb = s_ // SWA_T
    c2 = HEAD_DIM ** -0.5 * LOG2E

    def body(q_ref, kc_ref, kp_ref, vc_ref, vp_ref, sk_ref, o32_ref, o16_ref, lse_ref, kx, vx):
        kx[0:BLOCK, :] = kp_ref[...]
        kx[BLOCK:5 * BLOCK, :] = kc_ref[...]
        vx[0:BLOCK, :] = vp_ref[...]
        vx[BLOCK:5 * BLOCK, :] = vc_ref[...]
        band, band0 = _swa_masks(pl.program_id(1))
        for hh in range(SWA_GROUP):
            sink2 = sk_ref[0, hh:hh + 1, 0:1] * LOG2E
            cs = slice(hh * LANES, (hh + 1) * LANES)
            for b in range(4):
                rs = slice(b * BLOCK, (b + 1) * BLOCK)
                ks = slice(b * BLOCK, (b + 2) * BLOCK)
                st = lax.dot_general(kx[ks, :], q_ref[rs, cs], NT, preferred_element_type=F32) * c2
                st = jnp.where(band0 if b == 0 else band, st, -jnp.inf)
                m = jnp.maximum(jnp.max(st, axis=0, keepdims=True), sink2)
                pt = jnp.exp2(st - m)
                den = jnp.sum(pt, axis=0, keepdims=True) + jnp.exp2(sink2 - m)
                o = lax.dot_general((pt * (1.0 / den)).astype(MXU_DTYPE), vx[ks, :], TN, preferred_element_type=F32)
                o32_ref[rs, cs] = o
                o16_ref[rs, cs] = o.astype(o16_ref.dtype)
                lse_ref[hh, :, rs] = m + jnp.log2(den)

    q, cur, prev, sink, lse_spec = _swa_in_specs(False, nsb)
    return pl.pallas_call(
        body, name="swa_fwd", grid=(SWA_KV_HEADS, nsb), in_specs=[q, cur, prev, cur, prev, sink],
        out_specs=[q, q, lse_spec],
        out_shape=[jax.ShapeDtypeStruct((s_, SWA_HEADS * LANES), F32), jax.ShapeDtypeStruct((s_, SWA_HEADS * LANES), MXU_DTYPE),
                   jax.ShapeDtypeStruct((SWA_HEADS, 1, s_), F32)],
        scratch_shapes=[pltpu.VMEM((5 * BLOCK, LANES), MXU_DTYPE), pltpu.VMEM((5 * BLOCK, LANES), MXU_DTYPE)],
        compiler_params=_cparams(("parallel", "arbitrary")),
    )(qa, ka, ka, va, va, sink_b)


def _swa_bwd(qa, ka, va, sink_b, o32, do, lse):
    s_ = qa.shape[0]
    nsb = s_ // SWA_T
    scale = HEAD_DIM ** -0.5
    c2 = scale * LOG2E

    def body(q_ref, kc_ref, kp_ref, vc_ref, vp_ref, sk_ref, o_ref, do_ref, lse_ref,
             dq_ref, dk_ref, dv_ref, dsk_ref, kx, vx, kacc, vacc, kcar, vcar):
        j = pl.program_id(1)
        kx[0:BLOCK, :] = kp_ref[...]
        kx[BLOCK:5 * BLOCK, :] = kc_ref[...]
        vx[0:BLOCK, :] = vp_ref[...]
        vx[BLOCK:5 * BLOCK, :] = vc_ref[...]
        band, band0 = _swa_masks(nsb - 1 - j)
        kacc[...] = jnp.zeros(kacc.shape, F32)
        vacc[...] = jnp.zeros(vacc.shape, F32)

        @pl.when(j == 0)
        def _():
            kcar[...] = jnp.zeros(kcar.shape, F32)
            vcar[...] = jnp.zeros(vcar.shape, F32)
            dsk_ref[...] = jnp.zeros(dsk_ref.shape, F32)

        for hh in range(SWA_GROUP):
            sink2 = sk_ref[0, hh:hh + 1, 0:1] * LOG2E
            cs = slice(hh * LANES, (hh + 1) * LANES)
            dsink = jnp.zeros((1, 1), F32)
            for b in range(4):
                rs = slice(b * BLOCK, (b + 1) * BLOCK)
                ks = slice(b * BLOCK, (b + 2) * BLOCK)
                q, k2, v2 = q_ref[rs, cs], kx[ks, :], vx[ks, :]
                d = do_ref[rs, cs]
                delta = jnp.sum((d * o_ref[rs, cs]).T, axis=0, keepdims=True)
                l2 = lse_ref[hh, :, rs]
                st = lax.dot_general(k2, q, NT, preferred_element_type=F32) * c2
                pt = jnp.exp2(jnp.where(band0 if b == 0 else band, st, -jnp.inf) - l2)
                db = d.astype(MXU_DTYPE)
                dst = (pt * (lax.dot_general(v2, db, NT, preferred_element_type=F32) - delta) * scale).astype(MXU_DTYPE)
                dq_ref[rs, cs] = lax.dot_general(dst, k2, TN, preferred_element_type=F32)
                kacc[ks, :] += jnp.dot(dst, q, preferred_element_type=F32)
                vacc[ks, :] += jnp.dot(pt.astype(MXU_DTYPE), db, preferred_element_type=F32)
                dsink = dsink - jnp.sum(jnp.exp2(sink2 - l2) * delta, axis=1, keepdims=True)
            dsk_ref[0, hh:hh + 1, :] += jnp.broadcast_to(dsink, (1, LANES))

        dk_ref[0:3 * BLOCK, :] = kacc[BLOCK:4 * BLOCK, :]
        dk_ref[3 * BLOCK:4 * BLOCK, :] = kacc[4 * BLOCK:5 * BLOCK, :] + kcar[...]
        dv_ref[0:3 * BLOCK, :] = vacc[BLOCK:4 * BLOCK, :].astype(dv_ref.dtype)
        dv_ref[3 * BLOCK:4 * BLOCK, :] = (vacc[4 * BLOCK:5 * BLOCK, :] + vcar[...]).astype(dv_ref.dtype)
        kcar[...] = kacc[0:BLOCK, :]
        vcar[...] = vacc[0:BLOCK, :]

    q, cur, prev, sink, lse_spec = _swa_in_specs(True, nsb)
    return pl.pallas_call(
        body, name="swa_bwd", grid=(SWA_KV_HEADS, nsb),
        in_specs=[q, cur, prev, cur, prev, sink, q, q, lse_spec],
        out_specs=[q, cur, cur, sink],
        out_shape=[jax.ShapeDtypeStruct((s_, SWA_HEADS * LANES), F32), jax.ShapeDtypeStruct((s_, SWA_KV_HEADS * LANES), F32),
                   jax.ShapeDtypeStruct((s_, SWA_KV_HEADS * LANES), MXU_DTYPE),
                   jax.ShapeDtypeStruct((SWA_KV_HEADS, SUBLANES, LANES), F32)],
        scratch_shapes=[pltpu.VMEM((5 * BLOCK, LANES), MXU_DTYPE), pltpu.VMEM((5 * BLOCK, LANES), MXU_DTYPE),
                        pltpu.VMEM((5 * BLOCK, LANES), F32), pltpu.VMEM((5 * BLOCK, LANES), F32),
                        pltpu.VMEM((BLOCK, LANES), F32), pltpu.VMEM((BLOCK, LANES), F32)],
        compiler_params=_cparams(("arbitrary", "arbitrary")),
    )(qa, ka, ka, va, va, sink_b, o32, do, lse)


MLA_T = 512
MLA_FWD_GROUP = 4
MLA_BWD_GROUP = 2


def _mla_specs(s_, t, group):
    w = group * LANES
    qs = pl.BlockSpec((t, w), lambda g, i: (i, g))
    kv = pl.BlockSpec((s_, w), lambda g, i: (0, g))
    row = pl.BlockSpec((group, 1, t), lambda g, i: (g, 0, i))
    return qs, kv, row


def _causal_scores_t(k, q, t, c2, masked):
    st = lax.dot_general(k, q, NT, preferred_element_type=F32) * c2
    if masked:
        kr = lax.broadcasted_iota(jnp.int32, (t, t), 0)
        qc = lax.broadcasted_iota(jnp.int32, (t, t), 1)
        st = jnp.where(kr <= qc, st, -jnp.inf)
    return st


def _mla_fwd(qc, kc, vp):
    s_ = qc.shape[0]
    t = min(MLA_T, s_)
    c2 = MLA_QK ** -0.5 * LOG2E
    grp = MLA_FWD_GROUP

    def body(q_ref, k_ref, v_ref, o32_ref, o16_ref, lse_ref, m_s, acc_s):
        qi = pl.program_id(1)
        m_s[...] = jnp.full(m_s.shape, -jnp.inf, F32)
        acc_s[...] = jnp.zeros(acc_s.shape, F32)
        ones_lane = lax.broadcasted_iota(jnp.int32, (t, LANES), 1) == MLA_V

        def step(ki, masked):
            off = pl.multiple_of(ki * t, t)
            for g in range(grp):
                cs = slice(g * LANES, (g + 1) * LANES)
                st = _causal_scores_t(k_ref[pl.ds(off, t), cs], q_ref[:, cs], t, c2, masked)
                m_old = m_s[g]
                m_new = jnp.maximum(m_old, jnp.max(st, axis=0, keepdims=True))
                alpha = jnp.exp2(m_old - m_new)
                pt = jnp.exp2(st - m_new).astype(MXU_DTYPE)
                v = v_ref[pl.ds(off, t), cs]
                v = jnp.where(ones_lane, jnp.ones((), v.dtype), v)
                acc_s[g] = alpha * acc_s[g] + lax.dot_general(v, pt, TN, preferred_element_type=F32)
                m_s[g] = m_new

        def full_block(ki, carry):
            step(ki, False)
            return carry

        lax.fori_loop(0, qi, full_block, 0)
        step(qi, True)
        for g in range(grp):
            cs = slice(g * LANES, (g + 1) * LANES)
            acc = acc_s[g]
            l = acc[MLA_V:MLA_V + 1, :]
            o = (acc * (1.0 / l)).T
            o32_ref[:, cs] = o
            o16_ref[:, cs] = o.astype(o16_ref.dtype)
            lse_ref[g] = m_s[g] + jnp.log2(l)

    qs, kv, row = _mla_specs(s_, t, grp)
    return pl.pallas_call(
        body, name="mla_fwd", grid=(MLA_HEADS // grp, s_ // t), in_specs=[qs, kv, kv], out_specs=[qs, qs, row],
        out_shape=[jax.ShapeDtypeStruct((s_, MLA_HEADS * LANES), F32), jax.ShapeDtypeStruct((s_, MLA_HEADS * LANES), MXU_DTYPE),
                   jax.ShapeDtypeStruct((MLA_HEADS, 1, s_), F32)],
        scratch_shapes=[pltpu.VMEM((grp, 1, t), F32), pltpu.VMEM((grp, LANES, t), F32)],
        compiler_params=_cparams(("parallel", "arbitrary")),
    )(qc, kc, vp)


def _mla_bwd(qc, kc, vp, dob, lse, delta):
    s_ = qc.shape[0]
    t = min(MLA_T, s_)
    n = s_ // t
    scale = MLA_QK ** -0.5
    c2 = scale * LOG2E
    grp = MLA_BWD_GROUP
    tt = (((0,), (1,)), ((), ()))

    def body(q_ref, do_ref, lse_ref, dl_ref, k_ref, v_ref, dq_ref, dk_ref, dv_ref, dqt_s):
        qi = pl.program_id(1)

        @pl.when(qi == 0)
        def _():
            dk_ref[...] = jnp.zeros(dk_ref.shape, F32)
            dv_ref[...] = jnp.zeros(dv_ref.shape, F32)

        dqt_s[...] = jnp.zeros(dqt_s.shape, F32)

        def step(ki, masked):
            off = pl.multiple_of(ki * t, t)
            for g in range(grp):
                cs = slice(g * LANES, (g + 1) * LANES)
                q, d, k = q_ref[:, cs], do_ref[:, cs], k_ref[pl.ds(off, t), cs]
                pt = jnp.exp2(_causal_scores_t(k, q, t, c2, masked) - lse_ref[g])
                dpt = lax.dot_general(v_ref[pl.ds(off, t), cs], d, NT, preferred_element_type=F32)
                dst = (pt * (dpt - dl_ref[g]) * scale).astype(MXU_DTYPE)
                dv_ref[g, ki] += lax.dot_general(d, pt.astype(MXU_DTYPE), tt, preferred_element_type=F32)
                dk_ref[g, ki] += lax.dot_general(q, dst, tt, preferred_element_type=F32)
                dqt_s[g] += lax.dot_general(k, dst, TN, preferred_element_type=F32)

        def full_block(ki, carry):
            step(ki, False)
            return carry

        lax.fori_loop(0, qi, full_block, 0)
        step(qi, True)
        for g in range(grp):
            dq_ref[:, g * LANES:(g + 1) * LANES] = dqt_s[g].T

    qs, kv, row = _mla_specs(s_, t, grp)
    acc_spec = pl.BlockSpec((grp, n, LANES, t), lambda g, i: (g, 0, 0, 0))
    acc_shape = jax.ShapeDtypeStruct((MLA_HEADS, n, LANES, t), F32)
    return pl.pallas_call(
        body, name="mla_bwd", grid=(MLA_HEADS // grp, n), in_specs=[qs, qs, row, row, kv, kv],
        out_specs=[qs, acc_spec, acc_spec], out_shape=[jax.ShapeDtypeStruct((s_, MLA_HEADS * LANES), F32), acc_shape, acc_shape],
        scratch_shapes=[pltpu.VMEM((grp, LANES, t), F32)], compiler_params=_cparams(("parallel", "arbitrary")),
    )(qc, dob, lse, delta, kc, vp)


def _pad_heads(w, nh, hd, axis):
    shp = w.shape
    w = w.reshape(shp[:axis] + (nh, hd) + shp[axis + 1:])
    pad = [(0, 0)] * w.ndim
    pad[axis + 1] = (0, LANES - hd)
    w = jnp.pad(w, pad)
    return w.reshape(shp[:axis] + (nh * LANES,) + shp[axis + 1:])


def _unpad_heads(w, nh, hd, axis):
    shp = w.shape
    w = w.reshape(shp[:axis] + (nh, LANES) + shp[axis + 1:])
    w = lax.slice_in_dim(w, 0, hd, axis=axis + 1)
    return w.reshape(shp[:axis] + (nh * hd,) + shp[axis + 1:])


PACK_W = 1024
ROW_TILE = 16
FULL_SHAPE = dict(w_in=(1024, 3488), w_uq=(384, 768), w_ukv=(256, 1024), w_o_swa=(512, 1024), w_o_mla=(512, 1024),
                  w_out=(1024, 1024), w_gate=(1024, 2816), w_up=(1024, 2816), w_down=(2816, 1024))
BIG = tuple(FULL_SHAPE)
ROW_SHARDED = ("w_out", "w_down")
W_IN_COLS = FULL_SHAPE["w_in"][1] // N_DEV
W_IN_ROWS = -(-W_IN_COLS // ROW_TILE) * ROW_TILE
FF_COLS = D_FF // N_DEV
OUT_ROWS = D_MODEL // N_DEV
SMALL_ROW0 = W_IN_ROWS + OUT_ROWS
SMALL_FLAT = (("w_uq", 0, 36), ("w_ukv", 48, 32), ("w_o_swa", 80, 64), ("w_o_mla", 144, 64))
SMALL_ROWS = 208
EARLY_ROWS = SMALL_ROW0 + SMALL_ROWS
LATE_ROWS = 3 * FF_COLS
PACK_ROWS = EARLY_ROWS + LATE_ROWS


def _shard_shape(n):
    r, c = FULL_SHAPE[n]
    return (r // N_DEV, c) if n in ROW_SHARDED else (r, c // N_DEV)


def _wire_pack(sh, dtype):
    c = lambda n: sh[n].astype(dtype)
    rows = [jnp.pad(c("w_in").T, ((0, W_IN_ROWS - W_IN_COLS), (0, 0))), c("w_out")]
    for n, _, r in SMALL_FLAT:
        rows.append(jnp.pad(c(n).reshape(r, PACK_W), ((0, -r % ROW_TILE), (0, 0))))
    return jnp.concatenate(rows + [c("w_gate").T, c("w_up").T, c("w_down")], 0)


MID_ROWS = OUT_ROWS + SMALL_ROWS


def _mid_unpack(p):
    out = dict(w_out=p[0:OUT_ROWS])
    for n, off, r in SMALL_FLAT:
        out[n] = p[OUT_ROWS + off:OUT_ROWS + off + r].reshape(_shard_shape(n))
    return out


def _w_in_row_maps():
    sp = lambda col: (col // W_IN_COLS) * W_IN_ROWS + col % W_IN_COLS
    fwd = np.full((P_W,), -1, np.int64)

    def put(t0, c0, n):
        fwd[t0:t0 + n] = [sp(c) for c in range(c0, c0 + n)]

    put(P_GA, IN_OFF[6], D_MODEL)
    put(P_GB, IN_OFF[7], D_MODEL)
    for h in range(SWA_HEADS):
        put(P_Q + LANES * h, IN_OFF[0] + HEAD_DIM * h, HEAD_DIM)
    put(P_QLAT, IN_OFF[3], Q_LORA)
    put(P_KR + KR_LANE, IN_OFF[5], MLA_ROPE)
    for h in range(SWA_KV_HEADS):
        put(P_K + LANES * h, IN_OFF[1] + HEAD_DIM * h, HEAD_DIM)
        put(P_V + LANES * h, IN_OFF[2] + HEAD_DIM * h, HEAD_DIM)
    put(P_KVLAT, IN_OFF[4], KV_LORA)
    inv = np.full((N_DEV * W_IN_ROWS,), -1, np.int64)
    inv[fwd[fwd >= 0]] = np.nonzero(fwd >= 0)[0]
    return fwd, inv


def _take_rows(src, idx, *, name):
    n_out, n_src, width = len(idx), src.shape[0], src.shape[1]
    assert n_out % BLOCK == 0 and n_src % BLOCK == 0
    n_tiles = n_out // BLOCK
    blocks = [sorted({int(v) // BLOCK for v in idx[i * BLOCK:(i + 1) * BLOCK] if v >= 0}) for i in range(n_tiles)]
    k_max = max(1, max(len(b) for b in blocks))
    tab = np.zeros((n_tiles, k_max), np.int32)
    sel = np.zeros((n_tiles, k_max, BLOCK, BLOCK), np.float32)
    for i, blks in enumerate(blocks):
        for m, b in enumerate(blks):
            tab[i, m] = b
            for r in range(BLOCK):
                v = int(idx[i * BLOCK + r])
                if v >= 0 and v // BLOCK == b:
                    sel[i, m, r, v % BLOCK] = 1.0

    def body(tab_ref, sel_ref, *refs):
        o_ref = refs[k_max]
        acc = jnp.dot(sel_ref[0, 0], refs[0][...], preferred_element_type=F32)
        for m in range(1, k_max):
            acc = acc + jnp.dot(sel_ref[0, m], refs[m][...], preferred_element_type=F32)
        o_ref[...] = acc.astype(o_ref.dtype)

    def src_spec(m):
        return pl.BlockSpec((BLOCK, width), lambda i, t: (t[i * k_max + m], 0))

    return pl.pallas_call(
        body, name=name,
        grid_spec=pltpu.PrefetchScalarGridSpec(
            num_scalar_prefetch=1, grid=(n_tiles,),
            in_specs=[pl.BlockSpec((1, k_max, BLOCK, BLOCK), lambda i, t: (i, 0, 0, 0))] + [src_spec(m) for m in range(k_max)],
            out_specs=pl.BlockSpec((BLOCK, width), lambda i, t: (i, 0))),
        out_shape=jax.ShapeDtypeStruct((n_out, width), src.dtype),
        compiler_params=_cparams(("parallel",)),
    )(jnp.asarray(tab.reshape(-1)), jnp.asarray(sel, src.dtype), *([src] * k_max))


def _w_in_operand(win_g):
    return _take_rows(win_g.reshape(N_DEV * W_IN_ROWS, PACK_W), _w_in_row_maps()[0], name="w_in_rows")


def _mid_operands(wout_g, small_g):
    def full(n, off, r):
        a = small_g[:, off:off + r].reshape((N_DEV,) + _shard_shape(n))
        return jnp.moveaxis(a, 0, 1).reshape(FULL_SHAPE[n])

    w = {n: full(n, off, r) for n, off, r in SMALL_FLAT}
    ukv = w["w_ukv"].reshape(KV_LORA, MLA_HEADS, MLA_NOPE + MLA_V)
    return dict(
        wout=wout_g.reshape(D_MODEL, D_MODEL),
        wuq=_pad_heads(w["w_uq"], MLA_HEADS, MLA_QK, 1),
        wuk=_pad_heads(ukv[:, :, :MLA_NOPE].reshape(KV_LORA, -1), MLA_HEADS, MLA_NOPE, 1),
        wuv=_pad_heads(ukv[:, :, MLA_NOPE:].reshape(KV_LORA, -1), MLA_HEADS, MLA_V, 1),
        woa=_pad_heads(w["w_o_swa"], SWA_HEADS, HEAD_DIM, 0),
        wob=_pad_heads(w["w_o_mla"], MLA_HEADS, MLA_V, 0),
    )


def _mid_grad_pack(g):
    uk = _unpad_heads(g["wukv"][:, :1024], MLA_HEADS, MLA_NOPE, 1).reshape(KV_LORA, MLA_HEADS, MLA_NOPE)
    uv = _unpad_heads(g["wukv"][:, 1024:], MLA_HEADS, MLA_V, 1).reshape(KV_LORA, MLA_HEADS, MLA_V)
    w = dict(w_uq=_unpad_heads(g["wuq"], MLA_HEADS, MLA_QK, 1), w_ukv=jnp.concatenate([uk, uv], 2).reshape(KV_LORA, -1),
             w_o_swa=_unpad_heads(g["woa"], SWA_HEADS, HEAD_DIM, 0), w_o_mla=_unpad_heads(g["wob"], MLA_HEADS, MLA_V, 0))

    def flat(n, r):
        rr, cc = FULL_SHAPE[n]
        a = jnp.moveaxis(w[n].reshape(rr, N_DEV, cc // N_DEV), 1, 0).reshape(N_DEV, r, PACK_W)
        return jnp.pad(a, ((0, 0), (0, -r % ROW_TILE), (0, 0))).astype(WIRE_DTYPE)

    return jnp.concatenate([g["wout"].reshape(N_DEV, OUT_ROWS, PACK_W)] + [flat(n, r) for n, _, r in SMALL_FLAT], 1)


def _w_in_grad_chunks(g_win_t):
    return _take_rows(g_win_t, _w_in_row_maps()[1], name="dw_in_rows").reshape(N_DEV, W_IN_ROWS, PACK_W)


def _local_step(x, tgt, win_t, small, weights, grads):
    s_ = x.shape[0]
    tabs = _rope_tables(s_)
    sink_b = jnp.broadcast_to(small["swa_sinks"].reshape(SWA_KV_HEADS, SWA_GROUP, 1), (SWA_KV_HEADS, SWA_GROUP, LANES))
    sink_b = jnp.pad(sink_b, ((0, 0), (0, SUBLANES - SWA_GROUP), (0, 0)))

    h = _norm_fwd(x, small["mix_norm_g"], name="norm1")
    p = _mm(h, win_t, "nt", name="proj_in", tm=1024, tn=2176)
    qa, ka, va, cq, ckv, kro = _attn_prep(p, small["q_norm_g"], small["kv_norm_g"], tabs)
    ops = weights.mid(cq)
    oa32, oa16, lse_a = _swa_fwd(qa, ka, va, sink_b)
    qp = _mm(cq, ops["wuq"], "nn", name="mla_q_up", tm=1024, tn=1024)
    kp = _mm(ckv, ops["wuk"], "nn", name="mla_k_up", tm=1024, tn=1024)
    vp = _mm(ckv, ops["wuv"], "nn", name="mla_v_up", tm=1024, tn=1024, out_dtype=MXU_DTYPE)
    qc, kc = _mla_prep(qp, kp, kro, tabs)
    ob32, ob16, lse_b = _mla_fwd(qc, kc, vp)
    ta, tb, y = _attn_out_gate(oa16, ob16, ops["woa"], ops["wob"], p)
    x1 = _mm(y, ops["wout"], "nn", name="out_proj", add=x, tm=1024, tn=1024)
    wgu_t, wd = weights.late(x1)
    h2 = _norm_fwd(x1, small["ffn_norm_g"], name="norm2")
    gu, act = _ffn_in_act(h2, wgu_t)
    x2 = _mm(act, wd, "nn", name="ffn_out", add=x1, tn=1024)

    dx2, dx2b, dg3, _, tot = _loss_bwd(x2, small["final_norm_g"].reshape(1, D_MODEL), tgt)
    g = {}
    g_wd = _mm(act, dx2b, "tn", name="dw_down", tm=1408, tn=1024, tk=1024, out_dtype=WIRE_DTYPE)
    dgu = _d_act_swiglu(dx2b, wd, gu)
    dh2 = _mm(dgu, wgu_t, "nn", name="d_h2", tn=1024, tk=2816)
    g_wgu = _mm(dgu, h2, "tn", name="dw_ffn_in", tm=1408, tn=1024, tk=1024, out_dtype=WIRE_DTYPE)
    token = grads.late(g_wgu, g_wd)
    dx1, dx1b, dg2 = _norm_bwd(x1, small["ffn_norm_g"] + token[0:1, 0:1], dh2, dx2, name="norm2_bwd")
    g["wout"] = _mm(y, dx1b, "tn", name="dw_out", tm=1024, tn=1024, tk=1024, out_dtype=WIRE_DTYPE)
    dta, dtb, dgab = _d_y_gate(dx1b, ops["wout"], p, ta, tb)
    doa = _mm(dta, ops["woa"], "nt", name="d_oa", tm=1024, tn=1024)
    g["woa"] = _mm(oa16, dta, "tn", name="dw_o_swa", tm=1024, tn=1024, tk=1024)
    dob = _mm(dtb, ops["wob"], "nt", name="d_ob", tm=1024, tn=1024)
    g["wob"] = _mm(ob16, dtb, "tn", name="dw_o_mla", tm=1024, tn=1024, tk=1024)
    dob16, delta_b = _mla_bwd_prep(dob, ob32)
    dqc, dkc, dvp = _mla_bwd(qc, kc, vp, dob16, lse_b, delta_b)
    dqp, dkv, dkr = _mla_unprep(dqc, dkc, dvp, tabs)
    dcq = _mm(dqp, ops["wuq"], "nt", name="d_cq", tn=Q_LORA)
    g["wuq"] = _mm(cq, dqp, "tn", name="dw_uq", tm=Q_LORA, tn=1024, tk=512)
    dckv = _mm(dkv, jnp.concatenate([ops["wuk"], ops["wuv"]], 1), "nt", name="d_ckv", tn=KV_LORA)
    g["wukv"] = _mm(ckv, dkv, "tn", name="dw_ukv", tm=KV_LORA, tn=1024, tk=512)
    token = grads.mid(g)
    _, dqlat, dgq = _norm_bwd(p, small["q_norm_g"] + token[0:1, 0:1], dcq, None, name="qnorm_bwd", x_cb=P_QLAT // Q_LORA)
    _, dkvlat, dgkv = _norm_bwd(p, small["kv_norm_g"], dckv, None, name="kvnorm_bwd", x_cb=P_KVLAT // KV_LORA)
    dqa, dka, dva, dsk = _swa_bwd(qa, ka, va, sink_b, oa32, doa, lse_a)
    dq_raw, dk_raw = _swa_unrope(dqa, dka, tabs)
    dp = jnp.concatenate([dgab, dq_raw, dqlat, dkr, dk_raw, dva, dkvlat], 1)
    token = grads.last(_mm(dp, h, "tn", name="dw_in", tm=2176, tn=1024, tk=1024, out_dtype=WIRE_DTYPE))
    dh = _mm(dp, win_t, "nn", name="d_h", after=token, tm=1024, tn=1024, tk=2176)
    gx, _, dg1 = _norm_bwd(x, small["mix_norm_g"], dh, dx1, name="norm1_bwd")

    sm = dict(mix_norm_g=dg1, ffn_norm_g=dg2, final_norm_g=dg3, q_norm_g=dgq, kv_norm_g=dgkv,
              swa_sinks=dsk[:, :SWA_GROUP, 0].reshape(1, SWA_HEADS))
    return tot, gx, sm


MESH = pl.DeviceIdType.MESH
ANY = pl.BlockSpec(memory_space=pl.ANY)


def _position():
    return lax.axis_index("x"), lax.axis_index("y"), lax.axis_index("c")


def _all_gather(block, pieces, shapes, *, name):
    n_out = len(shapes)
    n_rows = sum(p[3] for p in pieces)

    def body(x_ref, *refs):
        outs, (send_sems, recv_sems, local_sem) = refs[:n_out], refs[n_out:]
        x, y, c = _position()
        me, sibling = (x, y, c), (x, y, 1 - c)
        chips = [(1 - x, y), (x, 1 - y), (1 - x, 1 - y)]

        def dst(piece, blk):
            arr, lead, _, _ = piece
            return outs[arr].at[lead(4 * blk[0] + 2 * blk[1] + blk[2])]

        def own(piece):
            return x_ref.at[pl.ds(piece[2], piece[3])]

        def copies(k, blk, to, from_input):
            return [pltpu.make_async_remote_copy(
                src_ref=own(p) if from_input else dst(p, blk), dst_ref=dst(p, blk), send_sem=send_sems.at[k],
                recv_sem=recv_sems.at[k], device_id=to, device_id_type=MESH) for p in pieces]

        gathered_rows = x_ref.at[pl.ds(0, n_rows)]

        def whole_block(k):
            return pltpu.make_async_remote_copy(src_ref=gathered_rows, dst_ref=gathered_rows, send_sem=send_sems.at[k],
                                                recv_sem=recv_sems.at[k], device_id=me, device_id_type=MESH)

        for p in pieces:
            pltpu.make_async_copy(own(p), dst(p, me), local_sem).start()
        for cp in copies(0, me, sibling, True):
            cp.start()
        for j, chip in enumerate(chips):
            for cp in copies(1 + j, me, (*chip, c), True):
                cp.start()
        for j, chip in enumerate(chips):
            whole_block(1 + j).wait_recv()
            for cp in copies(4 + j, (*chip, c), sibling, False):
                cp.start()
        whole_block(0).wait_recv()
        for j in range(3):
            whole_block(4 + j).wait_recv()
        for k in range(7):
            whole_block(k).wait_send()
        pltpu.make_async_copy(gathered_rows, gathered_rows, local_sem).wait()

    return pl.pallas_call(
        body, name=name, out_shape=[jax.ShapeDtypeStruct(s, block.dtype) for s in shapes], in_specs=[ANY],
        out_specs=[ANY] * n_out,
        scratch_shapes=[pltpu.SemaphoreType.DMA((7,)), pltpu.SemaphoreType.DMA((7,)), pltpu.SemaphoreType.DMA],
    )(block)


HBM = pl.BlockSpec(memory_space=pltpu.HBM)
SEM = pl.BlockSpec(memory_space=pltpu.SEMAPHORE)
TILE_DEVS = FF_TILE // FF_COLS
GU_SHAPE = (2, 2, TILE_DEVS, FF_COLS, PACK_W)


def _gate_slab(d):
    return (d // TILE_DEVS, 0, d % TILE_DEVS)


def _up_slab(d):
    return (d // TILE_DEVS, 1, d % TILE_DEVS)
D_SHAPE = (N_DEV, FF_COLS, PACK_W)
LAND_SHAPE = (N_DEV, LATE_ROWS, PACK_W)


def _split_params():
    return pltpu.CompilerParams(has_side_effects=pltpu.SideEffectType.DATAFLOW_SIDE_EFFECTING)


def _peer(x, y, c, k):
    return ((1 - x) if k & 4 else x, (1 - y) if k & 2 else y, (1 - c) if k & 1 else c)


def _empty_hbm(shape, dtype):
    return pltpu.with_memory_space_constraint(lax.empty(shape, dtype), pltpu.HBM)


def _wait_all(rows, send_sems, recv_sems, me):
    for k in range(N_DEV - 1):
        cp = pltpu.make_async_remote_copy(src_ref=rows, dst_ref=rows, send_sem=send_sems.at[k], recv_sem=recv_sems.at[k],
                                          device_id=me, device_id_type=MESH)
        cp.wait_send()
        cp.wait_recv()


def _token_shape():
    return jax.ShapeDtypeStruct((SUBLANES, LANES), F32)


def _gather_start(pack, row0, pieces, shapes, *, name):
    n = len(shapes)

    def body(*refs):
        p_ref, bufs, send_sems, recv_sems, token = refs[0], refs[1:1 + n], refs[1 + n], refs[2 + n], refs[-1]
        x, y, c = _position()
        me = 4 * x + 2 * y + c
        for k in range(1, N_DEV):
            off = row0
            for buf, lead, rows in pieces:
                pltpu.make_async_remote_copy(
                    src_ref=p_ref.at[pl.ds(off, rows)], dst_ref=bufs[buf].at[lead(me)], send_sem=send_sems.at[k - 1],
                    recv_sem=recv_sems.at[k - 1], device_id=_peer(x, y, c, k), device_id_type=MESH).start()
                off += rows
        token[...] = jnp.zeros_like(token)

    sems, dt = pltpu.SemaphoreType.DMA((N_DEV - 1,)), pack.dtype
    return pl.pallas_call(
        body, name=name,
        out_shape=(sems, sems, pltpu.HBM(pack.shape, dt)) + tuple(pltpu.HBM(s, dt) for s in shapes) + (_token_shape(),),
        in_specs=(HBM,) * (1 + n), out_specs=(SEM, SEM) + (HBM,) * (1 + n) + (pl.BlockSpec(memory_space=pltpu.VMEM),),
        input_output_aliases={i: 2 + i for i in range(1 + n)}, compiler_params=_split_params(),
    )(pltpu.with_memory_space_constraint(pack, pltpu.HBM), *[_empty_hbm(s, dt) for s in shapes])


def _gather_wait(started, row0, n_rows, after, *, name):
    send_sems, recv_sems, pack, *bufs = started[:-1]
    n = len(bufs)

    def body(*refs):
        _wait_all(refs[0].at[pl.ds(row0, n_rows)], refs[1 + n], refs[2 + n], _position())

    outs = pl.pallas_call(
        body, name=name, out_shape=tuple(pltpu.HBM(a.shape, a.dtype) for a in (pack, *bufs)),
        in_specs=(HBM,) * (1 + n) + (SEM, SEM, ANY), out_specs=(HBM,) * (1 + n),
        input_output_aliases={i: i for i in range(1 + n)}, compiler_params=_split_params(),
    )(pack, *bufs, send_sems, recv_sems, after)
    return outs[0], outs[1:]


def _scatter_start(srcs, pieces, *, name):
    n = len(srcs)
    land_shape = (N_DEV, sum(p[2] for p in pieces), PACK_W)

    def body(*refs):
        src_refs, land_ref, send_sems, recv_sems, token = refs[:n], refs[n], refs[n + 1], refs[n + 2], refs[-1]
        x, y, c = _position()
        me = 4 * x + 2 * y + c
        for k in range(1, N_DEV):
            px, py, pc = _peer(x, y, c, k)
            off = 0
            for si, lead, rows in pieces:
                pltpu.make_async_remote_copy(
                    src_ref=src_refs[si].at[lead(4 * px + 2 * py + pc)], dst_ref=land_ref.at[me, pl.ds(off, rows)],
                    send_sem=send_sems.at[k - 1], recv_sem=recv_sems.at[k - 1], device_id=(px, py, pc),
                    device_id_type=MESH).start()
                off += rows
        token[...] = jnp.zeros_like(token)

    sems, dt = pltpu.SemaphoreType.DMA((N_DEV - 1,)), srcs[0].dtype
    return pl.pallas_call(
        body, name=name,
        out_shape=(sems, sems) + tuple(pltpu.HBM(a.shape, dt) for a in srcs) + (pltpu.HBM(land_shape, dt), _token_shape()),
        in_specs=(HBM,) * (n + 1), out_specs=(SEM, SEM) + (HBM,) * (n + 1) + (pl.BlockSpec(memory_space=pltpu.VMEM),),
        input_output_aliases={i: 2 + i for i in range(n + 1)}, compiler_params=_split_params(),
    )(*[pltpu.with_memory_space_constraint(a, pltpu.HBM) for a in srcs], _empty_hbm(land_shape, dt))


def _scatter_wait(started, after, *, name):
    send_sems, recv_sems, *bufs = started[:-1]
    n = len(bufs)

    def body(*refs):
        _wait_all(refs[n - 1].at[0], refs[n], refs[n + 1], _position())

    return pl.pallas_call(
        body, name=name, out_shape=tuple(pltpu.HBM(a.shape, a.dtype) for a in bufs),
        in_specs=(HBM,) * n + (SEM, SEM, ANY), out_specs=(HBM,) * n, input_output_aliases={i: i for i in range(n)},
        compiler_params=_split_params(),
    )(*bufs, send_sems, recv_sems, after)


def _peer_sum(own, own_lead, land, block, rows, idx, *, name):
    lead_rank = own.ndim - 2

    def body(idx_ref, own_ref, *refs):
        o_ref = refs[N_DEV - 1]
        acc = own_ref[(0,) * lead_rank].astype(F32)
        for k in range(N_DEV - 1):
            acc = acc + refs[k][0].astype(F32)
        o_ref[...] = acc

    own_spec = pl.BlockSpec((1,) * lead_rank + (rows, PACK_W), lambda i, t: own_lead(t[0]) + (0, 0))

    def land_spec(k):
        return pl.BlockSpec((1, rows, PACK_W), lambda i, t: (t[k + 1], block, 0))

    return pl.pallas_call(
        body, name=name,
        grid_spec=pltpu.PrefetchScalarGridSpec(
            num_scalar_prefetch=1, grid=(1,), in_specs=[own_spec] + [land_spec(k) for k in range(N_DEV - 1)],
            out_specs=pl.BlockSpec((rows, PACK_W), lambda i, t: (0, 0))),
        out_shape=jax.ShapeDtypeStruct((rows, PACK_W), F32), compiler_params=_cparams(("arbitrary",)),
    )(idx, own, *([land] * (N_DEV - 1)))


def _adamw(w, g, m, v):
    m = ADAM_B1 * m + (1.0 - ADAM_B1) * g
    v = ADAM_B2 * v + (1.0 - ADAM_B2) * (g * g)
    m_hat = m / (1.0 - ADAM_B1 ** ADAM_STEP)
    v_hat = v / (1.0 - ADAM_B2 ** ADAM_STEP)
    delta = -ADAM_LR * (m_hat / (jnp.sqrt(v_hat) + ADAM_EPS) + ADAM_WD * w)
    return delta, m, v


def _adamw_call(w, g, m, v, *, name, max_rows=256):
    r, c_ = w.shape
    tr = max_rows if r > max_rows and r % max_rows == 0 else r

    def body(w_ref, g_ref, m_ref, v_ref, d_ref, mo_ref, vo_ref):
        d, mn, vn = _adamw(w_ref[...], g_ref[...], m_ref[...], v_ref[...])
        d_ref[...] = d
        mo_ref[...] = mn
        vo_ref[...] = vn

    row = pl.BlockSpec((tr, c_), lambda i: (i, 0))
    shp = jax.ShapeDtypeStruct((r, c_), F32)
    return pl.pallas_call(
        body, name=name, grid=(r // tr,), in_specs=[row] * 4, out_specs=[row] * 3, out_shape=[shp] * 3,
        compiler_params=_cparams(("parallel",)),
    )(w, g, m, v)


SMALL = ("mix_norm_g", "ffn_norm_g", "final_norm_g", "q_norm_g", "kv_norm_g", "swa_sinks")
SMALL_W = dict(mix_norm_g=1024, ffn_norm_g=1024, final_norm_g=1024, q_norm_g=Q_LORA, kv_norm_g=KV_LORA, swa_sinks=SWA_HEADS)


def _small_adamw(parts, w, m, v):
    n_par = parts.shape[1] // SUBLANES

    def body(p_ref, w_ref, m_ref, v_ref, g_ref, d_ref, mo_ref, vo_ref):
        tot = p_ref[0]
        for dev in range(1, N_DEV):
            tot = tot + p_ref[dev]
        row_id = lax.broadcasted_iota(jnp.int32, (SUBLANES, PACK_W), 0)
        g = jnp.zeros((SUBLANES, PACK_W), F32)
        for k in range(n_par):
            g = jnp.where(row_id == k, jnp.sum(tot[k * SUBLANES:(k + 1) * SUBLANES, :], axis=0, keepdims=True), g)
        d, mn, vn = _adamw(w_ref[...], g, m_ref[...], v_ref[...])
        g_ref[...] = g
        d_ref[...] = d
        mo_ref[...] = mn
        vo_ref[...] = vn

    shp = jax.ShapeDtypeStruct((SUBLANES, PACK_W), F32)
    vm = pl.BlockSpec(memory_space=pltpu.VMEM)
    return pl.pallas_call(body, name="small_adamw", in_specs=[vm] * 4, out_specs=[vm] * 4, out_shape=[shp] * 4)(parts, w, m, v)


def _small_pack(d, rows_each):
    parts = [jnp.pad(d[n].astype(F32), ((0, 0), (0, PACK_W - SMALL_W[n]))) for n in SMALL]
    out = jnp.concatenate(parts, 0)
    pad = -out.shape[0] % SUBLANES
    return jnp.pad(out, ((0, pad), (0, 0)))


def kernel(x, mix_norm_g, w_in, swa_sinks, q_norm_g, w_uq, kv_norm_g, w_ukv, w_o_swa, w_o_mla, w_out, ffn_norm_g, w_gate, w_up, w_down, final_norm_g, loss_target, m_mix_norm_g, m_w_in, m_swa_sinks, m_q_norm_g, m_w_uq, m_kv_norm_g, m_w_ukv, m_w_o_swa, m_w_o_mla, m_w_out, m_ffn_norm_g, m_w_gate, m_w_up, m_w_down, m_final_norm_g, v_mix_norm_g, v_w_in, v_swa_sinks, v_q_norm_g, v_w_uq, v_kv_norm_g, v_w_ukv, v_w_o_swa, v_w_o_mla, v_w_out, v_ffn_norm_g, v_w_gate, v_w_up, v_w_down, v_final_norm_g):
    big_w = dict(w_in=w_in[0], w_uq=w_uq[0], w_ukv=w_ukv[0], w_o_swa=w_o_swa[0], w_o_mla=w_o_mla[0], w_out=w_out[0],
                 w_gate=w_gate[0], w_up=w_up[0], w_down=w_down[0])
    big_m = dict(w_in=m_w_in[0], w_uq=m_w_uq[0], w_ukv=m_w_ukv[0], w_o_swa=m_w_o_swa[0], w_o_mla=m_w_o_mla[0],
                 w_out=m_w_out[0], w_gate=m_w_gate[0], w_up=m_w_up[0], w_down=m_w_down[0])
    big_v = dict(w_in=v_w_in[0], w_uq=v_w_uq[0], w_ukv=v_w_ukv[0], w_o_swa=v_w_o_swa[0], w_o_mla=v_w_o_mla[0],
                 w_out=v_w_out[0], w_gate=v_w_gate[0], w_up=v_w_up[0], w_down=v_w_down[0])
    small_w = dict(mix_norm_g=mix_norm_g, ffn_norm_g=ffn_norm_g, final_norm_g=final_norm_g.reshape(1, D_MODEL),
                   q_norm_g=q_norm_g, kv_norm_g=kv_norm_g, swa_sinks=swa_sinks)
    small_m = dict(mix_norm_g=m_mix_norm_g, ffn_norm_g=m_ffn_norm_g, final_norm_g=m_final_norm_g.reshape(1, D_MODEL),
                   q_norm_g=m_q_norm_g, kv_norm_g=m_kv_norm_g, swa_sinks=m_swa_sinks)
    small_v = dict(mix_norm_g=v_mix_norm_g, ffn_norm_g=v_ffn_norm_g, final_norm_g=v_final_norm_g.reshape(1, D_MODEL),
                   q_norm_g=v_q_norm_g, kv_norm_g=v_kv_norm_g, swa_sinks=v_swa_sinks)

    px, py, pc = _position()
    me = 4 * px + 2 * py + pc
    idx = jnp.stack([me] + [4 * qx + 2 * qy + qc for qx, qy, qc in (_peer(px, py, pc, k) for k in range(1, N_DEV))])
    idx = idx.astype(jnp.int32)

    dev = lambda d: (d,)
    pack = _wire_pack(big_w, WIRE_DTYPE)
    win_g, = _all_gather(pack, ((0, dev, 0, W_IN_ROWS),), ((N_DEV, W_IN_ROWS, PACK_W),), name="ag_early")
    ag_mid = _gather_start(pack, W_IN_ROWS, ((0, dev, OUT_ROWS), (1, dev, SMALL_ROWS)),
                           ((N_DEV, OUT_ROWS, PACK_W), (N_DEV, SMALL_ROWS, PACK_W)), name="ag_mid_start")
    ag = {}

    def own_rows(r0, r1, shape):
        return pack[r0:r1].reshape(shape)

    def mid_weights(after):
        pack_mid, (wout_g, small_g) = _gather_wait(ag_mid, W_IN_ROWS, MID_ROWS, after, name="ag_mid_wait")
        ag["late"] = _gather_start(pack_mid, EARLY_ROWS, ((0, _gate_slab, FF_COLS), (0, _up_slab, FF_COLS), (1, dev, FF_COLS)),
                                   (GU_SHAPE, D_SHAPE), name="ag_late_start")
        wout_g = lax.dynamic_update_slice(wout_g, own_rows(W_IN_ROWS, SMALL_ROW0, (1, OUT_ROWS, PACK_W)), (me, 0, 0))
        small_g = lax.dynamic_update_slice(small_g, own_rows(SMALL_ROW0, EARLY_ROWS, (1, SMALL_ROWS, PACK_W)), (me, 0, 0))
        ops = _mid_operands(wout_g, small_g)
        ops["wuq"] = ops["wuq"] + ag["late"][-1][0:1, 0:1].astype(ops["wuq"].dtype)
        return ops

    def late_weights(after):
        _, (gu, d) = _gather_wait(ag["late"], EARLY_ROWS, LATE_ROWS, after, name="ag_late_wait")
        slab = (1, 1, 1, FF_COLS, PACK_W)
        gu = lax.dynamic_update_slice(gu, own_rows(EARLY_ROWS, EARLY_ROWS + FF_COLS, slab), _gate_slab(me) + (0, 0))
        gu = lax.dynamic_update_slice(gu, own_rows(EARLY_ROWS + FF_COLS, EARLY_ROWS + 2 * FF_COLS, slab), _up_slab(me) + (0, 0))
        d = lax.dynamic_update_slice(d, own_rows(EARLY_ROWS + 2 * FF_COLS, PACK_ROWS, (1, FF_COLS, PACK_W)), (me, 0, 0))
        return gu.reshape(2 * D_FF, D_MODEL), d.reshape(D_FF, D_MODEL)

    rs = {}

    def late_grads(g_gu, g_d):
        rs["late"] = _scatter_start([g_gu.reshape(GU_SHAPE), g_d.reshape(D_SHAPE)],
                                    ((0, _gate_slab, FF_COLS), (0, _up_slab, FF_COLS), (1, dev, FF_COLS)),
                                    name="rs_late_start")
        return rs["late"][-1]

    def mid_grads(g):
        rs["mid"] = _scatter_start([_mid_grad_pack(g)], ((0, dev, MID_ROWS),), name="rs_mid_start")
        return rs["mid"][-1]

    def last_grads(g_win_t):
        rs["last"] = _scatter_start([_w_in_grad_chunks(g_win_t)], ((0, dev, W_IN_ROWS),), name="rs_last_start")
        return rs["last"][-1]

    first_w = dict(small_w, mix_norm_g=mix_norm_g + ag_mid[-1][0:1, 0:1])
    loss_tot, gx, g_small = _local_step(
        x[0], loss_target[0], _w_in_operand(win_g), first_w, types.SimpleNamespace(mid=mid_weights, late=late_weights),
        types.SimpleNamespace(late=late_grads, mid=mid_grads, last=last_grads))

    g_gu, g_d, land_late = _scatter_wait(rs["late"], gx, name="rs_late_wait")
    g_mid, land_mid = _scatter_wait(rs["mid"], gx, name="rs_mid_wait")
    g_win, land_last = _scatter_wait(rs["last"], gx, name="rs_last_wait")
    gw = dict(w_gate=_peer_sum(g_gu, _gate_slab, land_late, 0, FF_COLS, idx, name="rs_sum_gate").T,
              w_up=_peer_sum(g_gu, _up_slab, land_late, 1, FF_COLS, idx, name="rs_sum_up").T,
              w_down=_peer_sum(g_d, dev, land_late, 2, FF_COLS, idx, name="rs_sum_down"),
              w_in=_peer_sum(g_win, dev, land_last, 0, W_IN_ROWS, idx, name="rs_sum_in")[0:W_IN_COLS].T)
    gw.update(_mid_unpack(_peer_sum(g_mid, dev, land_mid, 0, MID_ROWS, idx, name="rs_sum_mid")))
    dw, mw, vw = {}, {}, {}
    for n in BIG:
        dw[n], mw[n], vw[n] = _adamw_call(big_w[n], gw[n], big_m[n], big_v[n], name="adamw_" + n)

    loss_rows = jnp.pad(loss_tot[0:1, 0:1], ((0, SUBLANES - 1), (0, PACK_W - 1)))
    small_rows = jnp.concatenate([_small_pack(g_small_rows(g_small), SUBLANES), loss_rows], 0)
    parts, = _all_gather(small_rows, ((0, lambda d: (d,), 0, small_rows.shape[0]),), ((N_DEV,) + small_rows.shape,),
                         name="ag_small")
    gs, ds, ms, vs = _small_adamw(parts, _small_pack(small_w, 1), _small_pack(small_m, 1), _small_pack(small_v, 1))
    loss = gs[len(SMALL), 0]

    def small_out(packed):
        out = {}
        for k, n in enumerate(SMALL):
            out[n] = packed[k:k + 1, :SMALL_W[n]]
        out["final_norm_g"] = out["final_norm_g"].reshape(D_MODEL)
        return out

    gs, ds, ms, vs = small_out(gs), small_out(ds), small_out(ms), small_out(vs)

    order = ("mix_norm_g", "w_in", "swa_sinks", "q_norm_g", "w_uq", "kv_norm_g", "w_ukv", "w_o_swa", "w_o_mla", "w_out",
             "ffn_norm_g", "w_gate", "w_up", "w_down", "final_norm_g")

    def leaves(big, small):
        return [big[n][None] if n in big else small[n] for n in order]

    return (loss, gx[None], *leaves(gw, gs), *leaves(dw, ds), *leaves(mw, ms), *leaves(vw, vs))


def g_small_rows(g_small):
    out = dict(g_small)
    out["swa_sinks"] = jnp.pad(g_small["swa_sinks"], ((0, SUBLANES - 1), (0, 0)))
    return out
```

```python
import types

import numpy as np
import jax
import jax.numpy as jnp
from jax import lax
from jax.experimental import pallas as pl
from jax.experimental.pallas import tpu as pltpu

F32 = jnp.float32
MXU_DTYPE = jnp.bfloat16
WIRE_DTYPE = jnp.bfloat16

D_MODEL = 1024
EPS = 1e-6
ROPE_THETA = 10000.0
BLOCK = 128
HEAD_DIM = 64
SWA_HEADS = 8
SWA_KV_HEADS = 2
SWA_GROUP = SWA_HEADS // SWA_KV_HEADS
MLA_HEADS = 8
MLA_NOPE = 64
MLA_ROPE = 32
MLA_V = 64
MLA_QK = MLA_NOPE + MLA_ROPE
Q_LORA = 384
KV_LORA = 256
D_FF = 2816
IN_SIZES = (512, 128, 128, Q_LORA, KV_LORA, MLA_ROPE, D_MODEL, D_MODEL)
IN_OFF = tuple(int(v) for v in np.cumsum((0,) + IN_SIZES))
ADAM_LR, ADAM_B1, ADAM_B2, ADAM_EPS, ADAM_WD, ADAM_STEP = 0.001, 0.9, 0.999, 1e-08, 0.01, 10

LANES = 128
SUBLANES = 8
VMEM_LIMIT = 48 * 1024 * 1024
N_DEV = 8
AXES = ("x", "y", "c")

P_GA, P_GB, P_Q, P_QLAT, P_KR, P_K, P_V, P_KVLAT, P_W = 0, 1024, 2048, 3072, 3456, 3584, 3840, 4096, 4352
KR_LANE = 64

LOG2E = 1.4426950408889634

NT = (((1,), (1,)), ((), ()))
NN = (((1,), (0,)), ((), ()))
TN = (((0,), (0,)), ((), ()))


def _cparams(sem):
    return pltpu.CompilerParams(dimension_semantics=sem, vmem_limit_bytes=VMEM_LIMIT)


def _mm(a, b, mode, *, name, out_dtype=F32, add=None, after=None, tm=512, tn=512, tk=None):
    if mode == "nn":
        (M, K), (K2, N) = a.shape, b.shape
    elif mode == "nt":
        (M, K), (N, K2) = a.shape, b.shape
    else:
        (K, M), (K2, N) = a.shape, b.shape
    assert K == K2, (a.shape, b.shape, mode)
    tk = K if tk is None else tk
    tm, tn = min(tm, M), min(tn, N)
    assert M % tm == 0 and N % tn == 0 and K % tk == 0, (M, N, K, tm, tn, tk)
    nk = K // tk
    dn = {"nn": NN, "nt": NT, "tn": TN}[mode]
    if mode == "tn":
        a_spec = pl.BlockSpec((tk, tm), lambda i, j, k: (k, i))
    else:
        a_spec = pl.BlockSpec((tm, tk), lambda i, j, k: (i, k))
    if mode == "nt":
        b_spec = pl.BlockSpec((tn, tk), lambda i, j, k: (j, k))
    else:
        b_spec = pl.BlockSpec((tk, tn), lambda i, j, k: (k, j))
    o_spec = pl.BlockSpec((tm, tn), lambda i, j, k: (i, j))
    has_add, has_after = add is not None, after is not None

    def body(*refs):
        a_ref, b_ref = refs[0], refs[1]
        add_ref = refs[2] if has_add else None
        o_ref = refs[2 + has_add + has_after]
        p = lax.dot_general(a_ref[...], b_ref[...], dn, preferred_element_type=F32)

        def finish(acc):
            if has_add:
                acc = acc + add_ref[...]
            o_ref[...] = acc.astype(o_ref.dtype)

        if nk == 1:
            finish(p)
        else:
            acc_ref = refs[-1]
            k = pl.program_id(2)

            @pl.when(k == 0)
            def _():
                acc_ref[...] = p

            @pl.when(k > 0)
            def _():
                acc_ref[...] += p

            @pl.when(k == nk - 1)
            def _():
                finish(acc_ref[...])

    ins = [a, b] + ([add] if has_add else []) + ([after] if has_after else [])
    in_specs = [a_spec, b_spec] + ([o_spec] if has_add else []) + ([pl.BlockSpec(memory_space=pl.ANY)] if has_after else [])
    return pl.pallas_call(
        body, name=name, grid=(M // tm, N // tn, nk), in_specs=in_specs, out_specs=o_spec,
        out_shape=jax.ShapeDtypeStruct((M, N), out_dtype),
        scratch_shapes=[pltpu.VMEM((tm, tn), F32)] if nk > 1 else [],
        compiler_params=_cparams(("parallel", "parallel", "arbitrary")),
    )(*ins)


def _rows(ts, w, cb=0):
    return pl.BlockSpec((ts, w), lambda i: (i, cb))


def _const(r, w):
    return pl.BlockSpec((r, w), lambda i: (0, 0))


def _sublane_sum(v):
    ts, c = v.shape
    return jnp.sum(v.reshape(ts // SUBLANES, SUBLANES, c), axis=0)


def _sigmoid(v):
    return 1.0 / (1.0 + jnp.exp(-v))


def _rope(v, cos, s_up, s_dn, up, dn):
    return v * cos + pltpu.roll(v, up, 1) * s_up + pltpu.roll(v, dn, 1) * s_dn


def _rope_t(dv, cos, s_up, s_dn, up, dn):
    return dv * cos + pltpu.roll(dv * s_up, dn, 1) + pltpu.roll(dv * s_dn, up, 1)


def _rope_tables(seq):
    pos = np.arange(seq, dtype=np.float32)[:, None]

    def base(dim):
        inv = np.float32(ROPE_THETA) ** (-np.arange(0, dim, 2, dtype=np.float32) / np.float32(dim))
        ang = (pos * inv.astype(np.float32)[None, :]).astype(np.float32)
        return np.cos(ang).astype(np.float32), np.sin(ang).astype(np.float32)

    z = lambda n: np.zeros((seq, n), np.float32)
    ca, sa = base(HEAD_DIM)
    a_cos = np.concatenate([ca, ca, z(64)], 1)
    a_up = np.concatenate([-sa, z(96)], 1)
    a_dn = np.concatenate([z(32), sa, z(64)], 1)
    cb, sb = base(MLA_ROPE)
    one = np.ones((seq, 64), np.float32)
    q_cos = np.concatenate([one, cb, cb, z(32)], 1)
    k_cos = np.concatenate([z(64), cb, cb, z(32)], 1)
    b_up = np.concatenate([z(64), -sb, z(48)], 1)
    b_dn = np.concatenate([z(80), sb, z(32)], 1)
    return tuple(jnp.asarray(t) for t in (a_cos, a_up, a_dn, q_cos, k_cos, b_up, b_dn))


def _rms(v, g):
    return v * lax.rsqrt(jnp.mean(v * v, axis=-1, keepdims=True) + EPS) * g


def _rms_bwd(v, g, d):
    r = lax.rsqrt(jnp.mean(v * v, axis=-1, keepdims=True) + EPS)
    xh = v * r
    dxh = d * g
    return r * (dxh - xh * jnp.mean(dxh * xh, axis=-1, keepdims=True)), d * xh


def _norm_mm(x, g, w_t, *, name, tn, tm=512):
    s_, c = x.shape
    n = w_t.shape[0]

    def body(x_ref, g_ref, w_ref, h_ref, o_ref):
        h = _rms(x_ref[...], g_ref[...]).astype(h_ref.dtype)
        h_ref[...] = h
        o_ref[...] = lax.dot_general(h, w_ref[...], NT, preferred_element_type=F32)

    return pl.pallas_call(
        body, name=name, grid=(s_ // tm, n // tn),
        in_specs=[pl.BlockSpec((tm, c), lambda i, j: (i, 0)), pl.BlockSpec((1, c), lambda i, j: (0, 0)),
                  pl.BlockSpec((tn, c), lambda i, j: (j, 0))],
        out_specs=[pl.BlockSpec((tm, c), lambda i, j: (i, 0)), pl.BlockSpec((tm, tn), lambda i, j: (i, j))],
        out_shape=[jax.ShapeDtypeStruct((s_, c), MXU_DTYPE), jax.ShapeDtypeStruct((s_, n), F32)],
        compiler_params=_cparams(("parallel", "arbitrary")),
    )(x, g, w_t)


def _mm_norm_bwd(a, b, x, g, res, *, name, tk, after=None, tm=512):
    s_, kk = a.shape
    c = b.shape[1]
    nk = kk // tk
    has_after = after is not None

    def body(*refs):
        a_ref, b_ref, x_ref, g_ref, res_ref = refs[:5]
        dx_ref, dxb_ref, dg_ref, acc_ref = refs[5 + has_after:]
        i, k = pl.program_id(0), pl.program_id(1)
        p = jnp.dot(a_ref[...], b_ref[...], preferred_element_type=F32)

        @pl.when(k == 0)
        def _():
            acc_ref[...] = p

        @pl.when(k > 0)
        def _():
            acc_ref[...] += p

        @pl.when(k == nk - 1)
        def _():
            dx, gg = _rms_bwd(x_ref[...], g_ref[...], acc_ref[...])
            dx = dx + res_ref[...]
            dx_ref[...] = dx
            dxb_ref[...] = dx.astype(dxb_ref.dtype)

            @pl.when(i == 0)
            def _():
                dg_ref[...] = jnp.zeros(dg_ref.shape, F32)

            dg_ref[...] += _sublane_sum(gg)

    row = pl.BlockSpec((tm, c), lambda i, k: (i, 0))
    in_specs = [pl.BlockSpec((tm, tk), lambda i, k: (i, k)), pl.BlockSpec((tk, c), lambda i, k: (k, 0)), row,
                pl.BlockSpec((1, c), lambda i, k: (0, 0)), row] + ([pl.BlockSpec(memory_space=pl.ANY)] if has_after else [])
    return pl.pallas_call(
        body, name=name, grid=(s_ // tm, nk), in_specs=in_specs,
        out_specs=[row, row, pl.BlockSpec((SUBLANES, c), lambda i, k: (0, 0))],
        out_shape=[jax.ShapeDtypeStruct((s_, c), F32), jax.ShapeDtypeStruct((s_, c), MXU_DTYPE),
                   jax.ShapeDtypeStruct((SUBLANES, c), F32)],
        scratch_shapes=[pltpu.VMEM((tm, c), F32)], compiler_params=_cparams(("arbitrary", "arbitrary")),
    )(*([a, b, x, g, res] + ([after] if has_after else [])))


def _norm_bwd(x, g, dy, res, *, name, ts=256, x_cb=0, x_src_w=None):
    s_ = x.shape[0]
    c = dy.shape[1]
    has_res = res is not None

    def body(*refs):
        x_ref, g_ref, dy_ref = refs[0], refs[1], refs[2]
        res_ref = refs[3] if has_res else None
        dx_ref, dxb_ref, dg_ref = refs[-3], refs[-2], refs[-1]
        v = x_ref[...]
        r = lax.rsqrt(jnp.mean(v * v, axis=-1, keepdims=True) + EPS)
        xh = v * r
        d = dy_ref[...]
        dxh = d * g_ref[...]
        dx = r * (dxh - xh * jnp.mean(dxh * xh, axis=-1, keepdims=True))
        if has_res:
            dx = dx + res_ref[...]
        dx_ref[...] = dx
        dxb_ref[...] = dx.astype(dxb_ref.dtype)

        @pl.when(pl.program_id(0) == 0)
        def _():
            dg_ref[...] = jnp.zeros(dg_ref.shape, F32)

        dg_ref[...] += _sublane_sum(d * xh)

    ins = [x, g, dy] + ([res] if has_res else [])
    in_specs = [_rows(ts, c, x_cb), _const(1, c), _rows(ts, c)] + ([_rows(ts, c)] if has_res else [])
    return pl.pallas_call(
        body, name=name, grid=(s_ // ts,), in_specs=in_specs,
        out_specs=[_rows(ts, c), _rows(ts, c), _const(SUBLANES, c)],
        out_shape=[jax.ShapeDtypeStruct((s_, c), F32), jax.ShapeDtypeStruct((s_, c), MXU_DTYPE),
                   jax.ShapeDtypeStruct((SUBLANES, c), F32)],
        compiler_params=_cparams(("arbitrary",)),
    )(*ins)


def _attn_prep(p, gq, gkv, tabs, *, ts=256):
    s_ = p.shape[0]
    a_cos, a_up, a_dn, _, k_cos, b_up, b_dn = tabs

    def body(q_ref, k_ref, v_ref, ql_ref, kvl_ref, kr_ref, gq_ref, gkv_ref, ac, au, ad, kc, bu, bd,
             qa_ref, ka_ref, va_ref, cq_ref, ckv_ref, kro_ref):
        c_, u_, d_ = ac[...], au[...], ad[...]
        for h in range(SWA_HEADS):
            sl = slice(h * LANES, (h + 1) * LANES)
            qa_ref[:, sl] = _rope(q_ref[:, sl], c_, u_, d_, 96, 32).astype(qa_ref.dtype)
        for h in range(SWA_KV_HEADS):
            sl = slice(h * LANES, (h + 1) * LANES)
            ka_ref[:, sl] = _rope(k_ref[:, sl], c_, u_, d_, 96, 32).astype(ka_ref.dtype)
        va_ref[...] = v_ref[...].astype(va_ref.dtype)
        for src, gref, dst in ((ql_ref, gq_ref, cq_ref), (kvl_ref, gkv_ref, ckv_ref)):
            v = src[...]
            r = lax.rsqrt(jnp.mean(v * v, axis=-1, keepdims=True) + EPS)
            dst[...] = (v * r * gref[...]).astype(dst.dtype)
        kro_ref[...] = _rope(kr_ref[...], kc[...], bu[...], bd[...], 112, 16)

    tab = _rows(ts, LANES)
    return pl.pallas_call(
        body, name="attn_prep", grid=(s_ // ts,),
        in_specs=[_rows(ts, 1024, P_Q // 1024), _rows(ts, 256, P_K // 256), _rows(ts, 256, P_V // 256),
                  _rows(ts, Q_LORA, P_QLAT // Q_LORA), _rows(ts, KV_LORA, P_KVLAT // KV_LORA),
                  _rows(ts, LANES, P_KR // LANES), _const(1, Q_LORA), _const(1, KV_LORA), tab, tab, tab, tab, tab, tab],
        out_specs=[_rows(ts, 1024), _rows(ts, 256), _rows(ts, 256), _rows(ts, Q_LORA), _rows(ts, KV_LORA),
                   _rows(ts, LANES)],
        out_shape=[jax.ShapeDtypeStruct((s_, 1024), MXU_DTYPE), jax.ShapeDtypeStruct((s_, 256), MXU_DTYPE),
                   jax.ShapeDtypeStruct((s_, 256), MXU_DTYPE), jax.ShapeDtypeStruct((s_, Q_LORA), MXU_DTYPE),
                   jax.ShapeDtypeStruct((s_, KV_LORA), MXU_DTYPE), jax.ShapeDtypeStruct((s_, LANES), F32)],
        compiler_params=_cparams(("parallel",)),
    )(p, p, p, p, p, p, gq, gkv, a_cos, a_up, a_dn, k_cos, b_up, b_dn)


def _mla_prep(qp, kp, kro, tabs, *, ts=256):
    s_ = qp.shape[0]
    _, _, _, q_cos, _, b_up, b_dn = tabs

    def body(q_ref, k_ref, kr_ref, qc, bu, bd, qo_ref, ko_ref):
        c_, u_, d_ = qc[...], bu[...], bd[...]
        kr = kr_ref[...]
        for h in range(MLA_HEADS):
            sl = slice(h * LANES, (h + 1) * LANES)
            qo_ref[:, sl] = _rope(q_ref[:, sl], c_, u_, d_, 112, 16).astype(qo_ref.dtype)
            ko_ref[:, sl] = (k_ref[:, sl] + kr).astype(ko_ref.dtype)

    tab = _rows(ts, LANES)
    return pl.pallas_call(
        body, name="mla_prep", grid=(s_ // ts,),
        in_specs=[_rows(ts, 1024), _rows(ts, 1024), tab, tab, tab, tab],
        out_specs=[_rows(ts, 1024), _rows(ts, 1024)],
        out_shape=[jax.ShapeDtypeStruct((s_, 1024), MXU_DTYPE)] * 2,
        compiler_params=_cparams(("parallel",)),
    )(qp, kp, kro, q_cos, b_up, b_dn)


def _mla_unprep(dqc, dkc, dvp, tabs, *, ts=256):
    s_ = dqc.shape[0]
    _, _, _, q_cos, k_cos, b_up, b_dn = tabs

    def body(dq_ref, dk_ref, dv_ref, qc, kc, bu, bd, dqo_ref, dkvo_ref, dkr_ref):
        c_, u_, d_ = qc[...], bu[...], bd[...]
        tot = jnp.zeros((ts, LANES), F32)
        for h in range(MLA_HEADS):
            sl = slice(h * LANES, (h + 1) * LANES)
            dqo_ref[:, sl] = _rope_t(dq_ref[:, sl], c_, u_, d_, 112, 16).astype(dqo_ref.dtype)
            dk = dk_ref[:, sl]
            dkvo_ref[:, sl] = dk.astype(dkvo_ref.dtype)
            tot = tot + dk
        dkvo_ref[:, 1024:2048] = dv_ref[...].astype(dkvo_ref.dtype)
        dkr_ref[...] = _rope_t(tot, kc[...], u_, d_, 112, 16).astype(dkr_ref.dtype)

    tab = _rows(ts, LANES)
    return pl.pallas_call(
        body, name="mla_unprep", grid=(s_ // ts,),
        in_specs=[_rows(ts, 1024), _rows(ts, 1024), _rows(ts, 1024), tab, tab, tab, tab],
        out_specs=[_rows(ts, 1024), _rows(ts, 2048), _rows(ts, LANES)],
        out_shape=[jax.ShapeDtypeStruct((s_, 1024), MXU_DTYPE), jax.ShapeDtypeStruct((s_, 2048), MXU_DTYPE),
                   jax.ShapeDtypeStruct((s_, LANES), MXU_DTYPE)],
        compiler_params=_cparams(("parallel",)),
    )(dqc, dkc, dvp, q_cos, k_cos, b_up, b_dn)


def _swa_unrope(dqa, dka, tabs, *, ts=256):
    s_ = dqa.shape[0]
    a_cos, a_up, a_dn = tabs[0], tabs[1], tabs[2]

    def body(dq_ref, dk_ref, ac, au, ad, dqo_ref, dko_ref):
        c_, u_, d_ = ac[...], au[...], ad[...]
        for h in range(SWA_HEADS):
            sl = slice(h * LANES, (h + 1) * LANES)
            dqo_ref[:, sl] = _rope_t(dq_ref[:, sl], c_, u_, d_, 96, 32).astype(dqo_ref.dtype)
        for h in range(SWA_KV_HEADS):
            sl = slice(h * LANES, (h + 1) * LANES)
            dko_ref[:, sl] = _rope_t(dk_ref[:, sl], c_, u_, d_, 96, 32).astype(dko_ref.dtype)

    tab = _rows(ts, LANES)
    return pl.pallas_call(
        body, name="swa_unrope", grid=(s_ // ts,),
        in_specs=[_rows(ts, 1024), _rows(ts, 256), tab, tab, tab],
        out_specs=[_rows(ts, 1024), _rows(ts, 256)],
        out_shape=[jax.ShapeDtypeStruct((s_, 1024), MXU_DTYPE), jax.ShapeDtypeStruct((s_, 256), MXU_DTYPE)],
        compiler_params=_cparams(("parallel",)),
    )(dqa, dka, a_cos, a_up, a_dn)


def _attn_out_gate(oa, ob, woa, wob, p, *, ts=512):
    s_ = p.shape[0]

    def body(oa_ref, ob_ref, wa_ref, wb_ref, ga_ref, gb_ref, ta_ref, tb_ref, y_ref):
        ta = jnp.dot(oa_ref[...], wa_ref[...], preferred_element_type=F32)
        tb = jnp.dot(ob_ref[...], wb_ref[...], preferred_element_type=F32)
        ta_ref[...] = ta
        tb_ref[...] = tb
        y_ref[...] = (_sigmoid(ga_ref[...]) * ta + _sigmoid(gb_ref[...]) * tb).astype(y_ref.dtype)

    w = _const(1024, 1024)
    return pl.pallas_call(
        body, name="attn_out_gate", grid=(s_ // ts,),
        in_specs=[_rows(ts, 1024), _rows(ts, 1024), w, w, _rows(ts, 1024, P_GA // 1024), _rows(ts, 1024, P_GB // 1024)],
        out_specs=[_rows(ts, 1024)] * 3,
        out_shape=[jax.ShapeDtypeStruct((s_, 1024), F32)] * 2 + [jax.ShapeDtypeStruct((s_, 1024), MXU_DTYPE)],
        compiler_params=_cparams(("parallel",)),
    )(oa, ob, woa, wob, p, p)


def _d_y_gate(dx1b, wout, p, ta, tb, *, ts=512):
    s_ = p.shape[0]

    def body(dx_ref, w_ref, ga_ref, gb_ref, ta_ref, tb_ref, dta_ref, dtb_ref, dg_ref):
        d = lax.dot_general(dx_ref[...], w_ref[...], NT, preferred_element_type=F32)
        sa, sb = _sigmoid(ga_ref[...]), _sigmoid(gb_ref[...])
        dta_ref[...] = (d * sa).astype(dta_ref.dtype)
        dtb_ref[...] = (d * sb).astype(dtb_ref.dtype)
        dg_ref[:, 0:1024] = (d * ta_ref[...] * (sa * (1.0 - sa))).astype(dg_ref.dtype)
        dg_ref[:, 1024:2048] = (d * tb_ref[...] * (sb * (1.0 - sb))).astype(dg_ref.dtype)

    return pl.pallas_call(
        body, name="d_y_gate", grid=(s_ // ts,),
        in_specs=[_rows(ts, 1024), _const(1024, 1024), _rows(ts, 1024, P_GA // 1024), _rows(ts, 1024, P_GB // 1024),
                  _rows(ts, 1024), _rows(ts, 1024)],
        out_specs=[_rows(ts, 1024), _rows(ts, 1024), _rows(ts, 2048)],
        out_shape=[jax.ShapeDtypeStruct((s_, 1024), MXU_DTYPE)] * 2 + [jax.ShapeDtypeStruct((s_, 2048), MXU_DTYPE)],
        compiler_params=_cparams(("parallel",)),
    )(dx1b, wout, p, p, ta, tb)


FF_TILE = D_FF // 2


def _ffn_in_act(x1, g, wgu_t, *, tm=512):
    s_ = x1.shape[0]

    def body(x_ref, g_ref, w_ref, h_ref, gu_ref, a_ref):
        h = _rms(x_ref[...], g_ref[...]).astype(h_ref.dtype)
        h_ref[...] = h
        p = lax.dot_general(h, w_ref[...], NT, preferred_element_type=F32)
        gu_ref[...] = p
        gate = p[:, :FF_TILE]
        a_ref[...] = (gate * _sigmoid(gate) * p[:, FF_TILE:]).astype(a_ref.dtype)

    return pl.pallas_call(
        body, name="ffn_in", grid=(s_ // tm, 2),
        in_specs=[pl.BlockSpec((tm, D_MODEL), lambda i, j: (i, 0)), pl.BlockSpec((1, D_MODEL), lambda i, j: (0, 0)),
                  pl.BlockSpec((2 * FF_TILE, D_MODEL), lambda i, j: (j, 0))],
        out_specs=[pl.BlockSpec((tm, D_MODEL), lambda i, j: (i, 0)), pl.BlockSpec((tm, 2 * FF_TILE), lambda i, j: (i, j)),
                   pl.BlockSpec((tm, FF_TILE), lambda i, j: (i, j))],
        out_shape=[jax.ShapeDtypeStruct((s_, D_MODEL), MXU_DTYPE), jax.ShapeDtypeStruct((s_, 2 * D_FF), F32),
                   jax.ShapeDtypeStruct((s_, D_FF), MXU_DTYPE)],
        compiler_params=_cparams(("parallel", "arbitrary")),
    )(x1, g, wgu_t)


def _d_act_swiglu(dx2b, wd, gu, *, tm=512):
    s_ = dx2b.shape[0]

    def body(d_ref, w_ref, gu_ref, o_ref):
        da = lax.dot_general(d_ref[...], w_ref[...], NT, preferred_element_type=F32)
        g, u = gu_ref[:, :FF_TILE], gu_ref[:, FF_TILE:]
        sg = _sigmoid(g)
        o_ref[:, :FF_TILE] = (da * u * (sg * (1.0 + g * (1.0 - sg)))).astype(o_ref.dtype)
        o_ref[:, FF_TILE:] = (da * (g * sg)).astype(o_ref.dtype)

    gu_spec = pl.BlockSpec((tm, 2 * FF_TILE), lambda i, j: (i, j))
    return pl.pallas_call(
        body, name="d_act", grid=(s_ // tm, 2),
        in_specs=[pl.BlockSpec((tm, D_MODEL), lambda i, j: (i, 0)), pl.BlockSpec((FF_TILE, D_MODEL), lambda i, j: (j, 0)), gu_spec],
        out_specs=gu_spec, out_shape=jax.ShapeDtypeStruct((s_, 2 * D_FF), MXU_DTYPE),
        compiler_params=_cparams(("parallel", "parallel")),
    )(dx2b, wd, gu)


def _ffn_out_loss(act, wd, x1, g, tgt, *, ts=512):
    s_, c = x1.shape
    kk = act.shape[1]

    def body(a_ref, w_ref, x_ref, g_ref, t_ref, dx_ref, dxb_ref, dg_ref, lp_ref, tot_ref):
        v = x_ref[...] + jnp.dot(a_ref[...], w_ref[...], preferred_element_type=F32)
        r = lax.rsqrt(jnp.mean(v * v, axis=-1, keepdims=True) + EPS)
        xh = v * r
        gg = g_ref[...]
        e = xh * gg - t_ref[...]
        do = e * (1.0 / c)
        dxh = do * gg
        dx = r * (dxh - xh * jnp.mean(dxh * xh, axis=-1, keepdims=True))
        dx_ref[...] = dx
        dxb_ref[...] = dx.astype(dxb_ref.dtype)
        i = pl.program_id(0)

        @pl.when(i == 0)
        def _():
            dg_ref[...] = jnp.zeros(dg_ref.shape, F32)
            lp_ref[...] = jnp.zeros(lp_ref.shape, F32)

        dg_ref[...] += _sublane_sum(do * xh)
        lp_ref[...] += _sublane_sum(e * e)
        tot_ref[...] = jnp.full(tot_ref.shape, (0.5 / c) * jnp.sum(lp_ref[...]), F32)

    return pl.pallas_call(
        body, name="ffn_out_loss", grid=(s_ // ts,),
        in_specs=[_rows(ts, kk), _const(kk, c), _rows(ts, c), _const(1, c), _rows(ts, c)],
        out_specs=[_rows(ts, c), _rows(ts, c), _const(SUBLANES, c), _const(SUBLANES, c), _const(SUBLANES, LANES)],
        out_shape=[jax.ShapeDtypeStruct((s_, c), F32), jax.ShapeDtypeStruct((s_, c), MXU_DTYPE),
                   jax.ShapeDtypeStruct((SUBLANES, c), F32), jax.ShapeDtypeStruct((SUBLANES, c), F32),
                   jax.ShapeDtypeStruct((SUBLANES, LANES), F32)],
        compiler_params=_cparams(("arbitrary",)),
    )(act, wd, x1, g, tgt)


def _mla_bwd_prep(dob, o32, *, ts=256):
    s_ = dob.shape[0]

    def body(do_ref, o_ref, dob_ref, dl_ref):
        d = do_ref[...]
        dob_ref[...] = d.astype(dob_ref.dtype)
        prod = d * o_ref[...]
        for h in range(MLA_HEADS):
            dl_ref[h] = jnp.sum(prod[:, h * LANES:(h + 1) * LANES].T, axis=0, keepdims=True)

    return pl.pallas_call(
        body, name="mla_bwd_prep", grid=(s_ // ts,), in_specs=[_rows(ts, 1024), _rows(ts, 1024)],
        out_specs=[_rows(ts, 1024), pl.BlockSpec((MLA_HEADS, 1, ts), lambda i: (0, 0, i))],
        out_shape=[jax.ShapeDtypeStruct((s_, 1024), MXU_DTYPE), jax.ShapeDtypeStruct((MLA_HEADS, 1, s_), F32)],
        compiler_params=_cparams(("parallel",)),
    )(dob, o32)


SWA_T = 4 * BLOCK


def _swa_masks(sb):
    kr = lax.broadcasted_iota(jnp.int32, (2 * BLOCK, BLOCK), 0)
    qc = lax.broadcasted_iota(jnp.int32, (2 * BLOCK, BLOCK), 1)
    band = jnp.logical_and(kr > qc, kr <= qc + BLOCK)
    first = jnp.logical_and(band, kr >= BLOCK)
    return band, jnp.logical_or(first, jnp.logical_and(band, sb > 0))


def _swa_in_specs(rev, nsb):
    sbi = (lambda j: nsb - 1 - j) if rev else (lambda j: j)
    cur = pl.BlockSpec((SWA_T, LANES), lambda g, j: (sbi(j), g))
    prev = pl.BlockSpec((BLOCK, LANES), lambda g, j: (jnp.maximum(4 * sbi(j) - 1, 0), g))
    q = pl.BlockSpec((SWA_T, SWA_GROUP * LANES), lambda g, j: (sbi(j), g))
    sink = pl.BlockSpec((1, SUBLANES, LANES), lambda g, j: (g, 0, 0))
    lse = pl.BlockSpec((SWA_GROUP, 1, SWA_T), lambda g, j: (g, 0, sbi(j)))
    return q, cur, prev, sink, lse


def _swa_fwd(qa, ka, va, sink_b):
    s_ = qa.shape[0]
    nsb = s_ // SWA_T
    c2 = HEAD_DIM ** -0.5 * LOG2E

    def body(q_ref, kc_ref, kp_ref, vc_ref, vp_ref, sk_ref, o32_ref, o16_ref, lse_ref, kx, vx):
        kx[0:BLOCK, :] = kp_ref[...]
        kx[BLOCK:5 * BLOCK, :] = kc_ref[...]
        vx[0:BLOCK, :] = vp_ref[...]
        vx[BLOCK:5 * BLOCK, :] = vc_ref[...]
        band, band0 = _swa_masks(pl.program_id(1))
        for hh in range(SWA_GROUP):
            sink2 = sk_ref[0, hh:hh + 1, 0:1] * LOG2E
            cs = slice(hh * LANES, (hh + 1) * LANES)
            for b in range(4):
                rs = slice(b * BLOCK, (b + 1) * BLOCK)
                ks = slice(b * BLOCK, (b + 2) * BLOCK)
                st = lax.dot_general(kx[ks, :], q_ref[rs, cs], NT, preferred_element_type=F32) * c2
                st = jnp.where(band0 if b == 0 else band, st, -jnp.inf)
                m = jnp.maximum(jnp.max(st, axis=0, keepdims=True), sink2)
                pt = jnp.exp2(st - m)
                den = jnp.sum(pt, axis=0, keepdims=True) + jnp.exp2(sink2 - m)
                o = lax.dot_general((pt * (1.0 / den)).astype(MXU_DTYPE), vx[ks, :], TN, preferred_element_type=F32)
                o32_ref[rs, cs] = o
                o16_ref[rs, cs] = o.astype(o16_ref.dtype)
                lse_ref[hh, :, rs] = m + jnp.log2(den)

    q, cur, prev, sink, lse_spec = _swa_in_specs(False, nsb)
    return pl.pallas_call(
        body, name="swa_fwd", grid=(SWA_KV_HEADS, nsb), in_specs=[q, cur, prev, cur, prev, sink],
        out_specs=[q, q, lse_spec],
        out_shape=[jax.ShapeDtypeStruct((s_, SWA_HEADS * LANES), F32), jax.ShapeDtypeStruct((s_, SWA_HEADS * LANES), MXU_DTYPE),
                   jax.ShapeDtypeStruct((SWA_HEADS, 1, s_), F32)],
        scratch_shapes=[pltpu.VMEM((5 * BLOCK, LANES), MXU_DTYPE), pltpu.VMEM((5 * BLOCK, LANES), MXU_DTYPE)],
        compiler_params=_cparams(("parallel", "arbitrary")),
    )(qa, ka, ka, va, va, sink_b)


def _swa_bwd(qa, ka, va, sink_b, o32, do, lse):
    s_ = qa.shape[0]
    nsb = s_ // SWA_T
    scale = HEAD_DIM ** -0.5
    c2 = scale * LOG2E

    def body(q_ref, kc_ref, kp_ref, vc_ref, vp_ref, sk_ref, o_ref, do_ref, lse_ref,
             dq_ref, dk_ref, dv_ref, dsk_ref, kx, vx, kacc, vacc, kcar, vcar):
        j = pl.program_id(1)
        kx[0:BLOCK, :] = kp_ref[...]
        kx[BLOCK:5 * BLOCK, :] = kc_ref[...]
        vx[0:BLOCK, :] = vp_ref[...]
        vx[BLOCK:5 * BLOCK, :] = vc_ref[...]
        band, band0 = _swa_masks(nsb - 1 - j)
        kacc[...] = jnp.zeros(kacc.shape, F32)
        vacc[...] = jnp.zeros(vacc.shape, F32)

        @pl.when(j == 0)
        def _():
            kcar[...] = jnp.zeros(kcar.shape, F32)
            vcar[...] = jnp.zeros(vcar.shape, F32)
            dsk_ref[...] = jnp.zeros(dsk_ref.shape, F32)

        for hh in range(SWA_GROUP):
            sink2 = sk_ref[0, hh:hh + 1, 0:1] * LOG2E
            cs = slice(hh * LANES, (hh + 1) * LANES)
            dsink = jnp.zeros((1, 1), F32)
            for b in range(4):
                rs = slice(b * BLOCK, (b + 1) * BLOCK)
                ks = slice(b * BLOCK, (b + 2) * BLOCK)
                q, k2, v2 = q_ref[rs, cs], kx[ks, :], vx[ks, :]
                d = do_ref[rs, cs]
                delta = jnp.sum((d * o_ref[rs, cs]).T, axis=0, keepdims=True)
                l2 = lse_ref[hh, :, rs]
                st = lax.dot_general(k2, q, NT, preferred_element_type=F32) * c2
                pt = jnp.exp2(jnp.where(band0 if b == 0 else band, st, -jnp.inf) - l2)
                db = d.astype(MXU_DTYPE)
                dst = (pt * (lax.dot_general(v2, db, NT, preferred_element_type=F32) - delta) * scale).astype(MXU_DTYPE)
                dq_ref[rs, cs] = lax.dot_general(dst, k2, TN, preferred_element_type=F32)
                kacc[ks, :] += jnp.dot(dst, q, preferred_element_type=F32)
                vacc[ks, :] += jnp.dot(pt.astype(MXU_DTYPE), db, preferred_element_type=F32)
                dsink = dsink - jnp.sum(jnp.exp2(sink2 - l2) * delta, axis=1, keepdims=True)
            dsk_ref[0, hh:hh + 1, :] += jnp.broadcast_to(dsink, (1, LANES))

        dk_ref[0:3 * BLOCK, :] = kacc[BLOCK:4 * BLOCK, :]
        dk_ref[3 * BLOCK:4 * BLOCK, :] = kacc[4 * BLOCK:5 * BLOCK, :] + kcar[...]
        dv_ref[0:3 * BLOCK, :] = vacc[BLOCK:4 * BLOCK, :].astype(dv_ref.dtype)
        dv_ref[3 * BLOCK:4 * BLOCK, :] = (vacc[4 * BLOCK:5 * BLOCK, :] + vcar[...]).astype(dv_ref.dtype)
        kcar[...] = kacc[0:BLOCK, :]
        vcar[...] = vacc[0:BLOCK, :]

    q, cur, prev, sink, lse_spec = _swa_in_specs(True, nsb)
    return pl.pallas_call(
        body, name="swa_bwd", grid=(SWA_KV_HEADS, nsb),
        in_specs=[q, cur, prev, cur, prev, sink, q, q, lse_spec],
        out_specs=[q, cur, cur, sink],
        out_shape=[jax.ShapeDtypeStruct((s_, SWA_HEADS * LANES), F32), jax.ShapeDtypeStruct((s_, SWA_KV_HEADS * LANES), F32),
                   jax.ShapeDtypeStruct((s_, SWA_KV_HEADS * LANES), MXU_DTYPE),
                   jax.ShapeDtypeStruct((SWA_KV_HEADS, SUBLANES, LANES), F32)],
        scratch_shapes=[pltpu.VMEM((5 * BLOCK, LANES), MXU_DTYPE), pltpu.VMEM((5 * BLOCK, LANES), MXU_DTYPE),
                        pltpu.VMEM((5 * BLOCK, LANES), F32), pltpu.VMEM((5 * BLOCK, LANES), F32),
                        pltpu.VMEM((BLOCK, LANES), F32), pltpu.VMEM((BLOCK, LANES), F32)],
        compiler_params=_cparams(("arbitrary", "arbitrary")),
    )(qa, ka, ka, va, va, sink_b, o32, do, lse)


MLA_T = 512
MLA_FWD_GROUP = 4
MLA_BWD_GROUP = 2


def _mla_specs(s_, t, group):
    w = group * LANES
    qs = pl.BlockSpec((t, w), lambda g, i: (i, g))
    kv = pl.BlockSpec((s_, w), lambda g, i: (0, g))
    row = pl.BlockSpec((group, 1, t), lambda g, i: (g, 0, i))
    return qs, kv, row


def _causal_scores_t(k, q, t, c2, masked):
    st = lax.dot_general(k, q, NT, preferred_element_type=F32) * c2
    if masked:
        kr = lax.broadcasted_iota(jnp.int32, (t, t), 0)
        qc = lax.broadcasted_iota(jnp.int32, (t, t), 1)
        st = jnp.where(kr <= qc, st, -jnp.inf)
    return st


def _mla_fwd(qc, kc, vp):
    s_ = qc.shape[0]
    t = min(MLA_T, s_)
    c2 = MLA_QK ** -0.5 * LOG2E
    grp = MLA_FWD_GROUP

    def body(q_ref, k_ref, v_ref, o32_ref, o16_ref, lse_ref, m_s, acc_s):
        qi = pl.program_id(1)
        m_s[...] = jnp.full(m_s.shape, -jnp.inf, F32)
        acc_s[...] = jnp.zeros(acc_s.shape, F32)
        ones_lane = lax.broadcasted_iota(jnp.int32, (t, LANES), 1) == MLA_V

        def step(ki, masked):
            off = pl.multiple_of(ki * t, t)
            for g in range(grp):
                cs = slice(g * LANES, (g + 1) * LANES)
                st = _causal_scores_t(k_ref[pl.ds(off, t), cs], q_ref[:, cs], t, c2, masked)
                m_old = m_s[g]
                m_new = jnp.maximum(m_old, jnp.max(st, axis=0, keepdims=True))
                alpha = jnp.exp2(m_old - m_new)
                pt = jnp.exp2(st - m_new).astype(MXU_DTYPE)
                v = v_ref[pl.ds(off, t), cs]
                v = jnp.where(ones_lane, jnp.ones((), v.dtype), v)
                acc_s[g] = alpha * acc_s[g] + lax.dot_general(v, pt, TN, preferred_element_type=F32)
                m_s[g] = m_new

        def full_block(ki, carry):
            step(ki, False)
            return carry

        lax.fori_loop(0, qi, full_block, 0)
        step(qi, True)
        for g in range(grp):
            cs = slice(g * LANES, (g + 1) * LANES)
            acc = acc_s[g]
            l = acc[MLA_V:MLA_V + 1, :]
            o = (acc * (1.0 / l)).T
            o32_ref[:, cs] = o
            o16_ref[:, cs] = o.astype(o16_ref.dtype)
            lse_ref[g] = m_s[g] + jnp.log2(l)

    qs, kv, row = _mla_specs(s_, t, grp)
    return pl.pallas_call(
        body, name="mla_fwd", grid=(MLA_HEADS // grp, s_ // t), in_specs=[qs, kv, kv], out_specs=[qs, qs, row],
        out_shape=[jax.ShapeDtypeStruct((s_, MLA_HEADS * LANES), F32), jax.ShapeDtypeStruct((s_, MLA_HEADS * LANES), MXU_DTYPE),
                   jax.ShapeDtypeStruct((MLA_HEADS, 1, s_), F32)],
        scratch_shapes=[pltpu.VMEM((grp, 1, t), F32), pltpu.VMEM((grp, LANES, t), F32)],
        compiler_params=_cparams(("parallel", "arbitrary")),
    )(qc, kc, vp)


def _mla_bwd(qc, kc, vp, dob, lse, delta):
    s_ = qc.shape[0]
    t = min(MLA_T, s_)
    scale = MLA_QK ** -0.5
    c2 = scale * LOG2E
    grp = MLA_BWD_GROUP

    def body(q_ref, do_ref, lse_ref, dl_ref, k_ref, v_ref, dq_ref, dk_ref, dv_ref, dqt_s):
        qi = pl.program_id(1)

        @pl.when(qi == 0)
        def _():
            dk_ref[...] = jnp.zeros(dk_ref.shape, F32)
            dv_ref[...] = jnp.zeros(dv_ref.shape, F32)

        dqt_s[...] = jnp.zeros(dqt_s.shape, F32)

        def step(ki, masked):
            off = pl.multiple_of(ki * t, t)
            for g in range(grp):
                cs = slice(g * LANES, (g + 1) * LANES)
                q, d, k = q_ref[:, cs], do_ref[:, cs], k_ref[pl.ds(off, t), cs]
                pt = jnp.exp2(_causal_scores_t(k, q, t, c2, masked) - lse_ref[g])
                dpt = lax.dot_general(v_ref[pl.ds(off, t), cs], d, NT, preferred_element_type=F32)
                dst = (pt * (dpt - dl_ref[g]) * scale).astype(MXU_DTYPE)
                dv_ref[pl.ds(off, t), cs] += jnp.dot(pt.astype(MXU_DTYPE), d, preferred_element_type=F32)
                dk_ref[pl.ds(off, t), cs] += jnp.dot(dst, q, preferred_element_type=F32)
                dqt_s[g] += lax.dot_general(k, dst, TN, preferred_element_type=F32)

        def full_block(ki, carry):
            step(ki, False)
            return carry

        lax.fori_loop(0, qi, full_block, 0)
        step(qi, True)
        for g in range(grp):
            dq_ref[:, g * LANES:(g + 1) * LANES] = dqt_s[g].T

    qs, kv, row = _mla_specs(s_, t, grp)
    shp = jax.ShapeDtypeStruct((s_, MLA_HEADS * LANES), F32)
    return pl.pallas_call(
        body, name="mla_bwd", grid=(MLA_HEADS // grp, s_ // t), in_specs=[qs, qs, row, row, kv, kv],
        out_specs=[qs, kv, kv], out_shape=[shp, shp, shp], scratch_shapes=[pltpu.VMEM((grp, LANES, t), F32)],
        compiler_params=_cparams(("parallel", "arbitrary")),
    )(qc, dob, lse, delta, kc, vp)


def _pad_heads(w, nh, hd, axis):
    shp = w.shape
    w = w.reshape(shp[:axis] + (nh, hd) + shp[axis + 1:])
    pad = [(0, 0)] * w.ndim
    pad[axis + 1] = (0, LANES - hd)
    w = jnp.pad(w, pad)
    return w.reshape(shp[:axis] + (nh * LANES,) + shp[axis + 1:])


def _unpad_heads(w, nh, hd, axis):
    shp = w.shape
    w = w.reshape(shp[:axis] + (nh, LANES) + shp[axis + 1:])
    w = lax.slice_in_dim(w, 0, hd, axis=axis + 1)
    return w.reshape(shp[:axis] + (nh * hd,) + shp[axis + 1:])


PACK_W = 1024
ROW_TILE = 16
FULL_SHAPE = dict(w_in=(1024, 3488), w_uq=(384, 768), w_ukv=(256, 1024), w_o_swa=(512, 1024), w_o_mla=(512, 1024),
                  w_out=(1024, 1024), w_gate=(1024, 2816), w_up=(1024, 2816), w_down=(2816, 1024))
BIG = tuple(FULL_SHAPE)
ROW_SHARDED = ("w_out", "w_down")
W_IN_COLS = FULL_SHAPE["w_in"][1] // N_DEV
W_IN_ROWS = -(-W_IN_COLS // ROW_TILE) * ROW_TILE
FF_COLS = D_FF // N_DEV
OUT_ROWS = D_MODEL // N_DEV
SMALL_ROW0 = W_IN_ROWS + OUT_ROWS
SMALL_FLAT = (("w_uq", 0, 36), ("w_ukv", 48, 32), ("w_o_swa", 80, 64), ("w_o_mla", 144, 64))
SMALL_ROWS = 208
EARLY_ROWS = SMALL_ROW0 + SMALL_ROWS
LATE_ROWS = 3 * FF_COLS
PACK_ROWS = EARLY_ROWS + LATE_ROWS


def _shard_shape(n):
    r, c = FULL_SHAPE[n]
    return (r // N_DEV, c) if n in ROW_SHARDED else (r, c // N_DEV)


def _wire_pack(sh, dtype):
    c = lambda n: sh[n].astype(dtype)
    rows = [jnp.pad(c("w_in").T, ((0, W_IN_ROWS - W_IN_COLS), (0, 0))), c("w_out")]
    for n, _, r in SMALL_FLAT:
        rows.append(jnp.pad(c(n).reshape(r, PACK_W), ((0, -r % ROW_TILE), (0, 0))))
    return jnp.concatenate(rows + [c("w_gate").T, c("w_up").T, c("w_down")], 0)


MID_ROWS = OUT_ROWS + SMALL_ROWS


def _mid_unpack(p):
    out = dict(w_out=p[0:OUT_ROWS])
    for n, off, r in SMALL_FLAT:
        out[n] = p[OUT_ROWS + off:OUT_ROWS + off + r].reshape(_shard_shape(n))
    return out


def _w_in_row_maps():
    sp = lambda col: (col // W_IN_COLS) * W_IN_ROWS + col % W_IN_COLS
    fwd = np.full((P_W,), -1, np.int64)

    def put(t0, c0, n):
        fwd[t0:t0 + n] = [sp(c) for c in range(c0, c0 + n)]

    put(P_GA, IN_OFF[6], D_MODEL)
    put(P_GB, IN_OFF[7], D_MODEL)
    for h in range(SWA_HEADS):
        put(P_Q + LANES * h, IN_OFF[0] + HEAD_DIM * h, HEAD_DIM)
    put(P_QLAT, IN_OFF[3], Q_LORA)
    put(P_KR + KR_LANE, IN_OFF[5], MLA_ROPE)
    for h in range(SWA_KV_HEADS):
        put(P_K + LANES * h, IN_OFF[1] + HEAD_DIM * h, HEAD_DIM)
        put(P_V + LANES * h, IN_OFF[2] + HEAD_DIM * h, HEAD_DIM)
    put(P_KVLAT, IN_OFF[4], KV_LORA)
    inv = np.full((N_DEV * W_IN_ROWS,), -1, np.int64)
    inv[fwd[fwd >= 0]] = np.nonzero(fwd >= 0)[0]
    return fwd, inv


def _take_rows(src, idx, *, name):
    n_out, n_src, width = len(idx), src.shape[0], src.shape[1]
    assert n_out % BLOCK == 0 and n_src % BLOCK == 0
    n_tiles = n_out // BLOCK
    blocks = [sorted({int(v) // BLOCK for v in idx[i * BLOCK:(i + 1) * BLOCK] if v >= 0}) for i in range(n_tiles)]
    k_max = max(1, max(len(b) for b in blocks))
    tab = np.zeros((n_tiles, k_max), np.int32)
    sel = np.zeros((n_tiles, k_max, BLOCK, BLOCK), np.float32)
    for i, blks in enumerate(blocks):
        for m, b in enumerate(blks):
            tab[i, m] = b
            for r in range(BLOCK):
                v = int(idx[i * BLOCK + r])
                if v >= 0 and v // BLOCK == b:
                    sel[i, m, r, v % BLOCK] = 1.0

    def body(tab_ref, sel_ref, *refs):
        o_ref = refs[k_max]
        acc = jnp.dot(sel_ref[0, 0], refs[0][...], preferred_element_type=F32)
        for m in range(1, k_max):
            acc = acc + jnp.dot(sel_ref[0, m], refs[m][...], preferred_element_type=F32)
        o_ref[...] = acc.astype(o_ref.dtype)

    def src_spec(m):
        return pl.BlockSpec((BLOCK, width), lambda i, t: (t[i * k_max + m], 0))

    return pl.pallas_call(
        body, name=name,
        grid_spec=pltpu.PrefetchScalarGridSpec(
            num_scalar_prefetch=1, grid=(n_tiles,),
            in_specs=[pl.BlockSpec((1, k_max, BLOCK, BLOCK), lambda i, t: (i, 0, 0, 0))] + [src_spec(m) for m in range(k_max)],
            out_specs=pl.BlockSpec((BLOCK, width), lambda i, t: (i, 0))),
        out_shape=jax.ShapeDtypeStruct((n_out, width), src.dtype),
        compiler_params=_cparams(("parallel",)),
    )(jnp.asarray(tab.reshape(-1)), jnp.asarray(sel, src.dtype), *([src] * k_max))


def _w_in_operand(win_g):
    return _take_rows(win_g.reshape(N_DEV * W_IN_ROWS, PACK_W), _w_in_row_maps()[0], name="w_in_rows")


def _mid_operands(wout_g, small_g):
    def full(n, off, r):
        a = small_g[:, off:off + r].reshape((N_DEV,) + _shard_shape(n))
        return jnp.moveaxis(a, 0, 1).reshape(FULL_SHAPE[n])

    w = {n: full(n, off, r) for n, off, r in SMALL_FLAT}
    ukv = w["w_ukv"].reshape(KV_LORA, MLA_HEADS, MLA_NOPE + MLA_V)
    return dict(
        wout=wout_g.reshape(D_MODEL, D_MODEL),
        wuq=_pad_heads(w["w_uq"], MLA_HEADS, MLA_QK, 1),
        wuk=_pad_heads(ukv[:, :, :MLA_NOPE].reshape(KV_LORA, -1), MLA_HEADS, MLA_NOPE, 1),
        wuv=_pad_heads(ukv[:, :, MLA_NOPE:].reshape(KV_LORA, -1), MLA_HEADS, MLA_V, 1),
        woa=_pad_heads(w["w_o_swa"], SWA_HEADS, HEAD_DIM, 0),
        wob=_pad_heads(w["w_o_mla"], MLA_HEADS, MLA_V, 0),
    )


def _mid_grad_pack(g):
    uk = _unpad_heads(g["wukv"][:, :1024], MLA_HEADS, MLA_NOPE, 1).reshape(KV_LORA, MLA_HEADS, MLA_NOPE)
    uv = _unpad_heads(g["wukv"][:, 1024:], MLA_HEADS, MLA_V, 1).reshape(KV_LORA, MLA_HEADS, MLA_V)
    w = dict(w_uq=_unpad_heads(g["wuq"], MLA_HEADS, MLA_QK, 1), w_ukv=jnp.concatenate([uk, uv], 2).reshape(KV_LORA, -1),
             w_o_swa=_unpad_heads(g["woa"], SWA_HEADS, HEAD_DIM, 0), w_o_mla=_unpad_heads(g["wob"], MLA_HEADS, MLA_V, 0))

    def flat(n, r):
        rr, cc = FULL_SHAPE[n]
        a = jnp.moveaxis(w[n].reshape(rr, N_DEV, cc // N_DEV), 1, 0).reshape(N_DEV, r, PACK_W)
        return jnp.pad(a, ((0, 0), (0, -r % ROW_TILE), (0, 0))).astype(WIRE_DTYPE)

    return jnp.concatenate([g["wout"].reshape(N_DEV, OUT_ROWS, PACK_W)] + [flat(n, r) for n, _, r in SMALL_FLAT], 1)


def _w_in_grad_chunks(g_win_t):
    return _take_rows(g_win_t, _w_in_row_maps()[1], name="dw_in_rows").reshape(N_DEV, W_IN_ROWS, PACK_W)


def _local_step(x, tgt, win_t, small, weights, grads):
    s_ = x.shape[0]
    tabs = _rope_tables(s_)
    sink_b = jnp.broadcast_to(small["swa_sinks"].reshape(SWA_KV_HEADS, SWA_GROUP, 1), (SWA_KV_HEADS, SWA_GROUP, LANES))
    sink_b = jnp.pad(sink_b, ((0, 0), (0, SUBLANES - SWA_GROUP), (0, 0)))

    h, p = _norm_mm(x, small["mix_norm_g"], win_t, name="proj_in", tn=2176)
    qa, ka, va, cq, ckv, kro = _attn_prep(p, small["q_norm_g"], small["kv_norm_g"], tabs)
    ops = weights.mid(cq)
    oa32, oa16, lse_a = _swa_fwd(qa, ka, va, sink_b)
    qp = _mm(cq, ops["wuq"], "nn", name="mla_q_up", tm=1024, tn=1024)
    kp = _mm(ckv, ops["wuk"], "nn", name="mla_k_up", tm=1024, tn=1024)
    vp = _mm(ckv, ops["wuv"], "nn", name="mla_v_up", tm=1024, tn=1024, out_dtype=MXU_DTYPE)
    qc, kc = _mla_prep(qp, kp, kro, tabs)
    ob32, ob16, lse_b = _mla_fwd(qc, kc, vp)
    ta, tb, y = _attn_out_gate(oa16, ob16, ops["woa"], ops["wob"], p)
    x1 = _mm(y, ops["wout"], "nn", name="out_proj", add=x, tm=1024, tn=1024)
    wgu_t, wd = weights.late(x1)
    h2, gu, act = _ffn_in_act(x1, small["ffn_norm_g"], wgu_t)

    dx2, dx2b, dg3, _, tot = _ffn_out_loss(act, wd, x1, small["final_norm_g"].reshape(1, D_MODEL), tgt)
    g = {}
    g_wd = _mm(act, dx2b, "tn", name="dw_down", tm=1408, tn=1024, tk=1024, out_dtype=WIRE_DTYPE)
    dgu = _d_act_swiglu(dx2b, wd, gu)
    g_wgu = _mm(dgu, h2, "tn", name="dw_ffn_in", tm=1408, tn=1024, tk=1024, out_dtype=WIRE_DTYPE)
    token = grads.late(g_wgu, g_wd)
    dx1, dx1b, dg2 = _mm_norm_bwd(dgu, wgu_t, x1, small["ffn_norm_g"] + token[0:1, 0:1], dx2, name="d_h2", tk=2816)
    g["wout"] = _mm(y, dx1b, "tn", name="dw_out", tm=1024, tn=1024, tk=1024, out_dtype=WIRE_DTYPE)
    dta, dtb, dgab = _d_y_gate(dx1b, ops["wout"], p, ta, tb)
    doa = _mm(dta, ops["woa"], "nt", name="d_oa", tm=1024, tn=1024)
    g["woa"] = _mm(oa16, dta, "tn", name="dw_o_swa", tm=1024, tn=1024, tk=1024)
    dob = _mm(dtb, ops["wob"], "nt", name="d_ob", tm=1024, tn=1024)
    g["wob"] = _mm(ob16, dtb, "tn", name="dw_o_mla", tm=1024, tn=1024, tk=1024)
    dob16, delta_b = _mla_bwd_prep(dob, ob32)
    dqc, dkc, dvp = _mla_bwd(qc, kc, vp, dob16, lse_b, delta_b)
    dqp, dkv, dkr = _mla_unprep(dqc, dkc, dvp, tabs)
    dcq = _mm(dqp, ops["wuq"], "nt", name="d_cq", tn=Q_LORA)
    g["wuq"] = _mm(cq, dqp, "tn", name="dw_uq", tm=Q_LORA, tn=1024, tk=512)
    dckv = _mm(dkv, jnp.concatenate([ops["wuk"], ops["wuv"]], 1), "nt", name="d_ckv", tn=KV_LORA)
    g["wukv"] = _mm(ckv, dkv, "tn", name="dw_ukv", tm=KV_LORA, tn=1024, tk=512)
    token = grads.mid(g)
    _, dqlat, dgq = _norm_bwd(p, small["q_norm_g"] + token[0:1, 0:1], dcq, None, name="qnorm_bwd", x_cb=P_QLAT // Q_LORA)
    _, dkvlat, dgkv = _norm_bwd(p, small["kv_norm_g"], dckv, None, name="kvnorm_bwd", x_cb=P_KVLAT // KV_LORA)
    dqa, dka, dva, dsk = _swa_bwd(qa, ka, va, sink_b, oa32, doa, lse_a)
    dq_raw, dk_raw = _swa_unrope(dqa, dka, tabs)
    dp = jnp.concatenate([dgab, dq_raw, dqlat, dkr, dk_raw, dva, dkvlat], 1)
    token = grads.last(_mm(dp, h, "tn", name="dw_in", tm=2176, tn=1024, tk=1024, out_dtype=WIRE_DTYPE))
    gx, _, dg1 = _mm_norm_bwd(dp, win_t, x, small["mix_norm_g"], dx1, name="d_h", tk=2176, after=token)

    sm = dict(mix_norm_g=dg1, ffn_norm_g=dg2, final_norm_g=dg3, q_norm_g=dgq, kv_norm_g=dgkv,
              swa_sinks=dsk[:, :SWA_GROUP, 0].reshape(1, SWA_HEADS))
    return tot, gx, sm


MESH = pl.DeviceIdType.MESH
ANY = pl.BlockSpec(memory_space=pl.ANY)


def _position():
    return lax.axis_index("x"), lax.axis_index("y"), lax.axis_index("c")


def _all_gather(block, pieces, shapes, *, name):
    n_out = len(shapes)
    n_rows = sum(p[3] for p in pieces)

    def body(x_ref, *refs):
        outs, (send_sems, recv_sems, local_sem) = refs[:n_out], refs[n_out:]
        x, y, c = _position()
        me, sibling = (x, y, c), (x, y, 1 - c)
        chips = [(1 - x, y), (x, 1 - y), (1 - x, 1 - y)]

        def dst(piece, blk):
            arr, lead, _, _ = piece
            return outs[arr].at[lead(4 * blk[0] + 2 * blk[1] + blk[2])]

        def own(piece):
            return x_ref.at[pl.ds(piece[2], piece[3])]

        def copies(k, blk, to, from_input):
            return [pltpu.make_async_remote_copy(
                src_ref=own(p) if from_input else dst(p, blk), dst_ref=dst(p, blk), send_sem=send_sems.at[k],
                recv_sem=recv_sems.at[k], device_id=to, device_id_type=MESH) for p in pieces]

        gathered_rows = x_ref.at[pl.ds(0, n_rows)]

        def whole_block(k):
            return pltpu.make_async_remote_copy(src_ref=gathered_rows, dst_ref=gathered_rows, send_sem=send_sems.at[k],
                                                recv_sem=recv_sems.at[k], device_id=me, device_id_type=MESH)

        for p in pieces:
            pltpu.make_async_copy(own(p), dst(p, me), local_sem).start()
        for cp in copies(0, me, sibling, True):
            cp.start()
        for j, chip in enumerate(chips):
            for cp in copies(1 + j, me, (*chip, c), True):
                cp.start()
        for j, chip in enumerate(chips):
            whole_block(1 + j).wait_recv()
            for cp in copies(4 + j, (*chip, c), sibling, False):
                cp.start()
        whole_block(0).wait_recv()
        for j in range(3):
            whole_block(4 + j).wait_recv()
        for k in range(7):
            whole_block(k).wait_send()
        pltpu.make_async_copy(gathered_rows, gathered_rows, local_sem).wait()

    return pl.pallas_call(
        body, name=name, out_shape=[jax.ShapeDtypeStruct(s, block.dtype) for s in shapes], in_specs=[ANY],
        out_specs=[ANY] * n_out,
        scratch_shapes=[pltpu.SemaphoreType.DMA((7,)), pltpu.SemaphoreType.DMA((7,)), pltpu.SemaphoreType.DMA],
    )(block)


HBM = pl.BlockSpec(memory_space=pltpu.HBM)
SEM = pl.BlockSpec(memory_space=pltpu.SEMAPHORE)
TILE_DEVS = FF_TILE // FF_COLS
GU_SHAPE = (2, 2, TILE_DEVS, FF_COLS, PACK_W)


def _gate_slab(d):
    return (d // TILE_DEVS, 0, d % TILE_DEVS)


def _up_slab(d):
    return (d // TILE_DEVS, 1, d % TILE_DEVS)
D_SHAPE = (N_DEV, FF_COLS, PACK_W)
LAND_SHAPE = (N_DEV, LATE_ROWS, PACK_W)


def _split_params():
    return pltpu.CompilerParams(has_side_effects=pltpu.SideEffectType.DATAFLOW_SIDE_EFFECTING)


def _peer(x, y, c, k):
    return ((1 - x) if k & 4 else x, (1 - y) if k & 2 else y, (1 - c) if k & 1 else c)


def _empty_hbm(shape, dtype):
    return pltpu.with_memory_space_constraint(lax.empty(shape, dtype), pltpu.HBM)


def _wait_all(rows, send_sems, recv_sems, me):
    for k in range(N_DEV - 1):
        cp = pltpu.make_async_remote_copy(src_ref=rows, dst_ref=rows, send_sem=send_sems.at[k], recv_sem=recv_sems.at[k],
                                          device_id=me, device_id_type=MESH)
        cp.wait_send()
        cp.wait_recv()


def _token_shape():
    return jax.ShapeDtypeStruct((SUBLANES, LANES), F32)


def _gather_start(pack, row0, pieces, shapes, *, name):
    n = len(shapes)

    def body(*refs):
        p_ref, bufs, send_sems, recv_sems, token = refs[0], refs[1:1 + n], refs[1 + n], refs[2 + n], refs[-1]
        x, y, c = _position()
        me = 4 * x + 2 * y + c
        for k in range(1, N_DEV):
            off = row0
            for buf, lead, rows in pieces:
                pltpu.make_async_remote_copy(
                    src_ref=p_ref.at[pl.ds(off, rows)], dst_ref=bufs[buf].at[lead(me)], send_sem=send_sems.at[k - 1],
                    recv_sem=recv_sems.at[k - 1], device_id=_peer(x, y, c, k), device_id_type=MESH).start()
                off += rows
        token[...] = jnp.zeros_like(token)

    sems, dt = pltpu.SemaphoreType.DMA((N_DEV - 1,)), pack.dtype
    return pl.pallas_call(
        body, name=name,
        out_shape=(sems, sems, pltpu.HBM(pack.shape, dt)) + tuple(pltpu.HBM(s, dt) for s in shapes) + (_token_shape(),),
        in_specs=(HBM,) * (1 + n), out_specs=(SEM, SEM) + (HBM,) * (1 + n) + (pl.BlockSpec(memory_space=pltpu.VMEM),),
        input_output_aliases={i: 2 + i for i in range(1 + n)}, compiler_params=_split_params(),
    )(pltpu.with_memory_space_constraint(pack, pltpu.HBM), *[_empty_hbm(s, dt) for s in shapes])


def _gather_wait(started, row0, n_rows, after, *, name):
    send_sems, recv_sems, pack, *bufs = started[:-1]
    n = len(bufs)

    def body(*refs):
        _wait_all(refs[0].at[pl.ds(row0, n_rows)], refs[1 + n], refs[2 + n], _position())

    outs = pl.pallas_call(
        body, name=name, out_shape=tuple(pltpu.HBM(a.shape, a.dtype) for a in (pack, *bufs)),
        in_specs=(HBM,) * (1 + n) + (SEM, SEM, ANY), out_specs=(HBM,) * (1 + n),
        input_output_aliases={i: i for i in range(1 + n)}, compiler_params=_split_params(),
    )(pack, *bufs, send_sems, recv_sems, after)
    return outs[0], outs[1:]


def _scatter_start(srcs, pieces, *, name):
    n = len(srcs)
    land_shape = (N_DEV, sum(p[2] for p in pieces), PACK_W)

    def body(*refs):
        src_refs, land_ref, send_sems, recv_sems, token = refs[:n], refs[n], refs[n + 1], refs[n + 2], refs[-1]
        x, y, c = _position()
        me = 4 * x + 2 * y + c
        for k in range(1, N_DEV):
            px, py, pc = _peer(x, y, c, k)
            off = 0
            for si, lead, rows in pieces:
                pltpu.make_async_remote_copy(
                    src_ref=src_refs[si].at[lead(4 * px + 2 * py + pc)], dst_ref=land_ref.at[me, pl.ds(off, rows)],
                    send_sem=send_sems.at[k - 1], recv_sem=recv_sems.at[k - 1], device_id=(px, py, pc),
                    device_id_type=MESH).start()
                off += rows
        token[...] = jnp.zeros_like(token)

    sems, dt = pltpu.SemaphoreType.DMA((N_DEV - 1,)), srcs[0].dtype
    return pl.pallas_call(
        body, name=name,
        out_shape=(sems, sems) + tuple(pltpu.HBM(a.shape, dt) for a in srcs) + (pltpu.HBM(land_shape, dt), _token_shape()),
        in_specs=(HBM,) * (n + 1), out_specs=(SEM, SEM) + (HBM,) * (n + 1) + (pl.BlockSpec(memory_space=pltpu.VMEM),),
        input_output_aliases={i: 2 + i for i in range(n + 1)}, compiler_params=_split_params(),
    )(*[pltpu.with_memory_space_constraint(a, pltpu.HBM) for a in srcs], _empty_hbm(land_shape, dt))


def _scatter_wait(started, after, *, name):
    send_sems, recv_sems, *bufs = started[:-1]
    n = len(bufs)

    def body(*refs):
        _wait_all(refs[n - 1].at[0], refs[n], refs[n + 1], _position())

    return pl.pallas_call(
        body, name=name, out_shape=tuple(pltpu.HBM(a.shape, a.dtype) for a in bufs),
        in_specs=(HBM,) * n + (SEM, SEM, ANY), out_specs=(HBM,) * n, input_output_aliases={i: i for i in range(n)},
        compiler_params=_split_params(),
    )(*bufs, send_sems, recv_sems, after)


def _peer_sum(own, own_lead, land, block, rows, idx, *, name):
    lead_rank = own.ndim - 2

    def body(idx_ref, own_ref, *refs):
        o_ref = refs[N_DEV - 1]
        acc = own_ref[(0,) * lead_rank].astype(F32)
        for k in range(N_DEV - 1):
            acc = acc + refs[k][0].astype(F32)
        o_ref[...] = acc

    own_spec = pl.BlockSpec((1,) * lead_rank + (rows, PACK_W), lambda i, t: own_lead(t[0]) + (0, 0))

    def land_spec(k):
        return pl.BlockSpec((1, rows, PACK_W), lambda i, t: (t[k + 1], block, 0))

    return pl.pallas_call(
        body, name=name,
        grid_spec=pltpu.PrefetchScalarGridSpec(
            num_scalar_prefetch=1, grid=(1,), in_specs=[own_spec] + [land_spec(k) for k in range(N_DEV - 1)],
            out_specs=pl.BlockSpec((rows, PACK_W), lambda i, t: (0, 0))),
        out_shape=jax.ShapeDtypeStruct((rows, PACK_W), F32), compiler_params=_cparams(("arbitrary",)),
    )(idx, own, *([land] * (N_DEV - 1)))


def _adamw(w, g, m, v):
    m = ADAM_B1 * m + (1.0 - ADAM_B1) * g
    v = ADAM_B2 * v + (1.0 - ADAM_B2) * (g * g)
    m_hat = m / (1.0 - ADAM_B1 ** ADAM_STEP)
    v_hat = v / (1.0 - ADAM_B2 ** ADAM_STEP)
    delta = -ADAM_LR * (m_hat / (jnp.sqrt(v_hat) + ADAM_EPS) + ADAM_WD * w)
    return delta, m, v


def _adamw_call(w, g, m, v, *, name, max_rows=256):
    r, c_ = w.shape
    tr = max_rows if r > max_rows and r % max_rows == 0 else r

    def body(w_ref, g_ref, m_ref, v_ref, d_ref, mo_ref, vo_ref):
        d, mn, vn = _adamw(w_ref[...], g_ref[...], m_ref[...], v_ref[...])
        d_ref[...] = d
        mo_ref[...] = mn
        vo_ref[...] = vn

    row = pl.BlockSpec((tr, c_), lambda i: (i, 0))
    shp = jax.ShapeDtypeStruct((r, c_), F32)
    return pl.pallas_call(
        body, name=name, grid=(r // tr,), in_specs=[row] * 4, out_specs=[row] * 3, out_shape=[shp] * 3,
        compiler_params=_cparams(("parallel",)),
    )(w, g, m, v)


SMALL = ("mix_norm_g", "ffn_norm_g", "final_norm_g", "q_norm_g", "kv_norm_g", "swa_sinks")
SMALL_W = dict(mix_norm_g=1024, ffn_norm_g=1024, final_norm_g=1024, q_norm_g=Q_LORA, kv_norm_g=KV_LORA, swa_sinks=SWA_HEADS)


def _small_adamw(parts, w, m, v):
    n_par = parts.shape[1] // SUBLANES

    def body(p_ref, w_ref, m_ref, v_ref, g_ref, d_ref, mo_ref, vo_ref):
        tot = p_ref[0]
        for dev in range(1, N_DEV):
            tot = tot + p_ref[dev]
        row_id = lax.broadcasted_iota(jnp.int32, (SUBLANES, PACK_W), 0)
        g = jnp.zeros((SUBLANES, PACK_W), F32)
        for k in range(n_par):
            g = jnp.where(row_id == k, jnp.sum(tot[k * SUBLANES:(k + 1) * SUBLANES, :], axis=0, keepdims=True), g)
        d, mn, vn = _adamw(w_ref[...], g, m_ref[...], v_ref[...])
        g_ref[...] = g
        d_ref[...] = d
        mo_ref[...] = mn
        vo_ref[...] = vn

    shp = jax.ShapeDtypeStruct((SUBLANES, PACK_W), F32)
    vm = pl.BlockSpec(memory_space=pltpu.VMEM)
    return pl.pallas_call(body, name="small_adamw", in_specs=[vm] * 4, out_specs=[vm] * 4, out_shape=[shp] * 4)(parts, w, m, v)


def _small_pack(d, rows_each):
    parts = [jnp.pad(d[n].astype(F32), ((0, 0), (0, PACK_W - SMALL_W[n]))) for n in SMALL]
    out = jnp.concatenate(parts, 0)
    pad = -out.shape[0] % SUBLANES
    return jnp.pad(out, ((0, pad), (0, 0)))


def kernel(x, mix_norm_g, w_in, swa_sinks, q_norm_g, w_uq, kv_norm_g, w_ukv, w_o_swa, w_o_mla, w_out, ffn_norm_g, w_gate, w_up, w_down, final_norm_g, loss_target, m_mix_norm_g, m_w_in, m_swa_sinks, m_q_norm_g, m_w_uq, m_kv_norm_g, m_w_ukv, m_w_o_swa, m_w_o_mla, m_w_out, m_ffn_norm_g, m_w_gate, m_w_up, m_w_down, m_final_norm_g, v_mix_norm_g, v_w_in, v_swa_sinks, v_q_norm_g, v_w_uq, v_kv_norm_g, v_w_ukv, v_w_o_swa, v_w_o_mla, v_w_out, v_ffn_norm_g, v_w_gate, v_w_up, v_w_down, v_final_norm_g):
    big_w = dict(w_in=w_in[0], w_uq=w_uq[0], w_ukv=w_ukv[0], w_o_swa=w_o_swa[0], w_o_mla=w_o_mla[0], w_out=w_out[0],
                 w_gate=w_gate[0], w_up=w_up[0], w_down=w_down[0])
    big_m = dict(w_in=m_w_in[0], w_uq=m_w_uq[0], w_ukv=m_w_ukv[0], w_o_swa=m_w_o_swa[0], w_o_mla=m_w_o_mla[0],
                 w_out=m_w_out[0], w_gate=m_w_gate[0], w_up=m_w_up[0], w_down=m_w_down[0])
    big_v = dict(w_in=v_w_in[0], w_uq=v_w_uq[0], w_ukv=v_w_ukv[0], w_o_swa=v_w_o_swa[0], w_o_mla=v_w_o_mla[0],
                 w_out=v_w_out[0], w_gate=v_w_gate[0], w_up=v_w_up[0], w_down=v_w_down[0])
    small_w = dict(mix_norm_g=mix_norm_g, ffn_norm_g=ffn_norm_g, final_norm_g=final_norm_g.reshape(1, D_MODEL),
                   q_norm_g=q_norm_g, kv_norm_g=kv_norm_g, swa_sinks=swa_sinks)
    small_m = dict(mix_norm_g=m_mix_norm_g, ffn_norm_g=m_ffn_norm_g, final_norm_g=m_final_norm_g.reshape(1, D_MODEL),
                   q_norm_g=m_q_norm_g, kv_norm_g=m_kv_norm_g, swa_sinks=m_swa_sinks)
    small_v = dict(mix_norm_g=v_mix_norm_g, ffn_norm_g=v_ffn_norm_g, final_norm_g=v_final_norm_g.reshape(1, D_MODEL),
                   q_norm_g=v_q_norm_g, kv_norm_g=v_kv_norm_g, swa_sinks=v_swa_sinks)

    px, py, pc = _position()
    me = 4 * px + 2 * py + pc
    idx = jnp.stack([me] + [4 * qx + 2 * qy + qc for qx, qy, qc in (_peer(px, py, pc, k) for k in range(1, N_DEV))])
    idx = idx.astype(jnp.int32)

    dev = lambda d: (d,)
    pack = _wire_pack(big_w, WIRE_DTYPE)
    win_g, = _all_gather(pack, ((0, dev, 0, W_IN_ROWS),), ((N_DEV, W_IN_ROWS, PACK_W),), name="ag_early")
    ag_mid = _gather_start(pack, W_IN_ROWS, ((0, dev, OUT_ROWS), (1, dev, SMALL_ROWS)),
                           ((N_DEV, OUT_ROWS, PACK_W), (N_DEV, SMALL_ROWS, PACK_W)), name="ag_mid_start")
    ag = {}

    def own_rows(r0, r1, shape):
        return pack[r0:r1].reshape(shape)

    def mid_weights(after):
        pack_mid, (wout_g, small_g) = _gather_wait(ag_mid, W_IN_ROWS, MID_ROWS, after, name="ag_mid_wait")
        ag["late"] = _gather_start(pack_mid, EARLY_ROWS, ((0, _gate_slab, FF_COLS), (0, _up_slab, FF_COLS), (1, dev, FF_COLS)),
                                   (GU_SHAPE, D_SHAPE), name="ag_late_start")
        wout_g = lax.dynamic_update_slice(wout_g, own_rows(W_IN_ROWS, SMALL_ROW0, (1, OUT_ROWS, PACK_W)), (me, 0, 0))
        small_g = lax.dynamic_update_slice(small_g, own_rows(SMALL_ROW0, EARLY_ROWS, (1, SMALL_ROWS, PACK_W)), (me, 0, 0))
        ops = _mid_operands(wout_g, small_g)
        ops["wuq"] = ops["wuq"] + ag["late"][-1][0:1, 0:1].astype(ops["wuq"].dtype)
        return ops

    def late_weights(after):
        _, (gu, d) = _gather_wait(ag["late"], EARLY_ROWS, LATE_ROWS, after, name="ag_late_wait")
        slab = (1, 1, 1, FF_COLS, PACK_W)
        gu = lax.dynamic_update_slice(gu, own_rows(EARLY_ROWS, EARLY_ROWS + FF_COLS, slab), _gate_slab(me) + (0, 0))
        gu = lax.dynamic_update_slice(gu, own_rows(EARLY_ROWS + FF_COLS, EARLY_ROWS + 2 * FF_COLS, slab), _up_slab(me) + (0, 0))
        d = lax.dynamic_update_slice(d, own_rows(EARLY_ROWS + 2 * FF_COLS, PACK_ROWS, (1, FF_COLS, PACK_W)), (me, 0, 0))
        return gu.reshape(2 * D_FF, D_MODEL), d.reshape(D_FF, D_MODEL)

    rs = {}

    def late_grads(g_gu, g_d):
        rs["late"] = _scatter_start([g_gu.reshape(GU_SHAPE), g_d.reshape(D_SHAPE)],
                                    ((0, _gate_slab, FF_COLS), (0, _up_slab, FF_COLS), (1, dev, FF_COLS)),
                                    name="rs_late_start")
        return rs["late"][-1]

    def mid_grads(g):
        rs["mid"] = _scatter_start([_mid_grad_pack(g)], ((0, dev, MID_ROWS),), name="rs_mid_start")
        return rs["mid"][-1]

    def last_grads(g_win_t):
        rs["last"] = _scatter_start([_w_in_grad_chunks(g_win_t)], ((0, dev, W_IN_ROWS),), name="rs_last_start")
        return rs["last"][-1]

    first_w = dict(small_w, mix_norm_g=mix_norm_g + ag_mid[-1][0:1, 0:1])
    loss_tot, gx, g_small = _local_step(
        x[0], loss_target[0], _w_in_operand(win_g), first_w, types.SimpleNamespace(mid=mid_weights, late=late_weights),
        types.SimpleNamespace(late=late_grads, mid=mid_grads, last=last_grads))

    g_gu, g_d, land_late = _scatter_wait(rs["late"], gx, name="rs_late_wait")
    g_mid, land_mid = _scatter_wait(rs["mid"], gx, name="rs_mid_wait")
    g_win, land_last = _scatter_wait(rs["last"], gx, name="rs_last_wait")
    gw = dict(w_gate=_peer_sum(g_gu, _gate_slab, land_late, 0, FF_COLS, idx, name="rs_sum_gate").T,
              w_up=_peer_sum(g_gu, _up_slab, land_late, 1, FF_COLS, idx, name="rs_sum_up").T,
              w_down=_peer_sum(g_d, dev, land_late, 2, FF_COLS, idx, name="rs_sum_down"),
              w_in=_peer_sum(g_win, dev, land_last, 0, W_IN_ROWS, idx, name="rs_sum_in")[0:W_IN_COLS].T)
    gw.update(_mid_unpack(_peer_sum(g_mid, dev, land_mid, 0, MID_ROWS, idx, name="rs_sum_mid")))
    dw, mw, vw = {}, {}, {}
    for n in BIG:
        dw[n], mw[n], vw[n] = _adamw_call(big_w[n], gw[n], big_m[n], big_v[n], name="adamw_" + n)

    loss_rows = jnp.pad(loss_tot[0:1, 0:1], ((0, SUBLANES - 1), (0, PACK_W - 1)))
    small_rows = jnp.concatenate([_small_pack(g_small_rows(g_small), SUBLANES), loss_rows], 0)
    parts, = _all_gather(small_rows, ((0, lambda d: (d,), 0, small_rows.shape[0]),), ((N_DEV,) + small_rows.shape,),
                         name="ag_small")
    gs, ds, ms, vs = _small_adamw(parts, _small_pack(small_w, 1), _small_pack(small_m, 1), _small_pack(small_v, 1))
    loss = gs[len(SMALL), 0]

    def small_out(packed):
        out = {}
        for k, n in enumerate(SMALL):
            out[n] = packed[k:k + 1, :SMALL_W[n]]
        out["final_norm_g"] = out["final_norm_g"].reshape(D_MODEL)
        return out

    gs, ds, ms, vs = small_out(gs), small_out(ds), small_out(ms), small_out(vs)

    order = ("mix_norm_g", "w_in", "swa_sinks", "q_norm_g", "w_uq", "kv_norm_g", "w_ukv", "w_o_swa", "w_o_mla", "w_out",
             "ffn_norm_g", "w_gate", "w_up", "w_down", "final_norm_g")

    def leaves(big, small):
        return [big[n][None] if n in big else small[n] for n in order]

    return (loss, gx[None], *leaves(gw, gs), *leaves(dw, ds), *leaves(mw, ms), *leaves(vw, vs))


def g_small_rows(g_small):
    out = dict(g_small)
    out["swa_sinks"] = jnp.pad(g_small["swa_sinks"], ((0, SUBLANES - 1), (0, 0)))
    return out
```

```python
import types

import numpy as np
import jax
import jax.numpy as jnp
from jax import lax
from jax.experimental import pallas as pl
from jax.experimental.pallas import tpu as pltpu

F32 = jnp.float32
MXU_DTYPE = jnp.bfloat16
WIRE_DTYPE = jnp.bfloat16

D_MODEL = 1024
EPS = 1e-6
ROPE_THETA = 10000.0
BLOCK = 128
HEAD_DIM = 64
SWA_HEADS = 8
SWA_KV_HEADS = 2
SWA_GROUP = SWA_HEADS // SWA_KV_HEADS
MLA_HEADS = 8
MLA_NOPE = 64
MLA_ROPE = 32
MLA_V = 64
MLA_QK = MLA_NOPE + MLA_ROPE
Q_LORA = 384
KV_LORA = 256
D_FF = 2816
IN_SIZES = (512, 128, 128, Q_LORA, KV_LORA, MLA_ROPE, D_MODEL, D_MODEL)
IN_OFF = tuple(int(v) for v in np.cumsum((0,) + IN_SIZES))
ADAM_LR, ADAM_B1, ADAM_B2, ADAM_EPS, ADAM_WD, ADAM_STEP = 0.001, 0.9, 0.999, 1e-08, 0.01, 10

LANES = 128
SUBLANES = 8
VMEM_LIMIT = 48 * 1024 * 1024
N_DEV = 8
AXES = ("x", "y", "c")

P_GA, P_GB, P_Q, P_QLAT, P_KR, P_K, P_V, P_KVLAT, P_W = 0, 1024, 2048, 3072, 3456, 3584, 3840, 4096, 4352
KR_LANE = 64

LOG2E = 1.4426950408889634

NT = (((1,), (1,)), ((), ()))
NN = (((1,), (0,)), ((), ()))
TN = (((0,), (0,)), ((), ()))


def _cparams(sem):
    return pltpu.CompilerParams(dimension_semantics=sem, vmem_limit_bytes=VMEM_LIMIT)


def _mm(a, b, mode, *, name, out_dtype=F32, add=None, after=None, tm=512, tn=512, tk=None):
    if mode == "nn":
        (M, K), (K2, N) = a.shape, b.shape
    elif mode == "nt":
        (M, K), (N, K2) = a.shape, b.shape
    else:
        (K, M), (K2, N) = a.shape, b.shape
    assert K == K2, (a.shape, b.shape, mode)
    tk = K if tk is None else tk
    tm, tn = min(tm, M), min(tn, N)
    assert M % tm == 0 and N % tn == 0 and K % tk == 0, (M, N, K, tm, tn, tk)
    nk = K // tk
    dn = {"nn": NN, "nt": NT, "tn": TN}[mode]
    if mode == "tn":
        a_spec = pl.BlockSpec((tk, tm), lambda i, j, k: (k, i))
    else:
        a_spec = pl.BlockSpec((tm, tk), lambda i, j, k: (i, k))
    if mode == "nt":
        b_spec = pl.BlockSpec((tn, tk), lambda i, j, k: (j, k))
    else:
        b_spec = pl.BlockSpec((tk, tn), lambda i, j, k: (k, j))
    o_spec = pl.BlockSpec((tm, tn), lambda i, j, k: (i, j))
    has_add, has_after = add is not None, after is not None

    def body(*refs):
        a_ref, b_ref = refs[0], refs[1]
        add_ref = refs[2] if has_add else None
        o_ref = refs[2 + has_add + has_after]
        p = lax.dot_general(a_ref[...], b_ref[...], dn, preferred_element_type=F32)

        def finish(acc):
            if has_add:
                acc = acc + add_ref[...]
            o_ref[...] = acc.astype(o_ref.dtype)

        if nk == 1:
            finish(p)
        else:
            acc_ref = refs[-1]
            k = pl.program_id(2)

            @pl.when(k == 0)
            def _():
                acc_ref[...] = p

            @pl.when(k > 0)
            def _():
                acc_ref[...] += p

            @pl.when(k == nk - 1)
            def _():
                finish(acc_ref[...])

    ins = [a, b] + ([add] if has_add else []) + ([after] if has_after else [])
    in_specs = [a_spec, b_spec] + ([o_spec] if has_add else []) + ([pl.BlockSpec(memory_space=pl.ANY)] if has_after else [])
    return pl.pallas_call(
        body, name=name, grid=(M // tm, N // tn, nk), in_specs=in_specs, out_specs=o_spec,
        out_shape=jax.ShapeDtypeStruct((M, N), out_dtype),
        scratch_shapes=[pltpu.VMEM((tm, tn), F32)] if nk > 1 else [],
        compiler_params=_cparams(("parallel", "parallel", "arbitrary")),
    )(*ins)


def _rows(ts, w, cb=0):
    return pl.BlockSpec((ts, w), lambda i: (i, cb))


def _const(r, w):
    return pl.BlockSpec((r, w), lambda i: (0, 0))


def _sublane_sum(v):
    ts, c = v.shape
    return jnp.sum(v.reshape(ts // SUBLANES, SUBLANES, c), axis=0)


def _sigmoid(v):
    return 1.0 / (1.0 + jnp.exp(-v))


def _rope(v, cos, s_up, s_dn, up, dn):
    return v * cos + pltpu.roll(v, up, 1) * s_up + pltpu.roll(v, dn, 1) * s_dn


def _rope_t(dv, cos, s_up, s_dn, up, dn):
    return dv * cos + pltpu.roll(dv * s_up, dn, 1) + pltpu.roll(dv * s_dn, up, 1)


def _rope_tables(seq):
    pos = np.arange(seq, dtype=np.float32)[:, None]

    def base(dim):
        inv = np.float32(ROPE_THETA) ** (-np.arange(0, dim, 2, dtype=np.float32) / np.float32(dim))
        ang = (pos * inv.astype(np.float32)[None, :]).astype(np.float32)
        return np.cos(ang).astype(np.float32), np.sin(ang).astype(np.float32)

    z = lambda n: np.zeros((seq, n), np.float32)
    ca, sa = base(HEAD_DIM)
    a_cos = np.concatenate([ca, ca, z(64)], 1)
    a_up = np.concatenate([-sa, z(96)], 1)
    a_dn = np.concatenate([z(32), sa, z(64)], 1)
    cb, sb = base(MLA_ROPE)
    one = np.ones((seq, 64), np.float32)
    q_cos = np.concatenate([one, cb, cb, z(32)], 1)
    k_cos = np.concatenate([z(64), cb, cb, z(32)], 1)
    b_up = np.concatenate([z(64), -sb, z(48)], 1)
    b_dn = np.concatenate([z(80), sb, z(32)], 1)
    return tuple(jnp.asarray(t) for t in (a_cos, a_up, a_dn, q_cos, k_cos, b_up, b_dn))


def _rms(v, g):
    return v * lax.rsqrt(jnp.mean(v * v, axis=-1, keepdims=True) + EPS) * g


def _rms_bwd(v, g, d):
    r = lax.rsqrt(jnp.mean(v * v, axis=-1, keepdims=True) + EPS)
    xh = v * r
    dxh = d * g
    return r * (dxh - xh * jnp.mean(dxh * xh, axis=-1, keepdims=True)), d * xh


def _norm_mm(x, g, w_t, *, name, tn, tm=512):
    s_, c = x.shape
    n = w_t.shape[0]

    def body(x_ref, g_ref, w_ref, h_ref, o_ref):
        h = _rms(x_ref[...], g_ref[...]).astype(h_ref.dtype)
        h_ref[...] = h
        o_ref[...] = lax.dot_general(h, w_ref[...], NT, preferred_element_type=F32)

    return pl.pallas_call(
        body, name=name, grid=(s_ // tm, n // tn),
        in_specs=[pl.BlockSpec((tm, c), lambda i, j: (i, 0)), pl.BlockSpec((1, c), lambda i, j: (0, 0)),
                  pl.BlockSpec((tn, c), lambda i, j: (j, 0))],
        out_specs=[pl.BlockSpec((tm, c), lambda i, j: (i, 0)), pl.BlockSpec((tm, tn), lambda i, j: (i, j))],
        out_shape=[jax.ShapeDtypeStruct((s_, c), MXU_DTYPE), jax.ShapeDtypeStruct((s_, n), F32)],
        compiler_params=_cparams(("parallel", "arbitrary")),
    )(x, g, w_t)


def _mm_norm_bwd(a, b, x, g, res, *, name, tk, after=None, tm=512):
    s_, kk = a.shape
    c = b.shape[1]
    nk = kk // tk
    has_after = after is not None

    def body(*refs):
        a_ref, b_ref, x_ref, g_ref, res_ref = refs[:5]
        dx_ref, dxb_ref, dg_ref, acc_ref = refs[5 + has_after:]
        i, k = pl.program_id(0), pl.program_id(1)
        p = jnp.dot(a_ref[...], b_ref[...], preferred_element_type=F32)

        @pl.when(k == 0)
        def _():
            acc_ref[...] = p

        @pl.when(k > 0)
        def _():
            acc_ref[...] += p

        @pl.when(k == nk - 1)
        def _():
            dx, gg = _rms_bwd(x_ref[...], g_ref[...], acc_ref[...])
            dx = dx + res_ref[...]
            dx_ref[...] = dx
            dxb_ref[...] = dx.astype(dxb_ref.dtype)

            @pl.when(i == 0)
            def _():
                dg_ref[...] = jnp.zeros(dg_ref.shape, F32)

            dg_ref[...] += _sublane_sum(gg)

    row = pl.BlockSpec((tm, c), lambda i, k: (i, 0))
    in_specs = [pl.BlockSpec((tm, tk), lambda i, k: (i, k)), pl.BlockSpec((tk, c), lambda i, k: (k, 0)), row,
                pl.BlockSpec((1, c), lambda i, k: (0, 0)), row] + ([pl.BlockSpec(memory_space=pl.ANY)] if has_after else [])
    return pl.pallas_call(
        body, name=name, grid=(s_ // tm, nk), in_specs=in_specs,
        out_specs=[row, row, pl.BlockSpec((SUBLANES, c), lambda i, k: (0, 0))],
        out_shape=[jax.ShapeDtypeStruct((s_, c), F32), jax.ShapeDtypeStruct((s_, c), MXU_DTYPE),
                   jax.ShapeDtypeStruct((SUBLANES, c), F32)],
        scratch_shapes=[pltpu.VMEM((tm, c), F32)], compiler_params=_cparams(("arbitrary", "arbitrary")),
    )(*([a, b, x, g, res] + ([after] if has_after else [])))


def _norm_bwd(x, g, dy, res, *, name, ts=256, x_cb=0, x_src_w=None):
    s_ = x.shape[0]
    c = dy.shape[1]
    has_res = res is not None

    def body(*refs):
        x_ref, g_ref, dy_ref = refs[0], refs[1], refs[2]
        res_ref = refs[3] if has_res else None
        dx_ref, dxb_ref, dg_ref = refs[-3], refs[-2], refs[-1]
        v = x_ref[...]
        r = lax.rsqrt(jnp.mean(v * v, axis=-1, keepdims=True) + EPS)
        xh = v * r
        d = dy_ref[...]
        dxh = d * g_ref[...]
        dx = r * (dxh - xh * jnp.mean(dxh * xh, axis=-1, keepdims=True))
        if has_res:
            dx = dx + res_ref[...]
        dx_ref[...] = dx
        dxb_ref[...] = dx.astype(dxb_ref.dtype)

        @pl.when(pl.program_id(0) == 0)
        def _():
            dg_ref[...] = jnp.zeros(dg_ref.shape, F32)

        dg_ref[...] += _sublane_sum(d * xh)

    ins = [x, g, dy] + ([res] if has_res else [])
    in_specs = [_rows(ts, c, x_cb), _const(1, c), _rows(ts, c)] + ([_rows(ts, c)] if has_res else [])
    return pl.pallas_call(
        body, name=name, grid=(s_ // ts,), in_specs=in_specs,
        out_specs=[_rows(ts, c), _rows(ts, c), _const(SUBLANES, c)],
        out_shape=[jax.ShapeDtypeStruct((s_, c), F32), jax.ShapeDtypeStruct((s_, c), MXU_DTYPE),
                   jax.ShapeDtypeStruct((SUBLANES, c), F32)],
        compiler_params=_cparams(("arbitrary",)),
    )(*ins)


def _attn_prep(p, gq, gkv, tabs, *, ts=256):
    s_ = p.shape[0]
    a_cos, a_up, a_dn, _, k_cos, b_up, b_dn = tabs

    def body(q_ref, k_ref, v_ref, ql_ref, kvl_ref, kr_ref, gq_ref, gkv_ref, ac, au, ad, kc, bu, bd,
             qa_ref, ka_ref, va_ref, cq_ref, ckv_ref, kro_ref):
        c_, u_, d_ = ac[...], au[...], ad[...]
        for h in range(SWA_HEADS):
            sl = slice(h * LANES, (h + 1) * LANES)
            qa_ref[:, sl] = _rope(q_ref[:, sl], c_, u_, d_, 96, 32).astype(qa_ref.dtype)
        for h in range(SWA_KV_HEADS):
            sl = slice(h * LANES, (h + 1) * LANES)
            ka_ref[:, sl] = _rope(k_ref[:, sl], c_, u_, d_, 96, 32).astype(ka_ref.dtype)
        va_ref[...] = v_ref[...].astype(va_ref.dtype)
        for src, gref, dst in ((ql_ref, gq_ref, cq_ref), (kvl_ref, gkv_ref, ckv_ref)):
            v = src[...]
            r = lax.rsqrt(jnp.mean(v * v, axis=-1, keepdims=True) + EPS)
            dst[...] = (v * r * gref[...]).astype(dst.dtype)
        kro_ref[...] = _rope(kr_ref[...], kc[...], bu[...], bd[...], 112, 16)

    tab = _rows(ts, LANES)
    return pl.pallas_call(
        body, name="attn_prep", grid=(s_ // ts,),
        in_specs=[_rows(ts, 1024, P_Q // 1024), _rows(ts, 256, P_K // 256), _rows(ts, 256, P_V // 256),
                  _rows(ts, Q_LORA, P_QLAT // Q_LORA), _rows(ts, KV_LORA, P_KVLAT // KV_LORA),
                  _rows(ts, LANES, P_KR // LANES), _const(1, Q_LORA), _const(1, KV_LORA), tab, tab, tab, tab, tab, tab],
        out_specs=[_rows(ts, 1024), _rows(ts, 256), _rows(ts, 256), _rows(ts, Q_LORA), _rows(ts, KV_LORA),
                   _rows(ts, LANES)],
        out_shape=[jax.ShapeDtypeStruct((s_, 1024), MXU_DTYPE), jax.ShapeDtypeStruct((s_, 256), MXU_DTYPE),
                   jax.ShapeDtypeStruct((s_, 256), MXU_DTYPE), jax.ShapeDtypeStruct((s_, Q_LORA), MXU_DTYPE),
                   jax.ShapeDtypeStruct((s_, KV_LORA), MXU_DTYPE), jax.ShapeDtypeStruct((s_, LANES), F32)],
        compiler_params=_cparams(("parallel",)),
    )(p, p, p, p, p, p, gq, gkv, a_cos, a_up, a_dn, k_cos, b_up, b_dn)


def _mla_prep(qp, kp, kro, tabs, *, ts=256):
    s_ = qp.shape[0]
    _, _, _, q_cos, _, b_up, b_dn = tabs

    def body(q_ref, k_ref, kr_ref, qc, bu, bd, qo_ref, ko_ref):
        c_, u_, d_ = qc[...], bu[...], bd[...]
        kr = kr_ref[...]
        for h in range(MLA_HEADS):
            sl = slice(h * LANES, (h + 1) * LANES)
            qo_ref[:, sl] = _rope(q_ref[:, sl], c_, u_, d_, 112, 16).astype(qo_ref.dtype)
            ko_ref[:, sl] = (k_ref[:, sl] + kr).astype(ko_ref.dtype)

    tab = _rows(ts, LANES)
    return pl.pallas_call(
        body, name="mla_prep", grid=(s_ // ts,),
        in_specs=[_rows(ts, 1024), _rows(ts, 1024), tab, tab, tab, tab],
        out_specs=[_rows(ts, 1024), _rows(ts, 1024)],
        out_shape=[jax.ShapeDtypeStruct((s_, 1024), MXU_DTYPE)] * 2,
        compiler_params=_cparams(("parallel",)),
    )(qp, kp, kro, q_cos, b_up, b_dn)


def _mla_unprep(dqc, dkc, dvp, tabs, *, ts=256):
    s_ = dqc.shape[0]
    _, _, _, q_cos, k_cos, b_up, b_dn = tabs

    def body(dq_ref, dk_ref, dv_ref, qc, kc, bu, bd, dqo_ref, dkvo_ref, dkr_ref):
        c_, u_, d_ = qc[...], bu[...], bd[...]
        tot = jnp.zeros((ts, LANES), F32)
        for h in range(MLA_HEADS):
            sl = slice(h * LANES, (h + 1) * LANES)
            dqo_ref[:, sl] = _rope_t(dq_ref[:, sl], c_, u_, d_, 112, 16).astype(dqo_ref.dtype)
            dk = dk_ref[:, sl]
            dkvo_ref[:, sl] = dk.astype(dkvo_ref.dtype)
            tot = tot + dk
        dkvo_ref[:, 1024:2048] = dv_ref[...].astype(dkvo_ref.dtype)
        dkr_ref[...] = _rope_t(tot, kc[...], u_, d_, 112, 16).astype(dkr_ref.dtype)

    tab = _rows(ts, LANES)
    return pl.pallas_call(
        body, name="mla_unprep", grid=(s_ // ts,),
        in_specs=[_rows(ts, 1024), _rows(ts, 1024), _rows(ts, 1024), tab, tab, tab, tab],
        out_specs=[_rows(ts, 1024), _rows(ts, 2048), _rows(ts, LANES)],
        out_shape=[jax.ShapeDtypeStruct((s_, 1024), MXU_DTYPE), jax.ShapeDtypeStruct((s_, 2048), MXU_DTYPE),
                   jax.ShapeDtypeStruct((s_, LANES), MXU_DTYPE)],
        compiler_params=_cparams(("parallel",)),
    )(dqc, dkc, dvp, q_cos, k_cos, b_up, b_dn)


def _swa_unrope(dqa, dka, tabs, *, ts=256):
    s_ = dqa.shape[0]
    a_cos, a_up, a_dn = tabs[0], tabs[1], tabs[2]

    def body(dq_ref, dk_ref, ac, au, ad, dqo_ref, dko_ref):
        c_, u_, d_ = ac[...], au[...], ad[...]
        for h in range(SWA_HEADS):
            sl = slice(h * LANES, (h + 1) * LANES)
            dqo_ref[:, sl] = _rope_t(dq_ref[:, sl], c_, u_, d_, 96, 32).astype(dqo_ref.dtype)
        for h in range(SWA_KV_HEADS):
            sl = slice(h * LANES, (h + 1) * LANES)
            dko_ref[:, sl] = _rope_t(dk_ref[:, sl], c_, u_, d_, 96, 32).astype(dko_ref.dtype)

    tab = _rows(ts, LANES)
    return pl.pallas_call(
        body, name="swa_unrope", grid=(s_ // ts,),
        in_specs=[_rows(ts, 1024), _rows(ts, 256), tab, tab, tab],
        out_specs=[_rows(ts, 1024), _rows(ts, 256)],
        out_shape=[jax.ShapeDtypeStruct((s_, 1024), MXU_DTYPE), jax.ShapeDtypeStruct((s_, 256), MXU_DTYPE)],
        compiler_params=_cparams(("parallel",)),
    )(dqa, dka, a_cos, a_up, a_dn)


def _attn_out_gate(oa, ob, woa, wob, p, *, ts=512):
    s_ = p.shape[0]

    def body(oa_ref, ob_ref, wa_ref, wb_ref, ga_ref, gb_ref, ta_ref, tb_ref, y_ref):
        ta = jnp.dot(oa_ref[...], wa_ref[...], preferred_element_type=F32)
        tb = jnp.dot(ob_ref[...], wb_ref[...], preferred_element_type=F32)
        ta_ref[...] = ta
        tb_ref[...] = tb
        y_ref[...] = (_sigmoid(ga_ref[...]) * ta + _sigmoid(gb_ref[...]) * tb).astype(y_ref.dtype)

    w = _const(1024, 1024)
    return pl.pallas_call(
        body, name="attn_out_gate", grid=(s_ // ts,),
        in_specs=[_rows(ts, 1024), _rows(ts, 1024), w, w, _rows(ts, 1024, P_GA // 1024), _rows(ts, 1024, P_GB // 1024)],
        out_specs=[_rows(ts, 1024)] * 3,
        out_shape=[jax.ShapeDtypeStruct((s_, 1024), F32)] * 2 + [jax.ShapeDtypeStruct((s_, 1024), MXU_DTYPE)],
        compiler_params=_cparams(("parallel",)),
    )(oa, ob, woa, wob, p, p)


def _d_y_gate(dx1b, wout, p, ta, tb, *, ts=512):
    s_ = p.shape[0]

    def body(dx_ref, w_ref, ga_ref, gb_ref, ta_ref, tb_ref, dta_ref, dtb_ref, dg_ref):
        d = lax.dot_general(dx_ref[...], w_ref[...], NT, preferred_element_type=F32)
        sa, sb = _sigmoid(ga_ref[...]), _sigmoid(gb_ref[...])
        dta_ref[...] = (d * sa).astype(dta_ref.dtype)
        dtb_ref[...] = (d * sb).astype(dtb_ref.dtype)
        dg_ref[:, 0:1024] = (d * ta_ref[...] * (sa * (1.0 - sa))).astype(dg_ref.dtype)
        dg_ref[:, 1024:2048] = (d * tb_ref[...] * (sb * (1.0 - sb))).astype(dg_ref.dtype)

    return pl.pallas_call(
        body, name="d_y_gate", grid=(s_ // ts,),
        in_specs=[_rows(ts, 1024), _const(1024, 1024), _rows(ts, 1024, P_GA // 1024), _rows(ts, 1024, P_GB // 1024),
                  _rows(ts, 1024), _rows(ts, 1024)],
        out_specs=[_rows(ts, 1024), _rows(ts, 1024), _rows(ts, 2048)],
        out_shape=[jax.ShapeDtypeStruct((s_, 1024), MXU_DTYPE)] * 2 + [jax.ShapeDtypeStruct((s_, 2048), MXU_DTYPE)],
        compiler_params=_cparams(("parallel",)),
    )(dx1b, wout, p, p, ta, tb)


FF_TILE = D_FF // 2


def _ffn_in_act(x1, g, wgu_t, *, tm=512):
    s_ = x1.shape[0]

    def body(x_ref, g_ref, w_ref, h_ref, gu_ref, a_ref):
        h = _rms(x_ref[...], g_ref[...]).astype(h_ref.dtype)
        h_ref[...] = h
        p = lax.dot_general(h, w_ref[...], NT, preferred_element_type=F32)
        gu_ref[...] = p
        gate = p[:, :FF_TILE]
        a_ref[...] = (gate * _sigmoid(gate) * p[:, FF_TILE:]).astype(a_ref.dtype)

    return pl.pallas_call(
        body, name="ffn_in", grid=(s_ // tm, 2),
        in_specs=[pl.BlockSpec((tm, D_MODEL), lambda i, j: (i, 0)), pl.BlockSpec((1, D_MODEL), lambda i, j: (0, 0)),
                  pl.BlockSpec((2 * FF_TILE, D_MODEL), lambda i, j: (j, 0))],
        out_specs=[pl.BlockSpec((tm, D_MODEL), lambda i, j: (i, 0)), pl.BlockSpec((tm, 2 * FF_TILE), lambda i, j: (i, j)),
                   pl.BlockSpec((tm, FF_TILE), lambda i, j: (i, j))],
        out_shape=[jax.ShapeDtypeStruct((s_, D_MODEL), MXU_DTYPE), jax.ShapeDtypeStruct((s_, 2 * D_FF), F32),
                   jax.ShapeDtypeStruct((s_, D_FF), MXU_DTYPE)],
        compiler_params=_cparams(("parallel", "arbitrary")),
    )(x1, g, wgu_t)


def _d_act_swiglu(dx2b, wd, gu, *, tm=512):
    s_ = dx2b.shape[0]

    def body(d_ref, w_ref, gu_ref, o_ref):
        da = lax.dot_general(d_ref[...], w_ref[...], NT, preferred_element_type=F32)
        g, u = gu_ref[:, :FF_TILE], gu_ref[:, FF_TILE:]
        sg = _sigmoid(g)
        o_ref[:, :FF_TILE] = (da * u * (sg * (1.0 + g * (1.0 - sg)))).astype(o_ref.dtype)
        o_ref[:, FF_TILE:] = (da * (g * sg)).astype(o_ref.dtype)

    gu_spec = pl.BlockSpec((tm, 2 * FF_TILE), lambda i, j: (i, j))
    return pl.pallas_call(
        body, name="d_act", grid=(s_ // tm, 2),
        in_specs=[pl.BlockSpec((tm, D_MODEL), lambda i, j: (i, 0)), pl.BlockSpec((FF_TILE, D_MODEL), lambda i, j: (j, 0)), gu_spec],
        out_specs=gu_spec, out_shape=jax.ShapeDtypeStruct((s_, 2 * D_FF), MXU_DTYPE),
        compiler_params=_cparams(("parallel", "parallel")),
    )(dx2b, wd, gu)


def _ffn_out_loss(act, wd, x1, g, tgt, *, ts=512):
    s_, c = x1.shape
    kk = act.shape[1]

    def body(a_ref, w_ref, x_ref, g_ref, t_ref, dx_ref, dxb_ref, dg_ref, lp_ref, tot_ref):
        v = x_ref[...] + jnp.dot(a_ref[...], w_ref[...], preferred_element_type=F32)
        r = lax.rsqrt(jnp.mean(v * v, axis=-1, keepdims=True) + EPS)
        xh = v * r
        gg = g_ref[...]
        e = xh * gg - t_ref[...]
        do = e * (1.0 / c)
        dxh = do * gg
        dx = r * (dxh - xh * jnp.mean(dxh * xh, axis=-1, keepdims=True))
        dx_ref[...] = dx
        dxb_ref[...] = dx.astype(dxb_ref.dtype)
        i = pl.program_id(0)

        @pl.when(i == 0)
        def _():
            dg_ref[...] = jnp.zeros(dg_ref.shape, F32)
            lp_ref[...] = jnp.zeros(lp_ref.shape, F32)

        dg_ref[...] += _sublane_sum(do * xh)
        lp_ref[...] += _sublane_sum(e * e)
        tot_ref[...] = jnp.full(tot_ref.shape, (0.5 / c) * jnp.sum(lp_ref[...]), F32)

    return pl.pallas_call(
        body, name="ffn_out_loss", grid=(s_ // ts,),
        in_specs=[_rows(ts, kk), _const(kk, c), _rows(ts, c), _const(1, c), _rows(ts, c)],
        out_specs=[_rows(ts, c), _rows(ts, c), _const(SUBLANES, c), _const(SUBLANES, c), _const(SUBLANES, LANES)],
        out_shape=[jax.ShapeDtypeStruct((s_, c), F32), jax.ShapeDtypeStruct((s_, c), MXU_DTYPE),
                   jax.ShapeDtypeStruct((SUBLANES, c), F32), jax.ShapeDtypeStruct((SUBLANES, c), F32),
                   jax.ShapeDtypeStruct((SUBLANES, LANES), F32)],
        compiler_params=_cparams(("arbitrary",)),
    )(act, wd, x1, g, tgt)


def _mla_bwd_prep(dob, o32, *, ts=256):
    s_ = dob.shape[0]

    def body(do_ref, o_ref, dob_ref, dl_ref):
        d = do_ref[...]
        dob_ref[...] = d.astype(dob_ref.dtype)
        prod = d * o_ref[...]
        for h in range(MLA_HEADS):
            dl_ref[h] = jnp.sum(prod[:, h * LANES:(h + 1) * LANES].T, axis=0, keepdims=True)

    return pl.pallas_call(
        body, name="mla_bwd_prep", grid=(s_ // ts,), in_specs=[_rows(ts, 1024), _rows(ts, 1024)],
        out_specs=[_rows(ts, 1024), pl.BlockSpec((MLA_HEADS, 1, ts), lambda i: (0, 0, i))],
        out_shape=[jax.ShapeDtypeStruct((s_, 1024), MXU_DTYPE), jax.ShapeDtypeStruct((MLA_HEADS, 1, s_), F32)],
        compiler_params=_cparams(("parallel",)),
    )(dob, o32)


SWA_T = 4 * BLOCK


SWA_W = SWA_GROUP * BLOCK


def _swa_masks(sb):
    kr = lax.broadcasted_iota(jnp.int32, (2 * BLOCK, SWA_W), 0)
    qc = jnp.bitwise_and(lax.broadcasted_iota(jnp.int32, (2 * BLOCK, SWA_W), 1), BLOCK - 1)
    band = jnp.logical_and(kr > qc, kr <= qc + BLOCK)
    first = jnp.logical_and(band, kr >= BLOCK)
    return band, jnp.logical_or(first, jnp.logical_and(band, sb > 0))


def _heads_to_rows(ref, rs):
    return jnp.concatenate([ref[rs, h * LANES:(h + 1) * LANES] for h in range(SWA_GROUP)], axis=0)


def _sink_row(sk_ref):
    return jnp.concatenate([sk_ref[0, h:h + 1, :] for h in range(SWA_GROUP)], axis=1) * LOG2E


def _swa_in_specs(rev, nsb):
    sbi = (lambda j: nsb - 1 - j) if rev else (lambda j: j)
    cur = pl.BlockSpec((SWA_T, LANES), lambda g, j: (sbi(j), g))
    prev = pl.BlockSpec((BLOCK, LANES), lambda g, j: (jnp.maximum(4 * sbi(j) - 1, 0), g))
    q = pl.BlockSpec((SWA_T, SWA_GROUP * LANES), lambda g, j: (sbi(j), g))
    sink = pl.BlockSpec((1, SUBLANES, LANES), lambda g, j: (g, 0, 0))
    lse = pl.BlockSpec((SWA_GROUP, 1, SWA_T), lambda g, j: (g, 0, sbi(j)))
    return q, cur, prev, sink, lse


def _swa_fwd(qa, ka, va, sink_b):
    s_ = qa.shape[0]
    nsb = s_ // SWA_T
    c2 = HEAD_DIM ** -0.5 * LOG2E

    def body(q_ref, kc_ref, kp_ref, vc_ref, vp_ref, sk_ref, o32_ref, o16_ref, lse_ref, kx, vx):
        kx[0:BLOCK, :] = kp_ref[...]
        kx[BLOCK:5 * BLOCK, :] = kc_ref[...]
        vx[0:BLOCK, :] = vp_ref[...]
        vx[BLOCK:5 * BLOCK, :] = vc_ref[...]
        band, band0 = _swa_masks(pl.program_id(1))
        sink2 = _sink_row(sk_ref)
        for b in range(4):
            rs = slice(b * BLOCK, (b + 1) * BLOCK)
            ks = slice(b * BLOCK, (b + 2) * BLOCK)
            st = lax.dot_general(kx[ks, :], _heads_to_rows(q_ref, rs), NT, preferred_element_type=F32) * c2
            st = jnp.where(band0 if b == 0 else band, st, -jnp.inf)
            m = jnp.maximum(jnp.max(st, axis=0, keepdims=True), sink2)
            pt = jnp.exp2(st - m)
            den = jnp.sum(pt, axis=0, keepdims=True) + jnp.exp2(sink2 - m)
            o = lax.dot_general((pt * (1.0 / den)).astype(MXU_DTYPE), vx[ks, :], TN, preferred_element_type=F32)
            lse = m + jnp.log2(den)
            for hh in range(SWA_GROUP):
                cs = slice(hh * LANES, (hh + 1) * LANES)
                o32_ref[rs, cs] = o[cs, :]
                o16_ref[rs, cs] = o[cs, :].astype(o16_ref.dtype)
                lse_ref[hh, :, rs] = lse[:, cs]

    q, cur, prev, sink, lse_spec = _swa_in_specs(False, nsb)
    return pl.pallas_call(
        body, name="swa_fwd", grid=(SWA_KV_HEADS, nsb), in_specs=[q, cur, prev, cur, prev, sink],
        out_specs=[q, q, lse_spec],
        out_shape=[jax.ShapeDtypeStruct((s_, SWA_HEADS * LANES), F32), jax.ShapeDtypeStruct((s_, SWA_HEADS * LANES), MXU_DTYPE),
                   jax.ShapeDtypeStruct((SWA_HEADS, 1, s_), F32)],
        scratch_shapes=[pltpu.VMEM((5 * BLOCK, LANES), MXU_DTYPE), pltpu.VMEM((5 * BLOCK, LANES), MXU_DTYPE)],
        compiler_params=_cparams(("parallel", "arbitrary")),
    )(qa, ka, ka, va, va, sink_b)


def _swa_bwd(qa, ka, va, sink_b, o32, do, lse):
    s_ = qa.shape[0]
    nsb = s_ // SWA_T
    scale = HEAD_DIM ** -0.5
    c2 = scale * LOG2E

    def body(q_ref, kc_ref, kp_ref, vc_ref, vp_ref, sk_ref, o_ref, do_ref, lse_ref,
             dq_ref, dk_ref, dv_ref, dsk_ref, kx, vx, kacc, vacc, kcar, vcar):
        j = pl.program_id(1)
        kx[0:BLOCK, :] = kp_ref[...]
        kx[BLOCK:5 * BLOCK, :] = kc_ref[...]
        vx[0:BLOCK, :] = vp_ref[...]
        vx[BLOCK:5 * BLOCK, :] = vc_ref[...]
        band, band0 = _swa_masks(nsb - 1 - j)
        kacc[...] = jnp.zeros(kacc.shape, F32)
        vacc[...] = jnp.zeros(vacc.shape, F32)

        @pl.when(j == 0)
        def _():
            kcar[...] = jnp.zeros(kcar.shape, F32)
            vcar[...] = jnp.zeros(vcar.shape, F32)
            dsk_ref[...] = jnp.zeros(dsk_ref.shape, F32)

        sink2 = _sink_row(sk_ref)
        dsink = jnp.zeros((1, SWA_W), F32)
        for b in range(4):
            rs = slice(b * BLOCK, (b + 1) * BLOCK)
            ks = slice(b * BLOCK, (b + 2) * BLOCK)
            q, k2, v2 = _heads_to_rows(q_ref, rs), kx[ks, :], vx[ks, :]
            d = _heads_to_rows(do_ref, rs)
            delta = jnp.sum((d * _heads_to_rows(o_ref, rs)).T, axis=0, keepdims=True)
            l2 = jnp.concatenate([lse_ref[hh, :, rs] for hh in range(SWA_GROUP)], axis=1)
            st = lax.dot_general(k2, q, NT, preferred_element_type=F32) * c2
            pt = jnp.exp2(jnp.where(band0 if b == 0 else band, st, -jnp.inf) - l2)
            db = d.astype(MXU_DTYPE)
            dst = (pt * (lax.dot_general(v2, db, NT, preferred_element_type=F32) - delta) * scale).astype(MXU_DTYPE)
            dq = lax.dot_general(dst, k2, TN, preferred_element_type=F32)
            for hh in range(SWA_GROUP):
                dq_ref[rs, hh * LANES:(hh + 1) * LANES] = dq[hh * LANES:(hh + 1) * LANES, :]
            kacc[ks, :] += jnp.dot(dst, q, preferred_element_type=F32)
            vacc[ks, :] += jnp.dot(pt.astype(MXU_DTYPE), db, preferred_element_type=F32)
            dsink = dsink - jnp.exp2(sink2 - l2) * delta
        for hh in range(SWA_GROUP):
            tot = jnp.sum(dsink[:, hh * LANES:(hh + 1) * LANES], axis=1, keepdims=True)
            dsk_ref[0, hh:hh + 1, :] += jnp.broadcast_to(tot, (1, LANES))

        dk_ref[0:3 * BLOCK, :] = kacc[BLOCK:4 * BLOCK, :]
        dk_ref[3 * BLOCK:4 * BLOCK, :] = kacc[4 * BLOCK:5 * BLOCK, :] + kcar[...]
        dv_ref[0:3 * BLOCK, :] = vacc[BLOCK:4 * BLOCK, :].astype(dv_ref.dtype)
        dv_ref[3 * BLOCK:4 * BLOCK, :] = (vacc[4 * BLOCK:5 * BLOCK, :] + vcar[...]).astype(dv_ref.dtype)
        kcar[...] = kacc[0:BLOCK, :]
        vcar[...] = vacc[0:BLOCK, :]

    q, cur, prev, sink, lse_spec = _swa_in_specs(True, nsb)
    return pl.pallas_call(
        body, name="swa_bwd", grid=(SWA_KV_HEADS, nsb),
        in_specs=[q, cur, prev, cur, prev, sink, q, q, lse_spec],
        out_specs=[q, cur, cur, sink],
        out_shape=[jax.ShapeDtypeStruct((s_, SWA_HEADS * LANES), F32), jax.ShapeDtypeStruct((s_, SWA_KV_HEADS * LANES), F32),
                   jax.ShapeDtypeStruct((s_, SWA_KV_HEADS * LANES), MXU_DTYPE),
                   jax.ShapeDtypeStruct((SWA_KV_HEADS, SUBLANES, LANES), F32)],
        scratch_shapes=[pltpu.VMEM((5 * BLOCK, LANES), MXU_DTYPE), pltpu.VMEM((5 * BLOCK, LANES), MXU_DTYPE),
                        pltpu.VMEM((5 * BLOCK, LANES), F32), pltpu.VMEM((5 * BLOCK, LANES), F32),
                        pltpu.VMEM((BLOCK, LANES), F32), pltpu.VMEM((BLOCK, LANES), F32)],
        compiler_params=_cparams(("arbitrary", "arbitrary")),
    )(qa, ka, ka, va, va, sink_b, o32, do, lse)


MLA_T = 512
MLA_FWD_GROUP = 4
MLA_BWD_GROUP = 2


def _mla_specs(s_, t, group):
    w = group * LANES
    qs = pl.BlockSpec((t, w), lambda g, i: (i, g))
    kv = pl.BlockSpec((s_, w), lambda g, i: (0, g))
    row = pl.BlockSpec((group, 1, t), lambda g, i: (g, 0, i))
    return qs, kv, row


def _causal_scores_t(k, q, t, c2, masked):
    st = lax.dot_general(k, q, NT, preferred_element_type=F32) * c2
    if masked:
        kr = lax.broadcasted_iota(jnp.int32, (t, t), 0)
        qc = lax.broadcasted_iota(jnp.int32, (t, t), 1)
        st = jnp.where(kr <= qc, st, -jnp.inf)
    return st


def _mla_fwd(qc, kc, vp):
    s_ = qc.shape[0]
    t = min(MLA_T, s_)
    c2 = MLA_QK ** -0.5 * LOG2E
    grp = MLA_FWD_GROUP

    def body(q_ref, k_ref, v_ref, o32_ref, o16_ref, lse_ref, m_s, acc_s):
        qi = pl.program_id(1)
        m_s[...] = jnp.full(m_s.shape, -jnp.inf, F32)
        acc_s[...] = jnp.zeros(acc_s.shape, F32)
        ones_lane = lax.broadcasted_iota(jnp.int32, (t, LANES), 1) == MLA_V

        def step(ki, masked):
            off = pl.multiple_of(ki * t, t)
            for g in range(grp):
                cs = slice(g * LANES, (g + 1) * LANES)
                st = _causal_scores_t(k_ref[pl.ds(off, t), cs], q_ref[:, cs], t, c2, masked)
                m_old = m_s[g]
                m_new = jnp.maximum(m_old, jnp.max(st, axis=0, keepdims=True))
                alpha = jnp.exp2(m_old - m_new)
                pt = jnp.exp2(st - m_new).astype(MXU_DTYPE)
                v = v_ref[pl.ds(off, t), cs]
                v = jnp.where(ones_lane, jnp.ones((), v.dtype), v)
                acc_s[g] = alpha * acc_s[g] + lax.dot_general(v, pt, TN, preferred_element_type=F32)
                m_s[g] = m_new

        def full_block(ki, carry):
            step(ki, False)
            return carry

        lax.fori_loop(0, qi, full_block, 0)
        step(qi, True)
        for g in range(grp):
            cs = slice(g * LANES, (g + 1) * LANES)
            acc = acc_s[g]
            l = acc[MLA_V:MLA_V + 1, :]
            o = (acc * (1.0 / l)).T
            o32_ref[:, cs] = o
            o16_ref[:, cs] = o.astype(o16_ref.dtype)
            lse_ref[g] = m_s[g] + jnp.log2(l)

    qs, kv, row = _mla_specs(s_, t, grp)
    return pl.pallas_call(
        body, name="mla_fwd", grid=(MLA_HEADS // grp, s_ // t), in_specs=[qs, kv, kv], out_specs=[qs, qs, row],
        out_shape=[jax.ShapeDtypeStruct((s_, MLA_HEADS * LANES), F32), jax.ShapeDtypeStruct((s_, MLA_HEADS * LANES), MXU_DTYPE),
                   jax.ShapeDtypeStruct((MLA_HEADS, 1, s_), F32)],
        scratch_shapes=[pltpu.VMEM((grp, 1, t), F32), pltpu.VMEM((grp, LANES, t), F32)],
        compiler_params=_cparams(("parallel", "arbitrary")),
    )(qc, kc, vp)


def _mla_bwd(qc, kc, vp, dob, lse, delta):
    s_ = qc.shape[0]
    t = min(MLA_T, s_)
    scale = MLA_QK ** -0.5
    c2 = scale * LOG2E
    grp = MLA_BWD_GROUP

    def body(q_ref, do_ref, lse_ref, dl_ref, k_ref, v_ref, dq_ref, dk_ref, dv_ref, dqt_s):
        qi = pl.program_id(1)

        @pl.when(qi == 0)
        def _():
            dk_ref[...] = jnp.zeros(dk_ref.shape, F32)
            dv_ref[...] = jnp.zeros(dv_ref.shape, F32)

        dqt_s[...] = jnp.zeros(dqt_s.shape, F32)

        def step(ki, masked):
            off = pl.multiple_of(ki * t, t)
            for g in range(grp):
                cs = slice(g * LANES, (g + 1) * LANES)
                q, d, k = q_ref[:, cs], do_ref[:, cs], k_ref[pl.ds(off, t), cs]
                pt = jnp.exp2(_causal_scores_t(k, q, t, c2, masked) - lse_ref[g])
                dpt = lax.dot_general(v_ref[pl.ds(off, t), cs], d, NT, preferred_element_type=F32)
                dst = (pt * (dpt - dl_ref[g]) * scale).astype(MXU_DTYPE)
                dv_ref[pl.ds(off, t), cs] += jnp.dot(pt.astype(MXU_DTYPE), d, preferred_element_type=F32)
                dk_ref[pl.ds(off, t), cs] += jnp.dot(dst, q, preferred_element_type=F32)
                dqt_s[g] += lax.dot_general(k, dst, TN, preferred_element_type=F32)

        def full_block(ki, carry):
            step(ki, False)
            return carry

        lax.fori_loop(0, qi, full_block, 0)
        step(qi, True)
        for g in range(grp):
            dq_ref[:, g * LANES:(g + 1) * LANES] = dqt_s[g].T

    qs, kv, row = _mla_specs(s_, t, grp)
    shp = jax.ShapeDtypeStruct((s_, MLA_HEADS * LANES), F32)
    return pl.pallas_call(
        body, name="mla_bwd", grid=(MLA_HEADS // grp, s_ // t), in_specs=[qs, qs, row, row, kv, kv],
        out_specs=[qs, kv, kv], out_shape=[shp, shp, shp], scratch_shapes=[pltpu.VMEM((grp, LANES, t), F32)],
        compiler_params=_cparams(("parallel", "arbitrary")),
    )(qc, dob, lse, delta, kc, vp)


def _pad_heads(w, nh, hd, axis):
    shp = w.shape
    w = w.reshape(shp[:axis] + (nh, hd) + shp[axis + 1:])
    pad = [(0, 0)] * w.ndim
    pad[axis + 1] = (0, LANES - hd)
    w = jnp.pad(w, pad)
    return w.reshape(shp[:axis] + (nh * LANES,) + shp[axis + 1:])


def _unpad_heads(w, nh, hd, axis):
    shp = w.shape
    w = w.reshape(shp[:axis] + (nh, LANES) + shp[axis + 1:])
    w = lax.slice_in_dim(w, 0, hd, axis=axis + 1)
    return w.reshape(shp[:axis] + (nh * hd,) + shp[axis + 1:])


PACK_W = 1024
ROW_TILE = 16
FULL_SHAPE = dict(w_in=(1024, 3488), w_uq=(384, 768), w_ukv=(256, 1024), w_o_swa=(512, 1024), w_o_mla=(512, 1024),
                  w_out=(1024, 1024), w_gate=(1024, 2816), w_up=(1024, 2816), w_down=(2816, 1024))
BIG = tuple(FULL_SHAPE)
ROW_SHARDED = ("w_out", "w_down")
W_IN_COLS = FULL_SHAPE["w_in"][1] // N_DEV
W_IN_ROWS = -(-W_IN_COLS // ROW_TILE) * ROW_TILE
FF_COLS = D_FF // N_DEV
OUT_ROWS = D_MODEL // N_DEV
SMALL_ROW0 = W_IN_ROWS + OUT_ROWS
SMALL_FLAT = (("w_uq", 0, 36), ("w_ukv", 48, 32), ("w_o_swa", 80, 64), ("w_o_mla", 144, 64))
SMALL_ROWS = 208
EARLY_ROWS = SMALL_ROW0 + SMALL_ROWS
LATE_ROWS = 3 * FF_COLS
PACK_ROWS = EARLY_ROWS + LATE_ROWS


def _shard_shape(n):
    r, c = FULL_SHAPE[n]
    return (r // N_DEV, c) if n in ROW_SHARDED else (r, c // N_DEV)


def _wire_pack(sh, dtype):
    c = lambda n: sh[n].astype(dtype)
    rows = [jnp.pad(c("w_in").T, ((0, W_IN_ROWS - W_IN_COLS), (0, 0))), c("w_out")]
    for n, _, r in SMALL_FLAT:
        rows.append(jnp.pad(c(n).reshape(r, PACK_W), ((0, -r % ROW_TILE), (0, 0))))
    return jnp.concatenate(rows + [c("w_gate").T, c("w_up").T, c("w_down")], 0)


MID_ROWS = OUT_ROWS + SMALL_ROWS


def _mid_unpack(p):
    out = dict(w_out=p[0:OUT_ROWS])
    for n, off, r in SMALL_FLAT:
        out[n] = p[OUT_ROWS + off:OUT_ROWS + off + r].reshape(_shard_shape(n))
    return out


def _w_in_row_maps():
    sp = lambda col: (col // W_IN_COLS) * W_IN_ROWS + col % W_IN_COLS
    fwd = np.full((P_W,), -1, np.int64)

    def put(t0, c0, n):
        fwd[t0:t0 + n] = [sp(c) for c in range(c0, c0 + n)]

    put(P_GA, IN_OFF[6], D_MODEL)
    put(P_GB, IN_OFF[7], D_MODEL)
    for h in range(SWA_HEADS):
        put(P_Q + LANES * h, IN_OFF[0] + HEAD_DIM * h, HEAD_DIM)
    put(P_QLAT, IN_OFF[3], Q_LORA)
    put(P_KR + KR_LANE, IN_OFF[5], MLA_ROPE)
    for h in range(SWA_KV_HEADS):
        put(P_K + LANES * h, IN_OFF[1] + HEAD_DIM * h, HEAD_DIM)
        put(P_V + LANES * h, IN_OFF[2] + HEAD_DIM * h, HEAD_DIM)
    put(P_KVLAT, IN_OFF[4], KV_LORA)
    inv = np.full((N_DEV * W_IN_ROWS,), -1, np.int64)
    inv[fwd[fwd >= 0]] = np.nonzero(fwd >= 0)[0]
    return fwd, inv


def _take_rows(src, idx, *, name):
    n_out, n_src, width = len(idx), src.shape[0], src.shape[1]
    assert n_out % BLOCK == 0 and n_src % BLOCK == 0
    n_tiles = n_out // BLOCK
    blocks = [sorted({int(v) // BLOCK for v in idx[i * BLOCK:(i + 1) * BLOCK] if v >= 0}) for i in range(n_tiles)]
    k_max = max(1, max(len(b) for b in blocks))
    tab = np.zeros((n_tiles, k_max), np.int32)
    sel = np.zeros((n_tiles, k_max, BLOCK, BLOCK), np.float32)
    for i, blks in enumerate(blocks):
        for m, b in enumerate(blks):
            tab[i, m] = b
            for r in range(BLOCK):
                v = int(idx[i * BLOCK + r])
                if v >= 0 and v // BLOCK == b:
                    sel[i, m, r, v % BLOCK] = 1.0

    def body(tab_ref, sel_ref, *refs):
        o_ref = refs[k_max]
        acc = jnp.dot(sel_ref[0, 0], refs[0][...], preferred_element_type=F32)
        for m in range(1, k_max):
            acc = acc + jnp.dot(sel_ref[0, m], refs[m][...], preferred_element_type=F32)
        o_ref[...] = acc.astype(o_ref.dtype)

    def src_spec(m):
        return pl.BlockSpec((BLOCK, width), lambda i, t: (t[i * k_max + m], 0))

    return pl.pallas_call(
        body, name=name,
        grid_spec=pltpu.PrefetchScalarGridSpec(
            num_scalar_prefetch=1, grid=(n_tiles,),
            in_specs=[pl.BlockSpec((1, k_max, BLOCK, BLOCK), lambda i, t: (i, 0, 0, 0))] + [src_spec(m) for m in range(k_max)],
            out_specs=pl.BlockSpec((BLOCK, width), lambda i, t: (i, 0))),
        out_shape=jax.ShapeDtypeStruct((n_out, width), src.dtype),
        compiler_params=_cparams(("parallel",)),
    )(jnp.asarray(tab.reshape(-1)), jnp.asarray(sel, src.dtype), *([src] * k_max))


def _w_in_operand(win_g):
    return _take_rows(win_g.reshape(N_DEV * W_IN_ROWS, PACK_W), _w_in_row_maps()[0], name="w_in_rows")


def _mid_operands(wout_g, small_g):
    def full(n, off, r):
        a = small_g[:, off:off + r].reshape((N_DEV,) + _shard_shape(n))
        return jnp.moveaxis(a, 0, 1).reshape(FULL_SHAPE[n])

    w = {n: full(n, off, r) for n, off, r in SMALL_FLAT}
    ukv = w["w_ukv"].reshape(KV_LORA, MLA_HEADS, MLA_NOPE + MLA_V)
    return dict(
        wout=wout_g.reshape(D_MODEL, D_MODEL),
        wuq=_pad_heads(w["w_uq"], MLA_HEADS, MLA_QK, 1),
        wuk=_pad_heads(ukv[:, :, :MLA_NOPE].reshape(KV_LORA, -1), MLA_HEADS, MLA_NOPE, 1),
        wuv=_pad_heads(ukv[:, :, MLA_NOPE:].reshape(KV_LORA, -1), MLA_HEADS, MLA_V, 1),
        woa=_pad_heads(w["w_o_swa"], SWA_HEADS, HEAD_DIM, 0),
        wob=_pad_heads(w["w_o_mla"], MLA_HEADS, MLA_V, 0),
    )


def _mid_grad_pack(g):
    uk = _unpad_heads(g["wukv"][:, :1024], MLA_HEADS, MLA_NOPE, 1).reshape(KV_LORA, MLA_HEADS, MLA_NOPE)
    uv = _unpad_heads(g["wukv"][:, 1024:], MLA_HEADS, MLA_V, 1).reshape(KV_LORA, MLA_HEADS, MLA_V)
    w = dict(w_uq=_unpad_heads(g["wuq"], MLA_HEADS, MLA_QK, 1), w_ukv=jnp.concatenate([uk, uv], 2).reshape(KV_LORA, -1),
             w_o_swa=_unpad_heads(g["woa"], SWA_HEADS, HEAD_DIM, 0), w_o_mla=_unpad_heads(g["wob"], MLA_HEADS, MLA_V, 0))

    def flat(n, r):
        rr, cc = FULL_SHAPE[n]
        a = jnp.moveaxis(w[n].reshape(rr, N_DEV, cc // N_DEV), 1, 0).reshape(N_DEV, r, PACK_W)
        return jnp.pad(a, ((0, 0), (0, -r % ROW_TILE), (0, 0))).astype(WIRE_DTYPE)

    return jnp.concatenate([g["wout"].reshape(N_DEV, OUT_ROWS, PACK_W)] + [flat(n, r) for n, _, r in SMALL_FLAT], 1)


def _w_in_grad_chunks(g_win_t):
    return _take_rows(g_win_t, _w_in_row_maps()[1], name="dw_in_rows").reshape(N_DEV, W_IN_ROWS, PACK_W)


def _local_step(x, tgt, win_t, small, weights, grads):
    s_ = x.shape[0]
    tabs = _rope_tables(s_)
    sink_b = jnp.broadcast_to(small["swa_sinks"].reshape(SWA_KV_HEADS, SWA_GROUP, 1), (SWA_KV_HEADS, SWA_GROUP, LANES))
    sink_b = jnp.pad(sink_b, ((0, 0), (0, SUBLANES - SWA_GROUP), (0, 0)))

    h, p = _norm_mm(x, small["mix_norm_g"], win_t, name="proj_in", tn=2176, tm=1024)
    qa, ka, va, cq, ckv, kro = _attn_prep(p, small["q_norm_g"], small["kv_norm_g"], tabs)
    ops = weights.mid(cq)
    oa32, oa16, lse_a = _swa_fwd(qa, ka, va, sink_b)
    qp = _mm(cq, ops["wuq"], "nn", name="mla_q_up", tm=1024, tn=1024)
    kp = _mm(ckv, ops["wuk"], "nn", name="mla_k_up", tm=1024, tn=1024)
    vp = _mm(ckv, ops["wuv"], "nn", name="mla_v_up", tm=1024, tn=1024, out_dtype=MXU_DTYPE)
    qc, kc = _mla_prep(qp, kp, kro, tabs)
    ob32, ob16, lse_b = _mla_fwd(qc, kc, vp)
    ta, tb, y = _attn_out_gate(oa16, ob16, ops["woa"], ops["wob"], p)
    x1 = _mm(y, ops["wout"], "nn", name="out_proj", add=x, tm=1024, tn=1024)
    wgu_t, wd = weights.late(x1)
    h2, gu, act = _ffn_in_act(x1, small["ffn_norm_g"], wgu_t)

    dx2, dx2b, dg3, _, tot = _ffn_out_loss(act, wd, x1, small["final_norm_g"].reshape(1, D_MODEL), tgt)
    g = {}
    g_wd = _mm(act, dx2b, "tn", name="dw_down", tm=1408, tn=1024, tk=1024, out_dtype=WIRE_DTYPE)
    dgu = _d_act_swiglu(dx2b, wd, gu)
    g_wgu = _mm(dgu, h2, "tn", name="dw_ffn_in", tm=1408, tn=1024, tk=1024, out_dtype=WIRE_DTYPE)
    token = grads.late(g_wgu, g_wd)
    dx1, dx1b, dg2 = _mm_norm_bwd(dgu, wgu_t, x1, small["ffn_norm_g"] + token[0:1, 0:1], dx2, name="d_h2", tk=2816)
    g["wout"] = _mm(y, dx1b, "tn", name="dw_out", tm=1024, tn=1024, tk=1024, out_dtype=WIRE_DTYPE)
    dta, dtb, dgab = _d_y_gate(dx1b, ops["wout"], p, ta, tb)
    doa = _mm(dta, ops["woa"], "nt", name="d_oa", tm=1024, tn=1024)
    g["woa"] = _mm(oa16, dta, "tn", name="dw_o_swa", tm=1024, tn=1024, tk=1024)
    dob = _mm(dtb, ops["wob"], "nt", name="d_ob", tm=1024, tn=1024)
    g["wob"] = _mm(ob16, dtb, "tn", name="dw_o_mla", tm=1024, tn=1024, tk=1024)
    dob16, delta_b = _mla_bwd_prep(dob, ob32)
    dqc, dkc, dvp = _mla_bwd(qc, kc, vp, dob16, lse_b, delta_b)
    dqp, dkv, dkr = _mla_unprep(dqc, dkc, dvp, tabs)
    dcq = _mm(dqp, ops["wuq"], "nt", name="d_cq", tn=Q_LORA)
    g["wuq"] = _mm(cq, dqp, "tn", name="dw_uq", tm=Q_LORA, tn=1024, tk=512)
    dckv = _mm(dkv, jnp.concatenate([ops["wuk"], ops["wuv"]], 1), "nt", name="d_ckv", tn=KV_LORA)
    g["wukv"] = _mm(ckv, dkv, "tn", name="dw_ukv", tm=KV_LORA, tn=1024, tk=512)
    token = grads.mid(g)
    _, dqlat, dgq = _norm_bwd(p, small["q_norm_g"] + token[0:1, 0:1], dcq, None, name="qnorm_bwd", x_cb=P_QLAT // Q_LORA)
    _, dkvlat, dgkv = _norm_bwd(p, small["kv_norm_g"], dckv, None, name="kvnorm_bwd", x_cb=P_KVLAT // KV_LORA)
    dqa, dka, dva, dsk = _swa_bwd(qa, ka, va, sink_b, oa32, doa, lse_a)
    dq_raw, dk_raw = _swa_unrope(dqa, dka, tabs)
    dp = jnp.concatenate([dgab, dq_raw, dqlat, dkr, dk_raw, dva, dkvlat], 1)
    token = grads.last(_mm(dp, h, "tn", name="dw_in", tm=2176, tn=1024, tk=1024, out_dtype=WIRE_DTYPE))
    gx, _, dg1 = _mm_norm_bwd(dp, win_t, x, small["mix_norm_g"], dx1, name="d_h", tk=2176, after=token)

    sm = dict(mix_norm_g=dg1, ffn_norm_g=dg2, final_norm_g=dg3, q_norm_g=dgq, kv_norm_g=dgkv,
              swa_sinks=dsk[:, :SWA_GROUP, 0].reshape(1, SWA_HEADS))
    return tot, gx, sm


MESH = pl.DeviceIdType.MESH
ANY = pl.BlockSpec(memory_space=pl.ANY)


def _position():
    return lax.axis_index("x"), lax.axis_index("y"), lax.axis_index("c")


def _all_gather(block, pieces, shapes, *, name):
    n_out = len(shapes)
    n_rows = sum(p[3] for p in pieces)

    def body(x_ref, *refs):
        outs, (send_sems, recv_sems, local_sem) = refs[:n_out], refs[n_out:]
        x, y, c = _position()
        me, sibling = (x, y, c), (x, y, 1 - c)
        chips = [(1 - x, y), (x, 1 - y), (1 - x, 1 - y)]

        def dst(piece, blk):
            arr, lead, _, _ = piece
            return outs[arr].at[lead(4 * blk[0] + 2 * blk[1] + blk[2])]

        def own(piece):
            return x_ref.at[pl.ds(piece[2], piece[3])]

        def copies(k, blk, to, from_input):
            return [pltpu.make_async_remote_copy(
                src_ref=own(p) if from_input else dst(p, blk), dst_ref=dst(p, blk), send_sem=send_sems.at[k],
                recv_sem=recv_sems.at[k], device_id=to, device_id_type=MESH) for p in pieces]

        gathered_rows = x_ref.at[pl.ds(0, n_rows)]

        def whole_block(k):
            return pltpu.make_async_remote_copy(src_ref=gathered_rows, dst_ref=gathered_rows, send_sem=send_sems.at[k],
                                                recv_sem=recv_sems.at[k], device_id=me, device_id_type=MESH)

        for p in pieces:
            pltpu.make_async_copy(own(p), dst(p, me), local_sem).start()
        for cp in copies(0, me, sibling, True):
            cp.start()
        for j, chip in enumerate(chips):
            for cp in copies(1 + j, me, (*chip, c), True):
                cp.start()
        for j, chip in enumerate(chips):
            whole_block(1 + j).wait_recv()
            for cp in copies(4 + j, (*chip, c), sibling, False):
                cp.start()
        whole_block(0).wait_recv()
        for j in range(3):
            whole_block(4 + j).wait_recv()
        for k in range(7):
            whole_block(k).wait_send()
        pltpu.make_async_copy(gathered_rows, gathered_rows, local_sem).wait()

    return pl.pallas_call(
        body, name=name, out_shape=[jax.ShapeDtypeStruct(s, block.dtype) for s in shapes], in_specs=[ANY],
        out_specs=[ANY] * n_out,
        scratch_shapes=[pltpu.SemaphoreType.DMA((7,)), pltpu.SemaphoreType.DMA((7,)), pltpu.SemaphoreType.DMA],
    )(block)


HBM = pl.BlockSpec(memory_space=pltpu.HBM)
SEM = pl.BlockSpec(memory_space=pltpu.SEMAPHORE)
TILE_DEVS = FF_TILE // FF_COLS
GU_SHAPE = (2, 2, TILE_DEVS, FF_COLS, PACK_W)


def _gate_slab(d):
    return (d // TILE_DEVS, 0, d % TILE_DEVS)


def _up_slab(d):
    return (d // TILE_DEVS, 1, d % TILE_DEVS)
D_SHAPE = (N_DEV, FF_COLS, PACK_W)
LAND_SHAPE = (N_DEV, LATE_ROWS, PACK_W)


def _split_params():
    return pltpu.CompilerParams(has_side_effects=pltpu.SideEffectType.DATAFLOW_SIDE_EFFECTING)


def _peer(x, y, c, k):
    return ((1 - x) if k & 4 else x, (1 - y) if k & 2 else y, (1 - c) if k & 1 else c)


def _empty_hbm(shape, dtype):
    return pltpu.with_memory_space_constraint(lax.empty(shape, dtype), pltpu.HBM)


def _wait_all(rows, send_sems, recv_sems, me):
    for k in range(N_DEV - 1):
        cp = pltpu.make_async_remote_copy(src_ref=rows, dst_ref=rows, send_sem=send_sems.at[k], recv_sem=recv_sems.at[k],
                                          device_id=me, device_id_type=MESH)
        cp.wait_send()
        cp.wait_recv()


def _token_shape():
    return jax.ShapeDtypeStruct((SUBLANES, LANES), F32)


def _gather_start(pack, row0, pieces, shapes, *, name):
    n = len(shapes)

    def body(*refs):
        p_ref, bufs, send_sems, recv_sems, token = refs[0], refs[1:1 + n], refs[1 + n], refs[2 + n], refs[-1]
        x, y, c = _position()
        me = 4 * x + 2 * y + c
        for k in range(1, N_DEV):
            off = row0
            for buf, lead, rows in pieces:
                pltpu.make_async_remote_copy(
                    src_ref=p_ref.at[pl.ds(off, rows)], dst_ref=bufs[buf].at[lead(me)], send_sem=send_sems.at[k - 1],
                    recv_sem=recv_sems.at[k - 1], device_id=_peer(x, y, c, k), device_id_type=MESH).start()
                off += rows
        token[...] = jnp.zeros_like(token)

    sems, dt = pltpu.SemaphoreType.DMA((N_DEV - 1,)), pack.dtype
    return pl.pallas_call(
        body, name=name,
        out_shape=(sems, sems, pltpu.HBM(pack.shape, dt)) + tuple(pltpu.HBM(s, dt) for s in shapes) + (_token_shape(),),
        in_specs=(HBM,) * (1 + n), out_specs=(SEM, SEM) + (HBM,) * (1 + n) + (pl.BlockSpec(memory_space=pltpu.VMEM),),
        input_output_aliases={i: 2 + i for i in range(1 + n)}, compiler_params=_split_params(),
    )(pltpu.with_memory_space_constraint(pack, pltpu.HBM), *[_empty_hbm(s, dt) for s in shapes])


def _gather_wait(started, row0, n_rows, after, *, name):
    send_sems, recv_sems, pack, *bufs = started[:-1]
    n = len(bufs)

    def body(*refs):
        _wait_all(refs[0].at[pl.ds(row0, n_rows)], refs[1 + n], refs[2 + n], _position())

    outs = pl.pallas_call(
        body, name=name, out_shape=tuple(pltpu.HBM(a.shape, a.dtype) for a in (pack, *bufs)),
        in_specs=(HBM,) * (1 + n) + (SEM, SEM, ANY), out_specs=(HBM,) * (1 + n),
        input_output_aliases={i: i for i in range(1 + n)}, compiler_params=_split_params(),
    )(pack, *bufs, send_sems, recv_sems, after)
    return outs[0], outs[1:]


def _scatter_start(srcs, pieces, *, name):
    n = len(srcs)
    land_shape = (N_DEV, sum(p[2] for p in pieces), PACK_W)

    def body(*refs):
        src_refs, land_ref, send_sems, recv_sems, token = refs[:n], refs[n], refs[n + 1], refs[n + 2], refs[-1]
        x, y, c = _position()
        me = 4 * x + 2 * y + c
        for k in range(1, N_DEV):
            px, py, pc = _peer(x, y, c, k)
            off = 0
            for si, lead, rows in pieces:
                pltpu.make_async_remote_copy(
                    src_ref=src_refs[si].at[lead(4 * px + 2 * py + pc)], dst_ref=land_ref.at[me, pl.ds(off, rows)],
                    send_sem=send_sems.at[k - 1], recv_sem=recv_sems.at[k - 1], device_id=(px, py, pc),
                    device_id_type=MESH).start()
                off += rows
        token[...] = jnp.zeros_like(token)

    sems, dt = pltpu.SemaphoreType.DMA((N_DEV - 1,)), srcs[0].dtype
    return pl.pallas_call(
        body, name=name,
        out_shape=(sems, sems) + tuple(pltpu.HBM(a.shape, dt) for a in srcs) + (pltpu.HBM(land_shape, dt), _token_shape()),
        in_specs=(HBM,) * (n + 1), out_specs=(SEM, SEM) + (HBM,) * (n + 1) + (pl.BlockSpec(memory_space=pltpu.VMEM),),
        input_output_aliases={i: 2 + i for i in range(n + 1)}, compiler_params=_split_params(),
    )(*[pltpu.with_memory_space_constraint(a, pltpu.HBM) for a in srcs], _empty_hbm(land_shape, dt))


def _scatter_wait(started, after, *, name):
    send_sems, recv_sems, *bufs = started[:-1]
    n = len(bufs)

    def body(*refs):
        _wait_all(refs[n - 1].at[0], refs[n], refs[n + 1], _position())

    return pl.pallas_call(
        body, name=name, out_shape=tuple(pltpu.HBM(a.shape, a.dtype) for a in bufs),
        in_specs=(HBM,) * n + (SEM, SEM, ANY), out_specs=(HBM,) * n, input_output_aliases={i: i for i in range(n)},
        compiler_params=_split_params(),
    )(*bufs, send_sems, recv_sems, after)


def _peer_sum(own, own_lead, land, block, rows, idx, *, name):
    lead_rank = own.ndim - 2

    def body(idx_ref, own_ref, *refs):
        o_ref = refs[N_DEV - 1]
        acc = own_ref[(0,) * lead_rank].astype(F32)
        for k in range(N_DEV - 1):
            acc = acc + refs[k][0].astype(F32)
        o_ref[...] = acc

    own_spec = pl.BlockSpec((1,) * lead_rank + (rows, PACK_W), lambda i, t: own_lead(t[0]) + (0, 0))

    def land_spec(k):
        return pl.BlockSpec((1, rows, PACK_W), lambda i, t: (t[k + 1], block, 0))

    return pl.pallas_call(
        body, name=name,
        grid_spec=pltpu.PrefetchScalarGridSpec(
            num_scalar_prefetch=1, grid=(1,), in_specs=[own_spec] + [land_spec(k) for k in range(N_DEV - 1)],
            out_specs=pl.BlockSpec((rows, PACK_W), lambda i, t: (0, 0))),
        out_shape=jax.ShapeDtypeStruct((rows, PACK_W), F32), compiler_params=_cparams(("arbitrary",)),
    )(idx, own, *([land] * (N_DEV - 1)))


def _adamw(w, g, m, v):
    m = ADAM_B1 * m + (1.0 - ADAM_B1) * g
    v = ADAM_B2 * v + (1.0 - ADAM_B2) * (g * g)
    m_hat = m / (1.0 - ADAM_B1 ** ADAM_STEP)
    v_hat = v / (1.0 - ADAM_B2 ** ADAM_STEP)
    delta = -ADAM_LR * (m_hat / (jnp.sqrt(v_hat) + ADAM_EPS) + ADAM_WD * w)
    return delta, m, v


def _adamw_call(w, g, m, v, *, name, max_rows=256):
    r, c_ = w.shape
    tr = max_rows if r > max_rows and r % max_rows == 0 else r

    def body(w_ref, g_ref, m_ref, v_ref, d_ref, mo_ref, vo_ref):
        d, mn, vn = _adamw(w_ref[...], g_ref[...], m_ref[...], v_ref[...])
        d_ref[...] = d
        mo_ref[...] = mn
        vo_ref[...] = vn

    row = pl.BlockSpec((tr, c_), lambda i: (i, 0))
    shp = jax.ShapeDtypeStruct((r, c_), F32)
    return pl.pallas_call(
        body, name=name, grid=(r // tr,), in_specs=[row] * 4, out_specs=[row] * 3, out_shape=[shp] * 3,
        compiler_params=_cparams(("parallel",)),
    )(w, g, m, v)


SMALL = ("mix_norm_g", "ffn_norm_g", "final_norm_g", "q_norm_g", "kv_norm_g", "swa_sinks")
SMALL_W = dict(mix_norm_g=1024, ffn_norm_g=1024, final_norm_g=1024, q_norm_g=Q_LORA, kv_norm_g=KV_LORA, swa_sinks=SWA_HEADS)


def _small_adamw(parts, w, m, v):
    n_par = parts.shape[1] // SUBLANES

    def body(p_ref, w_ref, m_ref, v_ref, g_ref, d_ref, mo_ref, vo_ref):
        tot = p_ref[0]
        for dev in range(1, N_DEV):
            tot = tot + p_ref[dev]
        row_id = lax.broadcasted_iota(jnp.int32, (SUBLANES, PACK_W), 0)
        g = jnp.zeros((SUBLANES, PACK_W), F32)
        for k in range(n_par):
            g = jnp.where(row_id == k, jnp.sum(tot[k * SUBLANES:(k + 1) * SUBLANES, :], axis=0, keepdims=True), g)
        d, mn, vn = _adamw(w_ref[...], g, m_ref[...], v_ref[...])
        g_ref[...] = g
        d_ref[...] = d
        mo_ref[...] = mn
        vo_ref[...] = vn

    shp = jax.ShapeDtypeStruct((SUBLANES, PACK_W), F32)
    vm = pl.BlockSpec(memory_space=pltpu.VMEM)
    return pl.pallas_call(body, name="small_adamw", in_specs=[vm] * 4, out_specs=[vm] * 4, out_shape=[shp] * 4)(parts, w, m, v)


def _small_pack(d, rows_each):
    parts = [jnp.pad(d[n].astype(F32), ((0, 0), (0, PACK_W - SMALL_W[n]))) for n in SMALL]
    out = jnp.concatenate(parts, 0)
    pad = -out.shape[0] % SUBLANES
    return jnp.pad(out, ((0, pad), (0, 0)))


def kernel(x, mix_norm_g, w_in, swa_sinks, q_norm_g, w_uq, kv_norm_g, w_ukv, w_o_swa, w_o_mla, w_out, ffn_norm_g, w_gate, w_up, w_down, final_norm_g, loss_target, m_mix_norm_g, m_w_in, m_swa_sinks, m_q_norm_g, m_w_uq, m_kv_norm_g, m_w_ukv, m_w_o_swa, m_w_o_mla, m_w_out, m_ffn_norm_g, m_w_gate, m_w_up, m_w_down, m_final_norm_g, v_mix_norm_g, v_w_in, v_swa_sinks, v_q_norm_g, v_w_uq, v_kv_norm_g, v_w_ukv, v_w_o_swa, v_w_o_mla, v_w_out, v_ffn_norm_g, v_w_gate, v_w_up, v_w_down, v_final_norm_g):
    big_w = dict(w_in=w_in[0], w_uq=w_uq[0], w_ukv=w_ukv[0], w_o_swa=w_o_swa[0], w_o_mla=w_o_mla[0], w_out=w_out[0],
                 w_gate=w_gate[0], w_up=w_up[0], w_down=w_down[0])
    big_m = dict(w_in=m_w_in[0], w_uq=m_w_uq[0], w_ukv=m_w_ukv[0], w_o_swa=m_w_o_swa[0], w_o_mla=m_w_o_mla[0],
                 w_out=m_w_out[0], w_gate=m_w_gate[0], w_up=m_w_up[0], w_down=m_w_down[0])
    big_v = dict(w_in=v_w_in[0], w_uq=v_w_uq[0], w_ukv=v_w_ukv[0], w_o_swa=v_w_o_swa[0], w_o_mla=v_w_o_mla[0],
                 w_out=v_w_out[0], w_gate=v_w_gate[0], w_up=v_w_up[0], w_down=v_w_down[0])
    small_w = dict(mix_norm_g=mix_norm_g, ffn_norm_g=ffn_norm_g, final_norm_g=final_norm_g.reshape(1, D_MODEL),
                   q_norm_g=q_norm_g, kv_norm_g=kv_norm_g, swa_sinks=swa_sinks)
    small_m = dict(mix_norm_g=m_mix_norm_g, ffn_norm_g=m_ffn_norm_g, final_norm_g=m_final_norm_g.reshape(1, D_MODEL),
                   q_norm_g=m_q_norm_g, kv_norm_g=m_kv_norm_g, swa_sinks=m_swa_sinks)
    small_v = dict(mix_norm_g=v_mix_norm_g, ffn_norm_g=v_ffn_norm_g, final_norm_g=v_final_norm_g.reshape(1, D_MODEL),
                   q_norm_g=v_q_norm_g, kv_norm_g=v_kv_norm_g, swa_sinks=v_swa_sinks)

    px, py, pc = _position()
    me = 4 * px + 2 * py + pc
    idx = jnp.stack([me] + [4 * qx + 2 * qy + qc for qx, qy, qc in (_peer(px, py, pc, k) for k in range(1, N_DEV))])
    idx = idx.astype(jnp.int32)

    dev = lambda d: (d,)
    pack = _wire_pack(big_w, WIRE_DTYPE)
    win_g, = _all_gather(pack, ((0, dev, 0, W_IN_ROWS),), ((N_DEV, W_IN_ROWS, PACK_W),), name="ag_early")
    ag_mid = _gather_start(pack, W_IN_ROWS, ((0, dev, OUT_ROWS), (1, dev, SMALL_ROWS)),
                           ((N_DEV, OUT_ROWS, PACK_W), (N_DEV, SMALL_ROWS, PACK_W)), name="ag_mid_start")
    ag = {}

    def own_rows(r0, r1, shape):
        return pack[r0:r1].reshape(shape)

    def mid_weights(after):
        pack_mid, (wout_g, small_g) = _gather_wait(ag_mid, W_IN_ROWS, MID_ROWS, after, name="ag_mid_wait")
        ag["late"] = _gather_start(pack_mid, EARLY_ROWS, ((0, _gate_slab, FF_COLS), (0, _up_slab, FF_COLS), (1, dev, FF_COLS)),
                                   (GU_SHAPE, D_SHAPE), name="ag_late_start")
        wout_g = lax.dynamic_update_slice(wout_g, own_rows(W_IN_ROWS, SMALL_ROW0, (1, OUT_ROWS, PACK_W)), (me, 0, 0))
        small_g = lax.dynamic_update_slice(small_g, own_rows(SMALL_ROW0, EARLY_ROWS, (1, SMALL_ROWS, PACK_W)), (me, 0, 0))
        ops = _mid_operands(wout_g, small_g)
        ops["wuq"] = ops["wuq"] + ag["late"][-1][0:1, 0:1].astype(ops["wuq"].dtype)
        return ops

    def late_weights(after):
        _, (gu, d) = _gather_wait(ag["late"], EARLY_ROWS, LATE_ROWS, after, name="ag_late_wait")
        slab = (1, 1, 1, FF_COLS, PACK_W)
        gu = lax.dynamic_update_slice(gu, own_rows(EARLY_ROWS, EARLY_ROWS + FF_COLS, slab), _gate_slab(me) + (0, 0))
        gu = lax.dynamic_update_slice(gu, own_rows(EARLY_ROWS + FF_COLS, EARLY_ROWS + 2 * FF_COLS, slab), _up_slab(me) + (0, 0))
        d = lax.dynamic_update_slice(d, own_rows(EARLY_ROWS + 2 * FF_COLS, PACK_ROWS, (1, FF_COLS, PACK_W)), (me, 0, 0))
        return gu.reshape(2 * D_FF, D_MODEL), d.reshape(D_FF, D_MODEL)

    rs = {}

    def late_grads(g_gu, g_d):
        rs["late"] = _scatter_start([g_gu.reshape(GU_SHAPE), g_d.reshape(D_SHAPE)],
                                    ((0, _gate_slab, FF_COLS), (0, _up_slab, FF_COLS), (1, dev, FF_COLS)),
                                    name="rs_late_start")
        return rs["late"][-1]

    def mid_grads(g):
        rs["mid"] = _scatter_start([_mid_grad_pack(g)], ((0, dev, MID_ROWS),), name="rs_mid_start")
        return rs["mid"][-1]

    def last_grads(g_win_t):
        rs["last"] = _scatter_start([_w_in_grad_chunks(g_win_t)], ((0, dev, W_IN_ROWS),), name="rs_last_start")
        return rs["last"][-1]

    first_w = dict(small_w, mix_norm_g=mix_norm_g + ag_mid[-1][0:1, 0:1])
    loss_tot, gx, g_small = _local_step(
        x[0], loss_target[0], _w_in_operand(win_g), first_w, types.SimpleNamespace(mid=mid_weights, late=late_weights),
        types.SimpleNamespace(late=late_grads, mid=mid_grads, last=last_grads))

    g_gu, g_d, land_late = _scatter_wait(rs["late"], gx, name="rs_late_wait")
    g_mid, land_mid = _scatter_wait(rs["mid"], gx, name="rs_mid_wait")
    g_win, land_last = _scatter_wait(rs["last"], gx, name="rs_last_wait")
    gw = dict(w_gate=_peer_sum(g_gu, _gate_slab, land_late, 0, FF_COLS, idx, name="rs_sum_gate").T,
              w_up=_peer_sum(g_gu, _up_slab, land_late, 1, FF_COLS, idx, name="rs_sum_up").T,
              w_down=_peer_sum(g_d, dev, land_late, 2, FF_COLS, idx, name="rs_sum_down"),
              w_in=_peer_sum(g_win, dev, land_last, 0, W_IN_ROWS, idx, name="rs_sum_in")[0:W_IN_COLS].T)
    gw.update(_mid_unpack(_peer_sum(g_mid, dev, land_mid, 0, MID_ROWS, idx, name="rs_sum_mid")))
    dw, mw, vw = {}, {}, {}
    for n in BIG:
        dw[n], mw[n], vw[n] = _adamw_call(big_w[n], gw[n], big_m[n], big_v[n], name="adamw_" + n)

    loss_rows = jnp.pad(loss_tot[0:1, 0:1], ((0, SUBLANES - 1), (0, PACK_W - 1)))
    small_rows = jnp.concatenate([_small_pack(g_small_rows(g_small), SUBLANES), loss_rows], 0)
    parts, = _all_gather(small_rows, ((0, lambda d: (d,), 0, small_rows.shape[0]),), ((N_DEV,) + small_rows.shape,),
                         name="ag_small")
    gs, ds, ms, vs = _small_adamw(parts, _small_pack(small_w, 1), _small_pack(small_m, 1), _small_pack(small_v, 1))
    loss = gs[len(SMALL), 0]

    def small_out(packed):
        out = {}
        for k, n in enumerate(SMALL):
            out[n] = packed[k:k + 1, :SMALL_W[n]]
        out["final_norm_g"] = out["final_norm_g"].reshape(D_MODEL)
        return out

    gs, ds, ms, vs = small_out(gs), small_out(ds), small_out(ms), small_out(vs)

    order = ("mix_norm_g", "w_in", "swa_sinks", "q_norm_g", "w_uq", "kv_norm_g", "w_ukv", "w_o_swa", "w_o_mla", "w_out",
             "ffn_norm_g", "w_gate", "w_up", "w_down", "final_norm_g")

    def leaves(big, small):
        return [big[n][None] if n in big else small[n] for n in order]

    return (loss, gx[None], *leaves(gw, gs), *leaves(dw, ds), *leaves(mw, ms), *leaves(vw, vs))


def g_small_rows(g_small):
    out = dict(g_small)
    out["swa_sinks"] = jnp.pad(g_small["swa_sinks"], ((0, SUBLANES - 1), (0, 0)))
    return out
```

```python
import types

import numpy as np
import jax
import jax.numpy as jnp
from jax import lax
from jax.experimental import pallas as pl
from jax.experimental.pallas import tpu as pltpu

F32 = jnp.float32
MXU_DTYPE = jnp.bfloat16
WIRE_DTYPE = jnp.bfloat16

D_MODEL = 1024
EPS = 1e-6
ROPE_THETA = 10000.0
BLOCK = 128
HEAD_DIM = 64
SWA_HEADS = 8
SWA_KV_HEADS = 2
SWA_GROUP = SWA_HEADS // SWA_KV_HEADS
MLA_HEADS = 8
MLA_NOPE = 64
MLA_ROPE = 32
MLA_V = 64
MLA_QK = MLA_NOPE + MLA_ROPE
Q_LORA = 384
KV_LORA = 256
D_FF = 2816
IN_SIZES = (512, 128, 128, Q_LORA, KV_LORA, MLA_ROPE, D_MODEL, D_MODEL)
IN_OFF = tuple(int(v) for v in np.cumsum((0,) + IN_SIZES))
ADAM_LR, ADAM_B1, ADAM_B2, ADAM_EPS, ADAM_WD, ADAM_STEP = 0.001, 0.9, 0.999, 1e-08, 0.01, 10

LANES = 128
SUBLANES = 8
VMEM_LIMIT = 48 * 1024 * 1024
N_DEV = 8
AXES = ("x", "y", "c")

P_GA, P_GB, P_Q, P_QLAT, P_KR, P_K, P_V, P_KVLAT, P_W = 0, 1024, 2048, 3072, 3456, 3584, 3840, 4096, 4352
KR_LANE = 64

LOG2E = 1.4426950408889634
MLA_SCALE = MLA_QK ** -0.5
MLA_C2 = MLA_SCALE * LOG2E

NT = (((1,), (1,)), ((), ()))
NN = (((1,), (0,)), ((), ()))
TN = (((0,), (0,)), ((), ()))


def _cparams(sem):
    return pltpu.CompilerParams(dimension_semantics=sem, vmem_limit_bytes=VMEM_LIMIT)


def _mm(a, b, mode, *, name, out_dtype=F32, add=None, after=None, tm=512, tn=512, tk=None):
    if mode == "nn":
        (M, K), (K2, N) = a.shape, b.shape
    elif mode == "nt":
        (M, K), (N, K2) = a.shape, b.shape
    else:
        (K, M), (K2, N) = a.shape, b.shape
    assert K == K2, (a.shape, b.shape, mode)
    tk = K if tk is None else tk
    tm, tn = min(tm, M), min(tn, N)
    assert M % tm == 0 and N % tn == 0 and K % tk == 0, (M, N, K, tm, tn, tk)
    nk = K // tk
    dn = {"nn": NN, "nt": NT, "tn": TN}[mode]
    if mode == "tn":
        a_spec = pl.BlockSpec((tk, tm), lambda i, j, k: (k, i))
    else:
        a_spec = pl.BlockSpec((tm, tk), lambda i, j, k: (i, k))
    if mode == "nt":
        b_spec = pl.BlockSpec((tn, tk), lambda i, j, k: (j, k))
    else:
        b_spec = pl.BlockSpec((tk, tn), lambda i, j, k: (k, j))
    o_spec = pl.BlockSpec((tm, tn), lambda i, j, k: (i, j))
    has_add, has_after = add is not None, after is not None

    def body(*refs):
        a_ref, b_ref = refs[0], refs[1]
        add_ref = refs[2] if has_add else None
        o_ref = refs[2 + has_add + has_after]
        p = lax.dot_general(a_ref[...], b_ref[...], dn, preferred_element_type=F32)

        def finish(acc):
            if has_add:
                acc = acc + add_ref[...]
            o_ref[...] = acc.astype(o_ref.dtype)

        if nk == 1:
            finish(p)
        else:
            acc_ref = refs[-1]
            k = pl.program_id(2)

            @pl.when(k == 0)
            def _():
                acc_ref[...] = p

            @pl.when(k > 0)
            def _():
                acc_ref[...] += p

            @pl.when(k == nk - 1)
            def _():
                finish(acc_ref[...])

    ins = [a, b] + ([add] if has_add else []) + ([after] if has_after else [])
    in_specs = [a_spec, b_spec] + ([o_spec] if has_add else []) + ([pl.BlockSpec(memory_space=pl.ANY)] if has_after else [])
    return pl.pallas_call(
        body, name=name, grid=(M // tm, N // tn, nk), in_specs=in_specs, out_specs=o_spec,
        out_shape=jax.ShapeDtypeStruct((M, N), out_dtype),
        scratch_shapes=[pltpu.VMEM((tm, tn), F32)] if nk > 1 else [],
        compiler_params=_cparams(("parallel", "parallel", "arbitrary")),
    )(*ins)


def _rows(ts, w, cb=0):
    return pl.BlockSpec((ts, w), lambda i: (i, cb))


def _const(r, w):
    return pl.BlockSpec((r, w), lambda i: (0, 0))


def _sublane_sum(v):
    ts, c = v.shape
    return jnp.sum(v.reshape(ts // SUBLANES, SUBLANES, c), axis=0)


def _sigmoid(v):
    return 1.0 / (1.0 + jnp.exp(-v))


def _rope(v, cos, s_up, s_dn, up, dn):
    return v * cos + pltpu.roll(v, up, 1) * s_up + pltpu.roll(v, dn, 1) * s_dn


def _rope_t(dv, cos, s_up, s_dn, up, dn):
    return dv * cos + pltpu.roll(dv * s_up, dn, 1) + pltpu.roll(dv * s_dn, up, 1)


def _rope_tables(seq):
    pos = np.arange(seq, dtype=np.float32)[:, None]

    def base(dim):
        inv = np.float32(ROPE_THETA) ** (-np.arange(0, dim, 2, dtype=np.float32) / np.float32(dim))
        ang = (pos * inv.astype(np.float32)[None, :]).astype(np.float32)
        return np.cos(ang).astype(np.float32), np.sin(ang).astype(np.float32)

    z = lambda n: np.zeros((seq, n), np.float32)
    ca, sa = base(HEAD_DIM)
    a_cos = np.concatenate([ca, ca, z(64)], 1)
    a_up = np.concatenate([-sa, z(96)], 1)
    a_dn = np.concatenate([z(32), sa, z(64)], 1)
    cb, sb = base(MLA_ROPE)
    one = np.ones((seq, 64), np.float32)
    q_cos = np.concatenate([one, cb, cb, z(32)], 1)
    k_cos = np.concatenate([z(64), cb, cb, z(32)], 1)
    b_up = np.concatenate([z(64), -sb, z(48)], 1)
    b_dn = np.concatenate([z(80), sb, z(32)], 1)
    return tuple(jnp.asarray(t) for t in (a_cos, a_up, a_dn, q_cos, k_cos, b_up, b_dn))


def _rms(v, g):
    return v * lax.rsqrt(jnp.mean(v * v, axis=-1, keepdims=True) + EPS) * g


def _rms_bwd(v, g, d):
    r = lax.rsqrt(jnp.mean(v * v, axis=-1, keepdims=True) + EPS)
    xh = v * r
    dxh = d * g
    return r * (dxh - xh * jnp.mean(dxh * xh, axis=-1, keepdims=True)), d * xh


def _norm_mm(x, g, w_t, *, name, tn, tm=512):
    s_, c = x.shape
    n = w_t.shape[0]

    def body(x_ref, g_ref, w_ref, h_ref, o_ref):
        h = _rms(x_ref[...], g_ref[...]).astype(h_ref.dtype)
        h_ref[...] = h
        o_ref[...] = lax.dot_general(h, w_ref[...], NT, preferred_element_type=F32)

    return pl.pallas_call(
        body, name=name, grid=(s_ // tm, n // tn),
        in_specs=[pl.BlockSpec((tm, c), lambda i, j: (i, 0)), pl.BlockSpec((1, c), lambda i, j: (0, 0)),
                  pl.BlockSpec((tn, c), lambda i, j: (j, 0))],
        out_specs=[pl.BlockSpec((tm, c), lambda i, j: (i, 0)), pl.BlockSpec((tm, tn), lambda i, j: (i, j))],
        out_shape=[jax.ShapeDtypeStruct((s_, c), MXU_DTYPE), jax.ShapeDtypeStruct((s_, n), F32)],
        compiler_params=_cparams(("parallel", "arbitrary")),
    )(x, g, w_t)


def _mm_norm_bwd(a, b, x, g, res, *, name, tk, after=None, tm=512):
    s_, kk = a.shape
    c = b.shape[1]
    nk = kk // tk
    has_after = after is not None

    def body(*refs):
        a_ref, b_ref, x_ref, g_ref, res_ref = refs[:5]
        dx_ref, dxb_ref, dg_ref, acc_ref = refs[5 + has_after:]
        i, k = pl.program_id(0), pl.program_id(1)
        p = jnp.dot(a_ref[...], b_ref[...], preferred_element_type=F32)

        @pl.when(k == 0)
        def _():
            acc_ref[...] = p

        @pl.when(k > 0)
        def _():
            acc_ref[...] += p

        @pl.when(k == nk - 1)
        def _():
            dx, gg = _rms_bwd(x_ref[...], g_ref[...], acc_ref[...])
            dx = dx + res_ref[...]
            dx_ref[...] = dx
            dxb_ref[...] = dx.astype(dxb_ref.dtype)

            @pl.when(i == 0)
            def _():
                dg_ref[...] = jnp.zeros(dg_ref.shape, F32)

            dg_ref[...] += _sublane_sum(gg)

    row = pl.BlockSpec((tm, c), lambda i, k: (i, 0))
    in_specs = [pl.BlockSpec((tm, tk), lambda i, k: (i, k)), pl.BlockSpec((tk, c), lambda i, k: (k, 0)), row,
                pl.BlockSpec((1, c), lambda i, k: (0, 0)), row] + ([pl.BlockSpec(memory_space=pl.ANY)] if has_after else [])
    return pl.pallas_call(
        body, name=name, grid=(s_ // tm, nk), in_specs=in_specs,
        out_specs=[row, row, pl.BlockSpec((SUBLANES, c), lambda i, k: (0, 0))],
        out_shape=[jax.ShapeDtypeStruct((s_, c), F32), jax.ShapeDtypeStruct((s_, c), MXU_DTYPE),
                   jax.ShapeDtypeStruct((SUBLANES, c), F32)],
        scratch_shapes=[pltpu.VMEM((tm, c), F32)], compiler_params=_cparams(("arbitrary", "arbitrary")),
    )(*([a, b, x, g, res] + ([after] if has_after else [])))


def _norm_bwd(x, g, dy, res, *, name, ts=256, x_cb=0, x_src_w=None):
    s_ = x.shape[0]
    c = dy.shape[1]
    has_res = res is not None

    def body(*refs):
        x_ref, g_ref, dy_ref = refs[0], refs[1], refs[2]
        res_ref = refs[3] if has_res else None
        dx_ref, dxb_ref, dg_ref = refs[-3], refs[-2], refs[-1]
        v = x_ref[...]
        r = lax.rsqrt(jnp.mean(v * v, axis=-1, keepdims=True) + EPS)
        xh = v * r
        d = dy_ref[...]
        dxh = d * g_ref[...]
        dx = r * (dxh - xh * jnp.mean(dxh * xh, axis=-1, keepdims=True))
        if has_res:
            dx = dx + res_ref[...]
        dx_ref[...] = dx
        dxb_ref[...] = dx.astype(dxb_ref.dtype)

        @pl.when(pl.program_id(0) == 0)
        def _():
            dg_ref[...] = jnp.zeros(dg_ref.shape, F32)

        dg_ref[...] += _sublane_sum(d * xh)

    ins = [x, g, dy] + ([res] if has_res else [])
    in_specs = [_rows(ts, c, x_cb), _const(1, c), _rows(ts, c)] + ([_rows(ts, c)] if has_res else [])
    return pl.pallas_call(
        body, name=name, grid=(s_ // ts,), in_specs=in_specs,
        out_specs=[_rows(ts, c), _rows(ts, c), _const(SUBLANES, c)],
        out_shape=[jax.ShapeDtypeStruct((s_, c), F32), jax.ShapeDtypeStruct((s_, c), MXU_DTYPE),
                   jax.ShapeDtypeStruct((SUBLANES, c), F32)],
        compiler_params=_cparams(("arbitrary",)),
    )(*ins)


def _attn_prep(p, gq, gkv, tabs, *, ts=256):
    s_ = p.shape[0]
    a_cos, a_up, a_dn, _, k_cos, b_up, b_dn = tabs

    def body(q_ref, k_ref, v_ref, ql_ref, kvl_ref, kr_ref, gq_ref, gkv_ref, ac, au, ad, kc, bu, bd,
             qa_ref, ka_ref, va_ref, cq_ref, ckv_ref, kro_ref):
        c_, u_, d_ = ac[...], au[...], ad[...]
        for h in range(SWA_HEADS):
            sl = slice(h * LANES, (h + 1) * LANES)
            qa_ref[:, sl] = _rope(q_ref[:, sl], c_, u_, d_, 96, 32).astype(qa_ref.dtype)
        for h in range(SWA_KV_HEADS):
            sl = slice(h * LANES, (h + 1) * LANES)
            ka_ref[:, sl] = _rope(k_ref[:, sl], c_, u_, d_, 96, 32).astype(ka_ref.dtype)
        va_ref[...] = v_ref[...].astype(va_ref.dtype)
        for src, gref, dst in ((ql_ref, gq_ref, cq_ref), (kvl_ref, gkv_ref, ckv_ref)):
            v = src[...]
            r = lax.rsqrt(jnp.mean(v * v, axis=-1, keepdims=True) + EPS)
            dst[...] = (v * r * gref[...]).astype(dst.dtype)
        kro_ref[...] = _rope(kr_ref[...], kc[...], bu[...], bd[...], 112, 16)

    tab = _rows(ts, LANES)
    return pl.pallas_call(
        body, name="attn_prep", grid=(s_ // ts,),
        in_specs=[_rows(ts, 1024, P_Q // 1024), _rows(ts, 256, P_K // 256), _rows(ts, 256, P_V // 256),
                  _rows(ts, Q_LORA, P_QLAT // Q_LORA), _rows(ts, KV_LORA, P_KVLAT // KV_LORA),
                  _rows(ts, LANES, P_KR // LANES), _const(1, Q_LORA), _const(1, KV_LORA), tab, tab, tab, tab, tab, tab],
        out_specs=[_rows(ts, 1024), _rows(ts, 256), _rows(ts, 256), _rows(ts, Q_LORA), _rows(ts, KV_LORA),
                   _rows(ts, LANES)],
        out_shape=[jax.ShapeDtypeStruct((s_, 1024), MXU_DTYPE), jax.ShapeDtypeStruct((s_, 256), MXU_DTYPE),
                   jax.ShapeDtypeStruct((s_, 256), MXU_DTYPE), jax.ShapeDtypeStruct((s_, Q_LORA), MXU_DTYPE),
                   jax.ShapeDtypeStruct((s_, KV_LORA), MXU_DTYPE), jax.ShapeDtypeStruct((s_, LANES), F32)],
        compiler_params=_cparams(("parallel",)),
    )(p, p, p, p, p, p, gq, gkv, a_cos, a_up, a_dn, k_cos, b_up, b_dn)


def _mla_prep(qp, kp, kro, tabs, *, ts=256):
    s_ = qp.shape[0]
    _, _, _, q_cos, _, b_up, b_dn = tabs

    def body(q_ref, k_ref, kr_ref, qc, bu, bd, qo_ref, ko_ref):
        c_, u_, d_ = qc[...], bu[...], bd[...]
        kr = kr_ref[...]
        for h in range(MLA_HEADS):
            sl = slice(h * LANES, (h + 1) * LANES)
            qo_ref[:, sl] = (_rope(q_ref[:, sl], c_, u_, d_, 112, 16) * MLA_C2).astype(qo_ref.dtype)
            ko_ref[:, sl] = (k_ref[:, sl] + kr).astype(ko_ref.dtype)

    tab = _rows(ts, LANES)
    return pl.pallas_call(
        body, name="mla_prep", grid=(s_ // ts,),
        in_specs=[_rows(ts, 1024), _rows(ts, 1024), tab, tab, tab, tab],
        out_specs=[_rows(ts, 1024), _rows(ts, 1024)],
        out_shape=[jax.ShapeDtypeStruct((s_, 1024), MXU_DTYPE)] * 2,
        compiler_params=_cparams(("parallel",)),
    )(qp, kp, kro, q_cos, b_up, b_dn)


def _mla_unprep(dqc, dkc, dvp, tabs, *, ts=256):
    s_ = dqc.shape[0]
    _, _, _, q_cos, k_cos, b_up, b_dn = tabs

    def body(dq_ref, dk_ref, dv_ref, qc, kc, bu, bd, dqo_ref, dkvo_ref, dkr_ref):
        c_, u_, d_ = qc[...], bu[...], bd[...]
        tot = jnp.zeros((ts, LANES), F32)
        for h in range(MLA_HEADS):
            sl = slice(h * LANES, (h + 1) * LANES)
            dqo_ref[:, sl] = _rope_t(dq_ref[:, sl], c_, u_, d_, 112, 16).astype(dqo_ref.dtype)
            dk = dk_ref[:, sl] * (MLA_SCALE / MLA_C2)
            dkvo_ref[:, sl] = dk.astype(dkvo_ref.dtype)
            tot = tot + dk
        dkvo_ref[:, 1024:2048] = dv_ref[...].astype(dkvo_ref.dtype)
        dkr_ref[...] = _rope_t(tot, kc[...], u_, d_, 112, 16).astype(dkr_ref.dtype)

    tab = _rows(ts, LANES)
    return pl.pallas_call(
        body, name="mla_unprep", grid=(s_ // ts,),
        in_specs=[_rows(ts, 1024), _rows(ts, 1024), _rows(ts, 1024), tab, tab, tab, tab],
        out_specs=[_rows(ts, 1024), _rows(ts, 2048), _rows(ts, LANES)],
        out_shape=[jax.ShapeDtypeStruct((s_, 1024), MXU_DTYPE), jax.ShapeDtypeStruct((s_, 2048), MXU_DTYPE),
                   jax.ShapeDtypeStruct((s_, LANES), MXU_DTYPE)],
        compiler_params=_cparams(("parallel",)),
    )(dqc, dkc, dvp, q_cos, k_cos, b_up, b_dn)


def _swa_unrope(dqa, dka, tabs, *, ts=256):
    s_ = dqa.shape[0]
    a_cos, a_up, a_dn = tabs[0], tabs[1], tabs[2]

    def body(dq_ref, dk_ref, ac, au, ad, dqo_ref, dko_ref):
        c_, u_, d_ = ac[...], au[...], ad[...]
        for h in range(SWA_HEADS):
            sl = slice(h * LANES, (h + 1) * LANES)
            dqo_ref[:, sl] = _rope_t(dq_ref[:, sl], c_, u_, d_, 96, 32).astype(dqo_ref.dtype)
        for h in range(SWA_KV_HEADS):
            sl = slice(h * LANES, (h + 1) * LANES)
            dko_ref[:, sl] = _rope_t(dk_ref[:, sl], c_, u_, d_, 96, 32).astype(dko_ref.dtype)

    tab = _rows(ts, LANES)
    return pl.pallas_call(
        body, name="swa_unrope", grid=(s_ // ts,),
        in_specs=[_rows(ts, 1024), _rows(ts, 256), tab, tab, tab],
        out_specs=[_rows(ts, 1024), _rows(ts, 256)],
        out_shape=[jax.ShapeDtypeStruct((s_, 1024), MXU_DTYPE), jax.ShapeDtypeStruct((s_, 256), MXU_DTYPE)],
        compiler_params=_cparams(("parallel",)),
    )(dqa, dka, a_cos, a_up, a_dn)


def _attn_out_gate(oa, ob, woa, wob, p, *, ts=512):
    s_ = p.shape[0]

    def body(oa_ref, ob_ref, wa_ref, wb_ref, ga_ref, gb_ref, ta_ref, tb_ref, y_ref):
        ta = jnp.dot(oa_ref[...], wa_ref[...], preferred_element_type=F32)
        tb = jnp.dot(ob_ref[...], wb_ref[...], preferred_element_type=F32)
        ta_ref[...] = ta
        tb_ref[...] = tb
        y_ref[...] = (_sigmoid(ga_ref[...]) * ta + _sigmoid(gb_ref[...]) * tb).astype(y_ref.dtype)

    w = _const(1024, 1024)
    return pl.pallas_call(
        body, name="attn_out_gate", grid=(s_ // ts,),
        in_specs=[_rows(ts, 1024), _rows(ts, 1024), w, w, _rows(ts, 1024, P_GA // 1024), _rows(ts, 1024, P_GB // 1024)],
        out_specs=[_rows(ts, 1024)] * 3,
        out_shape=[jax.ShapeDtypeStruct((s_, 1024), F32)] * 2 + [jax.ShapeDtypeStruct((s_, 1024), MXU_DTYPE)],
        compiler_params=_cparams(("parallel",)),
    )(oa, ob, woa, wob, p, p)


def _d_y_gate(dx1b, wout, p, ta, tb, *, ts=512):
    s_ = p.shape[0]

    def body(dx_ref, w_ref, ga_ref, gb_ref, ta_ref, tb_ref, dta_ref, dtb_ref, dg_ref):
        d = lax.dot_general(dx_ref[...], w_ref[...], NT, preferred_element_type=F32)
        sa, sb = _sigmoid(ga_ref[...]), _sigmoid(gb_ref[...])
        dta_ref[...] = (d * sa).astype(dta_ref.dtype)
        dtb_ref[...] = (d * sb).astype(dtb_ref.dtype)
        dg_ref[:, 0:1024] = (d * ta_ref[...] * (sa * (1.0 - sa))).astype(dg_ref.dtype)
        dg_ref[:, 1024:2048] = (d * tb_ref[...] * (sb * (1.0 - sb))).astype(dg_ref.dtype)

    return pl.pallas_call(
        body, name="d_y_gate", grid=(s_ // ts,),
        in_specs=[_rows(ts, 1024), _const(1024, 1024), _rows(ts, 1024, P_GA // 1024), _rows(ts, 1024, P_GB // 1024),
                  _rows(ts, 1024), _rows(ts, 1024)],
        out_specs=[_rows(ts, 1024), _rows(ts, 1024), _rows(ts, 2048)],
        out_shape=[jax.ShapeDtypeStruct((s_, 1024), MXU_DTYPE)] * 2 + [jax.ShapeDtypeStruct((s_, 2048), MXU_DTYPE)],
        compiler_params=_cparams(("parallel",)),
    )(dx1b, wout, p, p, ta, tb)


FF_TILE = D_FF // 2


def _ffn_in_act(x1, g, wgu_t, *, tm=512):
    s_ = x1.shape[0]

    def body(x_ref, g_ref, w_ref, h_ref, gu_ref, a_ref):
        h = _rms(x_ref[...], g_ref[...]).astype(h_ref.dtype)
        h_ref[...] = h
        p = lax.dot_general(h, w_ref[...], NT, preferred_element_type=F32)
        gu_ref[...] = p
        gate = p[:, :FF_TILE]
        a_ref[...] = (gate * _sigmoid(gate) * p[:, FF_TILE:]).astype(a_ref.dtype)

    return pl.pallas_call(
        body, name="ffn_in", grid=(s_ // tm, 2),
        in_specs=[pl.BlockSpec((tm, D_MODEL), lambda i, j: (i, 0)), pl.BlockSpec((1, D_MODEL), lambda i, j: (0, 0)),
                  pl.BlockSpec((2 * FF_TILE, D_MODEL), lambda i, j: (j, 0))],
        out_specs=[pl.BlockSpec((tm, D_MODEL), lambda i, j: (i, 0)), pl.BlockSpec((tm, 2 * FF_TILE), lambda i, j: (i, j)),
                   pl.BlockSpec((tm, FF_TILE), lambda i, j: (i, j))],
        out_shape=[jax.ShapeDtypeStruct((s_, D_MODEL), MXU_DTYPE), jax.ShapeDtypeStruct((s_, 2 * D_FF), F32),
                   jax.ShapeDtypeStruct((s_, D_FF), MXU_DTYPE)],
        compiler_params=_cparams(("parallel", "arbitrary")),
    )(x1, g, wgu_t)


def _d_act_swiglu(dx2b, wd, gu, *, tm=512):
    s_ = dx2b.shape[0]

    def body(d_ref, w_ref, gu_ref, o_ref):
        da = lax.dot_general(d_ref[...], w_ref[...], NT, preferred_element_type=F32)
        g, u = gu_ref[:, :FF_TILE], gu_ref[:, FF_TILE:]
        sg = _sigmoid(g)
        o_ref[:, :FF_TILE] = (da * u * (sg * (1.0 + g * (1.0 - sg)))).astype(o_ref.dtype)
        o_ref[:, FF_TILE:] = (da * (g * sg)).astype(o_ref.dtype)

    gu_spec = pl.BlockSpec((tm, 2 * FF_TILE), lambda i, j: (i, j))
    return pl.pallas_call(
        body, name="d_act", grid=(s_ // tm, 2),
        in_specs=[pl.BlockSpec((tm, D_MODEL), lambda i, j: (i, 0)), pl.BlockSpec((FF_TILE, D_MODEL), lambda i, j: (j, 0)), gu_spec],
        out_specs=gu_spec, out_shape=jax.ShapeDtypeStruct((s_, 2 * D_FF), MXU_DTYPE),
        compiler_params=_cparams(("parallel", "parallel")),
    )(dx2b, wd, gu)


def _ffn_out_loss(act, wd, x1, g, tgt, *, ts=512):
    s_, c = x1.shape
    kk = act.shape[1]

    def body(a_ref, w_ref, x_ref, g_ref, t_ref, dx_ref, dxb_ref, dg_ref, lp_ref, tot_ref):
        v = x_ref[...] + jnp.dot(a_ref[...], w_ref[...], preferred_element_type=F32)
        r = lax.rsqrt(jnp.mean(v * v, axis=-1, keepdims=True) + EPS)
        xh = v * r
        gg = g_ref[...]
        e = xh * gg - t_ref[...]
        do = e * (1.0 / c)
        dxh = do * gg
        dx = r * (dxh - xh * jnp.mean(dxh * xh, axis=-1, keepdims=True))
        dx_ref[...] = dx
        dxb_ref[...] = dx.astype(dxb_ref.dtype)
        i = pl.program_id(0)

        @pl.when(i == 0)
        def _():
            dg_ref[...] = jnp.zeros(dg_ref.shape, F32)
            lp_ref[...] = jnp.zeros(lp_ref.shape, F32)

        dg_ref[...] += _sublane_sum(do * xh)
        lp_ref[...] += _sublane_sum(e * e)
        tot_ref[...] = jnp.full(tot_ref.shape, (0.5 / c) * jnp.sum(lp_ref[...]), F32)

    return pl.pallas_call(
        body, name="ffn_out_loss", grid=(s_ // ts,),
        in_specs=[_rows(ts, kk), _const(kk, c), _rows(ts, c), _const(1, c), _rows(ts, c)],
        out_specs=[_rows(ts, c), _rows(ts, c), _const(SUBLANES, c), _const(SUBLANES, c), _const(SUBLANES, LANES)],
        out_shape=[jax.ShapeDtypeStruct((s_, c), F32), jax.ShapeDtypeStruct((s_, c), MXU_DTYPE),
                   jax.ShapeDtypeStruct((SUBLANES, c), F32), jax.ShapeDtypeStruct((SUBLANES, c), F32),
                   jax.ShapeDtypeStruct((SUBLANES, LANES), F32)],
        compiler_params=_cparams(("arbitrary",)),
    )(act, wd, x1, g, tgt)


def _mla_bwd_prep(dob, o32, *, ts=256):
    s_ = dob.shape[0]

    def body(do_ref, o_ref, dob_ref, dl_ref):
        d = do_ref[...]
        dob_ref[...] = d.astype(dob_ref.dtype)
        prod = d * o_ref[...]
        for h in range(MLA_HEADS):
            dl_ref[h] = jnp.sum(prod[:, h * LANES:(h + 1) * LANES].T, axis=0, keepdims=True)

    return pl.pallas_call(
        body, name="mla_bwd_prep", grid=(s_ // ts,), in_specs=[_rows(ts, 1024), _rows(ts, 1024)],
        out_specs=[_rows(ts, 1024), pl.BlockSpec((MLA_HEADS, 1, ts), lambda i: (0, 0, i))],
        out_shape=[jax.ShapeDtypeStruct((s_, 1024), MXU_DTYPE), jax.ShapeDtypeStruct((MLA_HEADS, 1, s_), F32)],
        compiler_params=_cparams(("parallel",)),
    )(dob, o32)


SWA_T = 4 * BLOCK


SWA_W = SWA_GROUP * BLOCK


def _swa_masks(sb):
    kr = lax.broadcasted_iota(jnp.int32, (2 * BLOCK, SWA_W), 0)
    qc = jnp.bitwise_and(lax.broadcasted_iota(jnp.int32, (2 * BLOCK, SWA_W), 1), BLOCK - 1)
    band = jnp.logical_and(kr > qc, kr <= qc + BLOCK)
    first = jnp.logical_and(band, kr >= BLOCK)
    return band, jnp.logical_or(first, jnp.logical_and(band, sb > 0))


def _heads_to_rows(ref, rs):
    return jnp.concatenate([ref[rs, h * LANES:(h + 1) * LANES] for h in range(SWA_GROUP)], axis=0)


def _sink_row(sk_ref):
    return jnp.concatenate([sk_ref[0, h:h + 1, :] for h in range(SWA_GROUP)], axis=1) * LOG2E


def _swa_in_specs(rev, nsb):
    sbi = (lambda j: nsb - 1 - j) if rev else (lambda j: j)
    cur = pl.BlockSpec((SWA_T, LANES), lambda g, j: (sbi(j), g))
    prev = pl.BlockSpec((BLOCK, LANES), lambda g, j: (jnp.maximum(4 * sbi(j) - 1, 0), g))
    q = pl.BlockSpec((SWA_T, SWA_GROUP * LANES), lambda g, j: (sbi(j), g))
    sink = pl.BlockSpec((1, SUBLANES, LANES), lambda g, j: (g, 0, 0))
    lse = pl.BlockSpec((SWA_GROUP, 1, SWA_T), lambda g, j: (g, 0, sbi(j)))
    return q, cur, prev, sink, lse


def _swa_fwd(qa, ka, va, sink_b):
    s_ = qa.shape[0]
    nsb = s_ // SWA_T
    c2 = HEAD_DIM ** -0.5 * LOG2E

    def body(q_ref, kc_ref, kp_ref, vc_ref, vp_ref, sk_ref, o32_ref, o16_ref, lse_ref, kx, vx):
        kx[0:BLOCK, :] = kp_ref[...]
        kx[BLOCK:5 * BLOCK, :] = kc_ref[...]
        vx[0:BLOCK, :] = vp_ref[...]
        vx[BLOCK:5 * BLOCK, :] = vc_ref[...]
        band, band0 = _swa_masks(pl.program_id(1))
        sink2 = _sink_row(sk_ref)
        for b in range(4):
            rs = slice(b * BLOCK, (b + 1) * BLOCK)
            ks = slice(b * BLOCK, (b + 2) * BLOCK)
            st = lax.dot_general(kx[ks, :], _heads_to_rows(q_ref, rs), NT, preferred_element_type=F32) * c2
            st = jnp.where(band0 if b == 0 else band, st, -jnp.inf)
            m = jnp.maximum(jnp.max(st, axis=0, keepdims=True), sink2)
            pt = jnp.exp2(st - m)
            den = jnp.sum(pt, axis=0, keepdims=True) + jnp.exp2(sink2 - m)
            o = lax.dot_general((pt * (1.0 / den)).astype(MXU_DTYPE), vx[ks, :], TN, preferred_element_type=F32)
            lse = m + jnp.log2(den)
            for hh in range(SWA_GROUP):
                cs = slice(hh * LANES, (hh + 1) * LANES)
                o32_ref[rs, cs] = o[cs, :]
                o16_ref[rs, cs] = o[cs, :].astype(o16_ref.dtype)
                lse_ref[hh, :, rs] = lse[:, cs]

    q, cur, prev, sink, lse_spec = _swa_in_specs(False, nsb)
    return pl.pallas_call(
        body, name="swa_fwd", grid=(SWA_KV_HEADS, nsb), in_specs=[q, cur, prev, cur, prev, sink],
        out_specs=[q, q, lse_spec],
        out_shape=[jax.ShapeDtypeStruct((s_, SWA_HEADS * LANES), F32), jax.ShapeDtypeStruct((s_, SWA_HEADS * LANES), MXU_DTYPE),
                   jax.ShapeDtypeStruct((SWA_HEADS, 1, s_), F32)],
        scratch_shapes=[pltpu.VMEM((5 * BLOCK, LANES), MXU_DTYPE), pltpu.VMEM((5 * BLOCK, LANES), MXU_DTYPE)],
        compiler_params=_cparams(("parallel", "arbitrary")),
    )(qa, ka, ka, va, va, sink_b)


def _swa_bwd(qa, ka, va, sink_b, o32, do, lse):
    s_ = qa.shape[0]
    nsb = s_ // SWA_T
    scale = HEAD_DIM ** -0.5
    c2 = scale * LOG2E

    def body(q_ref, kc_ref, kp_ref, vc_ref, vp_ref, sk_ref, o_ref, do_ref, lse_ref,
             dq_ref, dk_ref, dv_ref, dsk_ref, kx, vx, kacc, vacc, kcar, vcar):
        j = pl.program_id(1)
        kx[0:BLOCK, :] = kp_ref[...]
        kx[BLOCK:5 * BLOCK, :] = kc_ref[...]
        vx[0:BLOCK, :] = vp_ref[...]
        vx[BLOCK:5 * BLOCK, :] = vc_ref[...]
        band, band0 = _swa_masks(nsb - 1 - j)
        kacc[...] = jnp.zeros(kacc.shape, F32)
        vacc[...] = jnp.zeros(vacc.shape, F32)

        @pl.when(j == 0)
        def _():
            kcar[...] = jnp.zeros(kcar.shape, F32)
            vcar[...] = jnp.zeros(vcar.shape, F32)
            dsk_ref[...] = jnp.zeros(dsk_ref.shape, F32)

        sink2 = _sink_row(sk_ref)
        dsink = jnp.zeros((1, SWA_W), F32)
        for b in range(4):
            rs = slice(b * BLOCK, (b + 1) * BLOCK)
            ks = slice(b * BLOCK, (b + 2) * BLOCK)
            q, k2, v2 = _heads_to_rows(q_ref, rs), kx[ks, :], vx[ks, :]
            d = _heads_to_rows(do_ref, rs)
            delta = jnp.sum((d * _heads_to_rows(o_ref, rs)).T, axis=0, keepdims=True)
            l2 = jnp.concatenate([lse_ref[hh, :, rs] for hh in range(SWA_GROUP)], axis=1)
            st = lax.dot_general(k2, q, NT, preferred_element_type=F32) * c2
            pt = jnp.exp2(jnp.where(band0 if b == 0 else band, st, -jnp.inf) - l2)
            db = d.astype(MXU_DTYPE)
            dst = (pt * (lax.dot_general(v2, db, NT, preferred_element_type=F32) - delta) * scale).astype(MXU_DTYPE)
            dq = lax.dot_general(dst, k2, TN, preferred_element_type=F32)
            for hh in range(SWA_GROUP):
                dq_ref[rs, hh * LANES:(hh + 1) * LANES] = dq[hh * LANES:(hh + 1) * LANES, :]
            kacc[ks, :] += jnp.dot(dst, q, preferred_element_type=F32)
            vacc[ks, :] += jnp.dot(pt.astype(MXU_DTYPE), db, preferred_element_type=F32)
            dsink = dsink - jnp.exp2(sink2 - l2) * delta
        for hh in range(SWA_GROUP):
            tot = jnp.sum(dsink[:, hh * LANES:(hh + 1) * LANES], axis=1, keepdims=True)
            dsk_ref[0, hh:hh + 1, :] += jnp.broadcast_to(tot, (1, LANES))

        dk_ref[0:3 * BLOCK, :] = kacc[BLOCK:4 * BLOCK, :]
        dk_ref[3 * BLOCK:4 * BLOCK, :] = kacc[4 * BLOCK:5 * BLOCK, :] + kcar[...]
        dv_ref[0:3 * BLOCK, :] = vacc[BLOCK:4 * BLOCK, :].astype(dv_ref.dtype)
        dv_ref[3 * BLOCK:4 * BLOCK, :] = (vacc[4 * BLOCK:5 * BLOCK, :] + vcar[...]).astype(dv_ref.dtype)
        kcar[...] = kacc[0:BLOCK, :]
        vcar[...] = vacc[0:BLOCK, :]

    q, cur, prev, sink, lse_spec = _swa_in_specs(True, nsb)
    return pl.pallas_call(
        body, name="swa_bwd", grid=(SWA_KV_HEADS, nsb),
        in_specs=[q, cur, prev, cur, prev, sink, q, q, lse_spec],
        out_specs=[q, cur, cur, sink],
        out_shape=[jax.ShapeDtypeStruct((s_, SWA_HEADS * LANES), F32), jax.ShapeDtypeStruct((s_, SWA_KV_HEADS * LANES), F32),
                   jax.ShapeDtypeStruct((s_, SWA_KV_HEADS * LANES), MXU_DTYPE),
                   jax.ShapeDtypeStruct((SWA_KV_HEADS, SUBLANES, LANES), F32)],
        scratch_shapes=[pltpu.VMEM((5 * BLOCK, LANES), MXU_DTYPE), pltpu.VMEM((5 * BLOCK, LANES), MXU_DTYPE),
                        pltpu.VMEM((5 * BLOCK, LANES), F32), pltpu.VMEM((5 * BLOCK, LANES), F32),
                        pltpu.VMEM((BLOCK, LANES), F32), pltpu.VMEM((BLOCK, LANES), F32)],
        compiler_params=_cparams(("arbitrary", "arbitrary")),
    )(qa, ka, ka, va, va, sink_b, o32, do, lse)


MLA_T = 512
MLA_FWD_GROUP = 4
MLA_BWD_GROUP = 2


def _mla_specs(s_, t, group):
    w = group * LANES
    qs = pl.BlockSpec((t, w), lambda g, i: (i, g))
    kv = pl.BlockSpec((s_, w), lambda g, i: (0, g))
    row = pl.BlockSpec((group, 1, t), lambda g, i: (g, 0, i))
    return qs, kv, row


def _causal_scores_t(k, q, t, masked):
    st = lax.dot_general(k, q, NT, preferred_element_type=F32)
    if masked:
        kr = lax.broadcasted_iota(jnp.int32, (t, t), 0)
        qc = lax.broadcasted_iota(jnp.int32, (t, t), 1)
        st = jnp.where(kr <= qc, st, -jnp.inf)
    return st


def _mla_fwd(qc, kc, vp):
    s_ = qc.shape[0]
    t = min(MLA_T, s_)
    grp = MLA_FWD_GROUP

    def body(q_ref, k_ref, v_ref, o32_ref, o16_ref, lse_ref, m_s, acc_s):
        qi = pl.program_id(1)
        m_s[...] = jnp.full(m_s.shape, -jnp.inf, F32)
        acc_s[...] = jnp.zeros(acc_s.shape, F32)
        ones_lane = lax.broadcasted_iota(jnp.int32, (t, LANES), 1) == MLA_V

        def step(ki, masked):
            off = pl.multiple_of(ki * t, t)
            for g in range(grp):
                cs = slice(g * LANES, (g + 1) * LANES)
                st = _causal_scores_t(k_ref[pl.ds(off, t), cs], q_ref[:, cs], t, masked)
                m_old = m_s[g]
                m_new = jnp.maximum(m_old, jnp.max(st, axis=0, keepdims=True))
                alpha = jnp.exp2(m_old - m_new)
                pt = jnp.exp2(st - m_new).astype(MXU_DTYPE)
                v = v_ref[pl.ds(off, t), cs]
                v = jnp.where(ones_lane, jnp.ones((), v.dtype), v)
                acc_s[g] = alpha * acc_s[g] + lax.dot_general(v, pt, TN, preferred_element_type=F32)
                m_s[g] = m_new

        def full_block(ki, carry):
            step(ki, False)
            return carry

        lax.fori_loop(0, qi, full_block, 0)
        step(qi, True)
        for g in range(grp):
            cs = slice(g * LANES, (g + 1) * LANES)
            acc = acc_s[g]
            l = acc[MLA_V:MLA_V + 1, :]
            o = (acc * (1.0 / l)).T
            o32_ref[:, cs] = o
            o16_ref[:, cs] = o.astype(o16_ref.dtype)
            lse_ref[g] = m_s[g] + jnp.log2(l)

    qs, kv, row = _mla_specs(s_, t, grp)
    return pl.pallas_call(
        body, name="mla_fwd", grid=(MLA_HEADS // grp, s_ // t), in_specs=[qs, kv, kv], out_specs=[qs, qs, row],
        out_shape=[jax.ShapeDtypeStruct((s_, MLA_HEADS * LANES), F32), jax.ShapeDtypeStruct((s_, MLA_HEADS * LANES), MXU_DTYPE),
                   jax.ShapeDtypeStruct((MLA_HEADS, 1, s_), F32)],
        scratch_shapes=[pltpu.VMEM((grp, 1, t), F32), pltpu.VMEM((grp, LANES, t), F32)],
        compiler_params=_cparams(("parallel", "arbitrary")),
    )(qc, kc, vp)


def _mla_bwd(qc, kc, vp, dob, lse, delta):
    s_ = qc.shape[0]
    t = min(MLA_T, s_)
    grp = MLA_BWD_GROUP

    def body(q_ref, do_ref, lse_ref, dl_ref, k_ref, v_ref, dq_ref, dk_ref, dv_ref, dqt_s):
        qi = pl.program_id(1)

        @pl.when(qi == 0)
        def _():
            dk_ref[...] = jnp.zeros(dk_ref.shape, F32)
            dv_ref[...] = jnp.zeros(dv_ref.shape, F32)

        dqt_s[...] = jnp.zeros(dqt_s.shape, F32)

        def step(ki, masked):
            off = pl.multiple_of(ki * t, t)
            for g in range(grp):
                cs = slice(g * LANES, (g + 1) * LANES)
                q, d, k = q_ref[:, cs], do_ref[:, cs], k_ref[pl.ds(off, t), cs]
                pt = jnp.exp2(_causal_scores_t(k, q, t, masked) - lse_ref[g])
                dpt = lax.dot_general(v_ref[pl.ds(off, t), cs], d, NT, preferred_element_type=F32)
                dst = (pt * (dpt - dl_ref[g])).astype(MXU_DTYPE)
                dv_ref[pl.ds(off, t), cs] += jnp.dot(pt.astype(MXU_DTYPE), d, preferred_element_type=F32)
                dk_ref[pl.ds(off, t), cs] += jnp.dot(dst, q, preferred_element_type=F32)
                dqt_s[g] += lax.dot_general(k, dst, TN, preferred_element_type=F32)

        def full_block(ki, carry):
            step(ki, False)
            return carry

        lax.fori_loop(0, qi, full_block, 0)
        step(qi, True)
        for g in range(grp):
            dq_ref[:, g * LANES:(g + 1) * LANES] = dqt_s[g].T * MLA_SCALE

    qs, kv, row = _mla_specs(s_, t, grp)
    shp = jax.ShapeDtypeStruct((s_, MLA_HEADS * LANES), F32)
    return pl.pallas_call(
        body, name="mla_bwd", grid=(MLA_HEADS // grp, s_ // t), in_specs=[qs, qs, row, row, kv, kv],
        out_specs=[qs, kv, kv], out_shape=[shp, shp, shp], scratch_shapes=[pltpu.VMEM((grp, LANES, t), F32)],
        compiler_params=_cparams(("parallel", "arbitrary")),
    )(qc, dob, lse, delta, kc, vp)


def _pad_heads(w, nh, hd, axis):
    shp = w.shape
    w = w.reshape(shp[:axis] + (nh, hd) + shp[axis + 1:])
    pad = [(0, 0)] * w.ndim
    pad[axis + 1] = (0, LANES - hd)
    w = jnp.pad(w, pad)
    return w.reshape(shp[:axis] + (nh * LANES,) + shp[axis + 1:])


def _unpad_heads(w, nh, hd, axis):
    shp = w.shape
    w = w.reshape(shp[:axis] + (nh, LANES) + shp[axis + 1:])
    w = lax.slice_in_dim(w, 0, hd, axis=axis + 1)
    return w.reshape(shp[:axis] + (nh * hd,) + shp[axis + 1:])


PACK_W = 1024
ROW_TILE = 16
FULL_SHAPE = dict(w_in=(1024, 3488), w_uq=(384, 768), w_ukv=(256, 1024), w_o_swa=(512, 1024), w_o_mla=(512, 1024),
                  w_out=(1024, 1024), w_gate=(1024, 2816), w_up=(1024, 2816), w_down=(2816, 1024))
BIG = tuple(FULL_SHAPE)
ROW_SHARDED = ("w_out", "w_down")
W_IN_COLS = FULL_SHAPE["w_in"][1] // N_DEV
W_IN_ROWS = -(-W_IN_COLS // ROW_TILE) * ROW_TILE
FF_COLS = D_FF // N_DEV
OUT_ROWS = D_MODEL // N_DEV
SMALL_ROW0 = W_IN_ROWS + OUT_ROWS
SMALL_FLAT = (("w_uq", 0, 36), ("w_ukv", 48, 32), ("w_o_swa", 80, 64), ("w_o_mla", 144, 64))
SMALL_ROWS = 208
EARLY_ROWS = SMALL_ROW0 + SMALL_ROWS
LATE_ROWS = 3 * FF_COLS
PACK_ROWS = EARLY_ROWS + LATE_ROWS


def _shard_shape(n):
    r, c = FULL_SHAPE[n]
    return (r // N_DEV, c) if n in ROW_SHARDED else (r, c // N_DEV)


def _wire_pack(sh, dtype):
    c = lambda n: sh[n].astype(dtype)
    rows = [jnp.pad(c("w_in").T, ((0, W_IN_ROWS - W_IN_COLS), (0, 0))), c("w_out")]
    for n, _, r in SMALL_FLAT:
        rows.append(jnp.pad(c(n).reshape(r, PACK_W), ((0, -r % ROW_TILE), (0, 0))))
    return jnp.concatenate(rows + [c("w_gate").T, c("w_up").T, c("w_down")], 0)


MID_ROWS = OUT_ROWS + SMALL_ROWS


def _mid_unpack(p):
    out = dict(w_out=p[0:OUT_ROWS])
    for n, off, r in SMALL_FLAT:
        out[n] = p[OUT_ROWS + off:OUT_ROWS + off + r].reshape(_shard_shape(n))
    return out


def _w_in_row_maps():
    sp = lambda col: (col // W_IN_COLS) * W_IN_ROWS + col % W_IN_COLS
    fwd = np.full((P_W,), -1, np.int64)

    def put(t0, c0, n):
        fwd[t0:t0 + n] = [sp(c) for c in range(c0, c0 + n)]

    put(P_GA, IN_OFF[6], D_MODEL)
    put(P_GB, IN_OFF[7], D_MODEL)
    for h in range(SWA_HEADS):
        put(P_Q + LANES * h, IN_OFF[0] + HEAD_DIM * h, HEAD_DIM)
    put(P_QLAT, IN_OFF[3], Q_LORA)
    put(P_KR + KR_LANE, IN_OFF[5], MLA_ROPE)
    for h in range(SWA_KV_HEADS):
        put(P_K + LANES * h, IN_OFF[1] + HEAD_DIM * h, HEAD_DIM)
        put(P_V + LANES * h, IN_OFF[2] + HEAD_DIM * h, HEAD_DIM)
    put(P_KVLAT, IN_OFF[4], KV_LORA)
    inv = np.full((N_DEV * W_IN_ROWS,), -1, np.int64)
    inv[fwd[fwd >= 0]] = np.nonzero(fwd >= 0)[0]
    return fwd, inv


def _take_rows(src, idx, *, name):
    n_out, n_src, width = len(idx), src.shape[0], src.shape[1]
    assert n_out % BLOCK == 0 and n_src % BLOCK == 0
    n_tiles = n_out // BLOCK
    blocks = [sorted({int(v) // BLOCK for v in idx[i * BLOCK:(i + 1) * BLOCK] if v >= 0}) for i in range(n_tiles)]
    k_max = max(1, max(len(b) for b in blocks))
    tab = np.zeros((n_tiles, k_max), np.int32)
    sel = np.zeros((n_tiles, k_max, BLOCK, BLOCK), np.float32)
    for i, blks in enumerate(blocks):
        for m, b in enumerate(blks):
            tab[i, m] = b
            for r in range(BLOCK):
                v = int(idx[i * BLOCK + r])
                if v >= 0 and v // BLOCK == b:
                    sel[i, m, r, v % BLOCK] = 1.0

    def body(tab_ref, sel_ref, *refs):
        o_ref = refs[k_max]
        acc = jnp.dot(sel_ref[0, 0], refs[0][...], preferred_element_type=F32)
        for m in range(1, k_max):
            acc = acc + jnp.dot(sel_ref[0, m], refs[m][...], preferred_element_type=F32)
        o_ref[...] = acc.astype(o_ref.dtype)

    def src_spec(m):
        return pl.BlockSpec((BLOCK, width), lambda i, t: (t[i * k_max + m], 0))

    return pl.pallas_call(
        body, name=name,
        grid_spec=pltpu.PrefetchScalarGridSpec(
            num_scalar_prefetch=1, grid=(n_tiles,),
            in_specs=[pl.BlockSpec((1, k_max, BLOCK, BLOCK), lambda i, t: (i, 0, 0, 0))] + [src_spec(m) for m in range(k_max)],
            out_specs=pl.BlockSpec((BLOCK, width), lambda i, t: (i, 0))),
        out_shape=jax.ShapeDtypeStruct((n_out, width), src.dtype),
        compiler_params=_cparams(("parallel",)),
    )(jnp.asarray(tab.reshape(-1)), jnp.asarray(sel, src.dtype), *([src] * k_max))


def _w_in_operand(win_g):
    return _take_rows(win_g.reshape(N_DEV * W_IN_ROWS, PACK_W), _w_in_row_maps()[0], name="w_in_rows")


def _mid_operands(wout_g, small_g):
    def full(n, off, r):
        a = small_g[:, off:off + r].reshape((N_DEV,) + _shard_shape(n))
        return jnp.moveaxis(a, 0, 1).reshape(FULL_SHAPE[n])

    w = {n: full(n, off, r) for n, off, r in SMALL_FLAT}
    ukv = w["w_ukv"].reshape(KV_LORA, MLA_HEADS, MLA_NOPE + MLA_V)
    return dict(
        wout=wout_g.reshape(D_MODEL, D_MODEL),
        wuq=_pad_heads(w["w_uq"], MLA_HEADS, MLA_QK, 1),
        wuk=_pad_heads(ukv[:, :, :MLA_NOPE].reshape(KV_LORA, -1), MLA_HEADS, MLA_NOPE, 1),
        wuv=_pad_heads(ukv[:, :, MLA_NOPE:].reshape(KV_LORA, -1), MLA_HEADS, MLA_V, 1),
        woa=_pad_heads(w["w_o_swa"], SWA_HEADS, HEAD_DIM, 0),
        wob=_pad_heads(w["w_o_mla"], MLA_HEADS, MLA_V, 0),
    )


def _mid_grad_pack(g):
    uk = _unpad_heads(g["wukv"][:, :1024], MLA_HEADS, MLA_NOPE, 1).reshape(KV_LORA, MLA_HEADS, MLA_NOPE)
    uv = _unpad_heads(g["wukv"][:, 1024:], MLA_HEADS, MLA_V, 1).reshape(KV_LORA, MLA_HEADS, MLA_V)
    w = dict(w_uq=_unpad_heads(g["wuq"], MLA_HEADS, MLA_QK, 1), w_ukv=jnp.concatenate([uk, uv], 2).reshape(KV_LORA, -1),
             w_o_swa=_unpad_heads(g["woa"], SWA_HEADS, HEAD_DIM, 0), w_o_mla=_unpad_heads(g["wob"], MLA_HEADS, MLA_V, 0))

    def flat(n, r):
        rr, cc = FULL_SHAPE[n]
        a = jnp.moveaxis(w[n].reshape(rr, N_DEV, cc // N_DEV), 1, 0).reshape(N_DEV, r, PACK_W)
        return jnp.pad(a, ((0, 0), (0, -r % ROW_TILE), (0, 0))).astype(WIRE_DTYPE)

    return jnp.concatenate([g["wout"].reshape(N_DEV, OUT_ROWS, PACK_W)] + [flat(n, r) for n, _, r in SMALL_FLAT], 1)


def _w_in_grad_chunks(g_win_t):
    return _take_rows(g_win_t, _w_in_row_maps()[1], name="dw_in_rows").reshape(N_DEV, W_IN_ROWS, PACK_W)


def _local_step(x, tgt, win_t, small, weights, grads):
    s_ = x.shape[0]
    tabs = _rope_tables(s_)
    sink_b = jnp.broadcast_to(small["swa_sinks"].reshape(SWA_KV_HEADS, SWA_GROUP, 1), (SWA_KV_HEADS, SWA_GROUP, LANES))
    sink_b = jnp.pad(sink_b, ((0, 0), (0, SUBLANES - SWA_GROUP), (0, 0)))

    h, p = _norm_mm(x, small["mix_norm_g"], win_t, name="proj_in", tn=2176, tm=1024)
    qa, ka, va, cq, ckv, kro = _attn_prep(p, small["q_norm_g"], small["kv_norm_g"], tabs)
    ops = weights.mid(cq)
    oa32, oa16, lse_a = _swa_fwd(qa, ka, va, sink_b)
    qp = _mm(cq, ops["wuq"], "nn", name="mla_q_up", tm=1024, tn=1024)
    kp = _mm(ckv, ops["wuk"], "nn", name="mla_k_up", tm=1024, tn=1024)
    vp = _mm(ckv, ops["wuv"], "nn", name="mla_v_up", tm=1024, tn=1024, out_dtype=MXU_DTYPE)
    qc, kc = _mla_prep(qp, kp, kro, tabs)
    ob32, ob16, lse_b = _mla_fwd(qc, kc, vp)
    ta, tb, y = _attn_out_gate(oa16, ob16, ops["woa"], ops["wob"], p)
    x1 = _mm(y, ops["wout"], "nn", name="out_proj", add=x, tm=1024, tn=1024)
    wgu_t, wd = weights.late(x1)
    h2, gu, act = _ffn_in_act(x1, small["ffn_norm_g"], wgu_t)

    dx2, dx2b, dg3, _, tot = _ffn_out_loss(act, wd, x1, small["final_norm_g"].reshape(1, D_MODEL), tgt)
    g = {}
    g_wd = _mm(act, dx2b, "tn", name="dw_down", tm=1408, tn=1024, tk=1024, out_dtype=WIRE_DTYPE)
    dgu = _d_act_swiglu(dx2b, wd, gu)
    g_wgu = _mm(dgu, h2, "tn", name="dw_ffn_in", tm=1408, tn=1024, tk=1024, out_dtype=WIRE_DTYPE)
    token = grads.late(g_wgu, g_wd)
    dx1, dx1b, dg2 = _mm_norm_bwd(dgu, wgu_t, x1, small["ffn_norm_g"] + token[0:1, 0:1], dx2, name="d_h2", tk=2816)
    g["wout"] = _mm(y, dx1b, "tn", name="dw_out", tm=1024, tn=1024, tk=1024, out_dtype=WIRE_DTYPE)
    dta, dtb, dgab = _d_y_gate(dx1b, ops["wout"], p, ta, tb)
    doa = _mm(dta, ops["woa"], "nt", name="d_oa", tm=1024, tn=1024)
    g["woa"] = _mm(oa16, dta, "tn", name="dw_o_swa", tm=1024, tn=1024, tk=1024)
    dob = _mm(dtb, ops["wob"], "nt", name="d_ob", tm=1024, tn=1024)
    g["wob"] = _mm(ob16, dtb, "tn", name="dw_o_mla", tm=1024, tn=1024, tk=1024)
    dob16, delta_b = _mla_bwd_prep(dob, ob32)
    dqc, dkc, dvp = _mla_bwd(qc, kc, vp, dob16, lse_b, delta_b)
    dqp, dkv, dkr = _mla_unprep(dqc, dkc, dvp, tabs)
    dcq = _mm(dqp, ops["wuq"], "nt", name="d_cq", tn=Q_LORA)
    g["wuq"] = _mm(cq, dqp, "tn", name="dw_uq", tm=Q_LORA, tn=1024, tk=512)
    dckv = _mm(dkv, jnp.concatenate([ops["wuk"], ops["wuv"]], 1), "nt", name="d_ckv", tn=KV_LORA)
    g["wukv"] = _mm(ckv, dkv, "tn", name="dw_ukv", tm=KV_LORA, tn=1024, tk=512)
    token = grads.mid(g)
    _, dqlat, dgq = _norm_bwd(p, small["q_norm_g"] + token[0:1, 0:1], dcq, None, name="qnorm_bwd", x_cb=P_QLAT // Q_LORA)
    _, dkvlat, dgkv = _norm_bwd(p, small["kv_norm_g"], dckv, None, name="kvnorm_bwd", x_cb=P_KVLAT // KV_LORA)
    dqa, dka, dva, dsk = _swa_bwd(qa, ka, va, sink_b, oa32, doa, lse_a)
    dq_raw, dk_raw = _swa_unrope(dqa, dka, tabs)
    dp = jnp.concatenate([dgab, dq_raw, dqlat, dkr, dk_raw, dva, dkvlat], 1)
    token = grads.last(_mm(dp, h, "tn", name="dw_in", tm=2176, tn=1024, tk=1024, out_dtype=WIRE_DTYPE))
    gx, _, dg1 = _mm_norm_bwd(dp, win_t, x, small["mix_norm_g"], dx1, name="d_h", tk=2176, after=token)

    sm = dict(mix_norm_g=dg1, ffn_norm_g=dg2, final_norm_g=dg3, q_norm_g=dgq, kv_norm_g=dgkv,
              swa_sinks=dsk[:, :SWA_GROUP, 0].reshape(1, SWA_HEADS))
    return tot, gx, sm


MESH = pl.DeviceIdType.MESH
ANY = pl.BlockSpec(memory_space=pl.ANY)


def _position():
    return lax.axis_index("x"), lax.axis_index("y"), lax.axis_index("c")


def _all_gather(block, pieces, shapes, *, name):
    n_out = len(shapes)
    n_rows = sum(p[3] for p in pieces)

    def body(x_ref, *refs):
        outs, (send_sems, recv_sems, local_sem) = refs[:n_out], refs[n_out:]
        x, y, c = _position()
        me, sibling = (x, y, c), (x, y, 1 - c)
        chips = [(1 - x, y), (x, 1 - y), (1 - x, 1 - y)]

        def dst(piece, blk):
            arr, lead, _, _ = piece
            return outs[arr].at[lead(4 * blk[0] + 2 * blk[1] + blk[2])]

        def own(piece):
            return x_ref.at[pl.ds(piece[2], piece[3])]

        def copies(k, blk, to, from_input):
            return [pltpu.make_async_remote_copy(
                src_ref=own(p) if from_input else dst(p, blk), dst_ref=dst(p, blk), send_sem=send_sems.at[k],
                recv_sem=recv_sems.at[k], device_id=to, device_id_type=MESH) for p in pieces]

        gathered_rows = x_ref.at[pl.ds(0, n_rows)]

        def whole_block(k):
            return pltpu.make_async_remote_copy(src_ref=gathered_rows, dst_ref=gathered_rows, send_sem=send_sems.at[k],
                                                recv_sem=recv_sems.at[k], device_id=me, device_id_type=MESH)

        for p in pieces:
            pltpu.make_async_copy(own(p), dst(p, me), local_sem).start()
        for cp in copies(0, me, sibling, True):
            cp.start()
        for j, chip in enumerate(chips):
            for cp in copies(1 + j, me, (*chip, c), True):
                cp.start()
        for j, chip in enumerate(chips):
            whole_block(1 + j).wait_recv()
            for cp in copies(4 + j, (*chip, c), sibling, False):
                cp.start()
        whole_block(0).wait_recv()
        for j in range(3):
            whole_block(4 + j).wait_recv()
        for k in range(7):
            whole_block(k).wait_send()
        pltpu.make_async_copy(gathered_rows, gathered_rows, local_sem).wait()

    return pl.pallas_call(
        body, name=name, out_shape=[jax.ShapeDtypeStruct(s, block.dtype) for s in shapes], in_specs=[ANY],
        out_specs=[ANY] * n_out,
        scratch_shapes=[pltpu.SemaphoreType.DMA((7,)), pltpu.SemaphoreType.DMA((7,)), pltpu.SemaphoreType.DMA],
    )(block)


HBM = pl.BlockSpec(memory_space=pltpu.HBM)
SEM = pl.BlockSpec(memory_space=pltpu.SEMAPHORE)
TILE_DEVS = FF_TILE // FF_COLS
GU_SHAPE = (2, 2, TILE_DEVS, FF_COLS, PACK_W)


def _gate_slab(d):
    return (d // TILE_DEVS, 0, d % TILE_DEVS)


def _up_slab(d):
    return (d // TILE_DEVS, 1, d % TILE_DEVS)
D_SHAPE = (N_DEV, FF_COLS, PACK_W)
LAND_SHAPE = (N_DEV, LATE_ROWS, PACK_W)


def _split_params():
    return pltpu.CompilerParams(has_side_effects=pltpu.SideEffectType.DATAFLOW_SIDE_EFFECTING)


def _peer(x, y, c, k):
    return ((1 - x) if k & 4 else x, (1 - y) if k & 2 else y, (1 - c) if k & 1 else c)


def _empty_hbm(shape, dtype):
    return pltpu.with_memory_space_constraint(lax.empty(shape, dtype), pltpu.HBM)


def _wait_all(rows, send_sems, recv_sems, me):
    for k in range(N_DEV - 1):
        cp = pltpu.make_async_remote_copy(src_ref=rows, dst_ref=rows, send_sem=send_sems.at[k], recv_sem=recv_sems.at[k],
                                          device_id=me, device_id_type=MESH)
        cp.wait_send()
        cp.wait_recv()


def _token_shape():
    return jax.ShapeDtypeStruct((SUBLANES, LANES), F32)


def _gather_start(pack, row0, pieces, shapes, *, name):
    n = len(shapes)

    def body(*refs):
        p_ref, bufs, send_sems, recv_sems, token = refs[0], refs[1:1 + n], refs[1 + n], refs[2 + n], refs[-1]
        x, y, c = _position()
        me = 4 * x + 2 * y + c
        for k in range(1, N_DEV):
            off = row0
            for buf, lead, rows in pieces:
                pltpu.make_async_remote_copy(
                    src_ref=p_ref.at[pl.ds(off, rows)], dst_ref=bufs[buf].at[lead(me)], send_sem=send_sems.at[k - 1],
                    recv_sem=recv_sems.at[k - 1], device_id=_peer(x, y, c, k), device_id_type=MESH).start()
                off += rows
        token[...] = jnp.zeros_like(token)

    sems, dt = pltpu.SemaphoreType.DMA((N_DEV - 1,)), pack.dtype
    return pl.pallas_call(
        body, name=name,
        out_shape=(sems, sems, pltpu.HBM(pack.shape, dt)) + tuple(pltpu.HBM(s, dt) for s in shapes) + (_token_shape(),),
        in_specs=(HBM,) * (1 + n), out_specs=(SEM, SEM) + (HBM,) * (1 + n) + (pl.BlockSpec(memory_space=pltpu.VMEM),),
        input_output_aliases={i: 2 + i for i in range(1 + n)}, compiler_params=_split_params(),
    )(pltpu.with_memory_space_constraint(pack, pltpu.HBM), *[_empty_hbm(s, dt) for s in shapes])


def _gather_wait(started, row0, n_rows, after, *, name):
    send_sems, recv_sems, pack, *bufs = started[:-1]
    n = len(bufs)

    def body(*refs):
        _wait_all(refs[0].at[pl.ds(row0, n_rows)], refs[1 + n], refs[2 + n], _position())

    outs = pl.pallas_call(
        body, name=name, out_shape=tuple(pltpu.HBM(a.shape, a.dtype) for a in (pack, *bufs)),
        in_specs=(HBM,) * (1 + n) + (SEM, SEM, ANY), out_specs=(HBM,) * (1 + n),
        input_output_aliases={i: i for i in range(1 + n)}, compiler_params=_split_params(),
    )(pack, *bufs, send_sems, recv_sems, after)
    return outs[0], outs[1:]


def _scatter_start(srcs, pieces, *, name):
    n = len(srcs)
    land_shape = (N_DEV, sum(p[2] for p in pieces), PACK_W)

    def body(*refs):
        src_refs, land_ref, send_sems, recv_sems, token = refs[:n], refs[n], refs[n + 1], refs[n + 2], refs[-1]
        x, y, c = _position()
        me = 4 * x + 2 * y + c
        for k in range(1, N_DEV):
            px, py, pc = _peer(x, y, c, k)
            off = 0
            for si, lead, rows in pieces:
                pltpu.make_async_remote_copy(
                    src_ref=src_refs[si].at[lead(4 * px + 2 * py + pc)], dst_ref=land_ref.at[me, pl.ds(off, rows)],
                    send_sem=send_sems.at[k - 1], recv_sem=recv_sems.at[k - 1], device_id=(px, py, pc),
                    device_id_type=MESH).start()
                off += rows
        token[...] = jnp.zeros_like(token)

    sems, dt = pltpu.SemaphoreType.DMA((N_DEV - 1,)), srcs[0].dtype
    return pl.pallas_call(
        body, name=name,
        out_shape=(sems, sems) + tuple(pltpu.HBM(a.shape, dt) for a in srcs) + (pltpu.HBM(land_shape, dt), _token_shape()),
        in_specs=(HBM,) * (n + 1), out_specs=(SEM, SEM) + (HBM,) * (n + 1) + (pl.BlockSpec(memory_space=pltpu.VMEM),),
        input_output_aliases={i: 2 + i for i in range(n + 1)}, compiler_params=_split_params(),
    )(*[pltpu.with_memory_space_constraint(a, pltpu.HBM) for a in srcs], _empty_hbm(land_shape, dt))


def _scatter_wait(started, after, *, name):
    send_sems, recv_sems, *bufs = started[:-1]
    n = len(bufs)

    def body(*refs):
        _wait_all(refs[n - 1].at[0], refs[n], refs[n + 1], _position())

    return pl.pallas_call(
        body, name=name, out_shape=tuple(pltpu.HBM(a.shape, a.dtype) for a in bufs),
        in_specs=(HBM,) * n + (SEM, SEM, ANY), out_specs=(HBM,) * n, input_output_aliases={i: i for i in range(n)},
        compiler_params=_split_params(),
    )(*bufs, send_sems, recv_sems, after)


def _peer_sum(own, own_lead, land, block, rows, idx, *, name):
    lead_rank = own.ndim - 2

    def body(idx_ref, own_ref, *refs):
        o_ref = refs[N_DEV - 1]
        acc = own_ref[(0,) * lead_rank].astype(F32)
        for k in range(N_DEV - 1):
            acc = acc + refs[k][0].astype(F32)
        o_ref[...] = acc

    own_spec = pl.BlockSpec((1,) * lead_rank + (rows, PACK_W), lambda i, t: own_lead(t[0]) + (0, 0))

    def land_spec(k):
        return pl.BlockSpec((1, rows, PACK_W), lambda i, t: (t[k + 1], block, 0))

    return pl.pallas_call(
        body, name=name,
        grid_spec=pltpu.PrefetchScalarGridSpec(
            num_scalar_prefetch=1, grid=(1,), in_specs=[own_spec] + [land_spec(k) for k in range(N_DEV - 1)],
            out_specs=pl.BlockSpec((rows, PACK_W), lambda i, t: (0, 0))),
        out_shape=jax.ShapeDtypeStruct((rows, PACK_W), F32), compiler_params=_cparams(("arbitrary",)),
    )(idx, own, *([land] * (N_DEV - 1)))


def _adamw(w, g, m, v):
    m = ADAM_B1 * m + (1.0 - ADAM_B1) * g
    v = ADAM_B2 * v + (1.0 - ADAM_B2) * (g * g)
    m_hat = m / (1.0 - ADAM_B1 ** ADAM_STEP)
    v_hat = v / (1.0 - ADAM_B2 ** ADAM_STEP)
    delta = -ADAM_LR * (m_hat / (jnp.sqrt(v_hat) + ADAM_EPS) + ADAM_WD * w)
    return delta, m, v


def _adamw_call(w, g, m, v, *, name, max_rows=256):
    r, c_ = w.shape
    tr = max_rows if r > max_rows and r % max_rows == 0 else r

    def body(w_ref, g_ref, m_ref, v_ref, d_ref, mo_ref, vo_ref):
        d, mn, vn = _adamw(w_ref[...], g_ref[...], m_ref[...], v_ref[...])
        d_ref[...] = d
        mo_ref[...] = mn
        vo_ref[...] = vn

    row = pl.BlockSpec((tr, c_), lambda i: (i, 0))
    shp = jax.ShapeDtypeStruct((r, c_), F32)
    return pl.pallas_call(
        body, name=name, grid=(r // tr,), in_specs=[row] * 4, out_specs=[row] * 3, out_shape=[shp] * 3,
        compiler_params=_cparams(("parallel",)),
    )(w, g, m, v)


SMALL = ("mix_norm_g", "ffn_norm_g", "final_norm_g", "q_norm_g", "kv_norm_g", "swa_sinks")
SMALL_W = dict(mix_norm_g=1024, ffn_norm_g=1024, final_norm_g=1024, q_norm_g=Q_LORA, kv_norm_g=KV_LORA, swa_sinks=SWA_HEADS)


def _small_adamw(parts, w, m, v):
    n_par = parts.shape[1] // SUBLANES

    def body(p_ref, w_ref, m_ref, v_ref, g_ref, d_ref, mo_ref, vo_ref):
        tot = p_ref[0]
        for dev in range(1, N_DEV):
            tot = tot + p_ref[dev]
        row_id = lax.broadcasted_iota(jnp.int32, (SUBLANES, PACK_W), 0)
        g = jnp.zeros((SUBLANES, PACK_W), F32)
        for k in range(n_par):
            g = jnp.where(row_id == k, jnp.sum(tot[k * SUBLANES:(k + 1) * SUBLANES, :], axis=0, keepdims=True), g)
        d, mn, vn = _adamw(w_ref[...], g, m_ref[...], v_ref[...])
        g_ref[...] = g
        d_ref[...] = d
        mo_ref[...] = mn
        vo_ref[...] = vn

    shp = jax.ShapeDtypeStruct((SUBLANES, PACK_W), F32)
    vm = pl.BlockSpec(memory_space=pltpu.VMEM)
    return pl.pallas_call(body, name="small_adamw", in_specs=[vm] * 4, out_specs=[vm] * 4, out_shape=[shp] * 4)(parts, w, m, v)


def _small_pack(d, rows_each):
    parts = [jnp.pad(d[n].astype(F32), ((0, 0), (0, PACK_W - SMALL_W[n]))) for n in SMALL]
    out = jnp.concatenate(parts, 0)
    pad = -out.shape[0] % SUBLANES
    return jnp.pad(out, ((0, pad), (0, 0)))


def kernel(x, mix_norm_g, w_in, swa_sinks, q_norm_g, w_uq, kv_norm_g, w_ukv, w_o_swa, w_o_mla, w_out, ffn_norm_g, w_gate, w_up, w_down, final_norm_g, loss_target, m_mix_norm_g, m_w_in, m_swa_sinks, m_q_norm_g, m_w_uq, m_kv_norm_g, m_w_ukv, m_w_o_swa, m_w_o_mla, m_w_out, m_ffn_norm_g, m_w_gate, m_w_up, m_w_down, m_final_norm_g, v_mix_norm_g, v_w_in, v_swa_sinks, v_q_norm_g, v_w_uq, v_kv_norm_g, v_w_ukv, v_w_o_swa, v_w_o_mla, v_w_out, v_ffn_norm_g, v_w_gate, v_w_up, v_w_down, v_final_norm_g):
    big_w = dict(w_in=w_in[0], w_uq=w_uq[0], w_ukv=w_ukv[0], w_o_swa=w_o_swa[0], w_o_mla=w_o_mla[0], w_out=w_out[0],
                 w_gate=w_gate[0], w_up=w_up[0], w_down=w_down[0])
    big_m = dict(w_in=m_w_in[0], w_uq=m_w_uq[0], w_ukv=m_w_ukv[0], w_o_swa=m_w_o_swa[0], w_o_mla=m_w_o_mla[0],
                 w_out=m_w_out[0], w_gate=m_w_gate[0], w_up=m_w_up[0], w_down=m_w_down[0])
    big_v = dict(w_in=v_w_in[0], w_uq=v_w_uq[0], w_ukv=v_w_ukv[0], w_o_swa=v_w_o_swa[0], w_o_mla=v_w_o_mla[0],
                 w_out=v_w_out[0], w_gate=v_w_gate[0], w_up=v_w_up[0], w_down=v_w_down[0])
    small_w = dict(mix_norm_g=mix_norm_g, ffn_norm_g=ffn_norm_g, final_norm_g=final_norm_g.reshape(1, D_MODEL),
                   q_norm_g=q_norm_g, kv_norm_g=kv_norm_g, swa_sinks=swa_sinks)
    small_m = dict(mix_norm_g=m_mix_norm_g, ffn_norm_g=m_ffn_norm_g, final_norm_g=m_final_norm_g.reshape(1, D_MODEL),
                   q_norm_g=m_q_norm_g, kv_norm_g=m_kv_norm_g, swa_sinks=m_swa_sinks)
    small_v = dict(mix_norm_g=v_mix_norm_g, ffn_norm_g=v_ffn_norm_g, final_norm_g=v_final_norm_g.reshape(1, D_MODEL),
                   q_norm_g=v_q_norm_g, kv_norm_g=v_kv_norm_g, swa_sinks=v_swa_sinks)

    px, py, pc = _position()
    me = 4 * px + 2 * py + pc
    idx = jnp.stack([me] + [4 * qx + 2 * qy + qc for qx, qy, qc in (_peer(px, py, pc, k) for k in range(1, N_DEV))])
    idx = idx.astype(jnp.int32)

    dev = lambda d: (d,)
    pack = _wire_pack(big_w, WIRE_DTYPE)
    win_g, = _all_gather(pack, ((0, dev, 0, W_IN_ROWS),), ((N_DEV, W_IN_ROWS, PACK_W),), name="ag_early")
    ag_mid = _gather_start(pack, W_IN_ROWS, ((0, dev, OUT_ROWS), (1, dev, SMALL_ROWS)),
                           ((N_DEV, OUT_ROWS, PACK_W), (N_DEV, SMALL_ROWS, PACK_W)), name="ag_mid_start")
    ag = {}

    def own_rows(r0, r1, shape):
        return pack[r0:r1].reshape(shape)

    def mid_weights(after):
        pack_mid, (wout_g, small_g) = _gather_wait(ag_mid, W_IN_ROWS, MID_ROWS, after, name="ag_mid_wait")
        ag["late"] = _gather_start(pack_mid, EARLY_ROWS, ((0, _gate_slab, FF_COLS), (0, _up_slab, FF_COLS), (1, dev, FF_COLS)),
                                   (GU_SHAPE, D_SHAPE), name="ag_late_start")
        wout_g = lax.dynamic_update_slice(wout_g, own_rows(W_IN_ROWS, SMALL_ROW0, (1, OUT_ROWS, PACK_W)), (me, 0, 0))
        small_g = lax.dynamic_update_slice(small_g, own_rows(SMALL_ROW0, EARLY_ROWS, (1, SMALL_ROWS, PACK_W)), (me, 0, 0))
        ops = _mid_operands(wout_g, small_g)
        ops["wuq"] = ops["wuq"] + ag["late"][-1][0:1, 0:1].astype(ops["wuq"].dtype)
        return ops

    def late_weights(after):
        _, (gu, d) = _gather_wait(ag["late"], EARLY_ROWS, LATE_ROWS, after, name="ag_late_wait")
        slab = (1, 1, 1, FF_COLS, PACK_W)
        gu = lax.dynamic_update_slice(gu, own_rows(EARLY_ROWS, EARLY_ROWS + FF_COLS, slab), _gate_slab(me) + (0, 0))
        gu = lax.dynamic_update_slice(gu, own_rows(EARLY_ROWS + FF_COLS, EARLY_ROWS + 2 * FF_COLS, slab), _up_slab(me) + (0, 0))
        d = lax.dynamic_update_slice(d, own_rows(EARLY_ROWS + 2 * FF_COLS, PACK_ROWS, (1, FF_COLS, PACK_W)), (me, 0, 0))
        return gu.reshape(2 * D_FF, D_MODEL), d.reshape(D_FF, D_MODEL)

    rs = {}

    def late_grads(g_gu, g_d):
        rs["late"] = _scatter_start([g_gu.reshape(GU_SHAPE), g_d.reshape(D_SHAPE)],
                                    ((0, _gate_slab, FF_COLS), (0, _up_slab, FF_COLS), (1, dev, FF_COLS)),
                                    name="rs_late_start")
        return rs["late"][-1]

    def mid_grads(g):
        rs["mid"] = _scatter_start([_mid_grad_pack(g)], ((0, dev, MID_ROWS),), name="rs_mid_start")
        return rs["mid"][-1]

    def last_grads(g_win_t):
        rs["last"] = _scatter_start([_w_in_grad_chunks(g_win_t)], ((0, dev, W_IN_ROWS),), name="rs_last_start")
        return rs["last"][-1]

    first_w = dict(small_w, mix_norm_g=mix_norm_g + ag_mid[-1][0:1, 0:1])
    loss_tot, gx, g_small = _local_step(
        x[0], loss_target[0], _w_in_operand(win_g), first_w, types.SimpleNamespace(mid=mid_weights, late=late_weights),
        types.SimpleNamespace(late=late_grads, mid=mid_grads, last=last_grads))

    g_gu, g_d, land_late = _scatter_wait(rs["late"], gx, name="rs_late_wait")
    g_mid, land_mid = _scatter_wait(rs["mid"], gx, name="rs_mid_wait")
    g_win, land_last = _scatter_wait(rs["last"], gx, name="rs_last_wait")
    gw = dict(w_gate=_peer_sum(g_gu, _gate_slab, land_late, 0, FF_COLS, idx, name="rs_sum_gate").T,
              w_up=_peer_sum(g_gu, _up_slab, land_late, 1, FF_COLS, idx, name="rs_sum_up").T,
              w_down=_peer_sum(g_d, dev, land_late, 2, FF_COLS, idx, name="rs_sum_down"),
              w_in=_peer_sum(g_win, dev, land_last, 0, W_IN_ROWS, idx, name="rs_sum_in")[0:W_IN_COLS].T)
    gw.update(_mid_unpack(_peer_sum(g_mid, dev, land_mid, 0, MID_ROWS, idx, name="rs_sum_mid")))
    dw, mw, vw = {}, {}, {}
    for n in BIG:
        dw[n], mw[n], vw[n] = _adamw_call(big_w[n], gw[n], big_m[n], big_v[n], name="adamw_" + n)

    loss_rows = jnp.pad(loss_tot[0:1, 0:1], ((0, SUBLANES - 1), (0, PACK_W - 1)))
    small_rows = jnp.concatenate([_small_pack(g_small_rows(g_small), SUBLANES), loss_rows], 0)
    parts, = _all_gather(small_rows, ((0, lambda d: (d,), 0, small_rows.shape[0]),), ((N_DEV,) + small_rows.shape,),
                         name="ag_small")
    gs, ds, ms, vs = _small_adamw(parts, _small_pack(small_w, 1), _small_pack(small_m, 1), _small_pack(small_v, 1))
    loss = gs[len(SMALL), 0]

    def small_out(packed):
        out = {}
        for k, n in enumerate(SMALL):
            out[n] = packed[k:k + 1, :SMALL_W[n]]
        out["final_norm_g"] = out["final_norm_g"].reshape(D_MODEL)
        return out

    gs, ds, ms, vs = small_out(gs), small_out(ds), small_out(ms), small_out(vs)

    order = ("mix_norm_g", "w_in", "swa_sinks", "q_norm_g", "w_uq", "kv_norm_g", "w_ukv", "w_o_swa", "w_o_mla", "w_out",
             "ffn_norm_g", "w_gate", "w_up", "w_down", "final_norm_g")

    def leaves(big, small):
        return [big[n][None] if n in big else small[n] for n in order]

    return (loss, gx[None], *leaves(gw, gs), *leaves(dw, ds), *leaves(mw, ms), *leaves(vw, vs))


def g_small_rows(g_small):
    out = dict(g_small)
    out["swa_sinks"] = jnp.pad(g_small["swa_sinks"], ((0, SUBLANES - 1), (0, 0)))
    return out
```

```python
import types

import numpy as np
import jax
import jax.numpy as jnp
from jax import lax
from jax.experimental import pallas as pl
from jax.experimental.pallas import tpu as pltpu

F32 = jnp.float32
MXU_DTYPE = jnp.bfloat16
WIRE_DTYPE = jnp.bfloat16

D_MODEL = 1024
EPS = 1e-6
ROPE_THETA = 10000.0
BLOCK = 128
HEAD_DIM = 64
SWA_HEADS = 8
SWA_KV_HEADS = 2
SWA_GROUP = SWA_HEADS // SWA_KV_HEADS
MLA_HEADS = 8
MLA_NOPE = 64
MLA_ROPE = 32
MLA_V = 64
MLA_QK = MLA_NOPE + MLA_ROPE
Q_LORA = 384
KV_LORA = 256
D_FF = 2816
IN_SIZES = (512, 128, 128, Q_LORA, KV_LORA, MLA_ROPE, D_MODEL, D_MODEL)
IN_OFF = tuple(int(v) for v in np.cumsum((0,) + IN_SIZES))
ADAM_LR, ADAM_B1, ADAM_B2, ADAM_EPS, ADAM_WD, ADAM_STEP = 0.001, 0.9, 0.999, 1e-08, 0.01, 10

LANES = 128
SUBLANES = 8
VMEM_LIMIT = 48 * 1024 * 1024
N_DEV = 8
AXES = ("x", "y", "c")

P_GA, P_GB, P_Q, P_QLAT, P_KR, P_K, P_V, P_KVLAT, P_W = 0, 1024, 2048, 3072, 3456, 3584, 3840, 4096, 4352
KR_LANE = 64

LOG2E = 1.4426950408889634

NT = (((1,), (1,)), ((), ()))
NN = (((1,), (0,)), ((), ()))
TN = (((0,), (0,)), ((), ()))


def _cparams(sem):
    return pltpu.CompilerParams(dimension_semantics=sem, vmem_limit_bytes=VMEM_LIMIT)


def _mm(a, b, mode, *, name, out_dtype=F32, add=None, after=None, tm=512, tn=512, tk=None):
    if mode == "nn":
        (M, K), (K2, N) = a.shape, b.shape
    elif mode == "nt":
        (M, K), (N, K2) = a.shape, b.shape
    else:
        (K, M), (K2, N) = a.shape, b.shape
    assert K == K2, (a.shape, b.shape, mode)
    tk = K if tk is None else tk
    tm, tn = min(tm, M), min(tn, N)
    assert M % tm == 0 and N % tn == 0 and K % tk == 0, (M, N, K, tm, tn, tk)
    nk = K // tk
    dn = {"nn": NN, "nt": NT, "tn": TN}[mode]
    if mode == "tn":
        a_spec = pl.BlockSpec((tk, tm), lambda i, j, k: (k, i))
    else:
        a_spec = pl.BlockSpec((tm, tk), lambda i, j, k: (i, k))
    if mode == "nt":
        b_spec = pl.BlockSpec((tn, tk), lambda i, j, k: (j, k))
    else:
        b_spec = pl.BlockSpec((tk, tn), lambda i, j, k: (k, j))
    o_spec = pl.BlockSpec((tm, tn), lambda i, j, k: (i, j))
    has_add, has_after = add is not None, after is not None

    def body(*refs):
        a_ref, b_ref = refs[0], refs[1]
        add_ref = refs[2] if has_add else None
        o_ref = refs[2 + has_add + has_after]
        p = lax.dot_general(a_ref[...], b_ref[...], dn, preferred_element_type=F32)

        def finish(acc):
            if has_add:
                acc = acc + add_ref[...]
            o_ref[...] = acc.astype(o_ref.dtype)

        if nk == 1:
            finish(p)
        else:
            acc_ref = refs[-1]
            k = pl.program_id(2)

            @pl.when(k == 0)
            def _():
                acc_ref[...] = p

            @pl.when(k > 0)
            def _():
                acc_ref[...] += p

            @pl.when(k == nk - 1)
            def _():
                finish(acc_ref[...])

    ins = [a, b] + ([add] if has_add else []) + ([after] if has_after else [])
    in_specs = [a_spec, b_spec] + ([o_spec] if has_add else []) + ([pl.BlockSpec(memory_space=pl.ANY)] if has_after else [])
    return pl.pallas_call(
        body, name=name, grid=(M // tm, N // tn, nk), in_specs=in_specs, out_specs=o_spec,
        out_shape=jax.ShapeDtypeStruct((M, N), out_dtype),
        scratch_shapes=[pltpu.VMEM((tm, tn), F32)] if nk > 1 else [],
        compiler_params=_cparams(("parallel", "parallel", "arbitrary")),
    )(*ins)


def _rows(ts, w, cb=0):
    return pl.BlockSpec((ts, w), lambda i: (i, cb))


def _const(r, w):
    return pl.BlockSpec((r, w), lambda i: (0, 0))


def _sublane_sum(v):
    ts, c = v.shape
    return jnp.sum(v.reshape(ts // SUBLANES, SUBLANES, c), axis=0)


def _sigmoid(v):
    return 1.0 / (1.0 + jnp.exp(-v))


def _rope(v, cos, s_up, s_dn, up, dn):
    return v * cos + pltpu.roll(v, up, 1) * s_up + pltpu.roll(v, dn, 1) * s_dn


def _rope_t(dv, cos, s_up, s_dn, up, dn):
    return dv * cos + pltpu.roll(dv * s_up, dn, 1) + pltpu.roll(dv * s_dn, up, 1)


def _rope_tables(seq):
    pos = np.arange(seq, dtype=np.float32)[:, None]

    def base(dim):
        inv = np.float32(ROPE_THETA) ** (-np.arange(0, dim, 2, dtype=np.float32) / np.float32(dim))
        ang = (pos * inv.astype(np.float32)[None, :]).astype(np.float32)
        return np.cos(ang).astype(np.float32), np.sin(ang).astype(np.float32)

    z = lambda n: np.zeros((seq, n), np.float32)
    ca, sa = base(HEAD_DIM)
    a_cos = np.concatenate([ca, ca, z(64)], 1)
    a_up = np.concatenate([-sa, z(96)], 1)
    a_dn = np.concatenate([z(32), sa, z(64)], 1)
    cb, sb = base(MLA_ROPE)
    one = np.ones((seq, 64), np.float32)
    q_cos = np.concatenate([one, cb, cb, z(32)], 1)
    k_cos = np.concatenate([z(64), cb, cb, z(32)], 1)
    b_up = np.concatenate([z(64), -sb, z(48)], 1)
    b_dn = np.concatenate([z(80), sb, z(32)], 1)
    return tuple(jnp.asarray(t) for t in (a_cos, a_up, a_dn, q_cos, k_cos, b_up, b_dn))


def _rms(v, g):
    return v * lax.rsqrt(jnp.mean(v * v, axis=-1, keepdims=True) + EPS) * g


def _rms_bwd(v, g, d):
    r = lax.rsqrt(jnp.mean(v * v, axis=-1, keepdims=True) + EPS)
    xh = v * r
    dxh = d * g
    return r * (dxh - xh * jnp.mean(dxh * xh, axis=-1, keepdims=True)), d * xh


def _norm_mm(x, g, w_t, *, name, tn, tm=512):
    s_, c = x.shape
    n = w_t.shape[0]

    def body(x_ref, g_ref, w_ref, h_ref, o_ref):
        h = _rms(x_ref[...], g_ref[...]).astype(h_ref.dtype)
        h_ref[...] = h
        o_ref[...] = lax.dot_general(h, w_ref[...], NT, preferred_element_type=F32)

    return pl.pallas_call(
        body, name=name, grid=(s_ // tm, n // tn),
        in_specs=[pl.BlockSpec((tm, c), lambda i, j: (i, 0)), pl.BlockSpec((1, c), lambda i, j: (0, 0)),
                  pl.BlockSpec((tn, c), lambda i, j: (j, 0))],
        out_specs=[pl.BlockSpec((tm, c), lambda i, j: (i, 0)), pl.BlockSpec((tm, tn), lambda i, j: (i, j))],
        out_shape=[jax.ShapeDtypeStruct((s_, c), MXU_DTYPE), jax.ShapeDtypeStruct((s_, n), F32)],
        compiler_params=_cparams(("parallel", "arbitrary")),
    )(x, g, w_t)


def _mm_norm_bwd(a, b, x, g, res, *, name, tk, after=None, tm=512):
    s_, kk = a.shape
    c = b.shape[1]
    nk = kk // tk
    has_after = after is not None

    def body(*refs):
        a_ref, b_ref, x_ref, g_ref, res_ref = refs[:5]
        dx_ref, dxb_ref, dg_ref, acc_ref = refs[5 + has_after:]
        i, k = pl.program_id(0), pl.program_id(1)
        p = jnp.dot(a_ref[...], b_ref[...], preferred_element_type=F32)

        @pl.when(k == 0)
        def _():
            acc_ref[...] = p

        @pl.when(k > 0)
        def _():
            acc_ref[...] += p

        @pl.when(k == nk - 1)
        def _():
            dx, gg = _rms_bwd(x_ref[...], g_ref[...], acc_ref[...])
            dx = dx + res_ref[...]
            dx_ref[...] = dx
            dxb_ref[...] = dx.astype(dxb_ref.dtype)

            @pl.when(i == 0)
            def _():
                dg_ref[...] = jnp.zeros(dg_ref.shape, F32)

            dg_ref[...] += _sublane_sum(gg)

    row = pl.BlockSpec((tm, c), lambda i, k: (i, 0))
    in_specs = [pl.BlockSpec((tm, tk), lambda i, k: (i, k)), pl.BlockSpec((tk, c), lambda i, k: (k, 0)), row,
                pl.BlockSpec((1, c), lambda i, k: (0, 0)), row] + ([pl.BlockSpec(memory_space=pl.ANY)] if has_after else [])
    return pl.pallas_call(
        body, name=name, grid=(s_ // tm, nk), in_specs=in_specs,
        out_specs=[row, row, pl.BlockSpec((SUBLANES, c), lambda i, k: (0, 0))],
        out_shape=[jax.ShapeDtypeStruct((s_, c), F32), jax.ShapeDtypeStruct((s_, c), MXU_DTYPE),
                   jax.ShapeDtypeStruct((SUBLANES, c), F32)],
        scratch_shapes=[pltpu.VMEM((tm, c), F32)], compiler_params=_cparams(("arbitrary", "arbitrary")),
    )(*([a, b, x, g, res] + ([after] if has_after else [])))


def _norm_bwd(x, g, dy, res, *, name, ts=256, x_cb=0, x_src_w=None):
    s_ = x.shape[0]
    c = dy.shape[1]
    has_res = res is not None

    def body(*refs):
        x_ref, g_ref, dy_ref = refs[0], refs[1], refs[2]
        res_ref = refs[3] if has_res else None
        dx_ref, dxb_ref, dg_ref = refs[-3], refs[-2], refs[-1]
        v = x_ref[...]
        r = lax.rsqrt(jnp.mean(v * v, axis=-1, keepdims=True) + EPS)
        xh = v * r
        d = dy_ref[...]
        dxh = d * g_ref[...]
        dx = r * (dxh - xh * jnp.mean(dxh * xh, axis=-1, keepdims=True))
        if has_res:
            dx = dx + res_ref[...]
        dx_ref[...] = dx
        dxb_ref[...] = dx.astype(dxb_ref.dtype)

        @pl.when(pl.program_id(0) == 0)
        def _():
            dg_ref[...] = jnp.zeros(dg_ref.shape, F32)

        dg_ref[...] += _sublane_sum(d * xh)

    ins = [x, g, dy] + ([res] if has_res else [])
    in_specs = [_rows(ts, c, x_cb), _const(1, c), _rows(ts, c)] + ([_rows(ts, c)] if has_res else [])
    return pl.pallas_call(
        body, name=name, grid=(s_ // ts,), in_specs=in_specs,
        out_specs=[_rows(ts, c), _rows(ts, c), _const(SUBLANES, c)],
        out_shape=[jax.ShapeDtypeStruct((s_, c), F32), jax.ShapeDtypeStruct((s_, c), MXU_DTYPE),
                   jax.ShapeDtypeStruct((SUBLANES, c), F32)],
        compiler_params=_cparams(("arbitrary",)),
    )(*ins)


def _attn_prep(p, gq, gkv, tabs, *, ts=256):
    s_ = p.shape[0]
    a_cos, a_up, a_dn, _, k_cos, b_up, b_dn = tabs

    def body(q_ref, k_ref, v_ref, ql_ref, kvl_ref, kr_ref, gq_ref, gkv_ref, ac, au, ad, kc, bu, bd,
             qa_ref, ka_ref, va_ref, cq_ref, ckv_ref, kro_ref):
        c_, u_, d_ = ac[...], au[...], ad[...]
        for h in range(SWA_HEADS):
            sl = slice(h * LANES, (h + 1) * LANES)
            qa_ref[:, sl] = _rope(q_ref[:, sl], c_, u_, d_, 96, 32).astype(qa_ref.dtype)
        for h in range(SWA_KV_HEADS):
            sl = slice(h * LANES, (h + 1) * LANES)
            ka_ref[:, sl] = _rope(k_ref[:, sl], c_, u_, d_, 96, 32).astype(ka_ref.dtype)
        va_ref[...] = v_ref[...].astype(va_ref.dtype)
        for src, gref, dst in ((ql_ref, gq_ref, cq_ref), (kvl_ref, gkv_ref, ckv_ref)):
            v = src[...]
            r = lax.rsqrt(jnp.mean(v * v, axis=-1, keepdims=True) + EPS)
            dst[...] = (v * r * gref[...]).astype(dst.dtype)
        kro_ref[...] = _rope(kr_ref[...], kc[...], bu[...], bd[...], 112, 16)

    tab = _rows(ts, LANES)
    return pl.pallas_call(
        body, name="attn_prep", grid=(s_ // ts,),
        in_specs=[_rows(ts, 1024, P_Q // 1024), _rows(ts, 256, P_K // 256), _rows(ts, 256, P_V // 256),
                  _rows(ts, Q_LORA, P_QLAT // Q_LORA), _rows(ts, KV_LORA, P_KVLAT // KV_LORA),
                  _rows(ts, LANES, P_KR // LANES), _const(1, Q_LORA), _const(1, KV_LORA), tab, tab, tab, tab, tab, tab],
        out_specs=[_rows(ts, 1024), _rows(ts, 256), _rows(ts, 256), _rows(ts, Q_LORA), _rows(ts, KV_LORA),
                   _rows(ts, LANES)],
        out_shape=[jax.ShapeDtypeStruct((s_, 1024), MXU_DTYPE), jax.ShapeDtypeStruct((s_, 256), MXU_DTYPE),
                   jax.ShapeDtypeStruct((s_, 256), MXU_DTYPE), jax.ShapeDtypeStruct((s_, Q_LORA), MXU_DTYPE),
                   jax.ShapeDtypeStruct((s_, KV_LORA), MXU_DTYPE), jax.ShapeDtypeStruct((s_, LANES), F32)],
        compiler_params=_cparams(("parallel",)),
    )(p, p, p, p, p, p, gq, gkv, a_cos, a_up, a_dn, k_cos, b_up, b_dn)


def _mla_prep(qp, kp, kro, tabs, *, ts=256):
    s_ = qp.shape[0]
    _, _, _, q_cos, _, b_up, b_dn = tabs

    def body(q_ref, k_ref, kr_ref, qc, bu, bd, qo_ref, ko_ref):
        c_, u_, d_ = qc[...], bu[...], bd[...]
        kr = kr_ref[...]
        for h in range(MLA_HEADS):
            sl = slice(h * LANES, (h + 1) * LANES)
            qo_ref[:, sl] = _rope(q_ref[:, sl], c_, u_, d_, 112, 16).astype(qo_ref.dtype)
            ko_ref[:, sl] = (k_ref[:, sl] + kr).astype(ko_ref.dtype)

    tab = _rows(ts, LANES)
    return pl.pallas_call(
        body, name="mla_prep", grid=(s_ // ts,),
        in_specs=[_rows(ts, 1024), _rows(ts, 1024), tab, tab, tab, tab],
        out_specs=[_rows(ts, 1024), _rows(ts, 1024)],
        out_shape=[jax.ShapeDtypeStruct((s_, 1024), MXU_DTYPE)] * 2,
        compiler_params=_cparams(("parallel",)),
    )(qp, kp, kro, q_cos, b_up, b_dn)


def _mla_unprep(dqc, dkc, dvp, tabs, *, ts=256):
    s_ = dqc.shape[0]
    _, _, _, q_cos, k_cos, b_up, b_dn = tabs

    def body(dq_ref, dk_ref, dv_ref, qc, kc, bu, bd, dqo_ref, dkvo_ref, dkr_ref):
        c_, u_, d_ = qc[...], bu[...], bd[...]
        tot = jnp.zeros((ts, LANES), F32)
        for h in range(MLA_HEADS):
            sl = slice(h * LANES, (h + 1) * LANES)
            dqo_ref[:, sl] = _rope_t(dq_ref[:, sl], c_, u_, d_, 112, 16).astype(dqo_ref.dtype)
            dk = dk_ref[:, sl]
            dkvo_ref[:, sl] = dk.astype(dkvo_ref.dtype)
            tot = tot + dk
        dkvo_ref[:, 1024:2048] = dv_ref[...].astype(dkvo_ref.dtype)
        dkr_ref[...] = _rope_t(tot, kc[...], u_, d_, 112, 16).astype(dkr_ref.dtype)

    tab = _rows(ts, LANES)
    return pl.pallas_call(
        body, name="mla_unprep", grid=(s_ // ts,),
        in_specs=[_rows(ts, 1024), _rows(ts, 1024), _rows(ts, 1024), tab, tab, tab, tab],
        out_specs=[_rows(ts, 1024), _rows(ts, 2048), _rows(ts, LANES)],
        out_shape=[jax.ShapeDtypeStruct((s_, 1024), MXU_DTYPE), jax.ShapeDtypeStruct((s_, 2048), MXU_DTYPE),
                   jax.ShapeDtypeStruct((s_, LANES), MXU_DTYPE)],
        compiler_params=_cparams(("parallel",)),
    )(dqc, dkc, dvp, q_cos, k_cos, b_up, b_dn)


def _assemble_dp(dgab, dqa, dqlat, dkr, dka, dva, dkvlat, tabs, *, ts=256):
    s_ = dqa.shape[0]
    a_cos, a_up, a_dn = tabs[0], tabs[1], tabs[2]

    def body(dg_ref, dq_ref, dql_ref, dkr_ref, dk_ref, dv_ref, dkvl_ref, ac, au, ad, o_ref):
        c_, u_, d_ = ac[...], au[...], ad[...]
        o_ref[:, P_GA:P_Q] = dg_ref[...]
        for h in range(SWA_HEADS):
            sl = slice(h * LANES, (h + 1) * LANES)
            o_ref[:, P_Q + h * LANES:P_Q + (h + 1) * LANES] = _rope_t(dq_ref[:, sl], c_, u_, d_, 96, 32).astype(o_ref.dtype)
        o_ref[:, P_QLAT:P_KR] = dql_ref[...]
        o_ref[:, P_KR:P_K] = dkr_ref[...]
        for h in range(SWA_KV_HEADS):
            sl = slice(h * LANES, (h + 1) * LANES)
            o_ref[:, P_K + h * LANES:P_K + (h + 1) * LANES] = _rope_t(dk_ref[:, sl], c_, u_, d_, 96, 32).astype(o_ref.dtype)
        o_ref[:, P_V:P_KVLAT] = dv_ref[...]
        o_ref[:, P_KVLAT:P_W] = dkvl_ref[...]

    tab = _rows(ts, LANES)
    return pl.pallas_call(
        body, name="assemble_dp", grid=(s_ // ts,),
        in_specs=[_rows(ts, 2048), _rows(ts, 1024), _rows(ts, Q_LORA), _rows(ts, LANES), _rows(ts, 256), _rows(ts, 256),
                  _rows(ts, KV_LORA), tab, tab, tab],
        out_specs=_rows(ts, P_W), out_shape=jax.ShapeDtypeStruct((s_, P_W), MXU_DTYPE),
        compiler_params=_cparams(("parallel",)),
    )(dgab, dqa, dqlat, dkr, dka, dva, dkvlat, a_cos, a_up, a_dn)


def _attn_out_gate(oa, ob, woa, wob, p, *, ts=512):
    s_ = p.shape[0]

    def body(oa_ref, ob_ref, wa_ref, wb_ref, ga_ref, gb_ref, ta_ref, tb_ref, y_ref):
        ta = jnp.dot(oa_ref[...], wa_ref[...], preferred_element_type=F32)
        tb = jnp.dot(ob_ref[...], wb_ref[...], preferred_element_type=F32)
        ta_ref[...] = ta
        tb_ref[...] = tb
        y_ref[...] = (_sigmoid(ga_ref[...]) * ta + _sigmoid(gb_ref[...]) * tb).astype(y_ref.dtype)

    w = _const(1024, 1024)
    return pl.pallas_call(
        body, name="attn_out_gate", grid=(s_ // ts,),
        in_specs=[_rows(ts, 1024), _rows(ts, 1024), w, w, _rows(ts, 1024, P_GA // 1024), _rows(ts, 1024, P_GB // 1024)],
        out_specs=[_rows(ts, 1024)] * 3,
        out_shape=[jax.ShapeDtypeStruct((s_, 1024), F32)] * 2 + [jax.ShapeDtypeStruct((s_, 1024), MXU_DTYPE)],
        compiler_params=_cparams(("parallel",)),
    )(oa, ob, woa, wob, p, p)


def _d_y_gate(dx1b, wout, p, ta, tb, *, ts=512):
    s_ = p.shape[0]

    def body(dx_ref, w_ref, ga_ref, gb_ref, ta_ref, tb_ref, dta_ref, dtb_ref, dg_ref):
        d = lax.dot_general(dx_ref[...], w_ref[...], NT, preferred_element_type=F32)
        sa, sb = _sigmoid(ga_ref[...]), _sigmoid(gb_ref[...])
        dta_ref[...] = (d * sa).astype(dta_ref.dtype)
        dtb_ref[...] = (d * sb).astype(dtb_ref.dtype)
        dg_ref[:, 0:1024] = (d * ta_ref[...] * (sa * (1.0 - sa))).astype(dg_ref.dtype)
        dg_ref[:, 1024:2048] = (d * tb_ref[...] * (sb * (1.0 - sb))).astype(dg_ref.dtype)

    return pl.pallas_call(
        body, name="d_y_gate", grid=(s_ // ts,),
        in_specs=[_rows(ts, 1024), _const(1024, 1024), _rows(ts, 1024, P_GA // 1024), _rows(ts, 1024, P_GB // 1024),
                  _rows(ts, 1024), _rows(ts, 1024)],
        out_specs=[_rows(ts, 1024), _rows(ts, 1024), _rows(ts, 2048)],
        out_shape=[jax.ShapeDtypeStruct((s_, 1024), MXU_DTYPE)] * 2 + [jax.ShapeDtypeStruct((s_, 2048), MXU_DTYPE)],
        compiler_params=_cparams(("parallel",)),
    )(dx1b, wout, p, p, ta, tb)


FF_TILE = D_FF // 2


def _ffn_in_act(x1, g, wgu_t, *, tm=512):
    s_ = x1.shape[0]

    def body(x_ref, g_ref, w_ref, h_ref, gu_ref, a_ref):
        h = _rms(x_ref[...], g_ref[...]).astype(h_ref.dtype)
        h_ref[...] = h
        p = lax.dot_general(h, w_ref[...], NT, preferred_element_type=F32)
        gu_ref[...] = p
        gate = p[:, :FF_TILE]
        a_ref[...] = (gate * _sigmoid(gate) * p[:, FF_TILE:]).astype(a_ref.dtype)

    return pl.pallas_call(
        body, name="ffn_in", grid=(s_ // tm, 2),
        in_specs=[pl.BlockSpec((tm, D_MODEL), lambda i, j: (i, 0)), pl.BlockSpec((1, D_MODEL), lambda i, j: (0, 0)),
                  pl.BlockSpec((2 * FF_TILE, D_MODEL), lambda i, j: (j, 0))],
        out_specs=[pl.BlockSpec((tm, D_MODEL), lambda i, j: (i, 0)), pl.BlockSpec((tm, 2 * FF_TILE), lambda i, j: (i, j)),
                   pl.BlockSpec((tm, FF_TILE), lambda i, j: (i, j))],
        out_shape=[jax.ShapeDtypeStruct((s_, D_MODEL), MXU_DTYPE), jax.ShapeDtypeStruct((s_, 2 * D_FF), F32),
                   jax.ShapeDtypeStruct((s_, D_FF), MXU_DTYPE)],
        compiler_params=_cparams(("parallel", "arbitrary")),
    )(x1, g, wgu_t)


def _d_act_swiglu(dx2b, wd, gu, *, tm=512):
    s_ = dx2b.shape[0]

    def body(d_ref, w_ref, gu_ref, o_ref):
        da = lax.dot_general(d_ref[...], w_ref[...], NT, preferred_element_type=F32)
        g, u = gu_ref[:, :FF_TILE], gu_ref[:, FF_TILE:]
        sg = _sigmoid(g)
        o_ref[:, :FF_TILE] = (da * u * (sg * (1.0 + g * (1.0 - sg)))).astype(o_ref.dtype)
        o_ref[:, FF_TILE:] = (da * (g * sg)).astype(o_ref.dtype)

    gu_spec = pl.BlockSpec((tm, 2 * FF_TILE), lambda i, j: (i, j))
    return pl.pallas_call(
        body, name="d_act", grid=(s_ // tm, 2),
        in_specs=[pl.BlockSpec((tm, D_MODEL), lambda i, j: (i, 0)), pl.BlockSpec((FF_TILE, D_MODEL), lambda i, j: (j, 0)), gu_spec],
        out_specs=gu_spec, out_shape=jax.ShapeDtypeStruct((s_, 2 * D_FF), MXU_DTYPE),
        compiler_params=_cparams(("parallel", "parallel")),
    )(dx2b, wd, gu)


def _ffn_out_loss(act, wd, x1, g, tgt, *, ts=512):
    s_, c = x1.shape
    kk = act.shape[1]

    def body(a_ref, w_ref, x_ref, g_ref, t_ref, dx_ref, dxb_ref, dg_ref, lp_ref, tot_ref):
        v = x_ref[...] + jnp.dot(a_ref[...], w_ref[...], preferred_element_type=F32)
        r = lax.rsqrt(jnp.mean(v * v, axis=-1, keepdims=True) + EPS)
        xh = v * r
        gg = g_ref[...]
        e = xh * gg - t_ref[...]
        do = e * (1.0 / c)
        dxh = do * gg
        dx = r * (dxh - xh * jnp.mean(dxh * xh, axis=-1, keepdims=True))
        dx_ref[...] = dx
        dxb_ref[...] = dx.astype(dxb_ref.dtype)
        i = pl.program_id(0)

        @pl.when(i == 0)
        def _():
            dg_ref[...] = jnp.zeros(dg_ref.shape, F32)
            lp_ref[...] = jnp.zeros(lp_ref.shape, F32)

        dg_ref[...] += _sublane_sum(do * xh)
        lp_ref[...] += _sublane_sum(e * e)
        tot_ref[...] = jnp.full(tot_ref.shape, (0.5 / c) * jnp.sum(lp_ref[...]), F32)

    return pl.pallas_call(
        body, name="ffn_out_loss", grid=(s_ // ts,),
        in_specs=[_rows(ts, kk), _const(kk, c), _rows(ts, c), _const(1, c), _rows(ts, c)],
        out_specs=[_rows(ts, c), _rows(ts, c), _const(SUBLANES, c), _const(SUBLANES, c), _const(SUBLANES, LANES)],
        out_shape=[jax.ShapeDtypeStruct((s_, c), F32), jax.ShapeDtypeStruct((s_, c), MXU_DTYPE),
                   jax.ShapeDtypeStruct((SUBLANES, c), F32), jax.ShapeDtypeStruct((SUBLANES, c), F32),
                   jax.ShapeDtypeStruct((SUBLANES, LANES), F32)],
        compiler_params=_cparams(("arbitrary",)),
    )(act, wd, x1, g, tgt)


def _mla_bwd_prep(dob, o32, *, ts=256):
    s_ = dob.shape[0]

    def body(do_ref, o_ref, dob_ref, dl_ref):
        d = do_ref[...]
        dob_ref[...] = d.astype(dob_ref.dtype)
        prod = d * o_ref[...]
        for h in range(MLA_HEADS):
            dl_ref[h] = jnp.sum(prod[:, h * LANES:(h + 1) * LANES].T, axis=0, keepdims=True)

    return pl.pallas_call(
        body, name="mla_bwd_prep", grid=(s_ // ts,), in_specs=[_rows(ts, 1024), _rows(ts, 1024)],
        out_specs=[_rows(ts, 1024), pl.BlockSpec((MLA_HEADS, 1, ts), lambda i: (0, 0, i))],
        out_shape=[jax.ShapeDtypeStruct((s_, 1024), MXU_DTYPE), jax.ShapeDtypeStruct((MLA_HEADS, 1, s_), F32)],
        compiler_params=_cparams(("parallel",)),
    )(dob, o32)


SWA_T = 4 * BLOCK


SWA_W = SWA_GROUP * BLOCK


def _swa_masks(sb):
    kr = lax.broadcasted_iota(jnp.int32, (2 * BLOCK, SWA_W), 0)
    qc = jnp.bitwise_and(lax.broadcasted_iota(jnp.int32, (2 * BLOCK, SWA_W), 1), BLOCK - 1)
    band = jnp.logical_and(kr > qc, kr <= qc + BLOCK)
    first = jnp.logical_and(band, kr >= BLOCK)
    return band, jnp.logical_or(first, jnp.logical_and(band, sb > 0))


def _heads_to_rows(ref, rs):
    return jnp.concatenate([ref[rs, h * LANES:(h + 1) * LANES] for h in range(SWA_GROUP)], axis=0)


def _sink_row(sk_ref):
    return jnp.concatenate([sk_ref[0, h:h + 1, :] for h in range(SWA_GROUP)], axis=1) * LOG2E


def _swa_in_specs(rev, nsb):
    sbi = (lambda j: nsb - 1 - j) if rev else (lambda j: j)
    cur = pl.BlockSpec((SWA_T, LANES), lambda g, j: (sbi(j), g))
    prev = pl.BlockSpec((BLOCK, LANES), lambda g, j: (jnp.maximum(4 * sbi(j) - 1, 0), g))
    q = pl.BlockSpec((SWA_T, SWA_GROUP * LANES), lambda g, j: (sbi(j), g))
    sink = pl.BlockSpec((1, SUBLANES, LANES), lambda g, j: (g, 0, 0))
    lse = pl.BlockSpec((SWA_GROUP, 1, SWA_T), lambda g, j: (g, 0, sbi(j)))
    return q, cur, prev, sink, lse


def _swa_fwd(qa, ka, va, sink_b):
    s_ = qa.shape[0]
    nsb = s_ // SWA_T
    c2 = HEAD_DIM ** -0.5 * LOG2E

    def body(q_ref, kc_ref, kp_ref, vc_ref, vp_ref, sk_ref, o32_ref, o16_ref, lse_ref, kx, vx):
        kx[0:BLOCK, :] = kp_ref[...]
        kx[BLOCK:5 * BLOCK, :] = kc_ref[...]
        vx[0:BLOCK, :] = vp_ref[...]
        vx[BLOCK:5 * BLOCK, :] = vc_ref[...]
        band, band0 = _swa_masks(pl.program_id(1))
        sink2 = _sink_row(sk_ref)
        for b in range(4):
            rs = slice(b * BLOCK, (b + 1) * BLOCK)
            ks = slice(b * BLOCK, (b + 2) * BLOCK)
            st = lax.dot_general(kx[ks, :], _heads_to_rows(q_ref, rs), NT, preferred_element_type=F32) * c2
            st = jnp.where(band0 if b == 0 else band, st, -jnp.inf)
            m = jnp.maximum(jnp.max(st, axis=0, keepdims=True), sink2)
            pt = jnp.exp2(st - m)
            den = jnp.sum(pt, axis=0, keepdims=True) + jnp.exp2(sink2 - m)
            o = lax.dot_general((pt * (1.0 / den)).astype(MXU_DTYPE), vx[ks, :], TN, preferred_element_type=F32)
            lse = m + jnp.log2(den)
            for hh in range(SWA_GROUP):
                cs = slice(hh * LANES, (hh + 1) * LANES)
                o32_ref[rs, cs] = o[cs, :]
                o16_ref[rs, cs] = o[cs, :].astype(o16_ref.dtype)
                lse_ref[hh, :, rs] = lse[:, cs]

    q, cur, prev, sink, lse_spec = _swa_in_specs(False, nsb)
    return pl.pallas_call(
        body, name="swa_fwd", grid=(SWA_KV_HEADS, nsb), in_specs=[q, cur, prev, cur, prev, sink],
        out_specs=[q, q, lse_spec],
        out_shape=[jax.ShapeDtypeStruct((s_, SWA_HEADS * LANES), F32), jax.ShapeDtypeStruct((s_, SWA_HEADS * LANES), MXU_DTYPE),
                   jax.ShapeDtypeStruct((SWA_HEADS, 1, s_), F32)],
        scratch_shapes=[pltpu.VMEM((5 * BLOCK, LANES), MXU_DTYPE), pltpu.VMEM((5 * BLOCK, LANES), MXU_DTYPE)],
        compiler_params=_cparams(("parallel", "arbitrary")),
    )(qa, ka, ka, va, va, sink_b)


def _swa_bwd(qa, ka, va, sink_b, o32, do, lse):
    s_ = qa.shape[0]
    nsb = s_ // SWA_T
    scale = HEAD_DIM ** -0.5
    c2 = scale * LOG2E

    def body(q_ref, kc_ref, kp_ref, vc_ref, vp_ref, sk_ref, o_ref, do_ref, lse_ref,
             dq_ref, dk_ref, dv_ref, dsk_ref, kx, vx, kacc, vacc, kcar, vcar):
        j = pl.program_id(1)
        kx[0:BLOCK, :] = kp_ref[...]
        kx[BLOCK:5 * BLOCK, :] = kc_ref[...]
        vx[0:BLOCK, :] = vp_ref[...]
        vx[BLOCK:5 * BLOCK, :] = vc_ref[...]
        band, band0 = _swa_masks(nsb - 1 - j)
        kacc[...] = jnp.zeros(kacc.shape, F32)
        vacc[...] = jnp.zeros(vacc.shape, F32)

        @pl.when(j == 0)
        def _():
            kcar[...] = jnp.zeros(kcar.shape, F32)
            vcar[...] = jnp.zeros(vcar.shape, F32)
            dsk_ref[...] = jnp.zeros(dsk_ref.shape, F32)

        sink2 = _sink_row(sk_ref)
        dsink = jnp.zeros((1, SWA_W), F32)
        for b in range(4):
            rs = slice(b * BLOCK, (b + 1) * BLOCK)
            ks = slice(b * BLOCK, (b + 2) * BLOCK)
            q, k2, v2 = _heads_to_rows(q_ref, rs), kx[ks, :], vx[ks, :]
            d = _heads_to_rows(do_ref, rs)
            delta = jnp.sum((d * _heads_to_rows(o_ref, rs)).T, axis=0, keepdims=True)
            l2 = jnp.concatenate([lse_ref[hh, :, rs] for hh in range(SWA_GROUP)], axis=1)
            st = lax.dot_general(k2, q, NT, preferred_element_type=F32) * c2
            pt = jnp.exp2(jnp.where(band0 if b == 0 else band, st, -jnp.inf) - l2)
            db = d.astype(MXU_DTYPE)
            dst = (pt * (lax.dot_general(v2, db, NT, preferred_element_type=F32) - delta) * scale).astype(MXU_DTYPE)
            dq = lax.dot_general(dst, k2, TN, preferred_element_type=F32)
            for hh in range(SWA_GROUP):
                dq_ref[rs, hh * LANES:(hh + 1) * LANES] = dq[hh * LANES:(hh + 1) * LANES, :]
            kacc[ks, :] += jnp.dot(dst, q, preferred_element_type=F32)
            vacc[ks, :] += jnp.dot(pt.astype(MXU_DTYPE), db, preferred_element_type=F32)
            dsink = dsink - jnp.exp2(sink2 - l2) * delta
        for hh in range(SWA_GROUP):
            tot = jnp.sum(dsink[:, hh * LANES:(hh + 1) * LANES], axis=1, keepdims=True)
            dsk_ref[0, hh:hh + 1, :] += jnp.broadcast_to(tot, (1, LANES))

        dk_ref[0:3 * BLOCK, :] = kacc[BLOCK:4 * BLOCK, :]
        dk_ref[3 * BLOCK:4 * BLOCK, :] = kacc[4 * BLOCK:5 * BLOCK, :] + kcar[...]
        dv_ref[0:3 * BLOCK, :] = vacc[BLOCK:4 * BLOCK, :].astype(dv_ref.dtype)
        dv_ref[3 * BLOCK:4 * BLOCK, :] = (vacc[4 * BLOCK:5 * BLOCK, :] + vcar[...]).astype(dv_ref.dtype)
        kcar[...] = kacc[0:BLOCK, :]
        vcar[...] = vacc[0:BLOCK, :]

    q, cur, prev, sink, lse_spec = _swa_in_specs(True, nsb)
    return pl.pallas_call(
        body, name="swa_bwd", grid=(SWA_KV_HEADS, nsb),
        in_specs=[q, cur, prev, cur, prev, sink, q, q, lse_spec],
        out_specs=[q, cur, cur, sink],
        out_shape=[jax.ShapeDtypeStruct((s_, SWA_HEADS * LANES), F32), jax.ShapeDtypeStruct((s_, SWA_KV_HEADS * LANES), F32),
                   jax.ShapeDtypeStruct((s_, SWA_KV_HEADS * LANES), MXU_DTYPE),
                   jax.ShapeDtypeStruct((SWA_KV_HEADS, SUBLANES, LANES), F32)],
        scratch_shapes=[pltpu.VMEM((5 * BLOCK, LANES), MXU_DTYPE), pltpu.VMEM((5 * BLOCK, LANES), MXU_DTYPE),
                        pltpu.VMEM((5 * BLOCK, LANES), F32), pltpu.VMEM((5 * BLOCK, LANES), F32),
                        pltpu.VMEM((BLOCK, LANES), F32), pltpu.VMEM((BLOCK, LANES), F32)],
        compiler_params=_cparams(("arbitrary", "arbitrary")),
    )(qa, ka, ka, va, va, sink_b, o32, do, lse)


MLA_T = 512
MLA_FWD_GROUP = 4
MLA_BWD_GROUP = 2


def _mla_specs(s_, t, group):
    w = group * LANES
    qs = pl.BlockSpec((t, w), lambda g, i: (i, g))
    kv = pl.BlockSpec((s_, w), lambda g, i: (0, g))
    row = pl.BlockSpec((group, 1, t), lambda g, i: (g, 0, i))
    return qs, kv, row


def _causal_scores_t(k, q, t, c2, masked):
    st = lax.dot_general(k, q, NT, preferred_element_type=F32) * c2
    if masked:
        kr = lax.broadcasted_iota(jnp.int32, (t, t), 0)
        qc = lax.broadcasted_iota(jnp.int32, (t, t), 1)
        st = jnp.where(kr <= qc, st, -jnp.inf)
    return st


def _mla_fwd(qc, kc, vp):
    s_ = qc.shape[0]
    t = min(MLA_T, s_)
    c2 = MLA_QK ** -0.5 * LOG2E
    grp = MLA_FWD_GROUP

    def body(q_ref, k_ref, v_ref, o32_ref, o16_ref, lse_ref, m_s, acc_s):
        qi = pl.program_id(1)
        m_s[...] = jnp.full(m_s.shape, -jnp.inf, F32)
        acc_s[...] = jnp.zeros(acc_s.shape, F32)
        ones_lane = lax.broadcasted_iota(jnp.int32, (t, LANES), 1) == MLA_V

        def step(ki, masked):
            off = pl.multiple_of(ki * t, t)
            for g in range(grp):
                cs = slice(g * LANES, (g + 1) * LANES)
                st = _causal_scores_t(k_ref[pl.ds(off, t), cs], q_ref[:, cs], t, c2, masked)
                m_old = m_s[g]
                m_new = jnp.maximum(m_old, jnp.max(st, axis=0, keepdims=True))
                alpha = jnp.exp2(m_old - m_new)
                pt = jnp.exp2(st - m_new).astype(MXU_DTYPE)
                v = v_ref[pl.ds(off, t), cs]
                v = jnp.where(ones_lane, jnp.ones((), v.dtype), v)
                acc_s[g] = alpha * acc_s[g] + lax.dot_general(v, pt, TN, preferred_element_type=F32)
                m_s[g] = m_new

        def full_block(ki, carry):
            step(ki, False)
            return carry

        lax.fori_loop(0, qi, full_block, 0)
        step(qi, True)
        for g in range(grp):
            cs = slice(g * LANES, (g + 1) * LANES)
            acc = acc_s[g]
            l = acc[MLA_V:MLA_V + 1, :]
            o = (acc * (1.0 / l)).T
            o32_ref[:, cs] = o
            o16_ref[:, cs] = o.astype(o16_ref.dtype)
            lse_ref[g] = m_s[g] + jnp.log2(l)

    qs, kv, row = _mla_specs(s_, t, grp)
    return pl.pallas_call(
        body, name="mla_fwd", grid=(MLA_HEADS // grp, s_ // t), in_specs=[qs, kv, kv], out_specs=[qs, qs, row],
        out_shape=[jax.ShapeDtypeStruct((s_, MLA_HEADS * LANES), F32), jax.ShapeDtypeStruct((s_, MLA_HEADS * LANES), MXU_DTYPE),
                   jax.ShapeDtypeStruct((MLA_HEADS, 1, s_), F32)],
        scratch_shapes=[pltpu.VMEM((grp, 1, t), F32), pltpu.VMEM((grp, LANES, t), F32)],
        compiler_params=_cparams(("parallel", "arbitrary")),
    )(qc, kc, vp)


def _mla_bwd(qc, kc, vp, dob, lse, delta):
    s_ = qc.shape[0]
    t = min(MLA_T, s_)
    scale = MLA_QK ** -0.5
    c2 = scale * LOG2E
    grp = MLA_BWD_GROUP

    def body(q_ref, do_ref, lse_ref, dl_ref, k_ref, v_ref, dq_ref, dk_ref, dv_ref, dqt_s):
        qi = pl.program_id(1)

        @pl.when(qi == 0)
        def _():
            dk_ref[...] = jnp.zeros(dk_ref.shape, F32)
            dv_ref[...] = jnp.zeros(dv_ref.shape, F32)

        dqt_s[...] = jnp.zeros(dqt_s.shape, F32)

        def step(ki, masked):
            off = pl.multiple_of(ki * t, t)
            for g in range(grp):
                cs = slice(g * LANES, (g + 1) * LANES)
                q, d, k = q_ref[:, cs], do_ref[:, cs], k_ref[pl.ds(off, t), cs]
                pt = jnp.exp2(_causal_scores_t(k, q, t, c2, masked) - lse_ref[g])
                dpt = lax.dot_general(v_ref[pl.ds(off, t), cs], d, NT, preferred_element_type=F32)
                dst = (pt * (dpt - dl_ref[g]) * scale).astype(MXU_DTYPE)
                dv_ref[pl.ds(off, t), cs] += jnp.dot(pt.astype(MXU_DTYPE), d, preferred_element_type=F32)
                dk_ref[pl.ds(off, t), cs] += jnp.dot(dst, q, preferred_element_type=F32)
                dqt_s[g] += lax.dot_general(k, dst, TN, preferred_element_type=F32)

        def full_block(ki, carry):
            step(ki, False)
            return carry

        lax.fori_loop(0, qi, full_block, 0)
        step(qi, True)
        for g in range(grp):
            dq_ref[:, g * LANES:(g + 1) * LANES] = dqt_s[g].T

    qs, kv, row = _mla_specs(s_, t, grp)
    shp = jax.ShapeDtypeStruct((s_, MLA_HEADS * LANES), F32)
    return pl.pallas_call(
        body, name="mla_bwd", grid=(MLA_HEADS // grp, s_ // t), in_specs=[qs, qs, row, row, kv, kv],
        out_specs=[qs, kv, kv], out_shape=[shp, shp, shp], scratch_shapes=[pltpu.VMEM((grp, LANES, t), F32)],
        compiler_params=_cparams(("parallel", "arbitrary")),
    )(qc, dob, lse, delta, kc, vp)


def _pad_heads(w, nh, hd, axis):
    shp = w.shape
    w = w.reshape(shp[:axis] + (nh, hd) + shp[axis + 1:])
    pad = [(0, 0)] * w.ndim
    pad[axis + 1] = (0, LANES - hd)
    w = jnp.pad(w, pad)
    return w.reshape(shp[:axis] + (nh * LANES,) + shp[axis + 1:])


def _unpad_heads(w, nh, hd, axis):
    shp = w.shape
    w = w.reshape(shp[:axis] + (nh, LANES) + shp[axis + 1:])
    w = lax.slice_in_dim(w, 0, hd, axis=axis + 1)
    return w.reshape(shp[:axis] + (nh * hd,) + shp[axis + 1:])


PACK_W = 1024
ROW_TILE = 16
FULL_SHAPE = dict(w_in=(1024, 3488), w_uq=(384, 768), w_ukv=(256, 1024), w_o_swa=(512, 1024), w_o_mla=(512, 1024),
                  w_out=(1024, 1024), w_gate=(1024, 2816), w_up=(1024, 2816), w_down=(2816, 1024))
BIG = tuple(FULL_SHAPE)
ROW_SHARDED = ("w_out", "w_down")
W_IN_COLS = FULL_SHAPE["w_in"][1] // N_DEV
W_IN_ROWS = -(-W_IN_COLS // ROW_TILE) * ROW_TILE
FF_COLS = D_FF // N_DEV
OUT_ROWS = D_MODEL // N_DEV
SMALL_ROW0 = W_IN_ROWS + OUT_ROWS
SMALL_FLAT = (("w_uq", 0, 36), ("w_ukv", 48, 32), ("w_o_swa", 80, 64), ("w_o_mla", 144, 64))
SMALL_ROWS = 208
EARLY_ROWS = SMALL_ROW0 + SMALL_ROWS
LATE_ROWS = 3 * FF_COLS
PACK_ROWS = EARLY_ROWS + LATE_ROWS


def _shard_shape(n):
    r, c = FULL_SHAPE[n]
    return (r // N_DEV, c) if n in ROW_SHARDED else (r, c // N_DEV)


def _wire_pack(sh, dtype):
    c = lambda n: sh[n].astype(dtype)
    rows = [jnp.pad(c("w_in").T, ((0, W_IN_ROWS - W_IN_COLS), (0, 0))), c("w_out")]
    for n, _, r in SMALL_FLAT:
        rows.append(jnp.pad(c(n).reshape(r, PACK_W), ((0, -r % ROW_TILE), (0, 0))))
    return jnp.concatenate(rows + [c("w_gate").T, c("w_up").T, c("w_down")], 0)


MID_ROWS = OUT_ROWS + SMALL_ROWS


def _mid_unpack(p):
    out = dict(w_out=p[0:OUT_ROWS])
    for n, off, r in SMALL_FLAT:
        out[n] = p[OUT_ROWS + off:OUT_ROWS + off + r].reshape(_shard_shape(n))
    return out


def _w_in_row_maps():
    sp = lambda col: (col // W_IN_COLS) * W_IN_ROWS + col % W_IN_COLS
    fwd = np.full((P_W,), -1, np.int64)

    def put(t0, c0, n):
        fwd[t0:t0 + n] = [sp(c) for c in range(c0, c0 + n)]

    put(P_GA, IN_OFF[6], D_MODEL)
    put(P_GB, IN_OFF[7], D_MODEL)
    for h in range(SWA_HEADS):
        put(P_Q + LANES * h, IN_OFF[0] + HEAD_DIM * h, HEAD_DIM)
    put(P_QLAT, IN_OFF[3], Q_LORA)
    put(P_KR + KR_LANE, IN_OFF[5], MLA_ROPE)
    for h in range(SWA_KV_HEADS):
        put(P_K + LANES * h, IN_OFF[1] + HEAD_DIM * h, HEAD_DIM)
        put(P_V + LANES * h, IN_OFF[2] + HEAD_DIM * h, HEAD_DIM)
    put(P_KVLAT, IN_OFF[4], KV_LORA)
    inv = np.full((N_DEV * W_IN_ROWS,), -1, np.int64)
    inv[fwd[fwd >= 0]] = np.nonzero(fwd >= 0)[0]
    return fwd, inv


def _take_rows(src, idx, *, name):
    n_out, n_src, width = len(idx), src.shape[0], src.shape[1]
    assert n_out % BLOCK == 0 and n_src % BLOCK == 0
    n_tiles = n_out // BLOCK
    blocks = [sorted({int(v) // BLOCK for v in idx[i * BLOCK:(i + 1) * BLOCK] if v >= 0}) for i in range(n_tiles)]
    k_max = max(1, max(len(b) for b in blocks))
    tab = np.zeros((n_tiles, k_max), np.int32)
    sel = np.zeros((n_tiles, k_max, BLOCK, BLOCK), np.float32)
    for i, blks in enumerate(blocks):
        for m, b in enumerate(blks):
            tab[i, m] = b
            for r in range(BLOCK):
                v = int(idx[i * BLOCK + r])
                if v >= 0 and v // BLOCK == b:
                    sel[i, m, r, v % BLOCK] = 1.0

    def body(tab_ref, sel_ref, *refs):
        o_ref = refs[k_max]
        acc = jnp.dot(sel_ref[0, 0], refs[0][...], preferred_element_type=F32)
        for m in range(1, k_max):
            acc = acc + jnp.dot(sel_ref[0, m], refs[m][...], preferred_element_type=F32)
        o_ref[...] = acc.astype(o_ref.dtype)

    def src_spec(m):
        return pl.BlockSpec((BLOCK, width), lambda i, t: (t[i * k_max + m], 0))

    return pl.pallas_call(
        body, name=name,
        grid_spec=pltpu.PrefetchScalarGridSpec(
            num_scalar_prefetch=1, grid=(n_tiles,),
            in_specs=[pl.BlockSpec((1, k_max, BLOCK, BLOCK), lambda i, t: (i, 0, 0, 0))] + [src_spec(m) for m in range(k_max)],
            out_specs=pl.BlockSpec((BLOCK, width), lambda i, t: (i, 0))),
        out_shape=jax.ShapeDtypeStruct((n_out, width), src.dtype),
        compiler_params=_cparams(("parallel",)),
    )(jnp.asarray(tab.reshape(-1)), jnp.asarray(sel, src.dtype), *([src] * k_max))


def _w_in_operand(win_g):
    return _take_rows(win_g.reshape(N_DEV * W_IN_ROWS, PACK_W), _w_in_row_maps()[0], name="w_in_rows")


def _mid_operands(wout_g, small_g):
    def full(n, off, r):
        a = small_g[:, off:off + r].reshape((N_DEV,) + _shard_shape(n))
        return jnp.moveaxis(a, 0, 1).reshape(FULL_SHAPE[n])

    w = {n: full(n, off, r) for n, off, r in SMALL_FLAT}
    ukv = w["w_ukv"].reshape(KV_LORA, MLA_HEADS, MLA_NOPE + MLA_V)
    return dict(
        wout=wout_g.reshape(D_MODEL, D_MODEL),
        wuq=_pad_heads(w["w_uq"], MLA_HEADS, MLA_QK, 1),
        wuk=_pad_heads(ukv[:, :, :MLA_NOPE].reshape(KV_LORA, -1), MLA_HEADS, MLA_NOPE, 1),
        wuv=_pad_heads(ukv[:, :, MLA_NOPE:].reshape(KV_LORA, -1), MLA_HEADS, MLA_V, 1),
        woa=_pad_heads(w["w_o_swa"], SWA_HEADS, HEAD_DIM, 0),
        wob=_pad_heads(w["w_o_mla"], MLA_HEADS, MLA_V, 0),
    )


def _mid_grad_pack(g):
    uk = _unpad_heads(g["wukv"][:, :1024], MLA_HEADS, MLA_NOPE, 1).reshape(KV_LORA, MLA_HEADS, MLA_NOPE)
    uv = _unpad_heads(g["wukv"][:, 1024:], MLA_HEADS, MLA_V, 1).reshape(KV_LORA, MLA_HEADS, MLA_V)
    w = dict(w_uq=_unpad_heads(g["wuq"], MLA_HEADS, MLA_QK, 1), w_ukv=jnp.concatenate([uk, uv], 2).reshape(KV_LORA, -1),
             w_o_swa=_unpad_heads(g["woa"], SWA_HEADS, HEAD_DIM, 0), w_o_mla=_unpad_heads(g["wob"], MLA_HEADS, MLA_V, 0))

    def flat(n, r):
        rr, cc = FULL_SHAPE[n]
        a = jnp.moveaxis(w[n].reshape(rr, N_DEV, cc // N_DEV), 1, 0).reshape(N_DEV, r, PACK_W)
        return jnp.pad(a, ((0, 0), (0, -r % ROW_TILE), (0, 0))).astype(WIRE_DTYPE)

    return jnp.concatenate([g["wout"].reshape(N_DEV, OUT_ROWS, PACK_W)] + [flat(n, r) for n, _, r in SMALL_FLAT], 1)


def _w_in_grad_chunks(g_win_t):
    return _take_rows(g_win_t, _w_in_row_maps()[1], name="dw_in_rows").reshape(N_DEV, W_IN_ROWS, PACK_W)


def _local_step(x, tgt, win_t, small, weights, grads):
    s_ = x.shape[0]
    tabs = _rope_tables(s_)
    sink_b = jnp.broadcast_to(small["swa_sinks"].reshape(SWA_KV_HEADS, SWA_GROUP, 1), (SWA_KV_HEADS, SWA_GROUP, LANES))
    sink_b = jnp.pad(sink_b, ((0, 0), (0, SUBLANES - SWA_GROUP), (0, 0)))

    h, p = _norm_mm(x, small["mix_norm_g"], win_t, name="proj_in", tn=2176, tm=1024)
    qa, ka, va, cq, ckv, kro = _attn_prep(p, small["q_norm_g"], small["kv_norm_g"], tabs)
    ops = weights.mid(cq)
    oa32, oa16, lse_a = _swa_fwd(qa, ka, va, sink_b)
    qp = _mm(cq, ops["wuq"], "nn", name="mla_q_up", tm=1024, tn=1024)
    kp = _mm(ckv, ops["wuk"], "nn", name="mla_k_up", tm=1024, tn=1024)
    vp = _mm(ckv, ops["wuv"], "nn", name="mla_v_up", tm=1024, tn=1024, out_dtype=MXU_DTYPE)
    qc, kc = _mla_prep(qp, kp, kro, tabs)
    ob32, ob16, lse_b = _mla_fwd(qc, kc, vp)
    ta, tb, y = _attn_out_gate(oa16, ob16, ops["woa"], ops["wob"], p)
    x1 = _mm(y, ops["wout"], "nn", name="out_proj", add=x, tm=1024, tn=1024)
    wgu_t, wd = weights.late(x1)
    h2, gu, act = _ffn_in_act(x1, small["ffn_norm_g"], wgu_t)

    dx2, dx2b, dg3, _, tot = _ffn_out_loss(act, wd, x1, small["final_norm_g"].reshape(1, D_MODEL), tgt)
    g = {}
    g_wd = _mm(act, dx2b, "tn", name="dw_down", tm=1408, tn=1024, tk=1024, out_dtype=WIRE_DTYPE)
    dgu = _d_act_swiglu(dx2b, wd, gu)
    g_wgu = _mm(dgu, h2, "tn", name="dw_ffn_in", tm=1408, tn=1024, tk=1024, out_dtype=WIRE_DTYPE)
    token = grads.late(g_wgu, g_wd)
    dx1, dx1b, dg2 = _mm_norm_bwd(dgu, wgu_t, x1, small["ffn_norm_g"] + token[0:1, 0:1], dx2, name="d_h2", tk=2816)
    g["wout"] = _mm(y, dx1b, "tn", name="dw_out", tm=1024, tn=1024, tk=1024, out_dtype=WIRE_DTYPE)
    dta, dtb, dgab = _d_y_gate(dx1b, ops["wout"], p, ta, tb)
    doa = _mm(dta, ops["woa"], "nt", name="d_oa", tm=1024, tn=1024)
    g["woa"] = _mm(oa16, dta, "tn", name="dw_o_swa", tm=1024, tn=1024, tk=1024)
    dob = _mm(dtb, ops["wob"], "nt", name="d_ob", tm=1024, tn=1024)
    g["wob"] = _mm(ob16, dtb, "tn", name="dw_o_mla", tm=1024, tn=1024, tk=1024)
    dob16, delta_b = _mla_bwd_prep(dob, ob32)
    dqc, dkc, dvp = _mla_bwd(qc, kc, vp, dob16, lse_b, delta_b)
    dqp, dkv, dkr = _mla_unprep(dqc, dkc, dvp, tabs)
    dcq = _mm(dqp, ops["wuq"], "nt", name="d_cq", tn=Q_LORA)
    g["wuq"] = _mm(cq, dqp, "tn", name="dw_uq", tm=Q_LORA, tn=1024, tk=512)
    dckv = _mm(dkv, jnp.concatenate([ops["wuk"], ops["wuv"]], 1), "nt", name="d_ckv", tn=KV_LORA)
    g["wukv"] = _mm(ckv, dkv, "tn", name="dw_ukv", tm=KV_LORA, tn=1024, tk=512)
    token = grads.mid(g)
    _, dqlat, dgq = _norm_bwd(p, small["q_norm_g"] + token[0:1, 0:1], dcq, None, name="qnorm_bwd", x_cb=P_QLAT // Q_LORA)
    _, dkvlat, dgkv = _norm_bwd(p, small["kv_norm_g"], dckv, None, name="kvnorm_bwd", x_cb=P_KVLAT // KV_LORA)
    dqa, dka, dva, dsk = _swa_bwd(qa, ka, va, sink_b, oa32, doa, lse_a)
    dp = _assemble_dp(dgab, dqa, dqlat, dkr, dka, dva, dkvlat, tabs)
    token = grads.last(_mm(dp, h, "tn", name="dw_in", tm=2176, tn=1024, tk=1024, out_dtype=WIRE_DTYPE))
    gx, _, dg1 = _mm_norm_bwd(dp, win_t, x, small["mix_norm_g"], dx1, name="d_h", tk=2176, after=token)

    sm = dict(mix_norm_g=dg1, ffn_norm_g=dg2, final_norm_g=dg3, q_norm_g=dgq, kv_norm_g=dgkv,
              swa_sinks=dsk[:, :SWA_GROUP, 0].reshape(1, SWA_HEADS))
    return tot, gx, sm


MESH = pl.DeviceIdType.MESH
ANY = pl.BlockSpec(memory_space=pl.ANY)


def _position():
    return lax.axis_index("x"), lax.axis_index("y"), lax.axis_index("c")


def _all_gather(block, pieces, shapes, *, name):
    n_out = len(shapes)
    n_rows = sum(p[3] for p in pieces)

    def body(x_ref, *refs):
        outs, (send_sems, recv_sems, local_sem) = refs[:n_out], refs[n_out:]
        x, y, c = _position()
        me, sibling = (x, y, c), (x, y, 1 - c)
        chips = [(1 - x, y), (x, 1 - y), (1 - x, 1 - y)]

        def dst(piece, blk):
            arr, lead, _, _ = piece
            return outs[arr].at[lead(4 * blk[0] + 2 * blk[1] + blk[2])]

        def own(piece):
            return x_ref.at[pl.ds(piece[2], piece[3])]

        def copies(k, blk, to, from_input):
            return [pltpu.make_async_remote_copy(
                src_ref=own(p) if from_input else dst(p, blk), dst_ref=dst(p, blk), send_sem=send_sems.at[k],
                recv_sem=recv_sems.at[k], device_id=to, device_id_type=MESH) for p in pieces]

        gathered_rows = x_ref.at[pl.ds(0, n_rows)]

        def whole_block(k):
            return pltpu.make_async_remote_copy(src_ref=gathered_rows, dst_ref=gathered_rows, send_sem=send_sems.at[k],
                                                recv_sem=recv_sems.at[k], device_id=me, device_id_type=MESH)

        for p in pieces:
            pltpu.make_async_copy(own(p), dst(p, me), local_sem).start()
        for cp in copies(0, me, sibling, True):
            cp.start()
        for j, chip in enumerate(chips):
            for cp in copies(1 + j, me, (*chip, c), True):
                cp.start()
        for j, chip in enumerate(chips):
            whole_block(1 + j).wait_recv()
            for cp in copies(4 + j, (*chip, c), sibling, False):
                cp.start()
        whole_block(0).wait_recv()
        for j in range(3):
            whole_block(4 + j).wait_recv()
        for k in range(7):
            whole_block(k).wait_send()
        pltpu.make_async_copy(gathered_rows, gathered_rows, local_sem).wait()

    return pl.pallas_call(
        body, name=name, out_shape=[jax.ShapeDtypeStruct(s, block.dtype) for s in shapes], in_specs=[ANY],
        out_specs=[ANY] * n_out,
        scratch_shapes=[pltpu.SemaphoreType.DMA((7,)), pltpu.SemaphoreType.DMA((7,)), pltpu.SemaphoreType.DMA],
    )(block)


HBM = pl.BlockSpec(memory_space=pltpu.HBM)
SEM = pl.BlockSpec(memory_space=pltpu.SEMAPHORE)
TILE_DEVS = FF_TILE // FF_COLS
GU_SHAPE = (2, 2, TILE_DEVS, FF_COLS, PACK_W)


def _gate_slab(d):
    return (d // TILE_DEVS, 0, d % TILE_DEVS)


def _up_slab(d):
    return (d // TILE_DEVS, 1, d % TILE_DEVS)
D_SHAPE = (N_DEV, FF_COLS, PACK_W)
LAND_SHAPE = (N_DEV, LATE_ROWS, PACK_W)


def _split_params():
    return pltpu.CompilerParams(has_side_effects=pltpu.SideEffectType.DATAFLOW_SIDE_EFFECTING)


def _peer(x, y, c, k):
    return ((1 - x) if k & 4 else x, (1 - y) if k & 2 else y, (1 - c) if k & 1 else c)


def _empty_hbm(shape, dtype):
    return pltpu.with_memory_space_constraint(lax.empty(shape, dtype), pltpu.HBM)


def _wait_all(rows, send_sems, recv_sems, me):
    for k in range(N_DEV - 1):
        cp = pltpu.make_async_remote_copy(src_ref=rows, dst_ref=rows, send_sem=send_sems.at[k], recv_sem=recv_sems.at[k],
                                          device_id=me, device_id_type=MESH)
        cp.wait_send()
        cp.wait_recv()


def _token_shape():
    return jax.ShapeDtypeStruct((SUBLANES, LANES), F32)


def _gather_start(pack, row0, pieces, shapes, *, name):
    n = len(shapes)

    def body(*refs):
        p_ref, bufs, send_sems, recv_sems, token = refs[0], refs[1:1 + n], refs[1 + n], refs[2 + n], refs[-1]
        x, y, c = _position()
        me = 4 * x + 2 * y + c
        for k in range(1, N_DEV):
            off = row0
            for buf, lead, rows in pieces:
                pltpu.make_async_remote_copy(
                    src_ref=p_ref.at[pl.ds(off, rows)], dst_ref=bufs[buf].at[lead(me)], send_sem=send_sems.at[k - 1],
                    recv_sem=recv_sems.at[k - 1], device_id=_peer(x, y, c, k), device_id_type=MESH).start()
                off += rows
        token[...] = jnp.zeros_like(token)

    sems, dt = pltpu.SemaphoreType.DMA((N_DEV - 1,)), pack.dtype
    return pl.pallas_call(
        body, name=name,
        out_shape=(sems, sems, pltpu.HBM(pack.shape, dt)) + tuple(pltpu.HBM(s, dt) for s in shapes) + (_token_shape(),),
        in_specs=(HBM,) * (1 + n), out_specs=(SEM, SEM) + (HBM,) * (1 + n) + (pl.BlockSpec(memory_space=pltpu.VMEM),),
        input_output_aliases={i: 2 + i for i in range(1 + n)}, compiler_params=_split_params(),
    )(pltpu.with_memory_space_constraint(pack, pltpu.HBM), *[_empty_hbm(s, dt) for s in shapes])


def _gather_wait(started, row0, n_rows, after, *, name):
    send_sems, recv_sems, pack, *bufs = started[:-1]
    n = len(bufs)

    def body(*refs):
        _wait_all(refs[0].at[pl.ds(row0, n_rows)], refs[1 + n], refs[2 + n], _position())

    outs = pl.pallas_call(
        body, name=name, out_shape=tuple(pltpu.HBM(a.shape, a.dtype) for a in (pack, *bufs)),
        in_specs=(HBM,) * (1 + n) + (SEM, SEM, ANY), out_specs=(HBM,) * (1 + n),
        input_output_aliases={i: i for i in range(1 + n)}, compiler_params=_split_params(),
    )(pack, *bufs, send_sems, recv_sems, after)
    return outs[0], outs[1:]


def _scatter_start(srcs, pieces, *, name):
    n = len(srcs)
    land_shape = (N_DEV, sum(p[2] for p in pieces), PACK_W)

    def body(*refs):
        src_refs, land_ref, send_sems, recv_sems, token = refs[:n], refs[n], refs[n + 1], refs[n + 2], refs[-1]
        x, y, c = _position()
        me = 4 * x + 2 * y + c
        for k in range(1, N_DEV):
            px, py, pc = _peer(x, y, c, k)
            off = 0
            for si, lead, rows in pieces:
                pltpu.make_async_remote_copy(
                    src_ref=src_refs[si].at[lead(4 * px + 2 * py + pc)], dst_ref=land_ref.at[me, pl.ds(off, rows)],
                    send_sem=send_sems.at[k - 1], recv_sem=recv_sems.at[k - 1], device_id=(px, py, pc),
                    device_id_type=MESH).start()
                off += rows
        token[...] = jnp.zeros_like(token)

    sems, dt = pltpu.SemaphoreType.DMA((N_DEV - 1,)), srcs[0].dtype
    return pl.pallas_call(
        body, name=name,
        out_shape=(sems, sems) + tuple(pltpu.HBM(a.shape, dt) for a in srcs) + (pltpu.HBM(land_shape, dt), _token_shape()),
        in_specs=(HBM,) * (n + 1), out_specs=(SEM, SEM) + (HBM,) * (n + 1) + (pl.BlockSpec(memory_space=pltpu.VMEM),),
        input_output_aliases={i: 2 + i for i in range(n + 1)}, compiler_params=_split_params(),
    )(*[pltpu.with_memory_space_constraint(a, pltpu.HBM) for a in srcs], _empty_hbm(land_shape, dt))


def _scatter_wait(started, after, *, name):
    send_sems, recv_sems, *bufs = started[:-1]
    n = len(bufs)

    def body(*refs):
        _wait_all(refs[n - 1].at[0], refs[n], refs[n + 1], _position())

    return pl.pallas_call(
        body, name=name, out_shape=tuple(pltpu.HBM(a.shape, a.dtype) for a in bufs),
        in_specs=(HBM,) * n + (SEM, SEM, ANY), out_specs=(HBM,) * n, input_output_aliases={i: i for i in range(n)},
        compiler_params=_split_params(),
    )(*bufs, send_sems, recv_sems, after)


def _peer_sum(own, own_lead, land, block, rows, idx, *, name):
    lead_rank = own.ndim - 2

    def body(idx_ref, own_ref, *refs):
        o_ref = refs[N_DEV - 1]
        acc = own_ref[(0,) * lead_rank].astype(F32)
        for k in range(N_DEV - 1):
            acc = acc + refs[k][0].astype(F32)
        o_ref[...] = acc

    own_spec = pl.BlockSpec((1,) * lead_rank + (rows, PACK_W), lambda i, t: own_lead(t[0]) + (0, 0))

    def land_spec(k):
        return pl.BlockSpec((1, rows, PACK_W), lambda i, t: (t[k + 1], block, 0))

    return pl.pallas_call(
        body, name=name,
        grid_spec=pltpu.PrefetchScalarGridSpec(
            num_scalar_prefetch=1, grid=(1,), in_specs=[own_spec] + [land_spec(k) for k in range(N_DEV - 1)],
            out_specs=pl.BlockSpec((rows, PACK_W), lambda i, t: (0, 0))),
        out_shape=jax.ShapeDtypeStruct((rows, PACK_W), F32), compiler_params=_cparams(("arbitrary",)),
    )(idx, own, *([land] * (N_DEV - 1)))


def _adamw(w, g, m, v):
    m = ADAM_B1 * m + (1.0 - ADAM_B1) * g
    v = ADAM_B2 * v + (1.0 - ADAM_B2) * (g * g)
    m_hat = m / (1.0 - ADAM_B1 ** ADAM_STEP)
    v_hat = v / (1.0 - ADAM_B2 ** ADAM_STEP)
    delta = -ADAM_LR * (m_hat / (jnp.sqrt(v_hat) + ADAM_EPS) + ADAM_WD * w)
    return delta, m, v


def _adamw_call(w, g, m, v, *, name, max_rows=256):
    _, r, c_ = w.shape
    tr = max_rows if r > max_rows and r % max_rows == 0 else r

    def body(w_ref, g_ref, m_ref, v_ref, d_ref, mo_ref, vo_ref):
        d, mn, vn = _adamw(w_ref[0], g_ref[...], m_ref[0], v_ref[0])
        d_ref[0] = d
        mo_ref[0] = mn
        vo_ref[0] = vn

    row3 = pl.BlockSpec((1, tr, c_), lambda i: (0, i, 0))
    shp = jax.ShapeDtypeStruct((1, r, c_), F32)
    return pl.pallas_call(
        body, name=name, grid=(r // tr,), in_specs=[row3, pl.BlockSpec((tr, c_), lambda i: (i, 0)), row3, row3],
        out_specs=[row3] * 3, out_shape=[shp] * 3, compiler_params=_cparams(("parallel",)),
    )(w, g, m, v)


SMALL = ("mix_norm_g", "ffn_norm_g", "final_norm_g", "q_norm_g", "kv_norm_g", "swa_sinks")
SMALL_W = dict(mix_norm_g=1024, ffn_norm_g=1024, final_norm_g=1024, q_norm_g=Q_LORA, kv_norm_g=KV_LORA, swa_sinks=SWA_HEADS)


def _small_adamw(parts, w, m, v):
    n_par = parts.shape[1] // SUBLANES

    def body(p_ref, w_ref, m_ref, v_ref, g_ref, d_ref, mo_ref, vo_ref):
        tot = p_ref[0]
        for dev in range(1, N_DEV):
            tot = tot + p_ref[dev]
        row_id = lax.broadcasted_iota(jnp.int32, (SUBLANES, PACK_W), 0)
        g = jnp.zeros((SUBLANES, PACK_W), F32)
        for k in range(n_par):
            g = jnp.where(row_id == k, jnp.sum(tot[k * SUBLANES:(k + 1) * SUBLANES, :], axis=0, keepdims=True), g)
        d, mn, vn = _adamw(w_ref[...], g, m_ref[...], v_ref[...])
        g_ref[...] = g
        d_ref[...] = d
        mo_ref[...] = mn
        vo_ref[...] = vn

    shp = jax.ShapeDtypeStruct((SUBLANES, PACK_W), F32)
    vm = pl.BlockSpec(memory_space=pltpu.VMEM)
    return pl.pallas_call(body, name="small_adamw", in_specs=[vm] * 4, out_specs=[vm] * 4, out_shape=[shp] * 4)(parts, w, m, v)


def _small_pack(d, rows_each):
    parts = [jnp.pad(d[n].astype(F32), ((0, 0), (0, PACK_W - SMALL_W[n]))) for n in SMALL]
    out = jnp.concatenate(parts, 0)
    pad = -out.shape[0] % SUBLANES
    return jnp.pad(out, ((0, pad), (0, 0)))


def kernel(x, mix_norm_g, w_in, swa_sinks, q_norm_g, w_uq, kv_norm_g, w_ukv, w_o_swa, w_o_mla, w_out, ffn_norm_g, w_gate, w_up, w_down, final_norm_g, loss_target, m_mix_norm_g, m_w_in, m_swa_sinks, m_q_norm_g, m_w_uq, m_kv_norm_g, m_w_ukv, m_w_o_swa, m_w_o_mla, m_w_out, m_ffn_norm_g, m_w_gate, m_w_up, m_w_down, m_final_norm_g, v_mix_norm_g, v_w_in, v_swa_sinks, v_q_norm_g, v_w_uq, v_kv_norm_g, v_w_ukv, v_w_o_swa, v_w_o_mla, v_w_out, v_ffn_norm_g, v_w_gate, v_w_up, v_w_down, v_final_norm_g):
    big_w = dict(w_in=w_in[0], w_uq=w_uq[0], w_ukv=w_ukv[0], w_o_swa=w_o_swa[0], w_o_mla=w_o_mla[0], w_out=w_out[0],
                 w_gate=w_gate[0], w_up=w_up[0], w_down=w_down[0])
    big_w3 = dict(w_in=w_in, w_uq=w_uq, w_ukv=w_ukv, w_o_swa=w_o_swa, w_o_mla=w_o_mla, w_out=w_out, w_gate=w_gate, w_up=w_up,
                  w_down=w_down)
    big_m = dict(w_in=m_w_in, w_uq=m_w_uq, w_ukv=m_w_ukv, w_o_swa=m_w_o_swa, w_o_mla=m_w_o_mla, w_out=m_w_out,
                 w_gate=m_w_gate, w_up=m_w_up, w_down=m_w_down)
    big_v = dict(w_in=v_w_in, w_uq=v_w_uq, w_ukv=v_w_ukv, w_o_swa=v_w_o_swa, w_o_mla=v_w_o_mla, w_out=v_w_out,
                 w_gate=v_w_gate, w_up=v_w_up, w_down=v_w_down)
    small_w = dict(mix_norm_g=mix_norm_g, ffn_norm_g=ffn_norm_g, final_norm_g=final_norm_g.reshape(1, D_MODEL),
                   q_norm_g=q_norm_g, kv_norm_g=kv_norm_g, swa_sinks=swa_sinks)
    small_m = dict(mix_norm_g=m_mix_norm_g, ffn_norm_g=m_ffn_norm_g, final_norm_g=m_final_norm_g.reshape(1, D_MODEL),
                   q_norm_g=m_q_norm_g, kv_norm_g=m_kv_norm_g, swa_sinks=m_swa_sinks)
    small_v = dict(mix_norm_g=v_mix_norm_g, ffn_norm_g=v_ffn_norm_g, final_norm_g=v_final_norm_g.reshape(1, D_MODEL),
                   q_norm_g=v_q_norm_g, kv_norm_g=v_kv_norm_g, swa_sinks=v_swa_sinks)

    px, py, pc = _position()
    me = 4 * px + 2 * py + pc
    idx = jnp.stack([me] + [4 * qx + 2 * qy + qc for qx, qy, qc in (_peer(px, py, pc, k) for k in range(1, N_DEV))])
    idx = idx.astype(jnp.int32)

    dev = lambda d: (d,)
    pack = _wire_pack(big_w, WIRE_DTYPE)
    win_g, = _all_gather(pack, ((0, dev, 0, W_IN_ROWS),), ((N_DEV, W_IN_ROWS, PACK_W),), name="ag_early")
    ag_mid = _gather_start(pack, W_IN_ROWS, ((0, dev, OUT_ROWS), (1, dev, SMALL_ROWS)),
                           ((N_DEV, OUT_ROWS, PACK_W), (N_DEV, SMALL_ROWS, PACK_W)), name="ag_mid_start")
    ag = {}

    def own_rows(r0, r1, shape):
        return pack[r0:r1].reshape(shape)

    def mid_weights(after):
        pack_mid, (wout_g, small_g) = _gather_wait(ag_mid, W_IN_ROWS, MID_ROWS, after, name="ag_mid_wait")
        ag["late"] = _gather_start(pack_mid, EARLY_ROWS, ((0, _gate_slab, FF_COLS), (0, _up_slab, FF_COLS), (1, dev, FF_COLS)),
                                   (GU_SHAPE, D_SHAPE), name="ag_late_start")
        wout_g = lax.dynamic_update_slice(wout_g, own_rows(W_IN_ROWS, SMALL_ROW0, (1, OUT_ROWS, PACK_W)), (me, 0, 0))
        small_g = lax.dynamic_update_slice(small_g, own_rows(SMALL_ROW0, EARLY_ROWS, (1, SMALL_ROWS, PACK_W)), (me, 0, 0))
        ops = _mid_operands(wout_g, small_g)
        ops["wuq"] = ops["wuq"] + ag["late"][-1][0:1, 0:1].astype(ops["wuq"].dtype)
        return ops

    def late_weights(after):
        _, (gu, d) = _gather_wait(ag["late"], EARLY_ROWS, LATE_ROWS, after, name="ag_late_wait")
        slab = (1, 1, 1, FF_COLS, PACK_W)
        gu = lax.dynamic_update_slice(gu, own_rows(EARLY_ROWS, EARLY_ROWS + FF_COLS, slab), _gate_slab(me) + (0, 0))
        gu = lax.dynamic_update_slice(gu, own_rows(EARLY_ROWS + FF_COLS, EARLY_ROWS + 2 * FF_COLS, slab), _up_slab(me) + (0, 0))
        d = lax.dynamic_update_slice(d, own_rows(EARLY_ROWS + 2 * FF_COLS, PACK_ROWS, (1, FF_COLS, PACK_W)), (me, 0, 0))
        return gu.reshape(2 * D_FF, D_MODEL), d.reshape(D_FF, D_MODEL)

    rs = {}

    def late_grads(g_gu, g_d):
        rs["late"] = _scatter_start([g_gu.reshape(GU_SHAPE), g_d.reshape(D_SHAPE)],
                                    ((0, _gate_slab, FF_COLS), (0, _up_slab, FF_COLS), (1, dev, FF_COLS)),
                                    name="rs_late_start")
        return rs["late"][-1]

    def mid_grads(g):
        rs["mid"] = _scatter_start([_mid_grad_pack(g)], ((0, dev, MID_ROWS),), name="rs_mid_start")
        return rs["mid"][-1]

    def last_grads(g_win_t):
        rs["last"] = _scatter_start([_w_in_grad_chunks(g_win_t)], ((0, dev, W_IN_ROWS),), name="rs_last_start")
        return rs["last"][-1]

    first_w = dict(small_w, mix_norm_g=mix_norm_g + ag_mid[-1][0:1, 0:1])
    loss_tot, gx, g_small = _local_step(
        x[0], loss_target[0], _w_in_operand(win_g), first_w, types.SimpleNamespace(mid=mid_weights, late=late_weights),
        types.SimpleNamespace(late=late_grads, mid=mid_grads, last=last_grads))

    g_gu, g_d, land_late = _scatter_wait(rs["late"], gx, name="rs_late_wait")
    g_mid, land_mid = _scatter_wait(rs["mid"], gx, name="rs_mid_wait")
    g_win, land_last = _scatter_wait(rs["last"], gx, name="rs_last_wait")
    gw = dict(w_gate=_peer_sum(g_gu, _gate_slab, land_late, 0, FF_COLS, idx, name="rs_sum_gate").T,
              w_up=_peer_sum(g_gu, _up_slab, land_late, 1, FF_COLS, idx, name="rs_sum_up").T,
              w_down=_peer_sum(g_d, dev, land_late, 2, FF_COLS, idx, name="rs_sum_down"),
              w_in=_peer_sum(g_win, dev, land_last, 0, W_IN_ROWS, idx, name="rs_sum_in")[0:W_IN_COLS].T)
    gw.update(_mid_unpack(_peer_sum(g_mid, dev, land_mid, 0, MID_ROWS, idx, name="rs_sum_mid")))
    dw, mw, vw = {}, {}, {}
    for n in BIG:
        dw[n], mw[n], vw[n] = _adamw_call(big_w3[n], gw[n], big_m[n], big_v[n], name="adamw_" + n)
    gw = {n: g[None] for n, g in gw.items()}

    loss_rows = jnp.pad(loss_tot[0:1, 0:1], ((0, SUBLANES - 1), (0, PACK_W - 1)))
    small_rows = jnp.concatenate([_small_pack(g_small_rows(g_small), SUBLANES), loss_rows], 0)
    parts, = _all_gather(small_rows, ((0, lambda d: (d,), 0, small_rows.shape[0]),), ((N_DEV,) + small_rows.shape,),
                         name="ag_small")
    gs, ds, ms, vs = _small_adamw(parts, _small_pack(small_w, 1), _small_pack(small_m, 1), _small_pack(small_v, 1))
    loss = gs[len(SMALL), 0]

    def small_out(packed):
        out = {}
        for k, n in enumerate(SMALL):
            out[n] = packed[k:k + 1, :SMALL_W[n]]
        out["final_norm_g"] = out["final_norm_g"].reshape(D_MODEL)
        return out

    gs, ds, ms, vs = small_out(gs), small_out(ds), small_out(ms), small_out(vs)

    order = ("mix_norm_g", "w_in", "swa_sinks", "q_norm_g", "w_uq", "kv_norm_g", "w_ukv", "w_o_swa", "w_o_mla", "w_out",
             "ffn_norm_g", "w_gate", "w_up", "w_down", "final_norm_g")

    def leaves(big, small):
        return [big[n] if n in big else small[n] for n in order]

    return (loss, gx[None], *leaves(gw, gs), *leaves(dw, ds), *leaves(mw, ms), *leaves(vw, vs))


def g_small_rows(g_small):
    out = dict(g_small)
    out["swa_sinks"] = jnp.pad(g_small["swa_sinks"], ((0, SUBLANES - 1), (0, 0)))
    return out
```

```python
import types

import numpy as np
import jax
import jax.numpy as jnp
from jax import lax
from jax.experimental import pallas as pl
from jax.experimental.pallas import tpu as pltpu

F32 = jnp.float32
MXU_DTYPE = jnp.bfloat16
WIRE_DTYPE = jnp.bfloat16

D_MODEL = 1024
EPS = 1e-6
ROPE_THETA = 10000.0
BLOCK = 128
HEAD_DIM = 64
SWA_HEADS = 8
SWA_KV_HEADS = 2
SWA_GROUP = SWA_HEADS // SWA_KV_HEADS
MLA_HEADS = 8
MLA_NOPE = 64
MLA_ROPE = 32
MLA_V = 64
MLA_QK = MLA_NOPE + MLA_ROPE
Q_LORA = 384
KV_LORA = 256
D_FF = 2816
IN_SIZES = (512, 128, 128, Q_LORA, KV_LORA, MLA_ROPE, D_MODEL, D_MODEL)
IN_OFF = tuple(int(v) for v in np.cumsum((0,) + IN_SIZES))
ADAM_LR, ADAM_B1, ADAM_B2, ADAM_EPS, ADAM_WD, ADAM_STEP = 0.001, 0.9, 0.999, 1e-08, 0.01, 10

LANES = 128
SUBLANES = 8
VMEM_LIMIT = 48 * 1024 * 1024
N_DEV = 8
AXES = ("x", "y", "c")

P_GA, P_GB, P_Q, P_QLAT, P_KR, P_K, P_V, P_KVLAT, P_W = 0, 1024, 2048, 3072, 3456, 3584, 3840, 4096, 4352
KR_LANE = 64

LOG2E = 1.4426950408889634

NT = (((1,), (1,)), ((), ()))
NN = (((1,), (0,)), ((), ()))
TN = (((0,), (0,)), ((), ()))


def _cparams(sem):
    return pltpu.CompilerParams(dimension_semantics=sem, vmem_limit_bytes=VMEM_LIMIT)


def _mm(a, b, mode, *, name, out_dtype=F32, add=None, after=None, tm=512, tn=512, tk=None):
    if mode == "nn":
        (M, K), (K2, N) = a.shape, b.shape
    elif mode == "nt":
        (M, K), (N, K2) = a.shape, b.shape
    else:
        (K, M), (K2, N) = a.shape, b.shape
    assert K == K2, (a.shape, b.shape, mode)
    tk = K if tk is None else tk
    tm, tn = min(tm, M), min(tn, N)
    assert M % tm == 0 and N % tn == 0 and K % tk == 0, (M, N, K, tm, tn, tk)
    nk = K // tk
    dn = {"nn": NN, "nt": NT, "tn": TN}[mode]
    if mode == "tn":
        a_spec = pl.BlockSpec((tk, tm), lambda i, j, k: (k, i))
    else:
        a_spec = pl.BlockSpec((tm, tk), lambda i, j, k: (i, k))
    if mode == "nt":
        b_spec = pl.BlockSpec((tn, tk), lambda i, j, k: (j, k))
    else:
        b_spec = pl.BlockSpec((tk, tn), lambda i, j, k: (k, j))
    o_spec = pl.BlockSpec((tm, tn), lambda i, j, k: (i, j))
    has_add, has_after = add is not None, after is not None

    def body(*refs):
        a_ref, b_ref = refs[0], refs[1]
        add_ref = refs[2] if has_add else None
        o_ref = refs[2 + has_add + has_after]
        p = lax.dot_general(a_ref[...], b_ref[...], dn, preferred_element_type=F32)

        def finish(acc):
            if has_add:
                acc = acc + add_ref[...]
            o_ref[...] = acc.astype(o_ref.dtype)

        if nk == 1:
            finish(p)
        else:
            acc_ref = refs[-1]
            k = pl.program_id(2)

            @pl.when(k == 0)
            def _():
                acc_ref[...] = p

            @pl.when(k > 0)
            def _():
                acc_ref[...] += p

            @pl.when(k == nk - 1)
            def _():
                finish(acc_ref[...])

    ins = [a, b] + ([add] if has_add else []) + ([after] if has_after else [])
    in_specs = [a_spec, b_spec] + ([o_spec] if has_add else []) + ([pl.BlockSpec(memory_space=pl.ANY)] if has_after else [])
    return pl.pallas_call(
        body, name=name, grid=(M // tm, N // tn, nk), in_specs=in_specs, out_specs=o_spec,
        out_shape=jax.ShapeDtypeStruct((M, N), out_dtype),
        scratch_shapes=[pltpu.VMEM((tm, tn), F32)] if nk > 1 else [],
        compiler_params=_cparams(("parallel", "parallel", "arbitrary")),
    )(*ins)


def _rows(ts, w, cb=0):
    return pl.BlockSpec((ts, w), lambda i: (i, cb))


def _const(r, w):
    return pl.BlockSpec((r, w), lambda i: (0, 0))


def _sublane_sum(v):
    ts, c = v.shape
    return jnp.sum(v.reshape(ts // SUBLANES, SUBLANES, c), axis=0)


def _sigmoid(v):
    return 1.0 / (1.0 + jnp.exp(-v))


def _rope(v, cos, s_up, s_dn, up, dn):
    return v * cos + pltpu.roll(v, up, 1) * s_up + pltpu.roll(v, dn, 1) * s_dn


def _rope_t(dv, cos, s_up, s_dn, up, dn):
    return dv * cos + pltpu.roll(dv * s_up, dn, 1) + pltpu.roll(dv * s_dn, up, 1)


def _rope_tables(seq):
    pos = np.arange(seq, dtype=np.float32)[:, None]

    def base(dim):
        inv = np.float32(ROPE_THETA) ** (-np.arange(0, dim, 2, dtype=np.float32) / np.float32(dim))
        ang = (pos * inv.astype(np.float32)[None, :]).astype(np.float32)
        return np.cos(ang).astype(np.float32), np.sin(ang).astype(np.float32)

    z = lambda n: np.zeros((seq, n), np.float32)
    ca, sa = base(HEAD_DIM)
    a_cos = np.concatenate([ca, ca, z(64)], 1)
    a_up = np.concatenate([-sa, z(96)], 1)
    a_dn = np.concatenate([z(32), sa, z(64)], 1)
    cb, sb = base(MLA_ROPE)
    one = np.ones((seq, 64), np.float32)
    q_cos = np.concatenate([one, cb, cb, z(32)], 1)
    k_cos = np.concatenate([z(64), cb, cb, z(32)], 1)
    b_up = np.concatenate([z(64), -sb, z(48)], 1)
    b_dn = np.concatenate([z(80), sb, z(32)], 1)
    return tuple(jnp.asarray(t) for t in (a_cos, a_up, a_dn, q_cos, k_cos, b_up, b_dn))


def _rms(v, g):
    return v * lax.rsqrt(jnp.mean(v * v, axis=-1, keepdims=True) + EPS) * g


def _rms_bwd(v, g, d):
    r = lax.rsqrt(jnp.mean(v * v, axis=-1, keepdims=True) + EPS)
    xh = v * r
    dxh = d * g
    return r * (dxh - xh * jnp.mean(dxh * xh, axis=-1, keepdims=True)), d * xh


def _norm_mm(x, g, w_t, *, name, tn, tm=512):
    s_, c = x.shape
    n = w_t.shape[0]

    def body(x_ref, g_ref, w_ref, h_ref, o_ref):
        h = _rms(x_ref[...], g_ref[...]).astype(h_ref.dtype)
        h_ref[...] = h
        o_ref[...] = lax.dot_general(h, w_ref[...], NT, preferred_element_type=F32)

    return pl.pallas_call(
        body, name=name, grid=(s_ // tm, n // tn),
        in_specs=[pl.BlockSpec((tm, c), lambda i, j: (i, 0)), pl.BlockSpec((1, c), lambda i, j: (0, 0)),
                  pl.BlockSpec((tn, c), lambda i, j: (j, 0))],
        out_specs=[pl.BlockSpec((tm, c), lambda i, j: (i, 0)), pl.BlockSpec((tm, tn), lambda i, j: (i, j))],
        out_shape=[jax.ShapeDtypeStruct((s_, c), MXU_DTYPE), jax.ShapeDtypeStruct((s_, n), F32)],
        compiler_params=_cparams(("parallel", "arbitrary")),
    )(x, g, w_t)


def _mm_norm_bwd(a, b, x, g, res, *, name, tk, after=None, tm=512):
    s_, kk = a.shape
    c = b.shape[1]
    nk = kk // tk
    has_after = after is not None

    def body(*refs):
        a_ref, b_ref, x_ref, g_ref, res_ref = refs[:5]
        dx_ref, dxb_ref, dg_ref, acc_ref = refs[5 + has_after:]
        i, k = pl.program_id(0), pl.program_id(1)
        p = jnp.dot(a_ref[...], b_ref[...], preferred_element_type=F32)

        @pl.when(k == 0)
        def _():
            acc_ref[...] = p

        @pl.when(k > 0)
        def _():
            acc_ref[...] += p

        @pl.when(k == nk - 1)
        def _():
            dx, gg = _rms_bwd(x_ref[...], g_ref[...], acc_ref[...])
            dx = dx + res_ref[...]
            dx_ref[...] = dx
            dxb_ref[...] = dx.astype(dxb_ref.dtype)

            @pl.when(i == 0)
            def _():
                dg_ref[...] = jnp.zeros(dg_ref.shape, F32)

            dg_ref[...] += _sublane_sum(gg)

    row = pl.BlockSpec((tm, c), lambda i, k: (i, 0))
    in_specs = [pl.BlockSpec((tm, tk), lambda i, k: (i, k)), pl.BlockSpec((tk, c), lambda i, k: (k, 0)), row,
                pl.BlockSpec((1, c), lambda i, k: (0, 0)), row] + ([pl.BlockSpec(memory_space=pl.ANY)] if has_after else [])
    return pl.pallas_call(
        body, name=name, grid=(s_ // tm, nk), in_specs=in_specs,
        out_specs=[row, row, pl.BlockSpec((SUBLANES, c), lambda i, k: (0, 0))],
        out_shape=[jax.ShapeDtypeStruct((s_, c), F32), jax.ShapeDtypeStruct((s_, c), MXU_DTYPE),
                   jax.ShapeDtypeStruct((SUBLANES, c), F32)],
        scratch_shapes=[pltpu.VMEM((tm, c), F32)], compiler_params=_cparams(("arbitrary", "arbitrary")),
    )(*([a, b, x, g, res] + ([after] if has_after else [])))


def _norm_bwd(x, g, dy, res, *, name, ts=256, x_cb=0, x_src_w=None):
    s_ = x.shape[0]
    c = dy.shape[1]
    has_res = res is not None

    def body(*refs):
        x_ref, g_ref, dy_ref = refs[0], refs[1], refs[2]
        res_ref = refs[3] if has_res else None
        dx_ref, dxb_ref, dg_ref = refs[-3], refs[-2], refs[-1]
        v = x_ref[...]
        r = lax.rsqrt(jnp.mean(v * v, axis=-1, keepdims=True) + EPS)
        xh = v * r
        d = dy_ref[...]
        dxh = d * g_ref[...]
        dx = r * (dxh - xh * jnp.mean(dxh * xh, axis=-1, keepdims=True))
        if has_res:
            dx = dx + res_ref[...]
        dx_ref[...] = dx
        dxb_ref[...] = dx.astype(dxb_ref.dtype)

        @pl.when(pl.program_id(0) == 0)
        def _():
            dg_ref[...] = jnp.zeros(dg_ref.shape, F32)

        dg_ref[...] += _sublane_sum(d * xh)

    ins = [x, g, dy] + ([res] if has_res else [])
    in_specs = [_rows(ts, c, x_cb), _const(1, c), _rows(ts, c)] + ([_rows(ts, c)] if has_res else [])
    return pl.pallas_call(
        body, name=name, grid=(s_ // ts,), in_specs=in_specs,
        out_specs=[_rows(ts, c), _rows(ts, c), _const(SUBLANES, c)],
        out_shape=[jax.ShapeDtypeStruct((s_, c), F32), jax.ShapeDtypeStruct((s_, c), MXU_DTYPE),
                   jax.ShapeDtypeStruct((SUBLANES, c), F32)],
        compiler_params=_cparams(("arbitrary",)),
    )(*ins)


def _attn_prep(p, gq, gkv, tabs, *, ts=256):
    s_ = p.shape[0]
    a_cos, a_up, a_dn, _, k_cos, b_up, b_dn = tabs

    def body(q_ref, k_ref, v_ref, ql_ref, kvl_ref, kr_ref, gq_ref, gkv_ref, ac, au, ad, kc, bu, bd,
             qa_ref, ka_ref, va_ref, cq_ref, ckv_ref, kro_ref):
        c_, u_, d_ = ac[...], au[...], ad[...]
        for h in range(SWA_HEADS):
            sl = slice(h * LANES, (h + 1) * LANES)
            qa_ref[:, sl] = _rope(q_ref[:, sl], c_, u_, d_, 96, 32).astype(qa_ref.dtype)
        for h in range(SWA_KV_HEADS):
            sl = slice(h * LANES, (h + 1) * LANES)
            ka_ref[:, sl] = _rope(k_ref[:, sl], c_, u_, d_, 96, 32).astype(ka_ref.dtype)
        va_ref[...] = v_ref[...].astype(va_ref.dtype)
        for src, gref, dst in ((ql_ref, gq_ref, cq_ref), (kvl_ref, gkv_ref, ckv_ref)):
            v = src[...]
            r = lax.rsqrt(jnp.mean(v * v, axis=-1, keepdims=True) + EPS)
            dst[...] = (v * r * gref[...]).astype(dst.dtype)
        kro_ref[...] = _rope(kr_ref[...], kc[...], bu[...], bd[...], 112, 16)

    tab = _rows(ts, LANES)
    return pl.pallas_call(
        body, name="attn_prep", grid=(s_ // ts,),
        in_specs=[_rows(ts, 1024, P_Q // 1024), _rows(ts, 256, P_K // 256), _rows(ts, 256, P_V // 256),
                  _rows(ts, Q_LORA, P_QLAT // Q_LORA), _rows(ts, KV_LORA, P_KVLAT // KV_LORA),
                  _rows(ts, LANES, P_KR // LANES), _const(1, Q_LORA), _const(1, KV_LORA), tab, tab, tab, tab, tab, tab],
        out_specs=[_rows(ts, 1024), _rows(ts, 256), _rows(ts, 256), _rows(ts, Q_LORA), _rows(ts, KV_LORA),
                   _rows(ts, LANES)],
        out_shape=[jax.ShapeDtypeStruct((s_, 1024), MXU_DTYPE), jax.ShapeDtypeStruct((s_, 256), MXU_DTYPE),
                   jax.ShapeDtypeStruct((s_, 256), MXU_DTYPE), jax.ShapeDtypeStruct((s_, Q_LORA), MXU_DTYPE),
                   jax.ShapeDtypeStruct((s_, KV_LORA), MXU_DTYPE), jax.ShapeDtypeStruct((s_, LANES), F32)],
        compiler_params=_cparams(("parallel",)),
    )(p, p, p, p, p, p, gq, gkv, a_cos, a_up, a_dn, k_cos, b_up, b_dn)


def _mla_prep(qp, kp, kro, tabs, *, ts=256):
    s_ = qp.shape[0]
    _, _, _, q_cos, _, b_up, b_dn = tabs

    def body(q_ref, k_ref, kr_ref, qc, bu, bd, qo_ref, ko_ref):
        c_, u_, d_ = qc[...], bu[...], bd[...]
        kr = kr_ref[...]
        for h in range(MLA_HEADS):
            sl = slice(h * LANES, (h + 1) * LANES)
            qo_ref[:, sl] = _rope(q_ref[:, sl], c_, u_, d_, 112, 16).astype(qo_ref.dtype)
            ko_ref[:, sl] = (k_ref[:, sl] + kr).astype(ko_ref.dtype)

    tab = _rows(ts, LANES)
    return pl.pallas_call(
        body, name="mla_prep", grid=(s_ // ts,),
        in_specs=[_rows(ts, 1024), _rows(ts, 1024), tab, tab, tab, tab],
        out_specs=[_rows(ts, 1024), _rows(ts, 1024)],
        out_shape=[jax.ShapeDtypeStruct((s_, 1024), MXU_DTYPE)] * 2,
        compiler_params=_cparams(("parallel",)),
    )(qp, kp, kro, q_cos, b_up, b_dn)


def _mla_unprep(dqc, dkt, dvt, tabs):
    s_ = dqc.shape[0]
    ts = dkt.shape[3]
    _, _, _, q_cos, k_cos, b_up, b_dn = tabs

    def body(dq_ref, dk_ref, dv_ref, qc, kc, bu, bd, dqo_ref, dkvo_ref, dkr_ref):
        c_, u_, d_ = qc[...], bu[...], bd[...]
        tot = jnp.zeros((ts, LANES), F32)
        for h in range(MLA_HEADS):
            sl = slice(h * LANES, (h + 1) * LANES)
            dqo_ref[:, sl] = _rope_t(dq_ref[:, sl], c_, u_, d_, 112, 16).astype(dqo_ref.dtype)
            dk = dk_ref[h, 0].T
            dkvo_ref[:, sl] = dk.astype(dkvo_ref.dtype)
            tot = tot + dk
            dkvo_ref[:, 1024 + h * LANES:1024 + (h + 1) * LANES] = dv_ref[h, 0].T.astype(dkvo_ref.dtype)
        dkr_ref[...] = _rope_t(tot, kc[...], u_, d_, 112, 16).astype(dkr_ref.dtype)

    tab = _rows(ts, LANES)
    acc = pl.BlockSpec((MLA_HEADS, 1, LANES, ts), lambda i: (0, i, 0, 0))
    return pl.pallas_call(
        body, name="mla_unprep", grid=(s_ // ts,),
        in_specs=[_rows(ts, 1024), acc, acc, tab, tab, tab, tab],
        out_specs=[_rows(ts, 1024), _rows(ts, 2048), _rows(ts, LANES)],
        out_shape=[jax.ShapeDtypeStruct((s_, 1024), MXU_DTYPE), jax.ShapeDtypeStruct((s_, 2048), MXU_DTYPE),
                   jax.ShapeDtypeStruct((s_, LANES), MXU_DTYPE)],
        compiler_params=_cparams(("parallel",)),
    )(dqc, dkt, dvt, q_cos, k_cos, b_up, b_dn)


def _assemble_dp(dgab, dqa, dqlat, dkr, dka, dva, dkvlat, tabs, *, ts=256):
    s_ = dqa.shape[0]
    a_cos, a_up, a_dn = tabs[0], tabs[1], tabs[2]

    def body(dg_ref, dq_ref, dql_ref, dkr_ref, dk_ref, dv_ref, dkvl_ref, ac, au, ad, o_ref):
        c_, u_, d_ = ac[...], au[...], ad[...]
        o_ref[:, P_GA:P_Q] = dg_ref[...]
        for h in range(SWA_HEADS):
            sl = slice(h * LANES, (h + 1) * LANES)
            o_ref[:, P_Q + h * LANES:P_Q + (h + 1) * LANES] = _rope_t(dq_ref[:, sl], c_, u_, d_, 96, 32).astype(o_ref.dtype)
        o_ref[:, P_QLAT:P_KR] = dql_ref[...]
        o_ref[:, P_KR:P_K] = dkr_ref[...]
        for h in range(SWA_KV_HEADS):
            sl = slice(h * LANES, (h + 1) * LANES)
            o_ref[:, P_K + h * LANES:P_K + (h + 1) * LANES] = _rope_t(dk_ref[:, sl], c_, u_, d_, 96, 32).astype(o_ref.dtype)
        o_ref[:, P_V:P_KVLAT] = dv_ref[...]
        o_ref[:, P_KVLAT:P_W] = dkvl_ref[...]

    tab = _rows(ts, LANES)
    return pl.pallas_call(
        body, name="assemble_dp", grid=(s_ // ts,),
        in_specs=[_rows(ts, 2048), _rows(ts, 1024), _rows(ts, Q_LORA), _rows(ts, LANES), _rows(ts, 256), _rows(ts, 256),
                  _rows(ts, KV_LORA), tab, tab, tab],
        out_specs=_rows(ts, P_W), out_shape=jax.ShapeDtypeStruct((s_, P_W), MXU_DTYPE),
        compiler_params=_cparams(("parallel",)),
    )(dgab, dqa, dqlat, dkr, dka, dva, dkvlat, a_cos, a_up, a_dn)


def _attn_out_gate(oa, ob, woa, wob, p, *, ts=512):
    s_ = p.shape[0]

    def body(oa_ref, ob_ref, wa_ref, wb_ref, ga_ref, gb_ref, ta_ref, tb_ref, y_ref):
        ta = jnp.dot(oa_ref[...], wa_ref[...], preferred_element_type=F32)
        tb = jnp.dot(ob_ref[...], wb_ref[...], preferred_element_type=F32)
        ta_ref[...] = ta
        tb_ref[...] = tb
        y_ref[...] = (_sigmoid(ga_ref[...]) * ta + _sigmoid(gb_ref[...]) * tb).astype(y_ref.dtype)

    w = _const(1024, 1024)
    return pl.pallas_call(
        body, name="attn_out_gate", grid=(s_ // ts,),
        in_specs=[_rows(ts, 1024), _rows(ts, 1024), w, w, _rows(ts, 1024, P_GA // 1024), _rows(ts, 1024, P_GB // 1024)],
        out_specs=[_rows(ts, 1024)] * 3,
        out_shape=[jax.ShapeDtypeStruct((s_, 1024), F32)] * 2 + [jax.ShapeDtypeStruct((s_, 1024), MXU_DTYPE)],
        compiler_params=_cparams(("parallel",)),
    )(oa, ob, woa, wob, p, p)


def _d_y_gate(dx1b, wout, p, ta, tb, *, ts=512):
    s_ = p.shape[0]

    def body(dx_ref, w_ref, ga_ref, gb_ref, ta_ref, tb_ref, dta_ref, dtb_ref, dg_ref):
        d = lax.dot_general(dx_ref[...], w_ref[...], NT, preferred_element_type=F32)
        sa, sb = _sigmoid(ga_ref[...]), _sigmoid(gb_ref[...])
        dta_ref[...] = (d * sa).astype(dta_ref.dtype)
        dtb_ref[...] = (d * sb).astype(dtb_ref.dtype)
        dg_ref[:, 0:1024] = (d * ta_ref[...] * (sa * (1.0 - sa))).astype(dg_ref.dtype)
        dg_ref[:, 1024:2048] = (d * tb_ref[...] * (sb * (1.0 - sb))).astype(dg_ref.dtype)

    return pl.pallas_call(
        body, name="d_y_gate", grid=(s_ // ts,),
        in_specs=[_rows(ts, 1024), _const(1024, 1024), _rows(ts, 1024, P_GA // 1024), _rows(ts, 1024, P_GB // 1024),
                  _rows(ts, 1024), _rows(ts, 1024)],
        out_specs=[_rows(ts, 1024), _rows(ts, 1024), _rows(ts, 2048)],
        out_shape=[jax.ShapeDtypeStruct((s_, 1024), MXU_DTYPE)] * 2 + [jax.ShapeDtypeStruct((s_, 2048), MXU_DTYPE)],
        compiler_params=_cparams(("parallel",)),
    )(dx1b, wout, p, p, ta, tb)


FF_TILE = D_FF // 2


def _ffn_in_act(x1, g, wgu_t, *, tm=512):
    s_ = x1.shape[0]

    def body(x_ref, g_ref, w_ref, h_ref, gu_ref, a_ref):
        h = _rms(x_ref[...], g_ref[...]).astype(h_ref.dtype)
        h_ref[...] = h
        p = lax.dot_general(h, w_ref[...], NT, preferred_element_type=F32)
        gu_ref[...] = p
        gate = p[:, :FF_TILE]
        a_ref[...] = (gate * _sigmoid(gate) * p[:, FF_TILE:]).astype(a_ref.dtype)

    return pl.pallas_call(
        body, name="ffn_in", grid=(s_ // tm, 2),
        in_specs=[pl.BlockSpec((tm, D_MODEL), lambda i, j: (i, 0)), pl.BlockSpec((1, D_MODEL), lambda i, j: (0, 0)),
                  pl.BlockSpec((2 * FF_TILE, D_MODEL), lambda i, j: (j, 0))],
        out_specs=[pl.BlockSpec((tm, D_MODEL), lambda i, j: (i, 0)), pl.BlockSpec((tm, 2 * FF_TILE), lambda i, j: (i, j)),
                   pl.BlockSpec((tm, FF_TILE), lambda i, j: (i, j))],
        out_shape=[jax.ShapeDtypeStruct((s_, D_MODEL), MXU_DTYPE), jax.ShapeDtypeStruct((s_, 2 * D_FF), F32),
                   jax.ShapeDtypeStruct((s_, D_FF), MXU_DTYPE)],
        compiler_params=_cparams(("parallel", "arbitrary")),
    )(x1, g, wgu_t)


def _d_act_swiglu(dx2b, wd, gu, *, tm=512):
    s_ = dx2b.shape[0]

    def body(d_ref, w_ref, gu_ref, o_ref):
        da = lax.dot_general(d_ref[...], w_ref[...], NT, preferred_element_type=F32)
        g, u = gu_ref[:, :FF_TILE], gu_ref[:, FF_TILE:]
        sg = _sigmoid(g)
        o_ref[:, :FF_TILE] = (da * u * (sg * (1.0 + g * (1.0 - sg)))).astype(o_ref.dtype)
        o_ref[:, FF_TILE:] = (da * (g * sg)).astype(o_ref.dtype)

    gu_spec = pl.BlockSpec((tm, 2 * FF_TILE), lambda i, j: (i, j))
    return pl.pallas_call(
        body, name="d_act", grid=(s_ // tm, 2),
        in_specs=[pl.BlockSpec((tm, D_MODEL), lambda i, j: (i, 0)), pl.BlockSpec((FF_TILE, D_MODEL), lambda i, j: (j, 0)), gu_spec],
        out_specs=gu_spec, out_shape=jax.ShapeDtypeStruct((s_, 2 * D_FF), MXU_DTYPE),
        compiler_params=_cparams(("parallel", "parallel")),
    )(dx2b, wd, gu)


def _ffn_out_loss(act, wd, x1, g, tgt, *, ts=512):
    s_, c = x1.shape
    kk = act.shape[1]

    def body(a_ref, w_ref, x_ref, g_ref, t_ref, dx_ref, dxb_ref, dg_ref, lp_ref, tot_ref):
        v = x_ref[...] + jnp.dot(a_ref[...], w_ref[...], preferred_element_type=F32)
        r = lax.rsqrt(jnp.mean(v * v, axis=-1, keepdims=True) + EPS)
        xh = v * r
        gg = g_ref[...]
        e = xh * gg - t_ref[...]
        do = e * (1.0 / c)
        dxh = do * gg
        dx = r * (dxh - xh * jnp.mean(dxh * xh, axis=-1, keepdims=True))
        dx_ref[...] = dx
        dxb_ref[...] = dx.astype(dxb_ref.dtype)
        i = pl.program_id(0)

        @pl.when(i == 0)
        def _():
            dg_ref[...] = jnp.zeros(dg_ref.shape, F32)
            lp_ref[...] = jnp.zeros(lp_ref.shape, F32)

        dg_ref[...] += _sublane_sum(do * xh)
        lp_ref[...] += _sublane_sum(e * e)
        tot_ref[...] = jnp.full(tot_ref.shape, (0.5 / c) * jnp.sum(lp_ref[...]), F32)

    return pl.pallas_call(
        body, name="ffn_out_loss", grid=(s_ // ts,),
        in_specs=[_rows(ts, kk), _const(kk, c), _rows(ts, c), _const(1, c), _rows(ts, c)],
        out_specs=[_rows(ts, c), _rows(ts, c), _const(SUBLANES, c), _const(SUBLANES, c), _const(SUBLANES, LANES)],
        out_shape=[jax.ShapeDtypeStruct((s_, c), F32), jax.ShapeDtypeStruct((s_, c), MXU_DTYPE),
                   jax.ShapeDtypeStruct((SUBLANES, c), F32), jax.ShapeDtypeStruct((SUBLANES, c), F32),
                   jax.ShapeDtypeStruct((SUBLANES, LANES), F32)],
        compiler_params=_cparams(("arbitrary",)),
    )(act, wd, x1, g, tgt)


def _mla_bwd_prep(dob, o32, *, ts=256):
    s_ = dob.shape[0]

    def body(do_ref, o_ref, dob_ref, dl_ref):
        d = do_ref[...]
        dob_ref[...] = d.astype(dob_ref.dtype)
        prod = d * o_ref[...]
        for h in range(MLA_HEADS):
            dl_ref[h] = jnp.sum(prod[:, h * LANES:(h + 1) * LANES].T, axis=0, keepdims=True)

    return pl.pallas_call(
        body, name="mla_bwd_prep", grid=(s_ // ts,), in_specs=[_rows(ts, 1024), _rows(ts, 1024)],
        out_specs=[_rows(ts, 1024), pl.BlockSpec((MLA_HEADS, 1, ts), lambda i: (0, 0, i))],
        out_shape=[jax.ShapeDtypeStruct((s_, 1024), MXU_DTYPE), jax.ShapeDtypeStruct((MLA_HEADS, 1, s_), F32)],
        compiler_params=_cparams(("parallel",)),
    )(dob, o32)


SWA_T = 4 * BLOCK


SWA_W = SWA_GROUP * BLOCK


def _swa_masks(sb):
    kr = lax.broadcasted_iota(jnp.int32, (2 * BLOCK, SWA_W), 0)
    qc = jnp.bitwise_and(lax.broadcasted_iota(jnp.int32, (2 * BLOCK, SWA_W), 1), BLOCK - 1)
    band = jnp.logical_and(kr > qc, kr <= qc + BLOCK)
    first = jnp.logical_and(band, kr >= BLOCK)
    return band, jnp.logical_or(first, jnp.logical_and(band, sb > 0))


def _heads_to_rows(ref, rs):
    return jnp.concatenate([ref[rs, h * LANES:(h + 1) * LANES] for h in range(SWA_GROUP)], axis=0)


def _sink_row(sk_ref):
    return jnp.concatenate([sk_ref[0, h:h + 1, :] for h in range(SWA_GROUP)], axis=1) * LOG2E


def _swa_in_specs(rev, nsb):
    sbi = (lambda j: nsb - 1 - j) if rev else (lambda j: j)
    cur = pl.BlockSpec((SWA_T, LANES), lambda g, j: (sbi(j), g))
    prev = pl.BlockSpec((BLOCK, LANES), lambda g, j: (jnp.maximum(4 * sbi(j) - 1, 0), g))
    q = pl.BlockSpec((SWA_T, SWA_GROUP * LANES), lambda g, j: (sbi(j), g))
    sink = pl.BlockSpec((1, SUBLANES, LANES), lambda g, j: (g, 0, 0))
    lse = pl.BlockSpec((SWA_GROUP, 1, SWA_T), lambda g, j: (g, 0, sbi(j)))
    return q, cur, prev, sink, lse


def _swa_fwd(qa, ka, va, sink_b):
    s_ = qa.shape[0]
    nsb = s_ // SWA_T
    c2 = HEAD_DIM ** -0.5 * LOG2E

    def body(q_ref, kc_ref, kp_ref, vc_ref, vp_ref, sk_ref, o32_ref, o16_ref, lse_ref, kx, vx):
        kx[0:BLOCK, :] = kp_ref[...]
        kx[BLOCK:5 * BLOCK, :] = kc_ref[...]
        vx[0:BLOCK, :] = vp_ref[...]
        vx[BLOCK:5 * BLOCK, :] = vc_ref[...]
        band, band0 = _swa_masks(pl.program_id(1))
        sink2 = _sink_row(sk_ref)
        for b in range(4):
            rs = slice(b * BLOCK, (b + 1) * BLOCK)
            ks = slice(b * BLOCK, (b + 2) * BLOCK)
            st = lax.dot_general(kx[ks, :], _heads_to_rows(q_ref, rs), NT, preferred_element_type=F32) * c2
            st = jnp.where(band0 if b == 0 else band, st, -jnp.inf)
            m = jnp.maximum(jnp.max(st, axis=0, keepdims=True), sink2)
            pt = jnp.exp2(st - m)
            den = jnp.sum(pt, axis=0, keepdims=True) + jnp.exp2(sink2 - m)
            o = lax.dot_general((pt * (1.0 / den)).astype(MXU_DTYPE), vx[ks, :], TN, preferred_element_type=F32)
            lse = m + jnp.log2(den)
            for hh in range(SWA_GROUP):
                cs = slice(hh * LANES, (hh + 1) * LANES)
                o32_ref[rs, cs] = o[cs, :]
                o16_ref[rs, cs] = o[cs, :].astype(o16_ref.dtype)
                lse_ref[hh, :, rs] = lse[:, cs]

    q, cur, prev, sink, lse_spec = _swa_in_specs(False, nsb)
    return pl.pallas_call(
        body, name="swa_fwd", grid=(SWA_KV_HEADS, nsb), in_specs=[q, cur, prev, cur, prev, sink],
        out_specs=[q, q, lse_spec],
        out_shape=[jax.ShapeDtypeStruct((s_, SWA_HEADS * LANES), F32), jax.ShapeDtypeStruct((s_, SWA_HEADS * LANES), MXU_DTYPE),
                   jax.ShapeDtypeStruct((SWA_HEADS, 1, s_), F32)],
        scratch_shapes=[pltpu.VMEM((5 * BLOCK, LANES), MXU_DTYPE), pltpu.VMEM((5 * BLOCK, LANES), MXU_DTYPE)],
        compiler_params=_cparams(("parallel", "arbitrary")),
    )(qa, ka, ka, va, va, sink_b)


def _swa_bwd(qa, ka, va, sink_b, o32, do, lse):
    s_ = qa.shape[0]
    nsb = s_ // SWA_T
    scale = HEAD_DIM ** -0.5
    c2 = scale * LOG2E

    def body(q_ref, kc_ref, kp_ref, vc_ref, vp_ref, sk_ref, o_ref, do_ref, lse_ref,
             dq_ref, dk_ref, dv_ref, dsk_ref, kx, vx, kacc, vacc, kcar, vcar):
        j = pl.program_id(1)
        kx[0:BLOCK, :] = kp_ref[...]
        kx[BLOCK:5 * BLOCK, :] = kc_ref[...]
        vx[0:BLOCK, :] = vp_ref[...]
        vx[BLOCK:5 * BLOCK, :] = vc_ref[...]
        band, band0 = _swa_masks(nsb - 1 - j)
        kacc[...] = jnp.zeros(kacc.shape, F32)
        vacc[...] = jnp.zeros(vacc.shape, F32)

        @pl.when(j == 0)
        def _():
            kcar[...] = jnp.zeros(kcar.shape, F32)
            vcar[...] = jnp.zeros(vcar.shape, F32)
            dsk_ref[...] = jnp.zeros(dsk_ref.shape, F32)

        sink2 = _sink_row(sk_ref)
        dsink = jnp.zeros((1, SWA_W), F32)
        for b in range(4):
            rs = slice(b * BLOCK, (b + 1) * BLOCK)
            ks = slice(b * BLOCK, (b + 2) * BLOCK)
            q, k2, v2 = _heads_to_rows(q_ref, rs), kx[ks, :], vx[ks, :]
            d = _heads_to_rows(do_ref, rs)
            delta = jnp.sum((d * _heads_to_rows(o_ref, rs)).T, axis=0, keepdims=True)
            l2 = jnp.concatenate([lse_ref[hh, :, rs] for hh in range(SWA_GROUP)], axis=1)
            st = lax.dot_general(k2, q, NT, preferred_element_type=F32) * c2
            pt = jnp.exp2(jnp.where(band0 if b == 0 else band, st, -jnp.inf) - l2)
            db = d.astype(MXU_DTYPE)
            dst = (pt * (lax.dot_general(v2, db, NT, preferred_element_type=F32) - delta) * scale).astype(MXU_DTYPE)
            dq = lax.dot_general(dst, k2, TN, preferred_element_type=F32)
            for hh in range(SWA_GROUP):
                dq_ref[rs, hh * LANES:(hh + 1) * LANES] = dq[hh * LANES:(hh + 1) * LANES, :]
            kacc[ks, :] += jnp.dot(dst, q, preferred_element_type=F32)
            vacc[ks, :] += jnp.dot(pt.astype(MXU_DTYPE), db, preferred_element_type=F32)
            dsink = dsink - jnp.exp2(sink2 - l2) * delta
        for hh in range(SWA_GROUP):
            tot = jnp.sum(dsink[:, hh * LANES:(hh + 1) * LANES], axis=1, keepdims=True)
            dsk_ref[0, hh:hh + 1, :] += jnp.broadcast_to(tot, (1, LANES))

        dk_ref[0:3 * BLOCK, :] = kacc[BLOCK:4 * BLOCK, :]
        dk_ref[3 * BLOCK:4 * BLOCK, :] = kacc[4 * BLOCK:5 * BLOCK, :] + kcar[...]
        dv_ref[0:3 * BLOCK, :] = vacc[BLOCK:4 * BLOCK, :].astype(dv_ref.dtype)
        dv_ref[3 * BLOCK:4 * BLOCK, :] = (vacc[4 * BLOCK:5 * BLOCK, :] + vcar[...]).astype(dv_ref.dtype)
        kcar[...] = kacc[0:BLOCK, :]
        vcar[...] = vacc[0:BLOCK, :]

    q, cur, prev, sink, lse_spec = _swa_in_specs(True, nsb)
    return pl.pallas_call(
        body, name="swa_bwd", grid=(SWA_KV_HEADS, nsb),
        in_specs=[q, cur, prev, cur, prev, sink, q, q, lse_spec],
        out_specs=[q, cur, cur, sink],
        out_shape=[jax.ShapeDtypeStruct((s_, SWA_HEADS * LANES), F32), jax.ShapeDtypeStruct((s_, SWA_KV_HEADS * LANES), F32),
                   jax.ShapeDtypeStruct((s_, SWA_KV_HEADS * LANES), MXU_DTYPE),
                   jax.ShapeDtypeStruct((SWA_KV_HEADS, SUBLANES, LANES), F32)],
        scratch_shapes=[pltpu.VMEM((5 * BLOCK, LANES), MXU_DTYPE), pltpu.VMEM((5 * BLOCK, LANES), MXU_DTYPE),
                        pltpu.VMEM((5 * BLOCK, LANES), F32), pltpu.VMEM((5 * BLOCK, LANES), F32),
                        pltpu.VMEM((BLOCK, LANES), F32), pltpu.VMEM((BLOCK, LANES), F32)],
        compiler_params=_cparams(("arbitrary", "arbitrary")),
    )(qa, ka, ka, va, va, sink_b, o32, do, lse)


MLA_T = 512
MLA_FWD_GROUP = 4
MLA_BWD_GROUP = 2


def _mla_specs(s_, t, group):
    w = group * LANES
    qs = pl.BlockSpec((t, w), lambda g, i: (i, g))
    kv = pl.BlockSpec((s_, w), lambda g, i: (0, g))
    row = pl.BlockSpec((group, 1, t), lambda g, i: (g, 0, i))
    return qs, kv, row


def _causal_scores_t(k, q, t, c2, masked):
    st = lax.dot_general(k, q, NT, preferred_element_type=F32) * c2
    if masked:
        kr = lax.broadcasted_iota(jnp.int32, (t, t), 0)
        qc = lax.broadcasted_iota(jnp.int32, (t, t), 1)
        st = jnp.where(kr <= qc, st, -jnp.inf)
    return st


def _mla_fwd(qc, kc, vp):
    s_ = qc.shape[0]
    t = min(MLA_T, s_)
    c2 = MLA_QK ** -0.5 * LOG2E
    grp = MLA_FWD_GROUP

    def body(q_ref, k_ref, v_ref, o32_ref, o16_ref, lse_ref, m_s, acc_s):
        qi = pl.program_id(1)
        m_s[...] = jnp.full(m_s.shape, -jnp.inf, F32)
        acc_s[...] = jnp.zeros(acc_s.shape, F32)
        ones_lane = lax.broadcasted_iota(jnp.int32, (t, LANES), 1) == MLA_V

        def step(ki, masked):
            off = pl.multiple_of(ki * t, t)
            for g in range(grp):
                cs = slice(g * LANES, (g + 1) * LANES)
                st = _causal_scores_t(k_ref[pl.ds(off, t), cs], q_ref[:, cs], t, c2, masked)
                m_old = m_s[g]
                m_new = jnp.maximum(m_old, jnp.max(st, axis=0, keepdims=True))
                alpha = jnp.exp2(m_old - m_new)
                pt = jnp.exp2(st - m_new).astype(MXU_DTYPE)
                v = v_ref[pl.ds(off, t), cs]
                v = jnp.where(ones_lane, jnp.ones((), v.dtype), v)
                acc_s[g] = alpha * acc_s[g] + lax.dot_general(v, pt, TN, preferred_element_type=F32)
                m_s[g] = m_new

        def full_block(ki, carry):
            step(ki, False)
            return carry

        lax.fori_loop(0, qi, full_block, 0)
        step(qi, True)
        for g in range(grp):
            cs = slice(g * LANES, (g + 1) * LANES)
            acc = acc_s[g]
            l = acc[MLA_V:MLA_V + 1, :]
            o = (acc * (1.0 / l)).T
            o32_ref[:, cs] = o
            o16_ref[:, cs] = o.astype(o16_ref.dtype)
            lse_ref[g] = m_s[g] + jnp.log2(l)

    qs, kv, row = _mla_specs(s_, t, grp)
    return pl.pallas_call(
        body, name="mla_fwd", grid=(MLA_HEADS // grp, s_ // t), in_specs=[qs, kv, kv], out_specs=[qs, qs, row],
        out_shape=[jax.ShapeDtypeStruct((s_, MLA_HEADS * LANES), F32), jax.ShapeDtypeStruct((s_, MLA_HEADS * LANES), MXU_DTYPE),
                   jax.ShapeDtypeStruct((MLA_HEADS, 1, s_), F32)],
        scratch_shapes=[pltpu.VMEM((grp, 1, t), F32), pltpu.VMEM((grp, LANES, t), F32)],
        compiler_params=_cparams(("parallel", "arbitrary")),
    )(qc, kc, vp)


def _mla_bwd(qc, kc, vp, dob, lse, delta):
    s_ = qc.shape[0]
    t = min(MLA_T, s_)
    n = s_ // t
    scale = MLA_QK ** -0.5
    c2 = scale * LOG2E
    grp = MLA_BWD_GROUP

    def body(q_ref, do_ref, lse_ref, dl_ref, k_ref, v_ref, dq_ref, dk_ref, dv_ref, dqt_s, qt_s, dt_s):
        qi = pl.program_id(1)

        @pl.when(qi == 0)
        def _():
            dk_ref[...] = jnp.zeros(dk_ref.shape, F32)
            dv_ref[...] = jnp.zeros(dv_ref.shape, F32)

        dqt_s[...] = jnp.zeros(dqt_s.shape, F32)
        for g in range(grp):
            cs = slice(g * LANES, (g + 1) * LANES)
            qt_s[g] = q_ref[:, cs].astype(F32).T.astype(qt_s.dtype)
            dt_s[g] = do_ref[:, cs].astype(F32).T.astype(dt_s.dtype)

        def step(ki, masked):
            off = pl.multiple_of(ki * t, t)
            for g in range(grp):
                cs = slice(g * LANES, (g + 1) * LANES)
                k = k_ref[pl.ds(off, t), cs]
                pt = jnp.exp2(_causal_scores_t(k, q_ref[:, cs], t, c2, masked) - lse_ref[g])
                dpt = lax.dot_general(v_ref[pl.ds(off, t), cs], do_ref[:, cs], NT, preferred_element_type=F32)
                dst = (pt * (dpt - dl_ref[g]) * scale).astype(MXU_DTYPE)
                dv_ref[g, ki] += lax.dot_general(dt_s[g], pt.astype(MXU_DTYPE), NT, preferred_element_type=F32)
                dk_ref[g, ki] += lax.dot_general(qt_s[g], dst, NT, preferred_element_type=F32)
                dqt_s[g] += lax.dot_general(k, dst, TN, preferred_element_type=F32)

        def full_block(ki, carry):
            step(ki, False)
            return carry

        lax.fori_loop(0, qi, full_block, 0)
        step(qi, True)
        for g in range(grp):
            dq_ref[:, g * LANES:(g + 1) * LANES] = dqt_s[g].T

    qs, kv, row = _mla_specs(s_, t, grp)
    acc_spec = pl.BlockSpec((grp, n, LANES, t), lambda g, i: (g, 0, 0, 0))
    acc_shape = jax.ShapeDtypeStruct((MLA_HEADS, n, LANES, t), F32)
    return pl.pallas_call(
        body, name="mla_bwd", grid=(MLA_HEADS // grp, n), in_specs=[qs, qs, row, row, kv, kv],
        out_specs=[qs, acc_spec, acc_spec], out_shape=[jax.ShapeDtypeStruct((s_, MLA_HEADS * LANES), F32), acc_shape, acc_shape],
        scratch_shapes=[pltpu.VMEM((grp, LANES, t), F32), pltpu.VMEM((grp, LANES, t), MXU_DTYPE),
                        pltpu.VMEM((grp, LANES, t), MXU_DTYPE)],
        compiler_params=_cparams(("parallel", "arbitrary")),
    )(qc, dob, lse, delta, kc, vp)


def _pad_heads(w, nh, hd, axis):
    shp = w.shape
    w = w.reshape(shp[:axis] + (nh, hd) + shp[axis + 1:])
    pad = [(0, 0)] * w.ndim
    pad[axis + 1] = (0, LANES - hd)
    w = jnp.pad(w, pad)
    return w.reshape(shp[:axis] + (nh * LANES,) + shp[axis + 1:])


def _unpad_heads(w, nh, hd, axis):
    shp = w.shape
    w = w.reshape(shp[:axis] + (nh, LANES) + shp[axis + 1:])
    w = lax.slice_in_dim(w, 0, hd, axis=axis + 1)
    return w.reshape(shp[:axis] + (nh * hd,) + shp[axis + 1:])


PACK_W = 1024
ROW_TILE = 16
FULL_SHAPE = dict(w_in=(1024, 3488), w_uq=(384, 768), w_ukv=(256, 1024), w_o_swa=(512, 1024), w_o_mla=(512, 1024),
                  w_out=(1024, 1024), w_gate=(1024, 2816), w_up=(1024, 2816), w_down=(2816, 1024))
BIG = tuple(FULL_SHAPE)
ROW_SHARDED = ("w_out", "w_down")
W_IN_COLS = FULL_SHAPE["w_in"][1] // N_DEV
W_IN_ROWS = -(-W_IN_COLS // ROW_TILE) * ROW_TILE
FF_COLS = D_FF // N_DEV
OUT_ROWS = D_MODEL // N_DEV
SMALL_ROW0 = W_IN_ROWS + OUT_ROWS
SMALL_FLAT = (("w_uq", 0, 36), ("w_ukv", 48, 32), ("w_o_swa", 80, 64), ("w_o_mla", 144, 64))
SMALL_ROWS = 208
EARLY_ROWS = SMALL_ROW0 + SMALL_ROWS
LATE_ROWS = 3 * FF_COLS
PACK_ROWS = EARLY_ROWS + LATE_ROWS


def _shard_shape(n):
    r, c = FULL_SHAPE[n]
    return (r // N_DEV, c) if n in ROW_SHARDED else (r, c // N_DEV)


def _wire_pack(sh, dtype):
    c = lambda n: sh[n].astype(dtype)
    rows = [jnp.pad(c("w_in").T, ((0, W_IN_ROWS - W_IN_COLS), (0, 0))), c("w_out")]
    for n, _, r in SMALL_FLAT:
        rows.append(jnp.pad(c(n).reshape(r, PACK_W), ((0, -r % ROW_TILE), (0, 0))))
    return jnp.concatenate(rows + [c("w_gate").T, c("w_up").T, c("w_down")], 0)


MID_ROWS = OUT_ROWS + SMALL_ROWS


def _mid_unpack(p):
    out = dict(w_out=p[0:OUT_ROWS])
    for n, off, r in SMALL_FLAT:
        out[n] = p[OUT_ROWS + off:OUT_ROWS + off + r].reshape(_shard_shape(n))
    return out


def _w_in_row_maps():
    sp = lambda col: (col // W_IN_COLS) * W_IN_ROWS + col % W_IN_COLS
    fwd = np.full((P_W,), -1, np.int64)

    def put(t0, c0, n):
        fwd[t0:t0 + n] = [sp(c) for c in range(c0, c0 + n)]

    put(P_GA, IN_OFF[6], D_MODEL)
    put(P_GB, IN_OFF[7], D_MODEL)
    for h in range(SWA_HEADS):
        put(P_Q + LANES * h, IN_OFF[0] + HEAD_DIM * h, HEAD_DIM)
    put(P_QLAT, IN_OFF[3], Q_LORA)
    put(P_KR + KR_LANE, IN_OFF[5], MLA_ROPE)
    for h in range(SWA_KV_HEADS):
        put(P_K + LANES * h, IN_OFF[1] + HEAD_DIM * h, HEAD_DIM)
        put(P_V + LANES * h, IN_OFF[2] + HEAD_DIM * h, HEAD_DIM)
    put(P_KVLAT, IN_OFF[4], KV_LORA)
    inv = np.full((N_DEV * W_IN_ROWS,), -1, np.int64)
    inv[fwd[fwd >= 0]] = np.nonzero(fwd >= 0)[0]
    return fwd, inv


def _take_rows(src, idx, *, name):
    n_out, n_src, width = len(idx), src.shape[0], src.shape[1]
    assert n_out % BLOCK == 0 and n_src % BLOCK == 0
    n_tiles = n_out // BLOCK
    blocks = [sorted({int(v) // BLOCK for v in idx[i * BLOCK:(i + 1) * BLOCK] if v >= 0}) for i in range(n_tiles)]
    k_max = max(1, max(len(b) for b in blocks))
    tab = np.zeros((n_tiles, k_max), np.int32)
    sel = np.zeros((n_tiles, k_max, BLOCK, BLOCK), np.float32)
    for i, blks in enumerate(blocks):
        for m, b in enumerate(blks):
            tab[i, m] = b
            for r in range(BLOCK):
                v = int(idx[i * BLOCK + r])
                if v >= 0 and v // BLOCK == b:
                    sel[i, m, r, v % BLOCK] = 1.0

    def body(tab_ref, sel_ref, *refs):
        o_ref = refs[k_max]
        acc = jnp.dot(sel_ref[0, 0], refs[0][...], preferred_element_type=F32)
        for m in range(1, k_max):
            acc = acc + jnp.dot(sel_ref[0, m], refs[m][...], preferred_element_type=F32)
        o_ref[...] = acc.astype(o_ref.dtype)

    def src_spec(m):
        return pl.BlockSpec((BLOCK, width), lambda i, t: (t[i * k_max + m], 0))

    return pl.pallas_call(
        body, name=name,
        grid_spec=pltpu.PrefetchScalarGridSpec(
            num_scalar_prefetch=1, grid=(n_tiles,),
            in_specs=[pl.BlockSpec((1, k_max, BLOCK, BLOCK), lambda i, t: (i, 0, 0, 0))] + [src_spec(m) for m in range(k_max)],
            out_specs=pl.BlockSpec((BLOCK, width), lambda i, t: (i, 0))),
        out_shape=jax.ShapeDtypeStruct((n_out, width), src.dtype),
        compiler_params=_cparams(("parallel",)),
    )(jnp.asarray(tab.reshape(-1)), jnp.asarray(sel, src.dtype), *([src] * k_max))


def _w_in_operand(win_g):
    return _take_rows(win_g.reshape(N_DEV * W_IN_ROWS, PACK_W), _w_in_row_maps()[0], name="w_in_rows")


def _mid_operands(wout_g, small_g):
    def full(n, off, r):
        a = small_g[:, off:off + r].reshape((N_DEV,) + _shard_shape(n))
        return jnp.moveaxis(a, 0, 1).reshape(FULL_SHAPE[n])

    w = {n: full(n, off, r) for n, off, r in SMALL_FLAT}
    ukv = w["w_ukv"].reshape(KV_LORA, MLA_HEADS, MLA_NOPE + MLA_V)
    return dict(
        wout=wout_g.reshape(D_MODEL, D_MODEL),
        wuq=_pad_heads(w["w_uq"], MLA_HEADS, MLA_QK, 1),
        wuk=_pad_heads(ukv[:, :, :MLA_NOPE].reshape(KV_LORA, -1), MLA_HEADS, MLA_NOPE, 1),
        wuv=_pad_heads(ukv[:, :, MLA_NOPE:].reshape(KV_LORA, -1), MLA_HEADS, MLA_V, 1),
        woa=_pad_heads(w["w_o_swa"], SWA_HEADS, HEAD_DIM, 0),
        wob=_pad_heads(w["w_o_mla"], MLA_HEADS, MLA_V, 0),
    )


def _mid_grad_pack(g):
    uk = _unpad_heads(g["wukv"][:, :1024], MLA_HEADS, MLA_NOPE, 1).reshape(KV_LORA, MLA_HEADS, MLA_NOPE)
    uv = _unpad_heads(g["wukv"][:, 1024:], MLA_HEADS, MLA_V, 1).reshape(KV_LORA, MLA_HEADS, MLA_V)
    w = dict(w_uq=_unpad_heads(g["wuq"], MLA_HEADS, MLA_QK, 1), w_ukv=jnp.concatenate([uk, uv], 2).reshape(KV_LORA, -1),
             w_o_swa=_unpad_heads(g["woa"], SWA_HEADS, HEAD_DIM, 0), w_o_mla=_unpad_heads(g["wob"], MLA_HEADS, MLA_V, 0))

    def flat(n, r):
        rr, cc = FULL_SHAPE[n]
        a = jnp.moveaxis(w[n].reshape(rr, N_DEV, cc // N_DEV), 1, 0).reshape(N_DEV, r, PACK_W)
        return jnp.pad(a, ((0, 0), (0, -r % ROW_TILE), (0, 0))).astype(WIRE_DTYPE)

    return jnp.concatenate([g["wout"].reshape(N_DEV, OUT_ROWS, PACK_W)] + [flat(n, r) for n, _, r in SMALL_FLAT], 1)


def _w_in_grad_chunks(g_win_t):
    return _take_rows(g_win_t, _w_in_row_maps()[1], name="dw_in_rows").reshape(N_DEV, W_IN_ROWS, PACK_W)


def _local_step(x, tgt, win_t, small, weights, grads):
    s_ = x.shape[0]
    tabs = _rope_tables(s_)
    sink_b = jnp.broadcast_to(small["swa_sinks"].reshape(SWA_KV_HEADS, SWA_GROUP, 1), (SWA_KV_HEADS, SWA_GROUP, LANES))
    sink_b = jnp.pad(sink_b, ((0, 0), (0, SUBLANES - SWA_GROUP), (0, 0)))

    h, p = _norm_mm(x, small["mix_norm_g"], win_t, name="proj_in", tn=2176, tm=1024)
    qa, ka, va, cq, ckv, kro = _attn_prep(p, small["q_norm_g"], small["kv_norm_g"], tabs)
    ops = weights.mid(cq)
    oa32, oa16, lse_a = _swa_fwd(qa, ka, va, sink_b)
    qp = _mm(cq, ops["wuq"], "nn", name="mla_q_up", tm=1024, tn=1024)
    kp = _mm(ckv, ops["wuk"], "nn", name="mla_k_up", tm=1024, tn=1024)
    vp = _mm(ckv, ops["wuv"], "nn", name="mla_v_up", tm=1024, tn=1024, out_dtype=MXU_DTYPE)
    qc, kc = _mla_prep(qp, kp, kro, tabs)
    ob32, ob16, lse_b = _mla_fwd(qc, kc, vp)
    ta, tb, y = _attn_out_gate(oa16, ob16, ops["woa"], ops["wob"], p)
    x1 = _mm(y, ops["wout"], "nn", name="out_proj", add=x, tm=1024, tn=1024)
    wgu_t, wd = weights.late(x1)
    h2, gu, act = _ffn_in_act(x1, small["ffn_norm_g"], wgu_t)

    dx2, dx2b, dg3, _, tot = _ffn_out_loss(act, wd, x1, small["final_norm_g"].reshape(1, D_MODEL), tgt)
    g = {}
    g_wd = _mm(act, dx2b, "tn", name="dw_down", tm=1408, tn=1024, tk=1024, out_dtype=WIRE_DTYPE)
    dgu = _d_act_swiglu(dx2b, wd, gu)
    g_wgu = _mm(dgu, h2, "tn", name="dw_ffn_in", tm=1408, tn=1024, tk=1024, out_dtype=WIRE_DTYPE)
    token = grads.late(g_wgu, g_wd)
    dx1, dx1b, dg2 = _mm_norm_bwd(dgu, wgu_t, x1, small["ffn_norm_g"] + token[0:1, 0:1], dx2, name="d_h2", tk=2816)
    g["wout"] = _mm(y, dx1b, "tn", name="dw_out", tm=1024, tn=1024, tk=1024, out_dtype=WIRE_DTYPE)
    dta, dtb, dgab = _d_y_gate(dx1b, ops["wout"], p, ta, tb)
    doa = _mm(dta, ops["woa"], "nt", name="d_oa", tm=1024, tn=1024)
    g["woa"] = _mm(oa16, dta, "tn", name="dw_o_swa", tm=1024, tn=1024, tk=1024)
    dob = _mm(dtb, ops["wob"], "nt", name="d_ob", tm=1024, tn=1024)
    g["wob"] = _mm(ob16, dtb, "tn", name="dw_o_mla", tm=1024, tn=1024, tk=1024)
    dob16, delta_b = _mla_bwd_prep(dob, ob32)
    dqc, dkc, dvp = _mla_bwd(qc, kc, vp, dob16, lse_b, delta_b)
    dqp, dkv, dkr = _mla_unprep(dqc, dkc, dvp, tabs)
    dcq = _mm(dqp, ops["wuq"], "nt", name="d_cq", tn=Q_LORA)
    g["wuq"] = _mm(cq, dqp, "tn", name="dw_uq", tm=Q_LORA, tn=1024, tk=512)
    dckv = _mm(dkv, jnp.concatenate([ops["wuk"], ops["wuv"]], 1), "nt", name="d_ckv", tn=KV_LORA)
    g["wukv"] = _mm(ckv, dkv, "tn", name="dw_ukv", tm=KV_LORA, tn=1024, tk=512)
    token = grads.mid(g)
    _, dqlat, dgq = _norm_bwd(p, small["q_norm_g"] + token[0:1, 0:1], dcq, None, name="qnorm_bwd", x_cb=P_QLAT // Q_LORA)
    _, dkvlat, dgkv = _norm_bwd(p, small["kv_norm_g"], dckv, None, name="kvnorm_bwd", x_cb=P_KVLAT // KV_LORA)
    dqa, dka, dva, dsk = _swa_bwd(qa, ka, va, sink_b, oa32, doa, lse_a)
    dp = _assemble_dp(dgab, dqa, dqlat, dkr, dka, dva, dkvlat, tabs)
    token = grads.last(_mm(dp, h, "tn", name="dw_in", tm=2176, tn=1024, tk=1024, out_dtype=WIRE_DTYPE))
    gx, _, dg1 = _mm_norm_bwd(dp, win_t, x, small["mix_norm_g"], dx1, name="d_h", tk=2176, after=token)

    sm = dict(mix_norm_g=dg1, ffn_norm_g=dg2, final_norm_g=dg3, q_norm_g=dgq, kv_norm_g=dgkv,
              swa_sinks=dsk[:, :SWA_GROUP, 0].reshape(1, SWA_HEADS))
    return tot, gx, sm


MESH = pl.DeviceIdType.MESH
ANY = pl.BlockSpec(memory_space=pl.ANY)


def _position():
    return lax.axis_index("x"), lax.axis_index("y"), lax.axis_index("c")


def _all_gather(block, pieces, shapes, *, name):
    n_out = len(shapes)
    n_rows = sum(p[3] for p in pieces)

    def body(x_ref, *refs):
        outs, (send_sems, recv_sems, local_sem) = refs[:n_out], refs[n_out:]
        x, y, c = _position()
        me, sibling = (x, y, c), (x, y, 1 - c)
        chips = [(1 - x, y), (x, 1 - y), (1 - x, 1 - y)]

        def dst(piece, blk):
            arr, lead, _, _ = piece
            return outs[arr].at[lead(4 * blk[0] + 2 * blk[1] + blk[2])]

        def own(piece):
            return x_ref.at[pl.ds(piece[2], piece[3])]

        def copies(k, blk, to, from_input):
            return [pltpu.make_async_remote_copy(
                src_ref=own(p) if from_input else dst(p, blk), dst_ref=dst(p, blk), send_sem=send_sems.at[k],
                recv_sem=recv_sems.at[k], device_id=to, device_id_type=MESH) for p in pieces]

        gathered_rows = x_ref.at[pl.ds(0, n_rows)]

        def whole_block(k):
            return pltpu.make_async_remote_copy(src_ref=gathered_rows, dst_ref=gathered_rows, send_sem=send_sems.at[k],
                                                recv_sem=recv_sems.at[k], device_id=me, device_id_type=MESH)

        for p in pieces:
            pltpu.make_async_copy(own(p), dst(p, me), local_sem).start()
        for cp in copies(0, me, sibling, True):
            cp.start()
        for j, chip in enumerate(chips):
            for cp in copies(1 + j, me, (*chip, c), True):
                cp.start()
        for j, chip in enumerate(chips):
            whole_block(1 + j).wait_recv()
            for cp in copies(4 + j, (*chip, c), sibling, False):
                cp.start()
        whole_block(0).wait_recv()
        for j in range(3):
            whole_block(4 + j).wait_recv()
        for k in range(7):
            whole_block(k).wait_send()
        pltpu.make_async_copy(gathered_rows, gathered_rows, local_sem).wait()

    return pl.pallas_call(
        body, name=name, out_shape=[jax.ShapeDtypeStruct(s, block.dtype) for s in shapes], in_specs=[ANY],
        out_specs=[ANY] * n_out,
        scratch_shapes=[pltpu.SemaphoreType.DMA((7,)), pltpu.SemaphoreType.DMA((7,)), pltpu.SemaphoreType.DMA],
    )(block)


HBM = pl.BlockSpec(memory_space=pltpu.HBM)
SEM = pl.BlockSpec(memory_space=pltpu.SEMAPHORE)
TILE_DEVS = FF_TILE // FF_COLS
GU_SHAPE = (2, 2, TILE_DEVS, FF_COLS, PACK_W)


def _gate_slab(d):
    return (d // TILE_DEVS, 0, d % TILE_DEVS)


def _up_slab(d):
    return (d // TILE_DEVS, 1, d % TILE_DEVS)
D_SHAPE = (N_DEV, FF_COLS, PACK_W)
LAND_SHAPE = (N_DEV, LATE_ROWS, PACK_W)


def _split_params():
    return pltpu.CompilerParams(has_side_effects=pltpu.SideEffectType.DATAFLOW_SIDE_EFFECTING)


def _peer(x, y, c, k):
    return ((1 - x) if k & 4 else x, (1 - y) if k & 2 else y, (1 - c) if k & 1 else c)


def _empty_hbm(shape, dtype):
    return pltpu.with_memory_space_constraint(lax.empty(shape, dtype), pltpu.HBM)


def _wait_all(rows, send_sems, recv_sems, me):
    for k in range(N_DEV - 1):
        cp = pltpu.make_async_remote_copy(src_ref=rows, dst_ref=rows, send_sem=send_sems.at[k], recv_sem=recv_sems.at[k],
                                          device_id=me, device_id_type=MESH)
        cp.wait_send()
        cp.wait_recv()


def _token_shape():
    return jax.ShapeDtypeStruct((SUBLANES, LANES), F32)


def _gather_start(pack, row0, pieces, shapes, *, name):
    n = len(shapes)

    def body(*refs):
        p_ref, bufs, send_sems, recv_sems, token = refs[0], refs[1:1 + n], refs[1 + n], refs[2 + n], refs[-1]
        x, y, c = _position()
        me = 4 * x + 2 * y + c
        for k in range(1, N_DEV):
            off = row0
            for buf, lead, rows in pieces:
                pltpu.make_async_remote_copy(
                    src_ref=p_ref.at[pl.ds(off, rows)], dst_ref=bufs[buf].at[lead(me)], send_sem=send_sems.at[k - 1],
                    recv_sem=recv_sems.at[k - 1], device_id=_peer(x, y, c, k), device_id_type=MESH).start()
                off += rows
        token[...] = jnp.zeros_like(token)

    sems, dt = pltpu.SemaphoreType.DMA((N_DEV - 1,)), pack.dtype
    return pl.pallas_call(
        body, name=name,
        out_shape=(sems, sems, pltpu.HBM(pack.shape, dt)) + tuple(pltpu.HBM(s, dt) for s in shapes) + (_token_shape(),),
        in_specs=(HBM,) * (1 + n), out_specs=(SEM, SEM) + (HBM,) * (1 + n) + (pl.BlockSpec(memory_space=pltpu.VMEM),),
        input_output_aliases={i: 2 + i for i in range(1 + n)}, compiler_params=_split_params(),
    )(pltpu.with_memory_space_constraint(pack, pltpu.HBM), *[_empty_hbm(s, dt) for s in shapes])


def _gather_wait(started, row0, n_rows, after, *, name):
    send_sems, recv_sems, pack, *bufs = started[:-1]
    n = len(bufs)

    def body(*refs):
        _wait_all(refs[0].at[pl.ds(row0, n_rows)], refs[1 + n], refs[2 + n], _position())

    outs = pl.pallas_call(
        body, name=name, out_shape=tuple(pltpu.HBM(a.shape, a.dtype) for a in (pack, *bufs)),
        in_specs=(HBM,) * (1 + n) + (SEM, SEM, ANY), out_specs=(HBM,) * (1 + n),
        input_output_aliases={i: i for i in range(1 + n)}, compiler_params=_split_params(),
    )(pack, *bufs, send_sems, recv_sems, after)
    return outs[0], outs[1:]


def _scatter_start(srcs, pieces, *, name):
    n = len(srcs)
    land_shape = (N_DEV, sum(p[2] for p in pieces), PACK_W)

    def body(*refs):
        src_refs, land_ref, send_sems, recv_sems, token = refs[:n], refs[n], refs[n + 1], refs[n + 2], refs[-1]
        x, y, c = _position()
        me = 4 * x + 2 * y + c
        for k in range(1, N_DEV):
            px, py, pc = _peer(x, y, c, k)
            off = 0
            for si, lead, rows in pieces:
                pltpu.make_async_remote_copy(
                    src_ref=src_refs[si].at[lead(4 * px + 2 * py + pc)], dst_ref=land_ref.at[me, pl.ds(off, rows)],
                    send_sem=send_sems.at[k - 1], recv_sem=recv_sems.at[k - 1], device_id=(px, py, pc),
                    device_id_type=MESH).start()
                off += rows
        token[...] = jnp.zeros_like(token)

    sems, dt = pltpu.SemaphoreType.DMA((N_DEV - 1,)), srcs[0].dtype
    return pl.pallas_call(
        body, name=name,
        out_shape=(sems, sems) + tuple(pltpu.HBM(a.shape, dt) for a in srcs) + (pltpu.HBM(land_shape, dt), _token_shape()),
        in_specs=(HBM,) * (n + 1), out_specs=(SEM, SEM) + (HBM,) * (n + 1) + (pl.BlockSpec(memory_space=pltpu.VMEM),),
        input_output_aliases={i: 2 + i for i in range(n + 1)}, compiler_params=_split_params(),
    )(*[pltpu.with_memory_space_constraint(a, pltpu.HBM) for a in srcs], _empty_hbm(land_shape, dt))


def _scatter_wait(started, after, *, name):
    send_sems, recv_sems, *bufs = started[:-1]
    n = len(bufs)

    def body(*refs):
        _wait_all(refs[n - 1].at[0], refs[n], refs[n + 1], _position())

    return pl.pallas_call(
        body, name=name, out_shape=tuple(pltpu.HBM(a.shape, a.dtype) for a in bufs),
        in_specs=(HBM,) * n + (SEM, SEM, ANY), out_specs=(HBM,) * n, input_output_aliases={i: i for i in range(n)},
        compiler_params=_split_params(),
    )(*bufs, send_sems, recv_sems, after)


def _peer_sum(own, own_lead, land, block, rows, idx, *, name):
    lead_rank = own.ndim - 2

    def body(idx_ref, own_ref, *refs):
        o_ref = refs[N_DEV - 1]
        acc = own_ref[(0,) * lead_rank].astype(F32)
        for k in range(N_DEV - 1):
            acc = acc + refs[k][0].astype(F32)
        o_ref[...] = acc

    own_spec = pl.BlockSpec((1,) * lead_rank + (rows, PACK_W), lambda i, t: own_lead(t[0]) + (0, 0))

    def land_spec(k):
        return pl.BlockSpec((1, rows, PACK_W), lambda i, t: (t[k + 1], block, 0))

    return pl.pallas_call(
        body, name=name,
        grid_spec=pltpu.PrefetchScalarGridSpec(
            num_scalar_prefetch=1, grid=(1,), in_specs=[own_spec] + [land_spec(k) for k in range(N_DEV - 1)],
            out_specs=pl.BlockSpec((rows, PACK_W), lambda i, t: (0, 0))),
        out_shape=jax.ShapeDtypeStruct((rows, PACK_W), F32), compiler_params=_cparams(("arbitrary",)),
    )(idx, own, *([land] * (N_DEV - 1)))


def _adamw(w, g, m, v):
    m = ADAM_B1 * m + (1.0 - ADAM_B1) * g
    v = ADAM_B2 * v + (1.0 - ADAM_B2) * (g * g)
    m_hat = m / (1.0 - ADAM_B1 ** ADAM_STEP)
    v_hat = v / (1.0 - ADAM_B2 ** ADAM_STEP)
    delta = -ADAM_LR * (m_hat / (jnp.sqrt(v_hat) + ADAM_EPS) + ADAM_WD * w)
    return delta, m, v


def _adamw_call(w, g, m, v, *, name, max_rows=256):
    _, r, c_ = w.shape
    tr = max_rows if r > max_rows and r % max_rows == 0 else r

    def body(w_ref, g_ref, m_ref, v_ref, d_ref, mo_ref, vo_ref):
        d, mn, vn = _adamw(w_ref[0], g_ref[...], m_ref[0], v_ref[0])
        d_ref[0] = d
        mo_ref[0] = mn
        vo_ref[0] = vn

    row3 = pl.BlockSpec((1, tr, c_), lambda i: (0, i, 0))
    shp = jax.ShapeDtypeStruct((1, r, c_), F32)
    return pl.pallas_call(
        body, name=name, grid=(r // tr,), in_specs=[row3, pl.BlockSpec((tr, c_), lambda i: (i, 0)), row3, row3],
        out_specs=[row3] * 3, out_shape=[shp] * 3, compiler_params=_cparams(("parallel",)),
    )(w, g, m, v)


SMALL = ("mix_norm_g", "ffn_norm_g", "final_norm_g", "q_norm_g", "kv_norm_g", "swa_sinks")
SMALL_W = dict(mix_norm_g=1024, ffn_norm_g=1024, final_norm_g=1024, q_norm_g=Q_LORA, kv_norm_g=KV_LORA, swa_sinks=SWA_HEADS)


def _small_adamw(parts, w, m, v):
    n_par = parts.shape[1] // SUBLANES

    def body(p_ref, w_ref, m_ref, v_ref, g_ref, d_ref, mo_ref, vo_ref):
        tot = p_ref[0]
        for dev in range(1, N_DEV):
            tot = tot + p_ref[dev]
        row_id = lax.broadcasted_iota(jnp.int32, (SUBLANES, PACK_W), 0)
        g = jnp.zeros((SUBLANES, PACK_W), F32)
        for k in range(n_par):
            g = jnp.where(row_id == k, jnp.sum(tot[k * SUBLANES:(k + 1) * SUBLANES, :], axis=0, keepdims=True), g)
        d, mn, vn = _adamw(w_ref[...], g, m_ref[...], v_ref[...])
        g_ref[...] = g
        d_ref[...] = d
        mo_ref[...] = mn
        vo_ref[...] = vn

    shp = jax.ShapeDtypeStruct((SUBLANES, PACK_W), F32)
    vm = pl.BlockSpec(memory_space=pltpu.VMEM)
    return pl.pallas_call(body, name="small_adamw", in_specs=[vm] * 4, out_specs=[vm] * 4, out_shape=[shp] * 4)(parts, w, m, v)


def _small_pack(d, rows_each):
    parts = [jnp.pad(d[n].astype(F32), ((0, 0), (0, PACK_W - SMALL_W[n]))) for n in SMALL]
    out = jnp.concatenate(parts, 0)
    pad = -out.shape[0] % SUBLANES
    return jnp.pad(out, ((0, pad), (0, 0)))


def kernel(x, mix_norm_g, w_in, swa_sinks, q_norm_g, w_uq, kv_norm_g, w_ukv, w_o_swa, w_o_mla, w_out, ffn_norm_g, w_gate, w_up, w_down, final_norm_g, loss_target, m_mix_norm_g, m_w_in, m_swa_sinks, m_q_norm_g, m_w_uq, m_kv_norm_g, m_w_ukv, m_w_o_swa, m_w_o_mla, m_w_out, m_ffn_norm_g, m_w_gate, m_w_up, m_w_down, m_final_norm_g, v_mix_norm_g, v_w_in, v_swa_sinks, v_q_norm_g, v_w_uq, v_kv_norm_g, v_w_ukv, v_w_o_swa, v_w_o_mla, v_w_out, v_ffn_norm_g, v_w_gate, v_w_up, v_w_down, v_final_norm_g):
    big_w = dict(w_in=w_in[0], w_uq=w_uq[0], w_ukv=w_ukv[0], w_o_swa=w_o_swa[0], w_o_mla=w_o_mla[0], w_out=w_out[0],
                 w_gate=w_gate[0], w_up=w_up[0], w_down=w_down[0])
    big_w3 = dict(w_in=w_in, w_uq=w_uq, w_ukv=w_ukv, w_o_swa=w_o_swa, w_o_mla=w_o_mla, w_out=w_out, w_gate=w_gate, w_up=w_up,
                  w_down=w_down)
    big_m = dict(w_in=m_w_in, w_uq=m_w_uq, w_ukv=m_w_ukv, w_o_swa=m_w_o_swa, w_o_mla=m_w_o_mla, w_out=m_w_out,
                 w_gate=m_w_gate, w_up=m_w_up, w_down=m_w_down)
    big_v = dict(w_in=v_w_in, w_uq=v_w_uq, w_ukv=v_w_ukv, w_o_swa=v_w_o_swa, w_o_mla=v_w_o_mla, w_out=v_w_out,
                 w_gate=v_w_gate, w_up=v_w_up, w_down=v_w_down)
    small_w = dict(mix_norm_g=mix_norm_g, ffn_norm_g=ffn_norm_g, final_norm_g=final_norm_g.reshape(1, D_MODEL),
                   q_norm_g=q_norm_g, kv_norm_g=kv_norm_g, swa_sinks=swa_sinks)
    small_m = dict(mix_norm_g=m_mix_norm_g, ffn_norm_g=m_ffn_norm_g, final_norm_g=m_final_norm_g.reshape(1, D_MODEL),
                   q_norm_g=m_q_norm_g, kv_norm_g=m_kv_norm_g, swa_sinks=m_swa_sinks)
    small_v = dict(mix_norm_g=v_mix_norm_g, ffn_norm_g=v_ffn_norm_g, final_norm_g=v_final_norm_g.reshape(1, D_MODEL),
                   q_norm_g=v_q_norm_g, kv_norm_g=v_kv_norm_g, swa_sinks=v_swa_sinks)

    px, py, pc = _position()
    me = 4 * px + 2 * py + pc
    idx = jnp.stack([me] + [4 * qx + 2 * qy + qc for qx, qy, qc in (_peer(px, py, pc, k) for k in range(1, N_DEV))])
    idx = idx.astype(jnp.int32)

    dev = lambda d: (d,)
    pack = _wire_pack(big_w, WIRE_DTYPE)
    win_g, = _all_gather(pack, ((0, dev, 0, W_IN_ROWS),), ((N_DEV, W_IN_ROWS, PACK_W),), name="ag_early")
    ag_mid = _gather_start(pack, W_IN_ROWS, ((0, dev, OUT_ROWS), (1, dev, SMALL_ROWS)),
                           ((N_DEV, OUT_ROWS, PACK_W), (N_DEV, SMALL_ROWS, PACK_W)), name="ag_mid_start")
    ag = {}

    def own_rows(r0, r1, shape):
        return pack[r0:r1].reshape(shape)

    def mid_weights(after):
        pack_mid, (wout_g, small_g) = _gather_wait(ag_mid, W_IN_ROWS, MID_ROWS, after, name="ag_mid_wait")
        ag["late"] = _gather_start(pack_mid, EARLY_ROWS, ((0, _gate_slab, FF_COLS), (0, _up_slab, FF_COLS), (1, dev, FF_COLS)),
                                   (GU_SHAPE, D_SHAPE), name="ag_late_start")
        wout_g = lax.dynamic_update_slice(wout_g, own_rows(W_IN_ROWS, SMALL_ROW0, (1, OUT_ROWS, PACK_W)), (me, 0, 0))
        small_g = lax.dynamic_update_slice(small_g, own_rows(SMALL_ROW0, EARLY_ROWS, (1, SMALL_ROWS, PACK_W)), (me, 0, 0))
        ops = _mid_operands(wout_g, small_g)
        ops["wuq"] = ops["wuq"] + ag["late"][-1][0:1, 0:1].astype(ops["wuq"].dtype)
        return ops

    def late_weights(after):
        _, (gu, d) = _gather_wait(ag["late"], EARLY_ROWS, LATE_ROWS, after, name="ag_late_wait")
        slab = (1, 1, 1, FF_COLS, PACK_W)
        gu = lax.dynamic_update_slice(gu, own_rows(EARLY_ROWS, EARLY_ROWS + FF_COLS, slab), _gate_slab(me) + (0, 0))
        gu = lax.dynamic_update_slice(gu, own_rows(EARLY_ROWS + FF_COLS, EARLY_ROWS + 2 * FF_COLS, slab), _up_slab(me) + (0, 0))
        d = lax.dynamic_update_slice(d, own_rows(EARLY_ROWS + 2 * FF_COLS, PACK_ROWS, (1, FF_COLS, PACK_W)), (me, 0, 0))
        return gu.reshape(2 * D_FF, D_MODEL), d.reshape(D_FF, D_MODEL)

    rs = {}

    def late_grads(g_gu, g_d):
        rs["late"] = _scatter_start([g_gu.reshape(GU_SHAPE), g_d.reshape(D_SHAPE)],
                                    ((0, _gate_slab, FF_COLS), (0, _up_slab, FF_COLS), (1, dev, FF_COLS)),
                                    name="rs_late_start")
        return rs["late"][-1]

    def mid_grads(g):
        rs["mid"] = _scatter_start([_mid_grad_pack(g)], ((0, dev, MID_ROWS),), name="rs_mid_start")
        return rs["mid"][-1]

    def last_grads(g_win_t):
        rs["last"] = _scatter_start([_w_in_grad_chunks(g_win_t)], ((0, dev, W_IN_ROWS),), name="rs_last_start")
        return rs["last"][-1]

    first_w = dict(small_w, mix_norm_g=mix_norm_g + ag_mid[-1][0:1, 0:1])
    loss_tot, gx, g_small = _local_step(
        x[0], loss_target[0], _w_in_operand(win_g), first_w, types.SimpleNamespace(mid=mid_weights, late=late_weights),
        types.SimpleNamespace(late=late_grads, mid=mid_grads, last=last_grads))

    g_gu, g_d, land_late = _scatter_wait(rs["late"], gx, name="rs_late_wait")
    g_mid, land_mid = _scatter_wait(rs["mid"], gx, name="rs_mid_wait")
    g_win, land_last = _scatter_wait(rs["last"], gx, name="rs_last_wait")
    gw = dict(w_gate=_peer_sum(g_gu, _gate_slab, land_late, 0, FF_COLS, idx, name="rs_sum_gate").T,
              w_up=_peer_sum(g_gu, _up_slab, land_late, 1, FF_COLS, idx, name="rs_sum_up").T,
              w_down=_peer_sum(g_d, dev, land_late, 2, FF_COLS, idx, name="rs_sum_down"),
              w_in=_peer_sum(g_win, dev, land_last, 0, W_IN_ROWS, idx, name="rs_sum_in")[0:W_IN_COLS].T)
    gw.update(_mid_unpack(_peer_sum(g_mid, dev, land_mid, 0, MID_ROWS, idx, name="rs_sum_mid")))
    dw, mw, vw = {}, {}, {}
    for n in BIG:
        dw[n], mw[n], vw[n] = _adamw_call(big_w3[n], gw[n], big_m[n], big_v[n], name="adamw_" + n)
    gw = {n: g[None] for n, g in gw.items()}

    loss_rows = jnp.pad(loss_tot[0:1, 0:1], ((0, SUBLANES - 1), (0, PACK_W - 1)))
    small_rows = jnp.concatenate([_small_pack(g_small_rows(g_small), SUBLANES), loss_rows], 0)
    parts, = _all_gather(small_rows, ((0, lambda d: (d,), 0, small_rows.shape[0]),), ((N_DEV,) + small_rows.shape,),
                         name="ag_small")
    gs, ds, ms, vs = _small_adamw(parts, _small_pack(small_w, 1), _small_pack(small_m, 1), _small_pack(small_v, 1))
    loss = gs[len(SMALL), 0]

    def small_out(packed):
        out = {}
        for k, n in enumerate(SMALL):
            out[n] = packed[k:k + 1, :SMALL_W[n]]
        out["final_norm_g"] = out["final_norm_g"].reshape(D_MODEL)
        return out

    gs, ds, ms, vs = small_out(gs), small_out(ds), small_out(ms), small_out(vs)

    order = ("mix_norm_g", "w_in", "swa_sinks", "q_norm_g", "w_uq", "kv_norm_g", "w_ukv", "w_o_swa", "w_o_mla", "w_out",
             "ffn_norm_g", "w_gate", "w_up", "w_down", "final_norm_g")

    def leaves(big, small):
        return [big[n] if n in big else small[n] for n in order]

    return (loss, gx[None], *leaves(gw, gs), *leaves(dw, ds), *leaves(mw, ms), *leaves(vw, vs))


def g_small_rows(g_small):
    out = dict(g_small)
    out["swa_sinks"] = jnp.pad(g_small["swa_sinks"], ((0, SUBLANES - 1), (0, 0)))
    return out
```

```python
import types

import numpy as np
import jax
import jax.numpy as jnp
from jax import lax
from jax.experimental import pallas as pl
from jax.experimental.pallas import tpu as pltpu

F32 = jnp.float32
MXU_DTYPE = jnp.bfloat16
WIRE_DTYPE = jnp.bfloat16

D_MODEL = 1024
EPS = 1e-6
ROPE_THETA = 10000.0
BLOCK = 128
HEAD_DIM = 64
SWA_HEADS = 8
SWA_KV_HEADS = 2
SWA_GROUP = SWA_HEADS // SWA_KV_HEADS
MLA_HEADS = 8
MLA_NOPE = 64
MLA_ROPE = 32
MLA_V = 64
MLA_QK = MLA_NOPE + MLA_ROPE
Q_LORA = 384
KV_LORA = 256
D_FF = 2816
IN_SIZES = (512, 128, 128, Q_LORA, KV_LORA, MLA_ROPE, D_MODEL, D_MODEL)
IN_OFF = tuple(int(v) for v in np.cumsum((0,) + IN_SIZES))
ADAM_LR, ADAM_B1, ADAM_B2, ADAM_EPS, ADAM_WD, ADAM_STEP = 0.001, 0.9, 0.999, 1e-08, 0.01, 10

LANES = 128
SUBLANES = 8
VMEM_LIMIT = 48 * 1024 * 1024
N_DEV = 8
AXES = ("x", "y", "c")

P_GA, P_GB, P_Q, P_QLAT, P_KR, P_K, P_V, P_KVLAT, P_W = 0, 1024, 2048, 3072, 3456, 3584, 3840, 4096, 4352
KR_LANE = 64

LOG2E = 1.4426950408889634

NT = (((1,), (1,)), ((), ()))
NN = (((1,), (0,)), ((), ()))
TN = (((0,), (0,)), ((), ()))


def _cparams(sem):
    return pltpu.CompilerParams(dimension_semantics=sem, vmem_limit_bytes=VMEM_LIMIT)


def _mm(a, b, mode, *, name, out_dtype=F32, add=None, after=None, tm=512, tn=512, tk=None):
    if mode == "nn":
        (M, K), (K2, N) = a.shape, b.shape
    elif mode == "nt":
        (M, K), (N, K2) = a.shape, b.shape
    else:
        (K, M), (K2, N) = a.shape, b.shape
    assert K == K2, (a.shape, b.shape, mode)
    tm, tn, tk = min(tm, M), min(tn, N), K if tk is None else min(tk, K)
    assert M % tm == 0 and N % tn == 0 and K % tk == 0, (M, N, K, tm, tn, tk)
    nk = K // tk
    dn = {"nn": NN, "nt": NT, "tn": TN}[mode]
    if mode == "tn":
        a_spec = pl.BlockSpec((tk, tm), lambda i, j, k: (k, i))
    else:
        a_spec = pl.BlockSpec((tm, tk), lambda i, j, k: (i, k))
    once = dict(pipeline_mode=pl.Buffered(1)) if (nk == 1 and tn == N) else {}
    if mode == "nt":
        b_spec = pl.BlockSpec((tn, tk), lambda i, j, k: (j, k), **once)
    else:
        b_spec = pl.BlockSpec((tk, tn), lambda i, j, k: (k, j), **once)
    o_spec = pl.BlockSpec((tm, tn), lambda i, j, k: (i, j))
    has_add, has_after = add is not None, after is not None

    def body(*refs):
        a_ref, b_ref = refs[0], refs[1]
        add_ref = refs[2] if has_add else None
        o_ref = refs[2 + has_add + has_after]
        p = lax.dot_general(a_ref[...], b_ref[...], dn, preferred_element_type=F32)

        def finish(acc):
            if has_add:
                acc = acc + add_ref[...]
            o_ref[...] = acc.astype(o_ref.dtype)

        if nk == 1:
            finish(p)
        else:
            acc_ref = refs[-1]
            k = pl.program_id(2)

            @pl.when(k == 0)
            def _():
                acc_ref[...] = p

            @pl.when((k > 0) & (k < nk - 1))
            def _():
                acc_ref[...] += p

            @pl.when(k == nk - 1)
            def _():
                finish(acc_ref[...] + p)

    ins = [a, b] + ([add] if has_add else []) + ([after] if has_after else [])
    in_specs = [a_spec, b_spec] + ([o_spec] if has_add else []) + ([pl.BlockSpec(memory_space=pl.ANY)] if has_after else [])
    return pl.pallas_call(
        body, name=name, grid=(M // tm, N // tn, nk), in_specs=in_specs, out_specs=o_spec,
        out_shape=jax.ShapeDtypeStruct((M, N), out_dtype),
        scratch_shapes=[pltpu.VMEM((tm, tn), F32)] if nk > 1 else [],
        compiler_params=_cparams(("parallel", "parallel", "arbitrary")),
    )(*ins)


def _rows(ts, w, cb=0):
    return pl.BlockSpec((ts, w), lambda i: (i, cb))


def _const(r, w):
    return pl.BlockSpec((r, w), lambda i: (0, 0))


def _sublane_sum(v):
    ts, c = v.shape
    return jnp.sum(v.reshape(ts // SUBLANES, SUBLANES, c), axis=0)


def _sigmoid(v):
    return 1.0 / (1.0 + jnp.exp(-v))


def _rope(v, cos, s_up, s_dn, up, dn):
    return v * cos + pltpu.roll(v, up, 1) * s_up + pltpu.roll(v, dn, 1) * s_dn


def _rope_t(dv, cos, s_up, s_dn, up, dn):
    return dv * cos + pltpu.roll(dv * s_up, dn, 1) + pltpu.roll(dv * s_dn, up, 1)


def _rope_tables(seq):
    pos = np.arange(seq, dtype=np.float32)[:, None]

    def base(dim):
        inv = np.float32(ROPE_THETA) ** (-np.arange(0, dim, 2, dtype=np.float32) / np.float32(dim))
        ang = (pos * inv.astype(np.float32)[None, :]).astype(np.float32)
        return np.cos(ang).astype(np.float32), np.sin(ang).astype(np.float32)

    z = lambda n: np.zeros((seq, n), np.float32)
    ca, sa = base(HEAD_DIM)
    a_cos = np.concatenate([ca, ca, z(64)], 1)
    a_up = np.concatenate([-sa, z(96)], 1)
    a_dn = np.concatenate([z(32), sa, z(64)], 1)
    cb, sb = base(MLA_ROPE)
    one = np.ones((seq, 64), np.float32)
    q_cos = np.concatenate([one, cb, cb, z(32)], 1)
    k_cos = np.concatenate([z(64), cb, cb, z(32)], 1)
    b_up = np.concatenate([z(64), -sb, z(48)], 1)
    b_dn = np.concatenate([z(80), sb, z(32)], 1)
    return tuple(jnp.asarray(t) for t in (a_cos, a_up, a_dn, q_cos, k_cos, b_up, b_dn))


def _rms(v, g):
    return v * lax.rsqrt(jnp.mean(v * v, axis=-1, keepdims=True) + EPS) * g


def _rms_bwd(v, g, d):
    r = lax.rsqrt(jnp.mean(v * v, axis=-1, keepdims=True) + EPS)
    xh = v * r
    dxh = d * g
    return r * (dxh - xh * jnp.mean(dxh * xh, axis=-1, keepdims=True)), d * xh


def _norm_mm(x, g, w_t, *, name, tn, tm=512):
    s_, c = x.shape
    n = w_t.shape[0]

    def body(x_ref, g_ref, w_ref, h_ref, o_ref):
        h = _rms(x_ref[...], g_ref[...]).astype(h_ref.dtype)
        h_ref[...] = h
        o_ref[...] = lax.dot_general(h, w_ref[...], NT, preferred_element_type=F32)

    return pl.pallas_call(
        body, name=name, grid=(s_ // tm, n // tn),
        in_specs=[pl.BlockSpec((tm, c), lambda i, j: (i, 0)), pl.BlockSpec((1, c), lambda i, j: (0, 0)),
                  pl.BlockSpec((tn, c), lambda i, j: (j, 0))],
        out_specs=[pl.BlockSpec((tm, c), lambda i, j: (i, 0)), pl.BlockSpec((tm, tn), lambda i, j: (i, j))],
        out_shape=[jax.ShapeDtypeStruct((s_, c), MXU_DTYPE), jax.ShapeDtypeStruct((s_, n), F32)],
        compiler_params=_cparams(("parallel", "arbitrary")),
    )(x, g, w_t)


def _mm_norm_bwd(a, b, x, g, res, *, name, after=None, tm=512):
    s_, kk = a.shape
    c = b.shape[1]
    has_after = after is not None

    def body(*refs):
        a_ref, b_ref, x_ref, g_ref, res_ref = refs[:5]
        dx_ref, dxb_ref, dg_ref = refs[5 + has_after:]
        d = jnp.dot(a_ref[...], b_ref[...], preferred_element_type=F32)
        dx, gg = _rms_bwd(x_ref[...], g_ref[...], d)
        dx = dx + res_ref[...]
        dx_ref[...] = dx
        dxb_ref[...] = dx.astype(dxb_ref.dtype)

        @pl.when(pl.program_id(0) == 0)
        def _():
            dg_ref[...] = jnp.zeros(dg_ref.shape, F32)

        dg_ref[...] += _sublane_sum(gg)

    row = _rows(tm, c)
    in_specs = [_rows(tm, kk), pl.BlockSpec((kk, c), lambda i: (0, 0), pipeline_mode=pl.Buffered(1)), row, _const(1, c), row]
    return pl.pallas_call(
        body, name=name, grid=(s_ // tm,), in_specs=in_specs + ([pl.BlockSpec(memory_space=pl.ANY)] if has_after else []),
        out_specs=[row, row, _const(SUBLANES, c)],
        out_shape=[jax.ShapeDtypeStruct((s_, c), F32), jax.ShapeDtypeStruct((s_, c), MXU_DTYPE),
                   jax.ShapeDtypeStruct((SUBLANES, c), F32)],
        compiler_params=_cparams(("arbitrary",)),
    )(*([a, b, x, g, res] + ([after] if has_after else [])))


def _norm_bwd(x, g, dy, res, *, name, ts=256, x_cb=0, x_src_w=None):
    s_ = x.shape[0]
    c = dy.shape[1]
    has_res = res is not None

    def body(*refs):
        x_ref, g_ref, dy_ref = refs[0], refs[1], refs[2]
        res_ref = refs[3] if has_res else None
        dx_ref, dxb_ref, dg_ref = refs[-3], refs[-2], refs[-1]
        v = x_ref[...]
        r = lax.rsqrt(jnp.mean(v * v, axis=-1, keepdims=True) + EPS)
        xh = v * r
        d = dy_ref[...]
        dxh = d * g_ref[...]
        dx = r * (dxh - xh * jnp.mean(dxh * xh, axis=-1, keepdims=True))
        if has_res:
            dx = dx + res_ref[...]
        dx_ref[...] = dx
        dxb_ref[...] = dx.astype(dxb_ref.dtype)

        @pl.when(pl.program_id(0) == 0)
        def _():
            dg_ref[...] = jnp.zeros(dg_ref.shape, F32)

        dg_ref[...] += _sublane_sum(d * xh)

    ins = [x, g, dy] + ([res] if has_res else [])
    in_specs = [_rows(ts, c, x_cb), _const(1, c), _rows(ts, c)] + ([_rows(ts, c)] if has_res else [])
    return pl.pallas_call(
        body, name=name, grid=(s_ // ts,), in_specs=in_specs,
        out_specs=[_rows(ts, c), _rows(ts, c), _const(SUBLANES, c)],
        out_shape=[jax.ShapeDtypeStruct((s_, c), F32), jax.ShapeDtypeStruct((s_, c), MXU_DTYPE),
                   jax.ShapeDtypeStruct((SUBLANES, c), F32)],
        compiler_params=_cparams(("arbitrary",)),
    )(*ins)


def _attn_prep(p, gq, gkv, tabs, *, ts=256):
    s_ = p.shape[0]
    a_cos, a_up, a_dn, _, k_cos, b_up, b_dn = tabs

    def body(q_ref, k_ref, v_ref, ql_ref, kvl_ref, kr_ref, gq_ref, gkv_ref, ac, au, ad, kc, bu, bd,
             qa_ref, ka_ref, va_ref, cq_ref, ckv_ref, kro_ref):
        c_, u_, d_ = ac[...], au[...], ad[...]
        for h in range(SWA_HEADS):
            sl = slice(h * LANES, (h + 1) * LANES)
            qa_ref[:, sl] = _rope(q_ref[:, sl], c_, u_, d_, 96, 32).astype(qa_ref.dtype)
        for h in range(SWA_KV_HEADS):
            sl = slice(h * LANES, (h + 1) * LANES)
            ka_ref[:, sl] = _rope(k_ref[:, sl], c_, u_, d_, 96, 32).astype(ka_ref.dtype)
        va_ref[...] = v_ref[...].astype(va_ref.dtype)
        for src, gref, dst in ((ql_ref, gq_ref, cq_ref), (kvl_ref, gkv_ref, ckv_ref)):
            v = src[...]
            r = lax.rsqrt(jnp.mean(v * v, axis=-1, keepdims=True) + EPS)
            dst[...] = (v * r * gref[...]).astype(dst.dtype)
        kro_ref[...] = _rope(kr_ref[...], kc[...], bu[...], bd[...], 112, 16)

    tab = _rows(ts, LANES)
    return pl.pallas_call(
        body, name="attn_prep", grid=(s_ // ts,),
        in_specs=[_rows(ts, 1024, P_Q // 1024), _rows(ts, 256, P_K // 256), _rows(ts, 256, P_V // 256),
                  _rows(ts, Q_LORA, P_QLAT // Q_LORA), _rows(ts, KV_LORA, P_KVLAT // KV_LORA),
                  _rows(ts, LANES, P_KR // LANES), _const(1, Q_LORA), _const(1, KV_LORA), tab, tab, tab, tab, tab, tab],
        out_specs=[_rows(ts, 1024), _rows(ts, 256), _rows(ts, 256), _rows(ts, Q_LORA), _rows(ts, KV_LORA),
                   _rows(ts, LANES)],
        out_shape=[jax.ShapeDtypeStruct((s_, 1024), MXU_DTYPE), jax.ShapeDtypeStruct((s_, 256), MXU_DTYPE),
                   jax.ShapeDtypeStruct((s_, 256), MXU_DTYPE), jax.ShapeDtypeStruct((s_, Q_LORA), MXU_DTYPE),
                   jax.ShapeDtypeStruct((s_, KV_LORA), MXU_DTYPE), jax.ShapeDtypeStruct((s_, LANES), F32)],
        compiler_params=_cparams(("parallel",)),
    )(p, p, p, p, p, p, gq, gkv, a_cos, a_up, a_dn, k_cos, b_up, b_dn)


def _mla_prep(qp, kp, kro, tabs, *, ts=256):
    s_ = qp.shape[0]
    _, _, _, q_cos, _, b_up, b_dn = tabs

    def body(q_ref, k_ref, kr_ref, qc, bu, bd, qo_ref, ko_ref):
        c_, u_, d_ = qc[...], bu[...], bd[...]
        kr = kr_ref[...]
        for h in range(MLA_HEADS):
            sl = slice(h * LANES, (h + 1) * LANES)
            qo_ref[:, sl] = _rope(q_ref[:, sl], c_, u_, d_, 112, 16).astype(qo_ref.dtype)
            ko_ref[:, sl] = (k_ref[:, sl] + kr).astype(ko_ref.dtype)

    tab = _rows(ts, LANES)
    return pl.pallas_call(
        body, name="mla_prep", grid=(s_ // ts,),
        in_specs=[_rows(ts, 1024), _rows(ts, 1024), tab, tab, tab, tab],
        out_specs=[_rows(ts, 1024), _rows(ts, 1024)],
        out_shape=[jax.ShapeDtypeStruct((s_, 1024), MXU_DTYPE)] * 2,
        compiler_params=_cparams(("parallel",)),
    )(qp, kp, kro, q_cos, b_up, b_dn)


def _mla_unprep(dqc, dkc, dvp, tabs, *, ts=256):
    s_ = dqc.shape[0]
    _, _, _, q_cos, k_cos, b_up, b_dn = tabs

    def body(dq_ref, dk_ref, dv_ref, qc, kc, bu, bd, dqo_ref, dkvo_ref, dkr_ref):
        c_, u_, d_ = qc[...], bu[...], bd[...]
        tot = jnp.zeros((ts, LANES), F32)
        for h in range(MLA_HEADS):
            sl = slice(h * LANES, (h + 1) * LANES)
            dqo_ref[:, sl] = _rope_t(dq_ref[:, sl], c_, u_, d_, 112, 16).astype(dqo_ref.dtype)
            dk = dk_ref[:, sl]
            dkvo_ref[:, sl] = dk.astype(dkvo_ref.dtype)
            tot = tot + dk
        dkvo_ref[:, 1024:2048] = dv_ref[...].astype(dkvo_ref.dtype)
        dkr_ref[...] = _rope_t(tot, kc[...], u_, d_, 112, 16).astype(dkr_ref.dtype)

    tab = _rows(ts, LANES)
    return pl.pallas_call(
        body, name="mla_unprep", grid=(s_ // ts,),
        in_specs=[_rows(ts, 1024), _rows(ts, 1024), _rows(ts, 1024), tab, tab, tab, tab],
        out_specs=[_rows(ts, 1024), _rows(ts, 2048), _rows(ts, LANES)],
        out_shape=[jax.ShapeDtypeStruct((s_, 1024), MXU_DTYPE), jax.ShapeDtypeStruct((s_, 2048), MXU_DTYPE),
                   jax.ShapeDtypeStruct((s_, LANES), MXU_DTYPE)],
        compiler_params=_cparams(("parallel",)),
    )(dqc, dkc, dvp, q_cos, k_cos, b_up, b_dn)


def _assemble_dp(dgab, dqa, dqlat, dkr, dka, dva, dkvlat, tabs, *, ts=256):
    s_ = dqa.shape[0]
    a_cos, a_up, a_dn = tabs[0], tabs[1], tabs[2]

    def body(dg_ref, dq_ref, dql_ref, dkr_ref, dk_ref, dv_ref, dkvl_ref, ac, au, ad, o_ref):
        c_, u_, d_ = ac[...], au[...], ad[...]
        o_ref[:, P_GA:P_Q] = dg_ref[...]
        for h in range(SWA_HEADS):
            sl = slice(h * LANES, (h + 1) * LANES)
            o_ref[:, P_Q + h * LANES:P_Q + (h + 1) * LANES] = _rope_t(dq_ref[:, sl], c_, u_, d_, 96, 32).astype(o_ref.dtype)
        o_ref[:, P_QLAT:P_KR] = dql_ref[...]
        o_ref[:, P_KR:P_K] = dkr_ref[...]
        for h in range(SWA_KV_HEADS):
            sl = slice(h * LANES, (h + 1) * LANES)
            o_ref[:, P_K + h * LANES:P_K + (h + 1) * LANES] = _rope_t(dk_ref[:, sl], c_, u_, d_, 96, 32).astype(o_ref.dtype)
        o_ref[:, P_V:P_KVLAT] = dv_ref[...]
        o_ref[:, P_KVLAT:P_W] = dkvl_ref[...]

    tab = _rows(ts, LANES)
    return pl.pallas_call(
        body, name="assemble_dp", grid=(s_ // ts,),
        in_specs=[_rows(ts, 2048), _rows(ts, 1024), _rows(ts, Q_LORA), _rows(ts, LANES), _rows(ts, 256), _rows(ts, 256),
                  _rows(ts, KV_LORA), tab, tab, tab],
        out_specs=_rows(ts, P_W), out_shape=jax.ShapeDtypeStruct((s_, P_W), MXU_DTYPE),
        compiler_params=_cparams(("parallel",)),
    )(dgab, dqa, dqlat, dkr, dka, dva, dkvlat, a_cos, a_up, a_dn)


def _attn_out_gate(oa, ob, woa, wob, p, *, ts=512):
    s_ = p.shape[0]

    def body(oa_ref, ob_ref, wa_ref, wb_ref, ga_ref, gb_ref, ta_ref, tb_ref, y_ref):
        ta = jnp.dot(oa_ref[...], wa_ref[...], preferred_element_type=F32)
        tb = jnp.dot(ob_ref[...], wb_ref[...], preferred_element_type=F32)
        ta_ref[...] = ta
        tb_ref[...] = tb
        y_ref[...] = (_sigmoid(ga_ref[...]) * ta + _sigmoid(gb_ref[...]) * tb).astype(y_ref.dtype)

    w = _const(1024, 1024)
    return pl.pallas_call(
        body, name="attn_out_gate", grid=(s_ // ts,),
        in_specs=[_rows(ts, 1024), _rows(ts, 1024), w, w, _rows(ts, 1024, P_GA // 1024), _rows(ts, 1024, P_GB // 1024)],
        out_specs=[_rows(ts, 1024)] * 3,
        out_shape=[jax.ShapeDtypeStruct((s_, 1024), F32)] * 2 + [jax.ShapeDtypeStruct((s_, 1024), MXU_DTYPE)],
        compiler_params=_cparams(("parallel",)),
    )(oa, ob, woa, wob, p, p)


def _d_y_gate(dx1b, wout, p, ta, tb, *, ts=512):
    s_ = p.shape[0]

    def body(dx_ref, w_ref, ga_ref, gb_ref, ta_ref, tb_ref, dta_ref, dtb_ref, dg_ref):
        d = lax.dot_general(dx_ref[...], w_ref[...], NT, preferred_element_type=F32)
        sa, sb = _sigmoid(ga_ref[...]), _sigmoid(gb_ref[...])
        dta_ref[...] = (d * sa).astype(dta_ref.dtype)
        dtb_ref[...] = (d * sb).astype(dtb_ref.dtype)
        dg_ref[:, 0:1024] = (d * ta_ref[...] * (sa * (1.0 - sa))).astype(dg_ref.dtype)
        dg_ref[:, 1024:2048] = (d * tb_ref[...] * (sb * (1.0 - sb))).astype(dg_ref.dtype)

    return pl.pallas_call(
        body, name="d_y_gate", grid=(s_ // ts,),
        in_specs=[_rows(ts, 1024), _const(1024, 1024), _rows(ts, 1024, P_GA // 1024), _rows(ts, 1024, P_GB // 1024),
                  _rows(ts, 1024), _rows(ts, 1024)],
        out_specs=[_rows(ts, 1024), _rows(ts, 1024), _rows(ts, 2048)],
        out_shape=[jax.ShapeDtypeStruct((s_, 1024), MXU_DTYPE)] * 2 + [jax.ShapeDtypeStruct((s_, 2048), MXU_DTYPE)],
        compiler_params=_cparams(("parallel",)),
    )(dx1b, wout, p, p, ta, tb)


FF_TILE = D_FF // 2


def _ffn_in_act(x1, g, wgu_t, *, tm=512):
    s_ = x1.shape[0]
    n = s_ // tm

    def body(x_ref, g_ref, w_ref, h_ref, gu_ref, a_ref):
        h = _rms(x_ref[...], g_ref[...]).astype(h_ref.dtype)
        h_ref[...] = h
        p = lax.dot_general(h, w_ref[...], NT, preferred_element_type=F32)
        gu_ref[...] = p
        gate = p[:, :FF_TILE]
        a_ref[...] = (gate * _sigmoid(gate) * p[:, FF_TILE:]).astype(a_ref.dtype)

    return pl.pallas_call(
        body, name="ffn_in", grid=(2, s_ // tm),
        in_specs=[pl.BlockSpec((tm, D_MODEL), lambda j, i: (i, 0)), pl.BlockSpec((1, D_MODEL), lambda j, i: (0, 0)),
                  pl.BlockSpec((2 * FF_TILE, D_MODEL), lambda j, i: (j, 0))],
        out_specs=[pl.BlockSpec((tm, D_MODEL), lambda j, i: (i + j * (n - 1 - i), 0)),
                   pl.BlockSpec((tm, 2 * FF_TILE), lambda j, i: (i, j)),
                   pl.BlockSpec((tm, FF_TILE), lambda j, i: (i, j))],
        out_shape=[jax.ShapeDtypeStruct((s_, D_MODEL), MXU_DTYPE), jax.ShapeDtypeStruct((s_, 2 * D_FF), F32),
                   jax.ShapeDtypeStruct((s_, D_FF), MXU_DTYPE)],
        compiler_params=_cparams(("arbitrary", "arbitrary")),
    )(x1, g, wgu_t)


def _d_act_swiglu(dx2b, wd, gu, *, tm=512):
    s_ = dx2b.shape[0]

    def body(d_ref, w_ref, gu_ref, o_ref):
        da = lax.dot_general(d_ref[...], w_ref[...], NT, preferred_element_type=F32)
        g, u = gu_ref[:, :FF_TILE], gu_ref[:, FF_TILE:]
        sg = _sigmoid(g)
        o_ref[:, :FF_TILE] = (da * u * (sg * (1.0 + g * (1.0 - sg)))).astype(o_ref.dtype)
        o_ref[:, FF_TILE:] = (da * (g * sg)).astype(o_ref.dtype)

    gu_spec = pl.BlockSpec((tm, 2 * FF_TILE), lambda j, i: (i, j))
    return pl.pallas_call(
        body, name="d_act", grid=(2, s_ // tm),
        in_specs=[pl.BlockSpec((tm, D_MODEL), lambda j, i: (i, 0)), pl.BlockSpec((FF_TILE, D_MODEL), lambda j, i: (j, 0)), gu_spec],
        out_specs=gu_spec, out_shape=jax.ShapeDtypeStruct((s_, 2 * D_FF), MXU_DTYPE),
        compiler_params=_cparams(("parallel", "parallel")),
    )(dx2b, wd, gu)


def _ffn_out_loss(act, wd, x1, g, tgt, *, ts=512):
    s_, c = x1.shape
    kk = act.shape[1]

    def body(a_ref, w_ref, x_ref, g_ref, t_ref, dx_ref, dxb_ref, dg_ref, lp_ref, tot_ref):
        v = x_ref[...] + jnp.dot(a_ref[...], w_ref[...], preferred_element_type=F32)
        r = lax.rsqrt(jnp.mean(v * v, axis=-1, keepdims=True) + EPS)
        xh = v * r
        gg = g_ref[...]
        e = xh * gg - t_ref[...]
        do = e * (1.0 / c)
        dxh = do * gg
        dx = r * (dxh - xh * jnp.mean(dxh * xh, axis=-1, keepdims=True))
        dx_ref[...] = dx
        dxb_ref[...] = dx.astype(dxb_ref.dtype)
        i = pl.program_id(0)

        @pl.when(i == 0)
        def _():
            dg_ref[...] = jnp.zeros(dg_ref.shape, F32)
            lp_ref[...] = jnp.zeros(lp_ref.shape, F32)

        dg_ref[...] += _sublane_sum(do * xh)
        lp_ref[...] += _sublane_sum(e * e)
        tot_ref[...] = jnp.full(tot_ref.shape, (0.5 / c) * jnp.sum(lp_ref[...]), F32)

    return pl.pallas_call(
        body, name="ffn_out_loss", grid=(s_ // ts,),
        in_specs=[_rows(ts, kk), _const(kk, c), _rows(ts, c), _const(1, c), _rows(ts, c)],
        out_specs=[_rows(ts, c), _rows(ts, c), _const(SUBLANES, c), _const(SUBLANES, c), _const(SUBLANES, LANES)],
        out_shape=[jax.ShapeDtypeStruct((s_, c), F32), jax.ShapeDtypeStruct((s_, c), MXU_DTYPE),
                   jax.ShapeDtypeStruct((SUBLANES, c), F32), jax.ShapeDtypeStruct((SUBLANES, c), F32),
                   jax.ShapeDtypeStruct((SUBLANES, LANES), F32)],
        compiler_params=_cparams(("arbitrary",)),
    )(act, wd, x1, g, tgt)


def _mla_bwd_prep(dob, o32, *, ts=256):
    s_ = dob.shape[0]

    def body(do_ref, o_ref, dob_ref, dl_ref):
        d = do_ref[...]
        dob_ref[...] = d.astype(dob_ref.dtype)
        prod = d * o_ref[...]
        for h in range(MLA_HEADS):
            dl_ref[h] = jnp.sum(prod[:, h * LANES:(h + 1) * LANES].T, axis=0, keepdims=True)

    return pl.pallas_call(
        body, name="mla_bwd_prep", grid=(s_ // ts,), in_specs=[_rows(ts, 1024), _rows(ts, 1024)],
        out_specs=[_rows(ts, 1024), pl.BlockSpec((MLA_HEADS, 1, ts), lambda i: (0, 0, i))],
        out_shape=[jax.ShapeDtypeStruct((s_, 1024), MXU_DTYPE), jax.ShapeDtypeStruct((MLA_HEADS, 1, s_), F32)],
        compiler_params=_cparams(("parallel",)),
    )(dob, o32)


SWA_T = 4 * BLOCK


SWA_W = SWA_GROUP * BLOCK


def _swa_masks(sb):
    kr = lax.broadcasted_iota(jnp.int32, (2 * BLOCK, SWA_W), 0)
    qc = jnp.bitwise_and(lax.broadcasted_iota(jnp.int32, (2 * BLOCK, SWA_W), 1), BLOCK - 1)
    band = jnp.logical_and(kr > qc, kr <= qc + BLOCK)
    first = jnp.logical_and(band, kr >= BLOCK)
    return band, jnp.logical_or(first, jnp.logical_and(band, sb > 0))


def _heads_to_rows(ref, rs):
    return jnp.concatenate([ref[rs, h * LANES:(h + 1) * LANES] for h in range(SWA_GROUP)], axis=0)


def _sink_row(sk_ref):
    return jnp.concatenate([sk_ref[0, h:h + 1, :] for h in range(SWA_GROUP)], axis=1) * LOG2E


def _swa_in_specs(rev, nsb):
    sbi = (lambda j: nsb - 1 - j) if rev else (lambda j: j)
    cur = pl.BlockSpec((SWA_T, LANES), lambda g, j: (sbi(j), g))
    prev = pl.BlockSpec((BLOCK, LANES), lambda g, j: (jnp.maximum(4 * sbi(j) - 1, 0), g))
    q = pl.BlockSpec((SWA_T, SWA_GROUP * LANES), lambda g, j: (sbi(j), g))
    sink = pl.BlockSpec((1, SUBLANES, LANES), lambda g, j: (g, 0, 0))
    lse = pl.BlockSpec((SWA_GROUP, 1, SWA_T), lambda g, j: (g, 0, sbi(j)))
    return q, cur, prev, sink, lse


def _swa_fwd(qa, ka, va, sink_b):
    s_ = qa.shape[0]
    nsb = s_ // SWA_T
    c2 = HEAD_DIM ** -0.5 * LOG2E

    def body(q_ref, kc_ref, kp_ref, vc_ref, vp_ref, sk_ref, o32_ref, o16_ref, lse_ref, kx, vx):
        kx[0:BLOCK, :] = kp_ref[...]
        kx[BLOCK:5 * BLOCK, :] = kc_ref[...]
        vx[0:BLOCK, :] = vp_ref[...]
        vx[BLOCK:5 * BLOCK, :] = vc_ref[...]
        band, band0 = _swa_masks(pl.program_id(1))
        sink2 = _sink_row(sk_ref)
        for b in range(4):
            rs = slice(b * BLOCK, (b + 1) * BLOCK)
            ks = slice(b * BLOCK, (b + 2) * BLOCK)
            st = lax.dot_general(kx[ks, :], _heads_to_rows(q_ref, rs), NT, preferred_element_type=F32) * c2
            st = jnp.where(band0 if b == 0 else band, st, -jnp.inf)
            m = jnp.maximum(jnp.max(st, axis=0, keepdims=True), sink2)
            pt = jnp.exp2(st - m)
            den = jnp.sum(pt, axis=0, keepdims=True) + jnp.exp2(sink2 - m)
            o = lax.dot_general((pt * (1.0 / den)).astype(MXU_DTYPE), vx[ks, :], TN, preferred_element_type=F32)
            lse = m + jnp.log2(den)
            for hh in range(SWA_GROUP):
                cs = slice(hh * LANES, (hh + 1) * LANES)
                o32_ref[rs, cs] = o[cs, :]
                o16_ref[rs, cs] = o[cs, :].astype(o16_ref.dtype)
                lse_ref[hh, :, rs] = lse[:, cs]

    q, cur, prev, sink, lse_spec = _swa_in_specs(False, nsb)
    return pl.pallas_call(
        body, name="swa_fwd", grid=(SWA_KV_HEADS, nsb), in_specs=[q, cur, prev, cur, prev, sink],
        out_specs=[q, q, lse_spec],
        out_shape=[jax.ShapeDtypeStruct((s_, SWA_HEADS * LANES), F32), jax.ShapeDtypeStruct((s_, SWA_HEADS * LANES), MXU_DTYPE),
                   jax.ShapeDtypeStruct((SWA_HEADS, 1, s_), F32)],
        scratch_shapes=[pltpu.VMEM((5 * BLOCK, LANES), MXU_DTYPE), pltpu.VMEM((5 * BLOCK, LANES), MXU_DTYPE)],
        compiler_params=_cparams(("parallel", "arbitrary")),
    )(qa, ka, ka, va, va, sink_b)


def _swa_bwd(qa, ka, va, sink_b, o32, do, lse):
    s_ = qa.shape[0]
    nsb = s_ // SWA_T
    scale = HEAD_DIM ** -0.5
    c2 = scale * LOG2E

    def body(q_ref, kc_ref, kp_ref, vc_ref, vp_ref, sk_ref, o_ref, do_ref, lse_ref,
             dq_ref, dk_ref, dv_ref, dsk_ref, kx, vx, kacc, vacc, kcar, vcar):
        j = pl.program_id(1)
        kx[0:BLOCK, :] = kp_ref[...]
        kx[BLOCK:5 * BLOCK, :] = kc_ref[...]
        vx[0:BLOCK, :] = vp_ref[...]
        vx[BLOCK:5 * BLOCK, :] = vc_ref[...]
        band, band0 = _swa_masks(nsb - 1 - j)
        kacc[...] = jnp.zeros(kacc.shape, F32)
        vacc[...] = jnp.zeros(vacc.shape, F32)

        @pl.when(j == 0)
        def _():
            kcar[...] = jnp.zeros(kcar.shape, F32)
            vcar[...] = jnp.zeros(vcar.shape, F32)
            dsk_ref[...] = jnp.zeros(dsk_ref.shape, F32)

        sink2 = _sink_row(sk_ref)
        dsink = jnp.zeros((1, SWA_W), F32)
        for b in range(4):
            rs = slice(b * BLOCK, (b + 1) * BLOCK)
            ks = slice(b * BLOCK, (b + 2) * BLOCK)
            q, k2, v2 = _heads_to_rows(q_ref, rs), kx[ks, :], vx[ks, :]
            d = _heads_to_rows(do_ref, rs)
            delta = jnp.sum((d * _heads_to_rows(o_ref, rs)).T, axis=0, keepdims=True)
            l2 = jnp.concatenate([lse_ref[hh, :, rs] for hh in range(SWA_GROUP)], axis=1)
            st = lax.dot_general(k2, q, NT, preferred_element_type=F32) * c2
            pt = jnp.exp2(jnp.where(band0 if b == 0 else band, st, -jnp.inf) - l2)
            db = d.astype(MXU_DTYPE)
            dst = (pt * (lax.dot_general(v2, db, NT, preferred_element_type=F32) - delta) * scale).astype(MXU_DTYPE)
            dq = lax.dot_general(dst, k2, TN, preferred_element_type=F32)
            for hh in range(SWA_GROUP):
                dq_ref[rs, hh * LANES:(hh + 1) * LANES] = dq[hh * LANES:(hh + 1) * LANES, :]
            kacc[ks, :] += jnp.dot(dst, q, preferred_element_type=F32)
            vacc[ks, :] += jnp.dot(pt.astype(MXU_DTYPE), db, preferred_element_type=F32)
            dsink = dsink - jnp.exp2(sink2 - l2) * delta
        for hh in range(SWA_GROUP):
            tot = jnp.sum(dsink[:, hh * LANES:(hh + 1) * LANES], axis=1, keepdims=True)
            dsk_ref[0, hh:hh + 1, :] += jnp.broadcast_to(tot, (1, LANES))

        dk_ref[0:3 * BLOCK, :] = kacc[BLOCK:4 * BLOCK, :]
        dk_ref[3 * BLOCK:4 * BLOCK, :] = kacc[4 * BLOCK:5 * BLOCK, :] + kcar[...]
        dv_ref[0:3 * BLOCK, :] = vacc[BLOCK:4 * BLOCK, :].astype(dv_ref.dtype)
        dv_ref[3 * BLOCK:4 * BLOCK, :] = (vacc[4 * BLOCK:5 * BLOCK, :] + vcar[...]).astype(dv_ref.dtype)
        kcar[...] = kacc[0:BLOCK, :]
        vcar[...] = vacc[0:BLOCK, :]

    q, cur, prev, sink, lse_spec = _swa_in_specs(True, nsb)
    return pl.pallas_call(
        body, name="swa_bwd", grid=(SWA_KV_HEADS, nsb),
        in_specs=[q, cur, prev, cur, prev, sink, q, q, lse_spec],
        out_specs=[q, cur, cur, sink],
        out_shape=[jax.ShapeDtypeStruct((s_, SWA_HEADS * LANES), F32), jax.ShapeDtypeStruct((s_, SWA_KV_HEADS * LANES), F32),
                   jax.ShapeDtypeStruct((s_, SWA_KV_HEADS * LANES), MXU_DTYPE),
                   jax.ShapeDtypeStruct((SWA_KV_HEADS, SUBLANES, LANES), F32)],
        scratch_shapes=[pltpu.VMEM((5 * BLOCK, LANES), MXU_DTYPE), pltpu.VMEM((5 * BLOCK, LANES), MXU_DTYPE),
                        pltpu.VMEM((5 * BLOCK, LANES), F32), pltpu.VMEM((5 * BLOCK, LANES), F32),
                        pltpu.VMEM((BLOCK, LANES), F32), pltpu.VMEM((BLOCK, LANES), F32)],
        compiler_params=_cparams(("arbitrary", "arbitrary")),
    )(qa, ka, ka, va, va, sink_b, o32, do, lse)


MLA_T = 512
MLA_FWD_GROUP = 4
MLA_BWD_GROUP = 2


def _mla_specs(s_, t, group):
    w = group * LANES
    qs = pl.BlockSpec((t, w), lambda g, i: (i, g))
    kv = pl.BlockSpec((s_, w), lambda g, i: (0, g))
    row = pl.BlockSpec((group, 1, t), lambda g, i: (g, 0, i))
    return qs, kv, row


def _causal_scores_t(k, q, t, c2, masked):
    st = lax.dot_general(k, q, NT, preferred_element_type=F32) * c2
    if masked:
        kr = lax.broadcasted_iota(jnp.int32, (t, t), 0)
        qc = lax.broadcasted_iota(jnp.int32, (t, t), 1)
        st = jnp.where(kr <= qc, st, -jnp.inf)
    return st


def _mla_fwd(qc, kc, vp):
    s_ = qc.shape[0]
    t = min(MLA_T, s_)
    c2 = MLA_QK ** -0.5 * LOG2E
    grp = MLA_FWD_GROUP

    def body(q_ref, k_ref, v_ref, o32_ref, o16_ref, lse_ref, m_s, acc_s):
        qi = pl.program_id(1)
        m_s[...] = jnp.full(m_s.shape, -jnp.inf, F32)
        acc_s[...] = jnp.zeros(acc_s.shape, F32)
        ones_lane = lax.broadcasted_iota(jnp.int32, (t, LANES), 1) == MLA_V

        def step(ki, masked):
            off = pl.multiple_of(ki * t, t)
            for g in range(grp):
                cs = slice(g * LANES, (g + 1) * LANES)
                st = _causal_scores_t(k_ref[pl.ds(off, t), cs], q_ref[:, cs], t, c2, masked)
                m_old = m_s[g]
                m_new = jnp.maximum(m_old, jnp.max(st, axis=0, keepdims=True))
                alpha = jnp.exp2(m_old - m_new)
                pt = jnp.exp2(st - m_new).astype(MXU_DTYPE)
                v = v_ref[pl.ds(off, t), cs]
                v = jnp.where(ones_lane, jnp.ones((), v.dtype), v)
                acc_s[g] = alpha * acc_s[g] + lax.dot_general(v, pt, TN, preferred_element_type=F32)
                m_s[g] = m_new

        def full_block(ki, carry):
            step(ki, False)
            return carry

        lax.fori_loop(0, qi, full_block, 0)
        step(qi, True)
        for g in range(grp):
            cs = slice(g * LANES, (g + 1) * LANES)
            acc = acc_s[g]
            l = acc[MLA_V:MLA_V + 1, :]
            o = (acc * (1.0 / l)).T
            o32_ref[:, cs] = o
            o16_ref[:, cs] = o.astype(o16_ref.dtype)
            lse_ref[g] = m_s[g] + jnp.log2(l)

    qs, kv, row = _mla_specs(s_, t, grp)
    return pl.pallas_call(
        body, name="mla_fwd", grid=(MLA_HEADS // grp, s_ // t), in_specs=[qs, kv, kv], out_specs=[qs, qs, row],
        out_shape=[jax.ShapeDtypeStruct((s_, MLA_HEADS * LANES), F32), jax.ShapeDtypeStruct((s_, MLA_HEADS * LANES), MXU_DTYPE),
                   jax.ShapeDtypeStruct((MLA_HEADS, 1, s_), F32)],
        scratch_shapes=[pltpu.VMEM((grp, 1, t), F32), pltpu.VMEM((grp, LANES, t), F32)],
        compiler_params=_cparams(("parallel", "arbitrary")),
    )(qc, kc, vp)


def _mla_bwd(qc, kc, vp, dob, lse, delta):
    s_ = qc.shape[0]
    t = min(MLA_T, s_)
    scale = MLA_QK ** -0.5
    c2 = scale * LOG2E
    grp = MLA_BWD_GROUP

    def body(q_ref, do_ref, lse_ref, dl_ref, k_ref, v_ref, dq_ref, dk_ref, dv_ref, dqt_s):
        qi = pl.program_id(1)

        @pl.when(qi == 0)
        def _():
            dk_ref[...] = jnp.zeros(dk_ref.shape, F32)
            dv_ref[...] = jnp.zeros(dv_ref.shape, F32)

        dqt_s[...] = jnp.zeros(dqt_s.shape, F32)

        def step(ki, masked):
            off = pl.multiple_of(ki * t, t)
            for g in range(grp):
                cs = slice(g * LANES, (g + 1) * LANES)
                q, d, k = q_ref[:, cs], do_ref[:, cs], k_ref[pl.ds(off, t), cs]
                pt = jnp.exp2(_causal_scores_t(k, q, t, c2, masked) - lse_ref[g])
                dpt = lax.dot_general(v_ref[pl.ds(off, t), cs], d, NT, preferred_element_type=F32)
                dst = (pt * (dpt - dl_ref[g]) * scale).astype(MXU_DTYPE)
                dv_ref[pl.ds(off, t), cs] += jnp.dot(pt.astype(MXU_DTYPE), d, preferred_element_type=F32)
                dk_ref[pl.ds(off, t), cs] += jnp.dot(dst, q, preferred_element_type=F32)
                dqt_s[g] += lax.dot_general(k, dst, TN, preferred_element_type=F32)

        def full_block(ki, carry):
            step(ki, False)
            return carry

        lax.fori_loop(0, qi, full_block, 0)
        step(qi, True)
        for g in range(grp):
            dq_ref[:, g * LANES:(g + 1) * LANES] = dqt_s[g].T

    qs, kv, row = _mla_specs(s_, t, grp)
    shp = jax.ShapeDtypeStruct((s_, MLA_HEADS * LANES), F32)
    return pl.pallas_call(
        body, name="mla_bwd", grid=(MLA_HEADS // grp, s_ // t), in_specs=[qs, qs, row, row, kv, kv],
        out_specs=[qs, kv, kv], out_shape=[shp, shp, shp], scratch_shapes=[pltpu.VMEM((grp, LANES, t), F32)],
        compiler_params=_cparams(("parallel", "arbitrary")),
    )(qc, dob, lse, delta, kc, vp)


def _pad_heads(w, nh, hd, axis):
    shp = w.shape
    w = w.reshape(shp[:axis] + (nh, hd) + shp[axis + 1:])
    pad = [(0, 0)] * w.ndim
    pad[axis + 1] = (0, LANES - hd)
    w = jnp.pad(w, pad)
    return w.reshape(shp[:axis] + (nh * LANES,) + shp[axis + 1:])


def _unpad_heads(w, nh, hd, axis):
    shp = w.shape
    w = w.reshape(shp[:axis] + (nh, LANES) + shp[axis + 1:])
    w = lax.slice_in_dim(w, 0, hd, axis=axis + 1)
    return w.reshape(shp[:axis] + (nh * hd,) + shp[axis + 1:])


PACK_W = 1024
ROW_TILE = 16
FULL_SHAPE = dict(w_in=(1024, 3488), w_uq=(384, 768), w_ukv=(256, 1024), w_o_swa=(512, 1024), w_o_mla=(512, 1024),
                  w_out=(1024, 1024), w_gate=(1024, 2816), w_up=(1024, 2816), w_down=(2816, 1024))
BIG = tuple(FULL_SHAPE)
ROW_SHARDED = ("w_out", "w_down")
W_IN_COLS = FULL_SHAPE["w_in"][1] // N_DEV
W_IN_ROWS = -(-W_IN_COLS // ROW_TILE) * ROW_TILE
FF_COLS = D_FF // N_DEV
OUT_ROWS = D_MODEL // N_DEV
SMALL_ROW0 = W_IN_ROWS + OUT_ROWS
SMALL_FLAT = (("w_uq", 0, 36), ("w_ukv", 48, 32), ("w_o_swa", 80, 64), ("w_o_mla", 144, 64))
SMALL_ROWS = 208
EARLY_ROWS = SMALL_ROW0 + SMALL_ROWS
LATE_ROWS = 3 * FF_COLS
PACK_ROWS = EARLY_ROWS + LATE_ROWS


def _shard_shape(n):
    r, c = FULL_SHAPE[n]
    return (r // N_DEV, c) if n in ROW_SHARDED else (r, c // N_DEV)


def _wire_pack(sh, dtype):
    c = lambda n: sh[n].astype(dtype)
    rows = [jnp.pad(c("w_in").T, ((0, W_IN_ROWS - W_IN_COLS), (0, 0))), c("w_out")]
    for n, _, r in SMALL_FLAT:
        rows.append(jnp.pad(c(n).reshape(r, PACK_W), ((0, -r % ROW_TILE), (0, 0))))
    return jnp.concatenate(rows + [c("w_gate").T, c("w_up").T, c("w_down")], 0)


MID_ROWS = OUT_ROWS + SMALL_ROWS


def _mid_unpack(p):
    out = dict(w_out=p[0:OUT_ROWS])
    for n, off, r in SMALL_FLAT:
        out[n] = p[OUT_ROWS + off:OUT_ROWS + off + r].reshape(_shard_shape(n))
    return out


def _w_in_row_maps():
    sp = lambda col: (col // W_IN_COLS) * W_IN_ROWS + col % W_IN_COLS
    fwd = np.full((P_W,), -1, np.int64)

    def put(t0, c0, n):
        fwd[t0:t0 + n] = [sp(c) for c in range(c0, c0 + n)]

    put(P_GA, IN_OFF[6], D_MODEL)
    put(P_GB, IN_OFF[7], D_MODEL)
    for h in range(SWA_HEADS):
        put(P_Q + LANES * h, IN_OFF[0] + HEAD_DIM * h, HEAD_DIM)
    put(P_QLAT, IN_OFF[3], Q_LORA)
    put(P_KR + KR_LANE, IN_OFF[5], MLA_ROPE)
    for h in range(SWA_KV_HEADS):
        put(P_K + LANES * h, IN_OFF[1] + HEAD_DIM * h, HEAD_DIM)
        put(P_V + LANES * h, IN_OFF[2] + HEAD_DIM * h, HEAD_DIM)
    put(P_KVLAT, IN_OFF[4], KV_LORA)
    inv = np.full((N_DEV * W_IN_ROWS,), -1, np.int64)
    inv[fwd[fwd >= 0]] = np.nonzero(fwd >= 0)[0]
    return fwd, inv


def _take_rows(src, idx, *, name):
    n_out, n_src, width = len(idx), src.shape[0], src.shape[1]
    assert n_out % BLOCK == 0 and n_src % BLOCK == 0
    n_tiles = n_out // BLOCK
    blocks = [sorted({int(v) // BLOCK for v in idx[i * BLOCK:(i + 1) * BLOCK] if v >= 0}) for i in range(n_tiles)]
    k_max = max(1, max(len(b) for b in blocks))
    tab = np.zeros((n_tiles, k_max), np.int32)
    sel = np.zeros((n_tiles, k_max, BLOCK, BLOCK), np.float32)
    for i, blks in enumerate(blocks):
        for m, b in enumerate(blks):
            tab[i, m] = b
            for r in range(BLOCK):
                v = int(idx[i * BLOCK + r])
                if v >= 0 and v // BLOCK == b:
                    sel[i, m, r, v % BLOCK] = 1.0

    def body(tab_ref, sel_ref, *refs):
        o_ref = refs[k_max]
        acc = jnp.dot(sel_ref[0, 0], refs[0][...], preferred_element_type=F32)
        for m in range(1, k_max):
            acc = acc + jnp.dot(sel_ref[0, m], refs[m][...], preferred_element_type=F32)
        o_ref[...] = acc.astype(o_ref.dtype)

    def src_spec(m):
        return pl.BlockSpec((BLOCK, width), lambda i, t: (t[i * k_max + m], 0))

    return pl.pallas_call(
        body, name=name,
        grid_spec=pltpu.PrefetchScalarGridSpec(
            num_scalar_prefetch=1, grid=(n_tiles,),
            in_specs=[pl.BlockSpec((1, k_max, BLOCK, BLOCK), lambda i, t: (i, 0, 0, 0))] + [src_spec(m) for m in range(k_max)],
            out_specs=pl.BlockSpec((BLOCK, width), lambda i, t: (i, 0))),
        out_shape=jax.ShapeDtypeStruct((n_out, width), src.dtype),
        compiler_params=_cparams(("parallel",)),
    )(jnp.asarray(tab.reshape(-1)), jnp.asarray(sel, src.dtype), *([src] * k_max))


def _w_in_operand(win_g):
    return _take_rows(win_g.reshape(N_DEV * W_IN_ROWS, PACK_W), _w_in_row_maps()[0], name="w_in_rows")


def _mid_operands(wout_g, small_g):
    def full(n, off, r):
        a = small_g[:, off:off + r].reshape((N_DEV,) + _shard_shape(n))
        return jnp.moveaxis(a, 0, 1).reshape(FULL_SHAPE[n])

    w = {n: full(n, off, r) for n, off, r in SMALL_FLAT}
    ukv = w["w_ukv"].reshape(KV_LORA, MLA_HEADS, MLA_NOPE + MLA_V)
    return dict(
        wout=wout_g.reshape(D_MODEL, D_MODEL),
        wuq=_pad_heads(w["w_uq"], MLA_HEADS, MLA_QK, 1),
        wuk=_pad_heads(ukv[:, :, :MLA_NOPE].reshape(KV_LORA, -1), MLA_HEADS, MLA_NOPE, 1),
        wuv=_pad_heads(ukv[:, :, MLA_NOPE:].reshape(KV_LORA, -1), MLA_HEADS, MLA_V, 1),
        woa=_pad_heads(w["w_o_swa"], SWA_HEADS, HEAD_DIM, 0),
        wob=_pad_heads(w["w_o_mla"], MLA_HEADS, MLA_V, 0),
    )


def _mid_grad_pack(g):
    uk = _unpad_heads(g["wukv"][:, :1024], MLA_HEADS, MLA_NOPE, 1).reshape(KV_LORA, MLA_HEADS, MLA_NOPE)
    uv = _unpad_heads(g["wukv"][:, 1024:], MLA_HEADS, MLA_V, 1).reshape(KV_LORA, MLA_HEADS, MLA_V)
    w = dict(w_uq=_unpad_heads(g["wuq"], MLA_HEADS, MLA_QK, 1), w_ukv=jnp.concatenate([uk, uv], 2).reshape(KV_LORA, -1),
             w_o_swa=_unpad_heads(g["woa"], SWA_HEADS, HEAD_DIM, 0), w_o_mla=_unpad_heads(g["wob"], MLA_HEADS, MLA_V, 0))

    def flat(n, r):
        rr, cc = FULL_SHAPE[n]
        a = jnp.moveaxis(w[n].reshape(rr, N_DEV, cc // N_DEV), 1, 0).reshape(N_DEV, r, PACK_W)
        return jnp.pad(a, ((0, 0), (0, -r % ROW_TILE), (0, 0))).astype(WIRE_DTYPE)

    return jnp.concatenate([g["wout"].reshape(N_DEV, OUT_ROWS, PACK_W)] + [flat(n, r) for n, _, r in SMALL_FLAT], 1)


def _w_in_grad_chunks(g_win_t):
    return _take_rows(g_win_t, _w_in_row_maps()[1], name="dw_in_rows").reshape(N_DEV, W_IN_ROWS, PACK_W)


def _local_step(x, tgt, win_t, small, weights, grads):
    s_ = x.shape[0]
    tabs = _rope_tables(s_)
    sink_b = jnp.broadcast_to(small["swa_sinks"].reshape(SWA_KV_HEADS, SWA_GROUP, 1), (SWA_KV_HEADS, SWA_GROUP, LANES))
    sink_b = jnp.pad(sink_b, ((0, 0), (0, SUBLANES - SWA_GROUP), (0, 0)))

    h, p = _norm_mm(x, small["mix_norm_g"], win_t, name="proj_in", tn=2176, tm=1024)
    qa, ka, va, cq, ckv, kro = _attn_prep(p, small["q_norm_g"], small["kv_norm_g"], tabs)
    ops = weights.mid(cq)
    oa32, oa16, lse_a = _swa_fwd(qa, ka, va, sink_b)
    qp = _mm(cq, ops["wuq"], "nn", name="mla_q_up", tm=1024, tn=1024)
    kp = _mm(ckv, ops["wuk"], "nn", name="mla_k_up", tm=1024, tn=1024)
    vp = _mm(ckv, ops["wuv"], "nn", name="mla_v_up", tm=1024, tn=1024, out_dtype=MXU_DTYPE)
    qc, kc = _mla_prep(qp, kp, kro, tabs)
    ob32, ob16, lse_b = _mla_fwd(qc, kc, vp)
    ta, tb, y = _attn_out_gate(oa16, ob16, ops["woa"], ops["wob"], p)
    x1 = _mm(y, ops["wout"], "nn", name="out_proj", add=x, tm=1024, tn=1024)
    wgu_t, wd = weights.late(x1)
    h2, gu, act = _ffn_in_act(x1, small["ffn_norm_g"], wgu_t)

    dx2, dx2b, dg3, _, tot = _ffn_out_loss(act, wd, x1, small["final_norm_g"].reshape(1, D_MODEL), tgt)
    g = {}
    g_wd = _mm(act, dx2b, "tn", name="dw_down", tm=FF_TILE, tn=1024, tk=2048, out_dtype=WIRE_DTYPE)
    dgu = _d_act_swiglu(dx2b, wd, gu)
    g_wgu = _mm(dgu, h2, "tn", name="dw_ffn_in", tm=FF_TILE, tn=1024, tk=2048, out_dtype=WIRE_DTYPE)
    token = grads.late(g_wgu, g_wd)
    dx1, dx1b, dg2 = _mm_norm_bwd(dgu, wgu_t, x1, small["ffn_norm_g"] + token[0:1, 0:1], dx2, name="d_h2")
    g["wout"] = _mm(y, dx1b, "tn", name="dw_out", tm=1024, tn=1024, tk=1024, out_dtype=WIRE_DTYPE)
    dta, dtb, dgab = _d_y_gate(dx1b, ops["wout"], p, ta, tb)
    doa = _mm(dta, ops["woa"], "nt", name="d_oa", tm=1024, tn=1024)
    g["woa"] = _mm(oa16, dta, "tn", name="dw_o_swa", tm=1024, tn=1024, tk=1024)
    dob = _mm(dtb, ops["wob"], "nt", name="d_ob", tm=1024, tn=1024)
    g["wob"] = _mm(ob16, dtb, "tn", name="dw_o_mla", tm=1024, tn=1024, tk=1024)
    dob16, delta_b = _mla_bwd_prep(dob, ob32)
    dqc, dkc, dvp = _mla_bwd(qc, kc, vp, dob16, lse_b, delta_b)
    dqp, dkv, dkr = _mla_unprep(dqc, dkc, dvp, tabs)
    dcq = _mm(dqp, ops["wuq"], "nt", name="d_cq", tn=Q_LORA)
    g["wuq"] = _mm(cq, dqp, "tn", name="dw_uq", tm=Q_LORA, tn=1024, tk=512)
    dckv = _mm(dkv, jnp.concatenate([ops["wuk"], ops["wuv"]], 1), "nt", name="d_ckv", tn=KV_LORA)
    g["wukv"] = _mm(ckv, dkv, "tn", name="dw_ukv", tm=KV_LORA, tn=1024, tk=512)
    token = grads.mid(g)
    _, dqlat, dgq = _norm_bwd(p, small["q_norm_g"] + token[0:1, 0:1], dcq, None, name="qnorm_bwd", x_cb=P_QLAT // Q_LORA)
    _, dkvlat, dgkv = _norm_bwd(p, small["kv_norm_g"], dckv, None, name="kvnorm_bwd", x_cb=P_KVLAT // KV_LORA)
    dqa, dka, dva, dsk = _swa_bwd(qa, ka, va, sink_b, oa32, doa, lse_a)
    dp = _assemble_dp(dgab, dqa, dqlat, dkr, dka, dva, dkvlat, tabs)
    token = grads.last(_mm(dp, h, "tn", name="dw_in", tm=2176, tn=1024, tk=1024, out_dtype=WIRE_DTYPE))
    gx, _, dg1 = _mm_norm_bwd(dp, win_t, x, small["mix_norm_g"], dx1, name="d_h", after=token)

    sm = dict(mix_norm_g=dg1, ffn_norm_g=dg2, final_norm_g=dg3, q_norm_g=dgq, kv_norm_g=dgkv,
              swa_sinks=dsk[:, :SWA_GROUP, 0].reshape(1, SWA_HEADS))
    return tot, gx, sm


MESH = pl.DeviceIdType.MESH
ANY = pl.BlockSpec(memory_space=pl.ANY)


def _position():
    return lax.axis_index("x"), lax.axis_index("y"), lax.axis_index("c")


def _all_gather(block, pieces, shapes, *, name):
    n_out = len(shapes)
    n_rows = sum(p[3] for p in pieces)

    def body(x_ref, *refs):
        outs, (send_sems, recv_sems, local_sem) = refs[:n_out], refs[n_out:]
        x, y, c = _position()
        me, sibling = (x, y, c), (x, y, 1 - c)
        chips = [(1 - x, y), (x, 1 - y), (1 - x, 1 - y)]

        def dst(piece, blk):
            arr, lead, _, _ = piece
            return outs[arr].at[lead(4 * blk[0] + 2 * blk[1] + blk[2])]

        def own(piece):
            return x_ref.at[pl.ds(piece[2], piece[3])]

        def copies(k, blk, to, from_input):
            return [pltpu.make_async_remote_copy(
                src_ref=own(p) if from_input else dst(p, blk), dst_ref=dst(p, blk), send_sem=send_sems.at[k],
                recv_sem=recv_sems.at[k], device_id=to, device_id_type=MESH) for p in pieces]

        gathered_rows = x_ref.at[pl.ds(0, n_rows)]

        def whole_block(k):
            return pltpu.make_async_remote_copy(src_ref=gathered_rows, dst_ref=gathered_rows, send_sem=send_sems.at[k],
                                                recv_sem=recv_sems.at[k], device_id=me, device_id_type=MESH)

        for p in pieces:
            pltpu.make_async_copy(own(p), dst(p, me), local_sem).start()
        for cp in copies(0, me, sibling, True):
            cp.start()
        for j, chip in enumerate(chips):
            for cp in copies(1 + j, me, (*chip, c), True):
                cp.start()
        for j, chip in enumerate(chips):
            whole_block(1 + j).wait_recv()
            for cp in copies(4 + j, (*chip, c), sibling, False):
                cp.start()
        whole_block(0).wait_recv()
        for j in range(3):
            whole_block(4 + j).wait_recv()
        for k in range(7):
            whole_block(k).wait_send()
        pltpu.make_async_copy(gathered_rows, gathered_rows, local_sem).wait()

    return pl.pallas_call(
        body, name=name, out_shape=[jax.ShapeDtypeStruct(s, block.dtype) for s in shapes], in_specs=[ANY],
        out_specs=[ANY] * n_out,
        scratch_shapes=[pltpu.SemaphoreType.DMA((7,)), pltpu.SemaphoreType.DMA((7,)), pltpu.SemaphoreType.DMA],
    )(block)


HBM = pl.BlockSpec(memory_space=pltpu.HBM)
SEM = pl.BlockSpec(memory_space=pltpu.SEMAPHORE)
TILE_DEVS = FF_TILE // FF_COLS
GU_SHAPE = (2, 2, TILE_DEVS, FF_COLS, PACK_W)


def _gate_slab(d):
    return (d // TILE_DEVS, 0, d % TILE_DEVS)


def _up_slab(d):
    return (d // TILE_DEVS, 1, d % TILE_DEVS)
D_SHAPE = (N_DEV, FF_COLS, PACK_W)
LAND_SHAPE = (N_DEV, LATE_ROWS, PACK_W)


def _split_params():
    return pltpu.CompilerParams(has_side_effects=pltpu.SideEffectType.DATAFLOW_SIDE_EFFECTING)


def _peer(x, y, c, k):
    return ((1 - x) if k & 4 else x, (1 - y) if k & 2 else y, (1 - c) if k & 1 else c)


def _empty_hbm(shape, dtype):
    return pltpu.with_memory_space_constraint(lax.empty(shape, dtype), pltpu.HBM)


def _wait_all(rows, send_sems, recv_sems, me):
    for k in range(N_DEV - 1):
        cp = pltpu.make_async_remote_copy(src_ref=rows, dst_ref=rows, send_sem=send_sems.at[k], recv_sem=recv_sems.at[k],
                                          device_id=me, device_id_type=MESH)
        cp.wait_send()
        cp.wait_recv()


def _token_shape():
    return jax.ShapeDtypeStruct((SUBLANES, LANES), F32)


def _gather_start(pack, row0, pieces, shapes, *, name):
    n = len(shapes)

    def body(*refs):
        p_ref, bufs, send_sems, recv_sems, token = refs[0], refs[1:1 + n], refs[1 + n], refs[2 + n], refs[-1]
        x, y, c = _position()
        me = 4 * x + 2 * y + c
        for k in range(1, N_DEV):
            off = row0
            for buf, lead, rows in pieces:
                pltpu.make_async_remote_copy(
                    src_ref=p_ref.at[pl.ds(off, rows)], dst_ref=bufs[buf].at[lead(me)], send_sem=send_sems.at[k - 1],
                    recv_sem=recv_sems.at[k - 1], device_id=_peer(x, y, c, k), device_id_type=MESH).start()
                off += rows
        token[...] = jnp.zeros_like(token)

    sems, dt = pltpu.SemaphoreType.DMA((N_DEV - 1,)), pack.dtype
    return pl.pallas_call(
        body, name=name,
        out_shape=(sems, sems, pltpu.HBM(pack.shape, dt)) + tuple(pltpu.HBM(s, dt) for s in shapes) + (_token_shape(),),
        in_specs=(HBM,) * (1 + n), out_specs=(SEM, SEM) + (HBM,) * (1 + n) + (pl.BlockSpec(memory_space=pltpu.VMEM),),
        input_output_aliases={i: 2 + i for i in range(1 + n)}, compiler_params=_split_params(),
    )(pltpu.with_memory_space_constraint(pack, pltpu.HBM), *[_empty_hbm(s, dt) for s in shapes])


def _gather_wait(started, row0, n_rows, after, *, name):
    send_sems, recv_sems, pack, *bufs = started[:-1]
    n = len(bufs)

    def body(*refs):
        _wait_all(refs[0].at[pl.ds(row0, n_rows)], refs[1 + n], refs[2 + n], _position())

    outs = pl.pallas_call(
        body, name=name, out_shape=tuple(pltpu.HBM(a.shape, a.dtype) for a in (pack, *bufs)),
        in_specs=(HBM,) * (1 + n) + (SEM, SEM, ANY), out_specs=(HBM,) * (1 + n),
        input_output_aliases={i: i for i in range(1 + n)}, compiler_params=_split_params(),
    )(pack, *bufs, send_sems, recv_sems, after)
    return outs[0], outs[1:]


def _scatter_start(srcs, pieces, *, name):
    n = len(srcs)
    land_shape = (N_DEV, sum(p[2] for p in pieces), PACK_W)

    def body(*refs):
        src_refs, land_ref, send_sems, recv_sems, token = refs[:n], refs[n], refs[n + 1], refs[n + 2], refs[-1]
        x, y, c = _position()
        me = 4 * x + 2 * y + c
        for k in range(1, N_DEV):
            px, py, pc = _peer(x, y, c, k)
            off = 0
            for si, lead, rows in pieces:
                pltpu.make_async_remote_copy(
                    src_ref=src_refs[si].at[lead(4 * px + 2 * py + pc)], dst_ref=land_ref.at[me, pl.ds(off, rows)],
                    send_sem=send_sems.at[k - 1], recv_sem=recv_sems.at[k - 1], device_id=(px, py, pc),
                    device_id_type=MESH).start()
                off += rows
        token[...] = jnp.zeros_like(token)

    sems, dt = pltpu.SemaphoreType.DMA((N_DEV - 1,)), srcs[0].dtype
    return pl.pallas_call(
        body, name=name,
        out_shape=(sems, sems) + tuple(pltpu.HBM(a.shape, dt) for a in srcs) + (pltpu.HBM(land_shape, dt), _token_shape()),
        in_specs=(HBM,) * (n + 1), out_specs=(SEM, SEM) + (HBM,) * (n + 1) + (pl.BlockSpec(memory_space=pltpu.VMEM),),
        input_output_aliases={i: 2 + i for i in range(n + 1)}, compiler_params=_split_params(),
    )(*[pltpu.with_memory_space_constraint(a, pltpu.HBM) for a in srcs], _empty_hbm(land_shape, dt))


def _scatter_wait(started, after, *, name):
    send_sems, recv_sems, *bufs = started[:-1]
    n = len(bufs)

    def body(*refs):
        _wait_all(refs[n - 1].at[0], refs[n], refs[n + 1], _position())

    return pl.pallas_call(
        body, name=name, out_shape=tuple(pltpu.HBM(a.shape, a.dtype) for a in bufs),
        in_specs=(HBM,) * n + (SEM, SEM, ANY), out_specs=(HBM,) * n, input_output_aliases={i: i for i in range(n)},
        compiler_params=_split_params(),
    )(*bufs, send_sems, recv_sems, after)


def _peer_sum(own, own_lead, land, block, rows, idx, *, name):
    lead_rank = own.ndim - 2

    def body(idx_ref, own_ref, *refs):
        o_ref = refs[N_DEV - 1]
        acc = own_ref[(0,) * lead_rank].astype(F32)
        for k in range(N_DEV - 1):
            acc = acc + refs[k][0].astype(F32)
        o_ref[...] = acc

    own_spec = pl.BlockSpec((1,) * lead_rank + (rows, PACK_W), lambda i, t: own_lead(t[0]) + (0, 0))

    def land_spec(k):
        return pl.BlockSpec((1, rows, PACK_W), lambda i, t: (t[k + 1], block, 0))

    return pl.pallas_call(
        body, name=name,
        grid_spec=pltpu.PrefetchScalarGridSpec(
            num_scalar_prefetch=1, grid=(1,), in_specs=[own_spec] + [land_spec(k) for k in range(N_DEV - 1)],
            out_specs=pl.BlockSpec((rows, PACK_W), lambda i, t: (0, 0))),
        out_shape=jax.ShapeDtypeStruct((rows, PACK_W), F32), compiler_params=_cparams(("arbitrary",)),
    )(idx, own, *([land] * (N_DEV - 1)))


def _adamw(w, g, m, v):
    m = ADAM_B1 * m + (1.0 - ADAM_B1) * g
    v = ADAM_B2 * v + (1.0 - ADAM_B2) * (g * g)
    m_hat = m / (1.0 - ADAM_B1 ** ADAM_STEP)
    v_hat = v / (1.0 - ADAM_B2 ** ADAM_STEP)
    delta = -ADAM_LR * (m_hat / (jnp.sqrt(v_hat) + ADAM_EPS) + ADAM_WD * w)
    return delta, m, v


def _adamw_call(w, g, m, v, *, name, max_rows=256):
    _, r, c_ = w.shape
    tr = max_rows if r > max_rows and r % max_rows == 0 else r

    def body(w_ref, g_ref, m_ref, v_ref, d_ref, mo_ref, vo_ref):
        d, mn, vn = _adamw(w_ref[0], g_ref[...], m_ref[0], v_ref[0])
        d_ref[0] = d
        mo_ref[0] = mn
        vo_ref[0] = vn

    row3 = pl.BlockSpec((1, tr, c_), lambda i: (0, i, 0))
    shp = jax.ShapeDtypeStruct((1, r, c_), F32)
    return pl.pallas_call(
        body, name=name, grid=(r // tr,), in_specs=[row3, pl.BlockSpec((tr, c_), lambda i: (i, 0)), row3, row3],
        out_specs=[row3] * 3, out_shape=[shp] * 3, compiler_params=_cparams(("parallel",)),
    )(w, g, m, v)


SMALL = ("mix_norm_g", "ffn_norm_g", "final_norm_g", "q_norm_g", "kv_norm_g", "swa_sinks")
SMALL_W = dict(mix_norm_g=1024, ffn_norm_g=1024, final_norm_g=1024, q_norm_g=Q_LORA, kv_norm_g=KV_LORA, swa_sinks=SWA_HEADS)


def _small_adamw(parts, w, m, v):
    n_par = parts.shape[1] // SUBLANES

    def body(p_ref, w_ref, m_ref, v_ref, g_ref, d_ref, mo_ref, vo_ref):
        tot = p_ref[0]
        for dev in range(1, N_DEV):
            tot = tot + p_ref[dev]
        row_id = lax.broadcasted_iota(jnp.int32, (SUBLANES, PACK_W), 0)
        g = jnp.zeros((SUBLANES, PACK_W), F32)
        for k in range(n_par):
            g = jnp.where(row_id == k, jnp.sum(tot[k * SUBLANES:(k + 1) * SUBLANES, :], axis=0, keepdims=True), g)
        d, mn, vn = _adamw(w_ref[...], g, m_ref[...], v_ref[...])
        g_ref[...] = g
        d_ref[...] = d
        mo_ref[...] = mn
        vo_ref[...] = vn

    shp = jax.ShapeDtypeStruct((SUBLANES, PACK_W), F32)
    vm = pl.BlockSpec(memory_space=pltpu.VMEM)
    return pl.pallas_call(body, name="small_adamw", in_specs=[vm] * 4, out_specs=[vm] * 4, out_shape=[shp] * 4)(parts, w, m, v)


def _small_pack(d, rows_each):
    parts = [jnp.pad(d[n].astype(F32), ((0, 0), (0, PACK_W - SMALL_W[n]))) for n in SMALL]
    out = jnp.concatenate(parts, 0)
    pad = -out.shape[0] % SUBLANES
    return jnp.pad(out, ((0, pad), (0, 0)))


def kernel(x, mix_norm_g, w_in, swa_sinks, q_norm_g, w_uq, kv_norm_g, w_ukv, w_o_swa, w_o_mla, w_out, ffn_norm_g, w_gate, w_up, w_down, final_norm_g, loss_target, m_mix_norm_g, m_w_in, m_swa_sinks, m_q_norm_g, m_w_uq, m_kv_norm_g, m_w_ukv, m_w_o_swa, m_w_o_mla, m_w_out, m_ffn_norm_g, m_w_gate, m_w_up, m_w_down, m_final_norm_g, v_mix_norm_g, v_w_in, v_swa_sinks, v_q_norm_g, v_w_uq, v_kv_norm_g, v_w_ukv, v_w_o_swa, v_w_o_mla, v_w_out, v_ffn_norm_g, v_w_gate, v_w_up, v_w_down, v_final_norm_g):
    big_w = dict(w_in=w_in[0], w_uq=w_uq[0], w_ukv=w_ukv[0], w_o_swa=w_o_swa[0], w_o_mla=w_o_mla[0], w_out=w_out[0],
                 w_gate=w_gate[0], w_up=w_up[0], w_down=w_down[0])
    big_w3 = dict(w_in=w_in, w_uq=w_uq, w_ukv=w_ukv, w_o_swa=w_o_swa, w_o_mla=w_o_mla, w_out=w_out, w_gate=w_gate, w_up=w_up,
                  w_down=w_down)
    big_m = dict(w_in=m_w_in, w_uq=m_w_uq, w_ukv=m_w_ukv, w_o_swa=m_w_o_swa, w_o_mla=m_w_o_mla, w_out=m_w_out,
                 w_gate=m_w_gate, w_up=m_w_up, w_down=m_w_down)
    big_v = dict(w_in=v_w_in, w_uq=v_w_uq, w_ukv=v_w_ukv, w_o_swa=v_w_o_swa, w_o_mla=v_w_o_mla, w_out=v_w_out,
                 w_gate=v_w_gate, w_up=v_w_up, w_down=v_w_down)
    small_w = dict(mix_norm_g=mix_norm_g, ffn_norm_g=ffn_norm_g, final_norm_g=final_norm_g.reshape(1, D_MODEL),
                   q_norm_g=q_norm_g, kv_norm_g=kv_norm_g, swa_sinks=swa_sinks)
    small_m = dict(mix_norm_g=m_mix_norm_g, ffn_norm_g=m_ffn_norm_g, final_norm_g=m_final_norm_g.reshape(1, D_MODEL),
                   q_norm_g=m_q_norm_g, kv_norm_g=m_kv_norm_g, swa_sinks=m_swa_sinks)
    small_v = dict(mix_norm_g=v_mix_norm_g, ffn_norm_g=v_ffn_norm_g, final_norm_g=v_final_norm_g.reshape(1, D_MODEL),
                   q_norm_g=v_q_norm_g, kv_norm_g=v_kv_norm_g, swa_sinks=v_swa_sinks)

    px, py, pc = _position()
    me = 4 * px + 2 * py + pc
    idx = jnp.stack([me] + [4 * qx + 2 * qy + qc for qx, qy, qc in (_peer(px, py, pc, k) for k in range(1, N_DEV))])
    idx = idx.astype(jnp.int32)

    dev = lambda d: (d,)
    pack = _wire_pack(big_w, WIRE_DTYPE)
    win_g, = _all_gather(pack, ((0, dev, 0, W_IN_ROWS),), ((N_DEV, W_IN_ROWS, PACK_W),), name="ag_early")
    ag_mid = _gather_start(pack, W_IN_ROWS, ((0, dev, OUT_ROWS), (1, dev, SMALL_ROWS)),
                           ((N_DEV, OUT_ROWS, PACK_W), (N_DEV, SMALL_ROWS, PACK_W)), name="ag_mid_start")
    ag = {}

    def own_rows(r0, r1, shape):
        return pack[r0:r1].reshape(shape)

    def mid_weights(after):
        pack_mid, (wout_g, small_g) = _gather_wait(ag_mid, W_IN_ROWS, MID_ROWS, after, name="ag_mid_wait")
        ag["late"] = _gather_start(pack_mid, EARLY_ROWS, ((0, _gate_slab, FF_COLS), (0, _up_slab, FF_COLS), (1, dev, FF_COLS)),
                                   (GU_SHAPE, D_SHAPE), name="ag_late_start")
        wout_g = lax.dynamic_update_slice(wout_g, own_rows(W_IN_ROWS, SMALL_ROW0, (1, OUT_ROWS, PACK_W)), (me, 0, 0))
        small_g = lax.dynamic_update_slice(small_g, own_rows(SMALL_ROW0, EARLY_ROWS, (1, SMALL_ROWS, PACK_W)), (me, 0, 0))
        ops = _mid_operands(wout_g, small_g)
        ops["wuq"] = ops["wuq"] + ag["late"][-1][0:1, 0:1].astype(ops["wuq"].dtype)
        return ops

    def late_weights(after):
        _, (gu, d) = _gather_wait(ag["late"], EARLY_ROWS, LATE_ROWS, after, name="ag_late_wait")
        slab = (1, 1, 1, FF_COLS, PACK_W)
        gu = lax.dynamic_update_slice(gu, own_rows(EARLY_ROWS, EARLY_ROWS + FF_COLS, slab), _gate_slab(me) + (0, 0))
        gu = lax.dynamic_update_slice(gu, own_rows(EARLY_ROWS + FF_COLS, EARLY_ROWS + 2 * FF_COLS, slab), _up_slab(me) + (0, 0))
        d = lax.dynamic_update_slice(d, own_rows(EARLY_ROWS + 2 * FF_COLS, PACK_ROWS, (1, FF_COLS, PACK_W)), (me, 0, 0))
        return gu.reshape(2 * D_FF, D_MODEL), d.reshape(D_FF, D_MODEL)

    rs = {}

    def late_grads(g_gu, g_d):
        rs["late"] = _scatter_start([g_gu.reshape(GU_SHAPE), g_d.reshape(D_SHAPE)],
                                    ((0, _gate_slab, FF_COLS), (0, _up_slab, FF_COLS), (1, dev, FF_COLS)),
                                    name="rs_late_start")
        return rs["late"][-1]

    def mid_grads(g):
        rs["mid"] = _scatter_start([_mid_grad_pack(g)], ((0, dev, MID_ROWS),), name="rs_mid_start")
        return rs["mid"][-1]

    def last_grads(g_win_t):
        rs["last"] = _scatter_start([_w_in_grad_chunks(g_win_t)], ((0, dev, W_IN_ROWS),), name="rs_last_start")
        return rs["last"][-1]

    first_w = dict(small_w, mix_norm_g=mix_norm_g + ag_mid[-1][0:1, 0:1])
    loss_tot, gx, g_small = _local_step(
        x[0], loss_target[0], _w_in_operand(win_g), first_w, types.SimpleNamespace(mid=mid_weights, late=late_weights),
        types.SimpleNamespace(late=late_grads, mid=mid_grads, last=last_grads))

    g_gu, g_d, land_late = _scatter_wait(rs["late"], gx, name="rs_late_wait")
    g_mid, land_mid = _scatter_wait(rs["mid"], gx, name="rs_mid_wait")
    g_win, land_last = _scatter_wait(rs["last"], gx, name="rs_last_wait")
    gw = dict(w_gate=_peer_sum(g_gu, _gate_slab, land_late, 0, FF_COLS, idx, name="rs_sum_gate").T,
              w_up=_peer_sum(g_gu, _up_slab, land_late, 1, FF_COLS, idx, name="rs_sum_up").T,
              w_down=_peer_sum(g_d, dev, land_late, 2, FF_COLS, idx, name="rs_sum_down"),
              w_in=_peer_sum(g_win, dev, land_last, 0, W_IN_ROWS, idx, name="rs_sum_in")[0:W_IN_COLS].T)
    gw.update(_mid_unpack(_peer_sum(g_mid, dev, land_mid, 0, MID_ROWS, idx, name="rs_sum_mid")))
    dw, mw, vw = {}, {}, {}
    for n in BIG:
        dw[n], mw[n], vw[n] = _adamw_call(big_w3[n], gw[n], big_m[n], big_v[n], name="adamw_" + n)
    gw = {n: g[None] for n, g in gw.items()}

    loss_rows = jnp.pad(loss_tot[0:1, 0:1], ((0, SUBLANES - 1), (0, PACK_W - 1)))
    small_rows = jnp.concatenate([_small_pack(g_small_rows(g_small), SUBLANES), loss_rows], 0)
    parts, = _all_gather(small_rows, ((0, lambda d: (d,), 0, small_rows.shape[0]),), ((N_DEV,) + small_rows.shape,),
                         name="ag_small")
    gs, ds, ms, vs = _small_adamw(parts, _small_pack(small_w, 1), _small_pack(small_m, 1), _small_pack(small_v, 1))
    loss = gs[len(SMALL), 0]

    def small_out(packed):
        out = {}
        for k, n in enumerate(SMALL):
            out[n] = packed[k:k + 1, :SMALL_W[n]]
        out["final_norm_g"] = out["final_norm_g"].reshape(D_MODEL)
        return out

    gs, ds, ms, vs = small_out(gs), small_out(ds), small_out(ms), small_out(vs)

    order = ("mix_norm_g", "w_in", "swa_sinks", "q_norm_g", "w_uq", "kv_norm_g", "w_ukv", "w_o_swa", "w_o_mla", "w_out",
             "ffn_norm_g", "w_gate", "w_up", "w_down", "final_norm_g")

    def leaves(big, small):
        return [big[n] if n in big else small[n] for n in order]

    return (loss, gx[None], *leaves(gw, gs), *leaves(dw, ds), *leaves(mw, ms), *leaves(vw, vs))


def g_small_rows(g_small):
    out = dict(g_small)
    out["swa_sinks"] = jnp.pad(g_small["swa_sinks"], ((0, SUBLANES - 1), (0, 0)))
    return out
```

```python
import types

import numpy as np
import jax
import jax.numpy as jnp
from jax import lax
from jax.experimental import pallas as pl
from jax.experimental.pallas import tpu as pltpu

F32 = jnp.float32
MXU_DTYPE = jnp.bfloat16
WIRE_DTYPE = jnp.bfloat16

D_MODEL = 1024
EPS = 1e-6
ROPE_THETA = 10000.0
BLOCK = 128
HEAD_DIM = 64
SWA_HEADS = 8
SWA_KV_HEADS = 2
SWA_GROUP = SWA_HEADS // SWA_KV_HEADS
MLA_HEADS = 8
MLA_NOPE = 64
MLA_ROPE = 32
MLA_V = 64
MLA_QK = MLA_NOPE + MLA_ROPE
Q_LORA = 384
KV_LORA = 256
D_FF = 2816
IN_SIZES = (512, 128, 128, Q_LORA, KV_LORA, MLA_ROPE, D_MODEL, D_MODEL)
IN_OFF = tuple(int(v) for v in np.cumsum((0,) + IN_SIZES))
ADAM_LR, ADAM_B1, ADAM_B2, ADAM_EPS, ADAM_WD, ADAM_STEP = 0.001, 0.9, 0.999, 1e-08, 0.01, 10

LANES = 128
SUBLANES = 8
VMEM_LIMIT = 48 * 1024 * 1024
N_DEV = 8
AXES = ("x", "y", "c")

P_GA, P_GB, P_Q, P_QLAT, P_KR, P_K, P_V, P_KVLAT, P_W = 0, 1024, 2048, 3072, 3456, 3584, 3840, 4096, 4352
KR_LANE = 64

LOG2E = 1.4426950408889634

NT = (((1,), (1,)), ((), ()))
NN = (((1,), (0,)), ((), ()))
TN = (((0,), (0,)), ((), ()))


def _cparams(sem):
    return pltpu.CompilerParams(dimension_semantics=sem, vmem_limit_bytes=VMEM_LIMIT)


def _mm(a, b, mode, *, name, out_dtype=F32, add=None, after=None, tm=512, tn=512, tk=None):
    if mode == "nn":
        (M, K), (K2, N) = a.shape, b.shape
    elif mode == "nt":
        (M, K), (N, K2) = a.shape, b.shape
    else:
        (K, M), (K2, N) = a.shape, b.shape
    assert K == K2, (a.shape, b.shape, mode)
    tm, tn, tk = min(tm, M), min(tn, N), K if tk is None else min(tk, K)
    assert M % tm == 0 and N % tn == 0 and K % tk == 0, (M, N, K, tm, tn, tk)
    nk = K // tk
    dn = {"nn": NN, "nt": NT, "tn": TN}[mode]
    if mode == "tn":
        a_spec = pl.BlockSpec((tk, tm), lambda i, j, k: (k, i))
    else:
        a_spec = pl.BlockSpec((tm, tk), lambda i, j, k: (i, k))
    once = dict(pipeline_mode=pl.Buffered(1)) if (nk == 1 and tn == N) else {}
    if mode == "nt":
        b_spec = pl.BlockSpec((tn, tk), lambda i, j, k: (j, k), **once)
    else:
        b_spec = pl.BlockSpec((tk, tn), lambda i, j, k: (k, j), **once)
    o_spec = pl.BlockSpec((tm, tn), lambda i, j, k: (i, j))
    has_add, has_after = add is not None, after is not None

    def body(*refs):
        a_ref, b_ref = refs[0], refs[1]
        add_ref = refs[2] if has_add else None
        o_ref = refs[2 + has_add + has_after]
        p = lax.dot_general(a_ref[...], b_ref[...], dn, preferred_element_type=F32)

        def finish(acc):
            if has_add:
                acc = acc + add_ref[...]
            o_ref[...] = acc.astype(o_ref.dtype)

        if nk == 1:
            finish(p)
        else:
            acc_ref = refs[-1]
            k = pl.program_id(2)

            @pl.when(k == 0)
            def _():
                acc_ref[...] = p

            @pl.when((k > 0) & (k < nk - 1))
            def _():
                acc_ref[...] += p

            @pl.when(k == nk - 1)
            def _():
                finish(acc_ref[...] + p)

    ins = [a, b] + ([add] if has_add else []) + ([after] if has_after else [])
    in_specs = [a_spec, b_spec] + ([o_spec] if has_add else []) + ([pl.BlockSpec(memory_space=pl.ANY)] if has_after else [])
    return pl.pallas_call(
        body, name=name, grid=(M // tm, N // tn, nk), in_specs=in_specs, out_specs=o_spec,
        out_shape=jax.ShapeDtypeStruct((M, N), out_dtype),
        scratch_shapes=[pltpu.VMEM((tm, tn), F32)] if nk > 1 else [],
        compiler_params=_cparams(("parallel", "parallel", "arbitrary")),
    )(*ins)


def _rows(ts, w, cb=0):
    return pl.BlockSpec((ts, w), lambda i: (i, cb))


def _const(r, w):
    return pl.BlockSpec((r, w), lambda i: (0, 0))


def _sublane_sum(v):
    ts, c = v.shape
    return jnp.sum(v.reshape(ts // SUBLANES, SUBLANES, c), axis=0)


def _sigmoid(v):
    return 1.0 / (1.0 + jnp.exp(-v))


def _rope(v, cos, s_up, s_dn, up, dn):
    return v * cos + pltpu.roll(v, up, 1) * s_up + pltpu.roll(v, dn, 1) * s_dn


def _rope_t(dv, cos, s_up, s_dn, up, dn):
    return dv * cos + pltpu.roll(dv * s_up, dn, 1) + pltpu.roll(dv * s_dn, up, 1)


def _rope_tables(seq):
    pos = np.arange(seq, dtype=np.float32)[:, None]

    def base(dim):
        inv = np.float32(ROPE_THETA) ** (-np.arange(0, dim, 2, dtype=np.float32) / np.float32(dim))
        ang = (pos * inv.astype(np.float32)[None, :]).astype(np.float32)
        return np.cos(ang).astype(np.float32), np.sin(ang).astype(np.float32)

    z = lambda n: np.zeros((seq, n), np.float32)
    ca, sa = base(HEAD_DIM)
    a_cos = np.concatenate([ca, ca, z(64)], 1)
    a_up = np.concatenate([-sa, z(96)], 1)
    a_dn = np.concatenate([z(32), sa, z(64)], 1)
    cb, sb = base(MLA_ROPE)
    one = np.ones((seq, 64), np.float32)
    q_cos = np.concatenate([one, cb, cb, z(32)], 1)
    k_cos = np.concatenate([z(64), cb, cb, z(32)], 1)
    b_up = np.concatenate([z(64), -sb, z(48)], 1)
    b_dn = np.concatenate([z(80), sb, z(32)], 1)
    return tuple(jnp.asarray(t) for t in (a_cos, a_up, a_dn, q_cos, k_cos, b_up, b_dn))


def _rms(v, g):
    return v * lax.rsqrt(jnp.mean(v * v, axis=-1, keepdims=True) + EPS) * g


def _rms_bwd(v, g, d):
    r = lax.rsqrt(jnp.mean(v * v, axis=-1, keepdims=True) + EPS)
    xh = v * r
    dxh = d * g
    return r * (dxh - xh * jnp.mean(dxh * xh, axis=-1, keepdims=True)), d * xh


def _norm_mm(x, g, w_t, *, name, tn, tm=512):
    s_, c = x.shape
    n = w_t.shape[0]

    def body(x_ref, g_ref, w_ref, h_ref, o_ref):
        h = _rms(x_ref[...], g_ref[...]).astype(h_ref.dtype)
        h_ref[...] = h
        o_ref[...] = lax.dot_general(h, w_ref[...], NT, preferred_element_type=F32)

    return pl.pallas_call(
        body, name=name, grid=(s_ // tm, n // tn),
        in_specs=[pl.BlockSpec((tm, c), lambda i, j: (i, 0)), pl.BlockSpec((1, c), lambda i, j: (0, 0)),
                  pl.BlockSpec((tn, c), lambda i, j: (j, 0))],
        out_specs=[pl.BlockSpec((tm, c), lambda i, j: (i, 0)), pl.BlockSpec((tm, tn), lambda i, j: (i, j))],
        out_shape=[jax.ShapeDtypeStruct((s_, c), MXU_DTYPE), jax.ShapeDtypeStruct((s_, n), F32)],
        compiler_params=_cparams(("parallel", "arbitrary")),
    )(x, g, w_t)


def _mm_norm_bwd(a, b, x, g, res, *, name, after=None, tm=512):
    s_, kk = a.shape
    c = b.shape[1]
    has_after = after is not None

    def body(*refs):
        a_ref, b_ref, x_ref, g_ref, res_ref = refs[:5]
        dx_ref, dxb_ref, dg_ref = refs[5 + has_after:]
        d = jnp.dot(a_ref[...], b_ref[...], preferred_element_type=F32)
        dx, gg = _rms_bwd(x_ref[...], g_ref[...], d)
        dx = dx + res_ref[...]
        dx_ref[...] = dx
        dxb_ref[...] = dx.astype(dxb_ref.dtype)

        @pl.when(pl.program_id(0) == 0)
        def _():
            dg_ref[...] = jnp.zeros(dg_ref.shape, F32)

        dg_ref[...] += _sublane_sum(gg)

    row = _rows(tm, c)
    in_specs = [_rows(tm, kk), pl.BlockSpec((kk, c), lambda i: (0, 0), pipeline_mode=pl.Buffered(1)), row, _const(1, c), row]
    return pl.pallas_call(
        body, name=name, grid=(s_ // tm,), in_specs=in_specs + ([pl.BlockSpec(memory_space=pl.ANY)] if has_after else []),
        out_specs=[row, row, _const(SUBLANES, c)],
        out_shape=[jax.ShapeDtypeStruct((s_, c), F32), jax.ShapeDtypeStruct((s_, c), MXU_DTYPE),
                   jax.ShapeDtypeStruct((SUBLANES, c), F32)],
        compiler_params=_cparams(("arbitrary",)),
    )(*([a, b, x, g, res] + ([after] if has_after else [])))


def _attn_prep(p, gq, gkv, tabs, *, ts=256):
    s_ = p.shape[0]
    a_cos, a_up, a_dn, _, k_cos, b_up, b_dn = tabs

    def body(q_ref, k_ref, v_ref, ql_ref, kvl_ref, kr_ref, gq_ref, gkv_ref, ac, au, ad, kc, bu, bd,
             qa_ref, ka_ref, va_ref, cq_ref, ckv_ref, kro_ref):
        c_, u_, d_ = ac[...], au[...], ad[...]
        for h in range(SWA_HEADS):
            sl = slice(h * LANES, (h + 1) * LANES)
            qa_ref[:, sl] = _rope(q_ref[:, sl], c_, u_, d_, 96, 32).astype(qa_ref.dtype)
        for h in range(SWA_KV_HEADS):
            sl = slice(h * LANES, (h + 1) * LANES)
            ka_ref[:, sl] = _rope(k_ref[:, sl], c_, u_, d_, 96, 32).astype(ka_ref.dtype)
        va_ref[...] = v_ref[...].astype(va_ref.dtype)
        for src, gref, dst in ((ql_ref, gq_ref, cq_ref), (kvl_ref, gkv_ref, ckv_ref)):
            v = src[...]
            r = lax.rsqrt(jnp.mean(v * v, axis=-1, keepdims=True) + EPS)
            dst[...] = (v * r * gref[...]).astype(dst.dtype)
        kro_ref[...] = _rope(kr_ref[...], kc[...], bu[...], bd[...], 112, 16)

    tab = _rows(ts, LANES)
    return pl.pallas_call(
        body, name="attn_prep", grid=(s_ // ts,),
        in_specs=[_rows(ts, 1024, P_Q // 1024), _rows(ts, 256, P_K // 256), _rows(ts, 256, P_V // 256),
                  _rows(ts, Q_LORA, P_QLAT // Q_LORA), _rows(ts, KV_LORA, P_KVLAT // KV_LORA),
                  _rows(ts, LANES, P_KR // LANES), _const(1, Q_LORA), _const(1, KV_LORA), tab, tab, tab, tab, tab, tab],
        out_specs=[_rows(ts, 1024), _rows(ts, 256), _rows(ts, 256), _rows(ts, Q_LORA), _rows(ts, KV_LORA),
                   _rows(ts, LANES)],
        out_shape=[jax.ShapeDtypeStruct((s_, 1024), MXU_DTYPE), jax.ShapeDtypeStruct((s_, 256), MXU_DTYPE),
                   jax.ShapeDtypeStruct((s_, 256), MXU_DTYPE), jax.ShapeDtypeStruct((s_, Q_LORA), MXU_DTYPE),
                   jax.ShapeDtypeStruct((s_, KV_LORA), MXU_DTYPE), jax.ShapeDtypeStruct((s_, LANES), F32)],
        compiler_params=_cparams(("parallel",)),
    )(p, p, p, p, p, p, gq, gkv, a_cos, a_up, a_dn, k_cos, b_up, b_dn)


def _mla_up(cq, ckv, kro, wuq, wuk, wuv, tabs, *, ts=512):
    s_ = cq.shape[0]
    _, _, _, q_cos, _, b_up, b_dn = tabs

    def body(cq_ref, ckv_ref, kr_ref, wq_ref, wk_ref, wv_ref, qc, bu, bd, qo_ref, ko_ref, vo_ref):
        c_, u_, d_ = qc[...], bu[...], bd[...]
        kr = kr_ref[...]
        ckv_ = ckv_ref[...]
        vo_ref[...] = jnp.dot(ckv_, wv_ref[...], preferred_element_type=F32).astype(vo_ref.dtype)
        q = jnp.dot(cq_ref[...], wq_ref[...], preferred_element_type=F32)
        k = jnp.dot(ckv_, wk_ref[...], preferred_element_type=F32)
        for h in range(MLA_HEADS):
            sl = slice(h * LANES, (h + 1) * LANES)
            qo_ref[:, sl] = _rope(q[:, sl], c_, u_, d_, 112, 16).astype(qo_ref.dtype)
            ko_ref[:, sl] = (k[:, sl] + kr).astype(ko_ref.dtype)

    tab, out = _rows(ts, LANES), _rows(ts, 1024)
    return pl.pallas_call(
        body, name="mla_up", grid=(s_ // ts,),
        in_specs=[_rows(ts, Q_LORA), _rows(ts, KV_LORA), tab, _const(Q_LORA, 1024), _const(KV_LORA, 1024),
                  _const(KV_LORA, 1024), tab, tab, tab],
        out_specs=[out, out, out], out_shape=[jax.ShapeDtypeStruct((s_, 1024), MXU_DTYPE)] * 3,
        compiler_params=_cparams(("parallel",)),
    )(cq, ckv, kro, wuq, wuk, wuv, q_cos, b_up, b_dn)


def _mla_up_bwd(dqc, dkc, dvp, wuq, wukv, p, gq, gkv, tabs, *, ts=256):
    s_ = dqc.shape[0]
    _, _, _, q_cos, k_cos, b_up, b_dn = tabs

    def body(dq_ref, dk_ref, dv_ref, wq_ref, wkv_ref, ql_ref, kvl_ref, gq_ref, gkv_ref, qc, kc, bu, bd,
             dqo_ref, dkvo_ref, dkr_ref, dql_ref, dkvl_ref, dgq_ref, dgkv_ref):
        c_, u_, d_ = qc[...], bu[...], bd[...]
        tot = jnp.zeros((ts, LANES), F32)
        for h in range(MLA_HEADS):
            sl = slice(h * LANES, (h + 1) * LANES)
            dqo_ref[:, sl] = _rope_t(dq_ref[:, sl], c_, u_, d_, 112, 16).astype(dqo_ref.dtype)
            dk = dk_ref[:, sl]
            dkvo_ref[:, sl] = dk.astype(dkvo_ref.dtype)
            tot = tot + dk
        dkvo_ref[:, 1024:2048] = dv_ref[...].astype(dkvo_ref.dtype)
        dkr_ref[...] = _rope_t(tot, kc[...], u_, d_, 112, 16).astype(dkr_ref.dtype)

        @pl.when(pl.program_id(0) == 0)
        def _():
            dgq_ref[...] = jnp.zeros(dgq_ref.shape, F32)
            dgkv_ref[...] = jnp.zeros(dgkv_ref.shape, F32)

        for do_ref, w_ref, x_ref, g_ref, dx_ref, dg_ref in ((dqo_ref, wq_ref, ql_ref, gq_ref, dql_ref, dgq_ref),
                                                            (dkvo_ref, wkv_ref, kvl_ref, gkv_ref, dkvl_ref, dgkv_ref)):
            d = lax.dot_general(do_ref[...], w_ref[...], NT, preferred_element_type=F32)
            dx, gg = _rms_bwd(x_ref[...], g_ref[...], d)
            dx_ref[...] = dx.astype(dx_ref.dtype)
            dg_ref[...] += _sublane_sum(gg)

    tab = _rows(ts, LANES)
    return pl.pallas_call(
        body, name="mla_up_bwd", grid=(s_ // ts,),
        in_specs=[_rows(ts, 1024), _rows(ts, 1024), _rows(ts, 1024), _const(Q_LORA, 1024), _const(KV_LORA, 2048),
                  _rows(ts, Q_LORA, P_QLAT // Q_LORA), _rows(ts, KV_LORA, P_KVLAT // KV_LORA),
                  _const(1, Q_LORA), _const(1, KV_LORA), tab, tab, tab, tab],
        out_specs=[_rows(ts, 1024), _rows(ts, 2048), _rows(ts, LANES), _rows(ts, Q_LORA), _rows(ts, KV_LORA),
                   _const(SUBLANES, Q_LORA), _const(SUBLANES, KV_LORA)],
        out_shape=[jax.ShapeDtypeStruct((s_, 1024), MXU_DTYPE), jax.ShapeDtypeStruct((s_, 2048), MXU_DTYPE),
                   jax.ShapeDtypeStruct((s_, LANES), MXU_DTYPE), jax.ShapeDtypeStruct((s_, Q_LORA), MXU_DTYPE),
                   jax.ShapeDtypeStruct((s_, KV_LORA), MXU_DTYPE), jax.ShapeDtypeStruct((SUBLANES, Q_LORA), F32),
                   jax.ShapeDtypeStruct((SUBLANES, KV_LORA), F32)],
        compiler_params=_cparams(("arbitrary",)),
    )(dqc, dkc, dvp, wuq, wukv, p, p, gq, gkv, q_cos, k_cos, b_up, b_dn)


def _assemble_dp(dgab, dqa, dqlat, dkr, dka, dva, dkvlat, tabs, *, ts=256):
    s_ = dqa.shape[0]
    a_cos, a_up, a_dn = tabs[0], tabs[1], tabs[2]

    def body(dg_ref, dq_ref, dql_ref, dkr_ref, dk_ref, dv_ref, dkvl_ref, ac, au, ad, o_ref):
        c_, u_, d_ = ac[...], au[...], ad[...]
        o_ref[:, P_GA:P_Q] = dg_ref[...]
        for h in range(SWA_HEADS):
            sl = slice(h * LANES, (h + 1) * LANES)
            o_ref[:, P_Q + h * LANES:P_Q + (h + 1) * LANES] = _rope_t(dq_ref[:, sl], c_, u_, d_, 96, 32).astype(o_ref.dtype)
        o_ref[:, P_QLAT:P_KR] = dql_ref[...]
        o_ref[:, P_KR:P_K] = dkr_ref[...]
        for h in range(SWA_KV_HEADS):
            sl = slice(h * LANES, (h + 1) * LANES)
            o_ref[:, P_K + h * LANES:P_K + (h + 1) * LANES] = _rope_t(dk_ref[:, sl], c_, u_, d_, 96, 32).astype(o_ref.dtype)
        o_ref[:, P_V:P_KVLAT] = dv_ref[...]
        o_ref[:, P_KVLAT:P_W] = dkvl_ref[...]

    tab = _rows(ts, LANES)
    return pl.pallas_call(
        body, name="assemble_dp", grid=(s_ // ts,),
        in_specs=[_rows(ts, 2048), _rows(ts, 1024), _rows(ts, Q_LORA), _rows(ts, LANES), _rows(ts, 256), _rows(ts, 256),
                  _rows(ts, KV_LORA), tab, tab, tab],
        out_specs=_rows(ts, P_W), out_shape=jax.ShapeDtypeStruct((s_, P_W), MXU_DTYPE),
        compiler_params=_cparams(("parallel",)),
    )(dgab, dqa, dqlat, dkr, dka, dva, dkvlat, a_cos, a_up, a_dn)


def _attn_out_gate(oa, ob, woa, wob, p, *, ts=512):
    s_ = p.shape[0]

    def body(oa_ref, ob_ref, wa_ref, wb_ref, ga_ref, gb_ref, ta_ref, tb_ref, y_ref):
        ta = jnp.dot(oa_ref[...], wa_ref[...], preferred_element_type=F32)
        tb = jnp.dot(ob_ref[...], wb_ref[...], preferred_element_type=F32)
        ta_ref[...] = ta
        tb_ref[...] = tb
        y_ref[...] = (_sigmoid(ga_ref[...]) * ta + _sigmoid(gb_ref[...]) * tb).astype(y_ref.dtype)

    w = _const(1024, 1024)
    return pl.pallas_call(
        body, name="attn_out_gate", grid=(s_ // ts,),
        in_specs=[_rows(ts, 1024), _rows(ts, 1024), w, w, _rows(ts, 1024, P_GA // 1024), _rows(ts, 1024, P_GB // 1024)],
        out_specs=[_rows(ts, 1024)] * 3,
        out_shape=[jax.ShapeDtypeStruct((s_, 1024), F32)] * 2 + [jax.ShapeDtypeStruct((s_, 1024), MXU_DTYPE)],
        compiler_params=_cparams(("parallel",)),
    )(oa, ob, woa, wob, p, p)


def _d_y_gate(dx1b, wout, p, ta, tb, *, ts=512):
    s_ = p.shape[0]

    def body(dx_ref, w_ref, ga_ref, gb_ref, ta_ref, tb_ref, dta_ref, dtb_ref, dg_ref):
        d = lax.dot_general(dx_ref[...], w_ref[...], NT, preferred_element_type=F32)
        sa, sb = _sigmoid(ga_ref[...]), _sigmoid(gb_ref[...])
        dta_ref[...] = (d * sa).astype(dta_ref.dtype)
        dtb_ref[...] = (d * sb).astype(dtb_ref.dtype)
        dg_ref[:, 0:1024] = (d * ta_ref[...] * (sa * (1.0 - sa))).astype(dg_ref.dtype)
        dg_ref[:, 1024:2048] = (d * tb_ref[...] * (sb * (1.0 - sb))).astype(dg_ref.dtype)

    return pl.pallas_call(
        body, name="d_y_gate", grid=(s_ // ts,),
        in_specs=[_rows(ts, 1024), _const(1024, 1024), _rows(ts, 1024, P_GA // 1024), _rows(ts, 1024, P_GB // 1024),
                  _rows(ts, 1024), _rows(ts, 1024)],
        out_specs=[_rows(ts, 1024), _rows(ts, 1024), _rows(ts, 2048)],
        out_shape=[jax.ShapeDtypeStruct((s_, 1024), MXU_DTYPE)] * 2 + [jax.ShapeDtypeStruct((s_, 2048), MXU_DTYPE)],
        compiler_params=_cparams(("parallel",)),
    )(dx1b, wout, p, p, ta, tb)


FF_TILE = D_FF // 2


def _ffn_in_act(x1, g, wgu_t, *, tm=512):
    s_ = x1.shape[0]
    n = s_ // tm

    def body(x_ref, g_ref, w_ref, h_ref, gu_ref, a_ref):
        h = _rms(x_ref[...], g_ref[...]).astype(h_ref.dtype)
        h_ref[...] = h
        p = lax.dot_general(h, w_ref[...], NT, preferred_element_type=F32)
        gu_ref[...] = p
        gate = p[:, :FF_TILE]
        a_ref[...] = (gate * _sigmoid(gate) * p[:, FF_TILE:]).astype(a_ref.dtype)

    return pl.pallas_call(
        body, name="ffn_in", grid=(2, s_ // tm),
        in_specs=[pl.BlockSpec((tm, D_MODEL), lambda j, i: (i, 0)), pl.BlockSpec((1, D_MODEL), lambda j, i: (0, 0)),
                  pl.BlockSpec((2 * FF_TILE, D_MODEL), lambda j, i: (j, 0))],
        out_specs=[pl.BlockSpec((tm, D_MODEL), lambda j, i: (i + j * (n - 1 - i), 0)),
                   pl.BlockSpec((tm, 2 * FF_TILE), lambda j, i: (i, j)),
                   pl.BlockSpec((tm, FF_TILE), lambda j, i: (i, j))],
        out_shape=[jax.ShapeDtypeStruct((s_, D_MODEL), MXU_DTYPE), jax.ShapeDtypeStruct((s_, 2 * D_FF), F32),
                   jax.ShapeDtypeStruct((s_, D_FF), MXU_DTYPE)],
        compiler_params=_cparams(("arbitrary", "arbitrary")),
    )(x1, g, wgu_t)


def _d_act_swiglu(dx2b, wd, gu, *, tm=512):
    s_ = dx2b.shape[0]

    def body(d_ref, w_ref, gu_ref, o_ref):
        da = lax.dot_general(d_ref[...], w_ref[...], NT, preferred_element_type=F32)
        g, u = gu_ref[:, :FF_TILE], gu_ref[:, FF_TILE:]
        sg = _sigmoid(g)
        o_ref[:, :FF_TILE] = (da * u * (sg * (1.0 + g * (1.0 - sg)))).astype(o_ref.dtype)
        o_ref[:, FF_TILE:] = (da * (g * sg)).astype(o_ref.dtype)

    gu_spec = pl.BlockSpec((tm, 2 * FF_TILE), lambda j, i: (i, j))
    return pl.pallas_call(
        body, name="d_act", grid=(2, s_ // tm),
        in_specs=[pl.BlockSpec((tm, D_MODEL), lambda j, i: (i, 0)), pl.BlockSpec((FF_TILE, D_MODEL), lambda j, i: (j, 0)), gu_spec],
        out_specs=gu_spec, out_shape=jax.ShapeDtypeStruct((s_, 2 * D_FF), MXU_DTYPE),
        compiler_params=_cparams(("parallel", "parallel")),
    )(dx2b, wd, gu)


def _ffn_out_loss(act, wd, x1, g, tgt, *, ts=512):
    s_, c = x1.shape
    kk = act.shape[1]

    def body(a_ref, w_ref, x_ref, g_ref, t_ref, dx_ref, dxb_ref, dg_ref, lp_ref, tot_ref):
        v = x_ref[...] + jnp.dot(a_ref[...], w_ref[...], preferred_element_type=F32)
        r = lax.rsqrt(jnp.mean(v * v, axis=-1, keepdims=True) + EPS)
        xh = v * r
        gg = g_ref[...]
        e = xh * gg - t_ref[...]
        do = e * (1.0 / c)
        dxh = do * gg
        dx = r * (dxh - xh * jnp.mean(dxh * xh, axis=-1, keepdims=True))
        dx_ref[...] = dx
        dxb_ref[...] = dx.astype(dxb_ref.dtype)
        i = pl.program_id(0)

        @pl.when(i == 0)
        def _():
            dg_ref[...] = jnp.zeros(dg_ref.shape, F32)
            lp_ref[...] = jnp.zeros(lp_ref.shape, F32)

        dg_ref[...] += _sublane_sum(do * xh)
        lp_ref[...] += _sublane_sum(e * e)
        tot_ref[...] = jnp.full(tot_ref.shape, (0.5 / c) * jnp.sum(lp_ref[...]), F32)

    return pl.pallas_call(
        body, name="ffn_out_loss", grid=(s_ // ts,),
        in_specs=[_rows(ts, kk), _const(kk, c), _rows(ts, c), _const(1, c), _rows(ts, c)],
        out_specs=[_rows(ts, c), _rows(ts, c), _const(SUBLANES, c), _const(SUBLANES, c), _const(SUBLANES, LANES)],
        out_shape=[jax.ShapeDtypeStruct((s_, c), F32), jax.ShapeDtypeStruct((s_, c), MXU_DTYPE),
                   jax.ShapeDtypeStruct((SUBLANES, c), F32), jax.ShapeDtypeStruct((SUBLANES, c), F32),
                   jax.ShapeDtypeStruct((SUBLANES, LANES), F32)],
        compiler_params=_cparams(("arbitrary",)),
    )(act, wd, x1, g, tgt)


def _mla_d_out(dtb, wob, o32, *, ts=512):
    s_ = dtb.shape[0]

    def body(dt_ref, w_ref, o_ref, dob_ref, dl_ref):
        d = lax.dot_general(dt_ref[...], w_ref[...], NT, preferred_element_type=F32)
        dob_ref[...] = d.astype(dob_ref.dtype)
        prod = d * o_ref[...]
        for h in range(MLA_HEADS):
            dl_ref[h] = jnp.sum(prod[:, h * LANES:(h + 1) * LANES].T, axis=0, keepdims=True)

    return pl.pallas_call(
        body, name="mla_d_out", grid=(s_ // ts,), in_specs=[_rows(ts, 1024), _const(1024, 1024), _rows(ts, 1024)],
        out_specs=[_rows(ts, 1024), pl.BlockSpec((MLA_HEADS, 1, ts), lambda i: (0, 0, i))],
        out_shape=[jax.ShapeDtypeStruct((s_, 1024), MXU_DTYPE), jax.ShapeDtypeStruct((MLA_HEADS, 1, s_), F32)],
        compiler_params=_cparams(("parallel",)),
    )(dtb, wob, o32)


SWA_T = 4 * BLOCK


SWA_W = SWA_GROUP * BLOCK


def _swa_masks(sb):
    kr = lax.broadcasted_iota(jnp.int32, (2 * BLOCK, SWA_W), 0)
    qc = jnp.bitwise_and(lax.broadcasted_iota(jnp.int32, (2 * BLOCK, SWA_W), 1), BLOCK - 1)
    band = jnp.logical_and(kr > qc, kr <= qc + BLOCK)
    first = jnp.logical_and(band, kr >= BLOCK)
    return band, jnp.logical_or(first, jnp.logical_and(band, sb > 0))


def _heads_to_rows(ref, rs):
    return jnp.concatenate([ref[rs, h * LANES:(h + 1) * LANES] for h in range(SWA_GROUP)], axis=0)


def _sink_row(sk_ref):
    return jnp.concatenate([sk_ref[0, h:h + 1, :] for h in range(SWA_GROUP)], axis=1) * LOG2E


def _swa_in_specs(rev, nsb):
    sbi = (lambda j: nsb - 1 - j) if rev else (lambda j: j)
    cur = pl.BlockSpec((SWA_T, LANES), lambda g, j: (sbi(j), g))
    prev = pl.BlockSpec((BLOCK, LANES), lambda g, j: (jnp.maximum(4 * sbi(j) - 1, 0), g))
    q = pl.BlockSpec((SWA_T, SWA_GROUP * LANES), lambda g, j: (sbi(j), g))
    sink = pl.BlockSpec((1, SUBLANES, LANES), lambda g, j: (g, 0, 0))
    lse = pl.BlockSpec((SWA_GROUP, 1, SWA_T), lambda g, j: (g, 0, sbi(j)))
    return q, cur, prev, sink, lse


def _swa_fwd(qa, ka, va, sink_b):
    s_ = qa.shape[0]
    nsb = s_ // SWA_T
    c2 = HEAD_DIM ** -0.5 * LOG2E

    def body(q_ref, kc_ref, kp_ref, vc_ref, vp_ref, sk_ref, o32_ref, o16_ref, lse_ref, kx, vx):
        kx[0:BLOCK, :] = kp_ref[...]
        kx[BLOCK:5 * BLOCK, :] = kc_ref[...]
        vx[0:BLOCK, :] = vp_ref[...]
        vx[BLOCK:5 * BLOCK, :] = vc_ref[...]
        band, band0 = _swa_masks(pl.program_id(1))
        sink2 = _sink_row(sk_ref)
        for b in range(4):
            rs = slice(b * BLOCK, (b + 1) * BLOCK)
            ks = slice(b * BLOCK, (b + 2) * BLOCK)
            st = lax.dot_general(kx[ks, :], _heads_to_rows(q_ref, rs), NT, preferred_element_type=F32) * c2
            st = jnp.where(band0 if b == 0 else band, st, -jnp.inf)
            m = jnp.maximum(jnp.max(st, axis=0, keepdims=True), sink2)
            pt = jnp.exp2(st - m)
            den = jnp.sum(pt, axis=0, keepdims=True) + jnp.exp2(sink2 - m)
            o = lax.dot_general((pt * (1.0 / den)).astype(MXU_DTYPE), vx[ks, :], TN, preferred_element_type=F32)
            lse = m + jnp.log2(den)
            for hh in range(SWA_GROUP):
                cs = slice(hh * LANES, (hh + 1) * LANES)
                o32_ref[rs, cs] = o[cs, :]
                o16_ref[rs, cs] = o[cs, :].astype(o16_ref.dtype)
                lse_ref[hh, :, rs] = lse[:, cs]

    q, cur, prev, sink, lse_spec = _swa_in_specs(False, nsb)
    return pl.pallas_call(
        body, name="swa_fwd", grid=(SWA_KV_HEADS, nsb), in_specs=[q, cur, prev, cur, prev, sink],
        out_specs=[q, q, lse_spec],
        out_shape=[jax.ShapeDtypeStruct((s_, SWA_HEADS * LANES), F32), jax.ShapeDtypeStruct((s_, SWA_HEADS * LANES), MXU_DTYPE),
                   jax.ShapeDtypeStruct((SWA_HEADS, 1, s_), F32)],
        scratch_shapes=[pltpu.VMEM((5 * BLOCK, LANES), MXU_DTYPE), pltpu.VMEM((5 * BLOCK, LANES), MXU_DTYPE)],
        compiler_params=_cparams(("parallel", "arbitrary")),
    )(qa, ka, ka, va, va, sink_b)


def _swa_bwd(qa, ka, va, sink_b, o32, do, lse):
    s_ = qa.shape[0]
    nsb = s_ // SWA_T
    scale = HEAD_DIM ** -0.5
    c2 = scale * LOG2E

    def body(q_ref, kc_ref, kp_ref, vc_ref, vp_ref, sk_ref, o_ref, do_ref, lse_ref,
             dq_ref, dk_ref, dv_ref, dsk_ref, kx, vx, kacc, vacc, kcar, vcar):
        j = pl.program_id(1)
        kx[0:BLOCK, :] = kp_ref[...]
        kx[BLOCK:5 * BLOCK, :] = kc_ref[...]
        vx[0:BLOCK, :] = vp_ref[...]
        vx[BLOCK:5 * BLOCK, :] = vc_ref[...]
        band, band0 = _swa_masks(nsb - 1 - j)
        kacc[...] = jnp.zeros(kacc.shape, F32)
        vacc[...] = jnp.zeros(vacc.shape, F32)

        @pl.when(j == 0)
        def _():
            kcar[...] = jnp.zeros(kcar.shape, F32)
            vcar[...] = jnp.zeros(vcar.shape, F32)
            dsk_ref[...] = jnp.zeros(dsk_ref.shape, F32)

        sink2 = _sink_row(sk_ref)
        dsink = jnp.zeros((1, SWA_W), F32)
        for b in range(4):
            rs = slice(b * BLOCK, (b + 1) * BLOCK)
            ks = slice(b * BLOCK, (b + 2) * BLOCK)
            q, k2, v2 = _heads_to_rows(q_ref, rs), kx[ks, :], vx[ks, :]
            d = _heads_to_rows(do_ref, rs)
            delta = jnp.sum((d * _heads_to_rows(o_ref, rs)).T, axis=0, keepdims=True)
            l2 = jnp.concatenate([lse_ref[hh, :, rs] for hh in range(SWA_GROUP)], axis=1)
            st = lax.dot_general(k2, q, NT, preferred_element_type=F32) * c2
            pt = jnp.exp2(jnp.where(band0 if b == 0 else band, st, -jnp.inf) - l2)
            db = d.astype(MXU_DTYPE)
            dst = (pt * (lax.dot_general(v2, db, NT, preferred_element_type=F32) - delta) * scale).astype(MXU_DTYPE)
            dq = lax.dot_general(dst, k2, TN, preferred_element_type=F32)
            for hh in range(SWA_GROUP):
                dq_ref[rs, hh * LANES:(hh + 1) * LANES] = dq[hh * LANES:(hh + 1) * LANES, :]
            kacc[ks, :] += jnp.dot(dst, q, preferred_element_type=F32)
            vacc[ks, :] += jnp.dot(pt.astype(MXU_DTYPE), db, preferred_element_type=F32)
            dsink = dsink - jnp.exp2(sink2 - l2) * delta
        for hh in range(SWA_GROUP):
            tot = jnp.sum(dsink[:, hh * LANES:(hh + 1) * LANES], axis=1, keepdims=True)
            dsk_ref[0, hh:hh + 1, :] += jnp.broadcast_to(tot, (1, LANES))

        dk_ref[0:3 * BLOCK, :] = kacc[BLOCK:4 * BLOCK, :]
        dk_ref[3 * BLOCK:4 * BLOCK, :] = kacc[4 * BLOCK:5 * BLOCK, :] + kcar[...]
        dv_ref[0:3 * BLOCK, :] = vacc[BLOCK:4 * BLOCK, :].astype(dv_ref.dtype)
        dv_ref[3 * BLOCK:4 * BLOCK, :] = (vacc[4 * BLOCK:5 * BLOCK, :] + vcar[...]).astype(dv_ref.dtype)
        kcar[...] = kacc[0:BLOCK, :]
        vcar[...] = vacc[0:BLOCK, :]

    q, cur, prev, sink, lse_spec = _swa_in_specs(True, nsb)
    return pl.pallas_call(
        body, name="swa_bwd", grid=(SWA_KV_HEADS, nsb),
        in_specs=[q, cur, prev, cur, prev, sink, q, q, lse_spec],
        out_specs=[q, cur, cur, sink],
        out_shape=[jax.ShapeDtypeStruct((s_, SWA_HEADS * LANES), F32), jax.ShapeDtypeStruct((s_, SWA_KV_HEADS * LANES), F32),
                   jax.ShapeDtypeStruct((s_, SWA_KV_HEADS * LANES), MXU_DTYPE),
                   jax.ShapeDtypeStruct((SWA_KV_HEADS, SUBLANES, LANES), F32)],
        scratch_shapes=[pltpu.VMEM((5 * BLOCK, LANES), MXU_DTYPE), pltpu.VMEM((5 * BLOCK, LANES), MXU_DTYPE),
                        pltpu.VMEM((5 * BLOCK, LANES), F32), pltpu.VMEM((5 * BLOCK, LANES), F32),
                        pltpu.VMEM((BLOCK, LANES), F32), pltpu.VMEM((BLOCK, LANES), F32)],
        compiler_params=_cparams(("arbitrary", "arbitrary")),
    )(qa, ka, ka, va, va, sink_b, o32, do, lse)


MLA_T = 512
MLA_FWD_GROUP = 4
MLA_BWD_GROUP = 2


def _mla_specs(s_, t, group):
    w = group * LANES
    qs = pl.BlockSpec((t, w), lambda g, i: (i, g))
    kv = pl.BlockSpec((s_, w), lambda g, i: (0, g))
    row = pl.BlockSpec((group, 1, t), lambda g, i: (g, 0, i))
    return qs, kv, row


def _causal_scores_t(k, q, t, c2, masked):
    st = lax.dot_general(k, q, NT, preferred_element_type=F32) * c2
    if masked:
        kr = lax.broadcasted_iota(jnp.int32, (t, t), 0)
        qc = lax.broadcasted_iota(jnp.int32, (t, t), 1)
        st = jnp.where(kr <= qc, st, -jnp.inf)
    return st


def _mla_fwd(qc, kc, vp):
    s_ = qc.shape[0]
    t = min(MLA_T, s_)
    c2 = MLA_QK ** -0.5 * LOG2E
    grp = MLA_FWD_GROUP

    def body(q_ref, k_ref, v_ref, o32_ref, o16_ref, lse_ref, m_s, acc_s):
        qi = pl.program_id(1)
        m_s[...] = jnp.full(m_s.shape, -jnp.inf, F32)
        acc_s[...] = jnp.zeros(acc_s.shape, F32)
        ones_lane = lax.broadcasted_iota(jnp.int32, (t, LANES), 1) == MLA_V

        def step(ki, masked):
            off = pl.multiple_of(ki * t, t)
            for g in range(grp):
                cs = slice(g * LANES, (g + 1) * LANES)
                st = _causal_scores_t(k_ref[pl.ds(off, t), cs], q_ref[:, cs], t, c2, masked)
                m_old = m_s[g]
                m_new = jnp.maximum(m_old, jnp.max(st, axis=0, keepdims=True))
                alpha = jnp.exp2(m_old - m_new)
                pt = jnp.exp2(st - m_new).astype(MXU_DTYPE)
                v = v_ref[pl.ds(off, t), cs]
                v = jnp.where(ones_lane, jnp.ones((), v.dtype), v)
                acc_s[g] = alpha * acc_s[g] + lax.dot_general(v, pt, TN, preferred_element_type=F32)
                m_s[g] = m_new

        def full_block(ki, carry):
            step(ki, False)
            return carry

        lax.fori_loop(0, qi, full_block, 0)
        step(qi, True)
        for g in range(grp):
            cs = slice(g * LANES, (g + 1) * LANES)
            acc = acc_s[g]
            l = acc[MLA_V:MLA_V + 1, :]
            o = (acc * (1.0 / l)).T
            o32_ref[:, cs] = o
            o16_ref[:, cs] = o.astype(o16_ref.dtype)
            lse_ref[g] = m_s[g] + jnp.log2(l)

    qs, kv, row = _mla_specs(s_, t, grp)
    return pl.pallas_call(
        body, name="mla_fwd", grid=(MLA_HEADS // grp, s_ // t), in_specs=[qs, kv, kv], out_specs=[qs, qs, row],
        out_shape=[jax.ShapeDtypeStruct((s_, MLA_HEADS * LANES), F32), jax.ShapeDtypeStruct((s_, MLA_HEADS * LANES), MXU_DTYPE),
                   jax.ShapeDtypeStruct((MLA_HEADS, 1, s_), F32)],
        scratch_shapes=[pltpu.VMEM((grp, 1, t), F32), pltpu.VMEM((grp, LANES, t), F32)],
        compiler_params=_cparams(("parallel", "arbitrary")),
    )(qc, kc, vp)


def _mla_bwd(qc, kc, vp, dob, lse, delta):
    s_ = qc.shape[0]
    t = min(MLA_T, s_)
    scale = MLA_QK ** -0.5
    c2 = scale * LOG2E
    grp = MLA_BWD_GROUP

    def body(q_ref, do_ref, lse_ref, dl_ref, k_ref, v_ref, dq_ref, dk_ref, dv_ref, dqt_s):
        qi = pl.program_id(1)

        @pl.when(qi == 0)
        def _():
            dk_ref[...] = jnp.zeros(dk_ref.shape, F32)
            dv_ref[...] = jnp.zeros(dv_ref.shape, F32)

        dqt_s[...] = jnp.zeros(dqt_s.shape, F32)

        def step(ki, masked):
            off = pl.multiple_of(ki * t, t)
            for g in range(grp):
                cs = slice(g * LANES, (g + 1) * LANES)
                q, d, k = q_ref[:, cs], do_ref[:, cs], k_ref[pl.ds(off, t), cs]
                pt = jnp.exp2(_causal_scores_t(k, q, t, c2, masked) - lse_ref[g])
                dpt = lax.dot_general(v_ref[pl.ds(off, t), cs], d, NT, preferred_element_type=F32)
                dst = (pt * (dpt - dl_ref[g]) * scale).astype(MXU_DTYPE)
                dv_ref[pl.ds(off, t), cs] += jnp.dot(pt.astype(MXU_DTYPE), d, preferred_element_type=F32)
                dk_ref[pl.ds(off, t), cs] += jnp.dot(dst, q, preferred_element_type=F32)
                dqt_s[g] += lax.dot_general(k, dst, TN, preferred_element_type=F32)

        def full_block(ki, carry):
            step(ki, False)
            return carry

        lax.fori_loop(0, qi, full_block, 0)
        step(qi, True)
        for g in range(grp):
            dq_ref[:, g * LANES:(g + 1) * LANES] = dqt_s[g].T

    qs, kv, row = _mla_specs(s_, t, grp)
    shp = jax.ShapeDtypeStruct((s_, MLA_HEADS * LANES), F32)
    return pl.pallas_call(
        body, name="mla_bwd", grid=(MLA_HEADS // grp, s_ // t), in_specs=[qs, qs, row, row, kv, kv],
        out_specs=[qs, kv, kv], out_shape=[shp, shp, shp], scratch_shapes=[pltpu.VMEM((grp, LANES, t), F32)],
        compiler_params=_cparams(("parallel", "arbitrary")),
    )(qc, dob, lse, delta, kc, vp)


def _pad_heads(w, nh, hd, axis):
    shp = w.shape
    w = w.reshape(shp[:axis] + (nh, hd) + shp[axis + 1:])
    pad = [(0, 0)] * w.ndim
    pad[axis + 1] = (0, LANES - hd)
    w = jnp.pad(w, pad)
    return w.reshape(shp[:axis] + (nh * LANES,) + shp[axis + 1:])


def _unpad_heads(w, nh, hd, axis):
    shp = w.shape
    w = w.reshape(shp[:axis] + (nh, LANES) + shp[axis + 1:])
    w = lax.slice_in_dim(w, 0, hd, axis=axis + 1)
    return w.reshape(shp[:axis] + (nh * hd,) + shp[axis + 1:])


PACK_W = 1024
ROW_TILE = 16
FULL_SHAPE = dict(w_in=(1024, 3488), w_uq=(384, 768), w_ukv=(256, 1024), w_o_swa=(512, 1024), w_o_mla=(512, 1024),
                  w_out=(1024, 1024), w_gate=(1024, 2816), w_up=(1024, 2816), w_down=(2816, 1024))
BIG = tuple(FULL_SHAPE)
ROW_SHARDED = ("w_out", "w_down")
W_IN_COLS = FULL_SHAPE["w_in"][1] // N_DEV
W_IN_ROWS = -(-W_IN_COLS // ROW_TILE) * ROW_TILE
FF_COLS = D_FF // N_DEV
OUT_ROWS = D_MODEL // N_DEV
SMALL_ROW0 = W_IN_ROWS + OUT_ROWS
SMALL_FLAT = (("w_uq", 0, 36), ("w_ukv", 48, 32), ("w_o_swa", 80, 64), ("w_o_mla", 144, 64))
SMALL_ROWS = 208
EARLY_ROWS = SMALL_ROW0 + SMALL_ROWS
LATE_ROWS = 3 * FF_COLS
PACK_ROWS = EARLY_ROWS + LATE_ROWS


def _shard_shape(n):
    r, c = FULL_SHAPE[n]
    return (r // N_DEV, c) if n in ROW_SHARDED else (r, c // N_DEV)


def _wire_pack(sh, dtype):
    c = lambda n: sh[n].astype(dtype)
    rows = [jnp.pad(c("w_in").T, ((0, W_IN_ROWS - W_IN_COLS), (0, 0))), c("w_out")]
    for n, _, r in SMALL_FLAT:
        rows.append(jnp.pad(c(n).reshape(r, PACK_W), ((0, -r % ROW_TILE), (0, 0))))
    return jnp.concatenate(rows + [c("w_gate").T, c("w_up").T, c("w_down")], 0)


MID_ROWS = OUT_ROWS + SMALL_ROWS


def _mid_unpack(p):
    out = dict(w_out=p[0:OUT_ROWS])
    for n, off, r in SMALL_FLAT:
        out[n] = p[OUT_ROWS + off:OUT_ROWS + off + r].reshape(_shard_shape(n))
    return out


def _w_in_row_maps():
    sp = lambda col: (col // W_IN_COLS) * W_IN_ROWS + col % W_IN_COLS
    fwd = np.full((P_W,), -1, np.int64)

    def put(t0, c0, n):
        fwd[t0:t0 + n] = [sp(c) for c in range(c0, c0 + n)]

    put(P_GA, IN_OFF[6], D_MODEL)
    put(P_GB, IN_OFF[7], D_MODEL)
    for h in range(SWA_HEADS):
        put(P_Q + LANES * h, IN_OFF[0] + HEAD_DIM * h, HEAD_DIM)
    put(P_QLAT, IN_OFF[3], Q_LORA)
    put(P_KR + KR_LANE, IN_OFF[5], MLA_ROPE)
    for h in range(SWA_KV_HEADS):
        put(P_K + LANES * h, IN_OFF[1] + HEAD_DIM * h, HEAD_DIM)
        put(P_V + LANES * h, IN_OFF[2] + HEAD_DIM * h, HEAD_DIM)
    put(P_KVLAT, IN_OFF[4], KV_LORA)
    inv = np.full((N_DEV * W_IN_ROWS,), -1, np.int64)
    inv[fwd[fwd >= 0]] = np.nonzero(fwd >= 0)[0]
    return fwd, inv


def _take_rows(src, idx, *, name):
    n_out, n_src, width = len(idx), src.shape[0], src.shape[1]
    assert n_out % BLOCK == 0 and n_src % BLOCK == 0
    n_tiles = n_out // BLOCK
    blocks = [sorted({int(v) // BLOCK for v in idx[i * BLOCK:(i + 1) * BLOCK] if v >= 0}) for i in range(n_tiles)]
    k_max = max(1, max(len(b) for b in blocks))
    tab = np.zeros((n_tiles, k_max), np.int32)
    sel = np.zeros((n_tiles, k_max, BLOCK, BLOCK), np.float32)
    for i, blks in enumerate(blocks):
        for m, b in enumerate(blks):
            tab[i, m] = b
            for r in range(BLOCK):
                v = int(idx[i * BLOCK + r])
                if v >= 0 and v // BLOCK == b:
                    sel[i, m, r, v % BLOCK] = 1.0

    def body(tab_ref, sel_ref, *refs):
        o_ref = refs[k_max]
        acc = jnp.dot(sel_ref[0, 0], refs[0][...], preferred_element_type=F32)
        for m in range(1, k_max):
            acc = acc + jnp.dot(sel_ref[0, m], refs[m][...], preferred_element_type=F32)
        o_ref[...] = acc.astype(o_ref.dtype)

    def src_spec(m):
        return pl.BlockSpec((BLOCK, width), lambda i, t: (t[i * k_max + m], 0))

    return pl.pallas_call(
        body, name=name,
        grid_spec=pltpu.PrefetchScalarGridSpec(
            num_scalar_prefetch=1, grid=(n_tiles,),
            in_specs=[pl.BlockSpec((1, k_max, BLOCK, BLOCK), lambda i, t: (i, 0, 0, 0))] + [src_spec(m) for m in range(k_max)],
            out_specs=pl.BlockSpec((BLOCK, width), lambda i, t: (i, 0))),
        out_shape=jax.ShapeDtypeStruct((n_out, width), src.dtype),
        compiler_params=_cparams(("parallel",)),
    )(jnp.asarray(tab.reshape(-1)), jnp.asarray(sel, src.dtype), *([src] * k_max))


def _w_in_operand(win_g):
    return _take_rows(win_g.reshape(N_DEV * W_IN_ROWS, PACK_W), _w_in_row_maps()[0], name="w_in_rows")


def _mid_operands(wout_g, small_g):
    def full(n, off, r):
        a = small_g[:, off:off + r].reshape((N_DEV,) + _shard_shape(n))
        return jnp.moveaxis(a, 0, 1).reshape(FULL_SHAPE[n])

    w = {n: full(n, off, r) for n, off, r in SMALL_FLAT}
    ukv = w["w_ukv"].reshape(KV_LORA, MLA_HEADS, MLA_NOPE + MLA_V)
    return dict(
        wout=wout_g.reshape(D_MODEL, D_MODEL),
        wuq=_pad_heads(w["w_uq"], MLA_HEADS, MLA_QK, 1),
        wuk=_pad_heads(ukv[:, :, :MLA_NOPE].reshape(KV_LORA, -1), MLA_HEADS, MLA_NOPE, 1),
        wuv=_pad_heads(ukv[:, :, MLA_NOPE:].reshape(KV_LORA, -1), MLA_HEADS, MLA_V, 1),
        woa=_pad_heads(w["w_o_swa"], SWA_HEADS, HEAD_DIM, 0),
        wob=_pad_heads(w["w_o_mla"], MLA_HEADS, MLA_V, 0),
    )


def _mid_grad_pack(g):
    uk = _unpad_heads(g["wukv"][:, :1024], MLA_HEADS, MLA_NOPE, 1).reshape(KV_LORA, MLA_HEADS, MLA_NOPE)
    uv = _unpad_heads(g["wukv"][:, 1024:], MLA_HEADS, MLA_V, 1).reshape(KV_LORA, MLA_HEADS, MLA_V)
    w = dict(w_uq=_unpad_heads(g["wuq"], MLA_HEADS, MLA_QK, 1), w_ukv=jnp.concatenate([uk, uv], 2).reshape(KV_LORA, -1),
             w_o_swa=_unpad_heads(g["woa"], SWA_HEADS, HEAD_DIM, 0), w_o_mla=_unpad_heads(g["wob"], MLA_HEADS, MLA_V, 0))

    def flat(n, r):
        rr, cc = FULL_SHAPE[n]
        a = jnp.moveaxis(w[n].reshape(rr, N_DEV, cc // N_DEV), 1, 0).reshape(N_DEV, r, PACK_W)
        return jnp.pad(a, ((0, 0), (0, -r % ROW_TILE), (0, 0))).astype(WIRE_DTYPE)

    return jnp.concatenate([g["wout"].reshape(N_DEV, OUT_ROWS, PACK_W)] + [flat(n, r) for n, _, r in SMALL_FLAT], 1)


def _w_in_grad_chunks(g_win_t):
    return _take_rows(g_win_t, _w_in_row_maps()[1], name="dw_in_rows").reshape(N_DEV, W_IN_ROWS, PACK_W)


def _local_step(x, tgt, win_t, small, weights, grads):
    s_ = x.shape[0]
    tabs = _rope_tables(s_)
    sink_b = jnp.broadcast_to(small["swa_sinks"].reshape(SWA_KV_HEADS, SWA_GROUP, 1), (SWA_KV_HEADS, SWA_GROUP, LANES))
    sink_b = jnp.pad(sink_b, ((0, 0), (0, SUBLANES - SWA_GROUP), (0, 0)))

    h, p = _norm_mm(x, small["mix_norm_g"], win_t, name="proj_in", tn=2176, tm=1024)
    qa, ka, va, cq, ckv, kro = _attn_prep(p, small["q_norm_g"], small["kv_norm_g"], tabs)
    ops = weights.mid(cq)
    oa32, oa16, lse_a = _swa_fwd(qa, ka, va, sink_b)
    qc, kc, vp = _mla_up(cq, ckv, kro, ops["wuq"], ops["wuk"], ops["wuv"], tabs)
    ob32, ob16, lse_b = _mla_fwd(qc, kc, vp)
    ta, tb, y = _attn_out_gate(oa16, ob16, ops["woa"], ops["wob"], p)
    x1 = _mm(y, ops["wout"], "nn", name="out_proj", add=x, tm=1024, tn=1024)
    wgu_t, wd = weights.late(x1)
    h2, gu, act = _ffn_in_act(x1, small["ffn_norm_g"], wgu_t)

    dx2, dx2b, dg3, _, tot = _ffn_out_loss(act, wd, x1, small["final_norm_g"].reshape(1, D_MODEL), tgt)
    g = {}
    g_wd = _mm(act, dx2b, "tn", name="dw_down", tm=FF_TILE, tn=1024, tk=2048, out_dtype=WIRE_DTYPE)
    dgu = _d_act_swiglu(dx2b, wd, gu)
    g_wgu = _mm(dgu, h2, "tn", name="dw_ffn_in", tm=FF_TILE, tn=1024, tk=2048, out_dtype=WIRE_DTYPE)
    token = grads.late(g_wgu, g_wd)
    dx1, dx1b, dg2 = _mm_norm_bwd(dgu, wgu_t, x1, small["ffn_norm_g"] + token[0:1, 0:1], dx2, name="d_h2")
    g["wout"] = _mm(y, dx1b, "tn", name="dw_out", tm=1024, tn=1024, tk=1024, out_dtype=WIRE_DTYPE)
    dta, dtb, dgab = _d_y_gate(dx1b, ops["wout"], p, ta, tb)
    doa = _mm(dta, ops["woa"], "nt", name="d_oa", tm=1024, tn=1024)
    g["woa"] = _mm(oa16, dta, "tn", name="dw_o_swa", tm=1024, tn=1024, tk=1024)
    g["wob"] = _mm(ob16, dtb, "tn", name="dw_o_mla", tm=1024, tn=1024, tk=1024)
    dob16, delta_b = _mla_d_out(dtb, ops["wob"], ob32)
    dqc, dkc, dvp = _mla_bwd(qc, kc, vp, dob16, lse_b, delta_b)
    dqp, dkv, dkr, dqlat, dkvlat, dgq, dgkv = _mla_up_bwd(
        dqc, dkc, dvp, ops["wuq"], jnp.concatenate([ops["wuk"], ops["wuv"]], 1), p, small["q_norm_g"], small["kv_norm_g"], tabs)
    g["wuq"] = _mm(cq, dqp, "tn", name="dw_uq", tm=Q_LORA, tn=1024, tk=512)
    g["wukv"] = _mm(ckv, dkv, "tn", name="dw_ukv", tm=KV_LORA, tn=1024, tk=512)
    token = grads.mid(g)
    dqa, dka, dva, dsk = _swa_bwd(qa, ka, va, sink_b + token[0:1, 0:1], oa32, doa, lse_a)
    dp = _assemble_dp(dgab, dqa, dqlat, dkr, dka, dva, dkvlat, tabs)
    token = grads.last(_mm(dp, h, "tn", name="dw_in", tm=2176, tn=1024, tk=1024, out_dtype=WIRE_DTYPE))
    gx, _, dg1 = _mm_norm_bwd(dp, win_t, x, small["mix_norm_g"], dx1, name="d_h", after=token)

    sm = dict(mix_norm_g=dg1, ffn_norm_g=dg2, final_norm_g=dg3, q_norm_g=dgq, kv_norm_g=dgkv,
              swa_sinks=dsk[:, :SWA_GROUP, 0].reshape(1, SWA_HEADS))
    return tot, gx, sm


MESH = pl.DeviceIdType.MESH
ANY = pl.BlockSpec(memory_space=pl.ANY)


def _position():
    return lax.axis_index("x"), lax.axis_index("y"), lax.axis_index("c")


def _all_gather(block, pieces, shapes, *, name):
    n_out = len(shapes)
    n_rows = sum(p[3] for p in pieces)

    def body(x_ref, *refs):
        outs, (send_sems, recv_sems, local_sem) = refs[:n_out], refs[n_out:]
        x, y, c = _position()
        me, sibling = (x, y, c), (x, y, 1 - c)
        chips = [(1 - x, y), (x, 1 - y), (1 - x, 1 - y)]

        def dst(piece, blk):
            arr, lead, _, _ = piece
            return outs[arr].at[lead(4 * blk[0] + 2 * blk[1] + blk[2])]

        def own(piece):
            return x_ref.at[pl.ds(piece[2], piece[3])]

        def copies(k, blk, to, from_input):
            return [pltpu.make_async_remote_copy(
                src_ref=own(p) if from_input else dst(p, blk), dst_ref=dst(p, blk), send_sem=send_sems.at[k],
                recv_sem=recv_sems.at[k], device_id=to, device_id_type=MESH) for p in pieces]

        gathered_rows = x_ref.at[pl.ds(0, n_rows)]

        def whole_block(k):
            return pltpu.make_async_remote_copy(src_ref=gathered_rows, dst_ref=gathered_rows, send_sem=send_sems.at[k],
                                                recv_sem=recv_sems.at[k], device_id=me, device_id_type=MESH)

        for p in pieces:
            pltpu.make_async_copy(own(p), dst(p, me), local_sem).start()
        for cp in copies(0, me, sibling, True):
            cp.start()
        for j, chip in enumerate(chips):
            for cp in copies(1 + j, me, (*chip, c), True):
                cp.start()
        for j, chip in enumerate(chips):
            whole_block(1 + j).wait_recv()
            for cp in copies(4 + j, (*chip, c), sibling, False):
                cp.start()
        whole_block(0).wait_recv()
        for j in range(3):
            whole_block(4 + j).wait_recv()
        for k in range(7):
            whole_block(k).wait_send()
        pltpu.make_async_copy(gathered_rows, gathered_rows, local_sem).wait()

    return pl.pallas_call(
        body, name=name, out_shape=[jax.ShapeDtypeStruct(s, block.dtype) for s in shapes], in_specs=[ANY],
        out_specs=[ANY] * n_out,
        scratch_shapes=[pltpu.SemaphoreType.DMA((7,)), pltpu.SemaphoreType.DMA((7,)), pltpu.SemaphoreType.DMA],
    )(block)


HBM = pl.BlockSpec(memory_space=pltpu.HBM)
SEM = pl.BlockSpec(memory_space=pltpu.SEMAPHORE)
TILE_DEVS = FF_TILE // FF_COLS
GU_SHAPE = (2, 2, TILE_DEVS, FF_COLS, PACK_W)


def _gate_slab(d):
    return (d // TILE_DEVS, 0, d % TILE_DEVS)


def _up_slab(d):
    return (d // TILE_DEVS, 1, d % TILE_DEVS)
D_SHAPE = (N_DEV, FF_COLS, PACK_W)
LAND_SHAPE = (N_DEV, LATE_ROWS, PACK_W)


def _split_params():
    return pltpu.CompilerParams(has_side_effects=pltpu.SideEffectType.DATAFLOW_SIDE_EFFECTING)


def _peer(x, y, c, k):
    return ((1 - x) if k & 4 else x, (1 - y) if k & 2 else y, (1 - c) if k & 1 else c)


def _empty_hbm(shape, dtype):
    return pltpu.with_memory_space_constraint(lax.empty(shape, dtype), pltpu.HBM)


def _wait_all(rows, send_sems, recv_sems, me):
    for k in range(N_DEV - 1):
        cp = pltpu.make_async_remote_copy(src_ref=rows, dst_ref=rows, send_sem=send_sems.at[k], recv_sem=recv_sems.at[k],
                                          device_id=me, device_id_type=MESH)
        cp.wait_send()
        cp.wait_recv()


def _token_shape():
    return jax.ShapeDtypeStruct((SUBLANES, LANES), F32)


def _gather_start(pack, row0, pieces, shapes, *, name):
    n = len(shapes)

    def body(*refs):
        p_ref, bufs, send_sems, recv_sems, token = refs[0], refs[1:1 + n], refs[1 + n], refs[2 + n], refs[-1]
        x, y, c = _position()
        me = 4 * x + 2 * y + c
        for k in range(1, N_DEV):
            off = row0
            for buf, lead, rows in pieces:
                pltpu.make_async_remote_copy(
                    src_ref=p_ref.at[pl.ds(off, rows)], dst_ref=bufs[buf].at[lead(me)], send_sem=send_sems.at[k - 1],
                    recv_sem=recv_sems.at[k - 1], device_id=_peer(x, y, c, k), device_id_type=MESH).start()
                off += rows
        token[...] = jnp.zeros_like(token)

    sems, dt = pltpu.SemaphoreType.DMA((N_DEV - 1,)), pack.dtype
    return pl.pallas_call(
        body, name=name,
        out_shape=(sems, sems, pltpu.HBM(pack.shape, dt)) + tuple(pltpu.HBM(s, dt) for s in shapes) + (_token_shape(),),
        in_specs=(HBM,) * (1 + n), out_specs=(SEM, SEM) + (HBM,) * (1 + n) + (pl.BlockSpec(memory_space=pltpu.VMEM),),
        input_output_aliases={i: 2 + i for i in range(1 + n)}, compiler_params=_split_params(),
    )(pltpu.with_memory_space_constraint(pack, pltpu.HBM), *[_empty_hbm(s, dt) for s in shapes])


def _gather_wait(started, row0, n_rows, after, *, name):
    send_sems, recv_sems, pack, *bufs = started[:-1]
    n = len(bufs)

    def body(*refs):
        _wait_all(refs[0].at[pl.ds(row0, n_rows)], refs[1 + n], refs[2 + n], _position())

    outs = pl.pallas_call(
        body, name=name, out_shape=tuple(pltpu.HBM(a.shape, a.dtype) for a in (pack, *bufs)),
        in_specs=(HBM,) * (1 + n) + (SEM, SEM, ANY), out_specs=(HBM,) * (1 + n),
        input_output_aliases={i: i for i in range(1 + n)}, compiler_params=_split_params(),
    )(pack, *bufs, send_sems, recv_sems, after)
    return outs[0], outs[1:]


def _scatter_start(srcs, pieces, *, name):
    n = len(srcs)
    land_shape = (N_DEV, sum(p[2] for p in pieces), PACK_W)

    def body(*refs):
        src_refs, land_ref, send_sems, recv_sems, token = refs[:n], refs[n], refs[n + 1], refs[n + 2], refs[-1]
        x, y, c = _position()
        me = 4 * x + 2 * y + c
        for k in range(1, N_DEV):
            px, py, pc = _peer(x, y, c, k)
            off = 0
            for si, lead, rows in pieces:
                pltpu.make_async_remote_copy(
                    src_ref=src_refs[si].at[lead(4 * px + 2 * py + pc)], dst_ref=land_ref.at[me, pl.ds(off, rows)],
                    send_sem=send_sems.at[k - 1], recv_sem=recv_sems.at[k - 1], device_id=(px, py, pc),
                    device_id_type=MESH).start()
                off += rows
        token[...] = jnp.zeros_like(token)

    sems, dt = pltpu.SemaphoreType.DMA((N_DEV - 1,)), srcs[0].dtype
    return pl.pallas_call(
        body, name=name,
        out_shape=(sems, sems) + tuple(pltpu.HBM(a.shape, dt) for a in srcs) + (pltpu.HBM(land_shape, dt), _token_shape()),
        in_specs=(HBM,) * (n + 1), out_specs=(SEM, SEM) + (HBM,) * (n + 1) + (pl.BlockSpec(memory_space=pltpu.VMEM),),
        input_output_aliases={i: 2 + i for i in range(n + 1)}, compiler_params=_split_params(),
    )(*[pltpu.with_memory_space_constraint(a, pltpu.HBM) for a in srcs], _empty_hbm(land_shape, dt))


def _scatter_wait(started, after, *, name):
    send_sems, recv_sems, *bufs = started[:-1]
    n = len(bufs)

    def body(*refs):
        _wait_all(refs[n - 1].at[0], refs[n], refs[n + 1], _position())

    return pl.pallas_call(
        body, name=name, out_shape=tuple(pltpu.HBM(a.shape, a.dtype) for a in bufs),
        in_specs=(HBM,) * n + (SEM, SEM, ANY), out_specs=(HBM,) * n, input_output_aliases={i: i for i in range(n)},
        compiler_params=_split_params(),
    )(*bufs, send_sems, recv_sems, after)


def _peer_sum(own, own_lead, land, block, rows, idx, *, name):
    lead_rank = own.ndim - 2

    def body(idx_ref, own_ref, *refs):
        o_ref = refs[N_DEV - 1]
        acc = own_ref[(0,) * lead_rank].astype(F32)
        for k in range(N_DEV - 1):
            acc = acc + refs[k][0].astype(F32)
        o_ref[...] = acc

    own_spec = pl.BlockSpec((1,) * lead_rank + (rows, PACK_W), lambda i, t: own_lead(t[0]) + (0, 0))

    def land_spec(k):
        return pl.BlockSpec((1, rows, PACK_W), lambda i, t: (t[k + 1], block, 0))

    return pl.pallas_call(
        body, name=name,
        grid_spec=pltpu.PrefetchScalarGridSpec(
            num_scalar_prefetch=1, grid=(1,), in_specs=[own_spec] + [land_spec(k) for k in range(N_DEV - 1)],
            out_specs=pl.BlockSpec((rows, PACK_W), lambda i, t: (0, 0))),
        out_shape=jax.ShapeDtypeStruct((rows, PACK_W), F32), compiler_params=_cparams(("arbitrary",)),
    )(idx, own, *([land] * (N_DEV - 1)))


def _adamw(w, g, m, v):
    m = ADAM_B1 * m + (1.0 - ADAM_B1) * g
    v = ADAM_B2 * v + (1.0 - ADAM_B2) * (g * g)
    m_hat = m / (1.0 - ADAM_B1 ** ADAM_STEP)
    v_hat = v / (1.0 - ADAM_B2 ** ADAM_STEP)
    delta = -ADAM_LR * (m_hat / (jnp.sqrt(v_hat) + ADAM_EPS) + ADAM_WD * w)
    return delta, m, v


def _adamw_call(w, g, m, v, *, name, max_rows=256):
    _, r, c_ = w.shape
    tr = max_rows if r > max_rows and r % max_rows == 0 else r

    def body(w_ref, g_ref, m_ref, v_ref, d_ref, mo_ref, vo_ref):
        d, mn, vn = _adamw(w_ref[0], g_ref[...], m_ref[0], v_ref[0])
        d_ref[0] = d
        mo_ref[0] = mn
        vo_ref[0] = vn

    row3 = pl.BlockSpec((1, tr, c_), lambda i: (0, i, 0))
    shp = jax.ShapeDtypeStruct((1, r, c_), F32)
    return pl.pallas_call(
        body, name=name, grid=(r // tr,), in_specs=[row3, pl.BlockSpec((tr, c_), lambda i: (i, 0)), row3, row3],
        out_specs=[row3] * 3, out_shape=[shp] * 3, compiler_params=_cparams(("parallel",)),
    )(w, g, m, v)


SMALL = ("mix_norm_g", "ffn_norm_g", "final_norm_g", "q_norm_g", "kv_norm_g", "swa_sinks")
SMALL_W = dict(mix_norm_g=1024, ffn_norm_g=1024, final_norm_g=1024, q_norm_g=Q_LORA, kv_norm_g=KV_LORA, swa_sinks=SWA_HEADS)


def _small_adamw(parts, w, m, v):
    n_par = parts.shape[1] // SUBLANES

    def body(p_ref, w_ref, m_ref, v_ref, g_ref, d_ref, mo_ref, vo_ref):
        tot = p_ref[0]
        for dev in range(1, N_DEV):
            tot = tot + p_ref[dev]
        row_id = lax.broadcasted_iota(jnp.int32, (SUBLANES, PACK_W), 0)
        g = jnp.zeros((SUBLANES, PACK_W), F32)
        for k in range(n_par):
            g = jnp.where(row_id == k, jnp.sum(tot[k * SUBLANES:(k + 1) * SUBLANES, :], axis=0, keepdims=True), g)
        d, mn, vn = _adamw(w_ref[...], g, m_ref[...], v_ref[...])
        g_ref[...] = g
        d_ref[...] = d
        mo_ref[...] = mn
        vo_ref[...] = vn

    shp = jax.ShapeDtypeStruct((SUBLANES, PACK_W), F32)
    vm = pl.BlockSpec(memory_space=pltpu.VMEM)
    return pl.pallas_call(body, name="small_adamw", in_specs=[vm] * 4, out_specs=[vm] * 4, out_shape=[shp] * 4)(parts, w, m, v)


def _small_pack(d, rows_each):
    parts = [jnp.pad(d[n].astype(F32), ((0, 0), (0, PACK_W - SMALL_W[n]))) for n in SMALL]
    out = jnp.concatenate(parts, 0)
    pad = -out.shape[0] % SUBLANES
    return jnp.pad(out, ((0, pad), (0, 0)))


def kernel(x, mix_norm_g, w_in, swa_sinks, q_norm_g, w_uq, kv_norm_g, w_ukv, w_o_swa, w_o_mla, w_out, ffn_norm_g, w_gate, w_up, w_down, final_norm_g, loss_target, m_mix_norm_g, m_w_in, m_swa_sinks, m_q_norm_g, m_w_uq, m_kv_norm_g, m_w_ukv, m_w_o_swa, m_w_o_mla, m_w_out, m_ffn_norm_g, m_w_gate, m_w_up, m_w_down, m_final_norm_g, v_mix_norm_g, v_w_in, v_swa_sinks, v_q_norm_g, v_w_uq, v_kv_norm_g, v_w_ukv, v_w_o_swa, v_w_o_mla, v_w_out, v_ffn_norm_g, v_w_gate, v_w_up, v_w_down, v_final_norm_g):
    big_w = dict(w_in=w_in[0], w_uq=w_uq[0], w_ukv=w_ukv[0], w_o_swa=w_o_swa[0], w_o_mla=w_o_mla[0], w_out=w_out[0],
                 w_gate=w_gate[0], w_up=w_up[0], w_down=w_down[0])
    big_w3 = dict(w_in=w_in, w_uq=w_uq, w_ukv=w_ukv, w_o_swa=w_o_swa, w_o_mla=w_o_mla, w_out=w_out, w_gate=w_gate, w_up=w_up,
                  w_down=w_down)
    big_m = dict(w_in=m_w_in, w_uq=m_w_uq, w_ukv=m_w_ukv, w_o_swa=m_w_o_swa, w_o_mla=m_w_o_mla, w_out=m_w_out,
                 w_gate=m_w_gate, w_up=m_w_up, w_down=m_w_down)
    big_v = dict(w_in=v_w_in, w_uq=v_w_uq, w_ukv=v_w_ukv, w_o_swa=v_w_o_swa, w_o_mla=v_w_o_mla, w_out=v_w_out,
                 w_gate=v_w_gate, w_up=v_w_up, w_down=v_w_down)
    small_w = dict(mix_norm_g=mix_norm_g, ffn_norm_g=ffn_norm_g, final_norm_g=final_norm_g.reshape(1, D_MODEL),
                   q_norm_g=q_norm_g, kv_norm_g=kv_norm_g, swa_sinks=swa_sinks)
    small_m = dict(mix_norm_g=m_mix_norm_g, ffn_norm_g=m_ffn_norm_g, final_norm_g=m_final_norm_g.reshape(1, D_MODEL),
                   q_norm_g=m_q_norm_g, kv_norm_g=m_kv_norm_g, swa_sinks=m_swa_sinks)
    small_v = dict(mix_norm_g=v_mix_norm_g, ffn_norm_g=v_ffn_norm_g, final_norm_g=v_final_norm_g.reshape(1, D_MODEL),
                   q_norm_g=v_q_norm_g, kv_norm_g=v_kv_norm_g, swa_sinks=v_swa_sinks)

    px, py, pc = _position()
    me = 4 * px + 2 * py + pc
    idx = jnp.stack([me] + [4 * qx + 2 * qy + qc for qx, qy, qc in (_peer(px, py, pc, k) for k in range(1, N_DEV))])
    idx = idx.astype(jnp.int32)

    dev = lambda d: (d,)
    pack = _wire_pack(big_w, WIRE_DTYPE)
    win_g, = _all_gather(pack, ((0, dev, 0, W_IN_ROWS),), ((N_DEV, W_IN_ROWS, PACK_W),), name="ag_early")
    ag_mid = _gather_start(pack, W_IN_ROWS, ((0, dev, OUT_ROWS), (1, dev, SMALL_ROWS)),
                           ((N_DEV, OUT_ROWS, PACK_W), (N_DEV, SMALL_ROWS, PACK_W)), name="ag_mid_start")
    ag = {}

    def own_rows(r0, r1, shape):
        return pack[r0:r1].reshape(shape)

    def mid_weights(after):
        pack_mid, (wout_g, small_g) = _gather_wait(ag_mid, W_IN_ROWS, MID_ROWS, after, name="ag_mid_wait")
        ag["late"] = _gather_start(pack_mid, EARLY_ROWS, ((0, _gate_slab, FF_COLS), (0, _up_slab, FF_COLS), (1, dev, FF_COLS)),
                                   (GU_SHAPE, D_SHAPE), name="ag_late_start")
        wout_g = lax.dynamic_update_slice(wout_g, own_rows(W_IN_ROWS, SMALL_ROW0, (1, OUT_ROWS, PACK_W)), (me, 0, 0))
        small_g = lax.dynamic_update_slice(small_g, own_rows(SMALL_ROW0, EARLY_ROWS, (1, SMALL_ROWS, PACK_W)), (me, 0, 0))
        ops = _mid_operands(wout_g, small_g)
        ops["wuq"] = ops["wuq"] + ag["late"][-1][0:1, 0:1].astype(ops["wuq"].dtype)
        return ops

    def late_weights(after):
        _, (gu, d) = _gather_wait(ag["late"], EARLY_ROWS, LATE_ROWS, after, name="ag_late_wait")
        slab = (1, 1, 1, FF_COLS, PACK_W)
        gu = lax.dynamic_update_slice(gu, own_rows(EARLY_ROWS, EARLY_ROWS + FF_COLS, slab), _gate_slab(me) + (0, 0))
        gu = lax.dynamic_update_slice(gu, own_rows(EARLY_ROWS + FF_COLS, EARLY_ROWS + 2 * FF_COLS, slab), _up_slab(me) + (0, 0))
        d = lax.dynamic_update_slice(d, own_rows(EARLY_ROWS + 2 * FF_COLS, PACK_ROWS, (1, FF_COLS, PACK_W)), (me, 0, 0))
        return gu.reshape(2 * D_FF, D_MODEL), d.reshape(D_FF, D_MODEL)

    rs = {}

    def late_grads(g_gu, g_d):
        rs["late"] = _scatter_start([g_gu.reshape(GU_SHAPE), g_d.reshape(D_SHAPE)],
                                    ((0, _gate_slab, FF_COLS), (0, _up_slab, FF_COLS), (1, dev, FF_COLS)),
                                    name="rs_late_start")
        return rs["late"][-1]

    def mid_grads(g):
        rs["mid"] = _scatter_start([_mid_grad_pack(g)], ((0, dev, MID_ROWS),), name="rs_mid_start")
        return rs["mid"][-1]

    def last_grads(g_win_t):
        rs["last"] = _scatter_start([_w_in_grad_chunks(g_win_t)], ((0, dev, W_IN_ROWS),), name="rs_last_start")
        return rs["last"][-1]

    first_w = dict(small_w, mix_norm_g=mix_norm_g + ag_mid[-1][0:1, 0:1])
    loss_tot, gx, g_small = _local_step(
        x[0], loss_target[0], _w_in_operand(win_g), first_w, types.SimpleNamespace(mid=mid_weights, late=late_weights),
        types.SimpleNamespace(late=late_grads, mid=mid_grads, last=last_grads))

    g_gu, g_d, land_late = _scatter_wait(rs["late"], gx, name="rs_late_wait")
    g_mid, land_mid = _scatter_wait(rs["mid"], gx, name="rs_mid_wait")
    g_win, land_last = _scatter_wait(rs["last"], gx, name="rs_last_wait")
    gw = dict(w_gate=_peer_sum(g_gu, _gate_slab, land_late, 0, FF_COLS, idx, name="rs_sum_gate").T,
              w_up=_peer_sum(g_gu, _up_slab, land_late, 1, FF_COLS, idx, name="rs_sum_up").T,
              w_down=_peer_sum(g_d, dev, land_late, 2, FF_COLS, idx, name="rs_sum_down"),
              w_in=_peer_sum(g_win, dev, land_last, 0, W_IN_ROWS, idx, name="rs_sum_in")[0:W_IN_COLS].T)
    gw.update(_mid_unpack(_peer_sum(g_mid, dev, land_mid, 0, MID_ROWS, idx, name="rs_sum_mid")))
    dw, mw, vw = {}, {}, {}
    for n in BIG:
        dw[n], mw[n], vw[n] = _adamw_call(big_w3[n], gw[n], big_m[n], big_v[n], name="adamw_" + n)
    gw = {n: g[None] for n, g in gw.items()}

    loss_rows = jnp.pad(loss_tot[0:1, 0:1], ((0, SUBLANES - 1), (0, PACK_W - 1)))
    small_rows = jnp.concatenate([_small_pack(g_small_rows(g_small), SUBLANES), loss_rows], 0)
    parts, = _all_gather(small_rows, ((0, lambda d: (d,), 0, small_rows.shape[0]),), ((N_DEV,) + small_rows.shape,),
                         name="ag_small")
    gs, ds, ms, vs = _small_adamw(parts, _small_pack(small_w, 1), _small_pack(small_m, 1), _small_pack(small_v, 1))
    loss = gs[len(SMALL), 0]

    def small_out(packed):
        out = {}
        for k, n in enumerate(SMALL):
            out[n] = packed[k:k + 1, :SMALL_W[n]]
        out["final_norm_g"] = out["final_norm_g"].reshape(D_MODEL)
        return out

    gs, ds, ms, vs = small_out(gs), small_out(ds), small_out(ms), small_out(vs)

    order = ("mix_norm_g", "w_in", "swa_sinks", "q_norm_g", "w_uq", "kv_norm_g", "w_ukv", "w_o_swa", "w_o_mla", "w_out",
             "ffn_norm_g", "w_gate", "w_up", "w_down", "final_norm_g")

    def leaves(big, small):
        return [big[n] if n in big else small[n] for n in order]

    return (loss, gx[None], *leaves(gw, gs), *leaves(dw, ds), *leaves(mw, ms), *leaves(vw, vs))


def g_small_rows(g_small):
    out = dict(g_small)
    out["swa_sinks"] = jnp.pad(g_small["swa_sinks"], ((0, SUBLANES - 1), (0, 0)))
    return out
```

```python
import types

import numpy as np
import jax
import jax.numpy as jnp
from jax import lax
from jax.experimental import pallas as pl
from jax.experimental.pallas import tpu as pltpu

F32 = jnp.float32
MXU_DTYPE = jnp.bfloat16
WIRE_DTYPE = jnp.bfloat16

D_MODEL = 1024
EPS = 1e-6
ROPE_THETA = 10000.0
BLOCK = 128
HEAD_DIM = 64
SWA_HEADS = 8
SWA_KV_HEADS = 2
SWA_GROUP = SWA_HEADS // SWA_KV_HEADS
MLA_HEADS = 8
MLA_NOPE = 64
MLA_ROPE = 32
MLA_V = 64
MLA_QK = MLA_NOPE + MLA_ROPE
Q_LORA = 384
KV_LORA = 256
D_FF = 2816
IN_SIZES = (512, 128, 128, Q_LORA, KV_LORA, MLA_ROPE, D_MODEL, D_MODEL)
IN_OFF = tuple(int(v) for v in np.cumsum((0,) + IN_SIZES))
ADAM_LR, ADAM_B1, ADAM_B2, ADAM_EPS, ADAM_WD, ADAM_STEP = 0.001, 0.9, 0.999, 1e-08, 0.01, 10

LANES = 128
SUBLANES = 8
VMEM_LIMIT = 48 * 1024 * 1024
N_DEV = 8
AXES = ("x", "y", "c")

P_GA, P_GB, P_Q, P_QLAT, P_KR, P_K, P_V, P_KVLAT, P_W = 0, 1024, 2048, 3072, 3456, 3584, 3840, 4096, 4352
KR_LANE = 64

LOG2E = 1.4426950408889634

NT = (((1,), (1,)), ((), ()))
NN = (((1,), (0,)), ((), ()))
TN = (((0,), (0,)), ((), ()))


def _cparams(sem):
    return pltpu.CompilerParams(dimension_semantics=sem, vmem_limit_bytes=VMEM_LIMIT)


def _mm(a, b, mode, *, name, out_dtype=F32, add=None, after=None, tm=512, tn=512, tk=None):
    if mode == "nn":
        (M, K), (K2, N) = a.shape, b.shape
    elif mode == "nt":
        (M, K), (N, K2) = a.shape, b.shape
    else:
        (K, M), (K2, N) = a.shape, b.shape
    assert K == K2, (a.shape, b.shape, mode)
    tm, tn, tk = min(tm, M), min(tn, N), K if tk is None else min(tk, K)
    assert M % tm == 0 and N % tn == 0 and K % tk == 0, (M, N, K, tm, tn, tk)
    nk = K // tk
    dn = {"nn": NN, "nt": NT, "tn": TN}[mode]
    if mode == "tn":
        a_spec = pl.BlockSpec((tk, tm), lambda i, j, k: (k, i))
    else:
        a_spec = pl.BlockSpec((tm, tk), lambda i, j, k: (i, k))
    once = dict(pipeline_mode=pl.Buffered(1)) if (nk == 1 and tn == N) else {}
    if mode == "nt":
        b_spec = pl.BlockSpec((tn, tk), lambda i, j, k: (j, k), **once)
    else:
        b_spec = pl.BlockSpec((tk, tn), lambda i, j, k: (k, j), **once)
    o_spec = pl.BlockSpec((tm, tn), lambda i, j, k: (i, j))
    has_add, has_after = add is not None, after is not None

    def body(*refs):
        a_ref, b_ref = refs[0], refs[1]
        add_ref = refs[2] if has_add else None
        o_ref = refs[2 + has_add + has_after]
        p = lax.dot_general(a_ref[...], b_ref[...], dn, preferred_element_type=F32)

        def finish(acc):
            if has_add:
                acc = acc + add_ref[...]
            o_ref[...] = acc.astype(o_ref.dtype)

        if nk == 1:
            finish(p)
        else:
            acc_ref = refs[-1]
            k = pl.program_id(2)

            @pl.when(k == 0)
            def _():
                acc_ref[...] = p

            @pl.when((k > 0) & (k < nk - 1))
            def _():
                acc_ref[...] += p

            @pl.when(k == nk - 1)
            def _():
                finish(acc_ref[...] + p)

    ins = [a, b] + ([add] if has_add else []) + ([after] if has_after else [])
    in_specs = [a_spec, b_spec] + ([o_spec] if has_add else []) + ([pl.BlockSpec(memory_space=pl.ANY)] if has_after else [])
    return pl.pallas_call(
        body, name=name, grid=(M // tm, N // tn, nk), in_specs=in_specs, out_specs=o_spec,
        out_shape=jax.ShapeDtypeStruct((M, N), out_dtype),
        scratch_shapes=[pltpu.VMEM((tm, tn), F32)] if nk > 1 else [],
        compiler_params=_cparams(("parallel", "parallel", "arbitrary")),
    )(*ins)


def _rows(ts, w, cb=0):
    return pl.BlockSpec((ts, w), lambda i: (i, cb))


def _const(r, w):
    return pl.BlockSpec((r, w), lambda i: (0, 0))


def _sublane_sum(v):
    ts, c = v.shape
    return jnp.sum(v.reshape(ts // SUBLANES, SUBLANES, c), axis=0)


def _sigmoid(v):
    return 1.0 / (1.0 + jnp.exp(-v))


def _rope(v, cos, s_up, s_dn, up, dn):
    return v * cos + pltpu.roll(v, up, 1) * s_up + pltpu.roll(v, dn, 1) * s_dn


def _rope_t(dv, cos, s_up, s_dn, up, dn):
    return dv * cos + pltpu.roll(dv * s_up, dn, 1) + pltpu.roll(dv * s_dn, up, 1)


def _rope_tables(seq):
    pos = np.arange(seq, dtype=np.float32)[:, None]

    def base(dim):
        inv = np.float32(ROPE_THETA) ** (-np.arange(0, dim, 2, dtype=np.float32) / np.float32(dim))
        ang = (pos * inv.astype(np.float32)[None, :]).astype(np.float32)
        return np.cos(ang).astype(np.float32), np.sin(ang).astype(np.float32)

    z = lambda n: np.zeros((seq, n), np.float32)
    ca, sa = base(HEAD_DIM)
    a_cos = np.concatenate([ca, ca, z(64)], 1)
    a_up = np.concatenate([-sa, z(96)], 1)
    a_dn = np.concatenate([z(32), sa, z(64)], 1)
    cb, sb = base(MLA_ROPE)
    one = np.ones((seq, 64), np.float32)
    q_cos = np.concatenate([one, cb, cb, z(32)], 1)
    k_cos = np.concatenate([z(64), cb, cb, z(32)], 1)
    b_up = np.concatenate([z(64), -sb, z(48)], 1)
    b_dn = np.concatenate([z(80), sb, z(32)], 1)
    return tuple(jnp.asarray(t) for t in (a_cos, a_up, a_dn, q_cos, k_cos, b_up, b_dn))


def _rms(v, g):
    return v * lax.rsqrt(jnp.mean(v * v, axis=-1, keepdims=True) + EPS) * g


def _rms_bwd(v, g, d):
    r = lax.rsqrt(jnp.mean(v * v, axis=-1, keepdims=True) + EPS)
    xh = v * r
    dxh = d * g
    return r * (dxh - xh * jnp.mean(dxh * xh, axis=-1, keepdims=True)), d * xh


F_GA, F_GB, F_KVLAT, F_QLAT, F_W = 0, 1024, 2048, 2304, 2688


def _proj_in(x, g, w_t, gq, gkv, tabs, *, tm=512):
    s_, c = x.shape
    a_cos, a_up, a_dn, _, k_cos, b_up, b_dn = tabs

    def body(x_ref, g_ref, w_ref, gq_ref, gkv_ref, ac, au, ad, kc, bu, bd,
             h_ref, qa_ref, ka_ref, va_ref, cq_ref, ckv_ref, kro_ref, pf_ref):
        h = _rms(x_ref[...], g_ref[...]).astype(h_ref.dtype)
        h_ref[...] = h
        mm = lambda a, b: lax.dot_general(h, w_ref[a:b, :], NT, preferred_element_type=F32)
        pf_ref[:, F_GA:F_KVLAT] = mm(P_GA, P_Q)
        c_, u_, d_ = ac[...], au[...], ad[...]
        q = mm(P_Q, P_QLAT)
        for hd in range(SWA_HEADS):
            sl = slice(hd * LANES, (hd + 1) * LANES)
            qa_ref[:, sl] = _rope(q[:, sl], c_, u_, d_, 96, 32).astype(qa_ref.dtype)
        kv = mm(P_KR, P_KVLAT)
        kro_ref[...] = _rope(kv[:, :LANES], kc[...], bu[...], bd[...], 112, 16)
        for hd in range(SWA_KV_HEADS):
            sl = slice((1 + hd) * LANES, (2 + hd) * LANES)
            ka_ref[:, hd * LANES:(hd + 1) * LANES] = _rope(kv[:, sl], c_, u_, d_, 96, 32).astype(ka_ref.dtype)
        va_ref[...] = kv[:, P_V - P_KR:].astype(va_ref.dtype)
        for a, b, f0, gref, dst in ((P_QLAT, P_KR, F_QLAT, gq_ref, cq_ref), (P_KVLAT, P_W, F_KVLAT, gkv_ref, ckv_ref)):
            v = mm(a, b)
            pf_ref[:, f0:f0 + b - a] = v
            r = lax.rsqrt(jnp.mean(v * v, axis=-1, keepdims=True) + EPS)
            dst[...] = (v * r * gref[...]).astype(dst.dtype)

    tab = _rows(tm, LANES)
    widths = (c, SWA_HEADS * LANES, SWA_KV_HEADS * LANES, SWA_KV_HEADS * LANES, Q_LORA, KV_LORA)
    return pl.pallas_call(
        body, name="proj_in", grid=(s_ // tm,),
        in_specs=[_rows(tm, c), _const(1, c), pl.BlockSpec((P_W, c), lambda i: (0, 0), pipeline_mode=pl.Buffered(1)),
                  _const(1, Q_LORA), _const(1, KV_LORA), tab, tab, tab, tab, tab, tab],
        out_specs=[_rows(tm, w) for w in widths] + [tab, _rows(tm, F_W)],
        out_shape=[jax.ShapeDtypeStruct((s_, w), MXU_DTYPE) for w in widths]
        + [jax.ShapeDtypeStruct((s_, LANES), F32), jax.ShapeDtypeStruct((s_, F_W), F32)],
        compiler_params=_cparams(("parallel",)),
    )(x, g, w_t, gq, gkv, a_cos, a_up, a_dn, k_cos, b_up, b_dn)


def _mm_norm_bwd(a, b, x, g, res, *, name, after=None, tm=512):
    s_, kk = a.shape
    c = b.shape[1]
    has_after = after is not None

    def body(*refs):
        a_ref, b_ref, x_ref, g_ref, res_ref = refs[:5]
        dx_ref, dxb_ref, dg_ref = refs[5 + has_after:]
        d = jnp.dot(a_ref[...], b_ref[...], preferred_element_type=F32)
        dx, gg = _rms_bwd(x_ref[...], g_ref[...], d)
        dx = dx + res_ref[...]
        dx_ref[...] = dx
        dxb_ref[...] = dx.astype(dxb_ref.dtype)

        @pl.when(pl.program_id(0) == 0)
        def _():
            dg_ref[...] = jnp.zeros(dg_ref.shape, F32)

        dg_ref[...] += _sublane_sum(gg)

    row = _rows(tm, c)
    in_specs = [_rows(tm, kk), pl.BlockSpec((kk, c), lambda i: (0, 0), pipeline_mode=pl.Buffered(1)), row, _const(1, c), row]
    return pl.pallas_call(
        body, name=name, grid=(s_ // tm,), in_specs=in_specs + ([pl.BlockSpec(memory_space=pl.ANY)] if has_after else []),
        out_specs=[row, row, _const(SUBLANES, c)],
        out_shape=[jax.ShapeDtypeStruct((s_, c), F32), jax.ShapeDtypeStruct((s_, c), MXU_DTYPE),
                   jax.ShapeDtypeStruct((SUBLANES, c), F32)],
        compiler_params=_cparams(("arbitrary",)),
    )(*([a, b, x, g, res] + ([after] if has_after else [])))


def _mla_up(cq, ckv, kro, wuq, wuk, wuv, tabs, *, ts=512):
    s_ = cq.shape[0]
    _, _, _, q_cos, _, b_up, b_dn = tabs

    def body(cq_ref, ckv_ref, kr_ref, wq_ref, wk_ref, wv_ref, qc, bu, bd, qo_ref, ko_ref, vo_ref):
        c_, u_, d_ = qc[...], bu[...], bd[...]
        kr = kr_ref[...]
        ckv_ = ckv_ref[...]
        vo_ref[...] = jnp.dot(ckv_, wv_ref[...], preferred_element_type=F32).astype(vo_ref.dtype)
        q = jnp.dot(cq_ref[...], wq_ref[...], preferred_element_type=F32)
        k = jnp.dot(ckv_, wk_ref[...], preferred_element_type=F32)
        for h in range(MLA_HEADS):
            sl = slice(h * LANES, (h + 1) * LANES)
            qo_ref[:, sl] = _rope(q[:, sl], c_, u_, d_, 112, 16).astype(qo_ref.dtype)
            ko_ref[:, sl] = (k[:, sl] + kr).astype(ko_ref.dtype)

    tab, out = _rows(ts, LANES), _rows(ts, 1024)
    return pl.pallas_call(
        body, name="mla_up", grid=(s_ // ts,),
        in_specs=[_rows(ts, Q_LORA), _rows(ts, KV_LORA), tab, _const(Q_LORA, 1024), _const(KV_LORA, 1024),
                  _const(KV_LORA, 1024), tab, tab, tab],
        out_specs=[out, out, out], out_shape=[jax.ShapeDtypeStruct((s_, 1024), MXU_DTYPE)] * 3,
        compiler_params=_cparams(("parallel",)),
    )(cq, ckv, kro, wuq, wuk, wuv, q_cos, b_up, b_dn)


def _mla_up_bwd(dqc, dkc, dvp, wuq, wukv, p, gq, gkv, tabs, *, ts=256):
    s_ = dqc.shape[0]
    _, _, _, q_cos, k_cos, b_up, b_dn = tabs

    def body(dq_ref, dk_ref, dv_ref, wq_ref, wkv_ref, ql_ref, kvl_ref, gq_ref, gkv_ref, qc, kc, bu, bd,
             dqo_ref, dkvo_ref, dkr_ref, dql_ref, dkvl_ref, dgq_ref, dgkv_ref):
        c_, u_, d_ = qc[...], bu[...], bd[...]
        tot = jnp.zeros((ts, LANES), F32)
        for h in range(MLA_HEADS):
            sl = slice(h * LANES, (h + 1) * LANES)
            dqo_ref[:, sl] = _rope_t(dq_ref[:, sl], c_, u_, d_, 112, 16).astype(dqo_ref.dtype)
            dk = dk_ref[:, sl]
            dkvo_ref[:, sl] = dk.astype(dkvo_ref.dtype)
            tot = tot + dk
        dkvo_ref[:, 1024:2048] = dv_ref[...].astype(dkvo_ref.dtype)
        dkr_ref[...] = _rope_t(tot, kc[...], u_, d_, 112, 16).astype(dkr_ref.dtype)

        @pl.when(pl.program_id(0) == 0)
        def _():
            dgq_ref[...] = jnp.zeros(dgq_ref.shape, F32)
            dgkv_ref[...] = jnp.zeros(dgkv_ref.shape, F32)

        for do_ref, w_ref, x_ref, g_ref, dx_ref, dg_ref in ((dqo_ref, wq_ref, ql_ref, gq_ref, dql_ref, dgq_ref),
                                                            (dkvo_ref, wkv_ref, kvl_ref, gkv_ref, dkvl_ref, dgkv_ref)):
            d = lax.dot_general(do_ref[...], w_ref[...], NT, preferred_element_type=F32)
            dx, gg = _rms_bwd(x_ref[...], g_ref[...], d)
            dx_ref[...] = dx.astype(dx_ref.dtype)
            dg_ref[...] += _sublane_sum(gg)

    tab = _rows(ts, LANES)
    return pl.pallas_call(
        body, name="mla_up_bwd", grid=(s_ // ts,),
        in_specs=[_rows(ts, 1024), _rows(ts, 1024), _rows(ts, 1024), _const(Q_LORA, 1024), _const(KV_LORA, 2048),
                  _rows(ts, Q_LORA, F_QLAT // Q_LORA), _rows(ts, KV_LORA, F_KVLAT // KV_LORA),
                  _const(1, Q_LORA), _const(1, KV_LORA), tab, tab, tab, tab],
        out_specs=[_rows(ts, 1024), _rows(ts, 2048), _rows(ts, LANES), _rows(ts, Q_LORA), _rows(ts, KV_LORA),
                   _const(SUBLANES, Q_LORA), _const(SUBLANES, KV_LORA)],
        out_shape=[jax.ShapeDtypeStruct((s_, 1024), MXU_DTYPE), jax.ShapeDtypeStruct((s_, 2048), MXU_DTYPE),
                   jax.ShapeDtypeStruct((s_, LANES), MXU_DTYPE), jax.ShapeDtypeStruct((s_, Q_LORA), MXU_DTYPE),
                   jax.ShapeDtypeStruct((s_, KV_LORA), MXU_DTYPE), jax.ShapeDtypeStruct((SUBLANES, Q_LORA), F32),
                   jax.ShapeDtypeStruct((SUBLANES, KV_LORA), F32)],
        compiler_params=_cparams(("arbitrary",)),
    )(dqc, dkc, dvp, wuq, wukv, p, p, gq, gkv, q_cos, k_cos, b_up, b_dn)


def _assemble_dp(dgab, dqa, dqlat, dkr, dka, dva, dkvlat, tabs, *, ts=256):
    s_ = dqa.shape[0]
    a_cos, a_up, a_dn = tabs[0], tabs[1], tabs[2]

    def body(dg_ref, dq_ref, dql_ref, dkr_ref, dk_ref, dv_ref, dkvl_ref, ac, au, ad, o_ref):
        c_, u_, d_ = ac[...], au[...], ad[...]
        o_ref[:, P_GA:P_Q] = dg_ref[...]
        for h in range(SWA_HEADS):
            sl = slice(h * LANES, (h + 1) * LANES)
            o_ref[:, P_Q + h * LANES:P_Q + (h + 1) * LANES] = _rope_t(dq_ref[:, sl], c_, u_, d_, 96, 32).astype(o_ref.dtype)
        o_ref[:, P_QLAT:P_KR] = dql_ref[...]
        o_ref[:, P_KR:P_K] = dkr_ref[...]
        for h in range(SWA_KV_HEADS):
            sl = slice(h * LANES, (h + 1) * LANES)
            o_ref[:, P_K + h * LANES:P_K + (h + 1) * LANES] = _rope_t(dk_ref[:, sl], c_, u_, d_, 96, 32).astype(o_ref.dtype)
        o_ref[:, P_V:P_KVLAT] = dv_ref[...]
        o_ref[:, P_KVLAT:P_W] = dkvl_ref[...]

    tab = _rows(ts, LANES)
    return pl.pallas_call(
        body, name="assemble_dp", grid=(s_ // ts,),
        in_specs=[_rows(ts, 2048), _rows(ts, 1024), _rows(ts, Q_LORA), _rows(ts, LANES), _rows(ts, 256), _rows(ts, 256),
                  _rows(ts, KV_LORA), tab, tab, tab],
        out_specs=_rows(ts, P_W), out_shape=jax.ShapeDtypeStruct((s_, P_W), MXU_DTYPE),
        compiler_params=_cparams(("parallel",)),
    )(dgab, dqa, dqlat, dkr, dka, dva, dkvlat, a_cos, a_up, a_dn)


def _attn_out_gate(oa, ob, woa, wob, p, *, ts=512):
    s_ = p.shape[0]

    def body(oa_ref, ob_ref, wa_ref, wb_ref, ga_ref, gb_ref, ta_ref, tb_ref, y_ref):
        ta = jnp.dot(oa_ref[...], wa_ref[...], preferred_element_type=F32)
        tb = jnp.dot(ob_ref[...], wb_ref[...], preferred_element_type=F32)
        ta_ref[...] = ta
        tb_ref[...] = tb
        y_ref[...] = (_sigmoid(ga_ref[...]) * ta + _sigmoid(gb_ref[...]) * tb).astype(y_ref.dtype)

    w = _const(1024, 1024)
    return pl.pallas_call(
        body, name="attn_out_gate", grid=(s_ // ts,),
        in_specs=[_rows(ts, 1024), _rows(ts, 1024), w, w, _rows(ts, 1024, F_GA // 1024), _rows(ts, 1024, F_GB // 1024)],
        out_specs=[_rows(ts, 1024)] * 3,
        out_shape=[jax.ShapeDtypeStruct((s_, 1024), F32)] * 2 + [jax.ShapeDtypeStruct((s_, 1024), MXU_DTYPE)],
        compiler_params=_cparams(("parallel",)),
    )(oa, ob, woa, wob, p, p)


def _d_y_gate(dx1b, wout, p, ta, tb, *, ts=512):
    s_ = p.shape[0]

    def body(dx_ref, w_ref, ga_ref, gb_ref, ta_ref, tb_ref, dta_ref, dtb_ref, dg_ref):
        d = lax.dot_general(dx_ref[...], w_ref[...], NT, preferred_element_type=F32)
        sa, sb = _sigmoid(ga_ref[...]), _sigmoid(gb_ref[...])
        dta_ref[...] = (d * sa).astype(dta_ref.dtype)
        dtb_ref[...] = (d * sb).astype(dtb_ref.dtype)
        dg_ref[:, 0:1024] = (d * ta_ref[...] * (sa * (1.0 - sa))).astype(dg_ref.dtype)
        dg_ref[:, 1024:2048] = (d * tb_ref[...] * (sb * (1.0 - sb))).astype(dg_ref.dtype)

    return pl.pallas_call(
        body, name="d_y_gate", grid=(s_ // ts,),
        in_specs=[_rows(ts, 1024), _const(1024, 1024), _rows(ts, 1024, F_GA // 1024), _rows(ts, 1024, F_GB // 1024),
                  _rows(ts, 1024), _rows(ts, 1024)],
        out_specs=[_rows(ts, 1024), _rows(ts, 1024), _rows(ts, 2048)],
        out_shape=[jax.ShapeDtypeStruct((s_, 1024), MXU_DTYPE)] * 2 + [jax.ShapeDtypeStruct((s_, 2048), MXU_DTYPE)],
        compiler_params=_cparams(("parallel",)),
    )(dx1b, wout, p, p, ta, tb)


FF_TILE = D_FF // 2


def _ffn_in_act(x1, g, wgu_t, *, tm=512):
    s_ = x1.shape[0]
    n = s_ // tm

    def body(x_ref, g_ref, w_ref, h_ref, gu_ref, a_ref):
        h = _rms(x_ref[...], g_ref[...]).astype(h_ref.dtype)
        h_ref[...] = h
        p = lax.dot_general(h, w_ref[...], NT, preferred_element_type=F32)
        gu_ref[...] = p
        gate = p[:, :FF_TILE]
        a_ref[...] = (gate * _sigmoid(gate) * p[:, FF_TILE:]).astype(a_ref.dtype)

    return pl.pallas_call(
        body, name="ffn_in", grid=(2, s_ // tm),
        in_specs=[pl.BlockSpec((tm, D_MODEL), lambda j, i: (i, 0)), pl.BlockSpec((1, D_MODEL), lambda j, i: (0, 0)),
                  pl.BlockSpec((2 * FF_TILE, D_MODEL), lambda j, i: (j, 0))],
        out_specs=[pl.BlockSpec((tm, D_MODEL), lambda j, i: (i + j * (n - 1 - i), 0)),
                   pl.BlockSpec((tm, 2 * FF_TILE), lambda j, i: (i, j)),
                   pl.BlockSpec((tm, FF_TILE), lambda j, i: (i, j))],
        out_shape=[jax.ShapeDtypeStruct((s_, D_MODEL), MXU_DTYPE), jax.ShapeDtypeStruct((s_, 2 * D_FF), F32),
                   jax.ShapeDtypeStruct((s_, D_FF), MXU_DTYPE)],
        compiler_params=_cparams(("arbitrary", "arbitrary")),
    )(x1, g, wgu_t)


def _d_act_swiglu(dx2b, wd, gu, *, tm=512):
    s_ = dx2b.shape[0]

    def body(d_ref, w_ref, gu_ref, o_ref):
        da = lax.dot_general(d_ref[...], w_ref[...], NT, preferred_element_type=F32)
        g, u = gu_ref[:, :FF_TILE], gu_ref[:, FF_TILE:]
        sg = _sigmoid(g)
        o_ref[:, :FF_TILE] = (da * u * (sg * (1.0 + g * (1.0 - sg)))).astype(o_ref.dtype)
        o_ref[:, FF_TILE:] = (da * (g * sg)).astype(o_ref.dtype)

    gu_spec = pl.BlockSpec((tm, 2 * FF_TILE), lambda j, i: (i, j))
    return pl.pallas_call(
        body, name="d_act", grid=(2, s_ // tm),
        in_specs=[pl.BlockSpec((tm, D_MODEL), lambda j, i: (i, 0)), pl.BlockSpec((FF_TILE, D_MODEL), lambda j, i: (j, 0)), gu_spec],
        out_specs=gu_spec, out_shape=jax.ShapeDtypeStruct((s_, 2 * D_FF), MXU_DTYPE),
        compiler_params=_cparams(("parallel", "parallel")),
    )(dx2b, wd, gu)


def _ffn_out_loss(act, wd, x1, g, tgt, *, ts=512):
    s_, c = x1.shape
    kk = act.shape[1]

    def body(a_ref, w_ref, x_ref, g_ref, t_ref, dx_ref, dxb_ref, dg_ref, lp_ref, tot_ref):
        v = x_ref[...] + jnp.dot(a_ref[...], w_ref[...], preferred_element_type=F32)
        r = lax.rsqrt(jnp.mean(v * v, axis=-1, keepdims=True) + EPS)
        xh = v * r
        gg = g_ref[...]
        e = xh * gg - t_ref[...]
        do = e * (1.0 / c)
        dxh = do * gg
        dx = r * (dxh - xh * jnp.mean(dxh * xh, axis=-1, keepdims=True))
        dx_ref[...] = dx
        dxb_ref[...] = dx.astype(dxb_ref.dtype)
        i = pl.program_id(0)

        @pl.when(i == 0)
        def _():
            dg_ref[...] = jnp.zeros(dg_ref.shape, F32)
            lp_ref[...] = jnp.zeros(lp_ref.shape, F32)

        dg_ref[...] += _sublane_sum(do * xh)
        lp_ref[...] += _sublane_sum(e * e)
        tot_ref[...] = jnp.full(tot_ref.shape, (0.5 / c) * jnp.sum(lp_ref[...]), F32)

    return pl.pallas_call(
        body, name="ffn_out_loss", grid=(s_ // ts,),
        in_specs=[_rows(ts, kk), _const(kk, c), _rows(ts, c), _const(1, c), _rows(ts, c)],
        out_specs=[_rows(ts, c), _rows(ts, c), _const(SUBLANES, c), _const(SUBLANES, c), _const(SUBLANES, LANES)],
        out_shape=[jax.ShapeDtypeStruct((s_, c), F32), jax.ShapeDtypeStruct((s_, c), MXU_DTYPE),
                   jax.ShapeDtypeStruct((SUBLANES, c), F32), jax.ShapeDtypeStruct((SUBLANES, c), F32),
                   jax.ShapeDtypeStruct((SUBLANES, LANES), F32)],
        compiler_params=_cparams(("arbitrary",)),
    )(act, wd, x1, g, tgt)


def _mla_d_out(dtb, wob, o32, *, ts=512):
    s_ = dtb.shape[0]

    def body(dt_ref, w_ref, o_ref, dob_ref, dl_ref):
        d = lax.dot_general(dt_ref[...], w_ref[...], NT, preferred_element_type=F32)
        dob_ref[...] = d.astype(dob_ref.dtype)
        prod = d * o_ref[...]
        for h in range(MLA_HEADS):
            dl_ref[h] = jnp.sum(prod[:, h * LANES:(h + 1) * LANES].T, axis=0, keepdims=True)

    return pl.pallas_call(
        body, name="mla_d_out", grid=(s_ // ts,), in_specs=[_rows(ts, 1024), _const(1024, 1024), _rows(ts, 1024)],
        out_specs=[_rows(ts, 1024), pl.BlockSpec((MLA_HEADS, 1, ts), lambda i: (0, 0, i))],
        out_shape=[jax.ShapeDtypeStruct((s_, 1024), MXU_DTYPE), jax.ShapeDtypeStruct((MLA_HEADS, 1, s_), F32)],
        compiler_params=_cparams(("parallel",)),
    )(dtb, wob, o32)


SWA_T = 4 * BLOCK


SWA_W = SWA_GROUP * BLOCK


def _swa_masks(sb):
    kr = lax.broadcasted_iota(jnp.int32, (2 * BLOCK, SWA_W), 0)
    qc = jnp.bitwise_and(lax.broadcasted_iota(jnp.int32, (2 * BLOCK, SWA_W), 1), BLOCK - 1)
    band = jnp.logical_and(kr > qc, kr <= qc + BLOCK)
    first = jnp.logical_and(band, kr >= BLOCK)
    return band, jnp.logical_or(first, jnp.logical_and(band, sb > 0))


def _heads_to_rows(ref, rs):
    return jnp.concatenate([ref[rs, h * LANES:(h + 1) * LANES] for h in range(SWA_GROUP)], axis=0)


def _sink_row(sk_ref):
    return jnp.concatenate([sk_ref[0, h:h + 1, :] for h in range(SWA_GROUP)], axis=1) * LOG2E


def _swa_in_specs(rev, nsb):
    sbi = (lambda j: nsb - 1 - j) if rev else (lambda j: j)
    cur = pl.BlockSpec((SWA_T, LANES), lambda g, j: (sbi(j), g))
    prev = pl.BlockSpec((BLOCK, LANES), lambda g, j: (jnp.maximum(4 * sbi(j) - 1, 0), g))
    q = pl.BlockSpec((SWA_T, SWA_GROUP * LANES), lambda g, j: (sbi(j), g))
    sink = pl.BlockSpec((1, SUBLANES, LANES), lambda g, j: (g, 0, 0))
    lse = pl.BlockSpec((SWA_GROUP, 1, SWA_T), lambda g, j: (g, 0, sbi(j)))
    return q, cur, prev, sink, lse


def _swa_fwd(qa, ka, va, sink_b):
    s_ = qa.shape[0]
    nsb = s_ // SWA_T
    c2 = HEAD_DIM ** -0.5 * LOG2E

    def body(q_ref, kc_ref, kp_ref, vc_ref, vp_ref, sk_ref, o32_ref, o16_ref, lse_ref, kx, vx):
        kx[0:BLOCK, :] = kp_ref[...]
        kx[BLOCK:5 * BLOCK, :] = kc_ref[...]
        vx[0:BLOCK, :] = vp_ref[...]
        vx[BLOCK:5 * BLOCK, :] = vc_ref[...]
        band, band0 = _swa_masks(pl.program_id(1))
        sink2 = _sink_row(sk_ref)
        for b in range(4):
            rs = slice(b * BLOCK, (b + 1) * BLOCK)
            ks = slice(b * BLOCK, (b + 2) * BLOCK)
            st = lax.dot_general(kx[ks, :], _heads_to_rows(q_ref, rs), NT, preferred_element_type=F32) * c2
            st = jnp.where(band0 if b == 0 else band, st, -jnp.inf)
            m = jnp.maximum(jnp.max(st, axis=0, keepdims=True), sink2)
            pt = jnp.exp2(st - m)
            den = jnp.sum(pt, axis=0, keepdims=True) + jnp.exp2(sink2 - m)
            o = lax.dot_general((pt * (1.0 / den)).astype(MXU_DTYPE), vx[ks, :], TN, preferred_element_type=F32)
            lse = m + jnp.log2(den)
            for hh in range(SWA_GROUP):
                cs = slice(hh * LANES, (hh + 1) * LANES)
                o32_ref[rs, cs] = o[cs, :]
                o16_ref[rs, cs] = o[cs, :].astype(o16_ref.dtype)
                lse_ref[hh, :, rs] = lse[:, cs]

    q, cur, prev, sink, lse_spec = _swa_in_specs(False, nsb)
    return pl.pallas_call(
        body, name="swa_fwd", grid=(SWA_KV_HEADS, nsb), in_specs=[q, cur, prev, cur, prev, sink],
        out_specs=[q, q, lse_spec],
        out_shape=[jax.ShapeDtypeStruct((s_, SWA_HEADS * LANES), F32), jax.ShapeDtypeStruct((s_, SWA_HEADS * LANES), MXU_DTYPE),
                   jax.ShapeDtypeStruct((SWA_HEADS, 1, s_), F32)],
        scratch_shapes=[pltpu.VMEM((5 * BLOCK, LANES), MXU_DTYPE), pltpu.VMEM((5 * BLOCK, LANES), MXU_DTYPE)],
        compiler_params=_cparams(("parallel", "arbitrary")),
    )(qa, ka, ka, va, va, sink_b)


def _swa_bwd(qa, ka, va, sink_b, o32, do, lse):
    s_ = qa.shape[0]
    nsb = s_ // SWA_T
    scale = HEAD_DIM ** -0.5
    c2 = scale * LOG2E

    def body(q_ref, kc_ref, kp_ref, vc_ref, vp_ref, sk_ref, o_ref, do_ref, lse_ref,
             dq_ref, dk_ref, dv_ref, dsk_ref, kx, vx, kacc, vacc, kcar, vcar):
        j = pl.program_id(1)
        kx[0:BLOCK, :] = kp_ref[...]
        kx[BLOCK:5 * BLOCK, :] = kc_ref[...]
        vx[0:BLOCK, :] = vp_ref[...]
        vx[BLOCK:5 * BLOCK, :] = vc_ref[...]
        band, band0 = _swa_masks(nsb - 1 - j)
        kacc[...] = jnp.zeros(kacc.shape, F32)
        vacc[...] = jnp.zeros(vacc.shape, F32)

        @pl.when(j == 0)
        def _():
            kcar[...] = jnp.zeros(kcar.shape, F32)
            vcar[...] = jnp.zeros(vcar.shape, F32)
            dsk_ref[...] = jnp.zeros(dsk_ref.shape, F32)

        sink2 = _sink_row(sk_ref)
        dsink = jnp.zeros((1, SWA_W), F32)
        for b in range(4):
            rs = slice(b * BLOCK, (b + 1) * BLOCK)
            ks = slice(b * BLOCK, (b + 2) * BLOCK)
            q, k2, v2 = _heads_to_rows(q_ref, rs), kx[ks, :], vx[ks, :]
            d = _heads_to_rows(do_ref, rs)
            delta = jnp.sum((d * _heads_to_rows(o_ref, rs)).T, axis=0, keepdims=True)
            l2 = jnp.concatenate([lse_ref[hh, :, rs] for hh in range(SWA_GROUP)], axis=1)
            st = lax.dot_general(k2, q, NT, preferred_element_type=F32) * c2
            pt = jnp.exp2(jnp.where(band0 if b == 0 else band, st, -jnp.inf) - l2)
            db = d.astype(MXU_DTYPE)
            dst = (pt * (lax.dot_general(v2, db, NT, preferred_element_type=F32) - delta) * scale).astype(MXU_DTYPE)
            dq = lax.dot_general(dst, k2, TN, preferred_element_type=F32)
            for hh in range(SWA_GROUP):
                dq_ref[rs, hh * LANES:(hh + 1) * LANES] = dq[hh * LANES:(hh + 1) * LANES, :]
            kacc[ks, :] += jnp.dot(dst, q, preferred_element_type=F32)
            vacc[ks, :] += jnp.dot(pt.astype(MXU_DTYPE), db, preferred_element_type=F32)
            dsink = dsink - jnp.exp2(sink2 - l2) * delta
        for hh in range(SWA_GROUP):
            tot = jnp.sum(dsink[:, hh * LANES:(hh + 1) * LANES], axis=1, keepdims=True)
            dsk_ref[0, hh:hh + 1, :] += jnp.broadcast_to(tot, (1, LANES))

        dk_ref[0:3 * BLOCK, :] = kacc[BLOCK:4 * BLOCK, :]
        dk_ref[3 * BLOCK:4 * BLOCK, :] = kacc[4 * BLOCK:5 * BLOCK, :] + kcar[...]
        dv_ref[0:3 * BLOCK, :] = vacc[BLOCK:4 * BLOCK, :].astype(dv_ref.dtype)
        dv_ref[3 * BLOCK:4 * BLOCK, :] = (vacc[4 * BLOCK:5 * BLOCK, :] + vcar[...]).astype(dv_ref.dtype)
        kcar[...] = kacc[0:BLOCK, :]
        vcar[...] = vacc[0:BLOCK, :]

    q, cur, prev, sink, lse_spec = _swa_in_specs(True, nsb)
    return pl.pallas_call(
        body, name="swa_bwd", grid=(SWA_KV_HEADS, nsb),
        in_specs=[q, cur, prev, cur, prev, sink, q, q, lse_spec],
        out_specs=[q, cur, cur, sink],
        out_shape=[jax.ShapeDtypeStruct((s_, SWA_HEADS * LANES), F32), jax.ShapeDtypeStruct((s_, SWA_KV_HEADS * LANES), F32),
                   jax.ShapeDtypeStruct((s_, SWA_KV_HEADS * LANES), MXU_DTYPE),
                   jax.ShapeDtypeStruct((SWA_KV_HEADS, SUBLANES, LANES), F32)],
        scratch_shapes=[pltpu.VMEM((5 * BLOCK, LANES), MXU_DTYPE), pltpu.VMEM((5 * BLOCK, LANES), MXU_DTYPE),
                        pltpu.VMEM((5 * BLOCK, LANES), F32), pltpu.VMEM((5 * BLOCK, LANES), F32),
                        pltpu.VMEM((BLOCK, LANES), F32), pltpu.VMEM((BLOCK, LANES), F32)],
        compiler_params=_cparams(("arbitrary", "arbitrary")),
    )(qa, ka, ka, va, va, sink_b, o32, do, lse)


MLA_T = 512
MLA_FWD_GROUP = 4
MLA_BWD_GROUP = 2


def _mla_specs(s_, t, group):
    w = group * LANES
    qs = pl.BlockSpec((t, w), lambda g, i: (i, g))
    kv = pl.BlockSpec((s_, w), lambda g, i: (0, g))
    row = pl.BlockSpec((group, 1, t), lambda g, i: (g, 0, i))
    return qs, kv, row


def _causal_scores_t(k, q, t, c2, masked):
    st = lax.dot_general(k, q, NT, preferred_element_type=F32) * c2
    if masked:
        kr = lax.broadcasted_iota(jnp.int32, (t, t), 0)
        qc = lax.broadcasted_iota(jnp.int32, (t, t), 1)
        st = jnp.where(kr <= qc, st, -jnp.inf)
    return st


def _mla_fwd(qc, kc, vp):
    s_ = qc.shape[0]
    t = min(MLA_T, s_)
    c2 = MLA_QK ** -0.5 * LOG2E
    grp = MLA_FWD_GROUP

    def body(q_ref, k_ref, v_ref, o32_ref, o16_ref, lse_ref, m_s, acc_s):
        qi = pl.program_id(1)
        m_s[...] = jnp.full(m_s.shape, -jnp.inf, F32)
        acc_s[...] = jnp.zeros(acc_s.shape, F32)
        ones_lane = lax.broadcasted_iota(jnp.int32, (t, LANES), 1) == MLA_V

        def step(ki, masked):
            off = pl.multiple_of(ki * t, t)
            for g in range(grp):
                cs = slice(g * LANES, (g + 1) * LANES)
                st = _causal_scores_t(k_ref[pl.ds(off, t), cs], q_ref[:, cs], t, c2, masked)
                m_old = m_s[g]
                m_new = jnp.maximum(m_old, jnp.max(st, axis=0, keepdims=True))
                alpha = jnp.exp2(m_old - m_new)
                pt = jnp.exp2(st - m_new).astype(MXU_DTYPE)
                v = v_ref[pl.ds(off, t), cs]
                v = jnp.where(ones_lane, jnp.ones((), v.dtype), v)
                acc_s[g] = alpha * acc_s[g] + lax.dot_general(v, pt, TN, preferred_element_type=F32)
                m_s[g] = m_new

        def full_block(ki, carry):
            step(ki, False)
            return carry

        lax.fori_loop(0, qi, full_block, 0)
        step(qi, True)
        for g in range(grp):
            cs = slice(g * LANES, (g + 1) * LANES)
            acc = acc_s[g]
            l = acc[MLA_V:MLA_V + 1, :]
            o = (acc * (1.0 / l)).T
            o32_ref[:, cs] = o
            o16_ref[:, cs] = o.astype(o16_ref.dtype)
            lse_ref[g] = m_s[g] + jnp.log2(l)

    qs, kv, row = _mla_specs(s_, t, grp)
    return pl.pallas_call(
        body, name="mla_fwd", grid=(MLA_HEADS // grp, s_ // t), in_specs=[qs, kv, kv], out_specs=[qs, qs, row],
        out_shape=[jax.ShapeDtypeStruct((s_, MLA_HEADS * LANES), F32), jax.ShapeDtypeStruct((s_, MLA_HEADS * LANES), MXU_DTYPE),
                   jax.ShapeDtypeStruct((MLA_HEADS, 1, s_), F32)],
        scratch_shapes=[pltpu.VMEM((grp, 1, t), F32), pltpu.VMEM((grp, LANES, t), F32)],
        compiler_params=_cparams(("parallel", "arbitrary")),
    )(qc, kc, vp)


def _mla_bwd(qc, kc, vp, dob, lse, delta):
    s_ = qc.shape[0]
    t = min(MLA_T, s_)
    scale = MLA_QK ** -0.5
    c2 = scale * LOG2E
    grp = MLA_BWD_GROUP

    def body(q_ref, do_ref, lse_ref, dl_ref, k_ref, v_ref, dq_ref, dk_ref, dv_ref, dqt_s):
        qi = pl.program_id(1)

        @pl.when(qi == 0)
        def _():
            dk_ref[...] = jnp.zeros(dk_ref.shape, F32)
            dv_ref[...] = jnp.zeros(dv_ref.shape, F32)

        dqt_s[...] = jnp.zeros(dqt_s.shape, F32)

        def step(ki, masked):
            off = pl.multiple_of(ki * t, t)
            for g in range(grp):
                cs = slice(g * LANES, (g + 1) * LANES)
                q, d, k = q_ref[:, cs], do_ref[:, cs], k_ref[pl.ds(off, t), cs]
                pt = jnp.exp2(_causal_scores_t(k, q, t, c2, masked) - lse_ref[g])
                dpt = lax.dot_general(v_ref[pl.ds(off, t), cs], d, NT, preferred_element_type=F32)
                dst = (pt * (dpt - dl_ref[g]) * scale).astype(MXU_DTYPE)
                dv_ref[pl.ds(off, t), cs] += jnp.dot(pt.astype(MXU_DTYPE), d, preferred_element_type=F32)
                dk_ref[pl.ds(off, t), cs] += jnp.dot(dst, q, preferred_element_type=F32)
                dqt_s[g] += lax.dot_general(k, dst, TN, preferred_element_type=F32)

        def full_block(ki, carry):
            step(ki, False)
            return carry

        lax.fori_loop(0, qi, full_block, 0)
        step(qi, True)
        for g in range(grp):
            dq_ref[:, g * LANES:(g + 1) * LANES] = dqt_s[g].T

    qs, kv, row = _mla_specs(s_, t, grp)
    shp = jax.ShapeDtypeStruct((s_, MLA_HEADS * LANES), F32)
    return pl.pallas_call(
        body, name="mla_bwd", grid=(MLA_HEADS // grp, s_ // t), in_specs=[qs, qs, row, row, kv, kv],
        out_specs=[qs, kv, kv], out_shape=[shp, shp, shp], scratch_shapes=[pltpu.VMEM((grp, LANES, t), F32)],
        compiler_params=_cparams(("parallel", "arbitrary")),
    )(qc, dob, lse, delta, kc, vp)


def _pad_heads(w, nh, hd, axis):
    shp = w.shape
    w = w.reshape(shp[:axis] + (nh, hd) + shp[axis + 1:])
    pad = [(0, 0)] * w.ndim
    pad[axis + 1] = (0, LANES - hd)
    w = jnp.pad(w, pad)
    return w.reshape(shp[:axis] + (nh * LANES,) + shp[axis + 1:])


def _unpad_heads(w, nh, hd, axis):
    shp = w.shape
    w = w.reshape(shp[:axis] + (nh, LANES) + shp[axis + 1:])
    w = lax.slice_in_dim(w, 0, hd, axis=axis + 1)
    return w.reshape(shp[:axis] + (nh * hd,) + shp[axis + 1:])


PACK_W = 1024
ROW_TILE = 16
FULL_SHAPE = dict(w_in=(1024, 3488), w_uq=(384, 768), w_ukv=(256, 1024), w_o_swa=(512, 1024), w_o_mla=(512, 1024),
                  w_out=(1024, 1024), w_gate=(1024, 2816), w_up=(1024, 2816), w_down=(2816, 1024))
BIG = tuple(FULL_SHAPE)
ROW_SHARDED = ("w_out", "w_down")
W_IN_COLS = FULL_SHAPE["w_in"][1] // N_DEV
W_IN_ROWS = -(-W_IN_COLS // ROW_TILE) * ROW_TILE
FF_COLS = D_FF // N_DEV
OUT_ROWS = D_MODEL // N_DEV
SMALL_ROW0 = W_IN_ROWS + OUT_ROWS
SMALL_FLAT = (("w_uq", 0, 36), ("w_ukv", 48, 32), ("w_o_swa", 80, 64), ("w_o_mla", 144, 64))
SMALL_ROWS = 208
EARLY_ROWS = SMALL_ROW0 + SMALL_ROWS
LATE_ROWS = 3 * FF_COLS
PACK_ROWS = EARLY_ROWS + LATE_ROWS


def _shard_shape(n):
    r, c = FULL_SHAPE[n]
    return (r // N_DEV, c) if n in ROW_SHARDED else (r, c // N_DEV)


def _wire_pack(sh, dtype):
    c = lambda n: sh[n].astype(dtype)
    rows = [jnp.pad(c("w_in").T, ((0, W_IN_ROWS - W_IN_COLS), (0, 0))), c("w_out")]
    for n, _, r in SMALL_FLAT:
        rows.append(jnp.pad(c(n).reshape(r, PACK_W), ((0, -r % ROW_TILE), (0, 0))))
    return jnp.concatenate(rows + [c("w_gate").T, c("w_up").T, c("w_down")], 0)


MID_ROWS = OUT_ROWS + SMALL_ROWS


def _mid_unpack(p):
    out = dict(w_out=p[0:OUT_ROWS])
    for n, off, r in SMALL_FLAT:
        out[n] = p[OUT_ROWS + off:OUT_ROWS + off + r].reshape(_shard_shape(n))
    return out


def _w_in_row_maps():
    sp = lambda col: (col // W_IN_COLS) * W_IN_ROWS + col % W_IN_COLS
    fwd = np.full((P_W,), -1, np.int64)

    def put(t0, c0, n):
        fwd[t0:t0 + n] = [sp(c) for c in range(c0, c0 + n)]

    put(P_GA, IN_OFF[6], D_MODEL)
    put(P_GB, IN_OFF[7], D_MODEL)
    for h in range(SWA_HEADS):
        put(P_Q + LANES * h, IN_OFF[0] + HEAD_DIM * h, HEAD_DIM)
    put(P_QLAT, IN_OFF[3], Q_LORA)
    put(P_KR + KR_LANE, IN_OFF[5], MLA_ROPE)
    for h in range(SWA_KV_HEADS):
        put(P_K + LANES * h, IN_OFF[1] + HEAD_DIM * h, HEAD_DIM)
        put(P_V + LANES * h, IN_OFF[2] + HEAD_DIM * h, HEAD_DIM)
    put(P_KVLAT, IN_OFF[4], KV_LORA)
    inv = np.full((N_DEV * W_IN_ROWS,), -1, np.int64)
    inv[fwd[fwd >= 0]] = np.nonzero(fwd >= 0)[0]
    return fwd, inv


def _take_rows(src, idx, *, name, tile=2 * LANES):
    n_out, n_src, width = len(idx), src.shape[0], src.shape[1]
    assert n_out % tile == 0 and n_src % tile == 0
    n_tiles = n_out // tile
    blocks = [sorted({int(v) // tile for v in idx[i * tile:(i + 1) * tile] if v >= 0}) for i in range(n_tiles)]
    k_max = max(1, max(len(b) for b in blocks))
    tab = np.zeros((n_tiles, k_max), np.int32)
    sel = np.zeros((n_tiles, k_max, tile, tile), np.float32)
    for i, blks in enumerate(blocks):
        for m, b in enumerate(blks):
            tab[i, m] = b
            for r in range(tile):
                v = int(idx[i * tile + r])
                if v >= 0 and v // tile == b:
                    sel[i, m, r, v % tile] = 1.0

    def body(tab_ref, sel_ref, *refs):
        o_ref = refs[k_max]
        acc = jnp.dot(sel_ref[0, 0], refs[0][...], preferred_element_type=F32)
        for m in range(1, k_max):
            acc = acc + jnp.dot(sel_ref[0, m], refs[m][...], preferred_element_type=F32)
        o_ref[...] = acc.astype(o_ref.dtype)

    def src_spec(m):
        return pl.BlockSpec((tile, width), lambda i, t: (t[i * k_max + m], 0))

    return pl.pallas_call(
        body, name=name,
        grid_spec=pltpu.PrefetchScalarGridSpec(
            num_scalar_prefetch=1, grid=(n_tiles,),
            in_specs=[pl.BlockSpec((1, k_max, tile, tile), lambda i, t: (i, 0, 0, 0))] + [src_spec(m) for m in range(k_max)],
            out_specs=pl.BlockSpec((tile, width), lambda i, t: (i, 0))),
        out_shape=jax.ShapeDtypeStruct((n_out, width), src.dtype),
        compiler_params=_cparams(("parallel",)),
    )(jnp.asarray(tab.reshape(-1)), jnp.asarray(sel, src.dtype), *([src] * k_max))


def _w_in_operand(win_g):
    return _take_rows(win_g.reshape(N_DEV * W_IN_ROWS, PACK_W), _w_in_row_maps()[0], name="w_in_rows")


def _mid_operands(wout_g, small_g):
    def full(n, off, r):
        a = small_g[:, off:off + r].reshape((N_DEV,) + _shard_shape(n))
        return jnp.moveaxis(a, 0, 1).reshape(FULL_SHAPE[n])

    w = {n: full(n, off, r) for n, off, r in SMALL_FLAT}
    ukv = w["w_ukv"].reshape(KV_LORA, MLA_HEADS, MLA_NOPE + MLA_V)
    return dict(
        wout=wout_g.reshape(D_MODEL, D_MODEL),
        wuq=_pad_heads(w["w_uq"], MLA_HEADS, MLA_QK, 1),
        wuk=_pad_heads(ukv[:, :, :MLA_NOPE].reshape(KV_LORA, -1), MLA_HEADS, MLA_NOPE, 1),
        wuv=_pad_heads(ukv[:, :, MLA_NOPE:].reshape(KV_LORA, -1), MLA_HEADS, MLA_V, 1),
        woa=_pad_heads(w["w_o_swa"], SWA_HEADS, HEAD_DIM, 0),
        wob=_pad_heads(w["w_o_mla"], MLA_HEADS, MLA_V, 0),
    )


def _mid_grad_pack(g):
    uk = _unpad_heads(g["wukv"][:, :1024], MLA_HEADS, MLA_NOPE, 1).reshape(KV_LORA, MLA_HEADS, MLA_NOPE)
    uv = _unpad_heads(g["wukv"][:, 1024:], MLA_HEADS, MLA_V, 1).reshape(KV_LORA, MLA_HEADS, MLA_V)
    w = dict(w_uq=_unpad_heads(g["wuq"], MLA_HEADS, MLA_QK, 1), w_ukv=jnp.concatenate([uk, uv], 2).reshape(KV_LORA, -1),
             w_o_swa=_unpad_heads(g["woa"], SWA_HEADS, HEAD_DIM, 0), w_o_mla=_unpad_heads(g["wob"], MLA_HEADS, MLA_V, 0))

    def flat(n, r):
        rr, cc = FULL_SHAPE[n]
        a = jnp.moveaxis(w[n].reshape(rr, N_DEV, cc // N_DEV), 1, 0).reshape(N_DEV, r, PACK_W)
        return jnp.pad(a, ((0, 0), (0, -r % ROW_TILE), (0, 0))).astype(WIRE_DTYPE)

    return jnp.concatenate([g["wout"].reshape(N_DEV, OUT_ROWS, PACK_W)] + [flat(n, r) for n, _, r in SMALL_FLAT], 1)


def _w_in_grad_chunks(g_win_t):
    return _take_rows(g_win_t, _w_in_row_maps()[1], name="dw_in_rows").reshape(N_DEV, W_IN_ROWS, PACK_W)


def _local_step(x, tgt, win_t, small, weights, grads):
    s_ = x.shape[0]
    tabs = _rope_tables(s_)
    sink_b = jnp.broadcast_to(small["swa_sinks"].reshape(SWA_KV_HEADS, SWA_GROUP, 1), (SWA_KV_HEADS, SWA_GROUP, LANES))
    sink_b = jnp.pad(sink_b, ((0, 0), (0, SUBLANES - SWA_GROUP), (0, 0)))

    h, qa, ka, va, cq, ckv, kro, p = _proj_in(x, small["mix_norm_g"], win_t, small["q_norm_g"], small["kv_norm_g"], tabs)
    ops = weights.mid(cq)
    oa32, oa16, lse_a = _swa_fwd(qa, ka, va, sink_b)
    qc, kc, vp = _mla_up(cq, ckv, kro, ops["wuq"], ops["wuk"], ops["wuv"], tabs)
    ob32, ob16, lse_b = _mla_fwd(qc, kc, vp)
    ta, tb, y = _attn_out_gate(oa16, ob16, ops["woa"], ops["wob"], p)
    x1 = _mm(y, ops["wout"], "nn", name="out_proj", add=x, tm=1024, tn=1024)
    wgu_t, wd = weights.late(x1)
    h2, gu, act = _ffn_in_act(x1, small["ffn_norm_g"], wgu_t)

    dx2, dx2b, dg3, _, tot = _ffn_out_loss(act, wd, x1, small["final_norm_g"].reshape(1, D_MODEL), tgt)
    g = {}
    g_wd = _mm(act, dx2b, "tn", name="dw_down", tm=FF_TILE, tn=1024, tk=2048, out_dtype=WIRE_DTYPE)
    dgu = _d_act_swiglu(dx2b, wd, gu)
    g_wgu = _mm(dgu, h2, "tn", name="dw_ffn_in", tm=FF_TILE, tn=1024, tk=2048, out_dtype=WIRE_DTYPE)
    token = grads.late(g_wgu, g_wd)
    dx1, dx1b, dg2 = _mm_norm_bwd(dgu, wgu_t, x1, small["ffn_norm_g"] + token[0:1, 0:1], dx2, name="d_h2")
    g["wout"] = _mm(y, dx1b, "tn", name="dw_out", tm=1024, tn=1024, tk=1024, out_dtype=WIRE_DTYPE)
    dta, dtb, dgab = _d_y_gate(dx1b, ops["wout"], p, ta, tb)
    doa = _mm(dta, ops["woa"], "nt", name="d_oa", tm=1024, tn=1024)
    g["woa"] = _mm(oa16, dta, "tn", name="dw_o_swa", tm=1024, tn=1024, tk=1024)
    g["wob"] = _mm(ob16, dtb, "tn", name="dw_o_mla", tm=1024, tn=1024, tk=1024)
    dob16, delta_b = _mla_d_out(dtb, ops["wob"], ob32)
    dqc, dkc, dvp = _mla_bwd(qc, kc, vp, dob16, lse_b, delta_b)
    dqp, dkv, dkr, dqlat, dkvlat, dgq, dgkv = _mla_up_bwd(
        dqc, dkc, dvp, ops["wuq"], jnp.concatenate([ops["wuk"], ops["wuv"]], 1), p, small["q_norm_g"], small["kv_norm_g"], tabs)
    g["wuq"] = _mm(cq, dqp, "tn", name="dw_uq", tm=Q_LORA, tn=1024, tk=512)
    g["wukv"] = _mm(ckv, dkv, "tn", name="dw_ukv", tm=KV_LORA, tn=1024, tk=512)
    token = grads.mid(g)
    dqa, dka, dva, dsk = _swa_bwd(qa, ka, va, sink_b + token[0:1, 0:1], oa32, doa, lse_a)
    dp = _assemble_dp(dgab, dqa, dqlat, dkr, dka, dva, dkvlat, tabs)
    token = grads.last(_mm(dp, h, "tn", name="dw_in", tm=2176, tn=1024, tk=1024, out_dtype=WIRE_DTYPE))
    gx, _, dg1 = _mm_norm_bwd(dp, win_t, x, small["mix_norm_g"], dx1, name="d_h", after=token)

    sm = dict(mix_norm_g=dg1, ffn_norm_g=dg2, final_norm_g=dg3, q_norm_g=dgq, kv_norm_g=dgkv,
              swa_sinks=dsk[:, :SWA_GROUP, 0].reshape(1, SWA_HEADS))
    return tot, gx, sm


MESH = pl.DeviceIdType.MESH
ANY = pl.BlockSpec(memory_space=pl.ANY)


def _position():
    return lax.axis_index("x"), lax.axis_index("y"), lax.axis_index("c")


def _all_gather(block, pieces, shapes, *, name):
    n_out = len(shapes)
    n_rows = sum(p[3] for p in pieces)

    def body(x_ref, *refs):
        outs, (send_sems, recv_sems, local_sem) = refs[:n_out], refs[n_out:]
        x, y, c = _position()
        me, sibling = (x, y, c), (x, y, 1 - c)
        chips = [(1 - x, y), (x, 1 - y), (1 - x, 1 - y)]

        def dst(piece, blk):
            arr, lead, _, _ = piece
            return outs[arr].at[lead(4 * blk[0] + 2 * blk[1] + blk[2])]

        def own(piece):
            return x_ref.at[pl.ds(piece[2], piece[3])]

        def copies(k, blk, to, from_input):
            return [pltpu.make_async_remote_copy(
                src_ref=own(p) if from_input else dst(p, blk), dst_ref=dst(p, blk), send_sem=send_sems.at[k],
                recv_sem=recv_sems.at[k], device_id=to, device_id_type=MESH) for p in pieces]

        gathered_rows = x_ref.at[pl.ds(0, n_rows)]

        def whole_block(k):
            return pltpu.make_async_remote_copy(src_ref=gathered_rows, dst_ref=gathered_rows, send_sem=send_sems.at[k],
                                                recv_sem=recv_sems.at[k], device_id=me, device_id_type=MESH)

        for p in pieces:
            pltpu.make_async_copy(own(p), dst(p, me), local_sem).start()
        for cp in copies(0, me, sibling, True):
            cp.start()
        for j, chip in enumerate(chips):
            for cp in copies(1 + j, me, (*chip, c), True):
                cp.start()
        for j, chip in enumerate(chips):
            whole_block(1 + j).wait_recv()
            for cp in copies(4 + j, (*chip, c), sibling, False):
                cp.start()
        whole_block(0).wait_recv()
        for j in range(3):
            whole_block(4 + j).wait_recv()
        for k in range(7):
            whole_block(k).wait_send()
        pltpu.make_async_copy(gathered_rows, gathered_rows, local_sem).wait()

    return pl.pallas_call(
        body, name=name, out_shape=[jax.ShapeDtypeStruct(s, block.dtype) for s in shapes], in_specs=[ANY],
        out_specs=[ANY] * n_out,
        scratch_shapes=[pltpu.SemaphoreType.DMA((7,)), pltpu.SemaphoreType.DMA((7,)), pltpu.SemaphoreType.DMA],
    )(block)


HBM = pl.BlockSpec(memory_space=pltpu.HBM)
SEM = pl.BlockSpec(memory_space=pltpu.SEMAPHORE)
TILE_DEVS = FF_TILE // FF_COLS
GU_SHAPE = (2, 2, TILE_DEVS, FF_COLS, PACK_W)


def _gate_slab(d):
    return (d // TILE_DEVS, 0, d % TILE_DEVS)


def _up_slab(d):
    return (d // TILE_DEVS, 1, d % TILE_DEVS)
D_SHAPE = (N_DEV, FF_COLS, PACK_W)
LAND_SHAPE = (N_DEV, LATE_ROWS, PACK_W)


def _split_params():
    return pltpu.CompilerParams(has_side_effects=pltpu.SideEffectType.DATAFLOW_SIDE_EFFECTING)


def _peer(x, y, c, k):
    return ((1 - x) if k & 4 else x, (1 - y) if k & 2 else y, (1 - c) if k & 1 else c)


def _empty_hbm(shape, dtype):
    return pltpu.with_memory_space_constraint(lax.empty(shape, dtype), pltpu.HBM)


def _wait_all(rows, send_sems, recv_sems, me):
    for k in range(N_DEV - 1):
        cp = pltpu.make_async_remote_copy(src_ref=rows, dst_ref=rows, send_sem=send_sems.at[k], recv_sem=recv_sems.at[k],
                                          device_id=me, device_id_type=MESH)
        cp.wait_send()
        cp.wait_recv()


def _token_shape():
    return jax.ShapeDtypeStruct((SUBLANES, LANES), F32)


def _gather_start(pack, row0, pieces, shapes, *, name):
    n = len(shapes)

    def body(*refs):
        p_ref, bufs, send_sems, recv_sems, token = refs[0], refs[1:1 + n], refs[1 + n], refs[2 + n], refs[-1]
        x, y, c = _position()
        me = 4 * x + 2 * y + c
        for k in range(1, N_DEV):
            off = row0
            for buf, lead, rows in pieces:
                pltpu.make_async_remote_copy(
                    src_ref=p_ref.at[pl.ds(off, rows)], dst_ref=bufs[buf].at[lead(me)], send_sem=send_sems.at[k - 1],
                    recv_sem=recv_sems.at[k - 1], device_id=_peer(x, y, c, k), device_id_type=MESH).start()
                off += rows
        token[...] = jnp.zeros_like(token)

    sems, dt = pltpu.SemaphoreType.DMA((N_DEV - 1,)), pack.dtype
    return pl.pallas_call(
        body, name=name,
        out_shape=(sems, sems, pltpu.HBM(pack.shape, dt)) + tuple(pltpu.HBM(s, dt) for s in shapes) + (_token_shape(),),
        in_specs=(HBM,) * (1 + n), out_specs=(SEM, SEM) + (HBM,) * (1 + n) + (pl.BlockSpec(memory_space=pltpu.VMEM),),
        input_output_aliases={i: 2 + i for i in range(1 + n)}, compiler_params=_split_params(),
    )(pltpu.with_memory_space_constraint(pack, pltpu.HBM), *[_empty_hbm(s, dt) for s in shapes])


def _gather_wait(started, row0, n_rows, after, *, name):
    send_sems, recv_sems, pack, *bufs = started[:-1]
    n = len(bufs)

    def body(*refs):
        _wait_all(refs[0].at[pl.ds(row0, n_rows)], refs[1 + n], refs[2 + n], _position())

    outs = pl.pallas_call(
        body, name=name, out_shape=tuple(pltpu.HBM(a.shape, a.dtype) for a in (pack, *bufs)),
        in_specs=(HBM,) * (1 + n) + (SEM, SEM, ANY), out_specs=(HBM,) * (1 + n),
        input_output_aliases={i: i for i in range(1 + n)}, compiler_params=_split_params(),
    )(pack, *bufs, send_sems, recv_sems, after)
    return outs[0], outs[1:]


def _scatter_start(srcs, pieces, *, name):
    n = len(srcs)
    land_shape = (N_DEV, sum(p[2] for p in pieces), PACK_W)

    def body(*refs):
        src_refs, land_ref, send_sems, recv_sems, token = refs[:n], refs[n], refs[n + 1], refs[n + 2], refs[-1]
        x, y, c = _position()
        me = 4 * x + 2 * y + c
        for k in range(1, N_DEV):
            px, py, pc = _peer(x, y, c, k)
            off = 0
            for si, lead, rows in pieces:
                pltpu.make_async_remote_copy(
                    src_ref=src_refs[si].at[lead(4 * px + 2 * py + pc)], dst_ref=land_ref.at[me, pl.ds(off, rows)],
                    send_sem=send_sems.at[k - 1], recv_sem=recv_sems.at[k - 1], device_id=(px, py, pc),
                    device_id_type=MESH).start()
                off += rows
        token[...] = jnp.zeros_like(token)

    sems, dt = pltpu.SemaphoreType.DMA((N_DEV - 1,)), srcs[0].dtype
    return pl.pallas_call(
        body, name=name,
        out_shape=(sems, sems) + tuple(pltpu.HBM(a.shape, dt) for a in srcs) + (pltpu.HBM(land_shape, dt), _token_shape()),
        in_specs=(HBM,) * (n + 1), out_specs=(SEM, SEM) + (HBM,) * (n + 1) + (pl.BlockSpec(memory_space=pltpu.VMEM),),
        input_output_aliases={i: 2 + i for i in range(n + 1)}, compiler_params=_split_params(),
    )(*[pltpu.with_memory_space_constraint(a, pltpu.HBM) for a in srcs], _empty_hbm(land_shape, dt))


def _scatter_wait(started, after, *, name):
    send_sems, recv_sems, *bufs = started[:-1]
    n = len(bufs)

    def body(*refs):
        _wait_all(refs[n - 1].at[0], refs[n], refs[n + 1], _position())

    return pl.pallas_call(
        body, name=name, out_shape=tuple(pltpu.HBM(a.shape, a.dtype) for a in bufs),
        in_specs=(HBM,) * n + (SEM, SEM, ANY), out_specs=(HBM,) * n, input_output_aliases={i: i for i in range(n)},
        compiler_params=_split_params(),
    )(*bufs, send_sems, recv_sems, after)


def _peer_sum(own, own_lead, land, block, rows, idx, *, name):
    lead_rank = own.ndim - 2

    def body(idx_ref, own_ref, *refs):
        o_ref = refs[N_DEV - 1]
        acc = own_ref[(0,) * lead_rank].astype(F32)
        for k in range(N_DEV - 1):
            acc = acc + refs[k][0].astype(F32)
        o_ref[...] = acc

    own_spec = pl.BlockSpec((1,) * lead_rank + (rows, PACK_W), lambda i, t: own_lead(t[0]) + (0, 0))

    def land_spec(k):
        return pl.BlockSpec((1, rows, PACK_W), lambda i, t: (t[k + 1], block, 0))

    return pl.pallas_call(
        body, name=name,
        grid_spec=pltpu.PrefetchScalarGridSpec(
            num_scalar_prefetch=1, grid=(1,), in_specs=[own_spec] + [land_spec(k) for k in range(N_DEV - 1)],
            out_specs=pl.BlockSpec((rows, PACK_W), lambda i, t: (0, 0))),
        out_shape=jax.ShapeDtypeStruct((rows, PACK_W), F32), compiler_params=_cparams(("arbitrary",)),
    )(idx, own, *([land] * (N_DEV - 1)))


def _adamw(w, g, m, v):
    m = ADAM_B1 * m + (1.0 - ADAM_B1) * g
    v = ADAM_B2 * v + (1.0 - ADAM_B2) * (g * g)
    m_hat = m / (1.0 - ADAM_B1 ** ADAM_STEP)
    v_hat = v / (1.0 - ADAM_B2 ** ADAM_STEP)
    delta = -ADAM_LR * (m_hat / (jnp.sqrt(v_hat) + ADAM_EPS) + ADAM_WD * w)
    return delta, m, v


def _adamw_call(w, g, m, v, *, name, max_rows=256):
    _, r, c_ = w.shape
    tr = max_rows if r > max_rows and r % max_rows == 0 else r

    def body(w_ref, g_ref, m_ref, v_ref, d_ref, mo_ref, vo_ref):
        d, mn, vn = _adamw(w_ref[0], g_ref[...], m_ref[0], v_ref[0])
        d_ref[0] = d
        mo_ref[0] = mn
        vo_ref[0] = vn

    row3 = pl.BlockSpec((1, tr, c_), lambda i: (0, i, 0))
    shp = jax.ShapeDtypeStruct((1, r, c_), F32)
    return pl.pallas_call(
        body, name=name, grid=(r // tr,), in_specs=[row3, pl.BlockSpec((tr, c_), lambda i: (i, 0)), row3, row3],
        out_specs=[row3] * 3, out_shape=[shp] * 3, compiler_params=_cparams(("parallel",)),
    )(w, g, m, v)


SMALL = ("mix_norm_g", "ffn_norm_g", "final_norm_g", "q_norm_g", "kv_norm_g", "swa_sinks")
SMALL_W = dict(mix_norm_g=1024, ffn_norm_g=1024, final_norm_g=1024, q_norm_g=Q_LORA, kv_norm_g=KV_LORA, swa_sinks=SWA_HEADS)


def _small_adamw(parts, w, m, v):
    n_par = parts.shape[1] // SUBLANES

    def body(p_ref, w_ref, m_ref, v_ref, g_ref, d_ref, mo_ref, vo_ref):
        tot = p_ref[0]
        for dev in range(1, N_DEV):
            tot = tot + p_ref[dev]
        row_id = lax.broadcasted_iota(jnp.int32, (SUBLANES, PACK_W), 0)
        g = jnp.zeros((SUBLANES, PACK_W), F32)
        for k in range(n_par):
            g = jnp.where(row_id == k, jnp.sum(tot[k * SUBLANES:(k + 1) * SUBLANES, :], axis=0, keepdims=True), g)
        d, mn, vn = _adamw(w_ref[...], g, m_ref[...], v_ref[...])
        g_ref[...] = g
        d_ref[...] = d
        mo_ref[...] = mn
        vo_ref[...] = vn

    shp = jax.ShapeDtypeStruct((SUBLANES, PACK_W), F32)
    vm = pl.BlockSpec(memory_space=pltpu.VMEM)
    return pl.pallas_call(body, name="small_adamw", in_specs=[vm] * 4, out_specs=[vm] * 4, out_shape=[shp] * 4)(parts, w, m, v)


def _small_pack(d, rows_each):
    parts = [jnp.pad(d[n].astype(F32), ((0, 0), (0, PACK_W - SMALL_W[n]))) for n in SMALL]
    out = jnp.concatenate(parts, 0)
    pad = -out.shape[0] % SUBLANES
    return jnp.pad(out, ((0, pad), (0, 0)))


def kernel(x, mix_norm_g, w_in, swa_sinks, q_norm_g, w_uq, kv_norm_g, w_ukv, w_o_swa, w_o_mla, w_out, ffn_norm_g, w_gate, w_up, w_down, final_norm_g, loss_target, m_mix_norm_g, m_w_in, m_swa_sinks, m_q_norm_g, m_w_uq, m_kv_norm_g, m_w_ukv, m_w_o_swa, m_w_o_mla, m_w_out, m_ffn_norm_g, m_w_gate, m_w_up, m_w_down, m_final_norm_g, v_mix_norm_g, v_w_in, v_swa_sinks, v_q_norm_g, v_w_uq, v_kv_norm_g, v_w_ukv, v_w_o_swa, v_w_o_mla, v_w_out, v_ffn_norm_g, v_w_gate, v_w_up, v_w_down, v_final_norm_g):
    big_w = dict(w_in=w_in[0], w_uq=w_uq[0], w_ukv=w_ukv[0], w_o_swa=w_o_swa[0], w_o_mla=w_o_mla[0], w_out=w_out[0],
                 w_gate=w_gate[0], w_up=w_up[0], w_down=w_down[0])
    big_w3 = dict(w_in=w_in, w_uq=w_uq, w_ukv=w_ukv, w_o_swa=w_o_swa, w_o_mla=w_o_mla, w_out=w_out, w_gate=w_gate, w_up=w_up,
                  w_down=w_down)
    big_m = dict(w_in=m_w_in, w_uq=m_w_uq, w_ukv=m_w_ukv, w_o_swa=m_w_o_swa, w_o_mla=m_w_o_mla, w_out=m_w_out,
                 w_gate=m_w_gate, w_up=m_w_up, w_down=m_w_down)
    big_v = dict(w_in=v_w_in, w_uq=v_w_uq, w_ukv=v_w_ukv, w_o_swa=v_w_o_swa, w_o_mla=v_w_o_mla, w_out=v_w_out,
                 w_gate=v_w_gate, w_up=v_w_up, w_down=v_w_down)
    small_w = dict(mix_norm_g=mix_norm_g, ffn_norm_g=ffn_norm_g, final_norm_g=final_norm_g.reshape(1, D_MODEL),
                   q_norm_g=q_norm_g, kv_norm_g=kv_norm_g, swa_sinks=swa_sinks)
    small_m = dict(mix_norm_g=m_mix_norm_g, ffn_norm_g=m_ffn_norm_g, final_norm_g=m_final_norm_g.reshape(1, D_MODEL),
                   q_norm_g=m_q_norm_g, kv_norm_g=m_kv_norm_g, swa_sinks=m_swa_sinks)
    small_v = dict(mix_norm_g=v_mix_norm_g, ffn_norm_g=v_ffn_norm_g, final_norm_g=v_final_norm_g.reshape(1, D_MODEL),
                   q_norm_g=v_q_norm_g, kv_norm_g=v_kv_norm_g, swa_sinks=v_swa_sinks)

    px, py, pc = _position()
    me = 4 * px + 2 * py + pc
    idx = jnp.stack([me] + [4 * qx + 2 * qy + qc for qx, qy, qc in (_peer(px, py, pc, k) for k in range(1, N_DEV))])
    idx = idx.astype(jnp.int32)

    dev = lambda d: (d,)
    pack = _wire_pack(big_w, WIRE_DTYPE)
    win_g, = _all_gather(pack, ((0, dev, 0, W_IN_ROWS),), ((N_DEV, W_IN_ROWS, PACK_W),), name="ag_early")
    ag_mid = _gather_start(pack, W_IN_ROWS, ((0, dev, OUT_ROWS), (1, dev, SMALL_ROWS)),
                           ((N_DEV, OUT_ROWS, PACK_W), (N_DEV, SMALL_ROWS, PACK_W)), name="ag_mid_start")
    ag = {}

    def own_rows(r0, r1, shape):
        return pack[r0:r1].reshape(shape)

    def mid_weights(after):
        pack_mid, (wout_g, small_g) = _gather_wait(ag_mid, W_IN_ROWS, MID_ROWS, after, name="ag_mid_wait")
        ag["late"] = _gather_start(pack_mid, EARLY_ROWS, ((0, _gate_slab, FF_COLS), (0, _up_slab, FF_COLS), (1, dev, FF_COLS)),
                                   (GU_SHAPE, D_SHAPE), name="ag_late_start")
        wout_g = lax.dynamic_update_slice(wout_g, own_rows(W_IN_ROWS, SMALL_ROW0, (1, OUT_ROWS, PACK_W)), (me, 0, 0))
        small_g = lax.dynamic_update_slice(small_g, own_rows(SMALL_ROW0, EARLY_ROWS, (1, SMALL_ROWS, PACK_W)), (me, 0, 0))
        ops = _mid_operands(wout_g, small_g)
        ops["wuq"] = ops["wuq"] + ag["late"][-1][0:1, 0:1].astype(ops["wuq"].dtype)
        return ops

    def late_weights(after):
        _, (gu, d) = _gather_wait(ag["late"], EARLY_ROWS, LATE_ROWS, after, name="ag_late_wait")
        slab = (1, 1, 1, FF_COLS, PACK_W)
        gu = lax.dynamic_update_slice(gu, own_rows(EARLY_ROWS, EARLY_ROWS + FF_COLS, slab), _gate_slab(me) + (0, 0))
        gu = lax.dynamic_update_slice(gu, own_rows(EARLY_ROWS + FF_COLS, EARLY_ROWS + 2 * FF_COLS, slab), _up_slab(me) + (0, 0))
        d = lax.dynamic_update_slice(d, own_rows(EARLY_ROWS + 2 * FF_COLS, PACK_ROWS, (1, FF_COLS, PACK_W)), (me, 0, 0))
        return gu.reshape(2 * D_FF, D_MODEL), d.reshape(D_FF, D_MODEL)

    rs = {}

    def late_grads(g_gu, g_d):
        rs["late"] = _scatter_start([g_gu.reshape(GU_SHAPE), g_d.reshape(D_SHAPE)],
                                    ((0, _gate_slab, FF_COLS), (0, _up_slab, FF_COLS), (1, dev, FF_COLS)),
                                    name="rs_late_start")
        return rs["late"][-1]

    def mid_grads(g):
        rs["mid"] = _scatter_start([_mid_grad_pack(g)], ((0, dev, MID_ROWS),), name="rs_mid_start")
        return rs["mid"][-1]

    def last_grads(g_win_t):
        rs["last"] = _scatter_start([_w_in_grad_chunks(g_win_t)], ((0, dev, W_IN_ROWS),), name="rs_last_start")
        return rs["last"][-1]

    first_w = dict(small_w, mix_norm_g=mix_norm_g + ag_mid[-1][0:1, 0:1])
    loss_tot, gx, g_small = _local_step(
        x[0], loss_target[0], _w_in_operand(win_g), first_w, types.SimpleNamespace(mid=mid_weights, late=late_weights),
        types.SimpleNamespace(late=late_grads, mid=mid_grads, last=last_grads))

    g_gu, g_d, land_late = _scatter_wait(rs["late"], gx, name="rs_late_wait")
    g_mid, land_mid = _scatter_wait(rs["mid"], gx, name="rs_mid_wait")
    g_win, land_last = _scatter_wait(rs["last"], gx, name="rs_last_wait")
    gw = dict(w_gate=_peer_sum(g_gu, _gate_slab, land_late, 0, FF_COLS, idx, name="rs_sum_gate").T,
              w_up=_peer_sum(g_gu, _up_slab, land_late, 1, FF_COLS, idx, name="rs_sum_up").T,
              w_down=_peer_sum(g_d, dev, land_late, 2, FF_COLS, idx, name="rs_sum_down"),
              w_in=_peer_sum(g_win, dev, land_last, 0, W_IN_ROWS, idx, name="rs_sum_in")[0:W_IN_COLS].T)
    gw.update(_mid_unpack(_peer_sum(g_mid, dev, land_mid, 0, MID_ROWS, idx, name="rs_sum_mid")))
    dw, mw, vw = {}, {}, {}
    for n in BIG:
        dw[n], mw[n], vw[n] = _adamw_call(big_w3[n], gw[n], big_m[n], big_v[n], name="adamw_" + n)
    gw = {n: g[None] for n, g in gw.items()}

    loss_rows = jnp.pad(loss_tot[0:1, 0:1], ((0, SUBLANES - 1), (0, PACK_W - 1)))
    small_rows = jnp.concatenate([_small_pack(g_small_rows(g_small), SUBLANES), loss_rows], 0)
    parts, = _all_gather(small_rows, ((0, lambda d: (d,), 0, small_rows.shape[0]),), ((N_DEV,) + small_rows.shape,),
                         name="ag_small")
    gs, ds, ms, vs = _small_adamw(parts, _small_pack(small_w, 1), _small_pack(small_m, 1), _small_pack(small_v, 1))
    loss = gs[len(SMALL), 0]

    def small_out(packed):
        out = {}
        for k, n in enumerate(SMALL):
            out[n] = packed[k:k + 1, :SMALL_W[n]]
        out["final_norm_g"] = out["final_norm_g"].reshape(D_MODEL)
        return out

    gs, ds, ms, vs = small_out(gs), small_out(ds), small_out(ms), small_out(vs)

    order = ("mix_norm_g", "w_in", "swa_sinks", "q_norm_g", "w_uq", "kv_norm_g", "w_ukv", "w_o_swa", "w_o_mla", "w_out",
             "ffn_norm_g", "w_gate", "w_up", "w_down", "final_norm_g")

    def leaves(big, small):
        return [big[n] if n in big else small[n] for n in order]

    return (loss, gx[None], *leaves(gw, gs), *leaves(dw, ds), *leaves(mw, ms), *leaves(vw, vs))


def g_small_rows(g_small):
    out = dict(g_small)
    out["swa_sinks"] = jnp.pad(g_small["swa_sinks"], ((0, SUBLANES - 1), (0, 0)))
    return out
```

```python
import types

import numpy as np
import jax
import jax.numpy as jnp
from jax import lax
from jax.experimental import pallas as pl
from jax.experimental.pallas import tpu as pltpu

F32 = jnp.float32
MXU_DTYPE = jnp.bfloat16
WIRE_DTYPE = jnp.bfloat16

D_MODEL = 1024
EPS = 1e-6
ROPE_THETA = 10000.0
BLOCK = 128
HEAD_DIM = 64
SWA_HEADS = 8
SWA_KV_HEADS = 2
SWA_GROUP = SWA_HEADS // SWA_KV_HEADS
MLA_HEADS = 8
MLA_NOPE = 64
MLA_ROPE = 32
MLA_V = 64
MLA_QK = MLA_NOPE + MLA_ROPE
Q_LORA = 384
KV_LORA = 256
D_FF = 2816
IN_SIZES = (512, 128, 128, Q_LORA, KV_LORA, MLA_ROPE, D_MODEL, D_MODEL)
IN_OFF = tuple(int(v) for v in np.cumsum((0,) + IN_SIZES))
ADAM_LR, ADAM_B1, ADAM_B2, ADAM_EPS, ADAM_WD, ADAM_STEP = 0.001, 0.9, 0.999, 1e-08, 0.01, 10

LANES = 128
SUBLANES = 8
VMEM_LIMIT = 48 * 1024 * 1024
N_DEV = 8
AXES = ("x", "y", "c")

P_GA, P_GB, P_Q, P_QLAT, P_KR, P_K, P_V, P_KVLAT, P_W = 0, 1024, 2048, 3072, 3456, 3584, 3840, 4096, 4352
KR_LANE = 64

LOG2E = 1.4426950408889634

NT = (((1,), (1,)), ((), ()))
NN = (((1,), (0,)), ((), ()))
TN = (((0,), (0,)), ((), ()))


def _cparams(sem):
    return pltpu.CompilerParams(dimension_semantics=sem, vmem_limit_bytes=VMEM_LIMIT)


def _mm(a, b, mode, *, name, out_dtype=F32, add=None, after=None, tm=512, tn=512, tk=None):
    if mode == "nn":
        (M, K), (K2, N) = a.shape, b.shape
    elif mode == "nt":
        (M, K), (N, K2) = a.shape, b.shape
    else:
        (K, M), (K2, N) = a.shape, b.shape
    assert K == K2, (a.shape, b.shape, mode)
    tm, tn, tk = min(tm, M), min(tn, N), K if tk is None else min(tk, K)
    assert M % tm == 0 and N % tn == 0 and K % tk == 0, (M, N, K, tm, tn, tk)
    nk = K // tk
    dn = {"nn": NN, "nt": NT, "tn": TN}[mode]
    if mode == "tn":
        a_spec = pl.BlockSpec((tk, tm), lambda i, j, k: (k, i))
    else:
        a_spec = pl.BlockSpec((tm, tk), lambda i, j, k: (i, k))
    once = dict(pipeline_mode=pl.Buffered(1)) if (nk == 1 and tn == N) else {}
    if mode == "nt":
        b_spec = pl.BlockSpec((tn, tk), lambda i, j, k: (j, k), **once)
    else:
        b_spec = pl.BlockSpec((tk, tn), lambda i, j, k: (k, j), **once)
    o_spec = pl.BlockSpec((tm, tn), lambda i, j, k: (i, j))
    has_add, has_after = add is not None, after is not None

    def body(*refs):
        a_ref, b_ref = refs[0], refs[1]
        add_ref = refs[2] if has_add else None
        o_ref = refs[2 + has_add + has_after]
        p = lax.dot_general(a_ref[...], b_ref[...], dn, preferred_element_type=F32)

        def finish(acc):
            if has_add:
                acc = acc + add_ref[...]
            o_ref[...] = acc.astype(o_ref.dtype)

        if nk == 1:
            finish(p)
        else:
            acc_ref = refs[-1]
            k = pl.program_id(2)

            @pl.when(k == 0)
            def _():
                acc_ref[...] = p

            @pl.when((k > 0) & (k < nk - 1))
            def _():
                acc_ref[...] += p

            @pl.when(k == nk - 1)
            def _():
                finish(acc_ref[...] + p)

    ins = [a, b] + ([add] if has_add else []) + ([after] if has_after else [])
    in_specs = [a_spec, b_spec] + ([o_spec] if has_add else []) + ([pl.BlockSpec(memory_space=pl.ANY)] if has_after else [])
    return pl.pallas_call(
        body, name=name, grid=(M // tm, N // tn, nk), in_specs=in_specs, out_specs=o_spec,
        out_shape=jax.ShapeDtypeStruct((M, N), out_dtype),
        scratch_shapes=[pltpu.VMEM((tm, tn), F32)] if nk > 1 else [],
        compiler_params=_cparams(("parallel", "parallel", "arbitrary")),
    )(*ins)


def _rows(ts, w, cb=0):
    return pl.BlockSpec((ts, w), lambda i: (i, cb))


def _const(r, w):
    return pl.BlockSpec((r, w), lambda i: (0, 0))


def _sublane_sum(v):
    ts, c = v.shape
    return jnp.sum(v.reshape(ts // SUBLANES, SUBLANES, c), axis=0)


def _sigmoid(v):
    return 1.0 / (1.0 + jnp.exp(-v))


def _rope(v, cos, s_up, s_dn, up, dn):
    return v * cos + pltpu.roll(v, up, 1) * s_up + pltpu.roll(v, dn, 1) * s_dn


def _rope_t(dv, cos, s_up, s_dn, up, dn):
    return dv * cos + pltpu.roll(dv * s_up, dn, 1) + pltpu.roll(dv * s_dn, up, 1)


def _rope_tables(seq):
    pos = np.arange(seq, dtype=np.float32)[:, None]

    def base(dim):
        inv = np.float32(ROPE_THETA) ** (-np.arange(0, dim, 2, dtype=np.float32) / np.float32(dim))
        ang = (pos * inv.astype(np.float32)[None, :]).astype(np.float32)
        return np.cos(ang).astype(np.float32), np.sin(ang).astype(np.float32)

    z = lambda n: np.zeros((seq, n), np.float32)
    ca, sa = base(HEAD_DIM)
    a_cos = np.concatenate([ca, ca, z(64)], 1)
    a_up = np.concatenate([-sa, z(96)], 1)
    a_dn = np.concatenate([z(32), sa, z(64)], 1)
    cb, sb = base(MLA_ROPE)
    one = np.ones((seq, 64), np.float32)
    q_cos = np.concatenate([one, cb, cb, z(32)], 1)
    k_cos = np.concatenate([z(64), cb, cb, z(32)], 1)
    b_up = np.concatenate([z(64), -sb, z(48)], 1)
    b_dn = np.concatenate([z(80), sb, z(32)], 1)
    return tuple(jnp.asarray(t) for t in (a_cos, a_up, a_dn, q_cos, k_cos, b_up, b_dn))


def _rms(v, g):
    return v * lax.rsqrt(jnp.mean(v * v, axis=-1, keepdims=True) + EPS) * g


def _rms_bwd(v, g, d):
    r = lax.rsqrt(jnp.mean(v * v, axis=-1, keepdims=True) + EPS)
    xh = v * r
    dxh = d * g
    return r * (dxh - xh * jnp.mean(dxh * xh, axis=-1, keepdims=True)), d * xh


F_GA, F_GB, F_KVLAT, F_QLAT, F_W = 0, 1024, 2048, 2304, 2688


def _proj_in(x, g, w_t, gq, gkv, tabs, *, tm=512):
    s_, c = x.shape
    a_cos, a_up, a_dn, _, k_cos, b_up, b_dn = tabs

    def body(x_ref, g_ref, w_ref, gq_ref, gkv_ref, ac, au, ad, kc, bu, bd,
             h_ref, qa_ref, ka_ref, va_ref, cq_ref, ckv_ref, kro_ref, pf_ref):
        h = _rms(x_ref[...], g_ref[...]).astype(h_ref.dtype)
        h_ref[...] = h
        mm = lambda a, b: lax.dot_general(h, w_ref[a:b, :], NT, preferred_element_type=F32)
        pf_ref[:, F_GA:F_KVLAT] = mm(P_GA, P_Q)
        c_, u_, d_ = ac[...], au[...], ad[...]
        q = mm(P_Q, P_QLAT)
        for hd in range(SWA_HEADS):
            sl = slice(hd * LANES, (hd + 1) * LANES)
            qa_ref[:, sl] = _rope(q[:, sl], c_, u_, d_, 96, 32).astype(qa_ref.dtype)
        kv = mm(P_KR, P_KVLAT)
        kro_ref[...] = _rope(kv[:, :LANES], kc[...], bu[...], bd[...], 112, 16)
        for hd in range(SWA_KV_HEADS):
            sl = slice((1 + hd) * LANES, (2 + hd) * LANES)
            ka_ref[:, hd * LANES:(hd + 1) * LANES] = _rope(kv[:, sl], c_, u_, d_, 96, 32).astype(ka_ref.dtype)
        va_ref[...] = kv[:, P_V - P_KR:].astype(va_ref.dtype)
        for a, b, f0, gref, dst in ((P_QLAT, P_KR, F_QLAT, gq_ref, cq_ref), (P_KVLAT, P_W, F_KVLAT, gkv_ref, ckv_ref)):
            v = mm(a, b)
            pf_ref[:, f0:f0 + b - a] = v
            r = lax.rsqrt(jnp.mean(v * v, axis=-1, keepdims=True) + EPS)
            dst[...] = (v * r * gref[...]).astype(dst.dtype)

    tab = _rows(tm, LANES)
    widths = (c, SWA_HEADS * LANES, SWA_KV_HEADS * LANES, SWA_KV_HEADS * LANES, Q_LORA, KV_LORA)
    return pl.pallas_call(
        body, name="proj_in", grid=(s_ // tm,),
        in_specs=[_rows(tm, c), _const(1, c), pl.BlockSpec((P_W, c), lambda i: (0, 0), pipeline_mode=pl.Buffered(1)),
                  _const(1, Q_LORA), _const(1, KV_LORA), tab, tab, tab, tab, tab, tab],
        out_specs=[_rows(tm, w) for w in widths] + [tab, _rows(tm, F_W)],
        out_shape=[jax.ShapeDtypeStruct((s_, w), MXU_DTYPE) for w in widths]
        + [jax.ShapeDtypeStruct((s_, LANES), F32), jax.ShapeDtypeStruct((s_, F_W), F32)],
        compiler_params=_cparams(("parallel",)),
    )(x, g, w_t, gq, gkv, a_cos, a_up, a_dn, k_cos, b_up, b_dn)


def _mm_norm_bwd(a, b, x, g, res, *, name, after=None, tm=512):
    s_, kk = a.shape
    c = b.shape[1]
    has_after = after is not None

    def body(*refs):
        a_ref, b_ref, x_ref, g_ref, res_ref = refs[:5]
        dx_ref, dxb_ref, dg_ref = refs[5 + has_after:]
        d = jnp.dot(a_ref[...], b_ref[...], preferred_element_type=F32)
        dx, gg = _rms_bwd(x_ref[...], g_ref[...], d)
        dx = dx + res_ref[...]
        dx_ref[...] = dx
        dxb_ref[...] = dx.astype(dxb_ref.dtype)

        @pl.when(pl.program_id(0) == 0)
        def _():
            dg_ref[...] = jnp.zeros(dg_ref.shape, F32)

        dg_ref[...] += _sublane_sum(gg)

    row = _rows(tm, c)
    in_specs = [_rows(tm, kk), pl.BlockSpec((kk, c), lambda i: (0, 0), pipeline_mode=pl.Buffered(1)), row, _const(1, c), row]
    return pl.pallas_call(
        body, name=name, grid=(s_ // tm,), in_specs=in_specs + ([pl.BlockSpec(memory_space=pl.ANY)] if has_after else []),
        out_specs=[row, row, _const(SUBLANES, c)],
        out_shape=[jax.ShapeDtypeStruct((s_, c), F32), jax.ShapeDtypeStruct((s_, c), MXU_DTYPE),
                   jax.ShapeDtypeStruct((SUBLANES, c), F32)],
        compiler_params=_cparams(("arbitrary",)),
    )(*([a, b, x, g, res] + ([after] if has_after else [])))


def _mla_up(cq, ckv, kro, wuq, wuk, wuv, tabs, *, ts=512):
    s_ = cq.shape[0]
    _, _, _, q_cos, _, b_up, b_dn = tabs

    def body(cq_ref, ckv_ref, kr_ref, wq_ref, wk_ref, wv_ref, qc, bu, bd, qo_ref, ko_ref, vo_ref):
        c_, u_, d_ = qc[...], bu[...], bd[...]
        kr = kr_ref[...]
        ckv_ = ckv_ref[...]
        vo_ref[...] = jnp.dot(ckv_, wv_ref[...], preferred_element_type=F32).astype(vo_ref.dtype)
        q = jnp.dot(cq_ref[...], wq_ref[...], preferred_element_type=F32)
        k = jnp.dot(ckv_, wk_ref[...], preferred_element_type=F32)
        for h in range(MLA_HEADS):
            sl = slice(h * LANES, (h + 1) * LANES)
            qo_ref[:, sl] = _rope(q[:, sl], c_, u_, d_, 112, 16).astype(qo_ref.dtype)
            ko_ref[:, sl] = (k[:, sl] + kr).astype(ko_ref.dtype)

    tab, out = _rows(ts, LANES), _rows(ts, 1024)
    return pl.pallas_call(
        body, name="mla_up", grid=(s_ // ts,),
        in_specs=[_rows(ts, Q_LORA), _rows(ts, KV_LORA), tab, _const(Q_LORA, 1024), _const(KV_LORA, 1024),
                  _const(KV_LORA, 1024), tab, tab, tab],
        out_specs=[out, out, out], out_shape=[jax.ShapeDtypeStruct((s_, 1024), MXU_DTYPE)] * 3,
        compiler_params=_cparams(("parallel",)),
    )(cq, ckv, kro, wuq, wuk, wuv, q_cos, b_up, b_dn)


def _mla_up_bwd(dqc, dkc, dvp, wuq, wukv, p, gq, gkv, tabs, *, ts=256):
    s_ = dqc.shape[0]
    _, _, _, q_cos, k_cos, b_up, b_dn = tabs

    def body(dq_ref, dk_ref, dv_ref, wq_ref, wkv_ref, ql_ref, kvl_ref, gq_ref, gkv_ref, qc, kc, bu, bd,
             dqo_ref, dkvo_ref, dkr_ref, dql_ref, dkvl_ref, dgq_ref, dgkv_ref):
        c_, u_, d_ = qc[...], bu[...], bd[...]
        tot = jnp.zeros((ts, LANES), F32)
        for h in range(MLA_HEADS):
            sl = slice(h * LANES, (h + 1) * LANES)
            dqo_ref[:, sl] = _rope_t(dq_ref[:, sl], c_, u_, d_, 112, 16).astype(dqo_ref.dtype)
            dk = dk_ref[:, sl]
            dkvo_ref[:, sl] = dk.astype(dkvo_ref.dtype)
            tot = tot + dk
        dkvo_ref[:, 1024:2048] = dv_ref[...].astype(dkvo_ref.dtype)
        dkr_ref[...] = _rope_t(tot, kc[...], u_, d_, 112, 16).astype(dkr_ref.dtype)

        @pl.when(pl.program_id(0) == 0)
        def _():
            dgq_ref[...] = jnp.zeros(dgq_ref.shape, F32)
            dgkv_ref[...] = jnp.zeros(dgkv_ref.shape, F32)

        for do_ref, w_ref, x_ref, g_ref, dx_ref, dg_ref in ((dqo_ref, wq_ref, ql_ref, gq_ref, dql_ref, dgq_ref),
                                                            (dkvo_ref, wkv_ref, kvl_ref, gkv_ref, dkvl_ref, dgkv_ref)):
            d = lax.dot_general(do_ref[...], w_ref[...], NT, preferred_element_type=F32)
            dx, gg = _rms_bwd(x_ref[...], g_ref[...], d)
            dx_ref[...] = dx.astype(dx_ref.dtype)
            dg_ref[...] += _sublane_sum(gg)

    tab = _rows(ts, LANES)
    return pl.pallas_call(
        body, name="mla_up_bwd", grid=(s_ // ts,),
        in_specs=[_rows(ts, 1024), _rows(ts, 1024), _rows(ts, 1024), _const(Q_LORA, 1024), _const(KV_LORA, 2048),
                  _rows(ts, Q_LORA, F_QLAT // Q_LORA), _rows(ts, KV_LORA, F_KVLAT // KV_LORA),
                  _const(1, Q_LORA), _const(1, KV_LORA), tab, tab, tab, tab],
        out_specs=[_rows(ts, 1024), _rows(ts, 2048), _rows(ts, LANES), _rows(ts, Q_LORA), _rows(ts, KV_LORA),
                   _const(SUBLANES, Q_LORA), _const(SUBLANES, KV_LORA)],
        out_shape=[jax.ShapeDtypeStruct((s_, 1024), MXU_DTYPE), jax.ShapeDtypeStruct((s_, 2048), MXU_DTYPE),
                   jax.ShapeDtypeStruct((s_, LANES), MXU_DTYPE), jax.ShapeDtypeStruct((s_, Q_LORA), MXU_DTYPE),
                   jax.ShapeDtypeStruct((s_, KV_LORA), MXU_DTYPE), jax.ShapeDtypeStruct((SUBLANES, Q_LORA), F32),
                   jax.ShapeDtypeStruct((SUBLANES, KV_LORA), F32)],
        compiler_params=_cparams(("arbitrary",)),
    )(dqc, dkc, dvp, wuq, wukv, p, p, gq, gkv, q_cos, k_cos, b_up, b_dn)


def _assemble_dp(dgab, dqa, dqlat, dkr, dka, dva, dkvlat, tabs, *, ts=256):
    s_ = dqa.shape[0]
    a_cos, a_up, a_dn = tabs[0], tabs[1], tabs[2]

    def body(dg_ref, dq_ref, dql_ref, dkr_ref, dk_ref, dv_ref, dkvl_ref, ac, au, ad, o_ref):
        c_, u_, d_ = ac[...], au[...], ad[...]
        o_ref[:, P_GA:P_Q] = dg_ref[...]
        for h in range(SWA_HEADS):
            sl = slice(h * LANES, (h + 1) * LANES)
            o_ref[:, P_Q + h * LANES:P_Q + (h + 1) * LANES] = _rope_t(dq_ref[:, sl], c_, u_, d_, 96, 32).astype(o_ref.dtype)
        o_ref[:, P_QLAT:P_KR] = dql_ref[...]
        o_ref[:, P_KR:P_K] = dkr_ref[...]
        for h in range(SWA_KV_HEADS):
            sl = slice(h * LANES, (h + 1) * LANES)
            o_ref[:, P_K + h * LANES:P_K + (h + 1) * LANES] = _rope_t(dk_ref[:, sl], c_, u_, d_, 96, 32).astype(o_ref.dtype)
        o_ref[:, P_V:P_KVLAT] = dv_ref[...]
        o_ref[:, P_KVLAT:P_W] = dkvl_ref[...]

    tab = _rows(ts, LANES)
    return pl.pallas_call(
        body, name="assemble_dp", grid=(s_ // ts,),
        in_specs=[_rows(ts, 2048), _rows(ts, 1024), _rows(ts, Q_LORA), _rows(ts, LANES), _rows(ts, 256), _rows(ts, 256),
                  _rows(ts, KV_LORA), tab, tab, tab],
        out_specs=_rows(ts, P_W), out_shape=jax.ShapeDtypeStruct((s_, P_W), MXU_DTYPE),
        compiler_params=_cparams(("parallel",)),
    )(dgab, dqa, dqlat, dkr, dka, dva, dkvlat, a_cos, a_up, a_dn)


def _attn_out_gate(oa, ob, woa, wob, p, *, ts=512):
    s_ = p.shape[0]

    def body(oa_ref, ob_ref, wa_ref, wb_ref, ga_ref, gb_ref, ta_ref, tb_ref, y_ref):
        ta = jnp.dot(oa_ref[...], wa_ref[...], preferred_element_type=F32)
        tb = jnp.dot(ob_ref[...], wb_ref[...], preferred_element_type=F32)
        ta_ref[...] = ta
        tb_ref[...] = tb
        y_ref[...] = (_sigmoid(ga_ref[...]) * ta + _sigmoid(gb_ref[...]) * tb).astype(y_ref.dtype)

    w = _const(1024, 1024)
    return pl.pallas_call(
        body, name="attn_out_gate", grid=(s_ // ts,),
        in_specs=[_rows(ts, 1024), _rows(ts, 1024), w, w, _rows(ts, 1024, F_GA // 1024), _rows(ts, 1024, F_GB // 1024)],
        out_specs=[_rows(ts, 1024)] * 3,
        out_shape=[jax.ShapeDtypeStruct((s_, 1024), F32)] * 2 + [jax.ShapeDtypeStruct((s_, 1024), MXU_DTYPE)],
        compiler_params=_cparams(("parallel",)),
    )(oa, ob, woa, wob, p, p)


def _d_y_gate(dx1b, wout, p, ta, tb, *, ts=512):
    s_ = p.shape[0]

    def body(dx_ref, w_ref, ga_ref, gb_ref, ta_ref, tb_ref, dta_ref, dtb_ref, dg_ref):
        d = lax.dot_general(dx_ref[...], w_ref[...], NT, preferred_element_type=F32)
        sa, sb = _sigmoid(ga_ref[...]), _sigmoid(gb_ref[...])
        dta_ref[...] = (d * sa).astype(dta_ref.dtype)
        dtb_ref[...] = (d * sb).astype(dtb_ref.dtype)
        dg_ref[:, 0:1024] = (d * ta_ref[...] * (sa * (1.0 - sa))).astype(dg_ref.dtype)
        dg_ref[:, 1024:2048] = (d * tb_ref[...] * (sb * (1.0 - sb))).astype(dg_ref.dtype)

    return pl.pallas_call(
        body, name="d_y_gate", grid=(s_ // ts,),
        in_specs=[_rows(ts, 1024), _const(1024, 1024), _rows(ts, 1024, F_GA // 1024), _rows(ts, 1024, F_GB // 1024),
                  _rows(ts, 1024), _rows(ts, 1024)],
        out_specs=[_rows(ts, 1024), _rows(ts, 1024), _rows(ts, 2048)],
        out_shape=[jax.ShapeDtypeStruct((s_, 1024), MXU_DTYPE)] * 2 + [jax.ShapeDtypeStruct((s_, 2048), MXU_DTYPE)],
        compiler_params=_cparams(("parallel",)),
    )(dx1b, wout, p, p, ta, tb)


FF_TILE = D_FF // 2


def _ffn_in_act(x1, g, wgu_t, *, tm=512):
    s_ = x1.shape[0]
    n = s_ // tm

    def body(x_ref, g_ref, w_ref, h_ref, gu_ref, a_ref):
        h = _rms(x_ref[...], g_ref[...]).astype(h_ref.dtype)
        h_ref[...] = h
        p = lax.dot_general(h, w_ref[...], NT, preferred_element_type=F32)
        gu_ref[...] = p
        gate = p[:, :FF_TILE]
        a_ref[...] = (gate * _sigmoid(gate) * p[:, FF_TILE:]).astype(a_ref.dtype)

    return pl.pallas_call(
        body, name="ffn_in", grid=(2, s_ // tm),
        in_specs=[pl.BlockSpec((tm, D_MODEL), lambda j, i: (i, 0)), pl.BlockSpec((1, D_MODEL), lambda j, i: (0, 0)),
                  pl.BlockSpec((2 * FF_TILE, D_MODEL), lambda j, i: (j, 0))],
        out_specs=[pl.BlockSpec((tm, D_MODEL), lambda j, i: (i + j * (n - 1 - i), 0)),
                   pl.BlockSpec((tm, 2 * FF_TILE), lambda j, i: (i, j)),
                   pl.BlockSpec((tm, FF_TILE), lambda j, i: (i, j))],
        out_shape=[jax.ShapeDtypeStruct((s_, D_MODEL), MXU_DTYPE), jax.ShapeDtypeStruct((s_, 2 * D_FF), F32),
                   jax.ShapeDtypeStruct((s_, D_FF), MXU_DTYPE)],
        compiler_params=_cparams(("arbitrary", "arbitrary")),
    )(x1, g, wgu_t)


def _d_act_swiglu(dx2b, wd, gu, *, tm=512):
    s_ = dx2b.shape[0]

    def body(d_ref, w_ref, gu_ref, o_ref):
        da = lax.dot_general(d_ref[...], w_ref[...], NT, preferred_element_type=F32)
        g, u = gu_ref[:, :FF_TILE], gu_ref[:, FF_TILE:]
        sg = _sigmoid(g)
        o_ref[:, :FF_TILE] = (da * u * (sg * (1.0 + g * (1.0 - sg)))).astype(o_ref.dtype)
        o_ref[:, FF_TILE:] = (da * (g * sg)).astype(o_ref.dtype)

    gu_spec = pl.BlockSpec((tm, 2 * FF_TILE), lambda j, i: (i, j))
    return pl.pallas_call(
        body, name="d_act", grid=(2, s_ // tm),
        in_specs=[pl.BlockSpec((tm, D_MODEL), lambda j, i: (i, 0)), pl.BlockSpec((FF_TILE, D_MODEL), lambda j, i: (j, 0)), gu_spec],
        out_specs=gu_spec, out_shape=jax.ShapeDtypeStruct((s_, 2 * D_FF), MXU_DTYPE),
        compiler_params=_cparams(("parallel", "parallel")),
    )(dx2b, wd, gu)


def _ffn_out_loss(act, wd, x1, g, tgt, *, ts=512):
    s_, c = x1.shape
    kk = act.shape[1]

    def body(a_ref, w_ref, x_ref, g_ref, t_ref, dx_ref, dxb_ref, dg_ref, lp_ref, tot_ref):
        v = x_ref[...] + jnp.dot(a_ref[...], w_ref[...], preferred_element_type=F32)
        r = lax.rsqrt(jnp.mean(v * v, axis=-1, keepdims=True) + EPS)
        xh = v * r
        gg = g_ref[...]
        e = xh * gg - t_ref[...]
        do = e * (1.0 / c)
        dxh = do * gg
        dx = r * (dxh - xh * jnp.mean(dxh * xh, axis=-1, keepdims=True))
        dx_ref[...] = dx
        dxb_ref[...] = dx.astype(dxb_ref.dtype)
        i = pl.program_id(0)

        @pl.when(i == 0)
        def _():
            dg_ref[...] = jnp.zeros(dg_ref.shape, F32)
            lp_ref[...] = jnp.zeros(lp_ref.shape, F32)

        dg_ref[...] += _sublane_sum(do * xh)
        lp_ref[...] += _sublane_sum(e * e)
        tot_ref[...] = jnp.full(tot_ref.shape, (0.5 / c) * jnp.sum(lp_ref[...]), F32)

    return pl.pallas_call(
        body, name="ffn_out_loss", grid=(s_ // ts,),
        in_specs=[_rows(ts, kk), _const(kk, c), _rows(ts, c), _const(1, c), _rows(ts, c)],
        out_specs=[_rows(ts, c), _rows(ts, c), _const(SUBLANES, c), _const(SUBLANES, c), _const(SUBLANES, LANES)],
        out_shape=[jax.ShapeDtypeStruct((s_, c), F32), jax.ShapeDtypeStruct((s_, c), MXU_DTYPE),
                   jax.ShapeDtypeStruct((SUBLANES, c), F32), jax.ShapeDtypeStruct((SUBLANES, c), F32),
                   jax.ShapeDtypeStruct((SUBLANES, LANES), F32)],
        compiler_params=_cparams(("arbitrary",)),
    )(act, wd, x1, g, tgt)


def _mla_d_out(dtb, wob, o32, *, ts=512):
    s_ = dtb.shape[0]

    def body(dt_ref, w_ref, o_ref, dob_ref, dl_ref):
        d = lax.dot_general(dt_ref[...], w_ref[...], NT, preferred_element_type=F32)
        dob_ref[...] = d.astype(dob_ref.dtype)
        prod = d * o_ref[...]
        for h in range(MLA_HEADS):
            dl_ref[h] = jnp.sum(prod[:, h * LANES:(h + 1) * LANES].T, axis=0, keepdims=True)

    return pl.pallas_call(
        body, name="mla_d_out", grid=(s_ // ts,), in_specs=[_rows(ts, 1024), _const(1024, 1024), _rows(ts, 1024)],
        out_specs=[_rows(ts, 1024), pl.BlockSpec((MLA_HEADS, 1, ts), lambda i: (0, 0, i))],
        out_shape=[jax.ShapeDtypeStruct((s_, 1024), MXU_DTYPE), jax.ShapeDtypeStruct((MLA_HEADS, 1, s_), F32)],
        compiler_params=_cparams(("parallel",)),
    )(dtb, wob, o32)


SWA_T = 4 * BLOCK


SWA_W = SWA_GROUP * BLOCK


def _swa_masks(sb):
    kr = lax.broadcasted_iota(jnp.int32, (2 * BLOCK, SWA_W), 0)
    qc = jnp.bitwise_and(lax.broadcasted_iota(jnp.int32, (2 * BLOCK, SWA_W), 1), BLOCK - 1)
    band = jnp.logical_and(kr > qc, kr <= qc + BLOCK)
    first = jnp.logical_and(band, kr >= BLOCK)
    return band, jnp.logical_or(first, jnp.logical_and(band, sb > 0))


def _heads_to_rows(ref, rs):
    return jnp.concatenate([ref[rs, h * LANES:(h + 1) * LANES] for h in range(SWA_GROUP)], axis=0)


def _sink_row(sk_ref):
    return jnp.concatenate([sk_ref[0, h:h + 1, :] for h in range(SWA_GROUP)], axis=1) * LOG2E


def _swa_in_specs(rev, nsb):
    sbi = (lambda j: nsb - 1 - j) if rev else (lambda j: j)
    cur = pl.BlockSpec((SWA_T, LANES), lambda g, j: (sbi(j), g))
    prev = pl.BlockSpec((BLOCK, LANES), lambda g, j: (jnp.maximum(4 * sbi(j) - 1, 0), g))
    q = pl.BlockSpec((SWA_T, SWA_GROUP * LANES), lambda g, j: (sbi(j), g))
    sink = pl.BlockSpec((1, SUBLANES, LANES), lambda g, j: (g, 0, 0))
    lse = pl.BlockSpec((SWA_GROUP, 1, SWA_T), lambda g, j: (g, 0, sbi(j)))
    return q, cur, prev, sink, lse


def _swa_fwd(qa, ka, va, sink_b):
    s_ = qa.shape[0]
    nsb = s_ // SWA_T
    c2 = HEAD_DIM ** -0.5 * LOG2E

    def body(q_ref, kc_ref, kp_ref, vc_ref, vp_ref, sk_ref, o32_ref, o16_ref, lse_ref, kx, vx):
        kx[0:BLOCK, :] = kp_ref[...]
        kx[BLOCK:5 * BLOCK, :] = kc_ref[...]
        vx[0:BLOCK, :] = vp_ref[...]
        vx[BLOCK:5 * BLOCK, :] = vc_ref[...]
        band, band0 = _swa_masks(pl.program_id(1))
        sink2 = _sink_row(sk_ref)
        for b in range(4):
            rs = slice(b * BLOCK, (b + 1) * BLOCK)
            ks = slice(b * BLOCK, (b + 2) * BLOCK)
            st = lax.dot_general(kx[ks, :], _heads_to_rows(q_ref, rs), NT, preferred_element_type=F32) * c2
            st = jnp.where(band0 if b == 0 else band, st, -jnp.inf)
            m = jnp.maximum(jnp.max(st, axis=0, keepdims=True), sink2)
            pt = jnp.exp2(st - m)
            den = jnp.sum(pt, axis=0, keepdims=True) + jnp.exp2(sink2 - m)
            o = lax.dot_general((pt * (1.0 / den)).astype(MXU_DTYPE), vx[ks, :], TN, preferred_element_type=F32)
            lse = m + jnp.log2(den)
            for hh in range(SWA_GROUP):
                cs = slice(hh * LANES, (hh + 1) * LANES)
                o32_ref[rs, cs] = o[cs, :]
                o16_ref[rs, cs] = o[cs, :].astype(o16_ref.dtype)
                lse_ref[hh, :, rs] = lse[:, cs]

    q, cur, prev, sink, lse_spec = _swa_in_specs(False, nsb)
    return pl.pallas_call(
        body, name="swa_fwd", grid=(SWA_KV_HEADS, nsb), in_specs=[q, cur, prev, cur, prev, sink],
        out_specs=[q, q, lse_spec],
        out_shape=[jax.ShapeDtypeStruct((s_, SWA_HEADS * LANES), F32), jax.ShapeDtypeStruct((s_, SWA_HEADS * LANES), MXU_DTYPE),
                   jax.ShapeDtypeStruct((SWA_HEADS, 1, s_), F32)],
        scratch_shapes=[pltpu.VMEM((5 * BLOCK, LANES), MXU_DTYPE), pltpu.VMEM((5 * BLOCK, LANES), MXU_DTYPE)],
        compiler_params=_cparams(("parallel", "arbitrary")),
    )(qa, ka, ka, va, va, sink_b)


def _swa_bwd(qa, ka, va, sink_b, o32, do, lse):
    s_ = qa.shape[0]
    nsb = s_ // SWA_T
    scale = HEAD_DIM ** -0.5
    c2 = scale * LOG2E

    def body(q_ref, kc_ref, kp_ref, vc_ref, vp_ref, sk_ref, o_ref, do_ref, lse_ref,
             dq_ref, dk_ref, dv_ref, dsk_ref, kx, vx, kacc, vacc, kcar, vcar):
        j = pl.program_id(1)
        kx[0:BLOCK, :] = kp_ref[...]
        kx[BLOCK:5 * BLOCK, :] = kc_ref[...]
        vx[0:BLOCK, :] = vp_ref[...]
        vx[BLOCK:5 * BLOCK, :] = vc_ref[...]
        band, band0 = _swa_masks(nsb - 1 - j)
        kacc[...] = jnp.zeros(kacc.shape, F32)
        vacc[...] = jnp.zeros(vacc.shape, F32)

        @pl.when(j == 0)
        def _():
            kcar[...] = jnp.zeros(kcar.shape, F32)
            vcar[...] = jnp.zeros(vcar.shape, F32)
            dsk_ref[...] = jnp.zeros(dsk_ref.shape, F32)

        sink2 = _sink_row(sk_ref)
        dsink = jnp.zeros((1, SWA_W), F32)
        for b in range(4):
            rs = slice(b * BLOCK, (b + 1) * BLOCK)
            ks = slice(b * BLOCK, (b + 2) * BLOCK)
            q, k2, v2 = _heads_to_rows(q_ref, rs), kx[ks, :], vx[ks, :]
            d = _heads_to_rows(do_ref, rs)
            delta = jnp.sum((d * _heads_to_rows(o_ref, rs)).T, axis=0, keepdims=True)
            l2 = jnp.concatenate([lse_ref[hh, :, rs] for hh in range(SWA_GROUP)], axis=1)
            st = lax.dot_general(k2, q, NT, preferred_element_type=F32) * c2
            pt = jnp.exp2(jnp.where(band0 if b == 0 else band, st, -jnp.inf) - l2)
            db = d.astype(MXU_DTYPE)
            dst = (pt * (lax.dot_general(v2, db, NT, preferred_element_type=F32) - delta) * scale).astype(MXU_DTYPE)
            dq = lax.dot_general(dst, k2, TN, preferred_element_type=F32)
            for hh in range(SWA_GROUP):
                dq_ref[rs, hh * LANES:(hh + 1) * LANES] = dq[hh * LANES:(hh + 1) * LANES, :]
            kacc[ks, :] += jnp.dot(dst, q, preferred_element_type=F32)
            vacc[ks, :] += jnp.dot(pt.astype(MXU_DTYPE), db, preferred_element_type=F32)
            dsink = dsink - jnp.exp2(sink2 - l2) * delta
        for hh in range(SWA_GROUP):
            tot = jnp.sum(dsink[:, hh * LANES:(hh + 1) * LANES], axis=1, keepdims=True)
            dsk_ref[0, hh:hh + 1, :] += jnp.broadcast_to(tot, (1, LANES))

        dk_ref[0:3 * BLOCK, :] = kacc[BLOCK:4 * BLOCK, :]
        dk_ref[3 * BLOCK:4 * BLOCK, :] = kacc[4 * BLOCK:5 * BLOCK, :] + kcar[...]
        dv_ref[0:3 * BLOCK, :] = vacc[BLOCK:4 * BLOCK, :].astype(dv_ref.dtype)
        dv_ref[3 * BLOCK:4 * BLOCK, :] = (vacc[4 * BLOCK:5 * BLOCK, :] + vcar[...]).astype(dv_ref.dtype)
        kcar[...] = kacc[0:BLOCK, :]
        vcar[...] = vacc[0:BLOCK, :]

    q, cur, prev, sink, lse_spec = _swa_in_specs(True, nsb)
    return pl.pallas_call(
        body, name="swa_bwd", grid=(SWA_KV_HEADS, nsb),
        in_specs=[q, cur, prev, cur, prev, sink, q, q, lse_spec],
        out_specs=[q, cur, cur, sink],
        out_shape=[jax.ShapeDtypeStruct((s_, SWA_HEADS * LANES), F32), jax.ShapeDtypeStruct((s_, SWA_KV_HEADS * LANES), F32),
                   jax.ShapeDtypeStruct((s_, SWA_KV_HEADS * LANES), MXU_DTYPE),
                   jax.ShapeDtypeStruct((SWA_KV_HEADS, SUBLANES, LANES), F32)],
        scratch_shapes=[pltpu.VMEM((5 * BLOCK, LANES), MXU_DTYPE), pltpu.VMEM((5 * BLOCK, LANES), MXU_DTYPE),
                        pltpu.VMEM((5 * BLOCK, LANES), F32), pltpu.VMEM((5 * BLOCK, LANES), F32),
                        pltpu.VMEM((BLOCK, LANES), F32), pltpu.VMEM((BLOCK, LANES), F32)],
        compiler_params=_cparams(("arbitrary", "arbitrary")),
    )(qa, ka, ka, va, va, sink_b, o32, do, lse)


MLA_T = 512
MLA_FWD_GROUP = 4
MLA_BWD_GROUP = 2


def _mla_specs(s_, t, group):
    w = group * LANES
    qs = pl.BlockSpec((t, w), lambda g, i: (i, g))
    kv = pl.BlockSpec((s_, w), lambda g, i: (0, g))
    row = pl.BlockSpec((group, 1, t), lambda g, i: (g, 0, i))
    return qs, kv, row


def _causal_scores_t(k, q, t, c2, masked):
    st = lax.dot_general(k, q, NT, preferred_element_type=F32) * c2
    if masked:
        kr = lax.broadcasted_iota(jnp.int32, (t, t), 0)
        qc = lax.broadcasted_iota(jnp.int32, (t, t), 1)
        st = jnp.where(kr <= qc, st, -jnp.inf)
    return st


def _mla_fwd(qc, kc, vp):
    s_ = qc.shape[0]
    t = min(MLA_T, s_)
    c2 = MLA_QK ** -0.5 * LOG2E
    grp = MLA_FWD_GROUP

    def body(q_ref, k_ref, v_ref, o32_ref, o16_ref, lse_ref, m_s, acc_s):
        qi = pl.program_id(1)
        m_s[...] = jnp.full(m_s.shape, -jnp.inf, F32)
        acc_s[...] = jnp.zeros(acc_s.shape, F32)
        ones_lane = lax.broadcasted_iota(jnp.int32, (t, LANES), 1) == MLA_V

        def step(ki, masked):
            off = pl.multiple_of(ki * t, t)
            for g in range(grp):
                cs = slice(g * LANES, (g + 1) * LANES)
                st = _causal_scores_t(k_ref[pl.ds(off, t), cs], q_ref[:, cs], t, c2, masked)
                m_old = m_s[g]
                m_new = jnp.maximum(m_old, jnp.max(st, axis=0, keepdims=True))
                alpha = jnp.exp2(m_old - m_new)
                pt = jnp.exp2(st - m_new).astype(MXU_DTYPE)
                v = v_ref[pl.ds(off, t), cs]
                v = jnp.where(ones_lane, jnp.ones((), v.dtype), v)
                acc_s[g] = alpha * acc_s[g] + lax.dot_general(v, pt, TN, preferred_element_type=F32)
                m_s[g] = m_new

        def full_block(ki, carry):
            step(ki, False)
            return carry

        lax.fori_loop(0, qi, full_block, 0)
        step(qi, True)
        for g in range(grp):
            cs = slice(g * LANES, (g + 1) * LANES)
            acc = acc_s[g]
            l = acc[MLA_V:MLA_V + 1, :]
            o = (acc * (1.0 / l)).T
            o32_ref[:, cs] = o
            o16_ref[:, cs] = o.astype(o16_ref.dtype)
            lse_ref[g] = m_s[g] + jnp.log2(l)

    qs, kv, row = _mla_specs(s_, t, grp)
    return pl.pallas_call(
        body, name="mla_fwd", grid=(MLA_HEADS // grp, s_ // t), in_specs=[qs, kv, kv], out_specs=[qs, qs, row],
        out_shape=[jax.ShapeDtypeStruct((s_, MLA_HEADS * LANES), F32), jax.ShapeDtypeStruct((s_, MLA_HEADS * LANES), MXU_DTYPE),
                   jax.ShapeDtypeStruct((MLA_HEADS, 1, s_), F32)],
        scratch_shapes=[pltpu.VMEM((grp, 1, t), F32), pltpu.VMEM((grp, LANES, t), F32)],
        compiler_params=_cparams(("parallel", "arbitrary")),
    )(qc, kc, vp)


def _mla_bwd(qc, kc, vp, dob, lse, delta):
    s_ = qc.shape[0]
    t = min(MLA_T, s_)
    scale = MLA_QK ** -0.5
    c2 = scale * LOG2E
    grp = MLA_BWD_GROUP

    def body(q_ref, do_ref, lse_ref, dl_ref, k_ref, v_ref, dq_ref, dk_ref, dv_ref, dqt_s):
        qi = pl.program_id(1)

        @pl.when(qi == 0)
        def _():
            dk_ref[...] = jnp.zeros(dk_ref.shape, F32)
            dv_ref[...] = jnp.zeros(dv_ref.shape, F32)

        dqt_s[...] = jnp.zeros(dqt_s.shape, F32)

        def step(ki, masked):
            off = pl.multiple_of(ki * t, t)
            for g in range(grp):
                cs = slice(g * LANES, (g + 1) * LANES)
                q, d, k = q_ref[:, cs], do_ref[:, cs], k_ref[pl.ds(off, t), cs]
                pt = jnp.exp2(_causal_scores_t(k, q, t, c2, masked) - lse_ref[g])
                dpt = lax.dot_general(v_ref[pl.ds(off, t), cs], d, NT, preferred_element_type=F32)
                dst = (pt * (dpt - dl_ref[g]) * scale).astype(MXU_DTYPE)
                dv_ref[pl.ds(off, t), cs] += jnp.dot(pt.astype(MXU_DTYPE), d, preferred_element_type=F32)
                dk_ref[pl.ds(off, t), cs] += jnp.dot(dst, q, preferred_element_type=F32)
                dqt_s[g] += lax.dot_general(k, dst, TN, preferred_element_type=F32)

        def full_block(ki, carry):
            step(ki, False)
            return carry

        lax.fori_loop(0, qi, full_block, 0)
        step(qi, True)
        for g in range(grp):
            dq_ref[:, g * LANES:(g + 1) * LANES] = dqt_s[g].T

    qs, kv, row = _mla_specs(s_, t, grp)
    shp = jax.ShapeDtypeStruct((s_, MLA_HEADS * LANES), F32)
    return pl.pallas_call(
        body, name="mla_bwd", grid=(MLA_HEADS // grp, s_ // t), in_specs=[qs, qs, row, row, kv, kv],
        out_specs=[qs, kv, kv], out_shape=[shp, shp, shp], scratch_shapes=[pltpu.VMEM((grp, LANES, t), F32)],
        compiler_params=_cparams(("parallel", "arbitrary")),
    )(qc, dob, lse, delta, kc, vp)


def _pad_heads(w, nh, hd, axis):
    shp = w.shape
    w = w.reshape(shp[:axis] + (nh, hd) + shp[axis + 1:])
    pad = [(0, 0)] * w.ndim
    pad[axis + 1] = (0, LANES - hd)
    w = jnp.pad(w, pad)
    return w.reshape(shp[:axis] + (nh * LANES,) + shp[axis + 1:])


def _unpad_heads(w, nh, hd, axis):
    shp = w.shape
    w = w.reshape(shp[:axis] + (nh, LANES) + shp[axis + 1:])
    w = lax.slice_in_dim(w, 0, hd, axis=axis + 1)
    return w.reshape(shp[:axis] + (nh * hd,) + shp[axis + 1:])


PACK_W = 1024
ROW_TILE = 16
FULL_SHAPE = dict(w_in=(1024, 3488), w_uq=(384, 768), w_ukv=(256, 1024), w_o_swa=(512, 1024), w_o_mla=(512, 1024),
                  w_out=(1024, 1024), w_gate=(1024, 2816), w_up=(1024, 2816), w_down=(2816, 1024))
BIG = tuple(FULL_SHAPE)
ROW_SHARDED = ("w_out", "w_down")
W_IN_COLS = FULL_SHAPE["w_in"][1] // N_DEV
W_IN_ROWS = -(-W_IN_COLS // ROW_TILE) * ROW_TILE
FF_COLS = D_FF // N_DEV
OUT_ROWS = D_MODEL // N_DEV
SMALL_ROW0 = W_IN_ROWS + OUT_ROWS
SMALL_FLAT = (("w_uq", 0, 36), ("w_ukv", 48, 32), ("w_o_swa", 80, 64), ("w_o_mla", 144, 64))
SMALL_ROWS = 208
EARLY_ROWS = SMALL_ROW0 + SMALL_ROWS
LATE_ROWS = 3 * FF_COLS
PACK_ROWS = EARLY_ROWS + LATE_ROWS


def _shard_shape(n):
    r, c = FULL_SHAPE[n]
    return (r // N_DEV, c) if n in ROW_SHARDED else (r, c // N_DEV)


def _wire_pack(sh, dtype):
    c = lambda n: sh[n].astype(dtype)
    rows = [jnp.pad(c("w_in").T, ((0, W_IN_ROWS - W_IN_COLS), (0, 0))), c("w_out")]
    for n, _, r in SMALL_FLAT:
        rows.append(jnp.pad(c(n).reshape(r, PACK_W), ((0, -r % ROW_TILE), (0, 0))))
    return jnp.concatenate(rows + [c("w_gate").T, c("w_up").T, c("w_down")], 0)


MID_ROWS = OUT_ROWS + SMALL_ROWS


def _mid_unpack(p):
    out = dict(w_out=p[0:OUT_ROWS])
    for n, off, r in SMALL_FLAT:
        out[n] = p[OUT_ROWS + off:OUT_ROWS + off + r].reshape(_shard_shape(n))
    return out


def _w_in_row_maps():
    sp = lambda col: (col // W_IN_COLS) * W_IN_ROWS + col % W_IN_COLS
    fwd = np.full((P_W,), -1, np.int64)

    def put(t0, c0, n):
        fwd[t0:t0 + n] = [sp(c) for c in range(c0, c0 + n)]

    put(P_GA, IN_OFF[6], D_MODEL)
    put(P_GB, IN_OFF[7], D_MODEL)
    for h in range(SWA_HEADS):
        put(P_Q + LANES * h, IN_OFF[0] + HEAD_DIM * h, HEAD_DIM)
    put(P_QLAT, IN_OFF[3], Q_LORA)
    put(P_KR + KR_LANE, IN_OFF[5], MLA_ROPE)
    for h in range(SWA_KV_HEADS):
        put(P_K + LANES * h, IN_OFF[1] + HEAD_DIM * h, HEAD_DIM)
        put(P_V + LANES * h, IN_OFF[2] + HEAD_DIM * h, HEAD_DIM)
    put(P_KVLAT, IN_OFF[4], KV_LORA)
    inv = np.full((N_DEV * W_IN_ROWS,), -1, np.int64)
    inv[fwd[fwd >= 0]] = np.nonzero(fwd >= 0)[0]
    return fwd, inv


def _take_rows(src, idx, *, name, tile=2 * LANES):
    n_out, n_src, width = len(idx), src.shape[0], src.shape[1]
    assert n_out % tile == 0 and n_src % tile == 0
    n_tiles = n_out // tile
    blocks = [sorted({int(v) // tile for v in idx[i * tile:(i + 1) * tile] if v >= 0}) for i in range(n_tiles)]
    k_max = max(1, max(len(b) for b in blocks))
    tab = np.zeros((n_tiles, k_max), np.int32)
    sel = np.zeros((n_tiles, k_max, tile, tile), np.float32)
    for i, blks in enumerate(blocks):
        for m, b in enumerate(blks):
            tab[i, m] = b
            for r in range(tile):
                v = int(idx[i * tile + r])
                if v >= 0 and v // tile == b:
                    sel[i, m, r, v % tile] = 1.0

    def body(tab_ref, sel_ref, *refs):
        o_ref = refs[k_max]
        acc = jnp.dot(sel_ref[0, 0], refs[0][...], preferred_element_type=F32)
        for m in range(1, k_max):
            acc = acc + jnp.dot(sel_ref[0, m], refs[m][...], preferred_element_type=F32)
        o_ref[...] = acc.astype(o_ref.dtype)

    def src_spec(m):
        return pl.BlockSpec((tile, width), lambda i, t: (t[i * k_max + m], 0))

    return pl.pallas_call(
        body, name=name,
        grid_spec=pltpu.PrefetchScalarGridSpec(
            num_scalar_prefetch=1, grid=(n_tiles,),
            in_specs=[pl.BlockSpec((1, k_max, tile, tile), lambda i, t: (i, 0, 0, 0))] + [src_spec(m) for m in range(k_max)],
            out_specs=pl.BlockSpec((tile, width), lambda i, t: (i, 0))),
        out_shape=jax.ShapeDtypeStruct((n_out, width), src.dtype),
        compiler_params=_cparams(("parallel",)),
    )(jnp.asarray(tab.reshape(-1)), jnp.asarray(sel, src.dtype), *([src] * k_max))


def _w_in_operand(win_g):
    return _take_rows(win_g.reshape(N_DEV * W_IN_ROWS, PACK_W), _w_in_row_maps()[0], name="w_in_rows")


def _mid_operands(wout_g, small_g):
    def full(n, off, r):
        a = small_g[:, off:off + r].reshape((N_DEV,) + _shard_shape(n))
        return jnp.moveaxis(a, 0, 1).reshape(FULL_SHAPE[n])

    w = {n: full(n, off, r) for n, off, r in SMALL_FLAT}
    ukv = w["w_ukv"].reshape(KV_LORA, MLA_HEADS, MLA_NOPE + MLA_V)
    return dict(
        wout=wout_g.reshape(D_MODEL, D_MODEL),
        wuq=_pad_heads(w["w_uq"], MLA_HEADS, MLA_QK, 1),
        wuk=_pad_heads(ukv[:, :, :MLA_NOPE].reshape(KV_LORA, -1), MLA_HEADS, MLA_NOPE, 1),
        wuv=_pad_heads(ukv[:, :, MLA_NOPE:].reshape(KV_LORA, -1), MLA_HEADS, MLA_V, 1),
        woa=_pad_heads(w["w_o_swa"], SWA_HEADS, HEAD_DIM, 0),
        wob=_pad_heads(w["w_o_mla"], MLA_HEADS, MLA_V, 0),
    )


def _mid_grad_pack(g):
    uk = _unpad_heads(g["wukv"][:, :1024], MLA_HEADS, MLA_NOPE, 1).reshape(KV_LORA, MLA_HEADS, MLA_NOPE)
    uv = _unpad_heads(g["wukv"][:, 1024:], MLA_HEADS, MLA_V, 1).reshape(KV_LORA, MLA_HEADS, MLA_V)
    w = dict(w_uq=_unpad_heads(g["wuq"], MLA_HEADS, MLA_QK, 1), w_ukv=jnp.concatenate([uk, uv], 2).reshape(KV_LORA, -1),
             w_o_swa=_unpad_heads(g["woa"], SWA_HEADS, HEAD_DIM, 0), w_o_mla=_unpad_heads(g["wob"], MLA_HEADS, MLA_V, 0))

    def flat(n, r):
        rr, cc = FULL_SHAPE[n]
        a = jnp.moveaxis(w[n].reshape(rr, N_DEV, cc // N_DEV), 1, 0).reshape(N_DEV, r, PACK_W)
        return jnp.pad(a, ((0, 0), (0, -r % ROW_TILE), (0, 0))).astype(WIRE_DTYPE)

    return jnp.concatenate([g["wout"].reshape(N_DEV, OUT_ROWS, PACK_W)] + [flat(n, r) for n, _, r in SMALL_FLAT], 1)


def _w_in_grad_chunks(g_win_t):
    return _take_rows(g_win_t, _w_in_row_maps()[1], name="dw_in_rows").reshape(N_DEV, W_IN_ROWS, PACK_W)


def _local_step(x, tgt, win_t, small, weights, grads):
    s_ = x.shape[0]
    tabs = _rope_tables(s_)
    sink_b = jnp.broadcast_to(small["swa_sinks"].reshape(SWA_KV_HEADS, SWA_GROUP, 1), (SWA_KV_HEADS, SWA_GROUP, LANES))
    sink_b = jnp.pad(sink_b, ((0, 0), (0, SUBLANES - SWA_GROUP), (0, 0)))

    h, qa, ka, va, cq, ckv, kro, p = _proj_in(x, small["mix_norm_g"], win_t, small["q_norm_g"], small["kv_norm_g"], tabs)
    ops = weights.mid(cq)
    oa32, oa16, lse_a = _swa_fwd(qa, ka, va, sink_b)
    qc, kc, vp = _mla_up(cq, ckv, kro, ops["wuq"], ops["wuk"], ops["wuv"], tabs)
    ob32, ob16, lse_b = _mla_fwd(qc, kc, vp)
    ta, tb, y = _attn_out_gate(oa16, ob16, ops["woa"], ops["wob"], p)
    x1 = _mm(y, ops["wout"], "nn", name="out_proj", add=x, tm=1024, tn=1024)
    wgu_t, wd = weights.late(x1)
    h2, gu, act = _ffn_in_act(x1, small["ffn_norm_g"], wgu_t)

    dx2, dx2b, dg3, _, tot = _ffn_out_loss(act, wd, x1, small["final_norm_g"].reshape(1, D_MODEL), tgt)
    g = {}
    g_wd = _mm(act, dx2b, "tn", name="dw_down", tm=FF_TILE, tn=1024, tk=2048, out_dtype=WIRE_DTYPE)
    dgu = _d_act_swiglu(dx2b, wd, gu)
    g_wgu = _mm(dgu, h2, "tn", name="dw_ffn_in", tm=FF_TILE, tn=1024, tk=2048, out_dtype=WIRE_DTYPE)
    token = grads.late(g_wgu, g_wd)
    dx1, dx1b, dg2 = _mm_norm_bwd(dgu, wgu_t, x1, small["ffn_norm_g"] + token[0:1, 0:1], dx2, name="d_h2")
    g["wout"] = _mm(y, dx1b, "tn", name="dw_out", tm=1024, tn=1024, tk=1024, out_dtype=WIRE_DTYPE)
    dta, dtb, dgab = _d_y_gate(dx1b, ops["wout"], p, ta, tb)
    doa = _mm(dta, ops["woa"], "nt", name="d_oa", tm=1024, tn=1024)
    g["woa"] = _mm(oa16, dta, "tn", name="dw_o_swa", tm=1024, tn=1024, tk=1024)
    g["wob"] = _mm(ob16, dtb, "tn", name="dw_o_mla", tm=1024, tn=1024, tk=1024)
    dob16, delta_b = _mla_d_out(dtb, ops["wob"], ob32)
    dqc, dkc, dvp = _mla_bwd(qc, kc, vp, dob16, lse_b, delta_b)
    dqp, dkv, dkr, dqlat, dkvlat, dgq, dgkv = _mla_up_bwd(
        dqc, dkc, dvp, ops["wuq"], jnp.concatenate([ops["wuk"], ops["wuv"]], 1), p, small["q_norm_g"], small["kv_norm_g"], tabs)
    g["wuq"] = _mm(cq, dqp, "tn", name="dw_uq", tm=Q_LORA, tn=1024, tk=512)
    g["wukv"] = _mm(ckv, dkv, "tn", name="dw_ukv", tm=KV_LORA, tn=1024, tk=512)
    token = grads.mid(g)
    dqa, dka, dva, dsk = _swa_bwd(qa, ka, va, sink_b + token[0:1, 0:1], oa32, doa, lse_a)
    dp = _assemble_dp(dgab, dqa, dqlat, dkr, dka, dva, dkvlat, tabs)
    token = grads.last(_mm(dp, h, "tn", name="dw_in", tm=2176, tn=1024, tk=1024, out_dtype=WIRE_DTYPE))
    gx, _, dg1 = _mm_norm_bwd(dp, win_t, x, small["mix_norm_g"], dx1, name="d_h", after=token)

    sm = dict(mix_norm_g=dg1, ffn_norm_g=dg2, final_norm_g=dg3, q_norm_g=dgq, kv_norm_g=dgkv,
              swa_sinks=dsk[:, :SWA_GROUP, 0].reshape(1, SWA_HEADS))
    return tot, gx, sm


MESH = pl.DeviceIdType.MESH
ANY = pl.BlockSpec(memory_space=pl.ANY)


def _position():
    return lax.axis_index("x"), lax.axis_index("y"), lax.axis_index("c")


def _all_gather(block, pieces, shapes, *, name):
    n_out = len(shapes)
    n_rows = sum(p[3] for p in pieces)

    def body(x_ref, *refs):
        outs, (send_sems, recv_sems, local_sem) = refs[:n_out], refs[n_out:]
        x, y, c = _position()
        me, sibling = (x, y, c), (x, y, 1 - c)
        chips = [(1 - x, y), (x, 1 - y), (1 - x, 1 - y)]

        def dst(piece, blk):
            arr, lead, _, _ = piece
            return outs[arr].at[lead(4 * blk[0] + 2 * blk[1] + blk[2])]

        def own(piece):
            return x_ref.at[pl.ds(piece[2], piece[3])]

        def copies(k, blk, to, from_input):
            return [pltpu.make_async_remote_copy(
                src_ref=own(p) if from_input else dst(p, blk), dst_ref=dst(p, blk), send_sem=send_sems.at[k],
                recv_sem=recv_sems.at[k], device_id=to, device_id_type=MESH) for p in pieces]

        gathered_rows = x_ref.at[pl.ds(0, n_rows)]

        def whole_block(k):
            return pltpu.make_async_remote_copy(src_ref=gathered_rows, dst_ref=gathered_rows, send_sem=send_sems.at[k],
                                                recv_sem=recv_sems.at[k], device_id=me, device_id_type=MESH)

        for p in pieces:
            pltpu.make_async_copy(own(p), dst(p, me), local_sem).start()
        for cp in copies(0, me, sibling, True):
            cp.start()
        for j, chip in enumerate(chips):
            for cp in copies(1 + j, me, (*chip, c), True):
                cp.start()
        for j, chip in enumerate(chips):
            whole_block(1 + j).wait_recv()
            for cp in copies(4 + j, (*chip, c), sibling, False):
                cp.start()
        whole_block(0).wait_recv()
        for j in range(3):
            whole_block(4 + j).wait_recv()
        for k in range(7):
            whole_block(k).wait_send()
        pltpu.make_async_copy(gathered_rows, gathered_rows, local_sem).wait()

    return pl.pallas_call(
        body, name=name, out_shape=[jax.ShapeDtypeStruct(s, block.dtype) for s in shapes], in_specs=[ANY],
        out_specs=[ANY] * n_out,
        scratch_shapes=[pltpu.SemaphoreType.DMA((7,)), pltpu.SemaphoreType.DMA((7,)), pltpu.SemaphoreType.DMA],
    )(block)


HBM = pl.BlockSpec(memory_space=pltpu.HBM)
SEM = pl.BlockSpec(memory_space=pltpu.SEMAPHORE)
TILE_DEVS = FF_TILE // FF_COLS
GU_SHAPE = (2, 2, TILE_DEVS, FF_COLS, PACK_W)


def _gate_slab(d):
    return (d // TILE_DEVS, 0, d % TILE_DEVS)


def _up_slab(d):
    return (d // TILE_DEVS, 1, d % TILE_DEVS)
D_SHAPE = (N_DEV, FF_COLS, PACK_W)
LAND_SHAPE = (N_DEV, LATE_ROWS, PACK_W)


def _split_params():
    return pltpu.CompilerParams(has_side_effects=pltpu.SideEffectType.DATAFLOW_SIDE_EFFECTING)


def _peer(x, y, c, k):
    return ((1 - x) if k & 4 else x, (1 - y) if k & 2 else y, (1 - c) if k & 1 else c)


def _empty_hbm(shape, dtype):
    return pltpu.with_memory_space_constraint(lax.empty(shape, dtype), pltpu.HBM)


def _wait_all(rows, send_sems, recv_sems, me):
    for k in range(N_DEV - 1):
        cp = pltpu.make_async_remote_copy(src_ref=rows, dst_ref=rows, send_sem=send_sems.at[k], recv_sem=recv_sems.at[k],
                                          device_id=me, device_id_type=MESH)
        cp.wait_send()
        cp.wait_recv()


def _token_shape():
    return jax.ShapeDtypeStruct((SUBLANES, LANES), F32)


def _gather_start(pack, row0, pieces, shapes, *, name):
    n = len(shapes)

    def body(*refs):
        p_ref, bufs, send_sems, recv_sems, token = refs[0], refs[1:1 + n], refs[1 + n], refs[2 + n], refs[-1]
        x, y, c = _position()
        me = 4 * x + 2 * y + c
        for k in range(1, N_DEV):
            off = row0
            for buf, lead, rows in pieces:
                pltpu.make_async_remote_copy(
                    src_ref=p_ref.at[pl.ds(off, rows)], dst_ref=bufs[buf].at[lead(me)], send_sem=send_sems.at[k - 1],
                    recv_sem=recv_sems.at[k - 1], device_id=_peer(x, y, c, k), device_id_type=MESH).start()
                off += rows
        token[...] = jnp.zeros_like(token)

    sems, dt = pltpu.SemaphoreType.DMA((N_DEV - 1,)), pack.dtype
    return pl.pallas_call(
        body, name=name,
        out_shape=(sems, sems, pltpu.HBM(pack.shape, dt)) + tuple(pltpu.HBM(s, dt) for s in shapes) + (_token_shape(),),
        in_specs=(HBM,) * (1 + n), out_specs=(SEM, SEM) + (HBM,) * (1 + n) + (pl.BlockSpec(memory_space=pltpu.VMEM),),
        input_output_aliases={i: 2 + i for i in range(1 + n)}, compiler_params=_split_params(),
    )(pltpu.with_memory_space_constraint(pack, pltpu.HBM), *[_empty_hbm(s, dt) for s in shapes])


def _gather_wait(started, row0, n_rows, after, *, name):
    send_sems, recv_sems, pack, *bufs = started[:-1]
    n = len(bufs)

    def body(*refs):
        _wait_all(refs[0].at[pl.ds(row0, n_rows)], refs[1 + n], refs[2 + n], _position())

    outs = pl.pallas_call(
        body, name=name, out_shape=tuple(pltpu.HBM(a.shape, a.dtype) for a in (pack, *bufs)),
        in_specs=(HBM,) * (1 + n) + (SEM, SEM, ANY), out_specs=(HBM,) * (1 + n),
        input_output_aliases={i: i for i in range(1 + n)}, compiler_params=_split_params(),
    )(pack, *bufs, send_sems, recv_sems, after)
    return outs[0], outs[1:]


def _scatter_start(srcs, pieces, *, name):
    n = len(srcs)
    land_shape = (N_DEV, sum(p[2] for p in pieces), PACK_W)

    def body(*refs):
        src_refs, land_ref, send_sems, recv_sems, token = refs[:n], refs[n], refs[n + 1], refs[n + 2], refs[-1]
        x, y, c = _position()
        me = 4 * x + 2 * y + c
        for k in range(1, N_DEV):
            px, py, pc = _peer(x, y, c, k)
            off = 0
            for si, lead, rows in pieces:
                pltpu.make_async_remote_copy(
                    src_ref=src_refs[si].at[lead(4 * px + 2 * py + pc)], dst_ref=land_ref.at[me, pl.ds(off, rows)],
                    send_sem=send_sems.at[k - 1], recv_sem=recv_sems.at[k - 1], device_id=(px, py, pc),
                    device_id_type=MESH).start()
                off += rows
        token[...] = jnp.zeros_like(token)

    sems, dt = pltpu.SemaphoreType.DMA((N_DEV - 1,)), srcs[0].dtype
    return pl.pallas_call(
        body, name=name,
        out_shape=(sems, sems) + tuple(pltpu.HBM(a.shape, dt) for a in srcs) + (pltpu.HBM(land_shape, dt), _token_shape()),
        in_specs=(HBM,) * (n + 1), out_specs=(SEM, SEM) + (HBM,) * (n + 1) + (pl.BlockSpec(memory_space=pltpu.VMEM),),
        input_output_aliases={i: 2 + i for i in range(n + 1)}, compiler_params=_split_params(),
    )(*[pltpu.with_memory_space_constraint(a, pltpu.HBM) for a in srcs], _empty_hbm(land_shape, dt))


def _scatter_wait(started, after, *, name):
    send_sems, recv_sems, *bufs = started[:-1]
    n = len(bufs)

    def body(*refs):
        _wait_all(refs[n - 1].at[0], refs[n], refs[n + 1], _position())

    return pl.pallas_call(
        body, name=name, out_shape=tuple(pltpu.HBM(a.shape, a.dtype) for a in bufs),
        in_specs=(HBM,) * n + (SEM, SEM, ANY), out_specs=(HBM,) * n, input_output_aliases={i: i for i in range(n)},
        compiler_params=_split_params(),
    )(*bufs, send_sems, recv_sems, after)


def _peer_sum(own, own_lead, land, block, rows, idx, *, name):
    lead_rank = own.ndim - 2

    def body(idx_ref, own_ref, *refs):
        o_ref = refs[N_DEV - 1]
        acc = own_ref[(0,) * lead_rank].astype(F32)
        for k in range(N_DEV - 1):
            acc = acc + refs[k][0].astype(F32)
        o_ref[...] = acc

    own_spec = pl.BlockSpec((1,) * lead_rank + (rows, PACK_W), lambda i, t: own_lead(t[0]) + (0, 0))

    def land_spec(k):
        return pl.BlockSpec((1, rows, PACK_W), lambda i, t: (t[k + 1], block, 0))

    return pl.pallas_call(
        body, name=name,
        grid_spec=pltpu.PrefetchScalarGridSpec(
            num_scalar_prefetch=1, grid=(1,), in_specs=[own_spec] + [land_spec(k) for k in range(N_DEV - 1)],
            out_specs=pl.BlockSpec((rows, PACK_W), lambda i, t: (0, 0))),
        out_shape=jax.ShapeDtypeStruct((rows, PACK_W), F32), compiler_params=_cparams(("arbitrary",)),
    )(idx, own, *([land] * (N_DEV - 1)))


def _adamw(w, g, m, v):
    m = ADAM_B1 * m + (1.0 - ADAM_B1) * g
    v = ADAM_B2 * v + (1.0 - ADAM_B2) * (g * g)
    m_hat = m / (1.0 - ADAM_B1 ** ADAM_STEP)
    v_hat = v / (1.0 - ADAM_B2 ** ADAM_STEP)
    delta = -ADAM_LR * (m_hat / (jnp.sqrt(v_hat) + ADAM_EPS) + ADAM_WD * w)
    return delta, m, v


def _adamw_call(w, g, m, v, *, name, max_rows=256):
    _, r, c_ = w.shape
    tr = max_rows if r > max_rows and r % max_rows == 0 else r

    def body(w_ref, g_ref, m_ref, v_ref, d_ref, mo_ref, vo_ref):
        d, mn, vn = _adamw(w_ref[0], g_ref[...], m_ref[0], v_ref[0])
        d_ref[0] = d
        mo_ref[0] = mn
        vo_ref[0] = vn

    row3 = pl.BlockSpec((1, tr, c_), lambda i: (0, i, 0))
    shp = jax.ShapeDtypeStruct((1, r, c_), F32)
    return pl.pallas_call(
        body, name=name, grid=(r // tr,), in_specs=[row3, pl.BlockSpec((tr, c_), lambda i: (i, 0)), row3, row3],
        out_specs=[row3] * 3, out_shape=[shp] * 3, compiler_params=_cparams(("parallel",)),
    )(w, g, m, v)


SMALL = ("mix_norm_g", "ffn_norm_g", "final_norm_g", "q_norm_g", "kv_norm_g", "swa_sinks")
SMALL_W = dict(mix_norm_g=1024, ffn_norm_g=1024, final_norm_g=1024, q_norm_g=Q_LORA, kv_norm_g=KV_LORA, swa_sinks=SWA_HEADS)


def _small_adamw(parts, w, m, v):
    n_par = parts.shape[1] // SUBLANES

    def body(p_ref, w_ref, m_ref, v_ref, g_ref, d_ref, mo_ref, vo_ref):
        tot = p_ref[0]
        for dev in range(1, N_DEV):
            tot = tot + p_ref[dev]
        row_id = lax.broadcasted_iota(jnp.int32, (SUBLANES, PACK_W), 0)
        g = jnp.zeros((SUBLANES, PACK_W), F32)
        for k in range(n_par):
            g = jnp.where(row_id == k, jnp.sum(tot[k * SUBLANES:(k + 1) * SUBLANES, :], axis=0, keepdims=True), g)
        d, mn, vn = _adamw(w_ref[...], g, m_ref[...], v_ref[...])
        g_ref[...] = g
        d_ref[...] = d
        mo_ref[...] = mn
        vo_ref[...] = vn

    shp = jax.ShapeDtypeStruct((SUBLANES, PACK_W), F32)
    vm = pl.BlockSpec(memory_space=pltpu.VMEM)
    return pl.pallas_call(body, name="small_adamw", in_specs=[vm] * 4, out_specs=[vm] * 4, out_shape=[shp] * 4)(parts, w, m, v)


def _small_pack(d, rows_each):
    parts = [jnp.pad(d[n].astype(F32), ((0, 0), (0, PACK_W - SMALL_W[n]))) for n in SMALL]
    out = jnp.concatenate(parts, 0)
    pad = -out.shape[0] % SUBLANES
    return jnp.pad(out, ((0, pad), (0, 0)))


def kernel(x, mix_norm_g, w_in, swa_sinks, q_norm_g, w_uq, kv_norm_g, w_ukv, w_o_swa, w_o_mla, w_out, ffn_norm_g, w_gate, w_up, w_down, final_norm_g, loss_target, m_mix_norm_g, m_w_in, m_swa_sinks, m_q_norm_g, m_w_uq, m_kv_norm_g, m_w_ukv, m_w_o_swa, m_w_o_mla, m_w_out, m_ffn_norm_g, m_w_gate, m_w_up, m_w_down, m_final_norm_g, v_mix_norm_g, v_w_in, v_swa_sinks, v_q_norm_g, v_w_uq, v_kv_norm_g, v_w_ukv, v_w_o_swa, v_w_o_mla, v_w_out, v_ffn_norm_g, v_w_gate, v_w_up, v_w_down, v_final_norm_g):
    big_w = dict(w_in=w_in[0], w_uq=w_uq[0], w_ukv=w_ukv[0], w_o_swa=w_o_swa[0], w_o_mla=w_o_mla[0], w_out=w_out[0],
                 w_gate=w_gate[0], w_up=w_up[0], w_down=w_down[0])
    big_w3 = dict(w_in=w_in, w_uq=w_uq, w_ukv=w_ukv, w_o_swa=w_o_swa, w_o_mla=w_o_mla, w_out=w_out, w_gate=w_gate, w_up=w_up,
                  w_down=w_down)
    big_m = dict(w_in=m_w_in, w_uq=m_w_uq, w_ukv=m_w_ukv, w_o_swa=m_w_o_swa, w_o_mla=m_w_o_mla, w_out=m_w_out,
                 w_gate=m_w_gate, w_up=m_w_up, w_down=m_w_down)
    big_v = dict(w_in=v_w_in, w_uq=v_w_uq, w_ukv=v_w_ukv, w_o_swa=v_w_o_swa, w_o_mla=v_w_o_mla, w_out=v_w_out,
                 w_gate=v_w_gate, w_up=v_w_up, w_down=v_w_down)
    small_w = dict(mix_norm_g=mix_norm_g, ffn_norm_g=ffn_norm_g, final_norm_g=final_norm_g.reshape(1, D_MODEL),
                   q_norm_g=q_norm_g, kv_norm_g=kv_norm_g, swa_sinks=swa_sinks)
    small_m = dict(mix_norm_g=m_mix_norm_g, ffn_norm_g=m_ffn_norm_g, final_norm_g=m_final_norm_g.reshape(1, D_MODEL),
                   q_norm_g=m_q_norm_g, kv_norm_g=m_kv_norm_g, swa_sinks=m_swa_sinks)
    small_v = dict(mix_norm_g=v_mix_norm_g, ffn_norm_g=v_ffn_norm_g, final_norm_g=v_final_norm_g.reshape(1, D_MODEL),
                   q_norm_g=v_q_norm_g, kv_norm_g=v_kv_norm_g, swa_sinks=v_swa_sinks)

    px, py, pc = _position()
    me = 4 * px + 2 * py + pc
    idx = jnp.stack([me] + [4 * qx + 2 * qy + qc for qx, qy, qc in (_peer(px, py, pc, k) for k in range(1, N_DEV))])
    idx = idx.astype(jnp.int32)

    dev = lambda d: (d,)
    pack = _wire_pack(big_w, WIRE_DTYPE)
    win_g, = _all_gather(pack, ((0, dev, 0, W_IN_ROWS),), ((N_DEV, W_IN_ROWS, PACK_W),), name="ag_early")
    ag_mid = _gather_start(pack, W_IN_ROWS, ((0, dev, OUT_ROWS), (1, dev, SMALL_ROWS)),
                           ((N_DEV, OUT_ROWS, PACK_W), (N_DEV, SMALL_ROWS, PACK_W)), name="ag_mid_start")
    ag = {}

    def own_rows(r0, r1, shape):
        return pack[r0:r1].reshape(shape)

    def mid_weights(after):
        pack_mid, (wout_g, small_g) = _gather_wait(ag_mid, W_IN_ROWS, MID_ROWS, after, name="ag_mid_wait")
        ag["late"] = _gather_start(pack_mid, EARLY_ROWS, ((0, _gate_slab, FF_COLS), (0, _up_slab, FF_COLS), (1, dev, FF_COLS)),
                                   (GU_SHAPE, D_SHAPE), name="ag_late_start")
        wout_g = lax.dynamic_update_slice(wout_g, own_rows(W_IN_ROWS, SMALL_ROW0, (1, OUT_ROWS, PACK_W)), (me, 0, 0))
        small_g = lax.dynamic_update_slice(small_g, own_rows(SMALL_ROW0, EARLY_ROWS, (1, SMALL_ROWS, PACK_W)), (me, 0, 0))
        ops = _mid_operands(wout_g, small_g)
        ops["wuq"] = ops["wuq"] + ag["late"][-1][0:1, 0:1].astype(ops["wuq"].dtype)
        return ops

    def late_weights(after):
        _, (gu, d) = _gather_wait(ag["late"], EARLY_ROWS, LATE_ROWS, after, name="ag_late_wait")
        slab = (1, 1, 1, FF_COLS, PACK_W)
        gu = lax.dynamic_update_slice(gu, own_rows(EARLY_ROWS, EARLY_ROWS + FF_COLS, slab), _gate_slab(me) + (0, 0))
        gu = lax.dynamic_update_slice(gu, own_rows(EARLY_ROWS + FF_COLS, EARLY_ROWS + 2 * FF_COLS, slab), _up_slab(me) + (0, 0))
        d = lax.dynamic_update_slice(d, own_rows(EARLY_ROWS + 2 * FF_COLS, PACK_ROWS, (1, FF_COLS, PACK_W)), (me, 0, 0))
        return gu.reshape(2 * D_FF, D_MODEL), d.reshape(D_FF, D_MODEL)

    rs = {}

    def late_grads(g_gu, g_d):
        rs["late"] = _scatter_start([g_gu.reshape(GU_SHAPE), g_d.reshape(D_SHAPE)],
                                    ((0, _gate_slab, FF_COLS), (0, _up_slab, FF_COLS), (1, dev, FF_COLS)),
                                    name="rs_late_start")
        return rs["late"][-1]

    def mid_grads(g):
        rs["mid"] = _scatter_start([_mid_grad_pack(g)], ((0, dev, MID_ROWS),), name="rs_mid_start")
        return rs["mid"][-1]

    def last_grads(g_win_t):
        rs["last"] = _scatter_start([_w_in_grad_chunks(g_win_t)], ((0, dev, W_IN_ROWS),), name="rs_last_start")
        return rs["last"][-1]

    first_w = dict(small_w, mix_norm_g=mix_norm_g + ag_mid[-1][0:1, 0:1])
    loss_tot, gx, g_small = _local_step(
        x[0], loss_target[0], _w_in_operand(win_g), first_w, types.SimpleNamespace(mid=mid_weights, late=late_weights),
        types.SimpleNamespace(late=late_grads, mid=mid_grads, last=last_grads))

    g_gu, g_d, land_late = _scatter_wait(rs["late"], gx, name="rs_late_wait")
    g_mid, land_mid = _scatter_wait(rs["mid"], gx, name="rs_mid_wait")
    g_win, land_last = _scatter_wait(rs["last"], gx, name="rs_last_wait")
    gw_t = dict(w_gate=_peer_sum(g_gu, _gate_slab, land_late, 0, FF_COLS, idx, name="rs_sum_gate"),
                w_up=_peer_sum(g_gu, _up_slab, land_late, 1, FF_COLS, idx, name="rs_sum_up"),
                w_in=_peer_sum(g_win, dev, land_last, 0, W_IN_ROWS, idx, name="rs_sum_in")[0:W_IN_COLS])
    gw = dict(w_down=_peer_sum(g_d, dev, land_late, 2, FF_COLS, idx, name="rs_sum_down"))
    gw.update(_mid_unpack(_peer_sum(g_mid, dev, land_mid, 0, MID_ROWS, idx, name="rs_sum_mid")))
    dw, mw, vw = {}, {}, {}
    swap = lambda a: jnp.swapaxes(a, 1, 2)
    for n in BIG:
        if n in gw_t:
            res = _adamw_call(swap(big_w3[n]), gw_t[n], swap(big_m[n]), swap(big_v[n]), name="adamw_" + n)
            dw[n], mw[n], vw[n] = (swap(r) for r in res)
        else:
            dw[n], mw[n], vw[n] = _adamw_call(big_w3[n], gw[n], big_m[n], big_v[n], name="adamw_" + n)
    gw = {n: g[None] for n, g in gw.items()}
    gw.update({n: swap(g[None]) for n, g in gw_t.items()})

    loss_rows = jnp.pad(loss_tot[0:1, 0:1], ((0, SUBLANES - 1), (0, PACK_W - 1)))
    small_rows = jnp.concatenate([_small_pack(g_small_rows(g_small), SUBLANES), loss_rows], 0)
    parts, = _all_gather(small_rows, ((0, lambda d: (d,), 0, small_rows.shape[0]),), ((N_DEV,) + small_rows.shape,),
                         name="ag_small")
    gs, ds, ms, vs = _small_adamw(parts, _small_pack(small_w, 1), _small_pack(small_m, 1), _small_pack(small_v, 1))
    loss = gs[len(SMALL), 0]

    def small_out(packed):
        out = {}
        for k, n in enumerate(SMALL):
            out[n] = packed[k:k + 1, :SMALL_W[n]]
        out["final_norm_g"] = out["final_norm_g"].reshape(D_MODEL)
        return out

    gs, ds, ms, vs = small_out(gs), small_out(ds), small_out(ms), small_out(vs)

    order = ("mix_norm_g", "w_in", "swa_sinks", "q_norm_g", "w_uq", "kv_norm_g", "w_ukv", "w_o_swa", "w_o_mla", "w_out",
             "ffn_norm_g", "w_gate", "w_up", "w_down", "final_norm_g")

    def leaves(big, small):
        return [big[n] if n in big else small[n] for n in order]

    return (loss, gx[None], *leaves(gw, gs), *leaves(dw, ds), *leaves(mw, ms), *leaves(vw, vs))


def g_small_rows(g_small):
    out = dict(g_small)
    out["swa_sinks"] = jnp.pad(g_small["swa_sinks"], ((0, SUBLANES - 1), (0, 0)))
    return out
```

```python
import types

import numpy as np
import jax
import jax.numpy as jnp
from jax import lax
from jax.experimental import pallas as pl
from jax.experimental.pallas import tpu as pltpu

F32 = jnp.float32
MXU_DTYPE = jnp.bfloat16
WIRE_DTYPE = jnp.bfloat16

D_MODEL = 1024
EPS = 1e-6
ROPE_THETA = 10000.0
BLOCK = 128
HEAD_DIM = 64
SWA_HEADS = 8
SWA_KV_HEADS = 2
SWA_GROUP = SWA_HEADS // SWA_KV_HEADS
MLA_HEADS = 8
MLA_NOPE = 64
MLA_ROPE = 32
MLA_V = 64
MLA_QK = MLA_NOPE + MLA_ROPE
Q_LORA = 384
KV_LORA = 256
D_FF = 2816
IN_SIZES = (512, 128, 128, Q_LORA, KV_LORA, MLA_ROPE, D_MODEL, D_MODEL)
IN_OFF = tuple(int(v) for v in np.cumsum((0,) + IN_SIZES))
ADAM_LR, ADAM_B1, ADAM_B2, ADAM_EPS, ADAM_WD, ADAM_STEP = 0.001, 0.9, 0.999, 1e-08, 0.01, 10

LANES = 128
SUBLANES = 8
VMEM_LIMIT = 48 * 1024 * 1024
N_DEV = 8
AXES = ("x", "y", "c")

P_GA, P_GB, P_Q, P_QLAT, P_KR, P_K, P_V, P_KVLAT, P_W = 0, 1024, 2048, 3072, 3456, 3584, 3840, 4096, 4352
KR_LANE = 64

LOG2E = 1.4426950408889634

NT = (((1,), (1,)), ((), ()))
NN = (((1,), (0,)), ((), ()))
TN = (((0,), (0,)), ((), ()))


def _cparams(sem):
    return pltpu.CompilerParams(dimension_semantics=sem, vmem_limit_bytes=VMEM_LIMIT)


def _mm(a, b, mode, *, name, out_dtype=F32, add=None, after=None, tm=512, tn=512, tk=None):
    if mode == "nn":
        (M, K), (K2, N) = a.shape, b.shape
    elif mode == "nt":
        (M, K), (N, K2) = a.shape, b.shape
    else:
        (K, M), (K2, N) = a.shape, b.shape
    assert K == K2, (a.shape, b.shape, mode)
    tm, tn, tk = min(tm, M), min(tn, N), K if tk is None else min(tk, K)
    assert M % tm == 0 and N % tn == 0 and K % tk == 0, (M, N, K, tm, tn, tk)
    nk = K // tk
    dn = {"nn": NN, "nt": NT, "tn": TN}[mode]
    if mode == "tn":
        a_spec = pl.BlockSpec((tk, tm), lambda i, j, k: (k, i))
    else:
        a_spec = pl.BlockSpec((tm, tk), lambda i, j, k: (i, k))
    once = dict(pipeline_mode=pl.Buffered(1)) if (nk == 1 and tn == N) else {}
    if mode == "nt":
        b_spec = pl.BlockSpec((tn, tk), lambda i, j, k: (j, k), **once)
    else:
        b_spec = pl.BlockSpec((tk, tn), lambda i, j, k: (k, j), **once)
    o_spec = pl.BlockSpec((tm, tn), lambda i, j, k: (i, j))
    has_add, has_after = add is not None, after is not None

    def body(*refs):
        a_ref, b_ref = refs[0], refs[1]
        add_ref = refs[2] if has_add else None
        o_ref = refs[2 + has_add + has_after]
        p = lax.dot_general(a_ref[...], b_ref[...], dn, preferred_element_type=F32)

        def finish(acc):
            if has_add:
                acc = acc + add_ref[...]
            o_ref[...] = acc.astype(o_ref.dtype)

        if nk == 1:
            finish(p)
        else:
            acc_ref = refs[-1]
            k = pl.program_id(2)

            @pl.when(k == 0)
            def _():
                acc_ref[...] = p

            @pl.when((k > 0) & (k < nk - 1))
            def _():
                acc_ref[...] += p

            @pl.when(k == nk - 1)
            def _():
                finish(acc_ref[...] + p)

    ins = [a, b] + ([add] if has_add else []) + ([after] if has_after else [])
    in_specs = [a_spec, b_spec] + ([o_spec] if has_add else []) + ([pl.BlockSpec(memory_space=pl.ANY)] if has_after else [])
    return pl.pallas_call(
        body, name=name, grid=(M // tm, N // tn, nk), in_specs=in_specs, out_specs=o_spec,
        out_shape=jax.ShapeDtypeStruct((M, N), out_dtype),
        scratch_shapes=[pltpu.VMEM((tm, tn), F32)] if nk > 1 else [],
        compiler_params=_cparams(("parallel", "parallel", "arbitrary")),
    )(*ins)


def _rows(ts, w, cb=0):
    return pl.BlockSpec((ts, w), lambda i: (i, cb))


def _const(r, w):
    return pl.BlockSpec((r, w), lambda i: (0, 0))


def _sublane_sum(v):
    ts, c = v.shape
    return jnp.sum(v.reshape(ts // SUBLANES, SUBLANES, c), axis=0)


def _sigmoid(v):
    return 1.0 / (1.0 + jnp.exp(-v))


def _rope(v, cos, s_up, s_dn, up, dn):
    return v * cos + pltpu.roll(v, up, 1) * s_up + pltpu.roll(v, dn, 1) * s_dn


def _rope_t(dv, cos, s_up, s_dn, up, dn):
    return dv * cos + pltpu.roll(dv * s_up, dn, 1) + pltpu.roll(dv * s_dn, up, 1)


def _rope_tables(seq):
    pos = np.arange(seq, dtype=np.float32)[:, None]

    def base(dim):
        inv = np.float32(ROPE_THETA) ** (-np.arange(0, dim, 2, dtype=np.float32) / np.float32(dim))
        ang = (pos * inv.astype(np.float32)[None, :]).astype(np.float32)
        return np.cos(ang).astype(np.float32), np.sin(ang).astype(np.float32)

    z = lambda n: np.zeros((seq, n), np.float32)
    ca, sa = base(HEAD_DIM)
    a_cos = np.concatenate([ca, ca, z(64)], 1)
    a_up = np.concatenate([-sa, z(96)], 1)
    a_dn = np.concatenate([z(32), sa, z(64)], 1)
    cb, sb = base(MLA_ROPE)
    one = np.ones((seq, 64), np.float32)
    q_cos = np.concatenate([one, cb, cb, z(32)], 1)
    k_cos = np.concatenate([z(64), cb, cb, z(32)], 1)
    b_up = np.concatenate([z(64), -sb, z(48)], 1)
    b_dn = np.concatenate([z(80), sb, z(32)], 1)
    return tuple(jnp.asarray(t) for t in (a_cos, a_up, a_dn, q_cos, k_cos, b_up, b_dn))


def _rms(v, g):
    return v * lax.rsqrt(jnp.mean(v * v, axis=-1, keepdims=True) + EPS) * g


def _rms_bwd(v, g, d):
    r = lax.rsqrt(jnp.mean(v * v, axis=-1, keepdims=True) + EPS)
    xh = v * r
    dxh = d * g
    return r * (dxh - xh * jnp.mean(dxh * xh, axis=-1, keepdims=True)), d * xh


F_GA, F_GB, F_KVLAT, F_QLAT, F_W = 0, 1024, 2048, 2304, 2688


def _proj_in(x, g, w_t, gq, gkv, tabs, *, tm=512):
    s_, c = x.shape
    a_cos, a_up, a_dn, _, k_cos, b_up, b_dn = tabs

    def body(x_ref, g_ref, w_ref, gq_ref, gkv_ref, ac, au, ad, kc, bu, bd,
             h_ref, qa_ref, ka_ref, va_ref, cq_ref, ckv_ref, kro_ref, pf_ref):
        h = _rms(x_ref[...], g_ref[...]).astype(h_ref.dtype)
        h_ref[...] = h
        mm = lambda a, b: lax.dot_general(h, w_ref[a:b, :], NT, preferred_element_type=F32)
        pf_ref[:, F_GA:F_KVLAT] = mm(P_GA, P_Q)
        c_, u_, d_ = ac[...], au[...], ad[...]
        q = mm(P_Q, P_QLAT)
        for hd in range(SWA_HEADS):
            sl = slice(hd * LANES, (hd + 1) * LANES)
            qa_ref[:, sl] = _rope(q[:, sl], c_, u_, d_, 96, 32).astype(qa_ref.dtype)
        kv = mm(P_KR, P_KVLAT)
        kro_ref[...] = _rope(kv[:, :LANES], kc[...], bu[...], bd[...], 112, 16)
        for hd in range(SWA_KV_HEADS):
            sl = slice((1 + hd) * LANES, (2 + hd) * LANES)
            ka_ref[:, hd * LANES:(hd + 1) * LANES] = _rope(kv[:, sl], c_, u_, d_, 96, 32).astype(ka_ref.dtype)
        va_ref[...] = kv[:, P_V - P_KR:].astype(va_ref.dtype)
        for a, b, f0, gref, dst in ((P_QLAT, P_KR, F_QLAT, gq_ref, cq_ref), (P_KVLAT, P_W, F_KVLAT, gkv_ref, ckv_ref)):
            v = mm(a, b)
            pf_ref[:, f0:f0 + b - a] = v
            r = lax.rsqrt(jnp.mean(v * v, axis=-1, keepdims=True) + EPS)
            dst[...] = (v * r * gref[...]).astype(dst.dtype)

    tab = _rows(tm, LANES)
    widths = (c, SWA_HEADS * LANES, SWA_KV_HEADS * LANES, SWA_KV_HEADS * LANES, Q_LORA, KV_LORA)
    return pl.pallas_call(
        body, name="proj_in", grid=(s_ // tm,),
        in_specs=[_rows(tm, c), _const(1, c), pl.BlockSpec((P_W, c), lambda i: (0, 0), pipeline_mode=pl.Buffered(1)),
                  _const(1, Q_LORA), _const(1, KV_LORA), tab, tab, tab, tab, tab, tab],
        out_specs=[_rows(tm, w) for w in widths] + [tab, _rows(tm, F_W)],
        out_shape=[jax.ShapeDtypeStruct((s_, w), MXU_DTYPE) for w in widths]
        + [jax.ShapeDtypeStruct((s_, LANES), F32), jax.ShapeDtypeStruct((s_, F_W), F32)],
        compiler_params=_cparams(("parallel",)),
    )(x, g, w_t, gq, gkv, a_cos, a_up, a_dn, k_cos, b_up, b_dn)


def _mm_norm_bwd(a, b, x, g, res, *, name, after=None, tm=512):
    s_, kk = a.shape
    c = b.shape[1]
    has_after = after is not None

    def body(*refs):
        a_ref, b_ref, x_ref, g_ref, res_ref = refs[:5]
        dx_ref, dxb_ref, dg_ref = refs[5 + has_after:]
        d = jnp.dot(a_ref[...], b_ref[...], preferred_element_type=F32)
        dx, gg = _rms_bwd(x_ref[...], g_ref[...], d)
        dx = dx + res_ref[...]
        dx_ref[...] = dx
        dxb_ref[...] = dx.astype(dxb_ref.dtype)

        @pl.when(pl.program_id(0) == 0)
        def _():
            dg_ref[...] = jnp.zeros(dg_ref.shape, F32)

        dg_ref[...] += _sublane_sum(gg)

    row = _rows(tm, c)
    in_specs = [_rows(tm, kk), pl.BlockSpec((kk, c), lambda i: (0, 0), pipeline_mode=pl.Buffered(1)), row, _const(1, c), row]
    return pl.pallas_call(
        body, name=name, grid=(s_ // tm,), in_specs=in_specs + ([pl.BlockSpec(memory_space=pl.ANY)] if has_after else []),
        out_specs=[row, row, _const(SUBLANES, c)],
        out_shape=[jax.ShapeDtypeStruct((s_, c), F32), jax.ShapeDtypeStruct((s_, c), MXU_DTYPE),
                   jax.ShapeDtypeStruct((SUBLANES, c), F32)],
        compiler_params=_cparams(("arbitrary",)),
    )(*([a, b, x, g, res] + ([after] if has_after else [])))


def _mla_up(cq, ckv, kro, wuq, wuk, wuv, tabs, *, ts=512):
    s_ = cq.shape[0]
    _, _, _, q_cos, _, b_up, b_dn = tabs

    def body(cq_ref, ckv_ref, kr_ref, wq_ref, wk_ref, wv_ref, qc, bu, bd, qo_ref, ko_ref, vo_ref):
        c_, u_, d_ = qc[...], bu[...], bd[...]
        kr = kr_ref[...]
        ckv_ = ckv_ref[...]
        vo_ref[...] = jnp.dot(ckv_, wv_ref[...], preferred_element_type=F32).astype(vo_ref.dtype)
        q = jnp.dot(cq_ref[...], wq_ref[...], preferred_element_type=F32)
        k = jnp.dot(ckv_, wk_ref[...], preferred_element_type=F32)
        for h in range(MLA_HEADS):
            sl = slice(h * LANES, (h + 1) * LANES)
            qo_ref[:, sl] = _rope(q[:, sl], c_, u_, d_, 112, 16).astype(qo_ref.dtype)
            ko_ref[:, sl] = (k[:, sl] + kr).astype(ko_ref.dtype)

    tab, out = _rows(ts, LANES), _rows(ts, 1024)
    return pl.pallas_call(
        body, name="mla_up", grid=(s_ // ts,),
        in_specs=[_rows(ts, Q_LORA), _rows(ts, KV_LORA), tab, _const(Q_LORA, 1024), _const(KV_LORA, 1024),
                  _const(KV_LORA, 1024), tab, tab, tab],
        out_specs=[out, out, out], out_shape=[jax.ShapeDtypeStruct((s_, 1024), MXU_DTYPE)] * 3,
        compiler_params=_cparams(("parallel",)),
    )(cq, ckv, kro, wuq, wuk, wuv, q_cos, b_up, b_dn)


def _mla_up_bwd(dqc, dkc, dvp, wuq, wukv, p, gq, gkv, tabs, *, ts=256):
    s_ = dqc.shape[0]
    _, _, _, q_cos, k_cos, b_up, b_dn = tabs

    def body(dq_ref, dk_ref, dv_ref, wq_ref, wkv_ref, ql_ref, kvl_ref, gq_ref, gkv_ref, qc, kc, bu, bd,
             dqo_ref, dkvo_ref, dkr_ref, dql_ref, dkvl_ref, dgq_ref, dgkv_ref):
        c_, u_, d_ = qc[...], bu[...], bd[...]
        tot = jnp.zeros((ts, LANES), F32)
        for h in range(MLA_HEADS):
            sl = slice(h * LANES, (h + 1) * LANES)
            dqo_ref[:, sl] = _rope_t(dq_ref[:, sl], c_, u_, d_, 112, 16).astype(dqo_ref.dtype)
            dk = dk_ref[:, sl]
            dkvo_ref[:, sl] = dk.astype(dkvo_ref.dtype)
            tot = tot + dk
        dkvo_ref[:, 1024:2048] = dv_ref[...].astype(dkvo_ref.dtype)
        dkr_ref[...] = _rope_t(tot, kc[...], u_, d_, 112, 16).astype(dkr_ref.dtype)

        @pl.when(pl.program_id(0) == 0)
        def _():
            dgq_ref[...] = jnp.zeros(dgq_ref.shape, F32)
            dgkv_ref[...] = jnp.zeros(dgkv_ref.shape, F32)

        for do_ref, w_ref, x_ref, g_ref, dx_ref, dg_ref in ((dqo_ref, wq_ref, ql_ref, gq_ref, dql_ref, dgq_ref),
                                                            (dkvo_ref, wkv_ref, kvl_ref, gkv_ref, dkvl_ref, dgkv_ref)):
            d = lax.dot_general(do_ref[...], w_ref[...], NT, preferred_element_type=F32)
            dx, gg = _rms_bwd(x_ref[...], g_ref[...], d)
            dx_ref[...] = dx.astype(dx_ref.dtype)
            dg_ref[...] += _sublane_sum(gg)

    tab = _rows(ts, LANES)
    return pl.pallas_call(
        body, name="mla_up_bwd", grid=(s_ // ts,),
        in_specs=[_rows(ts, 1024), _rows(ts, 1024), _rows(ts, 1024), _const(Q_LORA, 1024), _const(KV_LORA, 2048),
                  _rows(ts, Q_LORA, F_QLAT // Q_LORA), _rows(ts, KV_LORA, F_KVLAT // KV_LORA),
                  _const(1, Q_LORA), _const(1, KV_LORA), tab, tab, tab, tab],
        out_specs=[_rows(ts, 1024), _rows(ts, 2048), _rows(ts, LANES), _rows(ts, Q_LORA), _rows(ts, KV_LORA),
                   _const(SUBLANES, Q_LORA), _const(SUBLANES, KV_LORA)],
        out_shape=[jax.ShapeDtypeStruct((s_, 1024), MXU_DTYPE), jax.ShapeDtypeStruct((s_, 2048), MXU_DTYPE),
                   jax.ShapeDtypeStruct((s_, LANES), MXU_DTYPE), jax.ShapeDtypeStruct((s_, Q_LORA), MXU_DTYPE),
                   jax.ShapeDtypeStruct((s_, KV_LORA), MXU_DTYPE), jax.ShapeDtypeStruct((SUBLANES, Q_LORA), F32),
                   jax.ShapeDtypeStruct((SUBLANES, KV_LORA), F32)],
        compiler_params=_cparams(("arbitrary",)),
    )(dqc, dkc, dvp, wuq, wukv, p, p, gq, gkv, q_cos, k_cos, b_up, b_dn)


def _assemble_dp(dgab, dqa, dqlat, dkr, dka, dva, dkvlat, tabs, *, ts=256):
    s_ = dqa.shape[0]
    a_cos, a_up, a_dn = tabs[0], tabs[1], tabs[2]

    def body(dg_ref, dq_ref, dql_ref, dkr_ref, dk_ref, dv_ref, dkvl_ref, ac, au, ad, o_ref):
        c_, u_, d_ = ac[...], au[...], ad[...]
        o_ref[:, P_GA:P_Q] = dg_ref[...]
        for h in range(SWA_HEADS):
            sl = slice(h * LANES, (h + 1) * LANES)
            o_ref[:, P_Q + h * LANES:P_Q + (h + 1) * LANES] = _rope_t(dq_ref[:, sl], c_, u_, d_, 96, 32).astype(o_ref.dtype)
        o_ref[:, P_QLAT:P_KR] = dql_ref[...]
        o_ref[:, P_KR:P_K] = dkr_ref[...]
        for h in range(SWA_KV_HEADS):
            sl = slice(h * LANES, (h + 1) * LANES)
            o_ref[:, P_K + h * LANES:P_K + (h + 1) * LANES] = _rope_t(dk_ref[:, sl], c_, u_, d_, 96, 32).astype(o_ref.dtype)
        o_ref[:, P_V:P_KVLAT] = dv_ref[...]
        o_ref[:, P_KVLAT:P_W] = dkvl_ref[...]

    tab = _rows(ts, LANES)
    return pl.pallas_call(
        body, name="assemble_dp", grid=(s_ // ts,),
        in_specs=[_rows(ts, 2048), _rows(ts, 1024), _rows(ts, Q_LORA), _rows(ts, LANES), _rows(ts, 256), _rows(ts, 256),
                  _rows(ts, KV_LORA), tab, tab, tab],
        out_specs=_rows(ts, P_W), out_shape=jax.ShapeDtypeStruct((s_, P_W), MXU_DTYPE),
        compiler_params=_cparams(("parallel",)),
    )(dgab, dqa, dqlat, dkr, dka, dva, dkvlat, a_cos, a_up, a_dn)


def _attn_out_gate(oa, ob, woa, wob, p, *, ts=512):
    s_ = p.shape[0]

    def body(oa_ref, ob_ref, wa_ref, wb_ref, ga_ref, gb_ref, ta_ref, tb_ref, y_ref):
        ta = jnp.dot(oa_ref[...], wa_ref[...], preferred_element_type=F32)
        tb = jnp.dot(ob_ref[...], wb_ref[...], preferred_element_type=F32)
        ta_ref[...] = ta
        tb_ref[...] = tb
        y_ref[...] = (_sigmoid(ga_ref[...]) * ta + _sigmoid(gb_ref[...]) * tb).astype(y_ref.dtype)

    w = _const(1024, 1024)
    return pl.pallas_call(
        body, name="attn_out_gate", grid=(s_ // ts,),
        in_specs=[_rows(ts, 1024), _rows(ts, 1024), w, w, _rows(ts, 1024, F_GA // 1024), _rows(ts, 1024, F_GB // 1024)],
        out_specs=[_rows(ts, 1024)] * 3,
        out_shape=[jax.ShapeDtypeStruct((s_, 1024), F32)] * 2 + [jax.ShapeDtypeStruct((s_, 1024), MXU_DTYPE)],
        compiler_params=_cparams(("parallel",)),
    )(oa, ob, woa, wob, p, p)


def _d_y_gate(dx1b, wout, p, ta, tb, *, ts=512):
    s_ = p.shape[0]

    def body(dx_ref, w_ref, ga_ref, gb_ref, ta_ref, tb_ref, dta_ref, dtb_ref, dg_ref):
        d = lax.dot_general(dx_ref[...], w_ref[...], NT, preferred_element_type=F32)
        sa, sb = _sigmoid(ga_ref[...]), _sigmoid(gb_ref[...])
        dta_ref[...] = (d * sa).astype(dta_ref.dtype)
        dtb_ref[...] = (d * sb).astype(dtb_ref.dtype)
        dg_ref[:, 0:1024] = (d * ta_ref[...] * (sa * (1.0 - sa))).astype(dg_ref.dtype)
        dg_ref[:, 1024:2048] = (d * tb_ref[...] * (sb * (1.0 - sb))).astype(dg_ref.dtype)

    return pl.pallas_call(
        body, name="d_y_gate", grid=(s_ // ts,),
        in_specs=[_rows(ts, 1024), _const(1024, 1024), _rows(ts, 1024, F_GA // 1024), _rows(ts, 1024, F_GB // 1024),
                  _rows(ts, 1024), _rows(ts, 1024)],
        out_specs=[_rows(ts, 1024), _rows(ts, 1024), _rows(ts, 2048)],
        out_shape=[jax.ShapeDtypeStruct((s_, 1024), MXU_DTYPE)] * 2 + [jax.ShapeDtypeStruct((s_, 2048), MXU_DTYPE)],
        compiler_params=_cparams(("parallel",)),
    )(dx1b, wout, p, p, ta, tb)


FF_TILE = D_FF // 2


def _ffn_in_act(x1, g, wgu_t, *, tm=512):
    s_ = x1.shape[0]
    n = s_ // tm

    def body(x_ref, g_ref, w_ref, h_ref, gu_ref, a_ref):
        h = _rms(x_ref[...], g_ref[...]).astype(h_ref.dtype)
        h_ref[...] = h
        p = lax.dot_general(h, w_ref[...], NT, preferred_element_type=F32)
        gu_ref[...] = p
        gate = p[:, :FF_TILE]
        a_ref[...] = (gate * _sigmoid(gate) * p[:, FF_TILE:]).astype(a_ref.dtype)

    return pl.pallas_call(
        body, name="ffn_in", grid=(2, s_ // tm),
        in_specs=[pl.BlockSpec((tm, D_MODEL), lambda j, i: (i, 0)), pl.BlockSpec((1, D_MODEL), lambda j, i: (0, 0)),
                  pl.BlockSpec((2 * FF_TILE, D_MODEL), lambda j, i: (j, 0))],
        out_specs=[pl.BlockSpec((tm, D_MODEL), lambda j, i: (i + j * (n - 1 - i), 0)),
                   pl.BlockSpec((tm, 2 * FF_TILE), lambda j, i: (i, j)),
                   pl.BlockSpec((tm, FF_TILE), lambda j, i: (i, j))],
        out_shape=[jax.ShapeDtypeStruct((s_, D_MODEL), MXU_DTYPE), jax.ShapeDtypeStruct((s_, 2 * D_FF), F32),
                   jax.ShapeDtypeStruct((s_, D_FF), MXU_DTYPE)],
        compiler_params=_cparams(("arbitrary", "arbitrary")),
    )(x1, g, wgu_t)


def _d_act_swiglu(dx2b, wd, gu, *, tm=512):
    s_ = dx2b.shape[0]

    def body(d_ref, w_ref, gu_ref, o_ref):
        da = lax.dot_general(d_ref[...], w_ref[...], NT, preferred_element_type=F32)
        g, u = gu_ref[:, :FF_TILE], gu_ref[:, FF_TILE:]
        sg = _sigmoid(g)
        o_ref[:, :FF_TILE] = (da * u * (sg * (1.0 + g * (1.0 - sg)))).astype(o_ref.dtype)
        o_ref[:, FF_TILE:] = (da * (g * sg)).astype(o_ref.dtype)

    gu_spec = pl.BlockSpec((tm, 2 * FF_TILE), lambda j, i: (i, j))
    return pl.pallas_call(
        body, name="d_act", grid=(2, s_ // tm),
        in_specs=[pl.BlockSpec((tm, D_MODEL), lambda j, i: (i, 0)), pl.BlockSpec((FF_TILE, D_MODEL), lambda j, i: (j, 0)), gu_spec],
        out_specs=gu_spec, out_shape=jax.ShapeDtypeStruct((s_, 2 * D_FF), MXU_DTYPE),
        compiler_params=_cparams(("parallel", "parallel")),
    )(dx2b, wd, gu)


def _ffn_out_loss(act, wd, x1, g, tgt, *, ts=512):
    s_, c = x1.shape
    kk = act.shape[1]

    def body(a_ref, w_ref, x_ref, g_ref, t_ref, dx_ref, dxb_ref, dg_ref, lp_ref, tot_ref):
        v = x_ref[...] + jnp.dot(a_ref[...], w_ref[...], preferred_element_type=F32)
        r = lax.rsqrt(jnp.mean(v * v, axis=-1, keepdims=True) + EPS)
        xh = v * r
        gg = g_ref[...]
        e = xh * gg - t_ref[...]
        do = e * (1.0 / c)
        dxh = do * gg
        dx = r * (dxh - xh * jnp.mean(dxh * xh, axis=-1, keepdims=True))
        dx_ref[...] = dx
        dxb_ref[...] = dx.astype(dxb_ref.dtype)
        i = pl.program_id(0)

        @pl.when(i == 0)
        def _():
            dg_ref[...] = jnp.zeros(dg_ref.shape, F32)
            lp_ref[...] = jnp.zeros(lp_ref.shape, F32)

        dg_ref[...] += _sublane_sum(do * xh)
        lp_ref[...] += _sublane_sum(e * e)
        tot_ref[...] = jnp.full(tot_ref.shape, (0.5 / c) * jnp.sum(lp_ref[...]), F32)

    return pl.pallas_call(
        body, name="ffn_out_loss", grid=(s_ // ts,),
        in_specs=[_rows(ts, kk), _const(kk, c), _rows(ts, c), _const(1, c), _rows(ts, c)],
        out_specs=[_rows(ts, c), _rows(ts, c), _const(SUBLANES, c), _const(SUBLANES, c), _const(SUBLANES, LANES)],
        out_shape=[jax.ShapeDtypeStruct((s_, c), F32), jax.ShapeDtypeStruct((s_, c), MXU_DTYPE),
                   jax.ShapeDtypeStruct((SUBLANES, c), F32), jax.ShapeDtypeStruct((SUBLANES, c), F32),
                   jax.ShapeDtypeStruct((SUBLANES, LANES), F32)],
        compiler_params=_cparams(("arbitrary",)),
    )(act, wd, x1, g, tgt)


def _mla_d_out(dtb, wob, o32, *, ts=512):
    s_ = dtb.shape[0]

    def body(dt_ref, w_ref, o_ref, dob_ref, dl_ref):
        d = lax.dot_general(dt_ref[...], w_ref[...], NT, preferred_element_type=F32)
        dob_ref[...] = d.astype(dob_ref.dtype)
        prod = d * o_ref[...]
        for h in range(MLA_HEADS):
            dl_ref[h] = jnp.sum(prod[:, h * LANES:(h + 1) * LANES].T, axis=0, keepdims=True)

    return pl.pallas_call(
        body, name="mla_d_out", grid=(s_ // ts,), in_specs=[_rows(ts, 1024), _const(1024, 1024), _rows(ts, 1024)],
        out_specs=[_rows(ts, 1024), pl.BlockSpec((MLA_HEADS, 1, ts), lambda i: (0, 0, i))],
        out_shape=[jax.ShapeDtypeStruct((s_, 1024), MXU_DTYPE), jax.ShapeDtypeStruct((MLA_HEADS, 1, s_), F32)],
        compiler_params=_cparams(("parallel",)),
    )(dtb, wob, o32)


SWA_T = 4 * BLOCK


SWA_W = SWA_GROUP * BLOCK


def _swa_masks(sb):
    kr = lax.broadcasted_iota(jnp.int32, (2 * BLOCK, SWA_W), 0)
    qc = jnp.bitwise_and(lax.broadcasted_iota(jnp.int32, (2 * BLOCK, SWA_W), 1), BLOCK - 1)
    band = jnp.logical_and(kr > qc, kr <= qc + BLOCK)
    first = jnp.logical_and(band, kr >= BLOCK)
    return band, jnp.logical_or(first, jnp.logical_and(band, sb > 0))


def _heads_to_rows(ref, rs):
    return jnp.concatenate([ref[rs, h * LANES:(h + 1) * LANES] for h in range(SWA_GROUP)], axis=0)


def _sink_row(sk_ref):
    return jnp.concatenate([sk_ref[0, h:h + 1, :] for h in range(SWA_GROUP)], axis=1) * LOG2E


def _swa_in_specs(rev, nsb):
    sbi = (lambda j: nsb - 1 - j) if rev else (lambda j: j)
    cur = pl.BlockSpec((SWA_T, LANES), lambda g, j: (sbi(j), g))
    prev = pl.BlockSpec((BLOCK, LANES), lambda g, j: (jnp.maximum(4 * sbi(j) - 1, 0), g))
    q = pl.BlockSpec((SWA_T, SWA_GROUP * LANES), lambda g, j: (sbi(j), g))
    sink = pl.BlockSpec((1, SUBLANES, LANES), lambda g, j: (g, 0, 0))
    lse = pl.BlockSpec((SWA_GROUP, 1, SWA_T), lambda g, j: (g, 0, sbi(j)))
    return q, cur, prev, sink, lse


def _swa_fwd(qa, ka, va, sink_b):
    s_ = qa.shape[0]
    nsb = s_ // SWA_T
    c2 = HEAD_DIM ** -0.5 * LOG2E

    def body(q_ref, kc_ref, kp_ref, vc_ref, vp_ref, sk_ref, o32_ref, o16_ref, lse_ref, kx, vx):
        kx[0:BLOCK, :] = kp_ref[...]
        kx[BLOCK:5 * BLOCK, :] = kc_ref[...]
        vx[0:BLOCK, :] = vp_ref[...]
        vx[BLOCK:5 * BLOCK, :] = vc_ref[...]
        band, band0 = _swa_masks(pl.program_id(1))
        sink2 = _sink_row(sk_ref)
        for b in range(4):
            rs = slice(b * BLOCK, (b + 1) * BLOCK)
            ks = slice(b * BLOCK, (b + 2) * BLOCK)
            st = lax.dot_general(kx[ks, :], _heads_to_rows(q_ref, rs), NT, preferred_element_type=F32) * c2
            st = jnp.where(band0 if b == 0 else band, st, -jnp.inf)
            m = jnp.maximum(jnp.max(st, axis=0, keepdims=True), sink2)
            pt = jnp.exp2(st - m)
            den = jnp.sum(pt, axis=0, keepdims=True) + jnp.exp2(sink2 - m)
            o = lax.dot_general((pt * (1.0 / den)).astype(MXU_DTYPE), vx[ks, :], TN, preferred_element_type=F32)
            lse = m + jnp.log2(den)
            for hh in range(SWA_GROUP):
                cs = slice(hh * LANES, (hh + 1) * LANES)
                o32_ref[rs, cs] = o[cs, :]
                o16_ref[rs, cs] = o[cs, :].astype(o16_ref.dtype)
                lse_ref[hh, :, rs] = lse[:, cs]

    q, cur, prev, sink, lse_spec = _swa_in_specs(False, nsb)
    return pl.pallas_call(
        body, name="swa_fwd", grid=(SWA_KV_HEADS, nsb), in_specs=[q, cur, prev, cur, prev, sink],
        out_specs=[q, q, lse_spec],
        out_shape=[jax.ShapeDtypeStruct((s_, SWA_HEADS * LANES), F32), jax.ShapeDtypeStruct((s_, SWA_HEADS * LANES), MXU_DTYPE),
                   jax.ShapeDtypeStruct((SWA_HEADS, 1, s_), F32)],
        scratch_shapes=[pltpu.VMEM((5 * BLOCK, LANES), MXU_DTYPE), pltpu.VMEM((5 * BLOCK, LANES), MXU_DTYPE)],
        compiler_params=_cparams(("parallel", "arbitrary")),
    )(qa, ka, ka, va, va, sink_b)


def _swa_bwd(qa, ka, va, sink_b, o32, do, lse):
    s_ = qa.shape[0]
    nsb = s_ // SWA_T
    scale = HEAD_DIM ** -0.5
    c2 = scale * LOG2E

    def body(q_ref, kc_ref, kp_ref, vc_ref, vp_ref, sk_ref, o_ref, do_ref, lse_ref,
             dq_ref, dk_ref, dv_ref, dsk_ref, kx, vx, kacc, vacc, kcar, vcar):
        j = pl.program_id(1)
        kx[0:BLOCK, :] = kp_ref[...]
        kx[BLOCK:5 * BLOCK, :] = kc_ref[...]
        vx[0:BLOCK, :] = vp_ref[...]
        vx[BLOCK:5 * BLOCK, :] = vc_ref[...]
        band, band0 = _swa_masks(nsb - 1 - j)
        kacc[...] = jnp.zeros(kacc.shape, F32)
        vacc[...] = jnp.zeros(vacc.shape, F32)

        @pl.when(j == 0)
        def _():
            kcar[...] = jnp.zeros(kcar.shape, F32)
            vcar[...] = jnp.zeros(vcar.shape, F32)
            dsk_ref[...] = jnp.zeros(dsk_ref.shape, F32)

        sink2 = _sink_row(sk_ref)
        dsink = jnp.zeros((1, SWA_W), F32)
        for b in range(4):
            rs = slice(b * BLOCK, (b + 1) * BLOCK)
            ks = slice(b * BLOCK, (b + 2) * BLOCK)
            q, k2, v2 = _heads_to_rows(q_ref, rs), kx[ks, :], vx[ks, :]
            d = _heads_to_rows(do_ref, rs)
            delta = jnp.sum((d * _heads_to_rows(o_ref, rs)).T, axis=0, keepdims=True)
            l2 = jnp.concatenate([lse_ref[hh, :, rs] for hh in range(SWA_GROUP)], axis=1)
            st = lax.dot_general(k2, q, NT, preferred_element_type=F32) * c2
            pt = jnp.exp2(jnp.where(band0 if b == 0 else band, st, -jnp.inf) - l2)
            db = d.astype(MXU_DTYPE)
            dst = (pt * (lax.dot_general(v2, db, NT, preferred_element_type=F32) - delta) * scale).astype(MXU_DTYPE)
            dq = lax.dot_general(dst, k2, TN, preferred_element_type=F32)
            for hh in range(SWA_GROUP):
                dq_ref[rs, hh * LANES:(hh + 1) * LANES] = dq[hh * LANES:(hh + 1) * LANES, :]
            kacc[ks, :] += jnp.dot(dst, q, preferred_element_type=F32)
            vacc[ks, :] += jnp.dot(pt.astype(MXU_DTYPE), db, preferred_element_type=F32)
            dsink = dsink - jnp.exp2(sink2 - l2) * delta
        for hh in range(SWA_GROUP):
            tot = jnp.sum(dsink[:, hh * LANES:(hh + 1) * LANES], axis=1, keepdims=True)
            dsk_ref[0, hh:hh + 1, :] += jnp.broadcast_to(tot, (1, LANES))

        dk_ref[0:3 * BLOCK, :] = kacc[BLOCK:4 * BLOCK, :]
        dk_ref[3 * BLOCK:4 * BLOCK, :] = kacc[4 * BLOCK:5 * BLOCK, :] + kcar[...]
        dv_ref[0:3 * BLOCK, :] = vacc[BLOCK:4 * BLOCK, :].astype(dv_ref.dtype)
        dv_ref[3 * BLOCK:4 * BLOCK, :] = (vacc[4 * BLOCK:5 * BLOCK, :] + vcar[...]).astype(dv_ref.dtype)
        kcar[...] = kacc[0:BLOCK, :]
        vcar[...] = vacc[0:BLOCK, :]

    q, cur, prev, sink, lse_spec = _swa_in_specs(True, nsb)
    return pl.pallas_call(
        body, name="swa_bwd", grid=(SWA_KV_HEADS, nsb),
        in_specs=[q, cur, prev, cur, prev, sink, q, q, lse_spec],
        out_specs=[q, cur, cur, sink],
        out_shape=[jax.ShapeDtypeStruct((s_, SWA_HEADS * LANES), F32), jax.ShapeDtypeStruct((s_, SWA_KV_HEADS * LANES), F32),
                   jax.ShapeDtypeStruct((s_, SWA_KV_HEADS * LANES), MXU_DTYPE),
                   jax.ShapeDtypeStruct((SWA_KV_HEADS, SUBLANES, LANES), F32)],
        scratch_shapes=[pltpu.VMEM((5 * BLOCK, LANES), MXU_DTYPE), pltpu.VMEM((5 * BLOCK, LANES), MXU_DTYPE),
                        pltpu.VMEM((5 * BLOCK, LANES), F32), pltpu.VMEM((5 * BLOCK, LANES), F32),
                        pltpu.VMEM((BLOCK, LANES), F32), pltpu.VMEM((BLOCK, LANES), F32)],
        compiler_params=_cparams(("arbitrary", "arbitrary")),
    )(qa, ka, ka, va, va, sink_b, o32, do, lse)


MLA_T = 512
MLA_FWD_GROUP = 4
MLA_BWD_GROUP = 2


def _mla_specs(s_, t, group):
    w = group * LANES
    qs = pl.BlockSpec((t, w), lambda g, i: (i, g))
    kv = pl.BlockSpec((s_, w), lambda g, i: (0, g))
    row = pl.BlockSpec((group, 1, t), lambda g, i: (g, 0, i))
    return qs, kv, row


def _causal_scores_t(k, q, t, c2, masked):
    st = lax.dot_general(k, q, NT, preferred_element_type=F32) * c2
    if masked:
        kr = lax.broadcasted_iota(jnp.int32, (t, t), 0)
        qc = lax.broadcasted_iota(jnp.int32, (t, t), 1)
        st = jnp.where(kr <= qc, st, -jnp.inf)
    return st


def _mla_fwd(qc, kc, vp):
    s_ = qc.shape[0]
    t = min(MLA_T, s_)
    c2 = MLA_QK ** -0.5 * LOG2E
    grp = MLA_FWD_GROUP

    def body(q_ref, k_ref, v_ref, o32_ref, o16_ref, lse_ref, m_s, acc_s):
        qi = pl.program_id(1)
        m_s[...] = jnp.full(m_s.shape, -jnp.inf, F32)
        acc_s[...] = jnp.zeros(acc_s.shape, F32)
        ones_lane = lax.broadcasted_iota(jnp.int32, (t, LANES), 1) == MLA_V

        def step(ki, masked):
            off = pl.multiple_of(ki * t, t)
            for g in range(grp):
                cs = slice(g * LANES, (g + 1) * LANES)
                st = _causal_scores_t(k_ref[pl.ds(off, t), cs], q_ref[:, cs], t, c2, masked)
                m_old = m_s[g]
                m_new = jnp.maximum(m_old, jnp.max(st, axis=0, keepdims=True))
                alpha = jnp.exp2(m_old - m_new)
                pt = jnp.exp2(st - m_new).astype(MXU_DTYPE)
                v = v_ref[pl.ds(off, t), cs]
                v = jnp.where(ones_lane, jnp.ones((), v.dtype), v)
                acc_s[g] = alpha * acc_s[g] + lax.dot_general(v, pt, TN, preferred_element_type=F32)
                m_s[g] = m_new

        def full_block(ki, carry):
            step(ki, False)
            return carry

        lax.fori_loop(0, qi, full_block, 0)
        step(qi, True)
        for g in range(grp):
            cs = slice(g * LANES, (g + 1) * LANES)
            acc = acc_s[g]
            l = acc[MLA_V:MLA_V + 1, :]
            o = (acc * (1.0 / l)).T
            o32_ref[:, cs] = o
            o16_ref[:, cs] = o.astype(o16_ref.dtype)
            lse_ref[g] = m_s[g] + jnp.log2(l)

    qs, kv, row = _mla_specs(s_, t, grp)
    return pl.pallas_call(
        body, name="mla_fwd", grid=(MLA_HEADS // grp, s_ // t), in_specs=[qs, kv, kv], out_specs=[qs, qs, row],
        out_shape=[jax.ShapeDtypeStruct((s_, MLA_HEADS * LANES), F32), jax.ShapeDtypeStruct((s_, MLA_HEADS * LANES), MXU_DTYPE),
                   jax.ShapeDtypeStruct((MLA_HEADS, 1, s_), F32)],
        scratch_shapes=[pltpu.VMEM((grp, 1, t), F32), pltpu.VMEM((grp, LANES, t), F32)],
        compiler_params=_cparams(("parallel", "arbitrary")),
    )(qc, kc, vp)


def _mla_bwd(qc, kc, vp, dob, lse, delta):
    s_ = qc.shape[0]
    t = min(MLA_T, s_)
    scale = MLA_QK ** -0.5
    c2 = scale * LOG2E
    grp = MLA_BWD_GROUP

    def body(q_ref, do_ref, lse_ref, dl_ref, k_ref, v_ref, dq_ref, dk_ref, dv_ref, dqt_s):
        qi = pl.program_id(1)

        @pl.when(qi == 0)
        def _():
            dk_ref[...] = jnp.zeros(dk_ref.shape, F32)
            dv_ref[...] = jnp.zeros(dv_ref.shape, F32)

        dqt_s[...] = jnp.zeros(dqt_s.shape, F32)

        def step(ki, masked):
            off = pl.multiple_of(ki * t, t)
            for g in range(grp):
                cs = slice(g * LANES, (g + 1) * LANES)
                q, d, k = q_ref[:, cs], do_ref[:, cs], k_ref[pl.ds(off, t), cs]
                pt = jnp.exp2(_causal_scores_t(k, q, t, c2, masked) - lse_ref[g])
                dpt = lax.dot_general(v_ref[pl.ds(off, t), cs], d, NT, preferred_element_type=F32)
                dst = (pt * (dpt - dl_ref[g]) * scale).astype(MXU_DTYPE)
                dv_ref[pl.ds(off, t), cs] += jnp.dot(pt.astype(MXU_DTYPE), d, preferred_element_type=F32)
                dk_ref[pl.ds(off, t), cs] += jnp.dot(dst, q, preferred_element_type=F32)
                dqt_s[g] += lax.dot_general(k, dst, TN, preferred_element_type=F32)

        def full_block(ki, carry):
            step(ki, False)
            return carry

        lax.fori_loop(0, qi, full_block, 0)
        step(qi, True)
        for g in range(grp):
            dq_ref[:, g * LANES:(g + 1) * LANES] = dqt_s[g].T

    qs, kv, row = _mla_specs(s_, t, grp)
    shp = jax.ShapeDtypeStruct((s_, MLA_HEADS * LANES), F32)
    return pl.pallas_call(
        body, name="mla_bwd", grid=(MLA_HEADS // grp, s_ // t), in_specs=[qs, qs, row, row, kv, kv],
        out_specs=[qs, kv, kv], out_shape=[shp, shp, shp], scratch_shapes=[pltpu.VMEM((grp, LANES, t), F32)],
        compiler_params=_cparams(("parallel", "arbitrary")),
    )(qc, dob, lse, delta, kc, vp)


def _pad_heads(w, nh, hd, axis):
    shp = w.shape
    w = w.reshape(shp[:axis] + (nh, hd) + shp[axis + 1:])
    pad = [(0, 0)] * w.ndim
    pad[axis + 1] = (0, LANES - hd)
    w = jnp.pad(w, pad)
    return w.reshape(shp[:axis] + (nh * LANES,) + shp[axis + 1:])


def _unpad_heads(w, nh, hd, axis):
    shp = w.shape
    w = w.reshape(shp[:axis] + (nh, LANES) + shp[axis + 1:])
    w = lax.slice_in_dim(w, 0, hd, axis=axis + 1)
    return w.reshape(shp[:axis] + (nh * hd,) + shp[axis + 1:])


PACK_W = 1024
ROW_TILE = 16
FULL_SHAPE = dict(w_in=(1024, 3488), w_uq=(384, 768), w_ukv=(256, 1024), w_o_swa=(512, 1024), w_o_mla=(512, 1024),
                  w_out=(1024, 1024), w_gate=(1024, 2816), w_up=(1024, 2816), w_down=(2816, 1024))
BIG = tuple(FULL_SHAPE)
ROW_SHARDED = ("w_out", "w_down")
W_IN_COLS = FULL_SHAPE["w_in"][1] // N_DEV
W_IN_ROWS = -(-W_IN_COLS // ROW_TILE) * ROW_TILE
FF_COLS = D_FF // N_DEV
OUT_ROWS = D_MODEL // N_DEV
SMALL_ROW0 = W_IN_ROWS + OUT_ROWS
SMALL_FLAT = (("w_uq", 0, 36), ("w_ukv", 48, 32), ("w_o_swa", 80, 64), ("w_o_mla", 144, 64))
SMALL_ROWS = 208
EARLY_ROWS = SMALL_ROW0 + SMALL_ROWS
LATE_ROWS = 3 * FF_COLS
PACK_ROWS = EARLY_ROWS + LATE_ROWS


def _shard_shape(n):
    r, c = FULL_SHAPE[n]
    return (r // N_DEV, c) if n in ROW_SHARDED else (r, c // N_DEV)


def _wire_pack(sh, dtype):
    c = lambda n: sh[n].astype(dtype)
    rows = [jnp.pad(c("w_in").T, ((0, W_IN_ROWS - W_IN_COLS), (0, 0))), c("w_out")]
    for n, _, r in SMALL_FLAT:
        rows.append(jnp.pad(c(n).reshape(r, PACK_W), ((0, -r % ROW_TILE), (0, 0))))
    return jnp.concatenate(rows + [c("w_gate").T, c("w_up").T, c("w_down")], 0)


MID_ROWS = OUT_ROWS + SMALL_ROWS


def _mid_unpack(p):
    out = dict(w_out=p[0:OUT_ROWS])
    for n, off, r in SMALL_FLAT:
        out[n] = p[OUT_ROWS + off:OUT_ROWS + off + r].reshape(_shard_shape(n))
    return out


def _w_in_row_maps():
    sp = lambda col: (col // W_IN_COLS) * W_IN_ROWS + col % W_IN_COLS
    fwd = np.full((P_W,), -1, np.int64)

    def put(t0, c0, n):
        fwd[t0:t0 + n] = [sp(c) for c in range(c0, c0 + n)]

    put(P_GA, IN_OFF[6], D_MODEL)
    put(P_GB, IN_OFF[7], D_MODEL)
    for h in range(SWA_HEADS):
        put(P_Q + LANES * h, IN_OFF[0] + HEAD_DIM * h, HEAD_DIM)
    put(P_QLAT, IN_OFF[3], Q_LORA)
    put(P_KR + KR_LANE, IN_OFF[5], MLA_ROPE)
    for h in range(SWA_KV_HEADS):
        put(P_K + LANES * h, IN_OFF[1] + HEAD_DIM * h, HEAD_DIM)
        put(P_V + LANES * h, IN_OFF[2] + HEAD_DIM * h, HEAD_DIM)
    put(P_KVLAT, IN_OFF[4], KV_LORA)
    inv = np.full((N_DEV * W_IN_ROWS,), -1, np.int64)
    inv[fwd[fwd >= 0]] = np.nonzero(fwd >= 0)[0]
    return fwd, inv


def _take_rows(src, idx, *, name, tile=2 * LANES):
    n_out, n_src, width = len(idx), src.shape[0], src.shape[1]
    assert n_out % tile == 0 and n_src % tile == 0
    n_tiles = n_out // tile
    blocks = [sorted({int(v) // tile for v in idx[i * tile:(i + 1) * tile] if v >= 0}) for i in range(n_tiles)]
    k_max = max(1, max(len(b) for b in blocks))
    tab = np.zeros((n_tiles, k_max), np.int32)
    sel = np.zeros((n_tiles, k_max, tile, tile), np.float32)
    for i, blks in enumerate(blocks):
        for m, b in enumerate(blks):
            tab[i, m] = b
            for r in range(tile):
                v = int(idx[i * tile + r])
                if v >= 0 and v // tile == b:
                    sel[i, m, r, v % tile] = 1.0

    def body(tab_ref, sel_ref, *refs):
        o_ref = refs[k_max]
        acc = jnp.dot(sel_ref[0, 0], refs[0][...], preferred_element_type=F32)
        for m in range(1, k_max):
            acc = acc + jnp.dot(sel_ref[0, m], refs[m][...], preferred_element_type=F32)
        o_ref[...] = acc.astype(o_ref.dtype)

    def src_spec(m):
        return pl.BlockSpec((tile, width), lambda i, t: (t[i * k_max + m], 0))

    return pl.pallas_call(
        body, name=name,
        grid_spec=pltpu.PrefetchScalarGridSpec(
            num_scalar_prefetch=1, grid=(n_tiles,),
            in_specs=[pl.BlockSpec((1, k_max, tile, tile), lambda i, t: (i, 0, 0, 0))] + [src_spec(m) for m in range(k_max)],
            out_specs=pl.BlockSpec((tile, width), lambda i, t: (i, 0))),
        out_shape=jax.ShapeDtypeStruct((n_out, width), src.dtype),
        compiler_params=_cparams(("parallel",)),
    )(jnp.asarray(tab.reshape(-1)), jnp.asarray(sel, src.dtype), *([src] * k_max))


def _w_in_operand(win_g):
    return _take_rows(win_g.reshape(N_DEV * W_IN_ROWS, PACK_W), _w_in_row_maps()[0], name="w_in_rows")


def _mid_operands(wout_g, small_g):
    def full(n, off, r):
        a = small_g[:, off:off + r].reshape((N_DEV,) + _shard_shape(n))
        return jnp.moveaxis(a, 0, 1).reshape(FULL_SHAPE[n])

    w = {n: full(n, off, r) for n, off, r in SMALL_FLAT}
    ukv = w["w_ukv"].reshape(KV_LORA, MLA_HEADS, MLA_NOPE + MLA_V)
    return dict(
        wout=wout_g.reshape(D_MODEL, D_MODEL),
        wuq=_pad_heads(w["w_uq"], MLA_HEADS, MLA_QK, 1),
        wuk=_pad_heads(ukv[:, :, :MLA_NOPE].reshape(KV_LORA, -1), MLA_HEADS, MLA_NOPE, 1),
        wuv=_pad_heads(ukv[:, :, MLA_NOPE:].reshape(KV_LORA, -1), MLA_HEADS, MLA_V, 1),
        woa=_pad_heads(w["w_o_swa"], SWA_HEADS, HEAD_DIM, 0),
        wob=_pad_heads(w["w_o_mla"], MLA_HEADS, MLA_V, 0),
    )


def _mid_grad_pack(g):
    uk = _unpad_heads(g["wukv"][:, :1024], MLA_HEADS, MLA_NOPE, 1).reshape(KV_LORA, MLA_HEADS, MLA_NOPE)
    uv = _unpad_heads(g["wukv"][:, 1024:], MLA_HEADS, MLA_V, 1).reshape(KV_LORA, MLA_HEADS, MLA_V)
    w = dict(w_uq=_unpad_heads(g["wuq"], MLA_HEADS, MLA_QK, 1), w_ukv=jnp.concatenate([uk, uv], 2).reshape(KV_LORA, -1),
             w_o_swa=_unpad_heads(g["woa"], SWA_HEADS, HEAD_DIM, 0), w_o_mla=_unpad_heads(g["wob"], MLA_HEADS, MLA_V, 0))

    def flat(n, r):
        rr, cc = FULL_SHAPE[n]
        a = jnp.moveaxis(w[n].reshape(rr, N_DEV, cc // N_DEV), 1, 0).reshape(N_DEV, r, PACK_W)
        return jnp.pad(a, ((0, 0), (0, -r % ROW_TILE), (0, 0))).astype(WIRE_DTYPE)

    return jnp.concatenate([g["wout"].reshape(N_DEV, OUT_ROWS, PACK_W)] + [flat(n, r) for n, _, r in SMALL_FLAT], 1)


def _w_in_grad_chunks(g_win_t):
    return _take_rows(g_win_t, _w_in_row_maps()[1], name="dw_in_rows").reshape(N_DEV, W_IN_ROWS, PACK_W)


def _local_step(x, tgt, win_t, small, weights, grads):
    s_ = x.shape[0]
    tabs = _rope_tables(s_)
    sink_b = jnp.broadcast_to(small["swa_sinks"].reshape(SWA_KV_HEADS, SWA_GROUP, 1), (SWA_KV_HEADS, SWA_GROUP, LANES))
    sink_b = jnp.pad(sink_b, ((0, 0), (0, SUBLANES - SWA_GROUP), (0, 0)))

    h, qa, ka, va, cq, ckv, kro, p = _proj_in(x, small["mix_norm_g"], win_t, small["q_norm_g"], small["kv_norm_g"], tabs)
    ops = weights.mid(cq)
    oa32, oa16, lse_a = _swa_fwd(qa, ka, va, sink_b)
    qc, kc, vp = _mla_up(cq, ckv, kro, ops["wuq"], ops["wuk"], ops["wuv"], tabs)
    ob32, ob16, lse_b = _mla_fwd(qc, kc, vp)
    ta, tb, y = _attn_out_gate(oa16, ob16, ops["woa"], ops["wob"], p)
    x1 = _mm(y, ops["wout"], "nn", name="out_proj", add=x, tm=1024, tn=1024)
    wgu_t, wd = weights.late(x1)
    h2, gu, act = _ffn_in_act(x1, small["ffn_norm_g"], wgu_t)

    dx2, dx2b, dg3, _, tot = _ffn_out_loss(act, wd, x1, small["final_norm_g"].reshape(1, D_MODEL), tgt)
    g = {}
    g_wd = _mm(act, dx2b, "tn", name="dw_down", tm=FF_TILE, tn=1024, tk=2048, out_dtype=WIRE_DTYPE)
    dgu = _d_act_swiglu(dx2b, wd, gu)
    g_wgu = _mm(dgu, h2, "tn", name="dw_ffn_in", tm=FF_TILE, tn=1024, tk=2048, out_dtype=WIRE_DTYPE)
    token = grads.late(g_wgu, g_wd)
    dx1, dx1b, dg2 = _mm_norm_bwd(dgu, wgu_t, x1, small["ffn_norm_g"] + token[0:1, 0:1], dx2, name="d_h2")
    g["wout"] = _mm(y, dx1b, "tn", name="dw_out", tm=1024, tn=1024, tk=1024, out_dtype=WIRE_DTYPE)
    dta, dtb, dgab = _d_y_gate(dx1b, ops["wout"], p, ta, tb)
    doa = _mm(dta, ops["woa"], "nt", name="d_oa", tm=1024, tn=1024)
    g["woa"] = _mm(oa16, dta, "tn", name="dw_o_swa", tm=1024, tn=1024, tk=1024)
    g["wob"] = _mm(ob16, dtb, "tn", name="dw_o_mla", tm=1024, tn=1024, tk=1024)
    dob16, delta_b = _mla_d_out(dtb, ops["wob"], ob32)
    dqc, dkc, dvp = _mla_bwd(qc, kc, vp, dob16, lse_b, delta_b)
    dqp, dkv, dkr, dqlat, dkvlat, dgq, dgkv = _mla_up_bwd(
        dqc, dkc, dvp, ops["wuq"], jnp.concatenate([ops["wuk"], ops["wuv"]], 1), p, small["q_norm_g"], small["kv_norm_g"], tabs)
    g["wuq"] = _mm(cq, dqp, "tn", name="dw_uq", tm=Q_LORA, tn=1024, tk=512)
    g["wukv"] = _mm(ckv, dkv, "tn", name="dw_ukv", tm=KV_LORA, tn=1024, tk=512)
    token = grads.mid(g)
    dqa, dka, dva, dsk = _swa_bwd(qa, ka, va, sink_b + token[0:1, 0:1], oa32, doa, lse_a)
    dp = _assemble_dp(dgab, dqa, dqlat, dkr, dka, dva, dkvlat, tabs)
    token = grads.last(_mm(dp, h, "tn", name="dw_in", tm=2176, tn=1024, tk=1024, out_dtype=WIRE_DTYPE))
    gx, _, dg1 = _mm_norm_bwd(dp, win_t, x, small["mix_norm_g"], dx1, name="d_h", after=token)

    sm = dict(mix_norm_g=dg1, ffn_norm_g=dg2, final_norm_g=dg3, q_norm_g=dgq, kv_norm_g=dgkv,
              swa_sinks=dsk[:, :SWA_GROUP, 0].reshape(1, SWA_HEADS))
    return tot, gx, sm


MESH = pl.DeviceIdType.MESH
ANY = pl.BlockSpec(memory_space=pl.ANY)


def _position():
    return lax.axis_index("x"), lax.axis_index("y"), lax.axis_index("c")


def _all_gather(block, pieces, shapes, *, name):
    n_out = len(shapes)
    n_rows = sum(p[3] for p in pieces)

    def body(x_ref, *refs):
        outs, (send_sems, recv_sems, local_sem) = refs[:n_out], refs[n_out:]
        x, y, c = _position()
        me, sibling = (x, y, c), (x, y, 1 - c)
        chips = [(1 - x, y), (x, 1 - y), (1 - x, 1 - y)]

        def dst(piece, blk):
            arr, lead, _, _ = piece
            return outs[arr].at[lead(4 * blk[0] + 2 * blk[1] + blk[2])]

        def own(piece):
            return x_ref.at[pl.ds(piece[2], piece[3])]

        def copies(k, blk, to, from_input):
            return [pltpu.make_async_remote_copy(
                src_ref=own(p) if from_input else dst(p, blk), dst_ref=dst(p, blk), send_sem=send_sems.at[k],
                recv_sem=recv_sems.at[k], device_id=to, device_id_type=MESH) for p in pieces]

        gathered_rows = x_ref.at[pl.ds(0, n_rows)]

        def whole_block(k):
            return pltpu.make_async_remote_copy(src_ref=gathered_rows, dst_ref=gathered_rows, send_sem=send_sems.at[k],
                                                recv_sem=recv_sems.at[k], device_id=me, device_id_type=MESH)

        for p in pieces:
            pltpu.make_async_copy(own(p), dst(p, me), local_sem).start()
        for cp in copies(0, me, sibling, True):
            cp.start()
        for j, chip in enumerate(chips):
            for cp in copies(1 + j, me, (*chip, c), True):
                cp.start()
        for j, chip in enumerate(chips):
            whole_block(1 + j).wait_recv()
            for cp in copies(4 + j, (*chip, c), sibling, False):
                cp.start()
        whole_block(0).wait_recv()
        for j in range(3):
            whole_block(4 + j).wait_recv()
        for k in range(7):
            whole_block(k).wait_send()
        pltpu.make_async_copy(gathered_rows, gathered_rows, local_sem).wait()

    return pl.pallas_call(
        body, name=name, out_shape=[jax.ShapeDtypeStruct(s, block.dtype) for s in shapes], in_specs=[ANY],
        out_specs=[ANY] * n_out,
        scratch_shapes=[pltpu.SemaphoreType.DMA((7,)), pltpu.SemaphoreType.DMA((7,)), pltpu.SemaphoreType.DMA],
    )(block)


HBM = pl.BlockSpec(memory_space=pltpu.HBM)
SEM = pl.BlockSpec(memory_space=pltpu.SEMAPHORE)
TILE_DEVS = FF_TILE // FF_COLS
GU_SHAPE = (2, 2, TILE_DEVS, FF_COLS, PACK_W)


def _gate_slab(d):
    return (d // TILE_DEVS, 0, d % TILE_DEVS)


def _up_slab(d):
    return (d // TILE_DEVS, 1, d % TILE_DEVS)
D_SHAPE = (N_DEV, FF_COLS, PACK_W)
LAND_SHAPE = (N_DEV, LATE_ROWS, PACK_W)


def _split_params():
    return pltpu.CompilerParams(has_side_effects=pltpu.SideEffectType.DATAFLOW_SIDE_EFFECTING)


def _peer(x, y, c, k):
    return ((1 - x) if k & 4 else x, (1 - y) if k & 2 else y, (1 - c) if k & 1 else c)


def _empty_hbm(shape, dtype):
    return pltpu.with_memory_space_constraint(lax.empty(shape, dtype), pltpu.HBM)


def _wait_all(rows, send_sems, recv_sems, me):
    for k in range(N_DEV - 1):
        cp = pltpu.make_async_remote_copy(src_ref=rows, dst_ref=rows, send_sem=send_sems.at[k], recv_sem=recv_sems.at[k],
                                          device_id=me, device_id_type=MESH)
        cp.wait_send()
        cp.wait_recv()


def _token_shape():
    return jax.ShapeDtypeStruct((SUBLANES, LANES), F32)


def _gather_start(pack, row0, pieces, shapes, *, name):
    n = len(shapes)

    def body(*refs):
        p_ref, bufs, send_sems, recv_sems, token = refs[0], refs[1:1 + n], refs[1 + n], refs[2 + n], refs[-1]
        x, y, c = _position()
        me = 4 * x + 2 * y + c
        for k in range(1, N_DEV):
            off = row0
            for buf, lead, rows in pieces:
                pltpu.make_async_remote_copy(
                    src_ref=p_ref.at[pl.ds(off, rows)], dst_ref=bufs[buf].at[lead(me)], send_sem=send_sems.at[k - 1],
                    recv_sem=recv_sems.at[k - 1], device_id=_peer(x, y, c, k), device_id_type=MESH).start()
                off += rows
        token[...] = jnp.zeros_like(token)

    sems, dt = pltpu.SemaphoreType.DMA((N_DEV - 1,)), pack.dtype
    return pl.pallas_call(
        body, name=name,
        out_shape=(sems, sems, pltpu.HBM(pack.shape, dt)) + tuple(pltpu.HBM(s, dt) for s in shapes) + (_token_shape(),),
        in_specs=(HBM,) * (1 + n), out_specs=(SEM, SEM) + (HBM,) * (1 + n) + (pl.BlockSpec(memory_space=pltpu.VMEM),),
        input_output_aliases={i: 2 + i for i in range(1 + n)}, compiler_params=_split_params(),
    )(pltpu.with_memory_space_constraint(pack, pltpu.HBM), *[_empty_hbm(s, dt) for s in shapes])


def _gather_wait(started, row0, n_rows, after, *, name):
    send_sems, recv_sems, pack, *bufs = started[:-1]
    n = len(bufs)

    def body(*refs):
        _wait_all(refs[0].at[pl.ds(row0, n_rows)], refs[1 + n], refs[2 + n], _position())

    outs = pl.pallas_call(
        body, name=name, out_shape=tuple(pltpu.HBM(a.shape, a.dtype) for a in (pack, *bufs)),
        in_specs=(HBM,) * (1 + n) + (SEM, SEM, ANY), out_specs=(HBM,) * (1 + n),
        input_output_aliases={i: i for i in range(1 + n)}, compiler_params=_split_params(),
    )(pack, *bufs, send_sems, recv_sems, after)
    return outs[0], outs[1:]


def _scatter_start(srcs, pieces, *, name):
    n = len(srcs)
    land_shape = (N_DEV, sum(p[2] for p in pieces), PACK_W)

    def body(*refs):
        src_refs, land_ref, send_sems, recv_sems, token = refs[:n], refs[n], refs[n + 1], refs[n + 2], refs[-1]
        x, y, c = _position()
        me = 4 * x + 2 * y + c
        for k in range(1, N_DEV):
            px, py, pc = _peer(x, y, c, k)
            off = 0
            for si, lead, rows in pieces:
                pltpu.make_async_remote_copy(
                    src_ref=src_refs[si].at[lead(4 * px + 2 * py + pc)], dst_ref=land_ref.at[me, pl.ds(off, rows)],
                    send_sem=send_sems.at[k - 1], recv_sem=recv_sems.at[k - 1], device_id=(px, py, pc),
                    device_id_type=MESH).start()
                off += rows
        token[...] = jnp.zeros_like(token)

    sems, dt = pltpu.SemaphoreType.DMA((N_DEV - 1,)), srcs[0].dtype
    return pl.pallas_call(
        body, name=name,
        out_shape=(sems, sems) + tuple(pltpu.HBM(a.shape, dt) for a in srcs) + (pltpu.HBM(land_shape, dt), _token_shape()),
        in_specs=(HBM,) * (n + 1), out_specs=(SEM, SEM) + (HBM,) * (n + 1) + (pl.BlockSpec(memory_space=pltpu.VMEM),),
        input_output_aliases={i: 2 + i for i in range(n + 1)}, compiler_params=_split_params(),
    )(*[pltpu.with_memory_space_constraint(a, pltpu.HBM) for a in srcs], _empty_hbm(land_shape, dt))


def _scatter_wait(started, after, *, name):
    send_sems, recv_sems, *bufs = started[:-1]
    n = len(bufs)

    def body(*refs):
        _wait_all(refs[n - 1].at[0], refs[n], refs[n + 1], _position())

    return pl.pallas_call(
        body, name=name, out_shape=tuple(pltpu.HBM(a.shape, a.dtype) for a in bufs),
        in_specs=(HBM,) * n + (SEM, SEM, ANY), out_specs=(HBM,) * n, input_output_aliases={i: i for i in range(n)},
        compiler_params=_split_params(),
    )(*bufs, send_sems, recv_sems, after)


def _peer_sum(own, own_lead, land, block, rows, idx, *, name):
    lead_rank = own.ndim - 2

    def body(idx_ref, own_ref, *refs):
        o_ref = refs[N_DEV - 1]
        acc = own_ref[(0,) * lead_rank].astype(F32)
        for k in range(N_DEV - 1):
            acc = acc + refs[k][0].astype(F32)
        o_ref[...] = acc

    own_spec = pl.BlockSpec((1,) * lead_rank + (rows, PACK_W), lambda i, t: own_lead(t[0]) + (0, 0))

    def land_spec(k):
        return pl.BlockSpec((1, rows, PACK_W), lambda i, t: (t[k + 1], block, 0))

    return pl.pallas_call(
        body, name=name,
        grid_spec=pltpu.PrefetchScalarGridSpec(
            num_scalar_prefetch=1, grid=(1,), in_specs=[own_spec] + [land_spec(k) for k in range(N_DEV - 1)],
            out_specs=pl.BlockSpec((rows, PACK_W), lambda i, t: (0, 0))),
        out_shape=jax.ShapeDtypeStruct((rows, PACK_W), F32), compiler_params=_cparams(("arbitrary",)),
    )(idx, own, *([land] * (N_DEV - 1)))


def _adamw(w, g, m, v):
    m = ADAM_B1 * m + (1.0 - ADAM_B1) * g
    v = ADAM_B2 * v + (1.0 - ADAM_B2) * (g * g)
    m_hat = m / (1.0 - ADAM_B1 ** ADAM_STEP)
    v_hat = v / (1.0 - ADAM_B2 ** ADAM_STEP)
    delta = -ADAM_LR * (m_hat / (jnp.sqrt(v_hat) + ADAM_EPS) + ADAM_WD * w)
    return delta, m, v


def _adamw_call(w, g, m, v, *, name, max_rows=256):
    _, r, c_ = w.shape
    tr = max_rows if r > max_rows and r % max_rows == 0 else r

    def body(w_ref, g_ref, m_ref, v_ref, d_ref, mo_ref, vo_ref):
        d, mn, vn = _adamw(w_ref[0], g_ref[...], m_ref[0], v_ref[0])
        d_ref[0] = d
        mo_ref[0] = mn
        vo_ref[0] = vn

    row3 = pl.BlockSpec((1, tr, c_), lambda i: (0, i, 0))
    shp = jax.ShapeDtypeStruct((1, r, c_), F32)
    return pl.pallas_call(
        body, name=name, grid=(r // tr,), in_specs=[row3, pl.BlockSpec((tr, c_), lambda i: (i, 0)), row3, row3],
        out_specs=[row3] * 3, out_shape=[shp] * 3, compiler_params=_cparams(("parallel",)),
    )(w, g, m, v)


SMALL = ("mix_norm_g", "ffn_norm_g", "final_norm_g", "q_norm_g", "kv_norm_g", "swa_sinks")
SMALL_W = dict(mix_norm_g=1024, ffn_norm_g=1024, final_norm_g=1024, q_norm_g=Q_LORA, kv_norm_g=KV_LORA, swa_sinks=SWA_HEADS)


def _small_adamw(parts, w, m, v):
    ns = len(SMALL)

    def body(p_ref, *refs):
        ins, outs = refs[:3 * ns], refs[3 * ns:]
        tot = p_ref[0]
        for dev in range(1, N_DEV):
            tot = tot + p_ref[dev]
        for k, n in enumerate(SMALL):
            g = jnp.sum(tot[k * SUBLANES:(k + 1) * SUBLANES, :SMALL_W[n]], axis=0, keepdims=True)
            res = _adamw(ins[k][...], g, ins[ns + k][...], ins[2 * ns + k][...])
            for j, r in enumerate((g,) + tuple(res)):
                outs[j * ns + k][...] = r
        outs[4 * ns][...] = jnp.sum(tot[ns * SUBLANES:(ns + 1) * SUBLANES, 0:1], axis=0, keepdims=True)

    shapes = [jax.ShapeDtypeStruct((1, SMALL_W[n]), F32) for n in SMALL]
    vm = pl.BlockSpec(memory_space=pltpu.VMEM)
    out = pl.pallas_call(
        body, name="small_adamw", in_specs=[vm] * (1 + 3 * ns), out_specs=[vm] * (4 * ns + 1),
        out_shape=shapes * 4 + [jax.ShapeDtypeStruct((1, 1), F32)],
    )(parts, *[d[n] for d in (w, m, v) for n in SMALL])
    return [dict(zip(SMALL, out[j * ns:(j + 1) * ns])) for j in range(4)] + [out[4 * ns]]


def _small_pack(d, rows_each):
    parts = [jnp.pad(d[n].astype(F32), ((0, 0), (0, PACK_W - SMALL_W[n]))) for n in SMALL]
    out = jnp.concatenate(parts, 0)
    pad = -out.shape[0] % SUBLANES
    return jnp.pad(out, ((0, pad), (0, 0)))


def kernel(x, mix_norm_g, w_in, swa_sinks, q_norm_g, w_uq, kv_norm_g, w_ukv, w_o_swa, w_o_mla, w_out, ffn_norm_g, w_gate, w_up, w_down, final_norm_g, loss_target, m_mix_norm_g, m_w_in, m_swa_sinks, m_q_norm_g, m_w_uq, m_kv_norm_g, m_w_ukv, m_w_o_swa, m_w_o_mla, m_w_out, m_ffn_norm_g, m_w_gate, m_w_up, m_w_down, m_final_norm_g, v_mix_norm_g, v_w_in, v_swa_sinks, v_q_norm_g, v_w_uq, v_kv_norm_g, v_w_ukv, v_w_o_swa, v_w_o_mla, v_w_out, v_ffn_norm_g, v_w_gate, v_w_up, v_w_down, v_final_norm_g):
    big_w = dict(w_in=w_in[0], w_uq=w_uq[0], w_ukv=w_ukv[0], w_o_swa=w_o_swa[0], w_o_mla=w_o_mla[0], w_out=w_out[0],
                 w_gate=w_gate[0], w_up=w_up[0], w_down=w_down[0])
    big_w3 = dict(w_in=w_in, w_uq=w_uq, w_ukv=w_ukv, w_o_swa=w_o_swa, w_o_mla=w_o_mla, w_out=w_out, w_gate=w_gate, w_up=w_up,
                  w_down=w_down)
    big_m = dict(w_in=m_w_in, w_uq=m_w_uq, w_ukv=m_w_ukv, w_o_swa=m_w_o_swa, w_o_mla=m_w_o_mla, w_out=m_w_out,
                 w_gate=m_w_gate, w_up=m_w_up, w_down=m_w_down)
    big_v = dict(w_in=v_w_in, w_uq=v_w_uq, w_ukv=v_w_ukv, w_o_swa=v_w_o_swa, w_o_mla=v_w_o_mla, w_out=v_w_out,
                 w_gate=v_w_gate, w_up=v_w_up, w_down=v_w_down)
    small_w = dict(mix_norm_g=mix_norm_g, ffn_norm_g=ffn_norm_g, final_norm_g=final_norm_g.reshape(1, D_MODEL),
                   q_norm_g=q_norm_g, kv_norm_g=kv_norm_g, swa_sinks=swa_sinks)
    small_m = dict(mix_norm_g=m_mix_norm_g, ffn_norm_g=m_ffn_norm_g, final_norm_g=m_final_norm_g.reshape(1, D_MODEL),
                   q_norm_g=m_q_norm_g, kv_norm_g=m_kv_norm_g, swa_sinks=m_swa_sinks)
    small_v = dict(mix_norm_g=v_mix_norm_g, ffn_norm_g=v_ffn_norm_g, final_norm_g=v_final_norm_g.reshape(1, D_MODEL),
                   q_norm_g=v_q_norm_g, kv_norm_g=v_kv_norm_g, swa_sinks=v_swa_sinks)

    px, py, pc = _position()
    me = 4 * px + 2 * py + pc
    idx = jnp.stack([me] + [4 * qx + 2 * qy + qc for qx, qy, qc in (_peer(px, py, pc, k) for k in range(1, N_DEV))])
    idx = idx.astype(jnp.int32)

    dev = lambda d: (d,)
    pack = _wire_pack(big_w, WIRE_DTYPE)
    win_g, = _all_gather(pack, ((0, dev, 0, W_IN_ROWS),), ((N_DEV, W_IN_ROWS, PACK_W),), name="ag_early")
    ag_mid = _gather_start(pack, W_IN_ROWS, ((0, dev, OUT_ROWS), (1, dev, SMALL_ROWS)),
                           ((N_DEV, OUT_ROWS, PACK_W), (N_DEV, SMALL_ROWS, PACK_W)), name="ag_mid_start")
    ag = {}

    def own_rows(r0, r1, shape):
        return pack[r0:r1].reshape(shape)

    def mid_weights(after):
        pack_mid, (wout_g, small_g) = _gather_wait(ag_mid, W_IN_ROWS, MID_ROWS, after, name="ag_mid_wait")
        ag["late"] = _gather_start(pack_mid, EARLY_ROWS, ((0, _gate_slab, FF_COLS), (0, _up_slab, FF_COLS), (1, dev, FF_COLS)),
                                   (GU_SHAPE, D_SHAPE), name="ag_late_start")
        wout_g = lax.dynamic_update_slice(wout_g, own_rows(W_IN_ROWS, SMALL_ROW0, (1, OUT_ROWS, PACK_W)), (me, 0, 0))
        small_g = lax.dynamic_update_slice(small_g, own_rows(SMALL_ROW0, EARLY_ROWS, (1, SMALL_ROWS, PACK_W)), (me, 0, 0))
        ops = _mid_operands(wout_g, small_g)
        ops["wuq"] = ops["wuq"] + ag["late"][-1][0:1, 0:1].astype(ops["wuq"].dtype)
        return ops

    def late_weights(after):
        _, (gu, d) = _gather_wait(ag["late"], EARLY_ROWS, LATE_ROWS, after, name="ag_late_wait")
        slab = (1, 1, 1, FF_COLS, PACK_W)
        gu = lax.dynamic_update_slice(gu, own_rows(EARLY_ROWS, EARLY_ROWS + FF_COLS, slab), _gate_slab(me) + (0, 0))
        gu = lax.dynamic_update_slice(gu, own_rows(EARLY_ROWS + FF_COLS, EARLY_ROWS + 2 * FF_COLS, slab), _up_slab(me) + (0, 0))
        d = lax.dynamic_update_slice(d, own_rows(EARLY_ROWS + 2 * FF_COLS, PACK_ROWS, (1, FF_COLS, PACK_W)), (me, 0, 0))
        return gu.reshape(2 * D_FF, D_MODEL), d.reshape(D_FF, D_MODEL)

    rs = {}

    def late_grads(g_gu, g_d):
        rs["late"] = _scatter_start([g_gu.reshape(GU_SHAPE), g_d.reshape(D_SHAPE)],
                                    ((0, _gate_slab, FF_COLS), (0, _up_slab, FF_COLS), (1, dev, FF_COLS)),
                                    name="rs_late_start")
        return rs["late"][-1]

    def mid_grads(g):
        rs["mid"] = _scatter_start([_mid_grad_pack(g)], ((0, dev, MID_ROWS),), name="rs_mid_start")
        return rs["mid"][-1]

    def last_grads(g_win_t):
        rs["last"] = _scatter_start([_w_in_grad_chunks(g_win_t)], ((0, dev, W_IN_ROWS),), name="rs_last_start")
        return rs["last"][-1]

    first_w = dict(small_w, mix_norm_g=mix_norm_g + ag_mid[-1][0:1, 0:1])
    loss_tot, gx, g_small = _local_step(
        x[0], loss_target[0], _w_in_operand(win_g), first_w, types.SimpleNamespace(mid=mid_weights, late=late_weights),
        types.SimpleNamespace(late=late_grads, mid=mid_grads, last=last_grads))

    g_gu, g_d, land_late = _scatter_wait(rs["late"], gx, name="rs_late_wait")
    g_mid, land_mid = _scatter_wait(rs["mid"], gx, name="rs_mid_wait")
    g_win, land_last = _scatter_wait(rs["last"], gx, name="rs_last_wait")
    gw_t = dict(w_gate=_peer_sum(g_gu, _gate_slab, land_late, 0, FF_COLS, idx, name="rs_sum_gate"),
                w_up=_peer_sum(g_gu, _up_slab, land_late, 1, FF_COLS, idx, name="rs_sum_up"),
                w_in=_peer_sum(g_win, dev, land_last, 0, W_IN_ROWS, idx, name="rs_sum_in")[0:W_IN_COLS])
    gw = dict(w_down=_peer_sum(g_d, dev, land_late, 2, FF_COLS, idx, name="rs_sum_down"))
    gw.update(_mid_unpack(_peer_sum(g_mid, dev, land_mid, 0, MID_ROWS, idx, name="rs_sum_mid")))
    dw, mw, vw = {}, {}, {}
    swap = lambda a: jnp.swapaxes(a, 1, 2)
    for n in BIG:
        if n in gw_t:
            res = _adamw_call(swap(big_w3[n]), gw_t[n], swap(big_m[n]), swap(big_v[n]), name="adamw_" + n)
            dw[n], mw[n], vw[n] = (swap(r) for r in res)
        else:
            dw[n], mw[n], vw[n] = _adamw_call(big_w3[n], gw[n], big_m[n], big_v[n], name="adamw_" + n)
    gw = {n: g[None] for n, g in gw.items()}
    gw.update({n: swap(g[None]) for n, g in gw_t.items()})

    loss_rows = jnp.pad(loss_tot[0:1, 0:1], ((0, SUBLANES - 1), (0, PACK_W - 1)))
    small_rows = jnp.concatenate([_small_pack(g_small_rows(g_small), SUBLANES), loss_rows], 0)
    parts, = _all_gather(small_rows, ((0, lambda d: (d,), 0, small_rows.shape[0]),), ((N_DEV,) + small_rows.shape,),
                         name="ag_small")
    gs, ds, ms, vs, loss = _small_adamw(parts, small_w, small_m, small_v)
    loss = loss[0, 0]
    for d in (gs, ds, ms, vs):
        d["final_norm_g"] = d["final_norm_g"].reshape(D_MODEL)

    order = ("mix_norm_g", "w_in", "swa_sinks", "q_norm_g", "w_uq", "kv_norm_g", "w_ukv", "w_o_swa", "w_o_mla", "w_out",
             "ffn_norm_g", "w_gate", "w_up", "w_down", "final_norm_g")

    def leaves(big, small):
        return [big[n] if n in big else small[n] for n in order]

    return (loss, gx[None], *leaves(gw, gs), *leaves(dw, ds), *leaves(mw, ms), *leaves(vw, vs))


def g_small_rows(g_small):
    out = dict(g_small)
    out["swa_sinks"] = jnp.pad(g_small["swa_sinks"], ((0, SUBLANES - 1), (0, 0)))
    return out
```

```python
import types

import numpy as np
import jax
import jax.numpy as jnp
from jax import lax
from jax.experimental import pallas as pl
from jax.experimental.pallas import tpu as pltpu

F32 = jnp.float32
MXU_DTYPE = jnp.bfloat16
WIRE_DTYPE = jnp.bfloat16

D_MODEL = 1024
EPS = 1e-6
ROPE_THETA = 10000.0
BLOCK = 128
HEAD_DIM = 64
SWA_HEADS = 8
SWA_KV_HEADS = 2
SWA_GROUP = SWA_HEADS // SWA_KV_HEADS
MLA_HEADS = 8
MLA_NOPE = 64
MLA_ROPE = 32
MLA_V = 64
MLA_QK = MLA_NOPE + MLA_ROPE
Q_LORA = 384
KV_LORA = 256
D_FF = 2816
IN_SIZES = (512, 128, 128, Q_LORA, KV_LORA, MLA_ROPE, D_MODEL, D_MODEL)
IN_OFF = tuple(int(v) for v in np.cumsum((0,) + IN_SIZES))
ADAM_LR, ADAM_B1, ADAM_B2, ADAM_EPS, ADAM_WD, ADAM_STEP = 0.001, 0.9, 0.999, 1e-08, 0.01, 10

LANES = 128
SUBLANES = 8
VMEM_LIMIT = 48 * 1024 * 1024
N_DEV = 8
AXES = ("x", "y", "c")

P_GA, P_GB, P_Q, P_QLAT, P_KR, P_K, P_V, P_KVLAT, P_W = 0, 1024, 2048, 3072, 3456, 3584, 3840, 4096, 4352
KR_LANE = 64

LOG2E = 1.4426950408889634

NT = (((1,), (1,)), ((), ()))
NN = (((1,), (0,)), ((), ()))
TN = (((0,), (0,)), ((), ()))


def _cparams(sem):
    return pltpu.CompilerParams(dimension_semantics=sem, vmem_limit_bytes=VMEM_LIMIT)


def _mm(a, b, mode, *, name, out_dtype=F32, add=None, after=None, tm=512, tn=512, tk=None):
    if mode == "nn":
        (M, K), (K2, N) = a.shape, b.shape
    elif mode == "nt":
        (M, K), (N, K2) = a.shape, b.shape
    else:
        (K, M), (K2, N) = a.shape, b.shape
    assert K == K2, (a.shape, b.shape, mode)
    tm, tn, tk = min(tm, M), min(tn, N), K if tk is None else min(tk, K)
    assert M % tm == 0 and N % tn == 0 and K % tk == 0, (M, N, K, tm, tn, tk)
    nk = K // tk
    dn = {"nn": NN, "nt": NT, "tn": TN}[mode]
    if mode == "tn":
        a_spec = pl.BlockSpec((tk, tm), lambda i, j, k: (k, i))
    else:
        a_spec = pl.BlockSpec((tm, tk), lambda i, j, k: (i, k))
    once = dict(pipeline_mode=pl.Buffered(1)) if (nk == 1 and tn == N) else {}
    if mode == "nt":
        b_spec = pl.BlockSpec((tn, tk), lambda i, j, k: (j, k), **once)
    else:
        b_spec = pl.BlockSpec((tk, tn), lambda i, j, k: (k, j), **once)
    o_spec = pl.BlockSpec((tm, tn), lambda i, j, k: (i, j))
    has_add, has_after = add is not None, after is not None

    def body(*refs):
        a_ref, b_ref = refs[0], refs[1]
        add_ref = refs[2] if has_add else None
        o_ref = refs[2 + has_add + has_after]
        p = lax.dot_general(a_ref[...], b_ref[...], dn, preferred_element_type=F32)

        def finish(acc):
            if has_add:
                acc = acc + add_ref[...]
            o_ref[...] = acc.astype(o_ref.dtype)

        if nk == 1:
            finish(p)
        else:
            acc_ref = refs[-1]
            k = pl.program_id(2)

            @pl.when(k == 0)
            def _():
                acc_ref[...] = p

            @pl.when((k > 0) & (k < nk - 1))
            def _():
                acc_ref[...] += p

            @pl.when(k == nk - 1)
            def _():
                finish(acc_ref[...] + p)

    ins = [a, b] + ([add] if has_add else []) + ([after] if has_after else [])
    in_specs = [a_spec, b_spec] + ([o_spec] if has_add else []) + ([pl.BlockSpec(memory_space=pl.ANY)] if has_after else [])
    return pl.pallas_call(
        body, name=name, grid=(M // tm, N // tn, nk), in_specs=in_specs, out_specs=o_spec,
        out_shape=jax.ShapeDtypeStruct((M, N), out_dtype),
        scratch_shapes=[pltpu.VMEM((tm, tn), F32)] if nk > 1 else [],
        compiler_params=_cparams(("parallel", "parallel", "arbitrary")),
    )(*ins)


def _rows(ts, w, cb=0):
    return pl.BlockSpec((ts, w), lambda i: (i, cb))


def _const(r, w):
    return pl.BlockSpec((r, w), lambda i: (0, 0))


def _sublane_sum(v):
    ts, c = v.shape
    return jnp.sum(v.reshape(ts // SUBLANES, SUBLANES, c), axis=0)


def _sigmoid(v):
    return 1.0 / (1.0 + jnp.exp(-v))


def _rope(v, cos, s_up, s_dn, up, dn):
    return v * cos + pltpu.roll(v, up, 1) * s_up + pltpu.roll(v, dn, 1) * s_dn


def _rope_t(dv, cos, s_up, s_dn, up, dn):
    return dv * cos + pltpu.roll(dv * s_up, dn, 1) + pltpu.roll(dv * s_dn, up, 1)


def _rope_tables(seq):
    pos = np.arange(seq, dtype=np.float32)[:, None]

    def base(dim):
        inv = np.float32(ROPE_THETA) ** (-np.arange(0, dim, 2, dtype=np.float32) / np.float32(dim))
        ang = (pos * inv.astype(np.float32)[None, :]).astype(np.float32)
        return np.cos(ang).astype(np.float32), np.sin(ang).astype(np.float32)

    z = lambda n: np.zeros((seq, n), np.float32)
    ca, sa = base(HEAD_DIM)
    a_cos = np.concatenate([ca, ca, z(64)], 1)
    a_up = np.concatenate([-sa, z(96)], 1)
    a_dn = np.concatenate([z(32), sa, z(64)], 1)
    cb, sb = base(MLA_ROPE)
    one = np.ones((seq, 64), np.float32)
    q_cos = np.concatenate([one, cb, cb, z(32)], 1)
    k_cos = np.concatenate([z(64), cb, cb, z(32)], 1)
    b_up = np.concatenate([z(64), -sb, z(48)], 1)
    b_dn = np.concatenate([z(80), sb, z(32)], 1)
    return tuple(jnp.asarray(t) for t in (a_cos, a_up, a_dn, q_cos, k_cos, b_up, b_dn))


def _rms(v, g):
    return v * lax.rsqrt(jnp.mean(v * v, axis=-1, keepdims=True) + EPS) * g


def _rms_bwd(v, g, d):
    r = lax.rsqrt(jnp.mean(v * v, axis=-1, keepdims=True) + EPS)
    xh = v * r
    dxh = d * g
    return r * (dxh - xh * jnp.mean(dxh * xh, axis=-1, keepdims=True)), d * xh


F_GA, F_GB, F_KVLAT, F_QLAT, F_W = 0, 1024, 2048, 2304, 2688


def _proj_in(x, g, w_t, gq, gkv, tabs, *, tm=512):
    s_, c = x.shape
    a_cos, a_up, a_dn, _, k_cos, b_up, b_dn = tabs

    def body(x_ref, g_ref, w_ref, gq_ref, gkv_ref, ac, au, ad, kc, bu, bd,
             h_ref, qa_ref, ka_ref, va_ref, cq_ref, ckv_ref, kro_ref, pf_ref):
        h = _rms(x_ref[...], g_ref[...]).astype(h_ref.dtype)
        h_ref[...] = h
        mm = lambda a, b: lax.dot_general(h, w_ref[a:b, :], NT, preferred_element_type=F32)
        pf_ref[:, F_GA:F_KVLAT] = mm(P_GA, P_Q)
        c_, u_, d_ = ac[...], au[...], ad[...]
        q = mm(P_Q, P_QLAT)
        for hd in range(SWA_HEADS):
            sl = slice(hd * LANES, (hd + 1) * LANES)
            qa_ref[:, sl] = _rope(q[:, sl], c_, u_, d_, 96, 32).astype(qa_ref.dtype)
        kv = mm(P_KR, P_KVLAT)
        kro_ref[...] = _rope(kv[:, :LANES], kc[...], bu[...], bd[...], 112, 16)
        for hd in range(SWA_KV_HEADS):
            sl = slice((1 + hd) * LANES, (2 + hd) * LANES)
            ka_ref[:, hd * LANES:(hd + 1) * LANES] = _rope(kv[:, sl], c_, u_, d_, 96, 32).astype(ka_ref.dtype)
        va_ref[...] = kv[:, P_V - P_KR:].astype(va_ref.dtype)
        for a, b, f0, gref, dst in ((P_QLAT, P_KR, F_QLAT, gq_ref, cq_ref), (P_KVLAT, P_W, F_KVLAT, gkv_ref, ckv_ref)):
            v = mm(a, b)
            pf_ref[:, f0:f0 + b - a] = v
            r = lax.rsqrt(jnp.mean(v * v, axis=-1, keepdims=True) + EPS)
            dst[...] = (v * r * gref[...]).astype(dst.dtype)

    tab = _rows(tm, LANES)
    widths = (c, SWA_HEADS * LANES, SWA_KV_HEADS * LANES, SWA_KV_HEADS * LANES, Q_LORA, KV_LORA)
    return pl.pallas_call(
        body, name="proj_in", grid=(s_ // tm,),
        in_specs=[_rows(tm, c), _const(1, c), pl.BlockSpec((P_W, c), lambda i: (0, 0), pipeline_mode=pl.Buffered(1)),
                  _const(1, Q_LORA), _const(1, KV_LORA), tab, tab, tab, tab, tab, tab],
        out_specs=[_rows(tm, w) for w in widths] + [tab, _rows(tm, F_W)],
        out_shape=[jax.ShapeDtypeStruct((s_, w), MXU_DTYPE) for w in widths]
        + [jax.ShapeDtypeStruct((s_, LANES), F32), jax.ShapeDtypeStruct((s_, F_W), F32)],
        compiler_params=_cparams(("parallel",)),
    )(x, g, w_t, gq, gkv, a_cos, a_up, a_dn, k_cos, b_up, b_dn)


def _mm_norm_bwd(a, b, x, g, res, *, name, after=None, tm=512):
    s_, kk = a.shape
    c = b.shape[1]
    has_after = after is not None

    def body(*refs):
        a_ref, b_ref, x_ref, g_ref, res_ref = refs[:5]
        dx_ref, dxb_ref, dg_ref = refs[5 + has_after:]
        d = jnp.dot(a_ref[...], b_ref[...], preferred_element_type=F32)
        dx, gg = _rms_bwd(x_ref[...], g_ref[...], d)
        dx = dx + res_ref[...]
        dx_ref[...] = dx
        dxb_ref[...] = dx.astype(dxb_ref.dtype)

        @pl.when(pl.program_id(0) == 0)
        def _():
            dg_ref[...] = jnp.zeros(dg_ref.shape, F32)

        dg_ref[...] += _sublane_sum(gg)

    row = _rows(tm, c)
    in_specs = [_rows(tm, kk), pl.BlockSpec((kk, c), lambda i: (0, 0), pipeline_mode=pl.Buffered(1)), row, _const(1, c), row]
    return pl.pallas_call(
        body, name=name, grid=(s_ // tm,), in_specs=in_specs + ([pl.BlockSpec(memory_space=pl.ANY)] if has_after else []),
        out_specs=[row, row, _const(SUBLANES, c)],
        out_shape=[jax.ShapeDtypeStruct((s_, c), F32), jax.ShapeDtypeStruct((s_, c), MXU_DTYPE),
                   jax.ShapeDtypeStruct((SUBLANES, c), F32)],
        compiler_params=_cparams(("arbitrary",)),
    )(*([a, b, x, g, res] + ([after] if has_after else [])))


def _mla_up(cq, ckv, kro, wuq, wuk, wuv, tabs, *, ts=512):
    s_ = cq.shape[0]
    _, _, _, q_cos, _, b_up, b_dn = tabs

    def body(cq_ref, ckv_ref, kr_ref, wq_ref, wk_ref, wv_ref, qc, bu, bd, qo_ref, ko_ref, vo_ref):
        c_, u_, d_ = qc[...], bu[...], bd[...]
        kr = kr_ref[...]
        ckv_ = ckv_ref[...]
        vo_ref[...] = jnp.dot(ckv_, wv_ref[...], preferred_element_type=F32).astype(vo_ref.dtype)
        q = jnp.dot(cq_ref[...], wq_ref[...], preferred_element_type=F32)
        k = jnp.dot(ckv_, wk_ref[...], preferred_element_type=F32)
        for h in range(MLA_HEADS):
            sl = slice(h * LANES, (h + 1) * LANES)
            qo_ref[:, sl] = _rope(q[:, sl], c_, u_, d_, 112, 16).astype(qo_ref.dtype)
            ko_ref[:, sl] = (k[:, sl] + kr).astype(ko_ref.dtype)

    tab, out = _rows(ts, LANES), _rows(ts, 1024)
    return pl.pallas_call(
        body, name="mla_up", grid=(s_ // ts,),
        in_specs=[_rows(ts, Q_LORA), _rows(ts, KV_LORA), tab, _const(Q_LORA, 1024), _const(KV_LORA, 1024),
                  _const(KV_LORA, 1024), tab, tab, tab],
        out_specs=[out, out, out], out_shape=[jax.ShapeDtypeStruct((s_, 1024), MXU_DTYPE)] * 3,
        compiler_params=_cparams(("parallel",)),
    )(cq, ckv, kro, wuq, wuk, wuv, q_cos, b_up, b_dn)


def _mla_up_bwd(dqc, dkc, dvp, wuq, wukv, p, gq, gkv, tabs, *, ts=256):
    s_ = dqc.shape[0]
    _, _, _, q_cos, k_cos, b_up, b_dn = tabs

    def body(dq_ref, dk_ref, dv_ref, wq_ref, wkv_ref, ql_ref, kvl_ref, gq_ref, gkv_ref, qc, kc, bu, bd,
             dqo_ref, dkvo_ref, dkr_ref, dql_ref, dkvl_ref, dgq_ref, dgkv_ref):
        c_, u_, d_ = qc[...], bu[...], bd[...]
        tot = jnp.zeros((ts, LANES), F32)
        for h in range(MLA_HEADS):
            sl = slice(h * LANES, (h + 1) * LANES)
            dqo_ref[:, sl] = _rope_t(dq_ref[:, sl], c_, u_, d_, 112, 16).astype(dqo_ref.dtype)
            dk = dk_ref[:, sl]
            dkvo_ref[:, sl] = dk.astype(dkvo_ref.dtype)
            tot = tot + dk
        dkvo_ref[:, 1024:2048] = dv_ref[...].astype(dkvo_ref.dtype)
        dkr_ref[...] = _rope_t(tot, kc[...], u_, d_, 112, 16).astype(dkr_ref.dtype)

        @pl.when(pl.program_id(0) == 0)
        def _():
            dgq_ref[...] = jnp.zeros(dgq_ref.shape, F32)
            dgkv_ref[...] = jnp.zeros(dgkv_ref.shape, F32)

        for do_ref, w_ref, x_ref, g_ref, dx_ref, dg_ref in ((dqo_ref, wq_ref, ql_ref, gq_ref, dql_ref, dgq_ref),
                                                            (dkvo_ref, wkv_ref, kvl_ref, gkv_ref, dkvl_ref, dgkv_ref)):
            d = lax.dot_general(do_ref[...], w_ref[...], NT, preferred_element_type=F32)
            dx, gg = _rms_bwd(x_ref[...], g_ref[...], d)
            dx_ref[...] = dx.astype(dx_ref.dtype)
            dg_ref[...] += _sublane_sum(gg)

    tab = _rows(ts, LANES)
    return pl.pallas_call(
        body, name="mla_up_bwd", grid=(s_ // ts,),
        in_specs=[_rows(ts, 1024), _rows(ts, 1024), _rows(ts, 1024), _const(Q_LORA, 1024), _const(KV_LORA, 2048),
                  _rows(ts, Q_LORA, F_QLAT // Q_LORA), _rows(ts, KV_LORA, F_KVLAT // KV_LORA),
                  _const(1, Q_LORA), _const(1, KV_LORA), tab, tab, tab, tab],
        out_specs=[_rows(ts, 1024), _rows(ts, 2048), _rows(ts, LANES), _rows(ts, Q_LORA), _rows(ts, KV_LORA),
                   _const(SUBLANES, Q_LORA), _const(SUBLANES, KV_LORA)],
        out_shape=[jax.ShapeDtypeStruct((s_, 1024), MXU_DTYPE), jax.ShapeDtypeStruct((s_, 2048), MXU_DTYPE),
                   jax.ShapeDtypeStruct((s_, LANES), MXU_DTYPE), jax.ShapeDtypeStruct((s_, Q_LORA), MXU_DTYPE),
                   jax.ShapeDtypeStruct((s_, KV_LORA), MXU_DTYPE), jax.ShapeDtypeStruct((SUBLANES, Q_LORA), F32),
                   jax.ShapeDtypeStruct((SUBLANES, KV_LORA), F32)],
        compiler_params=_cparams(("arbitrary",)),
    )(dqc, dkc, dvp, wuq, wukv, p, p, gq, gkv, q_cos, k_cos, b_up, b_dn)


def _assemble_dp(dgab, dqa, dqlat, dkr, dka, dva, dkvlat, tabs, *, ts=256):
    s_ = dqa.shape[0]
    a_cos, a_up, a_dn = tabs[0], tabs[1], tabs[2]

    def body(dg_ref, dq_ref, dql_ref, dkr_ref, dk_ref, dv_ref, dkvl_ref, ac, au, ad, o_ref):
        c_, u_, d_ = ac[...], au[...], ad[...]
        o_ref[:, P_GA:P_Q] = dg_ref[...]
        for h in range(SWA_HEADS):
            sl = slice(h * LANES, (h + 1) * LANES)
            o_ref[:, P_Q + h * LANES:P_Q + (h + 1) * LANES] = _rope_t(dq_ref[:, sl], c_, u_, d_, 96, 32).astype(o_ref.dtype)
        o_ref[:, P_QLAT:P_KR] = dql_ref[...]
        o_ref[:, P_KR:P_K] = dkr_ref[...]
        for h in range(SWA_KV_HEADS):
            sl = slice(h * LANES, (h + 1) * LANES)
            o_ref[:, P_K + h * LANES:P_K + (h + 1) * LANES] = _rope_t(dk_ref[:, sl], c_, u_, d_, 96, 32).astype(o_ref.dtype)
        o_ref[:, P_V:P_KVLAT] = dv_ref[...]
        o_ref[:, P_KVLAT:P_W] = dkvl_ref[...]

    tab = _rows(ts, LANES)
    return pl.pallas_call(
        body, name="assemble_dp", grid=(s_ // ts,),
        in_specs=[_rows(ts, 2048), _rows(ts, 1024), _rows(ts, Q_LORA), _rows(ts, LANES), _rows(ts, 256), _rows(ts, 256),
                  _rows(ts, KV_LORA), tab, tab, tab],
        out_specs=_rows(ts, P_W), out_shape=jax.ShapeDtypeStruct((s_, P_W), MXU_DTYPE),
        compiler_params=_cparams(("parallel",)),
    )(dgab, dqa, dqlat, dkr, dka, dva, dkvlat, a_cos, a_up, a_dn)


def _attn_out_gate(oa, ob, woa, wob, p, *, ts=512):
    s_ = p.shape[0]

    def body(oa_ref, ob_ref, wa_ref, wb_ref, ga_ref, gb_ref, ta_ref, tb_ref, y_ref):
        ta = jnp.dot(oa_ref[...], wa_ref[...], preferred_element_type=F32)
        tb = jnp.dot(ob_ref[...], wb_ref[...], preferred_element_type=F32)
        ta_ref[...] = ta
        tb_ref[...] = tb
        y_ref[...] = (_sigmoid(ga_ref[...]) * ta + _sigmoid(gb_ref[...]) * tb).astype(y_ref.dtype)

    w = _const(1024, 1024)
    return pl.pallas_call(
        body, name="attn_out_gate", grid=(s_ // ts,),
        in_specs=[_rows(ts, 1024), _rows(ts, 1024), w, w, _rows(ts, 1024, F_GA // 1024), _rows(ts, 1024, F_GB // 1024)],
        out_specs=[_rows(ts, 1024)] * 3,
        out_shape=[jax.ShapeDtypeStruct((s_, 1024), F32)] * 2 + [jax.ShapeDtypeStruct((s_, 1024), MXU_DTYPE)],
        compiler_params=_cparams(("parallel",)),
    )(oa, ob, woa, wob, p, p)


def _d_y_gate(dx1b, wout, p, ta, tb, *, ts=512):
    s_ = p.shape[0]

    def body(dx_ref, w_ref, ga_ref, gb_ref, ta_ref, tb_ref, dta_ref, dtb_ref, dg_ref):
        d = lax.dot_general(dx_ref[...], w_ref[...], NT, preferred_element_type=F32)
        sa, sb = _sigmoid(ga_ref[...]), _sigmoid(gb_ref[...])
        dta_ref[...] = (d * sa).astype(dta_ref.dtype)
        dtb_ref[...] = (d * sb).astype(dtb_ref.dtype)
        dg_ref[:, 0:1024] = (d * ta_ref[...] * (sa * (1.0 - sa))).astype(dg_ref.dtype)
        dg_ref[:, 1024:2048] = (d * tb_ref[...] * (sb * (1.0 - sb))).astype(dg_ref.dtype)

    return pl.pallas_call(
        body, name="d_y_gate", grid=(s_ // ts,),
        in_specs=[_rows(ts, 1024), _const(1024, 1024), _rows(ts, 1024, F_GA // 1024), _rows(ts, 1024, F_GB // 1024),
                  _rows(ts, 1024), _rows(ts, 1024)],
        out_specs=[_rows(ts, 1024), _rows(ts, 1024), _rows(ts, 2048)],
        out_shape=[jax.ShapeDtypeStruct((s_, 1024), MXU_DTYPE)] * 2 + [jax.ShapeDtypeStruct((s_, 2048), MXU_DTYPE)],
        compiler_params=_cparams(("parallel",)),
    )(dx1b, wout, p, p, ta, tb)


FF_TILE = D_FF // 2


def _ffn_in_act(x1, g, wgu_t, *, tm=512):
    s_ = x1.shape[0]
    n = s_ // tm

    def body(x_ref, g_ref, w_ref, h_ref, gu_ref, a_ref):
        h = _rms(x_ref[...], g_ref[...]).astype(h_ref.dtype)
        h_ref[...] = h
        p = lax.dot_general(h, w_ref[...], NT, preferred_element_type=F32)
        gu_ref[...] = p.astype(gu_ref.dtype)
        gate = p[:, :FF_TILE]
        a_ref[...] = (gate * _sigmoid(gate) * p[:, FF_TILE:]).astype(a_ref.dtype)

    return pl.pallas_call(
        body, name="ffn_in", grid=(2, s_ // tm),
        in_specs=[pl.BlockSpec((tm, D_MODEL), lambda j, i: (i, 0)), pl.BlockSpec((1, D_MODEL), lambda j, i: (0, 0)),
                  pl.BlockSpec((2 * FF_TILE, D_MODEL), lambda j, i: (j, 0))],
        out_specs=[pl.BlockSpec((tm, D_MODEL), lambda j, i: (i + j * (n - 1 - i), 0)),
                   pl.BlockSpec((tm, 2 * FF_TILE), lambda j, i: (i, j)),
                   pl.BlockSpec((tm, FF_TILE), lambda j, i: (i, j))],
        out_shape=[jax.ShapeDtypeStruct((s_, D_MODEL), MXU_DTYPE), jax.ShapeDtypeStruct((s_, 2 * D_FF), MXU_DTYPE),
                   jax.ShapeDtypeStruct((s_, D_FF), MXU_DTYPE)],
        compiler_params=_cparams(("arbitrary", "arbitrary")),
    )(x1, g, wgu_t)


def _d_act_swiglu(dx2b, wd, gu, *, tm=512):
    s_ = dx2b.shape[0]

    def body(d_ref, w_ref, gu_ref, o_ref):
        da = lax.dot_general(d_ref[...], w_ref[...], NT, preferred_element_type=F32)
        g, u = gu_ref[:, :FF_TILE].astype(F32), gu_ref[:, FF_TILE:].astype(F32)
        sg = _sigmoid(g)
        o_ref[:, :FF_TILE] = (da * u * (sg * (1.0 + g * (1.0 - sg)))).astype(o_ref.dtype)
        o_ref[:, FF_TILE:] = (da * (g * sg)).astype(o_ref.dtype)

    gu_spec = pl.BlockSpec((tm, 2 * FF_TILE), lambda j, i: (i, j))
    return pl.pallas_call(
        body, name="d_act", grid=(2, s_ // tm),
        in_specs=[pl.BlockSpec((tm, D_MODEL), lambda j, i: (i, 0)), pl.BlockSpec((FF_TILE, D_MODEL), lambda j, i: (j, 0)), gu_spec],
        out_specs=gu_spec, out_shape=jax.ShapeDtypeStruct((s_, 2 * D_FF), MXU_DTYPE),
        compiler_params=_cparams(("parallel", "parallel")),
    )(dx2b, wd, gu)


def _ffn_out_loss(act, wd, x1, g, tgt, *, ts=512):
    s_, c = x1.shape
    kk = act.shape[1]

    def body(a_ref, w_ref, x_ref, g_ref, t_ref, dx_ref, dxb_ref, dg_ref, lp_ref, tot_ref):
        v = x_ref[...] + jnp.dot(a_ref[...], w_ref[...], preferred_element_type=F32)
        r = lax.rsqrt(jnp.mean(v * v, axis=-1, keepdims=True) + EPS)
        xh = v * r
        gg = g_ref[...]
        e = xh * gg - t_ref[...]
        do = e * (1.0 / c)
        dxh = do * gg
        dx = r * (dxh - xh * jnp.mean(dxh * xh, axis=-1, keepdims=True))
        dx_ref[...] = dx
        dxb_ref[...] = dx.astype(dxb_ref.dtype)
        i = pl.program_id(0)

        @pl.when(i == 0)
        def _():
            dg_ref[...] = jnp.zeros(dg_ref.shape, F32)
            lp_ref[...] = jnp.zeros(lp_ref.shape, F32)

        dg_ref[...] += _sublane_sum(do * xh)
        lp_ref[...] += _sublane_sum(e * e)
        tot_ref[...] = jnp.full(tot_ref.shape, (0.5 / c) * jnp.sum(lp_ref[...]), F32)

    return pl.pallas_call(
        body, name="ffn_out_loss", grid=(s_ // ts,),
        in_specs=[_rows(ts, kk), _const(kk, c), _rows(ts, c), _const(1, c), _rows(ts, c)],
        out_specs=[_rows(ts, c), _rows(ts, c), _const(SUBLANES, c), _const(SUBLANES, c), _const(SUBLANES, LANES)],
        out_shape=[jax.ShapeDtypeStruct((s_, c), F32), jax.ShapeDtypeStruct((s_, c), MXU_DTYPE),
                   jax.ShapeDtypeStruct((SUBLANES, c), F32), jax.ShapeDtypeStruct((SUBLANES, c), F32),
                   jax.ShapeDtypeStruct((SUBLANES, LANES), F32)],
        compiler_params=_cparams(("arbitrary",)),
    )(act, wd, x1, g, tgt)


def _mla_d_out(dtb, wob, o32, *, ts=512):
    s_ = dtb.shape[0]

    def body(dt_ref, w_ref, o_ref, dob_ref, dl_ref):
        d = lax.dot_general(dt_ref[...], w_ref[...], NT, preferred_element_type=F32)
        dob_ref[...] = d.astype(dob_ref.dtype)
        prod = d * o_ref[...]
        for h in range(MLA_HEADS):
            dl_ref[h] = jnp.sum(prod[:, h * LANES:(h + 1) * LANES].T, axis=0, keepdims=True)

    return pl.pallas_call(
        body, name="mla_d_out", grid=(s_ // ts,), in_specs=[_rows(ts, 1024), _const(1024, 1024), _rows(ts, 1024)],
        out_specs=[_rows(ts, 1024), pl.BlockSpec((MLA_HEADS, 1, ts), lambda i: (0, 0, i))],
        out_shape=[jax.ShapeDtypeStruct((s_, 1024), MXU_DTYPE), jax.ShapeDtypeStruct((MLA_HEADS, 1, s_), F32)],
        compiler_params=_cparams(("parallel",)),
    )(dtb, wob, o32)


SWA_T = 4 * BLOCK


SWA_W = SWA_GROUP * BLOCK


def _swa_masks(sb):
    kr = lax.broadcasted_iota(jnp.int32, (2 * BLOCK, SWA_W), 0)
    qc = jnp.bitwise_and(lax.broadcasted_iota(jnp.int32, (2 * BLOCK, SWA_W), 1), BLOCK - 1)
    band = jnp.logical_and(kr > qc, kr <= qc + BLOCK)
    first = jnp.logical_and(band, kr >= BLOCK)
    return band, jnp.logical_or(first, jnp.logical_and(band, sb > 0))


def _heads_to_rows(ref, rs):
    return jnp.concatenate([ref[rs, h * LANES:(h + 1) * LANES] for h in range(SWA_GROUP)], axis=0)


def _sink_row(sk_ref):
    return jnp.concatenate([sk_ref[0, h:h + 1, :] for h in range(SWA_GROUP)], axis=1) * LOG2E


def _swa_in_specs(rev, nsb):
    sbi = (lambda j: nsb - 1 - j) if rev else (lambda j: j)
    cur = pl.BlockSpec((SWA_T, LANES), lambda g, j: (sbi(j), g))
    prev = pl.BlockSpec((BLOCK, LANES), lambda g, j: (jnp.maximum(4 * sbi(j) - 1, 0), g))
    q = pl.BlockSpec((SWA_T, SWA_GROUP * LANES), lambda g, j: (sbi(j), g))
    sink = pl.BlockSpec((1, SUBLANES, LANES), lambda g, j: (g, 0, 0))
    lse = pl.BlockSpec((SWA_GROUP, 1, SWA_T), lambda g, j: (g, 0, sbi(j)))
    return q, cur, prev, sink, lse


def _swa_fwd(qa, ka, va, sink_b):
    s_ = qa.shape[0]
    nsb = s_ // SWA_T
    c2 = HEAD_DIM ** -0.5 * LOG2E

    def body(q_ref, kc_ref, kp_ref, vc_ref, vp_ref, sk_ref, o32_ref, o16_ref, lse_ref, kx, vx):
        kx[0:BLOCK, :] = kp_ref[...]
        kx[BLOCK:5 * BLOCK, :] = kc_ref[...]
        vx[0:BLOCK, :] = vp_ref[...]
        vx[BLOCK:5 * BLOCK, :] = vc_ref[...]
        band, band0 = _swa_masks(pl.program_id(1))
        sink2 = _sink_row(sk_ref)
        for b in range(4):
            rs = slice(b * BLOCK, (b + 1) * BLOCK)
            ks = slice(b * BLOCK, (b + 2) * BLOCK)
            st = lax.dot_general(kx[ks, :], _heads_to_rows(q_ref, rs), NT, preferred_element_type=F32) * c2
            st = jnp.where(band0 if b == 0 else band, st, -jnp.inf)
            m = jnp.maximum(jnp.max(st, axis=0, keepdims=True), sink2)
            pt = jnp.exp2(st - m)
            den = jnp.sum(pt, axis=0, keepdims=True) + jnp.exp2(sink2 - m)
            o = lax.dot_general((pt * (1.0 / den)).astype(MXU_DTYPE), vx[ks, :], TN, preferred_element_type=F32)
            lse = m + jnp.log2(den)
            for hh in range(SWA_GROUP):
                cs = slice(hh * LANES, (hh + 1) * LANES)
                o32_ref[rs, cs] = o[cs, :]
                o16_ref[rs, cs] = o[cs, :].astype(o16_ref.dtype)
                lse_ref[hh, :, rs] = lse[:, cs]

    q, cur, prev, sink, lse_spec = _swa_in_specs(False, nsb)
    return pl.pallas_call(
        body, name="swa_fwd", grid=(SWA_KV_HEADS, nsb), in_specs=[q, cur, prev, cur, prev, sink],
        out_specs=[q, q, lse_spec],
        out_shape=[jax.ShapeDtypeStruct((s_, SWA_HEADS * LANES), F32), jax.ShapeDtypeStruct((s_, SWA_HEADS * LANES), MXU_DTYPE),
                   jax.ShapeDtypeStruct((SWA_HEADS, 1, s_), F32)],
        scratch_shapes=[pltpu.VMEM((5 * BLOCK, LANES), MXU_DTYPE), pltpu.VMEM((5 * BLOCK, LANES), MXU_DTYPE)],
        compiler_params=_cparams(("parallel", "arbitrary")),
    )(qa, ka, ka, va, va, sink_b)


def _swa_bwd(qa, ka, va, sink_b, o32, do, lse):
    s_ = qa.shape[0]
    nsb = s_ // SWA_T
    scale = HEAD_DIM ** -0.5
    c2 = scale * LOG2E

    def body(q_ref, kc_ref, kp_ref, vc_ref, vp_ref, sk_ref, o_ref, do_ref, lse_ref,
             dq_ref, dk_ref, dv_ref, dsk_ref, kx, vx, kacc, vacc, kcar, vcar):
        j = pl.program_id(1)
        kx[0:BLOCK, :] = kp_ref[...]
        kx[BLOCK:5 * BLOCK, :] = kc_ref[...]
        vx[0:BLOCK, :] = vp_ref[...]
        vx[BLOCK:5 * BLOCK, :] = vc_ref[...]
        band, band0 = _swa_masks(nsb - 1 - j)
        kacc[...] = jnp.zeros(kacc.shape, F32)
        vacc[...] = jnp.zeros(vacc.shape, F32)

        @pl.when(j == 0)
        def _():
            kcar[...] = jnp.zeros(kcar.shape, F32)
            vcar[...] = jnp.zeros(vcar.shape, F32)
            dsk_ref[...] = jnp.zeros(dsk_ref.shape, F32)

        sink2 = _sink_row(sk_ref)
        dsink = jnp.zeros((1, SWA_W), F32)
        for b in range(4):
            rs = slice(b * BLOCK, (b + 1) * BLOCK)
            ks = slice(b * BLOCK, (b + 2) * BLOCK)
            q, k2, v2 = _heads_to_rows(q_ref, rs), kx[ks, :], vx[ks, :]
            d = _heads_to_rows(do_ref, rs)
            delta = jnp.sum((d * _heads_to_rows(o_ref, rs)).T, axis=0, keepdims=True)
            l2 = jnp.concatenate([lse_ref[hh, :, rs] for hh in range(SWA_GROUP)], axis=1)
            st = lax.dot_general(k2, q, NT, preferred_element_type=F32) * c2
            pt = jnp.exp2(jnp.where(band0 if b == 0 else band, st, -jnp.inf) - l2)
            db = d.astype(MXU_DTYPE)
            dst = (pt * (lax.dot_general(v2, db, NT, preferred_element_type=F32) - delta) * scale).astype(MXU_DTYPE)
            dq = lax.dot_general(dst, k2, TN, preferred_element_type=F32)
            for hh in range(SWA_GROUP):
                dq_ref[rs, hh * LANES:(hh + 1) * LANES] = dq[hh * LANES:(hh + 1) * LANES, :]
            kacc[ks, :] += jnp.dot(dst, q, preferred_element_type=F32)
            vacc[ks, :] += jnp.dot(pt.astype(MXU_DTYPE), db, preferred_element_type=F32)
            dsink = dsink - jnp.exp2(sink2 - l2) * delta
        for hh in range(SWA_GROUP):
            tot = jnp.sum(dsink[:, hh * LANES:(hh + 1) * LANES], axis=1, keepdims=True)
            dsk_ref[0, hh:hh + 1, :] += jnp.broadcast_to(tot, (1, LANES))

        dk_ref[0:3 * BLOCK, :] = kacc[BLOCK:4 * BLOCK, :]
        dk_ref[3 * BLOCK:4 * BLOCK, :] = kacc[4 * BLOCK:5 * BLOCK, :] + kcar[...]
        dv_ref[0:3 * BLOCK, :] = vacc[BLOCK:4 * BLOCK, :].astype(dv_ref.dtype)
        dv_ref[3 * BLOCK:4 * BLOCK, :] = (vacc[4 * BLOCK:5 * BLOCK, :] + vcar[...]).astype(dv_ref.dtype)
        kcar[...] = kacc[0:BLOCK, :]
        vcar[...] = vacc[0:BLOCK, :]

    q, cur, prev, sink, lse_spec = _swa_in_specs(True, nsb)
    return pl.pallas_call(
        body, name="swa_bwd", grid=(SWA_KV_HEADS, nsb),
        in_specs=[q, cur, prev, cur, prev, sink, q, q, lse_spec],
        out_specs=[q, cur, cur, sink],
        out_shape=[jax.ShapeDtypeStruct((s_, SWA_HEADS * LANES), F32), jax.ShapeDtypeStruct((s_, SWA_KV_HEADS * LANES), F32),
                   jax.ShapeDtypeStruct((s_, SWA_KV_HEADS * LANES), MXU_DTYPE),
                   jax.ShapeDtypeStruct((SWA_KV_HEADS, SUBLANES, LANES), F32)],
        scratch_shapes=[pltpu.VMEM((5 * BLOCK, LANES), MXU_DTYPE), pltpu.VMEM((5 * BLOCK, LANES), MXU_DTYPE),
                        pltpu.VMEM((5 * BLOCK, LANES), F32), pltpu.VMEM((5 * BLOCK, LANES), F32),
                        pltpu.VMEM((BLOCK, LANES), F32), pltpu.VMEM((BLOCK, LANES), F32)],
        compiler_params=_cparams(("arbitrary", "arbitrary")),
    )(qa, ka, ka, va, va, sink_b, o32, do, lse)


MLA_T = 512
MLA_FWD_GROUP = 4
MLA_BWD_GROUP = 2


def _mla_specs(s_, t, group):
    w = group * LANES
    qs = pl.BlockSpec((t, w), lambda g, i: (i, g))
    kv = pl.BlockSpec((s_, w), lambda g, i: (0, g))
    row = pl.BlockSpec((group, 1, t), lambda g, i: (g, 0, i))
    return qs, kv, row


def _causal_scores_t(k, q, t, c2, masked):
    st = lax.dot_general(k, q, NT, preferred_element_type=F32) * c2
    if masked:
        kr = lax.broadcasted_iota(jnp.int32, (t, t), 0)
        qc = lax.broadcasted_iota(jnp.int32, (t, t), 1)
        st = jnp.where(kr <= qc, st, -jnp.inf)
    return st


def _mla_fwd(qc, kc, vp):
    s_ = qc.shape[0]
    t = min(MLA_T, s_)
    c2 = MLA_QK ** -0.5 * LOG2E
    grp = MLA_FWD_GROUP

    def body(q_ref, k_ref, v_ref, o32_ref, o16_ref, lse_ref, m_s, acc_s):
        qi = pl.program_id(1)
        m_s[...] = jnp.full(m_s.shape, -jnp.inf, F32)
        acc_s[...] = jnp.zeros(acc_s.shape, F32)
        ones_lane = lax.broadcasted_iota(jnp.int32, (t, LANES), 1) == MLA_V

        def step(ki, masked):
            off = pl.multiple_of(ki * t, t)
            for g in range(grp):
                cs = slice(g * LANES, (g + 1) * LANES)
                st = _causal_scores_t(k_ref[pl.ds(off, t), cs], q_ref[:, cs], t, c2, masked)
                m_old = m_s[g]
                m_new = jnp.maximum(m_old, jnp.max(st, axis=0, keepdims=True))
                alpha = jnp.exp2(m_old - m_new)
                pt = jnp.exp2(st - m_new).astype(MXU_DTYPE)
                v = v_ref[pl.ds(off, t), cs]
                v = jnp.where(ones_lane, jnp.ones((), v.dtype), v)
                acc_s[g] = alpha * acc_s[g] + lax.dot_general(v, pt, TN, preferred_element_type=F32)
                m_s[g] = m_new

        def full_block(ki, carry):
            step(ki, False)
            return carry

        lax.fori_loop(0, qi, full_block, 0)
        step(qi, True)
        for g in range(grp):
            cs = slice(g * LANES, (g + 1) * LANES)
            acc = acc_s[g]
            l = acc[MLA_V:MLA_V + 1, :]
            o = (acc * (1.0 / l)).T
            o32_ref[:, cs] = o
            o16_ref[:, cs] = o.astype(o16_ref.dtype)
            lse_ref[g] = m_s[g] + jnp.log2(l)

    qs, kv, row = _mla_specs(s_, t, grp)
    return pl.pallas_call(
        body, name="mla_fwd", grid=(MLA_HEADS // grp, s_ // t), in_specs=[qs, kv, kv], out_specs=[qs, qs, row],
        out_shape=[jax.ShapeDtypeStruct((s_, MLA_HEADS * LANES), F32), jax.ShapeDtypeStruct((s_, MLA_HEADS * LANES), MXU_DTYPE),
                   jax.ShapeDtypeStruct((MLA_HEADS, 1, s_), F32)],
        scratch_shapes=[pltpu.VMEM((grp, 1, t), F32), pltpu.VMEM((grp, LANES, t), F32)],
        compiler_params=_cparams(("parallel", "arbitrary")),
    )(qc, kc, vp)


def _mla_bwd(qc, kc, vp, dob, lse, delta):
    s_ = qc.shape[0]
    t = min(MLA_T, s_)
    scale = MLA_QK ** -0.5
    c2 = scale * LOG2E
    grp = MLA_BWD_GROUP

    def body(q_ref, do_ref, lse_ref, dl_ref, k_ref, v_ref, dq_ref, dk_ref, dv_ref, dqt_s):
        qi = pl.program_id(1)

        @pl.when(qi == 0)
        def _():
            dk_ref[...] = jnp.zeros(dk_ref.shape, F32)
            dv_ref[...] = jnp.zeros(dv_ref.shape, F32)

        dqt_s[...] = jnp.zeros(dqt_s.shape, F32)

        def step(ki, masked):
            off = pl.multiple_of(ki * t, t)
            for g in range(grp):
                cs = slice(g * LANES, (g + 1) * LANES)
                q, d, k = q_ref[:, cs], do_ref[:, cs], k_ref[pl.ds(off, t), cs]
                pt = jnp.exp2(_causal_scores_t(k, q, t, c2, masked) - lse_ref[g])
                dpt = lax.dot_general(v_ref[pl.ds(off, t), cs], d, NT, preferred_element_type=F32)
                dst = (pt * (dpt - dl_ref[g]) * scale).astype(MXU_DTYPE)
                dv_ref[pl.ds(off, t), cs] += jnp.dot(pt.astype(MXU_DTYPE), d, preferred_element_type=F32)
                dk_ref[pl.ds(off, t), cs] += jnp.dot(dst, q, preferred_element_type=F32)
                dqt_s[g] += lax.dot_general(k, dst, TN, preferred_element_type=F32)

        def full_block(ki, carry):
            step(ki, False)
            return carry

        lax.fori_loop(0, qi, full_block, 0)
        step(qi, True)
        for g in range(grp):
            dq_ref[:, g * LANES:(g + 1) * LANES] = dqt_s[g].T

    qs, kv, row = _mla_specs(s_, t, grp)
    shp = jax.ShapeDtypeStruct((s_, MLA_HEADS * LANES), F32)
    return pl.pallas_call(
        body, name="mla_bwd", grid=(MLA_HEADS // grp, s_ // t), in_specs=[qs, qs, row, row, kv, kv],
        out_specs=[qs, kv, kv], out_shape=[shp, shp, shp], scratch_shapes=[pltpu.VMEM((grp, LANES, t), F32)],
        compiler_params=_cparams(("parallel", "arbitrary")),
    )(qc, dob, lse, delta, kc, vp)


def _pad_heads(w, nh, hd, axis):
    shp = w.shape
    w = w.reshape(shp[:axis] + (nh, hd) + shp[axis + 1:])
    pad = [(0, 0)] * w.ndim
    pad[axis + 1] = (0, LANES - hd)
    w = jnp.pad(w, pad)
    return w.reshape(shp[:axis] + (nh * LANES,) + shp[axis + 1:])


def _unpad_heads(w, nh, hd, axis):
    shp = w.shape
    w = w.reshape(shp[:axis] + (nh, LANES) + shp[axis + 1:])
    w = lax.slice_in_dim(w, 0, hd, axis=axis + 1)
    return w.reshape(shp[:axis] + (nh * hd,) + shp[axis + 1:])


PACK_W = 1024
ROW_TILE = 16
FULL_SHAPE = dict(w_in=(1024, 3488), w_uq=(384, 768), w_ukv=(256, 1024), w_o_swa=(512, 1024), w_o_mla=(512, 1024),
                  w_out=(1024, 1024), w_gate=(1024, 2816), w_up=(1024, 2816), w_down=(2816, 1024))
BIG = tuple(FULL_SHAPE)
ROW_SHARDED = ("w_out", "w_down")
W_IN_COLS = FULL_SHAPE["w_in"][1] // N_DEV
W_IN_ROWS = -(-W_IN_COLS // ROW_TILE) * ROW_TILE
FF_COLS = D_FF // N_DEV
OUT_ROWS = D_MODEL // N_DEV
SMALL_ROW0 = W_IN_ROWS + OUT_ROWS
SMALL_FLAT = (("w_uq", 0, 36), ("w_ukv", 48, 32), ("w_o_swa", 80, 64), ("w_o_mla", 144, 64))
SMALL_ROWS = 208
EARLY_ROWS = SMALL_ROW0 + SMALL_ROWS
LATE_ROWS = 3 * FF_COLS
PACK_ROWS = EARLY_ROWS + LATE_ROWS


def _shard_shape(n):
    r, c = FULL_SHAPE[n]
    return (r // N_DEV, c) if n in ROW_SHARDED else (r, c // N_DEV)


def _wire_pack(sh, dtype):
    c = lambda n: sh[n].astype(dtype)
    rows = [jnp.pad(c("w_in").T, ((0, W_IN_ROWS - W_IN_COLS), (0, 0))), c("w_out")]
    for n, _, r in SMALL_FLAT:
        rows.append(jnp.pad(c(n).reshape(r, PACK_W), ((0, -r % ROW_TILE), (0, 0))))
    return jnp.concatenate(rows + [c("w_gate").T, c("w_up").T, c("w_down")], 0)


MID_ROWS = OUT_ROWS + SMALL_ROWS


def _mid_unpack(p):
    out = dict(w_out=p[0:OUT_ROWS])
    for n, off, r in SMALL_FLAT:
        out[n] = p[OUT_ROWS + off:OUT_ROWS + off + r].reshape(_shard_shape(n))
    return out


def _w_in_row_maps():
    sp = lambda col: (col // W_IN_COLS) * W_IN_ROWS + col % W_IN_COLS
    fwd = np.full((P_W,), -1, np.int64)

    def put(t0, c0, n):
        fwd[t0:t0 + n] = [sp(c) for c in range(c0, c0 + n)]

    put(P_GA, IN_OFF[6], D_MODEL)
    put(P_GB, IN_OFF[7], D_MODEL)
    for h in range(SWA_HEADS):
        put(P_Q + LANES * h, IN_OFF[0] + HEAD_DIM * h, HEAD_DIM)
    put(P_QLAT, IN_OFF[3], Q_LORA)
    put(P_KR + KR_LANE, IN_OFF[5], MLA_ROPE)
    for h in range(SWA_KV_HEADS):
        put(P_K + LANES * h, IN_OFF[1] + HEAD_DIM * h, HEAD_DIM)
        put(P_V + LANES * h, IN_OFF[2] + HEAD_DIM * h, HEAD_DIM)
    put(P_KVLAT, IN_OFF[4], KV_LORA)
    inv = np.full((N_DEV * W_IN_ROWS,), -1, np.int64)
    inv[fwd[fwd >= 0]] = np.nonzero(fwd >= 0)[0]
    return fwd, inv


def _take_rows(src, idx, *, name, tile=2 * LANES):
    n_out, n_src, width = len(idx), src.shape[0], src.shape[1]
    assert n_out % tile == 0 and n_src % tile == 0
    n_tiles = n_out // tile
    blocks = [sorted({int(v) // tile for v in idx[i * tile:(i + 1) * tile] if v >= 0}) for i in range(n_tiles)]
    k_max = max(1, max(len(b) for b in blocks))
    tab = np.zeros((n_tiles, k_max), np.int32)
    sel = np.zeros((n_tiles, k_max, tile, tile), np.float32)
    for i, blks in enumerate(blocks):
        for m, b in enumerate(blks):
            tab[i, m] = b
            for r in range(tile):
                v = int(idx[i * tile + r])
                if v >= 0 and v // tile == b:
                    sel[i, m, r, v % tile] = 1.0

    def body(tab_ref, sel_ref, *refs):
        o_ref = refs[k_max]
        acc = jnp.dot(sel_ref[0, 0], refs[0][...], preferred_element_type=F32)
        for m in range(1, k_max):
            acc = acc + jnp.dot(sel_ref[0, m], refs[m][...], preferred_element_type=F32)
        o_ref[...] = acc.astype(o_ref.dtype)

    def src_spec(m):
        return pl.BlockSpec((tile, width), lambda i, t: (t[i * k_max + m], 0))

    return pl.pallas_call(
        body, name=name,
        grid_spec=pltpu.PrefetchScalarGridSpec(
            num_scalar_prefetch=1, grid=(n_tiles,),
            in_specs=[pl.BlockSpec((1, k_max, tile, tile), lambda i, t: (i, 0, 0, 0))] + [src_spec(m) for m in range(k_max)],
            out_specs=pl.BlockSpec((tile, width), lambda i, t: (i, 0))),
        out_shape=jax.ShapeDtypeStruct((n_out, width), src.dtype),
        compiler_params=_cparams(("parallel",)),
    )(jnp.asarray(tab.reshape(-1)), jnp.asarray(sel, src.dtype), *([src] * k_max))


def _w_in_operand(win_g):
    return _take_rows(win_g.reshape(N_DEV * W_IN_ROWS, PACK_W), _w_in_row_maps()[0], name="w_in_rows")


def _mid_operands(wout_g, small_g):
    def full(n, off, r):
        a = small_g[:, off:off + r].reshape((N_DEV,) + _shard_shape(n))
        return jnp.moveaxis(a, 0, 1).reshape(FULL_SHAPE[n])

    w = {n: full(n, off, r) for n, off, r in SMALL_FLAT}
    ukv = w["w_ukv"].reshape(KV_LORA, MLA_HEADS, MLA_NOPE + MLA_V)
    return dict(
        wout=wout_g.reshape(D_MODEL, D_MODEL),
        wuq=_pad_heads(w["w_uq"], MLA_HEADS, MLA_QK, 1),
        wuk=_pad_heads(ukv[:, :, :MLA_NOPE].reshape(KV_LORA, -1), MLA_HEADS, MLA_NOPE, 1),
        wuv=_pad_heads(ukv[:, :, MLA_NOPE:].reshape(KV_LORA, -1), MLA_HEADS, MLA_V, 1),
        woa=_pad_heads(w["w_o_swa"], SWA_HEADS, HEAD_DIM, 0),
        wob=_pad_heads(w["w_o_mla"], MLA_HEADS, MLA_V, 0),
    )


def _mid_grad_pack(g):
    uk = _unpad_heads(g["wukv"][:, :1024], MLA_HEADS, MLA_NOPE, 1).reshape(KV_LORA, MLA_HEADS, MLA_NOPE)
    uv = _unpad_heads(g["wukv"][:, 1024:], MLA_HEADS, MLA_V, 1).reshape(KV_LORA, MLA_HEADS, MLA_V)
    w = dict(w_uq=_unpad_heads(g["wuq"], MLA_HEADS, MLA_QK, 1), w_ukv=jnp.concatenate([uk, uv], 2).reshape(KV_LORA, -1),
             w_o_swa=_unpad_heads(g["woa"], SWA_HEADS, HEAD_DIM, 0), w_o_mla=_unpad_heads(g["wob"], MLA_HEADS, MLA_V, 0))

    def flat(n, r):
        rr, cc = FULL_SHAPE[n]
        a = jnp.moveaxis(w[n].reshape(rr, N_DEV, cc // N_DEV), 1, 0).reshape(N_DEV, r, PACK_W)
        return jnp.pad(a, ((0, 0), (0, -r % ROW_TILE), (0, 0))).astype(WIRE_DTYPE)

    return jnp.concatenate([g["wout"].reshape(N_DEV, OUT_ROWS, PACK_W)] + [flat(n, r) for n, _, r in SMALL_FLAT], 1)


def _w_in_grad_chunks(g_win_t):
    return _take_rows(g_win_t, _w_in_row_maps()[1], name="dw_in_rows").reshape(N_DEV, W_IN_ROWS, PACK_W)


def _local_step(x, tgt, win_t, small, weights, grads):
    s_ = x.shape[0]
    tabs = _rope_tables(s_)
    sink_b = jnp.broadcast_to(small["swa_sinks"].reshape(SWA_KV_HEADS, SWA_GROUP, 1), (SWA_KV_HEADS, SWA_GROUP, LANES))
    sink_b = jnp.pad(sink_b, ((0, 0), (0, SUBLANES - SWA_GROUP), (0, 0)))

    h, qa, ka, va, cq, ckv, kro, p = _proj_in(x, small["mix_norm_g"], win_t, small["q_norm_g"], small["kv_norm_g"], tabs)
    ops = weights.mid(cq)
    oa32, oa16, lse_a = _swa_fwd(qa, ka, va, sink_b)
    qc, kc, vp = _mla_up(cq, ckv, kro, ops["wuq"], ops["wuk"], ops["wuv"], tabs)
    ob32, ob16, lse_b = _mla_fwd(qc, kc, vp)
    ta, tb, y = _attn_out_gate(oa16, ob16, ops["woa"], ops["wob"], p)
    x1 = _mm(y, ops["wout"], "nn", name="out_proj", add=x, tm=1024, tn=1024)
    wgu_t, wd = weights.late(x1)
    h2, gu, act = _ffn_in_act(x1, small["ffn_norm_g"], wgu_t)

    dx2, dx2b, dg3, _, tot = _ffn_out_loss(act, wd, x1, small["final_norm_g"].reshape(1, D_MODEL), tgt)
    g = {}
    g_wd = _mm(act, dx2b, "tn", name="dw_down", tm=FF_TILE, tn=1024, tk=2048, out_dtype=WIRE_DTYPE)
    dgu = _d_act_swiglu(dx2b, wd, gu)
    g_wgu = _mm(dgu, h2, "tn", name="dw_ffn_in", tm=FF_TILE, tn=1024, tk=2048, out_dtype=WIRE_DTYPE)
    token = grads.late(g_wgu, g_wd)
    dx1, dx1b, dg2 = _mm_norm_bwd(dgu, wgu_t, x1, small["ffn_norm_g"] + token[0:1, 0:1], dx2, name="d_h2")
    g["wout"] = _mm(y, dx1b, "tn", name="dw_out", tm=1024, tn=1024, tk=1024, out_dtype=WIRE_DTYPE)
    dta, dtb, dgab = _d_y_gate(dx1b, ops["wout"], p, ta, tb)
    doa = _mm(dta, ops["woa"], "nt", name="d_oa", tm=1024, tn=1024)
    g["woa"] = _mm(oa16, dta, "tn", name="dw_o_swa", tm=1024, tn=1024, tk=1024)
    g["wob"] = _mm(ob16, dtb, "tn", name="dw_o_mla", tm=1024, tn=1024, tk=1024)
    dob16, delta_b = _mla_d_out(dtb, ops["wob"], ob32)
    dqc, dkc, dvp = _mla_bwd(qc, kc, vp, dob16, lse_b, delta_b)
    dqp, dkv, dkr, dqlat, dkvlat, dgq, dgkv = _mla_up_bwd(
        dqc, dkc, dvp, ops["wuq"], jnp.concatenate([ops["wuk"], ops["wuv"]], 1), p, small["q_norm_g"], small["kv_norm_g"], tabs)
    g["wuq"] = _mm(cq, dqp, "tn", name="dw_uq", tm=Q_LORA, tn=1024, tk=512)
    g["wukv"] = _mm(ckv, dkv, "tn", name="dw_ukv", tm=KV_LORA, tn=1024, tk=512)
    token = grads.mid(g)
    dqa, dka, dva, dsk = _swa_bwd(qa, ka, va, sink_b + token[0:1, 0:1], oa32, doa, lse_a)
    dp = _assemble_dp(dgab, dqa, dqlat, dkr, dka, dva, dkvlat, tabs)
    token = grads.last(_mm(dp, h, "tn", name="dw_in", tm=2176, tn=1024, tk=1024, out_dtype=WIRE_DTYPE))
    gx, _, dg1 = _mm_norm_bwd(dp, win_t, x, small["mix_norm_g"], dx1, name="d_h", after=token)

    sm = dict(mix_norm_g=dg1, ffn_norm_g=dg2, final_norm_g=dg3, q_norm_g=dgq, kv_norm_g=dgkv,
              swa_sinks=dsk[:, :SWA_GROUP, 0].reshape(1, SWA_HEADS))
    return tot, gx, sm


MESH = pl.DeviceIdType.MESH
ANY = pl.BlockSpec(memory_space=pl.ANY)


def _position():
    return lax.axis_index("x"), lax.axis_index("y"), lax.axis_index("c")


def _all_gather(block, pieces, shapes, *, name):
    n_out = len(shapes)
    n_rows = sum(p[3] for p in pieces)

    def body(x_ref, *refs):
        outs, (send_sems, recv_sems, local_sem) = refs[:n_out], refs[n_out:]
        x, y, c = _position()
        me, sibling = (x, y, c), (x, y, 1 - c)
        chips = [(1 - x, y), (x, 1 - y), (1 - x, 1 - y)]

        def dst(piece, blk):
            arr, lead, _, _ = piece
            return outs[arr].at[lead(4 * blk[0] + 2 * blk[1] + blk[2])]

        def own(piece):
            return x_ref.at[pl.ds(piece[2], piece[3])]

        def copies(k, blk, to, from_input):
            return [pltpu.make_async_remote_copy(
                src_ref=own(p) if from_input else dst(p, blk), dst_ref=dst(p, blk), send_sem=send_sems.at[k],
                recv_sem=recv_sems.at[k], device_id=to, device_id_type=MESH) for p in pieces]

        gathered_rows = x_ref.at[pl.ds(0, n_rows)]

        def whole_block(k):
            return pltpu.make_async_remote_copy(src_ref=gathered_rows, dst_ref=gathered_rows, send_sem=send_sems.at[k],
                                                recv_sem=recv_sems.at[k], device_id=me, device_id_type=MESH)

        for p in pieces:
            pltpu.make_async_copy(own(p), dst(p, me), local_sem).start()
        for cp in copies(0, me, sibling, True):
            cp.start()
        for j, chip in enumerate(chips):
            for cp in copies(1 + j, me, (*chip, c), True):
                cp.start()
        for j, chip in enumerate(chips):
            whole_block(1 + j).wait_recv()
            for cp in copies(4 + j, (*chip, c), sibling, False):
                cp.start()
        whole_block(0).wait_recv()
        for j in range(3):
            whole_block(4 + j).wait_recv()
        for k in range(7):
            whole_block(k).wait_send()
        pltpu.make_async_copy(gathered_rows, gathered_rows, local_sem).wait()

    return pl.pallas_call(
        body, name=name, out_shape=[jax.ShapeDtypeStruct(s, block.dtype) for s in shapes], in_specs=[ANY],
        out_specs=[ANY] * n_out,
        scratch_shapes=[pltpu.SemaphoreType.DMA((7,)), pltpu.SemaphoreType.DMA((7,)), pltpu.SemaphoreType.DMA],
    )(block)


HBM = pl.BlockSpec(memory_space=pltpu.HBM)
SEM = pl.BlockSpec(memory_space=pltpu.SEMAPHORE)
TILE_DEVS = FF_TILE // FF_COLS
GU_SHAPE = (2, 2, TILE_DEVS, FF_COLS, PACK_W)


def _gate_slab(d):
    return (d // TILE_DEVS, 0, d % TILE_DEVS)


def _up_slab(d):
    return (d // TILE_DEVS, 1, d % TILE_DEVS)
D_SHAPE = (N_DEV, FF_COLS, PACK_W)
LAND_SHAPE = (N_DEV, LATE_ROWS, PACK_W)


def _split_params():
    return pltpu.CompilerParams(has_side_effects=pltpu.SideEffectType.DATAFLOW_SIDE_EFFECTING)


def _peer(x, y, c, k):
    return ((1 - x) if k & 4 else x, (1 - y) if k & 2 else y, (1 - c) if k & 1 else c)


def _empty_hbm(shape, dtype):
    return pltpu.with_memory_space_constraint(lax.empty(shape, dtype), pltpu.HBM)


def _wait_all(rows, send_sems, recv_sems, me):
    for k in range(N_DEV - 1):
        cp = pltpu.make_async_remote_copy(src_ref=rows, dst_ref=rows, send_sem=send_sems.at[k], recv_sem=recv_sems.at[k],
                                          device_id=me, device_id_type=MESH)
        cp.wait_send()
        cp.wait_recv()


def _token_shape():
    return jax.ShapeDtypeStruct((SUBLANES, LANES), F32)


def _gather_start(pack, row0, pieces, shapes, *, name):
    n = len(shapes)

    def body(*refs):
        p_ref, bufs, send_sems, recv_sems, token = refs[0], refs[1:1 + n], refs[1 + n], refs[2 + n], refs[-1]
        x, y, c = _position()
        me = 4 * x + 2 * y + c
        for k in range(1, N_DEV):
            off = row0
            for buf, lead, rows in pieces:
                pltpu.make_async_remote_copy(
                    src_ref=p_ref.at[pl.ds(off, rows)], dst_ref=bufs[buf].at[lead(me)], send_sem=send_sems.at[k - 1],
                    recv_sem=recv_sems.at[k - 1], device_id=_peer(x, y, c, k), device_id_type=MESH).start()
                off += rows
        token[...] = jnp.zeros_like(token)

    sems, dt = pltpu.SemaphoreType.DMA((N_DEV - 1,)), pack.dtype
    return pl.pallas_call(
        body, name=name,
        out_shape=(sems, sems, pltpu.HBM(pack.shape, dt)) + tuple(pltpu.HBM(s, dt) for s in shapes) + (_token_shape(),),
        in_specs=(HBM,) * (1 + n), out_specs=(SEM, SEM) + (HBM,) * (1 + n) + (pl.BlockSpec(memory_space=pltpu.VMEM),),
        input_output_aliases={i: 2 + i for i in range(1 + n)}, compiler_params=_split_params(),
    )(pltpu.with_memory_space_constraint(pack, pltpu.HBM), *[_empty_hbm(s, dt) for s in shapes])


def _gather_wait(started, row0, n_rows, after, *, name):
    send_sems, recv_sems, pack, *bufs = started[:-1]
    n = len(bufs)

    def body(*refs):
        _wait_all(refs[0].at[pl.ds(row0, n_rows)], refs[1 + n], refs[2 + n], _position())

    outs = pl.pallas_call(
        body, name=name, out_shape=tuple(pltpu.HBM(a.shape, a.dtype) for a in (pack, *bufs)),
        in_specs=(HBM,) * (1 + n) + (SEM, SEM, ANY), out_specs=(HBM,) * (1 + n),
        input_output_aliases={i: i for i in range(1 + n)}, compiler_params=_split_params(),
    )(pack, *bufs, send_sems, recv_sems, after)
    return outs[0], outs[1:]


def _scatter_start(srcs, pieces, *, name):
    n = len(srcs)
    land_shape = (N_DEV, sum(p[2] for p in pieces), PACK_W)

    def body(*refs):
        src_refs, land_ref, send_sems, recv_sems, token = refs[:n], refs[n], refs[n + 1], refs[n + 2], refs[-1]
        x, y, c = _position()
        me = 4 * x + 2 * y + c
        for k in range(1, N_DEV):
            px, py, pc = _peer(x, y, c, k)
            off = 0
            for si, lead, rows in pieces:
                pltpu.make_async_remote_copy(
                    src_ref=src_refs[si].at[lead(4 * px + 2 * py + pc)], dst_ref=land_ref.at[me, pl.ds(off, rows)],
                    send_sem=send_sems.at[k - 1], recv_sem=recv_sems.at[k - 1], device_id=(px, py, pc),
                    device_id_type=MESH).start()
                off += rows
        token[...] = jnp.zeros_like(token)

    sems, dt = pltpu.SemaphoreType.DMA((N_DEV - 1,)), srcs[0].dtype
    return pl.pallas_call(
        body, name=name,
        out_shape=(sems, sems) + tuple(pltpu.HBM(a.shape, dt) for a in srcs) + (pltpu.HBM(land_shape, dt), _token_shape()),
        in_specs=(HBM,) * (n + 1), out_specs=(SEM, SEM) + (HBM,) * (n + 1) + (pl.BlockSpec(memory_space=pltpu.VMEM),),
        input_output_aliases={i: 2 + i for i in range(n + 1)}, compiler_params=_split_params(),
    )(*[pltpu.with_memory_space_constraint(a, pltpu.HBM) for a in srcs], _empty_hbm(land_shape, dt))


def _scatter_wait(started, after, *, name):
    send_sems, recv_sems, *bufs = started[:-1]
    n = len(bufs)

    def body(*refs):
        _wait_all(refs[n - 1].at[0], refs[n], refs[n + 1], _position())

    return pl.pallas_call(
        body, name=name, out_shape=tuple(pltpu.HBM(a.shape, a.dtype) for a in bufs),
        in_specs=(HBM,) * n + (SEM, SEM, ANY), out_specs=(HBM,) * n, input_output_aliases={i: i for i in range(n)},
        compiler_params=_split_params(),
    )(*bufs, send_sems, recv_sems, after)


def _peer_sum(own, own_lead, land, block, rows, idx, *, name):
    lead_rank = own.ndim - 2

    def body(idx_ref, own_ref, *refs):
        o_ref = refs[N_DEV - 1]
        acc = own_ref[(0,) * lead_rank].astype(F32)
        for k in range(N_DEV - 1):
            acc = acc + refs[k][0].astype(F32)
        o_ref[...] = acc

    own_spec = pl.BlockSpec((1,) * lead_rank + (rows, PACK_W), lambda i, t: own_lead(t[0]) + (0, 0))

    def land_spec(k):
        return pl.BlockSpec((1, rows, PACK_W), lambda i, t: (t[k + 1], block, 0))

    return pl.pallas_call(
        body, name=name,
        grid_spec=pltpu.PrefetchScalarGridSpec(
            num_scalar_prefetch=1, grid=(1,), in_specs=[own_spec] + [land_spec(k) for k in range(N_DEV - 1)],
            out_specs=pl.BlockSpec((rows, PACK_W), lambda i, t: (0, 0))),
        out_shape=jax.ShapeDtypeStruct((rows, PACK_W), F32), compiler_params=_cparams(("arbitrary",)),
    )(idx, own, *([land] * (N_DEV - 1)))


def _adamw(w, g, m, v):
    m = ADAM_B1 * m + (1.0 - ADAM_B1) * g
    v = ADAM_B2 * v + (1.0 - ADAM_B2) * (g * g)
    m_hat = m / (1.0 - ADAM_B1 ** ADAM_STEP)
    v_hat = v / (1.0 - ADAM_B2 ** ADAM_STEP)
    delta = -ADAM_LR * (m_hat / (jnp.sqrt(v_hat) + ADAM_EPS) + ADAM_WD * w)
    return delta, m, v


def _adamw_call(w, g, m, v, *, name, max_rows=256):
    _, r, c_ = w.shape
    tr = max_rows if r > max_rows and r % max_rows == 0 else r

    def body(w_ref, g_ref, m_ref, v_ref, d_ref, mo_ref, vo_ref):
        d, mn, vn = _adamw(w_ref[0], g_ref[...], m_ref[0], v_ref[0])
        d_ref[0] = d
        mo_ref[0] = mn
        vo_ref[0] = vn

    row3 = pl.BlockSpec((1, tr, c_), lambda i: (0, i, 0))
    shp = jax.ShapeDtypeStruct((1, r, c_), F32)
    return pl.pallas_call(
        body, name=name, grid=(r // tr,), in_specs=[row3, pl.BlockSpec((tr, c_), lambda i: (i, 0)), row3, row3],
        out_specs=[row3] * 3, out_shape=[shp] * 3, compiler_params=_cparams(("parallel",)),
    )(w, g, m, v)


SMALL = ("mix_norm_g", "ffn_norm_g", "final_norm_g", "q_norm_g", "kv_norm_g", "swa_sinks")
SMALL_W = dict(mix_norm_g=1024, ffn_norm_g=1024, final_norm_g=1024, q_norm_g=Q_LORA, kv_norm_g=KV_LORA, swa_sinks=SWA_HEADS)


def _small_adamw(parts, w, m, v):
    ns = len(SMALL)

    def body(p_ref, *refs):
        ins, outs = refs[:3 * ns], refs[3 * ns:]
        tot = p_ref[0]
        for dev in range(1, N_DEV):
            tot = tot + p_ref[dev]
        for k, n in enumerate(SMALL):
            g = jnp.sum(tot[k * SUBLANES:(k + 1) * SUBLANES, :SMALL_W[n]], axis=0, keepdims=True)
            res = _adamw(ins[k][...], g, ins[ns + k][...], ins[2 * ns + k][...])
            for j, r in enumerate((g,) + tuple(res)):
                outs[j * ns + k][...] = r
        outs[4 * ns][...] = jnp.sum(tot[ns * SUBLANES:(ns + 1) * SUBLANES, 0:1], axis=0, keepdims=True)

    shapes = [jax.ShapeDtypeStruct((1, SMALL_W[n]), F32) for n in SMALL]
    vm = pl.BlockSpec(memory_space=pltpu.VMEM)
    out = pl.pallas_call(
        body, name="small_adamw", in_specs=[vm] * (1 + 3 * ns), out_specs=[vm] * (4 * ns + 1),
        out_shape=shapes * 4 + [jax.ShapeDtypeStruct((1, 1), F32)],
    )(parts, *[d[n] for d in (w, m, v) for n in SMALL])
    return [dict(zip(SMALL, out[j * ns:(j + 1) * ns])) for j in range(4)] + [out[4 * ns]]


def _small_pack(d, rows_each):
    parts = [jnp.pad(d[n].astype(F32), ((0, 0), (0, PACK_W - SMALL_W[n]))) for n in SMALL]
    out = jnp.concatenate(parts, 0)
    pad = -out.shape[0] % SUBLANES
    return jnp.pad(out, ((0, pad), (0, 0)))


def kernel(x, mix_norm_g, w_in, swa_sinks, q_norm_g, w_uq, kv_norm_g, w_ukv, w_o_swa, w_o_mla, w_out, ffn_norm_g, w_gate, w_up, w_down, final_norm_g, loss_target, m_mix_norm_g, m_w_in, m_swa_sinks, m_q_norm_g, m_w_uq, m_kv_norm_g, m_w_ukv, m_w_o_swa, m_w_o_mla, m_w_out, m_ffn_norm_g, m_w_gate, m_w_up, m_w_down, m_final_norm_g, v_mix_norm_g, v_w_in, v_swa_sinks, v_q_norm_g, v_w_uq, v_kv_norm_g, v_w_ukv, v_w_o_swa, v_w_o_mla, v_w_out, v_ffn_norm_g, v_w_gate, v_w_up, v_w_down, v_final_norm_g):
    big_w = dict(w_in=w_in[0], w_uq=w_uq[0], w_ukv=w_ukv[0], w_o_swa=w_o_swa[0], w_o_mla=w_o_mla[0], w_out=w_out[0],
                 w_gate=w_gate[0], w_up=w_up[0], w_down=w_down[0])
    big_w3 = dict(w_in=w_in, w_uq=w_uq, w_ukv=w_ukv, w_o_swa=w_o_swa, w_o_mla=w_o_mla, w_out=w_out, w_gate=w_gate, w_up=w_up,
                  w_down=w_down)
    big_m = dict(w_in=m_w_in, w_uq=m_w_uq, w_ukv=m_w_ukv, w_o_swa=m_w_o_swa, w_o_mla=m_w_o_mla, w_out=m_w_out,
                 w_gate=m_w_gate, w_up=m_w_up, w_down=m_w_down)
    big_v = dict(w_in=v_w_in, w_uq=v_w_uq, w_ukv=v_w_ukv, w_o_swa=v_w_o_swa, w_o_mla=v_w_o_mla, w_out=v_w_out,
                 w_gate=v_w_gate, w_up=v_w_up, w_down=v_w_down)
    small_w = dict(mix_norm_g=mix_norm_g, ffn_norm_g=ffn_norm_g, final_norm_g=final_norm_g.reshape(1, D_MODEL),
                   q_norm_g=q_norm_g, kv_norm_g=kv_norm_g, swa_sinks=swa_sinks)
    small_m = dict(mix_norm_g=m_mix_norm_g, ffn_norm_g=m_ffn_norm_g, final_norm_g=m_final_norm_g.reshape(1, D_MODEL),
                   q_norm_g=m_q_norm_g, kv_norm_g=m_kv_norm_g, swa_sinks=m_swa_sinks)
    small_v = dict(mix_norm_g=v_mix_norm_g, ffn_norm_g=v_ffn_norm_g, final_norm_g=v_final_norm_g.reshape(1, D_MODEL),
                   q_norm_g=v_q_norm_g, kv_norm_g=v_kv_norm_g, swa_sinks=v_swa_sinks)

    px, py, pc = _position()
    me = 4 * px + 2 * py + pc
    idx = jnp.stack([me] + [4 * qx + 2 * qy + qc for qx, qy, qc in (_peer(px, py, pc, k) for k in range(1, N_DEV))])
    idx = idx.astype(jnp.int32)

    dev = lambda d: (d,)
    pack = _wire_pack(big_w, WIRE_DTYPE)
    win_g, = _all_gather(pack, ((0, dev, 0, W_IN_ROWS),), ((N_DEV, W_IN_ROWS, PACK_W),), name="ag_early")
    ag_mid = _gather_start(pack, W_IN_ROWS, ((0, dev, OUT_ROWS), (1, dev, SMALL_ROWS)),
                           ((N_DEV, OUT_ROWS, PACK_W), (N_DEV, SMALL_ROWS, PACK_W)), name="ag_mid_start")
    ag = {}

    def own_rows(r0, r1, shape):
        return pack[r0:r1].reshape(shape)

    def mid_weights(after):
        pack_mid, (wout_g, small_g) = _gather_wait(ag_mid, W_IN_ROWS, MID_ROWS, after, name="ag_mid_wait")
        ag["late"] = _gather_start(pack_mid, EARLY_ROWS, ((0, _gate_slab, FF_COLS), (0, _up_slab, FF_COLS), (1, dev, FF_COLS)),
                                   (GU_SHAPE, D_SHAPE), name="ag_late_start")
        wout_g = lax.dynamic_update_slice(wout_g, own_rows(W_IN_ROWS, SMALL_ROW0, (1, OUT_ROWS, PACK_W)), (me, 0, 0))
        small_g = lax.dynamic_update_slice(small_g, own_rows(SMALL_ROW0, EARLY_ROWS, (1, SMALL_ROWS, PACK_W)), (me, 0, 0))
        ops = _mid_operands(wout_g, small_g)
        ops["wuq"] = ops["wuq"] + ag["late"][-1][0:1, 0:1].astype(ops["wuq"].dtype)
        return ops

    def late_weights(after):
        _, (gu, d) = _gather_wait(ag["late"], EARLY_ROWS, LATE_ROWS, after, name="ag_late_wait")
        slab = (1, 1, 1, FF_COLS, PACK_W)
        gu = lax.dynamic_update_slice(gu, own_rows(EARLY_ROWS, EARLY_ROWS + FF_COLS, slab), _gate_slab(me) + (0, 0))
        gu = lax.dynamic_update_slice(gu, own_rows(EARLY_ROWS + FF_COLS, EARLY_ROWS + 2 * FF_COLS, slab), _up_slab(me) + (0, 0))
        d = lax.dynamic_update_slice(d, own_rows(EARLY_ROWS + 2 * FF_COLS, PACK_ROWS, (1, FF_COLS, PACK_W)), (me, 0, 0))
        return gu.reshape(2 * D_FF, D_MODEL), d.reshape(D_FF, D_MODEL)

    rs = {}

    def late_grads(g_gu, g_d):
        rs["late"] = _scatter_start([g_gu.reshape(GU_SHAPE), g_d.reshape(D_SHAPE)],
                                    ((0, _gate_slab, FF_COLS), (0, _up_slab, FF_COLS), (1, dev, FF_COLS)),
                                    name="rs_late_start")
        return rs["late"][-1]

    def mid_grads(g):
        rs["mid"] = _scatter_start([_mid_grad_pack(g)], ((0, dev, MID_ROWS),), name="rs_mid_start")
        return rs["mid"][-1]

    def last_grads(g_win_t):
        rs["last"] = _scatter_start([_w_in_grad_chunks(g_win_t)], ((0, dev, W_IN_ROWS),), name="rs_last_start")
        return rs["last"][-1]

    first_w = dict(small_w, mix_norm_g=mix_norm_g + ag_mid[-1][0:1, 0:1])
    loss_tot, gx, g_small = _local_step(
        x[0], loss_target[0], _w_in_operand(win_g), first_w, types.SimpleNamespace(mid=mid_weights, late=late_weights),
        types.SimpleNamespace(late=late_grads, mid=mid_grads, last=last_grads))

    g_gu, g_d, land_late = _scatter_wait(rs["late"], gx, name="rs_late_wait")
    g_mid, land_mid = _scatter_wait(rs["mid"], gx, name="rs_mid_wait")
    g_win, land_last = _scatter_wait(rs["last"], gx, name="rs_last_wait")
    gw_t = dict(w_gate=_peer_sum(g_gu, _gate_slab, land_late, 0, FF_COLS, idx, name="rs_sum_gate"),
                w_up=_peer_sum(g_gu, _up_slab, land_late, 1, FF_COLS, idx, name="rs_sum_up"),
                w_in=_peer_sum(g_win, dev, land_last, 0, W_IN_ROWS, idx, name="rs_sum_in")[0:W_IN_COLS])
    gw = dict(w_down=_peer_sum(g_d, dev, land_late, 2, FF_COLS, idx, name="rs_sum_down"))
    gw.update(_mid_unpack(_peer_sum(g_mid, dev, land_mid, 0, MID_ROWS, idx, name="rs_sum_mid")))
    dw, mw, vw = {}, {}, {}
    swap = lambda a: jnp.swapaxes(a, 1, 2)
    for n in BIG:
        if n in gw_t:
            res = _adamw_call(swap(big_w3[n]), gw_t[n], swap(big_m[n]), swap(big_v[n]), name="adamw_" + n)
            dw[n], mw[n], vw[n] = (swap(r) for r in res)
        else:
            dw[n], mw[n], vw[n] = _adamw_call(big_w3[n], gw[n], big_m[n], big_v[n], name="adamw_" + n)
    gw = {n: g[None] for n, g in gw.items()}
    gw.update({n: swap(g[None]) for n, g in gw_t.items()})

    loss_rows = jnp.pad(loss_tot[0:1, 0:1], ((0, SUBLANES - 1), (0, PACK_W - 1)))
    small_rows = jnp.concatenate([_small_pack(g_small_rows(g_small), SUBLANES), loss_rows], 0)
    parts, = _all_gather(small_rows, ((0, lambda d: (d,), 0, small_rows.shape[0]),), ((N_DEV,) + small_rows.shape,),
                         name="ag_small")
    gs, ds, ms, vs, loss = _small_adamw(parts, small_w, small_m, small_v)
    loss = loss[0, 0]
    for d in (gs, ds, ms, vs):
        d["final_norm_g"] = d["final_norm_g"].reshape(D_MODEL)

    order = ("mix_norm_g", "w_in", "swa_sinks", "q_norm_g", "w_uq", "kv_norm_g", "w_ukv", "w_o_swa", "w_o_mla", "w_out",
             "ffn_norm_g", "w_gate", "w_up", "w_down", "final_norm_g")

    def leaves(big, small):
        return [big[n] if n in big else small[n] for n in order]

    return (loss, gx[None], *leaves(gw, gs), *leaves(dw, ds), *leaves(mw, ms), *leaves(vw, vs))


def g_small_rows(g_small):
    out = dict(g_small)
    out["swa_sinks"] = jnp.pad(g_small["swa_sinks"], ((0, SUBLANES - 1), (0, 0)))
    return out
```

```python
import types

import numpy as np
import jax
import jax.numpy as jnp
from jax import lax
from jax.experimental import pallas as pl
from jax.experimental.pallas import tpu as pltpu

F32 = jnp.float32
MXU_DTYPE = jnp.bfloat16
WIRE_DTYPE = jnp.bfloat16

D_MODEL = 1024
EPS = 1e-6
ROPE_THETA = 10000.0
BLOCK = 128
HEAD_DIM = 64
SWA_HEADS = 8
SWA_KV_HEADS = 2
SWA_GROUP = SWA_HEADS // SWA_KV_HEADS
MLA_HEADS = 8
MLA_NOPE = 64
MLA_ROPE = 32
MLA_V = 64
MLA_QK = MLA_NOPE + MLA_ROPE
Q_LORA = 384
KV_LORA = 256
D_FF = 2816
IN_SIZES = (512, 128, 128, Q_LORA, KV_LORA, MLA_ROPE, D_MODEL, D_MODEL)
IN_OFF = tuple(int(v) for v in np.cumsum((0,) + IN_SIZES))
ADAM_LR, ADAM_B1, ADAM_B2, ADAM_EPS, ADAM_WD, ADAM_STEP = 0.001, 0.9, 0.999, 1e-08, 0.01, 10

LANES = 128
SUBLANES = 8
VMEM_LIMIT = 48 * 1024 * 1024
N_DEV = 8
AXES = ("x", "y", "c")

P_GA, P_GB, P_Q, P_QLAT, P_KR, P_K, P_V, P_KVLAT, P_W = 0, 1024, 2048, 3072, 3456, 3584, 3840, 4096, 4352
KR_LANE = 64

LOG2E = 1.4426950408889634

NT = (((1,), (1,)), ((), ()))
NN = (((1,), (0,)), ((), ()))
TN = (((0,), (0,)), ((), ()))


def _cparams(sem):
    return pltpu.CompilerParams(dimension_semantics=sem, vmem_limit_bytes=VMEM_LIMIT)


def _mm(a, b, mode, *, name, out_dtype=F32, add=None, after=None, tm=512, tn=512, tk=None):
    if mode == "nn":
        (M, K), (K2, N) = a.shape, b.shape
    elif mode == "nt":
        (M, K), (N, K2) = a.shape, b.shape
    else:
        (K, M), (K2, N) = a.shape, b.shape
    assert K == K2, (a.shape, b.shape, mode)
    tm, tn, tk = min(tm, M), min(tn, N), K if tk is None else min(tk, K)
    assert M % tm == 0 and N % tn == 0 and K % tk == 0, (M, N, K, tm, tn, tk)
    nk = K // tk
    dn = {"nn": NN, "nt": NT, "tn": TN}[mode]
    if mode == "tn":
        a_spec = pl.BlockSpec((tk, tm), lambda i, j, k: (k, i))
    else:
        a_spec = pl.BlockSpec((tm, tk), lambda i, j, k: (i, k))
    once = dict(pipeline_mode=pl.Buffered(1)) if (nk == 1 and tn == N) else {}
    if mode == "nt":
        b_spec = pl.BlockSpec((tn, tk), lambda i, j, k: (j, k), **once)
    else:
        b_spec = pl.BlockSpec((tk, tn), lambda i, j, k: (k, j), **once)
    o_spec = pl.BlockSpec((tm, tn), lambda i, j, k: (i, j))
    has_add, has_after = add is not None, after is not None

    def body(*refs):
        a_ref, b_ref = refs[0], refs[1]
        add_ref = refs[2] if has_add else None
        o_ref = refs[2 + has_add + has_after]
        p = lax.dot_general(a_ref[...], b_ref[...], dn, preferred_element_type=F32)

        def finish(acc):
            if has_add:
                acc = acc + add_ref[...]
            o_ref[...] = acc.astype(o_ref.dtype)

        if nk == 1:
            finish(p)
        else:
            acc_ref = refs[-1]
            k = pl.program_id(2)

            @pl.when(k == 0)
            def _():
                acc_ref[...] = p

            @pl.when((k > 0) & (k < nk - 1))
            def _():
                acc_ref[...] += p

            @pl.when(k == nk - 1)
            def _():
                finish(acc_ref[...] + p)

    ins = [a, b] + ([add] if has_add else []) + ([after] if has_after else [])
    in_specs = [a_spec, b_spec] + ([o_spec] if has_add else []) + ([pl.BlockSpec(memory_space=pl.ANY)] if has_after else [])
    return pl.pallas_call(
        body, name=name, grid=(M // tm, N // tn, nk), in_specs=in_specs, out_specs=o_spec,
        out_shape=jax.ShapeDtypeStruct((M, N), out_dtype),
        scratch_shapes=[pltpu.VMEM((tm, tn), F32)] if nk > 1 else [],
        compiler_params=_cparams(("parallel", "parallel", "arbitrary")),
    )(*ins)


def _rows(ts, w, cb=0):
    return pl.BlockSpec((ts, w), lambda i: (i, cb))


def _const(r, w):
    return pl.BlockSpec((r, w), lambda i: (0, 0))


def _sublane_sum(v):
    ts, c = v.shape
    return jnp.sum(v.reshape(ts // SUBLANES, SUBLANES, c), axis=0)


def _sigmoid(v):
    return 1.0 / (1.0 + jnp.exp(-v))


def _rope(v, cos, s_up, s_dn, up, dn):
    return v * cos + pltpu.roll(v, up, 1) * s_up + pltpu.roll(v, dn, 1) * s_dn


def _rope_t(dv, cos, s_up, s_dn, up, dn):
    return dv * cos + pltpu.roll(dv * s_up, dn, 1) + pltpu.roll(dv * s_dn, up, 1)


def _rope_tables(seq):
    pos = np.arange(seq, dtype=np.float32)[:, None]

    def base(dim):
        inv = np.float32(ROPE_THETA) ** (-np.arange(0, dim, 2, dtype=np.float32) / np.float32(dim))
        ang = (pos * inv.astype(np.float32)[None, :]).astype(np.float32)
        return np.cos(ang).astype(np.float32), np.sin(ang).astype(np.float32)

    z = lambda n: np.zeros((seq, n), np.float32)
    ca, sa = base(HEAD_DIM)
    a_cos = np.concatenate([ca, ca, z(64)], 1)
    a_up = np.concatenate([-sa, z(96)], 1)
    a_dn = np.concatenate([z(32), sa, z(64)], 1)
    cb, sb = base(MLA_ROPE)
    one = np.ones((seq, 64), np.float32)
    q_cos = np.concatenate([one, cb, cb, z(32)], 1)
    k_cos = np.concatenate([z(64), cb, cb, z(32)], 1)
    b_up = np.concatenate([z(64), -sb, z(48)], 1)
    b_dn = np.concatenate([z(80), sb, z(32)], 1)
    return tuple(jnp.asarray(t) for t in (a_cos, a_up, a_dn, q_cos, k_cos, b_up, b_dn))


def _rms(v, g):
    return v * lax.rsqrt(jnp.mean(v * v, axis=-1, keepdims=True) + EPS) * g


def _rms_bwd(v, g, d):
    r = lax.rsqrt(jnp.mean(v * v, axis=-1, keepdims=True) + EPS)
    xh = v * r
    dxh = d * g
    return r * (dxh - xh * jnp.mean(dxh * xh, axis=-1, keepdims=True)), d * xh


F_GA, F_GB, F_KVLAT, F_QLAT, F_W = 0, 1024, 2048, 2304, 2688


def _proj_in(x, g, w_t, gq, gkv, tabs, *, tm=512):
    s_, c = x.shape
    a_cos, a_up, a_dn, _, k_cos, b_up, b_dn = tabs

    def body(x_ref, g_ref, w_ref, gq_ref, gkv_ref, ac, au, ad, kc, bu, bd,
             h_ref, qa_ref, ka_ref, va_ref, cq_ref, ckv_ref, kro_ref, pf_ref):
        h = _rms(x_ref[...], g_ref[...]).astype(h_ref.dtype)
        h_ref[...] = h
        mm = lambda a, b: lax.dot_general(h, w_ref[a:b, :], NT, preferred_element_type=F32)
        pf_ref[:, F_GA:F_KVLAT] = mm(P_GA, P_Q)
        c_, u_, d_ = ac[...], au[...], ad[...]
        q = mm(P_Q, P_QLAT)
        for hd in range(SWA_HEADS):
            sl = slice(hd * LANES, (hd + 1) * LANES)
            qa_ref[:, sl] = _rope(q[:, sl], c_, u_, d_, 96, 32).astype(qa_ref.dtype)
        kv = mm(P_KR, P_KVLAT)
        kro_ref[...] = _rope(kv[:, :LANES], kc[...], bu[...], bd[...], 112, 16)
        for hd in range(SWA_KV_HEADS):
            sl = slice((1 + hd) * LANES, (2 + hd) * LANES)
            ka_ref[:, hd * LANES:(hd + 1) * LANES] = _rope(kv[:, sl], c_, u_, d_, 96, 32).astype(ka_ref.dtype)
        va_ref[...] = kv[:, P_V - P_KR:].astype(va_ref.dtype)
        for a, b, f0, gref, dst in ((P_QLAT, P_KR, F_QLAT, gq_ref, cq_ref), (P_KVLAT, P_W, F_KVLAT, gkv_ref, ckv_ref)):
            v = mm(a, b)
            pf_ref[:, f0:f0 + b - a] = v
            r = lax.rsqrt(jnp.mean(v * v, axis=-1, keepdims=True) + EPS)
            dst[...] = (v * r * gref[...]).astype(dst.dtype)

    tab = _rows(tm, LANES)
    widths = (c, SWA_HEADS * LANES, SWA_KV_HEADS * LANES, SWA_KV_HEADS * LANES, Q_LORA, KV_LORA)
    return pl.pallas_call(
        body, name="proj_in", grid=(s_ // tm,),
        in_specs=[_rows(tm, c), _const(1, c), pl.BlockSpec((P_W, c), lambda i: (0, 0), pipeline_mode=pl.Buffered(1)),
                  _const(1, Q_LORA), _const(1, KV_LORA), tab, tab, tab, tab, tab, tab],
        out_specs=[_rows(tm, w) for w in widths] + [tab, _rows(tm, F_W)],
        out_shape=[jax.ShapeDtypeStruct((s_, w), MXU_DTYPE) for w in widths]
        + [jax.ShapeDtypeStruct((s_, LANES), F32), jax.ShapeDtypeStruct((s_, F_W), F32)],
        compiler_params=_cparams(("parallel",)),
    )(x, g, w_t, gq, gkv, a_cos, a_up, a_dn, k_cos, b_up, b_dn)


def _mm_norm_bwd(a, b, x, g, res, *, name, after=None, tm=512):
    s_, kk = a.shape
    c = b.shape[1]
    has_after = after is not None

    def body(*refs):
        a_ref, b_ref, x_ref, g_ref, res_ref = refs[:5]
        dx_ref, dxb_ref, dg_ref = refs[5 + has_after:]
        d = jnp.dot(a_ref[...], b_ref[...], preferred_element_type=F32)
        dx, gg = _rms_bwd(x_ref[...], g_ref[...], d)
        dx = dx + res_ref[...]
        dx_ref[...] = dx
        dxb_ref[...] = dx.astype(dxb_ref.dtype)

        @pl.when(pl.program_id(0) == 0)
        def _():
            dg_ref[...] = jnp.zeros(dg_ref.shape, F32)

        dg_ref[...] += _sublane_sum(gg)

    row = _rows(tm, c)
    in_specs = [_rows(tm, kk), pl.BlockSpec((kk, c), lambda i: (0, 0), pipeline_mode=pl.Buffered(1)), row, _const(1, c), row]
    return pl.pallas_call(
        body, name=name, grid=(s_ // tm,), in_specs=in_specs + ([pl.BlockSpec(memory_space=pl.ANY)] if has_after else []),
        out_specs=[row, row, _const(SUBLANES, c)],
        out_shape=[jax.ShapeDtypeStruct((s_, c), F32), jax.ShapeDtypeStruct((s_, c), MXU_DTYPE),
                   jax.ShapeDtypeStruct((SUBLANES, c), F32)],
        compiler_params=_cparams(("arbitrary",)),
    )(*([a, b, x, g, res] + ([after] if has_after else [])))


def _mla_up(cq, ckv, kro, wuq, wuk, wuv, tabs, *, ts=512):
    s_ = cq.shape[0]
    _, _, _, q_cos, _, b_up, b_dn = tabs

    def body(cq_ref, ckv_ref, kr_ref, wq_ref, wk_ref, wv_ref, qc, bu, bd, qo_ref, ko_ref, vo_ref):
        c_, u_, d_ = qc[...], bu[...], bd[...]
        kr = kr_ref[...]
        ckv_ = ckv_ref[...]
        vo_ref[...] = jnp.dot(ckv_, wv_ref[...], preferred_element_type=F32).astype(vo_ref.dtype)
        q = jnp.dot(cq_ref[...], wq_ref[...], preferred_element_type=F32)
        k = jnp.dot(ckv_, wk_ref[...], preferred_element_type=F32)
        for h in range(MLA_HEADS):
            sl = slice(h * LANES, (h + 1) * LANES)
            qo_ref[:, sl] = _rope(q[:, sl], c_, u_, d_, 112, 16).astype(qo_ref.dtype)
            ko_ref[:, sl] = (k[:, sl] + kr).astype(ko_ref.dtype)

    tab, out = _rows(ts, LANES), _rows(ts, 1024)
    return pl.pallas_call(
        body, name="mla_up", grid=(s_ // ts,),
        in_specs=[_rows(ts, Q_LORA), _rows(ts, KV_LORA), tab, _const(Q_LORA, 1024), _const(KV_LORA, 1024),
                  _const(KV_LORA, 1024), tab, tab, tab],
        out_specs=[out, out, out], out_shape=[jax.ShapeDtypeStruct((s_, 1024), MXU_DTYPE)] * 3,
        compiler_params=_cparams(("parallel",)),
    )(cq, ckv, kro, wuq, wuk, wuv, q_cos, b_up, b_dn)


def _mla_up_bwd(dqc, dkc, dvp, wuq, wukv, p, gq, gkv, tabs, *, ts=256):
    s_ = dqc.shape[0]
    _, _, _, q_cos, k_cos, b_up, b_dn = tabs

    def body(dq_ref, dk_ref, dv_ref, wq_ref, wkv_ref, ql_ref, kvl_ref, gq_ref, gkv_ref, qc, kc, bu, bd,
             dqo_ref, dkvo_ref, dkr_ref, dql_ref, dkvl_ref, dgq_ref, dgkv_ref):
        c_, u_, d_ = qc[...], bu[...], bd[...]
        tot = jnp.zeros((ts, LANES), F32)
        for h in range(MLA_HEADS):
            sl = slice(h * LANES, (h + 1) * LANES)
            dqo_ref[:, sl] = _rope_t(dq_ref[:, sl], c_, u_, d_, 112, 16).astype(dqo_ref.dtype)
            dk = dk_ref[:, sl]
            dkvo_ref[:, sl] = dk.astype(dkvo_ref.dtype)
            tot = tot + dk
        dkvo_ref[:, 1024:2048] = dv_ref[...].astype(dkvo_ref.dtype)
        dkr_ref[...] = _rope_t(tot, kc[...], u_, d_, 112, 16).astype(dkr_ref.dtype)

        @pl.when(pl.program_id(0) == 0)
        def _():
            dgq_ref[...] = jnp.zeros(dgq_ref.shape, F32)
            dgkv_ref[...] = jnp.zeros(dgkv_ref.shape, F32)

        for do_ref, w_ref, x_ref, g_ref, dx_ref, dg_ref in ((dqo_ref, wq_ref, ql_ref, gq_ref, dql_ref, dgq_ref),
                                                            (dkvo_ref, wkv_ref, kvl_ref, gkv_ref, dkvl_ref, dgkv_ref)):
            d = lax.dot_general(do_ref[...], w_ref[...], NT, preferred_element_type=F32)
            dx, gg = _rms_bwd(x_ref[...], g_ref[...], d)
            dx_ref[...] = dx.astype(dx_ref.dtype)
            dg_ref[...] += _sublane_sum(gg)

    tab = _rows(ts, LANES)
    return pl.pallas_call(
        body, name="mla_up_bwd", grid=(s_ // ts,),
        in_specs=[_rows(ts, 1024), _rows(ts, 1024), _rows(ts, 1024), _const(Q_LORA, 1024), _const(KV_LORA, 2048),
                  _rows(ts, Q_LORA, F_QLAT // Q_LORA), _rows(ts, KV_LORA, F_KVLAT // KV_LORA),
                  _const(1, Q_LORA), _const(1, KV_LORA), tab, tab, tab, tab],
        out_specs=[_rows(ts, 1024), _rows(ts, 2048), _rows(ts, LANES), _rows(ts, Q_LORA), _rows(ts, KV_LORA),
                   _const(SUBLANES, Q_LORA), _const(SUBLANES, KV_LORA)],
        out_shape=[jax.ShapeDtypeStruct((s_, 1024), MXU_DTYPE), jax.ShapeDtypeStruct((s_, 2048), MXU_DTYPE),
                   jax.ShapeDtypeStruct((s_, LANES), MXU_DTYPE), jax.ShapeDtypeStruct((s_, Q_LORA), MXU_DTYPE),
                   jax.ShapeDtypeStruct((s_, KV_LORA), MXU_DTYPE), jax.ShapeDtypeStruct((SUBLANES, Q_LORA), F32),
                   jax.ShapeDtypeStruct((SUBLANES, KV_LORA), F32)],
        compiler_params=_cparams(("arbitrary",)),
    )(dqc, dkc, dvp, wuq, wukv, p, p, gq, gkv, q_cos, k_cos, b_up, b_dn)


def _assemble_dp(dgab, dqa, dqlat, dkr, dka, dva, dkvlat, tabs, *, ts=256):
    s_ = dqa.shape[0]
    a_cos, a_up, a_dn = tabs[0], tabs[1], tabs[2]

    def body(dg_ref, dq_ref, dql_ref, dkr_ref, dk_ref, dv_ref, dkvl_ref, ac, au, ad, o_ref):
        c_, u_, d_ = ac[...], au[...], ad[...]
        o_ref[:, P_GA:P_Q] = dg_ref[...]
        for h in range(SWA_HEADS):
            sl = slice(h * LANES, (h + 1) * LANES)
            o_ref[:, P_Q + h * LANES:P_Q + (h + 1) * LANES] = _rope_t(dq_ref[:, sl], c_, u_, d_, 96, 32).astype(o_ref.dtype)
        o_ref[:, P_QLAT:P_KR] = dql_ref[...]
        o_ref[:, P_KR:P_K] = dkr_ref[...]
        for h in range(SWA_KV_HEADS):
            sl = slice(h * LANES, (h + 1) * LANES)
            o_ref[:, P_K + h * LANES:P_K + (h + 1) * LANES] = _rope_t(dk_ref[:, sl], c_, u_, d_, 96, 32).astype(o_ref.dtype)
        o_ref[:, P_V:P_KVLAT] = dv_ref[...]
        o_ref[:, P_KVLAT:P_W] = dkvl_ref[...]

    tab = _rows(ts, LANES)
    return pl.pallas_call(
        body, name="assemble_dp", grid=(s_ // ts,),
        in_specs=[_rows(ts, 2048), _rows(ts, 1024), _rows(ts, Q_LORA), _rows(ts, LANES), _rows(ts, 256), _rows(ts, 256),
                  _rows(ts, KV_LORA), tab, tab, tab],
        out_specs=_rows(ts, P_W), out_shape=jax.ShapeDtypeStruct((s_, P_W), MXU_DTYPE),
        compiler_params=_cparams(("parallel",)),
    )(dgab, dqa, dqlat, dkr, dka, dva, dkvlat, a_cos, a_up, a_dn)


def _attn_out_gate(oa, ob, woa_t, wob_t, p, *, ts=512):
    s_ = p.shape[0]

    def body(oa_ref, ob_ref, wa_ref, wb_ref, ga_ref, gb_ref, ta_ref, tb_ref, y_ref):
        ta = lax.dot_general(oa_ref[...], wa_ref[...], NT, preferred_element_type=F32)
        tb = lax.dot_general(ob_ref[...], wb_ref[...], NT, preferred_element_type=F32)
        ta_ref[...] = ta
        tb_ref[...] = tb
        y_ref[...] = (_sigmoid(ga_ref[...]) * ta + _sigmoid(gb_ref[...]) * tb).astype(y_ref.dtype)

    w = _const(1024, 1024)
    return pl.pallas_call(
        body, name="attn_out_gate", grid=(s_ // ts,),
        in_specs=[_rows(ts, 1024), _rows(ts, 1024), w, w, _rows(ts, 1024, F_GA // 1024), _rows(ts, 1024, F_GB // 1024)],
        out_specs=[_rows(ts, 1024)] * 3,
        out_shape=[jax.ShapeDtypeStruct((s_, 1024), F32)] * 2 + [jax.ShapeDtypeStruct((s_, 1024), MXU_DTYPE)],
        compiler_params=_cparams(("parallel",)),
    )(oa, ob, woa_t, wob_t, p, p)


def _d_y_gate(dx1b, wout, p, ta, tb, *, ts=512):
    s_ = p.shape[0]

    def body(dx_ref, w_ref, ga_ref, gb_ref, ta_ref, tb_ref, dta_ref, dtb_ref, dg_ref):
        d = lax.dot_general(dx_ref[...], w_ref[...], NT, preferred_element_type=F32)
        sa, sb = _sigmoid(ga_ref[...]), _sigmoid(gb_ref[...])
        dta_ref[...] = (d * sa).astype(dta_ref.dtype)
        dtb_ref[...] = (d * sb).astype(dtb_ref.dtype)
        dg_ref[:, 0:1024] = (d * ta_ref[...] * (sa * (1.0 - sa))).astype(dg_ref.dtype)
        dg_ref[:, 1024:2048] = (d * tb_ref[...] * (sb * (1.0 - sb))).astype(dg_ref.dtype)

    return pl.pallas_call(
        body, name="d_y_gate", grid=(s_ // ts,),
        in_specs=[_rows(ts, 1024), _const(1024, 1024), _rows(ts, 1024, F_GA // 1024), _rows(ts, 1024, F_GB // 1024),
                  _rows(ts, 1024), _rows(ts, 1024)],
        out_specs=[_rows(ts, 1024), _rows(ts, 1024), _rows(ts, 2048)],
        out_shape=[jax.ShapeDtypeStruct((s_, 1024), MXU_DTYPE)] * 2 + [jax.ShapeDtypeStruct((s_, 2048), MXU_DTYPE)],
        compiler_params=_cparams(("parallel",)),
    )(dx1b, wout, p, p, ta, tb)


FF_TILE = D_FF // 2


def _ffn_in_act(x1, g, wgu_t, *, tm=512):
    s_ = x1.shape[0]
    n = s_ // tm

    def body(x_ref, g_ref, w_ref, h_ref, gu_ref, a_ref):
        h = _rms(x_ref[...], g_ref[...]).astype(h_ref.dtype)
        h_ref[...] = h
        p = lax.dot_general(h, w_ref[...], NT, preferred_element_type=F32)
        gu_ref[...] = p
        gate = p[:, :FF_TILE]
        a_ref[...] = (gate * _sigmoid(gate) * p[:, FF_TILE:]).astype(a_ref.dtype)

    return pl.pallas_call(
        body, name="ffn_in", grid=(2, s_ // tm),
        in_specs=[pl.BlockSpec((tm, D_MODEL), lambda j, i: (i, 0)), pl.BlockSpec((1, D_MODEL), lambda j, i: (0, 0)),
                  pl.BlockSpec((2 * FF_TILE, D_MODEL), lambda j, i: (j, 0))],
        out_specs=[pl.BlockSpec((tm, D_MODEL), lambda j, i: (i + j * (n - 1 - i), 0)),
                   pl.BlockSpec((tm, 2 * FF_TILE), lambda j, i: (i, j)),
                   pl.BlockSpec((tm, FF_TILE), lambda j, i: (i, j))],
        out_shape=[jax.ShapeDtypeStruct((s_, D_MODEL), MXU_DTYPE), jax.ShapeDtypeStruct((s_, 2 * D_FF), F32),
                   jax.ShapeDtypeStruct((s_, D_FF), MXU_DTYPE)],
        compiler_params=_cparams(("arbitrary", "arbitrary")),
    )(x1, g, wgu_t)


def _d_act_swiglu(dx2b, wd, gu, *, tm=512):
    s_ = dx2b.shape[0]

    def body(d_ref, w_ref, gu_ref, o_ref):
        da = lax.dot_general(d_ref[...], w_ref[...], NT, preferred_element_type=F32)
        g, u = gu_ref[:, :FF_TILE], gu_ref[:, FF_TILE:]
        sg = _sigmoid(g)
        o_ref[:, :FF_TILE] = (da * u * (sg * (1.0 + g * (1.0 - sg)))).astype(o_ref.dtype)
        o_ref[:, FF_TILE:] = (da * (g * sg)).astype(o_ref.dtype)

    gu_spec = pl.BlockSpec((tm, 2 * FF_TILE), lambda j, i: (i, j))
    return pl.pallas_call(
        body, name="d_act", grid=(2, s_ // tm),
        in_specs=[pl.BlockSpec((tm, D_MODEL), lambda j, i: (i, 0)), pl.BlockSpec((FF_TILE, D_MODEL), lambda j, i: (j, 0)), gu_spec],
        out_specs=gu_spec, out_shape=jax.ShapeDtypeStruct((s_, 2 * D_FF), MXU_DTYPE),
        compiler_params=_cparams(("parallel", "parallel")),
    )(dx2b, wd, gu)


def _ffn_out_loss(act, wd, x1, g, tgt, *, ts=512):
    s_, c = x1.shape
    kk = act.shape[1]

    def body(a_ref, w_ref, x_ref, g_ref, t_ref, dx_ref, dxb_ref, dg_ref, lp_ref, tot_ref):
        v = x_ref[...] + jnp.dot(a_ref[...], w_ref[...], preferred_element_type=F32)
        r = lax.rsqrt(jnp.mean(v * v, axis=-1, keepdims=True) + EPS)
        xh = v * r
        gg = g_ref[...]
        e = xh * gg - t_ref[...]
        do = e * (1.0 / c)
        dxh = do * gg
        dx = r * (dxh - xh * jnp.mean(dxh * xh, axis=-1, keepdims=True))
        dx_ref[...] = dx
        dxb_ref[...] = dx.astype(dxb_ref.dtype)
        i = pl.program_id(0)

        @pl.when(i == 0)
        def _():
            dg_ref[...] = jnp.zeros(dg_ref.shape, F32)
            lp_ref[...] = jnp.zeros(lp_ref.shape, F32)

        dg_ref[...] += _sublane_sum(do * xh)
        lp_ref[...] += _sublane_sum(e * e)
        tot_ref[...] = jnp.full(tot_ref.shape, (0.5 / c) * jnp.sum(lp_ref[...]), F32)

    return pl.pallas_call(
        body, name="ffn_out_loss", grid=(s_ // ts,),
        in_specs=[_rows(ts, kk), _const(kk, c), _rows(ts, c), _const(1, c), _rows(ts, c)],
        out_specs=[_rows(ts, c), _rows(ts, c), _const(SUBLANES, c), _const(SUBLANES, c), _const(SUBLANES, LANES)],
        out_shape=[jax.ShapeDtypeStruct((s_, c), F32), jax.ShapeDtypeStruct((s_, c), MXU_DTYPE),
                   jax.ShapeDtypeStruct((SUBLANES, c), F32), jax.ShapeDtypeStruct((SUBLANES, c), F32),
                   jax.ShapeDtypeStruct((SUBLANES, LANES), F32)],
        compiler_params=_cparams(("arbitrary",)),
    )(act, wd, x1, g, tgt)


def _mla_d_out(dtb, wob_t, o32, *, ts=512):
    s_ = dtb.shape[0]

    def body(dt_ref, w_ref, o_ref, dob_ref, dl_ref):
        d = jnp.dot(dt_ref[...], w_ref[...], preferred_element_type=F32)
        dob_ref[...] = d.astype(dob_ref.dtype)
        prod = d * o_ref[...]
        for h in range(MLA_HEADS):
            dl_ref[h] = jnp.sum(prod[:, h * LANES:(h + 1) * LANES].T, axis=0, keepdims=True)

    return pl.pallas_call(
        body, name="mla_d_out", grid=(s_ // ts,), in_specs=[_rows(ts, 1024), _const(1024, 1024), _rows(ts, 1024)],
        out_specs=[_rows(ts, 1024), pl.BlockSpec((MLA_HEADS, 1, ts), lambda i: (0, 0, i))],
        out_shape=[jax.ShapeDtypeStruct((s_, 1024), MXU_DTYPE), jax.ShapeDtypeStruct((MLA_HEADS, 1, s_), F32)],
        compiler_params=_cparams(("parallel",)),
    )(dtb, wob_t, o32)


SWA_T = 4 * BLOCK


SWA_W = SWA_GROUP * BLOCK


def _swa_masks(sb):
    kr = lax.broadcasted_iota(jnp.int32, (2 * BLOCK, SWA_W), 0)
    qc = jnp.bitwise_and(lax.broadcasted_iota(jnp.int32, (2 * BLOCK, SWA_W), 1), BLOCK - 1)
    band = jnp.logical_and(kr > qc, kr <= qc + BLOCK)
    first = jnp.logical_and(band, kr >= BLOCK)
    return band, jnp.logical_or(first, jnp.logical_and(band, sb > 0))


def _heads_to_rows(ref, rs):
    return jnp.concatenate([ref[rs, h * LANES:(h + 1) * LANES] for h in range(SWA_GROUP)], axis=0)


def _sink_row(sk_ref):
    return jnp.concatenate([sk_ref[0, h:h + 1, :] for h in range(SWA_GROUP)], axis=1) * LOG2E


def _swa_in_specs(rev, nsb):
    sbi = (lambda j: nsb - 1 - j) if rev else (lambda j: j)
    cur = pl.BlockSpec((SWA_T, LANES), lambda g, j: (sbi(j), g))
    prev = pl.BlockSpec((BLOCK, LANES), lambda g, j: (jnp.maximum(4 * sbi(j) - 1, 0), g))
    q = pl.BlockSpec((SWA_T, SWA_GROUP * LANES), lambda g, j: (sbi(j), g))
    sink = pl.BlockSpec((1, SUBLANES, LANES), lambda g, j: (g, 0, 0))
    lse = pl.BlockSpec((SWA_GROUP, 1, SWA_T), lambda g, j: (g, 0, sbi(j)))
    return q, cur, prev, sink, lse


def _swa_fwd(qa, ka, va, sink_b):
    s_ = qa.shape[0]
    nsb = s_ // SWA_T
    c2 = HEAD_DIM ** -0.5 * LOG2E

    def body(q_ref, kc_ref, kp_ref, vc_ref, vp_ref, sk_ref, o32_ref, o16_ref, lse_ref, kx, vx):
        kx[0:BLOCK, :] = kp_ref[...]
        kx[BLOCK:5 * BLOCK, :] = kc_ref[...]
        vx[0:BLOCK, :] = vp_ref[...]
        vx[BLOCK:5 * BLOCK, :] = vc_ref[...]
        band, band0 = _swa_masks(pl.program_id(1))
        sink2 = _sink_row(sk_ref)
        for b in range(4):
            rs = slice(b * BLOCK, (b + 1) * BLOCK)
            ks = slice(b * BLOCK, (b + 2) * BLOCK)
            st = lax.dot_general(kx[ks, :], _heads_to_rows(q_ref, rs), NT, preferred_element_type=F32) * c2
            st = jnp.where(band0 if b == 0 else band, st, -jnp.inf)
            m = jnp.maximum(jnp.max(st, axis=0, keepdims=True), sink2)
            pt = jnp.exp2(st - m)
            den = jnp.sum(pt, axis=0, keepdims=True) + jnp.exp2(sink2 - m)
            o = lax.dot_general((pt * (1.0 / den)).astype(MXU_DTYPE), vx[ks, :], TN, preferred_element_type=F32)
            lse = m + jnp.log2(den)
            for hh in range(SWA_GROUP):
                cs = slice(hh * LANES, (hh + 1) * LANES)
                o32_ref[rs, cs] = o[cs, :]
                o16_ref[rs, cs] = o[cs, :].astype(o16_ref.dtype)
                lse_ref[hh, :, rs] = lse[:, cs]

    q, cur, prev, sink, lse_spec = _swa_in_specs(False, nsb)
    return pl.pallas_call(
        body, name="swa_fwd", grid=(SWA_KV_HEADS, nsb), in_specs=[q, cur, prev, cur, prev, sink],
        out_specs=[q, q, lse_spec],
        out_shape=[jax.ShapeDtypeStruct((s_, SWA_HEADS * LANES), F32), jax.ShapeDtypeStruct((s_, SWA_HEADS * LANES), MXU_DTYPE),
                   jax.ShapeDtypeStruct((SWA_HEADS, 1, s_), F32)],
        scratch_shapes=[pltpu.VMEM((5 * BLOCK, LANES), MXU_DTYPE), pltpu.VMEM((5 * BLOCK, LANES), MXU_DTYPE)],
        compiler_params=_cparams(("parallel", "arbitrary")),
    )(qa, ka, ka, va, va, sink_b)


def _swa_bwd(qa, ka, va, sink_b, o32, do, lse):
    s_ = qa.shape[0]
    nsb = s_ // SWA_T
    scale = HEAD_DIM ** -0.5
    c2 = scale * LOG2E

    def body(q_ref, kc_ref, kp_ref, vc_ref, vp_ref, sk_ref, o_ref, do_ref, lse_ref,
             dq_ref, dk_ref, dv_ref, dsk_ref, kx, vx, kacc, vacc, kcar, vcar):
        j = pl.program_id(1)
        kx[0:BLOCK, :] = kp_ref[...]
        kx[BLOCK:5 * BLOCK, :] = kc_ref[...]
        vx[0:BLOCK, :] = vp_ref[...]
        vx[BLOCK:5 * BLOCK, :] = vc_ref[...]
        band, band0 = _swa_masks(nsb - 1 - j)
        kacc[...] = jnp.zeros(kacc.shape, F32)
        vacc[...] = jnp.zeros(vacc.shape, F32)

        @pl.when(j == 0)
        def _():
            kcar[...] = jnp.zeros(kcar.shape, F32)
            vcar[...] = jnp.zeros(vcar.shape, F32)
            dsk_ref[...] = jnp.zeros(dsk_ref.shape, F32)

        sink2 = _sink_row(sk_ref)
        dsink = jnp.zeros((1, SWA_W), F32)
        for b in range(4):
            rs = slice(b * BLOCK, (b + 1) * BLOCK)
            ks = slice(b * BLOCK, (b + 2) * BLOCK)
            q, k2, v2 = _heads_to_rows(q_ref, rs), kx[ks, :], vx[ks, :]
            d = _heads_to_rows(do_ref, rs)
            delta = jnp.sum((d * _heads_to_rows(o_ref, rs)).T, axis=0, keepdims=True)
            l2 = jnp.concatenate([lse_ref[hh, :, rs] for hh in range(SWA_GROUP)], axis=1)
            st = lax.dot_general(k2, q, NT, preferred_element_type=F32) * c2
            pt = jnp.exp2(jnp.where(band0 if b == 0 else band, st, -jnp.inf) - l2)
            db = d.astype(MXU_DTYPE)
            dst = (pt * (lax.dot_general(v2, db, NT, preferred_element_type=F32) - delta) * scale).astype(MXU_DTYPE)
            dq = lax.dot_general(dst, k2, TN, preferred_element_type=F32)
            for hh in range(SWA_GROUP):
                dq_ref[rs, hh * LANES:(hh + 1) * LANES] = dq[hh * LANES:(hh + 1) * LANES, :]
            kacc[ks, :] += jnp.dot(dst, q, preferred_element_type=F32)
            vacc[ks, :] += jnp.dot(pt.astype(MXU_DTYPE), db, preferred_element_type=F32)
            dsink = dsink - jnp.exp2(sink2 - l2) * delta
        for hh in range(SWA_GROUP):
            tot = jnp.sum(dsink[:, hh * LANES:(hh + 1) * LANES], axis=1, keepdims=True)
            dsk_ref[0, hh:hh + 1, :] += jnp.broadcast_to(tot, (1, LANES))

        dk_ref[0:3 * BLOCK, :] = kacc[BLOCK:4 * BLOCK, :]
        dk_ref[3 * BLOCK:4 * BLOCK, :] = kacc[4 * BLOCK:5 * BLOCK, :] + kcar[...]
        dv_ref[0:3 * BLOCK, :] = vacc[BLOCK:4 * BLOCK, :].astype(dv_ref.dtype)
        dv_ref[3 * BLOCK:4 * BLOCK, :] = (vacc[4 * BLOCK:5 * BLOCK, :] + vcar[...]).astype(dv_ref.dtype)
        kcar[...] = kacc[0:BLOCK, :]
        vcar[...] = vacc[0:BLOCK, :]

    q, cur, prev, sink, lse_spec = _swa_in_specs(True, nsb)
    return pl.pallas_call(
        body, name="swa_bwd", grid=(SWA_KV_HEADS, nsb),
        in_specs=[q, cur, prev, cur, prev, sink, q, q, lse_spec],
        out_specs=[q, cur, cur, sink],
        out_shape=[jax.ShapeDtypeStruct((s_, SWA_HEADS * LANES), F32), jax.ShapeDtypeStruct((s_, SWA_KV_HEADS * LANES), F32),
                   jax.ShapeDtypeStruct((s_, SWA_KV_HEADS * LANES), MXU_DTYPE),
                   jax.ShapeDtypeStruct((SWA_KV_HEADS, SUBLANES, LANES), F32)],
        scratch_shapes=[pltpu.VMEM((5 * BLOCK, LANES), MXU_DTYPE), pltpu.VMEM((5 * BLOCK, LANES), MXU_DTYPE),
                        pltpu.VMEM((5 * BLOCK, LANES), F32), pltpu.VMEM((5 * BLOCK, LANES), F32),
                        pltpu.VMEM((BLOCK, LANES), F32), pltpu.VMEM((BLOCK, LANES), F32)],
        compiler_params=_cparams(("arbitrary", "arbitrary")),
    )(qa, ka, ka, va, va, sink_b, o32, do, lse)


MLA_T = 512
MLA_FWD_GROUP = 4
MLA_BWD_GROUP = 2


def _mla_specs(s_, t, group):
    w = group * LANES
    qs = pl.BlockSpec((t, w), lambda g, i: (i, g))
    kv = pl.BlockSpec((s_, w), lambda g, i: (0, g))
    row = pl.BlockSpec((group, 1, t), lambda g, i: (g, 0, i))
    return qs, kv, row


def _causal_scores_t(k, q, t, c2, masked):
    st = lax.dot_general(k, q, NT, preferred_element_type=F32) * c2
    if masked:
        kr = lax.broadcasted_iota(jnp.int32, (t, t), 0)
        qc = lax.broadcasted_iota(jnp.int32, (t, t), 1)
        st = jnp.where(kr <= qc, st, -jnp.inf)
    return st


def _mla_fwd(qc, kc, vp):
    s_ = qc.shape[0]
    t = min(MLA_T, s_)
    c2 = MLA_QK ** -0.5 * LOG2E
    grp = MLA_FWD_GROUP

    def body(q_ref, k_ref, v_ref, o32_ref, o16_ref, lse_ref, m_s, acc_s):
        qi = pl.program_id(1)
        m_s[...] = jnp.full(m_s.shape, -jnp.inf, F32)
        acc_s[...] = jnp.zeros(acc_s.shape, F32)
        ones_lane = lax.broadcasted_iota(jnp.int32, (t, LANES), 1) == MLA_V

        def step(ki, masked):
            off = pl.multiple_of(ki * t, t)
            for g in range(grp):
                cs = slice(g * LANES, (g + 1) * LANES)
                st = _causal_scores_t(k_ref[pl.ds(off, t), cs], q_ref[:, cs], t, c2, masked)
                m_old = m_s[g]
                m_new = jnp.maximum(m_old, jnp.max(st, axis=0, keepdims=True))
                alpha = jnp.exp2(m_old - m_new)
                pt = jnp.exp2(st - m_new).astype(MXU_DTYPE)
                v = v_ref[pl.ds(off, t), cs]
                v = jnp.where(ones_lane, jnp.ones((), v.dtype), v)
                acc_s[g] = alpha * acc_s[g] + lax.dot_general(v, pt, TN, preferred_element_type=F32)
                m_s[g] = m_new

        def full_block(ki, carry):
            step(ki, False)
            return carry

        lax.fori_loop(0, qi, full_block, 0)
        step(qi, True)
        for g in range(grp):
            cs = slice(g * LANES, (g + 1) * LANES)
            acc = acc_s[g]
            l = acc[MLA_V:MLA_V + 1, :]
            o = (acc * (1.0 / l)).T
            o32_ref[:, cs] = o
            o16_ref[:, cs] = o.astype(o16_ref.dtype)
            lse_ref[g] = m_s[g] + jnp.log2(l)

    qs, kv, row = _mla_specs(s_, t, grp)
    return pl.pallas_call(
        body, name="mla_fwd", grid=(MLA_HEADS // grp, s_ // t), in_specs=[qs, kv, kv], out_specs=[qs, qs, row],
        out_shape=[jax.ShapeDtypeStruct((s_, MLA_HEADS * LANES), F32), jax.ShapeDtypeStruct((s_, MLA_HEADS * LANES), MXU_DTYPE),
                   jax.ShapeDtypeStruct((MLA_HEADS, 1, s_), F32)],
        scratch_shapes=[pltpu.VMEM((grp, 1, t), F32), pltpu.VMEM((grp, LANES, t), F32)],
        compiler_params=_cparams(("parallel", "arbitrary")),
    )(qc, kc, vp)


def _mla_bwd(qc, kc, vp, dob, lse, delta):
    s_ = qc.shape[0]
    t = min(MLA_T, s_)
    scale = MLA_QK ** -0.5
    c2 = scale * LOG2E
    grp = MLA_BWD_GROUP

    def body(q_ref, do_ref, lse_ref, dl_ref, k_ref, v_ref, dq_ref, dk_ref, dv_ref, dqt_s):
        qi = pl.program_id(1)

        @pl.when(qi == 0)
        def _():
            dk_ref[...] = jnp.zeros(dk_ref.shape, F32)
            dv_ref[...] = jnp.zeros(dv_ref.shape, F32)

        dqt_s[...] = jnp.zeros(dqt_s.shape, F32)

        def step(ki, masked):
            off = pl.multiple_of(ki * t, t)
            for g in range(grp):
                cs = slice(g * LANES, (g + 1) * LANES)
                q, d, k = q_ref[:, cs], do_ref[:, cs], k_ref[pl.ds(off, t), cs]
                pt = jnp.exp2(_causal_scores_t(k, q, t, c2, masked) - lse_ref[g])
                dpt = lax.dot_general(v_ref[pl.ds(off, t), cs], d, NT, preferred_element_type=F32)
                dst = (pt * (dpt - dl_ref[g]) * scale).astype(MXU_DTYPE)
                dv_ref[pl.ds(off, t), cs] += jnp.dot(pt.astype(MXU_DTYPE), d, preferred_element_type=F32)
                dk_ref[pl.ds(off, t), cs] += jnp.dot(dst, q, preferred_element_type=F32)
                dqt_s[g] += lax.dot_general(k, dst, TN, preferred_element_type=F32)

        def full_block(ki, carry):
            step(ki, False)
            return carry

        lax.fori_loop(0, qi, full_block, 0)
        step(qi, True)
        for g in range(grp):
            dq_ref[:, g * LANES:(g + 1) * LANES] = dqt_s[g].T

    qs, kv, row = _mla_specs(s_, t, grp)
    shp = jax.ShapeDtypeStruct((s_, MLA_HEADS * LANES), F32)
    return pl.pallas_call(
        body, name="mla_bwd", grid=(MLA_HEADS // grp, s_ // t), in_specs=[qs, qs, row, row, kv, kv],
        out_specs=[qs, kv, kv], out_shape=[shp, shp, shp], scratch_shapes=[pltpu.VMEM((grp, LANES, t), F32)],
        compiler_params=_cparams(("parallel", "arbitrary")),
    )(qc, dob, lse, delta, kc, vp)


def _pad_heads(w, nh, hd, axis):
    shp = w.shape
    w = w.reshape(shp[:axis] + (nh, hd) + shp[axis + 1:])
    pad = [(0, 0)] * w.ndim
    pad[axis + 1] = (0, LANES - hd)
    w = jnp.pad(w, pad)
    return w.reshape(shp[:axis] + (nh * LANES,) + shp[axis + 1:])


def _unpad_heads(w, nh, hd, axis):
    shp = w.shape
    w = w.reshape(shp[:axis] + (nh, LANES) + shp[axis + 1:])
    w = lax.slice_in_dim(w, 0, hd, axis=axis + 1)
    return w.reshape(shp[:axis] + (nh * hd,) + shp[axis + 1:])


PACK_W = 1024
ROW_TILE = 16
FULL_SHAPE = dict(w_in=(1024, 3488), w_uq=(384, 768), w_ukv=(256, 1024), w_o_swa=(512, 1024), w_o_mla=(512, 1024),
                  w_out=(1024, 1024), w_gate=(1024, 2816), w_up=(1024, 2816), w_down=(2816, 1024))
BIG = tuple(FULL_SHAPE)
ROW_SHARDED = ("w_out", "w_down")
W_IN_COLS = FULL_SHAPE["w_in"][1] // N_DEV
W_IN_ROWS = -(-W_IN_COLS // ROW_TILE) * ROW_TILE
FF_COLS = D_FF // N_DEV
OUT_ROWS = D_MODEL // N_DEV
SMALL_FLAT = (("w_uq", 0, 36), ("w_ukv", 48, 32))
SMALL_USED = 80
MID_BLOCKS = 4
MID_ROWS = MID_BLOCKS * OUT_ROWS
EARLY_ROWS = W_IN_ROWS + MID_ROWS
LATE_ROWS = 3 * FF_COLS
PACK_ROWS = EARLY_ROWS + LATE_ROWS


def _shard_shape(n):
    r, c = FULL_SHAPE[n]
    return (r // N_DEV, c) if n in ROW_SHARDED else (r, c // N_DEV)


def _wire_pack(sh, dtype):
    c = lambda n: sh[n].astype(dtype)
    rows = [jnp.pad(c("w_in").T, ((0, W_IN_ROWS - W_IN_COLS), (0, 0))), c("w_out"),
            _pad_heads(c("w_o_swa").T, SWA_HEADS, HEAD_DIM, 1), _pad_heads(c("w_o_mla").T, MLA_HEADS, MLA_V, 1)]
    for n, _, r in SMALL_FLAT:
        rows.append(jnp.pad(c(n).reshape(r, PACK_W), ((0, -r % ROW_TILE), (0, 0))))
    rows.append(jnp.zeros((OUT_ROWS - SMALL_USED, PACK_W), dtype))
    return jnp.concatenate(rows + [c("w_gate").T, c("w_up").T, c("w_down")], 0)


def _mid_unpack(p):
    out = dict(w_out=p[0:OUT_ROWS], w_o_swa=_unpad_heads(p[OUT_ROWS:2 * OUT_ROWS], SWA_HEADS, HEAD_DIM, 1).T,
               w_o_mla=_unpad_heads(p[2 * OUT_ROWS:3 * OUT_ROWS], MLA_HEADS, MLA_V, 1).T)
    for n, off, r in SMALL_FLAT:
        out[n] = p[3 * OUT_ROWS + off:3 * OUT_ROWS + off + r].reshape(_shard_shape(n))
    return out


def _w_in_row_maps():
    sp = lambda col: (col // W_IN_COLS) * W_IN_ROWS + col % W_IN_COLS
    fwd = np.full((P_W,), -1, np.int64)

    def put(t0, c0, n):
        fwd[t0:t0 + n] = [sp(c) for c in range(c0, c0 + n)]

    put(P_GA, IN_OFF[6], D_MODEL)
    put(P_GB, IN_OFF[7], D_MODEL)
    for h in range(SWA_HEADS):
        put(P_Q + LANES * h, IN_OFF[0] + HEAD_DIM * h, HEAD_DIM)
    put(P_QLAT, IN_OFF[3], Q_LORA)
    put(P_KR + KR_LANE, IN_OFF[5], MLA_ROPE)
    for h in range(SWA_KV_HEADS):
        put(P_K + LANES * h, IN_OFF[1] + HEAD_DIM * h, HEAD_DIM)
        put(P_V + LANES * h, IN_OFF[2] + HEAD_DIM * h, HEAD_DIM)
    put(P_KVLAT, IN_OFF[4], KV_LORA)
    inv = np.full((N_DEV * W_IN_ROWS,), -1, np.int64)
    inv[fwd[fwd >= 0]] = np.nonzero(fwd >= 0)[0]
    return fwd, inv


def _take_rows(src, idx, *, name, tile=2 * LANES):
    n_out, n_src, width = len(idx), src.shape[0], src.shape[1]
    assert n_out % tile == 0 and n_src % tile == 0
    n_tiles = n_out // tile
    blocks = [sorted({int(v) // tile for v in idx[i * tile:(i + 1) * tile] if v >= 0}) for i in range(n_tiles)]
    k_max = max(1, max(len(b) for b in blocks))
    tab = np.zeros((n_tiles, k_max), np.int32)
    sel = np.zeros((n_tiles, k_max, tile, tile), np.float32)
    for i, blks in enumerate(blocks):
        for m, b in enumerate(blks):
            tab[i, m] = b
            for r in range(tile):
                v = int(idx[i * tile + r])
                if v >= 0 and v // tile == b:
                    sel[i, m, r, v % tile] = 1.0

    def body(tab_ref, sel_ref, *refs):
        o_ref = refs[k_max]
        acc = jnp.dot(sel_ref[0, 0], refs[0][...], preferred_element_type=F32)
        for m in range(1, k_max):
            acc = acc + jnp.dot(sel_ref[0, m], refs[m][...], preferred_element_type=F32)
        o_ref[...] = acc.astype(o_ref.dtype)

    def src_spec(m):
        return pl.BlockSpec((tile, width), lambda i, t: (t[i * k_max + m], 0))

    return pl.pallas_call(
        body, name=name,
        grid_spec=pltpu.PrefetchScalarGridSpec(
            num_scalar_prefetch=1, grid=(n_tiles,),
            in_specs=[pl.BlockSpec((1, k_max, tile, tile), lambda i, t: (i, 0, 0, 0))] + [src_spec(m) for m in range(k_max)],
            out_specs=pl.BlockSpec((tile, width), lambda i, t: (i, 0))),
        out_shape=jax.ShapeDtypeStruct((n_out, width), src.dtype),
        compiler_params=_cparams(("parallel",)),
    )(jnp.asarray(tab.reshape(-1)), jnp.asarray(sel, src.dtype), *([src] * k_max))


def _w_in_operand(win_g):
    return _take_rows(win_g.reshape(N_DEV * W_IN_ROWS, PACK_W), _w_in_row_maps()[0], name="w_in_rows")


def _mid_operands(wout_g, woa_g, wob_g, small_g):
    def full(n, off, r):
        a = small_g[:, off:off + r].reshape((N_DEV,) + _shard_shape(n))
        return jnp.moveaxis(a, 0, 1).reshape(FULL_SHAPE[n])

    w = {n: full(n, off, r) for n, off, r in SMALL_FLAT}
    ukv = w["w_ukv"].reshape(KV_LORA, MLA_HEADS, MLA_NOPE + MLA_V)
    return dict(
        wout=wout_g.reshape(D_MODEL, D_MODEL), woa_t=woa_g.reshape(D_MODEL, -1), wob_t=wob_g.reshape(D_MODEL, -1),
        wuq=_pad_heads(w["w_uq"], MLA_HEADS, MLA_QK, 1),
        wuk=_pad_heads(ukv[:, :, :MLA_NOPE].reshape(KV_LORA, -1), MLA_HEADS, MLA_NOPE, 1),
        wuv=_pad_heads(ukv[:, :, MLA_NOPE:].reshape(KV_LORA, -1), MLA_HEADS, MLA_V, 1),
    )


def _mid_grad_pack(g):
    uk = _unpad_heads(g["wukv"][:, :1024], MLA_HEADS, MLA_NOPE, 1).reshape(KV_LORA, MLA_HEADS, MLA_NOPE)
    uv = _unpad_heads(g["wukv"][:, 1024:], MLA_HEADS, MLA_V, 1).reshape(KV_LORA, MLA_HEADS, MLA_V)
    w = dict(w_uq=_unpad_heads(g["wuq"], MLA_HEADS, MLA_QK, 1), w_ukv=jnp.concatenate([uk, uv], 2).reshape(KV_LORA, -1))
    rows = []
    for n, _, r in SMALL_FLAT:
        rr, cc = FULL_SHAPE[n]
        a = jnp.moveaxis(w[n].reshape(rr, N_DEV, cc // N_DEV), 1, 0).reshape(N_DEV, r, PACK_W)
        rows.append(jnp.pad(a, ((0, 0), (0, -r % ROW_TILE), (0, 0))).astype(WIRE_DTYPE))
    rows.append(jnp.zeros((N_DEV, OUT_ROWS - SMALL_USED, PACK_W), WIRE_DTYPE))
    blk = lambda a: a.reshape(N_DEV, OUT_ROWS, PACK_W)
    return [blk(g["wout"]), blk(g["woa_t"]), blk(g["wob_t"]), jnp.concatenate(rows, 1)]


def _w_in_grad_chunks(g_win_t):
    return _take_rows(g_win_t, _w_in_row_maps()[1], name="dw_in_rows").reshape(N_DEV, W_IN_ROWS, PACK_W)


def _local_step(x, tgt, win_t, small, weights, grads):
    s_ = x.shape[0]
    tabs = _rope_tables(s_)
    sink_b = jnp.broadcast_to(small["swa_sinks"].reshape(SWA_KV_HEADS, SWA_GROUP, 1), (SWA_KV_HEADS, SWA_GROUP, LANES))
    sink_b = jnp.pad(sink_b, ((0, 0), (0, SUBLANES - SWA_GROUP), (0, 0)))

    h, qa, ka, va, cq, ckv, kro, p = _proj_in(x, small["mix_norm_g"], win_t, small["q_norm_g"], small["kv_norm_g"], tabs)
    ops = weights.mid(cq)
    oa32, oa16, lse_a = _swa_fwd(qa, ka, va, sink_b)
    qc, kc, vp = _mla_up(cq, ckv, kro, ops["wuq"], ops["wuk"], ops["wuv"], tabs)
    ob32, ob16, lse_b = _mla_fwd(qc, kc, vp)
    ta, tb, y = _attn_out_gate(oa16, ob16, ops["woa_t"], ops["wob_t"], p)
    x1 = _mm(y, ops["wout"], "nn", name="out_proj", add=x, tm=1024, tn=1024)
    wgu_t, wd = weights.late(x1)
    h2, gu, act = _ffn_in_act(x1, small["ffn_norm_g"], wgu_t)

    dx2, dx2b, dg3, _, tot = _ffn_out_loss(act, wd, x1, small["final_norm_g"].reshape(1, D_MODEL), tgt)
    g = {}
    g_wd = _mm(act, dx2b, "tn", name="dw_down", tm=FF_TILE, tn=1024, tk=2048, out_dtype=WIRE_DTYPE)
    dgu = _d_act_swiglu(dx2b, wd, gu)
    g_wgu = _mm(dgu, h2, "tn", name="dw_ffn_in", tm=FF_TILE, tn=1024, tk=2048, out_dtype=WIRE_DTYPE)
    token = grads.late(g_wgu, g_wd)
    dx1, dx1b, dg2 = _mm_norm_bwd(dgu, wgu_t, x1, small["ffn_norm_g"] + token[0:1, 0:1], dx2, name="d_h2")
    g["wout"] = _mm(y, dx1b, "tn", name="dw_out", tm=1024, tn=1024, tk=1024, out_dtype=WIRE_DTYPE)
    dta, dtb, dgab = _d_y_gate(dx1b, ops["wout"], p, ta, tb)
    doa = _mm(dta, ops["woa_t"], "nn", name="d_oa", tm=1024, tn=1024)
    g["woa_t"] = _mm(dta, oa16, "tn", name="dw_o_swa", tm=1024, tn=1024, tk=1024, out_dtype=WIRE_DTYPE)
    g["wob_t"] = _mm(dtb, ob16, "tn", name="dw_o_mla", tm=1024, tn=1024, tk=1024, out_dtype=WIRE_DTYPE)
    dob16, delta_b = _mla_d_out(dtb, ops["wob_t"], ob32)
    dqc, dkc, dvp = _mla_bwd(qc, kc, vp, dob16, lse_b, delta_b)
    dqp, dkv, dkr, dqlat, dkvlat, dgq, dgkv = _mla_up_bwd(
        dqc, dkc, dvp, ops["wuq"], jnp.concatenate([ops["wuk"], ops["wuv"]], 1), p, small["q_norm_g"], small["kv_norm_g"], tabs)
    g["wuq"] = _mm(cq, dqp, "tn", name="dw_uq", tm=Q_LORA, tn=1024, tk=512)
    g["wukv"] = _mm(ckv, dkv, "tn", name="dw_ukv", tm=KV_LORA, tn=1024, tk=512)
    token = grads.mid(g)
    dqa, dka, dva, dsk = _swa_bwd(qa, ka, va, sink_b + token[0:1, 0:1], oa32, doa, lse_a)
    dp = _assemble_dp(dgab, dqa, dqlat, dkr, dka, dva, dkvlat, tabs)
    token = grads.last(_mm(dp, h, "tn", name="dw_in", tm=2176, tn=1024, tk=1024, out_dtype=WIRE_DTYPE))
    gx, _, dg1 = _mm_norm_bwd(dp, win_t, x, small["mix_norm_g"], dx1, name="d_h", after=token)

    sm = dict(mix_norm_g=dg1, ffn_norm_g=dg2, final_norm_g=dg3, q_norm_g=dgq, kv_norm_g=dgkv,
              swa_sinks=dsk[:, :SWA_GROUP, 0].reshape(1, SWA_HEADS))
    return tot, gx, sm


MESH = pl.DeviceIdType.MESH
ANY = pl.BlockSpec(memory_space=pl.ANY)


def _position():
    return lax.axis_index("x"), lax.axis_index("y"), lax.axis_index("c")


def _all_gather(block, pieces, shapes, *, name):
    n_out = len(shapes)
    n_rows = sum(p[3] for p in pieces)

    def body(x_ref, *refs):
        outs, (send_sems, recv_sems, local_sem) = refs[:n_out], refs[n_out:]
        x, y, c = _position()
        me, sibling = (x, y, c), (x, y, 1 - c)
        chips = [(1 - x, y), (x, 1 - y), (1 - x, 1 - y)]

        def dst(piece, blk):
            arr, lead, _, _ = piece
            return outs[arr].at[lead(4 * blk[0] + 2 * blk[1] + blk[2])]

        def own(piece):
            return x_ref.at[pl.ds(piece[2], piece[3])]

        def copies(k, blk, to, from_input):
            return [pltpu.make_async_remote_copy(
                src_ref=own(p) if from_input else dst(p, blk), dst_ref=dst(p, blk), send_sem=send_sems.at[k],
                recv_sem=recv_sems.at[k], device_id=to, device_id_type=MESH) for p in pieces]

        gathered_rows = x_ref.at[pl.ds(0, n_rows)]

        def whole_block(k):
            return pltpu.make_async_remote_copy(src_ref=gathered_rows, dst_ref=gathered_rows, send_sem=send_sems.at[k],
                                                recv_sem=recv_sems.at[k], device_id=me, device_id_type=MESH)

        for p in pieces:
            pltpu.make_async_copy(own(p), dst(p, me), local_sem).start()
        for cp in copies(0, me, sibling, True):
            cp.start()
        for j, chip in enumerate(chips):
            for cp in copies(1 + j, me, (*chip, c), True):
                cp.start()
        for j, chip in enumerate(chips):
            whole_block(1 + j).wait_recv()
            for cp in copies(4 + j, (*chip, c), sibling, False):
                cp.start()
        whole_block(0).wait_recv()
        for j in range(3):
            whole_block(4 + j).wait_recv()
        for k in range(7):
            whole_block(k).wait_send()
        pltpu.make_async_copy(gathered_rows, gathered_rows, local_sem).wait()

    return pl.pallas_call(
        body, name=name, out_shape=[jax.ShapeDtypeStruct(s, block.dtype) for s in shapes], in_specs=[ANY],
        out_specs=[ANY] * n_out,
        scratch_shapes=[pltpu.SemaphoreType.DMA((7,)), pltpu.SemaphoreType.DMA((7,)), pltpu.SemaphoreType.DMA],
    )(block)


HBM = pl.BlockSpec(memory_space=pltpu.HBM)
SEM = pl.BlockSpec(memory_space=pltpu.SEMAPHORE)
TILE_DEVS = FF_TILE // FF_COLS
GU_SHAPE = (2, 2, TILE_DEVS, FF_COLS, PACK_W)


def _gate_slab(d):
    return (d // TILE_DEVS, 0, d % TILE_DEVS)


def _up_slab(d):
    return (d // TILE_DEVS, 1, d % TILE_DEVS)
D_SHAPE = (N_DEV, FF_COLS, PACK_W)
LAND_SHAPE = (N_DEV, LATE_ROWS, PACK_W)


def _split_params():
    return pltpu.CompilerParams(has_side_effects=pltpu.SideEffectType.DATAFLOW_SIDE_EFFECTING)


def _peer(x, y, c, k):
    return ((1 - x) if k & 4 else x, (1 - y) if k & 2 else y, (1 - c) if k & 1 else c)


def _empty_hbm(shape, dtype):
    return pltpu.with_memory_space_constraint(lax.empty(shape, dtype), pltpu.HBM)


def _wait_all(rows, send_sems, recv_sems, me):
    for k in range(N_DEV - 1):
        cp = pltpu.make_async_remote_copy(src_ref=rows, dst_ref=rows, send_sem=send_sems.at[k], recv_sem=recv_sems.at[k],
                                          device_id=me, device_id_type=MESH)
        cp.wait_send()
        cp.wait_recv()


def _token_shape():
    return jax.ShapeDtypeStruct((SUBLANES, LANES), F32)


def _gather_start(pack, row0, pieces, shapes, *, name):
    n = len(shapes)

    def body(*refs):
        p_ref, bufs, send_sems, recv_sems, token = refs[0], refs[1:1 + n], refs[1 + n], refs[2 + n], refs[-1]
        x, y, c = _position()
        me = 4 * x + 2 * y + c
        for k in range(1, N_DEV):
            off = row0
            for buf, lead, rows in pieces:
                pltpu.make_async_remote_copy(
                    src_ref=p_ref.at[pl.ds(off, rows)], dst_ref=bufs[buf].at[lead(me)], send_sem=send_sems.at[k - 1],
                    recv_sem=recv_sems.at[k - 1], device_id=_peer(x, y, c, k), device_id_type=MESH).start()
                off += rows
        token[...] = jnp.zeros_like(token)

    sems, dt = pltpu.SemaphoreType.DMA((N_DEV - 1,)), pack.dtype
    return pl.pallas_call(
        body, name=name,
        out_shape=(sems, sems, pltpu.HBM(pack.shape, dt)) + tuple(pltpu.HBM(s, dt) for s in shapes) + (_token_shape(),),
        in_specs=(HBM,) * (1 + n), out_specs=(SEM, SEM) + (HBM,) * (1 + n) + (pl.BlockSpec(memory_space=pltpu.VMEM),),
        input_output_aliases={i: 2 + i for i in range(1 + n)}, compiler_params=_split_params(),
    )(pltpu.with_memory_space_constraint(pack, pltpu.HBM), *[_empty_hbm(s, dt) for s in shapes])


def _gather_wait(started, row0, n_rows, after, *, name):
    send_sems, recv_sems, pack, *bufs = started[:-1]
    n = len(bufs)

    def body(*refs):
        _wait_all(refs[0].at[pl.ds(row0, n_rows)], refs[1 + n], refs[2 + n], _position())

    outs = pl.pallas_call(
        body, name=name, out_shape=tuple(pltpu.HBM(a.shape, a.dtype) for a in (pack, *bufs)),
        in_specs=(HBM,) * (1 + n) + (SEM, SEM, ANY), out_specs=(HBM,) * (1 + n),
        input_output_aliases={i: i for i in range(1 + n)}, compiler_params=_split_params(),
    )(pack, *bufs, send_sems, recv_sems, after)
    return outs[0], outs[1:]


def _scatter_start(srcs, pieces, *, name):
    n = len(srcs)
    land_shape = (N_DEV, sum(p[2] for p in pieces), PACK_W)

    def body(*refs):
        src_refs, land_ref, send_sems, recv_sems, token = refs[:n], refs[n], refs[n + 1], refs[n + 2], refs[-1]
        x, y, c = _position()
        me = 4 * x + 2 * y + c
        for k in range(1, N_DEV):
            px, py, pc = _peer(x, y, c, k)
            off = 0
            for si, lead, rows in pieces:
                pltpu.make_async_remote_copy(
                    src_ref=src_refs[si].at[lead(4 * px + 2 * py + pc)], dst_ref=land_ref.at[me, pl.ds(off, rows)],
                    send_sem=send_sems.at[k - 1], recv_sem=recv_sems.at[k - 1], device_id=(px, py, pc),
                    device_id_type=MESH).start()
                off += rows
        token[...] = jnp.zeros_like(token)

    sems, dt = pltpu.SemaphoreType.DMA((N_DEV - 1,)), srcs[0].dtype
    return pl.pallas_call(
        body, name=name,
        out_shape=(sems, sems) + tuple(pltpu.HBM(a.shape, dt) for a in srcs) + (pltpu.HBM(land_shape, dt), _token_shape()),
        in_specs=(HBM,) * (n + 1), out_specs=(SEM, SEM) + (HBM,) * (n + 1) + (pl.BlockSpec(memory_space=pltpu.VMEM),),
        input_output_aliases={i: 2 + i for i in range(n + 1)}, compiler_params=_split_params(),
    )(*[pltpu.with_memory_space_constraint(a, pltpu.HBM) for a in srcs], _empty_hbm(land_shape, dt))


def _scatter_wait(started, after, *, name):
    send_sems, recv_sems, *bufs = started[:-1]
    n = len(bufs)

    def body(*refs):
        _wait_all(refs[n - 1].at[0], refs[n], refs[n + 1], _position())

    return pl.pallas_call(
        body, name=name, out_shape=tuple(pltpu.HBM(a.shape, a.dtype) for a in bufs),
        in_specs=(HBM,) * n + (SEM, SEM, ANY), out_specs=(HBM,) * n, input_output_aliases={i: i for i in range(n)},
        compiler_params=_split_params(),
    )(*bufs, send_sems, recv_sems, after)


def _peer_sum(own, own_lead, land, block, rows, idx, *, name):
    owns = list(own) if isinstance(own, (list, tuple)) else [own]
    n, lead_rank = len(owns), owns[0].ndim - 2

    def body(idx_ref, *refs):
        own_refs, land_refs, o_ref = refs[:n], refs[n:n + N_DEV - 1], refs[n + N_DEV - 1]
        for j in range(n):
            rs_ = slice(j * rows, (j + 1) * rows)
            acc = own_refs[j][(0,) * lead_rank].astype(F32)
            for k in range(N_DEV - 1):
                acc = acc + land_refs[k][0, rs_].astype(F32)
            o_ref[rs_] = acc

    own_spec = pl.BlockSpec((1,) * lead_rank + (rows, PACK_W), lambda i, t: own_lead(t[0]) + (0, 0))

    def land_spec(k):
        return pl.BlockSpec((1, n * rows, PACK_W), lambda i, t: (t[k + 1], block, 0))

    return pl.pallas_call(
        body, name=name,
        grid_spec=pltpu.PrefetchScalarGridSpec(
            num_scalar_prefetch=1, grid=(1,), in_specs=[own_spec] * n + [land_spec(k) for k in range(N_DEV - 1)],
            out_specs=pl.BlockSpec((n * rows, PACK_W), lambda i, t: (0, 0))),
        out_shape=jax.ShapeDtypeStruct((n * rows, PACK_W), F32), compiler_params=_cparams(("arbitrary",)),
    )(idx, *owns, *([land] * (N_DEV - 1)))


def _adamw(w, g, m, v):
    m = ADAM_B1 * m + (1.0 - ADAM_B1) * g
    v = ADAM_B2 * v + (1.0 - ADAM_B2) * (g * g)
    m_hat = m / (1.0 - ADAM_B1 ** ADAM_STEP)
    v_hat = v / (1.0 - ADAM_B2 ** ADAM_STEP)
    delta = -ADAM_LR * (m_hat / (jnp.sqrt(v_hat) + ADAM_EPS) + ADAM_WD * w)
    return delta, m, v


def _adamw_call(w, g, m, v, *, name, max_rows=256):
    _, r, c_ = w.shape
    tr = max_rows if r > max_rows and r % max_rows == 0 else r

    def body(w_ref, g_ref, m_ref, v_ref, d_ref, mo_ref, vo_ref):
        d, mn, vn = _adamw(w_ref[0], g_ref[...], m_ref[0], v_ref[0])
        d_ref[0] = d
        mo_ref[0] = mn
        vo_ref[0] = vn

    row3 = pl.BlockSpec((1, tr, c_), lambda i: (0, i, 0))
    shp = jax.ShapeDtypeStruct((1, r, c_), F32)
    return pl.pallas_call(
        body, name=name, grid=(r // tr,), in_specs=[row3, pl.BlockSpec((tr, c_), lambda i: (i, 0)), row3, row3],
        out_specs=[row3] * 3, out_shape=[shp] * 3, compiler_params=_cparams(("parallel",)),
    )(w, g, m, v)


SMALL = ("mix_norm_g", "ffn_norm_g", "final_norm_g", "q_norm_g", "kv_norm_g", "swa_sinks")
SMALL_W = dict(mix_norm_g=1024, ffn_norm_g=1024, final_norm_g=1024, q_norm_g=Q_LORA, kv_norm_g=KV_LORA, swa_sinks=SWA_HEADS)


def _small_adamw(parts, w, m, v):
    ns = len(SMALL)

    def body(p_ref, *refs):
        ins, outs = refs[:3 * ns], refs[3 * ns:]
        tot = p_ref[0]
        for dev in range(1, N_DEV):
            tot = tot + p_ref[dev]
        for k, n in enumerate(SMALL):
            g = jnp.sum(tot[k * SUBLANES:(k + 1) * SUBLANES, :SMALL_W[n]], axis=0, keepdims=True)
            res = _adamw(ins[k][...], g, ins[ns + k][...], ins[2 * ns + k][...])
            for j, r in enumerate((g,) + tuple(res)):
                outs[j * ns + k][...] = r
        outs[4 * ns][...] = jnp.sum(tot[ns * SUBLANES:(ns + 1) * SUBLANES, 0:1], axis=0, keepdims=True)

    shapes = [jax.ShapeDtypeStruct((1, SMALL_W[n]), F32) for n in SMALL]
    vm = pl.BlockSpec(memory_space=pltpu.VMEM)
    out = pl.pallas_call(
        body, name="small_adamw", in_specs=[vm] * (1 + 3 * ns), out_specs=[vm] * (4 * ns + 1),
        out_shape=shapes * 4 + [jax.ShapeDtypeStruct((1, 1), F32)],
    )(parts, *[d[n] for d in (w, m, v) for n in SMALL])
    return [dict(zip(SMALL, out[j * ns:(j + 1) * ns])) for j in range(4)] + [out[4 * ns]]


def _small_pack(d, rows_each):
    parts = [jnp.pad(d[n].astype(F32), ((0, 0), (0, PACK_W - SMALL_W[n]))) for n in SMALL]
    out = jnp.concatenate(parts, 0)
    pad = -out.shape[0] % SUBLANES
    return jnp.pad(out, ((0, pad), (0, 0)))


def kernel(x, mix_norm_g, w_in, swa_sinks, q_norm_g, w_uq, kv_norm_g, w_ukv, w_o_swa, w_o_mla, w_out, ffn_norm_g, w_gate, w_up, w_down, final_norm_g, loss_target, m_mix_norm_g, m_w_in, m_swa_sinks, m_q_norm_g, m_w_uq, m_kv_norm_g, m_w_ukv, m_w_o_swa, m_w_o_mla, m_w_out, m_ffn_norm_g, m_w_gate, m_w_up, m_w_down, m_final_norm_g, v_mix_norm_g, v_w_in, v_swa_sinks, v_q_norm_g, v_w_uq, v_kv_norm_g, v_w_ukv, v_w_o_swa, v_w_o_mla, v_w_out, v_ffn_norm_g, v_w_gate, v_w_up, v_w_down, v_final_norm_g):
    big_w = dict(w_in=w_in[0], w_uq=w_uq[0], w_ukv=w_ukv[0], w_o_swa=w_o_swa[0], w_o_mla=w_o_mla[0], w_out=w_out[0],
                 w_gate=w_gate[0], w_up=w_up[0], w_down=w_down[0])
    big_w3 = dict(w_in=w_in, w_uq=w_uq, w_ukv=w_ukv, w_o_swa=w_o_swa, w_o_mla=w_o_mla, w_out=w_out, w_gate=w_gate, w_up=w_up,
                  w_down=w_down)
    big_m = dict(w_in=m_w_in, w_uq=m_w_uq, w_ukv=m_w_ukv, w_o_swa=m_w_o_swa, w_o_mla=m_w_o_mla, w_out=m_w_out,
                 w_gate=m_w_gate, w_up=m_w_up, w_down=m_w_down)
    big_v = dict(w_in=v_w_in, w_uq=v_w_uq, w_ukv=v_w_ukv, w_o_swa=v_w_o_swa, w_o_mla=v_w_o_mla, w_out=v_w_out,
                 w_gate=v_w_gate, w_up=v_w_up, w_down=v_w_down)
    small_w = dict(mix_norm_g=mix_norm_g, ffn_norm_g=ffn_norm_g, final_norm_g=final_norm_g.reshape(1, D_MODEL),
                   q_norm_g=q_norm_g, kv_norm_g=kv_norm_g, swa_sinks=swa_sinks)
    small_m = dict(mix_norm_g=m_mix_norm_g, ffn_norm_g=m_ffn_norm_g, final_norm_g=m_final_norm_g.reshape(1, D_MODEL),
                   q_norm_g=m_q_norm_g, kv_norm_g=m_kv_norm_g, swa_sinks=m_swa_sinks)
    small_v = dict(mix_norm_g=v_mix_norm_g, ffn_norm_g=v_ffn_norm_g, final_norm_g=v_final_norm_g.reshape(1, D_MODEL),
                   q_norm_g=v_q_norm_g, kv_norm_g=v_kv_norm_g, swa_sinks=v_swa_sinks)

    px, py, pc = _position()
    me = 4 * px + 2 * py + pc
    idx = jnp.stack([me] + [4 * qx + 2 * qy + qc for qx, qy, qc in (_peer(px, py, pc, k) for k in range(1, N_DEV))])
    idx = idx.astype(jnp.int32)

    dev = lambda d: (d,)
    pack = _wire_pack(big_w, WIRE_DTYPE)
    win_g, = _all_gather(pack, ((0, dev, 0, W_IN_ROWS),), ((N_DEV, W_IN_ROWS, PACK_W),), name="ag_early")
    mid_pieces = tuple((b, dev, OUT_ROWS) for b in range(MID_BLOCKS))
    ag_mid = _gather_start(pack, W_IN_ROWS, mid_pieces, ((N_DEV, OUT_ROWS, PACK_W),) * MID_BLOCKS, name="ag_mid_start")
    ag = {}

    def own_rows(r0, r1, shape):
        return pack[r0:r1].reshape(shape)

    def mid_weights(after):
        pack_mid, blocks = _gather_wait(ag_mid, W_IN_ROWS, MID_ROWS, after, name="ag_mid_wait")
        ag["late"] = _gather_start(pack_mid, EARLY_ROWS, ((0, _gate_slab, FF_COLS), (0, _up_slab, FF_COLS), (1, dev, FF_COLS)),
                                   (GU_SHAPE, D_SHAPE), name="ag_late_start")
        row0 = lambda b: W_IN_ROWS + b * OUT_ROWS
        ops = _mid_operands(*[lax.dynamic_update_slice(blk, own_rows(row0(b), row0(b + 1), (1, OUT_ROWS, PACK_W)), (me, 0, 0))
                              for b, blk in enumerate(blocks)])
        ops["wuq"] = ops["wuq"] + ag["late"][-1][0:1, 0:1].astype(ops["wuq"].dtype)
        return ops

    def late_weights(after):
        _, (gu, d) = _gather_wait(ag["late"], EARLY_ROWS, LATE_ROWS, after, name="ag_late_wait")
        slab = (1, 1, 1, FF_COLS, PACK_W)
        gu = lax.dynamic_update_slice(gu, own_rows(EARLY_ROWS, EARLY_ROWS + FF_COLS, slab), _gate_slab(me) + (0, 0))
        gu = lax.dynamic_update_slice(gu, own_rows(EARLY_ROWS + FF_COLS, EARLY_ROWS + 2 * FF_COLS, slab), _up_slab(me) + (0, 0))
        d = lax.dynamic_update_slice(d, own_rows(EARLY_ROWS + 2 * FF_COLS, PACK_ROWS, (1, FF_COLS, PACK_W)), (me, 0, 0))
        return gu.reshape(2 * D_FF, D_MODEL), d.reshape(D_FF, D_MODEL)

    rs = {}

    def late_grads(g_gu, g_d):
        rs["late"] = _scatter_start([g_gu.reshape(GU_SHAPE), g_d.reshape(D_SHAPE)],
                                    ((0, _gate_slab, FF_COLS), (0, _up_slab, FF_COLS), (1, dev, FF_COLS)),
                                    name="rs_late_start")
        return rs["late"][-1]

    def mid_grads(g):
        rs["mid"] = _scatter_start(_mid_grad_pack(g), mid_pieces, name="rs_mid_start")
        return rs["mid"][-1]

    def last_grads(g_win_t):
        rs["last"] = _scatter_start([_w_in_grad_chunks(g_win_t)], ((0, dev, W_IN_ROWS),), name="rs_last_start")
        return rs["last"][-1]

    first_w = dict(small_w, mix_norm_g=mix_norm_g + ag_mid[-1][0:1, 0:1])
    loss_tot, gx, g_small = _local_step(
        x[0], loss_target[0], _w_in_operand(win_g), first_w, types.SimpleNamespace(mid=mid_weights, late=late_weights),
        types.SimpleNamespace(late=late_grads, mid=mid_grads, last=last_grads))

    g_gu, g_d, land_late = _scatter_wait(rs["late"], gx, name="rs_late_wait")
    *g_mid, land_mid = _scatter_wait(rs["mid"], gx, name="rs_mid_wait")
    g_win, land_last = _scatter_wait(rs["last"], gx, name="rs_last_wait")
    gw_t = dict(w_gate=_peer_sum(g_gu, _gate_slab, land_late, 0, FF_COLS, idx, name="rs_sum_gate"),
                w_up=_peer_sum(g_gu, _up_slab, land_late, 1, FF_COLS, idx, name="rs_sum_up"),
                w_in=_peer_sum(g_win, dev, land_last, 0, W_IN_ROWS, idx, name="rs_sum_in")[0:W_IN_COLS])
    gw = dict(w_down=_peer_sum(g_d, dev, land_late, 2, FF_COLS, idx, name="rs_sum_down"))
    gw.update(_mid_unpack(_peer_sum(g_mid, dev, land_mid, 0, OUT_ROWS, idx, name="rs_sum_mid")))
    dw, mw, vw = {}, {}, {}
    swap = lambda a: jnp.swapaxes(a, 1, 2)
    for n in BIG:
        if n in gw_t:
            res = _adamw_call(swap(big_w3[n]), gw_t[n], swap(big_m[n]), swap(big_v[n]), name="adamw_" + n)
            dw[n], mw[n], vw[n] = (swap(r) for r in res)
        else:
            dw[n], mw[n], vw[n] = _adamw_call(big_w3[n], gw[n], big_m[n], big_v[n], name="adamw_" + n)
    gw = {n: g[None] for n, g in gw.items()}
    gw.update({n: swap(g[None]) for n, g in gw_t.items()})

    loss_rows = jnp.pad(loss_tot[0:1, 0:1], ((0, SUBLANES - 1), (0, PACK_W - 1)))
    small_rows = jnp.concatenate([_small_pack(g_small_rows(g_small), SUBLANES), loss_rows], 0)
    parts, = _all_gather(small_rows, ((0, lambda d: (d,), 0, small_rows.shape[0]),), ((N_DEV,) + small_rows.shape,),
                         name="ag_small")
    gs, ds, ms, vs, loss = _small_adamw(parts, small_w, small_m, small_v)
    loss = loss[0, 0]
    for d in (gs, ds, ms, vs):
        d["final_norm_g"] = d["final_norm_g"].reshape(D_MODEL)

    order = ("mix_norm_g", "w_in", "swa_sinks", "q_norm_g", "w_uq", "kv_norm_g", "w_ukv", "w_o_swa", "w_o_mla", "w_out",
             "ffn_norm_g", "w_gate", "w_up", "w_down", "final_norm_g")

    def leaves(big, small):
        return [big[n] if n in big else small[n] for n in order]

    return (loss, gx[None], *leaves(gw, gs), *leaves(dw, ds), *leaves(mw, ms), *leaves(vw, vs))


def g_small_rows(g_small):
    out = dict(g_small)
    out["swa_sinks"] = jnp.pad(g_small["swa_sinks"], ((0, SUBLANES - 1), (0, 0)))
    return out
```

```python
import types

import numpy as np
import jax
import jax.numpy as jnp
from jax import lax
from jax.experimental import pallas as pl
from jax.experimental.pallas import tpu as pltpu

F32 = jnp.float32
MXU_DTYPE = jnp.bfloat16
WIRE_DTYPE = jnp.bfloat16

D_MODEL = 1024
EPS = 1e-6
ROPE_THETA = 10000.0
BLOCK = 128
HEAD_DIM = 64
SWA_HEADS = 8
SWA_KV_HEADS = 2
SWA_GROUP = SWA_HEADS // SWA_KV_HEADS
MLA_HEADS = 8
MLA_NOPE = 64
MLA_ROPE = 32
MLA_V = 64
MLA_QK = MLA_NOPE + MLA_ROPE
Q_LORA = 384
KV_LORA = 256
D_FF = 2816
IN_SIZES = (512, 128, 128, Q_LORA, KV_LORA, MLA_ROPE, D_MODEL, D_MODEL)
IN_OFF = tuple(int(v) for v in np.cumsum((0,) + IN_SIZES))
ADAM_LR, ADAM_B1, ADAM_B2, ADAM_EPS, ADAM_WD, ADAM_STEP = 0.001, 0.9, 0.999, 1e-08, 0.01, 10

LANES = 128
SUBLANES = 8
VMEM_LIMIT = 48 * 1024 * 1024
N_DEV = 8
AXES = ("x", "y", "c")

P_GA, P_GB, P_Q, P_QLAT, P_KR, P_K, P_V, P_KVLAT, P_W = 0, 1024, 2048, 3072, 3456, 3584, 3840, 4096, 4352
KR_LANE = 64

LOG2E = 1.4426950408889634

NT = (((1,), (1,)), ((), ()))
NN = (((1,), (0,)), ((), ()))
TN = (((0,), (0,)), ((), ()))


def _cparams(sem):
    return pltpu.CompilerParams(dimension_semantics=sem, vmem_limit_bytes=VMEM_LIMIT)


def _mm(a, b, mode, *, name, out_dtype=F32, add=None, after=None, tm=512, tn=512, tk=None):
    if mode == "nn":
        (M, K), (K2, N) = a.shape, b.shape
    elif mode == "nt":
        (M, K), (N, K2) = a.shape, b.shape
    else:
        (K, M), (K2, N) = a.shape, b.shape
    assert K == K2, (a.shape, b.shape, mode)
    tm, tn, tk = min(tm, M), min(tn, N), K if tk is None else min(tk, K)
    assert M % tm == 0 and N % tn == 0 and K % tk == 0, (M, N, K, tm, tn, tk)
    nk = K // tk
    dn = {"nn": NN, "nt": NT, "tn": TN}[mode]
    if mode == "tn":
        a_spec = pl.BlockSpec((tk, tm), lambda i, j, k: (k, i))
    else:
        a_spec = pl.BlockSpec((tm, tk), lambda i, j, k: (i, k))
    once = dict(pipeline_mode=pl.Buffered(1)) if (nk == 1 and tn == N) else {}
    if mode == "nt":
        b_spec = pl.BlockSpec((tn, tk), lambda i, j, k: (j, k), **once)
    else:
        b_spec = pl.BlockSpec((tk, tn), lambda i, j, k: (k, j), **once)
    o_spec = pl.BlockSpec((tm, tn), lambda i, j, k: (i, j))
    has_add, has_after = add is not None, after is not None

    def body(*refs):
        a_ref, b_ref = refs[0], refs[1]
        add_ref = refs[2] if has_add else None
        o_ref = refs[2 + has_add + has_after]
        p = lax.dot_general(a_ref[...], b_ref[...], dn, preferred_element_type=F32)

        def finish(acc):
            if has_add:
                acc = acc + add_ref[...]
            o_ref[...] = acc.astype(o_ref.dtype)

        if nk == 1:
            finish(p)
        else:
            acc_ref = refs[-1]
            k = pl.program_id(2)

            @pl.when(k == 0)
            def _():
                acc_ref[...] = p

            @pl.when((k > 0) & (k < nk - 1))
            def _():
                acc_ref[...] += p

            @pl.when(k == nk - 1)
            def _():
                finish(acc_ref[...] + p)

    ins = [a, b] + ([add] if has_add else []) + ([after] if has_after else [])
    in_specs = [a_spec, b_spec] + ([o_spec] if has_add else []) + ([pl.BlockSpec(memory_space=pl.ANY)] if has_after else [])
    return pl.pallas_call(
        body, name=name, grid=(M // tm, N // tn, nk), in_specs=in_specs, out_specs=o_spec,
        out_shape=jax.ShapeDtypeStruct((M, N), out_dtype),
        scratch_shapes=[pltpu.VMEM((tm, tn), F32)] if nk > 1 else [],
        compiler_params=_cparams(("parallel", "parallel", "arbitrary")),
    )(*ins)


def _rows(ts, w, cb=0):
    return pl.BlockSpec((ts, w), lambda i: (i, cb))


def _const(r, w):
    return pl.BlockSpec((r, w), lambda i: (0, 0))


def _sublane_sum(v):
    ts, c = v.shape
    return jnp.sum(v.reshape(ts // SUBLANES, SUBLANES, c), axis=0)


def _sigmoid(v):
    return 1.0 / (1.0 + jnp.exp(-v))


def _rope(v, cos, s_up, s_dn, up, dn):
    return v * cos + pltpu.roll(v, up, 1) * s_up + pltpu.roll(v, dn, 1) * s_dn


def _rope_t(dv, cos, s_up, s_dn, up, dn):
    return dv * cos + pltpu.roll(dv * s_up, dn, 1) + pltpu.roll(dv * s_dn, up, 1)


def _rope_tables(seq):
    pos = np.arange(seq, dtype=np.float32)[:, None]

    def base(dim):
        inv = np.float32(ROPE_THETA) ** (-np.arange(0, dim, 2, dtype=np.float32) / np.float32(dim))
        ang = (pos * inv.astype(np.float32)[None, :]).astype(np.float32)
        return np.cos(ang).astype(np.float32), np.sin(ang).astype(np.float32)

    z = lambda n: np.zeros((seq, n), np.float32)
    ca, sa = base(HEAD_DIM)
    a_cos = np.concatenate([ca, ca, z(64)], 1)
    a_up = np.concatenate([-sa, z(96)], 1)
    a_dn = np.concatenate([z(32), sa, z(64)], 1)
    cb, sb = base(MLA_ROPE)
    one = np.ones((seq, 64), np.float32)
    q_cos = np.concatenate([one, cb, cb, z(32)], 1)
    k_cos = np.concatenate([z(64), cb, cb, z(32)], 1)
    b_up = np.concatenate([z(64), -sb, z(48)], 1)
    b_dn = np.concatenate([z(80), sb, z(32)], 1)
    return tuple(jnp.asarray(t) for t in (a_cos, a_up, a_dn, q_cos, k_cos, b_up, b_dn))


def _rms(v, g):
    return v * lax.rsqrt(jnp.mean(v * v, axis=-1, keepdims=True) + EPS) * g


def _rms_bwd(v, g, d):
    r = lax.rsqrt(jnp.mean(v * v, axis=-1, keepdims=True) + EPS)
    xh = v * r
    dxh = d * g
    return r * (dxh - xh * jnp.mean(dxh * xh, axis=-1, keepdims=True)), d * xh


F_GA, F_GB, F_KVLAT, F_QLAT, F_W = 0, 1024, 2048, 2304, 2688


def _proj_in(x, g, w_t, gq, gkv, tabs, *, tm=512):
    s_, c = x.shape
    a_cos, a_up, a_dn, _, k_cos, b_up, b_dn = tabs

    def body(x_ref, g_ref, w_ref, gq_ref, gkv_ref, ac, au, ad, kc, bu, bd,
             h_ref, qa_ref, ka_ref, va_ref, cq_ref, ckv_ref, kro_ref, pf_ref):
        h = _rms(x_ref[...], g_ref[...]).astype(h_ref.dtype)
        h_ref[...] = h
        mm = lambda a, b: lax.dot_general(h, w_ref[a:b, :], NT, preferred_element_type=F32)
        pf_ref[:, F_GA:F_KVLAT] = mm(P_GA, P_Q)
        c_, u_, d_ = ac[...], au[...], ad[...]
        q = mm(P_Q, P_QLAT)
        for hd in range(SWA_HEADS):
            sl = slice(hd * LANES, (hd + 1) * LANES)
            qa_ref[:, sl] = _rope(q[:, sl], c_, u_, d_, 96, 32).astype(qa_ref.dtype)
        kv = mm(P_KR, P_KVLAT)
        kro_ref[...] = _rope(kv[:, :LANES], kc[...], bu[...], bd[...], 112, 16)
        for hd in range(SWA_KV_HEADS):
            sl = slice((1 + hd) * LANES, (2 + hd) * LANES)
            ka_ref[:, hd * LANES:(hd + 1) * LANES] = _rope(kv[:, sl], c_, u_, d_, 96, 32).astype(ka_ref.dtype)
        va_ref[...] = kv[:, P_V - P_KR:].astype(va_ref.dtype)
        for a, b, f0, gref, dst in ((P_QLAT, P_KR, F_QLAT, gq_ref, cq_ref), (P_KVLAT, P_W, F_KVLAT, gkv_ref, ckv_ref)):
            v = mm(a, b)
            pf_ref[:, f0:f0 + b - a] = v
            r = lax.rsqrt(jnp.mean(v * v, axis=-1, keepdims=True) + EPS)
            dst[...] = (v * r * gref[...]).astype(dst.dtype)

    tab = _rows(tm, LANES)
    widths = (c, SWA_HEADS * LANES, SWA_KV_HEADS * LANES, SWA_KV_HEADS * LANES, Q_LORA, KV_LORA)
    return pl.pallas_call(
        body, name="proj_in", grid=(s_ // tm,),
        in_specs=[_rows(tm, c), _const(1, c), pl.BlockSpec((P_W, c), lambda i: (0, 0), pipeline_mode=pl.Buffered(1)),
                  _const(1, Q_LORA), _const(1, KV_LORA), tab, tab, tab, tab, tab, tab],
        out_specs=[_rows(tm, w) for w in widths] + [tab, _rows(tm, F_W)],
        out_shape=[jax.ShapeDtypeStruct((s_, w), MXU_DTYPE) for w in widths]
        + [jax.ShapeDtypeStruct((s_, LANES), F32), jax.ShapeDtypeStruct((s_, F_W), F32)],
        compiler_params=_cparams(("parallel",)),
    )(x, g, w_t, gq, gkv, a_cos, a_up, a_dn, k_cos, b_up, b_dn)


def _mm_norm_bwd(a, b, x, g, res, *, name, after=None, tm=512):
    s_, kk = a.shape
    c = b.shape[1]
    has_after = after is not None

    def body(*refs):
        a_ref, b_ref, x_ref, g_ref, res_ref = refs[:5]
        dx_ref, dxb_ref, dg_ref = refs[5 + has_after:]
        d = jnp.dot(a_ref[...], b_ref[...], preferred_element_type=F32)
        dx, gg = _rms_bwd(x_ref[...], g_ref[...], d)
        dx = dx + res_ref[...]
        dx_ref[...] = dx
        dxb_ref[...] = dx.astype(dxb_ref.dtype)

        @pl.when(pl.program_id(0) == 0)
        def _():
            dg_ref[...] = jnp.zeros(dg_ref.shape, F32)

        dg_ref[...] += _sublane_sum(gg)

    row = _rows(tm, c)
    in_specs = [_rows(tm, kk), pl.BlockSpec((kk, c), lambda i: (0, 0), pipeline_mode=pl.Buffered(1)), row, _const(1, c), row]
    return pl.pallas_call(
        body, name=name, grid=(s_ // tm,), in_specs=in_specs + ([pl.BlockSpec(memory_space=pl.ANY)] if has_after else []),
        out_specs=[row, row, _const(SUBLANES, c)],
        out_shape=[jax.ShapeDtypeStruct((s_, c), F32), jax.ShapeDtypeStruct((s_, c), MXU_DTYPE),
                   jax.ShapeDtypeStruct((SUBLANES, c), F32)],
        compiler_params=_cparams(("arbitrary",)),
    )(*([a, b, x, g, res] + ([after] if has_after else [])))


def _mla_up(cq, ckv, kro, wuq, wuk, wuv, tabs, *, ts=512):
    s_ = cq.shape[0]
    _, _, _, q_cos, _, b_up, b_dn = tabs

    def body(cq_ref, ckv_ref, kr_ref, wq_ref, wk_ref, wv_ref, qc, bu, bd, qo_ref, ko_ref, vo_ref):
        c_, u_, d_ = qc[...], bu[...], bd[...]
        kr = kr_ref[...]
        ckv_ = ckv_ref[...]
        vo_ref[...] = jnp.dot(ckv_, wv_ref[...], preferred_element_type=F32).astype(vo_ref.dtype)
        q = jnp.dot(cq_ref[...], wq_ref[...], preferred_element_type=F32)
        k = jnp.dot(ckv_, wk_ref[...], preferred_element_type=F32)
        for h in range(MLA_HEADS):
            sl = slice(h * LANES, (h + 1) * LANES)
            qo_ref[:, sl] = _rope(q[:, sl], c_, u_, d_, 112, 16).astype(qo_ref.dtype)
            ko_ref[:, sl] = (k[:, sl] + kr).astype(ko_ref.dtype)

    tab, out = _rows(ts, LANES), _rows(ts, 1024)
    return pl.pallas_call(
        body, name="mla_up", grid=(s_ // ts,),
        in_specs=[_rows(ts, Q_LORA), _rows(ts, KV_LORA), tab, _const(Q_LORA, 1024), _const(KV_LORA, 1024),
                  _const(KV_LORA, 1024), tab, tab, tab],
        out_specs=[out, out, out], out_shape=[jax.ShapeDtypeStruct((s_, 1024), MXU_DTYPE)] * 3,
        compiler_params=_cparams(("parallel",)),
    )(cq, ckv, kro, wuq, wuk, wuv, q_cos, b_up, b_dn)


def _mla_up_bwd(dqc, dkc, dvp, wuq, wukv, p, gq, gkv, tabs, *, ts=256):
    s_ = dqc.shape[0]
    _, _, _, q_cos, k_cos, b_up, b_dn = tabs

    def body(dq_ref, dk_ref, dv_ref, wq_ref, wkv_ref, ql_ref, kvl_ref, gq_ref, gkv_ref, qc, kc, bu, bd,
             dqo_ref, dkvo_ref, dkr_ref, dql_ref, dkvl_ref, dgq_ref, dgkv_ref):
        c_, u_, d_ = qc[...], bu[...], bd[...]
        tot = jnp.zeros((ts, LANES), F32)
        for h in range(MLA_HEADS):
            sl = slice(h * LANES, (h + 1) * LANES)
            dqo_ref[:, sl] = _rope_t(dq_ref[:, sl], c_, u_, d_, 112, 16).astype(dqo_ref.dtype)
            dk = dk_ref[:, sl]
            dkvo_ref[:, sl] = dk.astype(dkvo_ref.dtype)
            tot = tot + dk
        dkvo_ref[:, 1024:2048] = dv_ref[...].astype(dkvo_ref.dtype)
        dkr_ref[...] = _rope_t(tot, kc[...], u_, d_, 112, 16).astype(dkr_ref.dtype)

        @pl.when(pl.program_id(0) == 0)
        def _():
            dgq_ref[...] = jnp.zeros(dgq_ref.shape, F32)
            dgkv_ref[...] = jnp.zeros(dgkv_ref.shape, F32)

        for do_ref, w_ref, x_ref, g_ref, dx_ref, dg_ref in ((dqo_ref, wq_ref, ql_ref, gq_ref, dql_ref, dgq_ref),
                                                            (dkvo_ref, wkv_ref, kvl_ref, gkv_ref, dkvl_ref, dgkv_ref)):
            d = lax.dot_general(do_ref[...], w_ref[...], NT, preferred_element_type=F32)
            dx, gg = _rms_bwd(x_ref[...], g_ref[...], d)
            dx_ref[...] = dx.astype(dx_ref.dtype)
            dg_ref[...] += _sublane_sum(gg)

    tab = _rows(ts, LANES)
    return pl.pallas_call(
        body, name="mla_up_bwd", grid=(s_ // ts,),
        in_specs=[_rows(ts, 1024), _rows(ts, 1024), _rows(ts, 1024), _const(Q_LORA, 1024), _const(KV_LORA, 2048),
                  _rows(ts, Q_LORA, F_QLAT // Q_LORA), _rows(ts, KV_LORA, F_KVLAT // KV_LORA),
                  _const(1, Q_LORA), _const(1, KV_LORA), tab, tab, tab, tab],
        out_specs=[_rows(ts, 1024), _rows(ts, 2048), _rows(ts, LANES), _rows(ts, Q_LORA), _rows(ts, KV_LORA),
                   _const(SUBLANES, Q_LORA), _const(SUBLANES, KV_LORA)],
        out_shape=[jax.ShapeDtypeStruct((s_, 1024), MXU_DTYPE), jax.ShapeDtypeStruct((s_, 2048), MXU_DTYPE),
                   jax.ShapeDtypeStruct((s_, LANES), MXU_DTYPE), jax.ShapeDtypeStruct((s_, Q_LORA), MXU_DTYPE),
                   jax.ShapeDtypeStruct((s_, KV_LORA), MXU_DTYPE), jax.ShapeDtypeStruct((SUBLANES, Q_LORA), F32),
                   jax.ShapeDtypeStruct((SUBLANES, KV_LORA), F32)],
        compiler_params=_cparams(("arbitrary",)),
    )(dqc, dkc, dvp, wuq, wukv, p, p, gq, gkv, q_cos, k_cos, b_up, b_dn)


def _assemble_dp(dgab, dqa, dqlat, dkr, dka, dva, dkvlat, tabs, *, ts=256):
    s_ = dqa.shape[0]
    a_cos, a_up, a_dn = tabs[0], tabs[1], tabs[2]

    def body(dg_ref, dq_ref, dql_ref, dkr_ref, dk_ref, dv_ref, dkvl_ref, ac, au, ad, o_ref):
        c_, u_, d_ = ac[...], au[...], ad[...]
        o_ref[:, P_GA:P_Q] = dg_ref[...]
        for h in range(SWA_HEADS):
            sl = slice(h * LANES, (h + 1) * LANES)
            o_ref[:, P_Q + h * LANES:P_Q + (h + 1) * LANES] = _rope_t(dq_ref[:, sl], c_, u_, d_, 96, 32).astype(o_ref.dtype)
        o_ref[:, P_QLAT:P_KR] = dql_ref[...]
        o_ref[:, P_KR:P_K] = dkr_ref[...]
        for h in range(SWA_KV_HEADS):
            sl = slice(h * LANES, (h + 1) * LANES)
            o_ref[:, P_K + h * LANES:P_K + (h + 1) * LANES] = _rope_t(dk_ref[:, sl], c_, u_, d_, 96, 32).astype(o_ref.dtype)
        o_ref[:, P_V:P_KVLAT] = dv_ref[...]
        o_ref[:, P_KVLAT:P_W] = dkvl_ref[...]

    tab = _rows(ts, LANES)
    return pl.pallas_call(
        body, name="assemble_dp", grid=(s_ // ts,),
        in_specs=[_rows(ts, 2048), _rows(ts, 1024), _rows(ts, Q_LORA), _rows(ts, LANES), _rows(ts, 256), _rows(ts, 256),
                  _rows(ts, KV_LORA), tab, tab, tab],
        out_specs=_rows(ts, P_W), out_shape=jax.ShapeDtypeStruct((s_, P_W), MXU_DTYPE),
        compiler_params=_cparams(("parallel",)),
    )(dgab, dqa, dqlat, dkr, dka, dva, dkvlat, a_cos, a_up, a_dn)


def _attn_out_gate(oa, ob, woa_t, wob_t, p, *, ts=512):
    s_ = p.shape[0]

    def body(oa_ref, ob_ref, wa_ref, wb_ref, ga_ref, gb_ref, ta_ref, tb_ref, y_ref):
        ta = lax.dot_general(oa_ref[...], wa_ref[...], NT, preferred_element_type=F32)
        tb = lax.dot_general(ob_ref[...], wb_ref[...], NT, preferred_element_type=F32)
        ta_ref[...] = ta
        tb_ref[...] = tb
        y_ref[...] = (_sigmoid(ga_ref[...]) * ta + _sigmoid(gb_ref[...]) * tb).astype(y_ref.dtype)

    w = _const(1024, 1024)
    return pl.pallas_call(
        body, name="attn_out_gate", grid=(s_ // ts,),
        in_specs=[_rows(ts, 1024), _rows(ts, 1024), w, w, _rows(ts, 1024, F_GA // 1024), _rows(ts, 1024, F_GB // 1024)],
        out_specs=[_rows(ts, 1024)] * 3,
        out_shape=[jax.ShapeDtypeStruct((s_, 1024), F32)] * 2 + [jax.ShapeDtypeStruct((s_, 1024), MXU_DTYPE)],
        compiler_params=_cparams(("parallel",)),
    )(oa, ob, woa_t, wob_t, p, p)


def _d_y_gate(dx1b, wout, p, ta, tb, *, ts=512):
    s_ = p.shape[0]

    def body(dx_ref, w_ref, ga_ref, gb_ref, ta_ref, tb_ref, dta_ref, dtb_ref, dg_ref):
        d = lax.dot_general(dx_ref[...], w_ref[...], NT, preferred_element_type=F32)
        sa, sb = _sigmoid(ga_ref[...]), _sigmoid(gb_ref[...])
        dta_ref[...] = (d * sa).astype(dta_ref.dtype)
        dtb_ref[...] = (d * sb).astype(dtb_ref.dtype)
        dg_ref[:, 0:1024] = (d * ta_ref[...] * (sa * (1.0 - sa))).astype(dg_ref.dtype)
        dg_ref[:, 1024:2048] = (d * tb_ref[...] * (sb * (1.0 - sb))).astype(dg_ref.dtype)

    return pl.pallas_call(
        body, name="d_y_gate", grid=(s_ // ts,),
        in_specs=[_rows(ts, 1024), _const(1024, 1024), _rows(ts, 1024, F_GA // 1024), _rows(ts, 1024, F_GB // 1024),
                  _rows(ts, 1024), _rows(ts, 1024)],
        out_specs=[_rows(ts, 1024), _rows(ts, 1024), _rows(ts, 2048)],
        out_shape=[jax.ShapeDtypeStruct((s_, 1024), MXU_DTYPE)] * 2 + [jax.ShapeDtypeStruct((s_, 2048), MXU_DTYPE)],
        compiler_params=_cparams(("parallel",)),
    )(dx1b, wout, p, p, ta, tb)


FF_TILE = D_FF // 2


def _ffn_in_act(x1, g, wgu_t, *, tm=512):
    s_ = x1.shape[0]
    n = s_ // tm

    def body(x_ref, g_ref, w_ref, h_ref, gu_ref, a_ref):
        h = _rms(x_ref[...], g_ref[...]).astype(h_ref.dtype)
        h_ref[...] = h
        p = lax.dot_general(h, w_ref[...], NT, preferred_element_type=F32)
        gu_ref[...] = p
        gate = p[:, :FF_TILE]
        a_ref[...] = (gate * _sigmoid(gate) * p[:, FF_TILE:]).astype(a_ref.dtype)

    return pl.pallas_call(
        body, name="ffn_in", grid=(2, s_ // tm),
        in_specs=[pl.BlockSpec((tm, D_MODEL), lambda j, i: (i, 0)), pl.BlockSpec((1, D_MODEL), lambda j, i: (0, 0)),
                  pl.BlockSpec((2 * FF_TILE, D_MODEL), lambda j, i: (j, 0))],
        out_specs=[pl.BlockSpec((tm, D_MODEL), lambda j, i: (i + j * (n - 1 - i), 0)),
                   pl.BlockSpec((tm, 2 * FF_TILE), lambda j, i: (i, j)),
                   pl.BlockSpec((tm, FF_TILE), lambda j, i: (i, j))],
        out_shape=[jax.ShapeDtypeStruct((s_, D_MODEL), MXU_DTYPE), jax.ShapeDtypeStruct((s_, 2 * D_FF), F32),
                   jax.ShapeDtypeStruct((s_, D_FF), MXU_DTYPE)],
        compiler_params=_cparams(("arbitrary", "arbitrary")),
    )(x1, g, wgu_t)


def _d_act_swiglu(dx2b, wd, gu, *, tm=512):
    s_ = dx2b.shape[0]

    def body(d_ref, w_ref, gu_ref, o_ref):
        da = lax.dot_general(d_ref[...], w_ref[...], NT, preferred_element_type=F32)
        g, u = gu_ref[:, :FF_TILE], gu_ref[:, FF_TILE:]
        sg = _sigmoid(g)
        o_ref[:, :FF_TILE] = (da * u * (sg * (1.0 + g * (1.0 - sg)))).astype(o_ref.dtype)
        o_ref[:, FF_TILE:] = (da * (g * sg)).astype(o_ref.dtype)

    gu_spec = pl.BlockSpec((tm, 2 * FF_TILE), lambda j, i: (i, j))
    return pl.pallas_call(
        body, name="d_act", grid=(2, s_ // tm),
        in_specs=[pl.BlockSpec((tm, D_MODEL), lambda j, i: (i, 0)), pl.BlockSpec((FF_TILE, D_MODEL), lambda j, i: (j, 0)), gu_spec],
        out_specs=gu_spec, out_shape=jax.ShapeDtypeStruct((s_, 2 * D_FF), MXU_DTYPE),
        compiler_params=_cparams(("parallel", "parallel")),
    )(dx2b, wd, gu)


def _ffn_out_loss(act, wd, x1, g, tgt, *, ts=512):
    s_, c = x1.shape
    kk = act.shape[1]

    def body(a_ref, w_ref, x_ref, g_ref, t_ref, dx_ref, dxb_ref, dg_ref, lp_ref, tot_ref):
        v = x_ref[...] + jnp.dot(a_ref[...], w_ref[...], preferred_element_type=F32)
        r = lax.rsqrt(jnp.mean(v * v, axis=-1, keepdims=True) + EPS)
        xh = v * r
        gg = g_ref[...]
        e = xh * gg - t_ref[...]
        do = e * (1.0 / c)
        dxh = do * gg
        dx = r * (dxh - xh * jnp.mean(dxh * xh, axis=-1, keepdims=True))
        dx_ref[...] = dx
        dxb_ref[...] = dx.astype(dxb_ref.dtype)
        i = pl.program_id(0)

        @pl.when(i == 0)
        def _():
            dg_ref[...] = jnp.zeros(dg_ref.shape, F32)
            lp_ref[...] = jnp.zeros(lp_ref.shape, F32)

        dg_ref[...] += _sublane_sum(do * xh)
        lp_ref[...] += _sublane_sum(e * e)
        tot_ref[...] = jnp.full(tot_ref.shape, (0.5 / c) * jnp.sum(lp_ref[...]), F32)

    return pl.pallas_call(
        body, name="ffn_out_loss", grid=(s_ // ts,),
        in_specs=[_rows(ts, kk), _const(kk, c), _rows(ts, c), _const(1, c), _rows(ts, c)],
        out_specs=[_rows(ts, c), _rows(ts, c), _const(SUBLANES, c), _const(SUBLANES, c), _const(SUBLANES, LANES)],
        out_shape=[jax.ShapeDtypeStruct((s_, c), F32), jax.ShapeDtypeStruct((s_, c), MXU_DTYPE),
                   jax.ShapeDtypeStruct((SUBLANES, c), F32), jax.ShapeDtypeStruct((SUBLANES, c), F32),
                   jax.ShapeDtypeStruct((SUBLANES, LANES), F32)],
        compiler_params=_cparams(("arbitrary",)),
    )(act, wd, x1, g, tgt)


def _mla_d_out(dtb, wob_t, o32, *, ts=512):
    s_ = dtb.shape[0]

    def body(dt_ref, w_ref, o_ref, dob_ref, dl_ref):
        d = jnp.dot(dt_ref[...], w_ref[...], preferred_element_type=F32)
        dob_ref[...] = d.astype(dob_ref.dtype)
        prod = d * o_ref[...]
        for h in range(MLA_HEADS):
            dl_ref[h] = jnp.sum(prod[:, h * LANES:(h + 1) * LANES].T, axis=0, keepdims=True)

    return pl.pallas_call(
        body, name="mla_d_out", grid=(s_ // ts,), in_specs=[_rows(ts, 1024), _const(1024, 1024), _rows(ts, 1024)],
        out_specs=[_rows(ts, 1024), pl.BlockSpec((MLA_HEADS, 1, ts), lambda i: (0, 0, i))],
        out_shape=[jax.ShapeDtypeStruct((s_, 1024), MXU_DTYPE), jax.ShapeDtypeStruct((MLA_HEADS, 1, s_), F32)],
        compiler_params=_cparams(("parallel",)),
    )(dtb, wob_t, o32)


SWA_T = 4 * BLOCK


SWA_W = SWA_GROUP * BLOCK


def _swa_masks(sb):
    kr = lax.broadcasted_iota(jnp.int32, (2 * BLOCK, SWA_W), 0)
    qc = jnp.bitwise_and(lax.broadcasted_iota(jnp.int32, (2 * BLOCK, SWA_W), 1), BLOCK - 1)
    band = jnp.logical_and(kr > qc, kr <= qc + BLOCK)
    first = jnp.logical_and(band, kr >= BLOCK)
    return band, jnp.logical_or(first, jnp.logical_and(band, sb > 0))


def _heads_to_rows(ref, rs):
    return jnp.concatenate([ref[rs, h * LANES:(h + 1) * LANES] for h in range(SWA_GROUP)], axis=0)


def _sink_row(sk_ref):
    return jnp.concatenate([sk_ref[0, h:h + 1, :] for h in range(SWA_GROUP)], axis=1) * LOG2E


def _swa_in_specs(rev, nsb):
    sbi = (lambda j: nsb - 1 - j) if rev else (lambda j: j)
    cur = pl.BlockSpec((SWA_T, LANES), lambda g, j: (sbi(j), g))
    prev = pl.BlockSpec((BLOCK, LANES), lambda g, j: (jnp.maximum(4 * sbi(j) - 1, 0), g))
    q = pl.BlockSpec((SWA_T, SWA_GROUP * LANES), lambda g, j: (sbi(j), g))
    sink = pl.BlockSpec((1, SUBLANES, LANES), lambda g, j: (g, 0, 0))
    lse = pl.BlockSpec((SWA_GROUP, 1, SWA_T), lambda g, j: (g, 0, sbi(j)))
    return q, cur, prev, sink, lse


def _swa_fwd(qa, ka, va, sink_b):
    s_ = qa.shape[0]
    nsb = s_ // SWA_T
    c2 = HEAD_DIM ** -0.5 * LOG2E

    def body(q_ref, kc_ref, kp_ref, vc_ref, vp_ref, sk_ref, o32_ref, o16_ref, lse_ref, kx, vx):
        kx[0:BLOCK, :] = kp_ref[...]
        kx[BLOCK:5 * BLOCK, :] = kc_ref[...]
        vx[0:BLOCK, :] = vp_ref[...]
        vx[BLOCK:5 * BLOCK, :] = vc_ref[...]
        band, band0 = _swa_masks(pl.program_id(1))
        sink2 = _sink_row(sk_ref)
        for b in range(4):
            rs = slice(b * BLOCK, (b + 1) * BLOCK)
            ks = slice(b * BLOCK, (b + 2) * BLOCK)
            st = lax.dot_general(kx[ks, :], _heads_to_rows(q_ref, rs), NT, preferred_element_type=F32) * c2
            st = jnp.where(band0 if b == 0 else band, st, -jnp.inf)
            m = jnp.maximum(jnp.max(st, axis=0, keepdims=True), sink2)
            pt = jnp.exp2(st - m)
            den = jnp.sum(pt, axis=0, keepdims=True) + jnp.exp2(sink2 - m)
            o = lax.dot_general((pt * (1.0 / den)).astype(MXU_DTYPE), vx[ks, :], TN, preferred_element_type=F32)
            lse = m + jnp.log2(den)
            for hh in range(SWA_GROUP):
                cs = slice(hh * LANES, (hh + 1) * LANES)
                o32_ref[rs, cs] = o[cs, :]
                o16_ref[rs, cs] = o[cs, :].astype(o16_ref.dtype)
                lse_ref[hh, :, rs] = lse[:, cs]

    q, cur, prev, sink, lse_spec = _swa_in_specs(False, nsb)
    return pl.pallas_call(
        body, name="swa_fwd", grid=(SWA_KV_HEADS, nsb), in_specs=[q, cur, prev, cur, prev, sink],
        out_specs=[q, q, lse_spec],
        out_shape=[jax.ShapeDtypeStruct((s_, SWA_HEADS * LANES), F32), jax.ShapeDtypeStruct((s_, SWA_HEADS * LANES), MXU_DTYPE),
                   jax.ShapeDtypeStruct((SWA_HEADS, 1, s_), F32)],
        scratch_shapes=[pltpu.VMEM((5 * BLOCK, LANES), MXU_DTYPE), pltpu.VMEM((5 * BLOCK, LANES), MXU_DTYPE)],
        compiler_params=_cparams(("parallel", "arbitrary")),
    )(qa, ka, ka, va, va, sink_b)


def _swa_bwd(qa, ka, va, sink_b, o32, do, lse):
    s_ = qa.shape[0]
    nsb = s_ // SWA_T
    scale = HEAD_DIM ** -0.5
    c2 = scale * LOG2E

    def body(q_ref, kc_ref, kp_ref, vc_ref, vp_ref, sk_ref, o_ref, do_ref, lse_ref,
             dq_ref, dk_ref, dv_ref, dsk_ref, kx, vx, kacc, vacc, kcar, vcar):
        j = pl.program_id(1)
        kx[0:BLOCK, :] = kp_ref[...]
        kx[BLOCK:5 * BLOCK, :] = kc_ref[...]
        vx[0:BLOCK, :] = vp_ref[...]
        vx[BLOCK:5 * BLOCK, :] = vc_ref[...]
        band, band0 = _swa_masks(nsb - 1 - j)
        kacc[...] = jnp.zeros(kacc.shape, F32)
        vacc[...] = jnp.zeros(vacc.shape, F32)

        @pl.when(j == 0)
        def _():
            kcar[...] = jnp.zeros(kcar.shape, F32)
            vcar[...] = jnp.zeros(vcar.shape, F32)
            dsk_ref[...] = jnp.zeros(dsk_ref.shape, F32)

        sink2 = _sink_row(sk_ref)
        dsink = jnp.zeros((1, SWA_W), F32)
        for b in range(4):
            rs = slice(b * BLOCK, (b + 1) * BLOCK)
            ks = slice(b * BLOCK, (b + 2) * BLOCK)
            q, k2, v2 = _heads_to_rows(q_ref, rs), kx[ks, :], vx[ks, :]
            d = _heads_to_rows(do_ref, rs)
            delta = jnp.sum((d * _heads_to_rows(o_ref, rs)).T, axis=0, keepdims=True)
            l2 = jnp.concatenate([lse_ref[hh, :, rs] for hh in range(SWA_GROUP)], axis=1)
            st = lax.dot_general(k2, q, NT, preferred_element_type=F32) * c2
            pt = jnp.exp2(jnp.where(band0 if b == 0 else band, st, -jnp.inf) - l2)
            db = d.astype(MXU_DTYPE)
            dst = (pt * (lax.dot_general(v2, db, NT, preferred_element_type=F32) - delta) * scale).astype(MXU_DTYPE)
            dq = lax.dot_general(dst, k2, TN, preferred_element_type=F32)
            for hh in range(SWA_GROUP):
                dq_ref[rs, hh * LANES:(hh + 1) * LANES] = dq[hh * LANES:(hh + 1) * LANES, :]
            kacc[ks, :] += jnp.dot(dst, q, preferred_element_type=F32)
            vacc[ks, :] += jnp.dot(pt.astype(MXU_DTYPE), db, preferred_element_type=F32)
            dsink = dsink - jnp.exp2(sink2 - l2) * delta
        for hh in range(SWA_GROUP):
            tot = jnp.sum(dsink[:, hh * LANES:(hh + 1) * LANES], axis=1, keepdims=True)
            dsk_ref[0, hh:hh + 1, :] += jnp.broadcast_to(tot, (1, LANES))

        dk_ref[0:3 * BLOCK, :] = kacc[BLOCK:4 * BLOCK, :]
        dk_ref[3 * BLOCK:4 * BLOCK, :] = kacc[4 * BLOCK:5 * BLOCK, :] + kcar[...]
        dv_ref[0:3 * BLOCK, :] = vacc[BLOCK:4 * BLOCK, :].astype(dv_ref.dtype)
        dv_ref[3 * BLOCK:4 * BLOCK, :] = (vacc[4 * BLOCK:5 * BLOCK, :] + vcar[...]).astype(dv_ref.dtype)
        kcar[...] = kacc[0:BLOCK, :]
        vcar[...] = vacc[0:BLOCK, :]

    q, cur, prev, sink, lse_spec = _swa_in_specs(True, nsb)
    return pl.pallas_call(
        body, name="swa_bwd", grid=(SWA_KV_HEADS, nsb),
        in_specs=[q, cur, prev, cur, prev, sink, q, q, lse_spec],
        out_specs=[q, cur, cur, sink],
        out_shape=[jax.ShapeDtypeStruct((s_, SWA_HEADS * LANES), F32), jax.ShapeDtypeStruct((s_, SWA_KV_HEADS * LANES), F32),
                   jax.ShapeDtypeStruct((s_, SWA_KV_HEADS * LANES), MXU_DTYPE),
                   jax.ShapeDtypeStruct((SWA_KV_HEADS, SUBLANES, LANES), F32)],
        scratch_shapes=[pltpu.VMEM((5 * BLOCK, LANES), MXU_DTYPE), pltpu.VMEM((5 * BLOCK, LANES), MXU_DTYPE),
                        pltpu.VMEM((5 * BLOCK, LANES), F32), pltpu.VMEM((5 * BLOCK, LANES), F32),
                        pltpu.VMEM((BLOCK, LANES), F32), pltpu.VMEM((BLOCK, LANES), F32)],
        compiler_params=_cparams(("arbitrary", "arbitrary")),
    )(qa, ka, ka, va, va, sink_b, o32, do, lse)


MLA_T = 512
MLA_FWD_GROUP = 4
MLA_BWD_GROUP = 2


def _mla_specs(s_, t, group):
    w = group * LANES
    qs = pl.BlockSpec((t, w), lambda g, i: (i, g))
    kv = pl.BlockSpec((s_, w), lambda g, i: (0, g))
    row = pl.BlockSpec((group, 1, t), lambda g, i: (g, 0, i))
    return qs, kv, row


def _causal_scores_t(k, q, t, c2, masked):
    st = lax.dot_general(k, q, NT, preferred_element_type=F32) * c2
    if masked:
        kr = lax.broadcasted_iota(jnp.int32, (t, t), 0)
        qc = lax.broadcasted_iota(jnp.int32, (t, t), 1)
        st = jnp.where(kr <= qc, st, -jnp.inf)
    return st


def _mla_fwd(qc, kc, vp):
    s_ = qc.shape[0]
    t = min(MLA_T, s_)
    c2 = MLA_QK ** -0.5 * LOG2E
    grp = MLA_FWD_GROUP

    def body(q_ref, k_ref, v_ref, o32_ref, o16_ref, lse_ref, m_s, acc_s):
        qi = pl.program_id(1)
        m_s[...] = jnp.full(m_s.shape, -jnp.inf, F32)
        acc_s[...] = jnp.zeros(acc_s.shape, F32)
        ones_lane = lax.broadcasted_iota(jnp.int32, (t, LANES), 1) == MLA_V

        def step(ki, masked):
            off = pl.multiple_of(ki * t, t)
            for g in range(grp):
                cs = slice(g * LANES, (g + 1) * LANES)
                st = _causal_scores_t(k_ref[pl.ds(off, t), cs], q_ref[:, cs], t, c2, masked)
                m_old = m_s[g]
                m_new = jnp.maximum(m_old, jnp.max(st, axis=0, keepdims=True))
                alpha = jnp.exp2(m_old - m_new)
                pt = jnp.exp2(st - m_new).astype(MXU_DTYPE)
                v = v_ref[pl.ds(off, t), cs]
                v = jnp.where(ones_lane, jnp.ones((), v.dtype), v)
                acc_s[g] = alpha * acc_s[g] + lax.dot_general(v, pt, TN, preferred_element_type=F32)
                m_s[g] = m_new

        def full_block(ki, carry):
            step(ki, False)
            return carry

        lax.fori_loop(0, qi, full_block, 0)
        step(qi, True)
        for g in range(grp):
            cs = slice(g * LANES, (g + 1) * LANES)
            acc = acc_s[g]
            l = acc[MLA_V:MLA_V + 1, :]
            o = (acc * (1.0 / l)).T
            o32_ref[:, cs] = o
            o16_ref[:, cs] = o.astype(o16_ref.dtype)
            lse_ref[g] = m_s[g] + jnp.log2(l)

    qs, kv, row = _mla_specs(s_, t, grp)
    return pl.pallas_call(
        body, name="mla_fwd", grid=(MLA_HEADS // grp, s_ // t), in_specs=[qs, kv, kv], out_specs=[qs, qs, row],
        out_shape=[jax.ShapeDtypeStruct((s_, MLA_HEADS * LANES), F32), jax.ShapeDtypeStruct((s_, MLA_HEADS * LANES), MXU_DTYPE),
                   jax.ShapeDtypeStruct((MLA_HEADS, 1, s_), F32)],
        scratch_shapes=[pltpu.VMEM((grp, 1, t), F32), pltpu.VMEM((grp, LANES, t), F32)],
        compiler_params=_cparams(("parallel", "arbitrary")),
    )(qc, kc, vp)


def _mla_bwd(qc, kc, vp, dob, lse, delta):
    s_ = qc.shape[0]
    t = min(MLA_T, s_)
    scale = MLA_QK ** -0.5
    c2 = scale * LOG2E
    grp = MLA_BWD_GROUP

    def body(q_ref, do_ref, lse_ref, dl_ref, k_ref, v_ref, dq_ref, dk_ref, dv_ref, dqt_s):
        qi = pl.program_id(1)

        @pl.when(qi == 0)
        def _():
            dk_ref[...] = jnp.zeros(dk_ref.shape, F32)
            dv_ref[...] = jnp.zeros(dv_ref.shape, F32)

        dqt_s[...] = jnp.zeros(dqt_s.shape, F32)

        def step(ki, masked):
            off = pl.multiple_of(ki * t, t)
            for g in range(grp):
                cs = slice(g * LANES, (g + 1) * LANES)
                q, d, k = q_ref[:, cs], do_ref[:, cs], k_ref[pl.ds(off, t), cs]
                pt = jnp.exp2(_causal_scores_t(k, q, t, c2, masked) - lse_ref[g])
                dpt = lax.dot_general(v_ref[pl.ds(off, t), cs], d, NT, preferred_element_type=F32)
                dst = (pt * (dpt - dl_ref[g]) * scale).astype(MXU_DTYPE)
                dv_ref[pl.ds(off, t), cs] += jnp.dot(pt.astype(MXU_DTYPE), d, preferred_element_type=F32)
                dk_ref[pl.ds(off, t), cs] += jnp.dot(dst, q, preferred_element_type=F32)
                dqt_s[g] += lax.dot_general(k, dst, TN, preferred_element_type=F32)

        def full_block(ki, carry):
            step(ki, False)
            return carry

        lax.fori_loop(0, qi, full_block, 0)
        step(qi, True)
        for g in range(grp):
            dq_ref[:, g * LANES:(g + 1) * LANES] = dqt_s[g].T

    qs, kv, row = _mla_specs(s_, t, grp)
    shp = jax.ShapeDtypeStruct((s_, MLA_HEADS * LANES), F32)
    return pl.pallas_call(
        body, name="mla_bwd", grid=(MLA_HEADS // grp, s_ // t), in_specs=[qs, qs, row, row, kv, kv],
        out_specs=[qs, kv, kv], out_shape=[shp, shp, shp], scratch_shapes=[pltpu.VMEM((grp, LANES, t), F32)],
        compiler_params=_cparams(("parallel", "arbitrary")),
    )(qc, dob, lse, delta, kc, vp)


def _pad_heads(w, nh, hd, axis):
    shp = w.shape
    w = w.reshape(shp[:axis] + (nh, hd) + shp[axis + 1:])
    pad = [(0, 0)] * w.ndim
    pad[axis + 1] = (0, LANES - hd)
    w = jnp.pad(w, pad)
    return w.reshape(shp[:axis] + (nh * LANES,) + shp[axis + 1:])


def _unpad_heads(w, nh, hd, axis):
    shp = w.shape
    w = w.reshape(shp[:axis] + (nh, LANES) + shp[axis + 1:])
    w = lax.slice_in_dim(w, 0, hd, axis=axis + 1)
    return w.reshape(shp[:axis] + (nh * hd,) + shp[axis + 1:])


PACK_W = 1024
ROW_TILE = 16
FULL_SHAPE = dict(w_in=(1024, 3488), w_uq=(384, 768), w_ukv=(256, 1024), w_o_swa=(512, 1024), w_o_mla=(512, 1024),
                  w_out=(1024, 1024), w_gate=(1024, 2816), w_up=(1024, 2816), w_down=(2816, 1024))
BIG = tuple(FULL_SHAPE)
ROW_SHARDED = ("w_out", "w_down")
W_IN_COLS = FULL_SHAPE["w_in"][1] // N_DEV
W_IN_ROWS = -(-W_IN_COLS // ROW_TILE) * ROW_TILE
FF_COLS = D_FF // N_DEV
OUT_ROWS = D_MODEL // N_DEV
SMALL_FLAT = (("w_uq", 0, 36), ("w_ukv", 48, 32))
SMALL_USED = 80
MID_BLOCKS = 4
MID_ROWS = MID_BLOCKS * OUT_ROWS
EARLY_ROWS = W_IN_ROWS + MID_ROWS
LATE_ROWS = 3 * FF_COLS
PACK_ROWS = EARLY_ROWS + LATE_ROWS


def _shard_shape(n):
    r, c = FULL_SHAPE[n]
    return (r // N_DEV, c) if n in ROW_SHARDED else (r, c // N_DEV)


def _wire_pack(sh, dtype):
    c = lambda n: sh[n].astype(dtype)
    rows = [jnp.pad(c("w_in").T, ((0, W_IN_ROWS - W_IN_COLS), (0, 0))), c("w_out"),
            _pad_heads(c("w_o_swa").T, SWA_HEADS, HEAD_DIM, 1), _pad_heads(c("w_o_mla").T, MLA_HEADS, MLA_V, 1)]
    for n, _, r in SMALL_FLAT:
        rows.append(jnp.pad(c(n).reshape(r, PACK_W), ((0, -r % ROW_TILE), (0, 0))))
    rows.append(jnp.zeros((OUT_ROWS - SMALL_USED, PACK_W), dtype))
    return jnp.concatenate(rows + [c("w_gate").T, c("w_up").T, c("w_down")], 0)


def _mid_unpack(p):
    out = dict(w_out=p[0:OUT_ROWS], w_o_swa=_unpad_heads(p[OUT_ROWS:2 * OUT_ROWS], SWA_HEADS, HEAD_DIM, 1).T,
               w_o_mla=_unpad_heads(p[2 * OUT_ROWS:3 * OUT_ROWS], MLA_HEADS, MLA_V, 1).T)
    for n, off, r in SMALL_FLAT:
        out[n] = p[3 * OUT_ROWS + off:3 * OUT_ROWS + off + r].reshape(_shard_shape(n))
    return out


def _w_in_row_maps():
    sp = lambda col: (col // W_IN_COLS) * W_IN_ROWS + col % W_IN_COLS
    fwd = np.full((P_W,), -1, np.int64)

    def put(t0, c0, n):
        fwd[t0:t0 + n] = [sp(c) for c in range(c0, c0 + n)]

    put(P_GA, IN_OFF[6], D_MODEL)
    put(P_GB, IN_OFF[7], D_MODEL)
    for h in range(SWA_HEADS):
        put(P_Q + LANES * h, IN_OFF[0] + HEAD_DIM * h, HEAD_DIM)
    put(P_QLAT, IN_OFF[3], Q_LORA)
    put(P_KR + KR_LANE, IN_OFF[5], MLA_ROPE)
    for h in range(SWA_KV_HEADS):
        put(P_K + LANES * h, IN_OFF[1] + HEAD_DIM * h, HEAD_DIM)
        put(P_V + LANES * h, IN_OFF[2] + HEAD_DIM * h, HEAD_DIM)
    put(P_KVLAT, IN_OFF[4], KV_LORA)
    inv = np.full((N_DEV * W_IN_ROWS,), -1, np.int64)
    inv[fwd[fwd >= 0]] = np.nonzero(fwd >= 0)[0]
    return fwd, inv


def _take_rows(src, idx, *, name, tile=2 * LANES):
    n_out, n_src, width = len(idx), src.shape[0], src.shape[1]
    assert n_out % tile == 0 and n_src % tile == 0
    n_tiles = n_out // tile
    blocks = [sorted({int(v) // tile for v in idx[i * tile:(i + 1) * tile] if v >= 0}) for i in range(n_tiles)]
    k_max = max(1, max(len(b) for b in blocks))
    tab = np.zeros((n_tiles, k_max), np.int32)
    sel = np.zeros((n_tiles, k_max, tile, tile), np.float32)
    for i, blks in enumerate(blocks):
        for m, b in enumerate(blks):
            tab[i, m] = b
            for r in range(tile):
                v = int(idx[i * tile + r])
                if v >= 0 and v // tile == b:
                    sel[i, m, r, v % tile] = 1.0

    def body(tab_ref, sel_ref, *refs):
        o_ref = refs[k_max]
        acc = jnp.dot(sel_ref[0, 0], refs[0][...], preferred_element_type=F32)
        for m in range(1, k_max):
            acc = acc + jnp.dot(sel_ref[0, m], refs[m][...], preferred_element_type=F32)
        o_ref[...] = acc.astype(o_ref.dtype)

    def src_spec(m):
        return pl.BlockSpec((tile, width), lambda i, t: (t[i * k_max + m], 0))

    return pl.pallas_call(
        body, name=name,
        grid_spec=pltpu.PrefetchScalarGridSpec(
            num_scalar_prefetch=1, grid=(n_tiles,),
            in_specs=[pl.BlockSpec((1, k_max, tile, tile), lambda i, t: (i, 0, 0, 0))] + [src_spec(m) for m in range(k_max)],
            out_specs=pl.BlockSpec((tile, width), lambda i, t: (i, 0))),
        out_shape=jax.ShapeDtypeStruct((n_out, width), src.dtype),
        compiler_params=_cparams(("parallel",)),
    )(jnp.asarray(tab.reshape(-1)), jnp.asarray(sel, src.dtype), *([src] * k_max))


def _w_in_operand(win_g):
    return _take_rows(win_g.reshape(N_DEV * W_IN_ROWS, PACK_W), _w_in_row_maps()[0], name="w_in_rows")


def _mid_operands(wout_g, woa_g, wob_g, small_g):
    def full(n, off, r):
        a = small_g[:, off:off + r].reshape((N_DEV,) + _shard_shape(n))
        return jnp.moveaxis(a, 0, 1).reshape(FULL_SHAPE[n])

    w = {n: full(n, off, r) for n, off, r in SMALL_FLAT}
    ukv = w["w_ukv"].reshape(KV_LORA, MLA_HEADS, MLA_NOPE + MLA_V)
    return dict(
        wout=wout_g.reshape(D_MODEL, D_MODEL), woa_t=woa_g.reshape(D_MODEL, -1), wob_t=wob_g.reshape(D_MODEL, -1),
        wuq=_pad_heads(w["w_uq"], MLA_HEADS, MLA_QK, 1),
        wuk=_pad_heads(ukv[:, :, :MLA_NOPE].reshape(KV_LORA, -1), MLA_HEADS, MLA_NOPE, 1),
        wuv=_pad_heads(ukv[:, :, MLA_NOPE:].reshape(KV_LORA, -1), MLA_HEADS, MLA_V, 1),
    )


def _mid_grad_pack(g):
    uk = _unpad_heads(g["wukv"][:, :1024], MLA_HEADS, MLA_NOPE, 1).reshape(KV_LORA, MLA_HEADS, MLA_NOPE)
    uv = _unpad_heads(g["wukv"][:, 1024:], MLA_HEADS, MLA_V, 1).reshape(KV_LORA, MLA_HEADS, MLA_V)
    w = dict(w_uq=_unpad_heads(g["wuq"], MLA_HEADS, MLA_QK, 1), w_ukv=jnp.concatenate([uk, uv], 2).reshape(KV_LORA, -1))
    rows = []
    for n, _, r in SMALL_FLAT:
        rr, cc = FULL_SHAPE[n]
        a = jnp.moveaxis(w[n].reshape(rr, N_DEV, cc // N_DEV), 1, 0).reshape(N_DEV, r, PACK_W)
        rows.append(jnp.pad(a, ((0, 0), (0, -r % ROW_TILE), (0, 0))).astype(WIRE_DTYPE))
    rows.append(jnp.zeros((N_DEV, OUT_ROWS - SMALL_USED, PACK_W), WIRE_DTYPE))
    blk = lambda a: a.reshape(N_DEV, OUT_ROWS, PACK_W)
    return [blk(g["wout"]), blk(g["woa_t"]), blk(g["wob_t"]), jnp.concatenate(rows, 1)]


def _w_in_grad_chunks(g_win_t):
    return _take_rows(g_win_t, _w_in_row_maps()[1], name="dw_in_rows").reshape(N_DEV, W_IN_ROWS, PACK_W)


def _local_step(x, tgt, win_t, small, weights, grads):
    s_ = x.shape[0]
    tabs = _rope_tables(s_)
    sink_b = jnp.broadcast_to(small["swa_sinks"].reshape(SWA_KV_HEADS, SWA_GROUP, 1), (SWA_KV_HEADS, SWA_GROUP, LANES))
    sink_b = jnp.pad(sink_b, ((0, 0), (0, SUBLANES - SWA_GROUP), (0, 0)))

    h, qa, ka, va, cq, ckv, kro, p = _proj_in(x, small["mix_norm_g"], win_t, small["q_norm_g"], small["kv_norm_g"], tabs)
    oa32, oa16, lse_a = _swa_fwd(qa, ka, va, sink_b)
    ops = weights.mid(oa16)
    qc, kc, vp = _mla_up(cq, ckv, kro, ops["wuq"], ops["wuk"], ops["wuv"], tabs)
    ob32, ob16, lse_b = _mla_fwd(qc, kc, vp)
    ta, tb, y = _attn_out_gate(oa16, ob16, ops["woa_t"], ops["wob_t"], p)
    x1 = _mm(y, ops["wout"], "nn", name="out_proj", add=x, tm=1024, tn=1024)
    wgu_t, wd = weights.late(x1)
    h2, gu, act = _ffn_in_act(x1, small["ffn_norm_g"], wgu_t)

    dx2, dx2b, dg3, _, tot = _ffn_out_loss(act, wd, x1, small["final_norm_g"].reshape(1, D_MODEL), tgt)
    g = {}
    g_wd = _mm(act, dx2b, "tn", name="dw_down", tm=FF_TILE, tn=1024, tk=2048, out_dtype=WIRE_DTYPE)
    dgu = _d_act_swiglu(dx2b, wd, gu)
    g_wgu = _mm(dgu, h2, "tn", name="dw_ffn_in", tm=FF_TILE, tn=1024, tk=2048, out_dtype=WIRE_DTYPE)
    token = grads.late(g_wgu, g_wd)
    dx1, dx1b, dg2 = _mm_norm_bwd(dgu, wgu_t, x1, small["ffn_norm_g"] + token[0:1, 0:1], dx2, name="d_h2")
    g["wout"] = _mm(y, dx1b, "tn", name="dw_out", tm=1024, tn=1024, tk=1024, out_dtype=WIRE_DTYPE)
    dta, dtb, dgab = _d_y_gate(dx1b, ops["wout"], p, ta, tb)
    doa = _mm(dta, ops["woa_t"], "nn", name="d_oa", tm=1024, tn=1024)
    g["woa_t"] = _mm(dta, oa16, "tn", name="dw_o_swa", tm=1024, tn=1024, tk=1024, out_dtype=WIRE_DTYPE)
    g["wob_t"] = _mm(dtb, ob16, "tn", name="dw_o_mla", tm=1024, tn=1024, tk=1024, out_dtype=WIRE_DTYPE)
    dob16, delta_b = _mla_d_out(dtb, ops["wob_t"], ob32)
    dqc, dkc, dvp = _mla_bwd(qc, kc, vp, dob16, lse_b, delta_b)
    dqp, dkv, dkr, dqlat, dkvlat, dgq, dgkv = _mla_up_bwd(
        dqc, dkc, dvp, ops["wuq"], jnp.concatenate([ops["wuk"], ops["wuv"]], 1), p, small["q_norm_g"], small["kv_norm_g"], tabs)
    g["wuq"] = _mm(cq, dqp, "tn", name="dw_uq", tm=Q_LORA, tn=1024, tk=512)
    g["wukv"] = _mm(ckv, dkv, "tn", name="dw_ukv", tm=KV_LORA, tn=1024, tk=512)
    token = grads.mid(g)
    dqa, dka, dva, dsk = _swa_bwd(qa, ka, va, sink_b + token[0:1, 0:1], oa32, doa, lse_a)
    dp = _assemble_dp(dgab, dqa, dqlat, dkr, dka, dva, dkvlat, tabs)
    token = grads.last(_mm(dp, h, "tn", name="dw_in", tm=2176, tn=1024, tk=1024, out_dtype=WIRE_DTYPE))
    gx, _, dg1 = _mm_norm_bwd(dp, win_t, x, small["mix_norm_g"], dx1, name="d_h", after=token)

    sm = dict(mix_norm_g=dg1, ffn_norm_g=dg2, final_norm_g=dg3, q_norm_g=dgq, kv_norm_g=dgkv,
              swa_sinks=dsk[:, :SWA_GROUP, 0].reshape(1, SWA_HEADS))
    return tot, gx, sm


MESH = pl.DeviceIdType.MESH
ANY = pl.BlockSpec(memory_space=pl.ANY)


def _position():
    return lax.axis_index("x"), lax.axis_index("y"), lax.axis_index("c")


def _all_gather(block, pieces, shapes, *, name):
    n_out = len(shapes)
    n_rows = sum(p[3] for p in pieces)

    def body(x_ref, *refs):
        outs, (send_sems, recv_sems, local_sem) = refs[:n_out], refs[n_out:]
        x, y, c = _position()
        me, sibling = (x, y, c), (x, y, 1 - c)
        chips = [(1 - x, y), (x, 1 - y), (1 - x, 1 - y)]

        def dst(piece, blk):
            arr, lead, _, _ = piece
            return outs[arr].at[lead(4 * blk[0] + 2 * blk[1] + blk[2])]

        def own(piece):
            return x_ref.at[pl.ds(piece[2], piece[3])]

        def copies(k, blk, to, from_input):
            return [pltpu.make_async_remote_copy(
                src_ref=own(p) if from_input else dst(p, blk), dst_ref=dst(p, blk), send_sem=send_sems.at[k],
                recv_sem=recv_sems.at[k], device_id=to, device_id_type=MESH) for p in pieces]

        gathered_rows = x_ref.at[pl.ds(0, n_rows)]

        def whole_block(k):
            return pltpu.make_async_remote_copy(src_ref=gathered_rows, dst_ref=gathered_rows, send_sem=send_sems.at[k],
                                                recv_sem=recv_sems.at[k], device_id=me, device_id_type=MESH)

        for p in pieces:
            pltpu.make_async_copy(own(p), dst(p, me), local_sem).start()
        for cp in copies(0, me, sibling, True):
            cp.start()
        for j, chip in enumerate(chips):
            for cp in copies(1 + j, me, (*chip, c), True):
                cp.start()
        for j, chip in enumerate(chips):
            whole_block(1 + j).wait_recv()
            for cp in copies(4 + j, (*chip, c), sibling, False):
                cp.start()
        whole_block(0).wait_recv()
        for j in range(3):
            whole_block(4 + j).wait_recv()
        for k in range(7):
            whole_block(k).wait_send()
        pltpu.make_async_copy(gathered_rows, gathered_rows, local_sem).wait()

    return pl.pallas_call(
        body, name=name, out_shape=[jax.ShapeDtypeStruct(s, block.dtype) for s in shapes], in_specs=[ANY],
        out_specs=[ANY] * n_out,
        scratch_shapes=[pltpu.SemaphoreType.DMA((7,)), pltpu.SemaphoreType.DMA((7,)), pltpu.SemaphoreType.DMA],
    )(block)


HBM = pl.BlockSpec(memory_space=pltpu.HBM)
SEM = pl.BlockSpec(memory_space=pltpu.SEMAPHORE)
TILE_DEVS = FF_TILE // FF_COLS
GU_SHAPE = (2, 2, TILE_DEVS, FF_COLS, PACK_W)


def _gate_slab(d):
    return (d // TILE_DEVS, 0, d % TILE_DEVS)


def _up_slab(d):
    return (d // TILE_DEVS, 1, d % TILE_DEVS)
D_SHAPE = (N_DEV, FF_COLS, PACK_W)
LAND_SHAPE = (N_DEV, LATE_ROWS, PACK_W)


def _split_params():
    return pltpu.CompilerParams(has_side_effects=pltpu.SideEffectType.DATAFLOW_SIDE_EFFECTING)


def _peer(x, y, c, k):
    return ((1 - x) if k & 4 else x, (1 - y) if k & 2 else y, (1 - c) if k & 1 else c)


def _empty_hbm(shape, dtype):
    return pltpu.with_memory_space_constraint(lax.empty(shape, dtype), pltpu.HBM)


def _wait_all(rows, send_sems, recv_sems, me):
    for k in range(N_DEV - 1):
        cp = pltpu.make_async_remote_copy(src_ref=rows, dst_ref=rows, send_sem=send_sems.at[k], recv_sem=recv_sems.at[k],
                                          device_id=me, device_id_type=MESH)
        cp.wait_send()
        cp.wait_recv()


def _token_shape():
    return jax.ShapeDtypeStruct((SUBLANES, LANES), F32)


def _gather_start(pack, row0, pieces, shapes, *, name):
    n = len(shapes)

    def body(*refs):
        p_ref, bufs, send_sems, recv_sems, token = refs[0], refs[1:1 + n], refs[1 + n], refs[2 + n], refs[-1]
        x, y, c = _position()
        me = 4 * x + 2 * y + c
        for k in range(1, N_DEV):
            off = row0
            for buf, lead, rows in pieces:
                pltpu.make_async_remote_copy(
                    src_ref=p_ref.at[pl.ds(off, rows)], dst_ref=bufs[buf].at[lead(me)], send_sem=send_sems.at[k - 1],
                    recv_sem=recv_sems.at[k - 1], device_id=_peer(x, y, c, k), device_id_type=MESH).start()
                off += rows
        token[...] = jnp.zeros_like(token)

    sems, dt = pltpu.SemaphoreType.DMA((N_DEV - 1,)), pack.dtype
    return pl.pallas_call(
        body, name=name,
        out_shape=(sems, sems, pltpu.HBM(pack.shape, dt)) + tuple(pltpu.HBM(s, dt) for s in shapes) + (_token_shape(),),
        in_specs=(HBM,) * (1 + n), out_specs=(SEM, SEM) + (HBM,) * (1 + n) + (pl.BlockSpec(memory_space=pltpu.VMEM),),
        input_output_aliases={i: 2 + i for i in range(1 + n)}, compiler_params=_split_params(),
    )(pltpu.with_memory_space_constraint(pack, pltpu.HBM), *[_empty_hbm(s, dt) for s in shapes])


def _gather_wait(started, row0, n_rows, after, *, name):
    send_sems, recv_sems, pack, *bufs = started[:-1]
    n = len(bufs)

    def body(*refs):
        _wait_all(refs[0].at[pl.ds(row0, n_rows)], refs[1 + n], refs[2 + n], _position())

    outs = pl.pallas_call(
        body, name=name, out_shape=tuple(pltpu.HBM(a.shape, a.dtype) for a in (pack, *bufs)),
        in_specs=(HBM,) * (1 + n) + (SEM, SEM, ANY), out_specs=(HBM,) * (1 + n),
        input_output_aliases={i: i for i in range(1 + n)}, compiler_params=_split_params(),
    )(pack, *bufs, send_sems, recv_sems, after)
    return outs[0], outs[1:]


def _scatter_start(srcs, pieces, *, name):
    n = len(srcs)
    land_shape = (N_DEV, sum(p[2] for p in pieces), PACK_W)

    def body(*refs):
        src_refs, land_ref, send_sems, recv_sems, token = refs[:n], refs[n], refs[n + 1], refs[n + 2], refs[-1]
        x, y, c = _position()
        me = 4 * x + 2 * y + c
        for k in range(1, N_DEV):
            px, py, pc = _peer(x, y, c, k)
            off = 0
            for si, lead, rows in pieces:
                pltpu.make_async_remote_copy(
                    src_ref=src_refs[si].at[lead(4 * px + 2 * py + pc)], dst_ref=land_ref.at[me, pl.ds(off, rows)],
                    send_sem=send_sems.at[k - 1], recv_sem=recv_sems.at[k - 1], device_id=(px, py, pc),
                    device_id_type=MESH).start()
                off += rows
        token[...] = jnp.zeros_like(token)

    sems, dt = pltpu.SemaphoreType.DMA((N_DEV - 1,)), srcs[0].dtype
    return pl.pallas_call(
        body, name=name,
        out_shape=(sems, sems) + tuple(pltpu.HBM(a.shape, dt) for a in srcs) + (pltpu.HBM(land_shape, dt), _token_shape()),
        in_specs=(HBM,) * (n + 1), out_specs=(SEM, SEM) + (HBM,) * (n + 1) + (pl.BlockSpec(memory_space=pltpu.VMEM),),
        input_output_aliases={i: 2 + i for i in range(n + 1)}, compiler_params=_split_params(),
    )(*[pltpu.with_memory_space_constraint(a, pltpu.HBM) for a in srcs], _empty_hbm(land_shape, dt))


def _scatter_wait(started, after, *, name):
    send_sems, recv_sems, *bufs = started[:-1]
    n = len(bufs)

    def body(*refs):
        _wait_all(refs[n - 1].at[0], refs[n], refs[n + 1], _position())

    return pl.pallas_call(
        body, name=name, out_shape=tuple(pltpu.HBM(a.shape, a.dtype) for a in bufs),
        in_specs=(HBM,) * n + (SEM, SEM, ANY), out_specs=(HBM,) * n, input_output_aliases={i: i for i in range(n)},
        compiler_params=_split_params(),
    )(*bufs, send_sems, recv_sems, after)


def _peer_sum(own, own_lead, land, block, rows, idx, *, name):
    owns = list(own) if isinstance(own, (list, tuple)) else [own]
    n, lead_rank = len(owns), owns[0].ndim - 2

    def body(idx_ref, *refs):
        own_refs, land_refs, o_ref = refs[:n], refs[n:n + N_DEV - 1], refs[n + N_DEV - 1]
        for j in range(n):
            rs_ = slice(j * rows, (j + 1) * rows)
            acc = own_refs[j][(0,) * lead_rank].astype(F32)
            for k in range(N_DEV - 1):
                acc = acc + land_refs[k][0, rs_].astype(F32)
            o_ref[rs_] = acc

    own_spec = pl.BlockSpec((1,) * lead_rank + (rows, PACK_W), lambda i, t: own_lead(t[0]) + (0, 0))

    def land_spec(k):
        return pl.BlockSpec((1, n * rows, PACK_W), lambda i, t: (t[k + 1], block, 0))

    return pl.pallas_call(
        body, name=name,
        grid_spec=pltpu.PrefetchScalarGridSpec(
            num_scalar_prefetch=1, grid=(1,), in_specs=[own_spec] * n + [land_spec(k) for k in range(N_DEV - 1)],
            out_specs=pl.BlockSpec((n * rows, PACK_W), lambda i, t: (0, 0))),
        out_shape=jax.ShapeDtypeStruct((n * rows, PACK_W), F32), compiler_params=_cparams(("arbitrary",)),
    )(idx, *owns, *([land] * (N_DEV - 1)))


def _adamw(w, g, m, v):
    m = ADAM_B1 * m + (1.0 - ADAM_B1) * g
    v = ADAM_B2 * v + (1.0 - ADAM_B2) * (g * g)
    m_hat = m / (1.0 - ADAM_B1 ** ADAM_STEP)
    v_hat = v / (1.0 - ADAM_B2 ** ADAM_STEP)
    delta = -ADAM_LR * (m_hat / (jnp.sqrt(v_hat) + ADAM_EPS) + ADAM_WD * w)
    return delta, m, v


def _adamw_call(w, g, m, v, *, name, max_rows=256):
    _, r, c_ = w.shape
    tr = max_rows if r > max_rows and r % max_rows == 0 else r

    def body(w_ref, g_ref, m_ref, v_ref, d_ref, mo_ref, vo_ref):
        d, mn, vn = _adamw(w_ref[0], g_ref[...], m_ref[0], v_ref[0])
        d_ref[0] = d
        mo_ref[0] = mn
        vo_ref[0] = vn

    row3 = pl.BlockSpec((1, tr, c_), lambda i: (0, i, 0))
    shp = jax.ShapeDtypeStruct((1, r, c_), F32)
    return pl.pallas_call(
        body, name=name, grid=(r // tr,), in_specs=[row3, pl.BlockSpec((tr, c_), lambda i: (i, 0)), row3, row3],
        out_specs=[row3] * 3, out_shape=[shp] * 3, compiler_params=_cparams(("parallel",)),
    )(w, g, m, v)


SMALL = ("mix_norm_g", "ffn_norm_g", "final_norm_g", "q_norm_g", "kv_norm_g", "swa_sinks")
SMALL_W = dict(mix_norm_g=1024, ffn_norm_g=1024, final_norm_g=1024, q_norm_g=Q_LORA, kv_norm_g=KV_LORA, swa_sinks=SWA_HEADS)


def _small_adamw(parts, w, m, v):
    ns = len(SMALL)

    def body(p_ref, *refs):
        ins, outs = refs[:3 * ns], refs[3 * ns:]
        tot = p_ref[0]
        for dev in range(1, N_DEV):
            tot = tot + p_ref[dev]
        for k, n in enumerate(SMALL):
            g = jnp.sum(tot[k * SUBLANES:(k + 1) * SUBLANES, :SMALL_W[n]], axis=0, keepdims=True)
            res = _adamw(ins[k][...], g, ins[ns + k][...], ins[2 * ns + k][...])
            for j, r in enumerate((g,) + tuple(res)):
                outs[j * ns + k][...] = r
        outs[4 * ns][...] = jnp.sum(tot[ns * SUBLANES:(ns + 1) * SUBLANES, 0:1], axis=0, keepdims=True)

    shapes = [jax.ShapeDtypeStruct((1, SMALL_W[n]), F32) for n in SMALL]
    vm = pl.BlockSpec(memory_space=pltpu.VMEM)
    out = pl.pallas_call(
        body, name="small_adamw", in_specs=[vm] * (1 + 3 * ns), out_specs=[vm] * (4 * ns + 1),
        out_shape=shapes * 4 + [jax.ShapeDtypeStruct((1, 1), F32)],
    )(parts, *[d[n] for d in (w, m, v) for n in SMALL])
    return [dict(zip(SMALL, out[j * ns:(j + 1) * ns])) for j in range(4)] + [out[4 * ns]]


def _small_pack(d, rows_each):
    parts = [jnp.pad(d[n].astype(F32), ((0, 0), (0, PACK_W - SMALL_W[n]))) for n in SMALL]
    out = jnp.concatenate(parts, 0)
    pad = -out.shape[0] % SUBLANES
    return jnp.pad(out, ((0, pad), (0, 0)))


def kernel(x, mix_norm_g, w_in, swa_sinks, q_norm_g, w_uq, kv_norm_g, w_ukv, w_o_swa, w_o_mla, w_out, ffn_norm_g, w_gate, w_up, w_down, final_norm_g, loss_target, m_mix_norm_g, m_w_in, m_swa_sinks, m_q_norm_g, m_w_uq, m_kv_norm_g, m_w_ukv, m_w_o_swa, m_w_o_mla, m_w_out, m_ffn_norm_g, m_w_gate, m_w_up, m_w_down, m_final_norm_g, v_mix_norm_g, v_w_in, v_swa_sinks, v_q_norm_g, v_w_uq, v_kv_norm_g, v_w_ukv, v_w_o_swa, v_w_o_mla, v_w_out, v_ffn_norm_g, v_w_gate, v_w_up, v_w_down, v_final_norm_g):
    big_w = dict(w_in=w_in[0], w_uq=w_uq[0], w_ukv=w_ukv[0], w_o_swa=w_o_swa[0], w_o_mla=w_o_mla[0], w_out=w_out[0],
                 w_gate=w_gate[0], w_up=w_up[0], w_down=w_down[0])
    big_w3 = dict(w_in=w_in, w_uq=w_uq, w_ukv=w_ukv, w_o_swa=w_o_swa, w_o_mla=w_o_mla, w_out=w_out, w_gate=w_gate, w_up=w_up,
                  w_down=w_down)
    big_m = dict(w_in=m_w_in, w_uq=m_w_uq, w_ukv=m_w_ukv, w_o_swa=m_w_o_swa, w_o_mla=m_w_o_mla, w_out=m_w_out,
                 w_gate=m_w_gate, w_up=m_w_up, w_down=m_w_down)
    big_v = dict(w_in=v_w_in, w_uq=v_w_uq, w_ukv=v_w_ukv, w_o_swa=v_w_o_swa, w_o_mla=v_w_o_mla, w_out=v_w_out,
                 w_gate=v_w_gate, w_up=v_w_up, w_down=v_w_down)
    small_w = dict(mix_norm_g=mix_norm_g, ffn_norm_g=ffn_norm_g, final_norm_g=final_norm_g.reshape(1, D_MODEL),
                   q_norm_g=q_norm_g, kv_norm_g=kv_norm_g, swa_sinks=swa_sinks)
    small_m = dict(mix_norm_g=m_mix_norm_g, ffn_norm_g=m_ffn_norm_g, final_norm_g=m_final_norm_g.reshape(1, D_MODEL),
                   q_norm_g=m_q_norm_g, kv_norm_g=m_kv_norm_g, swa_sinks=m_swa_sinks)
    small_v = dict(mix_norm_g=v_mix_norm_g, ffn_norm_g=v_ffn_norm_g, final_norm_g=v_final_norm_g.reshape(1, D_MODEL),
                   q_norm_g=v_q_norm_g, kv_norm_g=v_kv_norm_g, swa_sinks=v_swa_sinks)

    px, py, pc = _position()
    me = 4 * px + 2 * py + pc
    idx = jnp.stack([me] + [4 * qx + 2 * qy + qc for qx, qy, qc in (_peer(px, py, pc, k) for k in range(1, N_DEV))])
    idx = idx.astype(jnp.int32)

    dev = lambda d: (d,)
    pack = _wire_pack(big_w, WIRE_DTYPE)
    win_g, = _all_gather(pack, ((0, dev, 0, W_IN_ROWS),), ((N_DEV, W_IN_ROWS, PACK_W),), name="ag_early")
    mid_pieces = tuple((b, dev, OUT_ROWS) for b in range(MID_BLOCKS))
    ag_mid = _gather_start(pack, W_IN_ROWS, mid_pieces, ((N_DEV, OUT_ROWS, PACK_W),) * MID_BLOCKS, name="ag_mid_start")
    ag = {}

    def own_rows(r0, r1, shape):
        return pack[r0:r1].reshape(shape)

    def mid_weights(after):
        pack_mid, blocks = _gather_wait(ag_mid, W_IN_ROWS, MID_ROWS, after, name="ag_mid_wait")
        ag["late"] = _gather_start(pack_mid, EARLY_ROWS, ((0, _gate_slab, FF_COLS), (0, _up_slab, FF_COLS), (1, dev, FF_COLS)),
                                   (GU_SHAPE, D_SHAPE), name="ag_late_start")
        row0 = lambda b: W_IN_ROWS + b * OUT_ROWS
        ops = _mid_operands(*[lax.dynamic_update_slice(blk, own_rows(row0(b), row0(b + 1), (1, OUT_ROWS, PACK_W)), (me, 0, 0))
                              for b, blk in enumerate(blocks)])
        ops["wuq"] = ops["wuq"] + ag["late"][-1][0:1, 0:1].astype(ops["wuq"].dtype)
        return ops

    def late_weights(after):
        _, (gu, d) = _gather_wait(ag["late"], EARLY_ROWS, LATE_ROWS, after, name="ag_late_wait")
        slab = (1, 1, 1, FF_COLS, PACK_W)
        gu = lax.dynamic_update_slice(gu, own_rows(EARLY_ROWS, EARLY_ROWS + FF_COLS, slab), _gate_slab(me) + (0, 0))
        gu = lax.dynamic_update_slice(gu, own_rows(EARLY_ROWS + FF_COLS, EARLY_ROWS + 2 * FF_COLS, slab), _up_slab(me) + (0, 0))
        d = lax.dynamic_update_slice(d, own_rows(EARLY_ROWS + 2 * FF_COLS, PACK_ROWS, (1, FF_COLS, PACK_W)), (me, 0, 0))
        return gu.reshape(2 * D_FF, D_MODEL), d.reshape(D_FF, D_MODEL)

    rs = {}

    def late_grads(g_gu, g_d):
        rs["late"] = _scatter_start([g_gu.reshape(GU_SHAPE), g_d.reshape(D_SHAPE)],
                                    ((0, _gate_slab, FF_COLS), (0, _up_slab, FF_COLS), (1, dev, FF_COLS)),
                                    name="rs_late_start")
        return rs["late"][-1]

    def mid_grads(g):
        rs["mid"] = _scatter_start(_mid_grad_pack(g), mid_pieces, name="rs_mid_start")
        return rs["mid"][-1]

    def last_grads(g_win_t):
        rs["last"] = _scatter_start([_w_in_grad_chunks(g_win_t)], ((0, dev, W_IN_ROWS),), name="rs_last_start")
        return rs["last"][-1]

    first_w = dict(small_w, mix_norm_g=mix_norm_g + ag_mid[-1][0:1, 0:1])
    loss_tot, gx, g_small = _local_step(
        x[0], loss_target[0], _w_in_operand(win_g), first_w, types.SimpleNamespace(mid=mid_weights, late=late_weights),
        types.SimpleNamespace(late=late_grads, mid=mid_grads, last=last_grads))

    g_gu, g_d, land_late = _scatter_wait(rs["late"], gx, name="rs_late_wait")
    *g_mid, land_mid = _scatter_wait(rs["mid"], gx, name="rs_mid_wait")
    g_win, land_last = _scatter_wait(rs["last"], gx, name="rs_last_wait")
    gw_t = dict(w_gate=_peer_sum(g_gu, _gate_slab, land_late, 0, FF_COLS, idx, name="rs_sum_gate"),
                w_up=_peer_sum(g_gu, _up_slab, land_late, 1, FF_COLS, idx, name="rs_sum_up"),
                w_in=_peer_sum(g_win, dev, land_last, 0, W_IN_ROWS, idx, name="rs_sum_in")[0:W_IN_COLS])
    gw = dict(w_down=_peer_sum(g_d, dev, land_late, 2, FF_COLS, idx, name="rs_sum_down"))
    gw.update(_mid_unpack(_peer_sum(g_mid, dev, land_mid, 0, OUT_ROWS, idx, name="rs_sum_mid")))
    dw, mw, vw = {}, {}, {}
    swap = lambda a: jnp.swapaxes(a, 1, 2)
    for n in BIG:
        if n in gw_t:
            res = _adamw_call(swap(big_w3[n]), gw_t[n], swap(big_m[n]), swap(big_v[n]), name="adamw_" + n)
            dw[n], mw[n], vw[n] = (swap(r) for r in res)
        else:
            dw[n], mw[n], vw[n] = _adamw_call(big_w3[n], gw[n], big_m[n], big_v[n], name="adamw_" + n)
    gw = {n: g[None] for n, g in gw.items()}
    gw.update({n: swap(g[None]) for n, g in gw_t.items()})

    loss_rows = jnp.pad(loss_tot[0:1, 0:1], ((0, SUBLANES - 1), (0, PACK_W - 1)))
    small_rows = jnp.concatenate([_small_pack(g_small_rows(g_small), SUBLANES), loss_rows], 0)
    parts, = _all_gather(small_rows, ((0, lambda d: (d,), 0, small_rows.shape[0]),), ((N_DEV,) + small_rows.shape,),
                         name="ag_small")
    gs, ds, ms, vs, loss = _small_adamw(parts, small_w, small_m, small_v)
    loss = loss[0, 0]
    for d in (gs, ds, ms, vs):
        d["final_norm_g"] = d["final_norm_g"].reshape(D_MODEL)

    order = ("mix_norm_g", "w_in", "swa_sinks", "q_norm_g", "w_uq", "kv_norm_g", "w_ukv", "w_o_swa", "w_o_mla", "w_out",
             "ffn_norm_g", "w_gate", "w_up", "w_down", "final_norm_g")

    def leaves(big, small):
        return [big[n] if n in big else small[n] for n in order]

    return (loss, gx[None], *leaves(gw, gs), *leaves(dw, ds), *leaves(mw, ms), *leaves(vw, vs))


def g_small_rows(g_small):
    out = dict(g_small)
    out["swa_sinks"] = jnp.pad(g_small["swa_sinks"], ((0, SUBLANES - 1), (0, 0)))
    return out
```

```python
import types

import numpy as np
import jax
import jax.numpy as jnp
from jax import lax
from jax.experimental import pallas as pl
from jax.experimental.pallas import tpu as pltpu

F32 = jnp.float32
MXU_DTYPE = jnp.bfloat16
WIRE_DTYPE = jnp.bfloat16

D_MODEL = 1024
EPS = 1e-6
ROPE_THETA = 10000.0
BLOCK = 128
HEAD_DIM = 64
SWA_HEADS = 8
SWA_KV_HEADS = 2
SWA_GROUP = SWA_HEADS // SWA_KV_HEADS
MLA_HEADS = 8
MLA_NOPE = 64
MLA_ROPE = 32
MLA_V = 64
MLA_QK = MLA_NOPE + MLA_ROPE
Q_LORA = 384
KV_LORA = 256
D_FF = 2816
IN_SIZES = (512, 128, 128, Q_LORA, KV_LORA, MLA_ROPE, D_MODEL, D_MODEL)
IN_OFF = tuple(int(v) for v in np.cumsum((0,) + IN_SIZES))
ADAM_LR, ADAM_B1, ADAM_B2, ADAM_EPS, ADAM_WD, ADAM_STEP = 0.001, 0.9, 0.999, 1e-08, 0.01, 10

LANES = 128
SUBLANES = 8
VMEM_LIMIT = 48 * 1024 * 1024
N_DEV = 8

P_GA, P_GB, P_Q, P_QLAT, P_KR, P_K, P_V, P_KVLAT, P_W = 0, 1024, 2048, 3072, 3456, 3584, 3840, 4096, 4352
KR_LANE = 64

LOG2E = 1.4426950408889634

NT = (((1,), (1,)), ((), ()))
NN = (((1,), (0,)), ((), ()))
TN = (((0,), (0,)), ((), ()))


def _cparams(sem):
    return pltpu.CompilerParams(dimension_semantics=sem, vmem_limit_bytes=VMEM_LIMIT)


def _mm(a, b, mode, *, name, out_dtype=F32, add=None, tm=512, tn=512, tk=None):
    if mode == "nn":
        (M, K), (K2, N) = a.shape, b.shape
    elif mode == "nt":
        (M, K), (N, K2) = a.shape, b.shape
    else:
        (K, M), (K2, N) = a.shape, b.shape
    assert K == K2, (a.shape, b.shape, mode)
    tm, tn, tk = min(tm, M), min(tn, N), K if tk is None else min(tk, K)
    assert M % tm == 0 and N % tn == 0 and K % tk == 0, (M, N, K, tm, tn, tk)
    nk = K // tk
    dn = {"nn": NN, "nt": NT, "tn": TN}[mode]
    if mode == "tn":
        a_spec = pl.BlockSpec((tk, tm), lambda i, j, k: (k, i))
    else:
        a_spec = pl.BlockSpec((tm, tk), lambda i, j, k: (i, k))
    once = dict(pipeline_mode=pl.Buffered(1)) if (nk == 1 and tn == N) else {}
    if mode == "nt":
        b_spec = pl.BlockSpec((tn, tk), lambda i, j, k: (j, k), **once)
    else:
        b_spec = pl.BlockSpec((tk, tn), lambda i, j, k: (k, j), **once)
    o_spec = pl.BlockSpec((tm, tn), lambda i, j, k: (i, j))
    has_add = add is not None

    def body(*refs):
        a_ref, b_ref = refs[0], refs[1]
        add_ref = refs[2] if has_add else None
        o_ref = refs[2 + has_add]
        p = lax.dot_general(a_ref[...], b_ref[...], dn, preferred_element_type=F32)

        def finish(acc):
            if has_add:
                acc = acc + add_ref[...]
            o_ref[...] = acc.astype(o_ref.dtype)

        if nk == 1:
            finish(p)
        else:
            acc_ref = refs[-1]
            k = pl.program_id(2)

            @pl.when(k == 0)
            def _():
                acc_ref[...] = p

            @pl.when((k > 0) & (k < nk - 1))
            def _():
                acc_ref[...] += p

            @pl.when(k == nk - 1)
            def _():
                finish(acc_ref[...] + p)

    ins = [a, b] + ([add] if has_add else [])
    return pl.pallas_call(
        body, name=name, grid=(M // tm, N // tn, nk), in_specs=[a_spec, b_spec] + ([o_spec] if has_add else []), out_specs=o_spec,
        out_shape=jax.ShapeDtypeStruct((M, N), out_dtype),
        scratch_shapes=[pltpu.VMEM((tm, tn), F32)] if nk > 1 else [],
        compiler_params=_cparams(("parallel", "parallel", "arbitrary")),
    )(*ins)


def _rows(ts, w, cb=0):
    return pl.BlockSpec((ts, w), lambda i: (i, cb))


def _const(r, w):
    return pl.BlockSpec((r, w), lambda i: (0, 0))


def _sublane_sum(v):
    ts, c = v.shape
    return jnp.sum(v.reshape(ts // SUBLANES, SUBLANES, c), axis=0)


def _sigmoid(v):
    return 1.0 / (1.0 + jnp.exp(-v))


def _rope(v, cos, s_up, s_dn, up, dn):
    return v * cos + pltpu.roll(v, up, 1) * s_up + pltpu.roll(v, dn, 1) * s_dn


def _rope_t(dv, cos, s_up, s_dn, up, dn):
    return dv * cos + pltpu.roll(dv * s_up, dn, 1) + pltpu.roll(dv * s_dn, up, 1)


def _rope_tables(seq):
    pos = np.arange(seq, dtype=np.float32)[:, None]

    def base(dim):
        inv = np.float32(ROPE_THETA) ** (-np.arange(0, dim, 2, dtype=np.float32) / np.float32(dim))
        ang = (pos * inv.astype(np.float32)[None, :]).astype(np.float32)
        return np.cos(ang).astype(np.float32), np.sin(ang).astype(np.float32)

    z = lambda n: np.zeros((seq, n), np.float32)
    ca, sa = base(HEAD_DIM)
    a_cos = np.concatenate([ca, ca, z(64)], 1)
    a_up = np.concatenate([-sa, z(96)], 1)
    a_dn = np.concatenate([z(32), sa, z(64)], 1)
    cb, sb = base(MLA_ROPE)
    one = np.ones((seq, 64), np.float32)
    q_cos = np.concatenate([one, cb, cb, z(32)], 1)
    k_cos = np.concatenate([z(64), cb, cb, z(32)], 1)
    b_up = np.concatenate([z(64), -sb, z(48)], 1)
    b_dn = np.concatenate([z(80), sb, z(32)], 1)
    return tuple(jnp.asarray(t) for t in (a_cos, a_up, a_dn, q_cos, k_cos, b_up, b_dn))


def _rms(v, g):
    return v * lax.rsqrt(jnp.mean(v * v, axis=-1, keepdims=True) + EPS) * g


def _rms_bwd(v, g, d):
    r = lax.rsqrt(jnp.mean(v * v, axis=-1, keepdims=True) + EPS)
    xh = v * r
    dxh = d * g
    return r * (dxh - xh * jnp.mean(dxh * xh, axis=-1, keepdims=True)), d * xh


F_GA, F_GB, F_KVLAT, F_QLAT, F_W = 0, 1024, 2048, 2304, 2688


def _proj_in(x, g, w_t, gq, gkv, tabs, *, tm=512):
    s_, c = x.shape
    a_cos, a_up, a_dn, _, k_cos, b_up, b_dn = tabs

    def body(x_ref, g_ref, w_ref, gq_ref, gkv_ref, ac, au, ad, kc, bu, bd,
             h_ref, qa_ref, ka_ref, va_ref, cq_ref, ckv_ref, kro_ref, pf_ref):
        h = _rms(x_ref[...], g_ref[...]).astype(h_ref.dtype)
        h_ref[...] = h
        mm = lambda a, b: lax.dot_general(h, w_ref[a:b, :], NT, preferred_element_type=F32)
        pf_ref[:, F_GA:F_KVLAT] = mm(P_GA, P_Q)
        c_, u_, d_ = ac[...], au[...], ad[...]
        q = mm(P_Q, P_QLAT)
        for hd in range(SWA_HEADS):
            sl = slice(hd * LANES, (hd + 1) * LANES)
            qa_ref[:, sl] = _rope(q[:, sl], c_, u_, d_, 96, 32).astype(qa_ref.dtype)
        kv = mm(P_KR, P_KVLAT)
        kro_ref[...] = _rope(kv[:, :LANES], kc[...], bu[...], bd[...], 112, 16)
        for hd in range(SWA_KV_HEADS):
            sl = slice((1 + hd) * LANES, (2 + hd) * LANES)
            ka_ref[:, hd * LANES:(hd + 1) * LANES] = _rope(kv[:, sl], c_, u_, d_, 96, 32).astype(ka_ref.dtype)
        va_ref[...] = kv[:, P_V - P_KR:].astype(va_ref.dtype)
        for a, b, f0, gref, dst in ((P_QLAT, P_KR, F_QLAT, gq_ref, cq_ref), (P_KVLAT, P_W, F_KVLAT, gkv_ref, ckv_ref)):
            v = mm(a, b)
            pf_ref[:, f0:f0 + b - a] = v
            r = lax.rsqrt(jnp.mean(v * v, axis=-1, keepdims=True) + EPS)
            dst[...] = (v * r * gref[...]).astype(dst.dtype)

    tab = _rows(tm, LANES)
    widths = (c, SWA_HEADS * LANES, SWA_KV_HEADS * LANES, SWA_KV_HEADS * LANES, Q_LORA, KV_LORA)
    return pl.pallas_call(
        body, name="proj_in", grid=(s_ // tm,),
        in_specs=[_rows(tm, c), _const(1, c), pl.BlockSpec((P_W, c), lambda i: (0, 0), pipeline_mode=pl.Buffered(1)),
                  _const(1, Q_LORA), _const(1, KV_LORA), tab, tab, tab, tab, tab, tab],
        out_specs=[_rows(tm, w) for w in widths] + [tab, _rows(tm, F_W)],
        out_shape=[jax.ShapeDtypeStruct((s_, w), MXU_DTYPE) for w in widths]
        + [jax.ShapeDtypeStruct((s_, LANES), F32), jax.ShapeDtypeStruct((s_, F_W), F32)],
        compiler_params=_cparams(("parallel",)),
    )(x, g, w_t, gq, gkv, a_cos, a_up, a_dn, k_cos, b_up, b_dn)


def _mm_norm_bwd(a, b, x, g, res, *, name, after=None, tm=512):
    s_, kk = a.shape
    c = b.shape[1]
    has_after = after is not None

    def body(*refs):
        a_ref, b_ref, x_ref, g_ref, res_ref = refs[:5]
        dx_ref, dxb_ref, dg_ref = refs[5 + has_after:]
        d = jnp.dot(a_ref[...], b_ref[...], preferred_element_type=F32)
        dx, gg = _rms_bwd(x_ref[...], g_ref[...], d)
        dx = dx + res_ref[...]
        dx_ref[...] = dx
        dxb_ref[...] = dx.astype(dxb_ref.dtype)

        @pl.when(pl.program_id(0) == 0)
        def _():
            dg_ref[...] = jnp.zeros(dg_ref.shape, F32)

        dg_ref[...] += _sublane_sum(gg)

    row = _rows(tm, c)
    in_specs = [_rows(tm, kk), pl.BlockSpec((kk, c), lambda i: (0, 0), pipeline_mode=pl.Buffered(1)), row, _const(1, c), row]
    return pl.pallas_call(
        body, name=name, grid=(s_ // tm,), in_specs=in_specs + ([pl.BlockSpec(memory_space=pl.ANY)] if has_after else []),
        out_specs=[row, row, _const(SUBLANES, c)],
        out_shape=[jax.ShapeDtypeStruct((s_, c), F32), jax.ShapeDtypeStruct((s_, c), MXU_DTYPE),
                   jax.ShapeDtypeStruct((SUBLANES, c), F32)],
        compiler_params=_cparams(("arbitrary",)),
    )(*([a, b, x, g, res] + ([after] if has_after else [])))


def _mla_up(cq, ckv, kro, wuq, wuk, wuv, tabs, *, ts=512):
    s_ = cq.shape[0]
    _, _, _, q_cos, _, b_up, b_dn = tabs

    def body(cq_ref, ckv_ref, kr_ref, wq_ref, wk_ref, wv_ref, qc, bu, bd, qo_ref, ko_ref, vo_ref):
        c_, u_, d_ = qc[...], bu[...], bd[...]
        kr = kr_ref[...]
        ckv_ = ckv_ref[...]
        vo_ref[...] = jnp.dot(ckv_, wv_ref[...], preferred_element_type=F32).astype(vo_ref.dtype)
        q = jnp.dot(cq_ref[...], wq_ref[...], preferred_element_type=F32)
        k = jnp.dot(ckv_, wk_ref[...], preferred_element_type=F32)
        for h in range(MLA_HEADS):
            sl = slice(h * LANES, (h + 1) * LANES)
            qo_ref[:, sl] = _rope(q[:, sl], c_, u_, d_, 112, 16).astype(qo_ref.dtype)
            ko_ref[:, sl] = (k[:, sl] + kr).astype(ko_ref.dtype)

    tab, out = _rows(ts, LANES), _rows(ts, 1024)
    return pl.pallas_call(
        body, name="mla_up", grid=(s_ // ts,),
        in_specs=[_rows(ts, Q_LORA), _rows(ts, KV_LORA), tab, _const(Q_LORA, 1024), _const(KV_LORA, 1024),
                  _const(KV_LORA, 1024), tab, tab, tab],
        out_specs=[out, out, out], out_shape=[jax.ShapeDtypeStruct((s_, 1024), MXU_DTYPE)] * 3,
        compiler_params=_cparams(("parallel",)),
    )(cq, ckv, kro, wuq, wuk, wuv, q_cos, b_up, b_dn)


def _mla_up_bwd(dqc, dkc, dvp, wuq, wukv, p, gq, gkv, tabs, *, ts=256):
    s_ = dqc.shape[0]
    _, _, _, q_cos, k_cos, b_up, b_dn = tabs

    def body(dq_ref, dk_ref, dv_ref, wq_ref, wkv_ref, ql_ref, kvl_ref, gq_ref, gkv_ref, qc, kc, bu, bd,
             dqo_ref, dkvo_ref, dkr_ref, dql_ref, dkvl_ref, dgq_ref, dgkv_ref):
        c_, u_, d_ = qc[...], bu[...], bd[...]
        tot = jnp.zeros((ts, LANES), F32)
        for h in range(MLA_HEADS):
            sl = slice(h * LANES, (h + 1) * LANES)
            dqo_ref[:, sl] = _rope_t(dq_ref[:, sl], c_, u_, d_, 112, 16).astype(dqo_ref.dtype)
            dk = dk_ref[:, sl]
            dkvo_ref[:, sl] = dk.astype(dkvo_ref.dtype)
            tot = tot + dk
        dkvo_ref[:, 1024:2048] = dv_ref[...].astype(dkvo_ref.dtype)
        dkr_ref[...] = _rope_t(tot, kc[...], u_, d_, 112, 16).astype(dkr_ref.dtype)

        @pl.when(pl.program_id(0) == 0)
        def _():
            dgq_ref[...] = jnp.zeros(dgq_ref.shape, F32)
            dgkv_ref[...] = jnp.zeros(dgkv_ref.shape, F32)

        for do_ref, w_ref, x_ref, g_ref, dx_ref, dg_ref in ((dqo_ref, wq_ref, ql_ref, gq_ref, dql_ref, dgq_ref),
                                                            (dkvo_ref, wkv_ref, kvl_ref, gkv_ref, dkvl_ref, dgkv_ref)):
            d = lax.dot_general(do_ref[...], w_ref[...], NT, preferred_element_type=F32)
            dx, gg = _rms_bwd(x_ref[...], g_ref[...], d)
            dx_ref[...] = dx.astype(dx_ref.dtype)
            dg_ref[...] += _sublane_sum(gg)

    tab = _rows(ts, LANES)
    return pl.pallas_call(
        body, name="mla_up_bwd", grid=(s_ // ts,),
        in_specs=[_rows(ts, 1024), _rows(ts, 1024), _rows(ts, 1024), _const(Q_LORA, 1024), _const(KV_LORA, 2048),
                  _rows(ts, Q_LORA, F_QLAT // Q_LORA), _rows(ts, KV_LORA, F_KVLAT // KV_LORA),
                  _const(1, Q_LORA), _const(1, KV_LORA), tab, tab, tab, tab],
        out_specs=[_rows(ts, 1024), _rows(ts, 2048), _rows(ts, LANES), _rows(ts, Q_LORA), _rows(ts, KV_LORA),
                   _const(SUBLANES, Q_LORA), _const(SUBLANES, KV_LORA)],
        out_shape=[jax.ShapeDtypeStruct((s_, 1024), MXU_DTYPE), jax.ShapeDtypeStruct((s_, 2048), MXU_DTYPE),
                   jax.ShapeDtypeStruct((s_, LANES), MXU_DTYPE), jax.ShapeDtypeStruct((s_, Q_LORA), MXU_DTYPE),
                   jax.ShapeDtypeStruct((s_, KV_LORA), MXU_DTYPE), jax.ShapeDtypeStruct((SUBLANES, Q_LORA), F32),
                   jax.ShapeDtypeStruct((SUBLANES, KV_LORA), F32)],
        compiler_params=_cparams(("arbitrary",)),
    )(dqc, dkc, dvp, wuq, wukv, p, p, gq, gkv, q_cos, k_cos, b_up, b_dn)


def _assemble_dp(dgab, dqa, dqlat, dkr, dka, dva, dkvlat, tabs, *, ts=256):
    s_ = dqa.shape[0]
    a_cos, a_up, a_dn = tabs[0], tabs[1], tabs[2]

    def body(dg_ref, dq_ref, dql_ref, dkr_ref, dk_ref, dv_ref, dkvl_ref, ac, au, ad, o_ref):
        c_, u_, d_ = ac[...], au[...], ad[...]
        o_ref[:, P_GA:P_Q] = dg_ref[...]
        for h in range(SWA_HEADS):
            sl = slice(h * LANES, (h + 1) * LANES)
            o_ref[:, P_Q + h * LANES:P_Q + (h + 1) * LANES] = _rope_t(dq_ref[:, sl], c_, u_, d_, 96, 32).astype(o_ref.dtype)
        o_ref[:, P_QLAT:P_KR] = dql_ref[...]
        o_ref[:, P_KR:P_K] = dkr_ref[...]
        for h in range(SWA_KV_HEADS):
            sl = slice(h * LANES, (h + 1) * LANES)
            o_ref[:, P_K + h * LANES:P_K + (h + 1) * LANES] = _rope_t(dk_ref[:, sl], c_, u_, d_, 96, 32).astype(o_ref.dtype)
        o_ref[:, P_V:P_KVLAT] = dv_ref[...]
        o_ref[:, P_KVLAT:P_W] = dkvl_ref[...]

    tab = _rows(ts, LANES)
    return pl.pallas_call(
        body, name="assemble_dp", grid=(s_ // ts,),
        in_specs=[_rows(ts, 2048), _rows(ts, 1024), _rows(ts, Q_LORA), _rows(ts, LANES), _rows(ts, 256), _rows(ts, 256),
                  _rows(ts, KV_LORA), tab, tab, tab],
        out_specs=_rows(ts, P_W), out_shape=jax.ShapeDtypeStruct((s_, P_W), MXU_DTYPE),
        compiler_params=_cparams(("parallel",)),
    )(dgab, dqa, dqlat, dkr, dka, dva, dkvlat, a_cos, a_up, a_dn)


def _attn_out_gate(oa, ob, woa_t, wob_t, p, *, ts=512):
    s_ = p.shape[0]

    def body(oa_ref, ob_ref, wa_ref, wb_ref, ga_ref, gb_ref, ta_ref, tb_ref, y_ref):
        ta = lax.dot_general(oa_ref[...], wa_ref[...], NT, preferred_element_type=F32)
        tb = lax.dot_general(ob_ref[...], wb_ref[...], NT, preferred_element_type=F32)
        ta_ref[...] = ta
        tb_ref[...] = tb
        y_ref[...] = (_sigmoid(ga_ref[...]) * ta + _sigmoid(gb_ref[...]) * tb).astype(y_ref.dtype)

    w = _const(1024, 1024)
    return pl.pallas_call(
        body, name="attn_out_gate", grid=(s_ // ts,),
        in_specs=[_rows(ts, 1024), _rows(ts, 1024), w, w, _rows(ts, 1024, F_GA // 1024), _rows(ts, 1024, F_GB // 1024)],
        out_specs=[_rows(ts, 1024)] * 3,
        out_shape=[jax.ShapeDtypeStruct((s_, 1024), F32)] * 2 + [jax.ShapeDtypeStruct((s_, 1024), MXU_DTYPE)],
        compiler_params=_cparams(("parallel",)),
    )(oa, ob, woa_t, wob_t, p, p)


def _d_y_gate(dx1b, wout, p, ta, tb, *, ts=512):
    s_ = p.shape[0]

    def body(dx_ref, w_ref, ga_ref, gb_ref, ta_ref, tb_ref, dta_ref, dtb_ref, dg_ref):
        d = lax.dot_general(dx_ref[...], w_ref[...], NT, preferred_element_type=F32)
        sa, sb = _sigmoid(ga_ref[...]), _sigmoid(gb_ref[...])
        dta_ref[...] = (d * sa).astype(dta_ref.dtype)
        dtb_ref[...] = (d * sb).astype(dtb_ref.dtype)
        dg_ref[:, 0:1024] = (d * ta_ref[...] * (sa * (1.0 - sa))).astype(dg_ref.dtype)
        dg_ref[:, 1024:2048] = (d * tb_ref[...] * (sb * (1.0 - sb))).astype(dg_ref.dtype)

    return pl.pallas_call(
        body, name="d_y_gate", grid=(s_ // ts,),
        in_specs=[_rows(ts, 1024), _const(1024, 1024), _rows(ts, 1024, F_GA // 1024), _rows(ts, 1024, F_GB // 1024),
                  _rows(ts, 1024), _rows(ts, 1024)],
        out_specs=[_rows(ts, 1024), _rows(ts, 1024), _rows(ts, 2048)],
        out_shape=[jax.ShapeDtypeStruct((s_, 1024), MXU_DTYPE)] * 2 + [jax.ShapeDtypeStruct((s_, 2048), MXU_DTYPE)],
        compiler_params=_cparams(("parallel",)),
    )(dx1b, wout, p, p, ta, tb)


FF_TILE = D_FF // 2


def _ffn_in_act(x1, g, wgu_t, *, tm=512):
    s_ = x1.shape[0]
    n = s_ // tm

    def body(x_ref, g_ref, w_ref, h_ref, gu_ref, a_ref):
        h = _rms(x_ref[...], g_ref[...]).astype(h_ref.dtype)
        h_ref[...] = h
        p = lax.dot_general(h, w_ref[...], NT, preferred_element_type=F32)
        gu_ref[...] = p
        gate = p[:, :FF_TILE]
        a_ref[...] = (gate * _sigmoid(gate) * p[:, FF_TILE:]).astype(a_ref.dtype)

    return pl.pallas_call(
        body, name="ffn_in", grid=(2, s_ // tm),
        in_specs=[pl.BlockSpec((tm, D_MODEL), lambda j, i: (i, 0)), pl.BlockSpec((1, D_MODEL), lambda j, i: (0, 0)),
                  pl.BlockSpec((2 * FF_TILE, D_MODEL), lambda j, i: (j, 0))],
        out_specs=[pl.BlockSpec((tm, D_MODEL), lambda j, i: (i + j * (n - 1 - i), 0)),
                   pl.BlockSpec((tm, 2 * FF_TILE), lambda j, i: (i, j)),
                   pl.BlockSpec((tm, FF_TILE), lambda j, i: (i, j))],
        out_shape=[jax.ShapeDtypeStruct((s_, D_MODEL), MXU_DTYPE), jax.ShapeDtypeStruct((s_, 2 * D_FF), F32),
                   jax.ShapeDtypeStruct((s_, D_FF), MXU_DTYPE)],
        compiler_params=_cparams(("arbitrary", "arbitrary")),
    )(x1, g, wgu_t)


def _d_act_swiglu(dx2b, wd, gu, *, tm=512):
    s_ = dx2b.shape[0]

    def body(d_ref, w_ref, gu_ref, o_ref):
        da = lax.dot_general(d_ref[...], w_ref[...], NT, preferred_element_type=F32)
        g, u = gu_ref[:, :FF_TILE], gu_ref[:, FF_TILE:]
        sg = _sigmoid(g)
        o_ref[:, :FF_TILE] = (da * u * (sg * (1.0 + g * (1.0 - sg)))).astype(o_ref.dtype)
        o_ref[:, FF_TILE:] = (da * (g * sg)).astype(o_ref.dtype)

    gu_spec = pl.BlockSpec((tm, 2 * FF_TILE), lambda j, i: (i, j))
    return pl.pallas_call(
        body, name="d_act", grid=(2, s_ // tm),
        in_specs=[pl.BlockSpec((tm, D_MODEL), lambda j, i: (i, 0)), pl.BlockSpec((FF_TILE, D_MODEL), lambda j, i: (j, 0)), gu_spec],
        out_specs=gu_spec, out_shape=jax.ShapeDtypeStruct((s_, 2 * D_FF), MXU_DTYPE),
        compiler_params=_cparams(("parallel", "parallel")),
    )(dx2b, wd, gu)


def _ffn_out_loss(act, wd, x1, g, tgt, *, ts=512):
    s_, c = x1.shape
    kk = act.shape[1]

    def body(a_ref, w_ref, x_ref, g_ref, t_ref, dx_ref, dxb_ref, dg_ref, lp_ref, tot_ref):
        v = x_ref[...] + jnp.dot(a_ref[...], w_ref[...], preferred_element_type=F32)
        r = lax.rsqrt(jnp.mean(v * v, axis=-1, keepdims=True) + EPS)
        xh = v * r
        gg = g_ref[...]
        e = xh * gg - t_ref[...]
        do = e * (1.0 / c)
        dxh = do * gg
        dx = r * (dxh - xh * jnp.mean(dxh * xh, axis=-1, keepdims=True))
        dx_ref[...] = dx
        dxb_ref[...] = dx.astype(dxb_ref.dtype)
        i = pl.program_id(0)

        @pl.when(i == 0)
        def _():
            dg_ref[...] = jnp.zeros(dg_ref.shape, F32)
            lp_ref[...] = jnp.zeros(lp_ref.shape, F32)

        dg_ref[...] += _sublane_sum(do * xh)
        lp_ref[...] += _sublane_sum(e * e)
        tot_ref[...] = jnp.full(tot_ref.shape, (0.5 / c) * jnp.sum(lp_ref[...]), F32)

    return pl.pallas_call(
        body, name="ffn_out_loss", grid=(s_ // ts,),
        in_specs=[_rows(ts, kk), _const(kk, c), _rows(ts, c), _const(1, c), _rows(ts, c)],
        out_specs=[_rows(ts, c), _rows(ts, c), _const(SUBLANES, c), _const(SUBLANES, c), _const(SUBLANES, LANES)],
        out_shape=[jax.ShapeDtypeStruct((s_, c), F32), jax.ShapeDtypeStruct((s_, c), MXU_DTYPE),
                   jax.ShapeDtypeStruct((SUBLANES, c), F32), jax.ShapeDtypeStruct((SUBLANES, c), F32),
                   jax.ShapeDtypeStruct((SUBLANES, LANES), F32)],
        compiler_params=_cparams(("arbitrary",)),
    )(act, wd, x1, g, tgt)


def _mla_d_out(dtb, wob_t, o32, *, ts=512):
    s_ = dtb.shape[0]

    def body(dt_ref, w_ref, o_ref, dob_ref, dl_ref):
        d = jnp.dot(dt_ref[...], w_ref[...], preferred_element_type=F32)
        dob_ref[...] = d.astype(dob_ref.dtype)
        prod = d * o_ref[...]
        for h in range(MLA_HEADS):
            dl_ref[h] = jnp.sum(prod[:, h * LANES:(h + 1) * LANES].T, axis=0, keepdims=True)

    return pl.pallas_call(
        body, name="mla_d_out", grid=(s_ // ts,), in_specs=[_rows(ts, 1024), _const(1024, 1024), _rows(ts, 1024)],
        out_specs=[_rows(ts, 1024), pl.BlockSpec((MLA_HEADS, 1, ts), lambda i: (0, 0, i))],
        out_shape=[jax.ShapeDtypeStruct((s_, 1024), MXU_DTYPE), jax.ShapeDtypeStruct((MLA_HEADS, 1, s_), F32)],
        compiler_params=_cparams(("parallel",)),
    )(dtb, wob_t, o32)


SWA_T = 4 * BLOCK


SWA_W = SWA_GROUP * BLOCK


def _swa_masks(sb):
    kr = lax.broadcasted_iota(jnp.int32, (2 * BLOCK, SWA_W), 0)
    qc = jnp.bitwise_and(lax.broadcasted_iota(jnp.int32, (2 * BLOCK, SWA_W), 1), BLOCK - 1)
    band = jnp.logical_and(kr > qc, kr <= qc + BLOCK)
    first = jnp.logical_and(band, kr >= BLOCK)
    return band, jnp.logical_or(first, jnp.logical_and(band, sb > 0))


def _heads_to_rows(ref, rs):
    return jnp.concatenate([ref[rs, h * LANES:(h + 1) * LANES] for h in range(SWA_GROUP)], axis=0)


def _sink_row(sk_ref):
    return jnp.concatenate([sk_ref[0, h:h + 1, :] for h in range(SWA_GROUP)], axis=1) * LOG2E


def _swa_in_specs(rev, nsb):
    sbi = (lambda j: nsb - 1 - j) if rev else (lambda j: j)
    cur = pl.BlockSpec((SWA_T, LANES), lambda g, j: (sbi(j), g))
    prev = pl.BlockSpec((BLOCK, LANES), lambda g, j: (jnp.maximum(4 * sbi(j) - 1, 0), g))
    q = pl.BlockSpec((SWA_T, SWA_GROUP * LANES), lambda g, j: (sbi(j), g))
    sink = pl.BlockSpec((1, SUBLANES, LANES), lambda g, j: (g, 0, 0))
    lse = pl.BlockSpec((SWA_GROUP, 1, SWA_T), lambda g, j: (g, 0, sbi(j)))
    return q, cur, prev, sink, lse


def _swa_fwd(qa, ka, va, sink_b):
    s_ = qa.shape[0]
    nsb = s_ // SWA_T
    c2 = HEAD_DIM ** -0.5 * LOG2E

    def body(q_ref, kc_ref, kp_ref, vc_ref, vp_ref, sk_ref, o32_ref, o16_ref, lse_ref, kx, vx):
        kx[0:BLOCK, :] = kp_ref[...]
        kx[BLOCK:5 * BLOCK, :] = kc_ref[...]
        vx[0:BLOCK, :] = vp_ref[...]
        vx[BLOCK:5 * BLOCK, :] = vc_ref[...]
        band, band0 = _swa_masks(pl.program_id(1))
        sink2 = _sink_row(sk_ref)
        for b in range(4):
            rs = slice(b * BLOCK, (b + 1) * BLOCK)
            ks = slice(b * BLOCK, (b + 2) * BLOCK)
            st = lax.dot_general(kx[ks, :], _heads_to_rows(q_ref, rs), NT, preferred_element_type=F32) * c2
            st = jnp.where(band0 if b == 0 else band, st, -jnp.inf)
            m = jnp.maximum(jnp.max(st, axis=0, keepdims=True), sink2)
            pt = jnp.exp2(st - m)
            den = jnp.sum(pt, axis=0, keepdims=True) + jnp.exp2(sink2 - m)
            o = lax.dot_general((pt * (1.0 / den)).astype(MXU_DTYPE), vx[ks, :], TN, preferred_element_type=F32)
            lse = m + jnp.log2(den)
            for hh in range(SWA_GROUP):
                cs = slice(hh * LANES, (hh + 1) * LANES)
                o32_ref[rs, cs] = o[cs, :]
                o16_ref[rs, cs] = o[cs, :].astype(o16_ref.dtype)
                lse_ref[hh, :, rs] = lse[:, cs]

    q, cur, prev, sink, lse_spec = _swa_in_specs(False, nsb)
    return pl.pallas_call(
        body, name="swa_fwd", grid=(SWA_KV_HEADS, nsb), in_specs=[q, cur, prev, cur, prev, sink],
        out_specs=[q, q, lse_spec],
        out_shape=[jax.ShapeDtypeStruct((s_, SWA_HEADS * LANES), F32), jax.ShapeDtypeStruct((s_, SWA_HEADS * LANES), MXU_DTYPE),
                   jax.ShapeDtypeStruct((SWA_HEADS, 1, s_), F32)],
        scratch_shapes=[pltpu.VMEM((5 * BLOCK, LANES), MXU_DTYPE), pltpu.VMEM((5 * BLOCK, LANES), MXU_DTYPE)],
        compiler_params=_cparams(("parallel", "arbitrary")),
    )(qa, ka, ka, va, va, sink_b)


def _swa_bwd(qa, ka, va, sink_b, o32, do, lse):
    s_ = qa.shape[0]
    nsb = s_ // SWA_T
    scale = HEAD_DIM ** -0.5
    c2 = scale * LOG2E

    def body(q_ref, kc_ref, kp_ref, vc_ref, vp_ref, sk_ref, o_ref, do_ref, lse_ref,
             dq_ref, dk_ref, dv_ref, dsk_ref, kx, vx, kacc, vacc, kcar, vcar):
        j = pl.program_id(1)
        kx[0:BLOCK, :] = kp_ref[...]
        kx[BLOCK:5 * BLOCK, :] = kc_ref[...]
        vx[0:BLOCK, :] = vp_ref[...]
        vx[BLOCK:5 * BLOCK, :] = vc_ref[...]
        band, band0 = _swa_masks(nsb - 1 - j)
        kacc[...] = jnp.zeros(kacc.shape, F32)
        vacc[...] = jnp.zeros(vacc.shape, F32)

        @pl.when(j == 0)
        def _():
            kcar[...] = jnp.zeros(kcar.shape, F32)
            vcar[...] = jnp.zeros(vcar.shape, F32)
            dsk_ref[...] = jnp.zeros(dsk_ref.shape, F32)

        sink2 = _sink_row(sk_ref)
        dsink = jnp.zeros((1, SWA_W), F32)
        for b in range(4):
            rs = slice(b * BLOCK, (b + 1) * BLOCK)
            ks = slice(b * BLOCK, (b + 2) * BLOCK)
            q, k2, v2 = _heads_to_rows(q_ref, rs), kx[ks, :], vx[ks, :]
            d = _heads_to_rows(do_ref, rs)
            delta = jnp.sum((d * _heads_to_rows(o_ref, rs)).T, axis=0, keepdims=True)
            l2 = jnp.concatenate([lse_ref[hh, :, rs] for hh in range(SWA_GROUP)], axis=1)
            st = lax.dot_general(k2, q, NT, preferred_element_type=F32) * c2
            pt = jnp.exp2(jnp.where(band0 if b == 0 else band, st, -jnp.inf) - l2)
            db = d.astype(MXU_DTYPE)
            dst = (pt * (lax.dot_general(v2, db, NT, preferred_element_type=F32) - delta) * scale).astype(MXU_DTYPE)
            dq = lax.dot_general(dst, k2, TN, preferred_element_type=F32)
            for hh in range(SWA_GROUP):
                dq_ref[rs, hh * LANES:(hh + 1) * LANES] = dq[hh * LANES:(hh + 1) * LANES, :]
            kacc[ks, :] += jnp.dot(dst, q, preferred_element_type=F32)
            vacc[ks, :] += jnp.dot(pt.astype(MXU_DTYPE), db, preferred_element_type=F32)
            dsink = dsink - jnp.exp2(sink2 - l2) * delta
        for hh in range(SWA_GROUP):
            tot = jnp.sum(dsink[:, hh * LANES:(hh + 1) * LANES], axis=1, keepdims=True)
            dsk_ref[0, hh:hh + 1, :] += jnp.broadcast_to(tot, (1, LANES))

        dk_ref[0:3 * BLOCK, :] = kacc[BLOCK:4 * BLOCK, :]
        dk_ref[3 * BLOCK:4 * BLOCK, :] = kacc[4 * BLOCK:5 * BLOCK, :] + kcar[...]
        dv_ref[0:3 * BLOCK, :] = vacc[BLOCK:4 * BLOCK, :].astype(dv_ref.dtype)
        dv_ref[3 * BLOCK:4 * BLOCK, :] = (vacc[4 * BLOCK:5 * BLOCK, :] + vcar[...]).astype(dv_ref.dtype)
        kcar[...] = kacc[0:BLOCK, :]
        vcar[...] = vacc[0:BLOCK, :]

    q, cur, prev, sink, lse_spec = _swa_in_specs(True, nsb)
    return pl.pallas_call(
        body, name="swa_bwd", grid=(SWA_KV_HEADS, nsb),
        in_specs=[q, cur, prev, cur, prev, sink, q, q, lse_spec],
        out_specs=[q, cur, cur, sink],
        out_shape=[jax.ShapeDtypeStruct((s_, SWA_HEADS * LANES), F32), jax.ShapeDtypeStruct((s_, SWA_KV_HEADS * LANES), F32),
                   jax.ShapeDtypeStruct((s_, SWA_KV_HEADS * LANES), MXU_DTYPE),
                   jax.ShapeDtypeStruct((SWA_KV_HEADS, SUBLANES, LANES), F32)],
        scratch_shapes=[pltpu.VMEM((5 * BLOCK, LANES), MXU_DTYPE), pltpu.VMEM((5 * BLOCK, LANES), MXU_DTYPE),
                        pltpu.VMEM((5 * BLOCK, LANES), F32), pltpu.VMEM((5 * BLOCK, LANES), F32),
                        pltpu.VMEM((BLOCK, LANES), F32), pltpu.VMEM((BLOCK, LANES), F32)],
        compiler_params=_cparams(("arbitrary", "arbitrary")),
    )(qa, ka, ka, va, va, sink_b, o32, do, lse)


MLA_T = 512
MLA_FWD_GROUP = 4
MLA_BWD_GROUP = 2


def _mla_specs(s_, t, group):
    w = group * LANES
    qs = pl.BlockSpec((t, w), lambda g, i: (i, g))
    kv = pl.BlockSpec((s_, w), lambda g, i: (0, g))
    row = pl.BlockSpec((group, 1, t), lambda g, i: (g, 0, i))
    return qs, kv, row


def _causal_scores_t(k, q, t, c2, masked):
    st = lax.dot_general(k, q, NT, preferred_element_type=F32) * c2
    if masked:
        kr = lax.broadcasted_iota(jnp.int32, (t, t), 0)
        qc = lax.broadcasted_iota(jnp.int32, (t, t), 1)
        st = jnp.where(kr <= qc, st, -jnp.inf)
    return st


def _mla_fwd(qc, kc, vp):
    s_ = qc.shape[0]
    t = min(MLA_T, s_)
    c2 = MLA_QK ** -0.5 * LOG2E
    grp = MLA_FWD_GROUP

    def body(q_ref, k_ref, v_ref, o32_ref, o16_ref, lse_ref, m_s, acc_s):
        qi = pl.program_id(1)
        m_s[...] = jnp.full(m_s.shape, -jnp.inf, F32)
        acc_s[...] = jnp.zeros(acc_s.shape, F32)
        ones_lane = lax.broadcasted_iota(jnp.int32, (t, LANES), 1) == MLA_V

        def step(ki, masked):
            off = pl.multiple_of(ki * t, t)
            for g in range(grp):
                cs = slice(g * LANES, (g + 1) * LANES)
                st = _causal_scores_t(k_ref[pl.ds(off, t), cs], q_ref[:, cs], t, c2, masked)
                m_old = m_s[g]
                m_new = jnp.maximum(m_old, jnp.max(st, axis=0, keepdims=True))
                alpha = jnp.exp2(m_old - m_new)
                pt = jnp.exp2(st - m_new).astype(MXU_DTYPE)
                v = v_ref[pl.ds(off, t), cs]
                v = jnp.where(ones_lane, jnp.ones((), v.dtype), v)
                acc_s[g] = alpha * acc_s[g] + lax.dot_general(v, pt, TN, preferred_element_type=F32)
                m_s[g] = m_new

        def full_block(ki, carry):
            step(ki, False)
            return carry

        lax.fori_loop(0, qi, full_block, 0)
        step(qi, True)
        for g in range(grp):
            cs = slice(g * LANES, (g + 1) * LANES)
            acc = acc_s[g]
            l = acc[MLA_V:MLA_V + 1, :]
            o = (acc * (1.0 / l)).T
            o32_ref[:, cs] = o
            o16_ref[:, cs] = o.astype(o16_ref.dtype)
            lse_ref[g] = m_s[g] + jnp.log2(l)

    qs, kv, row = _mla_specs(s_, t, grp)
    return pl.pallas_call(
        body, name="mla_fwd", grid=(MLA_HEADS // grp, s_ // t), in_specs=[qs, kv, kv], out_specs=[qs, qs, row],
        out_shape=[jax.ShapeDtypeStruct((s_, MLA_HEADS * LANES), F32), jax.ShapeDtypeStruct((s_, MLA_HEADS * LANES), MXU_DTYPE),
                   jax.ShapeDtypeStruct((MLA_HEADS, 1, s_), F32)],
        scratch_shapes=[pltpu.VMEM((grp, 1, t), F32), pltpu.VMEM((grp, LANES, t), F32)],
        compiler_params=_cparams(("parallel", "arbitrary")),
    )(qc, kc, vp)


def _mla_bwd(qc, kc, vp, dob, lse, delta):
    s_ = qc.shape[0]
    t = min(MLA_T, s_)
    scale = MLA_QK ** -0.5
    c2 = scale * LOG2E
    grp = MLA_BWD_GROUP

    def body(q_ref, do_ref, lse_ref, dl_ref, k_ref, v_ref, dq_ref, dk_ref, dv_ref, dqt_s):
        qi = pl.program_id(1)

        @pl.when(qi == 0)
        def _():
            dk_ref[...] = jnp.zeros(dk_ref.shape, F32)
            dv_ref[...] = jnp.zeros(dv_ref.shape, F32)

        dqt_s[...] = jnp.zeros(dqt_s.shape, F32)

        def step(ki, masked):
            off = pl.multiple_of(ki * t, t)
            for g in range(grp):
                cs = slice(g * LANES, (g + 1) * LANES)
                q, d, k = q_ref[:, cs], do_ref[:, cs], k_ref[pl.ds(off, t), cs]
                pt = jnp.exp2(_causal_scores_t(k, q, t, c2, masked) - lse_ref[g])
                dpt = lax.dot_general(v_ref[pl.ds(off, t), cs], d, NT, preferred_element_type=F32)
                dst = (pt * (dpt - dl_ref[g]) * scale).astype(MXU_DTYPE)
                dv_ref[pl.ds(off, t), cs] += jnp.dot(pt.astype(MXU_DTYPE), d, preferred_element_type=F32)
                dk_ref[pl.ds(off, t), cs] += jnp.dot(dst, q, preferred_element_type=F32)
                dqt_s[g] += lax.dot_general(k, dst, TN, preferred_element_type=F32)

        def full_block(ki, carry):
            step(ki, False)
            return carry

        lax.fori_loop(0, qi, full_block, 0)
        step(qi, True)
        for g in range(grp):
            dq_ref[:, g * LANES:(g + 1) * LANES] = dqt_s[g].T

    qs, kv, row = _mla_specs(s_, t, grp)
    shp = jax.ShapeDtypeStruct((s_, MLA_HEADS * LANES), F32)
    return pl.pallas_call(
        body, name="mla_bwd", grid=(MLA_HEADS // grp, s_ // t), in_specs=[qs, qs, row, row, kv, kv],
        out_specs=[qs, kv, kv], out_shape=[shp, shp, shp], scratch_shapes=[pltpu.VMEM((grp, LANES, t), F32)],
        compiler_params=_cparams(("parallel", "arbitrary")),
    )(qc, dob, lse, delta, kc, vp)


def _pad_heads(w, nh, hd, axis):
    shp = w.shape
    w = w.reshape(shp[:axis] + (nh, hd) + shp[axis + 1:])
    pad = [(0, 0)] * w.ndim
    pad[axis + 1] = (0, LANES - hd)
    w = jnp.pad(w, pad)
    return w.reshape(shp[:axis] + (nh * LANES,) + shp[axis + 1:])


def _unpad_heads(w, nh, hd, axis):
    shp = w.shape
    w = w.reshape(shp[:axis] + (nh, LANES) + shp[axis + 1:])
    w = lax.slice_in_dim(w, 0, hd, axis=axis + 1)
    return w.reshape(shp[:axis] + (nh * hd,) + shp[axis + 1:])


PACK_W = 1024
ROW_TILE = 16
FULL_SHAPE = dict(w_in=(1024, 3488), w_uq=(384, 768), w_ukv=(256, 1024), w_o_swa=(512, 1024), w_o_mla=(512, 1024),
                  w_out=(1024, 1024), w_gate=(1024, 2816), w_up=(1024, 2816), w_down=(2816, 1024))
BIG = tuple(FULL_SHAPE)
ROW_SHARDED = ("w_out", "w_down")
W_IN_COLS = FULL_SHAPE["w_in"][1] // N_DEV
W_IN_ROWS = -(-W_IN_COLS // ROW_TILE) * ROW_TILE
FF_COLS = D_FF // N_DEV
OUT_ROWS = D_MODEL // N_DEV
SMALL_FLAT = (("w_uq", 0, 36), ("w_ukv", 48, 32))
SMALL_USED = 80
MID_BLOCKS = 4
MID_ROWS = MID_BLOCKS * OUT_ROWS
EARLY_ROWS = W_IN_ROWS + MID_ROWS
LATE_ROWS = 3 * FF_COLS
PACK_ROWS = EARLY_ROWS + LATE_ROWS


def _shard_shape(n):
    r, c = FULL_SHAPE[n]
    return (r // N_DEV, c) if n in ROW_SHARDED else (r, c // N_DEV)


def _wire_pack(sh, dtype):
    c = lambda n: sh[n].astype(dtype)
    rows = [jnp.pad(c("w_in").T, ((0, W_IN_ROWS - W_IN_COLS), (0, 0))), c("w_out"),
            _pad_heads(c("w_o_swa").T, SWA_HEADS, HEAD_DIM, 1), _pad_heads(c("w_o_mla").T, MLA_HEADS, MLA_V, 1)]
    for n, _, r in SMALL_FLAT:
        rows.append(jnp.pad(c(n).reshape(r, PACK_W), ((0, -r % ROW_TILE), (0, 0))))
    rows.append(jnp.zeros((OUT_ROWS - SMALL_USED, PACK_W), dtype))
    return jnp.concatenate(rows + [c("w_gate").T, c("w_up").T, c("w_down")], 0)


def _mid_unpack(p):
    out = dict(w_out=p[0:OUT_ROWS], w_o_swa=_unpad_heads(p[OUT_ROWS:2 * OUT_ROWS], SWA_HEADS, HEAD_DIM, 1).T,
               w_o_mla=_unpad_heads(p[2 * OUT_ROWS:3 * OUT_ROWS], MLA_HEADS, MLA_V, 1).T)
    for n, off, r in SMALL_FLAT:
        out[n] = p[3 * OUT_ROWS + off:3 * OUT_ROWS + off + r].reshape(_shard_shape(n))
    return out


def _w_in_row_maps():
    sp = lambda col: (col // W_IN_COLS) * W_IN_ROWS + col % W_IN_COLS
    fwd = np.full((P_W,), -1, np.int64)

    def put(t0, c0, n):
        fwd[t0:t0 + n] = [sp(c) for c in range(c0, c0 + n)]

    put(P_GA, IN_OFF[6], D_MODEL)
    put(P_GB, IN_OFF[7], D_MODEL)
    for h in range(SWA_HEADS):
        put(P_Q + LANES * h, IN_OFF[0] + HEAD_DIM * h, HEAD_DIM)
    put(P_QLAT, IN_OFF[3], Q_LORA)
    put(P_KR + KR_LANE, IN_OFF[5], MLA_ROPE)
    for h in range(SWA_KV_HEADS):
        put(P_K + LANES * h, IN_OFF[1] + HEAD_DIM * h, HEAD_DIM)
        put(P_V + LANES * h, IN_OFF[2] + HEAD_DIM * h, HEAD_DIM)
    put(P_KVLAT, IN_OFF[4], KV_LORA)
    inv = np.full((N_DEV * W_IN_ROWS,), -1, np.int64)
    inv[fwd[fwd >= 0]] = np.nonzero(fwd >= 0)[0]
    return fwd, inv


def _take_rows(src, idx, *, name, tile=2 * LANES):
    n_out, n_src, width = len(idx), src.shape[0], src.shape[1]
    assert n_out % tile == 0 and n_src % tile == 0
    n_tiles = n_out // tile
    blocks = [sorted({int(v) // tile for v in idx[i * tile:(i + 1) * tile] if v >= 0}) for i in range(n_tiles)]
    k_max = max(1, max(len(b) for b in blocks))
    tab = np.zeros((n_tiles, k_max), np.int32)
    sel = np.zeros((n_tiles, k_max, tile, tile), np.float32)
    for i, blks in enumerate(blocks):
        for m, b in enumerate(blks):
            tab[i, m] = b
            for r in range(tile):
                v = int(idx[i * tile + r])
                if v >= 0 and v // tile == b:
                    sel[i, m, r, v % tile] = 1.0

    def body(tab_ref, sel_ref, *refs):
        o_ref = refs[k_max]
        acc = jnp.dot(sel_ref[0, 0], refs[0][...], preferred_element_type=F32)
        for m in range(1, k_max):
            acc = acc + jnp.dot(sel_ref[0, m], refs[m][...], preferred_element_type=F32)
        o_ref[...] = acc.astype(o_ref.dtype)

    def src_spec(m):
        return pl.BlockSpec((tile, width), lambda i, t: (t[i * k_max + m], 0))

    return pl.pallas_call(
        body, name=name,
        grid_spec=pltpu.PrefetchScalarGridSpec(
            num_scalar_prefetch=1, grid=(n_tiles,),
            in_specs=[pl.BlockSpec((1, k_max, tile, tile), lambda i, t: (i, 0, 0, 0))] + [src_spec(m) for m in range(k_max)],
            out_specs=pl.BlockSpec((tile, width), lambda i, t: (i, 0))),
        out_shape=jax.ShapeDtypeStruct((n_out, width), src.dtype),
        compiler_params=_cparams(("parallel",)),
    )(jnp.asarray(tab.reshape(-1)), jnp.asarray(sel, src.dtype), *([src] * k_max))


def _w_in_operand(win_g):
    return _take_rows(win_g.reshape(N_DEV * W_IN_ROWS, PACK_W), _w_in_row_maps()[0], name="w_in_rows")


def _mid_operands(wout_g, woa_g, wob_g, small_g):
    def full(n, off, r):
        a = small_g[:, off:off + r].reshape((N_DEV,) + _shard_shape(n))
        return jnp.moveaxis(a, 0, 1).reshape(FULL_SHAPE[n])

    w = {n: full(n, off, r) for n, off, r in SMALL_FLAT}
    ukv = w["w_ukv"].reshape(KV_LORA, MLA_HEADS, MLA_NOPE + MLA_V)
    return dict(
        wout=wout_g.reshape(D_MODEL, D_MODEL), woa_t=woa_g.reshape(D_MODEL, -1), wob_t=wob_g.reshape(D_MODEL, -1),
        wuq=_pad_heads(w["w_uq"], MLA_HEADS, MLA_QK, 1),
        wuk=_pad_heads(ukv[:, :, :MLA_NOPE].reshape(KV_LORA, -1), MLA_HEADS, MLA_NOPE, 1),
        wuv=_pad_heads(ukv[:, :, MLA_NOPE:].reshape(KV_LORA, -1), MLA_HEADS, MLA_V, 1),
    )


def _mid_grad_pack(g):
    uk = _unpad_heads(g["wukv"][:, :1024], MLA_HEADS, MLA_NOPE, 1).reshape(KV_LORA, MLA_HEADS, MLA_NOPE)
    uv = _unpad_heads(g["wukv"][:, 1024:], MLA_HEADS, MLA_V, 1).reshape(KV_LORA, MLA_HEADS, MLA_V)
    w = dict(w_uq=_unpad_heads(g["wuq"], MLA_HEADS, MLA_QK, 1), w_ukv=jnp.concatenate([uk, uv], 2).reshape(KV_LORA, -1))
    rows = []
    for n, _, r in SMALL_FLAT:
        rr, cc = FULL_SHAPE[n]
        a = jnp.moveaxis(w[n].reshape(rr, N_DEV, cc // N_DEV), 1, 0).reshape(N_DEV, r, PACK_W)
        rows.append(jnp.pad(a, ((0, 0), (0, -r % ROW_TILE), (0, 0))).astype(WIRE_DTYPE))
    rows.append(jnp.zeros((N_DEV, OUT_ROWS - SMALL_USED, PACK_W), WIRE_DTYPE))
    blk = lambda a: a.reshape(N_DEV, OUT_ROWS, PACK_W)
    return [blk(g["wout"]), blk(g["woa_t"]), blk(g["wob_t"]), jnp.concatenate(rows, 1)]


def _w_in_grad_chunks(g_win_t):
    return _take_rows(g_win_t, _w_in_row_maps()[1], name="dw_in_rows").reshape(N_DEV, W_IN_ROWS, PACK_W)


def _local_step(x, tgt, win_t, small, weights, grads):
    s_ = x.shape[0]
    tabs = _rope_tables(s_)
    sink_b = jnp.broadcast_to(small["swa_sinks"].reshape(SWA_KV_HEADS, SWA_GROUP, 1), (SWA_KV_HEADS, SWA_GROUP, LANES))
    sink_b = jnp.pad(sink_b, ((0, 0), (0, SUBLANES - SWA_GROUP), (0, 0)))

    h, qa, ka, va, cq, ckv, kro, p = _proj_in(x, small["mix_norm_g"], win_t, small["q_norm_g"], small["kv_norm_g"], tabs)
    oa32, oa16, lse_a = _swa_fwd(qa, ka, va, sink_b)
    ops = weights.mid(oa16)
    qc, kc, vp = _mla_up(cq, ckv, kro, ops["wuq"], ops["wuk"], ops["wuv"], tabs)
    ob32, ob16, lse_b = _mla_fwd(qc, kc, vp)
    ta, tb, y = _attn_out_gate(oa16, ob16, ops["woa_t"], ops["wob_t"], p)
    x1 = _mm(y, ops["wout"], "nn", name="out_proj", add=x, tm=1024, tn=1024)
    wgu_t, wd = weights.late(x1)
    h2, gu, act = _ffn_in_act(x1, small["ffn_norm_g"], wgu_t)

    dx2, dx2b, dg3, _, tot = _ffn_out_loss(act, wd, x1, small["final_norm_g"].reshape(1, D_MODEL), tgt)
    g = {}
    g_wd = _mm(act, dx2b, "tn", name="dw_down", tm=FF_TILE, tn=1024, tk=2048, out_dtype=WIRE_DTYPE)
    dgu = _d_act_swiglu(dx2b, wd, gu)
    g_wgu = _mm(dgu, h2, "tn", name="dw_ffn_in", tm=FF_TILE, tn=1024, tk=2048, out_dtype=WIRE_DTYPE)
    token = grads.late(g_wgu, g_wd)
    dx1, dx1b, dg2 = _mm_norm_bwd(dgu, wgu_t, x1, small["ffn_norm_g"] + token[0:1, 0:1], dx2, name="d_h2")
    g["wout"] = _mm(y, dx1b, "tn", name="dw_out", tm=1024, tn=1024, tk=2048, out_dtype=WIRE_DTYPE)
    dta, dtb, dgab = _d_y_gate(dx1b, ops["wout"], p, ta, tb)
    doa = _mm(dta, ops["woa_t"], "nn", name="d_oa", tm=1024, tn=1024)
    g["woa_t"] = _mm(dta, oa16, "tn", name="dw_o_swa", tm=1024, tn=1024, tk=2048, out_dtype=WIRE_DTYPE)
    g["wob_t"] = _mm(dtb, ob16, "tn", name="dw_o_mla", tm=1024, tn=1024, tk=2048, out_dtype=WIRE_DTYPE)
    dob16, delta_b = _mla_d_out(dtb, ops["wob_t"], ob32)
    dqc, dkc, dvp = _mla_bwd(qc, kc, vp, dob16, lse_b, delta_b)
    dqp, dkv, dkr, dqlat, dkvlat, dgq, dgkv = _mla_up_bwd(
        dqc, dkc, dvp, ops["wuq"], jnp.concatenate([ops["wuk"], ops["wuv"]], 1), p, small["q_norm_g"], small["kv_norm_g"], tabs)
    g["wuq"] = _mm(cq, dqp, "tn", name="dw_uq", tm=Q_LORA, tn=1024, tk=2048)
    g["wukv"] = _mm(ckv, dkv, "tn", name="dw_ukv", tm=KV_LORA, tn=2048, tk=2048)
    token = grads.mid(g)
    dqa, dka, dva, dsk = _swa_bwd(qa, ka, va, sink_b + token[0:1, 0:1], oa32, doa, lse_a)
    dp = _assemble_dp(dgab, dqa, dqlat, dkr, dka, dva, dkvlat, tabs)
    token = grads.last(_mm(dp, h, "tn", name="dw_in", tm=2176, tn=1024, tk=1024, out_dtype=WIRE_DTYPE))
    gx, _, dg1 = _mm_norm_bwd(dp, win_t, x, small["mix_norm_g"], dx1, name="d_h", after=token)

    sm = dict(mix_norm_g=dg1, ffn_norm_g=dg2, final_norm_g=dg3, q_norm_g=dgq, kv_norm_g=dgkv,
              swa_sinks=dsk[:, :SWA_GROUP, 0].reshape(1, SWA_HEADS))
    return tot, gx, sm


MESH = pl.DeviceIdType.MESH
ANY = pl.BlockSpec(memory_space=pl.ANY)


def _position():
    return lax.axis_index("x"), lax.axis_index("y"), lax.axis_index("c")


def _all_gather(block, pieces, shapes, *, name):
    n_out = len(shapes)
    n_rows = sum(p[3] for p in pieces)

    def body(x_ref, *refs):
        outs, (send_sems, recv_sems, local_sem) = refs[:n_out], refs[n_out:]
        x, y, c = _position()
        me, sibling = (x, y, c), (x, y, 1 - c)
        chips = [(1 - x, y), (x, 1 - y), (1 - x, 1 - y)]

        def dst(piece, blk):
            arr, lead, _, _ = piece
            return outs[arr].at[lead(4 * blk[0] + 2 * blk[1] + blk[2])]

        def own(piece):
            return x_ref.at[pl.ds(piece[2], piece[3])]

        def copies(k, blk, to, from_input):
            return [pltpu.make_async_remote_copy(
                src_ref=own(p) if from_input else dst(p, blk), dst_ref=dst(p, blk), send_sem=send_sems.at[k],
                recv_sem=recv_sems.at[k], device_id=to, device_id_type=MESH) for p in pieces]

        gathered_rows = x_ref.at[pl.ds(0, n_rows)]

        def whole_block(k):
            return pltpu.make_async_remote_copy(src_ref=gathered_rows, dst_ref=gathered_rows, send_sem=send_sems.at[k],
                                                recv_sem=recv_sems.at[k], device_id=me, device_id_type=MESH)

        for p in pieces:
            pltpu.make_async_copy(own(p), dst(p, me), local_sem).start()
        for cp in copies(0, me, sibling, True):
            cp.start()
        for j, chip in enumerate(chips):
            for cp in copies(1 + j, me, (*chip, c), True):
                cp.start()
        for j, chip in enumerate(chips):
            whole_block(1 + j).wait_recv()
            for cp in copies(4 + j, (*chip, c), sibling, False):
                cp.start()
        whole_block(0).wait_recv()
        for j in range(3):
            whole_block(4 + j).wait_recv()
        for k in range(7):
            whole_block(k).wait_send()
        pltpu.make_async_copy(gathered_rows, gathered_rows, local_sem).wait()

    return pl.pallas_call(
        body, name=name, out_shape=[jax.ShapeDtypeStruct(s, block.dtype) for s in shapes], in_specs=[ANY],
        out_specs=[ANY] * n_out,
        scratch_shapes=[pltpu.SemaphoreType.DMA((7,)), pltpu.SemaphoreType.DMA((7,)), pltpu.SemaphoreType.DMA],
    )(block)


HBM = pl.BlockSpec(memory_space=pltpu.HBM)
SEM = pl.BlockSpec(memory_space=pltpu.SEMAPHORE)
TILE_DEVS = FF_TILE // FF_COLS
GU_SHAPE = (2, 2, TILE_DEVS, FF_COLS, PACK_W)


def _gate_slab(d):
    return (d // TILE_DEVS, 0, d % TILE_DEVS)


def _up_slab(d):
    return (d // TILE_DEVS, 1, d % TILE_DEVS)
D_SHAPE = (N_DEV, FF_COLS, PACK_W)
LAND_SHAPE = (N_DEV, LATE_ROWS, PACK_W)


def _split_params():
    return pltpu.CompilerParams(has_side_effects=pltpu.SideEffectType.DATAFLOW_SIDE_EFFECTING)


def _peer(x, y, c, k):
    return ((1 - x) if k & 4 else x, (1 - y) if k & 2 else y, (1 - c) if k & 1 else c)


def _empty_hbm(shape, dtype):
    return pltpu.with_memory_space_constraint(lax.empty(shape, dtype), pltpu.HBM)


def _wait_all(rows, send_sems, recv_sems, me):
    for k in range(N_DEV - 1):
        cp = pltpu.make_async_remote_copy(src_ref=rows, dst_ref=rows, send_sem=send_sems.at[k], recv_sem=recv_sems.at[k],
                                          device_id=me, device_id_type=MESH)
        cp.wait_send()
        cp.wait_recv()


def _token_shape():
    return jax.ShapeDtypeStruct((SUBLANES, LANES), F32)


def _gather_start(pack, row0, pieces, shapes, *, name):
    n = len(shapes)

    def body(*refs):
        p_ref, bufs, send_sems, recv_sems, token = refs[0], refs[1:1 + n], refs[1 + n], refs[2 + n], refs[-1]
        x, y, c = _position()
        me = 4 * x + 2 * y + c
        for k in range(1, N_DEV):
            off = row0
            for buf, lead, rows in pieces:
                pltpu.make_async_remote_copy(
                    src_ref=p_ref.at[pl.ds(off, rows)], dst_ref=bufs[buf].at[lead(me)], send_sem=send_sems.at[k - 1],
                    recv_sem=recv_sems.at[k - 1], device_id=_peer(x, y, c, k), device_id_type=MESH).start()
                off += rows
        token[...] = jnp.zeros_like(token)

    sems, dt = pltpu.SemaphoreType.DMA((N_DEV - 1,)), pack.dtype
    return pl.pallas_call(
        body, name=name,
        out_shape=(sems, sems, pltpu.HBM(pack.shape, dt)) + tuple(pltpu.HBM(s, dt) for s in shapes) + (_token_shape(),),
        in_specs=(HBM,) * (1 + n), out_specs=(SEM, SEM) + (HBM,) * (1 + n) + (pl.BlockSpec(memory_space=pltpu.VMEM),),
        input_output_aliases={i: 2 + i for i in range(1 + n)}, compiler_params=_split_params(),
    )(pltpu.with_memory_space_constraint(pack, pltpu.HBM), *[_empty_hbm(s, dt) for s in shapes])


def _gather_wait(started, row0, n_rows, after, *, name):
    send_sems, recv_sems, pack, *bufs = started[:-1]
    n = len(bufs)

    def body(*refs):
        _wait_all(refs[0].at[pl.ds(row0, n_rows)], refs[1 + n], refs[2 + n], _position())

    outs = pl.pallas_call(
        body, name=name, out_shape=tuple(pltpu.HBM(a.shape, a.dtype) for a in (pack, *bufs)),
        in_specs=(HBM,) * (1 + n) + (SEM, SEM, ANY), out_specs=(HBM,) * (1 + n),
        input_output_aliases={i: i for i in range(1 + n)}, compiler_params=_split_params(),
    )(pack, *bufs, send_sems, recv_sems, after)
    return outs[0], outs[1:]


def _scatter_start(srcs, pieces, *, name):
    n = len(srcs)
    land_shape = (N_DEV, sum(p[2] for p in pieces), PACK_W)

    def body(*refs):
        src_refs, land_ref, send_sems, recv_sems, token = refs[:n], refs[n], refs[n + 1], refs[n + 2], refs[-1]
        x, y, c = _position()
        me = 4 * x + 2 * y + c
        for k in range(1, N_DEV):
            px, py, pc = _peer(x, y, c, k)
            off = 0
            for si, lead, rows in pieces:
                pltpu.make_async_remote_copy(
                    src_ref=src_refs[si].at[lead(4 * px + 2 * py + pc)], dst_ref=land_ref.at[me, pl.ds(off, rows)],
                    send_sem=send_sems.at[k - 1], recv_sem=recv_sems.at[k - 1], device_id=(px, py, pc),
                    device_id_type=MESH).start()
                off += rows
        token[...] = jnp.zeros_like(token)

    sems, dt = pltpu.SemaphoreType.DMA((N_DEV - 1,)), srcs[0].dtype
    return pl.pallas_call(
        body, name=name,
        out_shape=(sems, sems) + tuple(pltpu.HBM(a.shape, dt) for a in srcs) + (pltpu.HBM(land_shape, dt), _token_shape()),
        in_specs=(HBM,) * (n + 1), out_specs=(SEM, SEM) + (HBM,) * (n + 1) + (pl.BlockSpec(memory_space=pltpu.VMEM),),
        input_output_aliases={i: 2 + i for i in range(n + 1)}, compiler_params=_split_params(),
    )(*[pltpu.with_memory_space_constraint(a, pltpu.HBM) for a in srcs], _empty_hbm(land_shape, dt))


def _scatter_wait(started, after, *, name):
    send_sems, recv_sems, *bufs = started[:-1]
    n = len(bufs)

    def body(*refs):
        _wait_all(refs[n - 1].at[0], refs[n], refs[n + 1], _position())

    return pl.pallas_call(
        body, name=name, out_shape=tuple(pltpu.HBM(a.shape, a.dtype) for a in bufs),
        in_specs=(HBM,) * n + (SEM, SEM, ANY), out_specs=(HBM,) * n, input_output_aliases={i: i for i in range(n)},
        compiler_params=_split_params(),
    )(*bufs, send_sems, recv_sems, after)


def _peer_sum(own, own_lead, land, block, rows, idx, *, name):
    owns = list(own) if isinstance(own, (list, tuple)) else [own]
    n, lead_rank = len(owns), owns[0].ndim - 2

    def body(idx_ref, *refs):
        own_refs, land_refs, o_ref = refs[:n], refs[n:n + N_DEV - 1], refs[n + N_DEV - 1]
        for j in range(n):
            rs_ = slice(j * rows, (j + 1) * rows)
            acc = own_refs[j][(0,) * lead_rank].astype(F32)
            for k in range(N_DEV - 1):
                acc = acc + land_refs[k][0, rs_].astype(F32)
            o_ref[rs_] = acc

    own_spec = pl.BlockSpec((1,) * lead_rank + (rows, PACK_W), lambda i, t: own_lead(t[0]) + (0, 0))

    def land_spec(k):
        return pl.BlockSpec((1, n * rows, PACK_W), lambda i, t: (t[k + 1], block, 0))

    return pl.pallas_call(
        body, name=name,
        grid_spec=pltpu.PrefetchScalarGridSpec(
            num_scalar_prefetch=1, grid=(1,), in_specs=[own_spec] * n + [land_spec(k) for k in range(N_DEV - 1)],
            out_specs=pl.BlockSpec((n * rows, PACK_W), lambda i, t: (0, 0))),
        out_shape=jax.ShapeDtypeStruct((n * rows, PACK_W), F32), compiler_params=_cparams(("arbitrary",)),
    )(idx, *owns, *([land] * (N_DEV - 1)))


def _adamw(w, g, m, v):
    m = ADAM_B1 * m + (1.0 - ADAM_B1) * g
    v = ADAM_B2 * v + (1.0 - ADAM_B2) * (g * g)
    m_hat = m / (1.0 - ADAM_B1 ** ADAM_STEP)
    v_hat = v / (1.0 - ADAM_B2 ** ADAM_STEP)
    delta = -ADAM_LR * (m_hat / (jnp.sqrt(v_hat) + ADAM_EPS) + ADAM_WD * w)
    return delta, m, v


def _adamw_call(w, g, m, v, *, name, max_rows=256):
    _, r, c_ = w.shape
    tr = max_rows if r > max_rows and r % max_rows == 0 else r

    def body(w_ref, g_ref, m_ref, v_ref, d_ref, mo_ref, vo_ref):
        d, mn, vn = _adamw(w_ref[0], g_ref[...], m_ref[0], v_ref[0])
        d_ref[0] = d
        mo_ref[0] = mn
        vo_ref[0] = vn

    row3 = pl.BlockSpec((1, tr, c_), lambda i: (0, i, 0))
    shp = jax.ShapeDtypeStruct((1, r, c_), F32)
    return pl.pallas_call(
        body, name=name, grid=(r // tr,), in_specs=[row3, pl.BlockSpec((tr, c_), lambda i: (i, 0)), row3, row3],
        out_specs=[row3] * 3, out_shape=[shp] * 3, compiler_params=_cparams(("parallel",)),
    )(w, g, m, v)


SMALL = ("mix_norm_g", "ffn_norm_g", "final_norm_g", "q_norm_g", "kv_norm_g", "swa_sinks")
SMALL_W = dict(mix_norm_g=1024, ffn_norm_g=1024, final_norm_g=1024, q_norm_g=Q_LORA, kv_norm_g=KV_LORA, swa_sinks=SWA_HEADS)


def _small_adamw(parts, w, m, v):
    ns = len(SMALL)

    def body(p_ref, *refs):
        ins, outs = refs[:3 * ns], refs[3 * ns:]
        tot = p_ref[0]
        for dev in range(1, N_DEV):
            tot = tot + p_ref[dev]
        for k, n in enumerate(SMALL):
            g = jnp.sum(tot[k * SUBLANES:(k + 1) * SUBLANES, :SMALL_W[n]], axis=0, keepdims=True)
            res = _adamw(ins[k][...], g, ins[ns + k][...], ins[2 * ns + k][...])
            for j, r in enumerate((g,) + tuple(res)):
                outs[j * ns + k][...] = r
        outs[4 * ns][...] = jnp.sum(tot[ns * SUBLANES:(ns + 1) * SUBLANES, 0:1], axis=0, keepdims=True)

    shapes = [jax.ShapeDtypeStruct((1, SMALL_W[n]), F32) for n in SMALL]
    vm = pl.BlockSpec(memory_space=pltpu.VMEM)
    out = pl.pallas_call(
        body, name="small_adamw", in_specs=[vm] * (1 + 3 * ns), out_specs=[vm] * (4 * ns + 1),
        out_shape=shapes * 4 + [jax.ShapeDtypeStruct((1, 1), F32)],
    )(parts, *[d[n] for d in (w, m, v) for n in SMALL])
    return [dict(zip(SMALL, out[j * ns:(j + 1) * ns])) for j in range(4)] + [out[4 * ns]]


def _small_pack(d, rows_each):
    parts = [jnp.pad(d[n].astype(F32), ((0, 0), (0, PACK_W - SMALL_W[n]))) for n in SMALL]
    out = jnp.concatenate(parts, 0)
    pad = -out.shape[0] % SUBLANES
    return jnp.pad(out, ((0, pad), (0, 0)))


def kernel(x, mix_norm_g, w_in, swa_sinks, q_norm_g, w_uq, kv_norm_g, w_ukv, w_o_swa, w_o_mla, w_out, ffn_norm_g, w_gate, w_up, w_down, final_norm_g, loss_target, m_mix_norm_g, m_w_in, m_swa_sinks, m_q_norm_g, m_w_uq, m_kv_norm_g, m_w_ukv, m_w_o_swa, m_w_o_mla, m_w_out, m_ffn_norm_g, m_w_gate, m_w_up, m_w_down, m_final_norm_g, v_mix_norm_g, v_w_in, v_swa_sinks, v_q_norm_g, v_w_uq, v_kv_norm_g, v_w_ukv, v_w_o_swa, v_w_o_mla, v_w_out, v_ffn_norm_g, v_w_gate, v_w_up, v_w_down, v_final_norm_g):
    big_w = dict(w_in=w_in[0], w_uq=w_uq[0], w_ukv=w_ukv[0], w_o_swa=w_o_swa[0], w_o_mla=w_o_mla[0], w_out=w_out[0],
                 w_gate=w_gate[0], w_up=w_up[0], w_down=w_down[0])
    big_w3 = dict(w_in=w_in, w_uq=w_uq, w_ukv=w_ukv, w_o_swa=w_o_swa, w_o_mla=w_o_mla, w_out=w_out, w_gate=w_gate, w_up=w_up,
                  w_down=w_down)
    big_m = dict(w_in=m_w_in, w_uq=m_w_uq, w_ukv=m_w_ukv, w_o_swa=m_w_o_swa, w_o_mla=m_w_o_mla, w_out=m_w_out,
                 w_gate=m_w_gate, w_up=m_w_up, w_down=m_w_down)
    big_v = dict(w_in=v_w_in, w_uq=v_w_uq, w_ukv=v_w_ukv, w_o_swa=v_w_o_swa, w_o_mla=v_w_o_mla, w_out=v_w_out,
                 w_gate=v_w_gate, w_up=v_w_up, w_down=v_w_down)
    small_w = dict(mix_norm_g=mix_norm_g, ffn_norm_g=ffn_norm_g, final_norm_g=final_norm_g.reshape(1, D_MODEL),
                   q_norm_g=q_norm_g, kv_norm_g=kv_norm_g, swa_sinks=swa_sinks)
    small_m = dict(mix_norm_g=m_mix_norm_g, ffn_norm_g=m_ffn_norm_g, final_norm_g=m_final_norm_g.reshape(1, D_MODEL),
                   q_norm_g=m_q_norm_g, kv_norm_g=m_kv_norm_g, swa_sinks=m_swa_sinks)
    small_v = dict(mix_norm_g=v_mix_norm_g, ffn_norm_g=v_ffn_norm_g, final_norm_g=v_final_norm_g.reshape(1, D_MODEL),
                   q_norm_g=v_q_norm_g, kv_norm_g=v_kv_norm_g, swa_sinks=v_swa_sinks)

    px, py, pc = _position()
    me = 4 * px + 2 * py + pc
    idx = jnp.stack([me] + [4 * qx + 2 * qy + qc for qx, qy, qc in (_peer(px, py, pc, k) for k in range(1, N_DEV))])
    idx = idx.astype(jnp.int32)

    dev = lambda d: (d,)
    pack = _wire_pack(big_w, WIRE_DTYPE)
    win_g, = _all_gather(pack, ((0, dev, 0, W_IN_ROWS),), ((N_DEV, W_IN_ROWS, PACK_W),), name="ag_early")
    mid_pieces = tuple((b, dev, OUT_ROWS) for b in range(MID_BLOCKS))
    ag_mid = _gather_start(pack, W_IN_ROWS, mid_pieces, ((N_DEV, OUT_ROWS, PACK_W),) * MID_BLOCKS, name="ag_mid_start")
    ag = {}

    def own_rows(r0, r1, shape):
        return pack[r0:r1].reshape(shape)

    def mid_weights(after):
        pack_mid, blocks = _gather_wait(ag_mid, W_IN_ROWS, MID_ROWS, after, name="ag_mid_wait")
        ag["late"] = _gather_start(pack_mid, EARLY_ROWS, ((0, _gate_slab, FF_COLS), (0, _up_slab, FF_COLS), (1, dev, FF_COLS)),
                                   (GU_SHAPE, D_SHAPE), name="ag_late_start")
        row0 = lambda b: W_IN_ROWS + b * OUT_ROWS
        ops = _mid_operands(*[lax.dynamic_update_slice(blk, own_rows(row0(b), row0(b + 1), (1, OUT_ROWS, PACK_W)), (me, 0, 0))
                              for b, blk in enumerate(blocks)])
        ops["wuq"] = ops["wuq"] + ag["late"][-1][0:1, 0:1].astype(ops["wuq"].dtype)
        return ops

    def late_weights(after):
        _, (gu, d) = _gather_wait(ag["late"], EARLY_ROWS, LATE_ROWS, after, name="ag_late_wait")
        slab = (1, 1, 1, FF_COLS, PACK_W)
        gu = lax.dynamic_update_slice(gu, own_rows(EARLY_ROWS, EARLY_ROWS + FF_COLS, slab), _gate_slab(me) + (0, 0))
        gu = lax.dynamic_update_slice(gu, own_rows(EARLY_ROWS + FF_COLS, EARLY_ROWS + 2 * FF_COLS, slab), _up_slab(me) + (0, 0))
        d = lax.dynamic_update_slice(d, own_rows(EARLY_ROWS + 2 * FF_COLS, PACK_ROWS, (1, FF_COLS, PACK_W)), (me, 0, 0))
        return gu.reshape(2 * D_FF, D_MODEL), d.reshape(D_FF, D_MODEL)

    rs = {}

    def late_grads(g_gu, g_d):
        rs["late"] = _scatter_start([g_gu.reshape(GU_SHAPE), g_d.reshape(D_SHAPE)],
                                    ((0, _gate_slab, FF_COLS), (0, _up_slab, FF_COLS), (1, dev, FF_COLS)),
                                    name="rs_late_start")
        return rs["late"][-1]

    def mid_grads(g):
        rs["mid"] = _scatter_start(_mid_grad_pack(g), mid_pieces, name="rs_mid_start")
        return rs["mid"][-1]

    def last_grads(g_win_t):
        rs["last"] = _scatter_start([_w_in_grad_chunks(g_win_t)], ((0, dev, W_IN_ROWS),), name="rs_last_start")
        return rs["last"][-1]

    first_w = dict(small_w, mix_norm_g=mix_norm_g + ag_mid[-1][0:1, 0:1])
    loss_tot, gx, g_small = _local_step(
        x[0], loss_target[0], _w_in_operand(win_g), first_w, types.SimpleNamespace(mid=mid_weights, late=late_weights),
        types.SimpleNamespace(late=late_grads, mid=mid_grads, last=last_grads))

    g_gu, g_d, land_late = _scatter_wait(rs["late"], gx, name="rs_late_wait")
    *g_mid, land_mid = _scatter_wait(rs["mid"], gx, name="rs_mid_wait")
    g_win, land_last = _scatter_wait(rs["last"], gx, name="rs_last_wait")
    gw_t = dict(w_gate=_peer_sum(g_gu, _gate_slab, land_late, 0, FF_COLS, idx, name="rs_sum_gate"),
                w_up=_peer_sum(g_gu, _up_slab, land_late, 1, FF_COLS, idx, name="rs_sum_up"),
                w_in=_peer_sum(g_win, dev, land_last, 0, W_IN_ROWS, idx, name="rs_sum_in")[0:W_IN_COLS])
    gw = dict(w_down=_peer_sum(g_d, dev, land_late, 2, FF_COLS, idx, name="rs_sum_down"))
    gw.update(_mid_unpack(_peer_sum(g_mid, dev, land_mid, 0, OUT_ROWS, idx, name="rs_sum_mid")))
    dw, mw, vw = {}, {}, {}
    swap = lambda a: jnp.swapaxes(a, 1, 2)
    for n in BIG:
        if n in gw_t:
            res = _adamw_call(swap(big_w3[n]), gw_t[n], swap(big_m[n]), swap(big_v[n]), name="adamw_" + n)
            dw[n], mw[n], vw[n] = (swap(r) for r in res)
        else:
            dw[n], mw[n], vw[n] = _adamw_call(big_w3[n], gw[n], big_m[n], big_v[n], name="adamw_" + n)
    gw = {n: g[None] for n, g in gw.items()}
    gw.update({n: swap(g[None]) for n, g in gw_t.items()})

    loss_rows = jnp.pad(loss_tot[0:1, 0:1], ((0, SUBLANES - 1), (0, PACK_W - 1)))
    small_rows = jnp.concatenate([_small_pack(g_small_rows(g_small), SUBLANES), loss_rows], 0)
    parts, = _all_gather(small_rows, ((0, lambda d: (d,), 0, small_rows.shape[0]),), ((N_DEV,) + small_rows.shape,),
                         name="ag_small")
    gs, ds, ms, vs, loss = _small_adamw(parts, small_w, small_m, small_v)
    loss = loss[0, 0]
    for d in (gs, ds, ms, vs):
        d["final_norm_g"] = d["final_norm_g"].reshape(D_MODEL)

    order = ("mix_norm_g", "w_in", "swa_sinks", "q_norm_g", "w_uq", "kv_norm_g", "w_ukv", "w_o_swa", "w_o_mla", "w_out",
             "ffn_norm_g", "w_gate", "w_up", "w_down", "final_norm_g")

    def leaves(big, small):
        return [big[n] if n in big else small[n] for n in order]

    return (loss, gx[None], *leaves(gw, gs), *leaves(dw, ds), *leaves(mw, ms), *leaves(vw, vs))


def g_small_rows(g_small):
    out = dict(g_small)
    out["swa_sinks"] = jnp.pad(g_small["swa_sinks"], ((0, SUBLANES - 1), (0, 0)))
    return out
```

```python
import types

import numpy as np
import jax
import jax.numpy as jnp
from jax import lax
from jax.experimental import pallas as pl
from jax.experimental.pallas import tpu as pltpu

F32 = jnp.float32
MXU_DTYPE = jnp.bfloat16
WIRE_DTYPE = jnp.bfloat16

D_MODEL = 1024
EPS = 1e-6
ROPE_THETA = 10000.0
BLOCK = 128
HEAD_DIM = 64
SWA_HEADS = 8
SWA_KV_HEADS = 2
SWA_GROUP = SWA_HEADS // SWA_KV_HEADS
MLA_HEADS = 8
MLA_NOPE = 64
MLA_ROPE = 32
MLA_V = 64
MLA_QK = MLA_NOPE + MLA_ROPE
Q_LORA = 384
KV_LORA = 256
D_FF = 2816
IN_SIZES = (512, 128, 128, Q_LORA, KV_LORA, MLA_ROPE, D_MODEL, D_MODEL)
IN_OFF = tuple(int(v) for v in np.cumsum((0,) + IN_SIZES))
ADAM_LR, ADAM_B1, ADAM_B2, ADAM_EPS, ADAM_WD, ADAM_STEP = 0.001, 0.9, 0.999, 1e-08, 0.01, 10

LANES = 128
SUBLANES = 8
VMEM_LIMIT = 48 * 1024 * 1024
N_DEV = 8

P_GA, P_GB, P_Q, P_QLAT, P_KR, P_K, P_V, P_KVLAT, P_W = 0, 1024, 2048, 3072, 3456, 3584, 3840, 4096, 4352
KR_LANE = 64

LOG2E = 1.4426950408889634

NT = (((1,), (1,)), ((), ()))
NN = (((1,), (0,)), ((), ()))
TN = (((0,), (0,)), ((), ()))


def _cparams(sem):
    return pltpu.CompilerParams(dimension_semantics=sem, vmem_limit_bytes=VMEM_LIMIT)


def _mm(a, b, mode, *, name, out_dtype=F32, add=None, tm=512, tn=512, tk=None):
    if mode == "nn":
        (M, K), (K2, N) = a.shape, b.shape
    elif mode == "nt":
        (M, K), (N, K2) = a.shape, b.shape
    else:
        (K, M), (K2, N) = a.shape, b.shape
    assert K == K2, (a.shape, b.shape, mode)
    tm, tn, tk = min(tm, M), min(tn, N), K if tk is None else min(tk, K)
    assert M % tm == 0 and N % tn == 0 and K % tk == 0, (M, N, K, tm, tn, tk)
    nk = K // tk
    dn = {"nn": NN, "nt": NT, "tn": TN}[mode]
    if mode == "tn":
        a_spec = pl.BlockSpec((tk, tm), lambda i, j, k: (k, i))
    else:
        a_spec = pl.BlockSpec((tm, tk), lambda i, j, k: (i, k))
    once = dict(pipeline_mode=pl.Buffered(1)) if (nk == 1 and tn == N) else {}
    if mode == "nt":
        b_spec = pl.BlockSpec((tn, tk), lambda i, j, k: (j, k), **once)
    else:
        b_spec = pl.BlockSpec((tk, tn), lambda i, j, k: (k, j), **once)
    o_spec = pl.BlockSpec((tm, tn), lambda i, j, k: (i, j))
    has_add = add is not None

    def body(*refs):
        a_ref, b_ref = refs[0], refs[1]
        add_ref = refs[2] if has_add else None
        o_ref = refs[2 + has_add]
        p = lax.dot_general(a_ref[...], b_ref[...], dn, preferred_element_type=F32)

        def finish(acc):
            if has_add:
                acc = acc + add_ref[...]
            o_ref[...] = acc.astype(o_ref.dtype)

        if nk == 1:
            finish(p)
        else:
            acc_ref = refs[-1]
            k = pl.program_id(2)

            @pl.when(k == 0)
            def _():
                acc_ref[...] = p

            @pl.when((k > 0) & (k < nk - 1))
            def _():
                acc_ref[...] += p

            @pl.when(k == nk - 1)
            def _():
                finish(acc_ref[...] + p)

    ins = [a, b] + ([add] if has_add else [])
    return pl.pallas_call(
        body, name=name, grid=(M // tm, N // tn, nk), in_specs=[a_spec, b_spec] + ([o_spec] if has_add else []), out_specs=o_spec,
        out_shape=jax.ShapeDtypeStruct((M, N), out_dtype),
        scratch_shapes=[pltpu.VMEM((tm, tn), F32)] if nk > 1 else [],
        compiler_params=_cparams(("parallel", "parallel", "arbitrary")),
    )(*ins)


def _rows(ts, w, cb=0):
    return pl.BlockSpec((ts, w), lambda i: (i, cb))


def _const(r, w):
    return pl.BlockSpec((r, w), lambda i: (0, 0))


def _sublane_sum(v):
    ts, c = v.shape
    return jnp.sum(v.reshape(ts // SUBLANES, SUBLANES, c), axis=0)


def _sigmoid(v):
    return 1.0 / (1.0 + jnp.exp(-v))


def _rope(v, cos, s_up, s_dn, up, dn):
    return v * cos + pltpu.roll(v, up, 1) * s_up + pltpu.roll(v, dn, 1) * s_dn


def _rope_t(dv, cos, s_up, s_dn, up, dn):
    return dv * cos + pltpu.roll(dv * s_up, dn, 1) + pltpu.roll(dv * s_dn, up, 1)


def _rope_tables(seq):
    pos = np.arange(seq, dtype=np.float32)[:, None]

    def base(dim):
        inv = np.float32(ROPE_THETA) ** (-np.arange(0, dim, 2, dtype=np.float32) / np.float32(dim))
        ang = (pos * inv.astype(np.float32)[None, :]).astype(np.float32)
        return np.cos(ang).astype(np.float32), np.sin(ang).astype(np.float32)

    z = lambda n: np.zeros((seq, n), np.float32)
    ca, sa = base(HEAD_DIM)
    a_cos = np.concatenate([ca, ca, z(64)], 1)
    a_up = np.concatenate([-sa, z(96)], 1)
    a_dn = np.concatenate([z(32), sa, z(64)], 1)
    cb, sb = base(MLA_ROPE)
    one = np.ones((seq, 64), np.float32)
    q_cos = np.concatenate([one, cb, cb, z(32)], 1)
    k_cos = np.concatenate([z(64), cb, cb, z(32)], 1)
    b_up = np.concatenate([z(64), -sb, z(48)], 1)
    b_dn = np.concatenate([z(80), sb, z(32)], 1)
    return tuple(jnp.asarray(t) for t in (a_cos, a_up, a_dn, q_cos, k_cos, b_up, b_dn))


def _rms(v, g):
    return v * lax.rsqrt(jnp.mean(v * v, axis=-1, keepdims=True) + EPS) * g


def _rms_bwd(v, g, d):
    r = lax.rsqrt(jnp.mean(v * v, axis=-1, keepdims=True) + EPS)
    xh = v * r
    dxh = d * g
    return r * (dxh - xh * jnp.mean(dxh * xh, axis=-1, keepdims=True)), d * xh


F_GA, F_GB, F_KVLAT, F_QLAT, F_W = 0, 1024, 2048, 2304, 2688


def _proj_in(x, g, w_t, gq, gkv, tabs, *, tm=512):
    s_, c = x.shape
    a_cos, a_up, a_dn, _, k_cos, b_up, b_dn = tabs

    def body(x_ref, g_ref, w_ref, gq_ref, gkv_ref, ac, au, ad, kc, bu, bd,
             h_ref, qa_ref, ka_ref, va_ref, cq_ref, ckv_ref, kro_ref, pf_ref):
        h = _rms(x_ref[...], g_ref[...]).astype(h_ref.dtype)
        h_ref[...] = h
        mm = lambda a, b: lax.dot_general(h, w_ref[a:b, :], NT, preferred_element_type=F32)
        pf_ref[:, F_GA:F_KVLAT] = mm(P_GA, P_Q)
        c_, u_, d_ = ac[...], au[...], ad[...]
        q = mm(P_Q, P_QLAT)
        for hd in range(SWA_HEADS):
            sl = slice(hd * LANES, (hd + 1) * LANES)
            qa_ref[:, sl] = _rope(q[:, sl], c_, u_, d_, 96, 32).astype(qa_ref.dtype)
        kv = mm(P_KR, P_KVLAT)
        kro_ref[...] = _rope(kv[:, :LANES], kc[...], bu[...], bd[...], 112, 16)
        for hd in range(SWA_KV_HEADS):
            sl = slice((1 + hd) * LANES, (2 + hd) * LANES)
            ka_ref[:, hd * LANES:(hd + 1) * LANES] = _rope(kv[:, sl], c_, u_, d_, 96, 32).astype(ka_ref.dtype)
        va_ref[...] = kv[:, P_V - P_KR:].astype(va_ref.dtype)
        for a, b, f0, gref, dst in ((P_QLAT, P_KR, F_QLAT, gq_ref, cq_ref), (P_KVLAT, P_W, F_KVLAT, gkv_ref, ckv_ref)):
            v = mm(a, b)
            pf_ref[:, f0:f0 + b - a] = v
            r = lax.rsqrt(jnp.mean(v * v, axis=-1, keepdims=True) + EPS)
            dst[...] = (v * r * gref[...]).astype(dst.dtype)

    tab = _rows(tm, LANES)
    widths = (c, SWA_HEADS * LANES, SWA_KV_HEADS * LANES, SWA_KV_HEADS * LANES, Q_LORA, KV_LORA)
    return pl.pallas_call(
        body, name="proj_in", grid=(s_ // tm,),
        in_specs=[_rows(tm, c), _const(1, c), pl.BlockSpec((P_W, c), lambda i: (0, 0), pipeline_mode=pl.Buffered(1)),
                  _const(1, Q_LORA), _const(1, KV_LORA), tab, tab, tab, tab, tab, tab],
        out_specs=[_rows(tm, w) for w in widths] + [tab, _rows(tm, F_W)],
        out_shape=[jax.ShapeDtypeStruct((s_, w), MXU_DTYPE) for w in widths]
        + [jax.ShapeDtypeStruct((s_, LANES), F32), jax.ShapeDtypeStruct((s_, F_W), F32)],
        compiler_params=_cparams(("parallel",)),
    )(x, g, w_t, gq, gkv, a_cos, a_up, a_dn, k_cos, b_up, b_dn)


def _mm_norm_bwd(a, b, x, g, res, *, name, after=None, tm=512):
    s_, kk = a.shape
    c = b.shape[1]
    has_after = after is not None

    def body(*refs):
        a_ref, b_ref, x_ref, g_ref, res_ref = refs[:5]
        dx_ref, dxb_ref, dg_ref = refs[5 + has_after:]
        d = jnp.dot(a_ref[...], b_ref[...], preferred_element_type=F32)
        dx, gg = _rms_bwd(x_ref[...], g_ref[...], d)
        dx = dx + res_ref[...]
        dx_ref[...] = dx
        dxb_ref[...] = dx.astype(dxb_ref.dtype)

        @pl.when(pl.program_id(0) == 0)
        def _():
            dg_ref[...] = jnp.zeros(dg_ref.shape, F32)

        dg_ref[...] += _sublane_sum(gg)

    row = _rows(tm, c)
    in_specs = [_rows(tm, kk), pl.BlockSpec((kk, c), lambda i: (0, 0), pipeline_mode=pl.Buffered(1)), row, _const(1, c), row]
    return pl.pallas_call(
        body, name=name, grid=(s_ // tm,), in_specs=in_specs + ([pl.BlockSpec(memory_space=pl.ANY)] if has_after else []),
        out_specs=[row, row, _const(SUBLANES, c)],
        out_shape=[jax.ShapeDtypeStruct((s_, c), F32), jax.ShapeDtypeStruct((s_, c), MXU_DTYPE),
                   jax.ShapeDtypeStruct((SUBLANES, c), F32)],
        compiler_params=_cparams(("arbitrary",)),
    )(*([a, b, x, g, res] + ([after] if has_after else [])))


def _mla_up(cq, ckv, kro, wuq, wuk, wuv, tabs, *, ts=512):
    s_ = cq.shape[0]
    _, _, _, q_cos, _, b_up, b_dn = tabs

    def body(cq_ref, ckv_ref, kr_ref, wq_ref, wk_ref, wv_ref, qc, bu, bd, qo_ref, ko_ref, vo_ref):
        c_, u_, d_ = qc[...], bu[...], bd[...]
        kr = kr_ref[...]
        ckv_ = ckv_ref[...]
        vo_ref[...] = jnp.dot(ckv_, wv_ref[...], preferred_element_type=F32).astype(vo_ref.dtype)
        q = jnp.dot(cq_ref[...], wq_ref[...], preferred_element_type=F32)
        k = jnp.dot(ckv_, wk_ref[...], preferred_element_type=F32)
        for h in range(MLA_HEADS):
            sl = slice(h * LANES, (h + 1) * LANES)
            qo_ref[:, sl] = _rope(q[:, sl], c_, u_, d_, 112, 16).astype(qo_ref.dtype)
            ko_ref[:, sl] = (k[:, sl] + kr).astype(ko_ref.dtype)

    tab, out = _rows(ts, LANES), _rows(ts, 1024)
    return pl.pallas_call(
        body, name="mla_up", grid=(s_ // ts,),
        in_specs=[_rows(ts, Q_LORA), _rows(ts, KV_LORA), tab, _const(Q_LORA, 1024), _const(KV_LORA, 1024),
                  _const(KV_LORA, 1024), tab, tab, tab],
        out_specs=[out, out, out], out_shape=[jax.ShapeDtypeStruct((s_, 1024), MXU_DTYPE)] * 3,
        compiler_params=_cparams(("parallel",)),
    )(cq, ckv, kro, wuq, wuk, wuv, q_cos, b_up, b_dn)


def _mla_up_bwd(dqc, dkc, dvp, wuq, wukv, p, gq, gkv, tabs, *, ts=256):
    s_ = dqc.shape[0]
    _, _, _, q_cos, k_cos, b_up, b_dn = tabs

    def body(dq_ref, dk_ref, dv_ref, wq_ref, wkv_ref, ql_ref, kvl_ref, gq_ref, gkv_ref, qc, kc, bu, bd,
             dqo_ref, dkvo_ref, dkr_ref, dql_ref, dkvl_ref, dgq_ref, dgkv_ref):
        c_, u_, d_ = qc[...], bu[...], bd[...]
        tot = jnp.zeros((ts, LANES), F32)
        for h in range(MLA_HEADS):
            sl = slice(h * LANES, (h + 1) * LANES)
            dqo_ref[:, sl] = _rope_t(dq_ref[:, sl], c_, u_, d_, 112, 16).astype(dqo_ref.dtype)
            dk = dk_ref[:, sl]
            dkvo_ref[:, sl] = dk.astype(dkvo_ref.dtype)
            tot = tot + dk
        dkvo_ref[:, 1024:2048] = dv_ref[...].astype(dkvo_ref.dtype)
        dkr_ref[...] = _rope_t(tot, kc[...], u_, d_, 112, 16).astype(dkr_ref.dtype)

        @pl.when(pl.program_id(0) == 0)
        def _():
            dgq_ref[...] = jnp.zeros(dgq_ref.shape, F32)
            dgkv_ref[...] = jnp.zeros(dgkv_ref.shape, F32)

        for do_ref, w_ref, x_ref, g_ref, dx_ref, dg_ref in ((dqo_ref, wq_ref, ql_ref, gq_ref, dql_ref, dgq_ref),
                                                            (dkvo_ref, wkv_ref, kvl_ref, gkv_ref, dkvl_ref, dgkv_ref)):
            d = lax.dot_general(do_ref[...], w_ref[...], NT, preferred_element_type=F32)
            dx, gg = _rms_bwd(x_ref[...], g_ref[...], d)
            dx_ref[...] = dx.astype(dx_ref.dtype)
            dg_ref[...] += _sublane_sum(gg)

    tab = _rows(ts, LANES)
    return pl.pallas_call(
        body, name="mla_up_bwd", grid=(s_ // ts,),
        in_specs=[_rows(ts, 1024), _rows(ts, 1024), _rows(ts, 1024), _const(Q_LORA, 1024), _const(KV_LORA, 2048),
                  _rows(ts, Q_LORA, F_QLAT // Q_LORA), _rows(ts, KV_LORA, F_KVLAT // KV_LORA),
                  _const(1, Q_LORA), _const(1, KV_LORA), tab, tab, tab, tab],
        out_specs=[_rows(ts, 1024), _rows(ts, 2048), _rows(ts, LANES), _rows(ts, Q_LORA), _rows(ts, KV_LORA),
                   _const(SUBLANES, Q_LORA), _const(SUBLANES, KV_LORA)],
        out_shape=[jax.ShapeDtypeStruct((s_, 1024), MXU_DTYPE), jax.ShapeDtypeStruct((s_, 2048), MXU_DTYPE),
                   jax.ShapeDtypeStruct((s_, LANES), MXU_DTYPE), jax.ShapeDtypeStruct((s_, Q_LORA), MXU_DTYPE),
                   jax.ShapeDtypeStruct((s_, KV_LORA), MXU_DTYPE), jax.ShapeDtypeStruct((SUBLANES, Q_LORA), F32),
                   jax.ShapeDtypeStruct((SUBLANES, KV_LORA), F32)],
        compiler_params=_cparams(("arbitrary",)),
    )(dqc, dkc, dvp, wuq, wukv, p, p, gq, gkv, q_cos, k_cos, b_up, b_dn)


def _assemble_dp(dgab, dqa, dqlat, dkr, dka, dva, dkvlat, tabs, *, ts=256):
    s_ = dqa.shape[0]
    a_cos, a_up, a_dn = tabs[0], tabs[1], tabs[2]

    def body(dg_ref, dq_ref, dql_ref, dkr_ref, dk_ref, dv_ref, dkvl_ref, ac, au, ad, o_ref):
        c_, u_, d_ = ac[...], au[...], ad[...]
        o_ref[:, P_GA:P_Q] = dg_ref[...]
        for h in range(SWA_HEADS):
            sl = slice(h * LANES, (h + 1) * LANES)
            o_ref[:, P_Q + h * LANES:P_Q + (h + 1) * LANES] = _rope_t(dq_ref[:, sl], c_, u_, d_, 96, 32).astype(o_ref.dtype)
        o_ref[:, P_QLAT:P_KR] = dql_ref[...]
        o_ref[:, P_KR:P_K] = dkr_ref[...]
        for h in range(SWA_KV_HEADS):
            sl = slice(h * LANES, (h + 1) * LANES)
            o_ref[:, P_K + h * LANES:P_K + (h + 1) * LANES] = _rope_t(dk_ref[:, sl], c_, u_, d_, 96, 32).astype(o_ref.dtype)
        o_ref[:, P_V:P_KVLAT] = dv_ref[...]
        o_ref[:, P_KVLAT:P_W] = dkvl_ref[...]

    tab = _rows(ts, LANES)
    return pl.pallas_call(
        body, name="assemble_dp", grid=(s_ // ts,),
        in_specs=[_rows(ts, 2048), _rows(ts, 1024), _rows(ts, Q_LORA), _rows(ts, LANES), _rows(ts, 256), _rows(ts, 256),
                  _rows(ts, KV_LORA), tab, tab, tab],
        out_specs=_rows(ts, P_W), out_shape=jax.ShapeDtypeStruct((s_, P_W), MXU_DTYPE),
        compiler_params=_cparams(("parallel",)),
    )(dgab, dqa, dqlat, dkr, dka, dva, dkvlat, a_cos, a_up, a_dn)


def _attn_out_gate(oa, ob, woa_t, wob_t, p, *, ts=512):
    s_ = p.shape[0]

    def body(oa_ref, ob_ref, wa_ref, wb_ref, ga_ref, gb_ref, ta_ref, tb_ref, y_ref):
        ta = lax.dot_general(oa_ref[...], wa_ref[...], NT, preferred_element_type=F32)
        tb = lax.dot_general(ob_ref[...], wb_ref[...], NT, preferred_element_type=F32)
        ta_ref[...] = ta
        tb_ref[...] = tb
        y_ref[...] = (_sigmoid(ga_ref[...]) * ta + _sigmoid(gb_ref[...]) * tb).astype(y_ref.dtype)

    w = _const(1024, 1024)
    return pl.pallas_call(
        body, name="attn_out_gate", grid=(s_ // ts,),
        in_specs=[_rows(ts, 1024), _rows(ts, 1024), w, w, _rows(ts, 1024, F_GA // 1024), _rows(ts, 1024, F_GB // 1024)],
        out_specs=[_rows(ts, 1024)] * 3,
        out_shape=[jax.ShapeDtypeStruct((s_, 1024), F32)] * 2 + [jax.ShapeDtypeStruct((s_, 1024), MXU_DTYPE)],
        compiler_params=_cparams(("parallel",)),
    )(oa, ob, woa_t, wob_t, p, p)


def _d_y_gate(dx1b, wout, p, ta, tb, *, ts=512):
    s_ = p.shape[0]

    def body(dx_ref, w_ref, ga_ref, gb_ref, ta_ref, tb_ref, dta_ref, dtb_ref, dg_ref):
        d = lax.dot_general(dx_ref[...], w_ref[...], NT, preferred_element_type=F32)
        sa, sb = _sigmoid(ga_ref[...]), _sigmoid(gb_ref[...])
        dta_ref[...] = (d * sa).astype(dta_ref.dtype)
        dtb_ref[...] = (d * sb).astype(dtb_ref.dtype)
        dg_ref[:, 0:1024] = (d * ta_ref[...] * (sa * (1.0 - sa))).astype(dg_ref.dtype)
        dg_ref[:, 1024:2048] = (d * tb_ref[...] * (sb * (1.0 - sb))).astype(dg_ref.dtype)

    return pl.pallas_call(
        body, name="d_y_gate", grid=(s_ // ts,),
        in_specs=[_rows(ts, 1024), _const(1024, 1024), _rows(ts, 1024, F_GA // 1024), _rows(ts, 1024, F_GB // 1024),
                  _rows(ts, 1024), _rows(ts, 1024)],
        out_specs=[_rows(ts, 1024), _rows(ts, 1024), _rows(ts, 2048)],
        out_shape=[jax.ShapeDtypeStruct((s_, 1024), MXU_DTYPE)] * 2 + [jax.ShapeDtypeStruct((s_, 2048), MXU_DTYPE)],
        compiler_params=_cparams(("parallel",)),
    )(dx1b, wout, p, p, ta, tb)


FF_TILE = D_FF // 2


def _ffn_in_act(x1, g, wgu_t, *, tm=512):
    s_ = x1.shape[0]
    n = s_ // tm

    def body(x_ref, g_ref, w_ref, h_ref, gu_ref, a_ref):
        h = _rms(x_ref[...], g_ref[...]).astype(h_ref.dtype)
        h_ref[...] = h
        p = lax.dot_general(h, w_ref[...], NT, preferred_element_type=F32)
        gu_ref[...] = p
        gate = p[:, :FF_TILE]
        a_ref[...] = (gate * _sigmoid(gate) * p[:, FF_TILE:]).astype(a_ref.dtype)

    return pl.pallas_call(
        body, name="ffn_in", grid=(2, s_ // tm),
        in_specs=[pl.BlockSpec((tm, D_MODEL), lambda j, i: (i, 0)), pl.BlockSpec((1, D_MODEL), lambda j, i: (0, 0)),
                  pl.BlockSpec((2 * FF_TILE, D_MODEL), lambda j, i: (j, 0))],
        out_specs=[pl.BlockSpec((tm, D_MODEL), lambda j, i: (i + j * (n - 1 - i), 0)),
                   pl.BlockSpec((tm, 2 * FF_TILE), lambda j, i: (i, j)),
                   pl.BlockSpec((tm, FF_TILE), lambda j, i: (i, j))],
        out_shape=[jax.ShapeDtypeStruct((s_, D_MODEL), MXU_DTYPE), jax.ShapeDtypeStruct((s_, 2 * D_FF), F32),
                   jax.ShapeDtypeStruct((s_, D_FF), MXU_DTYPE)],
        compiler_params=_cparams(("arbitrary", "arbitrary")),
    )(x1, g, wgu_t)


def _d_act_swiglu(dx2b, wd, gu, *, tm=512):
    s_ = dx2b.shape[0]

    def body(d_ref, w_ref, gu_ref, o_ref):
        da = lax.dot_general(d_ref[...], w_ref[...], NT, preferred_element_type=F32)
        g, u = gu_ref[:, :FF_TILE], gu_ref[:, FF_TILE:]
        sg = _sigmoid(g)
        o_ref[:, :FF_TILE] = (da * u * (sg * (1.0 + g * (1.0 - sg)))).astype(o_ref.dtype)
        o_ref[:, FF_TILE:] = (da * (g * sg)).astype(o_ref.dtype)

    gu_spec = pl.BlockSpec((tm, 2 * FF_TILE), lambda j, i: (i, j))
    return pl.pallas_call(
        body, name="d_act", grid=(2, s_ // tm),
        in_specs=[pl.BlockSpec((tm, D_MODEL), lambda j, i: (i, 0)), pl.BlockSpec((FF_TILE, D_MODEL), lambda j, i: (j, 0)), gu_spec],
        out_specs=gu_spec, out_shape=jax.ShapeDtypeStruct((s_, 2 * D_FF), MXU_DTYPE),
        compiler_params=_cparams(("parallel", "parallel")),
    )(dx2b, wd, gu)


def _ffn_out_loss(act, wd, x1, g, tgt, *, ts=512):
    s_, c = x1.shape
    kk = act.shape[1]

    def body(a_ref, w_ref, x_ref, g_ref, t_ref, dx_ref, dxb_ref, dg_ref, lp_ref, tot_ref):
        v = x_ref[...] + jnp.dot(a_ref[...], w_ref[...], preferred_element_type=F32)
        r = lax.rsqrt(jnp.mean(v * v, axis=-1, keepdims=True) + EPS)
        xh = v * r
        gg = g_ref[...]
        e = xh * gg - t_ref[...]
        do = e * (1.0 / c)
        dxh = do * gg
        dx = r * (dxh - xh * jnp.mean(dxh * xh, axis=-1, keepdims=True))
        dx_ref[...] = dx
        dxb_ref[...] = dx.astype(dxb_ref.dtype)
        i = pl.program_id(0)

        @pl.when(i == 0)
        def _():
            dg_ref[...] = jnp.zeros(dg_ref.shape, F32)
            lp_ref[...] = jnp.zeros(lp_ref.shape, F32)

        dg_ref[...] += _sublane_sum(do * xh)
        lp_ref[...] += _sublane_sum(e * e)
        tot_ref[...] = jnp.full(tot_ref.shape, (0.5 / c) * jnp.sum(lp_ref[...]), F32)

    return pl.pallas_call(
        body, name="ffn_out_loss", grid=(s_ // ts,),
        in_specs=[_rows(ts, kk), _const(kk, c), _rows(ts, c), _const(1, c), _rows(ts, c)],
        out_specs=[_rows(ts, c), _rows(ts, c), _const(SUBLANES, c), _const(SUBLANES, c), _const(SUBLANES, LANES)],
        out_shape=[jax.ShapeDtypeStruct((s_, c), F32), jax.ShapeDtypeStruct((s_, c), MXU_DTYPE),
                   jax.ShapeDtypeStruct((SUBLANES, c), F32), jax.ShapeDtypeStruct((SUBLANES, c), F32),
                   jax.ShapeDtypeStruct((SUBLANES, LANES), F32)],
        compiler_params=_cparams(("arbitrary",)),
    )(act, wd, x1, g, tgt)


def _mla_d_out(dtb, wob_t, o32, *, ts=512):
    s_ = dtb.shape[0]

    def body(dt_ref, w_ref, o_ref, dob_ref, dl_ref):
        d = jnp.dot(dt_ref[...], w_ref[...], preferred_element_type=F32)
        dob_ref[...] = d.astype(dob_ref.dtype)
        prod = d * o_ref[...]
        for h in range(MLA_HEADS):
            dl_ref[h] = jnp.sum(prod[:, h * LANES:(h + 1) * LANES].T, axis=0, keepdims=True)

    return pl.pallas_call(
        body, name="mla_d_out", grid=(s_ // ts,), in_specs=[_rows(ts, 1024), _const(1024, 1024), _rows(ts, 1024)],
        out_specs=[_rows(ts, 1024), pl.BlockSpec((MLA_HEADS, 1, ts), lambda i: (0, 0, i))],
        out_shape=[jax.ShapeDtypeStruct((s_, 1024), MXU_DTYPE), jax.ShapeDtypeStruct((MLA_HEADS, 1, s_), F32)],
        compiler_params=_cparams(("parallel",)),
    )(dtb, wob_t, o32)


SWA_T = 4 * BLOCK


SWA_W = SWA_GROUP * BLOCK


def _swa_masks(sb):
    kr = lax.broadcasted_iota(jnp.int32, (2 * BLOCK, SWA_W), 0)
    qc = jnp.bitwise_and(lax.broadcasted_iota(jnp.int32, (2 * BLOCK, SWA_W), 1), BLOCK - 1)
    band = jnp.logical_and(kr > qc, kr <= qc + BLOCK)
    first = jnp.logical_and(band, kr >= BLOCK)
    return band, jnp.logical_or(first, jnp.logical_and(band, sb > 0))


def _heads_to_rows(ref, rs):
    return jnp.concatenate([ref[rs, h * LANES:(h + 1) * LANES] for h in range(SWA_GROUP)], axis=0)


def _sink_row(sk_ref):
    return jnp.concatenate([sk_ref[0, h:h + 1, :] for h in range(SWA_GROUP)], axis=1) * LOG2E


def _swa_in_specs(rev, nsb):
    sbi = (lambda j: nsb - 1 - j) if rev else (lambda j: j)
    cur = pl.BlockSpec((SWA_T, LANES), lambda g, j: (sbi(j), g))
    prev = pl.BlockSpec((BLOCK, LANES), lambda g, j: (jnp.maximum(4 * sbi(j) - 1, 0), g))
    q = pl.BlockSpec((SWA_T, SWA_GROUP * LANES), lambda g, j: (sbi(j), g))
    sink = pl.BlockSpec((1, SUBLANES, LANES), lambda g, j: (g, 0, 0))
    lse = pl.BlockSpec((SWA_GROUP, 1, SWA_T), lambda g, j: (g, 0, sbi(j)))
    return q, cur, prev, sink, lse


def _swa_fwd(qa, ka, va, sink_b):
    s_ = qa.shape[0]
    nsb = s_ // SWA_T
    c2 = HEAD_DIM ** -0.5 * LOG2E

    def body(q_ref, kc_ref, kp_ref, vc_ref, vp_ref, sk_ref, o32_ref, o16_ref, lse_ref, kx, vx):
        kx[0:BLOCK, :] = kp_ref[...]
        kx[BLOCK:5 * BLOCK, :] = kc_ref[...]
        vx[0:BLOCK, :] = vp_ref[...]
        vx[BLOCK:5 * BLOCK, :] = vc_ref[...]
        band, band0 = _swa_masks(pl.program_id(1))
        sink2 = _sink_row(sk_ref)
        for b in range(4):
            rs = slice(b * BLOCK, (b + 1) * BLOCK)
            ks = slice(b * BLOCK, (b + 2) * BLOCK)
            st = lax.dot_general(kx[ks, :], _heads_to_rows(q_ref, rs), NT, preferred_element_type=F32) * c2
            st = jnp.where(band0 if b == 0 else band, st, -jnp.inf)
            m = jnp.maximum(jnp.max(st, axis=0, keepdims=True), sink2)
            pt = jnp.exp2(st - m)
            den = jnp.sum(pt, axis=0, keepdims=True) + jnp.exp2(sink2 - m)
            o = lax.dot_general((pt * (1.0 / den)).astype(MXU_DTYPE), vx[ks, :], TN, preferred_element_type=F32)
            lse = m + jnp.log2(den)
            for hh in range(SWA_GROUP):
                cs = slice(hh * LANES, (hh + 1) * LANES)
                o32_ref[rs, cs] = o[cs, :]
                o16_ref[rs, cs] = o[cs, :].astype(o16_ref.dtype)
                lse_ref[hh, :, rs] = lse[:, cs]

    q, cur, prev, sink, lse_spec = _swa_in_specs(False, nsb)
    return pl.pallas_call(
        body, name="swa_fwd", grid=(SWA_KV_HEADS, nsb), in_specs=[q, cur, prev, cur, prev, sink],
        out_specs=[q, q, lse_spec],
        out_shape=[jax.ShapeDtypeStruct((s_, SWA_HEADS * LANES), F32), jax.ShapeDtypeStruct((s_, SWA_HEADS * LANES), MXU_DTYPE),
                   jax.ShapeDtypeStruct((SWA_HEADS, 1, s_), F32)],
        scratch_shapes=[pltpu.VMEM((5 * BLOCK, LANES), MXU_DTYPE), pltpu.VMEM((5 * BLOCK, LANES), MXU_DTYPE)],
        compiler_params=_cparams(("parallel", "arbitrary")),
    )(qa, ka, ka, va, va, sink_b)


def _swa_bwd(qa, ka, va, sink_b, o32, do, lse):
    s_ = qa.shape[0]
    nsb = s_ // SWA_T
    scale = HEAD_DIM ** -0.5
    c2 = scale * LOG2E

    def body(q_ref, kc_ref, kp_ref, vc_ref, vp_ref, sk_ref, o_ref, do_ref, lse_ref,
             dq_ref, dk_ref, dv_ref, dsk_ref, kx, vx, kacc, vacc, kcar, vcar):
        j = pl.program_id(1)
        kx[0:BLOCK, :] = kp_ref[...]
        kx[BLOCK:5 * BLOCK, :] = kc_ref[...]
        vx[0:BLOCK, :] = vp_ref[...]
        vx[BLOCK:5 * BLOCK, :] = vc_ref[...]
        band, band0 = _swa_masks(nsb - 1 - j)
        kacc[...] = jnp.zeros(kacc.shape, F32)
        vacc[...] = jnp.zeros(vacc.shape, F32)

        @pl.when(j == 0)
        def _():
            kcar[...] = jnp.zeros(kcar.shape, F32)
            vcar[...] = jnp.zeros(vcar.shape, F32)
            dsk_ref[...] = jnp.zeros(dsk_ref.shape, F32)

        sink2 = _sink_row(sk_ref)
        dsink = jnp.zeros((1, SWA_W), F32)
        for b in range(4):
            rs = slice(b * BLOCK, (b + 1) * BLOCK)
            ks = slice(b * BLOCK, (b + 2) * BLOCK)
            q, k2, v2 = _heads_to_rows(q_ref, rs), kx[ks, :], vx[ks, :]
            d = _heads_to_rows(do_ref, rs)
            delta = jnp.sum((d * _heads_to_rows(o_ref, rs)).T, axis=0, keepdims=True)
            l2 = jnp.concatenate([lse_ref[hh, :, rs] for hh in range(SWA_GROUP)], axis=1)
            st = lax.dot_general(k2, q, NT, preferred_element_type=F32) * c2
            pt = jnp.exp2(jnp.where(band0 if b == 0 else band, st, -jnp.inf) - l2)
            db = d.astype(MXU_DTYPE)
            dst = (pt * (lax.dot_general(v2, db, NT, preferred_element_type=F32) - delta) * scale).astype(MXU_DTYPE)
            dq = lax.dot_general(dst, k2, TN, preferred_element_type=F32)
            for hh in range(SWA_GROUP):
                dq_ref[rs, hh * LANES:(hh + 1) * LANES] = dq[hh * LANES:(hh + 1) * LANES, :]
            kacc[ks, :] += jnp.dot(dst, q, preferred_element_type=F32)
            vacc[ks, :] += jnp.dot(pt.astype(MXU_DTYPE), db, preferred_element_type=F32)
            dsink = dsink - jnp.exp2(sink2 - l2) * delta
        for hh in range(SWA_GROUP):
            tot = jnp.sum(dsink[:, hh * LANES:(hh + 1) * LANES], axis=1, keepdims=True)
            dsk_ref[0, hh:hh + 1, :] += jnp.broadcast_to(tot, (1, LANES))

        dk_ref[0:3 * BLOCK, :] = kacc[BLOCK:4 * BLOCK, :]
        dk_ref[3 * BLOCK:4 * BLOCK, :] = kacc[4 * BLOCK:5 * BLOCK, :] + kcar[...]
        dv_ref[0:3 * BLOCK, :] = vacc[BLOCK:4 * BLOCK, :].astype(dv_ref.dtype)
        dv_ref[3 * BLOCK:4 * BLOCK, :] = (vacc[4 * BLOCK:5 * BLOCK, :] + vcar[...]).astype(dv_ref.dtype)
        kcar[...] = kacc[0:BLOCK, :]
        vcar[...] = vacc[0:BLOCK, :]

    q, cur, prev, sink, lse_spec = _swa_in_specs(True, nsb)
    return pl.pallas_call(
        body, name="swa_bwd", grid=(SWA_KV_HEADS, nsb),
        in_specs=[q, cur, prev, cur, prev, sink, q, q, lse_spec],
        out_specs=[q, cur, cur, sink],
        out_shape=[jax.ShapeDtypeStruct((s_, SWA_HEADS * LANES), F32), jax.ShapeDtypeStruct((s_, SWA_KV_HEADS * LANES), F32),
                   jax.ShapeDtypeStruct((s_, SWA_KV_HEADS * LANES), MXU_DTYPE),
                   jax.ShapeDtypeStruct((SWA_KV_HEADS, SUBLANES, LANES), F32)],
        scratch_shapes=[pltpu.VMEM((5 * BLOCK, LANES), MXU_DTYPE), pltpu.VMEM((5 * BLOCK, LANES), MXU_DTYPE),
                        pltpu.VMEM((5 * BLOCK, LANES), F32), pltpu.VMEM((5 * BLOCK, LANES), F32),
                        pltpu.VMEM((BLOCK, LANES), F32), pltpu.VMEM((BLOCK, LANES), F32)],
        compiler_params=_cparams(("arbitrary", "arbitrary")),
    )(qa, ka, ka, va, va, sink_b, o32, do, lse)


MLA_T = 512
MLA_FWD_GROUP = 4
MLA_BWD_GROUP = 2


def _mla_specs(s_, t, group):
    w = group * LANES
    qs = pl.BlockSpec((t, w), lambda g, i: (i, g))
    kv = pl.BlockSpec((s_, w), lambda g, i: (0, g))
    row = pl.BlockSpec((group, 1, t), lambda g, i: (g, 0, i))
    return qs, kv, row


def _causal_scores_t(k, q, t, c2, masked):
    st = lax.dot_general(k, q, NT, preferred_element_type=F32) * c2
    if masked:
        kr = lax.broadcasted_iota(jnp.int32, (t, t), 0)
        qc = lax.broadcasted_iota(jnp.int32, (t, t), 1)
        st = jnp.where(kr <= qc, st, -jnp.inf)
    return st


def _mla_fwd(qc, kc, vp):
    s_ = qc.shape[0]
    t = min(MLA_T, s_)
    c2 = MLA_QK ** -0.5 * LOG2E
    grp = MLA_FWD_GROUP

    def body(q_ref, k_ref, v_ref, o32_ref, o16_ref, lse_ref, m_s, acc_s):
        qi = pl.program_id(1)
        m_s[...] = jnp.full(m_s.shape, -jnp.inf, F32)
        acc_s[...] = jnp.zeros(acc_s.shape, F32)
        ones_lane = lax.broadcasted_iota(jnp.int32, (t, LANES), 1) == MLA_V

        def step(ki, masked):
            off = pl.multiple_of(ki * t, t)
            for g in range(grp):
                cs = slice(g * LANES, (g + 1) * LANES)
                st = _causal_scores_t(k_ref[pl.ds(off, t), cs], q_ref[:, cs], t, c2, masked)
                m_old = m_s[g]
                m_new = jnp.maximum(m_old, jnp.max(st, axis=0, keepdims=True))
                alpha = jnp.exp2(m_old - m_new)
                pt = jnp.exp2(st - m_new).astype(MXU_DTYPE)
                v = v_ref[pl.ds(off, t), cs]
                v = jnp.where(ones_lane, jnp.ones((), v.dtype), v)
                acc_s[g] = alpha * acc_s[g] + lax.dot_general(v, pt, TN, preferred_element_type=F32)
                m_s[g] = m_new

        def full_block(ki, carry):
            step(ki, False)
            return carry

        lax.fori_loop(0, qi, full_block, 0)
        step(qi, True)
        for g in range(grp):
            cs = slice(g * LANES, (g + 1) * LANES)
            acc = acc_s[g]
            l = acc[MLA_V:MLA_V + 1, :]
            o = (acc * (1.0 / l)).T
            o32_ref[:, cs] = o
            o16_ref[:, cs] = o.astype(o16_ref.dtype)
            lse_ref[g] = m_s[g] + jnp.log2(l)

    qs, kv, row = _mla_specs(s_, t, grp)
    return pl.pallas_call(
        body, name="mla_fwd", grid=(MLA_HEADS // grp, s_ // t), in_specs=[qs, kv, kv], out_specs=[qs, qs, row],
        out_shape=[jax.ShapeDtypeStruct((s_, MLA_HEADS * LANES), F32), jax.ShapeDtypeStruct((s_, MLA_HEADS * LANES), MXU_DTYPE),
                   jax.ShapeDtypeStruct((MLA_HEADS, 1, s_), F32)],
        scratch_shapes=[pltpu.VMEM((grp, 1, t), F32), pltpu.VMEM((grp, LANES, t), F32)],
        compiler_params=_cparams(("parallel", "arbitrary")),
    )(qc, kc, vp)


def _mla_bwd(qc, kc, vp, dob, lse, delta):
    s_ = qc.shape[0]
    t = min(MLA_T, s_)
    scale = MLA_QK ** -0.5
    c2 = scale * LOG2E
    grp = MLA_BWD_GROUP

    def body(q_ref, do_ref, lse_ref, dl_ref, k_ref, v_ref, dq_ref, dk_ref, dv_ref, dqt_s):
        qi = pl.program_id(1)

        @pl.when(qi == 0)
        def _():
            dk_ref[...] = jnp.zeros(dk_ref.shape, F32)
            dv_ref[...] = jnp.zeros(dv_ref.shape, F32)

        dqt_s[...] = jnp.zeros(dqt_s.shape, F32)

        def step(ki, masked):
            off = pl.multiple_of(ki * t, t)
            for g in range(grp):
                cs = slice(g * LANES, (g + 1) * LANES)
                q, d, k = q_ref[:, cs], do_ref[:, cs], k_ref[pl.ds(off, t), cs]
                pt = jnp.exp2(_causal_scores_t(k, q, t, c2, masked) - lse_ref[g])
                dpt = lax.dot_general(v_ref[pl.ds(off, t), cs], d, NT, preferred_element_type=F32)
                dst = (pt * (dpt - dl_ref[g]) * scale).astype(MXU_DTYPE)
                dv_ref[pl.ds(off, t), cs] += jnp.dot(pt.astype(MXU_DTYPE), d, preferred_element_type=F32)
                dk_ref[pl.ds(off, t), cs] += jnp.dot(dst, q, preferred_element_type=F32)
                dqt_s[g] += lax.dot_general(k, dst, TN, preferred_element_type=F32)

        def full_block(ki, carry):
            step(ki, False)
            return carry

        lax.fori_loop(0, qi, full_block, 0)
        step(qi, True)
        for g in range(grp):
            dq_ref[:, g * LANES:(g + 1) * LANES] = dqt_s[g].T

    qs, kv, row = _mla_specs(s_, t, grp)
    shp = jax.ShapeDtypeStruct((s_, MLA_HEADS * LANES), F32)
    return pl.pallas_call(
        body, name="mla_bwd", grid=(MLA_HEADS // grp, s_ // t), in_specs=[qs, qs, row, row, kv, kv],
        out_specs=[qs, kv, kv], out_shape=[shp, shp, shp], scratch_shapes=[pltpu.VMEM((grp, LANES, t), F32)],
        compiler_params=_cparams(("parallel", "arbitrary")),
    )(qc, dob, lse, delta, kc, vp)


def _pad_heads(w, nh, hd, axis):
    shp = w.shape
    w = w.reshape(shp[:axis] + (nh, hd) + shp[axis + 1:])
    pad = [(0, 0)] * w.ndim
    pad[axis + 1] = (0, LANES - hd)
    w = jnp.pad(w, pad)
    return w.reshape(shp[:axis] + (nh * LANES,) + shp[axis + 1:])


def _unpad_heads(w, nh, hd, axis):
    shp = w.shape
    w = w.reshape(shp[:axis] + (nh, LANES) + shp[axis + 1:])
    w = lax.slice_in_dim(w, 0, hd, axis=axis + 1)
    return w.reshape(shp[:axis] + (nh * hd,) + shp[axis + 1:])


PACK_W = 1024
ROW_TILE = 16
FULL_SHAPE = dict(w_in=(1024, 3488), w_uq=(384, 768), w_ukv=(256, 1024), w_o_swa=(512, 1024), w_o_mla=(512, 1024),
                  w_out=(1024, 1024), w_gate=(1024, 2816), w_up=(1024, 2816), w_down=(2816, 1024))
BIG = tuple(FULL_SHAPE)
ROW_SHARDED = ("w_out", "w_down")
W_IN_COLS = FULL_SHAPE["w_in"][1] // N_DEV
W_IN_ROWS = -(-W_IN_COLS // ROW_TILE) * ROW_TILE
FF_COLS = D_FF // N_DEV
OUT_ROWS = D_MODEL // N_DEV
SMALL_FLAT = (("w_uq", 0, 36), ("w_ukv", 48, 32))
SMALL_USED = 80
MID_BLOCKS = 4
MID_ROWS = MID_BLOCKS * OUT_ROWS
EARLY_ROWS = W_IN_ROWS + MID_ROWS
LATE_ROWS = 3 * FF_COLS
PACK_ROWS = EARLY_ROWS + LATE_ROWS


def _shard_shape(n):
    r, c = FULL_SHAPE[n]
    return (r // N_DEV, c) if n in ROW_SHARDED else (r, c // N_DEV)


def _wire_pack(sh, dtype):
    c = lambda n: sh[n].astype(dtype)
    rows = [jnp.pad(c("w_in").T, ((0, W_IN_ROWS - W_IN_COLS), (0, 0))), c("w_out"),
            _pad_heads(c("w_o_swa").T, SWA_HEADS, HEAD_DIM, 1), _pad_heads(c("w_o_mla").T, MLA_HEADS, MLA_V, 1)]
    for n, _, r in SMALL_FLAT:
        rows.append(jnp.pad(c(n).reshape(r, PACK_W), ((0, -r % ROW_TILE), (0, 0))))
    rows.append(jnp.zeros((OUT_ROWS - SMALL_USED, PACK_W), dtype))
    return jnp.concatenate(rows + [c("w_gate").T, c("w_up").T, c("w_down")], 0)


def _mid_unpack(p):
    out = dict(w_out=p[0:OUT_ROWS], w_o_swa=_unpad_heads(p[OUT_ROWS:2 * OUT_ROWS], SWA_HEADS, HEAD_DIM, 1).T,
               w_o_mla=_unpad_heads(p[2 * OUT_ROWS:3 * OUT_ROWS], MLA_HEADS, MLA_V, 1).T)
    for n, off, r in SMALL_FLAT:
        out[n] = p[3 * OUT_ROWS + off:3 * OUT_ROWS + off + r].reshape(_shard_shape(n))
    return out


def _w_in_row_maps():
    sp = lambda col: (col // W_IN_COLS) * W_IN_ROWS + col % W_IN_COLS
    fwd = np.full((P_W,), -1, np.int64)

    def put(t0, c0, n):
        fwd[t0:t0 + n] = [sp(c) for c in range(c0, c0 + n)]

    put(P_GA, IN_OFF[6], D_MODEL)
    put(P_GB, IN_OFF[7], D_MODEL)
    for h in range(SWA_HEADS):
        put(P_Q + LANES * h, IN_OFF[0] + HEAD_DIM * h, HEAD_DIM)
    put(P_QLAT, IN_OFF[3], Q_LORA)
    put(P_KR + KR_LANE, IN_OFF[5], MLA_ROPE)
    for h in range(SWA_KV_HEADS):
        put(P_K + LANES * h, IN_OFF[1] + HEAD_DIM * h, HEAD_DIM)
        put(P_V + LANES * h, IN_OFF[2] + HEAD_DIM * h, HEAD_DIM)
    put(P_KVLAT, IN_OFF[4], KV_LORA)
    inv = np.full((N_DEV * W_IN_ROWS,), -1, np.int64)
    inv[fwd[fwd >= 0]] = np.nonzero(fwd >= 0)[0]
    return fwd, inv


def _take_rows(src, idx, *, name, tile=2 * LANES):
    n_out, n_src, width = len(idx), src.shape[0], src.shape[1]
    assert n_out % tile == 0 and n_src % tile == 0
    n_tiles = n_out // tile
    blocks = [sorted({int(v) // tile for v in idx[i * tile:(i + 1) * tile] if v >= 0}) for i in range(n_tiles)]
    k_max = max(1, max(len(b) for b in blocks))
    tab = np.zeros((n_tiles, k_max), np.int32)
    sel = np.zeros((n_tiles, k_max, tile, tile), np.float32)
    for i, blks in enumerate(blocks):
        for m, b in enumerate(blks):
            tab[i, m] = b
            for r in range(tile):
                v = int(idx[i * tile + r])
                if v >= 0 and v // tile == b:
                    sel[i, m, r, v % tile] = 1.0

    def body(tab_ref, sel_ref, *refs):
        o_ref = refs[k_max]
        acc = jnp.dot(sel_ref[0, 0], refs[0][...], preferred_element_type=F32)
        for m in range(1, k_max):
            acc = acc + jnp.dot(sel_ref[0, m], refs[m][...], preferred_element_type=F32)
        o_ref[...] = acc.astype(o_ref.dtype)

    def src_spec(m):
        return pl.BlockSpec((tile, width), lambda i, t: (t[i * k_max + m], 0))

    return pl.pallas_call(
        body, name=name,
        grid_spec=pltpu.PrefetchScalarGridSpec(
            num_scalar_prefetch=1, grid=(n_tiles,),
            in_specs=[pl.BlockSpec((1, k_max, tile, tile), lambda i, t: (i, 0, 0, 0))] + [src_spec(m) for m in range(k_max)],
            out_specs=pl.BlockSpec((tile, width), lambda i, t: (i, 0))),
        out_shape=jax.ShapeDtypeStruct((n_out, width), src.dtype),
        compiler_params=_cparams(("parallel",)),
    )(jnp.asarray(tab.reshape(-1)), jnp.asarray(sel, src.dtype), *([src] * k_max))


def _w_in_operand(win_g):
    return _take_rows(win_g.reshape(N_DEV * W_IN_ROWS, PACK_W), _w_in_row_maps()[0], name="w_in_rows")


def _mid_operands(wout_g, woa_g, wob_g, small_g):
    def full(n, off, r):
        a = small_g[:, off:off + r].reshape((N_DEV,) + _shard_shape(n))
        return jnp.moveaxis(a, 0, 1).reshape(FULL_SHAPE[n])

    w = {n: full(n, off, r) for n, off, r in SMALL_FLAT}
    ukv = w["w_ukv"].reshape(KV_LORA, MLA_HEADS, MLA_NOPE + MLA_V)
    return dict(
        wout=wout_g.reshape(D_MODEL, D_MODEL), woa_t=woa_g.reshape(D_MODEL, -1), wob_t=wob_g.reshape(D_MODEL, -1),
        wuq=_pad_heads(w["w_uq"], MLA_HEADS, MLA_QK, 1),
        wuk=_pad_heads(ukv[:, :, :MLA_NOPE].reshape(KV_LORA, -1), MLA_HEADS, MLA_NOPE, 1),
        wuv=_pad_heads(ukv[:, :, MLA_NOPE:].reshape(KV_LORA, -1), MLA_HEADS, MLA_V, 1),
    )


def _mid_grad_pack(g):
    uk = _unpad_heads(g["wukv"][:, :1024], MLA_HEADS, MLA_NOPE, 1).reshape(KV_LORA, MLA_HEADS, MLA_NOPE)
    uv = _unpad_heads(g["wukv"][:, 1024:], MLA_HEADS, MLA_V, 1).reshape(KV_LORA, MLA_HEADS, MLA_V)
    w = dict(w_uq=_unpad_heads(g["wuq"], MLA_HEADS, MLA_QK, 1), w_ukv=jnp.concatenate([uk, uv], 2).reshape(KV_LORA, -1))
    rows = []
    for n, _, r in SMALL_FLAT:
        rr, cc = FULL_SHAPE[n]
        a = jnp.moveaxis(w[n].reshape(rr, N_DEV, cc // N_DEV), 1, 0).reshape(N_DEV, r, PACK_W)
        rows.append(jnp.pad(a, ((0, 0), (0, -r % ROW_TILE), (0, 0))).astype(WIRE_DTYPE))
    rows.append(jnp.zeros((N_DEV, OUT_ROWS - SMALL_USED, PACK_W), WIRE_DTYPE))
    blk = lambda a: a.reshape(N_DEV, OUT_ROWS, PACK_W)
    return [blk(g["wout"]), blk(g["woa_t"]), blk(g["wob_t"]), jnp.concatenate(rows, 1)]


def _w_in_grad_chunks(g_win_t):
    return _take_rows(g_win_t, _w_in_row_maps()[1], name="dw_in_rows").reshape(N_DEV, W_IN_ROWS, PACK_W)


def _local_step(x, tgt, win_t, small, weights, grads):
    s_ = x.shape[0]
    tabs = _rope_tables(s_)
    sink_b = jnp.broadcast_to(small["swa_sinks"].reshape(SWA_KV_HEADS, SWA_GROUP, 1), (SWA_KV_HEADS, SWA_GROUP, LANES))
    sink_b = jnp.pad(sink_b, ((0, 0), (0, SUBLANES - SWA_GROUP), (0, 0)))

    h, qa, ka, va, cq, ckv, kro, p = _proj_in(x, small["mix_norm_g"], win_t, small["q_norm_g"], small["kv_norm_g"], tabs)
    oa32, oa16, lse_a = _swa_fwd(qa, ka, va, sink_b)
    ops = weights.mid(oa16)
    qc, kc, vp = _mla_up(cq, ckv, kro, ops["wuq"], ops["wuk"], ops["wuv"], tabs)
    ob32, ob16, lse_b = _mla_fwd(qc, kc, vp)
    ta, tb, y = _attn_out_gate(oa16, ob16, ops["woa_t"], ops["wob_t"], p)
    x1 = _mm(y, ops["wout"], "nn", name="out_proj", add=x, tm=1024, tn=1024)
    wgu_t, wd = weights.late(x1)
    h2, gu, act = _ffn_in_act(x1, small["ffn_norm_g"], wgu_t)

    dx2, dx2b, dg3, _, tot = _ffn_out_loss(act, wd, x1, small["final_norm_g"].reshape(1, D_MODEL), tgt)
    g = {}
    g_wd = _mm(act, dx2b, "tn", name="dw_down", tm=FF_TILE, tn=1024, tk=2048, out_dtype=WIRE_DTYPE)
    dgu = _d_act_swiglu(dx2b, wd, gu)
    g_wgu = _mm(dgu, h2, "tn", name="dw_ffn_in", tm=FF_TILE, tn=1024, tk=2048, out_dtype=WIRE_DTYPE)
    token = grads.late(g_wgu, g_wd)
    dx1, dx1b, dg2 = _mm_norm_bwd(dgu, wgu_t, x1, small["ffn_norm_g"] + token[0:1, 0:1], dx2, name="d_h2")
    g["wout"] = _mm(y, dx1b, "tn", name="dw_out", tm=1024, tn=1024, tk=2048, out_dtype=WIRE_DTYPE)
    dta, dtb, dgab = _d_y_gate(dx1b, ops["wout"], p, ta, tb)
    doa = _mm(dta, ops["woa_t"], "nn", name="d_oa", tm=1024, tn=1024)
    g["woa_t"] = _mm(dta, oa16, "tn", name="dw_o_swa", tm=1024, tn=1024, tk=2048, out_dtype=WIRE_DTYPE)
    g["wob_t"] = _mm(dtb, ob16, "tn", name="dw_o_mla", tm=1024, tn=1024, tk=2048, out_dtype=WIRE_DTYPE)
    dob16, delta_b = _mla_d_out(dtb, ops["wob_t"], ob32)
    dqc, dkc, dvp = _mla_bwd(qc, kc, vp, dob16, lse_b, delta_b)
    dqp, dkv, dkr, dqlat, dkvlat, dgq, dgkv = _mla_up_bwd(
        dqc, dkc, dvp, ops["wuq"], jnp.concatenate([ops["wuk"], ops["wuv"]], 1), p, small["q_norm_g"], small["kv_norm_g"], tabs)
    g["wuq"] = _mm(cq, dqp, "tn", name="dw_uq", tm=Q_LORA, tn=1024, tk=2048)
    g["wukv"] = _mm(ckv, dkv, "tn", name="dw_ukv", tm=KV_LORA, tn=2048, tk=2048)
    token = grads.mid(g)
    dqa, dka, dva, dsk = _swa_bwd(qa, ka, va, sink_b + token[0:1, 0:1], oa32, doa, lse_a)
    dp = _assemble_dp(dgab, dqa, dqlat, dkr, dka, dva, dkvlat, tabs)
    token = grads.last(_mm(dp, h, "tn", name="dw_in", tm=2176, tn=1024, tk=1024, out_dtype=WIRE_DTYPE))
    gx, _, dg1 = _mm_norm_bwd(dp, win_t, x, small["mix_norm_g"], dx1, name="d_h", after=token)

    sm = dict(mix_norm_g=dg1, ffn_norm_g=dg2, final_norm_g=dg3, q_norm_g=dgq, kv_norm_g=dgkv,
              swa_sinks=dsk[:, :SWA_GROUP, 0].reshape(1, SWA_HEADS))
    return tot, gx, sm


MESH = pl.DeviceIdType.MESH
ANY = pl.BlockSpec(memory_space=pl.ANY)


def _position():
    return lax.axis_index("x"), lax.axis_index("y"), lax.axis_index("c")


def _all_gather(block, pieces, shapes, *, name):
    n_out = len(shapes)
    n_rows = sum(p[3] for p in pieces)

    def body(x_ref, *refs):
        outs, (send_sems, recv_sems, local_sem) = refs[:n_out], refs[n_out:]
        x, y, c = _position()
        me, sibling = (x, y, c), (x, y, 1 - c)
        chips = [(1 - x, y), (x, 1 - y), (1 - x, 1 - y)]

        def dst(piece, blk):
            arr, lead, _, _ = piece
            return outs[arr].at[lead(4 * blk[0] + 2 * blk[1] + blk[2])]

        def own(piece):
            return x_ref.at[pl.ds(piece[2], piece[3])]

        def copies(k, blk, to, from_input):
            return [pltpu.make_async_remote_copy(
                src_ref=own(p) if from_input else dst(p, blk), dst_ref=dst(p, blk), send_sem=send_sems.at[k],
                recv_sem=recv_sems.at[k], device_id=to, device_id_type=MESH) for p in pieces]

        gathered_rows = x_ref.at[pl.ds(0, n_rows)]

        def whole_block(k):
            return pltpu.make_async_remote_copy(src_ref=gathered_rows, dst_ref=gathered_rows, send_sem=send_sems.at[k],
                                                recv_sem=recv_sems.at[k], device_id=me, device_id_type=MESH)

        for p in pieces:
            pltpu.make_async_copy(own(p), dst(p, me), local_sem).start()
        for cp in copies(0, me, sibling, True):
            cp.start()
        for j, chip in enumerate(chips):
            for cp in copies(1 + j, me, (*chip, c), True):
                cp.start()
        for j, chip in enumerate(chips):
            whole_block(1 + j).wait_recv()
            for cp in copies(4 + j, (*chip, c), sibling, False):
                cp.start()
        whole_block(0).wait_recv()
        for j in range(3):
            whole_block(4 + j).wait_recv()
        for k in range(7):
            whole_block(k).wait_send()
        pltpu.make_async_copy(gathered_rows, gathered_rows, local_sem).wait()

    return pl.pallas_call(
        body, name=name, out_shape=[jax.ShapeDtypeStruct(s, block.dtype) for s in shapes], in_specs=[ANY],
        out_specs=[ANY] * n_out,
        scratch_shapes=[pltpu.SemaphoreType.DMA((7,)), pltpu.SemaphoreType.DMA((7,)), pltpu.SemaphoreType.DMA],
    )(block)


HBM = pl.BlockSpec(memory_space=pltpu.HBM)
SEM = pl.BlockSpec(memory_space=pltpu.SEMAPHORE)
TILE_DEVS = FF_TILE // FF_COLS
GU_SHAPE = (2, 2, TILE_DEVS, FF_COLS, PACK_W)


def _gate_slab(d):
    return (d // TILE_DEVS, 0, d % TILE_DEVS)


def _up_slab(d):
    return (d // TILE_DEVS, 1, d % TILE_DEVS)
D_SHAPE = (N_DEV, FF_COLS, PACK_W)
LAND_SHAPE = (N_DEV, LATE_ROWS, PACK_W)


def _split_params():
    return pltpu.CompilerParams(has_side_effects=pltpu.SideEffectType.DATAFLOW_SIDE_EFFECTING)


def _peer(x, y, c, k):
    return ((1 - x) if k & 4 else x, (1 - y) if k & 2 else y, (1 - c) if k & 1 else c)


def _empty_hbm(shape, dtype):
    return pltpu.with_memory_space_constraint(lax.empty(shape, dtype), pltpu.HBM)


def _wait_all(rows, send_sems, recv_sems, me):
    for k in range(N_DEV - 1):
        cp = pltpu.make_async_remote_copy(src_ref=rows, dst_ref=rows, send_sem=send_sems.at[k], recv_sem=recv_sems.at[k],
                                          device_id=me, device_id_type=MESH)
        cp.wait_send()
        cp.wait_recv()


def _token_shape():
    return jax.ShapeDtypeStruct((SUBLANES, LANES), F32)


def _gather_start(pack, row0, pieces, shapes, *, name):
    n = len(shapes)

    def body(*refs):
        p_ref, bufs, send_sems, recv_sems, token = refs[0], refs[1:1 + n], refs[1 + n], refs[2 + n], refs[-1]
        x, y, c = _position()
        me = 4 * x + 2 * y + c
        for k in range(1, N_DEV):
            off = row0
            for buf, lead, rows in pieces:
                pltpu.make_async_remote_copy(
                    src_ref=p_ref.at[pl.ds(off, rows)], dst_ref=bufs[buf].at[lead(me)], send_sem=send_sems.at[k - 1],
                    recv_sem=recv_sems.at[k - 1], device_id=_peer(x, y, c, k), device_id_type=MESH).start()
                off += rows
        token[...] = jnp.zeros_like(token)

    sems, dt = pltpu.SemaphoreType.DMA((N_DEV - 1,)), pack.dtype
    return pl.pallas_call(
        body, name=name,
        out_shape=(sems, sems, pltpu.HBM(pack.shape, dt)) + tuple(pltpu.HBM(s, dt) for s in shapes) + (_token_shape(),),
        in_specs=(HBM,) * (1 + n), out_specs=(SEM, SEM) + (HBM,) * (1 + n) + (pl.BlockSpec(memory_space=pltpu.VMEM),),
        input_output_aliases={i: 2 + i for i in range(1 + n)}, compiler_params=_split_params(),
    )(pltpu.with_memory_space_constraint(pack, pltpu.HBM), *[_empty_hbm(s, dt) for s in shapes])


def _gather_wait(started, row0, n_rows, after, *, name):
    send_sems, recv_sems, pack, *bufs = started[:-1]
    n = len(bufs)

    def body(*refs):
        _wait_all(refs[0].at[pl.ds(row0, n_rows)], refs[1 + n], refs[2 + n], _position())

    outs = pl.pallas_call(
        body, name=name, out_shape=tuple(pltpu.HBM(a.shape, a.dtype) for a in (pack, *bufs)),
        in_specs=(HBM,) * (1 + n) + (SEM, SEM, ANY), out_specs=(HBM,) * (1 + n),
        input_output_aliases={i: i for i in range(1 + n)}, compiler_params=_split_params(),
    )(pack, *bufs, send_sems, recv_sems, after)
    return outs[0], outs[1:]


def _scatter_start(srcs, pieces, *, name):
    n = len(srcs)
    land_shape = (N_DEV, sum(p[2] for p in pieces), PACK_W)

    def body(*refs):
        src_refs, land_ref, send_sems, recv_sems, token = refs[:n], refs[n], refs[n + 1], refs[n + 2], refs[-1]
        x, y, c = _position()
        me = 4 * x + 2 * y + c
        for k in range(1, N_DEV):
            px, py, pc = _peer(x, y, c, k)
            off = 0
            for si, lead, rows in pieces:
                pltpu.make_async_remote_copy(
                    src_ref=src_refs[si].at[lead(4 * px + 2 * py + pc)], dst_ref=land_ref.at[me, pl.ds(off, rows)],
                    send_sem=send_sems.at[k - 1], recv_sem=recv_sems.at[k - 1], device_id=(px, py, pc),
                    device_id_type=MESH).start()
                off += rows
        token[...] = jnp.zeros_like(token)

    sems, dt = pltpu.SemaphoreType.DMA((N_DEV - 1,)), srcs[0].dtype
    return pl.pallas_call(
        body, name=name,
        out_shape=(sems, sems) + tuple(pltpu.HBM(a.shape, dt) for a in srcs) + (pltpu.HBM(land_shape, dt), _token_shape()),
        in_specs=(HBM,) * (n + 1), out_specs=(SEM, SEM) + (HBM,) * (n + 1) + (pl.BlockSpec(memory_space=pltpu.VMEM),),
        input_output_aliases={i: 2 + i for i in range(n + 1)}, compiler_params=_split_params(),
    )(*[pltpu.with_memory_space_constraint(a, pltpu.HBM) for a in srcs], _empty_hbm(land_shape, dt))


def _scatter_wait(started, after, *, name):
    send_sems, recv_sems, *bufs = started[:-1]
    n = len(bufs)

    def body(*refs):
        _wait_all(refs[n - 1].at[0], refs[n], refs[n + 1], _position())

    return pl.pallas_call(
        body, name=name, out_shape=tuple(pltpu.HBM(a.shape, a.dtype) for a in bufs),
        in_specs=(HBM,) * n + (SEM, SEM, ANY), out_specs=(HBM,) * n, input_output_aliases={i: i for i in range(n)},
        compiler_params=_split_params(),
    )(*bufs, send_sems, recv_sems, after)


def _peer_sum(own, own_lead, land, block, rows, idx, *, name):
    owns = list(own) if isinstance(own, (list, tuple)) else [own]
    n, lead_rank = len(owns), owns[0].ndim - 2

    def body(idx_ref, *refs):
        own_refs, land_refs, o_ref = refs[:n], refs[n:n + N_DEV - 1], refs[n + N_DEV - 1]
        for j in range(n):
            rs_ = slice(j * rows, (j + 1) * rows)
            acc = own_refs[j][(0,) * lead_rank].astype(F32)
            for k in range(N_DEV - 1):
                acc = acc + land_refs[k][0, rs_].astype(F32)
            o_ref[rs_] = acc

    own_spec = pl.BlockSpec((1,) * lead_rank + (rows, PACK_W), lambda i, t: own_lead(t[0]) + (0, 0))

    def land_spec(k):
        return pl.BlockSpec((1, n * rows, PACK_W), lambda i, t: (t[k + 1], block, 0))

    return pl.pallas_call(
        body, name=name,
        grid_spec=pltpu.PrefetchScalarGridSpec(
            num_scalar_prefetch=1, grid=(1,), in_specs=[own_spec] * n + [land_spec(k) for k in range(N_DEV - 1)],
            out_specs=pl.BlockSpec((n * rows, PACK_W), lambda i, t: (0, 0))),
        out_shape=jax.ShapeDtypeStruct((n * rows, PACK_W), F32), compiler_params=_cparams(("arbitrary",)),
    )(idx, *owns, *([land] * (N_DEV - 1)))


def _adamw(w, g, m, v):
    m = ADAM_B1 * m + (1.0 - ADAM_B1) * g
    v = ADAM_B2 * v + (1.0 - ADAM_B2) * (g * g)
    m_hat = m / (1.0 - ADAM_B1 ** ADAM_STEP)
    v_hat = v / (1.0 - ADAM_B2 ** ADAM_STEP)
    delta = -ADAM_LR * (m_hat / (jnp.sqrt(v_hat) + ADAM_EPS) + ADAM_WD * w)
    return delta, m, v


def _adamw_call(w, g, m, v, *, name, max_rows=256):
    _, r, c_ = w.shape
    tr = max_rows if r > max_rows and r % max_rows == 0 else r

    def body(w_ref, g_ref, m_ref, v_ref, d_ref, mo_ref, vo_ref):
        d, mn, vn = _adamw(w_ref[0], g_ref[...], m_ref[0], v_ref[0])
        d_ref[0] = d
        mo_ref[0] = mn
        vo_ref[0] = vn

    row3 = pl.BlockSpec((1, tr, c_), lambda i: (0, i, 0))
    shp = jax.ShapeDtypeStruct((1, r, c_), F32)
    return pl.pallas_call(
        body, name=name, grid=(r // tr,), in_specs=[row3, pl.BlockSpec((tr, c_), lambda i: (i, 0)), row3, row3],
        out_specs=[row3] * 3, out_shape=[shp] * 3, compiler_params=_cparams(("parallel",)),
    )(w, g, m, v)


SMALL = ("mix_norm_g", "ffn_norm_g", "final_norm_g", "q_norm_g", "kv_norm_g", "swa_sinks")
SMALL_W = dict(mix_norm_g=1024, ffn_norm_g=1024, final_norm_g=1024, q_norm_g=Q_LORA, kv_norm_g=KV_LORA, swa_sinks=SWA_HEADS)


def _small_adamw(parts, w, m, v):
    ns = len(SMALL)

    def body(p_ref, *refs):
        ins, outs = refs[:3 * ns], refs[3 * ns:]
        tot = p_ref[0]
        for dev in range(1, N_DEV):
            tot = tot + p_ref[dev]
        for k, n in enumerate(SMALL):
            g = jnp.sum(tot[k * SUBLANES:(k + 1) * SUBLANES, :SMALL_W[n]], axis=0, keepdims=True)
            res = _adamw(ins[k][...], g, ins[ns + k][...], ins[2 * ns + k][...])
            for j, r in enumerate((g,) + tuple(res)):
                outs[j * ns + k][...] = r
        outs[4 * ns][...] = jnp.sum(tot[ns * SUBLANES:(ns + 1) * SUBLANES, 0:1], axis=0, keepdims=True)

    shapes = [jax.ShapeDtypeStruct((1, SMALL_W[n]), F32) for n in SMALL]
    vm = pl.BlockSpec(memory_space=pltpu.VMEM)
    out = pl.pallas_call(
        body, name="small_adamw", in_specs=[vm] * (1 + 3 * ns), out_specs=[vm] * (4 * ns + 1),
        out_shape=shapes * 4 + [jax.ShapeDtypeStruct((1, 1), F32)],
    )(parts, *[d[n] for d in (w, m, v) for n in SMALL])
    return [dict(zip(SMALL, out[j * ns:(j + 1) * ns])) for j in range(4)] + [out[4 * ns]]


def _small_pack(d, rows_each):
    parts = [jnp.pad(d[n].astype(F32), ((0, 0), (0, PACK_W - SMALL_W[n]))) for n in SMALL]
    out = jnp.concatenate(parts, 0)
    pad = -out.shape[0] % SUBLANES
    return jnp.pad(out, ((0, pad), (0, 0)))


def kernel(x, mix_norm_g, w_in, swa_sinks, q_norm_g, w_uq, kv_norm_g, w_ukv, w_o_swa, w_o_mla, w_out, ffn_norm_g, w_gate, w_up, w_down, final_norm_g, loss_target, m_mix_norm_g, m_w_in, m_swa_sinks, m_q_norm_g, m_w_uq, m_kv_norm_g, m_w_ukv, m_w_o_swa, m_w_o_mla, m_w_out, m_ffn_norm_g, m_w_gate, m_w_up, m_w_down, m_final_norm_g, v_mix_norm_g, v_w_in, v_swa_sinks, v_q_norm_g, v_w_uq, v_kv_norm_g, v_w_ukv, v_w_o_swa, v_w_o_mla, v_w_out, v_ffn_norm_g, v_w_gate, v_w_up, v_w_down, v_final_norm_g):
    big_w = dict(w_in=w_in[0], w_uq=w_uq[0], w_ukv=w_ukv[0], w_o_swa=w_o_swa[0], w_o_mla=w_o_mla[0], w_out=w_out[0],
                 w_gate=w_gate[0], w_up=w_up[0], w_down=w_down[0])
    big_w3 = dict(w_in=w_in, w_uq=w_uq, w_ukv=w_ukv, w_o_swa=w_o_swa, w_o_mla=w_o_mla, w_out=w_out, w_gate=w_gate, w_up=w_up,
                  w_down=w_down)
    big_m = dict(w_in=m_w_in, w_uq=m_w_uq, w_ukv=m_w_ukv, w_o_swa=m_w_o_swa, w_o_mla=m_w_o_mla, w_out=m_w_out,
                 w_gate=m_w_gate, w_up=m_w_up, w_down=m_w_down)
    big_v = dict(w_in=v_w_in, w_uq=v_w_uq, w_ukv=v_w_ukv, w_o_swa=v_w_o_swa, w_o_mla=v_w_o_mla, w_out=v_w_out,
                 w_gate=v_w_gate, w_up=v_w_up, w_down=v_w_down)
    small_w = dict(mix_norm_g=mix_norm_g, ffn_norm_g=ffn_norm_g, final_norm_g=final_norm_g.reshape(1, D_MODEL),
                   q_norm_g=q_norm_g, kv_norm_g=kv_norm_g, swa_sinks=swa_sinks)
    small_m = dict(mix_norm_g=m_mix_norm_g, ffn_norm_g=m_ffn_norm_g, final_norm_g=m_final_norm_g.reshape(1, D_MODEL),
                   q_norm_g=m_q_norm_g, kv_norm_g=m_kv_norm_g, swa_sinks=m_swa_sinks)
    small_v = dict(mix_norm_g=v_mix_norm_g, ffn_norm_g=v_ffn_norm_g, final_norm_g=v_final_norm_g.reshape(1, D_MODEL),
                   q_norm_g=v_q_norm_g, kv_norm_g=v_kv_norm_g, swa_sinks=v_swa_sinks)

    px, py, pc = _position()
    me = 4 * px + 2 * py + pc
    idx = jnp.stack([me] + [4 * qx + 2 * qy + qc for qx, qy, qc in (_peer(px, py, pc, k) for k in range(1, N_DEV))])
    idx = idx.astype(jnp.int32)

    dev = lambda d: (d,)
    pack = _wire_pack(big_w, WIRE_DTYPE)
    win_g, = _all_gather(pack, ((0, dev, 0, W_IN_ROWS),), ((N_DEV, W_IN_ROWS, PACK_W),), name="ag_early")
    mid_pieces = tuple((b, dev, OUT_ROWS) for b in range(MID_BLOCKS))
    ag_mid = _gather_start(pack, W_IN_ROWS, mid_pieces, ((N_DEV, OUT_ROWS, PACK_W),) * MID_BLOCKS, name="ag_mid_start")
    ag = {}

    def own_rows(r0, r1, shape):
        return pack[r0:r1].reshape(shape)

    def mid_weights(after):
        pack_mid, blocks = _gather_wait(ag_mid, W_IN_ROWS, MID_ROWS, after, name="ag_mid_wait")
        ag["late"] = _gather_start(pack_mid, EARLY_ROWS, ((0, _gate_slab, FF_COLS), (0, _up_slab, FF_COLS), (1, dev, FF_COLS)),
                                   (GU_SHAPE, D_SHAPE), name="ag_late_start")
        row0 = lambda b: W_IN_ROWS + b * OUT_ROWS
        ops = _mid_operands(*[lax.dynamic_update_slice(blk, own_rows(row0(b), row0(b + 1), (1, OUT_ROWS, PACK_W)), (me, 0, 0))
                              for b, blk in enumerate(blocks)])
        ops["wuq"] = ops["wuq"] + ag["late"][-1][0:1, 0:1].astype(ops["wuq"].dtype)
        return ops

    def late_weights(after):
        _, (gu, d) = _gather_wait(ag["late"], EARLY_ROWS, LATE_ROWS, after, name="ag_late_wait")
        slab = (1, 1, 1, FF_COLS, PACK_W)
        gu = lax.dynamic_update_slice(gu, own_rows(EARLY_ROWS, EARLY_ROWS + FF_COLS, slab), _gate_slab(me) + (0, 0))
        gu = lax.dynamic_update_slice(gu, own_rows(EARLY_ROWS + FF_COLS, EARLY_ROWS + 2 * FF_COLS, slab), _up_slab(me) + (0, 0))
        d = lax.dynamic_update_slice(d, own_rows(EARLY_ROWS + 2 * FF_COLS, PACK_ROWS, (1, FF_COLS, PACK_W)), (me, 0, 0))
        return gu.reshape(2 * D_FF, D_MODEL), d.reshape(D_FF, D_MODEL)

    rs = {}

    def late_grads(g_gu, g_d):
        rs["late"] = _scatter_start([g_gu.reshape(GU_SHAPE), g_d.reshape(D_SHAPE)],
                                    ((0, _gate_slab, FF_COLS), (0, _up_slab, FF_COLS), (1, dev, FF_COLS)),
                                    name="rs_late_start")
        return rs["late"][-1]

    def mid_grads(g):
        rs["mid"] = _scatter_start(_mid_grad_pack(g), mid_pieces, name="rs_mid_start")
        return rs["mid"][-1]

    def last_grads(g_win_t):
        rs["last"] = _scatter_start([_w_in_grad_chunks(g_win_t)], ((0, dev, W_IN_ROWS),), name="rs_last_start")
        return rs["last"][-1]

    first_w = dict(small_w, mix_norm_g=mix_norm_g + ag_mid[-1][0:1, 0:1])
    loss_tot, gx, g_small = _local_step(
        x[0], loss_target[0], _w_in_operand(win_g), first_w, types.SimpleNamespace(mid=mid_weights, late=late_weights),
        types.SimpleNamespace(late=late_grads, mid=mid_grads, last=last_grads))

    loss_rows = jnp.pad(loss_tot[0:1, 0:1], ((0, SUBLANES - 1), (0, PACK_W - 1)))
    small_rows = jnp.concatenate([_small_pack(g_small_rows(g_small), SUBLANES), loss_rows], 0)
    n_small = small_rows.shape[0]
    ag_small = _gather_start(small_rows, 0, ((0, dev, n_small),), ((N_DEV, n_small, PACK_W),), name="ag_small_start")

    g_gu, g_d, land_late = _scatter_wait(rs["late"], ag_small[-1], name="rs_late_wait")
    *g_mid, land_mid = _scatter_wait(rs["mid"], ag_small[-1], name="rs_mid_wait")
    g_win, land_last = _scatter_wait(rs["last"], ag_small[-1], name="rs_last_wait")
    gw_t = dict(w_gate=_peer_sum(g_gu, _gate_slab, land_late, 0, FF_COLS, idx, name="rs_sum_gate"),
                w_up=_peer_sum(g_gu, _up_slab, land_late, 1, FF_COLS, idx, name="rs_sum_up"),
                w_in=_peer_sum(g_win, dev, land_last, 0, W_IN_ROWS, idx, name="rs_sum_in")[0:W_IN_COLS])
    gw = dict(w_down=_peer_sum(g_d, dev, land_late, 2, FF_COLS, idx, name="rs_sum_down"))
    gw.update(_mid_unpack(_peer_sum(g_mid, dev, land_mid, 0, OUT_ROWS, idx, name="rs_sum_mid")))
    dw, mw, vw = {}, {}, {}
    swap = lambda a: jnp.swapaxes(a, 1, 2)
    for n in BIG:
        if n in gw_t:
            res = _adamw_call(swap(big_w3[n]), gw_t[n], swap(big_m[n]), swap(big_v[n]), name="adamw_" + n)
            dw[n], mw[n], vw[n] = (swap(r) for r in res)
        else:
            dw[n], mw[n], vw[n] = _adamw_call(big_w3[n], gw[n], big_m[n], big_v[n], name="adamw_" + n)
    gw = {n: g[None] for n, g in gw.items()}
    gw.update({n: swap(g[None]) for n, g in gw_t.items()})

    own_small, (parts,) = _gather_wait(ag_small, 0, n_small, vw[BIG[-1]], name="ag_small_wait")
    parts = lax.dynamic_update_slice(parts, own_small[None], (me, 0, 0))
    gs, ds, ms, vs, loss = _small_adamw(parts, small_w, small_m, small_v)
    loss = loss[0, 0]
    for d in (gs, ds, ms, vs):
        d["final_norm_g"] = d["final_norm_g"].reshape(D_MODEL)

    order = ("mix_norm_g", "w_in", "swa_sinks", "q_norm_g", "w_uq", "kv_norm_g", "w_ukv", "w_o_swa", "w_o_mla", "w_out",
             "ffn_norm_g", "w_gate", "w_up", "w_down", "final_norm_g")

    def leaves(big, small):
        return [big[n] if n in big else small[n] for n in order]

    return (loss, gx[None], *leaves(gw, gs), *leaves(dw, ds), *leaves(mw, ms), *leaves(vw, vs))


def g_small_rows(g_small):
    out = dict(g_small)
    out["swa_sinks"] = jnp.pad(g_small["swa_sinks"], ((0, SUBLANES - 1), (0, 0)))
    return out
```

```python
import types

import numpy as np
import jax
import jax.numpy as jnp
from jax import lax
from jax.experimental import pallas as pl
from jax.experimental.pallas import tpu as pltpu

F32 = jnp.float32
MXU_DTYPE = jnp.bfloat16
WIRE_DTYPE = jnp.bfloat16

D_MODEL = 1024
EPS = 1e-6
ROPE_THETA = 10000.0
BLOCK = 128
HEAD_DIM = 64
SWA_HEADS = 8
SWA_KV_HEADS = 2
SWA_GROUP = SWA_HEADS // SWA_KV_HEADS
MLA_HEADS = 8
MLA_NOPE = 64
MLA_ROPE = 32
MLA_V = 64
MLA_QK = MLA_NOPE + MLA_ROPE
Q_LORA = 384
KV_LORA = 256
D_FF = 2816
IN_SIZES = (512, 128, 128, Q_LORA, KV_LORA, MLA_ROPE, D_MODEL, D_MODEL)
IN_OFF = tuple(int(v) for v in np.cumsum((0,) + IN_SIZES))
ADAM_LR, ADAM_B1, ADAM_B2, ADAM_EPS, ADAM_WD, ADAM_STEP = 0.001, 0.9, 0.999, 1e-08, 0.01, 10

LANES = 128
SUBLANES = 8
VMEM_LIMIT = 48 * 1024 * 1024
N_DEV = 8

P_GA, P_GB, P_Q, P_QLAT, P_KR, P_K, P_V, P_KVLAT, P_W = 0, 1024, 2048, 3072, 3456, 3584, 3840, 4096, 4352
KR_LANE = 64

LOG2E = 1.4426950408889634

NT = (((1,), (1,)), ((), ()))
NN = (((1,), (0,)), ((), ()))
TN = (((0,), (0,)), ((), ()))


def _cparams(sem):
    return pltpu.CompilerParams(dimension_semantics=sem, vmem_limit_bytes=VMEM_LIMIT)


def _mm(a, b, mode, *, name, out_dtype=F32, add=None, tm=512, tn=512, tk=None):
    if mode == "nn":
        (M, K), (K2, N) = a.shape, b.shape
    elif mode == "nt":
        (M, K), (N, K2) = a.shape, b.shape
    else:
        (K, M), (K2, N) = a.shape, b.shape
    assert K == K2, (a.shape, b.shape, mode)
    tm, tn, tk = min(tm, M), min(tn, N), K if tk is None else min(tk, K)
    assert M % tm == 0 and N % tn == 0 and K % tk == 0, (M, N, K, tm, tn, tk)
    nk = K // tk
    dn = {"nn": NN, "nt": NT, "tn": TN}[mode]
    if mode == "tn":
        a_spec = pl.BlockSpec((tk, tm), lambda i, j, k: (k, i))
    else:
        a_spec = pl.BlockSpec((tm, tk), lambda i, j, k: (i, k))
    once = dict(pipeline_mode=pl.Buffered(1)) if (nk == 1 and tn == N) else {}
    if mode == "nt":
        b_spec = pl.BlockSpec((tn, tk), lambda i, j, k: (j, k), **once)
    else:
        b_spec = pl.BlockSpec((tk, tn), lambda i, j, k: (k, j), **once)
    o_spec = pl.BlockSpec((tm, tn), lambda i, j, k: (i, j))
    has_add = add is not None

    def body(*refs):
        a_ref, b_ref = refs[0], refs[1]
        add_ref = refs[2] if has_add else None
        o_ref = refs[2 + has_add]
        p = lax.dot_general(a_ref[...], b_ref[...], dn, preferred_element_type=F32)

        def finish(acc):
            if has_add:
                acc = acc + add_ref[...]
            o_ref[...] = acc.astype(o_ref.dtype)

        if nk == 1:
            finish(p)
        else:
            acc_ref = refs[-1]
            k = pl.program_id(2)

            @pl.when(k == 0)
            def _():
                acc_ref[...] = p

            @pl.when((k > 0) & (k < nk - 1))
            def _():
                acc_ref[...] += p

            @pl.when(k == nk - 1)
            def _():
                finish(acc_ref[...] + p)

    ins = [a, b] + ([add] if has_add else [])
    return pl.pallas_call(
        body, name=name, grid=(M // tm, N // tn, nk), in_specs=[a_spec, b_spec] + ([o_spec] if has_add else []), out_specs=o_spec,
        out_shape=jax.ShapeDtypeStruct((M, N), out_dtype),
        scratch_shapes=[pltpu.VMEM((tm, tn), F32)] if nk > 1 else [],
        compiler_params=_cparams(("parallel", "parallel", "arbitrary")),
    )(*ins)


def _rows(ts, w, cb=0):
    return pl.BlockSpec((ts, w), lambda i: (i, cb))


def _const(r, w):
    return pl.BlockSpec((r, w), lambda i: (0, 0))


def _sublane_sum(v):
    ts, c = v.shape
    return jnp.sum(v.reshape(ts // SUBLANES, SUBLANES, c), axis=0)


def _sigmoid(v):
    return 1.0 / (1.0 + jnp.exp(-v))


def _rope(v, cos, s_up, s_dn, up, dn):
    return v * cos + pltpu.roll(v, up, 1) * s_up + pltpu.roll(v, dn, 1) * s_dn


def _rope_t(dv, cos, s_up, s_dn, up, dn):
    return dv * cos + pltpu.roll(dv * s_up, dn, 1) + pltpu.roll(dv * s_dn, up, 1)


def _rope_tables(seq):
    pos = np.arange(seq, dtype=np.float32)[:, None]

    def base(dim):
        inv = np.float32(ROPE_THETA) ** (-np.arange(0, dim, 2, dtype=np.float32) / np.float32(dim))
        ang = (pos * inv.astype(np.float32)[None, :]).astype(np.float32)
        return np.cos(ang).astype(np.float32), np.sin(ang).astype(np.float32)

    z = lambda n: np.zeros((seq, n), np.float32)
    ca, sa = base(HEAD_DIM)
    a_cos = np.concatenate([ca, ca, z(64)], 1)
    a_up = np.concatenate([-sa, z(96)], 1)
    a_dn = np.concatenate([z(32), sa, z(64)], 1)
    cb, sb = base(MLA_ROPE)
    one = np.ones((seq, 64), np.float32)
    q_cos = np.concatenate([one, cb, cb, z(32)], 1)
    k_cos = np.concatenate([z(64), cb, cb, z(32)], 1)
    b_up = np.concatenate([z(64), -sb, z(48)], 1)
    b_dn = np.concatenate([z(80), sb, z(32)], 1)
    return tuple(jnp.asarray(t) for t in (a_cos, a_up, a_dn, q_cos, k_cos, b_up, b_dn))


def _rms(v, g):
    return v * lax.rsqrt(jnp.mean(v * v, axis=-1, keepdims=True) + EPS) * g


def _rms_bwd(v, g, d):
    r = lax.rsqrt(jnp.mean(v * v, axis=-1, keepdims=True) + EPS)
    xh = v * r
    dxh = d * g
    return r * (dxh - xh * jnp.mean(dxh * xh, axis=-1, keepdims=True)), d * xh


F_GA, F_GB, F_KVLAT, F_QLAT, F_W = 0, 1024, 2048, 2304, 2688


def _proj_in(x, g, w_t, gq, gkv, tabs, *, tm=512):
    s_, c = x.shape
    a_cos, a_up, a_dn, _, k_cos, b_up, b_dn = tabs

    def body(x_ref, g_ref, w_ref, gq_ref, gkv_ref, ac, au, ad, kc, bu, bd,
             h_ref, qa_ref, ka_ref, va_ref, cq_ref, ckv_ref, kro_ref, pf_ref):
        h = _rms(x_ref[...], g_ref[...]).astype(h_ref.dtype)
        h_ref[...] = h
        mm = lambda a, b: lax.dot_general(h, w_ref[a:b, :], NT, preferred_element_type=F32)
        pf_ref[:, F_GA:F_KVLAT] = mm(P_GA, P_Q)
        c_, u_, d_ = ac[...], au[...], ad[...]
        q = mm(P_Q, P_QLAT)
        for hd in range(SWA_HEADS):
            sl = slice(hd * LANES, (hd + 1) * LANES)
            qa_ref[:, sl] = _rope(q[:, sl], c_, u_, d_, 96, 32).astype(qa_ref.dtype)
        kv = mm(P_KR, P_KVLAT)
        kro_ref[...] = _rope(kv[:, :LANES], kc[...], bu[...], bd[...], 112, 16)
        for hd in range(SWA_KV_HEADS):
            sl = slice((1 + hd) * LANES, (2 + hd) * LANES)
            ka_ref[:, hd * LANES:(hd + 1) * LANES] = _rope(kv[:, sl], c_, u_, d_, 96, 32).astype(ka_ref.dtype)
        va_ref[...] = kv[:, P_V - P_KR:].astype(va_ref.dtype)
        for a, b, f0, gref, dst in ((P_QLAT, P_KR, F_QLAT, gq_ref, cq_ref), (P_KVLAT, P_W, F_KVLAT, gkv_ref, ckv_ref)):
            v = mm(a, b)
            pf_ref[:, f0:f0 + b - a] = v
            r = lax.rsqrt(jnp.mean(v * v, axis=-1, keepdims=True) + EPS)
            dst[...] = (v * r * gref[...]).astype(dst.dtype)

    tab = _rows(tm, LANES)
    widths = (c, SWA_HEADS * LANES, SWA_KV_HEADS * LANES, SWA_KV_HEADS * LANES, Q_LORA, KV_LORA)
    return pl.pallas_call(
        body, name="proj_in", grid=(s_ // tm,),
        in_specs=[_rows(tm, c), _const(1, c), pl.BlockSpec((P_W, c), lambda i: (0, 0), pipeline_mode=pl.Buffered(1)),
                  _const(1, Q_LORA), _const(1, KV_LORA), tab, tab, tab, tab, tab, tab],
        out_specs=[_rows(tm, w) for w in widths] + [tab, _rows(tm, F_W)],
        out_shape=[jax.ShapeDtypeStruct((s_, w), MXU_DTYPE) for w in widths]
        + [jax.ShapeDtypeStruct((s_, LANES), F32), jax.ShapeDtypeStruct((s_, F_W), F32)],
        compiler_params=_cparams(("parallel",)),
    )(x, g, w_t, gq, gkv, a_cos, a_up, a_dn, k_cos, b_up, b_dn)


def _mm_norm_bwd(a, b, x, g, res, *, name, after=None, tm=512):
    s_, kk = a.shape
    c = b.shape[1]
    has_after = after is not None

    def body(*refs):
        a_ref, b_ref, x_ref, g_ref, res_ref = refs[:5]
        dx_ref, dxb_ref, dg_ref = refs[5 + has_after:]
        d = jnp.dot(a_ref[...], b_ref[...], preferred_element_type=F32)
        dx, gg = _rms_bwd(x_ref[...], g_ref[...], d)
        dx = dx + res_ref[...]
        dx_ref[...] = dx
        dxb_ref[...] = dx.astype(dxb_ref.dtype)

        @pl.when(pl.program_id(0) == 0)
        def _():
            dg_ref[...] = jnp.zeros(dg_ref.shape, F32)

        dg_ref[...] += _sublane_sum(gg)

    row = _rows(tm, c)
    in_specs = [_rows(tm, kk), pl.BlockSpec((kk, c), lambda i: (0, 0), pipeline_mode=pl.Buffered(1)), row, _const(1, c), row]
    return pl.pallas_call(
        body, name=name, grid=(s_ // tm,), in_specs=in_specs + ([pl.BlockSpec(memory_space=pl.ANY)] if has_after else []),
        out_specs=[row, row, _const(SUBLANES, c)],
        out_shape=[jax.ShapeDtypeStruct((s_, c), F32), jax.ShapeDtypeStruct((s_, c), MXU_DTYPE),
                   jax.ShapeDtypeStruct((SUBLANES, c), F32)],
        compiler_params=_cparams(("arbitrary",)),
    )(*([a, b, x, g, res] + ([after] if has_after else [])))


def _mla_up(cq, ckv, kro, wuq, wuk, wuv, tabs, *, ts=512):
    s_ = cq.shape[0]
    _, _, _, q_cos, _, b_up, b_dn = tabs

    def body(cq_ref, ckv_ref, kr_ref, wq_ref, wk_ref, wv_ref, qc, bu, bd, qo_ref, ko_ref, vo_ref):
        c_, u_, d_ = qc[...], bu[...], bd[...]
        kr = kr_ref[...]
        ckv_ = ckv_ref[...]
        vo_ref[...] = jnp.dot(ckv_, wv_ref[...], preferred_element_type=F32).astype(vo_ref.dtype)
        q = jnp.dot(cq_ref[...], wq_ref[...], preferred_element_type=F32)
        k = jnp.dot(ckv_, wk_ref[...], preferred_element_type=F32)
        for h in range(MLA_HEADS):
            sl = slice(h * LANES, (h + 1) * LANES)
            qo_ref[:, sl] = _rope(q[:, sl], c_, u_, d_, 112, 16).astype(qo_ref.dtype)
            ko_ref[:, sl] = (k[:, sl] + kr).astype(ko_ref.dtype)

    tab, out = _rows(ts, LANES), _rows(ts, 1024)
    return pl.pallas_call(
        body, name="mla_up", grid=(s_ // ts,),
        in_specs=[_rows(ts, Q_LORA), _rows(ts, KV_LORA), tab, _const(Q_LORA, 1024), _const(KV_LORA, 1024),
                  _const(KV_LORA, 1024), tab, tab, tab],
        out_specs=[out, out, out], out_shape=[jax.ShapeDtypeStruct((s_, 1024), MXU_DTYPE)] * 3,
        compiler_params=_cparams(("parallel",)),
    )(cq, ckv, kro, wuq, wuk, wuv, q_cos, b_up, b_dn)


def _mla_up_bwd(dqc, dkc, dvp, wuq, wukv, p, gq, gkv, tabs, *, ts=256):
    s_ = dqc.shape[0]
    _, _, _, q_cos, k_cos, b_up, b_dn = tabs

    def body(dq_ref, dk_ref, dv_ref, wq_ref, wkv_ref, ql_ref, kvl_ref, gq_ref, gkv_ref, qc, kc, bu, bd,
             dqo_ref, dkvo_ref, dkr_ref, dql_ref, dkvl_ref, dgq_ref, dgkv_ref):
        c_, u_, d_ = qc[...], bu[...], bd[...]
        tot = jnp.zeros((ts, LANES), F32)
        for h in range(MLA_HEADS):
            sl = slice(h * LANES, (h + 1) * LANES)
            dqo_ref[:, sl] = _rope_t(dq_ref[:, sl], c_, u_, d_, 112, 16).astype(dqo_ref.dtype)
            dk = dk_ref[:, sl]
            dkvo_ref[:, sl] = dk.astype(dkvo_ref.dtype)
            tot = tot + dk
        dkvo_ref[:, 1024:2048] = dv_ref[...].astype(dkvo_ref.dtype)
        dkr_ref[...] = _rope_t(tot, kc[...], u_, d_, 112, 16).astype(dkr_ref.dtype)

        @pl.when(pl.program_id(0) == 0)
        def _():
            dgq_ref[...] = jnp.zeros(dgq_ref.shape, F32)
            dgkv_ref[...] = jnp.zeros(dgkv_ref.shape, F32)

        for do_ref, w_ref, x_ref, g_ref, dx_ref, dg_ref in ((dqo_ref, wq_ref, ql_ref, gq_ref, dql_ref, dgq_ref),
                                                            (dkvo_ref, wkv_ref, kvl_ref, gkv_ref, dkvl_ref, dgkv_ref)):
            d = lax.dot_general(do_ref[...], w_ref[...], NT, preferred_element_type=F32)
            dx, gg = _rms_bwd(x_ref[...], g_ref[...], d)
            dx_ref[...] = dx.astype(dx_ref.dtype)
            dg_ref[...] += _sublane_sum(gg)

    tab = _rows(ts, LANES)
    return pl.pallas_call(
        body, name="mla_up_bwd", grid=(s_ // ts,),
        in_specs=[_rows(ts, 1024), _rows(ts, 1024), _rows(ts, 1024), _const(Q_LORA, 1024), _const(KV_LORA, 2048),
                  _rows(ts, Q_LORA, F_QLAT // Q_LORA), _rows(ts, KV_LORA, F_KVLAT // KV_LORA),
                  _const(1, Q_LORA), _const(1, KV_LORA), tab, tab, tab, tab],
        out_specs=[_rows(ts, 1024), _rows(ts, 2048), _rows(ts, LANES), _rows(ts, Q_LORA), _rows(ts, KV_LORA),
                   _const(SUBLANES, Q_LORA), _const(SUBLANES, KV_LORA)],
        out_shape=[jax.ShapeDtypeStruct((s_, 1024), MXU_DTYPE), jax.ShapeDtypeStruct((s_, 2048), MXU_DTYPE),
                   jax.ShapeDtypeStruct((s_, LANES), MXU_DTYPE), jax.ShapeDtypeStruct((s_, Q_LORA), MXU_DTYPE),
                   jax.ShapeDtypeStruct((s_, KV_LORA), MXU_DTYPE), jax.ShapeDtypeStruct((SUBLANES, Q_LORA), F32),
                   jax.ShapeDtypeStruct((SUBLANES, KV_LORA), F32)],
        compiler_params=_cparams(("arbitrary",)),
    )(dqc, dkc, dvp, wuq, wukv, p, p, gq, gkv, q_cos, k_cos, b_up, b_dn)


def _assemble_dp(dgab, dqa, dqlat, dkr, dka, dva, dkvlat, tabs, *, ts=256):
    s_ = dqa.shape[0]
    a_cos, a_up, a_dn = tabs[0], tabs[1], tabs[2]

    def body(dg_ref, dq_ref, dql_ref, dkr_ref, dk_ref, dv_ref, dkvl_ref, ac, au, ad, o_ref):
        c_, u_, d_ = ac[...], au[...], ad[...]
        o_ref[:, P_GA:P_Q] = dg_ref[...]
        for h in range(SWA_HEADS):
            sl = slice(h * LANES, (h + 1) * LANES)
            o_ref[:, P_Q + h * LANES:P_Q + (h + 1) * LANES] = _rope_t(dq_ref[:, sl], c_, u_, d_, 96, 32).astype(o_ref.dtype)
        o_ref[:, P_QLAT:P_KR] = dql_ref[...]
        o_ref[:, P_KR:P_K] = dkr_ref[...]
        for h in range(SWA_KV_HEADS):
            sl = slice(h * LANES, (h + 1) * LANES)
            o_ref[:, P_K + h * LANES:P_K + (h + 1) * LANES] = _rope_t(dk_ref[:, sl], c_, u_, d_, 96, 32).astype(o_ref.dtype)
        o_ref[:, P_V:P_KVLAT] = dv_ref[...]
        o_ref[:, P_KVLAT:P_W] = dkvl_ref[...]

    tab = _rows(ts, LANES)
    return pl.pallas_call(
        body, name="assemble_dp", grid=(s_ // ts,),
        in_specs=[_rows(ts, 2048), _rows(ts, 1024), _rows(ts, Q_LORA), _rows(ts, LANES), _rows(ts, 256), _rows(ts, 256),
                  _rows(ts, KV_LORA), tab, tab, tab],
        out_specs=_rows(ts, P_W), out_shape=jax.ShapeDtypeStruct((s_, P_W), MXU_DTYPE),
        compiler_params=_cparams(("parallel",)),
    )(dgab, dqa, dqlat, dkr, dka, dva, dkvlat, a_cos, a_up, a_dn)


def _attn_out_gate(oa, ob, woa_t, wob_t, p, *, ts=512):
    s_ = p.shape[0]

    def body(oa_ref, ob_ref, wa_ref, wb_ref, ga_ref, gb_ref, ta_ref, tb_ref, y_ref):
        ta = lax.dot_general(oa_ref[...], wa_ref[...], NT, preferred_element_type=F32)
        tb = lax.dot_general(ob_ref[...], wb_ref[...], NT, preferred_element_type=F32)
        ta_ref[...] = ta
        tb_ref[...] = tb
        y_ref[...] = (_sigmoid(ga_ref[...]) * ta + _sigmoid(gb_ref[...]) * tb).astype(y_ref.dtype)

    w = _const(1024, 1024)
    return pl.pallas_call(
        body, name="attn_out_gate", grid=(s_ // ts,),
        in_specs=[_rows(ts, 1024), _rows(ts, 1024), w, w, _rows(ts, 1024, F_GA // 1024), _rows(ts, 1024, F_GB // 1024)],
        out_specs=[_rows(ts, 1024)] * 3,
        out_shape=[jax.ShapeDtypeStruct((s_, 1024), F32)] * 2 + [jax.ShapeDtypeStruct((s_, 1024), MXU_DTYPE)],
        compiler_params=_cparams(("parallel",)),
    )(oa, ob, woa_t, wob_t, p, p)


def _d_y_gate(dx1b, wout, p, ta, tb, *, ts=512):
    s_ = p.shape[0]

    def body(dx_ref, w_ref, ga_ref, gb_ref, ta_ref, tb_ref, dta_ref, dtb_ref, dg_ref):
        d = lax.dot_general(dx_ref[...], w_ref[...], NT, preferred_element_type=F32)
        sa, sb = _sigmoid(ga_ref[...]), _sigmoid(gb_ref[...])
        dta_ref[...] = (d * sa).astype(dta_ref.dtype)
        dtb_ref[...] = (d * sb).astype(dtb_ref.dtype)
        dg_ref[:, 0:1024] = (d * ta_ref[...] * (sa * (1.0 - sa))).astype(dg_ref.dtype)
        dg_ref[:, 1024:2048] = (d * tb_ref[...] * (sb * (1.0 - sb))).astype(dg_ref.dtype)

    return pl.pallas_call(
        body, name="d_y_gate", grid=(s_ // ts,),
        in_specs=[_rows(ts, 1024), _const(1024, 1024), _rows(ts, 1024, F_GA // 1024), _rows(ts, 1024, F_GB // 1024),
                  _rows(ts, 1024), _rows(ts, 1024)],
        out_specs=[_rows(ts, 1024), _rows(ts, 1024), _rows(ts, 2048)],
        out_shape=[jax.ShapeDtypeStruct((s_, 1024), MXU_DTYPE)] * 2 + [jax.ShapeDtypeStruct((s_, 2048), MXU_DTYPE)],
        compiler_params=_cparams(("parallel",)),
    )(dx1b, wout, p, p, ta, tb)


FF_TILE = D_FF // 2


def _ffn_in_act(x1, g, wgu_t, *, tm=512):
    s_ = x1.shape[0]
    n = s_ // tm

    def body(x_ref, g_ref, w_ref, h_ref, gu_ref, a_ref):
        h = _rms(x_ref[...], g_ref[...]).astype(h_ref.dtype)
        h_ref[...] = h
        p = lax.dot_general(h, w_ref[...], NT, preferred_element_type=F32)
        gu_ref[...] = p
        gate = p[:, :FF_TILE]
        a_ref[...] = (gate * _sigmoid(gate) * p[:, FF_TILE:]).astype(a_ref.dtype)

    return pl.pallas_call(
        body, name="ffn_in", grid=(2, s_ // tm),
        in_specs=[pl.BlockSpec((tm, D_MODEL), lambda j, i: (i, 0)), pl.BlockSpec((1, D_MODEL), lambda j, i: (0, 0)),
                  pl.BlockSpec((2 * FF_TILE, D_MODEL), lambda j, i: (j, 0))],
        out_specs=[pl.BlockSpec((tm, D_MODEL), lambda j, i: (i + j * (n - 1 - i), 0)),
                   pl.BlockSpec((tm, 2 * FF_TILE), lambda j, i: (i, j)),
                   pl.BlockSpec((tm, FF_TILE), lambda j, i: (i, j))],
        out_shape=[jax.ShapeDtypeStruct((s_, D_MODEL), MXU_DTYPE), jax.ShapeDtypeStruct((s_, 2 * D_FF), F32),
                   jax.ShapeDtypeStruct((s_, D_FF), MXU_DTYPE)],
        compiler_params=_cparams(("arbitrary", "arbitrary")),
    )(x1, g, wgu_t)


def _d_act_swiglu(dx2b, wd, gu, *, tm=512):
    s_ = dx2b.shape[0]

    def body(d_ref, w_ref, gu_ref, o_ref):
        da = lax.dot_general(d_ref[...], w_ref[...], NT, preferred_element_type=F32)
        g, u = gu_ref[:, :FF_TILE], gu_ref[:, FF_TILE:]
        sg = _sigmoid(g)
        o_ref[:, :FF_TILE] = (da * u * (sg * (1.0 + g * (1.0 - sg)))).astype(o_ref.dtype)
        o_ref[:, FF_TILE:] = (da * (g * sg)).astype(o_ref.dtype)

    gu_spec = pl.BlockSpec((tm, 2 * FF_TILE), lambda j, i: (i, j))
    return pl.pallas_call(
        body, name="d_act", grid=(2, s_ // tm),
        in_specs=[pl.BlockSpec((tm, D_MODEL), lambda j, i: (i, 0)), pl.BlockSpec((FF_TILE, D_MODEL), lambda j, i: (j, 0)), gu_spec],
        out_specs=gu_spec, out_shape=jax.ShapeDtypeStruct((s_, 2 * D_FF), MXU_DTYPE),
        compiler_params=_cparams(("parallel", "parallel")),
    )(dx2b, wd, gu)


def _ffn_out_loss(act, wd, x1, g, tgt, *, ts=512):
    s_, c = x1.shape
    kk = act.shape[1]

    def body(a_ref, w_ref, x_ref, g_ref, t_ref, dx_ref, dxb_ref, dg_ref, lp_ref, tot_ref):
        v = x_ref[...] + jnp.dot(a_ref[...], w_ref[...], preferred_element_type=F32)
        r = lax.rsqrt(jnp.mean(v * v, axis=-1, keepdims=True) + EPS)
        xh = v * r
        gg = g_ref[...]
        e = xh * gg - t_ref[...]
        do = e * (1.0 / c)
        dxh = do * gg
        dx = r * (dxh - xh * jnp.mean(dxh * xh, axis=-1, keepdims=True))
        dx_ref[...] = dx
        dxb_ref[...] = dx.astype(dxb_ref.dtype)
        i = pl.program_id(0)

        @pl.when(i == 0)
        def _():
            dg_ref[...] = jnp.zeros(dg_ref.shape, F32)
            lp_ref[...] = jnp.zeros(lp_ref.shape, F32)

        dg_ref[...] += _sublane_sum(do * xh)
        lp_ref[...] += _sublane_sum(e * e)
        tot_ref[...] = jnp.full(tot_ref.shape, (0.5 / c) * jnp.sum(lp_ref[...]), F32)

    return pl.pallas_call(
        body, name="ffn_out_loss", grid=(s_ // ts,),
        in_specs=[_rows(ts, kk), _const(kk, c), _rows(ts, c), _const(1, c), _rows(ts, c)],
        out_specs=[_rows(ts, c), _rows(ts, c), _const(SUBLANES, c), _const(SUBLANES, c), _const(SUBLANES, LANES)],
        out_shape=[jax.ShapeDtypeStruct((s_, c), F32), jax.ShapeDtypeStruct((s_, c), MXU_DTYPE),
                   jax.ShapeDtypeStruct((SUBLANES, c), F32), jax.ShapeDtypeStruct((SUBLANES, c), F32),
                   jax.ShapeDtypeStruct((SUBLANES, LANES), F32)],
        compiler_params=_cparams(("arbitrary",)),
    )(act, wd, x1, g, tgt)


def _mla_d_out(dtb, wob_t, o32, *, ts=512):
    s_ = dtb.shape[0]

    def body(dt_ref, w_ref, o_ref, dob_ref, dl_ref):
        d = jnp.dot(dt_ref[...], w_ref[...], preferred_element_type=F32)
        dob_ref[...] = d.astype(dob_ref.dtype)
        prod = d * o_ref[...]
        for h in range(MLA_HEADS):
            dl_ref[h] = jnp.sum(prod[:, h * LANES:(h + 1) * LANES].T, axis=0, keepdims=True)

    return pl.pallas_call(
        body, name="mla_d_out", grid=(s_ // ts,), in_specs=[_rows(ts, 1024), _const(1024, 1024), _rows(ts, 1024)],
        out_specs=[_rows(ts, 1024), pl.BlockSpec((MLA_HEADS, 1, ts), lambda i: (0, 0, i))],
        out_shape=[jax.ShapeDtypeStruct((s_, 1024), MXU_DTYPE), jax.ShapeDtypeStruct((MLA_HEADS, 1, s_), F32)],
        compiler_params=_cparams(("parallel",)),
    )(dtb, wob_t, o32)


SWA_T = 4 * BLOCK


SWA_W = SWA_GROUP * BLOCK


def _swa_masks(sb):
    kr = lax.broadcasted_iota(jnp.int32, (2 * BLOCK, SWA_W), 0)
    qc = jnp.bitwise_and(lax.broadcasted_iota(jnp.int32, (2 * BLOCK, SWA_W), 1), BLOCK - 1)
    band = jnp.logical_and(kr > qc, kr <= qc + BLOCK)
    first = jnp.logical_and(band, kr >= BLOCK)
    return band, jnp.logical_or(first, jnp.logical_and(band, sb > 0))


def _heads_to_rows(ref, rs):
    return jnp.concatenate([ref[rs, h * LANES:(h + 1) * LANES] for h in range(SWA_GROUP)], axis=0)


def _sink_row(sk_ref):
    return jnp.concatenate([sk_ref[0, h:h + 1, :] for h in range(SWA_GROUP)], axis=1) * LOG2E


def _swa_in_specs(rev, nsb):
    sbi = (lambda j: nsb - 1 - j) if rev else (lambda j: j)
    cur = pl.BlockSpec((SWA_T, LANES), lambda g, j: (sbi(j), g))
    prev = pl.BlockSpec((BLOCK, LANES), lambda g, j: (jnp.maximum(4 * sbi(j) - 1, 0), g))
    q = pl.BlockSpec((SWA_T, SWA_GROUP * LANES), lambda g, j: (sbi(j), g))
    sink = pl.BlockSpec((1, SUBLANES, LANES), lambda g, j: (g, 0, 0))
    lse = pl.BlockSpec((SWA_GROUP, 1, SWA_T), lambda g, j: (g, 0, sbi(j)))
    return q, cur, prev, sink, lse


def _swa_fwd(qa, ka, va, sink_b):
    s_ = qa.shape[0]
    nsb = s_ // SWA_T
    c2 = HEAD_DIM ** -0.5 * LOG2E

    def body(q_ref, kc_ref, kp_ref, vc_ref, vp_ref, sk_ref, o32_ref, o16_ref, lse_ref, kx, vx):
        kx[0:BLOCK, :] = kp_ref[...]
        kx[BLOCK:5 * BLOCK, :] = kc_ref[...]
        vx[0:BLOCK, :] = vp_ref[...]
        vx[BLOCK:5 * BLOCK, :] = vc_ref[...]
        band, band0 = _swa_masks(pl.program_id(1))
        sink2 = _sink_row(sk_ref)
        for b in range(4):
            rs = slice(b * BLOCK, (b + 1) * BLOCK)
            ks = slice(b * BLOCK, (b + 2) * BLOCK)
            st = lax.dot_general(kx[ks, :], _heads_to_rows(q_ref, rs), NT, preferred_element_type=F32) * c2
            st = jnp.where(band0 if b == 0 else band, st, -jnp.inf)
            m = jnp.maximum(jnp.max(st, axis=0, keepdims=True), sink2)
            pt = jnp.exp2(st - m)
            den = jnp.sum(pt, axis=0, keepdims=True) + jnp.exp2(sink2 - m)
            o = lax.dot_general((pt * (1.0 / den)).astype(MXU_DTYPE), vx[ks, :], TN, preferred_element_type=F32)
            lse = m + jnp.log2(den)
            for hh in range(SWA_GROUP):
                cs = slice(hh * LANES, (hh + 1) * LANES)
                o32_ref[rs, cs] = o[cs, :]
                o16_ref[rs, cs] = o[cs, :].astype(o16_ref.dtype)
                lse_ref[hh, :, rs] = lse[:, cs]

    q, cur, prev, sink, lse_spec = _swa_in_specs(False, nsb)
    return pl.pallas_call(
        body, name="swa_fwd", grid=(SWA_KV_HEADS, nsb), in_specs=[q, cur, prev, cur, prev, sink],
        out_specs=[q, q, lse_spec],
        out_shape=[jax.ShapeDtypeStruct((s_, SWA_HEADS * LANES), F32), jax.ShapeDtypeStruct((s_, SWA_HEADS * LANES), MXU_DTYPE),
                   jax.ShapeDtypeStruct((SWA_HEADS, 1, s_), F32)],
        scratch_shapes=[pltpu.VMEM((5 * BLOCK, LANES), MXU_DTYPE), pltpu.VMEM((5 * BLOCK, LANES), MXU_DTYPE)],
        compiler_params=_cparams(("parallel", "arbitrary")),
    )(qa, ka, ka, va, va, sink_b)


def _swa_bwd(qa, ka, va, sink_b, o32, do, lse):
    s_ = qa.shape[0]
    nsb = s_ // SWA_T
    scale = HEAD_DIM ** -0.5
    c2 = scale * LOG2E

    def body(q_ref, kc_ref, kp_ref, vc_ref, vp_ref, sk_ref, o_ref, do_ref, lse_ref,
             dq_ref, dk_ref, dv_ref, dsk_ref, kx, vx, kacc, vacc, kcar, vcar):
        j = pl.program_id(1)
        kx[0:BLOCK, :] = kp_ref[...]
        kx[BLOCK:5 * BLOCK, :] = kc_ref[...]
        vx[0:BLOCK, :] = vp_ref[...]
        vx[BLOCK:5 * BLOCK, :] = vc_ref[...]
        band, band0 = _swa_masks(nsb - 1 - j)
        kacc[...] = jnp.zeros(kacc.shape, F32)
        vacc[...] = jnp.zeros(vacc.shape, F32)

        @pl.when(j == 0)
        def _():
            kcar[...] = jnp.zeros(kcar.shape, F32)
            vcar[...] = jnp.zeros(vcar.shape, F32)
            dsk_ref[...] = jnp.zeros(dsk_ref.shape, F32)

        sink2 = _sink_row(sk_ref)
        dsink = jnp.zeros((1, SWA_W), F32)
        for b in range(4):
            rs = slice(b * BLOCK, (b + 1) * BLOCK)
            ks = slice(b * BLOCK, (b + 2) * BLOCK)
            q, k2, v2 = _heads_to_rows(q_ref, rs), kx[ks, :], vx[ks, :]
            d = _heads_to_rows(do_ref, rs)
            delta = jnp.sum((d * _heads_to_rows(o_ref, rs)).T, axis=0, keepdims=True)
            l2 = jnp.concatenate([lse_ref[hh, :, rs] for hh in range(SWA_GROUP)], axis=1)
            st = lax.dot_general(k2, q, NT, preferred_element_type=F32) * c2
            pt = jnp.exp2(jnp.where(band0 if b == 0 else band, st, -jnp.inf) - l2)
            db = d.astype(MXU_DTYPE)
            dst = (pt * (lax.dot_general(v2, db, NT, preferred_element_type=F32) - delta) * scale).astype(MXU_DTYPE)
            dq = lax.dot_general(dst, k2, TN, preferred_element_type=F32)
            for hh in range(SWA_GROUP):
                dq_ref[rs, hh * LANES:(hh + 1) * LANES] = dq[hh * LANES:(hh + 1) * LANES, :]
            kacc[ks, :] += jnp.dot(dst, q, preferred_element_type=F32)
            vacc[ks, :] += jnp.dot(pt.astype(MXU_DTYPE), db, preferred_element_type=F32)
            dsink = dsink - jnp.exp2(sink2 - l2) * delta
        for hh in range(SWA_GROUP):
            tot = jnp.sum(dsink[:, hh * LANES:(hh + 1) * LANES], axis=1, keepdims=True)
            dsk_ref[0, hh:hh + 1, :] += jnp.broadcast_to(tot, (1, LANES))

        dk_ref[0:3 * BLOCK, :] = kacc[BLOCK:4 * BLOCK, :]
        dk_ref[3 * BLOCK:4 * BLOCK, :] = kacc[4 * BLOCK:5 * BLOCK, :] + kcar[...]
        dv_ref[0:3 * BLOCK, :] = vacc[BLOCK:4 * BLOCK, :].astype(dv_ref.dtype)
        dv_ref[3 * BLOCK:4 * BLOCK, :] = (vacc[4 * BLOCK:5 * BLOCK, :] + vcar[...]).astype(dv_ref.dtype)
        kcar[...] = kacc[0:BLOCK, :]
        vcar[...] = vacc[0:BLOCK, :]

    q, cur, prev, sink, lse_spec = _swa_in_specs(True, nsb)
    return pl.pallas_call(
        body, name="swa_bwd", grid=(SWA_KV_HEADS, nsb),
        in_specs=[q, cur, prev, cur, prev, sink, q, q, lse_spec],
        out_specs=[q, cur, cur, sink],
        out_shape=[jax.ShapeDtypeStruct((s_, SWA_HEADS * LANES), F32), jax.ShapeDtypeStruct((s_, SWA_KV_HEADS * LANES), F32),
                   jax.ShapeDtypeStruct((s_, SWA_KV_HEADS * LANES), MXU_DTYPE),
                   jax.ShapeDtypeStruct((SWA_KV_HEADS, SUBLANES, LANES), F32)],
        scratch_shapes=[pltpu.VMEM((5 * BLOCK, LANES), MXU_DTYPE), pltpu.VMEM((5 * BLOCK, LANES), MXU_DTYPE),
                        pltpu.VMEM((5 * BLOCK, LANES), F32), pltpu.VMEM((5 * BLOCK, LANES), F32),
                        pltpu.VMEM((BLOCK, LANES), F32), pltpu.VMEM((BLOCK, LANES), F32)],
        compiler_params=_cparams(("arbitrary", "arbitrary")),
    )(qa, ka, ka, va, va, sink_b, o32, do, lse)


MLA_T = 512
MLA_FWD_GROUP = 4
MLA_BWD_GROUP = 2


def _mla_specs(s_, t, group):
    w = group * LANES
    qs = pl.BlockSpec((t, w), lambda g, i: (i, g))
    kv = pl.BlockSpec((s_, w), lambda g, i: (0, g))
    row = pl.BlockSpec((group, 1, t), lambda g, i: (g, 0, i))
    return qs, kv, row


def _causal_scores_t(k, q, t, c2, masked):
    st = lax.dot_general(k, q, NT, preferred_element_type=F32) * c2
    if masked:
        kr = lax.broadcasted_iota(jnp.int32, (t, t), 0)
        qc = lax.broadcasted_iota(jnp.int32, (t, t), 1)
        st = jnp.where(kr <= qc, st, -jnp.inf)
    return st


def _mla_fwd(qc, kc, vp):
    s_ = qc.shape[0]
    t = min(MLA_T, s_)
    c2 = MLA_QK ** -0.5 * LOG2E
    grp = MLA_FWD_GROUP

    def body(q_ref, k_ref, v_ref, o32_ref, o16_ref, lse_ref, m_s, acc_s):
        qi = pl.program_id(1)
        m_s[...] = jnp.full(m_s.shape, -jnp.inf, F32)
        acc_s[...] = jnp.zeros(acc_s.shape, F32)
        ones_lane = lax.broadcasted_iota(jnp.int32, (t, LANES), 1) == MLA_V

        def step(ki, masked):
            off = pl.multiple_of(ki * t, t)
            for g in range(grp):
                cs = slice(g * LANES, (g + 1) * LANES)
                st = _causal_scores_t(k_ref[pl.ds(off, t), cs], q_ref[:, cs], t, c2, masked)
                m_old = m_s[g]
                m_new = jnp.maximum(m_old, jnp.max(st, axis=0, keepdims=True))
                alpha = jnp.exp2(m_old - m_new)
                pt = jnp.exp2(st - m_new).astype(MXU_DTYPE)
                v = v_ref[pl.ds(off, t), cs]
                v = jnp.where(ones_lane, jnp.ones((), v.dtype), v)
                acc_s[g] = alpha * acc_s[g] + lax.dot_general(v, pt, TN, preferred_element_type=F32)
                m_s[g] = m_new

        def full_block(ki, carry):
            step(ki, False)
            return carry

        lax.fori_loop(0, qi, full_block, 0)
        step(qi, True)
        for g in range(grp):
            cs = slice(g * LANES, (g + 1) * LANES)
            acc = acc_s[g]
            l = acc[MLA_V:MLA_V + 1, :]
            o = (acc * (1.0 / l)).T
            o32_ref[:, cs] = o
            o16_ref[:, cs] = o.astype(o16_ref.dtype)
            lse_ref[g] = m_s[g] + jnp.log2(l)

    qs, kv, row = _mla_specs(s_, t, grp)
    return pl.pallas_call(
        body, name="mla_fwd", grid=(MLA_HEADS // grp, s_ // t), in_specs=[qs, kv, kv], out_specs=[qs, qs, row],
        out_shape=[jax.ShapeDtypeStruct((s_, MLA_HEADS * LANES), F32), jax.ShapeDtypeStruct((s_, MLA_HEADS * LANES), MXU_DTYPE),
                   jax.ShapeDtypeStruct((MLA_HEADS, 1, s_), F32)],
        scratch_shapes=[pltpu.VMEM((grp, 1, t), F32), pltpu.VMEM((grp, LANES, t), F32)],
        compiler_params=_cparams(("parallel", "arbitrary")),
    )(qc, kc, vp)


def _mla_bwd(qc, kc, vp, dob, lse, delta):
    s_ = qc.shape[0]
    t = min(MLA_T, s_)
    scale = MLA_QK ** -0.5
    c2 = scale * LOG2E
    grp = MLA_BWD_GROUP

    def body(q_ref, do_ref, lse_ref, dl_ref, k_ref, v_ref, dq_ref, dk_ref, dv_ref, dqt_s):
        qi = pl.program_id(1)

        @pl.when(qi == 0)
        def _():
            dk_ref[...] = jnp.zeros(dk_ref.shape, F32)
            dv_ref[...] = jnp.zeros(dv_ref.shape, F32)

        dqt_s[...] = jnp.zeros(dqt_s.shape, F32)

        def step(ki, masked):
            off = pl.multiple_of(ki * t, t)
            for g in range(grp):
                cs = slice(g * LANES, (g + 1) * LANES)
                q, d, k = q_ref[:, cs], do_ref[:, cs], k_ref[pl.ds(off, t), cs]
                pt = jnp.exp2(_causal_scores_t(k, q, t, c2, masked) - lse_ref[g])
                dpt = lax.dot_general(v_ref[pl.ds(off, t), cs], d, NT, preferred_element_type=F32)
                dst = (pt * (dpt - dl_ref[g]) * scale).astype(MXU_DTYPE)
                dv_ref[pl.ds(off, t), cs] += jnp.dot(pt.astype(MXU_DTYPE), d, preferred_element_type=F32)
                dk_ref[pl.ds(off, t), cs] += jnp.dot(dst, q, preferred_element_type=F32)
                dqt_s[g] += lax.dot_general(k, dst, TN, preferred_element_type=F32)

        def full_block(ki, carry):
            step(ki, False)
            return carry

        lax.fori_loop(0, qi, full_block, 0)
        step(qi, True)
        for g in range(grp):
            dq_ref[:, g * LANES:(g + 1) * LANES] = dqt_s[g].T

    qs, kv, row = _mla_specs(s_, t, grp)
    shp = jax.ShapeDtypeStruct((s_, MLA_HEADS * LANES), F32)
    return pl.pallas_call(
        body, name="mla_bwd", grid=(MLA_HEADS // grp, s_ // t), in_specs=[qs, qs, row, row, kv, kv],
        out_specs=[qs, kv, kv], out_shape=[shp, shp, shp], scratch_shapes=[pltpu.VMEM((grp, LANES, t), F32)],
        compiler_params=_cparams(("parallel", "arbitrary")),
    )(qc, dob, lse, delta, kc, vp)


def _pad_heads(w, nh, hd, axis):
    shp = w.shape
    w = w.reshape(shp[:axis] + (nh, hd) + shp[axis + 1:])
    pad = [(0, 0)] * w.ndim
    pad[axis + 1] = (0, LANES - hd)
    w = jnp.pad(w, pad)
    return w.reshape(shp[:axis] + (nh * LANES,) + shp[axis + 1:])


def _unpad_heads(w, nh, hd, axis):
    shp = w.shape
    w = w.reshape(shp[:axis] + (nh, LANES) + shp[axis + 1:])
    w = lax.slice_in_dim(w, 0, hd, axis=axis + 1)
    return w.reshape(shp[:axis] + (nh * hd,) + shp[axis + 1:])


PACK_W = 1024
ROW_TILE = 16
FULL_SHAPE = dict(w_in=(1024, 3488), w_uq=(384, 768), w_ukv=(256, 1024), w_o_swa=(512, 1024), w_o_mla=(512, 1024),
                  w_out=(1024, 1024), w_gate=(1024, 2816), w_up=(1024, 2816), w_down=(2816, 1024))
BIG = tuple(FULL_SHAPE)
ROW_SHARDED = ("w_out", "w_down")
W_IN_COLS = FULL_SHAPE["w_in"][1] // N_DEV
W_IN_ROWS = -(-W_IN_COLS // ROW_TILE) * ROW_TILE
FF_COLS = D_FF // N_DEV
OUT_ROWS = D_MODEL // N_DEV
SMALL_FLAT = (("w_uq", 0, 36), ("w_ukv", 48, 32))
SMALL_USED = 80
MID_BLOCKS = 4
MID_ROWS = MID_BLOCKS * OUT_ROWS
EARLY_ROWS = W_IN_ROWS + MID_ROWS
LATE_ROWS = 3 * FF_COLS
PACK_ROWS = EARLY_ROWS + LATE_ROWS


def _shard_shape(n):
    r, c = FULL_SHAPE[n]
    return (r // N_DEV, c) if n in ROW_SHARDED else (r, c // N_DEV)


def _wire_pack(sh, dtype):
    c = lambda n: sh[n].astype(dtype)
    rows = [jnp.pad(c("w_in").T, ((0, W_IN_ROWS - W_IN_COLS), (0, 0))), c("w_out"),
            _pad_heads(c("w_o_swa").T, SWA_HEADS, HEAD_DIM, 1), _pad_heads(c("w_o_mla").T, MLA_HEADS, MLA_V, 1)]
    for n, _, r in SMALL_FLAT:
        rows.append(jnp.pad(c(n).reshape(r, PACK_W), ((0, -r % ROW_TILE), (0, 0))))
    rows.append(jnp.zeros((OUT_ROWS - SMALL_USED, PACK_W), dtype))
    return jnp.concatenate(rows + [c("w_gate").T, c("w_up").T, c("w_down")], 0)


def _mid_unpack(p):
    out = dict(w_out=p[0:OUT_ROWS], w_o_swa=_unpad_heads(p[OUT_ROWS:2 * OUT_ROWS], SWA_HEADS, HEAD_DIM, 1).T,
               w_o_mla=_unpad_heads(p[2 * OUT_ROWS:3 * OUT_ROWS], MLA_HEADS, MLA_V, 1).T)
    for n, off, r in SMALL_FLAT:
        out[n] = p[3 * OUT_ROWS + off:3 * OUT_ROWS + off + r].reshape(_shard_shape(n))
    return out


def _w_in_row_maps():
    sp = lambda col: (col // W_IN_COLS) * W_IN_ROWS + col % W_IN_COLS
    fwd = np.full((P_W,), -1, np.int64)

    def put(t0, c0, n):
        fwd[t0:t0 + n] = [sp(c) for c in range(c0, c0 + n)]

    put(P_GA, IN_OFF[6], D_MODEL)
    put(P_GB, IN_OFF[7], D_MODEL)
    for h in range(SWA_HEADS):
        put(P_Q + LANES * h, IN_OFF[0] + HEAD_DIM * h, HEAD_DIM)
    put(P_QLAT, IN_OFF[3], Q_LORA)
    put(P_KR + KR_LANE, IN_OFF[5], MLA_ROPE)
    for h in range(SWA_KV_HEADS):
        put(P_K + LANES * h, IN_OFF[1] + HEAD_DIM * h, HEAD_DIM)
        put(P_V + LANES * h, IN_OFF[2] + HEAD_DIM * h, HEAD_DIM)
    put(P_KVLAT, IN_OFF[4], KV_LORA)
    inv = np.full((N_DEV * W_IN_ROWS,), -1, np.int64)
    inv[fwd[fwd >= 0]] = np.nonzero(fwd >= 0)[0]
    return fwd, inv


def _take_rows(src, idx, *, name, tile=2 * LANES):
    n_out, n_src, width = len(idx), src.shape[0], src.shape[1]
    assert n_out % tile == 0 and n_src % tile == 0
    n_tiles = n_out // tile
    blocks = [sorted({int(v) // tile for v in idx[i * tile:(i + 1) * tile] if v >= 0}) for i in range(n_tiles)]
    k_max = max(1, max(len(b) for b in blocks))
    tab = np.zeros((n_tiles, k_max), np.int32)
    sel = np.zeros((n_tiles, k_max, tile, tile), np.float32)
    for i, blks in enumerate(blocks):
        for m, b in enumerate(blks):
            tab[i, m] = b
            for r in range(tile):
                v = int(idx[i * tile + r])
                if v >= 0 and v // tile == b:
                    sel[i, m, r, v % tile] = 1.0

    def body(tab_ref, sel_ref, *refs):
        o_ref = refs[k_max]
        acc = jnp.dot(sel_ref[0, 0], refs[0][...], preferred_element_type=F32)
        for m in range(1, k_max):
            acc = acc + jnp.dot(sel_ref[0, m], refs[m][...], preferred_element_type=F32)
        o_ref[...] = acc.astype(o_ref.dtype)

    def src_spec(m):
        return pl.BlockSpec((tile, width), lambda i, t: (t[i * k_max + m], 0))

    return pl.pallas_call(
        body, name=name,
        grid_spec=pltpu.PrefetchScalarGridSpec(
            num_scalar_prefetch=1, grid=(n_tiles,),
            in_specs=[pl.BlockSpec((1, k_max, tile, tile), lambda i, t: (i, 0, 0, 0))] + [src_spec(m) for m in range(k_max)],
            out_specs=pl.BlockSpec((tile, width), lambda i, t: (i, 0))),
        out_shape=jax.ShapeDtypeStruct((n_out, width), src.dtype),
        compiler_params=_cparams(("parallel",)),
    )(jnp.asarray(tab.reshape(-1)), jnp.asarray(sel, src.dtype), *([src] * k_max))


def _w_in_operand(win_g):
    return _take_rows(win_g.reshape(N_DEV * W_IN_ROWS, PACK_W), _w_in_row_maps()[0], name="w_in_rows")


def _mid_operands(wout_g, woa_g, wob_g, small_g):
    def full(n, off, r):
        a = small_g[:, off:off + r].reshape((N_DEV,) + _shard_shape(n))
        return jnp.moveaxis(a, 0, 1).reshape(FULL_SHAPE[n])

    w = {n: full(n, off, r) for n, off, r in SMALL_FLAT}
    ukv = w["w_ukv"].reshape(KV_LORA, MLA_HEADS, MLA_NOPE + MLA_V)
    return dict(
        wout=wout_g.reshape(D_MODEL, D_MODEL), woa_t=woa_g.reshape(D_MODEL, -1), wob_t=wob_g.reshape(D_MODEL, -1),
        wuq=_pad_heads(w["w_uq"], MLA_HEADS, MLA_QK, 1),
        wuk=_pad_heads(ukv[:, :, :MLA_NOPE].reshape(KV_LORA, -1), MLA_HEADS, MLA_NOPE, 1),
        wuv=_pad_heads(ukv[:, :, MLA_NOPE:].reshape(KV_LORA, -1), MLA_HEADS, MLA_V, 1),
    )


def _mid_grad_pack(g):
    uk = _unpad_heads(g["wukv"][:, :1024], MLA_HEADS, MLA_NOPE, 1).reshape(KV_LORA, MLA_HEADS, MLA_NOPE)
    uv = _unpad_heads(g["wukv"][:, 1024:], MLA_HEADS, MLA_V, 1).reshape(KV_LORA, MLA_HEADS, MLA_V)
    w = dict(w_uq=_unpad_heads(g["wuq"], MLA_HEADS, MLA_QK, 1), w_ukv=jnp.concatenate([uk, uv], 2).reshape(KV_LORA, -1))
    rows = []
    for n, _, r in SMALL_FLAT:
        rr, cc = FULL_SHAPE[n]
        a = jnp.moveaxis(w[n].reshape(rr, N_DEV, cc // N_DEV), 1, 0).reshape(N_DEV, r, PACK_W)
        rows.append(jnp.pad(a, ((0, 0), (0, -r % ROW_TILE), (0, 0))).astype(WIRE_DTYPE))
    rows.append(jnp.zeros((N_DEV, OUT_ROWS - SMALL_USED, PACK_W), WIRE_DTYPE))
    blk = lambda a: a.reshape(N_DEV, OUT_ROWS, PACK_W)
    return [blk(g["wout"]), blk(g["woa_t"]), blk(g["wob_t"]), jnp.concatenate(rows, 1)]


def _w_in_grad_chunks(g_win_t):
    return _take_rows(g_win_t, _w_in_row_maps()[1], name="dw_in_rows").reshape(N_DEV, W_IN_ROWS, PACK_W)


def _local_step(x, tgt, win_t, small, weights, grads):
    s_ = x.shape[0]
    tabs = _rope_tables(s_)
    sink_b = jnp.broadcast_to(small["swa_sinks"].reshape(SWA_KV_HEADS, SWA_GROUP, 1), (SWA_KV_HEADS, SWA_GROUP, LANES))
    sink_b = jnp.pad(sink_b, ((0, 0), (0, SUBLANES - SWA_GROUP), (0, 0)))

    h, qa, ka, va, cq, ckv, kro, p = _proj_in(x, small["mix_norm_g"], win_t, small["q_norm_g"], small["kv_norm_g"], tabs)
    oa32, oa16, lse_a = _swa_fwd(qa, ka, va, sink_b)
    ops = weights.mid(oa16)
    qc, kc, vp = _mla_up(cq, ckv, kro, ops["wuq"], ops["wuk"], ops["wuv"], tabs)
    ob32, ob16, lse_b = _mla_fwd(qc, kc, vp)
    ta, tb, y = _attn_out_gate(oa16, ob16, ops["woa_t"], ops["wob_t"], p)
    x1 = _mm(y, ops["wout"], "nn", name="out_proj", add=x, tm=1024, tn=1024)
    wgu_t, wd = weights.late(x1)
    h2, gu, act = _ffn_in_act(x1, small["ffn_norm_g"], wgu_t)

    dx2, dx2b, dg3, _, tot = _ffn_out_loss(act, wd, x1, small["final_norm_g"].reshape(1, D_MODEL), tgt)
    g = {}
    g_wd = _mm(act, dx2b, "tn", name="dw_down", tm=FF_TILE, tn=1024, tk=2048, out_dtype=WIRE_DTYPE)
    dgu = _d_act_swiglu(dx2b, wd, gu)
    g_wgu = _mm(dgu, h2, "tn", name="dw_ffn_in", tm=FF_TILE, tn=1024, tk=2048, out_dtype=WIRE_DTYPE)
    token = grads.late(g_wgu, g_wd)
    dx1, dx1b, dg2 = _mm_norm_bwd(dgu, wgu_t, x1, small["ffn_norm_g"] + token[0:1, 0:1], dx2, name="d_h2")
    g["wout"] = _mm(y, dx1b, "tn", name="dw_out", tm=1024, tn=1024, tk=2048, out_dtype=WIRE_DTYPE)
    dta, dtb, dgab = _d_y_gate(dx1b, ops["wout"], p, ta, tb)
    doa = _mm(dta, ops["woa_t"], "nn", name="d_oa", tm=1024, tn=1024)
    g["woa_t"] = _mm(dta, oa16, "tn", name="dw_o_swa", tm=1024, tn=1024, tk=2048, out_dtype=WIRE_DTYPE)
    g["wob_t"] = _mm(dtb, ob16, "tn", name="dw_o_mla", tm=1024, tn=1024, tk=2048, out_dtype=WIRE_DTYPE)
    dob16, delta_b = _mla_d_out(dtb, ops["wob_t"], ob32)
    dqc, dkc, dvp = _mla_bwd(qc, kc, vp, dob16, lse_b, delta_b)
    dqp, dkv, dkr, dqlat, dkvlat, dgq, dgkv = _mla_up_bwd(
        dqc, dkc, dvp, ops["wuq"], jnp.concatenate([ops["wuk"], ops["wuv"]], 1), p, small["q_norm_g"], small["kv_norm_g"], tabs)
    g["wuq"] = _mm(cq, dqp, "tn", name="dw_uq", tm=Q_LORA, tn=1024, tk=2048)
    g["wukv"] = _mm(ckv, dkv, "tn", name="dw_ukv", tm=KV_LORA, tn=2048, tk=2048)
    token = grads.mid(g)
    dqa, dka, dva, dsk = _swa_bwd(qa, ka, va, sink_b + token[0:1, 0:1], oa32, doa, lse_a)
    dp = _assemble_dp(dgab, dqa, dqlat, dkr, dka, dva, dkvlat, tabs)
    token = grads.last(_mm(dp, h, "tn", name="dw_in", tm=2176, tn=1024, tk=1024, out_dtype=WIRE_DTYPE))
    gx, _, dg1 = _mm_norm_bwd(dp, win_t, x, small["mix_norm_g"], dx1, name="d_h", after=token)

    sm = dict(mix_norm_g=dg1, ffn_norm_g=dg2, final_norm_g=dg3, q_norm_g=dgq, kv_norm_g=dgkv,
              swa_sinks=dsk[:, :SWA_GROUP, 0].reshape(1, SWA_HEADS))
    return tot, gx, sm


MESH = pl.DeviceIdType.MESH
ANY = pl.BlockSpec(memory_space=pl.ANY)


def _position():
    return lax.axis_index("x"), lax.axis_index("y"), lax.axis_index("c")


def _all_gather(block, pieces, shapes, *, name):
    n_out = len(shapes)
    n_rows = sum(p[3] for p in pieces)

    def body(x_ref, *refs):
        outs, (send_sems, recv_sems, local_sem) = refs[:n_out], refs[n_out:]
        x, y, c = _position()
        me, sibling = (x, y, c), (x, y, 1 - c)
        chips = [(1 - x, y), (x, 1 - y), (1 - x, 1 - y)]

        def dst(piece, blk):
            arr, lead, _, _ = piece
            return outs[arr].at[lead(4 * blk[0] + 2 * blk[1] + blk[2])]

        def own(piece):
            return x_ref.at[pl.ds(piece[2], piece[3])]

        def copies(k, blk, to, from_input):
            return [pltpu.make_async_remote_copy(
                src_ref=own(p) if from_input else dst(p, blk), dst_ref=dst(p, blk), send_sem=send_sems.at[k],
                recv_sem=recv_sems.at[k], device_id=to, device_id_type=MESH) for p in pieces]

        gathered_rows = x_ref.at[pl.ds(0, n_rows)]

        def whole_block(k):
            return pltpu.make_async_remote_copy(src_ref=gathered_rows, dst_ref=gathered_rows, send_sem=send_sems.at[k],
                                                recv_sem=recv_sems.at[k], device_id=me, device_id_type=MESH)

        for p in pieces:
            pltpu.make_async_copy(own(p), dst(p, me), local_sem).start()
        for cp in copies(0, me, sibling, True):
            cp.start()
        for j, chip in enumerate(chips):
            for cp in copies(1 + j, me, (*chip, c), True):
                cp.start()
        for j, chip in enumerate(chips):
            whole_block(1 + j).wait_recv()
            for cp in copies(4 + j, (*chip, c), sibling, False):
                cp.start()
        whole_block(0).wait_recv()
        for j in range(3):
            whole_block(4 + j).wait_recv()
        for k in range(7):
            whole_block(k).wait_send()
        pltpu.make_async_copy(gathered_rows, gathered_rows, local_sem).wait()

    return pl.pallas_call(
        body, name=name, out_shape=[jax.ShapeDtypeStruct(s, block.dtype) for s in shapes], in_specs=[ANY],
        out_specs=[ANY] * n_out,
        scratch_shapes=[pltpu.SemaphoreType.DMA((7,)), pltpu.SemaphoreType.DMA((7,)), pltpu.SemaphoreType.DMA],
    )(block)


HBM = pl.BlockSpec(memory_space=pltpu.HBM)
SEM = pl.BlockSpec(memory_space=pltpu.SEMAPHORE)
TILE_DEVS = FF_TILE // FF_COLS
GU_SHAPE = (2, 2, TILE_DEVS, FF_COLS, PACK_W)


def _gate_slab(d):
    return (d // TILE_DEVS, 0, d % TILE_DEVS)


def _up_slab(d):
    return (d // TILE_DEVS, 1, d % TILE_DEVS)
D_SHAPE = (N_DEV, FF_COLS, PACK_W)
LAND_SHAPE = (N_DEV, LATE_ROWS, PACK_W)


def _split_params():
    return pltpu.CompilerParams(has_side_effects=pltpu.SideEffectType.DATAFLOW_SIDE_EFFECTING)


def _peer(x, y, c, k):
    return ((1 - x) if k & 4 else x, (1 - y) if k & 2 else y, (1 - c) if k & 1 else c)


def _empty_hbm(shape, dtype):
    return pltpu.with_memory_space_constraint(lax.empty(shape, dtype), pltpu.HBM)


def _wait_all(rows, send_sems, recv_sems, me):
    for k in range(N_DEV - 1):
        cp = pltpu.make_async_remote_copy(src_ref=rows, dst_ref=rows, send_sem=send_sems.at[k], recv_sem=recv_sems.at[k],
                                          device_id=me, device_id_type=MESH)
        cp.wait_send()
        cp.wait_recv()


def _token_shape():
    return jax.ShapeDtypeStruct((SUBLANES, LANES), F32)


def _gather_start(pack, row0, pieces, shapes, *, name):
    n = len(shapes)

    def body(*refs):
        p_ref, bufs, send_sems, recv_sems, token = refs[0], refs[1:1 + n], refs[1 + n], refs[2 + n], refs[-1]
        x, y, c = _position()
        me = 4 * x + 2 * y + c
        for k in range(1, N_DEV):
            off = row0
            for buf, lead, rows in pieces:
                pltpu.make_async_remote_copy(
                    src_ref=p_ref.at[pl.ds(off, rows)], dst_ref=bufs[buf].at[lead(me)], send_sem=send_sems.at[k - 1],
                    recv_sem=recv_sems.at[k - 1], device_id=_peer(x, y, c, k), device_id_type=MESH).start()
                off += rows
        token[...] = jnp.zeros_like(token)

    sems, dt = pltpu.SemaphoreType.DMA((N_DEV - 1,)), pack.dtype
    return pl.pallas_call(
        body, name=name,
        out_shape=(sems, sems, pltpu.HBM(pack.shape, dt)) + tuple(pltpu.HBM(s, dt) for s in shapes) + (_token_shape(),),
        in_specs=(HBM,) * (1 + n), out_specs=(SEM, SEM) + (HBM,) * (1 + n) + (pl.BlockSpec(memory_space=pltpu.VMEM),),
        input_output_aliases={i: 2 + i for i in range(1 + n)}, compiler_params=_split_params(),
    )(pltpu.with_memory_space_constraint(pack, pltpu.HBM), *[_empty_hbm(s, dt) for s in shapes])


def _gather_wait(started, row0, n_rows, after, *, name):
    send_sems, recv_sems, pack, *bufs = started[:-1]
    n = len(bufs)

    def body(*refs):
        _wait_all(refs[0].at[pl.ds(row0, n_rows)], refs[1 + n], refs[2 + n], _position())

    outs = pl.pallas_call(
        body, name=name, out_shape=tuple(pltpu.HBM(a.shape, a.dtype) for a in (pack, *bufs)),
        in_specs=(HBM,) * (1 + n) + (SEM, SEM, ANY), out_specs=(HBM,) * (1 + n),
        input_output_aliases={i: i for i in range(1 + n)}, compiler_params=_split_params(),
    )(pack, *bufs, send_sems, recv_sems, after)
    return outs[0], outs[1:]


def _scatter_start(srcs, pieces, *, name):
    n = len(srcs)
    land_shape = (N_DEV, sum(p[2] for p in pieces), PACK_W)

    def body(*refs):
        src_refs, land_ref, send_sems, recv_sems, token = refs[:n], refs[n], refs[n + 1], refs[n + 2], refs[-1]
        x, y, c = _position()
        me = 4 * x + 2 * y + c
        for k in range(1, N_DEV):
            px, py, pc = _peer(x, y, c, k)
            off = 0
            for si, lead, rows in pieces:
                pltpu.make_async_remote_copy(
                    src_ref=src_refs[si].at[lead(4 * px + 2 * py + pc)], dst_ref=land_ref.at[me, pl.ds(off, rows)],
                    send_sem=send_sems.at[k - 1], recv_sem=recv_sems.at[k - 1], device_id=(px, py, pc),
                    device_id_type=MESH).start()
                off += rows
        token[...] = jnp.zeros_like(token)

    sems, dt = pltpu.SemaphoreType.DMA((N_DEV - 1,)), srcs[0].dtype
    return pl.pallas_call(
        body, name=name,
        out_shape=(sems, sems) + tuple(pltpu.HBM(a.shape, dt) for a in srcs) + (pltpu.HBM(land_shape, dt), _token_shape()),
        in_specs=(HBM,) * (n + 1), out_specs=(SEM, SEM) + (HBM,) * (n + 1) + (pl.BlockSpec(memory_space=pltpu.VMEM),),
        input_output_aliases={i: 2 + i for i in range(n + 1)}, compiler_params=_split_params(),
    )(*[pltpu.with_memory_space_constraint(a, pltpu.HBM) for a in srcs], _empty_hbm(land_shape, dt))


def _scatter_wait(started, after, *, name):
    send_sems, recv_sems, *bufs = started[:-1]
    n = len(bufs)

    def body(*refs):
        _wait_all(refs[n - 1].at[0], refs[n], refs[n + 1], _position())

    return pl.pallas_call(
        body, name=name, out_shape=tuple(pltpu.HBM(a.shape, a.dtype) for a in bufs),
        in_specs=(HBM,) * n + (SEM, SEM, ANY), out_specs=(HBM,) * n, input_output_aliases={i: i for i in range(n)},
        compiler_params=_split_params(),
    )(*bufs, send_sems, recv_sems, after)


def _peer_sum(own, own_lead, land, block, rows, idx, *, name):
    owns = list(own) if isinstance(own, (list, tuple)) else [own]
    n, lead_rank = len(owns), owns[0].ndim - 2

    def body(idx_ref, *refs):
        own_refs, land_refs, o_ref = refs[:n], refs[n:n + N_DEV - 1], refs[n + N_DEV - 1]
        for j in range(n):
            rs_ = slice(j * rows, (j + 1) * rows)
            acc = own_refs[j][(0,) * lead_rank].astype(F32)
            for k in range(N_DEV - 1):
                acc = acc + land_refs[k][0, rs_].astype(F32)
            o_ref[rs_] = acc

    own_spec = pl.BlockSpec((1,) * lead_rank + (rows, PACK_W), lambda i, t: own_lead(t[0]) + (0, 0))

    def land_spec(k):
        return pl.BlockSpec((1, n * rows, PACK_W), lambda i, t: (t[k + 1], block, 0))

    return pl.pallas_call(
        body, name=name,
        grid_spec=pltpu.PrefetchScalarGridSpec(
            num_scalar_prefetch=1, grid=(1,), in_specs=[own_spec] * n + [land_spec(k) for k in range(N_DEV - 1)],
            out_specs=pl.BlockSpec((n * rows, PACK_W), lambda i, t: (0, 0))),
        out_shape=jax.ShapeDtypeStruct((n * rows, PACK_W), F32), compiler_params=_cparams(("arbitrary",)),
    )(idx, *owns, *([land] * (N_DEV - 1)))


def _sum_adamw(own, own_lead, land, block, rows, idx, w, m, v, *, name):
    lead_rank, r = own.ndim - 2, w.shape[1]

    def body(idx_ref, own_ref, *refs):
        land_refs, (w_ref, m_ref, v_ref), outs = refs[:N_DEV - 1], refs[N_DEV - 1:N_DEV + 2], refs[N_DEV + 2:]
        g = own_ref[(0,) * lead_rank + (slice(0, r),)].astype(F32)
        for k in range(N_DEV - 1):
            g = g + land_refs[k][0, 0:r].astype(F32)
        for o_ref, val in zip(outs, (g,) + tuple(_adamw(w_ref[0], g, m_ref[0], v_ref[0]))):
            o_ref[0] = val

    own_spec = pl.BlockSpec((1,) * lead_rank + (rows, PACK_W), lambda i, t: own_lead(t[0]) + (0, 0))
    shard = pl.BlockSpec((1, r, PACK_W), lambda i, t: (0, 0, 0))

    def land_spec(k):
        return pl.BlockSpec((1, rows, PACK_W), lambda i, t: (t[k + 1], block, 0))

    return pl.pallas_call(
        body, name=name,
        grid_spec=pltpu.PrefetchScalarGridSpec(
            num_scalar_prefetch=1, grid=(1,),
            in_specs=[own_spec] + [land_spec(k) for k in range(N_DEV - 1)] + [shard] * 3, out_specs=[shard] * 4),
        out_shape=[jax.ShapeDtypeStruct((1, r, PACK_W), F32)] * 4, compiler_params=_cparams(("arbitrary",)),
    )(idx, own, *([land] * (N_DEV - 1)), w, m, v)


def _adamw(w, g, m, v):
    m = ADAM_B1 * m + (1.0 - ADAM_B1) * g
    v = ADAM_B2 * v + (1.0 - ADAM_B2) * (g * g)
    m_hat = m / (1.0 - ADAM_B1 ** ADAM_STEP)
    v_hat = v / (1.0 - ADAM_B2 ** ADAM_STEP)
    delta = -ADAM_LR * (m_hat / (jnp.sqrt(v_hat) + ADAM_EPS) + ADAM_WD * w)
    return delta, m, v


def _adamw_call(w, g, m, v, *, name, max_rows=256):
    _, r, c_ = w.shape
    tr = max_rows if r > max_rows and r % max_rows == 0 else r

    def body(w_ref, g_ref, m_ref, v_ref, d_ref, mo_ref, vo_ref):
        d, mn, vn = _adamw(w_ref[0], g_ref[...], m_ref[0], v_ref[0])
        d_ref[0] = d
        mo_ref[0] = mn
        vo_ref[0] = vn

    row3 = pl.BlockSpec((1, tr, c_), lambda i: (0, i, 0))
    shp = jax.ShapeDtypeStruct((1, r, c_), F32)
    return pl.pallas_call(
        body, name=name, grid=(r // tr,), in_specs=[row3, pl.BlockSpec((tr, c_), lambda i: (i, 0)), row3, row3],
        out_specs=[row3] * 3, out_shape=[shp] * 3, compiler_params=_cparams(("parallel",)),
    )(w, g, m, v)


SMALL = ("mix_norm_g", "ffn_norm_g", "final_norm_g", "q_norm_g", "kv_norm_g", "swa_sinks")
SMALL_W = dict(mix_norm_g=1024, ffn_norm_g=1024, final_norm_g=1024, q_norm_g=Q_LORA, kv_norm_g=KV_LORA, swa_sinks=SWA_HEADS)


def _small_adamw(parts, w, m, v):
    ns = len(SMALL)

    def body(p_ref, *refs):
        ins, outs = refs[:3 * ns], refs[3 * ns:]
        tot = p_ref[0]
        for dev in range(1, N_DEV):
            tot = tot + p_ref[dev]
        for k, n in enumerate(SMALL):
            g = jnp.sum(tot[k * SUBLANES:(k + 1) * SUBLANES, :SMALL_W[n]], axis=0, keepdims=True)
            res = _adamw(ins[k][...], g, ins[ns + k][...], ins[2 * ns + k][...])
            for j, r in enumerate((g,) + tuple(res)):
                outs[j * ns + k][...] = r
        outs[4 * ns][...] = jnp.sum(tot[ns * SUBLANES:(ns + 1) * SUBLANES, 0:1], axis=0, keepdims=True)

    shapes = [jax.ShapeDtypeStruct((1, SMALL_W[n]), F32) for n in SMALL]
    vm = pl.BlockSpec(memory_space=pltpu.VMEM)
    out = pl.pallas_call(
        body, name="small_adamw", in_specs=[vm] * (1 + 3 * ns), out_specs=[vm] * (4 * ns + 1),
        out_shape=shapes * 4 + [jax.ShapeDtypeStruct((1, 1), F32)],
    )(parts, *[d[n] for d in (w, m, v) for n in SMALL])
    return [dict(zip(SMALL, out[j * ns:(j + 1) * ns])) for j in range(4)] + [out[4 * ns]]


def _small_pack(d, rows_each):
    parts = [jnp.pad(d[n].astype(F32), ((0, 0), (0, PACK_W - SMALL_W[n]))) for n in SMALL]
    out = jnp.concatenate(parts, 0)
    pad = -out.shape[0] % SUBLANES
    return jnp.pad(out, ((0, pad), (0, 0)))


def kernel(x, mix_norm_g, w_in, swa_sinks, q_norm_g, w_uq, kv_norm_g, w_ukv, w_o_swa, w_o_mla, w_out, ffn_norm_g, w_gate, w_up, w_down, final_norm_g, loss_target, m_mix_norm_g, m_w_in, m_swa_sinks, m_q_norm_g, m_w_uq, m_kv_norm_g, m_w_ukv, m_w_o_swa, m_w_o_mla, m_w_out, m_ffn_norm_g, m_w_gate, m_w_up, m_w_down, m_final_norm_g, v_mix_norm_g, v_w_in, v_swa_sinks, v_q_norm_g, v_w_uq, v_kv_norm_g, v_w_ukv, v_w_o_swa, v_w_o_mla, v_w_out, v_ffn_norm_g, v_w_gate, v_w_up, v_w_down, v_final_norm_g):
    big_w = dict(w_in=w_in[0], w_uq=w_uq[0], w_ukv=w_ukv[0], w_o_swa=w_o_swa[0], w_o_mla=w_o_mla[0], w_out=w_out[0],
                 w_gate=w_gate[0], w_up=w_up[0], w_down=w_down[0])
    big_w3 = dict(w_in=w_in, w_uq=w_uq, w_ukv=w_ukv, w_o_swa=w_o_swa, w_o_mla=w_o_mla, w_out=w_out, w_gate=w_gate, w_up=w_up,
                  w_down=w_down)
    big_m = dict(w_in=m_w_in, w_uq=m_w_uq, w_ukv=m_w_ukv, w_o_swa=m_w_o_swa, w_o_mla=m_w_o_mla, w_out=m_w_out,
                 w_gate=m_w_gate, w_up=m_w_up, w_down=m_w_down)
    big_v = dict(w_in=v_w_in, w_uq=v_w_uq, w_ukv=v_w_ukv, w_o_swa=v_w_o_swa, w_o_mla=v_w_o_mla, w_out=v_w_out,
                 w_gate=v_w_gate, w_up=v_w_up, w_down=v_w_down)
    small_w = dict(mix_norm_g=mix_norm_g, ffn_norm_g=ffn_norm_g, final_norm_g=final_norm_g.reshape(1, D_MODEL),
                   q_norm_g=q_norm_g, kv_norm_g=kv_norm_g, swa_sinks=swa_sinks)
    small_m = dict(mix_norm_g=m_mix_norm_g, ffn_norm_g=m_ffn_norm_g, final_norm_g=m_final_norm_g.reshape(1, D_MODEL),
                   q_norm_g=m_q_norm_g, kv_norm_g=m_kv_norm_g, swa_sinks=m_swa_sinks)
    small_v = dict(mix_norm_g=v_mix_norm_g, ffn_norm_g=v_ffn_norm_g, final_norm_g=v_final_norm_g.reshape(1, D_MODEL),
                   q_norm_g=v_q_norm_g, kv_norm_g=v_kv_norm_g, swa_sinks=v_swa_sinks)

    px, py, pc = _position()
    me = 4 * px + 2 * py + pc
    idx = jnp.stack([me] + [4 * qx + 2 * qy + qc for qx, qy, qc in (_peer(px, py, pc, k) for k in range(1, N_DEV))])
    idx = idx.astype(jnp.int32)

    dev = lambda d: (d,)
    pack = _wire_pack(big_w, WIRE_DTYPE)
    win_g, = _all_gather(pack, ((0, dev, 0, W_IN_ROWS),), ((N_DEV, W_IN_ROWS, PACK_W),), name="ag_early")
    mid_pieces = tuple((b, dev, OUT_ROWS) for b in range(MID_BLOCKS))
    ag_mid = _gather_start(pack, W_IN_ROWS, mid_pieces, ((N_DEV, OUT_ROWS, PACK_W),) * MID_BLOCKS, name="ag_mid_start")
    ag = {}

    def own_rows(r0, r1, shape):
        return pack[r0:r1].reshape(shape)

    def mid_weights(after):
        pack_mid, blocks = _gather_wait(ag_mid, W_IN_ROWS, MID_ROWS, after, name="ag_mid_wait")
        ag["late"] = _gather_start(pack_mid, EARLY_ROWS, ((0, _gate_slab, FF_COLS), (0, _up_slab, FF_COLS), (1, dev, FF_COLS)),
                                   (GU_SHAPE, D_SHAPE), name="ag_late_start")
        row0 = lambda b: W_IN_ROWS + b * OUT_ROWS
        ops = _mid_operands(*[lax.dynamic_update_slice(blk, own_rows(row0(b), row0(b + 1), (1, OUT_ROWS, PACK_W)), (me, 0, 0))
                              for b, blk in enumerate(blocks)])
        ops["wuq"] = ops["wuq"] + ag["late"][-1][0:1, 0:1].astype(ops["wuq"].dtype)
        return ops

    def late_weights(after):
        _, (gu, d) = _gather_wait(ag["late"], EARLY_ROWS, LATE_ROWS, after, name="ag_late_wait")
        slab = (1, 1, 1, FF_COLS, PACK_W)
        gu = lax.dynamic_update_slice(gu, own_rows(EARLY_ROWS, EARLY_ROWS + FF_COLS, slab), _gate_slab(me) + (0, 0))
        gu = lax.dynamic_update_slice(gu, own_rows(EARLY_ROWS + FF_COLS, EARLY_ROWS + 2 * FF_COLS, slab), _up_slab(me) + (0, 0))
        d = lax.dynamic_update_slice(d, own_rows(EARLY_ROWS + 2 * FF_COLS, PACK_ROWS, (1, FF_COLS, PACK_W)), (me, 0, 0))
        return gu.reshape(2 * D_FF, D_MODEL), d.reshape(D_FF, D_MODEL)

    rs = {}

    def late_grads(g_gu, g_d):
        rs["late"] = _scatter_start([g_gu.reshape(GU_SHAPE), g_d.reshape(D_SHAPE)],
                                    ((0, _gate_slab, FF_COLS), (0, _up_slab, FF_COLS), (1, dev, FF_COLS)),
                                    name="rs_late_start")
        return rs["late"][-1]

    def mid_grads(g):
        rs["mid"] = _scatter_start(_mid_grad_pack(g), mid_pieces, name="rs_mid_start")
        return rs["mid"][-1]

    def last_grads(g_win_t):
        rs["last"] = _scatter_start([_w_in_grad_chunks(g_win_t)], ((0, dev, W_IN_ROWS),), name="rs_last_start")
        return rs["last"][-1]

    first_w = dict(small_w, mix_norm_g=mix_norm_g + ag_mid[-1][0:1, 0:1])
    loss_tot, gx, g_small = _local_step(
        x[0], loss_target[0], _w_in_operand(win_g), first_w, types.SimpleNamespace(mid=mid_weights, late=late_weights),
        types.SimpleNamespace(late=late_grads, mid=mid_grads, last=last_grads))

    loss_rows = jnp.pad(loss_tot[0:1, 0:1], ((0, SUBLANES - 1), (0, PACK_W - 1)))
    small_rows = jnp.concatenate([_small_pack(g_small_rows(g_small), SUBLANES), loss_rows], 0)
    n_small = small_rows.shape[0]
    ag_small = _gather_start(small_rows, 0, ((0, dev, n_small),), ((N_DEV, n_small, PACK_W),), name="ag_small_start")

    g_gu, g_d, land_late = _scatter_wait(rs["late"], ag_small[-1], name="rs_late_wait")
    *g_mid, land_mid = _scatter_wait(rs["mid"], ag_small[-1], name="rs_mid_wait")
    g_win, land_last = _scatter_wait(rs["last"], ag_small[-1], name="rs_last_wait")
    swap = lambda a: jnp.swapaxes(a, 1, 2)
    same = lambda a: a
    chunks = dict(w_gate=(swap, g_gu, _gate_slab, land_late, 0, FF_COLS), w_up=(swap, g_gu, _up_slab, land_late, 1, FF_COLS),
                  w_down=(same, g_d, dev, land_late, 2, FF_COLS), w_in=(swap, g_win, dev, land_last, 0, W_IN_ROWS))
    gw, dw, mw, vw = {}, {}, {}, {}
    for n, (view, own, lead, land, blk, rows) in chunks.items():
        res = _sum_adamw(own, lead, land, blk, rows, idx, view(big_w3[n]), view(big_m[n]), view(big_v[n]), name="adamw_" + n)
        gw[n], dw[n], mw[n], vw[n] = (view(r) for r in res)
    g_nat = _mid_unpack(_peer_sum(g_mid, dev, land_mid, 0, OUT_ROWS, idx, name="rs_sum_mid"))
    for n, g in g_nat.items():
        gw[n] = g[None]
        dw[n], mw[n], vw[n] = _adamw_call(big_w3[n], g, big_m[n], big_v[n], name="adamw_" + n)

    own_small, (parts,) = _gather_wait(ag_small, 0, n_small, vw[n], name="ag_small_wait")
    parts = lax.dynamic_update_slice(parts, own_small[None], (me, 0, 0))
    gs, ds, ms, vs, loss = _small_adamw(parts, small_w, small_m, small_v)
    loss = loss[0, 0]
    for d in (gs, ds, ms, vs):
        d["final_norm_g"] = d["final_norm_g"].reshape(D_MODEL)

    order = ("mix_norm_g", "w_in", "swa_sinks", "q_norm_g", "w_uq", "kv_norm_g", "w_ukv", "w_o_swa", "w_o_mla", "w_out",
             "ffn_norm_g", "w_gate", "w_up", "w_down", "final_norm_g")

    def leaves(big, small):
        return [big[n] if n in big else small[n] for n in order]

    return (loss, gx[None], *leaves(gw, gs), *leaves(dw, ds), *leaves(mw, ms), *leaves(vw, vs))


def g_small_rows(g_small):
    out = dict(g_small)
    out["swa_sinks"] = jnp.pad(g_small["swa_sinks"], ((0, SUBLANES - 1), (0, 0)))
    return out
```

```python
import types

import numpy as np
import jax
import jax.numpy as jnp
from jax import lax
from jax.experimental import pallas as pl
from jax.experimental.pallas import tpu as pltpu

F32 = jnp.float32
MXU_DTYPE = jnp.bfloat16
WIRE_DTYPE = jnp.bfloat16

D_MODEL = 1024
EPS = 1e-6
ROPE_THETA = 10000.0
BLOCK = 128
HEAD_DIM = 64
SWA_HEADS = 8
SWA_KV_HEADS = 2
SWA_GROUP = SWA_HEADS // SWA_KV_HEADS
MLA_HEADS = 8
MLA_NOPE = 64
MLA_ROPE = 32
MLA_V = 64
MLA_QK = MLA_NOPE + MLA_ROPE
Q_LORA = 384
KV_LORA = 256
D_FF = 2816
IN_SIZES = (512, 128, 128, Q_LORA, KV_LORA, MLA_ROPE, D_MODEL, D_MODEL)
IN_OFF = tuple(int(v) for v in np.cumsum((0,) + IN_SIZES))
ADAM_LR, ADAM_B1, ADAM_B2, ADAM_EPS, ADAM_WD, ADAM_STEP = 0.001, 0.9, 0.999, 1e-08, 0.01, 10

LANES = 128
SUBLANES = 8
VMEM_LIMIT = 48 * 1024 * 1024
N_DEV = 8

P_GA, P_GB, P_Q, P_QLAT, P_KR, P_K, P_V, P_KVLAT, P_W = 0, 1024, 2048, 3072, 3456, 3584, 3840, 4096, 4352
KR_LANE = 64

LOG2E = 1.4426950408889634

NT = (((1,), (1,)), ((), ()))
NN = (((1,), (0,)), ((), ()))
TN = (((0,), (0,)), ((), ()))


def _cparams(sem):
    return pltpu.CompilerParams(dimension_semantics=sem, vmem_limit_bytes=VMEM_LIMIT)


def _mm(a, b, mode, *, name, out_dtype=F32, add=None, tm=512, tn=512, tk=None):
    if mode == "nn":
        (M, K), (K2, N) = a.shape, b.shape
    elif mode == "nt":
        (M, K), (N, K2) = a.shape, b.shape
    else:
        (K, M), (K2, N) = a.shape, b.shape
    assert K == K2, (a.shape, b.shape, mode)
    tm, tn, tk = min(tm, M), min(tn, N), K if tk is None else min(tk, K)
    assert M % tm == 0 and N % tn == 0 and K % tk == 0, (M, N, K, tm, tn, tk)
    nk = K // tk
    dn = {"nn": NN, "nt": NT, "tn": TN}[mode]
    if mode == "tn":
        a_spec = pl.BlockSpec((tk, tm), lambda i, j, k: (k, i))
    else:
        a_spec = pl.BlockSpec((tm, tk), lambda i, j, k: (i, k))
    once = dict(pipeline_mode=pl.Buffered(1)) if (nk == 1 and tn == N) else {}
    if mode == "nt":
        b_spec = pl.BlockSpec((tn, tk), lambda i, j, k: (j, k), **once)
    else:
        b_spec = pl.BlockSpec((tk, tn), lambda i, j, k: (k, j), **once)
    o_spec = pl.BlockSpec((tm, tn), lambda i, j, k: (i, j))
    has_add = add is not None

    def body(*refs):
        a_ref, b_ref = refs[0], refs[1]
        add_ref = refs[2] if has_add else None
        o_ref = refs[2 + has_add]
        p = lax.dot_general(a_ref[...], b_ref[...], dn, preferred_element_type=F32)

        def finish(acc):
            if has_add:
                acc = acc + add_ref[...]
            o_ref[...] = acc.astype(o_ref.dtype)

        if nk == 1:
            finish(p)
        else:
            acc_ref = refs[-1]
            k = pl.program_id(2)

            @pl.when(k == 0)
            def _():
                acc_ref[...] = p

            @pl.when((k > 0) & (k < nk - 1))
            def _():
                acc_ref[...] += p

            @pl.when(k == nk - 1)
            def _():
                finish(acc_ref[...] + p)

    ins = [a, b] + ([add] if has_add else [])
    return pl.pallas_call(
        body, name=name, grid=(M // tm, N // tn, nk), in_specs=[a_spec, b_spec] + ([o_spec] if has_add else []), out_specs=o_spec,
        out_shape=jax.ShapeDtypeStruct((M, N), out_dtype),
        scratch_shapes=[pltpu.VMEM((tm, tn), F32)] if nk > 1 else [],
        compiler_params=_cparams(("parallel", "parallel", "arbitrary")),
    )(*ins)


def _rows(ts, w, cb=0):
    return pl.BlockSpec((ts, w), lambda i: (i, cb))


def _const(r, w):
    return pl.BlockSpec((r, w), lambda i: (0, 0))


def _sublane_sum(v):
    ts, c = v.shape
    return jnp.sum(v.reshape(ts // SUBLANES, SUBLANES, c), axis=0)


def _sigmoid(v):
    return 1.0 / (1.0 + jnp.exp(-v))


def _rope(v, cos, s_up, s_dn, up, dn):
    return v * cos + pltpu.roll(v, up, 1) * s_up + pltpu.roll(v, dn, 1) * s_dn


def _rope_t(dv, cos, s_up, s_dn, up, dn):
    return dv * cos + pltpu.roll(dv * s_up, dn, 1) + pltpu.roll(dv * s_dn, up, 1)


def _rope_tables(seq):
    pos = np.arange(seq, dtype=np.float32)[:, None]

    def base(dim):
        inv = np.float32(ROPE_THETA) ** (-np.arange(0, dim, 2, dtype=np.float32) / np.float32(dim))
        ang = (pos * inv.astype(np.float32)[None, :]).astype(np.float32)
        return np.cos(ang).astype(np.float32), np.sin(ang).astype(np.float32)

    z = lambda n: np.zeros((seq, n), np.float32)
    ca, sa = base(HEAD_DIM)
    a_cos = np.concatenate([ca, ca, z(64)], 1)
    a_up = np.concatenate([-sa, z(96)], 1)
    a_dn = np.concatenate([z(32), sa, z(64)], 1)
    cb, sb = base(MLA_ROPE)
    one = np.ones((seq, 64), np.float32)
    q_cos = np.concatenate([one, cb, cb, z(32)], 1)
    k_cos = np.concatenate([z(64), cb, cb, z(32)], 1)
    b_up = np.concatenate([z(64), -sb, z(48)], 1)
    b_dn = np.concatenate([z(80), sb, z(32)], 1)
    return tuple(jnp.asarray(t) for t in (a_cos, a_up, a_dn, q_cos, k_cos, b_up, b_dn))


def _rms(v, g):
    return v * lax.rsqrt(jnp.mean(v * v, axis=-1, keepdims=True) + EPS) * g


def _rms_bwd(v, g, d):
    r = lax.rsqrt(jnp.mean(v * v, axis=-1, keepdims=True) + EPS)
    xh = v * r
    dxh = d * g
    return r * (dxh - xh * jnp.mean(dxh * xh, axis=-1, keepdims=True)), d * xh


F_GA, F_GB, F_KVLAT, F_QLAT, F_W = 0, 1024, 2048, 2304, 2688


def _proj_in(x, g, w_t, gq, gkv, tabs, *, tm=512):
    s_, c = x.shape
    a_cos, a_up, a_dn, _, k_cos, b_up, b_dn = tabs

    def body(x_ref, g_ref, w_ref, gq_ref, gkv_ref, ac, au, ad, kc, bu, bd,
             h_ref, qa_ref, ka_ref, va_ref, cq_ref, ckv_ref, kro_ref, pf_ref):
        h = _rms(x_ref[...], g_ref[...]).astype(h_ref.dtype)
        h_ref[...] = h
        mm = lambda a, b: lax.dot_general(h, w_ref[a:b, :], NT, preferred_element_type=F32)
        pf_ref[:, F_GA:F_KVLAT] = mm(P_GA, P_Q)
        c_, u_, d_ = ac[...], au[...], ad[...]
        q = mm(P_Q, P_QLAT)
        for hd in range(SWA_HEADS):
            sl = slice(hd * LANES, (hd + 1) * LANES)
            qa_ref[:, sl] = _rope(q[:, sl], c_, u_, d_, 96, 32).astype(qa_ref.dtype)
        kv = mm(P_KR, P_KVLAT)
        kro_ref[...] = _rope(kv[:, :LANES], kc[...], bu[...], bd[...], 112, 16)
        for hd in range(SWA_KV_HEADS):
            sl = slice((1 + hd) * LANES, (2 + hd) * LANES)
            ka_ref[:, hd * LANES:(hd + 1) * LANES] = _rope(kv[:, sl], c_, u_, d_, 96, 32).astype(ka_ref.dtype)
        va_ref[...] = kv[:, P_V - P_KR:].astype(va_ref.dtype)
        for a, b, f0, gref, dst in ((P_QLAT, P_KR, F_QLAT, gq_ref, cq_ref), (P_KVLAT, P_W, F_KVLAT, gkv_ref, ckv_ref)):
            v = mm(a, b)
            pf_ref[:, f0:f0 + b - a] = v
            r = lax.rsqrt(jnp.mean(v * v, axis=-1, keepdims=True) + EPS)
            dst[...] = (v * r * gref[...]).astype(dst.dtype)

    tab = _rows(tm, LANES)
    widths = (c, SWA_HEADS * LANES, SWA_KV_HEADS * LANES, SWA_KV_HEADS * LANES, Q_LORA, KV_LORA)
    return pl.pallas_call(
        body, name="proj_in", grid=(s_ // tm,),
        in_specs=[_rows(tm, c), _const(1, c), pl.BlockSpec((P_W, c), lambda i: (0, 0), pipeline_mode=pl.Buffered(1)),
                  _const(1, Q_LORA), _const(1, KV_LORA), tab, tab, tab, tab, tab, tab],
        out_specs=[_rows(tm, w) for w in widths] + [tab, _rows(tm, F_W)],
        out_shape=[jax.ShapeDtypeStruct((s_, w), MXU_DTYPE) for w in widths]
        + [jax.ShapeDtypeStruct((s_, LANES), F32), jax.ShapeDtypeStruct((s_, F_W), F32)],
        compiler_params=_cparams(("parallel",)),
    )(x, g, w_t, gq, gkv, a_cos, a_up, a_dn, k_cos, b_up, b_dn)


def _mm_norm_bwd(a, b, x, g, res, *, name, after=None, tm=512):
    s_, kk = a.shape
    c = b.shape[1]
    has_after = after is not None

    def body(*refs):
        a_ref, b_ref, x_ref, g_ref, res_ref = refs[:5]
        dx_ref, dxb_ref, dg_ref = refs[5 + has_after:]
        d = jnp.dot(a_ref[...], b_ref[...], preferred_element_type=F32)
        dx, gg = _rms_bwd(x_ref[...], g_ref[...], d)
        dx = dx + res_ref[...]
        dx_ref[...] = dx
        dxb_ref[...] = dx.astype(dxb_ref.dtype)

        @pl.when(pl.program_id(0) == 0)
        def _():
            dg_ref[...] = jnp.zeros(dg_ref.shape, F32)

        dg_ref[...] += _sublane_sum(gg)

    row = _rows(tm, c)
    in_specs = [_rows(tm, kk), pl.BlockSpec((kk, c), lambda i: (0, 0), pipeline_mode=pl.Buffered(1)), row, _const(1, c), row]
    return pl.pallas_call(
        body, name=name, grid=(s_ // tm,), in_specs=in_specs + ([pl.BlockSpec(memory_space=pl.ANY)] if has_after else []),
        out_specs=[row, row, _const(SUBLANES, c)],
        out_shape=[jax.ShapeDtypeStruct((s_, c), F32), jax.ShapeDtypeStruct((s_, c), MXU_DTYPE),
                   jax.ShapeDtypeStruct((SUBLANES, c), F32)],
        compiler_params=_cparams(("arbitrary",)),
    )(*([a, b, x, g, res] + ([after] if has_after else [])))


def _mla_up(cq, ckv, kro, wuq, wuk, wuv, tabs, *, ts=512):
    s_ = cq.shape[0]
    _, _, _, q_cos, _, b_up, b_dn = tabs

    def body(cq_ref, ckv_ref, kr_ref, wq_ref, wk_ref, wv_ref, qc, bu, bd, qo_ref, ko_ref, vo_ref):
        c_, u_, d_ = qc[...], bu[...], bd[...]
        kr = kr_ref[...]
        ckv_ = ckv_ref[...]
        vo_ref[...] = jnp.dot(ckv_, wv_ref[...], preferred_element_type=F32).astype(vo_ref.dtype)
        q = jnp.dot(cq_ref[...], wq_ref[...], preferred_element_type=F32)
        k = jnp.dot(ckv_, wk_ref[...], preferred_element_type=F32)
        for h in range(MLA_HEADS):
            sl = slice(h * LANES, (h + 1) * LANES)
            qo_ref[:, sl] = _rope(q[:, sl], c_, u_, d_, 112, 16).astype(qo_ref.dtype)
            ko_ref[:, sl] = (k[:, sl] + kr).astype(ko_ref.dtype)

    tab, out = _rows(ts, LANES), _rows(ts, 1024)
    return pl.pallas_call(
        body, name="mla_up", grid=(s_ // ts,),
        in_specs=[_rows(ts, Q_LORA), _rows(ts, KV_LORA), tab, _const(Q_LORA, 1024), _const(KV_LORA, 1024),
                  _const(KV_LORA, 1024), tab, tab, tab],
        out_specs=[out, out, out], out_shape=[jax.ShapeDtypeStruct((s_, 1024), MXU_DTYPE)] * 3,
        compiler_params=_cparams(("parallel",)),
    )(cq, ckv, kro, wuq, wuk, wuv, q_cos, b_up, b_dn)


def _mla_up_bwd(dqc, dkc, dvp, wuq, wukv, p, gq, gkv, tabs, *, ts=256):
    s_ = dqc.shape[0]
    _, _, _, q_cos, k_cos, b_up, b_dn = tabs

    def body(dq_ref, dk_ref, dv_ref, wq_ref, wkv_ref, ql_ref, kvl_ref, gq_ref, gkv_ref, qc, kc, bu, bd,
             dqo_ref, dkvo_ref, dkr_ref, dql_ref, dkvl_ref, dgq_ref, dgkv_ref):
        c_, u_, d_ = qc[...], bu[...], bd[...]
        tot = jnp.zeros((ts, LANES), F32)
        for h in range(MLA_HEADS):
            sl = slice(h * LANES, (h + 1) * LANES)
            dqo_ref[:, sl] = _rope_t(dq_ref[:, sl], c_, u_, d_, 112, 16).astype(dqo_ref.dtype)
            dk = dk_ref[:, sl]
            dkvo_ref[:, sl] = dk.astype(dkvo_ref.dtype)
            tot = tot + dk
        dkvo_ref[:, 1024:2048] = dv_ref[...].astype(dkvo_ref.dtype)
        dkr_ref[...] = _rope_t(tot, kc[...], u_, d_, 112, 16).astype(dkr_ref.dtype)

        @pl.when(pl.program_id(0) == 0)
        def _():
            dgq_ref[...] = jnp.zeros(dgq_ref.shape, F32)
            dgkv_ref[...] = jnp.zeros(dgkv_ref.shape, F32)

        for do_ref, w_ref, x_ref, g_ref, dx_ref, dg_ref in ((dqo_ref, wq_ref, ql_ref, gq_ref, dql_ref, dgq_ref),
                                                            (dkvo_ref, wkv_ref, kvl_ref, gkv_ref, dkvl_ref, dgkv_ref)):
            d = lax.dot_general(do_ref[...], w_ref[...], NT, preferred_element_type=F32)
            dx, gg = _rms_bwd(x_ref[...], g_ref[...], d)
            dx_ref[...] = dx.astype(dx_ref.dtype)
            dg_ref[...] += _sublane_sum(gg)

    tab = _rows(ts, LANES)
    return pl.pallas_call(
        body, name="mla_up_bwd", grid=(s_ // ts,),
        in_specs=[_rows(ts, 1024), _rows(ts, 1024), _rows(ts, 1024), _const(Q_LORA, 1024), _const(KV_LORA, 2048),
                  _rows(ts, Q_LORA, F_QLAT // Q_LORA), _rows(ts, KV_LORA, F_KVLAT // KV_LORA),
                  _const(1, Q_LORA), _const(1, KV_LORA), tab, tab, tab, tab],
        out_specs=[_rows(ts, 1024), _rows(ts, 2048), _rows(ts, LANES), _rows(ts, Q_LORA), _rows(ts, KV_LORA),
                   _const(SUBLANES, Q_LORA), _const(SUBLANES, KV_LORA)],
        out_shape=[jax.ShapeDtypeStruct((s_, 1024), MXU_DTYPE), jax.ShapeDtypeStruct((s_, 2048), MXU_DTYPE),
                   jax.ShapeDtypeStruct((s_, LANES), MXU_DTYPE), jax.ShapeDtypeStruct((s_, Q_LORA), MXU_DTYPE),
                   jax.ShapeDtypeStruct((s_, KV_LORA), MXU_DTYPE), jax.ShapeDtypeStruct((SUBLANES, Q_LORA), F32),
                   jax.ShapeDtypeStruct((SUBLANES, KV_LORA), F32)],
        compiler_params=_cparams(("arbitrary",)),
    )(dqc, dkc, dvp, wuq, wukv, p, p, gq, gkv, q_cos, k_cos, b_up, b_dn)


def _assemble_dp(dgab, dqa, dqlat, dkr, dka, dva, dkvlat, tabs, *, ts=256):
    s_ = dqa.shape[0]
    a_cos, a_up, a_dn = tabs[0], tabs[1], tabs[2]

    def body(dg_ref, dq_ref, dql_ref, dkr_ref, dk_ref, dv_ref, dkvl_ref, ac, au, ad, o_ref):
        c_, u_, d_ = ac[...], au[...], ad[...]
        o_ref[:, P_GA:P_Q] = dg_ref[...]
        for h in range(SWA_HEADS):
            sl = slice(h * LANES, (h + 1) * LANES)
            o_ref[:, P_Q + h * LANES:P_Q + (h + 1) * LANES] = _rope_t(dq_ref[:, sl], c_, u_, d_, 96, 32).astype(o_ref.dtype)
        o_ref[:, P_QLAT:P_KR] = dql_ref[...]
        o_ref[:, P_KR:P_K] = dkr_ref[...]
        for h in range(SWA_KV_HEADS):
            sl = slice(h * LANES, (h + 1) * LANES)
            o_ref[:, P_K + h * LANES:P_K + (h + 1) * LANES] = _rope_t(dk_ref[:, sl], c_, u_, d_, 96, 32).astype(o_ref.dtype)
        o_ref[:, P_V:P_KVLAT] = dv_ref[...]
        o_ref[:, P_KVLAT:P_W] = dkvl_ref[...]

    tab = _rows(ts, LANES)
    return pl.pallas_call(
        body, name="assemble_dp", grid=(s_ // ts,),
        in_specs=[_rows(ts, 2048), _rows(ts, 1024), _rows(ts, Q_LORA), _rows(ts, LANES), _rows(ts, 256), _rows(ts, 256),
                  _rows(ts, KV_LORA), tab, tab, tab],
        out_specs=_rows(ts, P_W), out_shape=jax.ShapeDtypeStruct((s_, P_W), MXU_DTYPE),
        compiler_params=_cparams(("parallel",)),
    )(dgab, dqa, dqlat, dkr, dka, dva, dkvlat, a_cos, a_up, a_dn)


def _attn_out_gate(oa, ob, woa_t, wob_t, p, *, ts=512):
    s_ = p.shape[0]

    def body(oa_ref, ob_ref, wa_ref, wb_ref, ga_ref, gb_ref, y_ref):
        ta = lax.dot_general(oa_ref[...], wa_ref[...], NT, preferred_element_type=F32)
        tb = lax.dot_general(ob_ref[...], wb_ref[...], NT, preferred_element_type=F32)
        y_ref[...] = (_sigmoid(ga_ref[...]) * ta + _sigmoid(gb_ref[...]) * tb).astype(y_ref.dtype)

    w = _const(1024, 1024)
    return pl.pallas_call(
        body, name="attn_out_gate", grid=(s_ // ts,),
        in_specs=[_rows(ts, 1024), _rows(ts, 1024), w, w, _rows(ts, 1024, F_GA // 1024), _rows(ts, 1024, F_GB // 1024)],
        out_specs=_rows(ts, 1024), out_shape=jax.ShapeDtypeStruct((s_, 1024), MXU_DTYPE),
        compiler_params=_cparams(("parallel",)),
    )(oa, ob, woa_t, wob_t, p, p)


def _d_y_gate(dx1b, wout, p, oa, ob, woa_t, wob_t, *, ts=512):
    s_ = p.shape[0]

    def body(dx_ref, w_ref, ga_ref, gb_ref, oa_ref, ob_ref, wa_ref, wb_ref, dta_ref, dtb_ref, dg_ref):
        d = lax.dot_general(dx_ref[...], w_ref[...], NT, preferred_element_type=F32)
        sa, sb = _sigmoid(ga_ref[...]), _sigmoid(gb_ref[...])
        dta_ref[...] = (d * sa).astype(dta_ref.dtype)
        dtb_ref[...] = (d * sb).astype(dtb_ref.dtype)
        ta = lax.dot_general(oa_ref[...], wa_ref[...], NT, preferred_element_type=F32)
        dg_ref[:, 0:1024] = (d * ta * (sa * (1.0 - sa))).astype(dg_ref.dtype)
        tb = lax.dot_general(ob_ref[...], wb_ref[...], NT, preferred_element_type=F32)
        dg_ref[:, 1024:2048] = (d * tb * (sb * (1.0 - sb))).astype(dg_ref.dtype)

    w = _const(1024, 1024)
    return pl.pallas_call(
        body, name="d_y_gate", grid=(s_ // ts,),
        in_specs=[_rows(ts, 1024), w, _rows(ts, 1024, F_GA // 1024), _rows(ts, 1024, F_GB // 1024),
                  _rows(ts, 1024), _rows(ts, 1024), w, w],
        out_specs=[_rows(ts, 1024), _rows(ts, 1024), _rows(ts, 2048)],
        out_shape=[jax.ShapeDtypeStruct((s_, 1024), MXU_DTYPE)] * 2 + [jax.ShapeDtypeStruct((s_, 2048), MXU_DTYPE)],
        compiler_params=_cparams(("parallel",)),
    )(dx1b, wout, p, p, oa, ob, woa_t, wob_t)


FF_TILE = D_FF // 2


def _ffn_in_act(x1, g, wgu_t, *, tm=512):
    s_ = x1.shape[0]
    n = s_ // tm

    def body(x_ref, g_ref, w_ref, h_ref, gu_ref, a_ref):
        h = _rms(x_ref[...], g_ref[...]).astype(h_ref.dtype)
        h_ref[...] = h
        p = lax.dot_general(h, w_ref[...], NT, preferred_element_type=F32)
        gu_ref[...] = p
        gate = p[:, :FF_TILE]
        a_ref[...] = (gate * _sigmoid(gate) * p[:, FF_TILE:]).astype(a_ref.dtype)

    return pl.pallas_call(
        body, name="ffn_in", grid=(2, s_ // tm),
        in_specs=[pl.BlockSpec((tm, D_MODEL), lambda j, i: (i, 0)), pl.BlockSpec((1, D_MODEL), lambda j, i: (0, 0)),
                  pl.BlockSpec((2 * FF_TILE, D_MODEL), lambda j, i: (j, 0))],
        out_specs=[pl.BlockSpec((tm, D_MODEL), lambda j, i: (i + j * (n - 1 - i), 0)),
                   pl.BlockSpec((tm, 2 * FF_TILE), lambda j, i: (i, j)),
                   pl.BlockSpec((tm, FF_TILE), lambda j, i: (i, j))],
        out_shape=[jax.ShapeDtypeStruct((s_, D_MODEL), MXU_DTYPE), jax.ShapeDtypeStruct((s_, 2 * D_FF), F32),
                   jax.ShapeDtypeStruct((s_, D_FF), MXU_DTYPE)],
        compiler_params=_cparams(("arbitrary", "arbitrary")),
    )(x1, g, wgu_t)


def _d_act_swiglu(dx2b, wd, gu, *, tm=512):
    s_ = dx2b.shape[0]

    def body(d_ref, w_ref, gu_ref, o_ref):
        da = lax.dot_general(d_ref[...], w_ref[...], NT, preferred_element_type=F32)
        g, u = gu_ref[:, :FF_TILE], gu_ref[:, FF_TILE:]
        sg = _sigmoid(g)
        o_ref[:, :FF_TILE] = (da * u * (sg * (1.0 + g * (1.0 - sg)))).astype(o_ref.dtype)
        o_ref[:, FF_TILE:] = (da * (g * sg)).astype(o_ref.dtype)

    gu_spec = pl.BlockSpec((tm, 2 * FF_TILE), lambda j, i: (i, j))
    return pl.pallas_call(
        body, name="d_act", grid=(2, s_ // tm),
        in_specs=[pl.BlockSpec((tm, D_MODEL), lambda j, i: (i, 0)), pl.BlockSpec((FF_TILE, D_MODEL), lambda j, i: (j, 0)), gu_spec],
        out_specs=gu_spec, out_shape=jax.ShapeDtypeStruct((s_, 2 * D_FF), MXU_DTYPE),
        compiler_params=_cparams(("parallel", "parallel")),
    )(dx2b, wd, gu)


def _ffn_out_loss(act, wd, x1, g, tgt, *, ts=512):
    s_, c = x1.shape
    kk = act.shape[1]

    def body(a_ref, w_ref, x_ref, g_ref, t_ref, dx_ref, dxb_ref, dg_ref, lp_ref, tot_ref):
        v = x_ref[...] + jnp.dot(a_ref[...], w_ref[...], preferred_element_type=F32)
        r = lax.rsqrt(jnp.mean(v * v, axis=-1, keepdims=True) + EPS)
        xh = v * r
        gg = g_ref[...]
        e = xh * gg - t_ref[...]
        do = e * (1.0 / c)
        dxh = do * gg
        dx = r * (dxh - xh * jnp.mean(dxh * xh, axis=-1, keepdims=True))
        dx_ref[...] = dx
        dxb_ref[...] = dx.astype(dxb_ref.dtype)
        i = pl.program_id(0)

        @pl.when(i == 0)
        def _():
            dg_ref[...] = jnp.zeros(dg_ref.shape, F32)
            lp_ref[...] = jnp.zeros(lp_ref.shape, F32)

        dg_ref[...] += _sublane_sum(do * xh)
        lp_ref[...] += _sublane_sum(e * e)
        tot_ref[...] = jnp.full(tot_ref.shape, (0.5 / c) * jnp.sum(lp_ref[...]), F32)

    return pl.pallas_call(
        body, name="ffn_out_loss", grid=(s_ // ts,),
        in_specs=[_rows(ts, kk), _const(kk, c), _rows(ts, c), _const(1, c), _rows(ts, c)],
        out_specs=[_rows(ts, c), _rows(ts, c), _const(SUBLANES, c), _const(SUBLANES, c), _const(SUBLANES, LANES)],
        out_shape=[jax.ShapeDtypeStruct((s_, c), F32), jax.ShapeDtypeStruct((s_, c), MXU_DTYPE),
                   jax.ShapeDtypeStruct((SUBLANES, c), F32), jax.ShapeDtypeStruct((SUBLANES, c), F32),
                   jax.ShapeDtypeStruct((SUBLANES, LANES), F32)],
        compiler_params=_cparams(("arbitrary",)),
    )(act, wd, x1, g, tgt)


def _mla_d_out(dtb, wob_t, o32, *, ts=512):
    s_ = dtb.shape[0]

    def body(dt_ref, w_ref, o_ref, dob_ref, dl_ref):
        d = jnp.dot(dt_ref[...], w_ref[...], preferred_element_type=F32)
        dob_ref[...] = d.astype(dob_ref.dtype)
        prod = d * o_ref[...]
        for h in range(MLA_HEADS):
            dl_ref[h] = jnp.sum(prod[:, h * LANES:(h + 1) * LANES].T, axis=0, keepdims=True)

    return pl.pallas_call(
        body, name="mla_d_out", grid=(s_ // ts,), in_specs=[_rows(ts, 1024), _const(1024, 1024), _rows(ts, 1024)],
        out_specs=[_rows(ts, 1024), pl.BlockSpec((MLA_HEADS, 1, ts), lambda i: (0, 0, i))],
        out_shape=[jax.ShapeDtypeStruct((s_, 1024), MXU_DTYPE), jax.ShapeDtypeStruct((MLA_HEADS, 1, s_), F32)],
        compiler_params=_cparams(("parallel",)),
    )(dtb, wob_t, o32)


SWA_T = 4 * BLOCK


SWA_W = SWA_GROUP * BLOCK


def _swa_masks(sb):
    kr = lax.broadcasted_iota(jnp.int32, (2 * BLOCK, SWA_W), 0)
    qc = jnp.bitwise_and(lax.broadcasted_iota(jnp.int32, (2 * BLOCK, SWA_W), 1), BLOCK - 1)
    band = jnp.logical_and(kr > qc, kr <= qc + BLOCK)
    first = jnp.logical_and(band, kr >= BLOCK)
    return band, jnp.logical_or(first, jnp.logical_and(band, sb > 0))


def _heads_to_rows(ref, rs):
    return jnp.concatenate([ref[rs, h * LANES:(h + 1) * LANES] for h in range(SWA_GROUP)], axis=0)


def _sink_row(sk_ref):
    return jnp.concatenate([sk_ref[0, h:h + 1, :] for h in range(SWA_GROUP)], axis=1) * LOG2E


def _swa_in_specs(rev, nsb):
    sbi = (lambda j: nsb - 1 - j) if rev else (lambda j: j)
    cur = pl.BlockSpec((SWA_T, LANES), lambda g, j: (sbi(j), g))
    prev = pl.BlockSpec((BLOCK, LANES), lambda g, j: (jnp.maximum(4 * sbi(j) - 1, 0), g))
    q = pl.BlockSpec((SWA_T, SWA_GROUP * LANES), lambda g, j: (sbi(j), g))
    sink = pl.BlockSpec((1, SUBLANES, LANES), lambda g, j: (g, 0, 0))
    lse = pl.BlockSpec((SWA_GROUP, 1, SWA_T), lambda g, j: (g, 0, sbi(j)))
    return q, cur, prev, sink, lse


def _swa_fwd(qa, ka, va, sink_b):
    s_ = qa.shape[0]
    nsb = s_ // SWA_T
    c2 = HEAD_DIM ** -0.5 * LOG2E

    def body(q_ref, kc_ref, kp_ref, vc_ref, vp_ref, sk_ref, o32_ref, o16_ref, lse_ref, kx, vx):
        kx[0:BLOCK, :] = kp_ref[...]
        kx[BLOCK:5 * BLOCK, :] = kc_ref[...]
        vx[0:BLOCK, :] = vp_ref[...]
        vx[BLOCK:5 * BLOCK, :] = vc_ref[...]
        band, band0 = _swa_masks(pl.program_id(1))
        sink2 = _sink_row(sk_ref)
        for b in range(4):
            rs = slice(b * BLOCK, (b + 1) * BLOCK)
            ks = slice(b * BLOCK, (b + 2) * BLOCK)
            st = lax.dot_general(kx[ks, :], _heads_to_rows(q_ref, rs), NT, preferred_element_type=F32) * c2
            st = jnp.where(band0 if b == 0 else band, st, -jnp.inf)
            m = jnp.maximum(jnp.max(st, axis=0, keepdims=True), sink2)
            pt = jnp.exp2(st - m)
            den = jnp.sum(pt, axis=0, keepdims=True) + jnp.exp2(sink2 - m)
            o = lax.dot_general((pt * (1.0 / den)).astype(MXU_DTYPE), vx[ks, :], TN, preferred_element_type=F32)
            lse = m + jnp.log2(den)
            for hh in range(SWA_GROUP):
                cs = slice(hh * LANES, (hh + 1) * LANES)
                o32_ref[rs, cs] = o[cs, :]
                o16_ref[rs, cs] = o[cs, :].astype(o16_ref.dtype)
                lse_ref[hh, :, rs] = lse[:, cs]

    q, cur, prev, sink, lse_spec = _swa_in_specs(False, nsb)
    return pl.pallas_call(
        body, name="swa_fwd", grid=(SWA_KV_HEADS, nsb), in_specs=[q, cur, prev, cur, prev, sink],
        out_specs=[q, q, lse_spec],
        out_shape=[jax.ShapeDtypeStruct((s_, SWA_HEADS * LANES), F32), jax.ShapeDtypeStruct((s_, SWA_HEADS * LANES), MXU_DTYPE),
                   jax.ShapeDtypeStruct((SWA_HEADS, 1, s_), F32)],
        scratch_shapes=[pltpu.VMEM((5 * BLOCK, LANES), MXU_DTYPE), pltpu.VMEM((5 * BLOCK, LANES), MXU_DTYPE)],
        compiler_params=_cparams(("parallel", "arbitrary")),
    )(qa, ka, ka, va, va, sink_b)


def _swa_bwd(qa, ka, va, sink_b, o32, do, lse):
    s_ = qa.shape[0]
    nsb = s_ // SWA_T
    scale = HEAD_DIM ** -0.5
    c2 = scale * LOG2E

    def body(q_ref, kc_ref, kp_ref, vc_ref, vp_ref, sk_ref, o_ref, do_ref, lse_ref,
             dq_ref, dk_ref, dv_ref, dsk_ref, kx, vx, kacc, vacc, kcar, vcar):
        j = pl.program_id(1)
        kx[0:BLOCK, :] = kp_ref[...]
        kx[BLOCK:5 * BLOCK, :] = kc_ref[...]
        vx[0:BLOCK, :] = vp_ref[...]
        vx[BLOCK:5 * BLOCK, :] = vc_ref[...]
        band, band0 = _swa_masks(nsb - 1 - j)
        kacc[...] = jnp.zeros(kacc.shape, F32)
        vacc[...] = jnp.zeros(vacc.shape, F32)

        @pl.when(j == 0)
        def _():
            kcar[...] = jnp.zeros(kcar.shape, F32)
            vcar[...] = jnp.zeros(vcar.shape, F32)
            dsk_ref[...] = jnp.zeros(dsk_ref.shape, F32)

        sink2 = _sink_row(sk_ref)
        dsink = jnp.zeros((1, SWA_W), F32)
        for b in range(4):
            rs = slice(b * BLOCK, (b + 1) * BLOCK)
            ks = slice(b * BLOCK, (b + 2) * BLOCK)
            q, k2, v2 = _heads_to_rows(q_ref, rs), kx[ks, :], vx[ks, :]
            d = _heads_to_rows(do_ref, rs)
            delta = jnp.sum((d * _heads_to_rows(o_ref, rs)).T, axis=0, keepdims=True)
            l2 = jnp.concatenate([lse_ref[hh, :, rs] for hh in range(SWA_GROUP)], axis=1)
            st = lax.dot_general(k2, q, NT, preferred_element_type=F32) * c2
            pt = jnp.exp2(jnp.where(band0 if b == 0 else band, st, -jnp.inf) - l2)
            db = d.astype(MXU_DTYPE)
            dst = (pt * (lax.dot_general(v2, db, NT, preferred_element_type=F32) - delta) * scale).astype(MXU_DTYPE)
            dq = lax.dot_general(dst, k2, TN, preferred_element_type=F32)
            for hh in range(SWA_GROUP):
                dq_ref[rs, hh * LANES:(hh + 1) * LANES] = dq[hh * LANES:(hh + 1) * LANES, :]
            kacc[ks, :] += jnp.dot(dst, q, preferred_element_type=F32)
            vacc[ks, :] += jnp.dot(pt.astype(MXU_DTYPE), db, preferred_element_type=F32)
            dsink = dsink - jnp.exp2(sink2 - l2) * delta
        for hh in range(SWA_GROUP):
            tot = jnp.sum(dsink[:, hh * LANES:(hh + 1) * LANES], axis=1, keepdims=True)
            dsk_ref[0, hh:hh + 1, :] += jnp.broadcast_to(tot, (1, LANES))

        dk_ref[0:3 * BLOCK, :] = kacc[BLOCK:4 * BLOCK, :]
        dk_ref[3 * BLOCK:4 * BLOCK, :] = kacc[4 * BLOCK:5 * BLOCK, :] + kcar[...]
        dv_ref[0:3 * BLOCK, :] = vacc[BLOCK:4 * BLOCK, :].astype(dv_ref.dtype)
        dv_ref[3 * BLOCK:4 * BLOCK, :] = (vacc[4 * BLOCK:5 * BLOCK, :] + vcar[...]).astype(dv_ref.dtype)
        kcar[...] = kacc[0:BLOCK, :]
        vcar[...] = vacc[0:BLOCK, :]

    q, cur, prev, sink, lse_spec = _swa_in_specs(True, nsb)
    return pl.pallas_call(
        body, name="swa_bwd", grid=(SWA_KV_HEADS, nsb),
        in_specs=[q, cur, prev, cur, prev, sink, q, q, lse_spec],
        out_specs=[q, cur, cur, sink],
        out_shape=[jax.ShapeDtypeStruct((s_, SWA_HEADS * LANES), F32), jax.ShapeDtypeStruct((s_, SWA_KV_HEADS * LANES), F32),
                   jax.ShapeDtypeStruct((s_, SWA_KV_HEADS * LANES), MXU_DTYPE),
                   jax.ShapeDtypeStruct((SWA_KV_HEADS, SUBLANES, LANES), F32)],
        scratch_shapes=[pltpu.VMEM((5 * BLOCK, LANES), MXU_DTYPE), pltpu.VMEM((5 * BLOCK, LANES), MXU_DTYPE),
                        pltpu.VMEM((5 * BLOCK, LANES), F32), pltpu.VMEM((5 * BLOCK, LANES), F32),
                        pltpu.VMEM((BLOCK, LANES), F32), pltpu.VMEM((BLOCK, LANES), F32)],
        compiler_params=_cparams(("arbitrary", "arbitrary")),
    )(qa, ka, ka, va, va, sink_b, o32, do, lse)


MLA_T = 512
MLA_FWD_GROUP = 4
MLA_BWD_GROUP = 2


def _mla_specs(s_, t, group):
    w = group * LANES
    qs = pl.BlockSpec((t, w), lambda g, i: (i, g))
    kv = pl.BlockSpec((s_, w), lambda g, i: (0, g))
    row = pl.BlockSpec((group, 1, t), lambda g, i: (g, 0, i))
    return qs, kv, row


def _causal_scores_t(k, q, t, c2, masked):
    st = lax.dot_general(k, q, NT, preferred_element_type=F32) * c2
    if masked:
        kr = lax.broadcasted_iota(jnp.int32, (t, t), 0)
        qc = lax.broadcasted_iota(jnp.int32, (t, t), 1)
        st = jnp.where(kr <= qc, st, -jnp.inf)
    return st


def _mla_fwd(qc, kc, vp):
    s_ = qc.shape[0]
    t = min(MLA_T, s_)
    c2 = MLA_QK ** -0.5 * LOG2E
    grp = MLA_FWD_GROUP

    def body(q_ref, k_ref, v_ref, o32_ref, o16_ref, lse_ref, m_s, acc_s):
        qi = pl.program_id(1)
        m_s[...] = jnp.full(m_s.shape, -jnp.inf, F32)
        acc_s[...] = jnp.zeros(acc_s.shape, F32)
        ones_lane = lax.broadcasted_iota(jnp.int32, (t, LANES), 1) == MLA_V

        def step(ki, masked):
            off = pl.multiple_of(ki * t, t)
            for g in range(grp):
                cs = slice(g * LANES, (g + 1) * LANES)
                st = _causal_scores_t(k_ref[pl.ds(off, t), cs], q_ref[:, cs], t, c2, masked)
                m_old = m_s[g]
                m_new = jnp.maximum(m_old, jnp.max(st, axis=0, keepdims=True))
                alpha = jnp.exp2(m_old - m_new)
                pt = jnp.exp2(st - m_new).astype(MXU_DTYPE)
                v = v_ref[pl.ds(off, t), cs]
                v = jnp.where(ones_lane, jnp.ones((), v.dtype), v)
                acc_s[g] = alpha * acc_s[g] + lax.dot_general(v, pt, TN, preferred_element_type=F32)
                m_s[g] = m_new

        def full_block(ki, carry):
            step(ki, False)
            return carry

        lax.fori_loop(0, qi, full_block, 0)
        step(qi, True)
        for g in range(grp):
            cs = slice(g * LANES, (g + 1) * LANES)
            acc = acc_s[g]
            l = acc[MLA_V:MLA_V + 1, :]
            o = (acc * (1.0 / l)).T
            o32_ref[:, cs] = o
            o16_ref[:, cs] = o.astype(o16_ref.dtype)
            lse_ref[g] = m_s[g] + jnp.log2(l)

    qs, kv, row = _mla_specs(s_, t, grp)
    return pl.pallas_call(
        body, name="mla_fwd", grid=(MLA_HEADS // grp, s_ // t), in_specs=[qs, kv, kv], out_specs=[qs, qs, row],
        out_shape=[jax.ShapeDtypeStruct((s_, MLA_HEADS * LANES), F32), jax.ShapeDtypeStruct((s_, MLA_HEADS * LANES), MXU_DTYPE),
                   jax.ShapeDtypeStruct((MLA_HEADS, 1, s_), F32)],
        scratch_shapes=[pltpu.VMEM((grp, 1, t), F32), pltpu.VMEM((grp, LANES, t), F32)],
        compiler_params=_cparams(("parallel", "arbitrary")),
    )(qc, kc, vp)


def _mla_bwd(qc, kc, vp, dob, lse, delta):
    s_ = qc.shape[0]
    t = min(MLA_T, s_)
    scale = MLA_QK ** -0.5
    c2 = scale * LOG2E
    grp = MLA_BWD_GROUP

    def body(q_ref, do_ref, lse_ref, dl_ref, k_ref, v_ref, dq_ref, dk_ref, dv_ref, dqt_s):
        qi = pl.program_id(1)

        @pl.when(qi == 0)
        def _():
            dk_ref[...] = jnp.zeros(dk_ref.shape, F32)
            dv_ref[...] = jnp.zeros(dv_ref.shape, F32)

        dqt_s[...] = jnp.zeros(dqt_s.shape, F32)

        def step(ki, masked):
            off = pl.multiple_of(ki * t, t)
            for g in range(grp):
                cs = slice(g * LANES, (g + 1) * LANES)
                q, d, k = q_ref[:, cs], do_ref[:, cs], k_ref[pl.ds(off, t), cs]
                pt = jnp.exp2(_causal_scores_t(k, q, t, c2, masked) - lse_ref[g])
                dpt = lax.dot_general(v_ref[pl.ds(off, t), cs], d, NT, preferred_element_type=F32)
                dst = (pt * (dpt - dl_ref[g]) * scale).astype(MXU_DTYPE)
                dv_ref[pl.ds(off, t), cs] += jnp.dot(pt.astype(MXU_DTYPE), d, preferred_element_type=F32)
                dk_ref[pl.ds(off, t), cs] += jnp.dot(dst, q, preferred_element_type=F32)
                dqt_s[g] += lax.dot_general(k, dst, TN, preferred_element_type=F32)

        def full_block(ki, carry):
            step(ki, False)
            return carry

        lax.fori_loop(0, qi, full_block, 0)
        step(qi, True)
        for g in range(grp):
            dq_ref[:, g * LANES:(g + 1) * LANES] = dqt_s[g].T

    qs, kv, row = _mla_specs(s_, t, grp)
    shp = jax.ShapeDtypeStruct((s_, MLA_HEADS * LANES), F32)
    return pl.pallas_call(
        body, name="mla_bwd", grid=(MLA_HEADS // grp, s_ // t), in_specs=[qs, qs, row, row, kv, kv],
        out_specs=[qs, kv, kv], out_shape=[shp, shp, shp], scratch_shapes=[pltpu.VMEM((grp, LANES, t), F32)],
        compiler_params=_cparams(("parallel", "arbitrary")),
    )(qc, dob, lse, delta, kc, vp)


def _pad_heads(w, nh, hd, axis):
    shp = w.shape
    w = w.reshape(shp[:axis] + (nh, hd) + shp[axis + 1:])
    pad = [(0, 0)] * w.ndim
    pad[axis + 1] = (0, LANES - hd)
    w = jnp.pad(w, pad)
    return w.reshape(shp[:axis] + (nh * LANES,) + shp[axis + 1:])


def _unpad_heads(w, nh, hd, axis):
    shp = w.shape
    w = w.reshape(shp[:axis] + (nh, LANES) + shp[axis + 1:])
    w = lax.slice_in_dim(w, 0, hd, axis=axis + 1)
    return w.reshape(shp[:axis] + (nh * hd,) + shp[axis + 1:])


PACK_W = 1024
ROW_TILE = 16
FULL_SHAPE = dict(w_in=(1024, 3488), w_uq=(384, 768), w_ukv=(256, 1024), w_o_swa=(512, 1024), w_o_mla=(512, 1024),
                  w_out=(1024, 1024), w_gate=(1024, 2816), w_up=(1024, 2816), w_down=(2816, 1024))
BIG = tuple(FULL_SHAPE)
ROW_SHARDED = ("w_out", "w_down")
W_IN_COLS = FULL_SHAPE["w_in"][1] // N_DEV
W_IN_ROWS = -(-W_IN_COLS // ROW_TILE) * ROW_TILE
FF_COLS = D_FF // N_DEV
OUT_ROWS = D_MODEL // N_DEV
SMALL_FLAT = (("w_uq", 0, 36), ("w_ukv", 48, 32))
SMALL_USED = 80
MID_BLOCKS = 4
MID_ROWS = MID_BLOCKS * OUT_ROWS
EARLY_ROWS = W_IN_ROWS + MID_ROWS
LATE_ROWS = 3 * FF_COLS
PACK_ROWS = EARLY_ROWS + LATE_ROWS


def _shard_shape(n):
    r, c = FULL_SHAPE[n]
    return (r // N_DEV, c) if n in ROW_SHARDED else (r, c // N_DEV)


def _wire_pack(sh, dtype):
    c = lambda n: sh[n].astype(dtype)
    rows = [jnp.pad(c("w_in").T, ((0, W_IN_ROWS - W_IN_COLS), (0, 0))), c("w_out"),
            _pad_heads(c("w_o_swa").T, SWA_HEADS, HEAD_DIM, 1), _pad_heads(c("w_o_mla").T, MLA_HEADS, MLA_V, 1)]
    for n, _, r in SMALL_FLAT:
        rows.append(jnp.pad(c(n).reshape(r, PACK_W), ((0, -r % ROW_TILE), (0, 0))))
    rows.append(jnp.zeros((OUT_ROWS - SMALL_USED, PACK_W), dtype))
    return jnp.concatenate(rows + [c("w_gate").T, c("w_up").T, c("w_down")], 0)


def _mid_unpack(p):
    out = dict(w_out=p[0:OUT_ROWS], w_o_swa=_unpad_heads(p[OUT_ROWS:2 * OUT_ROWS], SWA_HEADS, HEAD_DIM, 1).T,
               w_o_mla=_unpad_heads(p[2 * OUT_ROWS:3 * OUT_ROWS], MLA_HEADS, MLA_V, 1).T)
    for n, off, r in SMALL_FLAT:
        out[n] = p[3 * OUT_ROWS + off:3 * OUT_ROWS + off + r].reshape(_shard_shape(n))
    return out


def _w_in_row_maps():
    sp = lambda col: (col // W_IN_COLS) * W_IN_ROWS + col % W_IN_COLS
    fwd = np.full((P_W,), -1, np.int64)

    def put(t0, c0, n):
        fwd[t0:t0 + n] = [sp(c) for c in range(c0, c0 + n)]

    put(P_GA, IN_OFF[6], D_MODEL)
    put(P_GB, IN_OFF[7], D_MODEL)
    for h in range(SWA_HEADS):
        put(P_Q + LANES * h, IN_OFF[0] + HEAD_DIM * h, HEAD_DIM)
    put(P_QLAT, IN_OFF[3], Q_LORA)
    put(P_KR + KR_LANE, IN_OFF[5], MLA_ROPE)
    for h in range(SWA_KV_HEADS):
        put(P_K + LANES * h, IN_OFF[1] + HEAD_DIM * h, HEAD_DIM)
        put(P_V + LANES * h, IN_OFF[2] + HEAD_DIM * h, HEAD_DIM)
    put(P_KVLAT, IN_OFF[4], KV_LORA)
    inv = np.full((N_DEV * W_IN_ROWS,), -1, np.int64)
    inv[fwd[fwd >= 0]] = np.nonzero(fwd >= 0)[0]
    return fwd, inv


def _take_rows(src, idx, *, name, tile=2 * LANES):
    n_out, n_src, width = len(idx), src.shape[0], src.shape[1]
    assert n_out % tile == 0 and n_src % tile == 0
    n_tiles = n_out // tile
    blocks = [sorted({int(v) // tile for v in idx[i * tile:(i + 1) * tile] if v >= 0}) for i in range(n_tiles)]
    k_max = max(1, max(len(b) for b in blocks))
    tab = np.zeros((n_tiles, k_max), np.int32)
    sel = np.zeros((n_tiles, k_max, tile, tile), np.float32)
    for i, blks in enumerate(blocks):
        for m, b in enumerate(blks):
            tab[i, m] = b
            for r in range(tile):
                v = int(idx[i * tile + r])
                if v >= 0 and v // tile == b:
                    sel[i, m, r, v % tile] = 1.0

    def body(tab_ref, sel_ref, *refs):
        o_ref = refs[k_max]
        acc = jnp.dot(sel_ref[0, 0], refs[0][...], preferred_element_type=F32)
        for m in range(1, k_max):
            acc = acc + jnp.dot(sel_ref[0, m], refs[m][...], preferred_element_type=F32)
        o_ref[...] = acc.astype(o_ref.dtype)

    def src_spec(m):
        return pl.BlockSpec((tile, width), lambda i, t: (t[i * k_max + m], 0))

    return pl.pallas_call(
        body, name=name,
        grid_spec=pltpu.PrefetchScalarGridSpec(
            num_scalar_prefetch=1, grid=(n_tiles,),
            in_specs=[pl.BlockSpec((1, k_max, tile, tile), lambda i, t: (i, 0, 0, 0))] + [src_spec(m) for m in range(k_max)],
            out_specs=pl.BlockSpec((tile, width), lambda i, t: (i, 0))),
        out_shape=jax.ShapeDtypeStruct((n_out, width), src.dtype),
        compiler_params=_cparams(("parallel",)),
    )(jnp.asarray(tab.reshape(-1)), jnp.asarray(sel, src.dtype), *([src] * k_max))


def _w_in_operand(win_g):
    return _take_rows(win_g.reshape(N_DEV * W_IN_ROWS, PACK_W), _w_in_row_maps()[0], name="w_in_rows")


def _mid_operands(wout_g, woa_g, wob_g, small_g):
    def full(n, off, r):
        a = small_g[:, off:off + r].reshape((N_DEV,) + _shard_shape(n))
        return jnp.moveaxis(a, 0, 1).reshape(FULL_SHAPE[n])

    w = {n: full(n, off, r) for n, off, r in SMALL_FLAT}
    ukv = w["w_ukv"].reshape(KV_LORA, MLA_HEADS, MLA_NOPE + MLA_V)
    return dict(
        wout=wout_g.reshape(D_MODEL, D_MODEL), woa_t=woa_g.reshape(D_MODEL, -1), wob_t=wob_g.reshape(D_MODEL, -1),
        wuq=_pad_heads(w["w_uq"], MLA_HEADS, MLA_QK, 1),
        wuk=_pad_heads(ukv[:, :, :MLA_NOPE].reshape(KV_LORA, -1), MLA_HEADS, MLA_NOPE, 1),
        wuv=_pad_heads(ukv[:, :, MLA_NOPE:].reshape(KV_LORA, -1), MLA_HEADS, MLA_V, 1),
    )


def _mid_grad_pack(g):
    uk = _unpad_heads(g["wukv"][:, :1024], MLA_HEADS, MLA_NOPE, 1).reshape(KV_LORA, MLA_HEADS, MLA_NOPE)
    uv = _unpad_heads(g["wukv"][:, 1024:], MLA_HEADS, MLA_V, 1).reshape(KV_LORA, MLA_HEADS, MLA_V)
    w = dict(w_uq=_unpad_heads(g["wuq"], MLA_HEADS, MLA_QK, 1), w_ukv=jnp.concatenate([uk, uv], 2).reshape(KV_LORA, -1))
    rows = []
    for n, _, r in SMALL_FLAT:
        rr, cc = FULL_SHAPE[n]
        a = jnp.moveaxis(w[n].reshape(rr, N_DEV, cc // N_DEV), 1, 0).reshape(N_DEV, r, PACK_W)
        rows.append(jnp.pad(a, ((0, 0), (0, -r % ROW_TILE), (0, 0))).astype(WIRE_DTYPE))
    rows.append(jnp.zeros((N_DEV, OUT_ROWS - SMALL_USED, PACK_W), WIRE_DTYPE))
    blk = lambda a: a.reshape(N_DEV, OUT_ROWS, PACK_W)
    return [blk(g["wout"]), blk(g["woa_t"]), blk(g["wob_t"]), jnp.concatenate(rows, 1)]


def _w_in_grad_chunks(g_win_t):
    return _take_rows(g_win_t, _w_in_row_maps()[1], name="dw_in_rows").reshape(N_DEV, W_IN_ROWS, PACK_W)


def _local_step(x, tgt, win_t, small, weights, grads):
    s_ = x.shape[0]
    tabs = _rope_tables(s_)
    sink_b = jnp.broadcast_to(small["swa_sinks"].reshape(SWA_KV_HEADS, SWA_GROUP, 1), (SWA_KV_HEADS, SWA_GROUP, LANES))
    sink_b = jnp.pad(sink_b, ((0, 0), (0, SUBLANES - SWA_GROUP), (0, 0)))

    h, qa, ka, va, cq, ckv, kro, p = _proj_in(x, small["mix_norm_g"], win_t, small["q_norm_g"], small["kv_norm_g"], tabs)
    oa32, oa16, lse_a = _swa_fwd(qa, ka, va, sink_b)
    ops = weights.mid(oa16)
    qc, kc, vp = _mla_up(cq, ckv, kro, ops["wuq"], ops["wuk"], ops["wuv"], tabs)
    ob32, ob16, lse_b = _mla_fwd(qc, kc, vp)
    y = _attn_out_gate(oa16, ob16, ops["woa_t"], ops["wob_t"], p)
    x1 = _mm(y, ops["wout"], "nn", name="out_proj", add=x, tm=1024, tn=1024)
    wgu_t, wd = weights.late(x1)
    h2, gu, act = _ffn_in_act(x1, small["ffn_norm_g"], wgu_t)

    dx2, dx2b, dg3, _, tot = _ffn_out_loss(act, wd, x1, small["final_norm_g"].reshape(1, D_MODEL), tgt)
    g = {}
    g_wd = _mm(act, dx2b, "tn", name="dw_down", tm=FF_TILE, tn=1024, tk=2048, out_dtype=WIRE_DTYPE)
    dgu = _d_act_swiglu(dx2b, wd, gu)
    g_wgu = _mm(dgu, h2, "tn", name="dw_ffn_in", tm=FF_TILE, tn=1024, tk=2048, out_dtype=WIRE_DTYPE)
    token = grads.late(g_wgu, g_wd)
    dx1, dx1b, dg2 = _mm_norm_bwd(dgu, wgu_t, x1, small["ffn_norm_g"] + token[0:1, 0:1], dx2, name="d_h2")
    g["wout"] = _mm(y, dx1b, "tn", name="dw_out", tm=1024, tn=1024, tk=2048, out_dtype=WIRE_DTYPE)
    dta, dtb, dgab = _d_y_gate(dx1b, ops["wout"], p, oa16, ob16, ops["woa_t"], ops["wob_t"])
    doa = _mm(dta, ops["woa_t"], "nn", name="d_oa", tm=1024, tn=1024)
    g["woa_t"] = _mm(dta, oa16, "tn", name="dw_o_swa", tm=1024, tn=1024, tk=2048, out_dtype=WIRE_DTYPE)
    g["wob_t"] = _mm(dtb, ob16, "tn", name="dw_o_mla", tm=1024, tn=1024, tk=2048, out_dtype=WIRE_DTYPE)
    dob16, delta_b = _mla_d_out(dtb, ops["wob_t"], ob32)
    dqc, dkc, dvp = _mla_bwd(qc, kc, vp, dob16, lse_b, delta_b)
    dqp, dkv, dkr, dqlat, dkvlat, dgq, dgkv = _mla_up_bwd(
        dqc, dkc, dvp, ops["wuq"], jnp.concatenate([ops["wuk"], ops["wuv"]], 1), p, small["q_norm_g"], small["kv_norm_g"], tabs)
    g["wuq"] = _mm(cq, dqp, "tn", name="dw_uq", tm=Q_LORA, tn=1024, tk=2048)
    g["wukv"] = _mm(ckv, dkv, "tn", name="dw_ukv", tm=KV_LORA, tn=2048, tk=2048)
    token = grads.mid(g)
    dqa, dka, dva, dsk = _swa_bwd(qa, ka, va, sink_b + token[0:1, 0:1], oa32, doa, lse_a)
    dp = _assemble_dp(dgab, dqa, dqlat, dkr, dka, dva, dkvlat, tabs)
    token = grads.last(_mm(dp, h, "tn", name="dw_in", tm=2176, tn=1024, tk=1024, out_dtype=WIRE_DTYPE))
    gx, _, dg1 = _mm_norm_bwd(dp, win_t, x, small["mix_norm_g"], dx1, name="d_h", after=token)

    sm = dict(mix_norm_g=dg1, ffn_norm_g=dg2, final_norm_g=dg3, q_norm_g=dgq, kv_norm_g=dgkv,
              swa_sinks=dsk[:, :SWA_GROUP, 0].reshape(1, SWA_HEADS))
    return tot, gx, sm


MESH = pl.DeviceIdType.MESH
ANY = pl.BlockSpec(memory_space=pl.ANY)


def _position():
    return lax.axis_index("x"), lax.axis_index("y"), lax.axis_index("c")


def _all_gather(block, pieces, shapes, *, name):
    n_out = len(shapes)
    n_rows = sum(p[3] for p in pieces)

    def body(x_ref, *refs):
        outs, (send_sems, recv_sems, local_sem) = refs[:n_out], refs[n_out:]
        x, y, c = _position()
        me, sibling = (x, y, c), (x, y, 1 - c)
        chips = [(1 - x, y), (x, 1 - y), (1 - x, 1 - y)]

        def dst(piece, blk):
            arr, lead, _, _ = piece
            return outs[arr].at[lead(4 * blk[0] + 2 * blk[1] + blk[2])]

        def own(piece):
            return x_ref.at[pl.ds(piece[2], piece[3])]

        def copies(k, blk, to, from_input):
            return [pltpu.make_async_remote_copy(
                src_ref=own(p) if from_input else dst(p, blk), dst_ref=dst(p, blk), send_sem=send_sems.at[k],
                recv_sem=recv_sems.at[k], device_id=to, device_id_type=MESH) for p in pieces]

        gathered_rows = x_ref.at[pl.ds(0, n_rows)]

        def whole_block(k):
            return pltpu.make_async_remote_copy(src_ref=gathered_rows, dst_ref=gathered_rows, send_sem=send_sems.at[k],
                                                recv_sem=recv_sems.at[k], device_id=me, device_id_type=MESH)

        for p in pieces:
            pltpu.make_async_copy(own(p), dst(p, me), local_sem).start()
        for cp in copies(0, me, sibling, True):
            cp.start()
        for j, chip in enumerate(chips):
            for cp in copies(1 + j, me, (*chip, c), True):
                cp.start()
        for j, chip in enumerate(chips):
            whole_block(1 + j).wait_recv()
            for cp in copies(4 + j, (*chip, c), sibling, False):
                cp.start()
        whole_block(0).wait_recv()
        for j in range(3):
            whole_block(4 + j).wait_recv()
        for k in range(7):
            whole_block(k).wait_send()
        pltpu.make_async_copy(gathered_rows, gathered_rows, local_sem).wait()

    return pl.pallas_call(
        body, name=name, out_shape=[jax.ShapeDtypeStruct(s, block.dtype) for s in shapes], in_specs=[ANY],
        out_specs=[ANY] * n_out,
        scratch_shapes=[pltpu.SemaphoreType.DMA((7,)), pltpu.SemaphoreType.DMA((7,)), pltpu.SemaphoreType.DMA],
    )(block)


HBM = pl.BlockSpec(memory_space=pltpu.HBM)
SEM = pl.BlockSpec(memory_space=pltpu.SEMAPHORE)
TILE_DEVS = FF_TILE // FF_COLS
GU_SHAPE = (2, 2, TILE_DEVS, FF_COLS, PACK_W)


def _gate_slab(d):
    return (d // TILE_DEVS, 0, d % TILE_DEVS)


def _up_slab(d):
    return (d // TILE_DEVS, 1, d % TILE_DEVS)
D_SHAPE = (N_DEV, FF_COLS, PACK_W)
LAND_SHAPE = (N_DEV, LATE_ROWS, PACK_W)


def _split_params():
    return pltpu.CompilerParams(has_side_effects=pltpu.SideEffectType.DATAFLOW_SIDE_EFFECTING)


def _peer(x, y, c, k):
    return ((1 - x) if k & 4 else x, (1 - y) if k & 2 else y, (1 - c) if k & 1 else c)


def _empty_hbm(shape, dtype):
    return pltpu.with_memory_space_constraint(lax.empty(shape, dtype), pltpu.HBM)


def _wait_all(rows, send_sems, recv_sems, me):
    for k in range(N_DEV - 1):
        cp = pltpu.make_async_remote_copy(src_ref=rows, dst_ref=rows, send_sem=send_sems.at[k], recv_sem=recv_sems.at[k],
                                          device_id=me, device_id_type=MESH)
        cp.wait_send()
        cp.wait_recv()


def _token_shape():
    return jax.ShapeDtypeStruct((SUBLANES, LANES), F32)


def _gather_start(pack, row0, pieces, shapes, *, name):
    n = len(shapes)

    def body(*refs):
        p_ref, bufs, send_sems, recv_sems, token = refs[0], refs[1:1 + n], refs[1 + n], refs[2 + n], refs[-1]
        x, y, c = _position()
        me = 4 * x + 2 * y + c
        for k in range(1, N_DEV):
            off = row0
            for buf, lead, rows in pieces:
                pltpu.make_async_remote_copy(
                    src_ref=p_ref.at[pl.ds(off, rows)], dst_ref=bufs[buf].at[lead(me)], send_sem=send_sems.at[k - 1],
                    recv_sem=recv_sems.at[k - 1], device_id=_peer(x, y, c, k), device_id_type=MESH).start()
                off += rows
        token[...] = jnp.zeros_like(token)

    sems, dt = pltpu.SemaphoreType.DMA((N_DEV - 1,)), pack.dtype
    return pl.pallas_call(
        body, name=name,
        out_shape=(sems, sems, pltpu.HBM(pack.shape, dt)) + tuple(pltpu.HBM(s, dt) for s in shapes) + (_token_shape(),),
        in_specs=(HBM,) * (1 + n), out_specs=(SEM, SEM) + (HBM,) * (1 + n) + (pl.BlockSpec(memory_space=pltpu.VMEM),),
        input_output_aliases={i: 2 + i for i in range(1 + n)}, compiler_params=_split_params(),
    )(pltpu.with_memory_space_constraint(pack, pltpu.HBM), *[_empty_hbm(s, dt) for s in shapes])


def _gather_wait(started, row0, n_rows, after, *, name):
    send_sems, recv_sems, pack, *bufs = started[:-1]
    n = len(bufs)

    def body(*refs):
        _wait_all(refs[0].at[pl.ds(row0, n_rows)], refs[1 + n], refs[2 + n], _position())

    outs = pl.pallas_call(
        body, name=name, out_shape=tuple(pltpu.HBM(a.shape, a.dtype) for a in (pack, *bufs)),
        in_specs=(HBM,) * (1 + n) + (SEM, SEM, ANY), out_specs=(HBM,) * (1 + n),
        input_output_aliases={i: i for i in range(1 + n)}, compiler_params=_split_params(),
    )(pack, *bufs, send_sems, recv_sems, after)
    return outs[0], outs[1:]


def _scatter_start(srcs, pieces, *, name):
    n = len(srcs)
    land_shape = (N_DEV, sum(p[2] for p in pieces), PACK_W)

    def body(*refs):
        src_refs, land_ref, send_sems, recv_sems, token = refs[:n], refs[n], refs[n + 1], refs[n + 2], refs[-1]
        x, y, c = _position()
        me = 4 * x + 2 * y + c
        for k in range(1, N_DEV):
            px, py, pc = _peer(x, y, c, k)
            off = 0
            for si, lead, rows in pieces:
                pltpu.make_async_remote_copy(
                    src_ref=src_refs[si].at[lead(4 * px + 2 * py + pc)], dst_ref=land_ref.at[me, pl.ds(off, rows)],
                    send_sem=send_sems.at[k - 1], recv_sem=recv_sems.at[k - 1], device_id=(px, py, pc),
                    device_id_type=MESH).start()
                off += rows
        token[...] = jnp.zeros_like(token)

    sems, dt = pltpu.SemaphoreType.DMA((N_DEV - 1,)), srcs[0].dtype
    return pl.pallas_call(
        body, name=name,
        out_shape=(sems, sems) + tuple(pltpu.HBM(a.shape, dt) for a in srcs) + (pltpu.HBM(land_shape, dt), _token_shape()),
        in_specs=(HBM,) * (n + 1), out_specs=(SEM, SEM) + (HBM,) * (n + 1) + (pl.BlockSpec(memory_space=pltpu.VMEM),),
        input_output_aliases={i: 2 + i for i in range(n + 1)}, compiler_params=_split_params(),
    )(*[pltpu.with_memory_space_constraint(a, pltpu.HBM) for a in srcs], _empty_hbm(land_shape, dt))


def _scatter_wait(started, after, *, name):
    send_sems, recv_sems, *bufs = started[:-1]
    n = len(bufs)

    def body(*refs):
        _wait_all(refs[n - 1].at[0], refs[n], refs[n + 1], _position())

    return pl.pallas_call(
        body, name=name, out_shape=tuple(pltpu.HBM(a.shape, a.dtype) for a in bufs),
        in_specs=(HBM,) * n + (SEM, SEM, ANY), out_specs=(HBM,) * n, input_output_aliases={i: i for i in range(n)},
        compiler_params=_split_params(),
    )(*bufs, send_sems, recv_sems, after)


def _peer_sum(own, own_lead, land, block, rows, idx, *, name):
    owns = list(own) if isinstance(own, (list, tuple)) else [own]
    n, lead_rank = len(owns), owns[0].ndim - 2

    def body(idx_ref, *refs):
        own_refs, land_refs, o_ref = refs[:n], refs[n:n + N_DEV - 1], refs[n + N_DEV - 1]
        for j in range(n):
            rs_ = slice(j * rows, (j + 1) * rows)
            acc = own_refs[j][(0,) * lead_rank].astype(F32)
            for k in range(N_DEV - 1):
                acc = acc + land_refs[k][0, rs_].astype(F32)
            o_ref[rs_] = acc

    own_spec = pl.BlockSpec((1,) * lead_rank + (rows, PACK_W), lambda i, t: own_lead(t[0]) + (0, 0))

    def land_spec(k):
        return pl.BlockSpec((1, n * rows, PACK_W), lambda i, t: (t[k + 1], block, 0))

    return pl.pallas_call(
        body, name=name,
        grid_spec=pltpu.PrefetchScalarGridSpec(
            num_scalar_prefetch=1, grid=(1,), in_specs=[own_spec] * n + [land_spec(k) for k in range(N_DEV - 1)],
            out_specs=pl.BlockSpec((n * rows, PACK_W), lambda i, t: (0, 0))),
        out_shape=jax.ShapeDtypeStruct((n * rows, PACK_W), F32), compiler_params=_cparams(("arbitrary",)),
    )(idx, *owns, *([land] * (N_DEV - 1)))


def _sum_adamw(own, own_lead, land, block, rows, idx, w, m, v, *, name):
    lead_rank, r = own.ndim - 2, w.shape[1]

    def body(idx_ref, own_ref, *refs):
        land_refs, (w_ref, m_ref, v_ref), outs = refs[:N_DEV - 1], refs[N_DEV - 1:N_DEV + 2], refs[N_DEV + 2:]
        g = own_ref[(0,) * lead_rank + (slice(0, r),)].astype(F32)
        for k in range(N_DEV - 1):
            g = g + land_refs[k][0, 0:r].astype(F32)
        for o_ref, val in zip(outs, (g,) + tuple(_adamw(w_ref[0], g, m_ref[0], v_ref[0]))):
            o_ref[0] = val

    own_spec = pl.BlockSpec((1,) * lead_rank + (rows, PACK_W), lambda i, t: own_lead(t[0]) + (0, 0))
    shard = pl.BlockSpec((1, r, PACK_W), lambda i, t: (0, 0, 0))

    def land_spec(k):
        return pl.BlockSpec((1, rows, PACK_W), lambda i, t: (t[k + 1], block, 0))

    return pl.pallas_call(
        body, name=name,
        grid_spec=pltpu.PrefetchScalarGridSpec(
            num_scalar_prefetch=1, grid=(1,),
            in_specs=[own_spec] + [land_spec(k) for k in range(N_DEV - 1)] + [shard] * 3, out_specs=[shard] * 4),
        out_shape=[jax.ShapeDtypeStruct((1, r, PACK_W), F32)] * 4, compiler_params=_cparams(("arbitrary",)),
    )(idx, own, *([land] * (N_DEV - 1)), w, m, v)


def _adamw(w, g, m, v):
    m = ADAM_B1 * m + (1.0 - ADAM_B1) * g
    v = ADAM_B2 * v + (1.0 - ADAM_B2) * (g * g)
    m_hat = m / (1.0 - ADAM_B1 ** ADAM_STEP)
    v_hat = v / (1.0 - ADAM_B2 ** ADAM_STEP)
    delta = -ADAM_LR * (m_hat / (jnp.sqrt(v_hat) + ADAM_EPS) + ADAM_WD * w)
    return delta, m, v


def _adamw_call(w, g, m, v, *, name, max_rows=256):
    _, r, c_ = w.shape
    tr = max_rows if r > max_rows and r % max_rows == 0 else r

    def body(w_ref, g_ref, m_ref, v_ref, d_ref, mo_ref, vo_ref):
        d, mn, vn = _adamw(w_ref[0], g_ref[...], m_ref[0], v_ref[0])
        d_ref[0] = d
        mo_ref[0] = mn
        vo_ref[0] = vn

    row3 = pl.BlockSpec((1, tr, c_), lambda i: (0, i, 0))
    shp = jax.ShapeDtypeStruct((1, r, c_), F32)
    return pl.pallas_call(
        body, name=name, grid=(r // tr,), in_specs=[row3, pl.BlockSpec((tr, c_), lambda i: (i, 0)), row3, row3],
        out_specs=[row3] * 3, out_shape=[shp] * 3, compiler_params=_cparams(("parallel",)),
    )(w, g, m, v)


SMALL = ("mix_norm_g", "ffn_norm_g", "final_norm_g", "q_norm_g", "kv_norm_g", "swa_sinks")
SMALL_W = dict(mix_norm_g=1024, ffn_norm_g=1024, final_norm_g=1024, q_norm_g=Q_LORA, kv_norm_g=KV_LORA, swa_sinks=SWA_HEADS)


def _small_adamw(parts, w, m, v):
    ns = len(SMALL)

    def body(p_ref, *refs):
        ins, outs = refs[:3 * ns], refs[3 * ns:]
        tot = p_ref[0]
        for dev in range(1, N_DEV):
            tot = tot + p_ref[dev]
        for k, n in enumerate(SMALL):
            g = jnp.sum(tot[k * SUBLANES:(k + 1) * SUBLANES, :SMALL_W[n]], axis=0, keepdims=True)
            res = _adamw(ins[k][...], g, ins[ns + k][...], ins[2 * ns + k][...])
            for j, r in enumerate((g,) + tuple(res)):
                outs[j * ns + k][...] = r
        outs[4 * ns][...] = jnp.sum(tot[ns * SUBLANES:(ns + 1) * SUBLANES, 0:1], axis=0, keepdims=True)

    shapes = [jax.ShapeDtypeStruct((1, SMALL_W[n]), F32) for n in SMALL]
    vm = pl.BlockSpec(memory_space=pltpu.VMEM)
    out = pl.pallas_call(
        body, name="small_adamw", in_specs=[vm] * (1 + 3 * ns), out_specs=[vm] * (4 * ns + 1),
        out_shape=shapes * 4 + [jax.ShapeDtypeStruct((1, 1), F32)],
    )(parts, *[d[n] for d in (w, m, v) for n in SMALL])
    return [dict(zip(SMALL, out[j * ns:(j + 1) * ns])) for j in range(4)] + [out[4 * ns]]


def _small_pack(d, rows_each):
    parts = [jnp.pad(d[n].astype(F32), ((0, 0), (0, PACK_W - SMALL_W[n]))) for n in SMALL]
    out = jnp.concatenate(parts, 0)
    pad = -out.shape[0] % SUBLANES
    return jnp.pad(out, ((0, pad), (0, 0)))


def kernel(x, mix_norm_g, w_in, swa_sinks, q_norm_g, w_uq, kv_norm_g, w_ukv, w_o_swa, w_o_mla, w_out, ffn_norm_g, w_gate, w_up, w_down, final_norm_g, loss_target, m_mix_norm_g, m_w_in, m_swa_sinks, m_q_norm_g, m_w_uq, m_kv_norm_g, m_w_ukv, m_w_o_swa, m_w_o_mla, m_w_out, m_ffn_norm_g, m_w_gate, m_w_up, m_w_down, m_final_norm_g, v_mix_norm_g, v_w_in, v_swa_sinks, v_q_norm_g, v_w_uq, v_kv_norm_g, v_w_ukv, v_w_o_swa, v_w_o_mla, v_w_out, v_ffn_norm_g, v_w_gate, v_w_up, v_w_down, v_final_norm_g):
    big_w = dict(w_in=w_in[0], w_uq=w_uq[0], w_ukv=w_ukv[0], w_o_swa=w_o_swa[0], w_o_mla=w_o_mla[0], w_out=w_out[0],
                 w_gate=w_gate[0], w_up=w_up[0], w_down=w_down[0])
    big_w3 = dict(w_in=w_in, w_uq=w_uq, w_ukv=w_ukv, w_o_swa=w_o_swa, w_o_mla=w_o_mla, w_out=w_out, w_gate=w_gate, w_up=w_up,
                  w_down=w_down)
    big_m = dict(w_in=m_w_in, w_uq=m_w_uq, w_ukv=m_w_ukv, w_o_swa=m_w_o_swa, w_o_mla=m_w_o_mla, w_out=m_w_out,
                 w_gate=m_w_gate, w_up=m_w_up, w_down=m_w_down)
    big_v = dict(w_in=v_w_in, w_uq=v_w_uq, w_ukv=v_w_ukv, w_o_swa=v_w_o_swa, w_o_mla=v_w_o_mla, w_out=v_w_out,
                 w_gate=v_w_gate, w_up=v_w_up, w_down=v_w_down)
    small_w = dict(mix_norm_g=mix_norm_g, ffn_norm_g=ffn_norm_g, final_norm_g=final_norm_g.reshape(1, D_MODEL),
                   q_norm_g=q_norm_g, kv_norm_g=kv_norm_g, swa_sinks=swa_sinks)
    small_m = dict(mix_norm_g=m_mix_norm_g, ffn_norm_g=m_ffn_norm_g, final_norm_g=m_final_norm_g.reshape(1, D_MODEL),
                   q_norm_g=m_q_norm_g, kv_norm_g=m_kv_norm_g, swa_sinks=m_swa_sinks)
    small_v = dict(mix_norm_g=v_mix_norm_g, ffn_norm_g=v_ffn_norm_g, final_norm_g=v_final_norm_g.reshape(1, D_MODEL),
                   q_norm_g=v_q_norm_g, kv_norm_g=v_kv_norm_g, swa_sinks=v_swa_sinks)

    px, py, pc = _position()
    me = 4 * px + 2 * py + pc
    idx = jnp.stack([me] + [4 * qx + 2 * qy + qc for qx, qy, qc in (_peer(px, py, pc, k) for k in range(1, N_DEV))])
    idx = idx.astype(jnp.int32)

    dev = lambda d: (d,)
    pack = _wire_pack(big_w, WIRE_DTYPE)
    win_g, = _all_gather(pack, ((0, dev, 0, W_IN_ROWS),), ((N_DEV, W_IN_ROWS, PACK_W),), name="ag_early")
    mid_pieces = tuple((b, dev, OUT_ROWS) for b in range(MID_BLOCKS))
    ag_mid = _gather_start(pack, W_IN_ROWS, mid_pieces, ((N_DEV, OUT_ROWS, PACK_W),) * MID_BLOCKS, name="ag_mid_start")
    ag = {}

    def own_rows(r0, r1, shape):
        return pack[r0:r1].reshape(shape)

    def mid_weights(after):
        pack_mid, blocks = _gather_wait(ag_mid, W_IN_ROWS, MID_ROWS, after, name="ag_mid_wait")
        ag["late"] = _gather_start(pack_mid, EARLY_ROWS, ((0, _gate_slab, FF_COLS), (0, _up_slab, FF_COLS), (1, dev, FF_COLS)),
                                   (GU_SHAPE, D_SHAPE), name="ag_late_start")
        row0 = lambda b: W_IN_ROWS + b * OUT_ROWS
        ops = _mid_operands(*[lax.dynamic_update_slice(blk, own_rows(row0(b), row0(b + 1), (1, OUT_ROWS, PACK_W)), (me, 0, 0))
                              for b, blk in enumerate(blocks)])
        ops["wuq"] = ops["wuq"] + ag["late"][-1][0:1, 0:1].astype(ops["wuq"].dtype)
        return ops

    def late_weights(after):
        _, (gu, d) = _gather_wait(ag["late"], EARLY_ROWS, LATE_ROWS, after, name="ag_late_wait")
        slab = (1, 1, 1, FF_COLS, PACK_W)
        gu = lax.dynamic_update_slice(gu, own_rows(EARLY_ROWS, EARLY_ROWS + FF_COLS, slab), _gate_slab(me) + (0, 0))
        gu = lax.dynamic_update_slice(gu, own_rows(EARLY_ROWS + FF_COLS, EARLY_ROWS + 2 * FF_COLS, slab), _up_slab(me) + (0, 0))
        d = lax.dynamic_update_slice(d, own_rows(EARLY_ROWS + 2 * FF_COLS, PACK_ROWS, (1, FF_COLS, PACK_W)), (me, 0, 0))
        return gu.reshape(2 * D_FF, D_MODEL), d.reshape(D_FF, D_MODEL)

    rs = {}

    def late_grads(g_gu, g_d):
        rs["late"] = _scatter_start([g_gu.reshape(GU_SHAPE), g_d.reshape(D_SHAPE)],
                                    ((0, _gate_slab, FF_COLS), (0, _up_slab, FF_COLS), (1, dev, FF_COLS)),
                                    name="rs_late_start")
        return rs["late"][-1]

    def mid_grads(g):
        rs["mid"] = _scatter_start(_mid_grad_pack(g), mid_pieces, name="rs_mid_start")
        return rs["mid"][-1]

    def last_grads(g_win_t):
        rs["last"] = _scatter_start([_w_in_grad_chunks(g_win_t)], ((0, dev, W_IN_ROWS),), name="rs_last_start")
        return rs["last"][-1]

    first_w = dict(small_w, mix_norm_g=mix_norm_g + ag_mid[-1][0:1, 0:1])
    loss_tot, gx, g_small = _local_step(
        x[0], loss_target[0], _w_in_operand(win_g), first_w, types.SimpleNamespace(mid=mid_weights, late=late_weights),
        types.SimpleNamespace(late=late_grads, mid=mid_grads, last=last_grads))

    loss_rows = jnp.pad(loss_tot[0:1, 0:1], ((0, SUBLANES - 1), (0, PACK_W - 1)))
    small_rows = jnp.concatenate([_small_pack(g_small_rows(g_small), SUBLANES), loss_rows], 0)
    n_small = small_rows.shape[0]
    ag_small = _gather_start(small_rows, 0, ((0, dev, n_small),), ((N_DEV, n_small, PACK_W),), name="ag_small_start")

    g_gu, g_d, land_late = _scatter_wait(rs["late"], ag_small[-1], name="rs_late_wait")
    *g_mid, land_mid = _scatter_wait(rs["mid"], ag_small[-1], name="rs_mid_wait")
    g_win, land_last = _scatter_wait(rs["last"], ag_small[-1], name="rs_last_wait")
    swap = lambda a: jnp.swapaxes(a, 1, 2)
    same = lambda a: a
    chunks = dict(w_gate=(swap, g_gu, _gate_slab, land_late, 0, FF_COLS), w_up=(swap, g_gu, _up_slab, land_late, 1, FF_COLS),
                  w_down=(same, g_d, dev, land_late, 2, FF_COLS), w_in=(swap, g_win, dev, land_last, 0, W_IN_ROWS))
    gw, dw, mw, vw = {}, {}, {}, {}
    for n, (view, own, lead, land, blk, rows) in chunks.items():
        res = _sum_adamw(own, lead, land, blk, rows, idx, view(big_w3[n]), view(big_m[n]), view(big_v[n]), name="adamw_" + n)
        gw[n], dw[n], mw[n], vw[n] = (view(r) for r in res)
    g_nat = _mid_unpack(_peer_sum(g_mid, dev, land_mid, 0, OUT_ROWS, idx, name="rs_sum_mid"))
    for n, g in g_nat.items():
        gw[n] = g[None]
        dw[n], mw[n], vw[n] = _adamw_call(big_w3[n], g, big_m[n], big_v[n], name="adamw_" + n)

    own_small, (parts,) = _gather_wait(ag_small, 0, n_small, vw[n], name="ag_small_wait")
    parts = lax.dynamic_update_slice(parts, own_small[None], (me, 0, 0))
    gs, ds, ms, vs, loss = _small_adamw(parts, small_w, small_m, small_v)
    loss = loss[0, 0]
    for d in (gs, ds, ms, vs):
        d["final_norm_g"] = d["final_norm_g"].reshape(D_MODEL)

    order = ("mix_norm_g", "w_in", "swa_sinks", "q_norm_g", "w_uq", "kv_norm_g", "w_ukv", "w_o_swa", "w_o_mla", "w_out",
             "ffn_norm_g", "w_gate", "w_up", "w_down", "final_norm_g")

    def leaves(big, small):
        return [big[n] if n in big else small[n] for n in order]

    return (loss, gx[None], *leaves(gw, gs), *leaves(dw, ds), *leaves(mw, ms), *leaves(vw, vs))


def g_small_rows(g_small):
    out = dict(g_small)
    out["swa_sinks"] = jnp.pad(g_small["swa_sinks"], ((0, SUBLANES - 1), (0, 0)))
    return out
```

```python
import types

import numpy as np
import jax
import jax.numpy as jnp
from jax import lax
from jax.experimental import pallas as pl
from jax.experimental.pallas import tpu as pltpu

F32 = jnp.float32
MXU_DTYPE = jnp.bfloat16
WIRE_DTYPE = jnp.bfloat16

D_MODEL = 1024
EPS = 1e-6
ROPE_THETA = 10000.0
BLOCK = 128
HEAD_DIM = 64
SWA_HEADS = 8
SWA_KV_HEADS = 2
SWA_GROUP = SWA_HEADS // SWA_KV_HEADS
MLA_HEADS = 8
MLA_NOPE = 64
MLA_ROPE = 32
MLA_V = 64
MLA_QK = MLA_NOPE + MLA_ROPE
Q_LORA = 384
KV_LORA = 256
D_FF = 2816
IN_SIZES = (512, 128, 128, Q_LORA, KV_LORA, MLA_ROPE, D_MODEL, D_MODEL)
IN_OFF = tuple(int(v) for v in np.cumsum((0,) + IN_SIZES))
ADAM_LR, ADAM_B1, ADAM_B2, ADAM_EPS, ADAM_WD, ADAM_STEP = 0.001, 0.9, 0.999, 1e-08, 0.01, 10

LANES = 128
SUBLANES = 8
VMEM_LIMIT = 48 * 1024 * 1024
N_DEV = 8

P_GA, P_GB, P_Q, P_QLAT, P_KR, P_K, P_V, P_KVLAT, P_W = 0, 1024, 2048, 3072, 3456, 3584, 3840, 4096, 4352
KR_LANE = 64

LOG2E = 1.4426950408889634

NT = (((1,), (1,)), ((), ()))
NN = (((1,), (0,)), ((), ()))
TN = (((0,), (0,)), ((), ()))


def _cparams(sem):
    return pltpu.CompilerParams(dimension_semantics=sem, vmem_limit_bytes=VMEM_LIMIT)


def _mm(a, b, mode, *, name, out_dtype=F32, add=None, tm=512, tn=512, tk=None):
    if mode == "nn":
        (M, K), (K2, N) = a.shape, b.shape
    elif mode == "nt":
        (M, K), (N, K2) = a.shape, b.shape
    else:
        (K, M), (K2, N) = a.shape, b.shape
    assert K == K2, (a.shape, b.shape, mode)
    tm, tn, tk = min(tm, M), min(tn, N), K if tk is None else min(tk, K)
    assert M % tm == 0 and N % tn == 0 and K % tk == 0, (M, N, K, tm, tn, tk)
    nk = K // tk
    dn = {"nn": NN, "nt": NT, "tn": TN}[mode]
    if mode == "tn":
        a_spec = pl.BlockSpec((tk, tm), lambda i, j, k: (k, i))
    else:
        a_spec = pl.BlockSpec((tm, tk), lambda i, j, k: (i, k))
    once = dict(pipeline_mode=pl.Buffered(1)) if (nk == 1 and tn == N) else {}
    if mode == "nt":
        b_spec = pl.BlockSpec((tn, tk), lambda i, j, k: (j, k), **once)
    else:
        b_spec = pl.BlockSpec((tk, tn), lambda i, j, k: (k, j), **once)
    o_spec = pl.BlockSpec((tm, tn), lambda i, j, k: (i, j))
    has_add = add is not None

    def body(*refs):
        a_ref, b_ref = refs[0], refs[1]
        add_ref = refs[2] if has_add else None
        o_ref = refs[2 + has_add]
        p = lax.dot_general(a_ref[...], b_ref[...], dn, preferred_element_type=F32)

        def finish(acc):
            if has_add:
                acc = acc + add_ref[...]
            o_ref[...] = acc.astype(o_ref.dtype)

        if nk == 1:
            finish(p)
        else:
            acc_ref = refs[-1]
            k = pl.program_id(2)

            @pl.when(k == 0)
            def _():
                acc_ref[...] = p

            @pl.when((k > 0) & (k < nk - 1))
            def _():
                acc_ref[...] += p

            @pl.when(k == nk - 1)
            def _():
                finish(acc_ref[...] + p)

    ins = [a, b] + ([add] if has_add else [])
    return pl.pallas_call(
        body, name=name, grid=(M // tm, N // tn, nk), in_specs=[a_spec, b_spec] + ([o_spec] if has_add else []), out_specs=o_spec,
        out_shape=jax.ShapeDtypeStruct((M, N), out_dtype),
        scratch_shapes=[pltpu.VMEM((tm, tn), F32)] if nk > 1 else [],
        compiler_params=_cparams(("parallel", "parallel", "arbitrary")),
    )(*ins)


def _rows(ts, w, cb=0):
    return pl.BlockSpec((ts, w), lambda i: (i, cb))


def _const(r, w):
    return pl.BlockSpec((r, w), lambda i: (0, 0))


def _sublane_sum(v):
    ts, c = v.shape
    return jnp.sum(v.reshape(ts // SUBLANES, SUBLANES, c), axis=0)


def _sigmoid(v):
    return 1.0 / (1.0 + jnp.exp(-v))


def _rope(v, cos, s_up, s_dn, up, dn):
    return v * cos + pltpu.roll(v, up, 1) * s_up + pltpu.roll(v, dn, 1) * s_dn


def _rope_t(dv, cos, s_up, s_dn, up, dn):
    return dv * cos + pltpu.roll(dv * s_up, dn, 1) + pltpu.roll(dv * s_dn, up, 1)


def _rope_tables(seq):
    pos = np.arange(seq, dtype=np.float32)[:, None]

    def base(dim):
        inv = np.float32(ROPE_THETA) ** (-np.arange(0, dim, 2, dtype=np.float32) / np.float32(dim))
        ang = (pos * inv.astype(np.float32)[None, :]).astype(np.float32)
        return np.cos(ang).astype(np.float32), np.sin(ang).astype(np.float32)

    z = lambda n: np.zeros((seq, n), np.float32)
    ca, sa = base(HEAD_DIM)
    a_cos = np.concatenate([ca, ca, z(64)], 1)
    a_up = np.concatenate([-sa, z(96)], 1)
    a_dn = np.concatenate([z(32), sa, z(64)], 1)
    cb, sb = base(MLA_ROPE)
    one = np.ones((seq, 64), np.float32)
    q_cos = np.concatenate([one, cb, cb, z(32)], 1)
    k_cos = np.concatenate([z(64), cb, cb, z(32)], 1)
    b_up = np.concatenate([z(64), -sb, z(48)], 1)
    b_dn = np.concatenate([z(80), sb, z(32)], 1)
    return tuple(jnp.asarray(t) for t in (a_cos, a_up, a_dn, q_cos, k_cos, b_up, b_dn))


def _rms(v, g):
    return v * lax.rsqrt(jnp.mean(v * v, axis=-1, keepdims=True) + EPS) * g


def _rms_bwd(v, g, d):
    r = lax.rsqrt(jnp.mean(v * v, axis=-1, keepdims=True) + EPS)
    xh = v * r
    dxh = d * g
    return r * (dxh - xh * jnp.mean(dxh * xh, axis=-1, keepdims=True)), d * xh


F_GA, F_GB, F_KVLAT, F_QLAT, F_W = 0, 1024, 2048, 2304, 2688


def _proj_in(x, g, w_t, gq, gkv, tabs, *, tm=512):
    s_, c = x.shape
    a_cos, a_up, a_dn, _, k_cos, b_up, b_dn = tabs

    def body(x_ref, g_ref, w_ref, gq_ref, gkv_ref, ac, au, ad, kc, bu, bd,
             h_ref, qa_ref, ka_ref, va_ref, cq_ref, ckv_ref, kro_ref, pf_ref):
        h = _rms(x_ref[...], g_ref[...]).astype(h_ref.dtype)
        h_ref[...] = h
        mm = lambda a, b: lax.dot_general(h, w_ref[a:b, :], NT, preferred_element_type=F32)
        pf_ref[:, F_GA:F_KVLAT] = mm(P_GA, P_Q)
        c_, u_, d_ = ac[...], au[...], ad[...]
        q = mm(P_Q, P_QLAT)
        for hd in range(SWA_HEADS):
            sl = slice(hd * LANES, (hd + 1) * LANES)
            qa_ref[:, sl] = _rope(q[:, sl], c_, u_, d_, 96, 32).astype(qa_ref.dtype)
        kv = mm(P_KR, P_KVLAT)
        kro_ref[...] = _rope(kv[:, :LANES], kc[...], bu[...], bd[...], 112, 16)
        for hd in range(SWA_KV_HEADS):
            sl = slice((1 + hd) * LANES, (2 + hd) * LANES)
            ka_ref[:, hd * LANES:(hd + 1) * LANES] = _rope(kv[:, sl], c_, u_, d_, 96, 32).astype(ka_ref.dtype)
        va_ref[...] = kv[:, P_V - P_KR:].astype(va_ref.dtype)
        for a, b, f0, gref, dst in ((P_QLAT, P_KR, F_QLAT, gq_ref, cq_ref), (P_KVLAT, P_W, F_KVLAT, gkv_ref, ckv_ref)):
            v = mm(a, b)
            pf_ref[:, f0:f0 + b - a] = v
            r = lax.rsqrt(jnp.mean(v * v, axis=-1, keepdims=True) + EPS)
            dst[...] = (v * r * gref[...]).astype(dst.dtype)

    tab = _rows(tm, LANES)
    widths = (c, SWA_HEADS * LANES, SWA_KV_HEADS * LANES, SWA_KV_HEADS * LANES, Q_LORA, KV_LORA)
    return pl.pallas_call(
        body, name="proj_in", grid=(s_ // tm,),
        in_specs=[_rows(tm, c), _const(1, c), pl.BlockSpec((P_W, c), lambda i: (0, 0), pipeline_mode=pl.Buffered(1)),
                  _const(1, Q_LORA), _const(1, KV_LORA), tab, tab, tab, tab, tab, tab],
        out_specs=[_rows(tm, w) for w in widths] + [tab, _rows(tm, F_W)],
        out_shape=[jax.ShapeDtypeStruct((s_, w), MXU_DTYPE) for w in widths]
        + [jax.ShapeDtypeStruct((s_, LANES), F32), jax.ShapeDtypeStruct((s_, F_W), F32)],
        compiler_params=_cparams(("parallel",)),
    )(x, g, w_t, gq, gkv, a_cos, a_up, a_dn, k_cos, b_up, b_dn)


def _mm_norm_bwd(a, b, x, g, res, *, name, after=None, tm=512):
    s_, kk = a.shape
    c = b.shape[1]
    has_after = after is not None

    def body(*refs):
        a_ref, b_ref, x_ref, g_ref, res_ref = refs[:5]
        dx_ref, dxb_ref, dg_ref = refs[5 + has_after:]
        d = jnp.dot(a_ref[...], b_ref[...], preferred_element_type=F32)
        dx, gg = _rms_bwd(x_ref[...], g_ref[...], d)
        dx = dx + res_ref[...]
        dx_ref[...] = dx
        dxb_ref[...] = dx.astype(dxb_ref.dtype)

        @pl.when(pl.program_id(0) == 0)
        def _():
            dg_ref[...] = jnp.zeros(dg_ref.shape, F32)

        dg_ref[...] += _sublane_sum(gg)

    row = _rows(tm, c)
    in_specs = [_rows(tm, kk), pl.BlockSpec((kk, c), lambda i: (0, 0), pipeline_mode=pl.Buffered(1)), row, _const(1, c), row]
    return pl.pallas_call(
        body, name=name, grid=(s_ // tm,), in_specs=in_specs + ([pl.BlockSpec(memory_space=pl.ANY)] if has_after else []),
        out_specs=[row, row, _const(SUBLANES, c)],
        out_shape=[jax.ShapeDtypeStruct((s_, c), F32), jax.ShapeDtypeStruct((s_, c), MXU_DTYPE),
                   jax.ShapeDtypeStruct((SUBLANES, c), F32)],
        compiler_params=_cparams(("arbitrary",)),
    )(*([a, b, x, g, res] + ([after] if has_after else [])))


def _mla_up(cq, ckv, kro, wuq, wuk, wuv, tabs, *, ts=512):
    s_ = cq.shape[0]
    _, _, _, q_cos, _, b_up, b_dn = tabs

    def body(cq_ref, ckv_ref, kr_ref, wq_ref, wk_ref, wv_ref, qc, bu, bd, qo_ref, ko_ref, vo_ref):
        c_, u_, d_ = qc[...], bu[...], bd[...]
        kr = kr_ref[...]
        ckv_ = ckv_ref[...]
        vo_ref[...] = jnp.dot(ckv_, wv_ref[...], preferred_element_type=F32).astype(vo_ref.dtype)
        q = jnp.dot(cq_ref[...], wq_ref[...], preferred_element_type=F32)
        k = jnp.dot(ckv_, wk_ref[...], preferred_element_type=F32)
        for h in range(MLA_HEADS):
            sl = slice(h * LANES, (h + 1) * LANES)
            qo_ref[:, sl] = _rope(q[:, sl], c_, u_, d_, 112, 16).astype(qo_ref.dtype)
            ko_ref[:, sl] = (k[:, sl] + kr).astype(ko_ref.dtype)

    tab, out = _rows(ts, LANES), _rows(ts, 1024)
    return pl.pallas_call(
        body, name="mla_up", grid=(s_ // ts,),
        in_specs=[_rows(ts, Q_LORA), _rows(ts, KV_LORA), tab, _const(Q_LORA, 1024), _const(KV_LORA, 1024),
                  _const(KV_LORA, 1024), tab, tab, tab],
        out_specs=[out, out, out], out_shape=[jax.ShapeDtypeStruct((s_, 1024), MXU_DTYPE)] * 3,
        compiler_params=_cparams(("parallel",)),
    )(cq, ckv, kro, wuq, wuk, wuv, q_cos, b_up, b_dn)


def _mla_up_bwd(dqc, dkc, dvp, wuq, wukv, p, gq, gkv, tabs, *, ts=256):
    s_ = dqc.shape[0]
    _, _, _, q_cos, k_cos, b_up, b_dn = tabs

    def body(dq_ref, dk_ref, dv_ref, wq_ref, wkv_ref, ql_ref, kvl_ref, gq_ref, gkv_ref, qc, kc, bu, bd,
             dqo_ref, dkvo_ref, dkr_ref, dql_ref, dkvl_ref, dgq_ref, dgkv_ref):
        c_, u_, d_ = qc[...], bu[...], bd[...]
        tot = jnp.zeros((ts, LANES), F32)
        for h in range(MLA_HEADS):
            sl = slice(h * LANES, (h + 1) * LANES)
            dqo_ref[:, sl] = _rope_t(dq_ref[:, sl], c_, u_, d_, 112, 16).astype(dqo_ref.dtype)
            dk = dk_ref[:, sl]
            dkvo_ref[:, sl] = dk.astype(dkvo_ref.dtype)
            tot = tot + dk
        dkvo_ref[:, 1024:2048] = dv_ref[...].astype(dkvo_ref.dtype)
        dkr_ref[...] = _rope_t(tot, kc[...], u_, d_, 112, 16).astype(dkr_ref.dtype)

        @pl.when(pl.program_id(0) == 0)
        def _():
            dgq_ref[...] = jnp.zeros(dgq_ref.shape, F32)
            dgkv_ref[...] = jnp.zeros(dgkv_ref.shape, F32)

        for do_ref, w_ref, x_ref, g_ref, dx_ref, dg_ref in ((dqo_ref, wq_ref, ql_ref, gq_ref, dql_ref, dgq_ref),
                                                            (dkvo_ref, wkv_ref, kvl_ref, gkv_ref, dkvl_ref, dgkv_ref)):
            d = lax.dot_general(do_ref[...], w_ref[...], NT, preferred_element_type=F32)
            dx, gg = _rms_bwd(x_ref[...], g_ref[...], d)
            dx_ref[...] = dx.astype(dx_ref.dtype)
            dg_ref[...] += _sublane_sum(gg)

    tab = _rows(ts, LANES)
    return pl.pallas_call(
        body, name="mla_up_bwd", grid=(s_ // ts,),
        in_specs=[_rows(ts, 1024), _rows(ts, 1024), _rows(ts, 1024), _const(Q_LORA, 1024), _const(KV_LORA, 2048),
                  _rows(ts, Q_LORA, F_QLAT // Q_LORA), _rows(ts, KV_LORA, F_KVLAT // KV_LORA),
                  _const(1, Q_LORA), _const(1, KV_LORA), tab, tab, tab, tab],
        out_specs=[_rows(ts, 1024), _rows(ts, 2048), _rows(ts, LANES), _rows(ts, Q_LORA), _rows(ts, KV_LORA),
                   _const(SUBLANES, Q_LORA), _const(SUBLANES, KV_LORA)],
        out_shape=[jax.ShapeDtypeStruct((s_, 1024), MXU_DTYPE), jax.ShapeDtypeStruct((s_, 2048), MXU_DTYPE),
                   jax.ShapeDtypeStruct((s_, LANES), MXU_DTYPE), jax.ShapeDtypeStruct((s_, Q_LORA), MXU_DTYPE),
                   jax.ShapeDtypeStruct((s_, KV_LORA), MXU_DTYPE), jax.ShapeDtypeStruct((SUBLANES, Q_LORA), F32),
                   jax.ShapeDtypeStruct((SUBLANES, KV_LORA), F32)],
        compiler_params=_cparams(("arbitrary",)),
    )(dqc, dkc, dvp, wuq, wukv, p, p, gq, gkv, q_cos, k_cos, b_up, b_dn)


def _assemble_dp(dgab, dqa, dqlat, dkr, dka, dva, dkvlat, tabs, *, ts=256):
    s_ = dqa.shape[0]
    a_cos, a_up, a_dn = tabs[0], tabs[1], tabs[2]

    def body(dg_ref, dq_ref, dql_ref, dkr_ref, dk_ref, dv_ref, dkvl_ref, ac, au, ad, o_ref):
        c_, u_, d_ = ac[...], au[...], ad[...]
        o_ref[:, P_GA:P_Q] = dg_ref[...]
        for h in range(SWA_HEADS):
            sl = slice(h * LANES, (h + 1) * LANES)
            o_ref[:, P_Q + h * LANES:P_Q + (h + 1) * LANES] = _rope_t(dq_ref[:, sl], c_, u_, d_, 96, 32).astype(o_ref.dtype)
        o_ref[:, P_QLAT:P_KR] = dql_ref[...]
        o_ref[:, P_KR:P_K] = dkr_ref[...]
        for h in range(SWA_KV_HEADS):
            sl = slice(h * LANES, (h + 1) * LANES)
            o_ref[:, P_K + h * LANES:P_K + (h + 1) * LANES] = _rope_t(dk_ref[:, sl], c_, u_, d_, 96, 32).astype(o_ref.dtype)
        o_ref[:, P_V:P_KVLAT] = dv_ref[...]
        o_ref[:, P_KVLAT:P_W] = dkvl_ref[...]

    tab = _rows(ts, LANES)
    return pl.pallas_call(
        body, name="assemble_dp", grid=(s_ // ts,),
        in_specs=[_rows(ts, 2048), _rows(ts, 1024), _rows(ts, Q_LORA), _rows(ts, LANES), _rows(ts, 256), _rows(ts, 256),
                  _rows(ts, KV_LORA), tab, tab, tab],
        out_specs=_rows(ts, P_W), out_shape=jax.ShapeDtypeStruct((s_, P_W), MXU_DTYPE),
        compiler_params=_cparams(("parallel",)),
    )(dgab, dqa, dqlat, dkr, dka, dva, dkvlat, a_cos, a_up, a_dn)


def _attn_out_gate(oa, ob, woa_t, wob_t, p, *, ts=512):
    s_ = p.shape[0]

    def body(oa_ref, ob_ref, wa_ref, wb_ref, ga_ref, gb_ref, y_ref):
        ta = lax.dot_general(oa_ref[...], wa_ref[...], NT, preferred_element_type=F32)
        tb = lax.dot_general(ob_ref[...], wb_ref[...], NT, preferred_element_type=F32)
        y_ref[...] = (_sigmoid(ga_ref[...]) * ta + _sigmoid(gb_ref[...]) * tb).astype(y_ref.dtype)

    w = _const(1024, 1024)
    return pl.pallas_call(
        body, name="attn_out_gate", grid=(s_ // ts,),
        in_specs=[_rows(ts, 1024), _rows(ts, 1024), w, w, _rows(ts, 1024, F_GA // 1024), _rows(ts, 1024, F_GB // 1024)],
        out_specs=_rows(ts, 1024), out_shape=jax.ShapeDtypeStruct((s_, 1024), MXU_DTYPE),
        compiler_params=_cparams(("parallel",)),
    )(oa, ob, woa_t, wob_t, p, p)


def _d_y_gate(dx1b, wout, p, oa, ob, woa_t, wob_t, *, ts=512):
    s_ = p.shape[0]

    def body(dx_ref, w_ref, ga_ref, gb_ref, oa_ref, ob_ref, wa_ref, wb_ref, dta_ref, dtb_ref, dg_ref):
        d = lax.dot_general(dx_ref[...], w_ref[...], NT, preferred_element_type=F32)
        sa, sb = _sigmoid(ga_ref[...]), _sigmoid(gb_ref[...])
        dta_ref[...] = (d * sa).astype(dta_ref.dtype)
        dtb_ref[...] = (d * sb).astype(dtb_ref.dtype)
        ta = lax.dot_general(oa_ref[...], wa_ref[...], NT, preferred_element_type=F32)
        dg_ref[:, 0:1024] = (d * ta * (sa * (1.0 - sa))).astype(dg_ref.dtype)
        tb = lax.dot_general(ob_ref[...], wb_ref[...], NT, preferred_element_type=F32)
        dg_ref[:, 1024:2048] = (d * tb * (sb * (1.0 - sb))).astype(dg_ref.dtype)

    w = _const(1024, 1024)
    return pl.pallas_call(
        body, name="d_y_gate", grid=(s_ // ts,),
        in_specs=[_rows(ts, 1024), w, _rows(ts, 1024, F_GA // 1024), _rows(ts, 1024, F_GB // 1024),
                  _rows(ts, 1024), _rows(ts, 1024), w, w],
        out_specs=[_rows(ts, 1024), _rows(ts, 1024), _rows(ts, 2048)],
        out_shape=[jax.ShapeDtypeStruct((s_, 1024), MXU_DTYPE)] * 2 + [jax.ShapeDtypeStruct((s_, 2048), MXU_DTYPE)],
        compiler_params=_cparams(("parallel",)),
    )(dx1b, wout, p, p, oa, ob, woa_t, wob_t)


FF_TILE = D_FF // 2


def _ffn_in_act(x1, g, wgu_t, *, tm=512):
    s_ = x1.shape[0]
    n = s_ // tm

    def body(x_ref, g_ref, w_ref, h_ref, gu_ref, a_ref):
        h = _rms(x_ref[...], g_ref[...]).astype(h_ref.dtype)
        h_ref[...] = h
        p = lax.dot_general(h, w_ref[...], NT, preferred_element_type=F32)
        gu_ref[...] = p
        gate = p[:, :FF_TILE]
        a_ref[...] = (gate * _sigmoid(gate) * p[:, FF_TILE:]).astype(a_ref.dtype)

    return pl.pallas_call(
        body, name="ffn_in", grid=(2, s_ // tm),
        in_specs=[pl.BlockSpec((tm, D_MODEL), lambda j, i: (i, 0)), pl.BlockSpec((1, D_MODEL), lambda j, i: (0, 0)),
                  pl.BlockSpec((2 * FF_TILE, D_MODEL), lambda j, i: (j, 0))],
        out_specs=[pl.BlockSpec((tm, D_MODEL), lambda j, i: (i + j * (n - 1 - i), 0)),
                   pl.BlockSpec((tm, 2 * FF_TILE), lambda j, i: (i, j)),
                   pl.BlockSpec((tm, FF_TILE), lambda j, i: (i, j))],
        out_shape=[jax.ShapeDtypeStruct((s_, D_MODEL), MXU_DTYPE), jax.ShapeDtypeStruct((s_, 2 * D_FF), F32),
                   jax.ShapeDtypeStruct((s_, D_FF), MXU_DTYPE)],
        compiler_params=_cparams(("arbitrary", "arbitrary")),
    )(x1, g, wgu_t)


def _d_act_swiglu(dx2b, wd, gu, *, tm=512):
    s_ = dx2b.shape[0]

    def body(d_ref, w_ref, gu_ref, o_ref):
        da = lax.dot_general(d_ref[...], w_ref[...], NT, preferred_element_type=F32)
        g, u = gu_ref[:, :FF_TILE], gu_ref[:, FF_TILE:]
        sg = _sigmoid(g)
        o_ref[:, :FF_TILE] = (da * u * (sg * (1.0 + g * (1.0 - sg)))).astype(o_ref.dtype)
        o_ref[:, FF_TILE:] = (da * (g * sg)).astype(o_ref.dtype)

    gu_spec = pl.BlockSpec((tm, 2 * FF_TILE), lambda j, i: (i, j))
    return pl.pallas_call(
        body, name="d_act", grid=(2, s_ // tm),
        in_specs=[pl.BlockSpec((tm, D_MODEL), lambda j, i: (i, 0)), pl.BlockSpec((FF_TILE, D_MODEL), lambda j, i: (j, 0)), gu_spec],
        out_specs=gu_spec, out_shape=jax.ShapeDtypeStruct((s_, 2 * D_FF), MXU_DTYPE),
        compiler_params=_cparams(("parallel", "parallel")),
    )(dx2b, wd, gu)


def _ffn_out_loss(act, wd, x1, g, tgt, *, ts=512):
    s_, c = x1.shape
    kk = act.shape[1]

    def body(a_ref, w_ref, x_ref, g_ref, t_ref, dx_ref, dxb_ref, dg_ref, lp_ref, tot_ref):
        @pl.when(pl.program_id(0) == 0)
        def _():
            dg_ref[...] = jnp.zeros(dg_ref.shape, F32)
            lp_ref[...] = jnp.zeros(lp_ref.shape, F32)

        gg = g_ref[...]
        for rows in (slice(0, ts // 2), slice(ts // 2, ts)):
            v = x_ref[rows] + jnp.dot(a_ref[rows], w_ref[...], preferred_element_type=F32)
            r = lax.rsqrt(jnp.mean(v * v, axis=-1, keepdims=True) + EPS)
            xh = v * r
            e = xh * gg - t_ref[rows]
            do = e * (1.0 / c)
            dxh = do * gg
            dx = r * (dxh - xh * jnp.mean(dxh * xh, axis=-1, keepdims=True))
            dx_ref[rows] = dx
            dxb_ref[rows] = dx.astype(dxb_ref.dtype)
            dg_ref[...] += _sublane_sum(do * xh)
            lp_ref[...] += _sublane_sum(e * e)
        tot_ref[...] = jnp.full(tot_ref.shape, (0.5 / c) * jnp.sum(lp_ref[...]), F32)

    return pl.pallas_call(
        body, name="ffn_out_loss", grid=(s_ // ts,),
        in_specs=[_rows(ts, kk), _const(kk, c), _rows(ts, c), _const(1, c), _rows(ts, c)],
        out_specs=[_rows(ts, c), _rows(ts, c), _const(SUBLANES, c), _const(SUBLANES, c), _const(SUBLANES, LANES)],
        out_shape=[jax.ShapeDtypeStruct((s_, c), F32), jax.ShapeDtypeStruct((s_, c), MXU_DTYPE),
                   jax.ShapeDtypeStruct((SUBLANES, c), F32), jax.ShapeDtypeStruct((SUBLANES, c), F32),
                   jax.ShapeDtypeStruct((SUBLANES, LANES), F32)],
        compiler_params=_cparams(("arbitrary",)),
    )(act, wd, x1, g, tgt)


def _mla_d_out(dtb, wob_t, o32, *, ts=512):
    s_ = dtb.shape[0]

    def body(dt_ref, w_ref, o_ref, dob_ref, dl_ref):
        d = jnp.dot(dt_ref[...], w_ref[...], preferred_element_type=F32)
        dob_ref[...] = d.astype(dob_ref.dtype)
        prod = d * o_ref[...]
        for h in range(MLA_HEADS):
            dl_ref[h] = jnp.sum(prod[:, h * LANES:(h + 1) * LANES].T, axis=0, keepdims=True)

    return pl.pallas_call(
        body, name="mla_d_out", grid=(s_ // ts,), in_specs=[_rows(ts, 1024), _const(1024, 1024), _rows(ts, 1024)],
        out_specs=[_rows(ts, 1024), pl.BlockSpec((MLA_HEADS, 1, ts), lambda i: (0, 0, i))],
        out_shape=[jax.ShapeDtypeStruct((s_, 1024), MXU_DTYPE), jax.ShapeDtypeStruct((MLA_HEADS, 1, s_), F32)],
        compiler_params=_cparams(("parallel",)),
    )(dtb, wob_t, o32)


SWA_T = 4 * BLOCK


SWA_W = SWA_GROUP * BLOCK


def _swa_masks(sb):
    kr = lax.broadcasted_iota(jnp.int32, (2 * BLOCK, SWA_W), 0)
    qc = jnp.bitwise_and(lax.broadcasted_iota(jnp.int32, (2 * BLOCK, SWA_W), 1), BLOCK - 1)
    band = jnp.logical_and(kr > qc, kr <= qc + BLOCK)
    first = jnp.logical_and(band, kr >= BLOCK)
    return band, jnp.logical_or(first, jnp.logical_and(band, sb > 0))


def _heads_to_rows(ref, rs):
    return jnp.concatenate([ref[rs, h * LANES:(h + 1) * LANES] for h in range(SWA_GROUP)], axis=0)


def _sink_row(sk_ref):
    return jnp.concatenate([sk_ref[0, h:h + 1, :] for h in range(SWA_GROUP)], axis=1) * LOG2E


def _swa_in_specs(rev, nsb):
    sbi = (lambda j: nsb - 1 - j) if rev else (lambda j: j)
    cur = pl.BlockSpec((SWA_T, LANES), lambda g, j: (sbi(j), g))
    prev = pl.BlockSpec((BLOCK, LANES), lambda g, j: (jnp.maximum(4 * sbi(j) - 1, 0), g))
    q = pl.BlockSpec((SWA_T, SWA_GROUP * LANES), lambda g, j: (sbi(j), g))
    sink = pl.BlockSpec((1, SUBLANES, LANES), lambda g, j: (g, 0, 0))
    lse = pl.BlockSpec((SWA_GROUP, 1, SWA_T), lambda g, j: (g, 0, sbi(j)))
    return q, cur, prev, sink, lse


def _swa_fwd(qa, ka, va, sink_b):
    s_ = qa.shape[0]
    nsb = s_ // SWA_T
    c2 = HEAD_DIM ** -0.5 * LOG2E

    def body(q_ref, kc_ref, kp_ref, vc_ref, vp_ref, sk_ref, o32_ref, o16_ref, lse_ref, kx, vx):
        kx[0:BLOCK, :] = kp_ref[...]
        kx[BLOCK:5 * BLOCK, :] = kc_ref[...]
        vx[0:BLOCK, :] = vp_ref[...]
        vx[BLOCK:5 * BLOCK, :] = vc_ref[...]
        band, band0 = _swa_masks(pl.program_id(1))
        sink2 = _sink_row(sk_ref)
        for b in range(4):
            rs = slice(b * BLOCK, (b + 1) * BLOCK)
            ks = slice(b * BLOCK, (b + 2) * BLOCK)
            st = lax.dot_general(kx[ks, :], _heads_to_rows(q_ref, rs), NT, preferred_element_type=F32) * c2
            st = jnp.where(band0 if b == 0 else band, st, -jnp.inf)
            m = jnp.maximum(jnp.max(st, axis=0, keepdims=True), sink2)
            pt = jnp.exp2(st - m)
            den = jnp.sum(pt, axis=0, keepdims=True) + jnp.exp2(sink2 - m)
            o = lax.dot_general((pt * (1.0 / den)).astype(MXU_DTYPE), vx[ks, :], TN, preferred_element_type=F32)
            lse = m + jnp.log2(den)
            for hh in range(SWA_GROUP):
                cs = slice(hh * LANES, (hh + 1) * LANES)
                o32_ref[rs, cs] = o[cs, :]
                o16_ref[rs, cs] = o[cs, :].astype(o16_ref.dtype)
                lse_ref[hh, :, rs] = lse[:, cs]

    q, cur, prev, sink, lse_spec = _swa_in_specs(False, nsb)
    return pl.pallas_call(
        body, name="swa_fwd", grid=(SWA_KV_HEADS, nsb), in_specs=[q, cur, prev, cur, prev, sink],
        out_specs=[q, q, lse_spec],
        out_shape=[jax.ShapeDtypeStruct((s_, SWA_HEADS * LANES), F32), jax.ShapeDtypeStruct((s_, SWA_HEADS * LANES), MXU_DTYPE),
                   jax.ShapeDtypeStruct((SWA_HEADS, 1, s_), F32)],
        scratch_shapes=[pltpu.VMEM((5 * BLOCK, LANES), MXU_DTYPE), pltpu.VMEM((5 * BLOCK, LANES), MXU_DTYPE)],
        compiler_params=_cparams(("parallel", "arbitrary")),
    )(qa, ka, ka, va, va, sink_b)


def _swa_bwd(qa, ka, va, sink_b, o32, do, lse):
    s_ = qa.shape[0]
    nsb = s_ // SWA_T
    scale = HEAD_DIM ** -0.5
    c2 = scale * LOG2E

    def body(q_ref, kc_ref, kp_ref, vc_ref, vp_ref, sk_ref, o_ref, do_ref, lse_ref,
             dq_ref, dk_ref, dv_ref, dsk_ref, kx, vx, kacc, vacc, kcar, vcar):
        j = pl.program_id(1)
        kx[0:BLOCK, :] = kp_ref[...]
        kx[BLOCK:5 * BLOCK, :] = kc_ref[...]
        vx[0:BLOCK, :] = vp_ref[...]
        vx[BLOCK:5 * BLOCK, :] = vc_ref[...]
        band, band0 = _swa_masks(nsb - 1 - j)
        kacc[...] = jnp.zeros(kacc.shape, F32)
        vacc[...] = jnp.zeros(vacc.shape, F32)

        @pl.when(j == 0)
        def _():
            kcar[...] = jnp.zeros(kcar.shape, F32)
            vcar[...] = jnp.zeros(vcar.shape, F32)
            dsk_ref[...] = jnp.zeros(dsk_ref.shape, F32)

        sink2 = _sink_row(sk_ref)
        dsink = jnp.zeros((1, SWA_W), F32)
        for b in range(4):
            rs = slice(b * BLOCK, (b + 1) * BLOCK)
            ks = slice(b * BLOCK, (b + 2) * BLOCK)
            q, k2, v2 = _heads_to_rows(q_ref, rs), kx[ks, :], vx[ks, :]
            d = _heads_to_rows(do_ref, rs)
            delta = jnp.sum((d * _heads_to_rows(o_ref, rs)).T, axis=0, keepdims=True)
            l2 = jnp.concatenate([lse_ref[hh, :, rs] for hh in range(SWA_GROUP)], axis=1)
            st = lax.dot_general(k2, q, NT, preferred_element_type=F32) * c2
            pt = jnp.exp2(jnp.where(band0 if b == 0 else band, st, -jnp.inf) - l2)
            db = d.astype(MXU_DTYPE)
            dst = (pt * (lax.dot_general(v2, db, NT, preferred_element_type=F32) - delta) * scale).astype(MXU_DTYPE)
            dq = lax.dot_general(dst, k2, TN, preferred_element_type=F32)
            for hh in range(SWA_GROUP):
                dq_ref[rs, hh * LANES:(hh + 1) * LANES] = dq[hh * LANES:(hh + 1) * LANES, :]
            kacc[ks, :] += jnp.dot(dst, q, preferred_element_type=F32)
            vacc[ks, :] += jnp.dot(pt.astype(MXU_DTYPE), db, preferred_element_type=F32)
            dsink = dsink - jnp.exp2(sink2 - l2) * delta
        for hh in range(SWA_GROUP):
            tot = jnp.sum(dsink[:, hh * LANES:(hh + 1) * LANES], axis=1, keepdims=True)
            dsk_ref[0, hh:hh + 1, :] += jnp.broadcast_to(tot, (1, LANES))

        dk_ref[0:3 * BLOCK, :] = kacc[BLOCK:4 * BLOCK, :]
        dk_ref[3 * BLOCK:4 * BLOCK, :] = kacc[4 * BLOCK:5 * BLOCK, :] + kcar[...]
        dv_ref[0:3 * BLOCK, :] = vacc[BLOCK:4 * BLOCK, :].astype(dv_ref.dtype)
        dv_ref[3 * BLOCK:4 * BLOCK, :] = (vacc[4 * BLOCK:5 * BLOCK, :] + vcar[...]).astype(dv_ref.dtype)
        kcar[...] = kacc[0:BLOCK, :]
        vcar[...] = vacc[0:BLOCK, :]

    q, cur, prev, sink, lse_spec = _swa_in_specs(True, nsb)
    return pl.pallas_call(
        body, name="swa_bwd", grid=(SWA_KV_HEADS, nsb),
        in_specs=[q, cur, prev, cur, prev, sink, q, q, lse_spec],
        out_specs=[q, cur, cur, sink],
        out_shape=[jax.ShapeDtypeStruct((s_, SWA_HEADS * LANES), F32), jax.ShapeDtypeStruct((s_, SWA_KV_HEADS * LANES), F32),
                   jax.ShapeDtypeStruct((s_, SWA_KV_HEADS * LANES), MXU_DTYPE),
                   jax.ShapeDtypeStruct((SWA_KV_HEADS, SUBLANES, LANES), F32)],
        scratch_shapes=[pltpu.VMEM((5 * BLOCK, LANES), MXU_DTYPE), pltpu.VMEM((5 * BLOCK, LANES), MXU_DTYPE),
                        pltpu.VMEM((5 * BLOCK, LANES), F32), pltpu.VMEM((5 * BLOCK, LANES), F32),
                        pltpu.VMEM((BLOCK, LANES), F32), pltpu.VMEM((BLOCK, LANES), F32)],
        compiler_params=_cparams(("arbitrary", "arbitrary")),
    )(qa, ka, ka, va, va, sink_b, o32, do, lse)


MLA_T = 512
MLA_FWD_GROUP = 4
MLA_BWD_GROUP = 2


def _mla_specs(s_, t, group):
    w = group * LANES
    qs = pl.BlockSpec((t, w), lambda g, i: (i, g))
    kv = pl.BlockSpec((s_, w), lambda g, i: (0, g))
    row = pl.BlockSpec((group, 1, t), lambda g, i: (g, 0, i))
    return qs, kv, row


def _causal_scores_t(k, q, t, c2, masked):
    st = lax.dot_general(k, q, NT, preferred_element_type=F32) * c2
    if masked:
        kr = lax.broadcasted_iota(jnp.int32, (t, t), 0)
        qc = lax.broadcasted_iota(jnp.int32, (t, t), 1)
        st = jnp.where(kr <= qc, st, -jnp.inf)
    return st


def _mla_fwd(qc, kc, vp):
    s_ = qc.shape[0]
    t = min(MLA_T, s_)
    c2 = MLA_QK ** -0.5 * LOG2E
    grp = MLA_FWD_GROUP

    def body(q_ref, k_ref, v_ref, o32_ref, o16_ref, lse_ref, m_s, acc_s):
        qi = pl.program_id(1)
        m_s[...] = jnp.full(m_s.shape, -jnp.inf, F32)
        acc_s[...] = jnp.zeros(acc_s.shape, F32)
        ones_lane = lax.broadcasted_iota(jnp.int32, (t, LANES), 1) == MLA_V

        def step(ki, masked):
            off = pl.multiple_of(ki * t, t)
            for g in range(grp):
                cs = slice(g * LANES, (g + 1) * LANES)
                st = _causal_scores_t(k_ref[pl.ds(off, t), cs], q_ref[:, cs], t, c2, masked)
                m_old = m_s[g]
                m_new = jnp.maximum(m_old, jnp.max(st, axis=0, keepdims=True))
                alpha = jnp.exp2(m_old - m_new)
                pt = jnp.exp2(st - m_new).astype(MXU_DTYPE)
                v = v_ref[pl.ds(off, t), cs]
                v = jnp.where(ones_lane, jnp.ones((), v.dtype), v)
                acc_s[g] = alpha * acc_s[g] + lax.dot_general(v, pt, TN, preferred_element_type=F32)
                m_s[g] = m_new

        def full_block(ki, carry):
            step(ki, False)
            return carry

        lax.fori_loop(0, qi, full_block, 0)
        step(qi, True)
        for g in range(grp):
            cs = slice(g * LANES, (g + 1) * LANES)
            acc = acc_s[g]
            l = acc[MLA_V:MLA_V + 1, :]
            o = (acc * (1.0 / l)).T
            o32_ref[:, cs] = o
            o16_ref[:, cs] = o.astype(o16_ref.dtype)
            lse_ref[g] = m_s[g] + jnp.log2(l)

    qs, kv, row = _mla_specs(s_, t, grp)
    return pl.pallas_call(
        body, name="mla_fwd", grid=(MLA_HEADS // grp, s_ // t), in_specs=[qs, kv, kv], out_specs=[qs, qs, row],
        out_shape=[jax.ShapeDtypeStruct((s_, MLA_HEADS * LANES), F32), jax.ShapeDtypeStruct((s_, MLA_HEADS * LANES), MXU_DTYPE),
                   jax.ShapeDtypeStruct((MLA_HEADS, 1, s_), F32)],
        scratch_shapes=[pltpu.VMEM((grp, 1, t), F32), pltpu.VMEM((grp, LANES, t), F32)],
        compiler_params=_cparams(("parallel", "arbitrary")),
    )(qc, kc, vp)


def _mla_bwd(qc, kc, vp, dob, lse, delta):
    s_ = qc.shape[0]
    t = min(MLA_T, s_)
    scale = MLA_QK ** -0.5
    c2 = scale * LOG2E
    grp = MLA_BWD_GROUP

    def body(q_ref, do_ref, lse_ref, dl_ref, k_ref, v_ref, dq_ref, dk_ref, dv_ref, dqt_s):
        qi = pl.program_id(1)

        @pl.when(qi == 0)
        def _():
            dk_ref[...] = jnp.zeros(dk_ref.shape, F32)
            dv_ref[...] = jnp.zeros(dv_ref.shape, F32)

        dqt_s[...] = jnp.zeros(dqt_s.shape, F32)

        def step(ki, masked):
            off = pl.multiple_of(ki * t, t)
            for g in range(grp):
                cs = slice(g * LANES, (g + 1) * LANES)
                q, d, k = q_ref[:, cs], do_ref[:, cs], k_ref[pl.ds(off, t), cs]
                pt = jnp.exp2(_causal_scores_t(k, q, t, c2, masked) - lse_ref[g])
                dpt = lax.dot_general(v_ref[pl.ds(off, t), cs], d, NT, preferred_element_type=F32)
                dst = (pt * (dpt - dl_ref[g]) * scale).astype(MXU_DTYPE)
                dv_ref[pl.ds(off, t), cs] += jnp.dot(pt.astype(MXU_DTYPE), d, preferred_element_type=F32)
                dk_ref[pl.ds(off, t), cs] += jnp.dot(dst, q, preferred_element_type=F32)
                dqt_s[g] += lax.dot_general(k, dst, TN, preferred_element_type=F32)

        def full_block(ki, carry):
            step(ki, False)
            return carry

        lax.fori_loop(0, qi, full_block, 0)
        step(qi, True)
        for g in range(grp):
            dq_ref[:, g * LANES:(g + 1) * LANES] = dqt_s[g].T

    qs, kv, row = _mla_specs(s_, t, grp)
    shp = jax.ShapeDtypeStruct((s_, MLA_HEADS * LANES), F32)
    return pl.pallas_call(
        body, name="mla_bwd", grid=(MLA_HEADS // grp, s_ // t), in_specs=[qs, qs, row, row, kv, kv],
        out_specs=[qs, kv, kv], out_shape=[shp, shp, shp], scratch_shapes=[pltpu.VMEM((grp, LANES, t), F32)],
        compiler_params=_cparams(("parallel", "arbitrary")),
    )(qc, dob, lse, delta, kc, vp)


def _pad_heads(w, nh, hd, axis):
    shp = w.shape
    w = w.reshape(shp[:axis] + (nh, hd) + shp[axis + 1:])
    pad = [(0, 0)] * w.ndim
    pad[axis + 1] = (0, LANES - hd)
    w = jnp.pad(w, pad)
    return w.reshape(shp[:axis] + (nh * LANES,) + shp[axis + 1:])


def _unpad_heads(w, nh, hd, axis):
    shp = w.shape
    w = w.reshape(shp[:axis] + (nh, LANES) + shp[axis + 1:])
    w = lax.slice_in_dim(w, 0, hd, axis=axis + 1)
    return w.reshape(shp[:axis] + (nh * hd,) + shp[axis + 1:])


PACK_W = 1024
ROW_TILE = 16
FULL_SHAPE = dict(w_in=(1024, 3488), w_uq=(384, 768), w_ukv=(256, 1024), w_o_swa=(512, 1024), w_o_mla=(512, 1024),
                  w_out=(1024, 1024), w_gate=(1024, 2816), w_up=(1024, 2816), w_down=(2816, 1024))
BIG = tuple(FULL_SHAPE)
ROW_SHARDED = ("w_out", "w_down")
W_IN_COLS = FULL_SHAPE["w_in"][1] // N_DEV
W_IN_ROWS = -(-W_IN_COLS // ROW_TILE) * ROW_TILE
FF_COLS = D_FF // N_DEV
OUT_ROWS = D_MODEL // N_DEV
SMALL_FLAT = (("w_uq", 0, 36), ("w_ukv", 48, 32))
SMALL_USED = 80
MID_BLOCKS = 4
MID_ROWS = MID_BLOCKS * OUT_ROWS
EARLY_ROWS = W_IN_ROWS + MID_ROWS
LATE_ROWS = 3 * FF_COLS
PACK_ROWS = EARLY_ROWS + LATE_ROWS


def _shard_shape(n):
    r, c = FULL_SHAPE[n]
    return (r // N_DEV, c) if n in ROW_SHARDED else (r, c // N_DEV)


def _wire_pack(sh, dtype):
    c = lambda n: sh[n].astype(dtype)
    rows = [jnp.pad(c("w_in").T, ((0, W_IN_ROWS - W_IN_COLS), (0, 0))), c("w_out"),
            _pad_heads(c("w_o_swa").T, SWA_HEADS, HEAD_DIM, 1), _pad_heads(c("w_o_mla").T, MLA_HEADS, MLA_V, 1)]
    for n, _, r in SMALL_FLAT:
        rows.append(jnp.pad(c(n).reshape(r, PACK_W), ((0, -r % ROW_TILE), (0, 0))))
    rows.append(jnp.zeros((OUT_ROWS - SMALL_USED, PACK_W), dtype))
    return jnp.concatenate(rows + [c("w_gate").T, c("w_up").T, c("w_down")], 0)


def _mid_unpack(p):
    out = dict(w_out=p[0:OUT_ROWS], w_o_swa=_unpad_heads(p[OUT_ROWS:2 * OUT_ROWS], SWA_HEADS, HEAD_DIM, 1).T,
               w_o_mla=_unpad_heads(p[2 * OUT_ROWS:3 * OUT_ROWS], MLA_HEADS, MLA_V, 1).T)
    for n, off, r in SMALL_FLAT:
        out[n] = p[3 * OUT_ROWS + off:3 * OUT_ROWS + off + r].reshape(_shard_shape(n))
    return out


def _w_in_row_maps():
    sp = lambda col: (col // W_IN_COLS) * W_IN_ROWS + col % W_IN_COLS
    fwd = np.full((P_W,), -1, np.int64)

    def put(t0, c0, n):
        fwd[t0:t0 + n] = [sp(c) for c in range(c0, c0 + n)]

    put(P_GA, IN_OFF[6], D_MODEL)
    put(P_GB, IN_OFF[7], D_MODEL)
    for h in range(SWA_HEADS):
        put(P_Q + LANES * h, IN_OFF[0] + HEAD_DIM * h, HEAD_DIM)
    put(P_QLAT, IN_OFF[3], Q_LORA)
    put(P_KR + KR_LANE, IN_OFF[5], MLA_ROPE)
    for h in range(SWA_KV_HEADS):
        put(P_K + LANES * h, IN_OFF[1] + HEAD_DIM * h, HEAD_DIM)
        put(P_V + LANES * h, IN_OFF[2] + HEAD_DIM * h, HEAD_DIM)
    put(P_KVLAT, IN_OFF[4], KV_LORA)
    inv = np.full((N_DEV * W_IN_ROWS,), -1, np.int64)
    inv[fwd[fwd >= 0]] = np.nonzero(fwd >= 0)[0]
    return fwd, inv


def _take_rows(src, idx, *, name, tile=2 * LANES):
    n_out, n_src, width = len(idx), src.shape[0], src.shape[1]
    assert n_out % tile == 0 and n_src % tile == 0
    n_tiles = n_out // tile
    blocks = [sorted({int(v) // tile for v in idx[i * tile:(i + 1) * tile] if v >= 0}) for i in range(n_tiles)]
    k_max = max(1, max(len(b) for b in blocks))
    tab = np.zeros((n_tiles, k_max), np.int32)
    sel = np.zeros((n_tiles, k_max, tile, tile), np.float32)
    for i, blks in enumerate(blocks):
        for m, b in enumerate(blks):
            tab[i, m] = b
            for r in range(tile):
                v = int(idx[i * tile + r])
                if v >= 0 and v // tile == b:
                    sel[i, m, r, v % tile] = 1.0

    def body(tab_ref, sel_ref, *refs):
        o_ref = refs[k_max]
        acc = jnp.dot(sel_ref[0, 0], refs[0][...], preferred_element_type=F32)
        for m in range(1, k_max):
            acc = acc + jnp.dot(sel_ref[0, m], refs[m][...], preferred_element_type=F32)
        o_ref[...] = acc.astype(o_ref.dtype)

    def src_spec(m):
        return pl.BlockSpec((tile, width), lambda i, t: (t[i * k_max + m], 0))

    return pl.pallas_call(
        body, name=name,
        grid_spec=pltpu.PrefetchScalarGridSpec(
            num_scalar_prefetch=1, grid=(n_tiles,),
            in_specs=[pl.BlockSpec((1, k_max, tile, tile), lambda i, t: (i, 0, 0, 0))] + [src_spec(m) for m in range(k_max)],
            out_specs=pl.BlockSpec((tile, width), lambda i, t: (i, 0))),
        out_shape=jax.ShapeDtypeStruct((n_out, width), src.dtype),
        compiler_params=_cparams(("parallel",)),
    )(jnp.asarray(tab.reshape(-1)), jnp.asarray(sel, src.dtype), *([src] * k_max))


def _w_in_operand(win_g):
    return _take_rows(win_g.reshape(N_DEV * W_IN_ROWS, PACK_W), _w_in_row_maps()[0], name="w_in_rows")


def _mid_operands(wout_g, woa_g, wob_g, small_g):
    def full(n, off, r):
        a = small_g[:, off:off + r].reshape((N_DEV,) + _shard_shape(n))
        return jnp.moveaxis(a, 0, 1).reshape(FULL_SHAPE[n])

    w = {n: full(n, off, r) for n, off, r in SMALL_FLAT}
    ukv = w["w_ukv"].reshape(KV_LORA, MLA_HEADS, MLA_NOPE + MLA_V)
    return dict(
        wout=wout_g.reshape(D_MODEL, D_MODEL), woa_t=woa_g.reshape(D_MODEL, -1), wob_t=wob_g.reshape(D_MODEL, -1),
        wuq=_pad_heads(w["w_uq"], MLA_HEADS, MLA_QK, 1),
        wuk=_pad_heads(ukv[:, :, :MLA_NOPE].reshape(KV_LORA, -1), MLA_HEADS, MLA_NOPE, 1),
        wuv=_pad_heads(ukv[:, :, MLA_NOPE:].reshape(KV_LORA, -1), MLA_HEADS, MLA_V, 1),
    )


def _mid_grad_pack(g):
    uk = _unpad_heads(g["wukv"][:, :1024], MLA_HEADS, MLA_NOPE, 1).reshape(KV_LORA, MLA_HEADS, MLA_NOPE)
    uv = _unpad_heads(g["wukv"][:, 1024:], MLA_HEADS, MLA_V, 1).reshape(KV_LORA, MLA_HEADS, MLA_V)
    w = dict(w_uq=_unpad_heads(g["wuq"], MLA_HEADS, MLA_QK, 1), w_ukv=jnp.concatenate([uk, uv], 2).reshape(KV_LORA, -1))
    rows = []
    for n, _, r in SMALL_FLAT:
        rr, cc = FULL_SHAPE[n]
        a = jnp.moveaxis(w[n].reshape(rr, N_DEV, cc // N_DEV), 1, 0).reshape(N_DEV, r, PACK_W)
        rows.append(jnp.pad(a, ((0, 0), (0, -r % ROW_TILE), (0, 0))).astype(WIRE_DTYPE))
    rows.append(jnp.zeros((N_DEV, OUT_ROWS - SMALL_USED, PACK_W), WIRE_DTYPE))
    blk = lambda a: a.reshape(N_DEV, OUT_ROWS, PACK_W)
    return [blk(g["wout"]), blk(g["woa_t"]), blk(g["wob_t"]), jnp.concatenate(rows, 1)]


def _w_in_grad_chunks(g_win_t):
    return _take_rows(g_win_t, _w_in_row_maps()[1], name="dw_in_rows").reshape(N_DEV, W_IN_ROWS, PACK_W)


def _local_step(x, tgt, win_t, small, weights, grads):
    s_ = x.shape[0]
    tabs = _rope_tables(s_)
    sink_b = jnp.broadcast_to(small["swa_sinks"].reshape(SWA_KV_HEADS, SWA_GROUP, 1), (SWA_KV_HEADS, SWA_GROUP, LANES))
    sink_b = jnp.pad(sink_b, ((0, 0), (0, SUBLANES - SWA_GROUP), (0, 0)))

    h, qa, ka, va, cq, ckv, kro, p = _proj_in(x, small["mix_norm_g"], win_t, small["q_norm_g"], small["kv_norm_g"], tabs)
    oa32, oa16, lse_a = _swa_fwd(qa, ka, va, sink_b)
    ops = weights.mid(oa16)
    qc, kc, vp = _mla_up(cq, ckv, kro, ops["wuq"], ops["wuk"], ops["wuv"], tabs)
    ob32, ob16, lse_b = _mla_fwd(qc, kc, vp)
    y = _attn_out_gate(oa16, ob16, ops["woa_t"], ops["wob_t"], p)
    x1 = _mm(y, ops["wout"], "nn", name="out_proj", add=x, tm=1024, tn=1024)
    wgu_t, wd = weights.late(x1)
    h2, gu, act = _ffn_in_act(x1, small["ffn_norm_g"], wgu_t)

    dx2, dx2b, dg3, _, tot = _ffn_out_loss(act, wd, x1, small["final_norm_g"].reshape(1, D_MODEL), tgt)
    g = {}
    g_wd = _mm(act, dx2b, "tn", name="dw_down", tm=FF_TILE, tn=1024, tk=2048, out_dtype=WIRE_DTYPE)
    dgu = _d_act_swiglu(dx2b, wd, gu)
    g_wgu = _mm(dgu, h2, "tn", name="dw_ffn_in", tm=FF_TILE, tn=1024, tk=2048, out_dtype=WIRE_DTYPE)
    token = grads.late(g_wgu, g_wd)
    dx1, dx1b, dg2 = _mm_norm_bwd(dgu, wgu_t, x1, small["ffn_norm_g"] + token[0:1, 0:1], dx2, name="d_h2")
    g["wout"] = _mm(y, dx1b, "tn", name="dw_out", tm=1024, tn=1024, tk=2048, out_dtype=WIRE_DTYPE)
    dta, dtb, dgab = _d_y_gate(dx1b, ops["wout"], p, oa16, ob16, ops["woa_t"], ops["wob_t"])
    doa = _mm(dta, ops["woa_t"], "nn", name="d_oa", tm=1024, tn=1024)
    g["woa_t"] = _mm(dta, oa16, "tn", name="dw_o_swa", tm=1024, tn=1024, tk=2048, out_dtype=WIRE_DTYPE)
    g["wob_t"] = _mm(dtb, ob16, "tn", name="dw_o_mla", tm=1024, tn=1024, tk=2048, out_dtype=WIRE_DTYPE)
    dob16, delta_b = _mla_d_out(dtb, ops["wob_t"], ob32)
    dqc, dkc, dvp = _mla_bwd(qc, kc, vp, dob16, lse_b, delta_b)
    dqp, dkv, dkr, dqlat, dkvlat, dgq, dgkv = _mla_up_bwd(
        dqc, dkc, dvp, ops["wuq"], jnp.concatenate([ops["wuk"], ops["wuv"]], 1), p, small["q_norm_g"], small["kv_norm_g"], tabs)
    g["wuq"] = _mm(cq, dqp, "tn", name="dw_uq", tm=Q_LORA, tn=1024, tk=2048)
    g["wukv"] = _mm(ckv, dkv, "tn", name="dw_ukv", tm=KV_LORA, tn=2048, tk=2048)
    token = grads.mid(g)
    dqa, dka, dva, dsk = _swa_bwd(qa, ka, va, sink_b + token[0:1, 0:1], oa32, doa, lse_a)
    dp = _assemble_dp(dgab, dqa, dqlat, dkr, dka, dva, dkvlat, tabs)
    token = grads.last(_mm(dp, h, "tn", name="dw_in", tm=2176, tn=1024, tk=1024, out_dtype=WIRE_DTYPE))
    gx, _, dg1 = _mm_norm_bwd(dp, win_t, x, small["mix_norm_g"], dx1, name="d_h", after=token)

    sm = dict(mix_norm_g=dg1, ffn_norm_g=dg2, final_norm_g=dg3, q_norm_g=dgq, kv_norm_g=dgkv,
              swa_sinks=dsk[:, :SWA_GROUP, 0].reshape(1, SWA_HEADS))
    return tot, gx, sm


MESH = pl.DeviceIdType.MESH
ANY = pl.BlockSpec(memory_space=pl.ANY)


def _position():
    return lax.axis_index("x"), lax.axis_index("y"), lax.axis_index("c")


def _all_gather(block, pieces, shapes, *, name):
    n_out = len(shapes)
    n_rows = sum(p[3] for p in pieces)

    def body(x_ref, *refs):
        outs, (send_sems, recv_sems, local_sem) = refs[:n_out], refs[n_out:]
        x, y, c = _position()
        me, sibling = (x, y, c), (x, y, 1 - c)
        chips = [(1 - x, y), (x, 1 - y), (1 - x, 1 - y)]

        def dst(piece, blk):
            arr, lead, _, _ = piece
            return outs[arr].at[lead(4 * blk[0] + 2 * blk[1] + blk[2])]

        def own(piece):
            return x_ref.at[pl.ds(piece[2], piece[3])]

        def copies(k, blk, to, from_input):
            return [pltpu.make_async_remote_copy(
                src_ref=own(p) if from_input else dst(p, blk), dst_ref=dst(p, blk), send_sem=send_sems.at[k],
                recv_sem=recv_sems.at[k], device_id=to, device_id_type=MESH) for p in pieces]

        gathered_rows = x_ref.at[pl.ds(0, n_rows)]

        def whole_block(k):
            return pltpu.make_async_remote_copy(src_ref=gathered_rows, dst_ref=gathered_rows, send_sem=send_sems.at[k],
                                                recv_sem=recv_sems.at[k], device_id=me, device_id_type=MESH)

        for p in pieces:
            pltpu.make_async_copy(own(p), dst(p, me), local_sem).start()
        for cp in copies(0, me, sibling, True):
            cp.start()
        for j, chip in enumerate(chips):
            for cp in copies(1 + j, me, (*chip, c), True):
                cp.start()
        for j, chip in enumerate(chips):
            whole_block(1 + j).wait_recv()
            for cp in copies(4 + j, (*chip, c), sibling, False):
                cp.start()
        whole_block(0).wait_recv()
        for j in range(3):
            whole_block(4 + j).wait_recv()
        for k in range(7):
            whole_block(k).wait_send()
        pltpu.make_async_copy(gathered_rows, gathered_rows, local_sem).wait()

    return pl.pallas_call(
        body, name=name, out_shape=[jax.ShapeDtypeStruct(s, block.dtype) for s in shapes], in_specs=[ANY],
        out_specs=[ANY] * n_out,
        scratch_shapes=[pltpu.SemaphoreType.DMA((7,)), pltpu.SemaphoreType.DMA((7,)), pltpu.SemaphoreType.DMA],
    )(block)


HBM = pl.BlockSpec(memory_space=pltpu.HBM)
SEM = pl.BlockSpec(memory_space=pltpu.SEMAPHORE)
TILE_DEVS = FF_TILE // FF_COLS
GU_SHAPE = (2, 2, TILE_DEVS, FF_COLS, PACK_W)


def _gate_slab(d):
    return (d // TILE_DEVS, 0, d % TILE_DEVS)


def _up_slab(d):
    return (d // TILE_DEVS, 1, d % TILE_DEVS)
D_SHAPE = (N_DEV, FF_COLS, PACK_W)
LAND_SHAPE = (N_DEV, LATE_ROWS, PACK_W)


def _split_params():
    return pltpu.CompilerParams(has_side_effects=pltpu.SideEffectType.DATAFLOW_SIDE_EFFECTING)


def _peer(x, y, c, k):
    return ((1 - x) if k & 4 else x, (1 - y) if k & 2 else y, (1 - c) if k & 1 else c)


def _empty_hbm(shape, dtype):
    return pltpu.with_memory_space_constraint(lax.empty(shape, dtype), pltpu.HBM)


def _wait_all(rows, send_sems, recv_sems, me):
    for k in range(N_DEV - 1):
        cp = pltpu.make_async_remote_copy(src_ref=rows, dst_ref=rows, send_sem=send_sems.at[k], recv_sem=recv_sems.at[k],
                                          device_id=me, device_id_type=MESH)
        cp.wait_send()
        cp.wait_recv()


def _token_shape():
    return jax.ShapeDtypeStruct((SUBLANES, LANES), F32)


def _gather_start(pack, row0, pieces, shapes, *, name):
    n = len(shapes)

    def body(*refs):
        p_ref, bufs, send_sems, recv_sems, token = refs[0], refs[1:1 + n], refs[1 + n], refs[2 + n], refs[-1]
        x, y, c = _position()
        me = 4 * x + 2 * y + c
        for k in range(1, N_DEV):
            off = row0
            for buf, lead, rows in pieces:
                pltpu.make_async_remote_copy(
                    src_ref=p_ref.at[pl.ds(off, rows)], dst_ref=bufs[buf].at[lead(me)], send_sem=send_sems.at[k - 1],
                    recv_sem=recv_sems.at[k - 1], device_id=_peer(x, y, c, k), device_id_type=MESH).start()
                off += rows
        token[...] = jnp.zeros_like(token)

    sems, dt = pltpu.SemaphoreType.DMA((N_DEV - 1,)), pack.dtype
    return pl.pallas_call(
        body, name=name,
        out_shape=(sems, sems, pltpu.HBM(pack.shape, dt)) + tuple(pltpu.HBM(s, dt) for s in shapes) + (_token_shape(),),
        in_specs=(HBM,) * (1 + n), out_specs=(SEM, SEM) + (HBM,) * (1 + n) + (pl.BlockSpec(memory_space=pltpu.VMEM),),
        input_output_aliases={i: 2 + i for i in range(1 + n)}, compiler_params=_split_params(),
    )(pltpu.with_memory_space_constraint(pack, pltpu.HBM), *[_empty_hbm(s, dt) for s in shapes])


def _gather_wait(started, row0, n_rows, after, *, name):
    send_sems, recv_sems, pack, *bufs = started[:-1]
    n = len(bufs)

    def body(*refs):
        _wait_all(refs[0].at[pl.ds(row0, n_rows)], refs[1 + n], refs[2 + n], _position())

    outs = pl.pallas_call(
        body, name=name, out_shape=tuple(pltpu.HBM(a.shape, a.dtype) for a in (pack, *bufs)),
        in_specs=(HBM,) * (1 + n) + (SEM, SEM, ANY), out_specs=(HBM,) * (1 + n),
        input_output_aliases={i: i for i in range(1 + n)}, compiler_params=_split_params(),
    )(pack, *bufs, send_sems, recv_sems, after)
    return outs[0], outs[1:]


def _scatter_start(srcs, pieces, *, name):
    n = len(srcs)
    land_shape = (N_DEV, sum(p[2] for p in pieces), PACK_W)

    def body(*refs):
        src_refs, land_ref, send_sems, recv_sems, token = refs[:n], refs[n], refs[n + 1], refs[n + 2], refs[-1]
        x, y, c = _position()
        me = 4 * x + 2 * y + c
        for k in range(1, N_DEV):
            px, py, pc = _peer(x, y, c, k)
            off = 0
            for si, lead, rows in pieces:
                pltpu.make_async_remote_copy(
                    src_ref=src_refs[si].at[lead(4 * px + 2 * py + pc)], dst_ref=land_ref.at[me, pl.ds(off, rows)],
                    send_sem=send_sems.at[k - 1], recv_sem=recv_sems.at[k - 1], device_id=(px, py, pc),
                    device_id_type=MESH).start()
                off += rows
        token[...] = jnp.zeros_like(token)

    sems, dt = pltpu.SemaphoreType.DMA((N_DEV - 1,)), srcs[0].dtype
    return pl.pallas_call(
        body, name=name,
        out_shape=(sems, sems) + tuple(pltpu.HBM(a.shape, dt) for a in srcs) + (pltpu.HBM(land_shape, dt), _token_shape()),
        in_specs=(HBM,) * (n + 1), out_specs=(SEM, SEM) + (HBM,) * (n + 1) + (pl.BlockSpec(memory_space=pltpu.VMEM),),
        input_output_aliases={i: 2 + i for i in range(n + 1)}, compiler_params=_split_params(),
    )(*[pltpu.with_memory_space_constraint(a, pltpu.HBM) for a in srcs], _empty_hbm(land_shape, dt))


def _scatter_wait(started, after, *, name):
    send_sems, recv_sems, *bufs = started[:-1]
    n = len(bufs)

    def body(*refs):
        _wait_all(refs[n - 1].at[0], refs[n], refs[n + 1], _position())

    return pl.pallas_call(
        body, name=name, out_shape=tuple(pltpu.HBM(a.shape, a.dtype) for a in bufs),
        in_specs=(HBM,) * n + (SEM, SEM, ANY), out_specs=(HBM,) * n, input_output_aliases={i: i for i in range(n)},
        compiler_params=_split_params(),
    )(*bufs, send_sems, recv_sems, after)


def _peer_sum(own, own_lead, land, block, rows, idx, *, name):
    owns = list(own) if isinstance(own, (list, tuple)) else [own]
    n, lead_rank = len(owns), owns[0].ndim - 2

    def body(idx_ref, *refs):
        own_refs, land_refs, o_ref = refs[:n], refs[n:n + N_DEV - 1], refs[n + N_DEV - 1]
        for j in range(n):
            rs_ = slice(j * rows, (j + 1) * rows)
            acc = own_refs[j][(0,) * lead_rank].astype(F32)
            for k in range(N_DEV - 1):
                acc = acc + land_refs[k][0, rs_].astype(F32)
            o_ref[rs_] = acc

    own_spec = pl.BlockSpec((1,) * lead_rank + (rows, PACK_W), lambda i, t: own_lead(t[0]) + (0, 0))

    def land_spec(k):
        return pl.BlockSpec((1, n * rows, PACK_W), lambda i, t: (t[k + 1], block, 0))

    return pl.pallas_call(
        body, name=name,
        grid_spec=pltpu.PrefetchScalarGridSpec(
            num_scalar_prefetch=1, grid=(1,), in_specs=[own_spec] * n + [land_spec(k) for k in range(N_DEV - 1)],
            out_specs=pl.BlockSpec((n * rows, PACK_W), lambda i, t: (0, 0))),
        out_shape=jax.ShapeDtypeStruct((n * rows, PACK_W), F32), compiler_params=_cparams(("arbitrary",)),
    )(idx, *owns, *([land] * (N_DEV - 1)))


def _sum_adamw(own, own_lead, land, block, rows, idx, w, m, v, *, name):
    lead_rank, r = own.ndim - 2, w.shape[1]

    def body(idx_ref, own_ref, *refs):
        land_refs, (w_ref, m_ref, v_ref), outs = refs[:N_DEV - 1], refs[N_DEV - 1:N_DEV + 2], refs[N_DEV + 2:]
        g = own_ref[(0,) * lead_rank + (slice(0, r),)].astype(F32)
        for k in range(N_DEV - 1):
            g = g + land_refs[k][0, 0:r].astype(F32)
        for o_ref, val in zip(outs, (g,) + tuple(_adamw(w_ref[0], g, m_ref[0], v_ref[0]))):
            o_ref[0] = val

    own_spec = pl.BlockSpec((1,) * lead_rank + (rows, PACK_W), lambda i, t: own_lead(t[0]) + (0, 0))
    shard = pl.BlockSpec((1, r, PACK_W), lambda i, t: (0, 0, 0))

    def land_spec(k):
        return pl.BlockSpec((1, rows, PACK_W), lambda i, t: (t[k + 1], block, 0))

    return pl.pallas_call(
        body, name=name,
        grid_spec=pltpu.PrefetchScalarGridSpec(
            num_scalar_prefetch=1, grid=(1,),
            in_specs=[own_spec] + [land_spec(k) for k in range(N_DEV - 1)] + [shard] * 3, out_specs=[shard] * 4),
        out_shape=[jax.ShapeDtypeStruct((1, r, PACK_W), F32)] * 4, compiler_params=_cparams(("arbitrary",)),
    )(idx, own, *([land] * (N_DEV - 1)), w, m, v)


def _adamw(w, g, m, v):
    m = ADAM_B1 * m + (1.0 - ADAM_B1) * g
    v = ADAM_B2 * v + (1.0 - ADAM_B2) * (g * g)
    m_hat = m / (1.0 - ADAM_B1 ** ADAM_STEP)
    v_hat = v / (1.0 - ADAM_B2 ** ADAM_STEP)
    delta = -ADAM_LR * (m_hat / (jnp.sqrt(v_hat) + ADAM_EPS) + ADAM_WD * w)
    return delta, m, v


def _adamw_call(w, g, m, v, *, name, max_rows=256):
    _, r, c_ = w.shape
    tr = max_rows if r > max_rows and r % max_rows == 0 else r

    def body(w_ref, g_ref, m_ref, v_ref, d_ref, mo_ref, vo_ref):
        d, mn, vn = _adamw(w_ref[0], g_ref[...], m_ref[0], v_ref[0])
        d_ref[0] = d
        mo_ref[0] = mn
        vo_ref[0] = vn

    row3 = pl.BlockSpec((1, tr, c_), lambda i: (0, i, 0))
    shp = jax.ShapeDtypeStruct((1, r, c_), F32)
    return pl.pallas_call(
        body, name=name, grid=(r // tr,), in_specs=[row3, pl.BlockSpec((tr, c_), lambda i: (i, 0)), row3, row3],
        out_specs=[row3] * 3, out_shape=[shp] * 3, compiler_params=_cparams(("parallel",)),
    )(w, g, m, v)


SMALL = ("mix_norm_g", "ffn_norm_g", "final_norm_g", "q_norm_g", "kv_norm_g", "swa_sinks")
SMALL_W = dict(mix_norm_g=1024, ffn_norm_g=1024, final_norm_g=1024, q_norm_g=Q_LORA, kv_norm_g=KV_LORA, swa_sinks=SWA_HEADS)


def _small_adamw(parts, w, m, v):
    ns = len(SMALL)

    def body(p_ref, *refs):
        ins, outs = refs[:3 * ns], refs[3 * ns:]
        tot = p_ref[0]
        for dev in range(1, N_DEV):
            tot = tot + p_ref[dev]
        for k, n in enumerate(SMALL):
            g = jnp.sum(tot[k * SUBLANES:(k + 1) * SUBLANES, :SMALL_W[n]], axis=0, keepdims=True)
            res = _adamw(ins[k][...], g, ins[ns + k][...], ins[2 * ns + k][...])
            for j, r in enumerate((g,) + tuple(res)):
                outs[j * ns + k][...] = r
        outs[4 * ns][...] = jnp.sum(tot[ns * SUBLANES:(ns + 1) * SUBLANES, 0:1], axis=0, keepdims=True)

    shapes = [jax.ShapeDtypeStruct((1, SMALL_W[n]), F32) for n in SMALL]
    vm = pl.BlockSpec(memory_space=pltpu.VMEM)
    out = pl.pallas_call(
        body, name="small_adamw", in_specs=[vm] * (1 + 3 * ns), out_specs=[vm] * (4 * ns + 1),
        out_shape=shapes * 4 + [jax.ShapeDtypeStruct((1, 1), F32)],
    )(parts, *[d[n] for d in (w, m, v) for n in SMALL])
    return [dict(zip(SMALL, out[j * ns:(j + 1) * ns])) for j in range(4)] + [out[4 * ns]]


def _small_pack(d, rows_each):
    parts = [jnp.pad(d[n].astype(F32), ((0, 0), (0, PACK_W - SMALL_W[n]))) for n in SMALL]
    out = jnp.concatenate(parts, 0)
    pad = -out.shape[0] % SUBLANES
    return jnp.pad(out, ((0, pad), (0, 0)))


def kernel(x, mix_norm_g, w_in, swa_sinks, q_norm_g, w_uq, kv_norm_g, w_ukv, w_o_swa, w_o_mla, w_out, ffn_norm_g, w_gate, w_up, w_down, final_norm_g, loss_target, m_mix_norm_g, m_w_in, m_swa_sinks, m_q_norm_g, m_w_uq, m_kv_norm_g, m_w_ukv, m_w_o_swa, m_w_o_mla, m_w_out, m_ffn_norm_g, m_w_gate, m_w_up, m_w_down, m_final_norm_g, v_mix_norm_g, v_w_in, v_swa_sinks, v_q_norm_g, v_w_uq, v_kv_norm_g, v_w_ukv, v_w_o_swa, v_w_o_mla, v_w_out, v_ffn_norm_g, v_w_gate, v_w_up, v_w_down, v_final_norm_g):
    big_w = dict(w_in=w_in[0], w_uq=w_uq[0], w_ukv=w_ukv[0], w_o_swa=w_o_swa[0], w_o_mla=w_o_mla[0], w_out=w_out[0],
                 w_gate=w_gate[0], w_up=w_up[0], w_down=w_down[0])
    big_w3 = dict(w_in=w_in, w_uq=w_uq, w_ukv=w_ukv, w_o_swa=w_o_swa, w_o_mla=w_o_mla, w_out=w_out, w_gate=w_gate, w_up=w_up,
                  w_down=w_down)
    big_m = dict(w_in=m_w_in, w_uq=m_w_uq, w_ukv=m_w_ukv, w_o_swa=m_w_o_swa, w_o_mla=m_w_o_mla, w_out=m_w_out,
                 w_gate=m_w_gate, w_up=m_w_up, w_down=m_w_down)
    big_v = dict(w_in=v_w_in, w_uq=v_w_uq, w_ukv=v_w_ukv, w_o_swa=v_w_o_swa, w_o_mla=v_w_o_mla, w_out=v_w_out,
                 w_gate=v_w_gate, w_up=v_w_up, w_down=v_w_down)
    small_w = dict(mix_norm_g=mix_norm_g, ffn_norm_g=ffn_norm_g, final_norm_g=final_norm_g.reshape(1, D_MODEL),
                   q_norm_g=q_norm_g, kv_norm_g=kv_norm_g, swa_sinks=swa_sinks)
    small_m = dict(mix_norm_g=m_mix_norm_g, ffn_norm_g=m_ffn_norm_g, final_norm_g=m_final_norm_g.reshape(1, D_MODEL),
                   q_norm_g=m_q_norm_g, kv_norm_g=m_kv_norm_g, swa_sinks=m_swa_sinks)
    small_v = dict(mix_norm_g=v_mix_norm_g, ffn_norm_g=v_ffn_norm_g, final_norm_g=v_final_norm_g.reshape(1, D_MODEL),
                   q_norm_g=v_q_norm_g, kv_norm_g=v_kv_norm_g, swa_sinks=v_swa_sinks)

    px, py, pc = _position()
    me = 4 * px + 2 * py + pc
    idx = jnp.stack([me] + [4 * qx + 2 * qy + qc for qx, qy, qc in (_peer(px, py, pc, k) for k in range(1, N_DEV))])
    idx = idx.astype(jnp.int32)

    dev = lambda d: (d,)
    pack = _wire_pack(big_w, WIRE_DTYPE)
    win_g, = _all_gather(pack, ((0, dev, 0, W_IN_ROWS),), ((N_DEV, W_IN_ROWS, PACK_W),), name="ag_early")
    mid_pieces = tuple((b, dev, OUT_ROWS) for b in range(MID_BLOCKS))
    ag_mid = _gather_start(pack, W_IN_ROWS, mid_pieces, ((N_DEV, OUT_ROWS, PACK_W),) * MID_BLOCKS, name="ag_mid_start")
    ag = {}

    def own_rows(r0, r1, shape):
        return pack[r0:r1].reshape(shape)

    def mid_weights(after):
        pack_mid, blocks = _gather_wait(ag_mid, W_IN_ROWS, MID_ROWS, after, name="ag_mid_wait")
        ag["late"] = _gather_start(pack_mid, EARLY_ROWS, ((0, _gate_slab, FF_COLS), (0, _up_slab, FF_COLS), (1, dev, FF_COLS)),
                                   (GU_SHAPE, D_SHAPE), name="ag_late_start")
        row0 = lambda b: W_IN_ROWS + b * OUT_ROWS
        ops = _mid_operands(*[lax.dynamic_update_slice(blk, own_rows(row0(b), row0(b + 1), (1, OUT_ROWS, PACK_W)), (me, 0, 0))
                              for b, blk in enumerate(blocks)])
        ops["wuq"] = ops["wuq"] + ag["late"][-1][0:1, 0:1].astype(ops["wuq"].dtype)
        return ops

    def late_weights(after):
        _, (gu, d) = _gather_wait(ag["late"], EARLY_ROWS, LATE_ROWS, after, name="ag_late_wait")
        slab = (1, 1, 1, FF_COLS, PACK_W)
        gu = lax.dynamic_update_slice(gu, own_rows(EARLY_ROWS, EARLY_ROWS + FF_COLS, slab), _gate_slab(me) + (0, 0))
        gu = lax.dynamic_update_slice(gu, own_rows(EARLY_ROWS + FF_COLS, EARLY_ROWS + 2 * FF_COLS, slab), _up_slab(me) + (0, 0))
        d = lax.dynamic_update_slice(d, own_rows(EARLY_ROWS + 2 * FF_COLS, PACK_ROWS, (1, FF_COLS, PACK_W)), (me, 0, 0))
        return gu.reshape(2 * D_FF, D_MODEL), d.reshape(D_FF, D_MODEL)

    rs = {}

    def late_grads(g_gu, g_d):
        rs["late"] = _scatter_start([g_gu.reshape(GU_SHAPE), g_d.reshape(D_SHAPE)],
                                    ((0, _gate_slab, FF_COLS), (0, _up_slab, FF_COLS), (1, dev, FF_COLS)),
                                    name="rs_late_start")
        return rs["late"][-1]

    def mid_grads(g):
        rs["mid"] = _scatter_start(_mid_grad_pack(g), mid_pieces, name="rs_mid_start")
        return rs["mid"][-1]

    def last_grads(g_win_t):
        rs["last"] = _scatter_start([_w_in_grad_chunks(g_win_t)], ((0, dev, W_IN_ROWS),), name="rs_last_start")
        return rs["last"][-1]

    first_w = dict(small_w, mix_norm_g=mix_norm_g + ag_mid[-1][0:1, 0:1])
    loss_tot, gx, g_small = _local_step(
        x[0], loss_target[0], _w_in_operand(win_g), first_w, types.SimpleNamespace(mid=mid_weights, late=late_weights),
        types.SimpleNamespace(late=late_grads, mid=mid_grads, last=last_grads))

    loss_rows = jnp.pad(loss_tot[0:1, 0:1], ((0, SUBLANES - 1), (0, PACK_W - 1)))
    small_rows = jnp.concatenate([_small_pack(g_small_rows(g_small), SUBLANES), loss_rows], 0)
    n_small = small_rows.shape[0]
    ag_small = _gather_start(small_rows, 0, ((0, dev, n_small),), ((N_DEV, n_small, PACK_W),), name="ag_small_start")

    g_gu, g_d, land_late = _scatter_wait(rs["late"], ag_small[-1], name="rs_late_wait")
    *g_mid, land_mid = _scatter_wait(rs["mid"], ag_small[-1], name="rs_mid_wait")
    g_win, land_last = _scatter_wait(rs["last"], ag_small[-1], name="rs_last_wait")
    swap = lambda a: jnp.swapaxes(a, 1, 2)
    same = lambda a: a
    chunks = dict(w_gate=(swap, g_gu, _gate_slab, land_late, 0, FF_COLS), w_up=(swap, g_gu, _up_slab, land_late, 1, FF_COLS),
                  w_down=(same, g_d, dev, land_late, 2, FF_COLS), w_in=(swap, g_win, dev, land_last, 0, W_IN_ROWS))
    gw, dw, mw, vw = {}, {}, {}, {}
    for n, (view, own, lead, land, blk, rows) in chunks.items():
        res = _sum_adamw(own, lead, land, blk, rows, idx, view(big_w3[n]), view(big_m[n]), view(big_v[n]), name="adamw_" + n)
        gw[n], dw[n], mw[n], vw[n] = (view(r) for r in res)
    g_nat = _mid_unpack(_peer_sum(g_mid, dev, land_mid, 0, OUT_ROWS, idx, name="rs_sum_mid"))
    for n, g in g_nat.items():
        gw[n] = g[None]
        dw[n], mw[n], vw[n] = _adamw_call(big_w3[n], g, big_m[n], big_v[n], name="adamw_" + n)

    own_small, (parts,) = _gather_wait(ag_small, 0, n_small, vw[n], name="ag_small_wait")
    parts = lax.dynamic_update_slice(parts, own_small[None], (me, 0, 0))
    gs, ds, ms, vs, loss = _small_adamw(parts, small_w, small_m, small_v)
    loss = loss[0, 0]
    for d in (gs, ds, ms, vs):
        d["final_norm_g"] = d["final_norm_g"].reshape(D_MODEL)

    order = ("mix_norm_g", "w_in", "swa_sinks", "q_norm_g", "w_uq", "kv_norm_g", "w_ukv", "w_o_swa", "w_o_mla", "w_out",
             "ffn_norm_g", "w_gate", "w_up", "w_down", "final_norm_g")

    def leaves(big, small):
        return [big[n] if n in big else small[n] for n in order]

    return (loss, gx[None], *leaves(gw, gs), *leaves(dw, ds), *leaves(mw, ms), *leaves(vw, vs))


def g_small_rows(g_small):
    out = dict(g_small)
    out["swa_sinks"] = jnp.pad(g_small["swa_sinks"], ((0, SUBLANES - 1), (0, 0)))
    return out
```

```python
import types

import numpy as np
import jax
import jax.numpy as jnp
from jax import lax
from jax.experimental import pallas as pl
from jax.experimental.pallas import tpu as pltpu

F32 = jnp.float32
MXU_DTYPE = jnp.bfloat16
WIRE_DTYPE = jnp.bfloat16

D_MODEL = 1024
EPS = 1e-6
ROPE_THETA = 10000.0
BLOCK = 128
HEAD_DIM = 64
SWA_HEADS = 8
SWA_KV_HEADS = 2
SWA_GROUP = SWA_HEADS // SWA_KV_HEADS
MLA_HEADS = 8
MLA_NOPE = 64
MLA_ROPE = 32
MLA_V = 64
MLA_QK = MLA_NOPE + MLA_ROPE
Q_LORA = 384
KV_LORA = 256
D_FF = 2816
IN_SIZES = (512, 128, 128, Q_LORA, KV_LORA, MLA_ROPE, D_MODEL, D_MODEL)
IN_OFF = tuple(int(v) for v in np.cumsum((0,) + IN_SIZES))
ADAM_LR, ADAM_B1, ADAM_B2, ADAM_EPS, ADAM_WD, ADAM_STEP = 0.001, 0.9, 0.999, 1e-08, 0.01, 10

LANES = 128
SUBLANES = 8
VMEM_LIMIT = 48 * 1024 * 1024
N_DEV = 8

P_GA, P_GB, P_Q, P_QLAT, P_KR, P_K, P_V, P_KVLAT, P_W = 0, 1024, 2048, 3072, 3456, 3584, 3840, 4096, 4352
KR_LANE = 64

LOG2E = 1.4426950408889634

NT = (((1,), (1,)), ((), ()))
NN = (((1,), (0,)), ((), ()))
TN = (((0,), (0,)), ((), ()))


def _cparams(sem):
    return pltpu.CompilerParams(dimension_semantics=sem, vmem_limit_bytes=VMEM_LIMIT)


def _mm(a, b, mode, *, name, out_dtype=F32, add=None, tm=512, tn=512, tk=None):
    if mode == "nn":
        (M, K), (K2, N) = a.shape, b.shape
    elif mode == "nt":
        (M, K), (N, K2) = a.shape, b.shape
    else:
        (K, M), (K2, N) = a.shape, b.shape
    assert K == K2, (a.shape, b.shape, mode)
    tm, tn, tk = min(tm, M), min(tn, N), K if tk is None else min(tk, K)
    assert M % tm == 0 and N % tn == 0 and K % tk == 0, (M, N, K, tm, tn, tk)
    nk = K // tk
    dn = {"nn": NN, "nt": NT, "tn": TN}[mode]
    if mode == "tn":
        a_spec = pl.BlockSpec((tk, tm), lambda i, j, k: (k, i))
    else:
        a_spec = pl.BlockSpec((tm, tk), lambda i, j, k: (i, k))
    once = dict(pipeline_mode=pl.Buffered(1)) if (nk == 1 and tn == N) else {}
    if mode == "nt":
        b_spec = pl.BlockSpec((tn, tk), lambda i, j, k: (j, k), **once)
    else:
        b_spec = pl.BlockSpec((tk, tn), lambda i, j, k: (k, j), **once)
    o_spec = pl.BlockSpec((tm, tn), lambda i, j, k: (i, j))
    has_add = add is not None

    def body(*refs):
        a_ref, b_ref = refs[0], refs[1]
        add_ref = refs[2] if has_add else None
        o_ref = refs[2 + has_add]
        p = lax.dot_general(a_ref[...], b_ref[...], dn, preferred_element_type=F32)

        def finish(acc):
            if has_add:
                acc = acc + add_ref[...]
            o_ref[...] = acc.astype(o_ref.dtype)

        if nk == 1:
            finish(p)
        else:
            acc_ref = refs[-1]
            k = pl.program_id(2)

            @pl.when(k == 0)
            def _():
                acc_ref[...] = p

            @pl.when((k > 0) & (k < nk - 1))
            def _():
                acc_ref[...] += p

            @pl.when(k == nk - 1)
            def _():
                finish(acc_ref[...] + p)

    ins = [a, b] + ([add] if has_add else [])
    return pl.pallas_call(
        body, name=name, grid=(M // tm, N // tn, nk), in_specs=[a_spec, b_spec] + ([o_spec] if has_add else []), out_specs=o_spec,
        out_shape=jax.ShapeDtypeStruct((M, N), out_dtype),
        scratch_shapes=[pltpu.VMEM((tm, tn), F32)] if nk > 1 else [],
        compiler_params=_cparams(("parallel", "parallel", "arbitrary")),
    )(*ins)


def _rows(ts, w, cb=0):
    return pl.BlockSpec((ts, w), lambda i: (i, cb))


def _const(r, w):
    return pl.BlockSpec((r, w), lambda i: (0, 0))


def _sublane_sum(v):
    ts, c = v.shape
    return jnp.sum(v.reshape(ts // SUBLANES, SUBLANES, c), axis=0)


def _sigmoid(v):
    return 1.0 / (1.0 + jnp.exp(-v))


def _rope(v, cos, s_up, s_dn, up, dn):
    return v * cos + pltpu.roll(v, up, 1) * s_up + pltpu.roll(v, dn, 1) * s_dn


def _rope_t(dv, cos, s_up, s_dn, up, dn):
    return dv * cos + pltpu.roll(dv * s_up, dn, 1) + pltpu.roll(dv * s_dn, up, 1)


def _rope_tables(seq):
    pos = np.arange(seq, dtype=np.float32)[:, None]

    def base(dim):
        inv = np.float32(ROPE_THETA) ** (-np.arange(0, dim, 2, dtype=np.float32) / np.float32(dim))
        ang = (pos * inv.astype(np.float32)[None, :]).astype(np.float32)
        return np.cos(ang).astype(np.float32), np.sin(ang).astype(np.float32)

    z = lambda n: np.zeros((seq, n), np.float32)
    ca, sa = base(HEAD_DIM)
    a_cos = np.concatenate([ca, ca, z(64)], 1)
    a_up = np.concatenate([-sa, z(96)], 1)
    a_dn = np.concatenate([z(32), sa, z(64)], 1)
    cb, sb = base(MLA_ROPE)
    one = np.ones((seq, 64), np.float32)
    q_cos = np.concatenate([one, cb, cb, z(32)], 1)
    k_cos = np.concatenate([z(64), cb, cb, z(32)], 1)
    b_up = np.concatenate([z(64), -sb, z(48)], 1)
    b_dn = np.concatenate([z(80), sb, z(32)], 1)
    return tuple(jnp.asarray(t) for t in (a_cos, a_up, a_dn, q_cos, k_cos, b_up, b_dn))


def _rms(v, g):
    return v * lax.rsqrt(jnp.mean(v * v, axis=-1, keepdims=True) + EPS) * g


def _rms_bwd(v, g, d):
    r = lax.rsqrt(jnp.mean(v * v, axis=-1, keepdims=True) + EPS)
    xh = v * r
    dxh = d * g
    return r * (dxh - xh * jnp.mean(dxh * xh, axis=-1, keepdims=True)), d * xh


F_GA, F_GB, F_KVLAT, F_QLAT, F_W = 0, 1024, 2048, 2304, 2688


def _proj_in(x, g, w_t, gq, gkv, tabs, *, tm=512):
    s_, c = x.shape
    a_cos, a_up, a_dn, _, k_cos, b_up, b_dn = tabs

    def body(x_ref, g_ref, w_ref, gq_ref, gkv_ref, ac, au, ad, kc, bu, bd,
             h_ref, qa_ref, ka_ref, va_ref, cq_ref, ckv_ref, kro_ref, pf_ref):
        h = _rms(x_ref[...], g_ref[...]).astype(h_ref.dtype)
        h_ref[...] = h
        mm = lambda a, b: lax.dot_general(h, w_ref[a:b, :], NT, preferred_element_type=F32)
        pf_ref[:, F_GA:F_KVLAT] = mm(P_GA, P_Q)
        c_, u_, d_ = ac[...], au[...], ad[...]
        q = mm(P_Q, P_QLAT)
        for hd in range(SWA_HEADS):
            sl = slice(hd * LANES, (hd + 1) * LANES)
            qa_ref[:, sl] = _rope(q[:, sl], c_, u_, d_, 96, 32).astype(qa_ref.dtype)
        kv = mm(P_KR, P_KVLAT)
        kro_ref[...] = _rope(kv[:, :LANES], kc[...], bu[...], bd[...], 112, 16)
        for hd in range(SWA_KV_HEADS):
            sl = slice((1 + hd) * LANES, (2 + hd) * LANES)
            ka_ref[:, hd * LANES:(hd + 1) * LANES] = _rope(kv[:, sl], c_, u_, d_, 96, 32).astype(ka_ref.dtype)
        va_ref[...] = kv[:, P_V - P_KR:].astype(va_ref.dtype)
        for a, b, f0, gref, dst in ((P_QLAT, P_KR, F_QLAT, gq_ref, cq_ref), (P_KVLAT, P_W, F_KVLAT, gkv_ref, ckv_ref)):
            v = mm(a, b)
            pf_ref[:, f0:f0 + b - a] = v
            r = lax.rsqrt(jnp.mean(v * v, axis=-1, keepdims=True) + EPS)
            dst[...] = (v * r * gref[...]).astype(dst.dtype)

    tab = _rows(tm, LANES)
    widths = (c, SWA_HEADS * LANES, SWA_KV_HEADS * LANES, SWA_KV_HEADS * LANES, Q_LORA, KV_LORA)
    return pl.pallas_call(
        body, name="proj_in", grid=(s_ // tm,),
        in_specs=[_rows(tm, c), _const(1, c), pl.BlockSpec((P_W, c), lambda i: (0, 0), pipeline_mode=pl.Buffered(1)),
                  _const(1, Q_LORA), _const(1, KV_LORA), tab, tab, tab, tab, tab, tab],
        out_specs=[_rows(tm, w) for w in widths] + [tab, _rows(tm, F_W)],
        out_shape=[jax.ShapeDtypeStruct((s_, w), MXU_DTYPE) for w in widths]
        + [jax.ShapeDtypeStruct((s_, LANES), F32), jax.ShapeDtypeStruct((s_, F_W), F32)],
        compiler_params=_cparams(("parallel",)),
    )(x, g, w_t, gq, gkv, a_cos, a_up, a_dn, k_cos, b_up, b_dn)


def _mm_norm_bwd(a, b, x, g, res, *, name, after=None, tm=512):
    s_, kk = a.shape
    c = b.shape[1]
    has_after = after is not None

    def body(*refs):
        a_ref, b_ref, x_ref, g_ref, res_ref = refs[:5]
        dx_ref, dxb_ref, dg_ref = refs[5 + has_after:]
        d = jnp.dot(a_ref[...], b_ref[...], preferred_element_type=F32)
        dx, gg = _rms_bwd(x_ref[...], g_ref[...], d)
        dx = dx + res_ref[...]
        dx_ref[...] = dx
        dxb_ref[...] = dx.astype(dxb_ref.dtype)

        @pl.when(pl.program_id(0) == 0)
        def _():
            dg_ref[...] = jnp.zeros(dg_ref.shape, F32)

        dg_ref[...] += _sublane_sum(gg)

    row = _rows(tm, c)
    in_specs = [_rows(tm, kk), pl.BlockSpec((kk, c), lambda i: (0, 0), pipeline_mode=pl.Buffered(1)), row, _const(1, c), row]
    return pl.pallas_call(
        body, name=name, grid=(s_ // tm,), in_specs=in_specs + ([pl.BlockSpec(memory_space=pl.ANY)] if has_after else []),
        out_specs=[row, row, _const(SUBLANES, c)],
        out_shape=[jax.ShapeDtypeStruct((s_, c), F32), jax.ShapeDtypeStruct((s_, c), MXU_DTYPE),
                   jax.ShapeDtypeStruct((SUBLANES, c), F32)],
        compiler_params=_cparams(("arbitrary",)),
    )(*([a, b, x, g, res] + ([after] if has_after else [])))


def _mla_up(cq, ckv, kro, wuq, wuk, wuv, tabs, *, ts=512):
    s_ = cq.shape[0]
    _, _, _, q_cos, _, b_up, b_dn = tabs

    def body(cq_ref, ckv_ref, kr_ref, wq_ref, wk_ref, wv_ref, qc, bu, bd, qo_ref, ko_ref, vo_ref):
        c_, u_, d_ = qc[...], bu[...], bd[...]
        kr = kr_ref[...]
        ckv_ = ckv_ref[...]
        vo_ref[...] = jnp.dot(ckv_, wv_ref[...], preferred_element_type=F32).astype(vo_ref.dtype)
        q = jnp.dot(cq_ref[...], wq_ref[...], preferred_element_type=F32)
        k = jnp.dot(ckv_, wk_ref[...], preferred_element_type=F32)
        for h in range(MLA_HEADS):
            sl = slice(h * LANES, (h + 1) * LANES)
            qo_ref[:, sl] = _rope(q[:, sl], c_, u_, d_, 112, 16).astype(qo_ref.dtype)
            ko_ref[:, sl] = (k[:, sl] + kr).astype(ko_ref.dtype)

    tab, out = _rows(ts, LANES), _rows(ts, 1024)
    return pl.pallas_call(
        body, name="mla_up", grid=(s_ // ts,),
        in_specs=[_rows(ts, Q_LORA), _rows(ts, KV_LORA), tab, _const(Q_LORA, 1024), _const(KV_LORA, 1024),
                  _const(KV_LORA, 1024), tab, tab, tab],
        out_specs=[out, out, out], out_shape=[jax.ShapeDtypeStruct((s_, 1024), MXU_DTYPE)] * 3,
        compiler_params=_cparams(("parallel",)),
    )(cq, ckv, kro, wuq, wuk, wuv, q_cos, b_up, b_dn)


def _mla_up_bwd(dqc, dkc, dvp, wuq, wukv, p, gq, gkv, tabs, *, ts=512):
    s_ = dqc.shape[0]
    _, _, _, q_cos, k_cos, b_up, b_dn = tabs

    def body(dq_ref, dk_ref, dv_ref, wq_ref, wkv_ref, ql_ref, kvl_ref, gq_ref, gkv_ref, qc, kc, bu, bd,
             dqo_ref, dkvo_ref, dkr_ref, dql_ref, dkvl_ref, dgq_ref, dgkv_ref):
        c_, u_, d_ = qc[...], bu[...], bd[...]
        tot = jnp.zeros((ts, LANES), F32)
        for h in range(MLA_HEADS):
            sl = slice(h * LANES, (h + 1) * LANES)
            dqo_ref[:, sl] = _rope_t(dq_ref[:, sl], c_, u_, d_, 112, 16).astype(dqo_ref.dtype)
            dk = dk_ref[:, sl]
            dkvo_ref[:, sl] = dk.astype(dkvo_ref.dtype)
            tot = tot + dk
        dkvo_ref[:, 1024:2048] = dv_ref[...].astype(dkvo_ref.dtype)
        dkr_ref[...] = _rope_t(tot, kc[...], u_, d_, 112, 16).astype(dkr_ref.dtype)

        @pl.when(pl.program_id(0) == 0)
        def _():
            dgq_ref[...] = jnp.zeros(dgq_ref.shape, F32)
            dgkv_ref[...] = jnp.zeros(dgkv_ref.shape, F32)

        for do_ref, w_ref, x_ref, g_ref, dx_ref, dg_ref in ((dqo_ref, wq_ref, ql_ref, gq_ref, dql_ref, dgq_ref),
                                                            (dkvo_ref, wkv_ref, kvl_ref, gkv_ref, dkvl_ref, dgkv_ref)):
            d = lax.dot_general(do_ref[...], w_ref[...], NT, preferred_element_type=F32)
            dx, gg = _rms_bwd(x_ref[...], g_ref[...], d)
            dx_ref[...] = dx.astype(dx_ref.dtype)
            dg_ref[...] += _sublane_sum(gg)

    tab = _rows(ts, LANES)
    return pl.pallas_call(
        body, name="mla_up_bwd", grid=(s_ // ts,),
        in_specs=[_rows(ts, 1024), _rows(ts, 1024), _rows(ts, 1024), _const(Q_LORA, 1024), _const(KV_LORA, 2048),
                  _rows(ts, Q_LORA, F_QLAT // Q_LORA), _rows(ts, KV_LORA, F_KVLAT // KV_LORA),
                  _const(1, Q_LORA), _const(1, KV_LORA), tab, tab, tab, tab],
        out_specs=[_rows(ts, 1024), _rows(ts, 2048), _rows(ts, LANES), _rows(ts, Q_LORA), _rows(ts, KV_LORA),
                   _const(SUBLANES, Q_LORA), _const(SUBLANES, KV_LORA)],
        out_shape=[jax.ShapeDtypeStruct((s_, 1024), MXU_DTYPE), jax.ShapeDtypeStruct((s_, 2048), MXU_DTYPE),
                   jax.ShapeDtypeStruct((s_, LANES), MXU_DTYPE), jax.ShapeDtypeStruct((s_, Q_LORA), MXU_DTYPE),
                   jax.ShapeDtypeStruct((s_, KV_LORA), MXU_DTYPE), jax.ShapeDtypeStruct((SUBLANES, Q_LORA), F32),
                   jax.ShapeDtypeStruct((SUBLANES, KV_LORA), F32)],
        compiler_params=_cparams(("arbitrary",)),
    )(dqc, dkc, dvp, wuq, wukv, p, p, gq, gkv, q_cos, k_cos, b_up, b_dn)


def _assemble_dp(dgab, dqa, dqlat, dkr, dka, dva, dkvlat, tabs, *, ts=512):
    s_ = dqa.shape[0]
    a_cos, a_up, a_dn = tabs[0], tabs[1], tabs[2]

    def body(dg_ref, dq_ref, dql_ref, dkr_ref, dk_ref, dv_ref, dkvl_ref, ac, au, ad, o_ref):
        c_, u_, d_ = ac[...], au[...], ad[...]
        o_ref[:, P_GA:P_Q] = dg_ref[...]
        for h in range(SWA_HEADS):
            sl = slice(h * LANES, (h + 1) * LANES)
            o_ref[:, P_Q + h * LANES:P_Q + (h + 1) * LANES] = _rope_t(dq_ref[:, sl], c_, u_, d_, 96, 32).astype(o_ref.dtype)
        o_ref[:, P_QLAT:P_KR] = dql_ref[...]
        o_ref[:, P_KR:P_K] = dkr_ref[...]
        for h in range(SWA_KV_HEADS):
            sl = slice(h * LANES, (h + 1) * LANES)
            o_ref[:, P_K + h * LANES:P_K + (h + 1) * LANES] = _rope_t(dk_ref[:, sl], c_, u_, d_, 96, 32).astype(o_ref.dtype)
        o_ref[:, P_V:P_KVLAT] = dv_ref[...]
        o_ref[:, P_KVLAT:P_W] = dkvl_ref[...]

    tab = _rows(ts, LANES)
    return pl.pallas_call(
        body, name="assemble_dp", grid=(s_ // ts,),
        in_specs=[_rows(ts, 2048), _rows(ts, 1024), _rows(ts, Q_LORA), _rows(ts, LANES), _rows(ts, 256), _rows(ts, 256),
                  _rows(ts, KV_LORA), tab, tab, tab],
        out_specs=_rows(ts, P_W), out_shape=jax.ShapeDtypeStruct((s_, P_W), MXU_DTYPE),
        compiler_params=_cparams(("parallel",)),
    )(dgab, dqa, dqlat, dkr, dka, dva, dkvlat, a_cos, a_up, a_dn)


def _attn_out_gate(oa, ob, woa_t, wob_t, p, *, ts=512):
    s_ = p.shape[0]

    def body(oa_ref, ob_ref, wa_ref, wb_ref, ga_ref, gb_ref, y_ref):
        ta = lax.dot_general(oa_ref[...], wa_ref[...], NT, preferred_element_type=F32)
        tb = lax.dot_general(ob_ref[...], wb_ref[...], NT, preferred_element_type=F32)
        y_ref[...] = (_sigmoid(ga_ref[...]) * ta + _sigmoid(gb_ref[...]) * tb).astype(y_ref.dtype)

    w = _const(1024, 1024)
    return pl.pallas_call(
        body, name="attn_out_gate", grid=(s_ // ts,),
        in_specs=[_rows(ts, 1024), _rows(ts, 1024), w, w, _rows(ts, 1024, F_GA // 1024), _rows(ts, 1024, F_GB // 1024)],
        out_specs=_rows(ts, 1024), out_shape=jax.ShapeDtypeStruct((s_, 1024), MXU_DTYPE),
        compiler_params=_cparams(("parallel",)),
    )(oa, ob, woa_t, wob_t, p, p)


def _d_y_gate(dx1b, wout, p, oa, ob, woa_t, wob_t, *, ts=512):
    s_ = p.shape[0]

    def body(dx_ref, w_ref, ga_ref, gb_ref, oa_ref, ob_ref, wa_ref, wb_ref, dta_ref, dtb_ref, dg_ref):
        d = lax.dot_general(dx_ref[...], w_ref[...], NT, preferred_element_type=F32)
        sa, sb = _sigmoid(ga_ref[...]), _sigmoid(gb_ref[...])
        dta_ref[...] = (d * sa).astype(dta_ref.dtype)
        dtb_ref[...] = (d * sb).astype(dtb_ref.dtype)
        ta = lax.dot_general(oa_ref[...], wa_ref[...], NT, preferred_element_type=F32)
        dg_ref[:, 0:1024] = (d * ta * (sa * (1.0 - sa))).astype(dg_ref.dtype)
        tb = lax.dot_general(ob_ref[...], wb_ref[...], NT, preferred_element_type=F32)
        dg_ref[:, 1024:2048] = (d * tb * (sb * (1.0 - sb))).astype(dg_ref.dtype)

    w = _const(1024, 1024)
    return pl.pallas_call(
        body, name="d_y_gate", grid=(s_ // ts,),
        in_specs=[_rows(ts, 1024), w, _rows(ts, 1024, F_GA // 1024), _rows(ts, 1024, F_GB // 1024),
                  _rows(ts, 1024), _rows(ts, 1024), w, w],
        out_specs=[_rows(ts, 1024), _rows(ts, 1024), _rows(ts, 2048)],
        out_shape=[jax.ShapeDtypeStruct((s_, 1024), MXU_DTYPE)] * 2 + [jax.ShapeDtypeStruct((s_, 2048), MXU_DTYPE)],
        compiler_params=_cparams(("parallel",)),
    )(dx1b, wout, p, p, oa, ob, woa_t, wob_t)


FF_TILE = D_FF // 2


def _ffn_in_act(x1, g, wgu_t, *, tm=512):
    s_ = x1.shape[0]
    n = s_ // tm

    def body(x_ref, g_ref, w_ref, h_ref, gu_ref, a_ref):
        h = _rms(x_ref[...], g_ref[...]).astype(h_ref.dtype)
        h_ref[...] = h
        p = lax.dot_general(h, w_ref[...], NT, preferred_element_type=F32)
        gu_ref[...] = p
        gate = p[:, :FF_TILE]
        a_ref[...] = (gate * _sigmoid(gate) * p[:, FF_TILE:]).astype(a_ref.dtype)

    return pl.pallas_call(
        body, name="ffn_in", grid=(2, s_ // tm),
        in_specs=[pl.BlockSpec((tm, D_MODEL), lambda j, i: (i, 0)), pl.BlockSpec((1, D_MODEL), lambda j, i: (0, 0)),
                  pl.BlockSpec((2 * FF_TILE, D_MODEL), lambda j, i: (j, 0))],
        out_specs=[pl.BlockSpec((tm, D_MODEL), lambda j, i: (i + j * (n - 1 - i), 0)),
                   pl.BlockSpec((tm, 2 * FF_TILE), lambda j, i: (i, j)),
                   pl.BlockSpec((tm, FF_TILE), lambda j, i: (i, j))],
        out_shape=[jax.ShapeDtypeStruct((s_, D_MODEL), MXU_DTYPE), jax.ShapeDtypeStruct((s_, 2 * D_FF), F32),
                   jax.ShapeDtypeStruct((s_, D_FF), MXU_DTYPE)],
        compiler_params=_cparams(("arbitrary", "arbitrary")),
    )(x1, g, wgu_t)


def _d_act_swiglu(dx2b, wd, gu, *, tm=512):
    s_ = dx2b.shape[0]

    def body(d_ref, w_ref, gu_ref, o_ref):
        da = lax.dot_general(d_ref[...], w_ref[...], NT, preferred_element_type=F32)
        g, u = gu_ref[:, :FF_TILE], gu_ref[:, FF_TILE:]
        sg = _sigmoid(g)
        o_ref[:, :FF_TILE] = (da * u * (sg * (1.0 + g * (1.0 - sg)))).astype(o_ref.dtype)
        o_ref[:, FF_TILE:] = (da * (g * sg)).astype(o_ref.dtype)

    gu_spec = pl.BlockSpec((tm, 2 * FF_TILE), lambda j, i: (i, j))
    return pl.pallas_call(
        body, name="d_act", grid=(2, s_ // tm),
        in_specs=[pl.BlockSpec((tm, D_MODEL), lambda j, i: (i, 0)), pl.BlockSpec((FF_TILE, D_MODEL), lambda j, i: (j, 0)), gu_spec],
        out_specs=gu_spec, out_shape=jax.ShapeDtypeStruct((s_, 2 * D_FF), MXU_DTYPE),
        compiler_params=_cparams(("parallel", "parallel")),
    )(dx2b, wd, gu)


def _ffn_out_loss(act, wd, x1, g, tgt, *, ts=512):
    s_, c = x1.shape
    kk = act.shape[1]

    def body(a_ref, w_ref, x_ref, g_ref, t_ref, dx_ref, dxb_ref, dg_ref, lp_ref, tot_ref):
        v = x_ref[...] + jnp.dot(a_ref[...], w_ref[...], preferred_element_type=F32)
        r = lax.rsqrt(jnp.mean(v * v, axis=-1, keepdims=True) + EPS)
        xh = v * r
        gg = g_ref[...]
        e = xh * gg - t_ref[...]
        do = e * (1.0 / c)
        dxh = do * gg
        dx = r * (dxh - xh * jnp.mean(dxh * xh, axis=-1, keepdims=True))
        dx_ref[...] = dx
        dxb_ref[...] = dx.astype(dxb_ref.dtype)
        i = pl.program_id(0)

        @pl.when(i == 0)
        def _():
            dg_ref[...] = jnp.zeros(dg_ref.shape, F32)
            lp_ref[...] = jnp.zeros(lp_ref.shape, F32)

        dg_ref[...] += _sublane_sum(do * xh)
        lp_ref[...] += _sublane_sum(e * e)
        tot_ref[...] = jnp.full(tot_ref.shape, (0.5 / c) * jnp.sum(lp_ref[...]), F32)

    return pl.pallas_call(
        body, name="ffn_out_loss", grid=(s_ // ts,),
        in_specs=[_rows(ts, kk), _const(kk, c), _rows(ts, c), _const(1, c), _rows(ts, c)],
        out_specs=[_rows(ts, c), _rows(ts, c), _const(SUBLANES, c), _const(SUBLANES, c), _const(SUBLANES, LANES)],
        out_shape=[jax.ShapeDtypeStruct((s_, c), F32), jax.ShapeDtypeStruct((s_, c), MXU_DTYPE),
                   jax.ShapeDtypeStruct((SUBLANES, c), F32), jax.ShapeDtypeStruct((SUBLANES, c), F32),
                   jax.ShapeDtypeStruct((SUBLANES, LANES), F32)],
        compiler_params=_cparams(("arbitrary",)),
    )(act, wd, x1, g, tgt)


def _mla_d_out(dtb, wob_t, o32, *, ts=512):
    s_ = dtb.shape[0]

    def body(dt_ref, w_ref, o_ref, dob_ref, dl_ref):
        d = jnp.dot(dt_ref[...], w_ref[...], preferred_element_type=F32)
        dob_ref[...] = d.astype(dob_ref.dtype)
        prod = d * o_ref[...]
        for h in range(MLA_HEADS):
            dl_ref[h] = jnp.sum(prod[:, h * LANES:(h + 1) * LANES].T, axis=0, keepdims=True)

    return pl.pallas_call(
        body, name="mla_d_out", grid=(s_ // ts,), in_specs=[_rows(ts, 1024), _const(1024, 1024), _rows(ts, 1024)],
        out_specs=[_rows(ts, 1024), pl.BlockSpec((MLA_HEADS, 1, ts), lambda i: (0, 0, i))],
        out_shape=[jax.ShapeDtypeStruct((s_, 1024), MXU_DTYPE), jax.ShapeDtypeStruct((MLA_HEADS, 1, s_), F32)],
        compiler_params=_cparams(("parallel",)),
    )(dtb, wob_t, o32)


SWA_T = 4 * BLOCK


SWA_W = SWA_GROUP * BLOCK


def _swa_masks(sb):
    kr = lax.broadcasted_iota(jnp.int32, (2 * BLOCK, SWA_W), 0)
    qc = jnp.bitwise_and(lax.broadcasted_iota(jnp.int32, (2 * BLOCK, SWA_W), 1), BLOCK - 1)
    band = jnp.logical_and(kr > qc, kr <= qc + BLOCK)
    first = jnp.logical_and(band, kr >= BLOCK)
    return band, jnp.logical_or(first, jnp.logical_and(band, sb > 0))


def _heads_to_rows(ref, rs):
    return jnp.concatenate([ref[rs, h * LANES:(h + 1) * LANES] for h in range(SWA_GROUP)], axis=0)


def _sink_row(sk_ref):
    return jnp.concatenate([sk_ref[0, h:h + 1, :] for h in range(SWA_GROUP)], axis=1) * LOG2E


def _swa_in_specs(rev, nsb):
    sbi = (lambda j: nsb - 1 - j) if rev else (lambda j: j)
    cur = pl.BlockSpec((SWA_T, LANES), lambda g, j: (sbi(j), g))
    prev = pl.BlockSpec((BLOCK, LANES), lambda g, j: (jnp.maximum(4 * sbi(j) - 1, 0), g))
    q = pl.BlockSpec((SWA_T, SWA_GROUP * LANES), lambda g, j: (sbi(j), g))
    sink = pl.BlockSpec((1, SUBLANES, LANES), lambda g, j: (g, 0, 0))
    lse = pl.BlockSpec((SWA_GROUP, 1, SWA_T), lambda g, j: (g, 0, sbi(j)))
    return q, cur, prev, sink, lse


def _swa_fwd(qa, ka, va, sink_b):
    s_ = qa.shape[0]
    nsb = s_ // SWA_T
    c2 = HEAD_DIM ** -0.5 * LOG2E

    def body(q_ref, kc_ref, kp_ref, vc_ref, vp_ref, sk_ref, o32_ref, o16_ref, lse_ref, kx, vx):
        kx[0:BLOCK, :] = kp_ref[...]
        kx[BLOCK:5 * BLOCK, :] = kc_ref[...]
        vx[0:BLOCK, :] = vp_ref[...]
        vx[BLOCK:5 * BLOCK, :] = vc_ref[...]
        band, band0 = _swa_masks(pl.program_id(1))
        sink2 = _sink_row(sk_ref)
        for b in range(4):
            rs = slice(b * BLOCK, (b + 1) * BLOCK)
            ks = slice(b * BLOCK, (b + 2) * BLOCK)
            st = lax.dot_general(kx[ks, :], _heads_to_rows(q_ref, rs), NT, preferred_element_type=F32) * c2
            st = jnp.where(band0 if b == 0 else band, st, -jnp.inf)
            m = jnp.maximum(jnp.max(st, axis=0, keepdims=True), sink2)
            pt = jnp.exp2(st - m)
            den = jnp.sum(pt, axis=0, keepdims=True) + jnp.exp2(sink2 - m)
            o = lax.dot_general((pt * (1.0 / den)).astype(MXU_DTYPE), vx[ks, :], TN, preferred_element_type=F32)
            lse = m + jnp.log2(den)
            for hh in range(SWA_GROUP):
                cs = slice(hh * LANES, (hh + 1) * LANES)
                o32_ref[rs, cs] = o[cs, :]
                o16_ref[rs, cs] = o[cs, :].astype(o16_ref.dtype)
                lse_ref[hh, :, rs] = lse[:, cs]

    q, cur, prev, sink, lse_spec = _swa_in_specs(False, nsb)
    return pl.pallas_call(
        body, name="swa_fwd", grid=(SWA_KV_HEADS, nsb), in_specs=[q, cur, prev, cur, prev, sink],
        out_specs=[q, q, lse_spec],
        out_shape=[jax.ShapeDtypeStruct((s_, SWA_HEADS * LANES), F32), jax.ShapeDtypeStruct((s_, SWA_HEADS * LANES), MXU_DTYPE),
                   jax.ShapeDtypeStruct((SWA_HEADS, 1, s_), F32)],
        scratch_shapes=[pltpu.VMEM((5 * BLOCK, LANES), MXU_DTYPE), pltpu.VMEM((5 * BLOCK, LANES), MXU_DTYPE)],
        compiler_params=_cparams(("parallel", "arbitrary")),
    )(qa, ka, ka, va, va, sink_b)


def _swa_bwd(qa, ka, va, sink_b, o32, do, lse):
    s_ = qa.shape[0]
    nsb = s_ // SWA_T
    scale = HEAD_DIM ** -0.5
    c2 = scale * LOG2E

    def body(q_ref, kc_ref, kp_ref, vc_ref, vp_ref, sk_ref, o_ref, do_ref, lse_ref,
             dq_ref, dk_ref, dv_ref, dsk_ref, kx, vx, kacc, vacc, kcar, vcar):
        j = pl.program_id(1)
        kx[0:BLOCK, :] = kp_ref[...]
        kx[BLOCK:5 * BLOCK, :] = kc_ref[...]
        vx[0:BLOCK, :] = vp_ref[...]
        vx[BLOCK:5 * BLOCK, :] = vc_ref[...]
        band, band0 = _swa_masks(nsb - 1 - j)
        kacc[...] = jnp.zeros(kacc.shape, F32)
        vacc[...] = jnp.zeros(vacc.shape, F32)

        @pl.when(j == 0)
        def _():
            kcar[...] = jnp.zeros(kcar.shape, F32)
            vcar[...] = jnp.zeros(vcar.shape, F32)
            dsk_ref[...] = jnp.zeros(dsk_ref.shape, F32)

        sink2 = _sink_row(sk_ref)
        dsink = jnp.zeros((1, SWA_W), F32)
        for b in range(4):
            rs = slice(b * BLOCK, (b + 1) * BLOCK)
            ks = slice(b * BLOCK, (b + 2) * BLOCK)
            q, k2, v2 = _heads_to_rows(q_ref, rs), kx[ks, :], vx[ks, :]
            d = _heads_to_rows(do_ref, rs)
            delta = jnp.sum((d * _heads_to_rows(o_ref, rs)).T, axis=0, keepdims=True)
            l2 = jnp.concatenate([lse_ref[hh, :, rs] for hh in range(SWA_GROUP)], axis=1)
            st = lax.dot_general(k2, q, NT, preferred_element_type=F32) * c2
            pt = jnp.exp2(jnp.where(band0 if b == 0 else band, st, -jnp.inf) - l2)
            db = d.astype(MXU_DTYPE)
            dst = (pt * (lax.dot_general(v2, db, NT, preferred_element_type=F32) - delta) * scale).astype(MXU_DTYPE)
            dq = lax.dot_general(dst, k2, TN, preferred_element_type=F32)
            for hh in range(SWA_GROUP):
                dq_ref[rs, hh * LANES:(hh + 1) * LANES] = dq[hh * LANES:(hh + 1) * LANES, :]
            kacc[ks, :] += jnp.dot(dst, q, preferred_element_type=F32)
            vacc[ks, :] += jnp.dot(pt.astype(MXU_DTYPE), db, preferred_element_type=F32)
            dsink = dsink - jnp.exp2(sink2 - l2) * delta
        for hh in range(SWA_GROUP):
            tot = jnp.sum(dsink[:, hh * LANES:(hh + 1) * LANES], axis=1, keepdims=True)
            dsk_ref[0, hh:hh + 1, :] += jnp.broadcast_to(tot, (1, LANES))

        dk_ref[0:3 * BLOCK, :] = kacc[BLOCK:4 * BLOCK, :]
        dk_ref[3 * BLOCK:4 * BLOCK, :] = kacc[4 * BLOCK:5 * BLOCK, :] + kcar[...]
        dv_ref[0:3 * BLOCK, :] = vacc[BLOCK:4 * BLOCK, :].astype(dv_ref.dtype)
        dv_ref[3 * BLOCK:4 * BLOCK, :] = (vacc[4 * BLOCK:5 * BLOCK, :] + vcar[...]).astype(dv_ref.dtype)
        kcar[...] = kacc[0:BLOCK, :]
        vcar[...] = vacc[0:BLOCK, :]

    q, cur, prev, sink, lse_spec = _swa_in_specs(True, nsb)
    return pl.pallas_call(
        body, name="swa_bwd", grid=(SWA_KV_HEADS, nsb),
        in_specs=[q, cur, prev, cur, prev, sink, q, q, lse_spec],
        out_specs=[q, cur, cur, sink],
        out_shape=[jax.ShapeDtypeStruct((s_, SWA_HEADS * LANES), F32), jax.ShapeDtypeStruct((s_, SWA_KV_HEADS * LANES), F32),
                   jax.ShapeDtypeStruct((s_, SWA_KV_HEADS * LANES), MXU_DTYPE),
                   jax.ShapeDtypeStruct((SWA_KV_HEADS, SUBLANES, LANES), F32)],
        scratch_shapes=[pltpu.VMEM((5 * BLOCK, LANES), MXU_DTYPE), pltpu.VMEM((5 * BLOCK, LANES), MXU_DTYPE),
                        pltpu.VMEM((5 * BLOCK, LANES), F32), pltpu.VMEM((5 * BLOCK, LANES), F32),
                        pltpu.VMEM((BLOCK, LANES), F32), pltpu.VMEM((BLOCK, LANES), F32)],
        compiler_params=_cparams(("arbitrary", "arbitrary")),
    )(qa, ka, ka, va, va, sink_b, o32, do, lse)


MLA_T = 512
MLA_FWD_GROUP = 4
MLA_BWD_GROUP = 2


def _mla_specs(s_, t, group):
    w = group * LANES
    qs = pl.BlockSpec((t, w), lambda g, i: (i, g))
    kv = pl.BlockSpec((s_, w), lambda g, i: (0, g))
    row = pl.BlockSpec((group, 1, t), lambda g, i: (g, 0, i))
    return qs, kv, row


def _causal_scores_t(k, q, t, c2, masked):
    st = lax.dot_general(k, q, NT, preferred_element_type=F32) * c2
    if masked:
        kr = lax.broadcasted_iota(jnp.int32, (t, t), 0)
        qc = lax.broadcasted_iota(jnp.int32, (t, t), 1)
        st = jnp.where(kr <= qc, st, -jnp.inf)
    return st


def _mla_fwd(qc, kc, vp):
    s_ = qc.shape[0]
    t = min(MLA_T, s_)
    c2 = MLA_QK ** -0.5 * LOG2E
    grp = MLA_FWD_GROUP

    def body(q_ref, k_ref, v_ref, o32_ref, o16_ref, lse_ref, m_s, acc_s):
        qi = pl.program_id(1)
        m_s[...] = jnp.full(m_s.shape, -jnp.inf, F32)
        acc_s[...] = jnp.zeros(acc_s.shape, F32)
        ones_lane = lax.broadcasted_iota(jnp.int32, (t, LANES), 1) == MLA_V

        def step(ki, masked):
            off = pl.multiple_of(ki * t, t)
            for g in range(grp):
                cs = slice(g * LANES, (g + 1) * LANES)
                st = _causal_scores_t(k_ref[pl.ds(off, t), cs], q_ref[:, cs], t, c2, masked)
                m_old = m_s[g]
                m_new = jnp.maximum(m_old, jnp.max(st, axis=0, keepdims=True))
                alpha = jnp.exp2(m_old - m_new)
                pt = jnp.exp2(st - m_new).astype(MXU_DTYPE)
                v = v_ref[pl.ds(off, t), cs]
                v = jnp.where(ones_lane, jnp.ones((), v.dtype), v)
                acc_s[g] = alpha * acc_s[g] + lax.dot_general(v, pt, TN, preferred_element_type=F32)
                m_s[g] = m_new

        def full_block(ki, carry):
            step(ki, False)
            return carry

        lax.fori_loop(0, qi, full_block, 0)
        step(qi, True)
        for g in range(grp):
            cs = slice(g * LANES, (g + 1) * LANES)
            acc = acc_s[g]
            l = acc[MLA_V:MLA_V + 1, :]
            o = (acc * (1.0 / l)).T
            o32_ref[:, cs] = o
            o16_ref[:, cs] = o.astype(o16_ref.dtype)
            lse_ref[g] = m_s[g] + jnp.log2(l)

    qs, kv, row = _mla_specs(s_, t, grp)
    return pl.pallas_call(
        body, name="mla_fwd", grid=(MLA_HEADS // grp, s_ // t), in_specs=[qs, kv, kv], out_specs=[qs, qs, row],
        out_shape=[jax.ShapeDtypeStruct((s_, MLA_HEADS * LANES), F32), jax.ShapeDtypeStruct((s_, MLA_HEADS * LANES), MXU_DTYPE),
                   jax.ShapeDtypeStruct((MLA_HEADS, 1, s_), F32)],
        scratch_shapes=[pltpu.VMEM((grp, 1, t), F32), pltpu.VMEM((grp, LANES, t), F32)],
        compiler_params=_cparams(("parallel", "arbitrary")),
    )(qc, kc, vp)


def _mla_bwd(qc, kc, vp, dob, lse, delta):
    s_ = qc.shape[0]
    t = min(MLA_T, s_)
    scale = MLA_QK ** -0.5
    c2 = scale * LOG2E
    grp = MLA_BWD_GROUP

    def body(q_ref, do_ref, lse_ref, dl_ref, k_ref, v_ref, dq_ref, dk_ref, dv_ref, dqt_s):
        qi = pl.program_id(1)

        @pl.when(qi == 0)
        def _():
            dk_ref[...] = jnp.zeros(dk_ref.shape, F32)
            dv_ref[...] = jnp.zeros(dv_ref.shape, F32)

        dqt_s[...] = jnp.zeros(dqt_s.shape, F32)

        def step(ki, masked):
            off = pl.multiple_of(ki * t, t)
            for g in range(grp):
                cs = slice(g * LANES, (g + 1) * LANES)
                q, d, k = q_ref[:, cs], do_ref[:, cs], k_ref[pl.ds(off, t), cs]
                pt = jnp.exp2(_causal_scores_t(k, q, t, c2, masked) - lse_ref[g])
                dpt = lax.dot_general(v_ref[pl.ds(off, t), cs], d, NT, preferred_element_type=F32)
                dst = (pt * (dpt - dl_ref[g]) * scale).astype(MXU_DTYPE)
                dv_ref[pl.ds(off, t), cs] += jnp.dot(pt.astype(MXU_DTYPE), d, preferred_element_type=F32)
                dk_ref[pl.ds(off, t), cs] += jnp.dot(dst, q, preferred_element_type=F32)
                dqt_s[g] += lax.dot_general(k, dst, TN, preferred_element_type=F32)

        def full_block(ki, carry):
            step(ki, False)
            return carry

        lax.fori_loop(0, qi, full_block, 0)
        step(qi, True)
        for g in range(grp):
            dq_ref[:, g * LANES:(g + 1) * LANES] = dqt_s[g].T

    qs, kv, row = _mla_specs(s_, t, grp)
    shp = jax.ShapeDtypeStruct((s_, MLA_HEADS * LANES), F32)
    return pl.pallas_call(
        body, name="mla_bwd", grid=(MLA_HEADS // grp, s_ // t), in_specs=[qs, qs, row, row, kv, kv],
        out_specs=[qs, kv, kv], out_shape=[shp, shp, shp], scratch_shapes=[pltpu.VMEM((grp, LANES, t), F32)],
        compiler_params=_cparams(("parallel", "arbitrary")),
    )(qc, dob, lse, delta, kc, vp)


def _pad_heads(w, nh, hd, axis):
    shp = w.shape
    w = w.reshape(shp[:axis] + (nh, hd) + shp[axis + 1:])
    pad = [(0, 0)] * w.ndim
    pad[axis + 1] = (0, LANES - hd)
    w = jnp.pad(w, pad)
    return w.reshape(shp[:axis] + (nh * LANES,) + shp[axis + 1:])


def _unpad_heads(w, nh, hd, axis):
    shp = w.shape
    w = w.reshape(shp[:axis] + (nh, LANES) + shp[axis + 1:])
    w = lax.slice_in_dim(w, 0, hd, axis=axis + 1)
    return w.reshape(shp[:axis] + (nh * hd,) + shp[axis + 1:])


PACK_W = 1024
ROW_TILE = 16
FULL_SHAPE = dict(w_in=(1024, 3488), w_uq=(384, 768), w_ukv=(256, 1024), w_o_swa=(512, 1024), w_o_mla=(512, 1024),
                  w_out=(1024, 1024), w_gate=(1024, 2816), w_up=(1024, 2816), w_down=(2816, 1024))
BIG = tuple(FULL_SHAPE)
ROW_SHARDED = ("w_out", "w_down")
W_IN_COLS = FULL_SHAPE["w_in"][1] // N_DEV
W_IN_ROWS = -(-W_IN_COLS // ROW_TILE) * ROW_TILE
FF_COLS = D_FF // N_DEV
OUT_ROWS = D_MODEL // N_DEV
SMALL_FLAT = (("w_uq", 0, 36), ("w_ukv", 48, 32))
SMALL_USED = 80
MID_BLOCKS = 4
MID_ROWS = MID_BLOCKS * OUT_ROWS
EARLY_ROWS = W_IN_ROWS + MID_ROWS
LATE_ROWS = 3 * FF_COLS
PACK_ROWS = EARLY_ROWS + LATE_ROWS


def _shard_shape(n):
    r, c = FULL_SHAPE[n]
    return (r // N_DEV, c) if n in ROW_SHARDED else (r, c // N_DEV)


def _wire_pack(sh, dtype):
    c = lambda n: sh[n].astype(dtype)
    rows = [jnp.pad(c("w_in").T, ((0, W_IN_ROWS - W_IN_COLS), (0, 0))), c("w_out"),
            _pad_heads(c("w_o_swa").T, SWA_HEADS, HEAD_DIM, 1), _pad_heads(c("w_o_mla").T, MLA_HEADS, MLA_V, 1)]
    for n, _, r in SMALL_FLAT:
        rows.append(jnp.pad(c(n).reshape(r, PACK_W), ((0, -r % ROW_TILE), (0, 0))))
    rows.append(jnp.zeros((OUT_ROWS - SMALL_USED, PACK_W), dtype))
    return jnp.concatenate(rows + [c("w_gate").T, c("w_up").T, c("w_down")], 0)


def _mid_unpack(p):
    out = dict(w_out=p[0:OUT_ROWS], w_o_swa=_unpad_heads(p[OUT_ROWS:2 * OUT_ROWS], SWA_HEADS, HEAD_DIM, 1).T,
               w_o_mla=_unpad_heads(p[2 * OUT_ROWS:3 * OUT_ROWS], MLA_HEADS, MLA_V, 1).T)
    for n, off, r in SMALL_FLAT:
        out[n] = p[3 * OUT_ROWS + off:3 * OUT_ROWS + off + r].reshape(_shard_shape(n))
    return out


def _w_in_row_maps():
    sp = lambda col: (col // W_IN_COLS) * W_IN_ROWS + col % W_IN_COLS
    fwd = np.full((P_W,), -1, np.int64)

    def put(t0, c0, n):
        fwd[t0:t0 + n] = [sp(c) for c in range(c0, c0 + n)]

    put(P_GA, IN_OFF[6], D_MODEL)
    put(P_GB, IN_OFF[7], D_MODEL)
    for h in range(SWA_HEADS):
        put(P_Q + LANES * h, IN_OFF[0] + HEAD_DIM * h, HEAD_DIM)
    put(P_QLAT, IN_OFF[3], Q_LORA)
    put(P_KR + KR_LANE, IN_OFF[5], MLA_ROPE)
    for h in range(SWA_KV_HEADS):
        put(P_K + LANES * h, IN_OFF[1] + HEAD_DIM * h, HEAD_DIM)
        put(P_V + LANES * h, IN_OFF[2] + HEAD_DIM * h, HEAD_DIM)
    put(P_KVLAT, IN_OFF[4], KV_LORA)
    inv = np.full((N_DEV * W_IN_ROWS,), -1, np.int64)
    inv[fwd[fwd >= 0]] = np.nonzero(fwd >= 0)[0]
    return fwd, inv


def _take_rows(src, idx, *, name, tile=2 * LANES):
    n_out, n_src, width = len(idx), src.shape[0], src.shape[1]
    assert n_out % tile == 0 and n_src % tile == 0
    n_tiles = n_out // tile
    blocks = [sorted({int(v) // tile for v in idx[i * tile:(i + 1) * tile] if v >= 0}) for i in range(n_tiles)]
    k_max = max(1, max(len(b) for b in blocks))
    tab = np.zeros((n_tiles, k_max), np.int32)
    sel = np.zeros((n_tiles, k_max, tile, tile), np.float32)
    for i, blks in enumerate(blocks):
        for m, b in enumerate(blks):
            tab[i, m] = b
            for r in range(tile):
                v = int(idx[i * tile + r])
                if v >= 0 and v // tile == b:
                    sel[i, m, r, v % tile] = 1.0

    def body(tab_ref, sel_ref, *refs):
        o_ref = refs[k_max]
        acc = jnp.dot(sel_ref[0, 0], refs[0][...], preferred_element_type=F32)
        for m in range(1, k_max):
            acc = acc + jnp.dot(sel_ref[0, m], refs[m][...], preferred_element_type=F32)
        o_ref[...] = acc.astype(o_ref.dtype)

    def src_spec(m):
        return pl.BlockSpec((tile, width), lambda i, t: (t[i * k_max + m], 0))

    return pl.pallas_call(
        body, name=name,
        grid_spec=pltpu.PrefetchScalarGridSpec(
            num_scalar_prefetch=1, grid=(n_tiles,),
            in_specs=[pl.BlockSpec((1, k_max, tile, tile), lambda i, t: (i, 0, 0, 0))] + [src_spec(m) for m in range(k_max)],
            out_specs=pl.BlockSpec((tile, width), lambda i, t: (i, 0))),
        out_shape=jax.ShapeDtypeStruct((n_out, width), src.dtype),
        compiler_params=_cparams(("parallel",)),
    )(jnp.asarray(tab.reshape(-1)), jnp.asarray(sel, src.dtype), *([src] * k_max))


def _w_in_operand(win_g):
    return _take_rows(win_g.reshape(N_DEV * W_IN_ROWS, PACK_W), _w_in_row_maps()[0], name="w_in_rows")


def _mid_operands(wout_g, woa_g, wob_g, small_g):
    def full(n, off, r):
        a = small_g[:, off:off + r].reshape((N_DEV,) + _shard_shape(n))
        return jnp.moveaxis(a, 0, 1).reshape(FULL_SHAPE[n])

    w = {n: full(n, off, r) for n, off, r in SMALL_FLAT}
    ukv = w["w_ukv"].reshape(KV_LORA, MLA_HEADS, MLA_NOPE + MLA_V)
    return dict(
        wout=wout_g.reshape(D_MODEL, D_MODEL), woa_t=woa_g.reshape(D_MODEL, -1), wob_t=wob_g.reshape(D_MODEL, -1),
        wuq=_pad_heads(w["w_uq"], MLA_HEADS, MLA_QK, 1),
        wuk=_pad_heads(ukv[:, :, :MLA_NOPE].reshape(KV_LORA, -1), MLA_HEADS, MLA_NOPE, 1),
        wuv=_pad_heads(ukv[:, :, MLA_NOPE:].reshape(KV_LORA, -1), MLA_HEADS, MLA_V, 1),
    )


def _mid_grad_pack(g):
    uk = _unpad_heads(g["wukv"][:, :1024], MLA_HEADS, MLA_NOPE, 1).reshape(KV_LORA, MLA_HEADS, MLA_NOPE)
    uv = _unpad_heads(g["wukv"][:, 1024:], MLA_HEADS, MLA_V, 1).reshape(KV_LORA, MLA_HEADS, MLA_V)
    w = dict(w_uq=_unpad_heads(g["wuq"], MLA_HEADS, MLA_QK, 1), w_ukv=jnp.concatenate([uk, uv], 2).reshape(KV_LORA, -1))
    rows = []
    for n, _, r in SMALL_FLAT:
        rr, cc = FULL_SHAPE[n]
        a = jnp.moveaxis(w[n].reshape(rr, N_DEV, cc // N_DEV), 1, 0).reshape(N_DEV, r, PACK_W)
        rows.append(jnp.pad(a, ((0, 0), (0, -r % ROW_TILE), (0, 0))).astype(WIRE_DTYPE))
    rows.append(jnp.zeros((N_DEV, OUT_ROWS - SMALL_USED, PACK_W), WIRE_DTYPE))
    blk = lambda a: a.reshape(N_DEV, OUT_ROWS, PACK_W)
    return [blk(g["wout"]), blk(g["woa_t"]), blk(g["wob_t"]), jnp.concatenate(rows, 1)]


def _w_in_grad_chunks(g_win_t):
    return _take_rows(g_win_t, _w_in_row_maps()[1], name="dw_in_rows").reshape(N_DEV, W_IN_ROWS, PACK_W)


def _local_step(x, tgt, win_t, small, weights, grads):
    s_ = x.shape[0]
    tabs = _rope_tables(s_)
    sink_b = jnp.broadcast_to(small["swa_sinks"].reshape(SWA_KV_HEADS, SWA_GROUP, 1), (SWA_KV_HEADS, SWA_GROUP, LANES))
    sink_b = jnp.pad(sink_b, ((0, 0), (0, SUBLANES - SWA_GROUP), (0, 0)))

    h, qa, ka, va, cq, ckv, kro, p = _proj_in(x, small["mix_norm_g"], win_t, small["q_norm_g"], small["kv_norm_g"], tabs)
    oa32, oa16, lse_a = _swa_fwd(qa, ka, va, sink_b)
    ops = weights.mid(oa16)
    qc, kc, vp = _mla_up(cq, ckv, kro, ops["wuq"], ops["wuk"], ops["wuv"], tabs)
    ob32, ob16, lse_b = _mla_fwd(qc, kc, vp)
    y = _attn_out_gate(oa16, ob16, ops["woa_t"], ops["wob_t"], p)
    x1 = _mm(y, ops["wout"], "nn", name="out_proj", add=x, tm=1024, tn=1024)
    wgu_t, wd = weights.late(x1)
    h2, gu, act = _ffn_in_act(x1, small["ffn_norm_g"], wgu_t)

    dx2, dx2b, dg3, _, tot = _ffn_out_loss(act, wd, x1, small["final_norm_g"].reshape(1, D_MODEL), tgt)
    g = {}
    g_wd = _mm(act, dx2b, "tn", name="dw_down", tm=FF_TILE, tn=1024, tk=2048, out_dtype=WIRE_DTYPE)
    dgu = _d_act_swiglu(dx2b, wd, gu)
    g_wgu = _mm(dgu, h2, "tn", name="dw_ffn_in", tm=FF_TILE, tn=1024, tk=2048, out_dtype=WIRE_DTYPE)
    token = grads.late(g_wgu, g_wd)
    dx1, dx1b, dg2 = _mm_norm_bwd(dgu, wgu_t, x1, small["ffn_norm_g"] + token[0:1, 0:1], dx2, name="d_h2")
    g["wout"] = _mm(y, dx1b, "tn", name="dw_out", tm=1024, tn=1024, tk=2048, out_dtype=WIRE_DTYPE)
    dta, dtb, dgab = _d_y_gate(dx1b, ops["wout"], p, oa16, ob16, ops["woa_t"], ops["wob_t"])
    doa = _mm(dta, ops["woa_t"], "nn", name="d_oa", tm=1024, tn=1024)
    g["woa_t"] = _mm(dta, oa16, "tn", name="dw_o_swa", tm=1024, tn=1024, tk=2048, out_dtype=WIRE_DTYPE)
    g["wob_t"] = _mm(dtb, ob16, "tn", name="dw_o_mla", tm=1024, tn=1024, tk=2048, out_dtype=WIRE_DTYPE)
    dob16, delta_b = _mla_d_out(dtb, ops["wob_t"], ob32)
    dqc, dkc, dvp = _mla_bwd(qc, kc, vp, dob16, lse_b, delta_b)
    dqp, dkv, dkr, dqlat, dkvlat, dgq, dgkv = _mla_up_bwd(
        dqc, dkc, dvp, ops["wuq"], jnp.concatenate([ops["wuk"], ops["wuv"]], 1), p, small["q_norm_g"], small["kv_norm_g"], tabs)
    g["wuq"] = _mm(cq, dqp, "tn", name="dw_uq", tm=Q_LORA, tn=1024, tk=2048)
    g["wukv"] = _mm(ckv, dkv, "tn", name="dw_ukv", tm=KV_LORA, tn=2048, tk=2048)
    token = grads.mid(g)
    dqa, dka, dva, dsk = _swa_bwd(qa, ka, va, sink_b + token[0:1, 0:1], oa32, doa, lse_a)
    dp = _assemble_dp(dgab, dqa, dqlat, dkr, dka, dva, dkvlat, tabs)
    token = grads.last(_mm(dp, h, "tn", name="dw_in", tm=2176, tn=1024, tk=1024, out_dtype=WIRE_DTYPE))
    gx, _, dg1 = _mm_norm_bwd(dp, win_t, x, small["mix_norm_g"], dx1, name="d_h", after=token)

    sm = dict(mix_norm_g=dg1, ffn_norm_g=dg2, final_norm_g=dg3, q_norm_g=dgq, kv_norm_g=dgkv,
              swa_sinks=dsk[:, :SWA_GROUP, 0].reshape(1, SWA_HEADS))
    return tot, gx, sm


MESH = pl.DeviceIdType.MESH
ANY = pl.BlockSpec(memory_space=pl.ANY)


def _position():
    return lax.axis_index("x"), lax.axis_index("y"), lax.axis_index("c")


def _all_gather(block, pieces, shapes, *, name):
    n_out = len(shapes)
    n_rows = sum(p[3] for p in pieces)

    def body(x_ref, *refs):
        outs, (send_sems, recv_sems, local_sem) = refs[:n_out], refs[n_out:]
        x, y, c = _position()
        me, sibling = (x, y, c), (x, y, 1 - c)
        chips = [(1 - x, y), (x, 1 - y), (1 - x, 1 - y)]

        def dst(piece, blk):
            arr, lead, _, _ = piece
            return outs[arr].at[lead(4 * blk[0] + 2 * blk[1] + blk[2])]

        def own(piece):
            return x_ref.at[pl.ds(piece[2], piece[3])]

        def copies(k, blk, to, from_input):
            return [pltpu.make_async_remote_copy(
                src_ref=own(p) if from_input else dst(p, blk), dst_ref=dst(p, blk), send_sem=send_sems.at[k],
                recv_sem=recv_sems.at[k], device_id=to, device_id_type=MESH) for p in pieces]

        gathered_rows = x_ref.at[pl.ds(0, n_rows)]

        def whole_block(k):
            return pltpu.make_async_remote_copy(src_ref=gathered_rows, dst_ref=gathered_rows, send_sem=send_sems.at[k],
                                                recv_sem=recv_sems.at[k], device_id=me, device_id_type=MESH)

        for p in pieces:
            pltpu.make_async_copy(own(p), dst(p, me), local_sem).start()
        for cp in copies(0, me, sibling, True):
            cp.start()
        for j, chip in enumerate(chips):
            for cp in copies(1 + j, me, (*chip, c), True):
                cp.start()
        for j, chip in enumerate(chips):
            whole_block(1 + j).wait_recv()
            for cp in copies(4 + j, (*chip, c), sibling, False):
                cp.start()
        whole_block(0).wait_recv()
        for j in range(3):
            whole_block(4 + j).wait_recv()
        for k in range(7):
            whole_block(k).wait_send()
        pltpu.make_async_copy(gathered_rows, gathered_rows, local_sem).wait()

    return pl.pallas_call(
        body, name=name, out_shape=[jax.ShapeDtypeStruct(s, block.dtype) for s in shapes], in_specs=[ANY],
        out_specs=[ANY] * n_out,
        scratch_shapes=[pltpu.SemaphoreType.DMA((7,)), pltpu.SemaphoreType.DMA((7,)), pltpu.SemaphoreType.DMA],
    )(block)


HBM = pl.BlockSpec(memory_space=pltpu.HBM)
SEM = pl.BlockSpec(memory_space=pltpu.SEMAPHORE)
TILE_DEVS = FF_TILE // FF_COLS
GU_SHAPE = (2, 2, TILE_DEVS, FF_COLS, PACK_W)


def _gate_slab(d):
    return (d // TILE_DEVS, 0, d % TILE_DEVS)


def _up_slab(d):
    return (d // TILE_DEVS, 1, d % TILE_DEVS)
D_SHAPE = (N_DEV, FF_COLS, PACK_W)
LAND_SHAPE = (N_DEV, LATE_ROWS, PACK_W)


def _split_params():
    return pltpu.CompilerParams(has_side_effects=pltpu.SideEffectType.DATAFLOW_SIDE_EFFECTING)


def _peer(x, y, c, k):
    return ((1 - x) if k & 4 else x, (1 - y) if k & 2 else y, (1 - c) if k & 1 else c)


def _empty_hbm(shape, dtype):
    return pltpu.with_memory_space_constraint(lax.empty(shape, dtype), pltpu.HBM)


def _wait_all(rows, send_sems, recv_sems, me):
    for k in range(N_DEV - 1):
        cp = pltpu.make_async_remote_copy(src_ref=rows, dst_ref=rows, send_sem=send_sems.at[k], recv_sem=recv_sems.at[k],
                                          device_id=me, device_id_type=MESH)
        cp.wait_send()
        cp.wait_recv()


def _token_shape():
    return jax.ShapeDtypeStruct((SUBLANES, LANES), F32)


def _gather_start(pack, row0, pieces, shapes, *, name):
    n = len(shapes)

    def body(*refs):
        p_ref, bufs, send_sems, recv_sems, token = refs[0], refs[1:1 + n], refs[1 + n], refs[2 + n], refs[-1]
        x, y, c = _position()
        me = 4 * x + 2 * y + c
        for k in range(1, N_DEV):
            off = row0
            for buf, lead, rows in pieces:
                pltpu.make_async_remote_copy(
                    src_ref=p_ref.at[pl.ds(off, rows)], dst_ref=bufs[buf].at[lead(me)], send_sem=send_sems.at[k - 1],
                    recv_sem=recv_sems.at[k - 1], device_id=_peer(x, y, c, k), device_id_type=MESH).start()
                off += rows
        token[...] = jnp.zeros_like(token)

    sems, dt = pltpu.SemaphoreType.DMA((N_DEV - 1,)), pack.dtype
    return pl.pallas_call(
        body, name=name,
        out_shape=(sems, sems, pltpu.HBM(pack.shape, dt)) + tuple(pltpu.HBM(s, dt) for s in shapes) + (_token_shape(),),
        in_specs=(HBM,) * (1 + n), out_specs=(SEM, SEM) + (HBM,) * (1 + n) + (pl.BlockSpec(memory_space=pltpu.VMEM),),
        input_output_aliases={i: 2 + i for i in range(1 + n)}, compiler_params=_split_params(),
    )(pltpu.with_memory_space_constraint(pack, pltpu.HBM), *[_empty_hbm(s, dt) for s in shapes])


def _gather_wait(started, row0, n_rows, after, *, name):
    send_sems, recv_sems, pack, *bufs = started[:-1]
    n = len(bufs)

    def body(*refs):
        _wait_all(refs[0].at[pl.ds(row0, n_rows)], refs[1 + n], refs[2 + n], _position())

    outs = pl.pallas_call(
        body, name=name, out_shape=tuple(pltpu.HBM(a.shape, a.dtype) for a in (pack, *bufs)),
        in_specs=(HBM,) * (1 + n) + (SEM, SEM, ANY), out_specs=(HBM,) * (1 + n),
        input_output_aliases={i: i for i in range(1 + n)}, compiler_params=_split_params(),
    )(pack, *bufs, send_sems, recv_sems, after)
    return outs[0], outs[1:]


def _scatter_start(srcs, pieces, *, name):
    n = len(srcs)
    land_shape = (N_DEV, sum(p[2] for p in pieces), PACK_W)

    def body(*refs):
        src_refs, land_ref, send_sems, recv_sems, token = refs[:n], refs[n], refs[n + 1], refs[n + 2], refs[-1]
        x, y, c = _position()
        me = 4 * x + 2 * y + c
        for k in range(1, N_DEV):
            px, py, pc = _peer(x, y, c, k)
            off = 0
            for si, lead, rows in pieces:
                pltpu.make_async_remote_copy(
                    src_ref=src_refs[si].at[lead(4 * px + 2 * py + pc)], dst_ref=land_ref.at[me, pl.ds(off, rows)],
                    send_sem=send_sems.at[k - 1], recv_sem=recv_sems.at[k - 1], device_id=(px, py, pc),
                    device_id_type=MESH).start()
                off += rows
        token[...] = jnp.zeros_like(token)

    sems, dt = pltpu.SemaphoreType.DMA((N_DEV - 1,)), srcs[0].dtype
    return pl.pallas_call(
        body, name=name,
        out_shape=(sems, sems) + tuple(pltpu.HBM(a.shape, dt) for a in srcs) + (pltpu.HBM(land_shape, dt), _token_shape()),
        in_specs=(HBM,) * (n + 1), out_specs=(SEM, SEM) + (HBM,) * (n + 1) + (pl.BlockSpec(memory_space=pltpu.VMEM),),
        input_output_aliases={i: 2 + i for i in range(n + 1)}, compiler_params=_split_params(),
    )(*[pltpu.with_memory_space_constraint(a, pltpu.HBM) for a in srcs], _empty_hbm(land_shape, dt))


def _scatter_wait(started, after, *, name):
    send_sems, recv_sems, *bufs = started[:-1]
    n = len(bufs)

    def body(*refs):
        _wait_all(refs[n - 1].at[0], refs[n], refs[n + 1], _position())

    return pl.pallas_call(
        body, name=name, out_shape=tuple(pltpu.HBM(a.shape, a.dtype) for a in bufs),
        in_specs=(HBM,) * n + (SEM, SEM, ANY), out_specs=(HBM,) * n, input_output_aliases={i: i for i in range(n)},
        compiler_params=_split_params(),
    )(*bufs, send_sems, recv_sems, after)


def _peer_sum(own, own_lead, land, block, rows, idx, *, name):
    owns = list(own) if isinstance(own, (list, tuple)) else [own]
    n, lead_rank = len(owns), owns[0].ndim - 2

    def body(idx_ref, *refs):
        own_refs, land_refs, o_ref = refs[:n], refs[n:n + N_DEV - 1], refs[n + N_DEV - 1]
        for j in range(n):
            rs_ = slice(j * rows, (j + 1) * rows)
            acc = own_refs[j][(0,) * lead_rank].astype(F32)
            for k in range(N_DEV - 1):
                acc = acc + land_refs[k][0, rs_].astype(F32)
            o_ref[rs_] = acc

    own_spec = pl.BlockSpec((1,) * lead_rank + (rows, PACK_W), lambda i, t: own_lead(t[0]) + (0, 0))

    def land_spec(k):
        return pl.BlockSpec((1, n * rows, PACK_W), lambda i, t: (t[k + 1], block, 0))

    return pl.pallas_call(
        body, name=name,
        grid_spec=pltpu.PrefetchScalarGridSpec(
            num_scalar_prefetch=1, grid=(1,), in_specs=[own_spec] * n + [land_spec(k) for k in range(N_DEV - 1)],
            out_specs=pl.BlockSpec((n * rows, PACK_W), lambda i, t: (0, 0))),
        out_shape=jax.ShapeDtypeStruct((n * rows, PACK_W), F32), compiler_params=_cparams(("arbitrary",)),
    )(idx, *owns, *([land] * (N_DEV - 1)))


def _sum_adamw(own, own_lead, land, block, rows, idx, w, m, v, *, name):
    lead_rank, r = own.ndim - 2, w.shape[1]

    def body(idx_ref, own_ref, *refs):
        land_refs, (w_ref, m_ref, v_ref), outs = refs[:N_DEV - 1], refs[N_DEV - 1:N_DEV + 2], refs[N_DEV + 2:]
        g = own_ref[(0,) * lead_rank + (slice(0, r),)].astype(F32)
        for k in range(N_DEV - 1):
            g = g + land_refs[k][0, 0:r].astype(F32)
        for o_ref, val in zip(outs, (g,) + tuple(_adamw(w_ref[0], g, m_ref[0], v_ref[0]))):
            o_ref[0] = val

    own_spec = pl.BlockSpec((1,) * lead_rank + (rows, PACK_W), lambda i, t: own_lead(t[0]) + (0, 0))
    shard = pl.BlockSpec((1, r, PACK_W), lambda i, t: (0, 0, 0))

    def land_spec(k):
        return pl.BlockSpec((1, rows, PACK_W), lambda i, t: (t[k + 1], block, 0))

    return pl.pallas_call(
        body, name=name,
        grid_spec=pltpu.PrefetchScalarGridSpec(
            num_scalar_prefetch=1, grid=(1,),
            in_specs=[own_spec] + [land_spec(k) for k in range(N_DEV - 1)] + [shard] * 3, out_specs=[shard] * 4),
        out_shape=[jax.ShapeDtypeStruct((1, r, PACK_W), F32)] * 4, compiler_params=_cparams(("arbitrary",)),
    )(idx, own, *([land] * (N_DEV - 1)), w, m, v)


def _adamw(w, g, m, v):
    m = ADAM_B1 * m + (1.0 - ADAM_B1) * g
    v = ADAM_B2 * v + (1.0 - ADAM_B2) * (g * g)
    m_hat = m / (1.0 - ADAM_B1 ** ADAM_STEP)
    v_hat = v / (1.0 - ADAM_B2 ** ADAM_STEP)
    delta = -ADAM_LR * (m_hat / (jnp.sqrt(v_hat) + ADAM_EPS) + ADAM_WD * w)
    return delta, m, v


def _adamw_call(w, g, m, v, *, name, max_rows=256):
    _, r, c_ = w.shape
    tr = max_rows if r > max_rows and r % max_rows == 0 else r

    def body(w_ref, g_ref, m_ref, v_ref, d_ref, mo_ref, vo_ref):
        d, mn, vn = _adamw(w_ref[0], g_ref[...], m_ref[0], v_ref[0])
        d_ref[0] = d
        mo_ref[0] = mn
        vo_ref[0] = vn

    row3 = pl.BlockSpec((1, tr, c_), lambda i: (0, i, 0))
    shp = jax.ShapeDtypeStruct((1, r, c_), F32)
    return pl.pallas_call(
        body, name=name, grid=(r // tr,), in_specs=[row3, pl.BlockSpec((tr, c_), lambda i: (i, 0)), row3, row3],
        out_specs=[row3] * 3, out_shape=[shp] * 3, compiler_params=_cparams(("parallel",)),
    )(w, g, m, v)


SMALL = ("mix_norm_g", "ffn_norm_g", "final_norm_g", "q_norm_g", "kv_norm_g", "swa_sinks")
SMALL_W = dict(mix_norm_g=1024, ffn_norm_g=1024, final_norm_g=1024, q_norm_g=Q_LORA, kv_norm_g=KV_LORA, swa_sinks=SWA_HEADS)


def _small_adamw(parts, w, m, v):
    ns = len(SMALL)

    def body(p_ref, *refs):
        ins, outs = refs[:3 * ns], refs[3 * ns:]
        tot = p_ref[0]
        for dev in range(1, N_DEV):
            tot = tot + p_ref[dev]
        for k, n in enumerate(SMALL):
            g = jnp.sum(tot[k * SUBLANES:(k + 1) * SUBLANES, :SMALL_W[n]], axis=0, keepdims=True)
            res = _adamw(ins[k][...], g, ins[ns + k][...], ins[2 * ns + k][...])
            for j, r in enumerate((g,) + tuple(res)):
                outs[j * ns + k][...] = r
        outs[4 * ns][...] = jnp.sum(tot[ns * SUBLANES:(ns + 1) * SUBLANES, 0:1], axis=0, keepdims=True)

    shapes = [jax.ShapeDtypeStruct((1, SMALL_W[n]), F32) for n in SMALL]
    vm = pl.BlockSpec(memory_space=pltpu.VMEM)
    out = pl.pallas_call(
        body, name="small_adamw", in_specs=[vm] * (1 + 3 * ns), out_specs=[vm] * (4 * ns + 1),
        out_shape=shapes * 4 + [jax.ShapeDtypeStruct((1, 1), F32)],
    )(parts, *[d[n] for d in (w, m, v) for n in SMALL])
    return [dict(zip(SMALL, out[j * ns:(j + 1) * ns])) for j in range(4)] + [out[4 * ns]]


def _small_pack(d, rows_each):
    parts = [jnp.pad(d[n].astype(F32), ((0, 0), (0, PACK_W - SMALL_W[n]))) for n in SMALL]
    out = jnp.concatenate(parts, 0)
    pad = -out.shape[0] % SUBLANES
    return jnp.pad(out, ((0, pad), (0, 0)))


def kernel(x, mix_norm_g, w_in, swa_sinks, q_norm_g, w_uq, kv_norm_g, w_ukv, w_o_swa, w_o_mla, w_out, ffn_norm_g, w_gate, w_up, w_down, final_norm_g, loss_target, m_mix_norm_g, m_w_in, m_swa_sinks, m_q_norm_g, m_w_uq, m_kv_norm_g, m_w_ukv, m_w_o_swa, m_w_o_mla, m_w_out, m_ffn_norm_g, m_w_gate, m_w_up, m_w_down, m_final_norm_g, v_mix_norm_g, v_w_in, v_swa_sinks, v_q_norm_g, v_w_uq, v_kv_norm_g, v_w_ukv, v_w_o_swa, v_w_o_mla, v_w_out, v_ffn_norm_g, v_w_gate, v_w_up, v_w_down, v_final_norm_g):
    big_w = dict(w_in=w_in[0], w_uq=w_uq[0], w_ukv=w_ukv[0], w_o_swa=w_o_swa[0], w_o_mla=w_o_mla[0], w_out=w_out[0],
                 w_gate=w_gate[0], w_up=w_up[0], w_down=w_down[0])
    big_w3 = dict(w_in=w_in, w_uq=w_uq, w_ukv=w_ukv, w_o_swa=w_o_swa, w_o_mla=w_o_mla, w_out=w_out, w_gate=w_gate, w_up=w_up,
                  w_down=w_down)
    big_m = dict(w_in=m_w_in, w_uq=m_w_uq, w_ukv=m_w_ukv, w_o_swa=m_w_o_swa, w_o_mla=m_w_o_mla, w_out=m_w_out,
                 w_gate=m_w_gate, w_up=m_w_up, w_down=m_w_down)
    big_v = dict(w_in=v_w_in, w_uq=v_w_uq, w_ukv=v_w_ukv, w_o_swa=v_w_o_swa, w_o_mla=v_w_o_mla, w_out=v_w_out,
                 w_gate=v_w_gate, w_up=v_w_up, w_down=v_w_down)
    small_w = dict(mix_norm_g=mix_norm_g, ffn_norm_g=ffn_norm_g, final_norm_g=final_norm_g.reshape(1, D_MODEL),
                   q_norm_g=q_norm_g, kv_norm_g=kv_norm_g, swa_sinks=swa_sinks)
    small_m = dict(mix_norm_g=m_mix_norm_g, ffn_norm_g=m_ffn_norm_g, final_norm_g=m_final_norm_g.reshape(1, D_MODEL),
                   q_norm_g=m_q_norm_g, kv_norm_g=m_kv_norm_g, swa_sinks=m_swa_sinks)
    small_v = dict(mix_norm_g=v_mix_norm_g, ffn_norm_g=v_ffn_norm_g, final_norm_g=v_final_norm_g.reshape(1, D_MODEL),
                   q_norm_g=v_q_norm_g, kv_norm_g=v_kv_norm_g, swa_sinks=v_swa_sinks)

    px, py, pc = _position()
    me = 4 * px + 2 * py + pc
    idx = jnp.stack([me] + [4 * qx + 2 * qy + qc for qx, qy, qc in (_peer(px, py, pc, k) for k in range(1, N_DEV))])
    idx = idx.astype(jnp.int32)

    dev = lambda d: (d,)
    pack = _wire_pack(big_w, WIRE_DTYPE)
    win_g, = _all_gather(pack, ((0, dev, 0, W_IN_ROWS),), ((N_DEV, W_IN_ROWS, PACK_W),), name="ag_early")
    mid_pieces = tuple((b, dev, OUT_ROWS) for b in range(MID_BLOCKS))
    ag_mid = _gather_start(pack, W_IN_ROWS, mid_pieces, ((N_DEV, OUT_ROWS, PACK_W),) * MID_BLOCKS, name="ag_mid_start")
    ag = {}

    def own_rows(r0, r1, shape):
        return pack[r0:r1].reshape(shape)

    def mid_weights(after):
        pack_mid, blocks = _gather_wait(ag_mid, W_IN_ROWS, MID_ROWS, after, name="ag_mid_wait")
        ag["late"] = _gather_start(pack_mid, EARLY_ROWS, ((0, _gate_slab, FF_COLS), (0, _up_slab, FF_COLS), (1, dev, FF_COLS)),
                                   (GU_SHAPE, D_SHAPE), name="ag_late_start")
        row0 = lambda b: W_IN_ROWS + b * OUT_ROWS
        ops = _mid_operands(*[lax.dynamic_update_slice(blk, own_rows(row0(b), row0(b + 1), (1, OUT_ROWS, PACK_W)), (me, 0, 0))
                              for b, blk in enumerate(blocks)])
        ops["wuq"] = ops["wuq"] + ag["late"][-1][0:1, 0:1].astype(ops["wuq"].dtype)
        return ops

    def late_weights(after):
        _, (gu, d) = _gather_wait(ag["late"], EARLY_ROWS, LATE_ROWS, after, name="ag_late_wait")
        slab = (1, 1, 1, FF_COLS, PACK_W)
        gu = lax.dynamic_update_slice(gu, own_rows(EARLY_ROWS, EARLY_ROWS + FF_COLS, slab), _gate_slab(me) + (0, 0))
        gu = lax.dynamic_update_slice(gu, own_rows(EARLY_ROWS + FF_COLS, EARLY_ROWS + 2 * FF_COLS, slab), _up_slab(me) + (0, 0))
        d = lax.dynamic_update_slice(d, own_rows(EARLY_ROWS + 2 * FF_COLS, PACK_ROWS, (1, FF_COLS, PACK_W)), (me, 0, 0))
        return gu.reshape(2 * D_FF, D_MODEL), d.reshape(D_FF, D_MODEL)

    rs = {}

    def late_grads(g_gu, g_d):
        rs["late"] = _scatter_start([g_gu.reshape(GU_SHAPE), g_d.reshape(D_SHAPE)],
                                    ((0, _gate_slab, FF_COLS), (0, _up_slab, FF_COLS), (1, dev, FF_COLS)),
                                    name="rs_late_start")
        return rs["late"][-1]

    def mid_grads(g):
        rs["mid"] = _scatter_start(_mid_grad_pack(g), mid_pieces, name="rs_mid_start")
        return rs["mid"][-1]

    def last_grads(g_win_t):
        rs["last"] = _scatter_start([_w_in_grad_chunks(g_win_t)], ((0, dev, W_IN_ROWS),), name="rs_last_start")
        return rs["last"][-1]

    first_w = dict(small_w, mix_norm_g=mix_norm_g + ag_mid[-1][0:1, 0:1])
    loss_tot, gx, g_small = _local_step(
        x[0], loss_target[0], _w_in_operand(win_g), first_w, types.SimpleNamespace(mid=mid_weights, late=late_weights),
        types.SimpleNamespace(late=late_grads, mid=mid_grads, last=last_grads))

    loss_rows = jnp.pad(loss_tot[0:1, 0:1], ((0, SUBLANES - 1), (0, PACK_W - 1)))
    small_rows = jnp.concatenate([_small_pack(g_small_rows(g_small), SUBLANES), loss_rows], 0)
    n_small = small_rows.shape[0]
    ag_small = _gather_start(small_rows, 0, ((0, dev, n_small),), ((N_DEV, n_small, PACK_W),), name="ag_small_start")

    g_gu, g_d, land_late = _scatter_wait(rs["late"], ag_small[-1], name="rs_late_wait")
    *g_mid, land_mid = _scatter_wait(rs["mid"], ag_small[-1], name="rs_mid_wait")
    g_win, land_last = _scatter_wait(rs["last"], ag_small[-1], name="rs_last_wait")
    swap = lambda a: jnp.swapaxes(a, 1, 2)
    same = lambda a: a
    chunks = dict(w_gate=(swap, g_gu, _gate_slab, land_late, 0, FF_COLS), w_up=(swap, g_gu, _up_slab, land_late, 1, FF_COLS),
                  w_down=(same, g_d, dev, land_late, 2, FF_COLS), w_in=(swap, g_win, dev, land_last, 0, W_IN_ROWS))
    gw, dw, mw, vw = {}, {}, {}, {}
    for n, (view, own, lead, land, blk, rows) in chunks.items():
        res = _sum_adamw(own, lead, land, blk, rows, idx, view(big_w3[n]), view(big_m[n]), view(big_v[n]), name="adamw_" + n)
        gw[n], dw[n], mw[n], vw[n] = (view(r) for r in res)
    g_nat = _mid_unpack(_peer_sum(g_mid, dev, land_mid, 0, OUT_ROWS, idx, name="rs_sum_mid"))
    for n, g in g_nat.items():
        gw[n] = g[None]
        dw[n], mw[n], vw[n] = _adamw_call(big_w3[n], g, big_m[n], big_v[n], name="adamw_" + n)

    own_small, (parts,) = _gather_wait(ag_small, 0, n_small, vw[n], name="ag_small_wait")
    parts = lax.dynamic_update_slice(parts, own_small[None], (me, 0, 0))
    gs, ds, ms, vs, loss = _small_adamw(parts, small_w, small_m, small_v)
    loss = loss[0, 0]
    for d in (gs, ds, ms, vs):
        d["final_norm_g"] = d["final_norm_g"].reshape(D_MODEL)

    order = ("mix_norm_g", "w_in", "swa_sinks", "q_norm_g", "w_uq", "kv_norm_g", "w_ukv", "w_o_swa", "w_o_mla", "w_out",
             "ffn_norm_g", "w_gate", "w_up", "w_down", "final_norm_g")

    def leaves(big, small):
        return [big[n] if n in big else small[n] for n in order]

    return (loss, gx[None], *leaves(gw, gs), *leaves(dw, ds), *leaves(mw, ms), *leaves(vw, vs))


def g_small_rows(g_small):
    out = dict(g_small)
    out["swa_sinks"] = jnp.pad(g_small["swa_sinks"], ((0, SUBLANES - 1), (0, 0)))
    return out
```

```python
import types

import numpy as np
import jax
import jax.numpy as jnp
from jax import lax
from jax.experimental import pallas as pl
from jax.experimental.pallas import tpu as pltpu

F32 = jnp.float32
MXU_DTYPE = jnp.bfloat16
WIRE_DTYPE = jnp.bfloat16

D_MODEL = 1024
EPS = 1e-6
ROPE_THETA = 10000.0
BLOCK = 128
HEAD_DIM = 64
SWA_HEADS = 8
SWA_KV_HEADS = 2
SWA_GROUP = SWA_HEADS // SWA_KV_HEADS
MLA_HEADS = 8
MLA_NOPE = 64
MLA_ROPE = 32
MLA_V = 64
MLA_QK = MLA_NOPE + MLA_ROPE
Q_LORA = 384
KV_LORA = 256
D_FF = 2816
IN_SIZES = (512, 128, 128, Q_LORA, KV_LORA, MLA_ROPE, D_MODEL, D_MODEL)
IN_OFF = tuple(int(v) for v in np.cumsum((0,) + IN_SIZES))
ADAM_LR, ADAM_B1, ADAM_B2, ADAM_EPS, ADAM_WD, ADAM_STEP = 0.001, 0.9, 0.999, 1e-08, 0.01, 10

LANES = 128
SUBLANES = 8
VMEM_LIMIT = 48 * 1024 * 1024
N_DEV = 8

P_GA, P_GB, P_Q, P_QLAT, P_KR, P_K, P_V, P_KVLAT, P_W = 0, 1024, 2048, 3072, 3456, 3584, 3840, 4096, 4352
KR_LANE = 64

LOG2E = 1.4426950408889634

NT = (((1,), (1,)), ((), ()))
NN = (((1,), (0,)), ((), ()))
TN = (((0,), (0,)), ((), ()))


def _cparams(sem):
    return pltpu.CompilerParams(dimension_semantics=sem, vmem_limit_bytes=VMEM_LIMIT)


def _mm(a, b, mode, *, name, out_dtype=F32, add=None, tm=512, tn=512, tk=None):
    if mode == "nn":
        (M, K), (K2, N) = a.shape, b.shape
    elif mode == "nt":
        (M, K), (N, K2) = a.shape, b.shape
    else:
        (K, M), (K2, N) = a.shape, b.shape
    assert K == K2, (a.shape, b.shape, mode)
    tm, tn, tk = min(tm, M), min(tn, N), K if tk is None else min(tk, K)
    assert M % tm == 0 and N % tn == 0 and K % tk == 0, (M, N, K, tm, tn, tk)
    nk = K // tk
    dn = {"nn": NN, "nt": NT, "tn": TN}[mode]
    if mode == "tn":
        a_spec = pl.BlockSpec((tk, tm), lambda i, j, k: (k, i))
    else:
        a_spec = pl.BlockSpec((tm, tk), lambda i, j, k: (i, k))
    once = dict(pipeline_mode=pl.Buffered(1)) if (nk == 1 and tn == N) else {}
    if mode == "nt":
        b_spec = pl.BlockSpec((tn, tk), lambda i, j, k: (j, k), **once)
    else:
        b_spec = pl.BlockSpec((tk, tn), lambda i, j, k: (k, j), **once)
    o_spec = pl.BlockSpec((tm, tn), lambda i, j, k: (i, j))
    has_add = add is not None

    def body(*refs):
        a_ref, b_ref = refs[0], refs[1]
        add_ref = refs[2] if has_add else None
        o_ref = refs[2 + has_add]
        p = lax.dot_general(a_ref[...], b_ref[...], dn, preferred_element_type=F32)

        def finish(acc):
            if has_add:
                acc = acc + add_ref[...]
            o_ref[...] = acc.astype(o_ref.dtype)

        if nk == 1:
            finish(p)
        else:
            acc_ref = refs[-1]
            k = pl.program_id(2)

            @pl.when(k == 0)
            def _():
                acc_ref[...] = p

            @pl.when((k > 0) & (k < nk - 1))
            def _():
                acc_ref[...] += p

            @pl.when(k == nk - 1)
            def _():
                finish(acc_ref[...] + p)

    ins = [a, b] + ([add] if has_add else [])
    return pl.pallas_call(
        body, name=name, grid=(M // tm, N // tn, nk), in_specs=[a_spec, b_spec] + ([o_spec] if has_add else []), out_specs=o_spec,
        out_shape=jax.ShapeDtypeStruct((M, N), out_dtype),
        scratch_shapes=[pltpu.VMEM((tm, tn), F32)] if nk > 1 else [],
        compiler_params=_cparams(("parallel", "parallel", "arbitrary")),
    )(*ins)


def _rows(ts, w, cb=0):
    return pl.BlockSpec((ts, w), lambda i: (i, cb))


def _const(r, w):
    return pl.BlockSpec((r, w), lambda i: (0, 0))


def _sublane_sum(v):
    ts, c = v.shape
    return jnp.sum(v.reshape(ts // SUBLANES, SUBLANES, c), axis=0)


def _sigmoid(v):
    return 1.0 / (1.0 + jnp.exp(-v))


def _rope(v, cos, s_up, s_dn, up, dn):
    return v * cos + pltpu.roll(v, up, 1) * s_up + pltpu.roll(v, dn, 1) * s_dn


def _rope_t(dv, cos, s_up, s_dn, up, dn):
    return dv * cos + pltpu.roll(dv * s_up, dn, 1) + pltpu.roll(dv * s_dn, up, 1)


def _rope_tables(seq):
    pos = np.arange(seq, dtype=np.float32)[:, None]

    def base(dim):
        inv = np.float32(ROPE_THETA) ** (-np.arange(0, dim, 2, dtype=np.float32) / np.float32(dim))
        ang = (pos * inv.astype(np.float32)[None, :]).astype(np.float32)
        return np.cos(ang).astype(np.float32), np.sin(ang).astype(np.float32)

    z = lambda n: np.zeros((seq, n), np.float32)
    ca, sa = base(HEAD_DIM)
    a_cos = np.concatenate([ca, ca, z(64)], 1)
    a_up = np.concatenate([-sa, z(96)], 1)
    a_dn = np.concatenate([z(32), sa, z(64)], 1)
    cb, sb = base(MLA_ROPE)
    one = np.ones((seq, 64), np.float32)
    q_cos = np.concatenate([one, cb, cb, z(32)], 1)
    k_cos = np.concatenate([z(64), cb, cb, z(32)], 1)
    b_up = np.concatenate([z(64), -sb, z(48)], 1)
    b_dn = np.concatenate([z(80), sb, z(32)], 1)
    return tuple(jnp.asarray(t) for t in (a_cos, a_up, a_dn, q_cos, k_cos, b_up, b_dn))


def _rms(v, g):
    return v * lax.rsqrt(jnp.mean(v * v, axis=-1, keepdims=True) + EPS) * g


def _rms_bwd(v, g, d):
    r = lax.rsqrt(jnp.mean(v * v, axis=-1, keepdims=True) + EPS)
    xh = v * r
    dxh = d * g
    return r * (dxh - xh * jnp.mean(dxh * xh, axis=-1, keepdims=True)), d * xh


F_GA, F_GB, F_KVLAT, F_QLAT, F_W = 0, 1024, 2048, 2304, 2688


def _proj_in(x, g, w_t, gq, gkv, tabs, *, tm=512):
    s_, c = x.shape
    a_cos, a_up, a_dn, _, k_cos, b_up, b_dn = tabs

    def body(x_ref, g_ref, w_ref, gq_ref, gkv_ref, ac, au, ad, kc, bu, bd,
             h_ref, qa_ref, ka_ref, va_ref, cq_ref, ckv_ref, kro_ref, pf_ref):
        h = _rms(x_ref[...], g_ref[...]).astype(h_ref.dtype)
        h_ref[...] = h
        mm = lambda a, b: lax.dot_general(h, w_ref[a:b, :], NT, preferred_element_type=F32)
        pf_ref[:, F_GA:F_KVLAT] = mm(P_GA, P_Q)
        c_, u_, d_ = ac[...], au[...], ad[...]
        q = mm(P_Q, P_QLAT)
        for hd in range(SWA_HEADS):
            sl = slice(hd * LANES, (hd + 1) * LANES)
            qa_ref[:, sl] = _rope(q[:, sl], c_, u_, d_, 96, 32).astype(qa_ref.dtype)
        kv = mm(P_KR, P_KVLAT)
        kro_ref[...] = _rope(kv[:, :LANES], kc[...], bu[...], bd[...], 112, 16)
        for hd in range(SWA_KV_HEADS):
            sl = slice((1 + hd) * LANES, (2 + hd) * LANES)
            ka_ref[:, hd * LANES:(hd + 1) * LANES] = _rope(kv[:, sl], c_, u_, d_, 96, 32).astype(ka_ref.dtype)
        va_ref[...] = kv[:, P_V - P_KR:].astype(va_ref.dtype)
        for a, b, f0, gref, dst in ((P_QLAT, P_KR, F_QLAT, gq_ref, cq_ref), (P_KVLAT, P_W, F_KVLAT, gkv_ref, ckv_ref)):
            v = mm(a, b)
            pf_ref[:, f0:f0 + b - a] = v
            r = lax.rsqrt(jnp.mean(v * v, axis=-1, keepdims=True) + EPS)
            dst[...] = (v * r * gref[...]).astype(dst.dtype)

    tab = _rows(tm, LANES)
    widths = (c, SWA_HEADS * LANES, SWA_KV_HEADS * LANES, SWA_KV_HEADS * LANES, Q_LORA, KV_LORA)
    return pl.pallas_call(
        body, name="proj_in", grid=(s_ // tm,),
        in_specs=[_rows(tm, c), _const(1, c), pl.BlockSpec((P_W, c), lambda i: (0, 0), pipeline_mode=pl.Buffered(1)),
                  _const(1, Q_LORA), _const(1, KV_LORA), tab, tab, tab, tab, tab, tab],
        out_specs=[_rows(tm, w) for w in widths] + [tab, _rows(tm, F_W)],
        out_shape=[jax.ShapeDtypeStruct((s_, w), MXU_DTYPE) for w in widths]
        + [jax.ShapeDtypeStruct((s_, LANES), F32), jax.ShapeDtypeStruct((s_, F_W), F32)],
        compiler_params=_cparams(("parallel",)),
    )(x, g, w_t, gq, gkv, a_cos, a_up, a_dn, k_cos, b_up, b_dn)


def _mm_norm_bwd(a, b, x, g, res, *, name, after=None, tm=512):
    s_, kk = a.shape
    c = b.shape[1]
    has_after = after is not None

    def body(*refs):
        a_ref, b_ref, x_ref, g_ref, res_ref = refs[:5]
        dx_ref, dxb_ref, dg_ref = refs[5 + has_after:]
        d = jnp.dot(a_ref[...], b_ref[...], preferred_element_type=F32)
        dx, gg = _rms_bwd(x_ref[...], g_ref[...], d)
        dx = dx + res_ref[...]
        dx_ref[...] = dx
        dxb_ref[...] = dx.astype(dxb_ref.dtype)

        @pl.when(pl.program_id(0) == 0)
        def _():
            dg_ref[...] = jnp.zeros(dg_ref.shape, F32)

        dg_ref[...] += _sublane_sum(gg)

    row = _rows(tm, c)
    in_specs = [_rows(tm, kk), pl.BlockSpec((kk, c), lambda i: (0, 0), pipeline_mode=pl.Buffered(1)), row, _const(1, c), row]
    return pl.pallas_call(
        body, name=name, grid=(s_ // tm,), in_specs=in_specs + ([pl.BlockSpec(memory_space=pl.ANY)] if has_after else []),
        out_specs=[row, row, _const(SUBLANES, c)],
        out_shape=[jax.ShapeDtypeStruct((s_, c), F32), jax.ShapeDtypeStruct((s_, c), MXU_DTYPE),
                   jax.ShapeDtypeStruct((SUBLANES, c), F32)],
        compiler_params=_cparams(("arbitrary",)),
    )(*([a, b, x, g, res] + ([after] if has_after else [])))


def _mla_up(cq, ckv, kro, wuq, wuk, wuv, tabs, *, ts=512):
    s_ = cq.shape[0]
    _, _, _, q_cos, _, b_up, b_dn = tabs

    def body(cq_ref, ckv_ref, kr_ref, wq_ref, wk_ref, wv_ref, qc, bu, bd, qo_ref, ko_ref, vo_ref):
        c_, u_, d_ = qc[...], bu[...], bd[...]
        kr = kr_ref[...]
        ckv_ = ckv_ref[...]
        vo_ref[...] = jnp.dot(ckv_, wv_ref[...], preferred_element_type=F32).astype(vo_ref.dtype)
        q = jnp.dot(cq_ref[...], wq_ref[...], preferred_element_type=F32)
        k = jnp.dot(ckv_, wk_ref[...], preferred_element_type=F32)
        for h in range(MLA_HEADS):
            sl = slice(h * LANES, (h + 1) * LANES)
            qo_ref[:, sl] = _rope(q[:, sl], c_, u_, d_, 112, 16).astype(qo_ref.dtype)
            ko_ref[:, sl] = (k[:, sl] + kr).astype(ko_ref.dtype)

    tab, out = _rows(ts, LANES), _rows(ts, 1024)
    return pl.pallas_call(
        body, name="mla_up", grid=(s_ // ts,),
        in_specs=[_rows(ts, Q_LORA), _rows(ts, KV_LORA), tab, _const(Q_LORA, 1024), _const(KV_LORA, 1024),
                  _const(KV_LORA, 1024), tab, tab, tab],
        out_specs=[out, out, out], out_shape=[jax.ShapeDtypeStruct((s_, 1024), MXU_DTYPE)] * 3,
        compiler_params=_cparams(("parallel",)),
    )(cq, ckv, kro, wuq, wuk, wuv, q_cos, b_up, b_dn)


def _mla_up_bwd(dqc, dkc, dvp, wuq, wukv, p, gq, gkv, tabs, *, ts=512):
    s_ = dqc.shape[0]
    _, _, _, q_cos, k_cos, b_up, b_dn = tabs

    def body(dq_ref, dk_ref, dv_ref, wq_ref, wkv_ref, ql_ref, kvl_ref, gq_ref, gkv_ref, qc, kc, bu, bd,
             dqo_ref, dkvo_ref, dkr_ref, dql_ref, dkvl_ref, dgq_ref, dgkv_ref):
        c_, u_, d_ = qc[...], bu[...], bd[...]
        tot = jnp.zeros((ts, LANES), F32)
        for h in range(MLA_HEADS):
            sl = slice(h * LANES, (h + 1) * LANES)
            dqo_ref[:, sl] = _rope_t(dq_ref[:, sl], c_, u_, d_, 112, 16).astype(dqo_ref.dtype)
            dk = dk_ref[:, sl]
            dkvo_ref[:, sl] = dk.astype(dkvo_ref.dtype)
            tot = tot + dk
        dkvo_ref[:, 1024:2048] = dv_ref[...].astype(dkvo_ref.dtype)
        dkr_ref[...] = _rope_t(tot, kc[...], u_, d_, 112, 16).astype(dkr_ref.dtype)

        @pl.when(pl.program_id(0) == 0)
        def _():
            dgq_ref[...] = jnp.zeros(dgq_ref.shape, F32)
            dgkv_ref[...] = jnp.zeros(dgkv_ref.shape, F32)

        for do_ref, w_ref, x_ref, g_ref, dx_ref, dg_ref in ((dqo_ref, wq_ref, ql_ref, gq_ref, dql_ref, dgq_ref),
                                                            (dkvo_ref, wkv_ref, kvl_ref, gkv_ref, dkvl_ref, dgkv_ref)):
            d = lax.dot_general(do_ref[...], w_ref[...], NT, preferred_element_type=F32)
            dx, gg = _rms_bwd(x_ref[...], g_ref[...], d)
            dx_ref[...] = dx.astype(dx_ref.dtype)
            dg_ref[...] += _sublane_sum(gg)

    tab = _rows(ts, LANES)
    return pl.pallas_call(
        body, name="mla_up_bwd", grid=(s_ // ts,),
        in_specs=[_rows(ts, 1024), _rows(ts, 1024), _rows(ts, 1024), _const(Q_LORA, 1024), _const(KV_LORA, 2048),
                  _rows(ts, Q_LORA, F_QLAT // Q_LORA), _rows(ts, KV_LORA, F_KVLAT // KV_LORA),
                  _const(1, Q_LORA), _const(1, KV_LORA), tab, tab, tab, tab],
        out_specs=[_rows(ts, 1024), _rows(ts, 2048), _rows(ts, LANES), _rows(ts, Q_LORA), _rows(ts, KV_LORA),
                   _const(SUBLANES, Q_LORA), _const(SUBLANES, KV_LORA)],
        out_shape=[jax.ShapeDtypeStruct((s_, 1024), MXU_DTYPE), jax.ShapeDtypeStruct((s_, 2048), MXU_DTYPE),
                   jax.ShapeDtypeStruct((s_, LANES), MXU_DTYPE), jax.ShapeDtypeStruct((s_, Q_LORA), MXU_DTYPE),
                   jax.ShapeDtypeStruct((s_, KV_LORA), MXU_DTYPE), jax.ShapeDtypeStruct((SUBLANES, Q_LORA), F32),
                   jax.ShapeDtypeStruct((SUBLANES, KV_LORA), F32)],
        compiler_params=_cparams(("arbitrary",)),
    )(dqc, dkc, dvp, wuq, wukv, p, p, gq, gkv, q_cos, k_cos, b_up, b_dn)


def _assemble_dp(dgab, dqa, dqlat, dkr, dka, dva, dkvlat, tabs, *, ts=512):
    s_ = dqa.shape[0]
    a_cos, a_up, a_dn = tabs[0], tabs[1], tabs[2]

    def body(dg_ref, dq_ref, dql_ref, dkr_ref, dk_ref, dv_ref, dkvl_ref, ac, au, ad, o_ref):
        c_, u_, d_ = ac[...], au[...], ad[...]
        o_ref[:, P_GA:P_Q] = dg_ref[...]
        for h in range(SWA_HEADS):
            sl = slice(h * LANES, (h + 1) * LANES)
            o_ref[:, P_Q + h * LANES:P_Q + (h + 1) * LANES] = _rope_t(dq_ref[:, sl], c_, u_, d_, 96, 32).astype(o_ref.dtype)
        o_ref[:, P_QLAT:P_KR] = dql_ref[...]
        o_ref[:, P_KR:P_K] = dkr_ref[...]
        for h in range(SWA_KV_HEADS):
            sl = slice(h * LANES, (h + 1) * LANES)
            o_ref[:, P_K + h * LANES:P_K + (h + 1) * LANES] = _rope_t(dk_ref[:, sl], c_, u_, d_, 96, 32).astype(o_ref.dtype)
        o_ref[:, P_V:P_KVLAT] = dv_ref[...]
        o_ref[:, P_KVLAT:P_W] = dkvl_ref[...]

    tab = _rows(ts, LANES)
    return pl.pallas_call(
        body, name="assemble_dp", grid=(s_ // ts,),
        in_specs=[_rows(ts, 2048), _rows(ts, 1024), _rows(ts, Q_LORA), _rows(ts, LANES), _rows(ts, 256), _rows(ts, 256),
                  _rows(ts, KV_LORA), tab, tab, tab],
        out_specs=_rows(ts, P_W), out_shape=jax.ShapeDtypeStruct((s_, P_W), MXU_DTYPE),
        compiler_params=_cparams(("parallel",)),
    )(dgab, dqa, dqlat, dkr, dka, dva, dkvlat, a_cos, a_up, a_dn)


def _attn_out_gate(oa, ob, woa_t, wob_t, p, *, ts=512):
    s_ = p.shape[0]

    def body(oa_ref, ob_ref, wa_ref, wb_ref, ga_ref, gb_ref, y_ref):
        ta = lax.dot_general(oa_ref[...], wa_ref[...], NT, preferred_element_type=F32)
        tb = lax.dot_general(ob_ref[...], wb_ref[...], NT, preferred_element_type=F32)
        y_ref[...] = (_sigmoid(ga_ref[...]) * ta + _sigmoid(gb_ref[...]) * tb).astype(y_ref.dtype)

    w = _const(1024, 1024)
    return pl.pallas_call(
        body, name="attn_out_gate", grid=(s_ // ts,),
        in_specs=[_rows(ts, 1024), _rows(ts, 1024), w, w, _rows(ts, 1024, F_GA // 1024), _rows(ts, 1024, F_GB // 1024)],
        out_specs=_rows(ts, 1024), out_shape=jax.ShapeDtypeStruct((s_, 1024), MXU_DTYPE),
        compiler_params=_cparams(("parallel",)),
    )(oa, ob, woa_t, wob_t, p, p)


def _d_y_gate(dx1b, wout, p, oa, ob, woa_t, wob_t, *, ts=512):
    s_ = p.shape[0]

    def body(dx_ref, w_ref, ga_ref, gb_ref, oa_ref, ob_ref, wa_ref, wb_ref, dta_ref, dtb_ref, dg_ref):
        d = lax.dot_general(dx_ref[...], w_ref[...], NT, preferred_element_type=F32)
        sa, sb = _sigmoid(ga_ref[...]), _sigmoid(gb_ref[...])
        dta_ref[...] = (d * sa).astype(dta_ref.dtype)
        dtb_ref[...] = (d * sb).astype(dtb_ref.dtype)
        ta = lax.dot_general(oa_ref[...], wa_ref[...], NT, preferred_element_type=F32)
        dg_ref[:, 0:1024] = (d * ta * (sa * (1.0 - sa))).astype(dg_ref.dtype)
        tb = lax.dot_general(ob_ref[...], wb_ref[...], NT, preferred_element_type=F32)
        dg_ref[:, 1024:2048] = (d * tb * (sb * (1.0 - sb))).astype(dg_ref.dtype)

    w = _const(1024, 1024)
    return pl.pallas_call(
        body, name="d_y_gate", grid=(s_ // ts,),
        in_specs=[_rows(ts, 1024), w, _rows(ts, 1024, F_GA // 1024), _rows(ts, 1024, F_GB // 1024),
                  _rows(ts, 1024), _rows(ts, 1024), w, w],
        out_specs=[_rows(ts, 1024), _rows(ts, 1024), _rows(ts, 2048)],
        out_shape=[jax.ShapeDtypeStruct((s_, 1024), MXU_DTYPE)] * 2 + [jax.ShapeDtypeStruct((s_, 2048), MXU_DTYPE)],
        compiler_params=_cparams(("parallel",)),
    )(dx1b, wout, p, p, oa, ob, woa_t, wob_t)


FF_TILE = D_FF // 2


def _ffn_in_act(x1, g, wgu_t, *, tm=512):
    s_ = x1.shape[0]
    n = s_ // tm

    def body(x_ref, g_ref, w_ref, h_ref, gu_ref, a_ref):
        h = _rms(x_ref[...], g_ref[...]).astype(h_ref.dtype)
        h_ref[...] = h
        p = lax.dot_general(h, w_ref[...], NT, preferred_element_type=F32)
        gu_ref[...] = p
        gate = p[:, :FF_TILE]
        a_ref[...] = (gate * _sigmoid(gate) * p[:, FF_TILE:]).astype(a_ref.dtype)

    return pl.pallas_call(
        body, name="ffn_in", grid=(2, s_ // tm),
        in_specs=[pl.BlockSpec((tm, D_MODEL), lambda j, i: (i, 0)), pl.BlockSpec((1, D_MODEL), lambda j, i: (0, 0)),
                  pl.BlockSpec((2 * FF_TILE, D_MODEL), lambda j, i: (j, 0))],
        out_specs=[pl.BlockSpec((tm, D_MODEL), lambda j, i: (i + j * (n - 1 - i), 0)),
                   pl.BlockSpec((tm, 2 * FF_TILE), lambda j, i: (i, j)),
                   pl.BlockSpec((tm, FF_TILE), lambda j, i: (i, j))],
        out_shape=[jax.ShapeDtypeStruct((s_, D_MODEL), MXU_DTYPE), jax.ShapeDtypeStruct((s_, 2 * D_FF), F32),
                   jax.ShapeDtypeStruct((s_, D_FF), MXU_DTYPE)],
        compiler_params=_cparams(("arbitrary", "arbitrary")),
    )(x1, g, wgu_t)


def _d_act_swiglu(dx2b, wd, gu, *, tm=512):
    s_ = dx2b.shape[0]
    n, slots = s_ // tm, 3
    steps = 2 * n

    def body(d_ref, w_ref, gu_hbm, o_ref, ring, sems):
        s = pl.program_id(0) * n + pl.program_id(1)

        def fetch(step):
            j, i = step // n, step % n
            src = gu_hbm.at[pl.ds(pl.multiple_of(i * tm, tm), tm), pl.ds(pl.multiple_of(j * 2 * FF_TILE, LANES), 2 * FF_TILE)]
            return pltpu.make_async_copy(src, ring.at[step % slots], sems.at[step % slots])

        @pl.when(s == 0)
        def _():
            for ahead in range(min(slots - 1, steps)):
                fetch(ahead).start()

        @pl.when(s + slots - 1 < steps)
        def _():
            fetch(s + slots - 1).start()

        fetch(s).wait()
        da = lax.dot_general(d_ref[...], w_ref[...], NT, preferred_element_type=F32)
        slot = s % slots
        g, u = ring[slot, :, :FF_TILE], ring[slot, :, FF_TILE:]
        sg = _sigmoid(g)
        o_ref[:, :FF_TILE] = (da * u * (sg * (1.0 + g * (1.0 - sg)))).astype(o_ref.dtype)
        o_ref[:, FF_TILE:] = (da * (g * sg)).astype(o_ref.dtype)

    return pl.pallas_call(
        body, name="d_act", grid=(2, n),
        in_specs=[pl.BlockSpec((tm, D_MODEL), lambda j, i: (i, 0)), pl.BlockSpec((FF_TILE, D_MODEL), lambda j, i: (j, 0)),
                  pl.BlockSpec(memory_space=pl.ANY)],
        out_specs=pl.BlockSpec((tm, 2 * FF_TILE), lambda j, i: (i, j)), out_shape=jax.ShapeDtypeStruct((s_, 2 * D_FF), MXU_DTYPE),
        scratch_shapes=[pltpu.VMEM((slots, tm, 2 * FF_TILE), gu.dtype), pltpu.SemaphoreType.DMA((slots,))],
        compiler_params=_cparams(("arbitrary", "arbitrary")),
    )(dx2b, wd, gu)


def _ffn_out_loss(act, wd, x1, g, tgt, *, ts=512):
    s_, c = x1.shape
    kk = act.shape[1]

    def body(a_ref, w_ref, x_ref, g_ref, t_ref, dx_ref, dxb_ref, dg_ref, lp_ref, tot_ref):
        v = x_ref[...] + jnp.dot(a_ref[...], w_ref[...], preferred_element_type=F32)
        r = lax.rsqrt(jnp.mean(v * v, axis=-1, keepdims=True) + EPS)
        xh = v * r
        gg = g_ref[...]
        e = xh * gg - t_ref[...]
        do = e * (1.0 / c)
        dxh = do * gg
        dx = r * (dxh - xh * jnp.mean(dxh * xh, axis=-1, keepdims=True))
        dx_ref[...] = dx
        dxb_ref[...] = dx.astype(dxb_ref.dtype)
        i = pl.program_id(0)

        @pl.when(i == 0)
        def _():
            dg_ref[...] = jnp.zeros(dg_ref.shape, F32)
            lp_ref[...] = jnp.zeros(lp_ref.shape, F32)

        dg_ref[...] += _sublane_sum(do * xh)
        lp_ref[...] += _sublane_sum(e * e)
        tot_ref[...] = jnp.full(tot_ref.shape, (0.5 / c) * jnp.sum(lp_ref[...]), F32)

    return pl.pallas_call(
        body, name="ffn_out_loss", grid=(s_ // ts,),
        in_specs=[_rows(ts, kk), _const(kk, c), _rows(ts, c), _const(1, c), _rows(ts, c)],
        out_specs=[_rows(ts, c), _rows(ts, c), _const(SUBLANES, c), _const(SUBLANES, c), _const(SUBLANES, LANES)],
        out_shape=[jax.ShapeDtypeStruct((s_, c), F32), jax.ShapeDtypeStruct((s_, c), MXU_DTYPE),
                   jax.ShapeDtypeStruct((SUBLANES, c), F32), jax.ShapeDtypeStruct((SUBLANES, c), F32),
                   jax.ShapeDtypeStruct((SUBLANES, LANES), F32)],
        compiler_params=_cparams(("arbitrary",)),
    )(act, wd, x1, g, tgt)


def _mla_d_out(dtb, wob_t, o32, *, ts=512):
    s_ = dtb.shape[0]

    def body(dt_ref, w_ref, o_ref, dob_ref, dl_ref):
        d = jnp.dot(dt_ref[...], w_ref[...], preferred_element_type=F32)
        dob_ref[...] = d.astype(dob_ref.dtype)
        prod = d * o_ref[...]
        for h in range(MLA_HEADS):
            dl_ref[h] = jnp.sum(prod[:, h * LANES:(h + 1) * LANES].T, axis=0, keepdims=True)

    return pl.pallas_call(
        body, name="mla_d_out", grid=(s_ // ts,), in_specs=[_rows(ts, 1024), _const(1024, 1024), _rows(ts, 1024)],
        out_specs=[_rows(ts, 1024), pl.BlockSpec((MLA_HEADS, 1, ts), lambda i: (0, 0, i))],
        out_shape=[jax.ShapeDtypeStruct((s_, 1024), MXU_DTYPE), jax.ShapeDtypeStruct((MLA_HEADS, 1, s_), F32)],
        compiler_params=_cparams(("parallel",)),
    )(dtb, wob_t, o32)


SWA_T = 4 * BLOCK


SWA_W = SWA_GROUP * BLOCK


def _swa_masks(sb):
    kr = lax.broadcasted_iota(jnp.int32, (2 * BLOCK, SWA_W), 0)
    qc = jnp.bitwise_and(lax.broadcasted_iota(jnp.int32, (2 * BLOCK, SWA_W), 1), BLOCK - 1)
    band = jnp.logical_and(kr > qc, kr <= qc + BLOCK)
    first = jnp.logical_and(band, kr >= BLOCK)
    return band, jnp.logical_or(first, jnp.logical_and(band, sb > 0))


def _heads_to_rows(ref, rs):
    return jnp.concatenate([ref[rs, h * LANES:(h + 1) * LANES] for h in range(SWA_GROUP)], axis=0)


def _sink_row(sk_ref):
    return jnp.concatenate([sk_ref[0, h:h + 1, :] for h in range(SWA_GROUP)], axis=1) * LOG2E


def _swa_in_specs(rev, nsb):
    sbi = (lambda j: nsb - 1 - j) if rev else (lambda j: j)
    cur = pl.BlockSpec((SWA_T, LANES), lambda g, j: (sbi(j), g))
    prev = pl.BlockSpec((BLOCK, LANES), lambda g, j: (jnp.maximum(4 * sbi(j) - 1, 0), g))
    q = pl.BlockSpec((SWA_T, SWA_GROUP * LANES), lambda g, j: (sbi(j), g))
    sink = pl.BlockSpec((1, SUBLANES, LANES), lambda g, j: (g, 0, 0))
    lse = pl.BlockSpec((SWA_GROUP, 1, SWA_T), lambda g, j: (g, 0, sbi(j)))
    return q, cur, prev, sink, lse


def _swa_fwd(qa, ka, va, sink_b):
    s_ = qa.shape[0]
    nsb = s_ // SWA_T
    c2 = HEAD_DIM ** -0.5 * LOG2E

    def body(q_ref, kc_ref, kp_ref, vc_ref, vp_ref, sk_ref, o32_ref, o16_ref, lse_ref, kx, vx):
        kx[0:BLOCK, :] = kp_ref[...]
        kx[BLOCK:5 * BLOCK, :] = kc_ref[...]
        vx[0:BLOCK, :] = vp_ref[...]
        vx[BLOCK:5 * BLOCK, :] = vc_ref[...]
        band, band0 = _swa_masks(pl.program_id(1))
        sink2 = _sink_row(sk_ref)
        for b in range(4):
            rs = slice(b * BLOCK, (b + 1) * BLOCK)
            ks = slice(b * BLOCK, (b + 2) * BLOCK)
            st = lax.dot_general(kx[ks, :], _heads_to_rows(q_ref, rs), NT, preferred_element_type=F32) * c2
            st = jnp.where(band0 if b == 0 else band, st, -jnp.inf)
            m = jnp.maximum(jnp.max(st, axis=0, keepdims=True), sink2)
            pt = jnp.exp2(st - m)
            den = jnp.sum(pt, axis=0, keepdims=True) + jnp.exp2(sink2 - m)
            o = lax.dot_general((pt * (1.0 / den)).astype(MXU_DTYPE), vx[ks, :], TN, preferred_element_type=F32)
            lse = m + jnp.log2(den)
            for hh in range(SWA_GROUP):
                cs = slice(hh * LANES, (hh + 1) * LANES)
                o32_ref[rs, cs] = o[cs, :]
                o16_ref[rs, cs] = o[cs, :].astype(o16_ref.dtype)
                lse_ref[hh, :, rs] = lse[:, cs]

    q, cur, prev, sink, lse_spec = _swa_in_specs(False, nsb)
    return pl.pallas_call(
        body, name="swa_fwd", grid=(SWA_KV_HEADS, nsb), in_specs=[q, cur, prev, cur, prev, sink],
        out_specs=[q, q, lse_spec],
        out_shape=[jax.ShapeDtypeStruct((s_, SWA_HEADS * LANES), F32), jax.ShapeDtypeStruct((s_, SWA_HEADS * LANES), MXU_DTYPE),
                   jax.ShapeDtypeStruct((SWA_HEADS, 1, s_), F32)],
        scratch_shapes=[pltpu.VMEM((5 * BLOCK, LANES), MXU_DTYPE), pltpu.VMEM((5 * BLOCK, LANES), MXU_DTYPE)],
        compiler_params=_cparams(("parallel", "arbitrary")),
    )(qa, ka, ka, va, va, sink_b)


def _swa_bwd(qa, ka, va, sink_b, o32, do, lse):
    s_ = qa.shape[0]
    nsb = s_ // SWA_T
    scale = HEAD_DIM ** -0.5
    c2 = scale * LOG2E

    def body(q_ref, kc_ref, kp_ref, vc_ref, vp_ref, sk_ref, o_ref, do_ref, lse_ref,
             dq_ref, dk_ref, dv_ref, dsk_ref, kx, vx, kacc, vacc, kcar, vcar):
        j = pl.program_id(1)
        kx[0:BLOCK, :] = kp_ref[...]
        kx[BLOCK:5 * BLOCK, :] = kc_ref[...]
        vx[0:BLOCK, :] = vp_ref[...]
        vx[BLOCK:5 * BLOCK, :] = vc_ref[...]
        band, band0 = _swa_masks(nsb - 1 - j)
        kacc[...] = jnp.zeros(kacc.shape, F32)
        vacc[...] = jnp.zeros(vacc.shape, F32)

        @pl.when(j == 0)
        def _():
            kcar[...] = jnp.zeros(kcar.shape, F32)
            vcar[...] = jnp.zeros(vcar.shape, F32)
            dsk_ref[...] = jnp.zeros(dsk_ref.shape, F32)

        sink2 = _sink_row(sk_ref)
        dsink = jnp.zeros((1, SWA_W), F32)
        for b in range(4):
            rs = slice(b * BLOCK, (b + 1) * BLOCK)
            ks = slice(b * BLOCK, (b + 2) * BLOCK)
            q, k2, v2 = _heads_to_rows(q_ref, rs), kx[ks, :], vx[ks, :]
            d = _heads_to_rows(do_ref, rs)
            delta = jnp.sum((d * _heads_to_rows(o_ref, rs)).T, axis=0, keepdims=True)
            l2 = jnp.concatenate([lse_ref[hh, :, rs] for hh in range(SWA_GROUP)], axis=1)
            st = lax.dot_general(k2, q, NT, preferred_element_type=F32) * c2
            pt = jnp.exp2(jnp.where(band0 if b == 0 else band, st, -jnp.inf) - l2)
            db = d.astype(MXU_DTYPE)
            dst = (pt * (lax.dot_general(v2, db, NT, preferred_element_type=F32) - delta) * scale).astype(MXU_DTYPE)
            dq = lax.dot_general(dst, k2, TN, preferred_element_type=F32)
            for hh in range(SWA_GROUP):
                dq_ref[rs, hh * LANES:(hh + 1) * LANES] = dq[hh * LANES:(hh + 1) * LANES, :]
            kacc[ks, :] += jnp.dot(dst, q, preferred_element_type=F32)
            vacc[ks, :] += jnp.dot(pt.astype(MXU_DTYPE), db, preferred_element_type=F32)
            dsink = dsink - jnp.exp2(sink2 - l2) * delta
        for hh in range(SWA_GROUP):
            tot = jnp.sum(dsink[:, hh * LANES:(hh + 1) * LANES], axis=1, keepdims=True)
            dsk_ref[0, hh:hh + 1, :] += jnp.broadcast_to(tot, (1, LANES))

        dk_ref[0:3 * BLOCK, :] = kacc[BLOCK:4 * BLOCK, :]
        dk_ref[3 * BLOCK:4 * BLOCK, :] = kacc[4 * BLOCK:5 * BLOCK, :] + kcar[...]
        dv_ref[0:3 * BLOCK, :] = vacc[BLOCK:4 * BLOCK, :].astype(dv_ref.dtype)
        dv_ref[3 * BLOCK:4 * BLOCK, :] = (vacc[4 * BLOCK:5 * BLOCK, :] + vcar[...]).astype(dv_ref.dtype)
        kcar[...] = kacc[0:BLOCK, :]
        vcar[...] = vacc[0:BLOCK, :]

    q, cur, prev, sink, lse_spec = _swa_in_specs(True, nsb)
    return pl.pallas_call(
        body, name="swa_bwd", grid=(SWA_KV_HEADS, nsb),
        in_specs=[q, cur, prev, cur, prev, sink, q, q, lse_spec],
        out_specs=[q, cur, cur, sink],
        out_shape=[jax.ShapeDtypeStruct((s_, SWA_HEADS * LANES), F32), jax.ShapeDtypeStruct((s_, SWA_KV_HEADS * LANES), F32),
                   jax.ShapeDtypeStruct((s_, SWA_KV_HEADS * LANES), MXU_DTYPE),
                   jax.ShapeDtypeStruct((SWA_KV_HEADS, SUBLANES, LANES), F32)],
        scratch_shapes=[pltpu.VMEM((5 * BLOCK, LANES), MXU_DTYPE), pltpu.VMEM((5 * BLOCK, LANES), MXU_DTYPE),
                        pltpu.VMEM((5 * BLOCK, LANES), F32), pltpu.VMEM((5 * BLOCK, LANES), F32),
                        pltpu.VMEM((BLOCK, LANES), F32), pltpu.VMEM((BLOCK, LANES), F32)],
        compiler_params=_cparams(("arbitrary", "arbitrary")),
    )(qa, ka, ka, va, va, sink_b, o32, do, lse)


MLA_T = 512
MLA_FWD_GROUP = 4
MLA_BWD_GROUP = 2


def _mla_specs(s_, t, group):
    w = group * LANES
    qs = pl.BlockSpec((t, w), lambda g, i: (i, g))
    kv = pl.BlockSpec((s_, w), lambda g, i: (0, g))
    row = pl.BlockSpec((group, 1, t), lambda g, i: (g, 0, i))
    return qs, kv, row


def _causal_scores_t(k, q, t, c2, masked):
    st = lax.dot_general(k, q, NT, preferred_element_type=F32) * c2
    if masked:
        kr = lax.broadcasted_iota(jnp.int32, (t, t), 0)
        qc = lax.broadcasted_iota(jnp.int32, (t, t), 1)
        st = jnp.where(kr <= qc, st, -jnp.inf)
    return st


def _mla_fwd(qc, kc, vp):
    s_ = qc.shape[0]
    t = min(MLA_T, s_)
    c2 = MLA_QK ** -0.5 * LOG2E
    grp = MLA_FWD_GROUP

    def body(q_ref, k_ref, v_ref, o32_ref, o16_ref, lse_ref, m_s, acc_s):
        qi = pl.program_id(1)
        m_s[...] = jnp.full(m_s.shape, -jnp.inf, F32)
        acc_s[...] = jnp.zeros(acc_s.shape, F32)
        ones_lane = lax.broadcasted_iota(jnp.int32, (t, LANES), 1) == MLA_V

        def step(ki, masked):
            off = pl.multiple_of(ki * t, t)
            for g in range(grp):
                cs = slice(g * LANES, (g + 1) * LANES)
                st = _causal_scores_t(k_ref[pl.ds(off, t), cs], q_ref[:, cs], t, c2, masked)
                m_old = m_s[g]
                m_new = jnp.maximum(m_old, jnp.max(st, axis=0, keepdims=True))
                alpha = jnp.exp2(m_old - m_new)
                pt = jnp.exp2(st - m_new).astype(MXU_DTYPE)
                v = v_ref[pl.ds(off, t), cs]
                v = jnp.where(ones_lane, jnp.ones((), v.dtype), v)
                acc_s[g] = alpha * acc_s[g] + lax.dot_general(v, pt, TN, preferred_element_type=F32)
                m_s[g] = m_new

        def full_block(ki, carry):
            step(ki, False)
            return carry

        lax.fori_loop(0, qi, full_block, 0)
        step(qi, True)
        for g in range(grp):
            cs = slice(g * LANES, (g + 1) * LANES)
            acc = acc_s[g]
            l = acc[MLA_V:MLA_V + 1, :]
            o = (acc * (1.0 / l)).T
            o32_ref[:, cs] = o
            o16_ref[:, cs] = o.astype(o16_ref.dtype)
            lse_ref[g] = m_s[g] + jnp.log2(l)

    qs, kv, row = _mla_specs(s_, t, grp)
    return pl.pallas_call(
        body, name="mla_fwd", grid=(MLA_HEADS // grp, s_ // t), in_specs=[qs, kv, kv], out_specs=[qs, qs, row],
        out_shape=[jax.ShapeDtypeStruct((s_, MLA_HEADS * LANES), F32), jax.ShapeDtypeStruct((s_, MLA_HEADS * LANES), MXU_DTYPE),
                   jax.ShapeDtypeStruct((MLA_HEADS, 1, s_), F32)],
        scratch_shapes=[pltpu.VMEM((grp, 1, t), F32), pltpu.VMEM((grp, LANES, t), F32)],
        compiler_params=_cparams(("parallel", "arbitrary")),
    )(qc, kc, vp)


def _mla_bwd(qc, kc, vp, dob, lse, delta):
    s_ = qc.shape[0]
    t = min(MLA_T, s_)
    scale = MLA_QK ** -0.5
    c2 = scale * LOG2E
    grp = MLA_BWD_GROUP

    def body(q_ref, do_ref, lse_ref, dl_ref, k_ref, v_ref, dq_ref, dk_ref, dv_ref, dqt_s):
        qi = pl.program_id(1)

        @pl.when(qi == 0)
        def _():
            dk_ref[...] = jnp.zeros(dk_ref.shape, F32)
            dv_ref[...] = jnp.zeros(dv_ref.shape, F32)

        dqt_s[...] = jnp.zeros(dqt_s.shape, F32)

        def step(ki, masked):
            off = pl.multiple_of(ki * t, t)
            for g in range(grp):
                cs = slice(g * LANES, (g + 1) * LANES)
                q, d, k = q_ref[:, cs], do_ref[:, cs], k_ref[pl.ds(off, t), cs]
                pt = jnp.exp2(_causal_scores_t(k, q, t, c2, masked) - lse_ref[g])
                dpt = lax.dot_general(v_ref[pl.ds(off, t), cs], d, NT, preferred_element_type=F32)
                dst = (pt * (dpt - dl_ref[g]) * scale).astype(MXU_DTYPE)
                dv_ref[pl.ds(off, t), cs] += jnp.dot(pt.astype(MXU_DTYPE), d, preferred_element_type=F32)
                dk_ref[pl.ds(off, t), cs] += jnp.dot(dst, q, preferred_element_type=F32)
                dqt_s[g] += lax.dot_general(k, dst, TN, preferred_element_type=F32)

        def full_block(ki, carry):
            step(ki, False)
            return carry

        lax.fori_loop(0, qi, full_block, 0)
        step(qi, True)
        for g in range(grp):
            dq_ref[:, g * LANES:(g + 1) * LANES] = dqt_s[g].T

    qs, kv, row = _mla_specs(s_, t, grp)
    shp = jax.ShapeDtypeStruct((s_, MLA_HEADS * LANES), F32)
    return pl.pallas_call(
        body, name="mla_bwd", grid=(MLA_HEADS // grp, s_ // t), in_specs=[qs, qs, row, row, kv, kv],
        out_specs=[qs, kv, kv], out_shape=[shp, shp, shp], scratch_shapes=[pltpu.VMEM((grp, LANES, t), F32)],
        compiler_params=_cparams(("parallel", "arbitrary")),
    )(qc, dob, lse, delta, kc, vp)


def _pad_heads(w, nh, hd, axis):
    shp = w.shape
    w = w.reshape(shp[:axis] + (nh, hd) + shp[axis + 1:])
    pad = [(0, 0)] * w.ndim
    pad[axis + 1] = (0, LANES - hd)
    w = jnp.pad(w, pad)
    return w.reshape(shp[:axis] + (nh * LANES,) + shp[axis + 1:])


def _unpad_heads(w, nh, hd, axis):
    shp = w.shape
    w = w.reshape(shp[:axis] + (nh, LANES) + shp[axis + 1:])
    w = lax.slice_in_dim(w, 0, hd, axis=axis + 1)
    return w.reshape(shp[:axis] + (nh * hd,) + shp[axis + 1:])


PACK_W = 1024
ROW_TILE = 16
FULL_SHAPE = dict(w_in=(1024, 3488), w_uq=(384, 768), w_ukv=(256, 1024), w_o_swa=(512, 1024), w_o_mla=(512, 1024),
                  w_out=(1024, 1024), w_gate=(1024, 2816), w_up=(1024, 2816), w_down=(2816, 1024))
BIG = tuple(FULL_SHAPE)
ROW_SHARDED = ("w_out", "w_down")
W_IN_COLS = FULL_SHAPE["w_in"][1] // N_DEV
W_IN_ROWS = -(-W_IN_COLS // ROW_TILE) * ROW_TILE
FF_COLS = D_FF // N_DEV
OUT_ROWS = D_MODEL // N_DEV
SMALL_FLAT = (("w_uq", 0, 36), ("w_ukv", 48, 32))
SMALL_USED = 80
MID_BLOCKS = 4
MID_ROWS = MID_BLOCKS * OUT_ROWS
EARLY_ROWS = W_IN_ROWS + MID_ROWS
LATE_ROWS = 3 * FF_COLS
PACK_ROWS = EARLY_ROWS + LATE_ROWS


def _shard_shape(n):
    r, c = FULL_SHAPE[n]
    return (r // N_DEV, c) if n in ROW_SHARDED else (r, c // N_DEV)


def _wire_pack(sh, dtype):
    c = lambda n: sh[n].astype(dtype)
    rows = [jnp.pad(c("w_in").T, ((0, W_IN_ROWS - W_IN_COLS), (0, 0))), c("w_out"),
            _pad_heads(c("w_o_swa").T, SWA_HEADS, HEAD_DIM, 1), _pad_heads(c("w_o_mla").T, MLA_HEADS, MLA_V, 1)]
    for n, _, r in SMALL_FLAT:
        rows.append(jnp.pad(c(n).reshape(r, PACK_W), ((0, -r % ROW_TILE), (0, 0))))
    rows.append(jnp.zeros((OUT_ROWS - SMALL_USED, PACK_W), dtype))
    return jnp.concatenate(rows + [c("w_gate").T, c("w_up").T, c("w_down")], 0)


def _mid_unpack(p):
    out = dict(w_out=p[0:OUT_ROWS], w_o_swa=_unpad_heads(p[OUT_ROWS:2 * OUT_ROWS], SWA_HEADS, HEAD_DIM, 1).T,
               w_o_mla=_unpad_heads(p[2 * OUT_ROWS:3 * OUT_ROWS], MLA_HEADS, MLA_V, 1).T)
    for n, off, r in SMALL_FLAT:
        out[n] = p[3 * OUT_ROWS + off:3 * OUT_ROWS + off + r].reshape(_shard_shape(n))
    return out


def _w_in_row_maps():
    sp = lambda col: (col // W_IN_COLS) * W_IN_ROWS + col % W_IN_COLS
    fwd = np.full((P_W,), -1, np.int64)

    def put(t0, c0, n):
        fwd[t0:t0 + n] = [sp(c) for c in range(c0, c0 + n)]

    put(P_GA, IN_OFF[6], D_MODEL)
    put(P_GB, IN_OFF[7], D_MODEL)
    for h in range(SWA_HEADS):
        put(P_Q + LANES * h, IN_OFF[0] + HEAD_DIM * h, HEAD_DIM)
    put(P_QLAT, IN_OFF[3], Q_LORA)
    put(P_KR + KR_LANE, IN_OFF[5], MLA_ROPE)
    for h in range(SWA_KV_HEADS):
        put(P_K + LANES * h, IN_OFF[1] + HEAD_DIM * h, HEAD_DIM)
        put(P_V + LANES * h, IN_OFF[2] + HEAD_DIM * h, HEAD_DIM)
    put(P_KVLAT, IN_OFF[4], KV_LORA)
    inv = np.full((N_DEV * W_IN_ROWS,), -1, np.int64)
    inv[fwd[fwd >= 0]] = np.nonzero(fwd >= 0)[0]
    return fwd, inv


def _take_rows(src, idx, *, name, tile=2 * LANES):
    n_out, n_src, width = len(idx), src.shape[0], src.shape[1]
    assert n_out % tile == 0 and n_src % tile == 0
    n_tiles = n_out // tile
    blocks = [sorted({int(v) // tile for v in idx[i * tile:(i + 1) * tile] if v >= 0}) for i in range(n_tiles)]
    k_max = max(1, max(len(b) for b in blocks))
    tab = np.zeros((n_tiles, k_max), np.int32)
    sel = np.zeros((n_tiles, k_max, tile, tile), np.float32)
    for i, blks in enumerate(blocks):
        for m, b in enumerate(blks):
            tab[i, m] = b
            for r in range(tile):
                v = int(idx[i * tile + r])
                if v >= 0 and v // tile == b:
                    sel[i, m, r, v % tile] = 1.0

    def body(tab_ref, sel_ref, *refs):
        o_ref = refs[k_max]
        acc = jnp.dot(sel_ref[0, 0], refs[0][...], preferred_element_type=F32)
        for m in range(1, k_max):
            acc = acc + jnp.dot(sel_ref[0, m], refs[m][...], preferred_element_type=F32)
        o_ref[...] = acc.astype(o_ref.dtype)

    def src_spec(m):
        return pl.BlockSpec((tile, width), lambda i, t: (t[i * k_max + m], 0))

    return pl.pallas_call(
        body, name=name,
        grid_spec=pltpu.PrefetchScalarGridSpec(
            num_scalar_prefetch=1, grid=(n_tiles,),
            in_specs=[pl.BlockSpec((1, k_max, tile, tile), lambda i, t: (i, 0, 0, 0))] + [src_spec(m) for m in range(k_max)],
            out_specs=pl.BlockSpec((tile, width), lambda i, t: (i, 0))),
        out_shape=jax.ShapeDtypeStruct((n_out, width), src.dtype),
        compiler_params=_cparams(("parallel",)),
    )(jnp.asarray(tab.reshape(-1)), jnp.asarray(sel, src.dtype), *([src] * k_max))


def _w_in_operand(win_g):
    return _take_rows(win_g.reshape(N_DEV * W_IN_ROWS, PACK_W), _w_in_row_maps()[0], name="w_in_rows")


def _mid_operands(wout_g, woa_g, wob_g, small_g):
    def full(n, off, r):
        a = small_g[:, off:off + r].reshape((N_DEV,) + _shard_shape(n))
        return jnp.moveaxis(a, 0, 1).reshape(FULL_SHAPE[n])

    w = {n: full(n, off, r) for n, off, r in SMALL_FLAT}
    ukv = w["w_ukv"].reshape(KV_LORA, MLA_HEADS, MLA_NOPE + MLA_V)
    return dict(
        wout=wout_g.reshape(D_MODEL, D_MODEL), woa_t=woa_g.reshape(D_MODEL, -1), wob_t=wob_g.reshape(D_MODEL, -1),
        wuq=_pad_heads(w["w_uq"], MLA_HEADS, MLA_QK, 1),
        wuk=_pad_heads(ukv[:, :, :MLA_NOPE].reshape(KV_LORA, -1), MLA_HEADS, MLA_NOPE, 1),
        wuv=_pad_heads(ukv[:, :, MLA_NOPE:].reshape(KV_LORA, -1), MLA_HEADS, MLA_V, 1),
    )


def _mid_grad_pack(g):
    uk = _unpad_heads(g["wukv"][:, :1024], MLA_HEADS, MLA_NOPE, 1).reshape(KV_LORA, MLA_HEADS, MLA_NOPE)
    uv = _unpad_heads(g["wukv"][:, 1024:], MLA_HEADS, MLA_V, 1).reshape(KV_LORA, MLA_HEADS, MLA_V)
    w = dict(w_uq=_unpad_heads(g["wuq"], MLA_HEADS, MLA_QK, 1), w_ukv=jnp.concatenate([uk, uv], 2).reshape(KV_LORA, -1))
    rows = []
    for n, _, r in SMALL_FLAT:
        rr, cc = FULL_SHAPE[n]
        a = jnp.moveaxis(w[n].reshape(rr, N_DEV, cc // N_DEV), 1, 0).reshape(N_DEV, r, PACK_W)
        rows.append(jnp.pad(a, ((0, 0), (0, -r % ROW_TILE), (0, 0))).astype(WIRE_DTYPE))
    rows.append(jnp.zeros((N_DEV, OUT_ROWS - SMALL_USED, PACK_W), WIRE_DTYPE))
    blk = lambda a: a.reshape(N_DEV, OUT_ROWS, PACK_W)
    return [blk(g["wout"]), blk(g["woa_t"]), blk(g["wob_t"]), jnp.concatenate(rows, 1)]


def _w_in_grad_chunks(g_win_t):
    return _take_rows(g_win_t, _w_in_row_maps()[1], name="dw_in_rows").reshape(N_DEV, W_IN_ROWS, PACK_W)


def _local_step(x, tgt, win_t, small, weights, grads):
    s_ = x.shape[0]
    tabs = _rope_tables(s_)
    sink_b = jnp.broadcast_to(small["swa_sinks"].reshape(SWA_KV_HEADS, SWA_GROUP, 1), (SWA_KV_HEADS, SWA_GROUP, LANES))
    sink_b = jnp.pad(sink_b, ((0, 0), (0, SUBLANES - SWA_GROUP), (0, 0)))

    h, qa, ka, va, cq, ckv, kro, p = _proj_in(x, small["mix_norm_g"], win_t, small["q_norm_g"], small["kv_norm_g"], tabs)
    oa32, oa16, lse_a = _swa_fwd(qa, ka, va, sink_b)
    ops = weights.mid(oa16)
    qc, kc, vp = _mla_up(cq, ckv, kro, ops["wuq"], ops["wuk"], ops["wuv"], tabs)
    ob32, ob16, lse_b = _mla_fwd(qc, kc, vp)
    y = _attn_out_gate(oa16, ob16, ops["woa_t"], ops["wob_t"], p)
    x1 = _mm(y, ops["wout"], "nn", name="out_proj", add=x, tm=1024, tn=1024)
    wgu_t, wd = weights.late(x1)
    h2, gu, act = _ffn_in_act(x1, small["ffn_norm_g"], wgu_t)

    dx2, dx2b, dg3, _, tot = _ffn_out_loss(act, wd, x1, small["final_norm_g"].reshape(1, D_MODEL), tgt)
    g = {}
    g_wd = _mm(act, dx2b, "tn", name="dw_down", tm=FF_TILE, tn=1024, tk=2048, out_dtype=WIRE_DTYPE)
    dgu = _d_act_swiglu(dx2b, wd, gu)
    g_wgu = _mm(dgu, h2, "tn", name="dw_ffn_in", tm=FF_TILE, tn=1024, tk=2048, out_dtype=WIRE_DTYPE)
    token = grads.late(g_wgu, g_wd)
    dx1, dx1b, dg2 = _mm_norm_bwd(dgu, wgu_t, x1, small["ffn_norm_g"] + token[0:1, 0:1], dx2, name="d_h2")
    g["wout"] = _mm(y, dx1b, "tn", name="dw_out", tm=1024, tn=1024, tk=2048, out_dtype=WIRE_DTYPE)
    dta, dtb, dgab = _d_y_gate(dx1b, ops["wout"], p, oa16, ob16, ops["woa_t"], ops["wob_t"])
    doa = _mm(dta, ops["woa_t"], "nn", name="d_oa", tm=1024, tn=1024)
    g["woa_t"] = _mm(dta, oa16, "tn", name="dw_o_swa", tm=1024, tn=1024, tk=2048, out_dtype=WIRE_DTYPE)
    g["wob_t"] = _mm(dtb, ob16, "tn", name="dw_o_mla", tm=1024, tn=1024, tk=2048, out_dtype=WIRE_DTYPE)
    dob16, delta_b = _mla_d_out(dtb, ops["wob_t"], ob32)
    dqc, dkc, dvp = _mla_bwd(qc, kc, vp, dob16, lse_b, delta_b)
    dqp, dkv, dkr, dqlat, dkvlat, dgq, dgkv = _mla_up_bwd(
        dqc, dkc, dvp, ops["wuq"], jnp.concatenate([ops["wuk"], ops["wuv"]], 1), p, small["q_norm_g"], small["kv_norm_g"], tabs)
    g["wuq"] = _mm(cq, dqp, "tn", name="dw_uq", tm=Q_LORA, tn=1024, tk=2048)
    g["wukv"] = _mm(ckv, dkv, "tn", name="dw_ukv", tm=KV_LORA, tn=2048, tk=2048)
    token = grads.mid(g)
    dqa, dka, dva, dsk = _swa_bwd(qa, ka, va, sink_b + token[0:1, 0:1], oa32, doa, lse_a)
    dp = _assemble_dp(dgab, dqa, dqlat, dkr, dka, dva, dkvlat, tabs)
    token = grads.last(_mm(dp, h, "tn", name="dw_in", tm=2176, tn=1024, tk=1024, out_dtype=WIRE_DTYPE))
    gx, _, dg1 = _mm_norm_bwd(dp, win_t, x, small["mix_norm_g"], dx1, name="d_h", after=token)

    sm = dict(mix_norm_g=dg1, ffn_norm_g=dg2, final_norm_g=dg3, q_norm_g=dgq, kv_norm_g=dgkv,
              swa_sinks=dsk[:, :SWA_GROUP, 0].reshape(1, SWA_HEADS))
    return tot, gx, sm


MESH = pl.DeviceIdType.MESH
ANY = pl.BlockSpec(memory_space=pl.ANY)


def _position():
    return lax.axis_index("x"), lax.axis_index("y"), lax.axis_index("c")


def _all_gather(block, pieces, shapes, *, name):
    n_out = len(shapes)
    n_rows = sum(p[3] for p in pieces)

    def body(x_ref, *refs):
        outs, (send_sems, recv_sems, local_sem) = refs[:n_out], refs[n_out:]
        x, y, c = _position()
        me, sibling = (x, y, c), (x, y, 1 - c)
        chips = [(1 - x, y), (x, 1 - y), (1 - x, 1 - y)]

        def dst(piece, blk):
            arr, lead, _, _ = piece
            return outs[arr].at[lead(4 * blk[0] + 2 * blk[1] + blk[2])]

        def own(piece):
            return x_ref.at[pl.ds(piece[2], piece[3])]

        def copies(k, blk, to, from_input):
            return [pltpu.make_async_remote_copy(
                src_ref=own(p) if from_input else dst(p, blk), dst_ref=dst(p, blk), send_sem=send_sems.at[k],
                recv_sem=recv_sems.at[k], device_id=to, device_id_type=MESH) for p in pieces]

        gathered_rows = x_ref.at[pl.ds(0, n_rows)]

        def whole_block(k):
            return pltpu.make_async_remote_copy(src_ref=gathered_rows, dst_ref=gathered_rows, send_sem=send_sems.at[k],
                                                recv_sem=recv_sems.at[k], device_id=me, device_id_type=MESH)

        for p in pieces:
            pltpu.make_async_copy(own(p), dst(p, me), local_sem).start()
        for cp in copies(0, me, sibling, True):
            cp.start()
        for j, chip in enumerate(chips):
            for cp in copies(1 + j, me, (*chip, c), True):
                cp.start()
        for j, chip in enumerate(chips):
            whole_block(1 + j).wait_recv()
            for cp in copies(4 + j, (*chip, c), sibling, False):
                cp.start()
        whole_block(0).wait_recv()
        for j in range(3):
            whole_block(4 + j).wait_recv()
        for k in range(7):
            whole_block(k).wait_send()
        pltpu.make_async_copy(gathered_rows, gathered_rows, local_sem).wait()

    return pl.pallas_call(
        body, name=name, out_shape=[jax.ShapeDtypeStruct(s, block.dtype) for s in shapes], in_specs=[ANY],
        out_specs=[ANY] * n_out,
        scratch_shapes=[pltpu.SemaphoreType.DMA((7,)), pltpu.SemaphoreType.DMA((7,)), pltpu.SemaphoreType.DMA],
    )(block)


HBM = pl.BlockSpec(memory_space=pltpu.HBM)
SEM = pl.BlockSpec(memory_space=pltpu.SEMAPHORE)
TILE_DEVS = FF_TILE // FF_COLS
GU_SHAPE = (2, 2, TILE_DEVS, FF_COLS, PACK_W)


def _gate_slab(d):
    return (d // TILE_DEVS, 0, d % TILE_DEVS)


def _up_slab(d):
    return (d // TILE_DEVS, 1, d % TILE_DEVS)
D_SHAPE = (N_DEV, FF_COLS, PACK_W)
LAND_SHAPE = (N_DEV, LATE_ROWS, PACK_W)


def _split_params():
    return pltpu.CompilerParams(has_side_effects=pltpu.SideEffectType.DATAFLOW_SIDE_EFFECTING)


def _peer(x, y, c, k):
    return ((1 - x) if k & 4 else x, (1 - y) if k & 2 else y, (1 - c) if k & 1 else c)


def _empty_hbm(shape, dtype):
    return pltpu.with_memory_space_constraint(lax.empty(shape, dtype), pltpu.HBM)


def _wait_all(rows, send_sems, recv_sems, me):
    for k in range(N_DEV - 1):
        cp = pltpu.make_async_remote_copy(src_ref=rows, dst_ref=rows, send_sem=send_sems.at[k], recv_sem=recv_sems.at[k],
                                          device_id=me, device_id_type=MESH)
        cp.wait_send()
        cp.wait_recv()


def _token_shape():
    return jax.ShapeDtypeStruct((SUBLANES, LANES), F32)


def _gather_start(pack, row0, pieces, shapes, *, name):
    n = len(shapes)

    def body(*refs):
        p_ref, bufs, send_sems, recv_sems, token = refs[0], refs[1:1 + n], refs[1 + n], refs[2 + n], refs[-1]
        x, y, c = _position()
        me = 4 * x + 2 * y + c
        for k in range(1, N_DEV):
            off = row0
            for buf, lead, rows in pieces:
                pltpu.make_async_remote_copy(
                    src_ref=p_ref.at[pl.ds(off, rows)], dst_ref=bufs[buf].at[lead(me)], send_sem=send_sems.at[k - 1],
                    recv_sem=recv_sems.at[k - 1], device_id=_peer(x, y, c, k), device_id_type=MESH).start()
                off += rows
        token[...] = jnp.zeros_like(token)

    sems, dt = pltpu.SemaphoreType.DMA((N_DEV - 1,)), pack.dtype
    return pl.pallas_call(
        body, name=name,
        out_shape=(sems, sems, pltpu.HBM(pack.shape, dt)) + tuple(pltpu.HBM(s, dt) for s in shapes) + (_token_shape(),),
        in_specs=(HBM,) * (1 + n), out_specs=(SEM, SEM) + (HBM,) * (1 + n) + (pl.BlockSpec(memory_space=pltpu.VMEM),),
        input_output_aliases={i: 2 + i for i in range(1 + n)}, compiler_params=_split_params(),
    )(pltpu.with_memory_space_constraint(pack, pltpu.HBM), *[_empty_hbm(s, dt) for s in shapes])


def _gather_wait(started, row0, n_rows, after, *, name):
    send_sems, recv_sems, pack, *bufs = started[:-1]
    n = len(bufs)

    def body(*refs):
        _wait_all(refs[0].at[pl.ds(row0, n_rows)], refs[1 + n], refs[2 + n], _position())

    outs = pl.pallas_call(
        body, name=name, out_shape=tuple(pltpu.HBM(a.shape, a.dtype) for a in (pack, *bufs)),
        in_specs=(HBM,) * (1 + n) + (SEM, SEM, ANY), out_specs=(HBM,) * (1 + n),
        input_output_aliases={i: i for i in range(1 + n)}, compiler_params=_split_params(),
    )(pack, *bufs, send_sems, recv_sems, after)
    return outs[0], outs[1:]


def _scatter_start(srcs, pieces, *, name):
    n = len(srcs)
    land_shape = (N_DEV, sum(p[2] for p in pieces), PACK_W)

    def body(*refs):
        src_refs, land_ref, send_sems, recv_sems, token = refs[:n], refs[n], refs[n + 1], refs[n + 2], refs[-1]
        x, y, c = _position()
        me = 4 * x + 2 * y + c
        for k in range(1, N_DEV):
            px, py, pc = _peer(x, y, c, k)
            off = 0
            for si, lead, rows in pieces:
                pltpu.make_async_remote_copy(
                    src_ref=src_refs[si].at[lead(4 * px + 2 * py + pc)], dst_ref=land_ref.at[me, pl.ds(off, rows)],
                    send_sem=send_sems.at[k - 1], recv_sem=recv_sems.at[k - 1], device_id=(px, py, pc),
                    device_id_type=MESH).start()
                off += rows
        token[...] = jnp.zeros_like(token)

    sems, dt = pltpu.SemaphoreType.DMA((N_DEV - 1,)), srcs[0].dtype
    return pl.pallas_call(
        body, name=name,
        out_shape=(sems, sems) + tuple(pltpu.HBM(a.shape, dt) for a in srcs) + (pltpu.HBM(land_shape, dt), _token_shape()),
        in_specs=(HBM,) * (n + 1), out_specs=(SEM, SEM) + (HBM,) * (n + 1) + (pl.BlockSpec(memory_space=pltpu.VMEM),),
        input_output_aliases={i: 2 + i for i in range(n + 1)}, compiler_params=_split_params(),
    )(*[pltpu.with_memory_space_constraint(a, pltpu.HBM) for a in srcs], _empty_hbm(land_shape, dt))


def _scatter_wait(started, after, *, name):
    send_sems, recv_sems, *bufs = started[:-1]
    n = len(bufs)

    def body(*refs):
        _wait_all(refs[n - 1].at[0], refs[n], refs[n + 1], _position())

    return pl.pallas_call(
        body, name=name, out_shape=tuple(pltpu.HBM(a.shape, a.dtype) for a in bufs),
        in_specs=(HBM,) * n + (SEM, SEM, ANY), out_specs=(HBM,) * n, input_output_aliases={i: i for i in range(n)},
        compiler_params=_split_params(),
    )(*bufs, send_sems, recv_sems, after)


def _peer_sum(own, own_lead, land, block, rows, idx, *, name):
    owns = list(own) if isinstance(own, (list, tuple)) else [own]
    n, lead_rank = len(owns), owns[0].ndim - 2

    def body(idx_ref, *refs):
        own_refs, land_refs, o_ref = refs[:n], refs[n:n + N_DEV - 1], refs[n + N_DEV - 1]
        for j in range(n):
            rs_ = slice(j * rows, (j + 1) * rows)
            acc = own_refs[j][(0,) * lead_rank].astype(F32)
            for k in range(N_DEV - 1):
                acc = acc + land_refs[k][0, rs_].astype(F32)
            o_ref[rs_] = acc

    own_spec = pl.BlockSpec((1,) * lead_rank + (rows, PACK_W), lambda i, t: own_lead(t[0]) + (0, 0))

    def land_spec(k):
        return pl.BlockSpec((1, n * rows, PACK_W), lambda i, t: (t[k + 1], block, 0))

    return pl.pallas_call(
        body, name=name,
        grid_spec=pltpu.PrefetchScalarGridSpec(
            num_scalar_prefetch=1, grid=(1,), in_specs=[own_spec] * n + [land_spec(k) for k in range(N_DEV - 1)],
            out_specs=pl.BlockSpec((n * rows, PACK_W), lambda i, t: (0, 0))),
        out_shape=jax.ShapeDtypeStruct((n * rows, PACK_W), F32), compiler_params=_cparams(("arbitrary",)),
    )(idx, *owns, *([land] * (N_DEV - 1)))


def _sum_adamw(own, own_lead, land, block, rows, idx, w, m, v, *, name):
    lead_rank, r = own.ndim - 2, w.shape[1]

    def body(idx_ref, own_ref, *refs):
        land_refs, (w_ref, m_ref, v_ref), outs = refs[:N_DEV - 1], refs[N_DEV - 1:N_DEV + 2], refs[N_DEV + 2:]
        g = own_ref[(0,) * lead_rank + (slice(0, r),)].astype(F32)
        for k in range(N_DEV - 1):
            g = g + land_refs[k][0, 0:r].astype(F32)
        for o_ref, val in zip(outs, (g,) + tuple(_adamw(w_ref[0], g, m_ref[0], v_ref[0]))):
            o_ref[0] = val

    own_spec = pl.BlockSpec((1,) * lead_rank + (rows, PACK_W), lambda i, t: own_lead(t[0]) + (0, 0))
    shard = pl.BlockSpec((1, r, PACK_W), lambda i, t: (0, 0, 0))

    def land_spec(k):
        return pl.BlockSpec((1, rows, PACK_W), lambda i, t: (t[k + 1], block, 0))

    return pl.pallas_call(
        body, name=name,
        grid_spec=pltpu.PrefetchScalarGridSpec(
            num_scalar_prefetch=1, grid=(1,),
            in_specs=[own_spec] + [land_spec(k) for k in range(N_DEV - 1)] + [shard] * 3, out_specs=[shard] * 4),
        out_shape=[jax.ShapeDtypeStruct((1, r, PACK_W), F32)] * 4, compiler_params=_cparams(("arbitrary",)),
    )(idx, own, *([land] * (N_DEV - 1)), w, m, v)


def _adamw(w, g, m, v):
    m = ADAM_B1 * m + (1.0 - ADAM_B1) * g
    v = ADAM_B2 * v + (1.0 - ADAM_B2) * (g * g)
    m_hat = m / (1.0 - ADAM_B1 ** ADAM_STEP)
    v_hat = v / (1.0 - ADAM_B2 ** ADAM_STEP)
    delta = -ADAM_LR * (m_hat / (jnp.sqrt(v_hat) + ADAM_EPS) + ADAM_WD * w)
    return delta, m, v


def _adamw_call(w, g, m, v, *, name, max_rows=256):
    _, r, c_ = w.shape
    tr = max_rows if r > max_rows and r % max_rows == 0 else r

    def body(w_ref, g_ref, m_ref, v_ref, d_ref, mo_ref, vo_ref):
        d, mn, vn = _adamw(w_ref[0], g_ref[...], m_ref[0], v_ref[0])
        d_ref[0] = d
        mo_ref[0] = mn
        vo_ref[0] = vn

    row3 = pl.BlockSpec((1, tr, c_), lambda i: (0, i, 0))
    shp = jax.ShapeDtypeStruct((1, r, c_), F32)
    return pl.pallas_call(
        body, name=name, grid=(r // tr,), in_specs=[row3, pl.BlockSpec((tr, c_), lambda i: (i, 0)), row3, row3],
        out_specs=[row3] * 3, out_shape=[shp] * 3, compiler_params=_cparams(("parallel",)),
    )(w, g, m, v)


SMALL = ("mix_norm_g", "ffn_norm_g", "final_norm_g", "q_norm_g", "kv_norm_g", "swa_sinks")
SMALL_W = dict(mix_norm_g=1024, ffn_norm_g=1024, final_norm_g=1024, q_norm_g=Q_LORA, kv_norm_g=KV_LORA, swa_sinks=SWA_HEADS)


def _small_adamw(parts, w, m, v):
    ns = len(SMALL)

    def body(p_ref, *refs):
        ins, outs = refs[:3 * ns], refs[3 * ns:]
        tot = p_ref[0]
        for dev in range(1, N_DEV):
            tot = tot + p_ref[dev]
        for k, n in enumerate(SMALL):
            g = jnp.sum(tot[k * SUBLANES:(k + 1) * SUBLANES, :SMALL_W[n]], axis=0, keepdims=True)
            res = _adamw(ins[k][...], g, ins[ns + k][...], ins[2 * ns + k][...])
            for j, r in enumerate((g,) + tuple(res)):
                outs[j * ns + k][...] = r
        outs[4 * ns][...] = jnp.sum(tot[ns * SUBLANES:(ns + 1) * SUBLANES, 0:1], axis=0, keepdims=True)

    shapes = [jax.ShapeDtypeStruct((1, SMALL_W[n]), F32) for n in SMALL]
    vm = pl.BlockSpec(memory_space=pltpu.VMEM)
    out = pl.pallas_call(
        body, name="small_adamw", in_specs=[vm] * (1 + 3 * ns), out_specs=[vm] * (4 * ns + 1),
        out_shape=shapes * 4 + [jax.ShapeDtypeStruct((1, 1), F32)],
    )(parts, *[d[n] for d in (w, m, v) for n in SMALL])
    return [dict(zip(SMALL, out[j * ns:(j + 1) * ns])) for j in range(4)] + [out[4 * ns]]


def _small_pack(d, rows_each):
    parts = [jnp.pad(d[n].astype(F32), ((0, 0), (0, PACK_W - SMALL_W[n]))) for n in SMALL]
    out = jnp.concatenate(parts, 0)
    pad = -out.shape[0] % SUBLANES
    return jnp.pad(out, ((0, pad), (0, 0)))


def kernel(x, mix_norm_g, w_in, swa_sinks, q_norm_g, w_uq, kv_norm_g, w_ukv, w_o_swa, w_o_mla, w_out, ffn_norm_g, w_gate, w_up, w_down, final_norm_g, loss_target, m_mix_norm_g, m_w_in, m_swa_sinks, m_q_norm_g, m_w_uq, m_kv_norm_g, m_w_ukv, m_w_o_swa, m_w_o_mla, m_w_out, m_ffn_norm_g, m_w_gate, m_w_up, m_w_down, m_final_norm_g, v_mix_norm_g, v_w_in, v_swa_sinks, v_q_norm_g, v_w_uq, v_kv_norm_g, v_w_ukv, v_w_o_swa, v_w_o_mla, v_w_out, v_ffn_norm_g, v_w_gate, v_w_up, v_w_down, v_final_norm_g):
    big_w = dict(w_in=w_in[0], w_uq=w_uq[0], w_ukv=w_ukv[0], w_o_swa=w_o_swa[0], w_o_mla=w_o_mla[0], w_out=w_out[0],
                 w_gate=w_gate[0], w_up=w_up[0], w_down=w_down[0])
    big_w3 = dict(w_in=w_in, w_uq=w_uq, w_ukv=w_ukv, w_o_swa=w_o_swa, w_o_mla=w_o_mla, w_out=w_out, w_gate=w_gate, w_up=w_up,
                  w_down=w_down)
    big_m = dict(w_in=m_w_in, w_uq=m_w_uq, w_ukv=m_w_ukv, w_o_swa=m_w_o_swa, w_o_mla=m_w_o_mla, w_out=m_w_out,
                 w_gate=m_w_gate, w_up=m_w_up, w_down=m_w_down)
    big_v = dict(w_in=v_w_in, w_uq=v_w_uq, w_ukv=v_w_ukv, w_o_swa=v_w_o_swa, w_o_mla=v_w_o_mla, w_out=v_w_out,
                 w_gate=v_w_gate, w_up=v_w_up, w_down=v_w_down)
    small_w = dict(mix_norm_g=mix_norm_g, ffn_norm_g=ffn_norm_g, final_norm_g=final_norm_g.reshape(1, D_MODEL),
                   q_norm_g=q_norm_g, kv_norm_g=kv_norm_g, swa_sinks=swa_sinks)
    small_m = dict(mix_norm_g=m_mix_norm_g, ffn_norm_g=m_ffn_norm_g, final_norm_g=m_final_norm_g.reshape(1, D_MODEL),
                   q_norm_g=m_q_norm_g, kv_norm_g=m_kv_norm_g, swa_sinks=m_swa_sinks)
    small_v = dict(mix_norm_g=v_mix_norm_g, ffn_norm_g=v_ffn_norm_g, final_norm_g=v_final_norm_g.reshape(1, D_MODEL),
                   q_norm_g=v_q_norm_g, kv_norm_g=v_kv_norm_g, swa_sinks=v_swa_sinks)

    px, py, pc = _position()
    me = 4 * px + 2 * py + pc
    idx = jnp.stack([me] + [4 * qx + 2 * qy + qc for qx, qy, qc in (_peer(px, py, pc, k) for k in range(1, N_DEV))])
    idx = idx.astype(jnp.int32)

    dev = lambda d: (d,)
    pack = _wire_pack(big_w, WIRE_DTYPE)
    win_g, = _all_gather(pack, ((0, dev, 0, W_IN_ROWS),), ((N_DEV, W_IN_ROWS, PACK_W),), name="ag_early")
    mid_pieces = tuple((b, dev, OUT_ROWS) for b in range(MID_BLOCKS))
    ag_mid = _gather_start(pack, W_IN_ROWS, mid_pieces, ((N_DEV, OUT_ROWS, PACK_W),) * MID_BLOCKS, name="ag_mid_start")
    ag = {}

    def own_rows(r0, r1, shape):
        return pack[r0:r1].reshape(shape)

    def mid_weights(after):
        pack_mid, blocks = _gather_wait(ag_mid, W_IN_ROWS, MID_ROWS, after, name="ag_mid_wait")
        ag["late"] = _gather_start(pack_mid, EARLY_ROWS, ((0, _gate_slab, FF_COLS), (0, _up_slab, FF_COLS), (1, dev, FF_COLS)),
                                   (GU_SHAPE, D_SHAPE), name="ag_late_start")
        row0 = lambda b: W_IN_ROWS + b * OUT_ROWS
        ops = _mid_operands(*[lax.dynamic_update_slice(blk, own_rows(row0(b), row0(b + 1), (1, OUT_ROWS, PACK_W)), (me, 0, 0))
                              for b, blk in enumerate(blocks)])
        ops["wuq"] = ops["wuq"] + ag["late"][-1][0:1, 0:1].astype(ops["wuq"].dtype)
        return ops

    def late_weights(after):
        _, (gu, d) = _gather_wait(ag["late"], EARLY_ROWS, LATE_ROWS, after, name="ag_late_wait")
        slab = (1, 1, 1, FF_COLS, PACK_W)
        gu = lax.dynamic_update_slice(gu, own_rows(EARLY_ROWS, EARLY_ROWS + FF_COLS, slab), _gate_slab(me) + (0, 0))
        gu = lax.dynamic_update_slice(gu, own_rows(EARLY_ROWS + FF_COLS, EARLY_ROWS + 2 * FF_COLS, slab), _up_slab(me) + (0, 0))
        d = lax.dynamic_update_slice(d, own_rows(EARLY_ROWS + 2 * FF_COLS, PACK_ROWS, (1, FF_COLS, PACK_W)), (me, 0, 0))
        return gu.reshape(2 * D_FF, D_MODEL), d.reshape(D_FF, D_MODEL)

    rs = {}

    def late_grads(g_gu, g_d):
        rs["late"] = _scatter_start([g_gu.reshape(GU_SHAPE), g_d.reshape(D_SHAPE)],
                                    ((0, _gate_slab, FF_COLS), (0, _up_slab, FF_COLS), (1, dev, FF_COLS)),
                                    name="rs_late_start")
        return rs["late"][-1]

    def mid_grads(g):
        rs["mid"] = _scatter_start(_mid_grad_pack(g), mid_pieces, name="rs_mid_start")
        return rs["mid"][-1]

    def last_grads(g_win_t):
        rs["last"] = _scatter_start([_w_in_grad_chunks(g_win_t)], ((0, dev, W_IN_ROWS),), name="rs_last_start")
        return rs["last"][-1]

    first_w = dict(small_w, mix_norm_g=mix_norm_g + ag_mid[-1][0:1, 0:1])
    loss_tot, gx, g_small = _local_step(
        x[0], loss_target[0], _w_in_operand(win_g), first_w, types.SimpleNamespace(mid=mid_weights, late=late_weights),
        types.SimpleNamespace(late=late_grads, mid=mid_grads, last=last_grads))

    loss_rows = jnp.pad(loss_tot[0:1, 0:1], ((0, SUBLANES - 1), (0, PACK_W - 1)))
    small_rows = jnp.concatenate([_small_pack(g_small_rows(g_small), SUBLANES), loss_rows], 0)
    n_small = small_rows.shape[0]
    ag_small = _gather_start(small_rows, 0, ((0, dev, n_small),), ((N_DEV, n_small, PACK_W),), name="ag_small_start")

    g_gu, g_d, land_late = _scatter_wait(rs["late"], ag_small[-1], name="rs_late_wait")
    *g_mid, land_mid = _scatter_wait(rs["mid"], ag_small[-1], name="rs_mid_wait")
    g_win, land_last = _scatter_wait(rs["last"], ag_small[-1], name="rs_last_wait")
    swap = lambda a: jnp.swapaxes(a, 1, 2)
    same = lambda a: a
    chunks = dict(w_gate=(swap, g_gu, _gate_slab, land_late, 0, FF_COLS), w_up=(swap, g_gu, _up_slab, land_late, 1, FF_COLS),
                  w_down=(same, g_d, dev, land_late, 2, FF_COLS), w_in=(swap, g_win, dev, land_last, 0, W_IN_ROWS))
    gw, dw, mw, vw = {}, {}, {}, {}
    for n, (view, own, lead, land, blk, rows) in chunks.items():
        res = _sum_adamw(own, lead, land, blk, rows, idx, view(big_w3[n]), view(big_m[n]), view(big_v[n]), name="adamw_" + n)
        gw[n], dw[n], mw[n], vw[n] = (view(r) for r in res)
    g_nat = _mid_unpack(_peer_sum(g_mid, dev, land_mid, 0, OUT_ROWS, idx, name="rs_sum_mid"))
    for n, g in g_nat.items():
        gw[n] = g[None]
        dw[n], mw[n], vw[n] = _adamw_call(big_w3[n], g, big_m[n], big_v[n], name="adamw_" + n)

    own_small, (parts,) = _gather_wait(ag_small, 0, n_small, vw[n], name="ag_small_wait")
    parts = lax.dynamic_update_slice(parts, own_small[None], (me, 0, 0))
    gs, ds, ms, vs, loss = _small_adamw(parts, small_w, small_m, small_v)
    loss = loss[0, 0]
    for d in (gs, ds, ms, vs):
        d["final_norm_g"] = d["final_norm_g"].reshape(D_MODEL)

    order = ("mix_norm_g", "w_in", "swa_sinks", "q_norm_g", "w_uq", "kv_norm_g", "w_ukv", "w_o_swa", "w_o_mla", "w_out",
             "ffn_norm_g", "w_gate", "w_up", "w_down", "final_norm_g")

    def leaves(big, small):
        return [big[n] if n in big else small[n] for n in order]

    return (loss, gx[None], *leaves(gw, gs), *leaves(dw, ds), *leaves(mw, ms), *leaves(vw, vs))


def g_small_rows(g_small):
    out = dict(g_small)
    out["swa_sinks"] = jnp.pad(g_small["swa_sinks"], ((0, SUBLANES - 1), (0, 0)))
    return out
```

```python
import types

import numpy as np
import jax
import jax.numpy as jnp
from jax import lax
from jax.experimental import pallas as pl
from jax.experimental.pallas import tpu as pltpu

F32 = jnp.float32
MXU_DTYPE = jnp.bfloat16
WIRE_DTYPE = jnp.bfloat16

D_MODEL = 1024
EPS = 1e-6
ROPE_THETA = 10000.0
BLOCK = 128
HEAD_DIM = 64
SWA_HEADS = 8
SWA_KV_HEADS = 2
SWA_GROUP = SWA_HEADS // SWA_KV_HEADS
MLA_HEADS = 8
MLA_NOPE = 64
MLA_ROPE = 32
MLA_V = 64
MLA_QK = MLA_NOPE + MLA_ROPE
Q_LORA = 384
KV_LORA = 256
D_FF = 2816
IN_SIZES = (512, 128, 128, Q_LORA, KV_LORA, MLA_ROPE, D_MODEL, D_MODEL)
IN_OFF = tuple(int(v) for v in np.cumsum((0,) + IN_SIZES))
ADAM_LR, ADAM_B1, ADAM_B2, ADAM_EPS, ADAM_WD, ADAM_STEP = 0.001, 0.9, 0.999, 1e-08, 0.01, 10

LANES = 128
SUBLANES = 8
VMEM_LIMIT = 48 * 1024 * 1024
N_DEV = 8

P_GA, P_GB, P_Q, P_QLAT, P_KR, P_K, P_V, P_KVLAT, P_W = 0, 1024, 2048, 3072, 3456, 3584, 3840, 4096, 4352
KR_LANE = 64

LOG2E = 1.4426950408889634

NT = (((1,), (1,)), ((), ()))
NN = (((1,), (0,)), ((), ()))
TN = (((0,), (0,)), ((), ()))


def _cparams(sem):
    return pltpu.CompilerParams(dimension_semantics=sem, vmem_limit_bytes=VMEM_LIMIT)


def _mm(a, b, mode, *, name, out_dtype=F32, add=None, tm=512, tn=512, tk=None):
    if mode == "nn":
        (M, K), (K2, N) = a.shape, b.shape
    elif mode == "nt":
        (M, K), (N, K2) = a.shape, b.shape
    else:
        (K, M), (K2, N) = a.shape, b.shape
    assert K == K2, (a.shape, b.shape, mode)
    tm, tn, tk = min(tm, M), min(tn, N), K if tk is None else min(tk, K)
    assert M % tm == 0 and N % tn == 0 and K % tk == 0, (M, N, K, tm, tn, tk)
    nk = K // tk
    dn = {"nn": NN, "nt": NT, "tn": TN}[mode]
    if mode == "tn":
        a_spec = pl.BlockSpec((tk, tm), lambda i, j, k: (k, i))
    else:
        a_spec = pl.BlockSpec((tm, tk), lambda i, j, k: (i, k))
    once = dict(pipeline_mode=pl.Buffered(1)) if (nk == 1 and tn == N) else {}
    if mode == "nt":
        b_spec = pl.BlockSpec((tn, tk), lambda i, j, k: (j, k), **once)
    else:
        b_spec = pl.BlockSpec((tk, tn), lambda i, j, k: (k, j), **once)
    o_spec = pl.BlockSpec((tm, tn), lambda i, j, k: (i, j))
    has_add = add is not None

    def body(*refs):
        a_ref, b_ref = refs[0], refs[1]
        add_ref = refs[2] if has_add else None
        o_ref = refs[2 + has_add]
        p = lax.dot_general(a_ref[...], b_ref[...], dn, preferred_element_type=F32)

        def finish(acc):
            if has_add:
                acc = acc + add_ref[...]
            o_ref[...] = acc.astype(o_ref.dtype)

        if nk == 1:
            finish(p)
        else:
            acc_ref = refs[-1]
            k = pl.program_id(2)

            @pl.when(k == 0)
            def _():
                acc_ref[...] = p

            @pl.when((k > 0) & (k < nk - 1))
            def _():
                acc_ref[...] += p

            @pl.when(k == nk - 1)
            def _():
                finish(acc_ref[...] + p)

    ins = [a, b] + ([add] if has_add else [])
    return pl.pallas_call(
        body, name=name, grid=(M // tm, N // tn, nk), in_specs=[a_spec, b_spec] + ([o_spec] if has_add else []), out_specs=o_spec,
        out_shape=jax.ShapeDtypeStruct((M, N), out_dtype),
        scratch_shapes=[pltpu.VMEM((tm, tn), F32)] if nk > 1 else [],
        compiler_params=_cparams(("parallel", "parallel", "arbitrary")),
    )(*ins)


def _rows(ts, w, cb=0):
    return pl.BlockSpec((ts, w), lambda i: (i, cb))


def _const(r, w):
    return pl.BlockSpec((r, w), lambda i: (0, 0))


def _sublane_sum(v):
    ts, c = v.shape
    return jnp.sum(v.reshape(ts // SUBLANES, SUBLANES, c), axis=0)


def _sigmoid(v):
    return 1.0 / (1.0 + jnp.exp(-v))


def _rope(v, cos, s_up, s_dn, up, dn):
    return v * cos + pltpu.roll(v, up, 1) * s_up + pltpu.roll(v, dn, 1) * s_dn


def _rope_t(dv, cos, s_up, s_dn, up, dn):
    return dv * cos + pltpu.roll(dv * s_up, dn, 1) + pltpu.roll(dv * s_dn, up, 1)


def _rope_tables(seq):
    pos = np.arange(seq, dtype=np.float32)[:, None]

    def base(dim):
        inv = np.float32(ROPE_THETA) ** (-np.arange(0, dim, 2, dtype=np.float32) / np.float32(dim))
        ang = (pos * inv.astype(np.float32)[None, :]).astype(np.float32)
        return np.cos(ang).astype(np.float32), np.sin(ang).astype(np.float32)

    z = lambda n: np.zeros((seq, n), np.float32)
    ca, sa = base(HEAD_DIM)
    a_cos = np.concatenate([ca, ca, z(64)], 1)
    a_up = np.concatenate([-sa, z(96)], 1)
    a_dn = np.concatenate([z(32), sa, z(64)], 1)
    cb, sb = base(MLA_ROPE)
    one = np.ones((seq, 64), np.float32)
    q_cos = np.concatenate([one, cb, cb, z(32)], 1)
    k_cos = np.concatenate([z(64), cb, cb, z(32)], 1)
    b_up = np.concatenate([z(64), -sb, z(48)], 1)
    b_dn = np.concatenate([z(80), sb, z(32)], 1)
    return tuple(jnp.asarray(t) for t in (a_cos, a_up, a_dn, q_cos, k_cos, b_up, b_dn))


def _rms(v, g):
    return v * lax.rsqrt(jnp.mean(v * v, axis=-1, keepdims=True) + EPS) * g


def _rms_bwd(v, g, d):
    r = lax.rsqrt(jnp.mean(v * v, axis=-1, keepdims=True) + EPS)
    xh = v * r
    dxh = d * g
    return r * (dxh - xh * jnp.mean(dxh * xh, axis=-1, keepdims=True)), d * xh


F_GA, F_GB, F_KVLAT, F_QLAT, F_W = 0, 1024, 2048, 2304, 2688


def _proj_in(x, g, w_t, gq, gkv, tabs, *, tm=512):
    s_, c = x.shape
    a_cos, a_up, a_dn, _, k_cos, b_up, b_dn = tabs

    def body(x_ref, g_ref, w_ref, gq_ref, gkv_ref, ac, au, ad, kc, bu, bd,
             h_ref, qa_ref, ka_ref, va_ref, cq_ref, ckv_ref, kro_ref, pf_ref):
        h = _rms(x_ref[...], g_ref[...]).astype(h_ref.dtype)
        h_ref[...] = h
        mm = lambda a, b: lax.dot_general(h, w_ref[a:b, :], NT, preferred_element_type=F32)
        pf_ref[:, F_GA:F_KVLAT] = mm(P_GA, P_Q)
        c_, u_, d_ = ac[...], au[...], ad[...]
        q = mm(P_Q, P_QLAT)
        for hd in range(SWA_HEADS):
            sl = slice(hd * LANES, (hd + 1) * LANES)
            qa_ref[:, sl] = _rope(q[:, sl], c_, u_, d_, 96, 32).astype(qa_ref.dtype)
        kv = mm(P_KR, P_KVLAT)
        kro_ref[...] = _rope(kv[:, :LANES], kc[...], bu[...], bd[...], 112, 16)
        for hd in range(SWA_KV_HEADS):
            sl = slice((1 + hd) * LANES, (2 + hd) * LANES)
            ka_ref[:, hd * LANES:(hd + 1) * LANES] = _rope(kv[:, sl], c_, u_, d_, 96, 32).astype(ka_ref.dtype)
        va_ref[...] = kv[:, P_V - P_KR:].astype(va_ref.dtype)
        for a, b, f0, gref, dst in ((P_QLAT, P_KR, F_QLAT, gq_ref, cq_ref), (P_KVLAT, P_W, F_KVLAT, gkv_ref, ckv_ref)):
            v = mm(a, b)
            pf_ref[:, f0:f0 + b - a] = v
            r = lax.rsqrt(jnp.mean(v * v, axis=-1, keepdims=True) + EPS)
            dst[...] = (v * r * gref[...]).astype(dst.dtype)

    tab = _rows(tm, LANES)
    widths = (c, SWA_HEADS * LANES, SWA_KV_HEADS * LANES, SWA_KV_HEADS * LANES, Q_LORA, KV_LORA)
    return pl.pallas_call(
        body, name="proj_in", grid=(s_ // tm,),
        in_specs=[_rows(tm, c), _const(1, c), pl.BlockSpec((P_W, c), lambda i: (0, 0), pipeline_mode=pl.Buffered(1)),
                  _const(1, Q_LORA), _const(1, KV_LORA), tab, tab, tab, tab, tab, tab],
        out_specs=[_rows(tm, w) for w in widths] + [tab, _rows(tm, F_W)],
        out_shape=[jax.ShapeDtypeStruct((s_, w), MXU_DTYPE) for w in widths]
        + [jax.ShapeDtypeStruct((s_, LANES), F32), jax.ShapeDtypeStruct((s_, F_W), F32)],
        compiler_params=_cparams(("parallel",)),
    )(x, g, w_t, gq, gkv, a_cos, a_up, a_dn, k_cos, b_up, b_dn)


def _mm_norm_bwd(a, b, x, g, res, *, name, after=None, tm=512):
    s_, kk = a.shape
    c = b.shape[1]
    has_after = after is not None
    n, slots = s_ // tm, 3

    def body(*refs):
        a_hbm, b_ref, x_ref, g_ref, res_ref = refs[:5]
        dx_ref, dxb_ref, dg_ref, ring, sems = refs[5 + has_after:]
        s = pl.program_id(0)

        def fetch(step):
            return pltpu.make_async_copy(a_hbm.at[pl.ds(pl.multiple_of(step * tm, tm), tm)], ring.at[step % slots],
                                         sems.at[step % slots])

        @pl.when(s == 0)
        def _():
            for ahead in range(min(slots - 1, n)):
                fetch(ahead).start()

        @pl.when(s + slots - 1 < n)
        def _():
            fetch(s + slots - 1).start()

        fetch(s).wait()
        d = jnp.dot(ring[s % slots], b_ref[...], preferred_element_type=F32)
        dx, gg = _rms_bwd(x_ref[...], g_ref[...], d)
        dx = dx + res_ref[...]
        dx_ref[...] = dx
        dxb_ref[...] = dx.astype(dxb_ref.dtype)

        @pl.when(pl.program_id(0) == 0)
        def _():
            dg_ref[...] = jnp.zeros(dg_ref.shape, F32)

        dg_ref[...] += _sublane_sum(gg)

    row = _rows(tm, c)
    any_ = pl.BlockSpec(memory_space=pl.ANY)
    in_specs = [any_, pl.BlockSpec((kk, c), lambda i: (0, 0), pipeline_mode=pl.Buffered(1)), row, _const(1, c), row]
    return pl.pallas_call(
        body, name=name, grid=(n,), in_specs=in_specs + ([any_] if has_after else []),
        out_specs=[row, row, _const(SUBLANES, c)],
        out_shape=[jax.ShapeDtypeStruct((s_, c), F32), jax.ShapeDtypeStruct((s_, c), MXU_DTYPE),
                   jax.ShapeDtypeStruct((SUBLANES, c), F32)],
        scratch_shapes=[pltpu.VMEM((slots, tm, kk), a.dtype), pltpu.SemaphoreType.DMA((slots,))],
        compiler_params=_cparams(("arbitrary",)),
    )(*([a, b, x, g, res] + ([after] if has_after else [])))


def _mla_up(cq, ckv, kro, wuq, wuk, wuv, tabs, *, ts=512):
    s_ = cq.shape[0]
    _, _, _, q_cos, _, b_up, b_dn = tabs

    def body(cq_ref, ckv_ref, kr_ref, wq_ref, wk_ref, wv_ref, qc, bu, bd, qo_ref, ko_ref, vo_ref):
        c_, u_, d_ = qc[...], bu[...], bd[...]
        kr = kr_ref[...]
        ckv_ = ckv_ref[...]
        vo_ref[...] = jnp.dot(ckv_, wv_ref[...], preferred_element_type=F32).astype(vo_ref.dtype)
        q = jnp.dot(cq_ref[...], wq_ref[...], preferred_element_type=F32)
        k = jnp.dot(ckv_, wk_ref[...], preferred_element_type=F32)
        for h in range(MLA_HEADS):
            sl = slice(h * LANES, (h + 1) * LANES)
            qo_ref[:, sl] = _rope(q[:, sl], c_, u_, d_, 112, 16).astype(qo_ref.dtype)
            ko_ref[:, sl] = (k[:, sl] + kr).astype(ko_ref.dtype)

    tab, out = _rows(ts, LANES), _rows(ts, 1024)
    return pl.pallas_call(
        body, name="mla_up", grid=(s_ // ts,),
        in_specs=[_rows(ts, Q_LORA), _rows(ts, KV_LORA), tab, _const(Q_LORA, 1024), _const(KV_LORA, 1024),
                  _const(KV_LORA, 1024), tab, tab, tab],
        out_specs=[out, out, out], out_shape=[jax.ShapeDtypeStruct((s_, 1024), MXU_DTYPE)] * 3,
        compiler_params=_cparams(("parallel",)),
    )(cq, ckv, kro, wuq, wuk, wuv, q_cos, b_up, b_dn)


def _mla_up_bwd(dqc, dkc, dvp, wuq, wukv, p, gq, gkv, tabs, *, ts=512):
    s_ = dqc.shape[0]
    _, _, _, q_cos, k_cos, b_up, b_dn = tabs

    def body(dq_ref, dk_ref, dv_ref, wq_ref, wkv_ref, ql_ref, kvl_ref, gq_ref, gkv_ref, qc, kc, bu, bd,
             dqo_ref, dkvo_ref, dkr_ref, dql_ref, dkvl_ref, dgq_ref, dgkv_ref):
        c_, u_, d_ = qc[...], bu[...], bd[...]
        tot = jnp.zeros((ts, LANES), F32)
        for h in range(MLA_HEADS):
            sl = slice(h * LANES, (h + 1) * LANES)
            dqo_ref[:, sl] = _rope_t(dq_ref[:, sl], c_, u_, d_, 112, 16).astype(dqo_ref.dtype)
            dk = dk_ref[:, sl]
            dkvo_ref[:, sl] = dk.astype(dkvo_ref.dtype)
            tot = tot + dk
        dkvo_ref[:, 1024:2048] = dv_ref[...].astype(dkvo_ref.dtype)
        dkr_ref[...] = _rope_t(tot, kc[...], u_, d_, 112, 16).astype(dkr_ref.dtype)

        @pl.when(pl.program_id(0) == 0)
        def _():
            dgq_ref[...] = jnp.zeros(dgq_ref.shape, F32)
            dgkv_ref[...] = jnp.zeros(dgkv_ref.shape, F32)

        for do_ref, w_ref, x_ref, g_ref, dx_ref, dg_ref in ((dqo_ref, wq_ref, ql_ref, gq_ref, dql_ref, dgq_ref),
                                                            (dkvo_ref, wkv_ref, kvl_ref, gkv_ref, dkvl_ref, dgkv_ref)):
            d = lax.dot_general(do_ref[...], w_ref[...], NT, preferred_element_type=F32)
            dx, gg = _rms_bwd(x_ref[...], g_ref[...], d)
            dx_ref[...] = dx.astype(dx_ref.dtype)
            dg_ref[...] += _sublane_sum(gg)

    tab = _rows(ts, LANES)
    return pl.pallas_call(
        body, name="mla_up_bwd", grid=(s_ // ts,),
        in_specs=[_rows(ts, 1024), _rows(ts, 1024), _rows(ts, 1024), _const(Q_LORA, 1024), _const(KV_LORA, 2048),
                  _rows(ts, Q_LORA, F_QLAT // Q_LORA), _rows(ts, KV_LORA, F_KVLAT // KV_LORA),
                  _const(1, Q_LORA), _const(1, KV_LORA), tab, tab, tab, tab],
        out_specs=[_rows(ts, 1024), _rows(ts, 2048), _rows(ts, LANES), _rows(ts, Q_LORA), _rows(ts, KV_LORA),
                   _const(SUBLANES, Q_LORA), _const(SUBLANES, KV_LORA)],
        out_shape=[jax.ShapeDtypeStruct((s_, 1024), MXU_DTYPE), jax.ShapeDtypeStruct((s_, 2048), MXU_DTYPE),
                   jax.ShapeDtypeStruct((s_, LANES), MXU_DTYPE), jax.ShapeDtypeStruct((s_, Q_LORA), MXU_DTYPE),
                   jax.ShapeDtypeStruct((s_, KV_LORA), MXU_DTYPE), jax.ShapeDtypeStruct((SUBLANES, Q_LORA), F32),
                   jax.ShapeDtypeStruct((SUBLANES, KV_LORA), F32)],
        compiler_params=_cparams(("arbitrary",)),
    )(dqc, dkc, dvp, wuq, wukv, p, p, gq, gkv, q_cos, k_cos, b_up, b_dn)


def _assemble_dp(dgab, dqa, dqlat, dkr, dka, dva, dkvlat, tabs, *, ts=512):
    s_ = dqa.shape[0]
    a_cos, a_up, a_dn = tabs[0], tabs[1], tabs[2]

    def body(dg_ref, dq_ref, dql_ref, dkr_ref, dk_ref, dv_ref, dkvl_ref, ac, au, ad, o_ref):
        c_, u_, d_ = ac[...], au[...], ad[...]
        o_ref[:, P_GA:P_Q] = dg_ref[...]
        for h in range(SWA_HEADS):
            sl = slice(h * LANES, (h + 1) * LANES)
            o_ref[:, P_Q + h * LANES:P_Q + (h + 1) * LANES] = _rope_t(dq_ref[:, sl], c_, u_, d_, 96, 32).astype(o_ref.dtype)
        o_ref[:, P_QLAT:P_KR] = dql_ref[...]
        o_ref[:, P_KR:P_K] = dkr_ref[...]
        for h in range(SWA_KV_HEADS):
            sl = slice(h * LANES, (h + 1) * LANES)
            o_ref[:, P_K + h * LANES:P_K + (h + 1) * LANES] = _rope_t(dk_ref[:, sl], c_, u_, d_, 96, 32).astype(o_ref.dtype)
        o_ref[:, P_V:P_KVLAT] = dv_ref[...]
        o_ref[:, P_KVLAT:P_W] = dkvl_ref[...]

    tab = _rows(ts, LANES)
    return pl.pallas_call(
        body, name="assemble_dp", grid=(s_ // ts,),
        in_specs=[_rows(ts, 2048), _rows(ts, 1024), _rows(ts, Q_LORA), _rows(ts, LANES), _rows(ts, 256), _rows(ts, 256),
                  _rows(ts, KV_LORA), tab, tab, tab],
        out_specs=_rows(ts, P_W), out_shape=jax.ShapeDtypeStruct((s_, P_W), MXU_DTYPE),
        compiler_params=_cparams(("parallel",)),
    )(dgab, dqa, dqlat, dkr, dka, dva, dkvlat, a_cos, a_up, a_dn)


def _attn_out_gate(oa, ob, woa_t, wob_t, p, *, ts=512):
    s_ = p.shape[0]

    def body(oa_ref, ob_ref, wa_ref, wb_ref, ga_ref, gb_ref, y_ref):
        ta = lax.dot_general(oa_ref[...], wa_ref[...], NT, preferred_element_type=F32)
        tb = lax.dot_general(ob_ref[...], wb_ref[...], NT, preferred_element_type=F32)
        y_ref[...] = (_sigmoid(ga_ref[...]) * ta + _sigmoid(gb_ref[...]) * tb).astype(y_ref.dtype)

    w = _const(1024, 1024)
    return pl.pallas_call(
        body, name="attn_out_gate", grid=(s_ // ts,),
        in_specs=[_rows(ts, 1024), _rows(ts, 1024), w, w, _rows(ts, 1024, F_GA // 1024), _rows(ts, 1024, F_GB // 1024)],
        out_specs=_rows(ts, 1024), out_shape=jax.ShapeDtypeStruct((s_, 1024), MXU_DTYPE),
        compiler_params=_cparams(("parallel",)),
    )(oa, ob, woa_t, wob_t, p, p)


def _d_y_gate(dx1b, wout, p, oa, ob, woa_t, wob_t, *, ts=512):
    s_ = p.shape[0]

    def body(dx_ref, w_ref, ga_ref, gb_ref, oa_ref, ob_ref, wa_ref, wb_ref, dta_ref, dtb_ref, dg_ref):
        d = lax.dot_general(dx_ref[...], w_ref[...], NT, preferred_element_type=F32)
        sa, sb = _sigmoid(ga_ref[...]), _sigmoid(gb_ref[...])
        dta_ref[...] = (d * sa).astype(dta_ref.dtype)
        dtb_ref[...] = (d * sb).astype(dtb_ref.dtype)
        ta = lax.dot_general(oa_ref[...], wa_ref[...], NT, preferred_element_type=F32)
        dg_ref[:, 0:1024] = (d * ta * (sa * (1.0 - sa))).astype(dg_ref.dtype)
        tb = lax.dot_general(ob_ref[...], wb_ref[...], NT, preferred_element_type=F32)
        dg_ref[:, 1024:2048] = (d * tb * (sb * (1.0 - sb))).astype(dg_ref.dtype)

    w = _const(1024, 1024)
    return pl.pallas_call(
        body, name="d_y_gate", grid=(s_ // ts,),
        in_specs=[_rows(ts, 1024), w, _rows(ts, 1024, F_GA // 1024), _rows(ts, 1024, F_GB // 1024),
                  _rows(ts, 1024), _rows(ts, 1024), w, w],
        out_specs=[_rows(ts, 1024), _rows(ts, 1024), _rows(ts, 2048)],
        out_shape=[jax.ShapeDtypeStruct((s_, 1024), MXU_DTYPE)] * 2 + [jax.ShapeDtypeStruct((s_, 2048), MXU_DTYPE)],
        compiler_params=_cparams(("parallel",)),
    )(dx1b, wout, p, p, oa, ob, woa_t, wob_t)


FF_TILE = D_FF // 2


def _ffn_in_act(x1, g, wgu_t, *, tm=512):
    s_ = x1.shape[0]
    n = s_ // tm

    def body(x_ref, g_ref, w_ref, h_ref, gu_ref, a_ref):
        h = _rms(x_ref[...], g_ref[...]).astype(h_ref.dtype)
        h_ref[...] = h
        p = lax.dot_general(h, w_ref[...], NT, preferred_element_type=F32)
        gu_ref[...] = p
        gate = p[:, :FF_TILE]
        a_ref[...] = (gate * _sigmoid(gate) * p[:, FF_TILE:]).astype(a_ref.dtype)

    return pl.pallas_call(
        body, name="ffn_in", grid=(2, s_ // tm),
        in_specs=[pl.BlockSpec((tm, D_MODEL), lambda j, i: (i, 0)), pl.BlockSpec((1, D_MODEL), lambda j, i: (0, 0)),
                  pl.BlockSpec((2 * FF_TILE, D_MODEL), lambda j, i: (j, 0))],
        out_specs=[pl.BlockSpec((tm, D_MODEL), lambda j, i: (i + j * (n - 1 - i), 0)),
                   pl.BlockSpec((tm, 2 * FF_TILE), lambda j, i: (i, j)),
                   pl.BlockSpec((tm, FF_TILE), lambda j, i: (i, j))],
        out_shape=[jax.ShapeDtypeStruct((s_, D_MODEL), MXU_DTYPE), jax.ShapeDtypeStruct((s_, 2 * D_FF), F32),
                   jax.ShapeDtypeStruct((s_, D_FF), MXU_DTYPE)],
        compiler_params=_cparams(("arbitrary", "arbitrary")),
    )(x1, g, wgu_t)


def _d_act_swiglu(dx2b, wd, gu, *, tm=512):
    s_ = dx2b.shape[0]
    n, slots = s_ // tm, 3
    steps = 2 * n

    def body(d_ref, w_ref, gu_hbm, o_ref, ring, sems):
        s = pl.program_id(0) * n + pl.program_id(1)

        def fetch(step):
            j, i = step // n, step % n
            src = gu_hbm.at[pl.ds(pl.multiple_of(i * tm, tm), tm), pl.ds(pl.multiple_of(j * 2 * FF_TILE, LANES), 2 * FF_TILE)]
            return pltpu.make_async_copy(src, ring.at[step % slots], sems.at[step % slots])

        @pl.when(s == 0)
        def _():
            for ahead in range(min(slots - 1, steps)):
                fetch(ahead).start()

        @pl.when(s + slots - 1 < steps)
        def _():
            fetch(s + slots - 1).start()

        fetch(s).wait()
        da = lax.dot_general(d_ref[...], w_ref[...], NT, preferred_element_type=F32)
        slot = s % slots
        g, u = ring[slot, :, :FF_TILE], ring[slot, :, FF_TILE:]
        sg = _sigmoid(g)
        o_ref[:, :FF_TILE] = (da * u * (sg * (1.0 + g * (1.0 - sg)))).astype(o_ref.dtype)
        o_ref[:, FF_TILE:] = (da * (g * sg)).astype(o_ref.dtype)

    return pl.pallas_call(
        body, name="d_act", grid=(2, n),
        in_specs=[pl.BlockSpec((tm, D_MODEL), lambda j, i: (i, 0)), pl.BlockSpec((FF_TILE, D_MODEL), lambda j, i: (j, 0)),
                  pl.BlockSpec(memory_space=pl.ANY)],
        out_specs=pl.BlockSpec((tm, 2 * FF_TILE), lambda j, i: (i, j)), out_shape=jax.ShapeDtypeStruct((s_, 2 * D_FF), MXU_DTYPE),
        scratch_shapes=[pltpu.VMEM((slots, tm, 2 * FF_TILE), gu.dtype), pltpu.SemaphoreType.DMA((slots,))],
        compiler_params=_cparams(("arbitrary", "arbitrary")),
    )(dx2b, wd, gu)


def _ffn_out_loss(act, wd, x1, g, tgt, *, ts=512):
    s_, c = x1.shape
    kk = act.shape[1]

    def body(a_ref, w_ref, x_ref, g_ref, t_ref, dx_ref, dxb_ref, dg_ref, lp_ref, tot_ref):
        v = x_ref[...] + jnp.dot(a_ref[...], w_ref[...], preferred_element_type=F32)
        r = lax.rsqrt(jnp.mean(v * v, axis=-1, keepdims=True) + EPS)
        xh = v * r
        gg = g_ref[...]
        e = xh * gg - t_ref[...]
        do = e * (1.0 / c)
        dxh = do * gg
        dx = r * (dxh - xh * jnp.mean(dxh * xh, axis=-1, keepdims=True))
        dx_ref[...] = dx
        dxb_ref[...] = dx.astype(dxb_ref.dtype)
        i = pl.program_id(0)

        @pl.when(i == 0)
        def _():
            dg_ref[...] = jnp.zeros(dg_ref.shape, F32)
            lp_ref[...] = jnp.zeros(lp_ref.shape, F32)

        dg_ref[...] += _sublane_sum(do * xh)
        lp_ref[...] += _sublane_sum(e * e)
        tot_ref[...] = jnp.full(tot_ref.shape, (0.5 / c) * jnp.sum(lp_ref[...]), F32)

    return pl.pallas_call(
        body, name="ffn_out_loss", grid=(s_ // ts,),
        in_specs=[_rows(ts, kk), _const(kk, c), _rows(ts, c), _const(1, c), _rows(ts, c)],
        out_specs=[_rows(ts, c), _rows(ts, c), _const(SUBLANES, c), _const(SUBLANES, c), _const(SUBLANES, LANES)],
        out_shape=[jax.ShapeDtypeStruct((s_, c), F32), jax.ShapeDtypeStruct((s_, c), MXU_DTYPE),
                   jax.ShapeDtypeStruct((SUBLANES, c), F32), jax.ShapeDtypeStruct((SUBLANES, c), F32),
                   jax.ShapeDtypeStruct((SUBLANES, LANES), F32)],
        compiler_params=_cparams(("arbitrary",)),
    )(act, wd, x1, g, tgt)


def _mla_d_out(dtb, wob_t, o32, *, ts=512):
    s_ = dtb.shape[0]

    def body(dt_ref, w_ref, o_ref, dob_ref, dl_ref):
        d = jnp.dot(dt_ref[...], w_ref[...], preferred_element_type=F32)
        dob_ref[...] = d.astype(dob_ref.dtype)
        prod = d * o_ref[...]
        for h in range(MLA_HEADS):
            dl_ref[h] = jnp.sum(prod[:, h * LANES:(h + 1) * LANES].T, axis=0, keepdims=True)

    return pl.pallas_call(
        body, name="mla_d_out", grid=(s_ // ts,), in_specs=[_rows(ts, 1024), _const(1024, 1024), _rows(ts, 1024)],
        out_specs=[_rows(ts, 1024), pl.BlockSpec((MLA_HEADS, 1, ts), lambda i: (0, 0, i))],
        out_shape=[jax.ShapeDtypeStruct((s_, 1024), MXU_DTYPE), jax.ShapeDtypeStruct((MLA_HEADS, 1, s_), F32)],
        compiler_params=_cparams(("parallel",)),
    )(dtb, wob_t, o32)


SWA_T = 4 * BLOCK


SWA_W = SWA_GROUP * BLOCK


def _swa_masks(sb):
    kr = lax.broadcasted_iota(jnp.int32, (2 * BLOCK, SWA_W), 0)
    qc = jnp.bitwise_and(lax.broadcasted_iota(jnp.int32, (2 * BLOCK, SWA_W), 1), BLOCK - 1)
    band = jnp.logical_and(kr > qc, kr <= qc + BLOCK)
    first = jnp.logical_and(band, kr >= BLOCK)
    return band, jnp.logical_or(first, jnp.logical_and(band, sb > 0))


def _heads_to_rows(ref, rs):
    return jnp.concatenate([ref[rs, h * LANES:(h + 1) * LANES] for h in range(SWA_GROUP)], axis=0)


def _sink_row(sk_ref):
    return jnp.concatenate([sk_ref[0, h:h + 1, :] for h in range(SWA_GROUP)], axis=1) * LOG2E


def _swa_in_specs(rev, nsb):
    sbi = (lambda j: nsb - 1 - j) if rev else (lambda j: j)
    cur = pl.BlockSpec((SWA_T, LANES), lambda g, j: (sbi(j), g))
    prev = pl.BlockSpec((BLOCK, LANES), lambda g, j: (jnp.maximum(4 * sbi(j) - 1, 0), g))
    q = pl.BlockSpec((SWA_T, SWA_GROUP * LANES), lambda g, j: (sbi(j), g))
    sink = pl.BlockSpec((1, SUBLANES, LANES), lambda g, j: (g, 0, 0))
    lse = pl.BlockSpec((SWA_GROUP, 1, SWA_T), lambda g, j: (g, 0, sbi(j)))
    return q, cur, prev, sink, lse


def _swa_fwd(qa, ka, va, sink_b):
    s_ = qa.shape[0]
    nsb = s_ // SWA_T
    c2 = HEAD_DIM ** -0.5 * LOG2E

    def body(q_ref, kc_ref, kp_ref, vc_ref, vp_ref, sk_ref, o32_ref, o16_ref, lse_ref, kx, vx):
        kx[0:BLOCK, :] = kp_ref[...]
        kx[BLOCK:5 * BLOCK, :] = kc_ref[...]
        vx[0:BLOCK, :] = vp_ref[...]
        vx[BLOCK:5 * BLOCK, :] = vc_ref[...]
        band, band0 = _swa_masks(pl.program_id(1))
        sink2 = _sink_row(sk_ref)
        for b in range(4):
            rs = slice(b * BLOCK, (b + 1) * BLOCK)
            ks = slice(b * BLOCK, (b + 2) * BLOCK)
            st = lax.dot_general(kx[ks, :], _heads_to_rows(q_ref, rs), NT, preferred_element_type=F32) * c2
            st = jnp.where(band0 if b == 0 else band, st, -jnp.inf)
            m = jnp.maximum(jnp.max(st, axis=0, keepdims=True), sink2)
            pt = jnp.exp2(st - m)
            den = jnp.sum(pt, axis=0, keepdims=True) + jnp.exp2(sink2 - m)
            o = lax.dot_general((pt * (1.0 / den)).astype(MXU_DTYPE), vx[ks, :], TN, preferred_element_type=F32)
            lse = m + jnp.log2(den)
            for hh in range(SWA_GROUP):
                cs = slice(hh * LANES, (hh + 1) * LANES)
                o32_ref[rs, cs] = o[cs, :]
                o16_ref[rs, cs] = o[cs, :].astype(o16_ref.dtype)
                lse_ref[hh, :, rs] = lse[:, cs]

    q, cur, prev, sink, lse_spec = _swa_in_specs(False, nsb)
    return pl.pallas_call(
        body, name="swa_fwd", grid=(SWA_KV_HEADS, nsb), in_specs=[q, cur, prev, cur, prev, sink],
        out_specs=[q, q, lse_spec],
        out_shape=[jax.ShapeDtypeStruct((s_, SWA_HEADS * LANES), F32), jax.ShapeDtypeStruct((s_, SWA_HEADS * LANES), MXU_DTYPE),
                   jax.ShapeDtypeStruct((SWA_HEADS, 1, s_), F32)],
        scratch_shapes=[pltpu.VMEM((5 * BLOCK, LANES), MXU_DTYPE), pltpu.VMEM((5 * BLOCK, LANES), MXU_DTYPE)],
        compiler_params=_cparams(("parallel", "arbitrary")),
    )(qa, ka, ka, va, va, sink_b)


def _swa_bwd(qa, ka, va, sink_b, o32, do, lse):
    s_ = qa.shape[0]
    nsb = s_ // SWA_T
    scale = HEAD_DIM ** -0.5
    c2 = scale * LOG2E

    def body(q_ref, kc_ref, kp_ref, vc_ref, vp_ref, sk_ref, o_ref, do_ref, lse_ref,
             dq_ref, dk_ref, dv_ref, dsk_ref, kx, vx, kacc, vacc, kcar, vcar):
        j = pl.program_id(1)
        kx[0:BLOCK, :] = kp_ref[...]
        kx[BLOCK:5 * BLOCK, :] = kc_ref[...]
        vx[0:BLOCK, :] = vp_ref[...]
        vx[BLOCK:5 * BLOCK, :] = vc_ref[...]
        band, band0 = _swa_masks(nsb - 1 - j)
        kacc[...] = jnp.zeros(kacc.shape, F32)
        vacc[...] = jnp.zeros(vacc.shape, F32)

        @pl.when(j == 0)
        def _():
            kcar[...] = jnp.zeros(kcar.shape, F32)
            vcar[...] = jnp.zeros(vcar.shape, F32)
            dsk_ref[...] = jnp.zeros(dsk_ref.shape, F32)

        sink2 = _sink_row(sk_ref)
        dsink = jnp.zeros((1, SWA_W), F32)
        for b in range(4):
            rs = slice(b * BLOCK, (b + 1) * BLOCK)
            ks = slice(b * BLOCK, (b + 2) * BLOCK)
            q, k2, v2 = _heads_to_rows(q_ref, rs), kx[ks, :], vx[ks, :]
            d = _heads_to_rows(do_ref, rs)
            delta = jnp.sum((d * _heads_to_rows(o_ref, rs)).T, axis=0, keepdims=True)
            l2 = jnp.concatenate([lse_ref[hh, :, rs] for hh in range(SWA_GROUP)], axis=1)
            st = lax.dot_general(k2, q, NT, preferred_element_type=F32) * c2
            pt = jnp.exp2(jnp.where(band0 if b == 0 else band, st, -jnp.inf) - l2)
            db = d.astype(MXU_DTYPE)
            dst = (pt * (lax.dot_general(v2, db, NT, preferred_element_type=F32) - delta) * scale).astype(MXU_DTYPE)
            dq = lax.dot_general(dst, k2, TN, preferred_element_type=F32)
            for hh in range(SWA_GROUP):
                dq_ref[rs, hh * LANES:(hh + 1) * LANES] = dq[hh * LANES:(hh + 1) * LANES, :]
            kacc[ks, :] += jnp.dot(dst, q, preferred_element_type=F32)
            vacc[ks, :] += jnp.dot(pt.astype(MXU_DTYPE), db, preferred_element_type=F32)
            dsink = dsink - jnp.exp2(sink2 - l2) * delta
        for hh in range(SWA_GROUP):
            tot = jnp.sum(dsink[:, hh * LANES:(hh + 1) * LANES], axis=1, keepdims=True)
            dsk_ref[0, hh:hh + 1, :] += jnp.broadcast_to(tot, (1, LANES))

        dk_ref[0:3 * BLOCK, :] = kacc[BLOCK:4 * BLOCK, :]
        dk_ref[3 * BLOCK:4 * BLOCK, :] = kacc[4 * BLOCK:5 * BLOCK, :] + kcar[...]
        dv_ref[0:3 * BLOCK, :] = vacc[BLOCK:4 * BLOCK, :].astype(dv_ref.dtype)
        dv_ref[3 * BLOCK:4 * BLOCK, :] = (vacc[4 * BLOCK:5 * BLOCK, :] + vcar[...]).astype(dv_ref.dtype)
        kcar[...] = kacc[0:BLOCK, :]
        vcar[...] = vacc[0:BLOCK, :]

    q, cur, prev, sink, lse_spec = _swa_in_specs(True, nsb)
    return pl.pallas_call(
        body, name="swa_bwd", grid=(SWA_KV_HEADS, nsb),
        in_specs=[q, cur, prev, cur, prev, sink, q, q, lse_spec],
        out_specs=[q, cur, cur, sink],
        out_shape=[jax.ShapeDtypeStruct((s_, SWA_HEADS * LANES), F32), jax.ShapeDtypeStruct((s_, SWA_KV_HEADS * LANES), F32),
                   jax.ShapeDtypeStruct((s_, SWA_KV_HEADS * LANES), MXU_DTYPE),
                   jax.ShapeDtypeStruct((SWA_KV_HEADS, SUBLANES, LANES), F32)],
        scratch_shapes=[pltpu.VMEM((5 * BLOCK, LANES), MXU_DTYPE), pltpu.VMEM((5 * BLOCK, LANES), MXU_DTYPE),
                        pltpu.VMEM((5 * BLOCK, LANES), F32), pltpu.VMEM((5 * BLOCK, LANES), F32),
                        pltpu.VMEM((BLOCK, LANES), F32), pltpu.VMEM((BLOCK, LANES), F32)],
        compiler_params=_cparams(("arbitrary", "arbitrary")),
    )(qa, ka, ka, va, va, sink_b, o32, do, lse)


MLA_T = 512
MLA_FWD_GROUP = 4
MLA_BWD_GROUP = 2


def _mla_specs(s_, t, group):
    w = group * LANES
    qs = pl.BlockSpec((t, w), lambda g, i: (i, g))
    kv = pl.BlockSpec((s_, w), lambda g, i: (0, g))
    row = pl.BlockSpec((group, 1, t), lambda g, i: (g, 0, i))
    return qs, kv, row


def _causal_scores_t(k, q, t, c2, masked):
    st = lax.dot_general(k, q, NT, preferred_element_type=F32) * c2
    if masked:
        kr = lax.broadcasted_iota(jnp.int32, (t, t), 0)
        qc = lax.broadcasted_iota(jnp.int32, (t, t), 1)
        st = jnp.where(kr <= qc, st, -jnp.inf)
    return st


def _mla_fwd(qc, kc, vp):
    s_ = qc.shape[0]
    t = min(MLA_T, s_)
    c2 = MLA_QK ** -0.5 * LOG2E
    grp = MLA_FWD_GROUP

    def body(q_ref, k_ref, v_ref, o32_ref, o16_ref, lse_ref, m_s, acc_s):
        qi = pl.program_id(1)
        m_s[...] = jnp.full(m_s.shape, -jnp.inf, F32)
        acc_s[...] = jnp.zeros(acc_s.shape, F32)
        ones_lane = lax.broadcasted_iota(jnp.int32, (t, LANES), 1) == MLA_V

        def step(ki, masked):
            off = pl.multiple_of(ki * t, t)
            for g in range(grp):
                cs = slice(g * LANES, (g + 1) * LANES)
                st = _causal_scores_t(k_ref[pl.ds(off, t), cs], q_ref[:, cs], t, c2, masked)
                m_old = m_s[g]
                m_new = jnp.maximum(m_old, jnp.max(st, axis=0, keepdims=True))
                alpha = jnp.exp2(m_old - m_new)
                pt = jnp.exp2(st - m_new).astype(MXU_DTYPE)
                v = v_ref[pl.ds(off, t), cs]
                v = jnp.where(ones_lane, jnp.ones((), v.dtype), v)
                acc_s[g] = alpha * acc_s[g] + lax.dot_general(v, pt, TN, preferred_element_type=F32)
                m_s[g] = m_new

        def full_block(ki, carry):
            step(ki, False)
            return carry

        lax.fori_loop(0, qi, full_block, 0)
        step(qi, True)
        for g in range(grp):
            cs = slice(g * LANES, (g + 1) * LANES)
            acc = acc_s[g]
            l = acc[MLA_V:MLA_V + 1, :]
            o = (acc * (1.0 / l)).T
            o32_ref[:, cs] = o
            o16_ref[:, cs] = o.astype(o16_ref.dtype)
            lse_ref[g] = m_s[g] + jnp.log2(l)

    qs, kv, row = _mla_specs(s_, t, grp)
    return pl.pallas_call(
        body, name="mla_fwd", grid=(MLA_HEADS // grp, s_ // t), in_specs=[qs, kv, kv], out_specs=[qs, qs, row],
        out_shape=[jax.ShapeDtypeStruct((s_, MLA_HEADS * LANES), F32), jax.ShapeDtypeStruct((s_, MLA_HEADS * LANES), MXU_DTYPE),
                   jax.ShapeDtypeStruct((MLA_HEADS, 1, s_), F32)],
        scratch_shapes=[pltpu.VMEM((grp, 1, t), F32), pltpu.VMEM((grp, LANES, t), F32)],
        compiler_params=_cparams(("parallel", "arbitrary")),
    )(qc, kc, vp)


def _mla_bwd(qc, kc, vp, dob, lse, delta):
    s_ = qc.shape[0]
    t = min(MLA_T, s_)
    scale = MLA_QK ** -0.5
    c2 = scale * LOG2E
    grp = MLA_BWD_GROUP

    def body(q_ref, do_ref, lse_ref, dl_ref, k_ref, v_ref, dq_ref, dk_ref, dv_ref, dqt_s):
        qi = pl.program_id(1)

        @pl.when(qi == 0)
        def _():
            dk_ref[...] = jnp.zeros(dk_ref.shape, F32)
            dv_ref[...] = jnp.zeros(dv_ref.shape, F32)

        dqt_s[...] = jnp.zeros(dqt_s.shape, F32)

        def step(ki, masked):
            off = pl.multiple_of(ki * t, t)
            for g in range(grp):
                cs = slice(g * LANES, (g + 1) * LANES)
                q, d, k = q_ref[:, cs], do_ref[:, cs], k_ref[pl.ds(off, t), cs]
                pt = jnp.exp2(_causal_scores_t(k, q, t, c2, masked) - lse_ref[g])
                dpt = lax.dot_general(v_ref[pl.ds(off, t), cs], d, NT, preferred_element_type=F32)
                dst = (pt * (dpt - dl_ref[g]) * scale).astype(MXU_DTYPE)
                dv_ref[pl.ds(off, t), cs] += jnp.dot(pt.astype(MXU_DTYPE), d, preferred_element_type=F32)
                dk_ref[pl.ds(off, t), cs] += jnp.dot(dst, q, preferred_element_type=F32)
                dqt_s[g] += lax.dot_general(k, dst, TN, preferred_element_type=F32)

        def full_block(ki, carry):
            step(ki, False)
            return carry

        lax.fori_loop(0, qi, full_block, 0)
        step(qi, True)
        for g in range(grp):
            dq_ref[:, g * LANES:(g + 1) * LANES] = dqt_s[g].T

    qs, kv, row = _mla_specs(s_, t, grp)
    shp = jax.ShapeDtypeStruct((s_, MLA_HEADS * LANES), F32)
    return pl.pallas_call(
        body, name="mla_bwd", grid=(MLA_HEADS // grp, s_ // t), in_specs=[qs, qs, row, row, kv, kv],
        out_specs=[qs, kv, kv], out_shape=[shp, shp, shp], scratch_shapes=[pltpu.VMEM((grp, LANES, t), F32)],
        compiler_params=_cparams(("parallel", "arbitrary")),
    )(qc, dob, lse, delta, kc, vp)


def _pad_heads(w, nh, hd, axis):
    shp = w.shape
    w = w.reshape(shp[:axis] + (nh, hd) + shp[axis + 1:])
    pad = [(0, 0)] * w.ndim
    pad[axis + 1] = (0, LANES - hd)
    w = jnp.pad(w, pad)
    return w.reshape(shp[:axis] + (nh * LANES,) + shp[axis + 1:])


def _unpad_heads(w, nh, hd, axis):
    shp = w.shape
    w = w.reshape(shp[:axis] + (nh, LANES) + shp[axis + 1:])
    w = lax.slice_in_dim(w, 0, hd, axis=axis + 1)
    return w.reshape(shp[:axis] + (nh * hd,) + shp[axis + 1:])


PACK_W = 1024
ROW_TILE = 16
FULL_SHAPE = dict(w_in=(1024, 3488), w_uq=(384, 768), w_ukv=(256, 1024), w_o_swa=(512, 1024), w_o_mla=(512, 1024),
                  w_out=(1024, 1024), w_gate=(1024, 2816), w_up=(1024, 2816), w_down=(2816, 1024))
BIG = tuple(FULL_SHAPE)
ROW_SHARDED = ("w_out", "w_down")
W_IN_COLS = FULL_SHAPE["w_in"][1] // N_DEV
W_IN_ROWS = -(-W_IN_COLS // ROW_TILE) * ROW_TILE
FF_COLS = D_FF // N_DEV
OUT_ROWS = D_MODEL // N_DEV
SMALL_FLAT = (("w_uq", 0, 36), ("w_ukv", 48, 32))
SMALL_USED = 80
MID_BLOCKS = 4
MID_ROWS = MID_BLOCKS * OUT_ROWS
EARLY_ROWS = W_IN_ROWS + MID_ROWS
LATE_ROWS = 3 * FF_COLS
PACK_ROWS = EARLY_ROWS + LATE_ROWS


def _shard_shape(n):
    r, c = FULL_SHAPE[n]
    return (r // N_DEV, c) if n in ROW_SHARDED else (r, c // N_DEV)


def _wire_pack(sh, dtype):
    c = lambda n: sh[n].astype(dtype)
    rows = [jnp.pad(c("w_in").T, ((0, W_IN_ROWS - W_IN_COLS), (0, 0))), c("w_out"),
            _pad_heads(c("w_o_swa").T, SWA_HEADS, HEAD_DIM, 1), _pad_heads(c("w_o_mla").T, MLA_HEADS, MLA_V, 1)]
    for n, _, r in SMALL_FLAT:
        rows.append(jnp.pad(c(n).reshape(r, PACK_W), ((0, -r % ROW_TILE), (0, 0))))
    rows.append(jnp.zeros((OUT_ROWS - SMALL_USED, PACK_W), dtype))
    return jnp.concatenate(rows + [c("w_gate").T, c("w_up").T, c("w_down")], 0)


def _mid_unpack(p):
    out = dict(w_out=p[0:OUT_ROWS], w_o_swa=_unpad_heads(p[OUT_ROWS:2 * OUT_ROWS], SWA_HEADS, HEAD_DIM, 1).T,
               w_o_mla=_unpad_heads(p[2 * OUT_ROWS:3 * OUT_ROWS], MLA_HEADS, MLA_V, 1).T)
    for n, off, r in SMALL_FLAT:
        out[n] = p[3 * OUT_ROWS + off:3 * OUT_ROWS + off + r].reshape(_shard_shape(n))
    return out


def _w_in_row_maps():
    sp = lambda col: (col // W_IN_COLS) * W_IN_ROWS + col % W_IN_COLS
    fwd = np.full((P_W,), -1, np.int64)

    def put(t0, c0, n):
        fwd[t0:t0 + n] = [sp(c) for c in range(c0, c0 + n)]

    put(P_GA, IN_OFF[6], D_MODEL)
    put(P_GB, IN_OFF[7], D_MODEL)
    for h in range(SWA_HEADS):
        put(P_Q + LANES * h, IN_OFF[0] + HEAD_DIM * h, HEAD_DIM)
    put(P_QLAT, IN_OFF[3], Q_LORA)
    put(P_KR + KR_LANE, IN_OFF[5], MLA_ROPE)
    for h in range(SWA_KV_HEADS):
        put(P_K + LANES * h, IN_OFF[1] + HEAD_DIM * h, HEAD_DIM)
        put(P_V + LANES * h, IN_OFF[2] + HEAD_DIM * h, HEAD_DIM)
    put(P_KVLAT, IN_OFF[4], KV_LORA)
    inv = np.full((N_DEV * W_IN_ROWS,), -1, np.int64)
    inv[fwd[fwd >= 0]] = np.nonzero(fwd >= 0)[0]
    return fwd, inv


def _take_rows(src, idx, *, name, tile=2 * LANES):
    n_out, n_src, width = len(idx), src.shape[0], src.shape[1]
    assert n_out % tile == 0 and n_src % tile == 0
    n_tiles = n_out // tile
    blocks = [sorted({int(v) // tile for v in idx[i * tile:(i + 1) * tile] if v >= 0}) for i in range(n_tiles)]
    k_max = max(1, max(len(b) for b in blocks))
    tab = np.zeros((n_tiles, k_max), np.int32)
    sel = np.zeros((n_tiles, k_max, tile, tile), np.float32)
    for i, blks in enumerate(blocks):
        for m, b in enumerate(blks):
            tab[i, m] = b
            for r in range(tile):
                v = int(idx[i * tile + r])
                if v >= 0 and v // tile == b:
                    sel[i, m, r, v % tile] = 1.0

    def body(tab_ref, sel_ref, *refs):
        o_ref = refs[k_max]
        acc = jnp.dot(sel_ref[0, 0], refs[0][...], preferred_element_type=F32)
        for m in range(1, k_max):
            acc = acc + jnp.dot(sel_ref[0, m], refs[m][...], preferred_element_type=F32)
        o_ref[...] = acc.astype(o_ref.dtype)

    def src_spec(m):
        return pl.BlockSpec((tile, width), lambda i, t: (t[i * k_max + m], 0))

    return pl.pallas_call(
        body, name=name,
        grid_spec=pltpu.PrefetchScalarGridSpec(
            num_scalar_prefetch=1, grid=(n_tiles,),
            in_specs=[pl.BlockSpec((1, k_max, tile, tile), lambda i, t: (i, 0, 0, 0))] + [src_spec(m) for m in range(k_max)],
            out_specs=pl.BlockSpec((tile, width), lambda i, t: (i, 0))),
        out_shape=jax.ShapeDtypeStruct((n_out, width), src.dtype),
        compiler_params=_cparams(("parallel",)),
    )(jnp.asarray(tab.reshape(-1)), jnp.asarray(sel, src.dtype), *([src] * k_max))


def _w_in_operand(win_g):
    return _take_rows(win_g.reshape(N_DEV * W_IN_ROWS, PACK_W), _w_in_row_maps()[0], name="w_in_rows")


def _mid_operands(wout_g, woa_g, wob_g, small_g):
    def full(n, off, r):
        a = small_g[:, off:off + r].reshape((N_DEV,) + _shard_shape(n))
        return jnp.moveaxis(a, 0, 1).reshape(FULL_SHAPE[n])

    w = {n: full(n, off, r) for n, off, r in SMALL_FLAT}
    ukv = w["w_ukv"].reshape(KV_LORA, MLA_HEADS, MLA_NOPE + MLA_V)
    return dict(
        wout=wout_g.reshape(D_MODEL, D_MODEL), woa_t=woa_g.reshape(D_MODEL, -1), wob_t=wob_g.reshape(D_MODEL, -1),
        wuq=_pad_heads(w["w_uq"], MLA_HEADS, MLA_QK, 1),
        wuk=_pad_heads(ukv[:, :, :MLA_NOPE].reshape(KV_LORA, -1), MLA_HEADS, MLA_NOPE, 1),
        wuv=_pad_heads(ukv[:, :, MLA_NOPE:].reshape(KV_LORA, -1), MLA_HEADS, MLA_V, 1),
    )


def _mid_grad_pack(g):
    uk = _unpad_heads(g["wukv"][:, :1024], MLA_HEADS, MLA_NOPE, 1).reshape(KV_LORA, MLA_HEADS, MLA_NOPE)
    uv = _unpad_heads(g["wukv"][:, 1024:], MLA_HEADS, MLA_V, 1).reshape(KV_LORA, MLA_HEADS, MLA_V)
    w = dict(w_uq=_unpad_heads(g["wuq"], MLA_HEADS, MLA_QK, 1), w_ukv=jnp.concatenate([uk, uv], 2).reshape(KV_LORA, -1))
    rows = []
    for n, _, r in SMALL_FLAT:
        rr, cc = FULL_SHAPE[n]
        a = jnp.moveaxis(w[n].reshape(rr, N_DEV, cc // N_DEV), 1, 0).reshape(N_DEV, r, PACK_W)
        rows.append(jnp.pad(a, ((0, 0), (0, -r % ROW_TILE), (0, 0))).astype(WIRE_DTYPE))
    rows.append(jnp.zeros((N_DEV, OUT_ROWS - SMALL_USED, PACK_W), WIRE_DTYPE))
    blk = lambda a: a.reshape(N_DEV, OUT_ROWS, PACK_W)
    return [blk(g["wout"]), blk(g["woa_t"]), blk(g["wob_t"]), jnp.concatenate(rows, 1)]


def _w_in_grad_chunks(g_win_t):
    return _take_rows(g_win_t, _w_in_row_maps()[1], name="dw_in_rows").reshape(N_DEV, W_IN_ROWS, PACK_W)


def _local_step(x, tgt, win_t, small, weights, grads):
    s_ = x.shape[0]
    tabs = _rope_tables(s_)
    sink_b = jnp.broadcast_to(small["swa_sinks"].reshape(SWA_KV_HEADS, SWA_GROUP, 1), (SWA_KV_HEADS, SWA_GROUP, LANES))
    sink_b = jnp.pad(sink_b, ((0, 0), (0, SUBLANES - SWA_GROUP), (0, 0)))

    h, qa, ka, va, cq, ckv, kro, p = _proj_in(x, small["mix_norm_g"], win_t, small["q_norm_g"], small["kv_norm_g"], tabs)
    oa32, oa16, lse_a = _swa_fwd(qa, ka, va, sink_b)
    ops = weights.mid(oa16)
    qc, kc, vp = _mla_up(cq, ckv, kro, ops["wuq"], ops["wuk"], ops["wuv"], tabs)
    ob32, ob16, lse_b = _mla_fwd(qc, kc, vp)
    y = _attn_out_gate(oa16, ob16, ops["woa_t"], ops["wob_t"], p)
    x1 = _mm(y, ops["wout"], "nn", name="out_proj", add=x, tm=1024, tn=1024)
    wgu_t, wd = weights.late(x1)
    h2, gu, act = _ffn_in_act(x1, small["ffn_norm_g"], wgu_t)

    dx2, dx2b, dg3, _, tot = _ffn_out_loss(act, wd, x1, small["final_norm_g"].reshape(1, D_MODEL), tgt)
    g = {}
    g_wd = _mm(act, dx2b, "tn", name="dw_down", tm=FF_TILE, tn=1024, tk=2048, out_dtype=WIRE_DTYPE)
    dgu = _d_act_swiglu(dx2b, wd, gu)
    g_wgu = _mm(dgu, h2, "tn", name="dw_ffn_in", tm=FF_TILE, tn=1024, tk=2048, out_dtype=WIRE_DTYPE)
    token = grads.late(g_wgu, g_wd)
    dx1, dx1b, dg2 = _mm_norm_bwd(dgu, wgu_t, x1, small["ffn_norm_g"] + token[0:1, 0:1], dx2, name="d_h2")
    g["wout"] = _mm(y, dx1b, "tn", name="dw_out", tm=1024, tn=1024, tk=2048, out_dtype=WIRE_DTYPE)
    dta, dtb, dgab = _d_y_gate(dx1b, ops["wout"], p, oa16, ob16, ops["woa_t"], ops["wob_t"])
    doa = _mm(dta, ops["woa_t"], "nn", name="d_oa", tm=1024, tn=1024)
    g["woa_t"] = _mm(dta, oa16, "tn", name="dw_o_swa", tm=1024, tn=1024, tk=2048, out_dtype=WIRE_DTYPE)
    g["wob_t"] = _mm(dtb, ob16, "tn", name="dw_o_mla", tm=1024, tn=1024, tk=2048, out_dtype=WIRE_DTYPE)
    dob16, delta_b = _mla_d_out(dtb, ops["wob_t"], ob32)
    dqc, dkc, dvp = _mla_bwd(qc, kc, vp, dob16, lse_b, delta_b)
    dqp, dkv, dkr, dqlat, dkvlat, dgq, dgkv = _mla_up_bwd(
        dqc, dkc, dvp, ops["wuq"], jnp.concatenate([ops["wuk"], ops["wuv"]], 1), p, small["q_norm_g"], small["kv_norm_g"], tabs)
    g["wuq"] = _mm(cq, dqp, "tn", name="dw_uq", tm=Q_LORA, tn=1024, tk=2048)
    g["wukv"] = _mm(ckv, dkv, "tn", name="dw_ukv", tm=KV_LORA, tn=2048, tk=2048)
    token = grads.mid(g)
    dqa, dka, dva, dsk = _swa_bwd(qa, ka, va, sink_b + token[0:1, 0:1], oa32, doa, lse_a)
    dp = _assemble_dp(dgab, dqa, dqlat, dkr, dka, dva, dkvlat, tabs)
    token = grads.last(_mm(dp, h, "tn", name="dw_in", tm=2176, tn=1024, tk=1024, out_dtype=WIRE_DTYPE))
    gx, _, dg1 = _mm_norm_bwd(dp, win_t, x, small["mix_norm_g"], dx1, name="d_h", after=token)

    sm = dict(mix_norm_g=dg1, ffn_norm_g=dg2, final_norm_g=dg3, q_norm_g=dgq, kv_norm_g=dgkv,
              swa_sinks=dsk[:, :SWA_GROUP, 0].reshape(1, SWA_HEADS))
    return tot, gx, sm


MESH = pl.DeviceIdType.MESH
ANY = pl.BlockSpec(memory_space=pl.ANY)


def _position():
    return lax.axis_index("x"), lax.axis_index("y"), lax.axis_index("c")


def _all_gather(block, pieces, shapes, *, name):
    n_out = len(shapes)
    n_rows = sum(p[3] for p in pieces)

    def body(x_ref, *refs):
        outs, (send_sems, recv_sems, local_sem) = refs[:n_out], refs[n_out:]
        x, y, c = _position()
        me, sibling = (x, y, c), (x, y, 1 - c)
        chips = [(1 - x, y), (x, 1 - y), (1 - x, 1 - y)]

        def dst(piece, blk):
            arr, lead, _, _ = piece
            return outs[arr].at[lead(4 * blk[0] + 2 * blk[1] + blk[2])]

        def own(piece):
            return x_ref.at[pl.ds(piece[2], piece[3])]

        def copies(k, blk, to, from_input):
            return [pltpu.make_async_remote_copy(
                src_ref=own(p) if from_input else dst(p, blk), dst_ref=dst(p, blk), send_sem=send_sems.at[k],
                recv_sem=recv_sems.at[k], device_id=to, device_id_type=MESH) for p in pieces]

        gathered_rows = x_ref.at[pl.ds(0, n_rows)]

        def whole_block(k):
            return pltpu.make_async_remote_copy(src_ref=gathered_rows, dst_ref=gathered_rows, send_sem=send_sems.at[k],
                                                recv_sem=recv_sems.at[k], device_id=me, device_id_type=MESH)

        for p in pieces:
            pltpu.make_async_copy(own(p), dst(p, me), local_sem).start()
        for cp in copies(0, me, sibling, True):
            cp.start()
        for j, chip in enumerate(chips):
            for cp in copies(1 + j, me, (*chip, c), True):
                cp.start()
        for j, chip in enumerate(chips):
            whole_block(1 + j).wait_recv()
            for cp in copies(4 + j, (*chip, c), sibling, False):
                cp.start()
        whole_block(0).wait_recv()
        for j in range(3):
            whole_block(4 + j).wait_recv()
        for k in range(7):
            whole_block(k).wait_send()
        pltpu.make_async_copy(gathered_rows, gathered_rows, local_sem).wait()

    return pl.pallas_call(
        body, name=name, out_shape=[jax.ShapeDtypeStruct(s, block.dtype) for s in shapes], in_specs=[ANY],
        out_specs=[ANY] * n_out,
        scratch_shapes=[pltpu.SemaphoreType.DMA((7,)), pltpu.SemaphoreType.DMA((7,)), pltpu.SemaphoreType.DMA],
    )(block)


HBM = pl.BlockSpec(memory_space=pltpu.HBM)
SEM = pl.BlockSpec(memory_space=pltpu.SEMAPHORE)
TILE_DEVS = FF_TILE // FF_COLS
GU_SHAPE = (2, 2, TILE_DEVS, FF_COLS, PACK_W)


def _gate_slab(d):
    return (d // TILE_DEVS, 0, d % TILE_DEVS)


def _up_slab(d):
    return (d // TILE_DEVS, 1, d % TILE_DEVS)
D_SHAPE = (N_DEV, FF_COLS, PACK_W)
LAND_SHAPE = (N_DEV, LATE_ROWS, PACK_W)


def _split_params():
    return pltpu.CompilerParams(has_side_effects=pltpu.SideEffectType.DATAFLOW_SIDE_EFFECTING)


def _peer(x, y, c, k):
    return ((1 - x) if k & 4 else x, (1 - y) if k & 2 else y, (1 - c) if k & 1 else c)


def _empty_hbm(shape, dtype):
    return pltpu.with_memory_space_constraint(lax.empty(shape, dtype), pltpu.HBM)


def _wait_all(rows, send_sems, recv_sems, me):
    for k in range(N_DEV - 1):
        cp = pltpu.make_async_remote_copy(src_ref=rows, dst_ref=rows, send_sem=send_sems.at[k], recv_sem=recv_sems.at[k],
                                          device_id=me, device_id_type=MESH)
        cp.wait_send()
        cp.wait_recv()


def _token_shape():
    return jax.ShapeDtypeStruct((SUBLANES, LANES), F32)


def _gather_start(pack, row0, pieces, shapes, *, name):
    n = len(shapes)

    def body(*refs):
        p_ref, bufs, send_sems, recv_sems, token = refs[0], refs[1:1 + n], refs[1 + n], refs[2 + n], refs[-1]
        x, y, c = _position()
        me = 4 * x + 2 * y + c
        for k in range(1, N_DEV):
            off = row0
            for buf, lead, rows in pieces:
                pltpu.make_async_remote_copy(
                    src_ref=p_ref.at[pl.ds(off, rows)], dst_ref=bufs[buf].at[lead(me)], send_sem=send_sems.at[k - 1],
                    recv_sem=recv_sems.at[k - 1], device_id=_peer(x, y, c, k), device_id_type=MESH).start()
                off += rows
        token[...] = jnp.zeros_like(token)

    sems, dt = pltpu.SemaphoreType.DMA((N_DEV - 1,)), pack.dtype
    return pl.pallas_call(
        body, name=name,
        out_shape=(sems, sems, pltpu.HBM(pack.shape, dt)) + tuple(pltpu.HBM(s, dt) for s in shapes) + (_token_shape(),),
        in_specs=(HBM,) * (1 + n), out_specs=(SEM, SEM) + (HBM,) * (1 + n) + (pl.BlockSpec(memory_space=pltpu.VMEM),),
        input_output_aliases={i: 2 + i for i in range(1 + n)}, compiler_params=_split_params(),
    )(pltpu.with_memory_space_constraint(pack, pltpu.HBM), *[_empty_hbm(s, dt) for s in shapes])


def _gather_wait(started, row0, n_rows, after, *, name):
    send_sems, recv_sems, pack, *bufs = started[:-1]
    n = len(bufs)

    def body(*refs):
        _wait_all(refs[0].at[pl.ds(row0, n_rows)], refs[1 + n], refs[2 + n], _position())

    outs = pl.pallas_call(
        body, name=name, out_shape=tuple(pltpu.HBM(a.shape, a.dtype) for a in (pack, *bufs)),
        in_specs=(HBM,) * (1 + n) + (SEM, SEM, ANY), out_specs=(HBM,) * (1 + n),
        input_output_aliases={i: i for i in range(1 + n)}, compiler_params=_split_params(),
    )(pack, *bufs, send_sems, recv_sems, after)
    return outs[0], outs[1:]


def _scatter_start(srcs, pieces, *, name):
    n = len(srcs)
    land_shape = (N_DEV, sum(p[2] for p in pieces), PACK_W)

    def body(*refs):
        src_refs, land_ref, send_sems, recv_sems, token = refs[:n], refs[n], refs[n + 1], refs[n + 2], refs[-1]
        x, y, c = _position()
        me = 4 * x + 2 * y + c
        for k in range(1, N_DEV):
            px, py, pc = _peer(x, y, c, k)
            off = 0
            for si, lead, rows in pieces:
                pltpu.make_async_remote_copy(
                    src_ref=src_refs[si].at[lead(4 * px + 2 * py + pc)], dst_ref=land_ref.at[me, pl.ds(off, rows)],
                    send_sem=send_sems.at[k - 1], recv_sem=recv_sems.at[k - 1], device_id=(px, py, pc),
                    device_id_type=MESH).start()
                off += rows
        token[...] = jnp.zeros_like(token)

    sems, dt = pltpu.SemaphoreType.DMA((N_DEV - 1,)), srcs[0].dtype
    return pl.pallas_call(
        body, name=name,
        out_shape=(sems, sems) + tuple(pltpu.HBM(a.shape, dt) for a in srcs) + (pltpu.HBM(land_shape, dt), _token_shape()),
        in_specs=(HBM,) * (n + 1), out_specs=(SEM, SEM) + (HBM,) * (n + 1) + (pl.BlockSpec(memory_space=pltpu.VMEM),),
        input_output_aliases={i: 2 + i for i in range(n + 1)}, compiler_params=_split_params(),
    )(*[pltpu.with_memory_space_constraint(a, pltpu.HBM) for a in srcs], _empty_hbm(land_shape, dt))


def _scatter_wait(started, after, *, name):
    send_sems, recv_sems, *bufs = started[:-1]
    n = len(bufs)

    def body(*refs):
        _wait_all(refs[n - 1].at[0], refs[n], refs[n + 1], _position())

    return pl.pallas_call(
        body, name=name, out_shape=tuple(pltpu.HBM(a.shape, a.dtype) for a in bufs),
        in_specs=(HBM,) * n + (SEM, SEM, ANY), out_specs=(HBM,) * n, input_output_aliases={i: i for i in range(n)},
        compiler_params=_split_params(),
    )(*bufs, send_sems, recv_sems, after)


def _peer_sum(own, own_lead, land, block, rows, idx, *, name):
    owns = list(own) if isinstance(own, (list, tuple)) else [own]
    n, lead_rank = len(owns), owns[0].ndim - 2

    def body(idx_ref, *refs):
        own_refs, land_refs, o_ref = refs[:n], refs[n:n + N_DEV - 1], refs[n + N_DEV - 1]
        for j in range(n):
            rs_ = slice(j * rows, (j + 1) * rows)
            acc = own_refs[j][(0,) * lead_rank].astype(F32)
            for k in range(N_DEV - 1):
                acc = acc + land_refs[k][0, rs_].astype(F32)
            o_ref[rs_] = acc

    own_spec = pl.BlockSpec((1,) * lead_rank + (rows, PACK_W), lambda i, t: own_lead(t[0]) + (0, 0))

    def land_spec(k):
        return pl.BlockSpec((1, n * rows, PACK_W), lambda i, t: (t[k + 1], block, 0))

    return pl.pallas_call(
        body, name=name,
        grid_spec=pltpu.PrefetchScalarGridSpec(
            num_scalar_prefetch=1, grid=(1,), in_specs=[own_spec] * n + [land_spec(k) for k in range(N_DEV - 1)],
            out_specs=pl.BlockSpec((n * rows, PACK_W), lambda i, t: (0, 0))),
        out_shape=jax.ShapeDtypeStruct((n * rows, PACK_W), F32), compiler_params=_cparams(("arbitrary",)),
    )(idx, *owns, *([land] * (N_DEV - 1)))


def _sum_adamw(own, own_lead, land, block, rows, idx, w, m, v, *, name):
    lead_rank, r = own.ndim - 2, w.shape[1]

    def body(idx_ref, own_ref, *refs):
        land_refs, (w_ref, m_ref, v_ref), outs = refs[:N_DEV - 1], refs[N_DEV - 1:N_DEV + 2], refs[N_DEV + 2:]
        g = own_ref[(0,) * lead_rank + (slice(0, r),)].astype(F32)
        for k in range(N_DEV - 1):
            g = g + land_refs[k][0, 0:r].astype(F32)
        for o_ref, val in zip(outs, (g,) + tuple(_adamw(w_ref[0], g, m_ref[0], v_ref[0]))):
            o_ref[0] = val

    own_spec = pl.BlockSpec((1,) * lead_rank + (rows, PACK_W), lambda i, t: own_lead(t[0]) + (0, 0))
    shard = pl.BlockSpec((1, r, PACK_W), lambda i, t: (0, 0, 0))

    def land_spec(k):
        return pl.BlockSpec((1, rows, PACK_W), lambda i, t: (t[k + 1], block, 0))

    return pl.pallas_call(
        body, name=name,
        grid_spec=pltpu.PrefetchScalarGridSpec(
            num_scalar_prefetch=1, grid=(1,),
            in_specs=[own_spec] + [land_spec(k) for k in range(N_DEV - 1)] + [shard] * 3, out_specs=[shard] * 4),
        out_shape=[jax.ShapeDtypeStruct((1, r, PACK_W), F32)] * 4, compiler_params=_cparams(("arbitrary",)),
    )(idx, own, *([land] * (N_DEV - 1)), w, m, v)


def _adamw(w, g, m, v):
    m = ADAM_B1 * m + (1.0 - ADAM_B1) * g
    v = ADAM_B2 * v + (1.0 - ADAM_B2) * (g * g)
    m_hat = m / (1.0 - ADAM_B1 ** ADAM_STEP)
    v_hat = v / (1.0 - ADAM_B2 ** ADAM_STEP)
    delta = -ADAM_LR * (m_hat / (jnp.sqrt(v_hat) + ADAM_EPS) + ADAM_WD * w)
    return delta, m, v


def _adamw_call(w, g, m, v, *, name, max_rows=256):
    _, r, c_ = w.shape
    tr = max_rows if r > max_rows and r % max_rows == 0 else r

    def body(w_ref, g_ref, m_ref, v_ref, d_ref, mo_ref, vo_ref):
        d, mn, vn = _adamw(w_ref[0], g_ref[...], m_ref[0], v_ref[0])
        d_ref[0] = d
        mo_ref[0] = mn
        vo_ref[0] = vn

    row3 = pl.BlockSpec((1, tr, c_), lambda i: (0, i, 0))
    shp = jax.ShapeDtypeStruct((1, r, c_), F32)
    return pl.pallas_call(
        body, name=name, grid=(r // tr,), in_specs=[row3, pl.BlockSpec((tr, c_), lambda i: (i, 0)), row3, row3],
        out_specs=[row3] * 3, out_shape=[shp] * 3, compiler_params=_cparams(("parallel",)),
    )(w, g, m, v)


SMALL = ("mix_norm_g", "ffn_norm_g", "final_norm_g", "q_norm_g", "kv_norm_g", "swa_sinks")
SMALL_W = dict(mix_norm_g=1024, ffn_norm_g=1024, final_norm_g=1024, q_norm_g=Q_LORA, kv_norm_g=KV_LORA, swa_sinks=SWA_HEADS)


def _small_adamw(parts, w, m, v):
    ns = len(SMALL)

    def body(p_ref, *refs):
        ins, outs = refs[:3 * ns], refs[3 * ns:]
        tot = p_ref[0]
        for dev in range(1, N_DEV):
            tot = tot + p_ref[dev]
        for k, n in enumerate(SMALL):
            g = jnp.sum(tot[k * SUBLANES:(k + 1) * SUBLANES, :SMALL_W[n]], axis=0, keepdims=True)
            res = _adamw(ins[k][...], g, ins[ns + k][...], ins[2 * ns + k][...])
            for j, r in enumerate((g,) + tuple(res)):
                outs[j * ns + k][...] = r
        outs[4 * ns][...] = jnp.sum(tot[ns * SUBLANES:(ns + 1) * SUBLANES, 0:1], axis=0, keepdims=True)

    shapes = [jax.ShapeDtypeStruct((1, SMALL_W[n]), F32) for n in SMALL]
    vm = pl.BlockSpec(memory_space=pltpu.VMEM)
    out = pl.pallas_call(
        body, name="small_adamw", in_specs=[vm] * (1 + 3 * ns), out_specs=[vm] * (4 * ns + 1),
        out_shape=shapes * 4 + [jax.ShapeDtypeStruct((1, 1), F32)],
    )(parts, *[d[n] for d in (w, m, v) for n in SMALL])
    return [dict(zip(SMALL, out[j * ns:(j + 1) * ns])) for j in range(4)] + [out[4 * ns]]


def _small_pack(d, rows_each):
    parts = [jnp.pad(d[n].astype(F32), ((0, 0), (0, PACK_W - SMALL_W[n]))) for n in SMALL]
    out = jnp.concatenate(parts, 0)
    pad = -out.shape[0] % SUBLANES
    return jnp.pad(out, ((0, pad), (0, 0)))


def kernel(x, mix_norm_g, w_in, swa_sinks, q_norm_g, w_uq, kv_norm_g, w_ukv, w_o_swa, w_o_mla, w_out, ffn_norm_g, w_gate, w_up, w_down, final_norm_g, loss_target, m_mix_norm_g, m_w_in, m_swa_sinks, m_q_norm_g, m_w_uq, m_kv_norm_g, m_w_ukv, m_w_o_swa, m_w_o_mla, m_w_out, m_ffn_norm_g, m_w_gate, m_w_up, m_w_down, m_final_norm_g, v_mix_norm_g, v_w_in, v_swa_sinks, v_q_norm_g, v_w_uq, v_kv_norm_g, v_w_ukv, v_w_o_swa, v_w_o_mla, v_w_out, v_ffn_norm_g, v_w_gate, v_w_up, v_w_down, v_final_norm_g):
    big_w = dict(w_in=w_in[0], w_uq=w_uq[0], w_ukv=w_ukv[0], w_o_swa=w_o_swa[0], w_o_mla=w_o_mla[0], w_out=w_out[0],
                 w_gate=w_gate[0], w_up=w_up[0], w_down=w_down[0])
    big_w3 = dict(w_in=w_in, w_uq=w_uq, w_ukv=w_ukv, w_o_swa=w_o_swa, w_o_mla=w_o_mla, w_out=w_out, w_gate=w_gate, w_up=w_up,
                  w_down=w_down)
    big_m = dict(w_in=m_w_in, w_uq=m_w_uq, w_ukv=m_w_ukv, w_o_swa=m_w_o_swa, w_o_mla=m_w_o_mla, w_out=m_w_out,
                 w_gate=m_w_gate, w_up=m_w_up, w_down=m_w_down)
    big_v = dict(w_in=v_w_in, w_uq=v_w_uq, w_ukv=v_w_ukv, w_o_swa=v_w_o_swa, w_o_mla=v_w_o_mla, w_out=v_w_out,
                 w_gate=v_w_gate, w_up=v_w_up, w_down=v_w_down)
    small_w = dict(mix_norm_g=mix_norm_g, ffn_norm_g=ffn_norm_g, final_norm_g=final_norm_g.reshape(1, D_MODEL),
                   q_norm_g=q_norm_g, kv_norm_g=kv_norm_g, swa_sinks=swa_sinks)
    small_m = dict(mix_norm_g=m_mix_norm_g, ffn_norm_g=m_ffn_norm_g, final_norm_g=m_final_norm_g.reshape(1, D_MODEL),
                   q_norm_g=m_q_norm_g, kv_norm_g=m_kv_norm_g, swa_sinks=m_swa_sinks)
    small_v = dict(mix_norm_g=v_mix_norm_g, ffn_norm_g=v_ffn_norm_g, final_norm_g=v_final_norm_g.reshape(1, D_MODEL),
                   q_norm_g=v_q_norm_g, kv_norm_g=v_kv_norm_g, swa_sinks=v_swa_sinks)

    px, py, pc = _position()
    me = 4 * px + 2 * py + pc
    idx = jnp.stack([me] + [4 * qx + 2 * qy + qc for qx, qy, qc in (_peer(px, py, pc, k) for k in range(1, N_DEV))])
    idx = idx.astype(jnp.int32)

    dev = lambda d: (d,)
    pack = _wire_pack(big_w, WIRE_DTYPE)
    win_g, = _all_gather(pack, ((0, dev, 0, W_IN_ROWS),), ((N_DEV, W_IN_ROWS, PACK_W),), name="ag_early")
    mid_pieces = tuple((b, dev, OUT_ROWS) for b in range(MID_BLOCKS))
    ag_mid = _gather_start(pack, W_IN_ROWS, mid_pieces, ((N_DEV, OUT_ROWS, PACK_W),) * MID_BLOCKS, name="ag_mid_start")
    ag = {}

    def own_rows(r0, r1, shape):
        return pack[r0:r1].reshape(shape)

    def mid_weights(after):
        pack_mid, blocks = _gather_wait(ag_mid, W_IN_ROWS, MID_ROWS, after, name="ag_mid_wait")
        ag["late"] = _gather_start(pack_mid, EARLY_ROWS, ((0, _gate_slab, FF_COLS), (0, _up_slab, FF_COLS), (1, dev, FF_COLS)),
                                   (GU_SHAPE, D_SHAPE), name="ag_late_start")
        row0 = lambda b: W_IN_ROWS + b * OUT_ROWS
        ops = _mid_operands(*[lax.dynamic_update_slice(blk, own_rows(row0(b), row0(b + 1), (1, OUT_ROWS, PACK_W)), (me, 0, 0))
                              for b, blk in enumerate(blocks)])
        ops["wuq"] = ops["wuq"] + ag["late"][-1][0:1, 0:1].astype(ops["wuq"].dtype)
        return ops

    def late_weights(after):
        _, (gu, d) = _gather_wait(ag["late"], EARLY_ROWS, LATE_ROWS, after, name="ag_late_wait")
        slab = (1, 1, 1, FF_COLS, PACK_W)
        gu = lax.dynamic_update_slice(gu, own_rows(EARLY_ROWS, EARLY_ROWS + FF_COLS, slab), _gate_slab(me) + (0, 0))
        gu = lax.dynamic_update_slice(gu, own_rows(EARLY_ROWS + FF_COLS, EARLY_ROWS + 2 * FF_COLS, slab), _up_slab(me) + (0, 0))
        d = lax.dynamic_update_slice(d, own_rows(EARLY_ROWS + 2 * FF_COLS, PACK_ROWS, (1, FF_COLS, PACK_W)), (me, 0, 0))
        return gu.reshape(2 * D_FF, D_MODEL), d.reshape(D_FF, D_MODEL)

    rs = {}

    def late_grads(g_gu, g_d):
        rs["late"] = _scatter_start([g_gu.reshape(GU_SHAPE), g_d.reshape(D_SHAPE)],
                                    ((0, _gate_slab, FF_COLS), (0, _up_slab, FF_COLS), (1, dev, FF_COLS)),
                                    name="rs_late_start")
        return rs["late"][-1]

    def mid_grads(g):
        rs["mid"] = _scatter_start(_mid_grad_pack(g), mid_pieces, name="rs_mid_start")
        return rs["mid"][-1]

    def last_grads(g_win_t):
        rs["last"] = _scatter_start([_w_in_grad_chunks(g_win_t)], ((0, dev, W_IN_ROWS),), name="rs_last_start")
        return rs["last"][-1]

    first_w = dict(small_w, mix_norm_g=mix_norm_g + ag_mid[-1][0:1, 0:1])
    loss_tot, gx, g_small = _local_step(
        x[0], loss_target[0], _w_in_operand(win_g), first_w, types.SimpleNamespace(mid=mid_weights, late=late_weights),
        types.SimpleNamespace(late=late_grads, mid=mid_grads, last=last_grads))

    loss_rows = jnp.pad(loss_tot[0:1, 0:1], ((0, SUBLANES - 1), (0, PACK_W - 1)))
    small_rows = jnp.concatenate([_small_pack(g_small_rows(g_small), SUBLANES), loss_rows], 0)
    n_small = small_rows.shape[0]
    ag_small = _gather_start(small_rows, 0, ((0, dev, n_small),), ((N_DEV, n_small, PACK_W),), name="ag_small_start")

    g_gu, g_d, land_late = _scatter_wait(rs["late"], ag_small[-1], name="rs_late_wait")
    *g_mid, land_mid = _scatter_wait(rs["mid"], ag_small[-1], name="rs_mid_wait")
    g_win, land_last = _scatter_wait(rs["last"], ag_small[-1], name="rs_last_wait")
    swap = lambda a: jnp.swapaxes(a, 1, 2)
    same = lambda a: a
    chunks = dict(w_gate=(swap, g_gu, _gate_slab, land_late, 0, FF_COLS), w_up=(swap, g_gu, _up_slab, land_late, 1, FF_COLS),
                  w_down=(same, g_d, dev, land_late, 2, FF_COLS), w_in=(swap, g_win, dev, land_last, 0, W_IN_ROWS))
    gw, dw, mw, vw = {}, {}, {}, {}
    for n, (view, own, lead, land, blk, rows) in chunks.items():
        res = _sum_adamw(own, lead, land, blk, rows, idx, view(big_w3[n]), view(big_m[n]), view(big_v[n]), name="adamw_" + n)
        gw[n], dw[n], mw[n], vw[n] = (view(r) for r in res)
    g_nat = _mid_unpack(_peer_sum(g_mid, dev, land_mid, 0, OUT_ROWS, idx, name="rs_sum_mid"))
    for n, g in g_nat.items():
        gw[n] = g[None]
        dw[n], mw[n], vw[n] = _adamw_call(big_w3[n], g, big_m[n], big_v[n], name="adamw_" + n)

    own_small, (parts,) = _gather_wait(ag_small, 0, n_small, vw[n], name="ag_small_wait")
    parts = lax.dynamic_update_slice(parts, own_small[None], (me, 0, 0))
    gs, ds, ms, vs, loss = _small_adamw(parts, small_w, small_m, small_v)
    loss = loss[0, 0]
    for d in (gs, ds, ms, vs):
        d["final_norm_g"] = d["final_norm_g"].reshape(D_MODEL)

    order = ("mix_norm_g", "w_in", "swa_sinks", "q_norm_g", "w_uq", "kv_norm_g", "w_ukv", "w_o_swa", "w_o_mla", "w_out",
             "ffn_norm_g", "w_gate", "w_up", "w_down", "final_norm_g")

    def leaves(big, small):
        return [big[n] if n in big else small[n] for n in order]

    return (loss, gx[None], *leaves(gw, gs), *leaves(dw, ds), *leaves(mw, ms), *leaves(vw, vs))


def g_small_rows(g_small):
    out = dict(g_small)
    out["swa_sinks"] = jnp.pad(g_small["swa_sinks"], ((0, SUBLANES - 1), (0, 0)))
    return out
```

```python
import types

import numpy as np
import jax
import jax.numpy as jnp
from jax import lax
from jax.experimental import pallas as pl
from jax.experimental.pallas import tpu as pltpu

F32 = jnp.float32
MXU_DTYPE = jnp.bfloat16
WIRE_DTYPE = jnp.bfloat16

D_MODEL = 1024
EPS = 1e-6
ROPE_THETA = 10000.0
BLOCK = 128
HEAD_DIM = 64
SWA_HEADS = 8
SWA_KV_HEADS = 2
SWA_GROUP = SWA_HEADS // SWA_KV_HEADS
MLA_HEADS = 8
MLA_NOPE = 64
MLA_ROPE = 32
MLA_V = 64
MLA_QK = MLA_NOPE + MLA_ROPE
Q_LORA = 384
KV_LORA = 256
D_FF = 2816
IN_SIZES = (512, 128, 128, Q_LORA, KV_LORA, MLA_ROPE, D_MODEL, D_MODEL)
IN_OFF = tuple(int(v) for v in np.cumsum((0,) + IN_SIZES))
ADAM_LR, ADAM_B1, ADAM_B2, ADAM_EPS, ADAM_WD, ADAM_STEP = 0.001, 0.9, 0.999, 1e-08, 0.01, 10

LANES = 128
SUBLANES = 8
VMEM_LIMIT = 48 * 1024 * 1024
N_DEV = 8

P_GA, P_GB, P_Q, P_QLAT, P_KR, P_K, P_V, P_KVLAT, P_W = 0, 1024, 2048, 3072, 3456, 3584, 3840, 4096, 4352
KR_LANE = 64

LOG2E = 1.4426950408889634

NT = (((1,), (1,)), ((), ()))
NN = (((1,), (0,)), ((), ()))
TN = (((0,), (0,)), ((), ()))


def _cparams(sem):
    return pltpu.CompilerParams(dimension_semantics=sem, vmem_limit_bytes=VMEM_LIMIT)


def _mm(a, b, mode, *, name, out_dtype=F32, add=None, tm=512, tn=512, tk=None):
    if mode == "nn":
        (M, K), (K2, N) = a.shape, b.shape
    elif mode == "nt":
        (M, K), (N, K2) = a.shape, b.shape
    else:
        (K, M), (K2, N) = a.shape, b.shape
    assert K == K2, (a.shape, b.shape, mode)
    tm, tn, tk = min(tm, M), min(tn, N), K if tk is None else min(tk, K)
    assert M % tm == 0 and N % tn == 0 and K % tk == 0, (M, N, K, tm, tn, tk)
    nk = K // tk
    dn = {"nn": NN, "nt": NT, "tn": TN}[mode]
    if mode == "tn":
        a_spec = pl.BlockSpec((tk, tm), lambda i, j, k: (k, i))
    else:
        a_spec = pl.BlockSpec((tm, tk), lambda i, j, k: (i, k))
    once = dict(pipeline_mode=pl.Buffered(1)) if (nk == 1 and tn == N) else {}
    if mode == "nt":
        b_spec = pl.BlockSpec((tn, tk), lambda i, j, k: (j, k), **once)
    else:
        b_spec = pl.BlockSpec((tk, tn), lambda i, j, k: (k, j), **once)
    o_spec = pl.BlockSpec((tm, tn), lambda i, j, k: (i, j))
    has_add = add is not None

    def body(*refs):
        a_ref, b_ref = refs[0], refs[1]
        add_ref = refs[2] if has_add else None
        o_ref = refs[2 + has_add]
        p = lax.dot_general(a_ref[...], b_ref[...], dn, preferred_element_type=F32)

        def finish(acc):
            if has_add:
                acc = acc + add_ref[...]
            o_ref[...] = acc.astype(o_ref.dtype)

        if nk == 1:
            finish(p)
        else:
            acc_ref = refs[-1]
            k = pl.program_id(2)

            @pl.when(k == 0)
            def _():
                acc_ref[...] = p

            @pl.when((k > 0) & (k < nk - 1))
            def _():
                acc_ref[...] += p

            @pl.when(k == nk - 1)
            def _():
                finish(acc_ref[...] + p)

    ins = [a, b] + ([add] if has_add else [])
    return pl.pallas_call(
        body, name=name, grid=(M // tm, N // tn, nk), in_specs=[a_spec, b_spec] + ([o_spec] if has_add else []), out_specs=o_spec,
        out_shape=jax.ShapeDtypeStruct((M, N), out_dtype),
        scratch_shapes=[pltpu.VMEM((tm, tn), F32)] if nk > 1 else [],
        compiler_params=_cparams(("parallel", "parallel", "arbitrary")),
    )(*ins)


def _rows(ts, w, cb=0):
    return pl.BlockSpec((ts, w), lambda i: (i, cb))


def _const(r, w):
    return pl.BlockSpec((r, w), lambda i: (0, 0))


def _sublane_sum(v):
    ts, c = v.shape
    return jnp.sum(v.reshape(ts // SUBLANES, SUBLANES, c), axis=0)


def _sigmoid(v):
    return 1.0 / (1.0 + jnp.exp(-v))


def _rope(v, cos, s_up, s_dn, up, dn):
    return v * cos + pltpu.roll(v, up, 1) * s_up + pltpu.roll(v, dn, 1) * s_dn


def _rope_t(dv, cos, s_up, s_dn, up, dn):
    return dv * cos + pltpu.roll(dv * s_up, dn, 1) + pltpu.roll(dv * s_dn, up, 1)


def _rope_tables(seq):
    pos = np.arange(seq, dtype=np.float32)[:, None]

    def base(dim):
        inv = np.float32(ROPE_THETA) ** (-np.arange(0, dim, 2, dtype=np.float32) / np.float32(dim))
        ang = (pos * inv.astype(np.float32)[None, :]).astype(np.float32)
        return np.cos(ang).astype(np.float32), np.sin(ang).astype(np.float32)

    z = lambda n: np.zeros((seq, n), np.float32)
    ca, sa = base(HEAD_DIM)
    a_cos = np.concatenate([ca, ca, z(64)], 1)
    a_up = np.concatenate([-sa, z(96)], 1)
    a_dn = np.concatenate([z(32), sa, z(64)], 1)
    cb, sb = base(MLA_ROPE)
    one = np.ones((seq, 64), np.float32)
    q_cos = np.concatenate([one, cb, cb, z(32)], 1)
    k_cos = np.concatenate([z(64), cb, cb, z(32)], 1)
    b_up = np.concatenate([z(64), -sb, z(48)], 1)
    b_dn = np.concatenate([z(80), sb, z(32)], 1)
    return tuple(jnp.asarray(t) for t in (a_cos, a_up, a_dn, q_cos, k_cos, b_up, b_dn))


def _rms(v, g):
    return v * lax.rsqrt(jnp.mean(v * v, axis=-1, keepdims=True) + EPS) * g


def _rms_bwd(v, g, d):
    r = lax.rsqrt(jnp.mean(v * v, axis=-1, keepdims=True) + EPS)
    xh = v * r
    dxh = d * g
    return r * (dxh - xh * jnp.mean(dxh * xh, axis=-1, keepdims=True)), d * xh


F_GA, F_GB, F_KVLAT, F_QLAT, F_W = 0, 1024, 2048, 2304, 2688


def _proj_in(x, g, w_t, gq, gkv, tabs, *, tm=512):
    s_, c = x.shape
    a_cos, a_up, a_dn, _, k_cos, b_up, b_dn = tabs

    def body(x_ref, g_ref, w_ref, gq_ref, gkv_ref, ac, au, ad, kc, bu, bd,
             h_ref, qa_ref, ka_ref, va_ref, cq_ref, ckv_ref, kro_ref, pf_ref):
        h = _rms(x_ref[...], g_ref[...]).astype(h_ref.dtype)
        h_ref[...] = h
        mm = lambda a, b: lax.dot_general(h, w_ref[a:b, :], NT, preferred_element_type=F32)
        pf_ref[:, F_GA:F_KVLAT] = mm(P_GA, P_Q)
        c_, u_, d_ = ac[...], au[...], ad[...]
        q = mm(P_Q, P_QLAT)
        for hd in range(SWA_HEADS):
            sl = slice(hd * LANES, (hd + 1) * LANES)
            qa_ref[:, sl] = _rope(q[:, sl], c_, u_, d_, 96, 32).astype(qa_ref.dtype)
        kv = mm(P_KR, P_KVLAT)
        kro_ref[...] = _rope(kv[:, :LANES], kc[...], bu[...], bd[...], 112, 16)
        for hd in range(SWA_KV_HEADS):
            sl = slice((1 + hd) * LANES, (2 + hd) * LANES)
            ka_ref[:, hd * LANES:(hd + 1) * LANES] = _rope(kv[:, sl], c_, u_, d_, 96, 32).astype(ka_ref.dtype)
        va_ref[...] = kv[:, P_V - P_KR:].astype(va_ref.dtype)
        for a, b, f0, gref, dst in ((P_QLAT, P_KR, F_QLAT, gq_ref, cq_ref), (P_KVLAT, P_W, F_KVLAT, gkv_ref, ckv_ref)):
            v = mm(a, b)
            pf_ref[:, f0:f0 + b - a] = v
            r = lax.rsqrt(jnp.mean(v * v, axis=-1, keepdims=True) + EPS)
            dst[...] = (v * r * gref[...]).astype(dst.dtype)

    tab = _rows(tm, LANES)
    widths = (c, SWA_HEADS * LANES, SWA_KV_HEADS * LANES, SWA_KV_HEADS * LANES, Q_LORA, KV_LORA)
    return pl.pallas_call(
        body, name="proj_in", grid=(s_ // tm,),
        in_specs=[_rows(tm, c), _const(1, c), pl.BlockSpec((P_W, c), lambda i: (0, 0), pipeline_mode=pl.Buffered(1)),
                  _const(1, Q_LORA), _const(1, KV_LORA), tab, tab, tab, tab, tab, tab],
        out_specs=[_rows(tm, w) for w in widths] + [tab, _rows(tm, F_W)],
        out_shape=[jax.ShapeDtypeStruct((s_, w), MXU_DTYPE) for w in widths]
        + [jax.ShapeDtypeStruct((s_, LANES), F32), jax.ShapeDtypeStruct((s_, F_W), F32)],
        compiler_params=_cparams(("parallel",)),
    )(x, g, w_t, gq, gkv, a_cos, a_up, a_dn, k_cos, b_up, b_dn)


def _mm_norm_bwd(a, b, x, g, res, *, name, after=None, tm=512):
    s_, kk = a.shape
    c = b.shape[1]
    has_after = after is not None
    n, slots = s_ // tm, 3

    def body(*refs):
        a_hbm, b_ref, x_ref, g_ref, res_ref = refs[:5]
        dx_ref, dxb_ref, dg_ref, ring, sems = refs[5 + has_after:]
        s = pl.program_id(0)

        def fetch(step):
            return pltpu.make_async_copy(a_hbm.at[pl.ds(pl.multiple_of(step * tm, tm), tm)], ring.at[step % slots],
                                         sems.at[step % slots])

        @pl.when(s == 0)
        def _():
            for ahead in range(min(slots - 1, n)):
                fetch(ahead).start()

        @pl.when(s + slots - 1 < n)
        def _():
            fetch(s + slots - 1).start()

        fetch(s).wait()
        d = jnp.dot(ring[s % slots], b_ref[...], preferred_element_type=F32)
        dx, gg = _rms_bwd(x_ref[...], g_ref[...], d)
        dx = dx + res_ref[...]
        dx_ref[...] = dx
        dxb_ref[...] = dx.astype(dxb_ref.dtype)

        @pl.when(pl.program_id(0) == 0)
        def _():
            dg_ref[...] = jnp.zeros(dg_ref.shape, F32)

        dg_ref[...] += _sublane_sum(gg)

    row = _rows(tm, c)
    any_ = pl.BlockSpec(memory_space=pl.ANY)
    in_specs = [any_, pl.BlockSpec((kk, c), lambda i: (0, 0), pipeline_mode=pl.Buffered(1)), row, _const(1, c), row]
    return pl.pallas_call(
        body, name=name, grid=(n,), in_specs=in_specs + ([any_] if has_after else []),
        out_specs=[row, row, _const(SUBLANES, c)],
        out_shape=[jax.ShapeDtypeStruct((s_, c), F32), jax.ShapeDtypeStruct((s_, c), MXU_DTYPE),
                   jax.ShapeDtypeStruct((SUBLANES, c), F32)],
        scratch_shapes=[pltpu.VMEM((slots, tm, kk), a.dtype), pltpu.SemaphoreType.DMA((slots,))],
        compiler_params=_cparams(("arbitrary",)),
    )(*([a, b, x, g, res] + ([after] if has_after else [])))


def _mla_up(cq, ckv, kro, wuq, wuk, wuv, tabs, *, ts=512):
    s_ = cq.shape[0]
    _, _, _, q_cos, _, b_up, b_dn = tabs

    def body(cq_ref, ckv_ref, kr_ref, wq_ref, wk_ref, wv_ref, qc, bu, bd, qo_ref, ko_ref, vo_ref):
        c_, u_, d_ = qc[...], bu[...], bd[...]
        kr = kr_ref[...]
        ckv_ = ckv_ref[...]
        vo_ref[...] = jnp.dot(ckv_, wv_ref[...], preferred_element_type=F32).astype(vo_ref.dtype)
        q = jnp.dot(cq_ref[...], wq_ref[...], preferred_element_type=F32)
        k = jnp.dot(ckv_, wk_ref[...], preferred_element_type=F32)
        for h in range(MLA_HEADS):
            sl = slice(h * LANES, (h + 1) * LANES)
            qo_ref[:, sl] = _rope(q[:, sl], c_, u_, d_, 112, 16).astype(qo_ref.dtype)
            ko_ref[:, sl] = (k[:, sl] + kr).astype(ko_ref.dtype)

    tab, out = _rows(ts, LANES), _rows(ts, 1024)
    return pl.pallas_call(
        body, name="mla_up", grid=(s_ // ts,),
        in_specs=[_rows(ts, Q_LORA), _rows(ts, KV_LORA), tab, _const(Q_LORA, 1024), _const(KV_LORA, 1024),
                  _const(KV_LORA, 1024), tab, tab, tab],
        out_specs=[out, out, out], out_shape=[jax.ShapeDtypeStruct((s_, 1024), MXU_DTYPE)] * 3,
        compiler_params=_cparams(("parallel",)),
    )(cq, ckv, kro, wuq, wuk, wuv, q_cos, b_up, b_dn)


def _mla_up_bwd(dqc, dkc, dvp, wuq, wukv, p, gq, gkv, tabs, *, ts=512):
    s_ = dqc.shape[0]
    _, _, _, q_cos, k_cos, b_up, b_dn = tabs

    def body(dq_ref, dk_ref, dv_ref, wq_ref, wkv_ref, ql_ref, kvl_ref, gq_ref, gkv_ref, qc, kc, bu, bd,
             dqo_ref, dkvo_ref, dkr_ref, dql_ref, dkvl_ref, dgq_ref, dgkv_ref):
        c_, u_, d_ = qc[...], bu[...], bd[...]
        tot = jnp.zeros((ts, LANES), F32)
        for h in range(MLA_HEADS):
            sl = slice(h * LANES, (h + 1) * LANES)
            dqo_ref[:, sl] = _rope_t(dq_ref[:, sl], c_, u_, d_, 112, 16).astype(dqo_ref.dtype)
            dk = dk_ref[:, sl]
            dkvo_ref[:, sl] = dk.astype(dkvo_ref.dtype)
            tot = tot + dk
        dkvo_ref[:, 1024:2048] = dv_ref[...].astype(dkvo_ref.dtype)
        dkr_ref[...] = _rope_t(tot, kc[...], u_, d_, 112, 16).astype(dkr_ref.dtype)

        @pl.when(pl.program_id(0) == 0)
        def _():
            dgq_ref[...] = jnp.zeros(dgq_ref.shape, F32)
            dgkv_ref[...] = jnp.zeros(dgkv_ref.shape, F32)

        for do_ref, w_ref, x_ref, g_ref, dx_ref, dg_ref in ((dqo_ref, wq_ref, ql_ref, gq_ref, dql_ref, dgq_ref),
                                                            (dkvo_ref, wkv_ref, kvl_ref, gkv_ref, dkvl_ref, dgkv_ref)):
            d = lax.dot_general(do_ref[...], w_ref[...], NT, preferred_element_type=F32)
            dx, gg = _rms_bwd(x_ref[...], g_ref[...], d)
            dx_ref[...] = dx.astype(dx_ref.dtype)
            dg_ref[...] += _sublane_sum(gg)

    tab = _rows(ts, LANES)
    return pl.pallas_call(
        body, name="mla_up_bwd", grid=(s_ // ts,),
        in_specs=[_rows(ts, 1024), _rows(ts, 1024), _rows(ts, 1024), _const(Q_LORA, 1024), _const(KV_LORA, 2048),
                  _rows(ts, Q_LORA, F_QLAT // Q_LORA), _rows(ts, KV_LORA, F_KVLAT // KV_LORA),
                  _const(1, Q_LORA), _const(1, KV_LORA), tab, tab, tab, tab],
        out_specs=[_rows(ts, 1024), _rows(ts, 2048), _rows(ts, LANES), _rows(ts, Q_LORA), _rows(ts, KV_LORA),
                   _const(SUBLANES, Q_LORA), _const(SUBLANES, KV_LORA)],
        out_shape=[jax.ShapeDtypeStruct((s_, 1024), MXU_DTYPE), jax.ShapeDtypeStruct((s_, 2048), MXU_DTYPE),
                   jax.ShapeDtypeStruct((s_, LANES), MXU_DTYPE), jax.ShapeDtypeStruct((s_, Q_LORA), MXU_DTYPE),
                   jax.ShapeDtypeStruct((s_, KV_LORA), MXU_DTYPE), jax.ShapeDtypeStruct((SUBLANES, Q_LORA), F32),
                   jax.ShapeDtypeStruct((SUBLANES, KV_LORA), F32)],
        compiler_params=_cparams(("arbitrary",)),
    )(dqc, dkc, dvp, wuq, wukv, p, p, gq, gkv, q_cos, k_cos, b_up, b_dn)


def _assemble_dp(dgab, dqa, dqlat, dkr, dka, dva, dkvlat, tabs, *, ts=512):
    s_ = dqa.shape[0]
    a_cos, a_up, a_dn = tabs[0], tabs[1], tabs[2]

    def body(dg_ref, dq_ref, dql_ref, dkr_ref, dk_ref, dv_ref, dkvl_ref, ac, au, ad, o_ref):
        c_, u_, d_ = ac[...], au[...], ad[...]
        o_ref[:, P_GA:P_Q] = dg_ref[...]
        for h in range(SWA_HEADS):
            sl = slice(h * LANES, (h + 1) * LANES)
            o_ref[:, P_Q + h * LANES:P_Q + (h + 1) * LANES] = _rope_t(dq_ref[:, sl], c_, u_, d_, 96, 32).astype(o_ref.dtype)
        o_ref[:, P_QLAT:P_KR] = dql_ref[...]
        o_ref[:, P_KR:P_K] = dkr_ref[...]
        for h in range(SWA_KV_HEADS):
            sl = slice(h * LANES, (h + 1) * LANES)
            o_ref[:, P_K + h * LANES:P_K + (h + 1) * LANES] = _rope_t(dk_ref[:, sl], c_, u_, d_, 96, 32).astype(o_ref.dtype)
        o_ref[:, P_V:P_KVLAT] = dv_ref[...]
        o_ref[:, P_KVLAT:P_W] = dkvl_ref[...]

    tab = _rows(ts, LANES)
    return pl.pallas_call(
        body, name="assemble_dp", grid=(s_ // ts,),
        in_specs=[_rows(ts, 2048), _rows(ts, 1024), _rows(ts, Q_LORA), _rows(ts, LANES), _rows(ts, 256), _rows(ts, 256),
                  _rows(ts, KV_LORA), tab, tab, tab],
        out_specs=_rows(ts, P_W), out_shape=jax.ShapeDtypeStruct((s_, P_W), MXU_DTYPE),
        compiler_params=_cparams(("parallel",)),
    )(dgab, dqa, dqlat, dkr, dka, dva, dkvlat, a_cos, a_up, a_dn)


def _attn_out_gate(oa, ob, woa_t, wob_t, p, *, ts=512):
    s_ = p.shape[0]

    def body(oa_ref, ob_ref, wa_ref, wb_ref, ga_ref, gb_ref, y_ref):
        ta = lax.dot_general(oa_ref[...], wa_ref[...], NT, preferred_element_type=F32)
        tb = lax.dot_general(ob_ref[...], wb_ref[...], NT, preferred_element_type=F32)
        y_ref[...] = (_sigmoid(ga_ref[...]) * ta + _sigmoid(gb_ref[...]) * tb).astype(y_ref.dtype)

    w = _const(1024, 1024)
    return pl.pallas_call(
        body, name="attn_out_gate", grid=(s_ // ts,),
        in_specs=[_rows(ts, 1024), _rows(ts, 1024), w, w, _rows(ts, 1024, F_GA // 1024), _rows(ts, 1024, F_GB // 1024)],
        out_specs=_rows(ts, 1024), out_shape=jax.ShapeDtypeStruct((s_, 1024), MXU_DTYPE),
        compiler_params=_cparams(("parallel",)),
    )(oa, ob, woa_t, wob_t, p, p)


def _d_y_gate(dx1b, wout, p, oa, ob, woa_t, wob_t, *, ts=512):
    s_ = p.shape[0]

    def body(dx_ref, w_ref, ga_ref, gb_ref, oa_ref, ob_ref, wa_ref, wb_ref, dta_ref, dtb_ref, dg_ref):
        d = lax.dot_general(dx_ref[...], w_ref[...], NT, preferred_element_type=F32)
        sa, sb = _sigmoid(ga_ref[...]), _sigmoid(gb_ref[...])
        dta_ref[...] = (d * sa).astype(dta_ref.dtype)
        dtb_ref[...] = (d * sb).astype(dtb_ref.dtype)
        ta = lax.dot_general(oa_ref[...], wa_ref[...], NT, preferred_element_type=F32)
        dg_ref[:, 0:1024] = (d * ta * (sa * (1.0 - sa))).astype(dg_ref.dtype)
        tb = lax.dot_general(ob_ref[...], wb_ref[...], NT, preferred_element_type=F32)
        dg_ref[:, 1024:2048] = (d * tb * (sb * (1.0 - sb))).astype(dg_ref.dtype)

    w = _const(1024, 1024)
    return pl.pallas_call(
        body, name="d_y_gate", grid=(s_ // ts,),
        in_specs=[_rows(ts, 1024), w, _rows(ts, 1024, F_GA // 1024), _rows(ts, 1024, F_GB // 1024),
                  _rows(ts, 1024), _rows(ts, 1024), w, w],
        out_specs=[_rows(ts, 1024), _rows(ts, 1024), _rows(ts, 2048)],
        out_shape=[jax.ShapeDtypeStruct((s_, 1024), MXU_DTYPE)] * 2 + [jax.ShapeDtypeStruct((s_, 2048), MXU_DTYPE)],
        compiler_params=_cparams(("parallel",)),
    )(dx1b, wout, p, p, oa, ob, woa_t, wob_t)


FF_TILE = D_FF // 2


def _ffn_in_act(x1, g, wgu_t, *, tm=512):
    s_ = x1.shape[0]
    n = s_ // tm

    def body(x_ref, g_ref, w_ref, h_ref, gu_ref, a_ref):
        h = _rms(x_ref[...], g_ref[...]).astype(h_ref.dtype)
        h_ref[...] = h
        p = lax.dot_general(h, w_ref[...], NT, preferred_element_type=F32)
        gu_ref[...] = p
        gate = p[:, :FF_TILE]
        a_ref[...] = (gate * _sigmoid(gate) * p[:, FF_TILE:]).astype(a_ref.dtype)

    return pl.pallas_call(
        body, name="ffn_in", grid=(2, s_ // tm),
        in_specs=[pl.BlockSpec((tm, D_MODEL), lambda j, i: (i, 0)), pl.BlockSpec((1, D_MODEL), lambda j, i: (0, 0)),
                  pl.BlockSpec((2 * FF_TILE, D_MODEL), lambda j, i: (j, 0))],
        out_specs=[pl.BlockSpec((tm, D_MODEL), lambda j, i: (i + j * (n - 1 - i), 0)),
                   pl.BlockSpec((tm, 2 * FF_TILE), lambda j, i: (i, j)),
                   pl.BlockSpec((tm, FF_TILE), lambda j, i: (i, j))],
        out_shape=[jax.ShapeDtypeStruct((s_, D_MODEL), MXU_DTYPE), jax.ShapeDtypeStruct((s_, 2 * D_FF), F32),
                   jax.ShapeDtypeStruct((s_, D_FF), MXU_DTYPE)],
        compiler_params=_cparams(("arbitrary", "arbitrary")),
    )(x1, g, wgu_t)


def _d_act_swiglu(dx2b, wd, gu, *, tm=512):
    s_ = dx2b.shape[0]
    n, slots = s_ // tm, 3
    steps = 2 * n

    def body(d_ref, w_ref, gu_hbm, o_ref, ring, sems):
        s = pl.program_id(0) * n + pl.program_id(1)

        def fetch(step):
            j, i = step // n, step % n
            src = gu_hbm.at[pl.ds(pl.multiple_of(i * tm, tm), tm), pl.ds(pl.multiple_of(j * 2 * FF_TILE, LANES), 2 * FF_TILE)]
            return pltpu.make_async_copy(src, ring.at[step % slots], sems.at[step % slots])

        @pl.when(s == 0)
        def _():
            for ahead in range(min(slots - 1, steps)):
                fetch(ahead).start()

        @pl.when(s + slots - 1 < steps)
        def _():
            fetch(s + slots - 1).start()

        fetch(s).wait()
        da = lax.dot_general(d_ref[...], w_ref[...], NT, preferred_element_type=F32)
        slot = s % slots
        g, u = ring[slot, :, :FF_TILE], ring[slot, :, FF_TILE:]
        sg = _sigmoid(g)
        o_ref[:, :FF_TILE] = (da * u * (sg * (1.0 + g * (1.0 - sg)))).astype(o_ref.dtype)
        o_ref[:, FF_TILE:] = (da * (g * sg)).astype(o_ref.dtype)

    return pl.pallas_call(
        body, name="d_act", grid=(2, n),
        in_specs=[pl.BlockSpec((tm, D_MODEL), lambda j, i: (i, 0)), pl.BlockSpec((FF_TILE, D_MODEL), lambda j, i: (j, 0)),
                  pl.BlockSpec(memory_space=pl.ANY)],
        out_specs=pl.BlockSpec((tm, 2 * FF_TILE), lambda j, i: (i, j)), out_shape=jax.ShapeDtypeStruct((s_, 2 * D_FF), MXU_DTYPE),
        scratch_shapes=[pltpu.VMEM((slots, tm, 2 * FF_TILE), gu.dtype), pltpu.SemaphoreType.DMA((slots,))],
        compiler_params=_cparams(("arbitrary", "arbitrary")),
    )(dx2b, wd, gu)


def _ffn_out_loss(act, wd, x1, g, tgt, *, ts=512):
    s_, c = x1.shape
    kk = act.shape[1]

    n, slots = s_ // ts, 3

    def body(a_hbm, w_ref, x_ref, g_ref, t_ref, dx_ref, dxb_ref, dg_ref, lp_ref, tot_ref, ring, sems):
        s = pl.program_id(0)

        def fetch(step):
            return pltpu.make_async_copy(a_hbm.at[pl.ds(pl.multiple_of(step * ts, ts), ts)], ring.at[step % slots],
                                         sems.at[step % slots])

        @pl.when(s == 0)
        def _():
            for ahead in range(min(slots - 1, n)):
                fetch(ahead).start()

        @pl.when(s + slots - 1 < n)
        def _():
            fetch(s + slots - 1).start()

        fetch(s).wait()
        v = x_ref[...] + jnp.dot(ring[s % slots], w_ref[...], preferred_element_type=F32)
        r = lax.rsqrt(jnp.mean(v * v, axis=-1, keepdims=True) + EPS)
        xh = v * r
        gg = g_ref[...]
        e = xh * gg - t_ref[...]
        do = e * (1.0 / c)
        dxh = do * gg
        dx = r * (dxh - xh * jnp.mean(dxh * xh, axis=-1, keepdims=True))
        dx_ref[...] = dx
        dxb_ref[...] = dx.astype(dxb_ref.dtype)
        i = pl.program_id(0)

        @pl.when(i == 0)
        def _():
            dg_ref[...] = jnp.zeros(dg_ref.shape, F32)
            lp_ref[...] = jnp.zeros(lp_ref.shape, F32)

        dg_ref[...] += _sublane_sum(do * xh)
        lp_ref[...] += _sublane_sum(e * e)
        tot_ref[...] = jnp.full(tot_ref.shape, (0.5 / c) * jnp.sum(lp_ref[...]), F32)

    return pl.pallas_call(
        body, name="ffn_out_loss", grid=(n,),
        scratch_shapes=[pltpu.VMEM((slots, ts, kk), act.dtype), pltpu.SemaphoreType.DMA((slots,))],
        in_specs=[pl.BlockSpec(memory_space=pl.ANY), _const(kk, c), _rows(ts, c), _const(1, c), _rows(ts, c)],
        out_specs=[_rows(ts, c), _rows(ts, c), _const(SUBLANES, c), _const(SUBLANES, c), _const(SUBLANES, LANES)],
        out_shape=[jax.ShapeDtypeStruct((s_, c), F32), jax.ShapeDtypeStruct((s_, c), MXU_DTYPE),
                   jax.ShapeDtypeStruct((SUBLANES, c), F32), jax.ShapeDtypeStruct((SUBLANES, c), F32),
                   jax.ShapeDtypeStruct((SUBLANES, LANES), F32)],
        compiler_params=_cparams(("arbitrary",)),
    )(act, wd, x1, g, tgt)


def _mla_d_out(dtb, wob_t, o32, *, ts=512):
    s_ = dtb.shape[0]

    def body(dt_ref, w_ref, o_ref, dob_ref, dl_ref):
        d = jnp.dot(dt_ref[...], w_ref[...], preferred_element_type=F32)
        dob_ref[...] = d.astype(dob_ref.dtype)
        prod = d * o_ref[...]
        for h in range(MLA_HEADS):
            dl_ref[h] = jnp.sum(prod[:, h * LANES:(h + 1) * LANES].T, axis=0, keepdims=True)

    return pl.pallas_call(
        body, name="mla_d_out", grid=(s_ // ts,), in_specs=[_rows(ts, 1024), _const(1024, 1024), _rows(ts, 1024)],
        out_specs=[_rows(ts, 1024), pl.BlockSpec((MLA_HEADS, 1, ts), lambda i: (0, 0, i))],
        out_shape=[jax.ShapeDtypeStruct((s_, 1024), MXU_DTYPE), jax.ShapeDtypeStruct((MLA_HEADS, 1, s_), F32)],
        compiler_params=_cparams(("parallel",)),
    )(dtb, wob_t, o32)


SWA_T = 4 * BLOCK


SWA_W = SWA_GROUP * BLOCK


def _swa_masks(sb):
    kr = lax.broadcasted_iota(jnp.int32, (2 * BLOCK, SWA_W), 0)
    qc = jnp.bitwise_and(lax.broadcasted_iota(jnp.int32, (2 * BLOCK, SWA_W), 1), BLOCK - 1)
    band = jnp.logical_and(kr > qc, kr <= qc + BLOCK)
    first = jnp.logical_and(band, kr >= BLOCK)
    return band, jnp.logical_or(first, jnp.logical_and(band, sb > 0))


def _heads_to_rows(ref, rs):
    return jnp.concatenate([ref[rs, h * LANES:(h + 1) * LANES] for h in range(SWA_GROUP)], axis=0)


def _sink_row(sk_ref):
    return jnp.concatenate([sk_ref[0, h:h + 1, :] for h in range(SWA_GROUP)], axis=1) * LOG2E


def _swa_in_specs(rev, nsb):
    sbi = (lambda j: nsb - 1 - j) if rev else (lambda j: j)
    cur = pl.BlockSpec((SWA_T, LANES), lambda g, j: (sbi(j), g))
    prev = pl.BlockSpec((BLOCK, LANES), lambda g, j: (jnp.maximum(4 * sbi(j) - 1, 0), g))
    q = pl.BlockSpec((SWA_T, SWA_GROUP * LANES), lambda g, j: (sbi(j), g))
    sink = pl.BlockSpec((1, SUBLANES, LANES), lambda g, j: (g, 0, 0))
    lse = pl.BlockSpec((SWA_GROUP, 1, SWA_T), lambda g, j: (g, 0, sbi(j)))
    return q, cur, prev, sink, lse


def _swa_fwd(qa, ka, va, sink_b):
    s_ = qa.shape[0]
    nsb = s_ // SWA_T
    c2 = HEAD_DIM ** -0.5 * LOG2E

    def body(q_ref, kc_ref, kp_ref, vc_ref, vp_ref, sk_ref, o32_ref, o16_ref, lse_ref, kx, vx):
        kx[0:BLOCK, :] = kp_ref[...]
        kx[BLOCK:5 * BLOCK, :] = kc_ref[...]
        vx[0:BLOCK, :] = vp_ref[...]
        vx[BLOCK:5 * BLOCK, :] = vc_ref[...]
        band, band0 = _swa_masks(pl.program_id(1))
        sink2 = _sink_row(sk_ref)
        for b in range(4):
            rs = slice(b * BLOCK, (b + 1) * BLOCK)
            ks = slice(b * BLOCK, (b + 2) * BLOCK)
            st = lax.dot_general(kx[ks, :], _heads_to_rows(q_ref, rs), NT, preferred_element_type=F32) * c2
            st = jnp.where(band0 if b == 0 else band, st, -jnp.inf)
            m = jnp.maximum(jnp.max(st, axis=0, keepdims=True), sink2)
            pt = jnp.exp2(st - m)
            den = jnp.sum(pt, axis=0, keepdims=True) + jnp.exp2(sink2 - m)
            o = lax.dot_general((pt * (1.0 / den)).astype(MXU_DTYPE), vx[ks, :], TN, preferred_element_type=F32)
            lse = m + jnp.log2(den)
            for hh in range(SWA_GROUP):
                cs = slice(hh * LANES, (hh + 1) * LANES)
                o32_ref[rs, cs] = o[cs, :]
                o16_ref[rs, cs] = o[cs, :].astype(o16_ref.dtype)
                lse_ref[hh, :, rs] = lse[:, cs]

    q, cur, prev, sink, lse_spec = _swa_in_specs(False, nsb)
    return pl.pallas_call(
        body, name="swa_fwd", grid=(SWA_KV_HEADS, nsb), in_specs=[q, cur, prev, cur, prev, sink],
        out_specs=[q, q, lse_spec],
        out_shape=[jax.ShapeDtypeStruct((s_, SWA_HEADS * LANES), F32), jax.ShapeDtypeStruct((s_, SWA_HEADS * LANES), MXU_DTYPE),
                   jax.ShapeDtypeStruct((SWA_HEADS, 1, s_), F32)],
        scratch_shapes=[pltpu.VMEM((5 * BLOCK, LANES), MXU_DTYPE), pltpu.VMEM((5 * BLOCK, LANES), MXU_DTYPE)],
        compiler_params=_cparams(("parallel", "arbitrary")),
    )(qa, ka, ka, va, va, sink_b)


def _swa_bwd(qa, ka, va, sink_b, o32, do, lse):
    s_ = qa.shape[0]
    nsb = s_ // SWA_T
    scale = HEAD_DIM ** -0.5
    c2 = scale * LOG2E

    def body(q_ref, kc_ref, kp_ref, vc_ref, vp_ref, sk_ref, o_ref, do_ref, lse_ref,
             dq_ref, dk_ref, dv_ref, dsk_ref, kx, vx, kacc, vacc, kcar, vcar):
        j = pl.program_id(1)
        kx[0:BLOCK, :] = kp_ref[...]
        kx[BLOCK:5 * BLOCK, :] = kc_ref[...]
        vx[0:BLOCK, :] = vp_ref[...]
        vx[BLOCK:5 * BLOCK, :] = vc_ref[...]
        band, band0 = _swa_masks(nsb - 1 - j)
        kacc[...] = jnp.zeros(kacc.shape, F32)
        vacc[...] = jnp.zeros(vacc.shape, F32)

        @pl.when(j == 0)
        def _():
            kcar[...] = jnp.zeros(kcar.shape, F32)
            vcar[...] = jnp.zeros(vcar.shape, F32)
            dsk_ref[...] = jnp.zeros(dsk_ref.shape, F32)

        sink2 = _sink_row(sk_ref)
        dsink = jnp.zeros((1, SWA_W), F32)
        for b in range(4):
            rs = slice(b * BLOCK, (b + 1) * BLOCK)
            ks = slice(b * BLOCK, (b + 2) * BLOCK)
            q, k2, v2 = _heads_to_rows(q_ref, rs), kx[ks, :], vx[ks, :]
            d = _heads_to_rows(do_ref, rs)
            delta = jnp.sum((d * _heads_to_rows(o_ref, rs)).T, axis=0, keepdims=True)
            l2 = jnp.concatenate([lse_ref[hh, :, rs] for hh in range(SWA_GROUP)], axis=1)
            st = lax.dot_general(k2, q, NT, preferred_element_type=F32) * c2
            pt = jnp.exp2(jnp.where(band0 if b == 0 else band, st, -jnp.inf) - l2)
            db = d.astype(MXU_DTYPE)
            dst = (pt * (lax.dot_general(v2, db, NT, preferred_element_type=F32) - delta) * scale).astype(MXU_DTYPE)
            dq = lax.dot_general(dst, k2, TN, preferred_element_type=F32)
            for hh in range(SWA_GROUP):
                dq_ref[rs, hh * LANES:(hh + 1) * LANES] = dq[hh * LANES:(hh + 1) * LANES, :]
            kacc[ks, :] += jnp.dot(dst, q, preferred_element_type=F32)
            vacc[ks, :] += jnp.dot(pt.astype(MXU_DTYPE), db, preferred_element_type=F32)
            dsink = dsink - jnp.exp2(sink2 - l2) * delta
        for hh in range(SWA_GROUP):
            tot = jnp.sum(dsink[:, hh * LANES:(hh + 1) * LANES], axis=1, keepdims=True)
            dsk_ref[0, hh:hh + 1, :] += jnp.broadcast_to(tot, (1, LANES))

        dk_ref[0:3 * BLOCK, :] = kacc[BLOCK:4 * BLOCK, :]
        dk_ref[3 * BLOCK:4 * BLOCK, :] = kacc[4 * BLOCK:5 * BLOCK, :] + kcar[...]
        dv_ref[0:3 * BLOCK, :] = vacc[BLOCK:4 * BLOCK, :].astype(dv_ref.dtype)
        dv_ref[3 * BLOCK:4 * BLOCK, :] = (vacc[4 * BLOCK:5 * BLOCK, :] + vcar[...]).astype(dv_ref.dtype)
        kcar[...] = kacc[0:BLOCK, :]
        vcar[...] = vacc[0:BLOCK, :]

    q, cur, prev, sink, lse_spec = _swa_in_specs(True, nsb)
    return pl.pallas_call(
        body, name="swa_bwd", grid=(SWA_KV_HEADS, nsb),
        in_specs=[q, cur, prev, cur, prev, sink, q, q, lse_spec],
        out_specs=[q, cur, cur, sink],
        out_shape=[jax.ShapeDtypeStruct((s_, SWA_HEADS * LANES), F32), jax.ShapeDtypeStruct((s_, SWA_KV_HEADS * LANES), F32),
                   jax.ShapeDtypeStruct((s_, SWA_KV_HEADS * LANES), MXU_DTYPE),
                   jax.ShapeDtypeStruct((SWA_KV_HEADS, SUBLANES, LANES), F32)],
        scratch_shapes=[pltpu.VMEM((5 * BLOCK, LANES), MXU_DTYPE), pltpu.VMEM((5 * BLOCK, LANES), MXU_DTYPE),
                        pltpu.VMEM((5 * BLOCK, LANES), F32), pltpu.VMEM((5 * BLOCK, LANES), F32),
                        pltpu.VMEM((BLOCK, LANES), F32), pltpu.VMEM((BLOCK, LANES), F32)],
        compiler_params=_cparams(("arbitrary", "arbitrary")),
    )(qa, ka, ka, va, va, sink_b, o32, do, lse)


MLA_T = 512
MLA_FWD_GROUP = 4
MLA_BWD_GROUP = 2


def _mla_specs(s_, t, group):
    w = group * LANES
    qs = pl.BlockSpec((t, w), lambda g, i: (i, g))
    kv = pl.BlockSpec((s_, w), lambda g, i: (0, g))
    row = pl.BlockSpec((group, 1, t), lambda g, i: (g, 0, i))
    return qs, kv, row


def _causal_scores_t(k, q, t, c2, masked):
    st = lax.dot_general(k, q, NT, preferred_element_type=F32) * c2
    if masked:
        kr = lax.broadcasted_iota(jnp.int32, (t, t), 0)
        qc = lax.broadcasted_iota(jnp.int32, (t, t), 1)
        st = jnp.where(kr <= qc, st, -jnp.inf)
    return st


def _mla_fwd(qc, kc, vp):
    s_ = qc.shape[0]
    t = min(MLA_T, s_)
    c2 = MLA_QK ** -0.5 * LOG2E
    grp = MLA_FWD_GROUP

    def body(q_ref, k_ref, v_ref, o32_ref, o16_ref, lse_ref, m_s, acc_s):
        qi = pl.program_id(1)
        m_s[...] = jnp.full(m_s.shape, -jnp.inf, F32)
        acc_s[...] = jnp.zeros(acc_s.shape, F32)
        ones_lane = lax.broadcasted_iota(jnp.int32, (t, LANES), 1) == MLA_V

        def step(ki, masked):
            off = pl.multiple_of(ki * t, t)
            for g in range(grp):
                cs = slice(g * LANES, (g + 1) * LANES)
                st = _causal_scores_t(k_ref[pl.ds(off, t), cs], q_ref[:, cs], t, c2, masked)
                m_old = m_s[g]
                m_new = jnp.maximum(m_old, jnp.max(st, axis=0, keepdims=True))
                alpha = jnp.exp2(m_old - m_new)
                pt = jnp.exp2(st - m_new).astype(MXU_DTYPE)
                v = v_ref[pl.ds(off, t), cs]
                v = jnp.where(ones_lane, jnp.ones((), v.dtype), v)
                acc_s[g] = alpha * acc_s[g] + lax.dot_general(v, pt, TN, preferred_element_type=F32)
                m_s[g] = m_new

        def full_block(ki, carry):
            step(ki, False)
            return carry

        lax.fori_loop(0, qi, full_block, 0)
        step(qi, True)
        for g in range(grp):
            cs = slice(g * LANES, (g + 1) * LANES)
            acc = acc_s[g]
            l = acc[MLA_V:MLA_V + 1, :]
            o = (acc * (1.0 / l)).T
            o32_ref[:, cs] = o
            o16_ref[:, cs] = o.astype(o16_ref.dtype)
            lse_ref[g] = m_s[g] + jnp.log2(l)

    qs, kv, row = _mla_specs(s_, t, grp)
    return pl.pallas_call(
        body, name="mla_fwd", grid=(MLA_HEADS // grp, s_ // t), in_specs=[qs, kv, kv], out_specs=[qs, qs, row],
        out_shape=[jax.ShapeDtypeStruct((s_, MLA_HEADS * LANES), F32), jax.ShapeDtypeStruct((s_, MLA_HEADS * LANES), MXU_DTYPE),
                   jax.ShapeDtypeStruct((MLA_HEADS, 1, s_), F32)],
        scratch_shapes=[pltpu.VMEM((grp, 1, t), F32), pltpu.VMEM((grp, LANES, t), F32)],
        compiler_params=_cparams(("parallel", "arbitrary")),
    )(qc, kc, vp)


def _mla_bwd(qc, kc, vp, dob, lse, delta):
    s_ = qc.shape[0]
    t = min(MLA_T, s_)
    scale = MLA_QK ** -0.5
    c2 = scale * LOG2E
    grp = MLA_BWD_GROUP

    def body(q_ref, do_ref, lse_ref, dl_ref, k_ref, v_ref, dq_ref, dk_ref, dv_ref, dqt_s):
        qi = pl.program_id(1)

        @pl.when(qi == 0)
        def _():
            dk_ref[...] = jnp.zeros(dk_ref.shape, F32)
            dv_ref[...] = jnp.zeros(dv_ref.shape, F32)

        dqt_s[...] = jnp.zeros(dqt_s.shape, F32)

        def step(ki, masked):
            off = pl.multiple_of(ki * t, t)
            for g in range(grp):
                cs = slice(g * LANES, (g + 1) * LANES)
                q, d, k = q_ref[:, cs], do_ref[:, cs], k_ref[pl.ds(off, t), cs]
                pt = jnp.exp2(_causal_scores_t(k, q, t, c2, masked) - lse_ref[g])
                dpt = lax.dot_general(v_ref[pl.ds(off, t), cs], d, NT, preferred_element_type=F32)
                dst = (pt * (dpt - dl_ref[g]) * scale).astype(MXU_DTYPE)
                dv_ref[pl.ds(off, t), cs] += jnp.dot(pt.astype(MXU_DTYPE), d, preferred_element_type=F32)
                dk_ref[pl.ds(off, t), cs] += jnp.dot(dst, q, preferred_element_type=F32)
                dqt_s[g] += lax.dot_general(k, dst, TN, preferred_element_type=F32)

        def full_block(ki, carry):
            step(ki, False)
            return carry

        lax.fori_loop(0, qi, full_block, 0)
        step(qi, True)
        for g in range(grp):
            dq_ref[:, g * LANES:(g + 1) * LANES] = dqt_s[g].T

    qs, kv, row = _mla_specs(s_, t, grp)
    shp = jax.ShapeDtypeStruct((s_, MLA_HEADS * LANES), F32)
    return pl.pallas_call(
        body, name="mla_bwd", grid=(MLA_HEADS // grp, s_ // t), in_specs=[qs, qs, row, row, kv, kv],
        out_specs=[qs, kv, kv], out_shape=[shp, shp, shp], scratch_shapes=[pltpu.VMEM((grp, LANES, t), F32)],
        compiler_params=_cparams(("parallel", "arbitrary")),
    )(qc, dob, lse, delta, kc, vp)


def _pad_heads(w, nh, hd, axis):
    shp = w.shape
    w = w.reshape(shp[:axis] + (nh, hd) + shp[axis + 1:])
    pad = [(0, 0)] * w.ndim
    pad[axis + 1] = (0, LANES - hd)
    w = jnp.pad(w, pad)
    return w.reshape(shp[:axis] + (nh * LANES,) + shp[axis + 1:])


def _unpad_heads(w, nh, hd, axis):
    shp = w.shape
    w = w.reshape(shp[:axis] + (nh, LANES) + shp[axis + 1:])
    w = lax.slice_in_dim(w, 0, hd, axis=axis + 1)
    return w.reshape(shp[:axis] + (nh * hd,) + shp[axis + 1:])


PACK_W = 1024
ROW_TILE = 16
FULL_SHAPE = dict(w_in=(1024, 3488), w_uq=(384, 768), w_ukv=(256, 1024), w_o_swa=(512, 1024), w_o_mla=(512, 1024),
                  w_out=(1024, 1024), w_gate=(1024, 2816), w_up=(1024, 2816), w_down=(2816, 1024))
BIG = tuple(FULL_SHAPE)
ROW_SHARDED = ("w_out", "w_down")
W_IN_COLS = FULL_SHAPE["w_in"][1] // N_DEV
W_IN_ROWS = -(-W_IN_COLS // ROW_TILE) * ROW_TILE
FF_COLS = D_FF // N_DEV
OUT_ROWS = D_MODEL // N_DEV
SMALL_FLAT = (("w_uq", 0, 36), ("w_ukv", 48, 32))
SMALL_USED = 80
MID_BLOCKS = 4
MID_ROWS = MID_BLOCKS * OUT_ROWS
EARLY_ROWS = W_IN_ROWS + MID_ROWS
LATE_ROWS = 3 * FF_COLS
PACK_ROWS = EARLY_ROWS + LATE_ROWS


def _shard_shape(n):
    r, c = FULL_SHAPE[n]
    return (r // N_DEV, c) if n in ROW_SHARDED else (r, c // N_DEV)


def _wire_pack(sh, dtype):
    c = lambda n: sh[n].astype(dtype)
    rows = [jnp.pad(c("w_in").T, ((0, W_IN_ROWS - W_IN_COLS), (0, 0))), c("w_out"),
            _pad_heads(c("w_o_swa").T, SWA_HEADS, HEAD_DIM, 1), _pad_heads(c("w_o_mla").T, MLA_HEADS, MLA_V, 1)]
    for n, _, r in SMALL_FLAT:
        rows.append(jnp.pad(c(n).reshape(r, PACK_W), ((0, -r % ROW_TILE), (0, 0))))
    rows.append(jnp.zeros((OUT_ROWS - SMALL_USED, PACK_W), dtype))
    return jnp.concatenate(rows + [c("w_gate").T, c("w_up").T, c("w_down")], 0)


def _mid_unpack(p):
    out = dict(w_out=p[0:OUT_ROWS], w_o_swa=_unpad_heads(p[OUT_ROWS:2 * OUT_ROWS], SWA_HEADS, HEAD_DIM, 1).T,
               w_o_mla=_unpad_heads(p[2 * OUT_ROWS:3 * OUT_ROWS], MLA_HEADS, MLA_V, 1).T)
    for n, off, r in SMALL_FLAT:
        out[n] = p[3 * OUT_ROWS + off:3 * OUT_ROWS + off + r].reshape(_shard_shape(n))
    return out


def _w_in_row_maps():
    sp = lambda col: (col // W_IN_COLS) * W_IN_ROWS + col % W_IN_COLS
    fwd = np.full((P_W,), -1, np.int64)

    def put(t0, c0, n):
        fwd[t0:t0 + n] = [sp(c) for c in range(c0, c0 + n)]

    put(P_GA, IN_OFF[6], D_MODEL)
    put(P_GB, IN_OFF[7], D_MODEL)
    for h in range(SWA_HEADS):
        put(P_Q + LANES * h, IN_OFF[0] + HEAD_DIM * h, HEAD_DIM)
    put(P_QLAT, IN_OFF[3], Q_LORA)
    put(P_KR + KR_LANE, IN_OFF[5], MLA_ROPE)
    for h in range(SWA_KV_HEADS):
        put(P_K + LANES * h, IN_OFF[1] + HEAD_DIM * h, HEAD_DIM)
        put(P_V + LANES * h, IN_OFF[2] + HEAD_DIM * h, HEAD_DIM)
    put(P_KVLAT, IN_OFF[4], KV_LORA)
    inv = np.full((N_DEV * W_IN_ROWS,), -1, np.int64)
    inv[fwd[fwd >= 0]] = np.nonzero(fwd >= 0)[0]
    return fwd, inv


def _take_rows(src, idx, *, name, tile=2 * LANES):
    n_out, n_src, width = len(idx), src.shape[0], src.shape[1]
    assert n_out % tile == 0 and n_src % tile == 0
    n_tiles = n_out // tile
    blocks = [sorted({int(v) // tile for v in idx[i * tile:(i + 1) * tile] if v >= 0}) for i in range(n_tiles)]
    k_max = max(1, max(len(b) for b in blocks))
    tab = np.zeros((n_tiles, k_max), np.int32)
    sel = np.zeros((n_tiles, k_max, tile, tile), np.float32)
    for i, blks in enumerate(blocks):
        for m, b in enumerate(blks):
            tab[i, m] = b
            for r in range(tile):
                v = int(idx[i * tile + r])
                if v >= 0 and v // tile == b:
                    sel[i, m, r, v % tile] = 1.0

    def body(tab_ref, sel_ref, *refs):
        o_ref = refs[k_max]
        acc = jnp.dot(sel_ref[0, 0], refs[0][...], preferred_element_type=F32)
        for m in range(1, k_max):
            acc = acc + jnp.dot(sel_ref[0, m], refs[m][...], preferred_element_type=F32)
        o_ref[...] = acc.astype(o_ref.dtype)

    def src_spec(m):
        return pl.BlockSpec((tile, width), lambda i, t: (t[i * k_max + m], 0))

    return pl.pallas_call(
        body, name=name,
        grid_spec=pltpu.PrefetchScalarGridSpec(
            num_scalar_prefetch=1, grid=(n_tiles,),
            in_specs=[pl.BlockSpec((1, k_max, tile, tile), lambda i, t: (i, 0, 0, 0))] + [src_spec(m) for m in range(k_max)],
            out_specs=pl.BlockSpec((tile, width), lambda i, t: (i, 0))),
        out_shape=jax.ShapeDtypeStruct((n_out, width), src.dtype),
        compiler_params=_cparams(("parallel",)),
    )(jnp.asarray(tab.reshape(-1)), jnp.asarray(sel, src.dtype), *([src] * k_max))


def _w_in_operand(win_g):
    return _take_rows(win_g.reshape(N_DEV * W_IN_ROWS, PACK_W), _w_in_row_maps()[0], name="w_in_rows")


def _mid_operands(wout_g, woa_g, wob_g, small_g):
    def full(n, off, r):
        a = small_g[:, off:off + r].reshape((N_DEV,) + _shard_shape(n))
        return jnp.moveaxis(a, 0, 1).reshape(FULL_SHAPE[n])

    w = {n: full(n, off, r) for n, off, r in SMALL_FLAT}
    ukv = w["w_ukv"].reshape(KV_LORA, MLA_HEADS, MLA_NOPE + MLA_V)
    return dict(
        wout=wout_g.reshape(D_MODEL, D_MODEL), woa_t=woa_g.reshape(D_MODEL, -1), wob_t=wob_g.reshape(D_MODEL, -1),
        wuq=_pad_heads(w["w_uq"], MLA_HEADS, MLA_QK, 1),
        wuk=_pad_heads(ukv[:, :, :MLA_NOPE].reshape(KV_LORA, -1), MLA_HEADS, MLA_NOPE, 1),
        wuv=_pad_heads(ukv[:, :, MLA_NOPE:].reshape(KV_LORA, -1), MLA_HEADS, MLA_V, 1),
    )


def _mid_grad_pack(g):
    uk = _unpad_heads(g["wukv"][:, :1024], MLA_HEADS, MLA_NOPE, 1).reshape(KV_LORA, MLA_HEADS, MLA_NOPE)
    uv = _unpad_heads(g["wukv"][:, 1024:], MLA_HEADS, MLA_V, 1).reshape(KV_LORA, MLA_HEADS, MLA_V)
    w = dict(w_uq=_unpad_heads(g["wuq"], MLA_HEADS, MLA_QK, 1), w_ukv=jnp.concatenate([uk, uv], 2).reshape(KV_LORA, -1))
    rows = []
    for n, _, r in SMALL_FLAT:
        rr, cc = FULL_SHAPE[n]
        a = jnp.moveaxis(w[n].reshape(rr, N_DEV, cc // N_DEV), 1, 0).reshape(N_DEV, r, PACK_W)
        rows.append(jnp.pad(a, ((0, 0), (0, -r % ROW_TILE), (0, 0))).astype(WIRE_DTYPE))
    rows.append(jnp.zeros((N_DEV, OUT_ROWS - SMALL_USED, PACK_W), WIRE_DTYPE))
    blk = lambda a: a.reshape(N_DEV, OUT_ROWS, PACK_W)
    return [blk(g["wout"]), blk(g["woa_t"]), blk(g["wob_t"]), jnp.concatenate(rows, 1)]


def _w_in_grad_chunks(g_win_t):
    return _take_rows(g_win_t, _w_in_row_maps()[1], name="dw_in_rows").reshape(N_DEV, W_IN_ROWS, PACK_W)


def _local_step(x, tgt, win_t, small, weights, grads):
    s_ = x.shape[0]
    tabs = _rope_tables(s_)
    sink_b = jnp.broadcast_to(small["swa_sinks"].reshape(SWA_KV_HEADS, SWA_GROUP, 1), (SWA_KV_HEADS, SWA_GROUP, LANES))
    sink_b = jnp.pad(sink_b, ((0, 0), (0, SUBLANES - SWA_GROUP), (0, 0)))

    h, qa, ka, va, cq, ckv, kro, p = _proj_in(x, small["mix_norm_g"], win_t, small["q_norm_g"], small["kv_norm_g"], tabs)
    oa32, oa16, lse_a = _swa_fwd(qa, ka, va, sink_b)
    ops = weights.mid(oa16)
    qc, kc, vp = _mla_up(cq, ckv, kro, ops["wuq"], ops["wuk"], ops["wuv"], tabs)
    ob32, ob16, lse_b = _mla_fwd(qc, kc, vp)
    y = _attn_out_gate(oa16, ob16, ops["woa_t"], ops["wob_t"], p)
    x1 = _mm(y, ops["wout"], "nn", name="out_proj", add=x, tm=1024, tn=1024)
    wgu_t, wd = weights.late(x1)
    h2, gu, act = _ffn_in_act(x1, small["ffn_norm_g"], wgu_t)

    dx2, dx2b, dg3, _, tot = _ffn_out_loss(act, wd, x1, small["final_norm_g"].reshape(1, D_MODEL), tgt)
    g = {}
    g_wd = _mm(act, dx2b, "tn", name="dw_down", tm=FF_TILE, tn=1024, tk=2048, out_dtype=WIRE_DTYPE)
    dgu = _d_act_swiglu(dx2b, wd, gu)
    g_wgu = _mm(dgu, h2, "tn", name="dw_ffn_in", tm=FF_TILE, tn=1024, tk=2048, out_dtype=WIRE_DTYPE)
    token = grads.late(g_wgu, g_wd)
    dx1, dx1b, dg2 = _mm_norm_bwd(dgu, wgu_t, x1, small["ffn_norm_g"] + token[0:1, 0:1], dx2, name="d_h2")
    g["wout"] = _mm(y, dx1b, "tn", name="dw_out", tm=1024, tn=1024, tk=2048, out_dtype=WIRE_DTYPE)
    dta, dtb, dgab = _d_y_gate(dx1b, ops["wout"], p, oa16, ob16, ops["woa_t"], ops["wob_t"])
    doa = _mm(dta, ops["woa_t"], "nn", name="d_oa", tm=1024, tn=1024)
    g["woa_t"] = _mm(dta, oa16, "tn", name="dw_o_swa", tm=1024, tn=1024, tk=2048, out_dtype=WIRE_DTYPE)
    g["wob_t"] = _mm(dtb, ob16, "tn", name="dw_o_mla", tm=1024, tn=1024, tk=2048, out_dtype=WIRE_DTYPE)
    dob16, delta_b = _mla_d_out(dtb, ops["wob_t"], ob32)
    dqc, dkc, dvp = _mla_bwd(qc, kc, vp, dob16, lse_b, delta_b)
    dqp, dkv, dkr, dqlat, dkvlat, dgq, dgkv = _mla_up_bwd(
        dqc, dkc, dvp, ops["wuq"], jnp.concatenate([ops["wuk"], ops["wuv"]], 1), p, small["q_norm_g"], small["kv_norm_g"], tabs)
    g["wuq"] = _mm(cq, dqp, "tn", name="dw_uq", tm=Q_LORA, tn=1024, tk=2048)
    g["wukv"] = _mm(ckv, dkv, "tn", name="dw_ukv", tm=KV_LORA, tn=2048, tk=2048)
    token = grads.mid(g)
    dqa, dka, dva, dsk = _swa_bwd(qa, ka, va, sink_b + token[0:1, 0:1], oa32, doa, lse_a)
    dp = _assemble_dp(dgab, dqa, dqlat, dkr, dka, dva, dkvlat, tabs)
    token = grads.last(_mm(dp, h, "tn", name="dw_in", tm=2176, tn=1024, tk=1024, out_dtype=WIRE_DTYPE))
    gx, _, dg1 = _mm_norm_bwd(dp, win_t, x, small["mix_norm_g"], dx1, name="d_h", after=token)

    sm = dict(mix_norm_g=dg1, ffn_norm_g=dg2, final_norm_g=dg3, q_norm_g=dgq, kv_norm_g=dgkv,
              swa_sinks=dsk[:, :SWA_GROUP, 0].reshape(1, SWA_HEADS))
    return tot, gx, sm


MESH = pl.DeviceIdType.MESH
ANY = pl.BlockSpec(memory_space=pl.ANY)


def _position():
    return lax.axis_index("x"), lax.axis_index("y"), lax.axis_index("c")


def _all_gather(block, pieces, shapes, *, name):
    n_out = len(shapes)
    n_rows = sum(p[3] for p in pieces)

    def body(x_ref, *refs):
        outs, (send_sems, recv_sems, local_sem) = refs[:n_out], refs[n_out:]
        x, y, c = _position()
        me, sibling = (x, y, c), (x, y, 1 - c)
        chips = [(1 - x, y), (x, 1 - y), (1 - x, 1 - y)]

        def dst(piece, blk):
            arr, lead, _, _ = piece
            return outs[arr].at[lead(4 * blk[0] + 2 * blk[1] + blk[2])]

        def own(piece):
            return x_ref.at[pl.ds(piece[2], piece[3])]

        def copies(k, blk, to, from_input):
            return [pltpu.make_async_remote_copy(
                src_ref=own(p) if from_input else dst(p, blk), dst_ref=dst(p, blk), send_sem=send_sems.at[k],
                recv_sem=recv_sems.at[k], device_id=to, device_id_type=MESH) for p in pieces]

        gathered_rows = x_ref.at[pl.ds(0, n_rows)]

        def whole_block(k):
            return pltpu.make_async_remote_copy(src_ref=gathered_rows, dst_ref=gathered_rows, send_sem=send_sems.at[k],
                                                recv_sem=recv_sems.at[k], device_id=me, device_id_type=MESH)

        for p in pieces:
            pltpu.make_async_copy(own(p), dst(p, me), local_sem).start()
        for cp in copies(0, me, sibling, True):
            cp.start()
        for j, chip in enumerate(chips):
            for cp in copies(1 + j, me, (*chip, c), True):
                cp.start()
        for j, chip in enumerate(chips):
            whole_block(1 + j).wait_recv()
            for cp in copies(4 + j, (*chip, c), sibling, False):
                cp.start()
        whole_block(0).wait_recv()
        for j in range(3):
            whole_block(4 + j).wait_recv()
        for k in range(7):
            whole_block(k).wait_send()
        pltpu.make_async_copy(gathered_rows, gathered_rows, local_sem).wait()

    return pl.pallas_call(
        body, name=name, out_shape=[jax.ShapeDtypeStruct(s, block.dtype) for s in shapes], in_specs=[ANY],
        out_specs=[ANY] * n_out,
        scratch_shapes=[pltpu.SemaphoreType.DMA((7,)), pltpu.SemaphoreType.DMA((7,)), pltpu.SemaphoreType.DMA],
    )(block)


HBM = pl.BlockSpec(memory_space=pltpu.HBM)
SEM = pl.BlockSpec(memory_space=pltpu.SEMAPHORE)
TILE_DEVS = FF_TILE // FF_COLS
GU_SHAPE = (2, 2, TILE_DEVS, FF_COLS, PACK_W)


def _gate_slab(d):
    return (d // TILE_DEVS, 0, d % TILE_DEVS)


def _up_slab(d):
    return (d // TILE_DEVS, 1, d % TILE_DEVS)
D_SHAPE = (N_DEV, FF_COLS, PACK_W)
LAND_SHAPE = (N_DEV, LATE_ROWS, PACK_W)


def _split_params():
    return pltpu.CompilerParams(has_side_effects=pltpu.SideEffectType.DATAFLOW_SIDE_EFFECTING)


def _peer(x, y, c, k):
    return ((1 - x) if k & 4 else x, (1 - y) if k & 2 else y, (1 - c) if k & 1 else c)


def _empty_hbm(shape, dtype):
    return pltpu.with_memory_space_constraint(lax.empty(shape, dtype), pltpu.HBM)


def _wait_all(rows, send_sems, recv_sems, me):
    for k in range(N_DEV - 1):
        cp = pltpu.make_async_remote_copy(src_ref=rows, dst_ref=rows, send_sem=send_sems.at[k], recv_sem=recv_sems.at[k],
                                          device_id=me, device_id_type=MESH)
        cp.wait_send()
        cp.wait_recv()


def _token_shape():
    return jax.ShapeDtypeStruct((SUBLANES, LANES), F32)


def _gather_start(pack, row0, pieces, shapes, *, name):
    n = len(shapes)

    def body(*refs):
        p_ref, bufs, send_sems, recv_sems, token = refs[0], refs[1:1 + n], refs[1 + n], refs[2 + n], refs[-1]
        x, y, c = _position()
        me = 4 * x + 2 * y + c
        for k in range(1, N_DEV):
            off = row0
            for buf, lead, rows in pieces:
                pltpu.make_async_remote_copy(
                    src_ref=p_ref.at[pl.ds(off, rows)], dst_ref=bufs[buf].at[lead(me)], send_sem=send_sems.at[k - 1],
                    recv_sem=recv_sems.at[k - 1], device_id=_peer(x, y, c, k), device_id_type=MESH).start()
                off += rows
        token[...] = jnp.zeros_like(token)

    sems, dt = pltpu.SemaphoreType.DMA((N_DEV - 1,)), pack.dtype
    return pl.pallas_call(
        body, name=name,
        out_shape=(sems, sems, pltpu.HBM(pack.shape, dt)) + tuple(pltpu.HBM(s, dt) for s in shapes) + (_token_shape(),),
        in_specs=(HBM,) * (1 + n), out_specs=(SEM, SEM) + (HBM,) * (1 + n) + (pl.BlockSpec(memory_space=pltpu.VMEM),),
        input_output_aliases={i: 2 + i for i in range(1 + n)}, compiler_params=_split_params(),
    )(pltpu.with_memory_space_constraint(pack, pltpu.HBM), *[_empty_hbm(s, dt) for s in shapes])


def _gather_wait(started, row0, n_rows, after, *, name):
    send_sems, recv_sems, pack, *bufs = started[:-1]
    n = len(bufs)

    def body(*refs):
        _wait_all(refs[0].at[pl.ds(row0, n_rows)], refs[1 + n], refs[2 + n], _position())

    outs = pl.pallas_call(
        body, name=name, out_shape=tuple(pltpu.HBM(a.shape, a.dtype) for a in (pack, *bufs)),
        in_specs=(HBM,) * (1 + n) + (SEM, SEM, ANY), out_specs=(HBM,) * (1 + n),
        input_output_aliases={i: i for i in range(1 + n)}, compiler_params=_split_params(),
    )(pack, *bufs, send_sems, recv_sems, after)
    return outs[0], outs[1:]


def _scatter_start(srcs, pieces, *, name):
    n = len(srcs)
    land_shape = (N_DEV, sum(p[2] for p in pieces), PACK_W)

    def body(*refs):
        src_refs, land_ref, send_sems, recv_sems, token = refs[:n], refs[n], refs[n + 1], refs[n + 2], refs[-1]
        x, y, c = _position()
        me = 4 * x + 2 * y + c
        for k in range(1, N_DEV):
            px, py, pc = _peer(x, y, c, k)
            off = 0
            for si, lead, rows in pieces:
                pltpu.make_async_remote_copy(
                    src_ref=src_refs[si].at[lead(4 * px + 2 * py + pc)], dst_ref=land_ref.at[me, pl.ds(off, rows)],
                    send_sem=send_sems.at[k - 1], recv_sem=recv_sems.at[k - 1], device_id=(px, py, pc),
                    device_id_type=MESH).start()
                off += rows
        token[...] = jnp.zeros_like(token)

    sems, dt = pltpu.SemaphoreType.DMA((N_DEV - 1,)), srcs[0].dtype
    return pl.pallas_call(
        body, name=name,
        out_shape=(sems, sems) + tuple(pltpu.HBM(a.shape, dt) for a in srcs) + (pltpu.HBM(land_shape, dt), _token_shape()),
        in_specs=(HBM,) * (n + 1), out_specs=(SEM, SEM) + (HBM,) * (n + 1) + (pl.BlockSpec(memory_space=pltpu.VMEM),),
        input_output_aliases={i: 2 + i for i in range(n + 1)}, compiler_params=_split_params(),
    )(*[pltpu.with_memory_space_constraint(a, pltpu.HBM) for a in srcs], _empty_hbm(land_shape, dt))


def _scatter_wait(started, after, *, name):
    send_sems, recv_sems, *bufs = started[:-1]
    n = len(bufs)

    def body(*refs):
        _wait_all(refs[n - 1].at[0], refs[n], refs[n + 1], _position())

    return pl.pallas_call(
        body, name=name, out_shape=tuple(pltpu.HBM(a.shape, a.dtype) for a in bufs),
        in_specs=(HBM,) * n + (SEM, SEM, ANY), out_specs=(HBM,) * n, input_output_aliases={i: i for i in range(n)},
        compiler_params=_split_params(),
    )(*bufs, send_sems, recv_sems, after)


def _peer_sum(own, own_lead, land, block, rows, idx, *, name):
    owns = list(own) if isinstance(own, (list, tuple)) else [own]
    n, lead_rank = len(owns), owns[0].ndim - 2

    def body(idx_ref, *refs):
        own_refs, land_refs, o_ref = refs[:n], refs[n:n + N_DEV - 1], refs[n + N_DEV - 1]
        for j in range(n):
            rs_ = slice(j * rows, (j + 1) * rows)
            acc = own_refs[j][(0,) * lead_rank].astype(F32)
            for k in range(N_DEV - 1):
                acc = acc + land_refs[k][0, rs_].astype(F32)
            o_ref[rs_] = acc

    own_spec = pl.BlockSpec((1,) * lead_rank + (rows, PACK_W), lambda i, t: own_lead(t[0]) + (0, 0))

    def land_spec(k):
        return pl.BlockSpec((1, n * rows, PACK_W), lambda i, t: (t[k + 1], block, 0))

    return pl.pallas_call(
        body, name=name,
        grid_spec=pltpu.PrefetchScalarGridSpec(
            num_scalar_prefetch=1, grid=(1,), in_specs=[own_spec] * n + [land_spec(k) for k in range(N_DEV - 1)],
            out_specs=pl.BlockSpec((n * rows, PACK_W), lambda i, t: (0, 0))),
        out_shape=jax.ShapeDtypeStruct((n * rows, PACK_W), F32), compiler_params=_cparams(("arbitrary",)),
    )(idx, *owns, *([land] * (N_DEV - 1)))


def _sum_adamw(own, own_lead, land, block, rows, idx, w, m, v, *, name):
    lead_rank, r = own.ndim - 2, w.shape[1]

    def body(idx_ref, own_ref, *refs):
        land_refs, (w_ref, m_ref, v_ref), outs = refs[:N_DEV - 1], refs[N_DEV - 1:N_DEV + 2], refs[N_DEV + 2:]
        g = own_ref[(0,) * lead_rank + (slice(0, r),)].astype(F32)
        for k in range(N_DEV - 1):
            g = g + land_refs[k][0, 0:r].astype(F32)
        for o_ref, val in zip(outs, (g,) + tuple(_adamw(w_ref[0], g, m_ref[0], v_ref[0]))):
            o_ref[0] = val

    own_spec = pl.BlockSpec((1,) * lead_rank + (rows, PACK_W), lambda i, t: own_lead(t[0]) + (0, 0))
    shard = pl.BlockSpec((1, r, PACK_W), lambda i, t: (0, 0, 0))

    def land_spec(k):
        return pl.BlockSpec((1, rows, PACK_W), lambda i, t: (t[k + 1], block, 0))

    return pl.pallas_call(
        body, name=name,
        grid_spec=pltpu.PrefetchScalarGridSpec(
            num_scalar_prefetch=1, grid=(1,),
            in_specs=[own_spec] + [land_spec(k) for k in range(N_DEV - 1)] + [shard] * 3, out_specs=[shard] * 4),
        out_shape=[jax.ShapeDtypeStruct((1, r, PACK_W), F32)] * 4, compiler_params=_cparams(("arbitrary",)),
    )(idx, own, *([land] * (N_DEV - 1)), w, m, v)


def _adamw(w, g, m, v):
    m = ADAM_B1 * m + (1.0 - ADAM_B1) * g
    v = ADAM_B2 * v + (1.0 - ADAM_B2) * (g * g)
    m_hat = m / (1.0 - ADAM_B1 ** ADAM_STEP)
    v_hat = v / (1.0 - ADAM_B2 ** ADAM_STEP)
    delta = -ADAM_LR * (m_hat / (jnp.sqrt(v_hat) + ADAM_EPS) + ADAM_WD * w)
    return delta, m, v


def _adamw_call(w, g, m, v, *, name, max_rows=256):
    _, r, c_ = w.shape
    tr = max_rows if r > max_rows and r % max_rows == 0 else r

    def body(w_ref, g_ref, m_ref, v_ref, d_ref, mo_ref, vo_ref):
        d, mn, vn = _adamw(w_ref[0], g_ref[...], m_ref[0], v_ref[0])
        d_ref[0] = d
        mo_ref[0] = mn
        vo_ref[0] = vn

    row3 = pl.BlockSpec((1, tr, c_), lambda i: (0, i, 0))
    shp = jax.ShapeDtypeStruct((1, r, c_), F32)
    return pl.pallas_call(
        body, name=name, grid=(r // tr,), in_specs=[row3, pl.BlockSpec((tr, c_), lambda i: (i, 0)), row3, row3],
        out_specs=[row3] * 3, out_shape=[shp] * 3, compiler_params=_cparams(("parallel",)),
    )(w, g, m, v)


SMALL = ("mix_norm_g", "ffn_norm_g", "final_norm_g", "q_norm_g", "kv_norm_g", "swa_sinks")
SMALL_W = dict(mix_norm_g=1024, ffn_norm_g=1024, final_norm_g=1024, q_norm_g=Q_LORA, kv_norm_g=KV_LORA, swa_sinks=SWA_HEADS)


def _small_adamw(parts, w, m, v):
    ns = len(SMALL)

    def body(p_ref, *refs):
        ins, outs = refs[:3 * ns], refs[3 * ns:]
        tot = p_ref[0]
        for dev in range(1, N_DEV):
            tot = tot + p_ref[dev]
        for k, n in enumerate(SMALL):
            g = jnp.sum(tot[k * SUBLANES:(k + 1) * SUBLANES, :SMALL_W[n]], axis=0, keepdims=True)
            res = _adamw(ins[k][...], g, ins[ns + k][...], ins[2 * ns + k][...])
            for j, r in enumerate((g,) + tuple(res)):
                outs[j * ns + k][...] = r
        outs[4 * ns][...] = jnp.sum(tot[ns * SUBLANES:(ns + 1) * SUBLANES, 0:1], axis=0, keepdims=True)

    shapes = [jax.ShapeDtypeStruct((1, SMALL_W[n]), F32) for n in SMALL]
    vm = pl.BlockSpec(memory_space=pltpu.VMEM)
    out = pl.pallas_call(
        body, name="small_adamw", in_specs=[vm] * (1 + 3 * ns), out_specs=[vm] * (4 * ns + 1),
        out_shape=shapes * 4 + [jax.ShapeDtypeStruct((1, 1), F32)],
    )(parts, *[d[n] for d in (w, m, v) for n in SMALL])
    return [dict(zip(SMALL, out[j * ns:(j + 1) * ns])) for j in range(4)] + [out[4 * ns]]


def _small_pack(d, rows_each):
    parts = [jnp.pad(d[n].astype(F32), ((0, 0), (0, PACK_W - SMALL_W[n]))) for n in SMALL]
    out = jnp.concatenate(parts, 0)
    pad = -out.shape[0] % SUBLANES
    return jnp.pad(out, ((0, pad), (0, 0)))


def kernel(x, mix_norm_g, w_in, swa_sinks, q_norm_g, w_uq, kv_norm_g, w_ukv, w_o_swa, w_o_mla, w_out, ffn_norm_g, w_gate, w_up, w_down, final_norm_g, loss_target, m_mix_norm_g, m_w_in, m_swa_sinks, m_q_norm_g, m_w_uq, m_kv_norm_g, m_w_ukv, m_w_o_swa, m_w_o_mla, m_w_out, m_ffn_norm_g, m_w_gate, m_w_up, m_w_down, m_final_norm_g, v_mix_norm_g, v_w_in, v_swa_sinks, v_q_norm_g, v_w_uq, v_kv_norm_g, v_w_ukv, v_w_o_swa, v_w_o_mla, v_w_out, v_ffn_norm_g, v_w_gate, v_w_up, v_w_down, v_final_norm_g):
    big_w = dict(w_in=w_in[0], w_uq=w_uq[0], w_ukv=w_ukv[0], w_o_swa=w_o_swa[0], w_o_mla=w_o_mla[0], w_out=w_out[0],
                 w_gate=w_gate[0], w_up=w_up[0], w_down=w_down[0])
    big_w3 = dict(w_in=w_in, w_uq=w_uq, w_ukv=w_ukv, w_o_swa=w_o_swa, w_o_mla=w_o_mla, w_out=w_out, w_gate=w_gate, w_up=w_up,
                  w_down=w_down)
    big_m = dict(w_in=m_w_in, w_uq=m_w_uq, w_ukv=m_w_ukv, w_o_swa=m_w_o_swa, w_o_mla=m_w_o_mla, w_out=m_w_out,
                 w_gate=m_w_gate, w_up=m_w_up, w_down=m_w_down)
    big_v = dict(w_in=v_w_in, w_uq=v_w_uq, w_ukv=v_w_ukv, w_o_swa=v_w_o_swa, w_o_mla=v_w_o_mla, w_out=v_w_out,
                 w_gate=v_w_gate, w_up=v_w_up, w_down=v_w_down)
    small_w = dict(mix_norm_g=mix_norm_g, ffn_norm_g=ffn_norm_g, final_norm_g=final_norm_g.reshape(1, D_MODEL),
                   q_norm_g=q_norm_g, kv_norm_g=kv_norm_g, swa_sinks=swa_sinks)
    small_m = dict(mix_norm_g=m_mix_norm_g, ffn_norm_g=m_ffn_norm_g, final_norm_g=m_final_norm_g.reshape(1, D_MODEL),
                   q_norm_g=m_q_norm_g, kv_norm_g=m_kv_norm_g, swa_sinks=m_swa_sinks)
    small_v = dict(mix_norm_g=v_mix_norm_g, ffn_norm_g=v_ffn_norm_g, final_norm_g=v_final_norm_g.reshape(1, D_MODEL),
                   q_norm_g=v_q_norm_g, kv_norm_g=v_kv_norm_g, swa_sinks=v_swa_sinks)

    px, py, pc = _position()
    me = 4 * px + 2 * py + pc
    idx = jnp.stack([me] + [4 * qx + 2 * qy + qc for qx, qy, qc in (_peer(px, py, pc, k) for k in range(1, N_DEV))])
    idx = idx.astype(jnp.int32)

    dev = lambda d: (d,)
    pack = _wire_pack(big_w, WIRE_DTYPE)
    win_g, = _all_gather(pack, ((0, dev, 0, W_IN_ROWS),), ((N_DEV, W_IN_ROWS, PACK_W),), name="ag_early")
    mid_pieces = tuple((b, dev, OUT_ROWS) for b in range(MID_BLOCKS))
    ag_mid = _gather_start(pack, W_IN_ROWS, mid_pieces, ((N_DEV, OUT_ROWS, PACK_W),) * MID_BLOCKS, name="ag_mid_start")
    ag = {}

    def own_rows(r0, r1, shape):
        return pack[r0:r1].reshape(shape)

    def mid_weights(after):
        pack_mid, blocks = _gather_wait(ag_mid, W_IN_ROWS, MID_ROWS, after, name="ag_mid_wait")
        ag["late"] = _gather_start(pack_mid, EARLY_ROWS, ((0, _gate_slab, FF_COLS), (0, _up_slab, FF_COLS), (1, dev, FF_COLS)),
                                   (GU_SHAPE, D_SHAPE), name="ag_late_start")
        row0 = lambda b: W_IN_ROWS + b * OUT_ROWS
        ops = _mid_operands(*[lax.dynamic_update_slice(blk, own_rows(row0(b), row0(b + 1), (1, OUT_ROWS, PACK_W)), (me, 0, 0))
                              for b, blk in enumerate(blocks)])
        ops["wuq"] = ops["wuq"] + ag["late"][-1][0:1, 0:1].astype(ops["wuq"].dtype)
        return ops

    def late_weights(after):
        _, (gu, d) = _gather_wait(ag["late"], EARLY_ROWS, LATE_ROWS, after, name="ag_late_wait")
        slab = (1, 1, 1, FF_COLS, PACK_W)
        gu = lax.dynamic_update_slice(gu, own_rows(EARLY_ROWS, EARLY_ROWS + FF_COLS, slab), _gate_slab(me) + (0, 0))
        gu = lax.dynamic_update_slice(gu, own_rows(EARLY_ROWS + FF_COLS, EARLY_ROWS + 2 * FF_COLS, slab), _up_slab(me) + (0, 0))
        d = lax.dynamic_update_slice(d, own_rows(EARLY_ROWS + 2 * FF_COLS, PACK_ROWS, (1, FF_COLS, PACK_W)), (me, 0, 0))
        return gu.reshape(2 * D_FF, D_MODEL), d.reshape(D_FF, D_MODEL)

    rs = {}

    def late_grads(g_gu, g_d):
        rs["late"] = _scatter_start([g_gu.reshape(GU_SHAPE), g_d.reshape(D_SHAPE)],
                                    ((0, _gate_slab, FF_COLS), (0, _up_slab, FF_COLS), (1, dev, FF_COLS)),
                                    name="rs_late_start")
        return rs["late"][-1]

    def mid_grads(g):
        rs["mid"] = _scatter_start(_mid_grad_pack(g), mid_pieces, name="rs_mid_start")
        return rs["mid"][-1]

    def last_grads(g_win_t):
        rs["last"] = _scatter_start([_w_in_grad_chunks(g_win_t)], ((0, dev, W_IN_ROWS),), name="rs_last_start")
        return rs["last"][-1]

    first_w = dict(small_w, mix_norm_g=mix_norm_g + ag_mid[-1][0:1, 0:1])
    loss_tot, gx, g_small = _local_step(
        x[0], loss_target[0], _w_in_operand(win_g), first_w, types.SimpleNamespace(mid=mid_weights, late=late_weights),
        types.SimpleNamespace(late=late_grads, mid=mid_grads, last=last_grads))

    loss_rows = jnp.pad(loss_tot[0:1, 0:1], ((0, SUBLANES - 1), (0, PACK_W - 1)))
    small_rows = jnp.concatenate([_small_pack(g_small_rows(g_small), SUBLANES), loss_rows], 0)
    n_small = small_rows.shape[0]
    ag_small = _gather_start(small_rows, 0, ((0, dev, n_small),), ((N_DEV, n_small, PACK_W),), name="ag_small_start")

    g_gu, g_d, land_late = _scatter_wait(rs["late"], ag_small[-1], name="rs_late_wait")
    *g_mid, land_mid = _scatter_wait(rs["mid"], ag_small[-1], name="rs_mid_wait")
    g_win, land_last = _scatter_wait(rs["last"], ag_small[-1], name="rs_last_wait")
    swap = lambda a: jnp.swapaxes(a, 1, 2)
    same = lambda a: a
    chunks = dict(w_gate=(swap, g_gu, _gate_slab, land_late, 0, FF_COLS), w_up=(swap, g_gu, _up_slab, land_late, 1, FF_COLS),
                  w_down=(same, g_d, dev, land_late, 2, FF_COLS), w_in=(swap, g_win, dev, land_last, 0, W_IN_ROWS))
    gw, dw, mw, vw = {}, {}, {}, {}
    for n, (view, own, lead, land, blk, rows) in chunks.items():
        res = _sum_adamw(own, lead, land, blk, rows, idx, view(big_w3[n]), view(big_m[n]), view(big_v[n]), name="adamw_" + n)
        gw[n], dw[n], mw[n], vw[n] = (view(r) for r in res)
    g_nat = _mid_unpack(_peer_sum(g_mid, dev, land_mid, 0, OUT_ROWS, idx, name="rs_sum_mid"))
    for n, g in g_nat.items():
        gw[n] = g[None]
        dw[n], mw[n], vw[n] = _adamw_call(big_w3[n], g, big_m[n], big_v[n], name="adamw_" + n)

    own_small, (parts,) = _gather_wait(ag_small, 0, n_small, vw[n], name="ag_small_wait")
    parts = lax.dynamic_update_slice(parts, own_small[None], (me, 0, 0))
    gs, ds, ms, vs, loss = _small_adamw(parts, small_w, small_m, small_v)
    loss = loss[0, 0]
    for d in (gs, ds, ms, vs):
        d["final_norm_g"] = d["final_norm_g"].reshape(D_MODEL)

    order = ("mix_norm_g", "w_in", "swa_sinks", "q_norm_g", "w_uq", "kv_norm_g", "w_ukv", "w_o_swa", "w_o_mla", "w_out",
             "ffn_norm_g", "w_gate", "w_up", "w_down", "final_norm_g")

    def leaves(big, small):
        return [big[n] if n in big else small[n] for n in order]

    return (loss, gx[None], *leaves(gw, gs), *leaves(dw, ds), *leaves(mw, ms), *leaves(vw, vs))


def g_small_rows(g_small):
    out = dict(g_small)
    out["swa_sinks"] = jnp.pad(g_small["swa_sinks"], ((0, SUBLANES - 1), (0, 0)))
    return out
```
